```python
import jax, jax.numpy as jnp
from jax import lax
import numpy as np

D_MODEL = 2048
BATCH = 8
SEQ = 2048
DEPTH = 1

GM_WIDTH = 2048
CHUNK = 128
GM_GROUPS = 16
GM_GROUP_DIM = GM_WIDTH // GM_GROUPS
MLA_HEADS = 16
Q_LORA = 512
KV_LORA = 256
QK_NOPE = 128
QK_ROPE = 64
V_HEAD = 128
ROPE_THETA = 10000.0
Q_BLOCK = 128
D_FF = 5632
CONV_W = 3
EPS = 1e-6
N_MOD = 6
IN_SIZES = (GM_WIDTH, GM_WIDTH, Q_LORA, KV_LORA, QK_ROPE, D_MODEL, D_MODEL)
IN_COLS = sum(IN_SIZES)
IN_SPLITS = tuple(int(s) for s in np.cumsum(IN_SIZES)[:-1])

kernel_name = "hybrid_gmlp_mla_convffn_block"


def rmsnorm(x, g):
    xf = x.astype(jnp.float32)
    y = xf * lax.rsqrt(jnp.mean(xf * xf, axis=-1, keepdims=True) + EPS)
    return (y * g.astype(jnp.float32)).astype(x.dtype)


def layernorm(x, g, b):
    xf = x.astype(jnp.float32)
    mu = jnp.mean(xf, axis=-1, keepdims=True)
    var = jnp.mean(jnp.square(xf - mu), axis=-1, keepdims=True)
    y = (xf - mu) * lax.rsqrt(var + EPS)
    return (y * g.astype(jnp.float32) + b.astype(jnp.float32)).astype(x.dtype)


def rope_tables(positions, dtype):
    inv = ROPE_THETA ** (-jnp.arange(0, QK_ROPE, 2, dtype=jnp.float32) / QK_ROPE)
    ang = positions.astype(jnp.float32)[..., None] * inv
    return jnp.cos(ang).astype(dtype), jnp.sin(ang).astype(dtype)


def apply_rope(x, cos, sin):
    x1, x2 = jnp.split(x, 2, axis=-1)
    return jnp.concatenate([x1 * cos - x2 * sin, x2 * cos + x1 * sin], axis=-1)


def gmlp_spatial_gating(u, v, ln_g, ln_b, w_s, b_s):
    B, S, _ = v.shape
    v = layernorm(v, ln_g, ln_b)
    v = v.reshape(B, S // CHUNK, CHUNK, GM_GROUPS, GM_GROUP_DIM)
    mask = jnp.tril(jnp.ones((CHUNK, CHUNK), dtype=w_s.dtype))
    mixed = jnp.einsum('bnpgd,gqp->bnqgd', v, w_s * mask) + b_s.T[None, None, :, :, None]
    return u * mixed.reshape(B, S, GM_WIDTH)


def mla_attention(q_lat, kv_lat, k_pe, positions, q_norm_g, w_uq, kv_norm_g, w_ukv):
    B, S, _ = q_lat.shape
    q = (rmsnorm(q_lat, q_norm_g) @ w_uq).reshape(B, S, MLA_HEADS, QK_NOPE + QK_ROPE)
    kv = (rmsnorm(kv_lat, kv_norm_g) @ w_ukv).reshape(B, S, MLA_HEADS, QK_NOPE + V_HEAD)
    q_nope, q_pe = q[..., :QK_NOPE], q[..., QK_NOPE:]
    k_nope, v = kv[..., :QK_NOPE], kv[..., QK_NOPE:]
    cos, sin = rope_tables(positions, q.dtype)
    q_pe = apply_rope(q_pe, cos[:, :, None], sin[:, :, None])
    k_pe = apply_rope(k_pe, cos, sin)
    q = jnp.concatenate([q_nope, q_pe], axis=-1)
    k = jnp.concatenate([k_nope, jnp.broadcast_to(k_pe[:, :, None], (B, S, MLA_HEADS, QK_ROPE))], axis=-1)
    scale = (QK_NOPE + QK_ROPE) ** -0.5
    n_blocks = S // Q_BLOCK
    q_blocks = q.reshape(B, n_blocks, Q_BLOCK, MLA_HEADS, QK_NOPE + QK_ROPE).transpose(1, 0, 2, 3, 4)
    key_pos = jnp.arange(S)

    def attend(args):
        qb, i = args
        s = jnp.einsum('bqhd,bkhd->bhqk', qb, k).astype(jnp.float32) * scale
        q_pos = i * Q_BLOCK + jnp.arange(Q_BLOCK)
        causal = key_pos[None, :] <= q_pos[:, None]
        s = jnp.where(causal[None, None], s, -1e30)
        p = jax.nn.softmax(s, axis=-1).astype(v.dtype)
        return jnp.einsum('bhqk,bkhd->bqhd', p, v)

    o = lax.map(attend, (q_blocks, jnp.arange(n_blocks)))
    return o.transpose(1, 0, 2, 3, 4).reshape(B, S, MLA_HEADS * V_HEAD)


def causal_dwconv(h, w, b):
    S = h.shape[1]
    hp = jnp.pad(h, ((0, 0), (CONV_W - 1, 0), (0, 0)))
    return sum(w[k] * hp[:, k:k + S] for k in range(CONV_W)) + b


def _fwd_setup_inputs(seed: int = 0) -> dict:
    key = jax.random.key(seed)
    ks = jax.random.split(key, 32)
    f32 = jnp.float32
    nrm = lambda k, shape, s: jax.random.normal(k, shape, f32) * s
    gain = lambda k, n: 1.0 + 0.02 * jax.random.normal(k, (n,), f32)
    offset = jax.random.randint(ks[2], (BATCH, 1), 0, 4096, dtype=jnp.int32)
    positions = (jnp.arange(SEQ, dtype=jnp.int32)[None, :] + offset).astype(jnp.int32)
    return {
        "x": nrm(ks[0], (BATCH, SEQ, D_MODEL), 1.0),
        "c": nrm(ks[1], (BATCH, D_MODEL), 1.0),
        "positions": positions,
        "w_ada": nrm(ks[3], (D_MODEL, N_MOD * D_MODEL), 0.5 * D_MODEL ** -0.5),
        "b_ada": nrm(ks[4], (N_MOD * D_MODEL,), 0.01),
        "pre_norm1_g": gain(ks[5], D_MODEL),
        "w_in": nrm(ks[6], (D_MODEL, IN_COLS), D_MODEL ** -0.5),
        "gm_ln_g": gain(ks[7], GM_WIDTH),
        "gm_ln_b": nrm(ks[8], (GM_WIDTH,), 0.01),
        "gm_w_s": nrm(ks[9], (GM_GROUPS, CHUNK, CHUNK), CHUNK ** -0.5),
        "gm_b_s": 1.0 + 0.02 * jax.random.normal(ks[10], (GM_GROUPS, CHUNK), f32),
        "w_branch_a": nrm(ks[11], (GM_WIDTH, D_MODEL), GM_WIDTH ** -0.5),
        "q_norm_g": gain(ks[12], Q_LORA),
        "w_uq": nrm(ks[13], (Q_LORA, MLA_HEADS * (QK_NOPE + QK_ROPE)), Q_LORA ** -0.5),
        "kv_norm_g": gain(ks[14], KV_LORA),
        "w_ukv": nrm(ks[15], (KV_LORA, MLA_HEADS * (QK_NOPE + V_HEAD)), KV_LORA ** -0.5),
        "w_branch_b": nrm(ks[16], (MLA_HEADS * V_HEAD, D_MODEL), (MLA_HEADS * V_HEAD) ** -0.5),
        "w_out": nrm(ks[17], (D_MODEL, D_MODEL), D_MODEL ** -0.5),
        "post_norm1_g": gain(ks[18], D_MODEL),
        "pre_norm2_g": gain(ks[19], D_MODEL),
        "w_up": nrm(ks[20], (D_MODEL, 2 * D_FF), D_MODEL ** -0.5),
        "conv_w": nrm(ks[21], (CONV_W, 2 * D_FF), CONV_W ** -0.5),
        "conv_b": nrm(ks[22], (2 * D_FF,), 0.01),
        "w_down": nrm(ks[23], (D_FF, D_MODEL), D_FF ** -0.5),
        "post_norm2_g": gain(ks[24], D_MODEL),
    }


def _fwd_reference(x, c, positions, w_ada, b_ada, pre_norm1_g, w_in, gm_ln_g, gm_ln_b, gm_w_s, gm_b_s,
              w_branch_a, q_norm_g, w_uq, kv_norm_g, w_ukv, w_branch_b, w_out, post_norm1_g,
              pre_norm2_g, w_up, conv_w, conv_b, w_down, post_norm2_g):
    B = x.shape[0]
    mod = (jax.nn.silu(c) @ w_ada + b_ada).reshape(B, N_MOD, D_MODEL)
    shift1, scale1, gate1 = mod[:, None, 0], mod[:, None, 1], mod[:, None, 2]
    shift2, scale2, gate2 = mod[:, None, 3], mod[:, None, 4], mod[:, None, 5]

    for _ in range(DEPTH):
        h = rmsnorm(x, pre_norm1_g) * (1.0 + scale1) + shift1
        z = h @ w_in
        u, v, q_lat, kv_lat, k_pe, g_a, g_b = jnp.split(z, IN_SPLITS, axis=-1)
        y_a = gmlp_spatial_gating(jax.nn.gelu(u), jax.nn.gelu(v), gm_ln_g, gm_ln_b, gm_w_s, gm_b_s) @ w_branch_a
        y_b = mla_attention(q_lat, kv_lat, k_pe, positions, q_norm_g, w_uq, kv_norm_g, w_ukv) @ w_branch_b
        merged = jax.nn.sigmoid(g_a) * y_a + jax.nn.sigmoid(g_b) * y_b
        x = x + gate1 * rmsnorm(merged @ w_out, post_norm1_g)

        h = rmsnorm(x, pre_norm2_g) * (1.0 + scale2) + shift2
        up = causal_dwconv(h @ w_up, conv_w, conv_b)
        gate_h, val_h = jnp.split(up, 2, axis=-1)
        ffn = (jax.nn.silu(gate_h) * val_h) @ w_down
        x = x + gate2 * rmsnorm(ffn, post_norm2_g)
    return x


import jax as _jax
import jax.numpy as _jnp

TWIN_FORMAT = 'train_step'
FWD_PARAMS = ['x', 'c', 'positions', 'w_ada', 'b_ada', 'pre_norm1_g', 'w_in', 'gm_ln_g', 'gm_ln_b', 'gm_w_s', 'gm_b_s', 'w_branch_a', 'q_norm_g', 'w_uq', 'kv_norm_g', 'w_ukv', 'w_branch_b', 'w_out', 'post_norm1_g', 'pre_norm2_g', 'w_up', 'conv_w', 'conv_b', 'w_down', 'post_norm2_g']
TWIN_WEIGHTS = ['w_ada', 'b_ada', 'pre_norm1_g', 'w_in', 'gm_ln_g', 'gm_ln_b', 'gm_w_s', 'gm_b_s', 'w_branch_a', 'q_norm_g', 'w_uq', 'kv_norm_g', 'w_ukv', 'w_branch_b', 'w_out', 'post_norm1_g', 'pre_norm2_g', 'w_up', 'conv_w', 'conv_b', 'w_down', 'post_norm2_g']
TWIN_DIFF_INPUT = 'x'
TWIN_INPUTS = ['x', 'c', 'positions', 'w_ada', 'b_ada', 'pre_norm1_g', 'w_in', 'gm_ln_g', 'gm_ln_b', 'gm_w_s', 'gm_b_s', 'w_branch_a', 'q_norm_g', 'w_uq', 'kv_norm_g', 'w_ukv', 'w_branch_b', 'w_out', 'post_norm1_g', 'pre_norm2_g', 'w_up', 'conv_w', 'conv_b', 'w_down', 'post_norm2_g', 'loss_target', 'm_w_ada', 'm_b_ada', 'm_pre_norm1_g', 'm_w_in', 'm_gm_ln_g', 'm_gm_ln_b', 'm_gm_w_s', 'm_gm_b_s', 'm_w_branch_a', 'm_q_norm_g', 'm_w_uq', 'm_kv_norm_g', 'm_w_ukv', 'm_w_branch_b', 'm_w_out', 'm_post_norm1_g', 'm_pre_norm2_g', 'm_w_up', 'm_conv_w', 'm_conv_b', 'm_w_down', 'm_post_norm2_g', 'v_w_ada', 'v_b_ada', 'v_pre_norm1_g', 'v_w_in', 'v_gm_ln_g', 'v_gm_ln_b', 'v_gm_w_s', 'v_gm_b_s', 'v_w_branch_a', 'v_q_norm_g', 'v_w_uq', 'v_kv_norm_g', 'v_w_ukv', 'v_w_branch_b', 'v_w_out', 'v_post_norm1_g', 'v_pre_norm2_g', 'v_w_up', 'v_conv_w', 'v_conv_b', 'v_w_down', 'v_post_norm2_g']
TWIN_OUTPUTS = ['loss', 'grad_x', 'grad_w_ada', 'grad_b_ada', 'grad_pre_norm1_g', 'grad_w_in', 'grad_gm_ln_g', 'grad_gm_ln_b', 'grad_gm_w_s', 'grad_gm_b_s', 'grad_w_branch_a', 'grad_q_norm_g', 'grad_w_uq', 'grad_kv_norm_g', 'grad_w_ukv', 'grad_w_branch_b', 'grad_w_out', 'grad_post_norm1_g', 'grad_pre_norm2_g', 'grad_w_up', 'grad_conv_w', 'grad_conv_b', 'grad_w_down', 'grad_post_norm2_g', 'delta_w_ada', 'delta_b_ada', 'delta_pre_norm1_g', 'delta_w_in', 'delta_gm_ln_g', 'delta_gm_ln_b', 'delta_gm_w_s', 'delta_gm_b_s', 'delta_w_branch_a', 'delta_q_norm_g', 'delta_w_uq', 'delta_kv_norm_g', 'delta_w_ukv', 'delta_w_branch_b', 'delta_w_out', 'delta_post_norm1_g', 'delta_pre_norm2_g', 'delta_w_up', 'delta_conv_w', 'delta_conv_b', 'delta_w_down', 'delta_post_norm2_g', 'new_m_w_ada', 'new_m_b_ada', 'new_m_pre_norm1_g', 'new_m_w_in', 'new_m_gm_ln_g', 'new_m_gm_ln_b', 'new_m_gm_w_s', 'new_m_gm_b_s', 'new_m_w_branch_a', 'new_m_q_norm_g', 'new_m_w_uq', 'new_m_kv_norm_g', 'new_m_w_ukv', 'new_m_w_branch_b', 'new_m_w_out', 'new_m_post_norm1_g', 'new_m_pre_norm2_g', 'new_m_w_up', 'new_m_conv_w', 'new_m_conv_b', 'new_m_w_down', 'new_m_post_norm2_g', 'new_v_w_ada', 'new_v_b_ada', 'new_v_pre_norm1_g', 'new_v_w_in', 'new_v_gm_ln_g', 'new_v_gm_ln_b', 'new_v_gm_w_s', 'new_v_gm_b_s', 'new_v_w_branch_a', 'new_v_q_norm_g', 'new_v_w_uq', 'new_v_kv_norm_g', 'new_v_w_ukv', 'new_v_w_branch_b', 'new_v_w_out', 'new_v_post_norm1_g', 'new_v_pre_norm2_g', 'new_v_w_up', 'new_v_conv_w', 'new_v_conv_b', 'new_v_w_down', 'new_v_post_norm2_g']
TWIN_LEAF_KINDS = {'loss': 'loss', 'grad_x': 'grad_x', 'grad_w_ada': 'grad_w', 'grad_b_ada': 'grad_w', 'grad_pre_norm1_g': 'grad_w', 'grad_w_in': 'grad_w', 'grad_gm_ln_g': 'grad_w', 'grad_gm_ln_b': 'grad_w', 'grad_gm_w_s': 'grad_w', 'grad_gm_b_s': 'grad_w', 'grad_w_branch_a': 'grad_w', 'grad_q_norm_g': 'grad_w', 'grad_w_uq': 'grad_w', 'grad_kv_norm_g': 'grad_w', 'grad_w_ukv': 'grad_w', 'grad_w_branch_b': 'grad_w', 'grad_w_out': 'grad_w', 'grad_post_norm1_g': 'grad_w', 'grad_pre_norm2_g': 'grad_w', 'grad_w_up': 'grad_w', 'grad_conv_w': 'grad_w', 'grad_conv_b': 'grad_w', 'grad_w_down': 'grad_w', 'grad_post_norm2_g': 'grad_w', 'delta_w_ada': 'delta_w', 'delta_b_ada': 'delta_w', 'delta_pre_norm1_g': 'delta_w', 'delta_w_in': 'delta_w', 'delta_gm_ln_g': 'delta_w', 'delta_gm_ln_b': 'delta_w', 'delta_gm_w_s': 'delta_w', 'delta_gm_b_s': 'delta_w', 'delta_w_branch_a': 'delta_w', 'delta_q_norm_g': 'delta_w', 'delta_w_uq': 'delta_w', 'delta_kv_norm_g': 'delta_w', 'delta_w_ukv': 'delta_w', 'delta_w_branch_b': 'delta_w', 'delta_w_out': 'delta_w', 'delta_post_norm1_g': 'delta_w', 'delta_pre_norm2_g': 'delta_w', 'delta_w_up': 'delta_w', 'delta_conv_w': 'delta_w', 'delta_conv_b': 'delta_w', 'delta_w_down': 'delta_w', 'delta_post_norm2_g': 'delta_w', 'new_m_w_ada': 'new_m', 'new_m_b_ada': 'new_m', 'new_m_pre_norm1_g': 'new_m', 'new_m_w_in': 'new_m', 'new_m_gm_ln_g': 'new_m', 'new_m_gm_ln_b': 'new_m', 'new_m_gm_w_s': 'new_m', 'new_m_gm_b_s': 'new_m', 'new_m_w_branch_a': 'new_m', 'new_m_q_norm_g': 'new_m', 'new_m_w_uq': 'new_m', 'new_m_kv_norm_g': 'new_m', 'new_m_w_ukv': 'new_m', 'new_m_w_branch_b': 'new_m', 'new_m_w_out': 'new_m', 'new_m_post_norm1_g': 'new_m', 'new_m_pre_norm2_g': 'new_m', 'new_m_w_up': 'new_m', 'new_m_conv_w': 'new_m', 'new_m_conv_b': 'new_m', 'new_m_w_down': 'new_m', 'new_m_post_norm2_g': 'new_m', 'new_v_w_ada': 'new_v', 'new_v_b_ada': 'new_v', 'new_v_pre_norm1_g': 'new_v', 'new_v_w_in': 'new_v', 'new_v_gm_ln_g': 'new_v', 'new_v_gm_ln_b': 'new_v', 'new_v_gm_w_s': 'new_v', 'new_v_gm_b_s': 'new_v', 'new_v_w_branch_a': 'new_v', 'new_v_q_norm_g': 'new_v', 'new_v_w_uq': 'new_v', 'new_v_kv_norm_g': 'new_v', 'new_v_w_ukv': 'new_v', 'new_v_w_branch_b': 'new_v', 'new_v_w_out': 'new_v', 'new_v_post_norm1_g': 'new_v', 'new_v_pre_norm2_g': 'new_v', 'new_v_w_up': 'new_v', 'new_v_conv_w': 'new_v', 'new_v_conv_b': 'new_v', 'new_v_w_down': 'new_v', 'new_v_post_norm2_g': 'new_v'}


def _forward(args):
    return _fwd_reference(*[args[k] for k in FWD_PARAMS])


def _output_shape():
    out = _jax.eval_shape(lambda: _forward(_fwd_setup_inputs(0)))
    return out.shape, out.dtype

N_MICROBATCH = 1
ADAM_LR = 0.001
ADAM_B1 = 0.9
ADAM_B2 = 0.999
ADAM_EPS = 1e-08
ADAM_WD = 0.01
ADAM_STEP = 10
PER_EXAMPLE_BATCH_AXIS = {'x': 0, 'c': 0, 'positions': 0, 'loss_target': 0}
SHARED_INPUTS = []
_WEIGHT_DTYPES = {'w_ada': _jnp.float32, 'b_ada': _jnp.float32, 'pre_norm1_g': _jnp.float32, 'w_in': _jnp.float32, 'gm_ln_g': _jnp.float32, 'gm_ln_b': _jnp.float32, 'gm_w_s': _jnp.float32, 'gm_b_s': _jnp.float32, 'w_branch_a': _jnp.float32, 'q_norm_g': _jnp.float32, 'w_uq': _jnp.float32, 'kv_norm_g': _jnp.float32, 'w_ukv': _jnp.float32, 'w_branch_b': _jnp.float32, 'w_out': _jnp.float32, 'post_norm1_g': _jnp.float32, 'pre_norm2_g': _jnp.float32, 'w_up': _jnp.float32, 'conv_w': _jnp.float32, 'conv_b': _jnp.float32, 'w_down': _jnp.float32, 'post_norm2_g': _jnp.float32}
MOMENT_SCALE = {'w_ada': 4.335864e-01, 'b_ada': 8.212620e-01, 'pre_norm1_g': 3.903824e-02, 'w_in': 3.515112e-02, 'gm_ln_g': 1.619680e-02, 'gm_ln_b': 1.728679e-02, 'gm_w_s': 1.644411e-02, 'gm_b_s': 2.562370e-02, 'w_branch_a': 7.897452e-02, 'q_norm_g': 1.235316e-02, 'w_uq': 4.985832e-03, 'kv_norm_g': 1.567278e-01, 'w_ukv': 4.051448e-02, 'w_branch_b': 5.793994e-02, 'w_out': 9.973279e-02, 'post_norm1_g': 9.499102e-01, 'pre_norm2_g': 4.021851e-02, 'w_up': 1.957156e-02, 'conv_w': 2.013961e-02, 'conv_b': 3.349333e-02, 'w_down': 3.497046e-02, 'post_norm2_g': 8.701822e-01}


def _to_microbatches(a, axis):
    t = _jnp.moveaxis(a, axis, 0)
    t = t.reshape((N_MICROBATCH, t.shape[0] // N_MICROBATCH) + t.shape[1:])
    return _jnp.moveaxis(t, 1, axis + 1)


def setup_inputs(seed: int = 0) -> dict:
    inp = _fwd_setup_inputs(seed)
    key = _jax.random.fold_in(_jax.random.key(seed), 7919)
    shape, _ = _output_shape()
    out = dict(inp)
    out["loss_target"] = _jax.random.normal(_jax.random.fold_in(key, 0), shape, _jnp.float32)
    for i, name in enumerate(TWIN_WEIGHTS):
        w = inp[name].astype(_jnp.float32)
        if MOMENT_SCALE is None:
            s = _jnp.sqrt(_jnp.mean(_jnp.square(w)) + 1e-30)
        else:
            s = MOMENT_SCALE[name]
        km, kv = _jax.random.split(_jax.random.fold_in(key, i + 1))
        out[name] = w
        out["m_" + name] = s * _jax.random.normal(km, w.shape, _jnp.float32)
        out["v_" + name] = (s * s) * _jax.random.uniform(kv, w.shape, _jnp.float32, 0.5, 1.5)
    if N_MICROBATCH > 1:
        for name, axis in PER_EXAMPLE_BATCH_AXIS.items():
            out[name] = _to_microbatches(out[name], axis)
    return {'x': out['x'], 'c': out['c'], 'positions': out['positions'], 'w_ada': out['w_ada'], 'b_ada': out['b_ada'], 'pre_norm1_g': out['pre_norm1_g'], 'w_in': out['w_in'], 'gm_ln_g': out['gm_ln_g'], 'gm_ln_b': out['gm_ln_b'], 'gm_w_s': out['gm_w_s'], 'gm_b_s': out['gm_b_s'], 'w_branch_a': out['w_branch_a'], 'q_norm_g': out['q_norm_g'], 'w_uq': out['w_uq'], 'kv_norm_g': out['kv_norm_g'], 'w_ukv': out['w_ukv'], 'w_branch_b': out['w_branch_b'], 'w_out': out['w_out'], 'post_norm1_g': out['post_norm1_g'], 'pre_norm2_g': out['pre_norm2_g'], 'w_up': out['w_up'], 'conv_w': out['conv_w'], 'conv_b': out['conv_b'], 'w_down': out['w_down'], 'post_norm2_g': out['post_norm2_g'], 'loss_target': out['loss_target'], 'm_w_ada': out['m_w_ada'], 'm_b_ada': out['m_b_ada'], 'm_pre_norm1_g': out['m_pre_norm1_g'], 'm_w_in': out['m_w_in'], 'm_gm_ln_g': out['m_gm_ln_g'], 'm_gm_ln_b': out['m_gm_ln_b'], 'm_gm_w_s': out['m_gm_w_s'], 'm_gm_b_s': out['m_gm_b_s'], 'm_w_branch_a': out['m_w_branch_a'], 'm_q_norm_g': out['m_q_norm_g'], 'm_w_uq': out['m_w_uq'], 'm_kv_norm_g': out['m_kv_norm_g'], 'm_w_ukv': out['m_w_ukv'], 'm_w_branch_b': out['m_w_branch_b'], 'm_w_out': out['m_w_out'], 'm_post_norm1_g': out['m_post_norm1_g'], 'm_pre_norm2_g': out['m_pre_norm2_g'], 'm_w_up': out['m_w_up'], 'm_conv_w': out['m_conv_w'], 'm_conv_b': out['m_conv_b'], 'm_w_down': out['m_w_down'], 'm_post_norm2_g': out['m_post_norm2_g'], 'v_w_ada': out['v_w_ada'], 'v_b_ada': out['v_b_ada'], 'v_pre_norm1_g': out['v_pre_norm1_g'], 'v_w_in': out['v_w_in'], 'v_gm_ln_g': out['v_gm_ln_g'], 'v_gm_ln_b': out['v_gm_ln_b'], 'v_gm_w_s': out['v_gm_w_s'], 'v_gm_b_s': out['v_gm_b_s'], 'v_w_branch_a': out['v_w_branch_a'], 'v_q_norm_g': out['v_q_norm_g'], 'v_w_uq': out['v_w_uq'], 'v_kv_norm_g': out['v_kv_norm_g'], 'v_w_ukv': out['v_w_ukv'], 'v_w_branch_b': out['v_w_branch_b'], 'v_w_out': out['v_w_out'], 'v_post_norm1_g': out['v_post_norm1_g'], 'v_pre_norm2_g': out['v_pre_norm2_g'], 'v_w_up': out['v_w_up'], 'v_conv_w': out['v_conv_w'], 'v_conv_b': out['v_conv_b'], 'v_w_down': out['v_w_down'], 'v_post_norm2_g': out['v_post_norm2_g']}


def _loss(weights, diff, rest, loss_target):
    with _jax.named_scope("forward"):
        args = {**rest, TWIN_DIFF_INPUT: diff, **{k: w.astype(_WEIGHT_DTYPES[k]) for k, w in weights.items()}}
        y = _forward(args)
    with _jax.named_scope("loss_head"):
        err = _jnp.square(y.astype(_jnp.float32) - loss_target)
        return 0.5 * _jnp.sum(_jnp.mean(err, axis=-1)) if err.ndim else 0.5 * err


def _adamw(w, g, m, v):
    m = ADAM_B1 * m + (1.0 - ADAM_B1) * g
    v = ADAM_B2 * v + (1.0 - ADAM_B2) * _jnp.square(g)
    m_hat = m / (1.0 - ADAM_B1 ** ADAM_STEP)
    v_hat = v / (1.0 - ADAM_B2 ** ADAM_STEP)
    delta = -ADAM_LR * (m_hat / (_jnp.sqrt(v_hat) + ADAM_EPS) + ADAM_WD * w)
    return delta, m, v


def reference(x, c, positions, w_ada, b_ada, pre_norm1_g, w_in, gm_ln_g, gm_ln_b, gm_w_s, gm_b_s, w_branch_a, q_norm_g, w_uq, kv_norm_g, w_ukv, w_branch_b, w_out, post_norm1_g, pre_norm2_g, w_up, conv_w, conv_b, w_down, post_norm2_g, loss_target, m_w_ada, m_b_ada, m_pre_norm1_g, m_w_in, m_gm_ln_g, m_gm_ln_b, m_gm_w_s, m_gm_b_s, m_w_branch_a, m_q_norm_g, m_w_uq, m_kv_norm_g, m_w_ukv, m_w_branch_b, m_w_out, m_post_norm1_g, m_pre_norm2_g, m_w_up, m_conv_w, m_conv_b, m_w_down, m_post_norm2_g, v_w_ada, v_b_ada, v_pre_norm1_g, v_w_in, v_gm_ln_g, v_gm_ln_b, v_gm_w_s, v_gm_b_s, v_w_branch_a, v_q_norm_g, v_w_uq, v_kv_norm_g, v_w_ukv, v_w_branch_b, v_w_out, v_post_norm1_g, v_pre_norm2_g, v_w_up, v_conv_w, v_conv_b, v_w_down, v_post_norm2_g):
    given = dict(x=x, c=c, positions=positions, w_ada=w_ada, b_ada=b_ada, pre_norm1_g=pre_norm1_g, w_in=w_in, gm_ln_g=gm_ln_g, gm_ln_b=gm_ln_b, gm_w_s=gm_w_s, gm_b_s=gm_b_s, w_branch_a=w_branch_a, q_norm_g=q_norm_g, w_uq=w_uq, kv_norm_g=kv_norm_g, w_ukv=w_ukv, w_branch_b=w_branch_b, w_out=w_out, post_norm1_g=post_norm1_g, pre_norm2_g=pre_norm2_g, w_up=w_up, conv_w=conv_w, conv_b=conv_b, w_down=w_down, post_norm2_g=post_norm2_g, loss_target=loss_target, m_w_ada=m_w_ada, m_b_ada=m_b_ada, m_pre_norm1_g=m_pre_norm1_g, m_w_in=m_w_in, m_gm_ln_g=m_gm_ln_g, m_gm_ln_b=m_gm_ln_b, m_gm_w_s=m_gm_w_s, m_gm_b_s=m_gm_b_s, m_w_branch_a=m_w_branch_a, m_q_norm_g=m_q_norm_g, m_w_uq=m_w_uq, m_kv_norm_g=m_kv_norm_g, m_w_ukv=m_w_ukv, m_w_branch_b=m_w_branch_b, m_w_out=m_w_out, m_post_norm1_g=m_post_norm1_g, m_pre_norm2_g=m_pre_norm2_g, m_w_up=m_w_up, m_conv_w=m_conv_w, m_conv_b=m_conv_b, m_w_down=m_w_down, m_post_norm2_g=m_post_norm2_g, v_w_ada=v_w_ada, v_b_ada=v_b_ada, v_pre_norm1_g=v_pre_norm1_g, v_w_in=v_w_in, v_gm_ln_g=v_gm_ln_g, v_gm_ln_b=v_gm_ln_b, v_gm_w_s=v_gm_w_s, v_gm_b_s=v_gm_b_s, v_w_branch_a=v_w_branch_a, v_q_norm_g=v_q_norm_g, v_w_uq=v_w_uq, v_kv_norm_g=v_kv_norm_g, v_w_ukv=v_w_ukv, v_w_branch_b=v_w_branch_b, v_w_out=v_w_out, v_post_norm1_g=v_post_norm1_g, v_pre_norm2_g=v_pre_norm2_g, v_w_up=v_w_up, v_conv_w=v_conv_w, v_conv_b=v_conv_b, v_w_down=v_w_down, v_post_norm2_g=v_post_norm2_g)
    weights = {n: given[n] for n in TWIN_WEIGHTS}
    shared = {n: given[n] for n in SHARED_INPUTS}
    per_example = {n: given[n] for n in ['x', 'c', 'positions']}
    grad_fn = _jax.value_and_grad(_loss, argnums=(0, 1))

    def one_microbatch(ex, loss_target):
        ex = dict(ex)
        diff = ex.pop(TWIN_DIFF_INPUT)
        return grad_fn(weights, diff, {**shared, **ex}, loss_target)

    if N_MICROBATCH == 1:
        loss, (grad_w, grad_x) = one_microbatch(per_example, given["loss_target"])
    else:
        def body(carry, xs):
            loss_sum, grad_sum = carry
            l_k, (gw_k, gx_k) = one_microbatch(xs[0], xs[1])
            with _jax.named_scope("update"):
                return (loss_sum + l_k, _jax.tree.map(_jnp.add, grad_sum, gw_k)), gx_k

        init = (_jnp.zeros((), _jnp.float32), _jax.tree.map(_jnp.zeros_like, weights))
        (loss, grad_w), grad_x = _jax.lax.scan(body, init, (per_example, given["loss_target"]))
    with _jax.named_scope("update"):
        delta_w, new_m, new_v = {}, {}, {}
        for n in TWIN_WEIGHTS:
            delta_w[n], new_m[n], new_v[n] = _adamw(weights[n], grad_w[n], given["m_" + n], given["v_" + n])
    return (loss, grad_x, *[grad_w[n] for n in TWIN_WEIGHTS], *[delta_w[n] for n in TWIN_WEIGHTS],
            *[new_m[n] for n in TWIN_WEIGHTS], *[new_v[n] for n in TWIN_WEIGHTS])
```

```python
import functools

import jax
import jax.numpy as jnp
from jax import lax
from jax.experimental import pallas as pl
from jax.experimental.pallas import tpu as pltpu

F32 = jnp.float32
BF16 = jnp.bfloat16
MESH = pl.DeviceIdType.MESH
HBM = pltpu.HBM

EPS = 1e-6
NOPE, ROPE, VHEAD = 128, 64, 128
HEAD_W = NOPE + 2 * ROPE
ROPE_THETA = 10000.0
CONV_TAPS = 3
N_MOD = 6
N_CHIPS, N_CORES, N_DEV = 4, 2, 8
ADAM_LR, ADAM_B1, ADAM_B2, ADAM_EPS, ADAM_WD, ADAM_STEP = 0.001, 0.9, 0.999, 1e-08, 0.01, 10

LANES = 128
SUBLANES = 8
VMEM_LIMIT = 56 * 2**20
PACK_COLS = 1024
PACK_ROW_TILE = 256

BIG = ("w_in", "w_branch_a", "w_uq", "w_ukv", "w_branch_b", "w_out", "w_up", "w_down")
COL_SHARDED = ("w_in", "w_uq", "w_ukv", "w_up")
WEIGHTS = ("w_ada", "b_ada", "pre_norm1_g", "w_in", "gm_ln_g", "gm_ln_b", "gm_w_s", "gm_b_s", "w_branch_a",
           "q_norm_g", "w_uq", "kv_norm_g", "w_ukv", "w_branch_b", "w_out", "post_norm1_g", "pre_norm2_g",
           "w_up", "conv_w", "conv_b", "w_down", "post_norm2_g")
SMALL_PARTIAL = ("pre_norm1_g", "gm_ln_g", "gm_ln_b", "gm_w_s", "gm_b_s", "q_norm_g", "kv_norm_g", "post_norm1_g",
                 "pre_norm2_g", "conv_w", "conv_b", "post_norm2_g")
SMALL = ("b_ada",) + SMALL_PARTIAL


def _div_tile(n, cap, mult=LANES):
    t = (min(cap, n) // mult) * mult
    while t >= mult:
        if n % t == 0:
            return t
        t -= mult
    return n


def _params(**kw):
    return pltpu.CompilerParams(vmem_limit_bytes=VMEM_LIMIT, **kw)


def _row_spec(width):
    return pl.BlockSpec((1, width), lambda *_: (0, 0))


def _gelu(x):
    k = 0.7978845608028654
    return 0.5 * x * (1.0 + jnp.tanh(k * (x + 0.044715 * x * x * x)))


def _gelu_grad(x):
    k = 0.7978845608028654
    t = jnp.tanh(k * (x + 0.044715 * x * x * x))
    return 0.5 * (1.0 + t) + 0.5 * x * (1.0 - t * t) * k * (1.0 + 3.0 * 0.044715 * x * x)


def _sigmoid(x):
    return 1.0 / (1.0 + jnp.exp(-x))


def _dot(a, b, dims):
    return lax.dot_general(a, b, (dims, ((), ())), preferred_element_type=F32)


NN = ((1,), (0,))
NT = ((1,), (1,))
TN = ((0,), (0,))


def _matmul(a, b, *, mode, out_dtype, name, tm=512, tn=512, tk=2048, mul=None, add=None, b_off=0):
    if mode == "nn":
        (m, kd), n = a.shape, b.shape[1]
    elif mode == "nt":
        (m, kd), n = a.shape, b.shape[0]
    else:
        (kd, m), n = a.shape, b.shape[1]
    tm, tn, tk = _div_tile(m, tm, SUBLANES), _div_tile(n, tn), _div_tile(kd, tk)
    nk = kd // tk
    dims = {"nn": NN, "nt": NT, "tn": TN}[mode]
    if mode == "nn":
        a_spec = pl.BlockSpec((tm, tk), lambda i, j, k: (i, k))
        b_spec = pl.BlockSpec((tk, tn), lambda i, j, k: (k, j))
    elif mode == "nt":
        a_spec = pl.BlockSpec((tm, tk), lambda i, j, k: (i, k))
        b_spec = pl.BlockSpec((tn, tk), lambda i, j, k: (j, k + b_off))
    else:
        a_spec = pl.BlockSpec((tk, tm), lambda i, j, k: (k, i))
        b_spec = pl.BlockSpec((tk, tn), lambda i, j, k: (k, j))
    in_specs, operands = [a_spec, b_spec], [a, b]
    if mul is not None:
        assert mul.shape == (m, tn)
        in_specs.append(pl.BlockSpec((tm, tn), lambda i, j, k: (i, 0)))
        operands.append(mul)
    if add is not None:
        in_specs.append(pl.BlockSpec((tm, tn), lambda i, j, k: (i, j)))
        operands.append(add)

    def body(*refs):
        a_ref, b_ref = refs[0], refs[1]
        pos = 2
        mul_ref = add_ref = None
        if mul is not None:
            mul_ref, pos = refs[pos], pos + 1
        if add is not None:
            add_ref, pos = refs[pos], pos + 1
        o_ref = refs[pos]

        def finish(r):
            if mul_ref is not None:
                r = r * mul_ref[...]
            if add_ref is not None:
                r = r + add_ref[...]
            o_ref[...] = r.astype(out_dtype)

        part = _dot(a_ref[...], b_ref[...], dims)
        if nk == 1:
            finish(part)
        else:
            acc_ref = refs[pos + 1]
            k = pl.program_id(2)

            @pl.when(k == 0)
            def _():
                acc_ref[...] = part

            @pl.when(k > 0)
            def _():
                acc_ref[...] += part

            @pl.when(k == nk - 1)
            def _():
                finish(acc_ref[...])

    return pl.pallas_call(
        body, name=name, grid=(m // tm, n // tn, nk), in_specs=in_specs,
        out_specs=pl.BlockSpec((tm, tn), lambda i, j, k: (i, j)),
        out_shape=jax.ShapeDtypeStruct((m, n), out_dtype),
        scratch_shapes=[] if nk == 1 else [pltpu.VMEM((tm, tn), F32)],
        compiler_params=_params(),
    )(*operands)


def _accumulate(ref, value):
    @pl.when(pl.program_id(0) == 0)
    def _():
        ref[...] = value

    @pl.when(pl.program_id(0) > 0)
    def _():
        ref[...] += value


def _colsum(v):
    return jnp.sum(v, axis=0, keepdims=True)


def _rowmean(v):
    return jnp.mean(v, axis=-1, keepdims=True)


def _prenorm(x, g, scale, shift, name):
    s, d = x.shape
    tb = _div_tile(s, 256, SUBLANES)

    def body(x_ref, g_ref, sc_ref, sh_ref, h_ref):
        xv = x_ref[...]
        r = lax.rsqrt(_rowmean(xv * xv) + EPS)
        h_ref[...] = ((xv * r) * g_ref[...] * (1.0 + sc_ref[...]) + sh_ref[...]).astype(BF16)

    blk = pl.BlockSpec((tb, d), lambda i: (i, 0))
    return pl.pallas_call(
        body, name=name, grid=(s // tb,), in_specs=[blk, _row_spec(d), _row_spec(d), _row_spec(d)],
        out_specs=blk, out_shape=jax.ShapeDtypeStruct((s, d), BF16), compiler_params=_params(),
    )(x, g, scale, shift)


def _post_pre(x, y, gate, pg, g2, scale2, shift2, name):
    s, d = x.shape
    tb = _div_tile(s, 256, SUBLANES)

    def body(x_ref, y_ref, gate_ref, pg_ref, g2_ref, sc_ref, sh_ref, x1_ref, h2_ref):
        yv = y_ref[...]
        rp = lax.rsqrt(_rowmean(yv * yv) + EPS)
        x1 = x_ref[...] + gate_ref[...] * ((yv * rp) * pg_ref[...])
        x1_ref[...] = x1
        r2 = lax.rsqrt(_rowmean(x1 * x1) + EPS)
        h2_ref[...] = ((x1 * r2) * g2_ref[...] * (1.0 + sc_ref[...]) + sh_ref[...]).astype(BF16)

    blk = pl.BlockSpec((tb, d), lambda i: (i, 0))
    return pl.pallas_call(
        body, name=name, grid=(s // tb,), in_specs=[blk, blk] + [_row_spec(d)] * 5,
        out_specs=[blk, blk],
        out_shape=[jax.ShapeDtypeStruct((s, d), F32), jax.ShapeDtypeStruct((s, d), BF16)],
        compiler_params=_params(),
    )(x, y, gate, pg, g2, scale2, shift2)


def _post_bwd(y, gate, pg, name, *, dxo=None, xin=None, target=None):
    s, d = y.shape
    tb = _div_tile(s, 256, SUBLANES)
    from_loss = target is not None

    def body(*refs):
        if from_loss:
            y_ref, gate_ref, pg_ref, xin_ref, t_ref, dy_ref, dgate_ref, dpg_ref, dxo_ref, loss_ref = refs
        else:
            y_ref, gate_ref, pg_ref, dxo_in_ref, dy_ref, dgate_ref, dpg_ref = refs
        yv = y_ref[...]
        rp = lax.rsqrt(_rowmean(yv * yv) + EPS)
        yh = yv * rp
        fn = yh * pg_ref[...]
        gate = gate_ref[...]
        if from_loss:
            err = xin_ref[...] + gate * fn - t_ref[...]
            dxo = err * (1.0 / d)
            dxo_ref[...] = dxo
            part = 0.5 * jnp.sum(_rowmean(err * err), axis=0, keepdims=True)
            _accumulate(loss_ref, jnp.broadcast_to(part, loss_ref.shape))
        else:
            dxo = dxo_in_ref[...]
        _accumulate(dgate_ref, _colsum(dxo * fn))
        dfn = dxo * gate
        _accumulate(dpg_ref, _colsum(dfn * yh))
        dyh = dfn * pg_ref[...]
        dy_ref[...] = (rp * (dyh - yh * _rowmean(dyh * yh))).astype(BF16)

    blk = pl.BlockSpec((tb, d), lambda i: (i, 0))
    in_specs = [blk, _row_spec(d), _row_spec(d)]
    out_specs = [blk, _row_spec(d), _row_spec(d)]
    out_shape = [jax.ShapeDtypeStruct((s, d), BF16), jax.ShapeDtypeStruct((1, d), F32),
                 jax.ShapeDtypeStruct((1, d), F32)]
    if from_loss:
        operands = (y, gate, pg, xin, target)
        in_specs += [blk, blk]
        out_specs += [blk, _row_spec(LANES)]
        out_shape += [jax.ShapeDtypeStruct((s, d), F32), jax.ShapeDtypeStruct((1, LANES), F32)]
    else:
        operands = (y, gate, pg, dxo)
        in_specs += [blk]
    return pl.pallas_call(
        body, name=name, grid=(s // tb,), in_specs=in_specs, out_specs=out_specs, out_shape=out_shape,
        compiler_params=_params(),
    )(*operands)


def _prenorm_bwd(xin, dh, dres, g, scale, name):
    s, d = xin.shape
    tb = _div_tile(s, 256, SUBLANES)

    def body(x_ref, dh_ref, dres_ref, g_ref, sc_ref, dx_ref, dshift_ref, dscale_ref, dg_ref):
        xv = x_ref[...]
        r = lax.rsqrt(_rowmean(xv * xv) + EPS)
        xn = xv * r
        dh = dh_ref[...]
        g1 = g_ref[...]
        s1 = 1.0 + sc_ref[...]
        _accumulate(dshift_ref, _colsum(dh))
        _accumulate(dscale_ref, _colsum(dh * xn * g1))
        _accumulate(dg_ref, _colsum(dh * xn * s1))
        dxn = dh * g1 * s1
        dx_ref[...] = dres_ref[...] + r * (dxn - xn * _rowmean(dxn * xn))

    blk = pl.BlockSpec((tb, d), lambda i: (i, 0))
    return pl.pallas_call(
        body, name=name, grid=(s // tb,), in_specs=[blk, blk, blk, _row_spec(d), _row_spec(d)],
        out_specs=[blk, _row_spec(d), _row_spec(d), _row_spec(d)],
        out_shape=[jax.ShapeDtypeStruct((s, d), F32)] + [jax.ShapeDtypeStruct((1, d), F32)] * 3,
        compiler_params=_params(),
    )(xin, dh, dres, g, scale)


def _merge(z_big, y_a, y_b, name):
    s, d = y_a.shape
    tb = _div_tile(s, 256, SUBLANES)

    def body(zg_ref, ya_ref, yb_ref, o_ref):
        o_ref[...] = (_sigmoid(zg_ref[:, :d]) * ya_ref[...] + _sigmoid(zg_ref[:, d:]) * yb_ref[...]).astype(BF16)

    blk = pl.BlockSpec((tb, d), lambda i: (i, 0))
    return pl.pallas_call(
        body, name=name, grid=(s // tb,), in_specs=[pl.BlockSpec((tb, 2 * d), lambda i: (i, 1)), blk, blk],
        out_specs=blk, out_shape=jax.ShapeDtypeStruct((s, d), BF16), compiler_params=_params(),
    )(z_big, y_a, y_b)


def _merge_bwd(dmerged, z_big, y_a, y_b, name):
    s, d = y_a.shape
    tb = _div_tile(s, 256, SUBLANES)

    def body(dm_ref, zg_ref, ya_ref, yb_ref, dya_ref, dyb_ref, dz_ref):
        dm = dm_ref[...]
        sa, sb = _sigmoid(zg_ref[:, :d]), _sigmoid(zg_ref[:, d:])
        dya_ref[...] = (dm * sa).astype(BF16)
        dyb_ref[...] = (dm * sb).astype(BF16)
        dz_ref[:, :d] = (dm * ya_ref[...] * sa * (1.0 - sa)).astype(BF16)
        dz_ref[:, d:] = (dm * yb_ref[...] * sb * (1.0 - sb)).astype(BF16)

    blk = pl.BlockSpec((tb, d), lambda i: (i, 0))
    wide = pl.BlockSpec((tb, 2 * d), lambda i: (i, 1))
    return pl.pallas_call(
        body, name=name, grid=(s // tb,), in_specs=[blk, wide, blk, blk], out_specs=[blk, blk, wide],
        out_shape=[jax.ShapeDtypeStruct((s, d), BF16), jax.ShapeDtypeStruct((s, d), BF16),
                   jax.ShapeDtypeStruct((s, 4 * d), BF16)],
        compiler_params=_params(),
    )(dmerged, z_big, y_a, y_b)


def _causal_mask(ch):
    q = lax.broadcasted_iota(jnp.int32, (ch, ch), 0)
    p = lax.broadcasted_iota(jnp.int32, (ch, ch), 1)
    return (p <= q).astype(F32)


def _gmlp_norm(zc, lng, lnb, gw):
    u_pre, v_pre = zc[:, :gw], zc[:, gw:]
    vg = _gelu(v_pre)
    mu = _rowmean(vg)
    cen = vg - mu
    rstd = lax.rsqrt(_rowmean(cen * cen) + EPS)
    vhat = cen * rstd
    return u_pre, v_pre, _gelu(u_pre), vhat, rstd, vhat * lng + lnb


def _gmlp_fwd(z_big, ln_g, ln_b, w_s, b_s_t, name):
    s = z_big.shape[0]
    groups, ch, _ = w_s.shape
    gw = ln_g.shape[1]
    gd = gw // groups

    def body(z_ref, lng_ref, lnb_ref, ws_ref, bt_ref, a_ref):
        _, _, u, _, _, vn = _gmlp_norm(z_ref[...], lng_ref[...], lnb_ref[...], gw)
        mask = _causal_mask(ch)
        for g in range(groups):
            cols = slice(g * gd, (g + 1) * gd)
            wm = (ws_ref[g] * mask).astype(BF16)
            mixed = _dot(wm, vn[:, cols].astype(BF16), NN) + bt_ref[:, g:g + 1]
            a_ref[:, cols] = (u[:, cols] * mixed).astype(BF16)

    return pl.pallas_call(
        body, name=name, grid=(s // ch,),
        in_specs=[pl.BlockSpec((ch, 2 * gw), lambda n: (n, 0)), _row_spec(gw), _row_spec(gw),
                  pl.BlockSpec((groups, ch, ch), lambda n: (0, 0, 0)), pl.BlockSpec((ch, groups), lambda n: (0, 0))],
        out_specs=pl.BlockSpec((ch, gw), lambda n: (n, 0)),
        out_shape=jax.ShapeDtypeStruct((s, gw), BF16), compiler_params=_params(),
    )(z_big, ln_g, ln_b, w_s, b_s_t)


def _gmlp_bwd(z_big, da, dz_big, ln_g, ln_b, w_s, b_s_t, name):
    s = z_big.shape[0]
    groups, ch, _ = w_s.shape
    gw = ln_g.shape[1]
    gd = gw // groups

    def body(z_ref, da_ref, dzin_ref, lng_ref, lnb_ref, ws_ref, bt_ref, dz_ref, gws_ref, gbt_ref, glng_ref, glnb_ref):
        del dzin_ref
        lng = lng_ref[...]
        u_pre, v_pre, u, vhat, rstd, vn = _gmlp_norm(z_ref[...], lng, lnb_ref[...], gw)
        da = da_ref[...]
        mask = _causal_mask(ch)
        first = pl.program_id(0) == 0
        dvn_parts = []
        lane = lax.broadcasted_iota(jnp.int32, (ch, LANES), 1)
        gb = jnp.zeros((ch, LANES), F32)
        for g in range(groups):
            cols = slice(g * gd, (g + 1) * gd)
            wm = (ws_ref[g] * mask).astype(BF16)
            vn_g = vn[:, cols].astype(BF16)
            mixed = _dot(wm, vn_g, NN) + bt_ref[:, g:g + 1]
            dz_ref[:, cols] = (da[:, cols] * mixed * _gelu_grad(u_pre[:, cols])).astype(BF16)
            dmixed = da[:, cols] * u[:, cols]
            dm16 = dmixed.astype(BF16)
            dvn_parts.append(_dot(wm, dm16, TN))
            gws = _dot(dm16, vn_g, NT) * mask

            @pl.when(first)
            def _(g=g, gws=gws):
                gws_ref[g] = gws

            @pl.when(jnp.logical_not(first))
            def _(g=g, gws=gws):
                gws_ref[g] += gws

            gb = gb + jnp.where(lane == g, jnp.sum(dmixed, axis=1, keepdims=True), 0.0)
        _accumulate(gbt_ref, gb)
        dvn = jnp.concatenate(dvn_parts, axis=1)
        _accumulate(glnb_ref, _colsum(dvn))
        _accumulate(glng_ref, _colsum(dvn * vhat))
        dvh = dvn * lng
        dvg = rstd * (dvh - _rowmean(dvh) - vhat * _rowmean(dvh * vhat))
        dz_ref[:, gw:] = (dvg * _gelu_grad(v_pre)).astype(BF16)

    zspec = pl.BlockSpec((ch, 2 * gw), lambda n: (n, 0))
    return pl.pallas_call(
        body, name=name, grid=(s // ch,),
        in_specs=[zspec, pl.BlockSpec((ch, gw), lambda n: (n, 0)), pl.BlockSpec(memory_space=HBM),
                  _row_spec(gw), _row_spec(gw), pl.BlockSpec((groups, ch, ch), lambda n: (0, 0, 0)),
                  pl.BlockSpec((ch, groups), lambda n: (0, 0))],
        out_specs=[zspec, pl.BlockSpec((groups, ch, ch), lambda n: (0, 0, 0)),
                   pl.BlockSpec((ch, LANES), lambda n: (0, 0)), _row_spec(gw), _row_spec(gw)],
        out_shape=[jax.ShapeDtypeStruct(dz_big.shape, BF16), jax.ShapeDtypeStruct((groups, ch, ch), F32),
                   jax.ShapeDtypeStruct((ch, LANES), F32), jax.ShapeDtypeStruct((1, gw), F32),
                   jax.ShapeDtypeStruct((1, gw), F32)],
        input_output_aliases={2: 0}, compiler_params=_params(),
    )(z_big, da, dz_big, ln_g, ln_b, w_s, b_s_t)


def _mla_prep(z_lat, q_g, kv_g, rope_k, name):
    s, latw = z_lat.shape
    ql, kvl = q_g.shape[1], kv_g.shape[1]
    tb = _div_tile(s, 256, SUBLANES)

    def body(z_ref, qg_ref, kvg_ref, t_ref, qn_ref, kvn_ref, kr_ref):
        q = z_ref[:, :ql]
        qn_ref[...] = ((q * lax.rsqrt(_rowmean(q * q) + EPS)) * qg_ref[...]).astype(BF16)
        kv = z_ref[:, ql:ql + kvl]
        kvn_ref[...] = ((kv * lax.rsqrt(_rowmean(kv * kv) + EPS)) * kvg_ref[...]).astype(BF16)
        kk = z_ref[:, ql + kvl:] * t_ref[...]
        kr_ref[...] = (kk + pltpu.roll(kk, ROPE, axis=1)).astype(BF16)

    return pl.pallas_call(
        body, name=name, grid=(s // tb,),
        in_specs=[pl.BlockSpec((tb, latw), lambda i: (i, 0)), _row_spec(ql), _row_spec(kvl),
                  pl.BlockSpec((tb, 2 * ROPE), lambda i: (i, 0))],
        out_specs=[pl.BlockSpec((tb, ql), lambda i: (i, 0)), pl.BlockSpec((tb, kvl), lambda i: (i, 0)),
                   pl.BlockSpec((tb, 2 * ROPE), lambda i: (i, 0))],
        out_shape=[jax.ShapeDtypeStruct((s, ql), BF16), jax.ShapeDtypeStruct((s, kvl), BF16),
                   jax.ShapeDtypeStruct((s, 2 * ROPE), BF16)],
        compiler_params=_params(),
    )(z_lat, q_g, kv_g, rope_k)


def _scores(q, k, kr, row0, col0, scale):
    s = (_dot(q[:, :NOPE], k, NT) + _dot(q[:, NOPE:], kr, NT)) * scale
    rows = row0 + lax.broadcasted_iota(jnp.int32, s.shape, 0)
    cols = col0 + lax.broadcasted_iota(jnp.int32, s.shape, 1)
    return jnp.where(cols <= rows, s, -1e30)


def _attn_fwd(q, kv, kr, heads, name):
    s = q.shape[0]
    t = _div_tile(s, 512)
    nb = s // t
    scale = float(NOPE + ROPE) ** -0.5

    def body(q_ref, k_ref, kr_ref, v_ref, o_ref, lse_ref, m_ref, l_ref, acc_ref):
        i, j = pl.program_id(1), pl.program_id(2)

        @pl.when(j == 0)
        def _():
            m_ref[...] = jnp.full(m_ref.shape, -1e30, F32)
            l_ref[...] = jnp.zeros(l_ref.shape, F32)
            acc_ref[...] = jnp.zeros(acc_ref.shape, F32)

        @pl.when(j <= i)
        def _():
            sc = _scores(q_ref[...], k_ref[...], kr_ref[...], i * t, j * t, scale)
            m_old = m_ref[...]
            m_new = jnp.maximum(m_old, jnp.max(sc, axis=-1, keepdims=True))
            p = jnp.exp(sc - m_new)
            alpha = jnp.exp(m_old - m_new)
            l_ref[...] = alpha * l_ref[...] + jnp.sum(p, axis=-1, keepdims=True)
            acc_ref[...] = alpha * acc_ref[...] + _dot(p.astype(BF16), v_ref[...], NN)
            m_ref[...] = m_new

        @pl.when(j == i)
        def _():
            o_ref[...] = (acc_ref[...] / l_ref[...]).astype(BF16)
            lse_ref[...] = jnp.broadcast_to(m_ref[...] + jnp.log(l_ref[...]), lse_ref.shape)

    kidx = lambda off: (lambda h, i, j: (jnp.minimum(i, j), off(h)))
    return pl.pallas_call(
        body, name=name, grid=(heads, nb, nb),
        in_specs=[pl.BlockSpec((t, HEAD_W), lambda h, i, j: (i, h)),
                  pl.BlockSpec((t, NOPE), kidx(lambda h: h)),
                  pl.BlockSpec((t, 2 * ROPE), kidx(lambda h: 0)),
                  pl.BlockSpec((t, VHEAD), kidx(lambda h: heads + h))],
        out_specs=[pl.BlockSpec((t, VHEAD), lambda h, i, j: (i, h)),
                   pl.BlockSpec((None, t, LANES), lambda h, i, j: (h, i, 0))],
        out_shape=[jax.ShapeDtypeStruct((s, heads * VHEAD), BF16), jax.ShapeDtypeStruct((heads, s, LANES), F32)],
        scratch_shapes=[pltpu.VMEM((t, 1), F32), pltpu.VMEM((t, 1), F32), pltpu.VMEM((t, VHEAD), F32)],
        compiler_params=_params(),
    )(q, kv, kr, kv)


def _attn_bwd(q, kv, kr, o, do, lse, heads, name):
    s = q.shape[0]
    t = _div_tile(s, 512)
    nb = s // t
    scale = float(NOPE + ROPE) ** -0.5

    def body(q_ref, k_ref, kr_ref, v_ref, o_ref, do_ref, lse_ref, dq_ref, dk_ref, dv_ref, dk_acc, dv_acc):
        j, i = pl.program_id(1), pl.program_id(2)

        @pl.when(jnp.logical_and(j == 0, i == 0))
        def _():
            dq_ref[...] = jnp.zeros(dq_ref.shape, F32)

        @pl.when(i == j)
        def _():
            dk_acc[...] = jnp.zeros(dk_acc.shape, F32)
            dv_acc[...] = jnp.zeros(dv_acc.shape, F32)

        @pl.when(i >= j)
        def _():
            qv, kn, krv, do_v = q_ref[...], k_ref[...], kr_ref[...], do_ref[...]
            sc = _scores(qv, kn, krv, i * t, j * t, scale)
            p = jnp.exp(sc - lse_ref[:, :1])
            dv_acc[...] += _dot(p.astype(BF16), do_v, TN)
            dp = _dot(do_v, v_ref[...], NT)
            delta = jnp.sum(do_v.astype(F32) * o_ref[...].astype(F32), axis=-1, keepdims=True)
            ds = (p * (dp - delta) * scale).astype(BF16)
            rows = pl.ds(pl.multiple_of(i * t, t), t)
            dq_ref[rows, :NOPE] += _dot(ds, kn, NN)
            dq_ref[rows, NOPE:] += _dot(ds, krv, NN)
            dk_acc[...] += _dot(ds, qv, TN)

        @pl.when(i == nb - 1)
        def _():
            dk_ref[...] = dk_acc[...].astype(BF16)
            dv_ref[...] = dv_acc[...].astype(BF16)

    qidx = lambda h, j, i: (jnp.maximum(i, j), h)
    return pl.pallas_call(
        body, name=name, grid=(heads, nb, nb),
        in_specs=[pl.BlockSpec((t, HEAD_W), qidx),
                  pl.BlockSpec((t, NOPE), lambda h, j, i: (j, h)),
                  pl.BlockSpec((t, 2 * ROPE), lambda h, j, i: (j, 0)),
                  pl.BlockSpec((t, VHEAD), lambda h, j, i: (j, heads + h)),
                  pl.BlockSpec((t, VHEAD), qidx), pl.BlockSpec((t, VHEAD), qidx),
                  pl.BlockSpec((None, t, LANES), lambda h, j, i: (h, jnp.maximum(i, j), 0))],
        out_specs=[pl.BlockSpec((s, HEAD_W), lambda h, j, i: (0, h)),
                   pl.BlockSpec((t, HEAD_W), lambda h, j, i: (j, h)),
                   pl.BlockSpec((t, VHEAD), lambda h, j, i: (j, h))],
        out_shape=[jax.ShapeDtypeStruct((s, heads * HEAD_W), F32), jax.ShapeDtypeStruct((s, heads * HEAD_W), BF16),
                   jax.ShapeDtypeStruct((s, heads * VHEAD), BF16)],
        scratch_shapes=[pltpu.VMEM((t, HEAD_W), F32), pltpu.VMEM((t, VHEAD), F32)],
        compiler_params=_params(),
    )(q, kv, kr, kv, o, do, lse)


def _mla_bwd_mid(dq, dk, dv, rope_q, rope_k, heads, name):
    s = dq.shape[0]
    tb = _div_tile(s, 256, SUBLANES)

    def body(dq_ref, dk_ref, dv_ref, tq_ref, tk_ref, dqb_ref, dkv_ref, dkk_ref):
        tq = tq_ref[...]
        dkr = jnp.zeros((tb, 2 * ROPE), F32)
        for h in range(heads):
            cols = slice(h * HEAD_W, (h + 1) * HEAD_W)
            dqb_ref[:, cols] = (dq_ref[:, cols] * tq).astype(BF16)
            dkv_ref[:, h * NOPE:(h + 1) * NOPE] = dk_ref[:, h * HEAD_W:h * HEAD_W + NOPE]
            dkr = dkr + dk_ref[:, h * HEAD_W + NOPE:(h + 1) * HEAD_W].astype(F32)
        dkv_ref[:, heads * NOPE:] = dv_ref[...]
        dkk_ref[...] = (dkr + pltpu.roll(dkr, ROPE, axis=1)) * tk_ref[...]

    wq, wv = heads * HEAD_W, heads * VHEAD
    return pl.pallas_call(
        body, name=name, grid=(s // tb,),
        in_specs=[pl.BlockSpec((tb, wq), lambda i: (i, 0)), pl.BlockSpec((tb, wq), lambda i: (i, 0)),
                  pl.BlockSpec((tb, wv), lambda i: (i, 0)), pl.BlockSpec((tb, HEAD_W), lambda i: (i, 0)),
                  pl.BlockSpec((tb, 2 * ROPE), lambda i: (i, 0))],
        out_specs=[pl.BlockSpec((tb, wq), lambda i: (i, 0)), pl.BlockSpec((tb, heads * NOPE + wv), lambda i: (i, 0)),
                   pl.BlockSpec((tb, 2 * ROPE), lambda i: (i, 0))],
        out_shape=[jax.ShapeDtypeStruct((s, wq), BF16), jax.ShapeDtypeStruct((s, heads * NOPE + wv), BF16),
                   jax.ShapeDtypeStruct((s, 2 * ROPE), F32)],
        compiler_params=_params(),
    )(dq, dk, dv, rope_q, rope_k)


def _mla_bwd_post(z_lat, dqn, dkvn, dkk, q_g, kv_g, name):
    s, latw = z_lat.shape
    ql, kvl = q_g.shape[1], kv_g.shape[1]
    tb = _div_tile(s, 256, SUBLANES)

    def norm_bwd(xv, dn, g, dg_ref):
        r = lax.rsqrt(_rowmean(xv * xv) + EPS)
        xh = xv * r
        _accumulate(dg_ref, _colsum(dn * xh))
        dxh = dn * g
        return r * (dxh - xh * _rowmean(dxh * xh))

    def body(z_ref, dqn_ref, dkvn_ref, dkk_ref, qg_ref, kvg_ref, dz_ref, gq_ref, gkv_ref):
        dz_ref[:, :ql] = norm_bwd(z_ref[:, :ql], dqn_ref[...], qg_ref[...], gq_ref).astype(BF16)
        dz_ref[:, ql:ql + kvl] = norm_bwd(z_ref[:, ql:ql + kvl], dkvn_ref[...], kvg_ref[...], gkv_ref).astype(BF16)
        dz_ref[:, ql + kvl:] = dkk_ref[...].astype(BF16)

    return pl.pallas_call(
        body, name=name, grid=(s // tb,),
        in_specs=[pl.BlockSpec((tb, latw), lambda i: (i, 0)), pl.BlockSpec((tb, ql), lambda i: (i, 0)),
                  pl.BlockSpec((tb, kvl), lambda i: (i, 0)), pl.BlockSpec((tb, 2 * ROPE), lambda i: (i, 0)),
                  _row_spec(ql), _row_spec(kvl)],
        out_specs=[pl.BlockSpec((tb, latw), lambda i: (i, 0)), _row_spec(ql), _row_spec(kvl)],
        out_shape=[jax.ShapeDtypeStruct((s, latw), BF16), jax.ShapeDtypeStruct((1, ql), F32),
                   jax.ShapeDtypeStruct((1, kvl), F32)],
        compiler_params=_params(),
    )(z_lat, dqn, dkvn, dkk, q_g, kv_g)


def _shift_down(x, n):
    rows = lax.broadcasted_iota(jnp.int32, x.shape, 0)
    return jnp.where(rows >= n, pltpu.roll(x, n, axis=0), 0.0)


def _shift_up(x, n):
    s = x.shape[0]
    rows = lax.broadcasted_iota(jnp.int32, x.shape, 0)
    return jnp.where(rows < s - n, pltpu.roll(x, s - n, axis=0), 0.0)


def _conv(pre, w_ref, b_ref):
    return (w_ref[2:3, :] * pre + w_ref[1:2, :] * _shift_down(pre, 1) + w_ref[0:1, :] * _shift_down(pre, 2)
            + b_ref[...])


def _conv_fwd(up_pre, conv_w, conv_b, name):
    s, ff2 = up_pre.shape
    ff = ff2 // 2
    tc = _div_tile(ff, 256)
    nb = ff // tc

    def body(pg_ref, pv_ref, wg_ref, wv_ref, bg_ref, bv_ref, act_ref):
        gate = _conv(pg_ref[...].astype(F32), wg_ref, bg_ref)
        val = _conv(pv_ref[...].astype(F32), wv_ref, bv_ref)
        act_ref[...] = (gate * _sigmoid(gate) * val).astype(BF16)

    def col(rows, off):
        return pl.BlockSpec((rows, tc), lambda j: (0, j + off))

    return pl.pallas_call(
        body, name=name, grid=(nb,),
        in_specs=[col(s, 0), col(s, nb), col(CONV_TAPS, 0), col(CONV_TAPS, nb), col(1, 0), col(1, nb)],
        out_specs=col(s, 0), out_shape=jax.ShapeDtypeStruct((s, ff), BF16), compiler_params=_params(),
    )(up_pre, up_pre, conv_w, conv_w, conv_b, conv_b)


def _conv_bwd(up_pre, dact, conv_w, conv_b, name):
    s, ff2 = up_pre.shape
    ff = ff2 // 2
    tc = _div_tile(ff, 256)
    nb = ff // tc

    def half(pre, dx, w_ref, dpre_ref, gw_ref, gb_ref):
        gb_ref[...] = _colsum(dx)
        gw_ref[0:1, :] = _colsum(dx * _shift_down(pre, 2))
        gw_ref[1:2, :] = _colsum(dx * _shift_down(pre, 1))
        gw_ref[2:3, :] = _colsum(dx * pre)
        dpre_ref[...] = (w_ref[2:3, :] * dx + w_ref[1:2, :] * _shift_up(dx, 1)
                         + w_ref[0:1, :] * _shift_up(dx, 2)).astype(BF16)

    def body(pg_ref, pv_ref, da_ref, wg_ref, wv_ref, bg_ref, bv_ref, dg_ref, dv_ref, gwg_ref, gwv_ref, gbg_ref, gbv_ref):
        pre_g, pre_v = pg_ref[...].astype(F32), pv_ref[...].astype(F32)
        gate = _conv(pre_g, wg_ref, bg_ref)
        val = _conv(pre_v, wv_ref, bv_ref)
        da = da_ref[...].astype(F32)
        sg = _sigmoid(gate)
        half(pre_v, da * gate * sg, wv_ref, dv_ref, gwv_ref, gbv_ref)
        half(pre_g, da * val * sg * (1.0 + gate * (1.0 - sg)), wg_ref, dg_ref, gwg_ref, gbg_ref)

    def col(rows, off):
        return pl.BlockSpec((rows, tc), lambda j: (0, j + off))

    return pl.pallas_call(
        body, name=name, grid=(nb,),
        in_specs=[col(s, 0), col(s, nb), col(s, 0), col(CONV_TAPS, 0), col(CONV_TAPS, nb), col(1, 0), col(1, nb)],
        out_specs=[col(s, 0), col(s, 0), col(CONV_TAPS, 0), col(CONV_TAPS, 0), col(1, 0), col(1, 0)],
        out_shape=[jax.ShapeDtypeStruct((s, ff), BF16)] * 2 + [jax.ShapeDtypeStruct((CONV_TAPS, ff), F32)] * 2
        + [jax.ShapeDtypeStruct((1, ff), F32)] * 2,
        compiler_params=_params(),
    )(up_pre, up_pre, dact, conv_w, conv_w, conv_b, conv_b)


def _ada_fwd(c_all, w, b, name):
    nseq, d = c_all.shape
    na = w.shape[1]
    tn = _div_tile(na, 512)

    def body(c_ref, w_ref, b_ref, o_ref):
        cv = c_ref[...]
        sc = cv * _sigmoid(cv)
        o_ref[...] = jnp.dot(sc, w_ref[...], preferred_element_type=F32, precision=lax.Precision.HIGHEST) + b_ref[...]

    return pl.pallas_call(
        body, name=name, grid=(na // tn,),
        in_specs=[pl.BlockSpec((nseq, d), lambda j: (0, 0)), pl.BlockSpec((d, tn), lambda j: (0, j)),
                  pl.BlockSpec((1, tn), lambda j: (0, j))],
        out_specs=pl.BlockSpec((nseq, tn), lambda j: (0, j)),
        out_shape=jax.ShapeDtypeStruct((nseq, na), F32), compiler_params=_params(),
    )(c_all, w, b)


def _ada_bwd(c_all_t, dmod, name):
    d, nseq = c_all_t.shape
    na = dmod.shape[1]
    tm, tn = _div_tile(d, 256, SUBLANES), _div_tile(na, 512)

    def body(c_ref, dm_ref, o_ref):
        cv = c_ref[...]
        sc = cv * _sigmoid(cv)
        acc = sc[:, 0:1] * dm_ref[0:1, :]
        for bi in range(1, nseq):
            acc = acc + sc[:, bi:bi + 1] * dm_ref[bi:bi + 1, :]
        o_ref[...] = acc

    return pl.pallas_call(
        body, name=name, grid=(d // tm, na // tn),
        in_specs=[pl.BlockSpec((tm, nseq), lambda i, j: (i, 0)), pl.BlockSpec((nseq, tn), lambda i, j: (0, j))],
        out_specs=pl.BlockSpec((tm, tn), lambda i, j: (i, j)),
        out_shape=jax.ShapeDtypeStruct((d, na), F32), compiler_params=_params(),
    )(c_all_t, dmod)


def _adamw(w, g, m, v, name):
    rows, cols = w.shape
    tb = _div_tile(rows, max(SUBLANES, (256 * 1024) // cols // SUBLANES * SUBLANES), SUBLANES)
    c1 = 1.0 / (1.0 - ADAM_B1 ** ADAM_STEP)
    c2 = 1.0 / (1.0 - ADAM_B2 ** ADAM_STEP)

    def body(w_ref, g_ref, m_ref, v_ref, d_ref, nm_ref, nv_ref):
        gv = g_ref[...]
        nm = ADAM_B1 * m_ref[...] + (1.0 - ADAM_B1) * gv
        nv = ADAM_B2 * v_ref[...] + (1.0 - ADAM_B2) * (gv * gv)
        nm_ref[...] = nm
        nv_ref[...] = nv
        d_ref[...] = -ADAM_LR * ((nm * c1) / (jnp.sqrt(nv * c2) + ADAM_EPS) + ADAM_WD * w_ref[...])

    blk = pl.BlockSpec((tb, cols), lambda i: (i, 0))
    return pl.pallas_call(
        body, name=name, grid=(rows // tb,), in_specs=[blk] * 4, out_specs=[blk] * 3,
        out_shape=[jax.ShapeDtypeStruct((rows, cols), F32)] * 3, compiler_params=_params(),
    )(w, g, m, v)


def _sum_leading(parts, name):
    n, rows, cols = parts.shape
    tb = _div_tile(rows, 512, SUBLANES)

    def body(p_ref, o_ref):
        acc = p_ref[0]
        for k in range(1, n):
            acc = acc + p_ref[k]
        o_ref[...] = acc

    return pl.pallas_call(
        body, name=name, grid=(rows // tb,), in_specs=[pl.BlockSpec((n, tb, cols), lambda i: (0, i, 0))],
        out_specs=pl.BlockSpec((tb, cols), lambda i: (i, 0)),
        out_shape=jax.ShapeDtypeStruct((rows, cols), F32), compiler_params=_params(),
    )(parts)


def _place():
    x, y, c = lax.axis_index("x"), lax.axis_index("y"), lax.axis_index("c")
    return x, y, c, [(1 - x, y), (x, 1 - y), (1 - x, 1 - y)]


def _all_gather(block, name):
    m_per, n = block.shape

    def body(x_ref, out_ref, send_sems, recv_sems, local_sem):
        x, y, c, chips = _place()
        me, sibling = (x, y, c), (x, y, 1 - c)

        def rows(px, py, pc):
            return out_ref.at[pl.ds((4 * px + 2 * py + pc) * m_per, m_per), :]

        def copy(k, blk, to, src=None):
            return pltpu.make_async_remote_copy(
                src_ref=rows(*blk) if src is None else src, dst_ref=rows(*blk), send_sem=send_sems.at[k],
                recv_sem=recv_sems.at[k], device_id=to, device_id_type=MESH)

        mine = pltpu.make_async_copy(x_ref, rows(*me), local_sem)
        mine.start()
        first = [copy(0, me, sibling, src=x_ref)]
        first += [copy(1 + j, me, (*chip, c), src=x_ref) for j, chip in enumerate(chips)]
        for cp in first:
            cp.start()
        passed = [copy(4 + j, (*chip, c), sibling) for j, chip in enumerate(chips)]
        for j, chip in enumerate(chips):
            copy(1 + j, (*chip, c), me).wait_recv()
            passed[j].start()
        copy(0, sibling, me).wait_recv()
        for j, chip in enumerate(chips):
            copy(4 + j, (*chip, 1 - c), me).wait_recv()
        for cp in first + passed:
            cp.wait_send()
        mine.wait()

    return pl.pallas_call(
        body, name=name, out_shape=jax.ShapeDtypeStruct((N_DEV * m_per, n), block.dtype),
        in_specs=[pl.BlockSpec(memory_space=pltpu.VMEM)], out_specs=pl.BlockSpec(memory_space=pltpu.VMEM),
        scratch_shapes=[pltpu.SemaphoreType.DMA((7,)), pltpu.SemaphoreType.DMA((7,)), pltpu.SemaphoreType.DMA],
        compiler_params=_params(),
    )(block)


def _gather_weights(mine, name):
    _, r, cols = mine.shape

    def body(mine_ref, out_ref, send_sems, recv_sems, local_sem):
        x, y, c, chips = _place()
        me, sibling = (x, y, c), (x, y, 1 - c)

        def blk(px, py, half):
            return out_ref.at[2 * px + py, half]

        def copy(k, block, to, src=None):
            return pltpu.make_async_remote_copy(
                src_ref=blk(*block) if src is None else src, dst_ref=blk(*block), send_sem=send_sems.at[k],
                recv_sem=recv_sems.at[k], device_id=to, device_id_type=MESH)

        local = pltpu.make_async_copy(mine_ref, out_ref.at[2 * x + y], local_sem)
        local.start()
        first = [copy(j, me, (*chip, c), src=mine_ref.at[c]) for j, chip in enumerate(chips)]
        for cp in first:
            cp.start()
        passed = [copy(3 + j, (*chip, c), sibling) for j, chip in enumerate(chips)]
        for j, chip in enumerate(chips):
            copy(j, (*chip, c), me).wait_recv()
            passed[j].start()
        for j, chip in enumerate(chips):
            copy(3 + j, (*chip, 1 - c), me).wait_recv()
        for cp in first + passed:
            cp.wait_send()
        local.wait()

    return pl.pallas_call(
        body, name=name, out_shape=jax.ShapeDtypeStruct((N_CHIPS, N_CORES, r, cols), mine.dtype),
        in_specs=[pl.BlockSpec(memory_space=HBM)], out_specs=pl.BlockSpec(memory_space=HBM),
        scratch_shapes=[pltpu.SemaphoreType.DMA((6,)), pltpu.SemaphoreType.DMA((6,)), pltpu.SemaphoreType.DMA],
        compiler_params=_params(),
    )(mine)


def _swap_halves(g, name):
    _, nch, r, cols = g.shape

    def body(g_ref, out_ref, send_sem, recv_sem):
        x, y, c, _ = _place()
        cp = pltpu.make_async_remote_copy(src_ref=g_ref.at[1 - c], dst_ref=out_ref, send_sem=send_sem,
                                          recv_sem=recv_sem, device_id=(x, y, 1 - c), device_id_type=MESH)
        cp.start()
        cp.wait()

    return pl.pallas_call(
        body, name=name, out_shape=jax.ShapeDtypeStruct((nch, r, cols), g.dtype),
        in_specs=[pl.BlockSpec(memory_space=HBM)], out_specs=pl.BlockSpec(memory_space=HBM),
        scratch_shapes=[pltpu.SemaphoreType.DMA, pltpu.SemaphoreType.DMA], compiler_params=_params(),
    )(g)


def _exchange_chips(s1, name):
    _, r, cols = s1.shape

    def body(s_ref, out_ref, send_sems, recv_sems):
        x, y, c, chips = _place()
        cps = [pltpu.make_async_remote_copy(
            src_ref=s_ref.at[2 * chip[0] + chip[1]], dst_ref=out_ref.at[j], send_sem=send_sems.at[j],
            recv_sem=recv_sems.at[j], device_id=(*chip, c), device_id_type=MESH) for j, chip in enumerate(chips)]
        for cp in cps:
            cp.start()
        for cp in cps:
            cp.wait()

    return pl.pallas_call(
        body, name=name, out_shape=jax.ShapeDtypeStruct((N_CHIPS - 1, r, cols), s1.dtype),
        in_specs=[pl.BlockSpec(memory_space=HBM)], out_specs=pl.BlockSpec(memory_space=HBM),
        scratch_shapes=[pltpu.SemaphoreType.DMA((3,)), pltpu.SemaphoreType.DMA((3,))], compiler_params=_params(),
    )(s1)


def _share_halves(f, name):
    r, cols = f.shape

    def body(f_ref, out_ref, send_sem, recv_sem, local_sem):
        x, y, c, _ = _place()
        local = pltpu.make_async_copy(f_ref, out_ref.at[c], local_sem)
        local.start()
        send = pltpu.make_async_remote_copy(src_ref=f_ref, dst_ref=out_ref.at[c], send_sem=send_sem,
                                            recv_sem=recv_sem, device_id=(x, y, 1 - c), device_id_type=MESH)
        send.start()
        pltpu.make_async_remote_copy(src_ref=f_ref, dst_ref=out_ref.at[1 - c], send_sem=send_sem, recv_sem=recv_sem,
                                     device_id=(x, y, 1 - c), device_id_type=MESH).wait_recv()
        send.wait_send()
        local.wait()

    return pl.pallas_call(
        body, name=name, out_shape=jax.ShapeDtypeStruct((N_CORES, r, cols), f.dtype),
        in_specs=[pl.BlockSpec(memory_space=HBM)], out_specs=pl.BlockSpec(memory_space=HBM),
        scratch_shapes=[pltpu.SemaphoreType.DMA, pltpu.SemaphoreType.DMA, pltpu.SemaphoreType.DMA],
        compiler_params=_params(),
    )(f)


def _add_sibling(g, r1, core, name):
    _, nch, r, cols = g.shape
    tr = _div_tile(r, PACK_ROW_TILE, 2 * SUBLANES)

    def body(core_ref, g_ref, r_ref, o_ref):
        del core_ref
        o_ref[...] = (g_ref[...].astype(F32) + r_ref[...].astype(F32)).astype(BF16)

    spec = pltpu.PrefetchScalarGridSpec(
        num_scalar_prefetch=1, grid=(nch, r // tr),
        in_specs=[pl.BlockSpec((None, None, tr, cols), lambda k, i, core_ref: (core_ref[0], k, i, 0)),
                  pl.BlockSpec((None, tr, cols), lambda k, i, core_ref: (k, i, 0))],
        out_specs=pl.BlockSpec((None, tr, cols), lambda k, i, core_ref: (k, i, 0)))
    return pl.pallas_call(body, name=name, grid_spec=spec, out_shape=jax.ShapeDtypeStruct((nch, r, cols), BF16),
                          compiler_params=_params())(core, g, r1)


def _add_chips(s1, r2, chip, name):
    _, r, cols = s1.shape
    tr = _div_tile(r, PACK_ROW_TILE, 2 * SUBLANES)

    def body(chip_ref, s_ref, r_ref, o_ref):
        del chip_ref
        acc = s_ref[...].astype(F32)
        for j in range(N_CHIPS - 1):
            acc = acc + r_ref[j].astype(F32)
        o_ref[...] = acc

    spec = pltpu.PrefetchScalarGridSpec(
        num_scalar_prefetch=1, grid=(r // tr,),
        in_specs=[pl.BlockSpec((None, tr, cols), lambda i, chip_ref: (chip_ref[0], i, 0)),
                  pl.BlockSpec((N_CHIPS - 1, tr, cols), lambda i, chip_ref: (0, i, 0))],
        out_specs=pl.BlockSpec((tr, cols), lambda i, chip_ref: (i, 0)))
    return pl.pallas_call(body, name=name, grid_spec=spec, out_shape=jax.ShapeDtypeStruct((r, cols), F32),
                          compiler_params=_params())(chip, s1, r2)


def _quarter_turn(m):
    h = m.shape[-1] // 2
    return jnp.concatenate([-m[..., h:], m[..., :h]], axis=-1)


def _quarter_turn_back(m):
    h = m.shape[-1] // 2
    return jnp.concatenate([m[..., h:], -m[..., :h]], axis=-1)


def _pack_rows(n_elems):
    rows = -(-n_elems // (N_CORES * PACK_COLS))
    return -(-rows // PACK_ROW_TILE) * PACK_ROW_TILE


def _pack(flats, dtype):
    flat = jnp.concatenate(flats, axis=1).astype(dtype)
    r = _pack_rows(flat.shape[1])
    flat = jnp.pad(flat, ((0, 0), (0, N_CORES * r * PACK_COLS - flat.shape[1])))
    return flat.reshape(flat.shape[0], N_CORES, r, PACK_COLS)


def _to_shards(name, full):
    rows, cols = full.shape
    if name in COL_SHARDED:
        return full.reshape(rows, N_CHIPS, cols // N_CHIPS).transpose(1, 0, 2).reshape(N_CHIPS, -1)
    return full.reshape(N_CHIPS, -1)


def _from_shards(name, flat, shard_shape):
    rows, cols = shard_shape
    sh = flat.reshape(N_CHIPS, rows, cols)
    if name in COL_SHARDED:
        return sh.transpose(1, 0, 2).reshape(rows, N_CHIPS * cols)
    return sh.reshape(N_CHIPS * rows, cols)


def kernel(x, c, positions, w_ada, b_ada, pre_norm1_g, w_in, gm_ln_g, gm_ln_b, gm_w_s, gm_b_s, w_branch_a, q_norm_g, w_uq, kv_norm_g, w_ukv, w_branch_b, w_out, post_norm1_g, pre_norm2_g, w_up, conv_w, conv_b, w_down, post_norm2_g, loss_target, m_w_ada, m_b_ada, m_pre_norm1_g, m_w_in, m_gm_ln_g, m_gm_ln_b, m_gm_w_s, m_gm_b_s, m_w_branch_a, m_q_norm_g, m_w_uq, m_kv_norm_g, m_w_ukv, m_w_branch_b, m_w_out, m_post_norm1_g, m_pre_norm2_g, m_w_up, m_conv_w, m_conv_b, m_w_down, m_post_norm2_g, v_w_ada, v_b_ada, v_pre_norm1_g, v_w_in, v_gm_ln_g, v_gm_ln_b, v_gm_w_s, v_gm_b_s, v_w_branch_a, v_q_norm_g, v_w_uq, v_kv_norm_g, v_w_ukv, v_w_branch_b, v_w_out, v_post_norm1_g, v_pre_norm2_g, v_w_up, v_conv_w, v_conv_b, v_w_down, v_post_norm2_g):
    given = dict(locals())
    s, d = x.shape[1], x.shape[2]
    gw = gm_ln_g.shape[0]
    ql, kvl = q_norm_g.shape[0], kv_norm_g.shape[0]
    heads = N_CHIPS * w_uq.shape[1] // (NOPE + ROPE)
    ff = N_CHIPS * w_down.shape[0]
    assert gw == d and N_CHIPS * w_ukv.shape[1] == heads * (NOPE + VHEAD)
    ix, iy, ic = lax.axis_index("x"), lax.axis_index("y"), lax.axis_index("c")
    chip = 2 * ix + iy
    dev = 2 * chip + ic
    row = lambda v: v.reshape(1, -1)

    c_all = _all_gather(jnp.pad(c, ((0, SUBLANES - 1), (0, 0))), "gather_c").reshape(N_DEV, SUBLANES, d)[:, 0]
    na = w_ada.shape[1]
    b_ada_mine = lax.dynamic_slice(b_ada, (chip * na,), (na,))
    mod_cols = _ada_fwd(c_all, w_ada, row(b_ada_mine), "ada_fwd")
    mod_all = _all_gather(mod_cols, "gather_mod").reshape(N_CHIPS, N_CORES, N_DEV, na)[:, 0]
    mod = lax.dynamic_index_in_dim(mod_all, dev, axis=1, keepdims=False).reshape(N_MOD, d)
    shift1, scale1, gate1, shift2, scale2, gate2 = (mod[i:i + 1] for i in range(N_MOD))

    shard_shapes = {n: given[n].shape for n in BIG}
    sizes = [shard_shapes[n][0] * shard_shapes[n][1] for n in BIG]
    mine = _pack([given[n].reshape(1, -1) for n in BIG], BF16)[0]
    gathered = _gather_weights(mine, "gather_weights").reshape(N_CHIPS, -1)
    full, off = {}, 0
    for n, size in zip(BIG, sizes):
        full[n] = _from_shards(n, gathered[:, off:off + size], shard_shapes[n])
        off += size
    wi = full["w_in"]
    o_q, o_kv, o_pe, o_ga = 2 * gw, 2 * gw + ql, 2 * gw + ql + kvl, 2 * gw + ql + kvl + ROPE
    w_in_big = jnp.concatenate([wi[:, :o_q], wi[:, o_ga:]], axis=1)
    w_in_lat = jnp.concatenate([wi[:, o_q:o_ga], _quarter_turn(wi[:, o_pe:o_ga])], axis=1)
    wq = full["w_uq"].reshape(ql, heads, NOPE + ROPE)
    w_q = jnp.concatenate([wq, _quarter_turn(wq[:, :, NOPE:])], axis=2).reshape(ql, heads * HEAD_W)
    w_kv = full["w_ukv"].reshape(kvl, heads, 2, NOPE).transpose(0, 2, 1, 3).reshape(kvl, 2 * heads * NOPE)
    w_a, w_b, w_o, w_upf, w_dn = full["w_branch_a"], full["w_branch_b"], full["w_out"], full["w_up"], full["w_down"]

    inv = ROPE_THETA ** (-jnp.arange(0, ROPE, 2, dtype=F32) / ROPE)
    ang = positions[0].astype(F32)[:, None] * inv
    cos, sin = jnp.cos(ang), jnp.sin(ang)
    rope_k = jnp.concatenate([cos, cos, sin, sin], axis=1)
    rope_q = jnp.concatenate([jnp.ones((s, NOPE), F32), rope_k], axis=1)

    x2d, tgt = x[0], loss_target[0]
    g_pre1, g_post1, g_pre2, g_post2 = row(pre_norm1_g), row(post_norm1_g), row(pre_norm2_g), row(post_norm2_g)
    ln_g, ln_b, q_g, kv_g = row(gm_ln_g), row(gm_ln_b), row(q_norm_g), row(kv_norm_g)
    b_s_t = gm_b_s.T
    conv_wf = _all_gather(jnp.pad(conv_w, ((0, SUBLANES - CONV_TAPS), (0, 0))), "gather_conv_w")
    conv_wf = conv_wf.reshape(N_CHIPS, N_CORES, SUBLANES, conv_w.shape[1])[:, 0, :CONV_TAPS]
    conv_wf = conv_wf.transpose(1, 0, 2).reshape(CONV_TAPS, 2 * ff)
    conv_bf = row(conv_b)

    h1 = _prenorm(x2d, g_pre1, scale1, shift1, "prenorm1")
    z_big = _matmul(h1, w_in_big, mode="nn", out_dtype=F32, name="mm_z_big", tm=s)
    z_lat = _matmul(h1, w_in_lat, mode="nn", out_dtype=F32, name="mm_z_lat", tm=s, tn=1024)
    a_act = _gmlp_fwd(z_big, ln_g, ln_b, gm_w_s, b_s_t, "gmlp_fwd")
    qn, kvn, kr = _mla_prep(z_lat, q_g, kv_g, rope_k, "mla_prep")
    q_rot = _matmul(qn, w_q, mode="nn", out_dtype=BF16, name="mm_q", tm=s, tn=HEAD_W, mul=rope_q)
    kv_all = _matmul(kvn, w_kv, mode="nn", out_dtype=BF16, name="mm_kv", tm=s, tn=1024)
    o_att, lse = _attn_fwd(q_rot, kv_all, kr, heads, "attn_fwd")
    y_a = _matmul(a_act, w_a, mode="nn", out_dtype=F32, name="mm_y_a", tm=s)
    y_b = _matmul(o_att, w_b, mode="nn", out_dtype=F32, name="mm_y_b", tm=s)
    merged = _merge(z_big, y_a, y_b, "merge")
    y1 = _matmul(merged, w_o, mode="nn", out_dtype=F32, name="mm_y1", tm=s)
    x1, h2 = _post_pre(x2d, y1, gate1, g_post1, g_pre2, scale2, shift2, "post1_pre2")

    up_pre = _matmul(h2, w_upf, mode="nn", out_dtype=BF16, name="mm_up", tm=s)
    act = _conv_fwd(up_pre, conv_wf, conv_bf, "conv_fwd")
    ffn = _matmul(act, w_dn, mode="nn", out_dtype=F32, name="mm_ffn", tm=s, tk=1408)

    dffn, dgate2, g_post2_grad, dx2, loss_part = _post_bwd(ffn, gate2, g_post2, "post2_bwd", xin=x1, target=tgt)
    loss = lax.psum(loss_part[0, 0], ("x", "y", "c"))
    dact = _matmul(dffn, w_dn, mode="nt", out_dtype=BF16, name="mm_dact", tm=s)
    gw_down = _matmul(act, dffn, mode="tn", out_dtype=BF16, name="mm_gw_down", tn=1024, tk=s)
    dup_g, dup_v, gcw_g, gcw_v, gcb_g, gcb_v = _conv_bwd(up_pre, dact, conv_wf, conv_bf, "conv_bwd")
    tk_up = _div_tile(ff, 1408)
    dh2 = _matmul(dup_g, w_upf, mode="nt", out_dtype=F32, name="mm_dh2_g", tm=s, tk=tk_up)
    dh2 = _matmul(dup_v, w_upf, mode="nt", out_dtype=F32, name="mm_dh2_v", tm=s, tk=tk_up, b_off=ff // tk_up, add=dh2)
    gw_up = jnp.concatenate([_matmul(h2, dup_g, mode="tn", out_dtype=BF16, name="mm_gw_up_g", tn=1024, tk=s),
                             _matmul(h2, dup_v, mode="tn", out_dtype=BF16, name="mm_gw_up_v", tn=1024, tk=s)], axis=1)
    dx1, dshift2, dscale2, g_pre2_grad = _prenorm_bwd(x1, dh2, dx2, g_pre2, scale2, "prenorm2_bwd")

    dy1, dgate1, g_post1_grad = _post_bwd(y1, gate1, g_post1, "post1_bwd", dxo=dx1)
    dmerged = _matmul(dy1, w_o, mode="nt", out_dtype=F32, name="mm_dmerged", tm=s)
    gw_out = _matmul(merged, dy1, mode="tn", out_dtype=BF16, name="mm_gw_out", tn=1024, tk=s)
    dy_a, dy_b, dz_big = _merge_bwd(dmerged, z_big, y_a, y_b, "merge_bwd")
    da = _matmul(dy_a, w_a, mode="nt", out_dtype=F32, name="mm_da", tm=s)
    gw_a = _matmul(a_act, dy_a, mode="tn", out_dtype=BF16, name="mm_gw_a", tn=1024, tk=s)
    do = _matmul(dy_b, w_b, mode="nt", out_dtype=BF16, name="mm_do", tm=s)
    gw_b = _matmul(o_att, dy_b, mode="tn", out_dtype=BF16, name="mm_gw_b", tn=1024, tk=s)
    dz_big, g_ws, g_bs_t, g_ln_g, g_ln_b = _gmlp_bwd(z_big, da, dz_big, ln_g, ln_b, gm_w_s, b_s_t, "gmlp_bwd")
    dq, dk, dv = _attn_bwd(q_rot, kv_all, kr, o_att, do, lse, heads, "attn_bwd")
    dq_big, dkv, dkk = _mla_bwd_mid(dq, dk, dv, rope_q, rope_k, heads, "mla_bwd_mid")
    gw_q = _matmul(qn, dq_big, mode="tn", out_dtype=F32, name="mm_gw_q", tn=1024, tk=s)
    dqn = _matmul(dq_big, w_q, mode="nt", out_dtype=F32, name="mm_dqn", tm=s, tk=1024)
    gw_kv = _matmul(kvn, dkv, mode="tn", out_dtype=BF16, name="mm_gw_kv", tn=1024, tk=s)
    dkvn = _matmul(dkv, w_kv, mode="nt", out_dtype=F32, name="mm_dkvn", tm=s, tk=1024)
    dz_lat, g_q, g_kv = _mla_bwd_post(z_lat, dqn, dkvn, dkk, q_g, kv_g, "mla_bwd_post")
    dh1 = _matmul(dz_big, w_in_big, mode="nt", out_dtype=F32, name="mm_dh1_big", tm=s)
    dh1 = _matmul(dz_lat, w_in_lat, mode="nt", out_dtype=F32, name="mm_dh1_lat", tm=s, tk=1024, add=dh1)
    gw_in_big = _matmul(h1, dz_big, mode="tn", out_dtype=BF16, name="mm_gw_in_big", tn=1024, tk=s)
    gw_in_lat = _matmul(h1, dz_lat, mode="tn", out_dtype=F32, name="mm_gw_in_lat", tn=1024, tk=s)
    grad_x, dshift1, dscale1, g_pre1_grad = _prenorm_bwd(x2d, dh1, dx1, g_pre1, scale1, "prenorm1_bwd")

    dmod = jnp.concatenate([dshift1, dscale1, dgate1, dshift2, dscale2, dgate2], axis=1)
    dmod_all = _all_gather(jnp.pad(dmod, ((0, SUBLANES - 1), (0, 0))), "gather_dmod")
    dmod_all = dmod_all.reshape(N_DEV, SUBLANES, N_MOD * d)[:, 0]
    grad_b_ada = _sum_leading(dmod_all.reshape(N_DEV, 1, N_MOD * d), "sum_b_ada")[0]
    dmod_mine = lax.dynamic_slice(dmod_all, (0, chip * na), (N_DEV, na))
    grad_w_ada = _ada_bwd(c_all.T, dmod_mine, "ada_bwd")

    gq = gw_q.reshape(ql, heads, HEAD_W)
    gq_pe = gq[:, :, NOPE:NOPE + ROPE] + _quarter_turn_back(gq[:, :, NOPE + ROPE:])
    g_pe = gw_in_lat[:, ql + kvl:ql + kvl + ROPE] + _quarter_turn_back(gw_in_lat[:, ql + kvl + ROPE:])
    grads_full = {
        "w_in": jnp.concatenate([gw_in_big[:, :o_q], gw_in_lat[:, :ql + kvl].astype(BF16), g_pe.astype(BF16),
                                 gw_in_big[:, o_q:]], axis=1),
        "w_branch_a": gw_a,
        "w_uq": jnp.concatenate([gq[:, :, :NOPE], gq_pe], axis=2).reshape(ql, heads * (NOPE + ROPE)).astype(BF16),
        "w_ukv": gw_kv.reshape(kvl, 2, heads, NOPE).transpose(0, 2, 1, 3).reshape(kvl, heads * 2 * NOPE),
        "w_branch_b": gw_b, "w_out": gw_out, "w_up": gw_up, "w_down": gw_down,
    }

    packed = _pack([_to_shards(n, grads_full[n]) for n in BIG], BF16).transpose(1, 0, 2, 3)
    core_ix, chip_ix = ic.reshape(1).astype(jnp.int32), chip.reshape(1).astype(jnp.int32)
    from_sibling = _swap_halves(packed, "rs_swap_halves")
    chip_sums = _add_sibling(packed, from_sibling, core_ix, "rs_add_sibling")
    from_chips = _exchange_chips(chip_sums, "rs_exchange_chips")
    reduced_half = _add_chips(chip_sums, from_chips, chip_ix, "rs_add_chips")
    reduced = _share_halves(reduced_half, "rs_share_halves").reshape(-1)
    grads, off = {}, 0
    for n, size in zip(BIG, sizes):
        grads[n] = reduced[off:off + size].reshape(shard_shapes[n])
        off += size
    grads["w_ada"] = grad_w_ada

    partial = {
        "pre_norm1_g": g_pre1_grad, "gm_ln_g": g_ln_g, "gm_ln_b": g_ln_b, "gm_w_s": g_ws, "gm_b_s": g_bs_t[:, :gm_b_s.shape[0]].T,
        "q_norm_g": g_q, "kv_norm_g": g_kv, "post_norm1_g": g_post1_grad, "pre_norm2_g": g_pre2_grad,
        "conv_w": jnp.concatenate([gcw_g, gcw_v], axis=1), "conv_b": jnp.concatenate([gcb_g, gcb_v], axis=1),
        "post_norm2_g": g_post2_grad,
    }
    flat = jnp.concatenate([partial[n].reshape(-1) for n in SMALL_PARTIAL])
    n_small = flat.shape[0]
    rows_small = -(-n_small // (LANES * SUBLANES)) * SUBLANES
    flat = jnp.pad(flat, (0, rows_small * LANES - n_small)).reshape(rows_small, LANES)
    small_sum = _sum_leading(_all_gather(flat, "gather_small").reshape(N_DEV, rows_small, LANES), "sum_small")
    small_sum = small_sum.reshape(-1)
    off = 0
    for n in SMALL_PARTIAL:
        shape = (CONV_TAPS, 2 * ff) if n == "conv_w" else given[n].shape
        size = partial[n].size
        grads[n] = small_sum[off:off + size].reshape(shape)
        off += size
    grads["conv_w"] = lax.dynamic_slice(grads["conv_w"], (0, chip * conv_w.shape[1]), conv_w.shape)
    grads["b_ada"] = grad_b_ada

    delta, new_m, new_v = {}, {}, {}
    for n in ("w_ada",) + BIG:
        delta[n], new_m[n], new_v[n] = _adamw(given[n], grads[n], given["m_" + n], given["v_" + n], "adamw_" + n)

    def small_pack(prefix, source):
        v = jnp.concatenate([source[prefix + n].reshape(-1) for n in SMALL])
        rows = -(-v.shape[0] // (LANES * SUBLANES)) * SUBLANES
        return jnp.pad(v, (0, rows * LANES - v.shape[0])).reshape(rows, LANES)

    outs = _adamw(small_pack("", given), small_pack("", grads), small_pack("m_", given), small_pack("v_", given),
                  "adamw_small")
    off = 0
    for n in SMALL:
        size = given[n].size
        for store, packed_out in zip((delta, new_m, new_v), outs):
            store[n] = packed_out.reshape(-1)[off:off + size].reshape(given[n].shape)
        off += size

    return (loss, grad_x[None], *[grads[n] for n in WEIGHTS], *[delta[n] for n in WEIGHTS],
            *[new_m[n] for n in WEIGHTS], *[new_v[n] for n in WEIGHTS])
```

```python
import functools

import jax
import jax.numpy as jnp
from jax import lax
from jax.experimental import pallas as pl
from jax.experimental.pallas import tpu as pltpu

F32 = jnp.float32
BF16 = jnp.bfloat16
MESH = pl.DeviceIdType.MESH
HBM = pltpu.HBM

EPS = 1e-6
NOPE, ROPE, VHEAD = 128, 64, 128
HEAD_W = NOPE + 2 * ROPE
ROPE_THETA = 10000.0
CONV_TAPS = 3
N_MOD = 6
N_CHIPS, N_CORES, N_DEV = 4, 2, 8
ADAM_LR, ADAM_B1, ADAM_B2, ADAM_EPS, ADAM_WD, ADAM_STEP = 0.001, 0.9, 0.999, 1e-08, 0.01, 10

LANES = 128
SUBLANES = 8
VMEM_LIMIT = 56 * 2**20

BIG = ("w_in", "w_branch_a", "w_uq", "w_ukv", "w_branch_b", "w_out", "w_up", "w_down")
WEIGHTS = ("w_ada", "b_ada", "pre_norm1_g", "w_in", "gm_ln_g", "gm_ln_b", "gm_w_s", "gm_b_s", "w_branch_a",
           "q_norm_g", "w_uq", "kv_norm_g", "w_ukv", "w_branch_b", "w_out", "post_norm1_g", "pre_norm2_g",
           "w_up", "conv_w", "conv_b", "w_down", "post_norm2_g")
SMALL_PARTIAL = ("pre_norm1_g", "gm_ln_g", "gm_ln_b", "gm_w_s", "gm_b_s", "q_norm_g", "kv_norm_g", "post_norm1_g",
                 "pre_norm2_g", "conv_w", "conv_b", "post_norm2_g")
SMALL = ("b_ada",) + SMALL_PARTIAL


def _div_tile(n, cap, mult=LANES):
    t = (min(cap, n) // mult) * mult
    while t >= mult:
        if n % t == 0:
            return t
        t -= mult
    return n


def _params(**kw):
    return pltpu.CompilerParams(vmem_limit_bytes=VMEM_LIMIT, **kw)


def _row_spec(width):
    return pl.BlockSpec((1, width), lambda *_: (0, 0))


def _gelu(x):
    k = 0.7978845608028654
    return 0.5 * x * (1.0 + jnp.tanh(k * (x + 0.044715 * x * x * x)))


def _gelu_grad(x):
    k = 0.7978845608028654
    t = jnp.tanh(k * (x + 0.044715 * x * x * x))
    return 0.5 * (1.0 + t) + 0.5 * x * (1.0 - t * t) * k * (1.0 + 3.0 * 0.044715 * x * x)


def _sigmoid(x):
    return 1.0 / (1.0 + jnp.exp(-x))


def _dot(a, b, dims):
    return lax.dot_general(a, b, (dims, ((), ())), preferred_element_type=F32)


NN = ((1,), (0,))
NT = ((1,), (1,))
TN = ((0,), (0,))


def _logical(arr):
    if arr.ndim == 2:
        return arr.shape[0], arr.shape[1], arr.shape[1]
    return arr.shape[1], arr.shape[0] * arr.shape[2], arr.shape[2]


def _tile_spec(ndim, group_w, blk_rows, blk_cols, row_of, col_of):
    if ndim == 2:
        return pl.BlockSpec((blk_rows, blk_cols), lambda i, j, k: (row_of(i, j, k), col_of(i, j, k)))
    per = group_w // blk_cols
    return pl.BlockSpec((None, blk_rows, blk_cols),
                        lambda i, j, k: (col_of(i, j, k) // per, row_of(i, j, k), col_of(i, j, k) % per))


def _matmul(a, b, *, mode, out_dtype, name, tm=512, tn=512, tk=2048, mul=None, add=None, out_groups=None):
    ar, ac, agw = _logical(a)
    br, bc, bgw = _logical(b)
    if mode == "nn":
        m, kd, n = ar, ac, bc
        m_w, k_w, n_w = (), (agw,), (bgw,)
    elif mode == "nt":
        m, kd, n = ar, ac, br
        m_w, k_w, n_w = (), (agw, bgw), ()
    else:
        m, kd, n = ac, ar, bc
        m_w, k_w, n_w = (agw,), (), (bgw,)
    if out_groups is not None:
        n_w = n_w + (n // out_groups,)
    tm = _div_tile(min((m,) + m_w), tm, SUBLANES)
    tn = _div_tile(min((n,) + n_w), tn)
    tk = _div_tile(min((kd,) + k_w), tk)
    assert all(w % tn == 0 for w in n_w) and all(w % tk == 0 for w in k_w) and all(w % tm == 0 for w in m_w)
    nk = kd // tk
    dims = {"nn": NN, "nt": NT, "tn": TN}[mode]
    gi, gj, gk = (lambda i, j, k: i), (lambda i, j, k: j), (lambda i, j, k: k)
    if mode == "nn":
        a_spec = _tile_spec(a.ndim, agw, tm, tk, gi, gk)
        b_spec = _tile_spec(b.ndim, bgw, tk, tn, gk, gj)
    elif mode == "nt":
        a_spec = _tile_spec(a.ndim, agw, tm, tk, gi, gk)
        b_spec = _tile_spec(b.ndim, bgw, tn, tk, gj, gk)
    else:
        a_spec = _tile_spec(a.ndim, agw, tk, tm, gk, gi)
        b_spec = _tile_spec(b.ndim, bgw, tk, tn, gk, gj)
    in_specs, operands = [a_spec, b_spec], [a, b]
    if mul is not None:
        assert mul.shape == (m, tn)
        in_specs.append(pl.BlockSpec((tm, tn), lambda i, j, k: (i, 0)))
        operands.append(mul)
    if add is not None:
        in_specs.append(pl.BlockSpec((tm, tn), lambda i, j, k: (i, j)))
        operands.append(add)

    def body(*refs):
        a_ref, b_ref = refs[0], refs[1]
        pos = 2
        mul_ref = add_ref = None
        if mul is not None:
            mul_ref, pos = refs[pos], pos + 1
        if add is not None:
            add_ref, pos = refs[pos], pos + 1
        o_ref = refs[pos]

        def finish(r):
            if mul_ref is not None:
                r = r * mul_ref[...]
            if add_ref is not None:
                r = r + add_ref[...]
            o_ref[...] = r.astype(out_dtype)

        part = _dot(a_ref[...], b_ref[...], dims)
        if nk == 1:
            finish(part)
        else:
            acc_ref = refs[pos + 1]
            k = pl.program_id(2)

            @pl.when(k == 0)
            def _():
                acc_ref[...] = part

            @pl.when(k > 0)
            def _():
                acc_ref[...] += part

            @pl.when(k == nk - 1)
            def _():
                finish(acc_ref[...])

    if out_groups is None:
        out_spec, out_dims = _tile_spec(2, n, tm, tn, gi, gj), (m, n)
    else:
        out_spec, out_dims = _tile_spec(3, n // out_groups, tm, tn, gi, gj), (out_groups, m, n // out_groups)
    return pl.pallas_call(
        body, name=name, grid=(m // tm, n // tn, nk), in_specs=in_specs, out_specs=out_spec,
        out_shape=jax.ShapeDtypeStruct(out_dims, out_dtype),
        scratch_shapes=[] if nk == 1 else [pltpu.VMEM((tm, tn), F32)],
        compiler_params=_params(),
    )(*operands)


def _accumulate(ref, value):
    @pl.when(pl.program_id(0) == 0)
    def _():
        ref[...] = value

    @pl.when(pl.program_id(0) > 0)
    def _():
        ref[...] += value


def _colsum(v):
    return jnp.sum(v, axis=0, keepdims=True)


def _rowmean(v):
    return jnp.mean(v, axis=-1, keepdims=True)


def _prenorm(x, g, scale, shift, name):
    s, d = x.shape
    tb = _div_tile(s, 256, SUBLANES)

    def body(x_ref, g_ref, sc_ref, sh_ref, h_ref):
        xv = x_ref[...]
        r = lax.rsqrt(_rowmean(xv * xv) + EPS)
        h_ref[...] = ((xv * r) * g_ref[...] * (1.0 + sc_ref[...]) + sh_ref[...]).astype(BF16)

    blk = pl.BlockSpec((tb, d), lambda i: (i, 0))
    return pl.pallas_call(
        body, name=name, grid=(s // tb,), in_specs=[blk, _row_spec(d), _row_spec(d), _row_spec(d)],
        out_specs=blk, out_shape=jax.ShapeDtypeStruct((s, d), BF16), compiler_params=_params(),
    )(x, g, scale, shift)


def _post_pre(x, y, gate, pg, g2, scale2, shift2, name):
    s, d = x.shape
    tb = _div_tile(s, 256, SUBLANES)

    def body(x_ref, y_ref, gate_ref, pg_ref, g2_ref, sc_ref, sh_ref, x1_ref, h2_ref):
        yv = y_ref[...]
        rp = lax.rsqrt(_rowmean(yv * yv) + EPS)
        x1 = x_ref[...] + gate_ref[...] * ((yv * rp) * pg_ref[...])
        x1_ref[...] = x1
        r2 = lax.rsqrt(_rowmean(x1 * x1) + EPS)
        h2_ref[...] = ((x1 * r2) * g2_ref[...] * (1.0 + sc_ref[...]) + sh_ref[...]).astype(BF16)

    blk = pl.BlockSpec((tb, d), lambda i: (i, 0))
    return pl.pallas_call(
        body, name=name, grid=(s // tb,), in_specs=[blk, blk] + [_row_spec(d)] * 5,
        out_specs=[blk, blk],
        out_shape=[jax.ShapeDtypeStruct((s, d), F32), jax.ShapeDtypeStruct((s, d), BF16)],
        compiler_params=_params(),
    )(x, y, gate, pg, g2, scale2, shift2)


def _post_bwd(y, gate, pg, name, *, dxo=None, xin=None, target=None):
    s, d = y.shape
    tb = _div_tile(s, 256, SUBLANES)
    from_loss = target is not None

    def body(*refs):
        if from_loss:
            y_ref, gate_ref, pg_ref, xin_ref, t_ref, dy_ref, dgate_ref, dpg_ref, dxo_ref, loss_ref = refs
        else:
            y_ref, gate_ref, pg_ref, dxo_in_ref, dy_ref, dgate_ref, dpg_ref = refs
        yv = y_ref[...]
        rp = lax.rsqrt(_rowmean(yv * yv) + EPS)
        yh = yv * rp
        fn = yh * pg_ref[...]
        gate = gate_ref[...]
        if from_loss:
            err = xin_ref[...] + gate * fn - t_ref[...]
            dxo = err * (1.0 / d)
            dxo_ref[...] = dxo
            part = 0.5 * jnp.sum(_rowmean(err * err), axis=0, keepdims=True)
            _accumulate(loss_ref, jnp.broadcast_to(part, loss_ref.shape))
        else:
            dxo = dxo_in_ref[...]
        _accumulate(dgate_ref, _colsum(dxo * fn))
        dfn = dxo * gate
        _accumulate(dpg_ref, _colsum(dfn * yh))
        dyh = dfn * pg_ref[...]
        dy_ref[...] = (rp * (dyh - yh * _rowmean(dyh * yh))).astype(BF16)

    blk = pl.BlockSpec((tb, d), lambda i: (i, 0))
    in_specs = [blk, _row_spec(d), _row_spec(d)]
    out_specs = [blk, _row_spec(d), _row_spec(d)]
    out_shape = [jax.ShapeDtypeStruct((s, d), BF16), jax.ShapeDtypeStruct((1, d), F32),
                 jax.ShapeDtypeStruct((1, d), F32)]
    if from_loss:
        operands = (y, gate, pg, xin, target)
        in_specs += [blk, blk]
        out_specs += [blk, _row_spec(LANES)]
        out_shape += [jax.ShapeDtypeStruct((s, d), F32), jax.ShapeDtypeStruct((1, LANES), F32)]
    else:
        operands = (y, gate, pg, dxo)
        in_specs += [blk]
    return pl.pallas_call(
        body, name=name, grid=(s // tb,), in_specs=in_specs, out_specs=out_specs, out_shape=out_shape,
        compiler_params=_params(),
    )(*operands)


def _prenorm_bwd(xin, dh, dres, g, scale, name):
    s, d = xin.shape
    tb = _div_tile(s, 256, SUBLANES)

    def body(x_ref, dh_ref, dres_ref, g_ref, sc_ref, dx_ref, dshift_ref, dscale_ref, dg_ref):
        xv = x_ref[...]
        r = lax.rsqrt(_rowmean(xv * xv) + EPS)
        xn = xv * r
        dh = dh_ref[...]
        g1 = g_ref[...]
        s1 = 1.0 + sc_ref[...]
        _accumulate(dshift_ref, _colsum(dh))
        _accumulate(dscale_ref, _colsum(dh * xn * g1))
        _accumulate(dg_ref, _colsum(dh * xn * s1))
        dxn = dh * g1 * s1
        dx_ref[...] = dres_ref[...] + r * (dxn - xn * _rowmean(dxn * xn))

    blk = pl.BlockSpec((tb, d), lambda i: (i, 0))
    return pl.pallas_call(
        body, name=name, grid=(s // tb,), in_specs=[blk, blk, blk, _row_spec(d), _row_spec(d)],
        out_specs=[blk, _row_spec(d), _row_spec(d), _row_spec(d)],
        out_shape=[jax.ShapeDtypeStruct((s, d), F32)] + [jax.ShapeDtypeStruct((1, d), F32)] * 3,
        compiler_params=_params(),
    )(xin, dh, dres, g, scale)


def _merge(z_big, y_a, y_b, name):
    s, d = y_a.shape
    tb = _div_tile(s, 256, SUBLANES)

    def body(zg_ref, ya_ref, yb_ref, o_ref):
        o_ref[...] = (_sigmoid(zg_ref[:, :d]) * ya_ref[...] + _sigmoid(zg_ref[:, d:]) * yb_ref[...]).astype(BF16)

    blk = pl.BlockSpec((tb, d), lambda i: (i, 0))
    return pl.pallas_call(
        body, name=name, grid=(s // tb,), in_specs=[pl.BlockSpec((tb, 2 * d), lambda i: (i, 1)), blk, blk],
        out_specs=blk, out_shape=jax.ShapeDtypeStruct((s, d), BF16), compiler_params=_params(),
    )(z_big, y_a, y_b)


def _merge_bwd(dmerged, z_big, y_a, y_b, name):
    s, d = y_a.shape
    tb = _div_tile(s, 256, SUBLANES)

    def body(dm_ref, zg_ref, ya_ref, yb_ref, dya_ref, dyb_ref, dz_ref):
        dm = dm_ref[...]
        sa, sb = _sigmoid(zg_ref[:, :d]), _sigmoid(zg_ref[:, d:])
        dya_ref[...] = (dm * sa).astype(BF16)
        dyb_ref[...] = (dm * sb).astype(BF16)
        dz_ref[:, :d] = (dm * ya_ref[...] * sa * (1.0 - sa)).astype(BF16)
        dz_ref[:, d:] = (dm * yb_ref[...] * sb * (1.0 - sb)).astype(BF16)

    blk = pl.BlockSpec((tb, d), lambda i: (i, 0))
    wide = pl.BlockSpec((tb, 2 * d), lambda i: (i, 1))
    return pl.pallas_call(
        body, name=name, grid=(s // tb,), in_specs=[blk, wide, blk, blk], out_specs=[blk, blk, wide],
        out_shape=[jax.ShapeDtypeStruct((s, d), BF16), jax.ShapeDtypeStruct((s, d), BF16),
                   jax.ShapeDtypeStruct((s, 4 * d), BF16)],
        compiler_params=_params(),
    )(dmerged, z_big, y_a, y_b)


def _causal_mask(ch):
    q = lax.broadcasted_iota(jnp.int32, (ch, ch), 0)
    p = lax.broadcasted_iota(jnp.int32, (ch, ch), 1)
    return (p <= q).astype(F32)


def _gmlp_norm(zc, lng, lnb, gw):
    u_pre, v_pre = zc[:, :gw], zc[:, gw:]
    vg = _gelu(v_pre)
    mu = _rowmean(vg)
    cen = vg - mu
    rstd = lax.rsqrt(_rowmean(cen * cen) + EPS)
    vhat = cen * rstd
    return u_pre, v_pre, _gelu(u_pre), vhat, rstd, vhat * lng + lnb


def _gmlp_fwd(z_big, ln_g, ln_b, w_s, b_s_t, name):
    s = z_big.shape[0]
    groups, ch, _ = w_s.shape
    gw = ln_g.shape[1]
    gd = gw // groups

    def body(z_ref, lng_ref, lnb_ref, ws_ref, bt_ref, a_ref):
        _, _, u, _, _, vn = _gmlp_norm(z_ref[...], lng_ref[...], lnb_ref[...], gw)
        mask = _causal_mask(ch)
        for g in range(groups):
            cols = slice(g * gd, (g + 1) * gd)
            wm = (ws_ref[g] * mask).astype(BF16)
            mixed = _dot(wm, vn[:, cols].astype(BF16), NN) + bt_ref[:, g:g + 1]
            a_ref[:, cols] = (u[:, cols] * mixed).astype(BF16)

    return pl.pallas_call(
        body, name=name, grid=(s // ch,),
        in_specs=[pl.BlockSpec((ch, 2 * gw), lambda n: (n, 0)), _row_spec(gw), _row_spec(gw),
                  pl.BlockSpec((groups, ch, ch), lambda n: (0, 0, 0)), pl.BlockSpec((ch, groups), lambda n: (0, 0))],
        out_specs=pl.BlockSpec((ch, gw), lambda n: (n, 0)),
        out_shape=jax.ShapeDtypeStruct((s, gw), BF16), compiler_params=_params(),
    )(z_big, ln_g, ln_b, w_s, b_s_t)


def _gmlp_bwd(z_big, da, dz_big, ln_g, ln_b, w_s, b_s_t, name):
    s = z_big.shape[0]
    groups, ch, _ = w_s.shape
    gw = ln_g.shape[1]
    gd = gw // groups

    def body(z_ref, da_ref, dzin_ref, lng_ref, lnb_ref, ws_ref, bt_ref, dz_ref, gws_ref, gbt_ref, glng_ref, glnb_ref):
        del dzin_ref
        lng = lng_ref[...]
        u_pre, v_pre, u, vhat, rstd, vn = _gmlp_norm(z_ref[...], lng, lnb_ref[...], gw)
        da = da_ref[...]
        mask = _causal_mask(ch)
        first = pl.program_id(0) == 0
        dvn_parts = []
        lane = lax.broadcasted_iota(jnp.int32, (ch, LANES), 1)
        gb = jnp.zeros((ch, LANES), F32)
        for g in range(groups):
            cols = slice(g * gd, (g + 1) * gd)
            wm = (ws_ref[g] * mask).astype(BF16)
            vn_g = vn[:, cols].astype(BF16)
            mixed = _dot(wm, vn_g, NN) + bt_ref[:, g:g + 1]
            dz_ref[:, cols] = (da[:, cols] * mixed * _gelu_grad(u_pre[:, cols])).astype(BF16)
            dmixed = da[:, cols] * u[:, cols]
            dm16 = dmixed.astype(BF16)
            dvn_parts.append(_dot(wm, dm16, TN))
            gws = _dot(dm16, vn_g, NT) * mask

            @pl.when(first)
            def _(g=g, gws=gws):
                gws_ref[g] = gws

            @pl.when(jnp.logical_not(first))
            def _(g=g, gws=gws):
                gws_ref[g] += gws

            gb = gb + jnp.where(lane == g, jnp.sum(dmixed, axis=1, keepdims=True), 0.0)
        _accumulate(gbt_ref, gb)
        dvn = jnp.concatenate(dvn_parts, axis=1)
        _accumulate(glnb_ref, _colsum(dvn))
        _accumulate(glng_ref, _colsum(dvn * vhat))
        dvh = dvn * lng
        dvg = rstd * (dvh - _rowmean(dvh) - vhat * _rowmean(dvh * vhat))
        dz_ref[:, gw:] = (dvg * _gelu_grad(v_pre)).astype(BF16)

    zspec = pl.BlockSpec((ch, 2 * gw), lambda n: (n, 0))
    return pl.pallas_call(
        body, name=name, grid=(s // ch,),
        in_specs=[zspec, pl.BlockSpec((ch, gw), lambda n: (n, 0)), pl.BlockSpec(memory_space=HBM),
                  _row_spec(gw), _row_spec(gw), pl.BlockSpec((groups, ch, ch), lambda n: (0, 0, 0)),
                  pl.BlockSpec((ch, groups), lambda n: (0, 0))],
        out_specs=[zspec, pl.BlockSpec((groups, ch, ch), lambda n: (0, 0, 0)),
                   pl.BlockSpec((ch, LANES), lambda n: (0, 0)), _row_spec(gw), _row_spec(gw)],
        out_shape=[jax.ShapeDtypeStruct(dz_big.shape, BF16), jax.ShapeDtypeStruct((groups, ch, ch), F32),
                   jax.ShapeDtypeStruct((ch, LANES), F32), jax.ShapeDtypeStruct((1, gw), F32),
                   jax.ShapeDtypeStruct((1, gw), F32)],
        input_output_aliases={2: 0}, compiler_params=_params(),
    )(z_big, da, dz_big, ln_g, ln_b, w_s, b_s_t)


def _mla_prep(z_lat, q_g, kv_g, rope_k, name):
    s, latw = z_lat.shape
    ql, kvl = q_g.shape[1], kv_g.shape[1]
    tb = _div_tile(s, 256, SUBLANES)

    def body(z_ref, qg_ref, kvg_ref, t_ref, qn_ref, kvn_ref, kr_ref):
        q = z_ref[:, :ql]
        qn_ref[...] = ((q * lax.rsqrt(_rowmean(q * q) + EPS)) * qg_ref[...]).astype(BF16)
        kv = z_ref[:, ql:ql + kvl]
        kvn_ref[...] = ((kv * lax.rsqrt(_rowmean(kv * kv) + EPS)) * kvg_ref[...]).astype(BF16)
        kk = z_ref[:, ql + kvl:] * t_ref[...]
        kr_ref[...] = (kk + pltpu.roll(kk, ROPE, axis=1)).astype(BF16)

    return pl.pallas_call(
        body, name=name, grid=(s // tb,),
        in_specs=[pl.BlockSpec((tb, latw), lambda i: (i, 0)), _row_spec(ql), _row_spec(kvl),
                  pl.BlockSpec((tb, 2 * ROPE), lambda i: (i, 0))],
        out_specs=[pl.BlockSpec((tb, ql), lambda i: (i, 0)), pl.BlockSpec((tb, kvl), lambda i: (i, 0)),
                   pl.BlockSpec((tb, 2 * ROPE), lambda i: (i, 0))],
        out_shape=[jax.ShapeDtypeStruct((s, ql), BF16), jax.ShapeDtypeStruct((s, kvl), BF16),
                   jax.ShapeDtypeStruct((s, 2 * ROPE), BF16)],
        compiler_params=_params(),
    )(z_lat, q_g, kv_g, rope_k)


def _scores(q, k, kr, row0, col0, scale):
    s = (_dot(q[:, :NOPE], k, NT) + _dot(q[:, NOPE:], kr, NT)) * scale
    rows = row0 + lax.broadcasted_iota(jnp.int32, s.shape, 0)
    cols = col0 + lax.broadcasted_iota(jnp.int32, s.shape, 1)
    return jnp.where(cols <= rows, s, -1e30)


def _attn_fwd(q, kv, kr, heads, name):
    s = q.shape[0]
    t = _div_tile(s, 512)
    nb = s // t
    scale = float(NOPE + ROPE) ** -0.5

    def body(q_ref, k_ref, kr_ref, v_ref, o_ref, lse_ref, m_ref, l_ref, acc_ref):
        i, j = pl.program_id(1), pl.program_id(2)

        @pl.when(j == 0)
        def _():
            m_ref[...] = jnp.full(m_ref.shape, -1e30, F32)
            l_ref[...] = jnp.zeros(l_ref.shape, F32)
            acc_ref[...] = jnp.zeros(acc_ref.shape, F32)

        @pl.when(j <= i)
        def _():
            sc = _scores(q_ref[...], k_ref[...], kr_ref[...], i * t, j * t, scale)
            m_old = m_ref[...]
            m_new = jnp.maximum(m_old, jnp.max(sc, axis=-1, keepdims=True))
            p = jnp.exp(sc - m_new)
            alpha = jnp.exp(m_old - m_new)
            l_ref[...] = alpha * l_ref[...] + jnp.sum(p, axis=-1, keepdims=True)
            acc_ref[...] = alpha * acc_ref[...] + _dot(p.astype(BF16), v_ref[...], NN)
            m_ref[...] = m_new

        @pl.when(j == i)
        def _():
            o_ref[...] = (acc_ref[...] / l_ref[...]).astype(BF16)
            lse_ref[...] = jnp.broadcast_to(m_ref[...] + jnp.log(l_ref[...]), lse_ref.shape)

    kidx = lambda off: (lambda h, i, j: (jnp.minimum(i, j), off(h)))
    return pl.pallas_call(
        body, name=name, grid=(heads, nb, nb),
        in_specs=[pl.BlockSpec((t, HEAD_W), lambda h, i, j: (i, h)),
                  pl.BlockSpec((t, NOPE), kidx(lambda h: h)),
                  pl.BlockSpec((t, 2 * ROPE), kidx(lambda h: 0)),
                  pl.BlockSpec((t, VHEAD), kidx(lambda h: heads + h))],
        out_specs=[pl.BlockSpec((t, VHEAD), lambda h, i, j: (i, h)),
                   pl.BlockSpec((None, t, LANES), lambda h, i, j: (h, i, 0))],
        out_shape=[jax.ShapeDtypeStruct((s, heads * VHEAD), BF16), jax.ShapeDtypeStruct((heads, s, LANES), F32)],
        scratch_shapes=[pltpu.VMEM((t, 1), F32), pltpu.VMEM((t, 1), F32), pltpu.VMEM((t, VHEAD), F32)],
        compiler_params=_params(),
    )(q, kv, kr, kv)


def _attn_bwd(q, kv, kr, o, do, lse, heads, name):
    s = q.shape[0]
    t = _div_tile(s, 512)
    nb = s // t
    scale = float(NOPE + ROPE) ** -0.5

    def body(q_ref, k_ref, kr_ref, v_ref, o_ref, do_ref, lse_ref, dq_ref, dk_ref, dv_ref, dk_acc, dv_acc):
        j, i = pl.program_id(1), pl.program_id(2)

        @pl.when(jnp.logical_and(j == 0, i == 0))
        def _():
            dq_ref[...] = jnp.zeros(dq_ref.shape, F32)

        @pl.when(i == j)
        def _():
            dk_acc[...] = jnp.zeros(dk_acc.shape, F32)
            dv_acc[...] = jnp.zeros(dv_acc.shape, F32)

        @pl.when(i >= j)
        def _():
            qv, kn, krv, do_v = q_ref[...], k_ref[...], kr_ref[...], do_ref[...]
            sc = _scores(qv, kn, krv, i * t, j * t, scale)
            p = jnp.exp(sc - lse_ref[:, :1])
            dv_acc[...] += _dot(p.astype(BF16), do_v, TN)
            dp = _dot(do_v, v_ref[...], NT)
            delta = jnp.sum(do_v.astype(F32) * o_ref[...].astype(F32), axis=-1, keepdims=True)
            ds = (p * (dp - delta) * scale).astype(BF16)
            rows = pl.ds(pl.multiple_of(i * t, t), t)
            dq_ref[rows, :NOPE] += _dot(ds, kn, NN)
            dq_ref[rows, NOPE:] += _dot(ds, krv, NN)
            dk_acc[...] += _dot(ds, qv, TN)

        @pl.when(i == nb - 1)
        def _():
            dk_ref[...] = dk_acc[...].astype(BF16)
            dv_ref[...] = dv_acc[...].astype(BF16)

    qidx = lambda h, j, i: (jnp.maximum(i, j), h)
    return pl.pallas_call(
        body, name=name, grid=(heads, nb, nb),
        in_specs=[pl.BlockSpec((t, HEAD_W), qidx),
                  pl.BlockSpec((t, NOPE), lambda h, j, i: (j, h)),
                  pl.BlockSpec((t, 2 * ROPE), lambda h, j, i: (j, 0)),
                  pl.BlockSpec((t, VHEAD), lambda h, j, i: (j, heads + h)),
                  pl.BlockSpec((t, VHEAD), qidx), pl.BlockSpec((t, VHEAD), qidx),
                  pl.BlockSpec((None, t, LANES), lambda h, j, i: (h, jnp.maximum(i, j), 0))],
        out_specs=[pl.BlockSpec((s, HEAD_W), lambda h, j, i: (0, h)),
                   pl.BlockSpec((t, HEAD_W), lambda h, j, i: (j, h)),
                   pl.BlockSpec((t, VHEAD), lambda h, j, i: (j, h))],
        out_shape=[jax.ShapeDtypeStruct((s, heads * HEAD_W), F32), jax.ShapeDtypeStruct((s, heads * HEAD_W), BF16),
                   jax.ShapeDtypeStruct((s, heads * VHEAD), BF16)],
        scratch_shapes=[pltpu.VMEM((t, HEAD_W), F32), pltpu.VMEM((t, VHEAD), F32)],
        compiler_params=_params(),
    )(q, kv, kr, kv, o, do, lse)


def _mla_bwd_mid(dq, dk, dv, rope_q, rope_k, heads, name):
    s = dq.shape[0]
    tb = _div_tile(s, 256, SUBLANES)

    def body(dq_ref, dk_ref, dv_ref, tq_ref, tk_ref, dqb_ref, dkv_ref, dkk_ref):
        tq = tq_ref[...]
        dkr = jnp.zeros((tb, 2 * ROPE), F32)
        for h in range(heads):
            cols = slice(h * HEAD_W, (h + 1) * HEAD_W)
            dqb_ref[:, cols] = (dq_ref[:, cols] * tq).astype(BF16)
            dkv_ref[:, h * NOPE:(h + 1) * NOPE] = dk_ref[:, h * HEAD_W:h * HEAD_W + NOPE]
            dkr = dkr + dk_ref[:, h * HEAD_W + NOPE:(h + 1) * HEAD_W].astype(F32)
        dkv_ref[:, heads * NOPE:] = dv_ref[...]
        dkk_ref[...] = (dkr + pltpu.roll(dkr, ROPE, axis=1)) * tk_ref[...]

    wq, wv = heads * HEAD_W, heads * VHEAD
    return pl.pallas_call(
        body, name=name, grid=(s // tb,),
        in_specs=[pl.BlockSpec((tb, wq), lambda i: (i, 0)), pl.BlockSpec((tb, wq), lambda i: (i, 0)),
                  pl.BlockSpec((tb, wv), lambda i: (i, 0)), pl.BlockSpec((tb, HEAD_W), lambda i: (i, 0)),
                  pl.BlockSpec((tb, 2 * ROPE), lambda i: (i, 0))],
        out_specs=[pl.BlockSpec((tb, wq), lambda i: (i, 0)), pl.BlockSpec((tb, heads * NOPE + wv), lambda i: (i, 0)),
                   pl.BlockSpec((tb, 2 * ROPE), lambda i: (i, 0))],
        out_shape=[jax.ShapeDtypeStruct((s, wq), BF16), jax.ShapeDtypeStruct((s, heads * NOPE + wv), BF16),
                   jax.ShapeDtypeStruct((s, 2 * ROPE), F32)],
        compiler_params=_params(),
    )(dq, dk, dv, rope_q, rope_k)


def _mla_bwd_post(z_lat, dqn, dkvn, dkk, q_g, kv_g, name):
    s, latw = z_lat.shape
    ql, kvl = q_g.shape[1], kv_g.shape[1]
    tb = _div_tile(s, 256, SUBLANES)

    def norm_bwd(xv, dn, g, dg_ref):
        r = lax.rsqrt(_rowmean(xv * xv) + EPS)
        xh = xv * r
        _accumulate(dg_ref, _colsum(dn * xh))
        dxh = dn * g
        return r * (dxh - xh * _rowmean(dxh * xh))

    def body(z_ref, dqn_ref, dkvn_ref, dkk_ref, qg_ref, kvg_ref, dz_ref, gq_ref, gkv_ref):
        dz_ref[:, :ql] = norm_bwd(z_ref[:, :ql], dqn_ref[...], qg_ref[...], gq_ref).astype(BF16)
        dz_ref[:, ql:ql + kvl] = norm_bwd(z_ref[:, ql:ql + kvl], dkvn_ref[...], kvg_ref[...], gkv_ref).astype(BF16)
        dz_ref[:, ql + kvl:] = dkk_ref[...].astype(BF16)

    return pl.pallas_call(
        body, name=name, grid=(s // tb,),
        in_specs=[pl.BlockSpec((tb, latw), lambda i: (i, 0)), pl.BlockSpec((tb, ql), lambda i: (i, 0)),
                  pl.BlockSpec((tb, kvl), lambda i: (i, 0)), pl.BlockSpec((tb, 2 * ROPE), lambda i: (i, 0)),
                  _row_spec(ql), _row_spec(kvl)],
        out_specs=[pl.BlockSpec((tb, latw), lambda i: (i, 0)), _row_spec(ql), _row_spec(kvl)],
        out_shape=[jax.ShapeDtypeStruct((s, latw), BF16), jax.ShapeDtypeStruct((1, ql), F32),
                   jax.ShapeDtypeStruct((1, kvl), F32)],
        compiler_params=_params(),
    )(z_lat, dqn, dkvn, dkk, q_g, kv_g)


def _shift_down(x, n):
    rows = lax.broadcasted_iota(jnp.int32, x.shape, 0)
    return jnp.where(rows >= n, pltpu.roll(x, n, axis=0), 0.0)


def _shift_up(x, n):
    s = x.shape[0]
    rows = lax.broadcasted_iota(jnp.int32, x.shape, 0)
    return jnp.where(rows < s - n, pltpu.roll(x, s - n, axis=0), 0.0)


def _conv(pre, w_ref, b_ref):
    return (w_ref[2:3, :] * pre + w_ref[1:2, :] * _shift_down(pre, 1) + w_ref[0:1, :] * _shift_down(pre, 2)
            + b_ref[...])


def _conv_fwd(up_pre, conv_w, conv_b, name):
    s, ff2 = up_pre.shape
    ff = ff2 // 2
    tc = _div_tile(ff, 256)
    nb = ff // tc

    def body(pg_ref, pv_ref, wg_ref, wv_ref, bg_ref, bv_ref, act_ref):
        gate = _conv(pg_ref[...].astype(F32), wg_ref, bg_ref)
        val = _conv(pv_ref[...].astype(F32), wv_ref, bv_ref)
        act_ref[...] = (gate * _sigmoid(gate) * val).astype(BF16)

    def col(rows, off):
        return pl.BlockSpec((rows, tc), lambda j: (0, j + off))

    return pl.pallas_call(
        body, name=name, grid=(nb,),
        in_specs=[col(s, 0), col(s, nb), col(CONV_TAPS, 0), col(CONV_TAPS, nb), col(1, 0), col(1, nb)],
        out_specs=col(s, 0), out_shape=jax.ShapeDtypeStruct((s, ff), BF16), compiler_params=_params(),
    )(up_pre, up_pre, conv_w, conv_w, conv_b, conv_b)


def _conv_bwd(up_pre, dact, conv_w, conv_b, name):
    s, ff2 = up_pre.shape
    ff = ff2 // 2
    tc = _div_tile(ff, 256)
    nb = ff // tc

    def half(pre, dx, w_ref, dpre_ref, gw_ref, gb_ref):
        gb_ref[...] = _colsum(dx)
        gw_ref[0:1, :] = _colsum(dx * _shift_down(pre, 2))
        gw_ref[1:2, :] = _colsum(dx * _shift_down(pre, 1))
        gw_ref[2:3, :] = _colsum(dx * pre)
        dpre_ref[...] = (w_ref[2:3, :] * dx + w_ref[1:2, :] * _shift_up(dx, 1)
                         + w_ref[0:1, :] * _shift_up(dx, 2)).astype(BF16)

    def body(pg_ref, pv_ref, da_ref, wg_ref, wv_ref, bg_ref, bv_ref, dup_ref, gwg_ref, gwv_ref, gbg_ref, gbv_ref):
        pre_g, pre_v = pg_ref[...].astype(F32), pv_ref[...].astype(F32)
        gate = _conv(pre_g, wg_ref, bg_ref)
        val = _conv(pre_v, wv_ref, bv_ref)
        da = da_ref[...].astype(F32)
        sg = _sigmoid(gate)
        half(pre_v, da * gate * sg, wv_ref, dup_ref.at[1], gwv_ref, gbv_ref)
        half(pre_g, da * val * sg * (1.0 + gate * (1.0 - sg)), wg_ref, dup_ref.at[0], gwg_ref, gbg_ref)

    def col(rows, off):
        return pl.BlockSpec((rows, tc), lambda j: (0, j + off))

    return pl.pallas_call(
        body, name=name, grid=(nb,),
        in_specs=[col(s, 0), col(s, nb), col(s, 0), col(CONV_TAPS, 0), col(CONV_TAPS, nb), col(1, 0), col(1, nb)],
        out_specs=[pl.BlockSpec((2, s, tc), lambda j: (0, 0, j)), col(CONV_TAPS, 0), col(CONV_TAPS, 0),
                   col(1, 0), col(1, 0)],
        out_shape=[jax.ShapeDtypeStruct((2, s, ff), BF16)] + [jax.ShapeDtypeStruct((CONV_TAPS, ff), F32)] * 2
        + [jax.ShapeDtypeStruct((1, ff), F32)] * 2,
        compiler_params=_params(),
    )(up_pre, up_pre, dact, conv_w, conv_w, conv_b, conv_b)


def _ada_fwd(c_all, w, b, name):
    nseq, d = c_all.shape
    na = w.shape[1]
    tn = _div_tile(na, 512)

    def body(c_ref, w_ref, b_ref, o_ref):
        cv = c_ref[...]
        sc = cv * _sigmoid(cv)
        o_ref[...] = jnp.dot(sc, w_ref[...], preferred_element_type=F32, precision=lax.Precision.HIGHEST) + b_ref[...]

    return pl.pallas_call(
        body, name=name, grid=(na // tn,),
        in_specs=[pl.BlockSpec((nseq, d), lambda j: (0, 0)), pl.BlockSpec((d, tn), lambda j: (0, j)),
                  pl.BlockSpec((1, tn), lambda j: (0, j))],
        out_specs=pl.BlockSpec((nseq, tn), lambda j: (0, j)),
        out_shape=jax.ShapeDtypeStruct((nseq, na), F32), compiler_params=_params(),
    )(c_all, w, b)


def _ada_bwd(c_all_t, dmod, name):
    d, nseq = c_all_t.shape
    na = dmod.shape[1]
    tm, tn = _div_tile(d, 256, SUBLANES), _div_tile(na, 512)

    def body(c_ref, dm_ref, o_ref):
        cv = c_ref[...]
        sc = cv * _sigmoid(cv)
        acc = sc[:, 0:1] * dm_ref[0:1, :]
        for bi in range(1, nseq):
            acc = acc + sc[:, bi:bi + 1] * dm_ref[bi:bi + 1, :]
        o_ref[...] = acc

    return pl.pallas_call(
        body, name=name, grid=(d // tm, na // tn),
        in_specs=[pl.BlockSpec((tm, nseq), lambda i, j: (i, 0)), pl.BlockSpec((nseq, tn), lambda i, j: (0, j))],
        out_specs=pl.BlockSpec((tm, tn), lambda i, j: (i, j)),
        out_shape=jax.ShapeDtypeStruct((d, na), F32), compiler_params=_params(),
    )(c_all_t, dmod)


def _adamw(w, g, m, v, name):
    rows, cols = w.shape
    tb = _div_tile(rows, max(SUBLANES, (256 * 1024) // cols // SUBLANES * SUBLANES), SUBLANES)
    c1 = 1.0 / (1.0 - ADAM_B1 ** ADAM_STEP)
    c2 = 1.0 / (1.0 - ADAM_B2 ** ADAM_STEP)

    def body(w_ref, g_ref, m_ref, v_ref, d_ref, nm_ref, nv_ref):
        gv = g_ref[...]
        nm = ADAM_B1 * m_ref[...] + (1.0 - ADAM_B1) * gv
        nv = ADAM_B2 * v_ref[...] + (1.0 - ADAM_B2) * (gv * gv)
        nm_ref[...] = nm
        nv_ref[...] = nv
        d_ref[...] = -ADAM_LR * ((nm * c1) / (jnp.sqrt(nv * c2) + ADAM_EPS) + ADAM_WD * w_ref[...])

    blk = pl.BlockSpec((tb, cols), lambda i: (i, 0))
    return pl.pallas_call(
        body, name=name, grid=(rows // tb,), in_specs=[blk] * 4, out_specs=[blk] * 3,
        out_shape=[jax.ShapeDtypeStruct((rows, cols), F32)] * 3, compiler_params=_params(),
    )(w, g, m, v)


def _sum_leading(parts, name):
    n, rows, cols = parts.shape
    tb = _div_tile(rows, 512, SUBLANES)

    def body(p_ref, o_ref):
        acc = p_ref[0]
        for k in range(1, n):
            acc = acc + p_ref[k]
        o_ref[...] = acc

    return pl.pallas_call(
        body, name=name, grid=(rows // tb,), in_specs=[pl.BlockSpec((n, tb, cols), lambda i: (0, i, 0))],
        out_specs=pl.BlockSpec((tb, cols), lambda i: (i, 0)),
        out_shape=jax.ShapeDtypeStruct((rows, cols), F32), compiler_params=_params(),
    )(parts)


def _place():
    x, y, c = lax.axis_index("x"), lax.axis_index("y"), lax.axis_index("c")
    return x, y, c, [(1 - x, y), (x, 1 - y), (1 - x, 1 - y)]


def _all_gather(block, name):
    m_per, n = block.shape

    def body(x_ref, out_ref, send_sems, recv_sems, local_sem):
        x, y, c, chips = _place()
        me, sibling = (x, y, c), (x, y, 1 - c)

        def rows(px, py, pc):
            return out_ref.at[pl.ds((4 * px + 2 * py + pc) * m_per, m_per), :]

        def copy(k, blk, to, src=None):
            return pltpu.make_async_remote_copy(
                src_ref=rows(*blk) if src is None else src, dst_ref=rows(*blk), send_sem=send_sems.at[k],
                recv_sem=recv_sems.at[k], device_id=to, device_id_type=MESH)

        mine = pltpu.make_async_copy(x_ref, rows(*me), local_sem)
        mine.start()
        first = [copy(0, me, sibling, src=x_ref)]
        first += [copy(1 + j, me, (*chip, c), src=x_ref) for j, chip in enumerate(chips)]
        for cp in first:
            cp.start()
        passed = [copy(4 + j, (*chip, c), sibling) for j, chip in enumerate(chips)]
        for j, chip in enumerate(chips):
            copy(1 + j, (*chip, c), me).wait_recv()
            passed[j].start()
        copy(0, sibling, me).wait_recv()
        for j, chip in enumerate(chips):
            copy(4 + j, (*chip, 1 - c), me).wait_recv()
        for cp in first + passed:
            cp.wait_send()
        mine.wait()

    return pl.pallas_call(
        body, name=name, out_shape=jax.ShapeDtypeStruct((N_DEV * m_per, n), block.dtype),
        in_specs=[pl.BlockSpec(memory_space=pltpu.VMEM)], out_specs=pl.BlockSpec(memory_space=pltpu.VMEM),
        scratch_shapes=[pltpu.SemaphoreType.DMA((7,)), pltpu.SemaphoreType.DMA((7,)), pltpu.SemaphoreType.DMA],
        compiler_params=_params(),
    )(block)


def _hbm_specs(n):
    return [pl.BlockSpec(memory_space=HBM)] * n


def _half_rows(ref, half, lead=None):
    h = ref.shape[-2] // 2
    rows = pl.ds(pl.multiple_of(half * h, 2 * SUBLANES), h)
    return ref.at[rows, :] if lead is None else ref.at[lead, rows, :]


def _gather_weights(shards, name):
    nw = len(shards)

    def body(*refs):
        in_refs, out_refs = refs[:nw], refs[nw:2 * nw]
        send_sems, recv_sems = refs[2 * nw:]
        x, y, c, chips = _place()
        me, sibling = (x, y, c), (x, y, 1 - c)

        def copy(w, k, block, half, to, src=None):
            dst = _half_rows(out_refs[w], half, 2 * block[0] + block[1])
            return pltpu.make_async_remote_copy(
                src_ref=dst if src is None else src, dst_ref=dst, send_sem=send_sems.at[w, k],
                recv_sem=recv_sems.at[w, k], device_id=to, device_id_type=MESH)

        first = [copy(w, j, (x, y), c, (*chip, c), src=_half_rows(in_refs[w], c))
                 for w in range(nw) for j, chip in enumerate(chips)]
        for cp in first:
            cp.start()
        passed = []
        for w in range(nw):
            for j, chip in enumerate(chips):
                copy(w, j, chip, c, me).wait_recv()
                passed.append(copy(w, 3 + j, chip, c, sibling))
                passed[-1].start()
        for w in range(nw):
            for j, chip in enumerate(chips):
                copy(w, 3 + j, chip, 1 - c, me).wait_recv()
        for cp in first + passed:
            cp.wait_send()

    return pl.pallas_call(
        body, name=name, out_shape=[jax.ShapeDtypeStruct((N_CHIPS,) + w.shape, w.dtype) for w in shards],
        in_specs=_hbm_specs(nw), out_specs=_hbm_specs(nw),
        scratch_shapes=[pltpu.SemaphoreType.DMA((nw, 6)), pltpu.SemaphoreType.DMA((nw, 6))],
        compiler_params=_params(),
    )(*shards)


def _swap_halves(gs, name):
    nw = len(gs)

    def body(*refs):
        in_refs, out_refs = refs[:nw], refs[nw:2 * nw]
        send_sems, recv_sems = refs[2 * nw:]
        x, y, c, _ = _place()
        cps = []
        for w in range(nw):
            h = in_refs[w].shape[1] // 2
            src = in_refs[w].at[:, pl.ds(pl.multiple_of((1 - c) * h, 2 * SUBLANES), h), :]
            cps.append(pltpu.make_async_remote_copy(
                src_ref=src, dst_ref=out_refs[w], send_sem=send_sems.at[w], recv_sem=recv_sems.at[w],
                device_id=(x, y, 1 - c), device_id_type=MESH))
            cps[-1].start()
        for cp in cps:
            cp.wait()

    return pl.pallas_call(
        body, name=name,
        out_shape=[jax.ShapeDtypeStruct((N_CHIPS, g.shape[1] // 2, g.shape[2]), g.dtype) for g in gs],
        in_specs=_hbm_specs(nw), out_specs=_hbm_specs(nw),
        scratch_shapes=[pltpu.SemaphoreType.DMA((nw,)), pltpu.SemaphoreType.DMA((nw,))], compiler_params=_params(),
    )(*gs)


def _exchange_chips(s1s, name):
    nw = len(s1s)

    def body(*refs):
        in_refs, out_refs = refs[:nw], refs[nw:2 * nw]
        send_sems, recv_sems = refs[2 * nw:]
        x, y, c, chips = _place()
        cps = [pltpu.make_async_remote_copy(
            src_ref=in_refs[w].at[2 * chip[0] + chip[1]], dst_ref=out_refs[w].at[j], send_sem=send_sems.at[w, j],
            recv_sem=recv_sems.at[w, j], device_id=(*chip, c), device_id_type=MESH)
            for w in range(nw) for j, chip in enumerate(chips)]
        for cp in cps:
            cp.start()
        for cp in cps:
            cp.wait()

    return pl.pallas_call(
        body, name=name, out_shape=[jax.ShapeDtypeStruct((N_CHIPS - 1,) + s.shape[1:], s.dtype) for s in s1s],
        in_specs=_hbm_specs(nw), out_specs=_hbm_specs(nw),
        scratch_shapes=[pltpu.SemaphoreType.DMA((nw, 3)), pltpu.SemaphoreType.DMA((nw, 3))],
        compiler_params=_params(),
    )(*s1s)


def _share_halves(fs, name):
    nw = len(fs)

    def body(*refs):
        out_refs = refs[nw:2 * nw]
        send_sems, recv_sems = refs[2 * nw:]
        x, y, c, _ = _place()
        sends = []
        for w in range(nw):
            mine = _half_rows(out_refs[w], c)
            sends.append(pltpu.make_async_remote_copy(
                src_ref=mine, dst_ref=mine, send_sem=send_sems.at[w], recv_sem=recv_sems.at[w],
                device_id=(x, y, 1 - c), device_id_type=MESH))
            sends[-1].start()
        for w in range(nw):
            other = _half_rows(out_refs[w], 1 - c)
            pltpu.make_async_remote_copy(
                src_ref=other, dst_ref=other, send_sem=send_sems.at[w], recv_sem=recv_sems.at[w],
                device_id=(x, y, 1 - c), device_id_type=MESH).wait_recv()
        for cp in sends:
            cp.wait_send()

    return pl.pallas_call(
        body, name=name, out_shape=[jax.ShapeDtypeStruct(f.shape, f.dtype) for f in fs],
        in_specs=_hbm_specs(nw), out_specs=_hbm_specs(nw), input_output_aliases={w: w for w in range(nw)},
        scratch_shapes=[pltpu.SemaphoreType.DMA((nw,)), pltpu.SemaphoreType.DMA((nw,))], compiler_params=_params(),
    )(*fs)


def _add_sibling(g, r1, place, name):
    nch, h, cols = r1.shape
    tr = _div_tile(h, 256, 2 * SUBLANES)
    nb = h // tr

    def body(place_ref, g_ref, r_ref, o_ref):
        del place_ref
        o_ref[...] = (g_ref[...].astype(F32) + r_ref[...].astype(F32)).astype(BF16)

    spec = pltpu.PrefetchScalarGridSpec(
        num_scalar_prefetch=1, grid=(nch, nb),
        in_specs=[pl.BlockSpec((None, tr, cols), lambda k, i, p: (k, p[0] * nb + i, 0)),
                  pl.BlockSpec((None, tr, cols), lambda k, i, p: (k, i, 0))],
        out_specs=pl.BlockSpec((None, tr, cols), lambda k, i, p: (k, i, 0)))
    return pl.pallas_call(body, name=name, grid_spec=spec, out_shape=jax.ShapeDtypeStruct((nch, h, cols), BF16),
                          compiler_params=_params())(place, g, r1)


def _add_chips(s1, r2, place, name):
    _, h, cols = s1.shape
    tr = _div_tile(h, 256, 2 * SUBLANES)
    nb = h // tr

    def body(place_ref, s_ref, r_ref, o_ref):
        del place_ref
        acc = s_ref[...].astype(F32)
        for j in range(N_CHIPS - 1):
            acc = acc + r_ref[j].astype(F32)
        o_ref[...] = acc

    spec = pltpu.PrefetchScalarGridSpec(
        num_scalar_prefetch=1, grid=(nb,),
        in_specs=[pl.BlockSpec((None, tr, cols), lambda i, p: (p[1], i, 0)),
                  pl.BlockSpec((N_CHIPS - 1, tr, cols), lambda i, p: (0, i, 0))],
        out_specs=pl.BlockSpec((tr, cols), lambda i, p: (p[0] * nb + i, 0)))
    return pl.pallas_call(body, name=name, grid_spec=spec, out_shape=jax.ShapeDtypeStruct((2 * h, cols), F32),
                          compiler_params=_params())(place, s1, r2)


def _quarter_turn(m):
    h = m.shape[-1] // 2
    return jnp.concatenate([-m[..., h:], m[..., :h]], axis=-1)


def _quarter_turn_back(m):
    h = m.shape[-1] // 2
    return jnp.concatenate([m[..., h:], -m[..., :h]], axis=-1)


def _join_cols(sh):
    return jnp.concatenate([sh[k] for k in range(N_CHIPS)], axis=1)


def _split_cols(full):
    c = full.shape[1] // N_CHIPS
    return jnp.stack([full[:, k * c:(k + 1) * c] for k in range(N_CHIPS)])


def kernel(x, c, positions, w_ada, b_ada, pre_norm1_g, w_in, gm_ln_g, gm_ln_b, gm_w_s, gm_b_s, w_branch_a, q_norm_g, w_uq, kv_norm_g, w_ukv, w_branch_b, w_out, post_norm1_g, pre_norm2_g, w_up, conv_w, conv_b, w_down, post_norm2_g, loss_target, m_w_ada, m_b_ada, m_pre_norm1_g, m_w_in, m_gm_ln_g, m_gm_ln_b, m_gm_w_s, m_gm_b_s, m_w_branch_a, m_q_norm_g, m_w_uq, m_kv_norm_g, m_w_ukv, m_w_branch_b, m_w_out, m_post_norm1_g, m_pre_norm2_g, m_w_up, m_conv_w, m_conv_b, m_w_down, m_post_norm2_g, v_w_ada, v_b_ada, v_pre_norm1_g, v_w_in, v_gm_ln_g, v_gm_ln_b, v_gm_w_s, v_gm_b_s, v_w_branch_a, v_q_norm_g, v_w_uq, v_kv_norm_g, v_w_ukv, v_w_branch_b, v_w_out, v_post_norm1_g, v_pre_norm2_g, v_w_up, v_conv_w, v_conv_b, v_w_down, v_post_norm2_g):
    given = dict(locals())
    s, d = x.shape[1], x.shape[2]
    gw = gm_ln_g.shape[0]
    ql, kvl = q_norm_g.shape[0], kv_norm_g.shape[0]
    heads = N_CHIPS * w_uq.shape[1] // (NOPE + ROPE)
    ff = N_CHIPS * w_down.shape[0]
    assert gw == d and N_CHIPS * w_ukv.shape[1] == heads * (NOPE + VHEAD)
    ix, iy, ic = lax.axis_index("x"), lax.axis_index("y"), lax.axis_index("c")
    chip = 2 * ix + iy
    dev = 2 * chip + ic
    row = lambda v: v.reshape(1, -1)

    c_all = _all_gather(jnp.pad(c, ((0, SUBLANES - 1), (0, 0))), "gather_c").reshape(N_DEV, SUBLANES, d)[:, 0]
    na = w_ada.shape[1]
    b_ada_mine = lax.dynamic_slice(b_ada, (chip * na,), (na,))
    mod_cols = _ada_fwd(c_all, w_ada, row(b_ada_mine), "ada_fwd")
    mod_all = _all_gather(mod_cols, "gather_mod").reshape(N_CHIPS, N_CORES, N_DEV, na)[:, 0]
    mod = lax.dynamic_index_in_dim(mod_all, dev, axis=1, keepdims=False).reshape(N_MOD, d)
    shift1, scale1, gate1, shift2, scale2, gate2 = (mod[i:i + 1] for i in range(N_MOD))

    mine = [given[n].astype(BF16) for n in BIG]
    gathered = _gather_weights(mine, "gather_weights")
    sh = {n: lax.dynamic_update_slice(g, w[None], (chip, 0, 0)) for n, g, w in zip(BIG, gathered, mine)}
    wi = _join_cols(sh["w_in"])
    o_q, o_kv, o_pe, o_ga = 2 * gw, 2 * gw + ql, 2 * gw + ql + kvl, 2 * gw + ql + kvl + ROPE
    w_in_big = jnp.concatenate([wi[:, :o_q], wi[:, o_ga:]], axis=1)
    w_in_lat = jnp.concatenate([wi[:, o_q:o_ga], _quarter_turn(wi[:, o_pe:o_ga])], axis=1)
    wq = _join_cols(sh["w_uq"]).reshape(ql, heads, NOPE + ROPE)
    w_q = jnp.concatenate([wq, _quarter_turn(wq[:, :, NOPE:])], axis=2).reshape(ql, heads * HEAD_W)
    w_kv = _join_cols(sh["w_ukv"]).reshape(kvl, heads, 2, NOPE).transpose(0, 2, 1, 3).reshape(kvl, 2 * heads * NOPE)
    w_a, w_b, w_o, w_dn = (sh[n].reshape(-1, sh[n].shape[2]) for n in ("w_branch_a", "w_branch_b", "w_out", "w_down"))
    w_upf = sh["w_up"]

    inv = ROPE_THETA ** (-jnp.arange(0, ROPE, 2, dtype=F32) / ROPE)
    ang = positions[0].astype(F32)[:, None] * inv
    cos, sin = jnp.cos(ang), jnp.sin(ang)
    rope_k = jnp.concatenate([cos, cos, sin, sin], axis=1)
    rope_q = jnp.concatenate([jnp.ones((s, NOPE), F32), rope_k], axis=1)

    x2d, tgt = x[0], loss_target[0]
    g_pre1, g_post1, g_pre2, g_post2 = row(pre_norm1_g), row(post_norm1_g), row(pre_norm2_g), row(post_norm2_g)
    ln_g, ln_b, q_g, kv_g = row(gm_ln_g), row(gm_ln_b), row(q_norm_g), row(kv_norm_g)
    b_s_t = gm_b_s.T
    conv_wf = _all_gather(jnp.pad(conv_w, ((0, SUBLANES - CONV_TAPS), (0, 0))), "gather_conv_w")
    conv_wf = conv_wf.reshape(N_CHIPS, N_CORES, SUBLANES, conv_w.shape[1])[:, 0, :CONV_TAPS]
    conv_wf = conv_wf.transpose(1, 0, 2).reshape(CONV_TAPS, 2 * ff)
    conv_bf = row(conv_b)

    h1 = _prenorm(x2d, g_pre1, scale1, shift1, "prenorm1")
    z_big = _matmul(h1, w_in_big, mode="nn", out_dtype=F32, name="mm_z_big", tm=s)
    z_lat = _matmul(h1, w_in_lat, mode="nn", out_dtype=F32, name="mm_z_lat", tm=s, tn=1024)
    a_act = _gmlp_fwd(z_big, ln_g, ln_b, gm_w_s, b_s_t, "gmlp_fwd")
    qn, kvn, kr = _mla_prep(z_lat, q_g, kv_g, rope_k, "mla_prep")
    q_rot = _matmul(qn, w_q, mode="nn", out_dtype=BF16, name="mm_q", tm=s, tn=HEAD_W, mul=rope_q)
    kv_all = _matmul(kvn, w_kv, mode="nn", out_dtype=BF16, name="mm_kv", tm=s, tn=1024)
    o_att, lse = _attn_fwd(q_rot, kv_all, kr, heads, "attn_fwd")
    y_a = _matmul(a_act, w_a, mode="nn", out_dtype=F32, name="mm_y_a", tm=s)
    y_b = _matmul(o_att, w_b, mode="nn", out_dtype=F32, name="mm_y_b", tm=s)
    merged = _merge(z_big, y_a, y_b, "merge")
    y1 = _matmul(merged, w_o, mode="nn", out_dtype=F32, name="mm_y1", tm=s)
    x1, h2 = _post_pre(x2d, y1, gate1, g_post1, g_pre2, scale2, shift2, "post1_pre2")

    up_pre = _matmul(h2, w_upf, mode="nn", out_dtype=BF16, name="mm_up", tm=s, tn=1408)
    act = _conv_fwd(up_pre, conv_wf, conv_bf, "conv_fwd")
    ffn = _matmul(act, w_dn, mode="nn", out_dtype=F32, name="mm_ffn", tm=s, tk=1408)

    dffn, dgate2, g_post2_grad, dx2, loss_part = _post_bwd(ffn, gate2, g_post2, "post2_bwd", xin=x1, target=tgt)
    loss = lax.psum(loss_part[0, 0], ("x", "y", "c"))
    dact = _matmul(dffn, w_dn, mode="nt", out_dtype=BF16, name="mm_dact", tm=s)
    gw_down = _matmul(act, dffn, mode="tn", out_dtype=BF16, name="mm_gw_down", tn=1024, tk=s)
    dup, gcw_g, gcw_v, gcb_g, gcb_v = _conv_bwd(up_pre, dact, conv_wf, conv_bf, "conv_bwd")
    dh2 = _matmul(dup, w_upf, mode="nt", out_dtype=F32, name="mm_dh2", tm=s, tk=1408)
    gw_up = _matmul(h2, dup, mode="tn", out_dtype=BF16, name="mm_gw_up", tn=1408, tk=s, out_groups=N_CHIPS)
    dx1, dshift2, dscale2, g_pre2_grad = _prenorm_bwd(x1, dh2, dx2, g_pre2, scale2, "prenorm2_bwd")

    dy1, dgate1, g_post1_grad = _post_bwd(y1, gate1, g_post1, "post1_bwd", dxo=dx1)
    dmerged = _matmul(dy1, w_o, mode="nt", out_dtype=F32, name="mm_dmerged", tm=s)
    gw_out = _matmul(merged, dy1, mode="tn", out_dtype=BF16, name="mm_gw_out", tn=1024, tk=s)
    dy_a, dy_b, dz_big = _merge_bwd(dmerged, z_big, y_a, y_b, "merge_bwd")
    da = _matmul(dy_a, w_a, mode="nt", out_dtype=F32, name="mm_da", tm=s)
    gw_a = _matmul(a_act, dy_a, mode="tn", out_dtype=BF16, name="mm_gw_a", tn=1024, tk=s)
    do = _matmul(dy_b, w_b, mode="nt", out_dtype=BF16, name="mm_do", tm=s)
    gw_b = _matmul(o_att, dy_b, mode="tn", out_dtype=BF16, name="mm_gw_b", tn=1024, tk=s)
    dz_big, g_ws, g_bs_t, g_ln_g, g_ln_b = _gmlp_bwd(z_big, da, dz_big, ln_g, ln_b, gm_w_s, b_s_t, "gmlp_bwd")
    dq, dk, dv = _attn_bwd(q_rot, kv_all, kr, o_att, do, lse, heads, "attn_bwd")
    dq_big, dkv, dkk = _mla_bwd_mid(dq, dk, dv, rope_q, rope_k, heads, "mla_bwd_mid")
    gw_q = _matmul(qn, dq_big, mode="tn", out_dtype=F32, name="mm_gw_q", tn=1024, tk=s)
    dqn = _matmul(dq_big, w_q, mode="nt", out_dtype=F32, name="mm_dqn", tm=s, tk=1024)
    gw_kv = _matmul(kvn, dkv, mode="tn", out_dtype=BF16, name="mm_gw_kv", tn=1024, tk=s)
    dkvn = _matmul(dkv, w_kv, mode="nt", out_dtype=F32, name="mm_dkvn", tm=s, tk=1024)
    dz_lat, g_q, g_kv = _mla_bwd_post(z_lat, dqn, dkvn, dkk, q_g, kv_g, "mla_bwd_post")
    dh1 = _matmul(dz_big, w_in_big, mode="nt", out_dtype=F32, name="mm_dh1_big", tm=s)
    dh1 = _matmul(dz_lat, w_in_lat, mode="nt", out_dtype=F32, name="mm_dh1_lat", tm=s, tk=1024, add=dh1)
    gw_in_big = _matmul(h1, dz_big, mode="tn", out_dtype=BF16, name="mm_gw_in_big", tn=1024, tk=s)
    gw_in_lat = _matmul(h1, dz_lat, mode="tn", out_dtype=F32, name="mm_gw_in_lat", tn=1024, tk=s)
    grad_x, dshift1, dscale1, g_pre1_grad = _prenorm_bwd(x2d, dh1, dx1, g_pre1, scale1, "prenorm1_bwd")

    dmod = jnp.concatenate([dshift1, dscale1, dgate1, dshift2, dscale2, dgate2], axis=1)
    dmod_all = _all_gather(jnp.pad(dmod, ((0, SUBLANES - 1), (0, 0))), "gather_dmod")
    dmod_all = dmod_all.reshape(N_DEV, SUBLANES, N_MOD * d)[:, 0]
    grad_b_ada = _sum_leading(dmod_all.reshape(N_DEV, 1, N_MOD * d), "sum_b_ada")[0]
    dmod_mine = lax.dynamic_slice(dmod_all, (0, chip * na), (N_DEV, na))
    grad_w_ada = _ada_bwd(c_all.T, dmod_mine, "ada_bwd")

    gq = gw_q.reshape(ql, heads, HEAD_W)
    gq_pe = gq[:, :, NOPE:NOPE + ROPE] + _quarter_turn_back(gq[:, :, NOPE + ROPE:])
    g_pe = gw_in_lat[:, ql + kvl:ql + kvl + ROPE] + _quarter_turn_back(gw_in_lat[:, ql + kvl + ROPE:])
    rows_of = lambda g: g.reshape(N_CHIPS, g.shape[0] // N_CHIPS, g.shape[1])
    partial_big = {
        "w_in": _split_cols(jnp.concatenate([gw_in_big[:, :o_q], gw_in_lat[:, :ql + kvl].astype(BF16),
                                             g_pe.astype(BF16), gw_in_big[:, o_q:]], axis=1)),
        "w_branch_a": rows_of(gw_a),
        "w_uq": _split_cols(jnp.concatenate([gq[:, :, :NOPE], gq_pe], axis=2)
                            .reshape(ql, heads * (NOPE + ROPE)).astype(BF16)),
        "w_ukv": _split_cols(gw_kv.reshape(kvl, 2, heads, NOPE).transpose(0, 2, 1, 3).reshape(kvl, heads * 2 * NOPE)),
        "w_branch_b": rows_of(gw_b), "w_out": rows_of(gw_out), "w_up": gw_up, "w_down": rows_of(gw_down),
    }

    place = jnp.stack([ic, chip]).astype(jnp.int32)
    gs = [partial_big[n] for n in BIG]
    from_sibling = _swap_halves(gs, "rs_swap_halves")
    chip_sums = [_add_sibling(g, r1, place, "rs_add_sibling_" + n) for n, g, r1 in zip(BIG, gs, from_sibling)]
    from_chips = _exchange_chips(chip_sums, "rs_exchange_chips")
    halves = [_add_chips(s1, r2, place, "rs_add_chips_" + n) for n, s1, r2 in zip(BIG, chip_sums, from_chips)]
    grads = dict(zip(BIG, _share_halves(halves, "rs_share_halves")))
    grads["w_ada"] = grad_w_ada

    partial = {
        "pre_norm1_g": g_pre1_grad, "gm_ln_g": g_ln_g, "gm_ln_b": g_ln_b, "gm_w_s": g_ws, "gm_b_s": g_bs_t[:, :gm_b_s.shape[0]].T,
        "q_norm_g": g_q, "kv_norm_g": g_kv, "post_norm1_g": g_post1_grad, "pre_norm2_g": g_pre2_grad,
        "conv_w": jnp.concatenate([gcw_g, gcw_v], axis=1), "conv_b": jnp.concatenate([gcb_g, gcb_v], axis=1),
        "post_norm2_g": g_post2_grad,
    }
    flat = jnp.concatenate([partial[n].reshape(-1) for n in SMALL_PARTIAL])
    n_small = flat.shape[0]
    rows_small = -(-n_small // (LANES * SUBLANES)) * SUBLANES
    flat = jnp.pad(flat, (0, rows_small * LANES - n_small)).reshape(rows_small, LANES)
    small_sum = _sum_leading(_all_gather(flat, "gather_small").reshape(N_DEV, rows_small, LANES), "sum_small")
    small_sum = small_sum.reshape(-1)
    off = 0
    for n in SMALL_PARTIAL:
        shape = (CONV_TAPS, 2 * ff) if n == "conv_w" else given[n].shape
        size = partial[n].size
        grads[n] = small_sum[off:off + size].reshape(shape)
        off += size
    grads["conv_w"] = lax.dynamic_slice(grads["conv_w"], (0, chip * conv_w.shape[1]), conv_w.shape)
    grads["b_ada"] = grad_b_ada

    delta, new_m, new_v = {}, {}, {}
    for n in ("w_ada",) + BIG:
        delta[n], new_m[n], new_v[n] = _adamw(given[n], grads[n], given["m_" + n], given["v_" + n], "adamw_" + n)

    def small_pack(prefix, source):
        v = jnp.concatenate([source[prefix + n].reshape(-1) for n in SMALL])
        rows = -(-v.shape[0] // (LANES * SUBLANES)) * SUBLANES
        return jnp.pad(v, (0, rows * LANES - v.shape[0])).reshape(rows, LANES)

    outs = _adamw(small_pack("", given), small_pack("", grads), small_pack("m_", given), small_pack("v_", given),
                  "adamw_small")
    off = 0
    for n in SMALL:
        size = given[n].size
        for store, packed_out in zip((delta, new_m, new_v), outs):
            store[n] = packed_out.reshape(-1)[off:off + size].reshape(given[n].shape)
        off += size

    return (loss, grad_x[None], *[grads[n] for n in WEIGHTS], *[delta[n] for n in WEIGHTS],
            *[new_m[n] for n in WEIGHTS], *[new_v[n] for n in WEIGHTS])
```

```python
import functools

import jax
import jax.numpy as jnp
from jax import lax
from jax.experimental import pallas as pl
from jax.experimental.pallas import tpu as pltpu

F32 = jnp.float32
BF16 = jnp.bfloat16
MESH = pl.DeviceIdType.MESH
HBM = pltpu.HBM

EPS = 1e-6
NOPE, ROPE, VHEAD = 128, 64, 128
HEAD_W = NOPE + 2 * ROPE
ROPE_THETA = 10000.0
CONV_TAPS = 3
N_MOD = 6
N_CHIPS, N_CORES, N_DEV = 4, 2, 8
ADAM_LR, ADAM_B1, ADAM_B2, ADAM_EPS, ADAM_WD, ADAM_STEP = 0.001, 0.9, 0.999, 1e-08, 0.01, 10

LANES = 128
SUBLANES = 8
VMEM_LIMIT = 56 * 2**20

BIG = ("w_in", "w_branch_a", "w_uq", "w_ukv", "w_branch_b", "w_out", "w_up", "w_down")
WEIGHTS = ("w_ada", "b_ada", "pre_norm1_g", "w_in", "gm_ln_g", "gm_ln_b", "gm_w_s", "gm_b_s", "w_branch_a",
           "q_norm_g", "w_uq", "kv_norm_g", "w_ukv", "w_branch_b", "w_out", "post_norm1_g", "pre_norm2_g",
           "w_up", "conv_w", "conv_b", "w_down", "post_norm2_g")
SMALL_PARTIAL = ("pre_norm1_g", "gm_ln_g", "gm_ln_b", "gm_w_s", "gm_b_s", "q_norm_g", "kv_norm_g", "post_norm1_g",
                 "pre_norm2_g", "conv_w", "conv_b", "post_norm2_g")
SMALL = ("b_ada",) + SMALL_PARTIAL


def _div_tile(n, cap, mult=LANES):
    t = (min(cap, n) // mult) * mult
    while t >= mult:
        if n % t == 0:
            return t
        t -= mult
    return n


def _params(**kw):
    return pltpu.CompilerParams(vmem_limit_bytes=VMEM_LIMIT, **kw)


def _row_spec(width):
    return pl.BlockSpec((1, width), lambda *_: (0, 0))


def _gelu(x):
    k = 0.7978845608028654
    return 0.5 * x * (1.0 + jnp.tanh(k * (x + 0.044715 * x * x * x)))


def _gelu_grad(x):
    k = 0.7978845608028654
    t = jnp.tanh(k * (x + 0.044715 * x * x * x))
    return 0.5 * (1.0 + t) + 0.5 * x * (1.0 - t * t) * k * (1.0 + 3.0 * 0.044715 * x * x)


def _sigmoid(x):
    return 1.0 / (1.0 + jnp.exp(-x))


def _dot(a, b, dims):
    return lax.dot_general(a, b, (dims, ((), ())), preferred_element_type=F32)


NN = ((1,), (0,))
NT = ((1,), (1,))
TN = ((0,), (0,))


def _logical(arr):
    if arr.ndim == 2:
        return arr.shape[0], arr.shape[1], arr.shape[1]
    return arr.shape[1], arr.shape[0] * arr.shape[2], arr.shape[2]


def _tile_spec(ndim, group_w, blk_rows, blk_cols, row_of, col_of):
    if ndim == 2:
        return pl.BlockSpec((blk_rows, blk_cols), lambda i, j, k: (row_of(i, j, k), col_of(i, j, k)))
    per = group_w // blk_cols
    return pl.BlockSpec((None, blk_rows, blk_cols),
                        lambda i, j, k: (col_of(i, j, k) // per, row_of(i, j, k), col_of(i, j, k) % per))


def _matmul(a, b, *, mode, out_dtype, name, tm=512, tn=512, tk=2048, mul=None, add=None, out_groups=None, comm=None):
    ar, ac, agw = _logical(a)
    br, bc, bgw = _logical(b)
    if mode == "nn":
        m, kd, n = ar, ac, bc
        m_w, k_w, n_w = (), (agw,), (bgw,)
    elif mode == "nt":
        m, kd, n = ar, ac, br
        m_w, k_w, n_w = (), (agw, bgw), ()
    else:
        m, kd, n = ac, ar, bc
        m_w, k_w, n_w = (agw,), (), (bgw,)
    if out_groups is not None:
        n_w = n_w + (n // out_groups,)
    tm = _div_tile(min((m,) + m_w), tm, SUBLANES)
    tn = _div_tile(min((n,) + n_w), tn)
    tk = _div_tile(min((kd,) + k_w), tk)
    assert all(w % tn == 0 for w in n_w) and all(w % tk == 0 for w in k_w) and all(w % tm == 0 for w in m_w)
    nk = kd // tk
    dims = {"nn": NN, "nt": NT, "tn": TN}[mode]
    gi, gj, gk = (lambda i, j, k: i), (lambda i, j, k: j), (lambda i, j, k: k)
    if mode == "nn":
        a_spec = _tile_spec(a.ndim, agw, tm, tk, gi, gk)
        b_spec = _tile_spec(b.ndim, bgw, tk, tn, gk, gj)
    elif mode == "nt":
        a_spec = _tile_spec(a.ndim, agw, tm, tk, gi, gk)
        b_spec = _tile_spec(b.ndim, bgw, tn, tk, gj, gk)
    else:
        a_spec = _tile_spec(a.ndim, agw, tk, tm, gk, gi)
        b_spec = _tile_spec(b.ndim, bgw, tk, tn, gk, gj)
    in_specs, operands = [a_spec, b_spec], [a, b]
    if mul is not None:
        assert mul.shape == (m, tn)
        in_specs.append(pl.BlockSpec((tm, tn), lambda i, j, k: (i, 0)))
        operands.append(mul)
    if add is not None:
        in_specs.append(pl.BlockSpec((tm, tn), lambda i, j, k: (i, j)))
        operands.append(add)

    def body(*refs):
        a_ref, b_ref = refs[0], refs[1]
        pos = 2
        mul_ref = add_ref = None
        if mul is not None:
            mul_ref, pos = refs[pos], pos + 1
        if add is not None:
            add_ref, pos = refs[pos], pos + 1
        o_ref = refs[pos]

        def finish(r):
            if mul_ref is not None:
                r = r * mul_ref[...]
            if add_ref is not None:
                r = r + add_ref[...]
            o_ref[...] = r.astype(out_dtype)

        part = _dot(a_ref[...], b_ref[...], dims)
        if nk == 1:
            finish(part)
        else:
            acc_ref = refs[pos + 1]
            k = pl.program_id(2)

            @pl.when(k == 0)
            def _():
                acc_ref[...] = part

            @pl.when(k > 0)
            def _():
                acc_ref[...] += part

            @pl.when(k == nk - 1)
            def _():
                finish(acc_ref[...])

    if out_groups is None:
        out_spec, out_dims = _tile_spec(2, n, tm, tn, gi, gj), (m, n)
    else:
        out_spec, out_dims = _tile_spec(3, n // out_groups, tm, tn, gi, gj), (out_groups, m, n // out_groups)
    return _call(body, operands, comm, name=name, grid=(m // tm, n // tn, nk), in_specs=in_specs, out_specs=out_spec,
                 out_shape=jax.ShapeDtypeStruct(out_dims, out_dtype),
                 scratch_shapes=[] if nk == 1 else [pltpu.VMEM((tm, tn), F32)])


def _accumulate(ref, value):
    @pl.when(pl.program_id(0) == 0)
    def _():
        ref[...] = value

    @pl.when(pl.program_id(0) > 0)
    def _():
        ref[...] += value


def _colsum(v):
    return jnp.sum(v, axis=0, keepdims=True)


def _rowmean(v):
    return jnp.mean(v, axis=-1, keepdims=True)


def _prenorm(x, g, scale, shift, name):
    s, d = x.shape
    tb = _div_tile(s, 256, SUBLANES)

    def body(x_ref, g_ref, sc_ref, sh_ref, h_ref):
        xv = x_ref[...]
        r = lax.rsqrt(_rowmean(xv * xv) + EPS)
        h_ref[...] = ((xv * r) * g_ref[...] * (1.0 + sc_ref[...]) + sh_ref[...]).astype(BF16)

    blk = pl.BlockSpec((tb, d), lambda i: (i, 0))
    return pl.pallas_call(
        body, name=name, grid=(s // tb,), in_specs=[blk, _row_spec(d), _row_spec(d), _row_spec(d)],
        out_specs=blk, out_shape=jax.ShapeDtypeStruct((s, d), BF16), compiler_params=_params(),
    )(x, g, scale, shift)


def _post_pre(x, y, gate, pg, g2, scale2, shift2, name):
    s, d = x.shape
    tb = _div_tile(s, 256, SUBLANES)

    def body(x_ref, y_ref, gate_ref, pg_ref, g2_ref, sc_ref, sh_ref, x1_ref, h2_ref):
        yv = y_ref[...]
        rp = lax.rsqrt(_rowmean(yv * yv) + EPS)
        x1 = x_ref[...] + gate_ref[...] * ((yv * rp) * pg_ref[...])
        x1_ref[...] = x1
        r2 = lax.rsqrt(_rowmean(x1 * x1) + EPS)
        h2_ref[...] = ((x1 * r2) * g2_ref[...] * (1.0 + sc_ref[...]) + sh_ref[...]).astype(BF16)

    blk = pl.BlockSpec((tb, d), lambda i: (i, 0))
    return pl.pallas_call(
        body, name=name, grid=(s // tb,), in_specs=[blk, blk] + [_row_spec(d)] * 5,
        out_specs=[blk, blk],
        out_shape=[jax.ShapeDtypeStruct((s, d), F32), jax.ShapeDtypeStruct((s, d), BF16)],
        compiler_params=_params(),
    )(x, y, gate, pg, g2, scale2, shift2)


def _post_bwd(y, gate, pg, name, *, dxo=None, xin=None, target=None):
    s, d = y.shape
    tb = _div_tile(s, 256, SUBLANES)
    from_loss = target is not None

    def body(*refs):
        if from_loss:
            y_ref, gate_ref, pg_ref, xin_ref, t_ref, dy_ref, dgate_ref, dpg_ref, dxo_ref, loss_ref = refs
        else:
            y_ref, gate_ref, pg_ref, dxo_in_ref, dy_ref, dgate_ref, dpg_ref = refs
        yv = y_ref[...]
        rp = lax.rsqrt(_rowmean(yv * yv) + EPS)
        yh = yv * rp
        fn = yh * pg_ref[...]
        gate = gate_ref[...]
        if from_loss:
            err = xin_ref[...] + gate * fn - t_ref[...]
            dxo = err * (1.0 / d)
            dxo_ref[...] = dxo
            part = 0.5 * jnp.sum(_rowmean(err * err), axis=0, keepdims=True)
            _accumulate(loss_ref, jnp.broadcast_to(part, loss_ref.shape))
        else:
            dxo = dxo_in_ref[...]
        _accumulate(dgate_ref, _colsum(dxo * fn))
        dfn = dxo * gate
        _accumulate(dpg_ref, _colsum(dfn * yh))
        dyh = dfn * pg_ref[...]
        dy_ref[...] = (rp * (dyh - yh * _rowmean(dyh * yh))).astype(BF16)

    blk = pl.BlockSpec((tb, d), lambda i: (i, 0))
    in_specs = [blk, _row_spec(d), _row_spec(d)]
    out_specs = [blk, _row_spec(d), _row_spec(d)]
    out_shape = [jax.ShapeDtypeStruct((s, d), BF16), jax.ShapeDtypeStruct((1, d), F32),
                 jax.ShapeDtypeStruct((1, d), F32)]
    if from_loss:
        operands = (y, gate, pg, xin, target)
        in_specs += [blk, blk]
        out_specs += [blk, _row_spec(LANES)]
        out_shape += [jax.ShapeDtypeStruct((s, d), F32), jax.ShapeDtypeStruct((1, LANES), F32)]
    else:
        operands = (y, gate, pg, dxo)
        in_specs += [blk]
    return pl.pallas_call(
        body, name=name, grid=(s // tb,), in_specs=in_specs, out_specs=out_specs, out_shape=out_shape,
        compiler_params=_params(),
    )(*operands)


def _prenorm_bwd(xin, dh, dres, g, scale, name, comm=None):
    s, d = xin.shape
    tb = _div_tile(s, 256, SUBLANES)

    def body(x_ref, dh_ref, dres_ref, g_ref, sc_ref, dx_ref, dshift_ref, dscale_ref, dg_ref):
        xv = x_ref[...]
        r = lax.rsqrt(_rowmean(xv * xv) + EPS)
        xn = xv * r
        dh = dh_ref[...]
        g1 = g_ref[...]
        s1 = 1.0 + sc_ref[...]
        _accumulate(dshift_ref, _colsum(dh))
        _accumulate(dscale_ref, _colsum(dh * xn * g1))
        _accumulate(dg_ref, _colsum(dh * xn * s1))
        dxn = dh * g1 * s1
        dx_ref[...] = dres_ref[...] + r * (dxn - xn * _rowmean(dxn * xn))

    blk = pl.BlockSpec((tb, d), lambda i: (i, 0))
    return _call(
        body, (xin, dh, dres, g, scale), comm, name=name, grid=(s // tb,),
        in_specs=[blk, blk, blk, _row_spec(d), _row_spec(d)],
        out_specs=[blk, _row_spec(d), _row_spec(d), _row_spec(d)],
        out_shape=[jax.ShapeDtypeStruct((s, d), F32)] + [jax.ShapeDtypeStruct((1, d), F32)] * 3)


def _merge(z_big, y_a, y_b, name):
    s, d = y_a.shape
    tb = _div_tile(s, 256, SUBLANES)

    def body(zg_ref, ya_ref, yb_ref, o_ref):
        o_ref[...] = (_sigmoid(zg_ref[:, :d]) * ya_ref[...] + _sigmoid(zg_ref[:, d:]) * yb_ref[...]).astype(BF16)

    blk = pl.BlockSpec((tb, d), lambda i: (i, 0))
    return pl.pallas_call(
        body, name=name, grid=(s // tb,), in_specs=[pl.BlockSpec((tb, 2 * d), lambda i: (i, 1)), blk, blk],
        out_specs=blk, out_shape=jax.ShapeDtypeStruct((s, d), BF16), compiler_params=_params(),
    )(z_big, y_a, y_b)


def _merge_bwd(dmerged, z_big, y_a, y_b, name):
    s, d = y_a.shape
    tb = _div_tile(s, 256, SUBLANES)

    def body(dm_ref, zg_ref, ya_ref, yb_ref, dya_ref, dyb_ref, dz_ref):
        dm = dm_ref[...]
        sa, sb = _sigmoid(zg_ref[:, :d]), _sigmoid(zg_ref[:, d:])
        dya_ref[...] = (dm * sa).astype(BF16)
        dyb_ref[...] = (dm * sb).astype(BF16)
        dz_ref[:, :d] = (dm * ya_ref[...] * sa * (1.0 - sa)).astype(BF16)
        dz_ref[:, d:] = (dm * yb_ref[...] * sb * (1.0 - sb)).astype(BF16)

    blk = pl.BlockSpec((tb, d), lambda i: (i, 0))
    wide = pl.BlockSpec((tb, 2 * d), lambda i: (i, 1))
    return pl.pallas_call(
        body, name=name, grid=(s // tb,), in_specs=[blk, wide, blk, blk], out_specs=[blk, blk, wide],
        out_shape=[jax.ShapeDtypeStruct((s, d), BF16), jax.ShapeDtypeStruct((s, d), BF16),
                   jax.ShapeDtypeStruct((s, 4 * d), BF16)],
        compiler_params=_params(),
    )(dmerged, z_big, y_a, y_b)


def _causal_mask(ch):
    q = lax.broadcasted_iota(jnp.int32, (ch, ch), 0)
    p = lax.broadcasted_iota(jnp.int32, (ch, ch), 1)
    return (p <= q).astype(F32)


def _gmlp_norm(zc, lng, lnb, gw):
    u_pre, v_pre = zc[:, :gw], zc[:, gw:]
    vg = _gelu(v_pre)
    mu = _rowmean(vg)
    cen = vg - mu
    rstd = lax.rsqrt(_rowmean(cen * cen) + EPS)
    vhat = cen * rstd
    return u_pre, v_pre, _gelu(u_pre), vhat, rstd, vhat * lng + lnb


def _gmlp_fwd(z_big, ln_g, ln_b, w_s, b_s_t, name):
    s = z_big.shape[0]
    groups, ch, _ = w_s.shape
    gw = ln_g.shape[1]
    gd = gw // groups

    def body(z_ref, lng_ref, lnb_ref, ws_ref, bt_ref, a_ref):
        _, _, u, _, _, vn = _gmlp_norm(z_ref[...], lng_ref[...], lnb_ref[...], gw)
        mask = _causal_mask(ch)
        for g in range(groups):
            cols = slice(g * gd, (g + 1) * gd)
            wm = (ws_ref[g] * mask).astype(BF16)
            mixed = _dot(wm, vn[:, cols].astype(BF16), NN) + bt_ref[:, g:g + 1]
            a_ref[:, cols] = (u[:, cols] * mixed).astype(BF16)

    return pl.pallas_call(
        body, name=name, grid=(s // ch,),
        in_specs=[pl.BlockSpec((ch, 2 * gw), lambda n: (n, 0)), _row_spec(gw), _row_spec(gw),
                  pl.BlockSpec((groups, ch, ch), lambda n: (0, 0, 0)), pl.BlockSpec((ch, groups), lambda n: (0, 0))],
        out_specs=pl.BlockSpec((ch, gw), lambda n: (n, 0)),
        out_shape=jax.ShapeDtypeStruct((s, gw), BF16), compiler_params=_params(),
    )(z_big, ln_g, ln_b, w_s, b_s_t)


def _gmlp_bwd(z_big, da, dz_big, ln_g, ln_b, w_s, b_s_t, name, comm=None):
    s = z_big.shape[0]
    groups, ch, _ = w_s.shape
    gw = ln_g.shape[1]
    gd = gw // groups

    def body(z_ref, da_ref, dzin_ref, lng_ref, lnb_ref, ws_ref, bt_ref, dz_ref, gws_ref, gbt_ref, glng_ref, glnb_ref):
        del dzin_ref
        lng = lng_ref[...]
        u_pre, v_pre, u, vhat, rstd, vn = _gmlp_norm(z_ref[...], lng, lnb_ref[...], gw)
        da = da_ref[...]
        mask = _causal_mask(ch)
        first = pl.program_id(0) == 0
        dvn_parts = []
        lane = lax.broadcasted_iota(jnp.int32, (ch, LANES), 1)
        gb = jnp.zeros((ch, LANES), F32)
        for g in range(groups):
            cols = slice(g * gd, (g + 1) * gd)
            wm = (ws_ref[g] * mask).astype(BF16)
            vn_g = vn[:, cols].astype(BF16)
            mixed = _dot(wm, vn_g, NN) + bt_ref[:, g:g + 1]
            dz_ref[:, cols] = (da[:, cols] * mixed * _gelu_grad(u_pre[:, cols])).astype(BF16)
            dmixed = da[:, cols] * u[:, cols]
            dm16 = dmixed.astype(BF16)
            dvn_parts.append(_dot(wm, dm16, TN))
            gws = _dot(dm16, vn_g, NT) * mask

            @pl.when(first)
            def _(g=g, gws=gws):
                gws_ref[g] = gws

            @pl.when(jnp.logical_not(first))
            def _(g=g, gws=gws):
                gws_ref[g] += gws

            gb = gb + jnp.where(lane == g, jnp.sum(dmixed, axis=1, keepdims=True), 0.0)
        _accumulate(gbt_ref, gb)
        dvn = jnp.concatenate(dvn_parts, axis=1)
        _accumulate(glnb_ref, _colsum(dvn))
        _accumulate(glng_ref, _colsum(dvn * vhat))
        dvh = dvn * lng
        dvg = rstd * (dvh - _rowmean(dvh) - vhat * _rowmean(dvh * vhat))
        dz_ref[:, gw:] = (dvg * _gelu_grad(v_pre)).astype(BF16)

    zspec = pl.BlockSpec((ch, 2 * gw), lambda n: (n, 0))
    return _call(
        body, (z_big, da, dz_big, ln_g, ln_b, w_s, b_s_t), comm, name=name, grid=(s // ch,),
        in_specs=[zspec, pl.BlockSpec((ch, gw), lambda n: (n, 0)), pl.BlockSpec(memory_space=HBM),
                  _row_spec(gw), _row_spec(gw), pl.BlockSpec((groups, ch, ch), lambda n: (0, 0, 0)),
                  pl.BlockSpec((ch, groups), lambda n: (0, 0))],
        out_specs=[zspec, pl.BlockSpec((groups, ch, ch), lambda n: (0, 0, 0)),
                   pl.BlockSpec((ch, LANES), lambda n: (0, 0)), _row_spec(gw), _row_spec(gw)],
        out_shape=[jax.ShapeDtypeStruct(dz_big.shape, BF16), jax.ShapeDtypeStruct((groups, ch, ch), F32),
                   jax.ShapeDtypeStruct((ch, LANES), F32), jax.ShapeDtypeStruct((1, gw), F32),
                   jax.ShapeDtypeStruct((1, gw), F32)],
        input_output_aliases={2: 0})


def _mla_prep(z_lat, q_g, kv_g, rope_k, name):
    s, latw = z_lat.shape
    ql, kvl = q_g.shape[1], kv_g.shape[1]
    tb = _div_tile(s, 256, SUBLANES)

    def body(z_ref, qg_ref, kvg_ref, t_ref, qn_ref, kvn_ref, kr_ref):
        q = z_ref[:, :ql]
        qn_ref[...] = ((q * lax.rsqrt(_rowmean(q * q) + EPS)) * qg_ref[...]).astype(BF16)
        kv = z_ref[:, ql:ql + kvl]
        kvn_ref[...] = ((kv * lax.rsqrt(_rowmean(kv * kv) + EPS)) * kvg_ref[...]).astype(BF16)
        kk = z_ref[:, ql + kvl:] * t_ref[...]
        kr_ref[...] = (kk + pltpu.roll(kk, ROPE, axis=1)).astype(BF16)

    return pl.pallas_call(
        body, name=name, grid=(s // tb,),
        in_specs=[pl.BlockSpec((tb, latw), lambda i: (i, 0)), _row_spec(ql), _row_spec(kvl),
                  pl.BlockSpec((tb, 2 * ROPE), lambda i: (i, 0))],
        out_specs=[pl.BlockSpec((tb, ql), lambda i: (i, 0)), pl.BlockSpec((tb, kvl), lambda i: (i, 0)),
                   pl.BlockSpec((tb, 2 * ROPE), lambda i: (i, 0))],
        out_shape=[jax.ShapeDtypeStruct((s, ql), BF16), jax.ShapeDtypeStruct((s, kvl), BF16),
                   jax.ShapeDtypeStruct((s, 2 * ROPE), BF16)],
        compiler_params=_params(),
    )(z_lat, q_g, kv_g, rope_k)


def _scores(q, k, kr, row0, col0, scale):
    s = (_dot(q[:, :NOPE], k, NT) + _dot(q[:, NOPE:], kr, NT)) * scale
    rows = row0 + lax.broadcasted_iota(jnp.int32, s.shape, 0)
    cols = col0 + lax.broadcasted_iota(jnp.int32, s.shape, 1)
    return jnp.where(cols <= rows, s, -1e30)


def _attn_fwd(q, kv, kr, heads, name, comm=None):
    s = q.shape[0]
    t = _div_tile(s, 512)
    nb = s // t
    scale = float(NOPE + ROPE) ** -0.5

    def body(q_ref, k_ref, kr_ref, v_ref, o_ref, lse_ref, m_ref, l_ref, acc_ref):
        i, j = pl.program_id(1), pl.program_id(2)

        @pl.when(j == 0)
        def _():
            m_ref[...] = jnp.full(m_ref.shape, -1e30, F32)
            l_ref[...] = jnp.zeros(l_ref.shape, F32)
            acc_ref[...] = jnp.zeros(acc_ref.shape, F32)

        @pl.when(j <= i)
        def _():
            sc = _scores(q_ref[...], k_ref[...], kr_ref[...], i * t, j * t, scale)
            m_old = m_ref[...]
            m_new = jnp.maximum(m_old, jnp.max(sc, axis=-1, keepdims=True))
            p = jnp.exp(sc - m_new)
            alpha = jnp.exp(m_old - m_new)
            l_ref[...] = alpha * l_ref[...] + jnp.sum(p, axis=-1, keepdims=True)
            acc_ref[...] = alpha * acc_ref[...] + _dot(p.astype(BF16), v_ref[...], NN)
            m_ref[...] = m_new

        @pl.when(j == i)
        def _():
            o_ref[...] = (acc_ref[...] / l_ref[...]).astype(BF16)
            lse_ref[...] = jnp.broadcast_to(m_ref[...] + jnp.log(l_ref[...]), lse_ref.shape)

    kidx = lambda off: (lambda h, i, j: (jnp.minimum(i, j), off(h)))
    return _call(
        body, (q, kv, kr, kv), comm, name=name, grid=(heads, nb, nb),
        in_specs=[pl.BlockSpec((t, HEAD_W), lambda h, i, j: (i, h)),
                  pl.BlockSpec((t, NOPE), kidx(lambda h: h)),
                  pl.BlockSpec((t, 2 * ROPE), kidx(lambda h: 0)),
                  pl.BlockSpec((t, VHEAD), kidx(lambda h: heads + h))],
        out_specs=[pl.BlockSpec((t, VHEAD), lambda h, i, j: (i, h)),
                   pl.BlockSpec((None, t, LANES), lambda h, i, j: (h, i, 0))],
        out_shape=[jax.ShapeDtypeStruct((s, heads * VHEAD), BF16), jax.ShapeDtypeStruct((heads, s, LANES), F32)],
        scratch_shapes=[pltpu.VMEM((t, 1), F32), pltpu.VMEM((t, 1), F32), pltpu.VMEM((t, VHEAD), F32)])


def _attn_bwd(q, kv, kr, o, do, lse, heads, name, comm=None):
    s = q.shape[0]
    t = _div_tile(s, 512)
    nb = s // t
    scale = float(NOPE + ROPE) ** -0.5

    def body(q_ref, k_ref, kr_ref, v_ref, o_ref, do_ref, lse_ref, dq_ref, dk_ref, dv_ref, dk_acc, dv_acc):
        j, i = pl.program_id(1), pl.program_id(2)

        @pl.when(jnp.logical_and(j == 0, i == 0))
        def _():
            dq_ref[...] = jnp.zeros(dq_ref.shape, F32)

        @pl.when(i == j)
        def _():
            dk_acc[...] = jnp.zeros(dk_acc.shape, F32)
            dv_acc[...] = jnp.zeros(dv_acc.shape, F32)

        @pl.when(i >= j)
        def _():
            qv, kn, krv, do_v = q_ref[...], k_ref[...], kr_ref[...], do_ref[...]
            sc = _scores(qv, kn, krv, i * t, j * t, scale)
            p = jnp.exp(sc - lse_ref[:, :1])
            dv_acc[...] += _dot(p.astype(BF16), do_v, TN)
            dp = _dot(do_v, v_ref[...], NT)
            delta = jnp.sum(do_v.astype(F32) * o_ref[...].astype(F32), axis=-1, keepdims=True)
            ds = (p * (dp - delta) * scale).astype(BF16)
            rows = pl.ds(pl.multiple_of(i * t, t), t)
            dq_ref[rows, :NOPE] += _dot(ds, kn, NN)
            dq_ref[rows, NOPE:] += _dot(ds, krv, NN)
            dk_acc[...] += _dot(ds, qv, TN)

        @pl.when(i == nb - 1)
        def _():
            dk_ref[...] = dk_acc[...].astype(BF16)
            dv_ref[...] = dv_acc[...].astype(BF16)

    qidx = lambda h, j, i: (jnp.maximum(i, j), h)
    return _call(
        body, (q, kv, kr, kv, o, do, lse), comm, name=name, grid=(heads, nb, nb),
        in_specs=[pl.BlockSpec((t, HEAD_W), qidx),
                  pl.BlockSpec((t, NOPE), lambda h, j, i: (j, h)),
                  pl.BlockSpec((t, 2 * ROPE), lambda h, j, i: (j, 0)),
                  pl.BlockSpec((t, VHEAD), lambda h, j, i: (j, heads + h)),
                  pl.BlockSpec((t, VHEAD), qidx), pl.BlockSpec((t, VHEAD), qidx),
                  pl.BlockSpec((None, t, LANES), lambda h, j, i: (h, jnp.maximum(i, j), 0))],
        out_specs=[pl.BlockSpec((s, HEAD_W), lambda h, j, i: (0, h)),
                   pl.BlockSpec((t, HEAD_W), lambda h, j, i: (j, h)),
                   pl.BlockSpec((t, VHEAD), lambda h, j, i: (j, h))],
        out_shape=[jax.ShapeDtypeStruct((s, heads * HEAD_W), F32), jax.ShapeDtypeStruct((s, heads * HEAD_W), BF16),
                   jax.ShapeDtypeStruct((s, heads * VHEAD), BF16)],
        scratch_shapes=[pltpu.VMEM((t, HEAD_W), F32), pltpu.VMEM((t, VHEAD), F32)])


def _mla_bwd_mid(dq, dk, dv, rope_q, rope_k, heads, name):
    s = dq.shape[0]
    tb = _div_tile(s, 256, SUBLANES)

    def body(dq_ref, dk_ref, dv_ref, tq_ref, tk_ref, dqb_ref, dkv_ref, dkk_ref):
        tq = tq_ref[...]
        dkr = jnp.zeros((tb, 2 * ROPE), F32)
        for h in range(heads):
            cols = slice(h * HEAD_W, (h + 1) * HEAD_W)
            dqb_ref[:, cols] = (dq_ref[:, cols] * tq).astype(BF16)
            dkv_ref[:, h * NOPE:(h + 1) * NOPE] = dk_ref[:, h * HEAD_W:h * HEAD_W + NOPE]
            dkr = dkr + dk_ref[:, h * HEAD_W + NOPE:(h + 1) * HEAD_W].astype(F32)
        dkv_ref[:, heads * NOPE:] = dv_ref[...]
        dkk_ref[...] = (dkr + pltpu.roll(dkr, ROPE, axis=1)) * tk_ref[...]

    wq, wv = heads * HEAD_W, heads * VHEAD
    return pl.pallas_call(
        body, name=name, grid=(s // tb,),
        in_specs=[pl.BlockSpec((tb, wq), lambda i: (i, 0)), pl.BlockSpec((tb, wq), lambda i: (i, 0)),
                  pl.BlockSpec((tb, wv), lambda i: (i, 0)), pl.BlockSpec((tb, HEAD_W), lambda i: (i, 0)),
                  pl.BlockSpec((tb, 2 * ROPE), lambda i: (i, 0))],
        out_specs=[pl.BlockSpec((tb, wq), lambda i: (i, 0)), pl.BlockSpec((tb, heads * NOPE + wv), lambda i: (i, 0)),
                   pl.BlockSpec((tb, 2 * ROPE), lambda i: (i, 0))],
        out_shape=[jax.ShapeDtypeStruct((s, wq), BF16), jax.ShapeDtypeStruct((s, heads * NOPE + wv), BF16),
                   jax.ShapeDtypeStruct((s, 2 * ROPE), F32)],
        compiler_params=_params(),
    )(dq, dk, dv, rope_q, rope_k)


def _mla_bwd_post(z_lat, dqn, dkvn, dkk, q_g, kv_g, name):
    s, latw = z_lat.shape
    ql, kvl = q_g.shape[1], kv_g.shape[1]
    tb = _div_tile(s, 256, SUBLANES)

    def norm_bwd(xv, dn, g, dg_ref):
        r = lax.rsqrt(_rowmean(xv * xv) + EPS)
        xh = xv * r
        _accumulate(dg_ref, _colsum(dn * xh))
        dxh = dn * g
        return r * (dxh - xh * _rowmean(dxh * xh))

    def body(z_ref, dqn_ref, dkvn_ref, dkk_ref, qg_ref, kvg_ref, dz_ref, gq_ref, gkv_ref):
        dz_ref[:, :ql] = norm_bwd(z_ref[:, :ql], dqn_ref[...], qg_ref[...], gq_ref).astype(BF16)
        dz_ref[:, ql:ql + kvl] = norm_bwd(z_ref[:, ql:ql + kvl], dkvn_ref[...], kvg_ref[...], gkv_ref).astype(BF16)
        dz_ref[:, ql + kvl:] = dkk_ref[...].astype(BF16)

    return pl.pallas_call(
        body, name=name, grid=(s // tb,),
        in_specs=[pl.BlockSpec((tb, latw), lambda i: (i, 0)), pl.BlockSpec((tb, ql), lambda i: (i, 0)),
                  pl.BlockSpec((tb, kvl), lambda i: (i, 0)), pl.BlockSpec((tb, 2 * ROPE), lambda i: (i, 0)),
                  _row_spec(ql), _row_spec(kvl)],
        out_specs=[pl.BlockSpec((tb, latw), lambda i: (i, 0)), _row_spec(ql), _row_spec(kvl)],
        out_shape=[jax.ShapeDtypeStruct((s, latw), BF16), jax.ShapeDtypeStruct((1, ql), F32),
                   jax.ShapeDtypeStruct((1, kvl), F32)],
        compiler_params=_params(),
    )(z_lat, dqn, dkvn, dkk, q_g, kv_g)


def _shift_down(x, n):
    rows = lax.broadcasted_iota(jnp.int32, x.shape, 0)
    return jnp.where(rows >= n, pltpu.roll(x, n, axis=0), 0.0)


def _shift_up(x, n):
    s = x.shape[0]
    rows = lax.broadcasted_iota(jnp.int32, x.shape, 0)
    return jnp.where(rows < s - n, pltpu.roll(x, s - n, axis=0), 0.0)


def _conv(pre, w_ref, b_ref):
    return (w_ref[2:3, :] * pre + w_ref[1:2, :] * _shift_down(pre, 1) + w_ref[0:1, :] * _shift_down(pre, 2)
            + b_ref[...])


def _conv_fwd(up_pre, conv_w, conv_b, name):
    s, ff2 = up_pre.shape
    ff = ff2 // 2
    tc = _div_tile(ff, 256)
    nb = ff // tc

    def body(pg_ref, pv_ref, wg_ref, wv_ref, bg_ref, bv_ref, act_ref):
        gate = _conv(pg_ref[...].astype(F32), wg_ref, bg_ref)
        val = _conv(pv_ref[...].astype(F32), wv_ref, bv_ref)
        act_ref[...] = (gate * _sigmoid(gate) * val).astype(BF16)

    def col(rows, off):
        return pl.BlockSpec((rows, tc), lambda j: (0, j + off))

    return pl.pallas_call(
        body, name=name, grid=(nb,),
        in_specs=[col(s, 0), col(s, nb), col(CONV_TAPS, 0), col(CONV_TAPS, nb), col(1, 0), col(1, nb)],
        out_specs=col(s, 0), out_shape=jax.ShapeDtypeStruct((s, ff), BF16), compiler_params=_params(),
    )(up_pre, up_pre, conv_w, conv_w, conv_b, conv_b)


def _conv_bwd(up_pre, dact, conv_w, conv_b, name, comm=None):
    s, ff2 = up_pre.shape
    ff = ff2 // 2
    tc = _div_tile(ff, 256)
    nb = ff // tc

    def half(pre, dx, w_ref, dpre_ref, gw_ref, gb_ref):
        gb_ref[...] = _colsum(dx)
        gw_ref[0:1, :] = _colsum(dx * _shift_down(pre, 2))
        gw_ref[1:2, :] = _colsum(dx * _shift_down(pre, 1))
        gw_ref[2:3, :] = _colsum(dx * pre)
        dpre_ref[...] = (w_ref[2:3, :] * dx + w_ref[1:2, :] * _shift_up(dx, 1)
                         + w_ref[0:1, :] * _shift_up(dx, 2)).astype(BF16)

    def body(pg_ref, pv_ref, da_ref, wg_ref, wv_ref, bg_ref, bv_ref, dup_ref, gwg_ref, gwv_ref, gbg_ref, gbv_ref):
        pre_g, pre_v = pg_ref[...].astype(F32), pv_ref[...].astype(F32)
        gate = _conv(pre_g, wg_ref, bg_ref)
        val = _conv(pre_v, wv_ref, bv_ref)
        da = da_ref[...].astype(F32)
        sg = _sigmoid(gate)
        half(pre_v, da * gate * sg, wv_ref, dup_ref.at[1], gwv_ref, gbv_ref)
        half(pre_g, da * val * sg * (1.0 + gate * (1.0 - sg)), wg_ref, dup_ref.at[0], gwg_ref, gbg_ref)

    def col(rows, off):
        return pl.BlockSpec((rows, tc), lambda j: (0, j + off))

    return _call(
        body, (up_pre, up_pre, dact, conv_w, conv_w, conv_b, conv_b), comm, name=name, grid=(nb,),
        in_specs=[col(s, 0), col(s, nb), col(s, 0), col(CONV_TAPS, 0), col(CONV_TAPS, nb), col(1, 0), col(1, nb)],
        out_specs=[pl.BlockSpec((2, s, tc), lambda j: (0, 0, j)), col(CONV_TAPS, 0), col(CONV_TAPS, 0),
                   col(1, 0), col(1, 0)],
        out_shape=[jax.ShapeDtypeStruct((2, s, ff), BF16)] + [jax.ShapeDtypeStruct((CONV_TAPS, ff), F32)] * 2
        + [jax.ShapeDtypeStruct((1, ff), F32)] * 2)


def _ada_fwd(c_all, w, b, name):
    nseq, d = c_all.shape
    na = w.shape[1]
    tn = _div_tile(na, 512)

    def body(c_ref, w_ref, b_ref, o_ref):
        cv = c_ref[...]
        sc = cv * _sigmoid(cv)
        o_ref[...] = jnp.dot(sc, w_ref[...], preferred_element_type=F32, precision=lax.Precision.HIGHEST) + b_ref[...]

    return pl.pallas_call(
        body, name=name, grid=(na // tn,),
        in_specs=[pl.BlockSpec((nseq, d), lambda j: (0, 0)), pl.BlockSpec((d, tn), lambda j: (0, j)),
                  pl.BlockSpec((1, tn), lambda j: (0, j))],
        out_specs=pl.BlockSpec((nseq, tn), lambda j: (0, j)),
        out_shape=jax.ShapeDtypeStruct((nseq, na), F32), compiler_params=_params(),
    )(c_all, w, b)


def _ada_bwd(c_all_t, dmod, name):
    d, nseq = c_all_t.shape
    na = dmod.shape[1]
    tm, tn = _div_tile(d, 256, SUBLANES), _div_tile(na, 512)

    def body(c_ref, dm_ref, o_ref):
        cv = c_ref[...]
        sc = cv * _sigmoid(cv)
        acc = sc[:, 0:1] * dm_ref[0:1, :]
        for bi in range(1, nseq):
            acc = acc + sc[:, bi:bi + 1] * dm_ref[bi:bi + 1, :]
        o_ref[...] = acc

    return pl.pallas_call(
        body, name=name, grid=(d // tm, na // tn),
        in_specs=[pl.BlockSpec((tm, nseq), lambda i, j: (i, 0)), pl.BlockSpec((nseq, tn), lambda i, j: (0, j))],
        out_specs=pl.BlockSpec((tm, tn), lambda i, j: (i, j)),
        out_shape=jax.ShapeDtypeStruct((d, na), F32), compiler_params=_params(),
    )(c_all_t, dmod)


def _adamw(w, g, m, v, name, comm=None):
    rows, cols = w.shape
    tb = _div_tile(rows, max(SUBLANES, (256 * 1024) // cols // SUBLANES * SUBLANES), SUBLANES)
    c1 = 1.0 / (1.0 - ADAM_B1 ** ADAM_STEP)
    c2 = 1.0 / (1.0 - ADAM_B2 ** ADAM_STEP)

    def body(w_ref, g_ref, m_ref, v_ref, d_ref, nm_ref, nv_ref):
        gv = g_ref[...]
        nm = ADAM_B1 * m_ref[...] + (1.0 - ADAM_B1) * gv
        nv = ADAM_B2 * v_ref[...] + (1.0 - ADAM_B2) * (gv * gv)
        nm_ref[...] = nm
        nv_ref[...] = nv
        d_ref[...] = -ADAM_LR * ((nm * c1) / (jnp.sqrt(nv * c2) + ADAM_EPS) + ADAM_WD * w_ref[...])

    blk = pl.BlockSpec((tb, cols), lambda i: (i, 0))
    return _call(body, (w, g, m, v), comm, name=name, grid=(rows // tb,), in_specs=[blk] * 4, out_specs=[blk] * 3,
                 out_shape=[jax.ShapeDtypeStruct((rows, cols), F32)] * 3)


def _sum_leading(parts, name):
    n, rows, cols = parts.shape
    tb = _div_tile(rows, 512, SUBLANES)

    def body(p_ref, o_ref):
        acc = p_ref[0]
        for k in range(1, n):
            acc = acc + p_ref[k]
        o_ref[...] = acc

    return pl.pallas_call(
        body, name=name, grid=(rows // tb,), in_specs=[pl.BlockSpec((n, tb, cols), lambda i: (0, i, 0))],
        out_specs=pl.BlockSpec((tb, cols), lambda i: (i, 0)),
        out_shape=jax.ShapeDtypeStruct((rows, cols), F32), compiler_params=_params(),
    )(parts)


def _place():
    x, y, c = lax.axis_index("x"), lax.axis_index("y"), lax.axis_index("c")
    return x, y, c, [(1 - x, y), (x, 1 - y), (1 - x, 1 - y)]


def _all_gather(block, name):
    m_per, n = block.shape

    def body(x_ref, out_ref, send_sems, recv_sems, local_sem):
        x, y, c, chips = _place()
        me, sibling = (x, y, c), (x, y, 1 - c)

        def rows(px, py, pc):
            return out_ref.at[pl.ds((4 * px + 2 * py + pc) * m_per, m_per), :]

        def copy(k, blk, to, src=None):
            return pltpu.make_async_remote_copy(
                src_ref=rows(*blk) if src is None else src, dst_ref=rows(*blk), send_sem=send_sems.at[k],
                recv_sem=recv_sems.at[k], device_id=to, device_id_type=MESH)

        mine = pltpu.make_async_copy(x_ref, rows(*me), local_sem)
        mine.start()
        first = [copy(0, me, sibling, src=x_ref)]
        first += [copy(1 + j, me, (*chip, c), src=x_ref) for j, chip in enumerate(chips)]
        for cp in first:
            cp.start()
        passed = [copy(4 + j, (*chip, c), sibling) for j, chip in enumerate(chips)]
        for j, chip in enumerate(chips):
            copy(1 + j, (*chip, c), me).wait_recv()
            passed[j].start()
        copy(0, sibling, me).wait_recv()
        for j, chip in enumerate(chips):
            copy(4 + j, (*chip, 1 - c), me).wait_recv()
        for cp in first + passed:
            cp.wait_send()
        mine.wait()

    return pl.pallas_call(
        body, name=name, out_shape=jax.ShapeDtypeStruct((N_DEV * m_per, n), block.dtype),
        in_specs=[pl.BlockSpec(memory_space=pltpu.VMEM)], out_specs=pl.BlockSpec(memory_space=pltpu.VMEM),
        scratch_shapes=[pltpu.SemaphoreType.DMA((7,)), pltpu.SemaphoreType.DMA((7,)), pltpu.SemaphoreType.DMA],
        compiler_params=_params(),
    )(block)


def _hbm_specs(n):
    return [pl.BlockSpec(memory_space=HBM)] * n


def _half_rows(ref, half, lead=None):
    h = ref.shape[-2] // 2
    rows = pl.ds(pl.multiple_of(half * h, 2 * SUBLANES), h)
    return ref.at[rows, :] if lead is None else ref.at[lead, rows, :]


class _Comm:
    def __init__(self, operands, out_shape, scratch, build, aliases=None):
        self.operands, self.out_shape, self.scratch = list(operands), list(out_shape), list(scratch)
        self.build, self.aliases = build, dict(aliases or {})


def _call(body, operands, comm=None, *, name, grid, in_specs, out_specs, out_shape, scratch_shapes=(),
          input_output_aliases=None):
    aliases = dict(input_output_aliases or {})
    if comm is None:
        return pl.pallas_call(
            body, name=name, grid=grid, in_specs=in_specs, out_specs=out_specs, out_shape=out_shape,
            scratch_shapes=list(scratch_shapes), input_output_aliases=aliases, compiler_params=_params())(*operands)
    single = not isinstance(out_shape, (list, tuple))
    outs = [out_shape] if single else list(out_shape)
    ospecs = [out_specs] if single else list(out_specs)
    n_in, n_out, n_scr = len(operands), len(outs), len(scratch_shapes)
    c_in, c_out = len(comm.operands), len(comm.out_shape)
    for i, o in comm.aliases.items():
        aliases[n_in + i] = n_out + o

    def hosted(*refs):
        ins, c_ins = refs[:n_in], refs[n_in:n_in + c_in]
        o0 = n_in + c_in
        o_refs, c_outs = refs[o0:o0 + n_out], refs[o0 + n_out:o0 + n_out + c_out]
        s0 = o0 + n_out + c_out
        scr, sems = refs[s0:s0 + n_scr], refs[s0 + n_scr:]
        start, finish = comm.build(c_ins, c_outs, sems)
        first = last = None
        for dim, size in enumerate(grid):
            at0, at1 = pl.program_id(dim) == 0, pl.program_id(dim) == size - 1
            first = at0 if first is None else jnp.logical_and(first, at0)
            last = at1 if last is None else jnp.logical_and(last, at1)
        pl.when(first)(start)
        body(*ins, *o_refs, *scr)
        pl.when(last)(finish)

    res = pl.pallas_call(
        hosted, name=name, grid=grid, in_specs=list(in_specs) + _hbm_specs(c_in),
        out_specs=ospecs + _hbm_specs(c_out), out_shape=outs + comm.out_shape,
        scratch_shapes=list(scratch_shapes) + comm.scratch, input_output_aliases=aliases,
        compiler_params=_params())(*operands, *comm.operands)
    return (res[0] if single else res[:n_out]), res[n_out:]


def _run_comm(comm, name):
    c_in, c_out = len(comm.operands), len(comm.out_shape)

    def body(*refs):
        start, finish = comm.build(refs[:c_in], refs[c_in:c_in + c_out], refs[c_in + c_out:])
        start()
        finish()

    return pl.pallas_call(
        body, name=name, in_specs=_hbm_specs(c_in), out_specs=_hbm_specs(c_out), out_shape=comm.out_shape,
        scratch_shapes=comm.scratch, input_output_aliases=comm.aliases, compiler_params=_params())(*comm.operands)


def _gather_comm(shards):
    nw = len(shards)

    def build(in_refs, out_refs, sems):
        send_sems, recv_sems = sems
        x, y, c, chips = _place()
        me, sibling = (x, y, c), (x, y, 1 - c)

        def copy(w, k, block, half, to, src=None):
            dst = _half_rows(out_refs[w], half, 2 * block[0] + block[1])
            return pltpu.make_async_remote_copy(
                src_ref=dst if src is None else src, dst_ref=dst, send_sem=send_sems.at[w, k],
                recv_sem=recv_sems.at[w, k], device_id=to, device_id_type=MESH)

        first = [copy(w, j, (x, y), c, (*chip, c), src=_half_rows(in_refs[w], c))
                 for w in range(nw) for j, chip in enumerate(chips)]

        def start():
            for cp in first:
                cp.start()

        def finish():
            passed = []
            for w in range(nw):
                for j, chip in enumerate(chips):
                    copy(w, j, chip, c, me).wait_recv()
                    passed.append(copy(w, 3 + j, chip, c, sibling))
                    passed[-1].start()
            for w in range(nw):
                for j, chip in enumerate(chips):
                    copy(w, 3 + j, chip, 1 - c, me).wait_recv()
            for cp in first + passed:
                cp.wait_send()

        return start, finish

    return _Comm(shards, [jax.ShapeDtypeStruct((N_CHIPS,) + w.shape, w.dtype) for w in shards],
                 [pltpu.SemaphoreType.DMA((nw, 6)), pltpu.SemaphoreType.DMA((nw, 6))], build)


def _swap_comm(gs):
    nw = len(gs)

    def build(in_refs, out_refs, sems):
        send_sems, recv_sems = sems
        x, y, c, _ = _place()
        cps = []
        for w in range(nw):
            h = in_refs[w].shape[1] // 2
            src = in_refs[w].at[:, pl.ds(pl.multiple_of((1 - c) * h, 2 * SUBLANES), h), :]
            cps.append(pltpu.make_async_remote_copy(
                src_ref=src, dst_ref=out_refs[w], send_sem=send_sems.at[w], recv_sem=recv_sems.at[w],
                device_id=(x, y, 1 - c), device_id_type=MESH))

        def start():
            for cp in cps:
                cp.start()

        def finish():
            for cp in cps:
                cp.wait()

        return start, finish

    return _Comm(gs, [jax.ShapeDtypeStruct((N_CHIPS, g.shape[1] // 2, g.shape[2]), g.dtype) for g in gs],
                 [pltpu.SemaphoreType.DMA((nw,)), pltpu.SemaphoreType.DMA((nw,))], build)


def _exchange_comm(s1s):
    nw = len(s1s)

    def build(in_refs, out_refs, sems):
        send_sems, recv_sems = sems
        x, y, c, chips = _place()
        cps = [pltpu.make_async_remote_copy(
            src_ref=in_refs[w].at[2 * chip[0] + chip[1]], dst_ref=out_refs[w].at[j], send_sem=send_sems.at[w, j],
            recv_sem=recv_sems.at[w, j], device_id=(*chip, c), device_id_type=MESH)
            for w in range(nw) for j, chip in enumerate(chips)]

        def start():
            for cp in cps:
                cp.start()

        def finish():
            for cp in cps:
                cp.wait()

        return start, finish

    return _Comm(s1s, [jax.ShapeDtypeStruct((N_CHIPS - 1,) + s.shape[1:], s.dtype) for s in s1s],
                 [pltpu.SemaphoreType.DMA((nw, 3)), pltpu.SemaphoreType.DMA((nw, 3))], build)


def _share_comm(fs):
    nw = len(fs)

    def build(in_refs, out_refs, sems):
        del in_refs
        send_sems, recv_sems = sems
        x, y, c, _ = _place()

        def copy(w, half):
            rows = _half_rows(out_refs[w], half)
            return pltpu.make_async_remote_copy(
                src_ref=rows, dst_ref=rows, send_sem=send_sems.at[w], recv_sem=recv_sems.at[w],
                device_id=(x, y, 1 - c), device_id_type=MESH)

        sends = [copy(w, c) for w in range(nw)]

        def start():
            for cp in sends:
                cp.start()

        def finish():
            for w in range(nw):
                copy(w, 1 - c).wait_recv()
            for cp in sends:
                cp.wait_send()

        return start, finish

    return _Comm(fs, [jax.ShapeDtypeStruct(f.shape, f.dtype) for f in fs],
                 [pltpu.SemaphoreType.DMA((nw,)), pltpu.SemaphoreType.DMA((nw,))], build,
                 aliases={w: w for w in range(nw)})


def _add_sibling(g, r1, place, name):
    nch, h, cols = r1.shape
    tr = _div_tile(h, 256, 2 * SUBLANES)
    nb = h // tr

    def body(place_ref, g_ref, r_ref, o_ref):
        del place_ref
        o_ref[...] = (g_ref[...].astype(F32) + r_ref[...].astype(F32)).astype(BF16)

    spec = pltpu.PrefetchScalarGridSpec(
        num_scalar_prefetch=1, grid=(nch, nb),
        in_specs=[pl.BlockSpec((None, tr, cols), lambda k, i, p: (k, p[0] * nb + i, 0)),
                  pl.BlockSpec((None, tr, cols), lambda k, i, p: (k, i, 0))],
        out_specs=pl.BlockSpec((None, tr, cols), lambda k, i, p: (k, i, 0)))
    return pl.pallas_call(body, name=name, grid_spec=spec, out_shape=jax.ShapeDtypeStruct((nch, h, cols), BF16),
                          compiler_params=_params())(place, g, r1)


def _add_chips(s1, r2, place, name):
    _, h, cols = s1.shape
    tr = _div_tile(h, 256, 2 * SUBLANES)
    nb = h // tr

    def body(place_ref, s_ref, r_ref, o_ref):
        del place_ref
        acc = s_ref[...].astype(F32)
        for j in range(N_CHIPS - 1):
            acc = acc + r_ref[j].astype(F32)
        o_ref[...] = acc

    spec = pltpu.PrefetchScalarGridSpec(
        num_scalar_prefetch=1, grid=(nb,),
        in_specs=[pl.BlockSpec((None, tr, cols), lambda i, p: (p[1], i, 0)),
                  pl.BlockSpec((N_CHIPS - 1, tr, cols), lambda i, p: (0, i, 0))],
        out_specs=pl.BlockSpec((tr, cols), lambda i, p: (p[0] * nb + i, 0)))
    return pl.pallas_call(body, name=name, grid_spec=spec, out_shape=jax.ShapeDtypeStruct((2 * h, cols), F32),
                          compiler_params=_params())(place, s1, r2)


def _quarter_turn(m):
    h = m.shape[-1] // 2
    return jnp.concatenate([-m[..., h:], m[..., :h]], axis=-1)


def _quarter_turn_back(m):
    h = m.shape[-1] // 2
    return jnp.concatenate([m[..., h:], -m[..., :h]], axis=-1)


def _join_cols(sh):
    return jnp.concatenate([sh[k] for k in range(N_CHIPS)], axis=1)


def _split_cols(full):
    c = full.shape[1] // N_CHIPS
    return jnp.stack([full[:, k * c:(k + 1) * c] for k in range(N_CHIPS)])


def kernel(x, c, positions, w_ada, b_ada, pre_norm1_g, w_in, gm_ln_g, gm_ln_b, gm_w_s, gm_b_s, w_branch_a, q_norm_g, w_uq, kv_norm_g, w_ukv, w_branch_b, w_out, post_norm1_g, pre_norm2_g, w_up, conv_w, conv_b, w_down, post_norm2_g, loss_target, m_w_ada, m_b_ada, m_pre_norm1_g, m_w_in, m_gm_ln_g, m_gm_ln_b, m_gm_w_s, m_gm_b_s, m_w_branch_a, m_q_norm_g, m_w_uq, m_kv_norm_g, m_w_ukv, m_w_branch_b, m_w_out, m_post_norm1_g, m_pre_norm2_g, m_w_up, m_conv_w, m_conv_b, m_w_down, m_post_norm2_g, v_w_ada, v_b_ada, v_pre_norm1_g, v_w_in, v_gm_ln_g, v_gm_ln_b, v_gm_w_s, v_gm_b_s, v_w_branch_a, v_q_norm_g, v_w_uq, v_kv_norm_g, v_w_ukv, v_w_branch_b, v_w_out, v_post_norm1_g, v_pre_norm2_g, v_w_up, v_conv_w, v_conv_b, v_w_down, v_post_norm2_g):
    given = dict(locals())
    s, d = x.shape[1], x.shape[2]
    gw = gm_ln_g.shape[0]
    ql, kvl = q_norm_g.shape[0], kv_norm_g.shape[0]
    heads = N_CHIPS * w_uq.shape[1] // (NOPE + ROPE)
    ff = N_CHIPS * w_down.shape[0]
    assert gw == d and N_CHIPS * w_ukv.shape[1] == heads * (NOPE + VHEAD)
    ix, iy, ic = lax.axis_index("x"), lax.axis_index("y"), lax.axis_index("c")
    chip = 2 * ix + iy
    dev = 2 * chip + ic
    row = lambda v: v.reshape(1, -1)

    c_all = _all_gather(jnp.pad(c, ((0, SUBLANES - 1), (0, 0))), "gather_c").reshape(N_DEV, SUBLANES, d)[:, 0]
    na = w_ada.shape[1]
    b_ada_mine = lax.dynamic_slice(b_ada, (chip * na,), (na,))
    mod_cols = _ada_fwd(c_all, w_ada, row(b_ada_mine), "ada_fwd")
    mod_all = _all_gather(mod_cols, "gather_mod").reshape(N_CHIPS, N_CORES, N_DEV, na)[:, 0]
    mod = lax.dynamic_index_in_dim(mod_all, dev, axis=1, keepdims=False).reshape(N_MOD, d)
    shift1, scale1, gate1, shift2, scale2, gate2 = (mod[i:i + 1] for i in range(N_MOD))

    mine = {n: given[n].astype(BF16) for n in BIG}
    gather = lambda names: _gather_comm([mine[n] for n in names])
    whole = lambda n, g: lax.dynamic_update_slice(g, mine[n][None], (chip, 0, 0))
    rows4 = lambda sh4: sh4.reshape(-1, sh4.shape[2])
    wi = _join_cols(whole("w_in", _run_comm(gather(["w_in"]), "gather_w_in")[0]))
    o_q, o_kv, o_pe, o_ga = 2 * gw, 2 * gw + ql, 2 * gw + ql + kvl, 2 * gw + ql + kvl + ROPE
    w_in_big = jnp.concatenate([wi[:, :o_q], wi[:, o_ga:]], axis=1)
    w_in_lat = jnp.concatenate([wi[:, o_q:o_ga], _quarter_turn(wi[:, o_pe:o_ga])], axis=1)

    inv = ROPE_THETA ** (-jnp.arange(0, ROPE, 2, dtype=F32) / ROPE)
    ang = positions[0].astype(F32)[:, None] * inv
    cos, sin = jnp.cos(ang), jnp.sin(ang)
    rope_k = jnp.concatenate([cos, cos, sin, sin], axis=1)
    rope_q = jnp.concatenate([jnp.ones((s, NOPE), F32), rope_k], axis=1)

    x2d, tgt = x[0], loss_target[0]
    g_pre1, g_post1, g_pre2, g_post2 = row(pre_norm1_g), row(post_norm1_g), row(pre_norm2_g), row(post_norm2_g)
    ln_g, ln_b, q_g, kv_g = row(gm_ln_g), row(gm_ln_b), row(q_norm_g), row(kv_norm_g)
    b_s_t = gm_b_s.T
    conv_wf = _all_gather(jnp.pad(conv_w, ((0, SUBLANES - CONV_TAPS), (0, 0))), "gather_conv_w")
    conv_wf = conv_wf.reshape(N_CHIPS, N_CORES, SUBLANES, conv_w.shape[1])[:, 0, :CONV_TAPS]
    conv_wf = conv_wf.transpose(1, 0, 2).reshape(CONV_TAPS, 2 * ff)
    conv_bf = row(conv_b)

    h1 = _prenorm(x2d, g_pre1, scale1, shift1, "prenorm1")
    z_big, (g_uq, g_ukv, g_a) = _matmul(h1, w_in_big, mode="nn", out_dtype=F32, name="mm_z_big", tm=s,
                                        comm=gather(["w_uq", "w_ukv", "w_branch_a"]))
    wq = _join_cols(whole("w_uq", g_uq)).reshape(ql, heads, NOPE + ROPE)
    w_q = jnp.concatenate([wq, _quarter_turn(wq[:, :, NOPE:])], axis=2).reshape(ql, heads * HEAD_W)
    w_kv = _join_cols(whole("w_ukv", g_ukv)).reshape(kvl, heads, 2, NOPE).transpose(0, 2, 1, 3)
    w_kv = w_kv.reshape(kvl, 2 * heads * NOPE)
    w_a = rows4(whole("w_branch_a", g_a))
    z_lat = _matmul(h1, w_in_lat, mode="nn", out_dtype=F32, name="mm_z_lat", tm=s, tn=1024)
    a_act = _gmlp_fwd(z_big, ln_g, ln_b, gm_w_s, b_s_t, "gmlp_fwd")
    qn, kvn, kr = _mla_prep(z_lat, q_g, kv_g, rope_k, "mla_prep")
    q_rot = _matmul(qn, w_q, mode="nn", out_dtype=BF16, name="mm_q", tm=s, tn=HEAD_W, mul=rope_q)
    kv_all = _matmul(kvn, w_kv, mode="nn", out_dtype=BF16, name="mm_kv", tm=s, tn=1024)
    (o_att, lse), (g_b, g_o, g_up) = _attn_fwd(q_rot, kv_all, kr, heads, "attn_fwd",
                                               comm=gather(["w_branch_b", "w_out", "w_up"]))
    w_b, w_o, w_upf = rows4(whole("w_branch_b", g_b)), rows4(whole("w_out", g_o)), whole("w_up", g_up)
    y_a = _matmul(a_act, w_a, mode="nn", out_dtype=F32, name="mm_y_a", tm=s)
    y_b = _matmul(o_att, w_b, mode="nn", out_dtype=F32, name="mm_y_b", tm=s)
    merged = _merge(z_big, y_a, y_b, "merge")
    y1 = _matmul(merged, w_o, mode="nn", out_dtype=F32, name="mm_y1", tm=s)
    x1, h2 = _post_pre(x2d, y1, gate1, g_post1, g_pre2, scale2, shift2, "post1_pre2")

    up_pre, (g_dn,) = _matmul(h2, w_upf, mode="nn", out_dtype=BF16, name="mm_up", tm=s, tn=1408,
                              comm=gather(["w_down"]))
    w_dn = rows4(whole("w_down", g_dn))
    act = _conv_fwd(up_pre, conv_wf, conv_bf, "conv_fwd")
    ffn = _matmul(act, w_dn, mode="nn", out_dtype=F32, name="mm_ffn", tm=s, tk=1408)

    dffn, dgate2, g_post2_grad, dx2, loss_part = _post_bwd(ffn, gate2, g_post2, "post2_bwd", xin=x1, target=tgt)
    loss = lax.psum(loss_part[0, 0], ("x", "y", "c"))
    place = jnp.stack([ic, chip]).astype(jnp.int32)
    rows_of = lambda g: g.reshape(N_CHIPS, g.shape[0] // N_CHIPS, g.shape[1])
    add_sibling = lambda names, gs, r1s: [_add_sibling(g, r1, place, "rs_add_sibling_" + n)
                                          for n, g, r1 in zip(names, gs, r1s)]
    add_chips = lambda names, s1s, r2s: [_add_chips(s1, r2, place, "rs_add_chips_" + n)
                                         for n, s1, r2 in zip(names, s1s, r2s)]
    dact = _matmul(dffn, w_dn, mode="nt", out_dtype=BF16, name="mm_dact", tm=s)
    gp_down = [rows_of(_matmul(act, dffn, mode="tn", out_dtype=BF16, name="mm_gw_down", tn=1024, tk=s))]
    (dup, gcw_g, gcw_v, gcb_g, gcb_v), r1_down = _conv_bwd(up_pre, dact, conv_wf, conv_bf, "conv_bwd",
                                                            comm=_swap_comm(gp_down))
    s1_down = add_sibling(["w_down"], gp_down, r1_down)
    dh2, r2_down = _matmul(dup, w_upf, mode="nt", out_dtype=F32, name="mm_dh2", tm=s, tk=1408,
                           comm=_exchange_comm(s1_down))
    half_down = add_chips(["w_down"], s1_down, r2_down)
    gw_up = _matmul(h2, dup, mode="tn", out_dtype=BF16, name="mm_gw_up", tn=1408, tk=s, out_groups=N_CHIPS)
    dx1, dshift2, dscale2, g_pre2_grad = _prenorm_bwd(x1, dh2, dx2, g_pre2, scale2, "prenorm2_bwd")

    dy1, dgate1, g_post1_grad = _post_bwd(y1, gate1, g_post1, "post1_bwd", dxo=dx1)
    dmerged = _matmul(dy1, w_o, mode="nt", out_dtype=F32, name="mm_dmerged", tm=s)
    gw_out = _matmul(merged, dy1, mode="tn", out_dtype=BF16, name="mm_gw_out", tn=1024, tk=s)
    dy_a, dy_b, dz_big = _merge_bwd(dmerged, z_big, y_a, y_b, "merge_bwd")
    da = _matmul(dy_a, w_a, mode="nt", out_dtype=F32, name="mm_da", tm=s)
    gw_a = _matmul(a_act, dy_a, mode="tn", out_dtype=BF16, name="mm_gw_a", tn=1024, tk=s)
    do = _matmul(dy_b, w_b, mode="nt", out_dtype=BF16, name="mm_do", tm=s)
    gw_b = _matmul(o_att, dy_b, mode="tn", out_dtype=BF16, name="mm_gw_b", tn=1024, tk=s)
    mid = ["w_up", "w_out", "w_branch_a", "w_branch_b"]
    gp_mid = [gw_up, rows_of(gw_out), rows_of(gw_a), rows_of(gw_b)]
    (dz_big, g_ws, g_bs_t, g_ln_g, g_ln_b), r1_mid = _gmlp_bwd(z_big, da, dz_big, ln_g, ln_b, gm_w_s, b_s_t,
                                                                "gmlp_bwd", comm=_swap_comm(gp_mid))
    s1_mid = add_sibling(mid, gp_mid, r1_mid)
    (dq, dk, dv), r2_up_out = _attn_bwd(q_rot, kv_all, kr, o_att, do, lse, heads, "attn_bwd",
                                        comm=_exchange_comm(s1_mid[:2]))
    dq_big, dkv, dkk = _mla_bwd_mid(dq, dk, dv, rope_q, rope_k, heads, "mla_bwd_mid")
    gw_q = _matmul(qn, dq_big, mode="tn", out_dtype=F32, name="mm_gw_q", tn=1024, tk=s)
    dqn = _matmul(dq_big, w_q, mode="nt", out_dtype=F32, name="mm_dqn", tm=s, tk=1024)
    gw_kv = _matmul(kvn, dkv, mode="tn", out_dtype=BF16, name="mm_gw_kv", tn=1024, tk=s)
    dkvn = _matmul(dkv, w_kv, mode="nt", out_dtype=F32, name="mm_dkvn", tm=s, tk=1024)
    dz_lat, g_q, g_kv = _mla_bwd_post(z_lat, dqn, dkvn, dkk, q_g, kv_g, "mla_bwd_post")
    dh1, r2_a_b = _matmul(dz_big, w_in_big, mode="nt", out_dtype=F32, name="mm_dh1_big", tm=s,
                          comm=_exchange_comm(s1_mid[2:]))
    half_mid = add_chips(mid, s1_mid, list(r2_up_out) + list(r2_a_b))
    dh1 = _matmul(dz_lat, w_in_lat, mode="nt", out_dtype=F32, name="mm_dh1_lat", tm=s, tk=1024, add=dh1)
    gw_in_big, shared = _matmul(h1, dz_big, mode="tn", out_dtype=BF16, name="mm_gw_in_big", tn=1024, tk=s,
                                comm=_share_comm(half_down + half_mid))
    grads = dict(zip(["w_down"] + mid, shared))
    gw_in_lat = _matmul(h1, dz_lat, mode="tn", out_dtype=F32, name="mm_gw_in_lat", tn=1024, tk=s)

    gq = gw_q.reshape(ql, heads, HEAD_W)
    gq_pe = gq[:, :, NOPE:NOPE + ROPE] + _quarter_turn_back(gq[:, :, NOPE + ROPE:])
    g_pe = gw_in_lat[:, ql + kvl:ql + kvl + ROPE] + _quarter_turn_back(gw_in_lat[:, ql + kvl + ROPE:])
    last = ["w_in", "w_uq", "w_ukv"]
    gp_last = [
        _split_cols(jnp.concatenate([gw_in_big[:, :o_q], gw_in_lat[:, :ql + kvl].astype(BF16), g_pe.astype(BF16),
                                     gw_in_big[:, o_q:]], axis=1)),
        _split_cols(jnp.concatenate([gq[:, :, :NOPE], gq_pe], axis=2).reshape(ql, heads * (NOPE + ROPE)).astype(BF16)),
        _split_cols(gw_kv.reshape(kvl, 2, heads, NOPE).transpose(0, 2, 1, 3).reshape(kvl, heads * 2 * NOPE)),
    ]
    (grad_x, dshift1, dscale1, g_pre1_grad), r1_last = _prenorm_bwd(x2d, dh1, dx1, g_pre1, scale1, "prenorm1_bwd",
                                                                    comm=_swap_comm(gp_last))
    s1_last = add_sibling(last, gp_last, r1_last)

    dmod = jnp.concatenate([dshift1, dscale1, dgate1, dshift2, dscale2, dgate2], axis=1)
    dmod_all = _all_gather(jnp.pad(dmod, ((0, SUBLANES - 1), (0, 0))), "gather_dmod")
    dmod_all = dmod_all.reshape(N_DEV, SUBLANES, N_MOD * d)[:, 0]
    grad_b_ada = _sum_leading(dmod_all.reshape(N_DEV, 1, N_MOD * d), "sum_b_ada")[0]
    dmod_mine = lax.dynamic_slice(dmod_all, (0, chip * na), (N_DEV, na))
    grads["w_ada"] = _ada_bwd(c_all.T, dmod_mine, "ada_bwd")

    partial = {
        "pre_norm1_g": g_pre1_grad, "gm_ln_g": g_ln_g, "gm_ln_b": g_ln_b, "gm_w_s": g_ws, "gm_b_s": g_bs_t[:, :gm_b_s.shape[0]].T,
        "q_norm_g": g_q, "kv_norm_g": g_kv, "post_norm1_g": g_post1_grad, "pre_norm2_g": g_pre2_grad,
        "conv_w": jnp.concatenate([gcw_g, gcw_v], axis=1), "conv_b": jnp.concatenate([gcb_g, gcb_v], axis=1),
        "post_norm2_g": g_post2_grad,
    }
    flat = jnp.concatenate([partial[n].reshape(-1) for n in SMALL_PARTIAL])
    n_small = flat.shape[0]
    rows_small = -(-n_small // (LANES * SUBLANES)) * SUBLANES
    flat = jnp.pad(flat, (0, rows_small * LANES - n_small)).reshape(rows_small, LANES)
    small_sum = _sum_leading(_all_gather(flat, "gather_small").reshape(N_DEV, rows_small, LANES), "sum_small")
    small_sum = small_sum.reshape(-1)
    off = 0
    for n in SMALL_PARTIAL:
        shape = (CONV_TAPS, 2 * ff) if n == "conv_w" else given[n].shape
        size = partial[n].size
        grads[n] = small_sum[off:off + size].reshape(shape)
        off += size
    grads["conv_w"] = lax.dynamic_slice(grads["conv_w"], (0, chip * conv_w.shape[1]), conv_w.shape)
    grads["b_ada"] = grad_b_ada

    delta, new_m, new_v = {}, {}, {}
    adamw = lambda n, comm=None: _adamw(given[n], grads[n], given["m_" + n], given["v_" + n], "adamw_" + n, comm)
    (delta["w_ada"], new_m["w_ada"], new_v["w_ada"]), r2_last = adamw("w_ada", _exchange_comm(s1_last))
    half_last = add_chips(last, s1_last, r2_last)
    (delta["w_up"], new_m["w_up"], new_v["w_up"]), shared = adamw("w_up", _share_comm(half_last))
    grads.update(zip(last, shared))
    for n in BIG:
        if n != "w_up":
            delta[n], new_m[n], new_v[n] = adamw(n)

    def small_pack(prefix, source):
        v = jnp.concatenate([source[prefix + n].reshape(-1) for n in SMALL])
        rows = -(-v.shape[0] // (LANES * SUBLANES)) * SUBLANES
        return jnp.pad(v, (0, rows * LANES - v.shape[0])).reshape(rows, LANES)

    outs = _adamw(small_pack("", given), small_pack("", grads), small_pack("m_", given), small_pack("v_", given),
                  "adamw_small")
    off = 0
    for n in SMALL:
        size = given[n].size
        for store, packed_out in zip((delta, new_m, new_v), outs):
            store[n] = packed_out.reshape(-1)[off:off + size].reshape(given[n].shape)
        off += size

    return (loss, grad_x[None], *[grads[n] for n in WEIGHTS], *[delta[n] for n in WEIGHTS],
            *[new_m[n] for n in WEIGHTS], *[new_v[n] for n in WEIGHTS])
```

```python
import functools

import jax
import jax.numpy as jnp
from jax import lax
from jax.experimental import pallas as pl
from jax.experimental.pallas import tpu as pltpu

F32 = jnp.float32
BF16 = jnp.bfloat16
MESH = pl.DeviceIdType.MESH
HBM = pltpu.HBM

EPS = 1e-6
NOPE, ROPE, VHEAD = 128, 64, 128
HEAD_W = NOPE + 2 * ROPE
ROPE_THETA = 10000.0
CONV_TAPS = 3
N_MOD = 6
N_CHIPS, N_CORES, N_DEV = 4, 2, 8
ADAM_LR, ADAM_B1, ADAM_B2, ADAM_EPS, ADAM_WD, ADAM_STEP = 0.001, 0.9, 0.999, 1e-08, 0.01, 10

LANES = 128
SUBLANES = 8
VMEM_LIMIT = 56 * 2**20

BIG = ("w_in", "w_branch_a", "w_uq", "w_ukv", "w_branch_b", "w_out", "w_up", "w_down")
WEIGHTS = ("w_ada", "b_ada", "pre_norm1_g", "w_in", "gm_ln_g", "gm_ln_b", "gm_w_s", "gm_b_s", "w_branch_a",
           "q_norm_g", "w_uq", "kv_norm_g", "w_ukv", "w_branch_b", "w_out", "post_norm1_g", "pre_norm2_g",
           "w_up", "conv_w", "conv_b", "w_down", "post_norm2_g")
SMALL_PARTIAL = ("pre_norm1_g", "gm_ln_g", "gm_ln_b", "gm_w_s", "gm_b_s", "q_norm_g", "kv_norm_g", "post_norm1_g",
                 "pre_norm2_g", "conv_w", "conv_b", "post_norm2_g")
SMALL = ("b_ada",) + SMALL_PARTIAL


def _div_tile(n, cap, mult=LANES):
    t = (min(cap, n) // mult) * mult
    while t >= mult:
        if n % t == 0:
            return t
        t -= mult
    return n


def _params(**kw):
    return pltpu.CompilerParams(vmem_limit_bytes=VMEM_LIMIT, **kw)


def _row_spec(width):
    return pl.BlockSpec((1, width), lambda *_: (0, 0))


def _gelu(x):
    k = 0.7978845608028654
    return 0.5 * x * (1.0 + jnp.tanh(k * (x + 0.044715 * x * x * x)))


def _gelu_grad(x):
    k = 0.7978845608028654
    t = jnp.tanh(k * (x + 0.044715 * x * x * x))
    return 0.5 * (1.0 + t) + 0.5 * x * (1.0 - t * t) * k * (1.0 + 3.0 * 0.044715 * x * x)


def _sigmoid(x):
    return 1.0 / (1.0 + jnp.exp(-x))


def _dot(a, b, dims):
    return lax.dot_general(a, b, (dims, ((), ())), preferred_element_type=F32)


NN = ((1,), (0,))
NT = ((1,), (1,))
TN = ((0,), (0,))


def _logical(arr):
    if arr.ndim == 2:
        return arr.shape[0], arr.shape[1], arr.shape[1]
    return arr.shape[1], arr.shape[0] * arr.shape[2], arr.shape[2]


def _tile_spec(ndim, group_w, blk_rows, blk_cols, row_of, col_of):
    if ndim == 2:
        return pl.BlockSpec((blk_rows, blk_cols), lambda i, j, k: (row_of(i, j, k), col_of(i, j, k)))
    per = group_w // blk_cols
    return pl.BlockSpec((None, blk_rows, blk_cols),
                        lambda i, j, k: (col_of(i, j, k) // per, row_of(i, j, k), col_of(i, j, k) % per))


def _matmul(a, b, *, mode, out_dtype, name, tm=512, tn=512, tk=2048, mul=None, add=None, out_groups=None, comm=None):
    ar, ac, agw = _logical(a)
    br, bc, bgw = _logical(b)
    if mode == "nn":
        m, kd, n = ar, ac, bc
        m_w, k_w, n_w = (), (agw,), (bgw,)
    elif mode == "nt":
        m, kd, n = ar, ac, br
        m_w, k_w, n_w = (), (agw, bgw), ()
    else:
        m, kd, n = ac, ar, bc
        m_w, k_w, n_w = (agw,), (), (bgw,)
    if out_groups is not None:
        n_w = n_w + (n // out_groups,)
    tm = _div_tile(min((m,) + m_w), tm, SUBLANES)
    tn = _div_tile(min((n,) + n_w), tn)
    tk = _div_tile(min((kd,) + k_w), tk)
    assert all(w % tn == 0 for w in n_w) and all(w % tk == 0 for w in k_w) and all(w % tm == 0 for w in m_w)
    nk = kd // tk
    dims = {"nn": NN, "nt": NT, "tn": TN}[mode]
    gi, gj, gk = (lambda i, j, k: i), (lambda i, j, k: j), (lambda i, j, k: k)
    if mode == "nn":
        a_spec = _tile_spec(a.ndim, agw, tm, tk, gi, gk)
        b_spec = _tile_spec(b.ndim, bgw, tk, tn, gk, gj)
    elif mode == "nt":
        a_spec = _tile_spec(a.ndim, agw, tm, tk, gi, gk)
        b_spec = _tile_spec(b.ndim, bgw, tn, tk, gj, gk)
    else:
        a_spec = _tile_spec(a.ndim, agw, tk, tm, gk, gi)
        b_spec = _tile_spec(b.ndim, bgw, tk, tn, gk, gj)
    in_specs, operands = [a_spec, b_spec], [a, b]
    if mul is not None:
        assert mul.shape == (m, tn)
        in_specs.append(pl.BlockSpec((tm, tn), lambda i, j, k: (i, 0)))
        operands.append(mul)
    if add is not None:
        in_specs.append(pl.BlockSpec((tm, tn), lambda i, j, k: (i, j)))
        operands.append(add)

    def body(*refs):
        a_ref, b_ref = refs[0], refs[1]
        pos = 2
        mul_ref = add_ref = None
        if mul is not None:
            mul_ref, pos = refs[pos], pos + 1
        if add is not None:
            add_ref, pos = refs[pos], pos + 1
        o_ref = refs[pos]

        def finish(r):
            if mul_ref is not None:
                r = r * mul_ref[...]
            if add_ref is not None:
                r = r + add_ref[...]
            o_ref[...] = r.astype(out_dtype)

        part = _dot(a_ref[...], b_ref[...], dims)
        if nk == 1:
            finish(part)
        else:
            acc_ref = refs[pos + 1]
            k = pl.program_id(2)

            @pl.when(k == 0)
            def _():
                acc_ref[...] = part

            @pl.when(k > 0)
            def _():
                acc_ref[...] += part

            @pl.when(k == nk - 1)
            def _():
                finish(acc_ref[...])

    if out_groups is None:
        out_spec, out_dims = _tile_spec(2, n, tm, tn, gi, gj), (m, n)
    else:
        out_spec, out_dims = _tile_spec(3, n // out_groups, tm, tn, gi, gj), (out_groups, m, n // out_groups)
    return _call(body, operands, comm, name=name, grid=(m // tm, n // tn, nk), in_specs=in_specs, out_specs=out_spec,
                 out_shape=jax.ShapeDtypeStruct(out_dims, out_dtype),
                 scratch_shapes=[] if nk == 1 else [pltpu.VMEM((tm, tn), F32)])


def _accumulate(ref, value):
    @pl.when(pl.program_id(0) == 0)
    def _():
        ref[...] = value

    @pl.when(pl.program_id(0) > 0)
    def _():
        ref[...] += value


def _colsum(v):
    return jnp.sum(v, axis=0, keepdims=True)


def _rowmean(v):
    return jnp.mean(v, axis=-1, keepdims=True)


def _prenorm(x, g, scale, shift, name):
    s, d = x.shape
    tb = _div_tile(s, 256, SUBLANES)

    def body(x_ref, g_ref, sc_ref, sh_ref, h_ref):
        xv = x_ref[...]
        r = lax.rsqrt(_rowmean(xv * xv) + EPS)
        h_ref[...] = ((xv * r) * g_ref[...] * (1.0 + sc_ref[...]) + sh_ref[...]).astype(BF16)

    blk = pl.BlockSpec((tb, d), lambda i: (i, 0))
    return pl.pallas_call(
        body, name=name, grid=(s // tb,), in_specs=[blk, _row_spec(d), _row_spec(d), _row_spec(d)],
        out_specs=blk, out_shape=jax.ShapeDtypeStruct((s, d), BF16), compiler_params=_params(),
    )(x, g, scale, shift)


def _post_pre(x, y, gate, pg, g2, scale2, shift2, name):
    s, d = x.shape
    tb = _div_tile(s, 256, SUBLANES)

    def body(x_ref, y_ref, gate_ref, pg_ref, g2_ref, sc_ref, sh_ref, x1_ref, h2_ref):
        yv = y_ref[...]
        rp = lax.rsqrt(_rowmean(yv * yv) + EPS)
        x1 = x_ref[...] + gate_ref[...] * ((yv * rp) * pg_ref[...])
        x1_ref[...] = x1
        r2 = lax.rsqrt(_rowmean(x1 * x1) + EPS)
        h2_ref[...] = ((x1 * r2) * g2_ref[...] * (1.0 + sc_ref[...]) + sh_ref[...]).astype(BF16)

    blk = pl.BlockSpec((tb, d), lambda i: (i, 0))
    return pl.pallas_call(
        body, name=name, grid=(s // tb,), in_specs=[blk, blk] + [_row_spec(d)] * 5,
        out_specs=[blk, blk],
        out_shape=[jax.ShapeDtypeStruct((s, d), F32), jax.ShapeDtypeStruct((s, d), BF16)],
        compiler_params=_params(),
    )(x, y, gate, pg, g2, scale2, shift2)


def _post_bwd(y, gate, pg, name, *, dxo=None, xin=None, target=None):
    s, d = y.shape
    tb = _div_tile(s, 256, SUBLANES)
    from_loss = target is not None

    def body(*refs):
        if from_loss:
            y_ref, gate_ref, pg_ref, xin_ref, t_ref, dy_ref, dgate_ref, dpg_ref, dxo_ref, loss_ref = refs
        else:
            y_ref, gate_ref, pg_ref, dxo_in_ref, dy_ref, dgate_ref, dpg_ref = refs
        yv = y_ref[...]
        rp = lax.rsqrt(_rowmean(yv * yv) + EPS)
        yh = yv * rp
        fn = yh * pg_ref[...]
        gate = gate_ref[...]
        if from_loss:
            err = xin_ref[...] + gate * fn - t_ref[...]
            dxo = err * (1.0 / d)
            dxo_ref[...] = dxo
            part = 0.5 * jnp.sum(_rowmean(err * err), axis=0, keepdims=True)
            _accumulate(loss_ref, jnp.broadcast_to(part, loss_ref.shape))
        else:
            dxo = dxo_in_ref[...]
        _accumulate(dgate_ref, _colsum(dxo * fn))
        dfn = dxo * gate
        _accumulate(dpg_ref, _colsum(dfn * yh))
        dyh = dfn * pg_ref[...]
        dy_ref[...] = (rp * (dyh - yh * _rowmean(dyh * yh))).astype(BF16)

    blk = pl.BlockSpec((tb, d), lambda i: (i, 0))
    in_specs = [blk, _row_spec(d), _row_spec(d)]
    out_specs = [blk, _row_spec(d), _row_spec(d)]
    out_shape = [jax.ShapeDtypeStruct((s, d), BF16), jax.ShapeDtypeStruct((1, d), F32),
                 jax.ShapeDtypeStruct((1, d), F32)]
    if from_loss:
        operands = (y, gate, pg, xin, target)
        in_specs += [blk, blk]
        out_specs += [blk, _row_spec(LANES)]
        out_shape += [jax.ShapeDtypeStruct((s, d), F32), jax.ShapeDtypeStruct((1, LANES), F32)]
    else:
        operands = (y, gate, pg, dxo)
        in_specs += [blk]
    return pl.pallas_call(
        body, name=name, grid=(s // tb,), in_specs=in_specs, out_specs=out_specs, out_shape=out_shape,
        compiler_params=_params(),
    )(*operands)


def _prenorm_bwd(xin, dh, dres, g, scale, name, comm=None):
    s, d = xin.shape
    tb = _div_tile(s, 256, SUBLANES)

    def body(x_ref, dh_ref, dres_ref, g_ref, sc_ref, dx_ref, dshift_ref, dscale_ref, dg_ref):
        xv = x_ref[...]
        r = lax.rsqrt(_rowmean(xv * xv) + EPS)
        xn = xv * r
        dh = dh_ref[...]
        g1 = g_ref[...]
        s1 = 1.0 + sc_ref[...]
        _accumulate(dshift_ref, _colsum(dh))
        _accumulate(dscale_ref, _colsum(dh * xn * g1))
        _accumulate(dg_ref, _colsum(dh * xn * s1))
        dxn = dh * g1 * s1
        dx_ref[...] = dres_ref[...] + r * (dxn - xn * _rowmean(dxn * xn))

    blk = pl.BlockSpec((tb, d), lambda i: (i, 0))
    return _call(
        body, (xin, dh, dres, g, scale), comm, name=name, grid=(s // tb,),
        in_specs=[blk, blk, blk, _row_spec(d), _row_spec(d)],
        out_specs=[blk, _row_spec(d), _row_spec(d), _row_spec(d)],
        out_shape=[jax.ShapeDtypeStruct((s, d), F32)] + [jax.ShapeDtypeStruct((1, d), F32)] * 3)


def _merge(z_big, y_a, y_b, name):
    s, d = y_a.shape
    tb = _div_tile(s, 256, SUBLANES)

    def body(zg_ref, ya_ref, yb_ref, o_ref):
        o_ref[...] = (_sigmoid(zg_ref[:, :d]) * ya_ref[...] + _sigmoid(zg_ref[:, d:]) * yb_ref[...]).astype(BF16)

    blk = pl.BlockSpec((tb, d), lambda i: (i, 0))
    return pl.pallas_call(
        body, name=name, grid=(s // tb,), in_specs=[pl.BlockSpec((tb, 2 * d), lambda i: (i, 1)), blk, blk],
        out_specs=blk, out_shape=jax.ShapeDtypeStruct((s, d), BF16), compiler_params=_params(),
    )(z_big, y_a, y_b)


def _merge_bwd(dmerged, z_big, y_a, y_b, name):
    s, d = y_a.shape
    tb = _div_tile(s, 256, SUBLANES)

    def body(dm_ref, zg_ref, ya_ref, yb_ref, dya_ref, dyb_ref, dz_ref):
        dm = dm_ref[...]
        sa, sb = _sigmoid(zg_ref[:, :d]), _sigmoid(zg_ref[:, d:])
        dya_ref[...] = (dm * sa).astype(BF16)
        dyb_ref[...] = (dm * sb).astype(BF16)
        dz_ref[:, :d] = (dm * ya_ref[...] * sa * (1.0 - sa)).astype(BF16)
        dz_ref[:, d:] = (dm * yb_ref[...] * sb * (1.0 - sb)).astype(BF16)

    blk = pl.BlockSpec((tb, d), lambda i: (i, 0))
    wide = pl.BlockSpec((tb, 2 * d), lambda i: (i, 1))
    return pl.pallas_call(
        body, name=name, grid=(s // tb,), in_specs=[blk, wide, blk, blk], out_specs=[blk, blk, wide],
        out_shape=[jax.ShapeDtypeStruct((s, d), BF16), jax.ShapeDtypeStruct((s, d), BF16),
                   jax.ShapeDtypeStruct((s, 4 * d), BF16)],
        compiler_params=_params(),
    )(dmerged, z_big, y_a, y_b)


def _causal_mask(ch):
    q = lax.broadcasted_iota(jnp.int32, (ch, ch), 0)
    p = lax.broadcasted_iota(jnp.int32, (ch, ch), 1)
    return (p <= q).astype(F32)


def _gmlp_norm(zc, lng, lnb, gw):
    u_pre, v_pre = zc[:, :gw], zc[:, gw:]
    vg = _gelu(v_pre)
    mu = _rowmean(vg)
    cen = vg - mu
    rstd = lax.rsqrt(_rowmean(cen * cen) + EPS)
    vhat = cen * rstd
    return u_pre, v_pre, _gelu(u_pre), vhat, rstd, vhat * lng + lnb


def _gmlp_fwd(z_big, ln_g, ln_b, w_s, b_s_t, name):
    s = z_big.shape[0]
    groups, ch, _ = w_s.shape
    gw = ln_g.shape[1]
    gd = gw // groups

    def body(z_ref, lng_ref, lnb_ref, ws_ref, bt_ref, a_ref):
        _, _, u, _, _, vn = _gmlp_norm(z_ref[...], lng_ref[...], lnb_ref[...], gw)
        mask = _causal_mask(ch)
        for g in range(groups):
            cols = slice(g * gd, (g + 1) * gd)
            wm = (ws_ref[g] * mask).astype(BF16)
            mixed = _dot(wm, vn[:, cols].astype(BF16), NN) + bt_ref[:, g:g + 1]
            a_ref[:, cols] = (u[:, cols] * mixed).astype(BF16)

    return pl.pallas_call(
        body, name=name, grid=(s // ch,),
        in_specs=[pl.BlockSpec((ch, 2 * gw), lambda n: (n, 0)), _row_spec(gw), _row_spec(gw),
                  pl.BlockSpec((groups, ch, ch), lambda n: (0, 0, 0)), pl.BlockSpec((ch, groups), lambda n: (0, 0))],
        out_specs=pl.BlockSpec((ch, gw), lambda n: (n, 0)),
        out_shape=jax.ShapeDtypeStruct((s, gw), BF16), compiler_params=_params(),
    )(z_big, ln_g, ln_b, w_s, b_s_t)


def _gmlp_bwd(z_big, da, dz_big, ln_g, ln_b, w_s, b_s_t, name, comm=None):
    s = z_big.shape[0]
    groups, ch, _ = w_s.shape
    gw = ln_g.shape[1]
    gd = gw // groups

    def body(z_ref, da_ref, dzin_ref, lng_ref, lnb_ref, ws_ref, bt_ref, dz_ref, gws_ref, gbt_ref, glng_ref, glnb_ref):
        del dzin_ref
        lng = lng_ref[...]
        u_pre, v_pre, u, vhat, rstd, vn = _gmlp_norm(z_ref[...], lng, lnb_ref[...], gw)
        da = da_ref[...]
        mask = _causal_mask(ch)
        first = pl.program_id(0) == 0
        dvn_parts = []
        lane = lax.broadcasted_iota(jnp.int32, (ch, LANES), 1)
        gb = jnp.zeros((ch, LANES), F32)
        for g in range(groups):
            cols = slice(g * gd, (g + 1) * gd)
            wm = (ws_ref[g] * mask).astype(BF16)
            vn_g = vn[:, cols].astype(BF16)
            mixed = _dot(wm, vn_g, NN) + bt_ref[:, g:g + 1]
            dz_ref[:, cols] = (da[:, cols] * mixed * _gelu_grad(u_pre[:, cols])).astype(BF16)
            dmixed = da[:, cols] * u[:, cols]
            dm16 = dmixed.astype(BF16)
            dvn_parts.append(_dot(wm, dm16, TN))
            gws = _dot(dm16, vn_g, NT) * mask

            @pl.when(first)
            def _(g=g, gws=gws):
                gws_ref[g] = gws

            @pl.when(jnp.logical_not(first))
            def _(g=g, gws=gws):
                gws_ref[g] += gws

            gb = gb + jnp.where(lane == g, jnp.sum(dmixed, axis=1, keepdims=True), 0.0)
        _accumulate(gbt_ref, gb)
        dvn = jnp.concatenate(dvn_parts, axis=1)
        _accumulate(glnb_ref, _colsum(dvn))
        _accumulate(glng_ref, _colsum(dvn * vhat))
        dvh = dvn * lng
        dvg = rstd * (dvh - _rowmean(dvh) - vhat * _rowmean(dvh * vhat))
        dz_ref[:, gw:] = (dvg * _gelu_grad(v_pre)).astype(BF16)

    zspec = pl.BlockSpec((ch, 2 * gw), lambda n: (n, 0))
    return _call(
        body, (z_big, da, dz_big, ln_g, ln_b, w_s, b_s_t), comm, name=name, grid=(s // ch,),
        in_specs=[zspec, pl.BlockSpec((ch, gw), lambda n: (n, 0)), pl.BlockSpec(memory_space=HBM),
                  _row_spec(gw), _row_spec(gw), pl.BlockSpec((groups, ch, ch), lambda n: (0, 0, 0)),
                  pl.BlockSpec((ch, groups), lambda n: (0, 0))],
        out_specs=[zspec, pl.BlockSpec((groups, ch, ch), lambda n: (0, 0, 0)),
                   pl.BlockSpec((ch, LANES), lambda n: (0, 0)), _row_spec(gw), _row_spec(gw)],
        out_shape=[jax.ShapeDtypeStruct(dz_big.shape, BF16), jax.ShapeDtypeStruct((groups, ch, ch), F32),
                   jax.ShapeDtypeStruct((ch, LANES), F32), jax.ShapeDtypeStruct((1, gw), F32),
                   jax.ShapeDtypeStruct((1, gw), F32)],
        input_output_aliases={2: 0})


def _mla_prep(z_lat, q_g, kv_g, rope_k, name):
    s, latw = z_lat.shape
    ql, kvl = q_g.shape[1], kv_g.shape[1]
    tb = _div_tile(s, 256, SUBLANES)

    def body(z_ref, qg_ref, kvg_ref, t_ref, qn_ref, kvn_ref, kr_ref):
        q = z_ref[:, :ql]
        qn_ref[...] = ((q * lax.rsqrt(_rowmean(q * q) + EPS)) * qg_ref[...]).astype(BF16)
        kv = z_ref[:, ql:ql + kvl]
        kvn_ref[...] = ((kv * lax.rsqrt(_rowmean(kv * kv) + EPS)) * kvg_ref[...]).astype(BF16)
        kk = z_ref[:, ql + kvl:] * t_ref[...]
        kr_ref[...] = (kk + pltpu.roll(kk, ROPE, axis=1)).astype(BF16)

    return pl.pallas_call(
        body, name=name, grid=(s // tb,),
        in_specs=[pl.BlockSpec((tb, latw), lambda i: (i, 0)), _row_spec(ql), _row_spec(kvl),
                  pl.BlockSpec((tb, 2 * ROPE), lambda i: (i, 0))],
        out_specs=[pl.BlockSpec((tb, ql), lambda i: (i, 0)), pl.BlockSpec((tb, kvl), lambda i: (i, 0)),
                   pl.BlockSpec((tb, 2 * ROPE), lambda i: (i, 0))],
        out_shape=[jax.ShapeDtypeStruct((s, ql), BF16), jax.ShapeDtypeStruct((s, kvl), BF16),
                   jax.ShapeDtypeStruct((s, 2 * ROPE), BF16)],
        compiler_params=_params(),
    )(z_lat, q_g, kv_g, rope_k)


def _scores(q, k, kr, on_diagonal):
    s = _dot(q[:, :NOPE], k, NT) + _dot(q[:, NOPE:], kr, NT)
    if not on_diagonal:
        return s
    rows = lax.broadcasted_iota(jnp.int32, s.shape, 0)
    cols = lax.broadcasted_iota(jnp.int32, s.shape, 1)
    return jnp.where(cols <= rows, s, -1e30)


def _attn_fwd(q, kv, kr, heads, name, comm=None):
    s = q.shape[0]
    t = _div_tile(s, 512)
    nb = s // t
    hp = 2 if heads % 2 == 0 else 1

    def body(q_ref, k_ref, kr_ref, v_ref, o_ref, lse_ref, m_ref, l_ref, acc_ref):
        i, j = pl.program_id(1), pl.program_id(2)

        @pl.when(j == 0)
        def _():
            m_ref[...] = jnp.full(m_ref.shape, -1e30, F32)
            l_ref[...] = jnp.zeros(l_ref.shape, F32)
            acc_ref[...] = jnp.zeros(acc_ref.shape, F32)

        def step(on_diagonal):
            krv = kr_ref[...]
            for h in range(hp):
                vc = slice(h * VHEAD, (h + 1) * VHEAD)
                sc = _scores(q_ref[:, h * HEAD_W:(h + 1) * HEAD_W], k_ref[:, h * NOPE:(h + 1) * NOPE], krv, on_diagonal)
                m_old = m_ref[h]
                m_new = jnp.maximum(m_old, jnp.max(sc, axis=-1, keepdims=True))
                p = jnp.exp(sc - m_new)
                alpha = jnp.exp(m_old - m_new)
                l_new = alpha * l_ref[h] + jnp.sum(p, axis=-1, keepdims=True)
                acc = alpha * acc_ref[:, vc] + _dot(p.astype(BF16), v_ref[:, vc], NN)
                if on_diagonal:
                    o_ref[:, vc] = (acc / l_new).astype(BF16)
                    lse_ref[h] = jnp.broadcast_to(m_new + jnp.log(l_new), (t, LANES))
                else:
                    m_ref[h], l_ref[h], acc_ref[:, vc] = m_new, l_new, acc

        pl.when(j < i)(lambda: step(False))
        pl.when(j == i)(lambda: step(True))

    kidx = lambda off: (lambda h, i, j: (jnp.minimum(i, j), off(h)))
    return _call(
        body, (q, kv, kr, kv), comm, name=name, grid=(heads // hp, nb, nb),
        in_specs=[pl.BlockSpec((t, hp * HEAD_W), lambda h, i, j: (i, h)),
                  pl.BlockSpec((t, hp * NOPE), kidx(lambda h: h)),
                  pl.BlockSpec((t, 2 * ROPE), kidx(lambda h: 0)),
                  pl.BlockSpec((t, hp * VHEAD), kidx(lambda h: heads // hp + h))],
        out_specs=[pl.BlockSpec((t, hp * VHEAD), lambda h, i, j: (i, h)),
                   pl.BlockSpec((hp, t, LANES), lambda h, i, j: (h, i, 0))],
        out_shape=[jax.ShapeDtypeStruct((s, heads * VHEAD), BF16), jax.ShapeDtypeStruct((heads, s, LANES), F32)],
        scratch_shapes=[pltpu.VMEM((hp, t, 1), F32), pltpu.VMEM((hp, t, 1), F32), pltpu.VMEM((t, hp * VHEAD), F32)])


def _attn_bwd(q, kv, kr, o, do, lse, heads, name, comm=None):
    s = q.shape[0]
    t = _div_tile(s, 512)
    nb = s // t
    hp = 2 if heads % 2 == 0 else 1

    def body(q_ref, k_ref, kr_ref, v_ref, o_ref, do_ref, lse_ref, dq_ref, dk_ref, dv_ref, dk_acc, dv_acc):
        j, i = pl.program_id(1), pl.program_id(2)

        @pl.when(jnp.logical_and(j == 0, i == 0))
        def _():
            dq_ref[...] = jnp.zeros(dq_ref.shape, F32)

        def step(on_diagonal):
            krv = kr_ref[...]
            rows = pl.ds(pl.multiple_of(i * t, t), t)
            for h in range(hp):
                qc, kc, vc = (slice(h * w, (h + 1) * w) for w in (HEAD_W, NOPE, VHEAD))
                qv, kn, do_v = q_ref[:, qc], k_ref[:, kc], do_ref[:, vc]
                p = jnp.exp(_scores(qv, kn, krv, on_diagonal) - lse_ref[h][:, :1])
                dp = _dot(do_v, v_ref[:, vc], NT)
                delta = jnp.sum(do_v.astype(F32) * o_ref[:, vc].astype(F32), axis=-1, keepdims=True)
                ds = (p * (dp - delta)).astype(BF16)
                dq_ref[rows, h * HEAD_W:h * HEAD_W + NOPE] += _dot(ds, kn, NN)
                dq_ref[rows, h * HEAD_W + NOPE:(h + 1) * HEAD_W] += _dot(ds, krv, NN)
                dv_part, dk_part = _dot(p.astype(BF16), do_v, TN), _dot(ds, qv, TN)
                if on_diagonal:
                    dv_acc[:, vc], dk_acc[:, qc] = dv_part, dk_part
                else:
                    dv_acc[:, vc] += dv_part
                    dk_acc[:, qc] += dk_part

        pl.when(i == j)(lambda: step(True))
        pl.when(i > j)(lambda: step(False))

        @pl.when(i == nb - 1)
        def _():
            dk_ref[...] = dk_acc[...].astype(BF16)
            dv_ref[...] = dv_acc[...].astype(BF16)

    qidx = lambda h, j, i: (jnp.maximum(i, j), h)
    return _call(
        body, (q, kv, kr, kv, o, do, lse), comm, name=name, grid=(heads // hp, nb, nb),
        in_specs=[pl.BlockSpec((t, hp * HEAD_W), qidx),
                  pl.BlockSpec((t, hp * NOPE), lambda h, j, i: (j, h)),
                  pl.BlockSpec((t, 2 * ROPE), lambda h, j, i: (j, 0)),
                  pl.BlockSpec((t, hp * VHEAD), lambda h, j, i: (j, heads // hp + h)),
                  pl.BlockSpec((t, hp * VHEAD), qidx), pl.BlockSpec((t, hp * VHEAD), qidx),
                  pl.BlockSpec((hp, t, LANES), lambda h, j, i: (h, jnp.maximum(i, j), 0))],
        out_specs=[pl.BlockSpec((s, hp * HEAD_W), lambda h, j, i: (0, h)),
                   pl.BlockSpec((t, hp * HEAD_W), lambda h, j, i: (j, h)),
                   pl.BlockSpec((t, hp * VHEAD), lambda h, j, i: (j, h))],
        out_shape=[jax.ShapeDtypeStruct((s, heads * HEAD_W), F32), jax.ShapeDtypeStruct((s, heads * HEAD_W), BF16),
                   jax.ShapeDtypeStruct((s, heads * VHEAD), BF16)],
        scratch_shapes=[pltpu.VMEM((t, hp * HEAD_W), F32), pltpu.VMEM((t, hp * VHEAD), F32)])


def _mla_bwd_mid(dq, dk, dv, rope_q, rope_k, heads, name):
    s = dq.shape[0]
    tb = _div_tile(s, 256, SUBLANES)

    def body(dq_ref, dk_ref, dv_ref, tq_ref, tk_ref, dqb_ref, dkv_ref, dkk_ref):
        tq = tq_ref[...]
        dkr = jnp.zeros((tb, 2 * ROPE), F32)
        for h in range(heads):
            cols = slice(h * HEAD_W, (h + 1) * HEAD_W)
            dqb_ref[:, cols] = (dq_ref[:, cols] * tq).astype(BF16)
            dkv_ref[:, h * NOPE:(h + 1) * NOPE] = dk_ref[:, h * HEAD_W:h * HEAD_W + NOPE]
            dkr = dkr + dk_ref[:, h * HEAD_W + NOPE:(h + 1) * HEAD_W].astype(F32)
        dkv_ref[:, heads * NOPE:] = dv_ref[...]
        dkk_ref[...] = (dkr + pltpu.roll(dkr, ROPE, axis=1)) * tk_ref[...]

    wq, wv = heads * HEAD_W, heads * VHEAD
    return pl.pallas_call(
        body, name=name, grid=(s // tb,),
        in_specs=[pl.BlockSpec((tb, wq), lambda i: (i, 0)), pl.BlockSpec((tb, wq), lambda i: (i, 0)),
                  pl.BlockSpec((tb, wv), lambda i: (i, 0)), pl.BlockSpec((tb, HEAD_W), lambda i: (i, 0)),
                  pl.BlockSpec((tb, 2 * ROPE), lambda i: (i, 0))],
        out_specs=[pl.BlockSpec((tb, wq), lambda i: (i, 0)), pl.BlockSpec((tb, heads * NOPE + wv), lambda i: (i, 0)),
                   pl.BlockSpec((tb, 2 * ROPE), lambda i: (i, 0))],
        out_shape=[jax.ShapeDtypeStruct((s, wq), BF16), jax.ShapeDtypeStruct((s, heads * NOPE + wv), BF16),
                   jax.ShapeDtypeStruct((s, 2 * ROPE), F32)],
        compiler_params=_params(),
    )(dq, dk, dv, rope_q, rope_k)


def _mla_bwd_post(z_lat, dqn, dkvn, dkk, q_g, kv_g, name):
    s, latw = z_lat.shape
    ql, kvl = q_g.shape[1], kv_g.shape[1]
    tb = _div_tile(s, 256, SUBLANES)

    def norm_bwd(xv, dn, g, dg_ref):
        r = lax.rsqrt(_rowmean(xv * xv) + EPS)
        xh = xv * r
        _accumulate(dg_ref, _colsum(dn * xh))
        dxh = dn * g
        return r * (dxh - xh * _rowmean(dxh * xh))

    def body(z_ref, dqn_ref, dkvn_ref, dkk_ref, qg_ref, kvg_ref, dz_ref, gq_ref, gkv_ref):
        dz_ref[:, :ql] = norm_bwd(z_ref[:, :ql], dqn_ref[...], qg_ref[...], gq_ref).astype(BF16)
        dz_ref[:, ql:ql + kvl] = norm_bwd(z_ref[:, ql:ql + kvl], dkvn_ref[...], kvg_ref[...], gkv_ref).astype(BF16)
        dz_ref[:, ql + kvl:] = dkk_ref[...].astype(BF16)

    return pl.pallas_call(
        body, name=name, grid=(s // tb,),
        in_specs=[pl.BlockSpec((tb, latw), lambda i: (i, 0)), pl.BlockSpec((tb, ql), lambda i: (i, 0)),
                  pl.BlockSpec((tb, kvl), lambda i: (i, 0)), pl.BlockSpec((tb, 2 * ROPE), lambda i: (i, 0)),
                  _row_spec(ql), _row_spec(kvl)],
        out_specs=[pl.BlockSpec((tb, latw), lambda i: (i, 0)), _row_spec(ql), _row_spec(kvl)],
        out_shape=[jax.ShapeDtypeStruct((s, latw), BF16), jax.ShapeDtypeStruct((1, ql), F32),
                   jax.ShapeDtypeStruct((1, kvl), F32)],
        compiler_params=_params(),
    )(z_lat, dqn, dkvn, dkk, q_g, kv_g)


def _shift_down(x, n):
    rows = lax.broadcasted_iota(jnp.int32, x.shape, 0)
    return jnp.where(rows >= n, pltpu.roll(x, n, axis=0), 0.0)


def _shift_up(x, n):
    s = x.shape[0]
    rows = lax.broadcasted_iota(jnp.int32, x.shape, 0)
    return jnp.where(rows < s - n, pltpu.roll(x, s - n, axis=0), 0.0)


def _conv(pre, w_ref, b_ref):
    return (w_ref[2:3, :] * pre + w_ref[1:2, :] * _shift_down(pre, 1) + w_ref[0:1, :] * _shift_down(pre, 2)
            + b_ref[...])


def _conv_fwd(up_pre, conv_w, conv_b, name):
    s, ff2 = up_pre.shape
    ff = ff2 // 2
    tc = _div_tile(ff, 256)
    nb = ff // tc

    def body(pg_ref, pv_ref, wg_ref, wv_ref, bg_ref, bv_ref, act_ref):
        gate = _conv(pg_ref[...].astype(F32), wg_ref, bg_ref)
        val = _conv(pv_ref[...].astype(F32), wv_ref, bv_ref)
        act_ref[...] = (gate * _sigmoid(gate) * val).astype(BF16)

    def col(rows, off):
        return pl.BlockSpec((rows, tc), lambda j: (0, j + off))

    return pl.pallas_call(
        body, name=name, grid=(nb,),
        in_specs=[col(s, 0), col(s, nb), col(CONV_TAPS, 0), col(CONV_TAPS, nb), col(1, 0), col(1, nb)],
        out_specs=col(s, 0), out_shape=jax.ShapeDtypeStruct((s, ff), BF16), compiler_params=_params(),
    )(up_pre, up_pre, conv_w, conv_w, conv_b, conv_b)


def _conv_bwd(up_pre, dact, conv_w, conv_b, name, comm=None):
    s, ff2 = up_pre.shape
    ff = ff2 // 2
    tc = _div_tile(ff, 256)
    nb = ff // tc

    def half(pre, dx, w_ref, dpre_ref, gw_ref, gb_ref):
        gb_ref[...] = _colsum(dx)
        gw_ref[0:1, :] = _colsum(dx * _shift_down(pre, 2))
        gw_ref[1:2, :] = _colsum(dx * _shift_down(pre, 1))
        gw_ref[2:3, :] = _colsum(dx * pre)
        dpre_ref[...] = (w_ref[2:3, :] * dx + w_ref[1:2, :] * _shift_up(dx, 1)
                         + w_ref[0:1, :] * _shift_up(dx, 2)).astype(BF16)

    def body(pg_ref, pv_ref, da_ref, wg_ref, wv_ref, bg_ref, bv_ref, dup_ref, gwg_ref, gwv_ref, gbg_ref, gbv_ref):
        pre_g, pre_v = pg_ref[...].astype(F32), pv_ref[...].astype(F32)
        gate = _conv(pre_g, wg_ref, bg_ref)
        val = _conv(pre_v, wv_ref, bv_ref)
        da = da_ref[...].astype(F32)
        sg = _sigmoid(gate)
        half(pre_v, da * gate * sg, wv_ref, dup_ref.at[1], gwv_ref, gbv_ref)
        half(pre_g, da * val * sg * (1.0 + gate * (1.0 - sg)), wg_ref, dup_ref.at[0], gwg_ref, gbg_ref)

    def col(rows, off):
        return pl.BlockSpec((rows, tc), lambda j: (0, j + off))

    return _call(
        body, (up_pre, up_pre, dact, conv_w, conv_w, conv_b, conv_b), comm, name=name, grid=(nb,),
        in_specs=[col(s, 0), col(s, nb), col(s, 0), col(CONV_TAPS, 0), col(CONV_TAPS, nb), col(1, 0), col(1, nb)],
        out_specs=[pl.BlockSpec((2, s, tc), lambda j: (0, 0, j)), col(CONV_TAPS, 0), col(CONV_TAPS, 0),
                   col(1, 0), col(1, 0)],
        out_shape=[jax.ShapeDtypeStruct((2, s, ff), BF16)] + [jax.ShapeDtypeStruct((CONV_TAPS, ff), F32)] * 2
        + [jax.ShapeDtypeStruct((1, ff), F32)] * 2)


def _ada_fwd(c_all, w, b, name):
    nseq, d = c_all.shape
    na = w.shape[1]
    tn = _div_tile(na, 512)

    def body(c_ref, w_ref, b_ref, o_ref):
        cv = c_ref[...]
        sc = cv * _sigmoid(cv)
        o_ref[...] = jnp.dot(sc, w_ref[...], preferred_element_type=F32, precision=lax.Precision.HIGHEST) + b_ref[...]

    return pl.pallas_call(
        body, name=name, grid=(na // tn,),
        in_specs=[pl.BlockSpec((nseq, d), lambda j: (0, 0)), pl.BlockSpec((d, tn), lambda j: (0, j)),
                  pl.BlockSpec((1, tn), lambda j: (0, j))],
        out_specs=pl.BlockSpec((nseq, tn), lambda j: (0, j)),
        out_shape=jax.ShapeDtypeStruct((nseq, na), F32), compiler_params=_params(),
    )(c_all, w, b)


def _ada_bwd(c_all_t, dmod, name):
    d, nseq = c_all_t.shape
    na = dmod.shape[1]
    tm, tn = _div_tile(d, 256, SUBLANES), _div_tile(na, 512)

    def body(c_ref, dm_ref, o_ref):
        cv = c_ref[...]
        sc = cv * _sigmoid(cv)
        acc = sc[:, 0:1] * dm_ref[0:1, :]
        for bi in range(1, nseq):
            acc = acc + sc[:, bi:bi + 1] * dm_ref[bi:bi + 1, :]
        o_ref[...] = acc

    return pl.pallas_call(
        body, name=name, grid=(d // tm, na // tn),
        in_specs=[pl.BlockSpec((tm, nseq), lambda i, j: (i, 0)), pl.BlockSpec((nseq, tn), lambda i, j: (0, j))],
        out_specs=pl.BlockSpec((tm, tn), lambda i, j: (i, j)),
        out_shape=jax.ShapeDtypeStruct((d, na), F32), compiler_params=_params(),
    )(c_all_t, dmod)


def _adamw(w, g, m, v, name, comm=None):
    rows, cols = w.shape
    tb = _div_tile(rows, max(SUBLANES, (256 * 1024) // cols // SUBLANES * SUBLANES), SUBLANES)
    c1 = 1.0 / (1.0 - ADAM_B1 ** ADAM_STEP)
    c2 = 1.0 / (1.0 - ADAM_B2 ** ADAM_STEP)

    def body(w_ref, g_ref, m_ref, v_ref, d_ref, nm_ref, nv_ref):
        gv = g_ref[...]
        nm = ADAM_B1 * m_ref[...] + (1.0 - ADAM_B1) * gv
        nv = ADAM_B2 * v_ref[...] + (1.0 - ADAM_B2) * (gv * gv)
        nm_ref[...] = nm
        nv_ref[...] = nv
        d_ref[...] = -ADAM_LR * ((nm * c1) / (jnp.sqrt(nv * c2) + ADAM_EPS) + ADAM_WD * w_ref[...])

    blk = pl.BlockSpec((tb, cols), lambda i: (i, 0))
    return _call(body, (w, g, m, v), comm, name=name, grid=(rows // tb,), in_specs=[blk] * 4, out_specs=[blk] * 3,
                 out_shape=[jax.ShapeDtypeStruct((rows, cols), F32)] * 3)


def _sum_leading(parts, name):
    n, rows, cols = parts.shape
    tb = _div_tile(rows, 512, SUBLANES)

    def body(p_ref, o_ref):
        acc = p_ref[0]
        for k in range(1, n):
            acc = acc + p_ref[k]
        o_ref[...] = acc

    return pl.pallas_call(
        body, name=name, grid=(rows // tb,), in_specs=[pl.BlockSpec((n, tb, cols), lambda i: (0, i, 0))],
        out_specs=pl.BlockSpec((tb, cols), lambda i: (i, 0)),
        out_shape=jax.ShapeDtypeStruct((rows, cols), F32), compiler_params=_params(),
    )(parts)


def _place():
    x, y, c = lax.axis_index("x"), lax.axis_index("y"), lax.axis_index("c")
    return x, y, c, [(1 - x, y), (x, 1 - y), (1 - x, 1 - y)]


def _all_gather(block, name):
    m_per, n = block.shape

    def body(x_ref, out_ref, send_sems, recv_sems, local_sem):
        x, y, c, chips = _place()
        me, sibling = (x, y, c), (x, y, 1 - c)

        def rows(px, py, pc):
            return out_ref.at[pl.ds((4 * px + 2 * py + pc) * m_per, m_per), :]

        def copy(k, blk, to, src=None):
            return pltpu.make_async_remote_copy(
                src_ref=rows(*blk) if src is None else src, dst_ref=rows(*blk), send_sem=send_sems.at[k],
                recv_sem=recv_sems.at[k], device_id=to, device_id_type=MESH)

        mine = pltpu.make_async_copy(x_ref, rows(*me), local_sem)
        mine.start()
        first = [copy(0, me, sibling, src=x_ref)]
        first += [copy(1 + j, me, (*chip, c), src=x_ref) for j, chip in enumerate(chips)]
        for cp in first:
            cp.start()
        passed = [copy(4 + j, (*chip, c), sibling) for j, chip in enumerate(chips)]
        for j, chip in enumerate(chips):
            copy(1 + j, (*chip, c), me).wait_recv()
            passed[j].start()
        copy(0, sibling, me).wait_recv()
        for j, chip in enumerate(chips):
            copy(4 + j, (*chip, 1 - c), me).wait_recv()
        for cp in first + passed:
            cp.wait_send()
        mine.wait()

    return pl.pallas_call(
        body, name=name, out_shape=jax.ShapeDtypeStruct((N_DEV * m_per, n), block.dtype),
        in_specs=[pl.BlockSpec(memory_space=pltpu.VMEM)], out_specs=pl.BlockSpec(memory_space=pltpu.VMEM),
        scratch_shapes=[pltpu.SemaphoreType.DMA((7,)), pltpu.SemaphoreType.DMA((7,)), pltpu.SemaphoreType.DMA],
        compiler_params=_params(),
    )(block)


def _hbm_specs(n):
    return [pl.BlockSpec(memory_space=HBM)] * n


def _half_rows(ref, half, lead=None):
    h = ref.shape[-2] // 2
    rows = pl.ds(pl.multiple_of(half * h, 2 * SUBLANES), h)
    return ref.at[rows, :] if lead is None else ref.at[lead, rows, :]


class _Comm:
    def __init__(self, operands, out_shape, scratch, build, aliases=None):
        self.operands, self.out_shape, self.scratch = list(operands), list(out_shape), list(scratch)
        self.build, self.aliases = build, dict(aliases or {})


def _call(body, operands, comm=None, *, name, grid, in_specs, out_specs, out_shape, scratch_shapes=(),
          input_output_aliases=None):
    aliases = dict(input_output_aliases or {})
    if comm is None:
        return pl.pallas_call(
            body, name=name, grid=grid, in_specs=in_specs, out_specs=out_specs, out_shape=out_shape,
            scratch_shapes=list(scratch_shapes), input_output_aliases=aliases, compiler_params=_params())(*operands)
    single = not isinstance(out_shape, (list, tuple))
    outs = [out_shape] if single else list(out_shape)
    ospecs = [out_specs] if single else list(out_specs)
    n_in, n_out, n_scr = len(operands), len(outs), len(scratch_shapes)
    c_in, c_out = len(comm.operands), len(comm.out_shape)
    for i, o in comm.aliases.items():
        aliases[n_in + i] = n_out + o

    def hosted(*refs):
        ins, c_ins = refs[:n_in], refs[n_in:n_in + c_in]
        o0 = n_in + c_in
        o_refs, c_outs = refs[o0:o0 + n_out], refs[o0 + n_out:o0 + n_out + c_out]
        s0 = o0 + n_out + c_out
        scr, sems = refs[s0:s0 + n_scr], refs[s0 + n_scr:]
        start, finish = comm.build(c_ins, c_outs, sems)
        first = last = None
        for dim, size in enumerate(grid):
            at0, at1 = pl.program_id(dim) == 0, pl.program_id(dim) == size - 1
            first = at0 if first is None else jnp.logical_and(first, at0)
            last = at1 if last is None else jnp.logical_and(last, at1)
        pl.when(first)(start)
        body(*ins, *o_refs, *scr)
        pl.when(last)(finish)

    res = pl.pallas_call(
        hosted, name=name, grid=grid, in_specs=list(in_specs) + _hbm_specs(c_in),
        out_specs=ospecs + _hbm_specs(c_out), out_shape=outs + comm.out_shape,
        scratch_shapes=list(scratch_shapes) + comm.scratch, input_output_aliases=aliases,
        compiler_params=_params())(*operands, *comm.operands)
    return (res[0] if single else res[:n_out]), res[n_out:]


def _run_comm(comm, name):
    c_in, c_out = len(comm.operands), len(comm.out_shape)

    def body(*refs):
        start, finish = comm.build(refs[:c_in], refs[c_in:c_in + c_out], refs[c_in + c_out:])
        start()
        finish()

    return pl.pallas_call(
        body, name=name, in_specs=_hbm_specs(c_in), out_specs=_hbm_specs(c_out), out_shape=comm.out_shape,
        scratch_shapes=comm.scratch, input_output_aliases=comm.aliases, compiler_params=_params())(*comm.operands)


def _gather_comm(shards):
    nw = len(shards)

    def build(in_refs, out_refs, sems):
        send_sems, recv_sems = sems
        x, y, c, chips = _place()
        me, sibling = (x, y, c), (x, y, 1 - c)

        def copy(w, k, block, half, to, src=None):
            dst = _half_rows(out_refs[w], half, 2 * block[0] + block[1])
            return pltpu.make_async_remote_copy(
                src_ref=dst if src is None else src, dst_ref=dst, send_sem=send_sems.at[w, k],
                recv_sem=recv_sems.at[w, k], device_id=to, device_id_type=MESH)

        first = [copy(w, j, (x, y), c, (*chip, c), src=_half_rows(in_refs[w], c))
                 for w in range(nw) for j, chip in enumerate(chips)]

        def start():
            for cp in first:
                cp.start()

        def finish():
            passed = []
            for w in range(nw):
                for j, chip in enumerate(chips):
                    copy(w, j, chip, c, me).wait_recv()
                    passed.append(copy(w, 3 + j, chip, c, sibling))
                    passed[-1].start()
            for w in range(nw):
                for j, chip in enumerate(chips):
                    copy(w, 3 + j, chip, 1 - c, me).wait_recv()
            for cp in first + passed:
                cp.wait_send()

        return start, finish

    return _Comm(shards, [jax.ShapeDtypeStruct((N_CHIPS,) + w.shape, w.dtype) for w in shards],
                 [pltpu.SemaphoreType.DMA((nw, 6)), pltpu.SemaphoreType.DMA((nw, 6))], build)


def _swap_comm(gs):
    nw = len(gs)

    def build(in_refs, out_refs, sems):
        send_sems, recv_sems = sems
        x, y, c, _ = _place()
        cps = []
        for w in range(nw):
            h = in_refs[w].shape[1] // 2
            src = in_refs[w].at[:, pl.ds(pl.multiple_of((1 - c) * h, 2 * SUBLANES), h), :]
            cps.append(pltpu.make_async_remote_copy(
                src_ref=src, dst_ref=out_refs[w], send_sem=send_sems.at[w], recv_sem=recv_sems.at[w],
                device_id=(x, y, 1 - c), device_id_type=MESH))

        def start():
            for cp in cps:
                cp.start()

        def finish():
            for cp in cps:
                cp.wait()

        return start, finish

    return _Comm(gs, [jax.ShapeDtypeStruct((N_CHIPS, g.shape[1] // 2, g.shape[2]), g.dtype) for g in gs],
                 [pltpu.SemaphoreType.DMA((nw,)), pltpu.SemaphoreType.DMA((nw,))], build)


def _exchange_comm(s1s):
    nw = len(s1s)

    def build(in_refs, out_refs, sems):
        send_sems, recv_sems = sems
        x, y, c, chips = _place()
        cps = [pltpu.make_async_remote_copy(
            src_ref=in_refs[w].at[2 * chip[0] + chip[1]], dst_ref=out_refs[w].at[j], send_sem=send_sems.at[w, j],
            recv_sem=recv_sems.at[w, j], device_id=(*chip, c), device_id_type=MESH)
            for w in range(nw) for j, chip in enumerate(chips)]

        def start():
            for cp in cps:
                cp.start()

        def finish():
            for cp in cps:
                cp.wait()

        return start, finish

    return _Comm(s1s, [jax.ShapeDtypeStruct((N_CHIPS - 1,) + s.shape[1:], s.dtype) for s in s1s],
                 [pltpu.SemaphoreType.DMA((nw, 3)), pltpu.SemaphoreType.DMA((nw, 3))], build)


def _share_comm(fs):
    nw = len(fs)

    def build(in_refs, out_refs, sems):
        del in_refs
        send_sems, recv_sems = sems
        x, y, c, _ = _place()

        def copy(w, half):
            rows = _half_rows(out_refs[w], half)
            return pltpu.make_async_remote_copy(
                src_ref=rows, dst_ref=rows, send_sem=send_sems.at[w], recv_sem=recv_sems.at[w],
                device_id=(x, y, 1 - c), device_id_type=MESH)

        sends = [copy(w, c) for w in range(nw)]

        def start():
            for cp in sends:
                cp.start()

        def finish():
            for w in range(nw):
                copy(w, 1 - c).wait_recv()
            for cp in sends:
                cp.wait_send()

        return start, finish

    return _Comm(fs, [jax.ShapeDtypeStruct(f.shape, f.dtype) for f in fs],
                 [pltpu.SemaphoreType.DMA((nw,)), pltpu.SemaphoreType.DMA((nw,))], build,
                 aliases={w: w for w in range(nw)})


def _add_sibling(g, r1, place, name):
    nch, h, cols = r1.shape
    tr = _div_tile(h, 256, 2 * SUBLANES)
    nb = h // tr

    def body(place_ref, g_ref, r_ref, o_ref):
        del place_ref
        o_ref[...] = (g_ref[...].astype(F32) + r_ref[...].astype(F32)).astype(BF16)

    spec = pltpu.PrefetchScalarGridSpec(
        num_scalar_prefetch=1, grid=(nch, nb),
        in_specs=[pl.BlockSpec((None, tr, cols), lambda k, i, p: (k, p[0] * nb + i, 0)),
                  pl.BlockSpec((None, tr, cols), lambda k, i, p: (k, i, 0))],
        out_specs=pl.BlockSpec((None, tr, cols), lambda k, i, p: (k, i, 0)))
    return pl.pallas_call(body, name=name, grid_spec=spec, out_shape=jax.ShapeDtypeStruct((nch, h, cols), BF16),
                          compiler_params=_params())(place, g, r1)


def _add_chips(s1, r2, place, name):
    _, h, cols = s1.shape
    tr = _div_tile(h, 256, 2 * SUBLANES)
    nb = h // tr

    def body(place_ref, s_ref, r_ref, o_ref):
        del place_ref
        acc = s_ref[...].astype(F32)
        for j in range(N_CHIPS - 1):
            acc = acc + r_ref[j].astype(F32)
        o_ref[...] = acc

    spec = pltpu.PrefetchScalarGridSpec(
        num_scalar_prefetch=1, grid=(nb,),
        in_specs=[pl.BlockSpec((None, tr, cols), lambda i, p: (p[1], i, 0)),
                  pl.BlockSpec((N_CHIPS - 1, tr, cols), lambda i, p: (0, i, 0))],
        out_specs=pl.BlockSpec((tr, cols), lambda i, p: (p[0] * nb + i, 0)))
    return pl.pallas_call(body, name=name, grid_spec=spec, out_shape=jax.ShapeDtypeStruct((2 * h, cols), F32),
                          compiler_params=_params())(place, s1, r2)


def _quarter_turn(m):
    h = m.shape[-1] // 2
    return jnp.concatenate([-m[..., h:], m[..., :h]], axis=-1)


def _quarter_turn_back(m):
    h = m.shape[-1] // 2
    return jnp.concatenate([m[..., h:], -m[..., :h]], axis=-1)


def _join_cols(sh):
    return jnp.concatenate([sh[k] for k in range(N_CHIPS)], axis=1)


def _split_cols(full):
    c = full.shape[1] // N_CHIPS
    return jnp.stack([full[:, k * c:(k + 1) * c] for k in range(N_CHIPS)])


def kernel(x, c, positions, w_ada, b_ada, pre_norm1_g, w_in, gm_ln_g, gm_ln_b, gm_w_s, gm_b_s, w_branch_a, q_norm_g, w_uq, kv_norm_g, w_ukv, w_branch_b, w_out, post_norm1_g, pre_norm2_g, w_up, conv_w, conv_b, w_down, post_norm2_g, loss_target, m_w_ada, m_b_ada, m_pre_norm1_g, m_w_in, m_gm_ln_g, m_gm_ln_b, m_gm_w_s, m_gm_b_s, m_w_branch_a, m_q_norm_g, m_w_uq, m_kv_norm_g, m_w_ukv, m_w_branch_b, m_w_out, m_post_norm1_g, m_pre_norm2_g, m_w_up, m_conv_w, m_conv_b, m_w_down, m_post_norm2_g, v_w_ada, v_b_ada, v_pre_norm1_g, v_w_in, v_gm_ln_g, v_gm_ln_b, v_gm_w_s, v_gm_b_s, v_w_branch_a, v_q_norm_g, v_w_uq, v_kv_norm_g, v_w_ukv, v_w_branch_b, v_w_out, v_post_norm1_g, v_pre_norm2_g, v_w_up, v_conv_w, v_conv_b, v_w_down, v_post_norm2_g):
    given = dict(locals())
    s, d = x.shape[1], x.shape[2]
    gw = gm_ln_g.shape[0]
    ql, kvl = q_norm_g.shape[0], kv_norm_g.shape[0]
    heads = N_CHIPS * w_uq.shape[1] // (NOPE + ROPE)
    ff = N_CHIPS * w_down.shape[0]
    assert gw == d and N_CHIPS * w_ukv.shape[1] == heads * (NOPE + VHEAD)
    ix, iy, ic = lax.axis_index("x"), lax.axis_index("y"), lax.axis_index("c")
    chip = 2 * ix + iy
    dev = 2 * chip + ic
    row = lambda v: v.reshape(1, -1)

    c_all = _all_gather(jnp.pad(c, ((0, SUBLANES - 1), (0, 0))), "gather_c").reshape(N_DEV, SUBLANES, d)[:, 0]
    na = w_ada.shape[1]
    b_ada_mine = lax.dynamic_slice(b_ada, (chip * na,), (na,))
    mod_cols = _ada_fwd(c_all, w_ada, row(b_ada_mine), "ada_fwd")
    mod_all = _all_gather(mod_cols, "gather_mod").reshape(N_CHIPS, N_CORES, N_DEV, na)[:, 0]
    mod = lax.dynamic_index_in_dim(mod_all, dev, axis=1, keepdims=False).reshape(N_MOD, d)
    shift1, scale1, gate1, shift2, scale2, gate2 = (mod[i:i + 1] for i in range(N_MOD))

    mine = {n: given[n].astype(BF16) for n in BIG}
    gather = lambda names: _gather_comm([mine[n] for n in names])
    whole = lambda n, g: lax.dynamic_update_slice(g, mine[n][None], (chip, 0, 0))
    rows4 = lambda sh4: sh4.reshape(-1, sh4.shape[2])
    wi = _join_cols(whole("w_in", _run_comm(gather(["w_in"]), "gather_w_in")[0]))
    o_q, o_kv, o_pe, o_ga = 2 * gw, 2 * gw + ql, 2 * gw + ql + kvl, 2 * gw + ql + kvl + ROPE
    w_in_big = jnp.concatenate([wi[:, :o_q], wi[:, o_ga:]], axis=1)
    w_in_lat = jnp.concatenate([wi[:, o_q:o_ga], _quarter_turn(wi[:, o_pe:o_ga])], axis=1)

    inv = ROPE_THETA ** (-jnp.arange(0, ROPE, 2, dtype=F32) / ROPE)
    ang = positions[0].astype(F32)[:, None] * inv
    cos, sin = jnp.cos(ang), jnp.sin(ang)
    rope_k = jnp.concatenate([cos, cos, sin, sin], axis=1)
    softmax_scale = float(NOPE + ROPE) ** -0.5
    rope_q = jnp.concatenate([jnp.ones((s, NOPE), F32), rope_k], axis=1) * softmax_scale

    x2d, tgt = x[0], loss_target[0]
    g_pre1, g_post1, g_pre2, g_post2 = row(pre_norm1_g), row(post_norm1_g), row(pre_norm2_g), row(post_norm2_g)
    ln_g, ln_b, q_g, kv_g = row(gm_ln_g), row(gm_ln_b), row(q_norm_g), row(kv_norm_g)
    b_s_t = gm_b_s.T
    conv_wf = _all_gather(jnp.pad(conv_w, ((0, SUBLANES - CONV_TAPS), (0, 0))), "gather_conv_w")
    conv_wf = conv_wf.reshape(N_CHIPS, N_CORES, SUBLANES, conv_w.shape[1])[:, 0, :CONV_TAPS]
    conv_wf = conv_wf.transpose(1, 0, 2).reshape(CONV_TAPS, 2 * ff)
    conv_bf = row(conv_b)

    h1 = _prenorm(x2d, g_pre1, scale1, shift1, "prenorm1")
    z_big, (g_uq, g_ukv, g_a) = _matmul(h1, w_in_big, mode="nn", out_dtype=F32, name="mm_z_big", tm=s,
                                        comm=gather(["w_uq", "w_ukv", "w_branch_a"]))
    wq = _join_cols(whole("w_uq", g_uq)).reshape(ql, heads, NOPE + ROPE)
    w_q = jnp.concatenate([wq, _quarter_turn(wq[:, :, NOPE:])], axis=2).reshape(ql, heads * HEAD_W)
    w_kv = _join_cols(whole("w_ukv", g_ukv)).reshape(kvl, heads, 2, NOPE).transpose(0, 2, 1, 3)
    w_kv = w_kv.reshape(kvl, 2 * heads * NOPE)
    w_a = rows4(whole("w_branch_a", g_a))
    z_lat = _matmul(h1, w_in_lat, mode="nn", out_dtype=F32, name="mm_z_lat", tm=s, tn=1024)
    a_act = _gmlp_fwd(z_big, ln_g, ln_b, gm_w_s, b_s_t, "gmlp_fwd")
    qn, kvn, kr = _mla_prep(z_lat, q_g, kv_g, rope_k, "mla_prep")
    q_rot = _matmul(qn, w_q, mode="nn", out_dtype=BF16, name="mm_q", tm=s, tn=HEAD_W, mul=rope_q)
    kv_all = _matmul(kvn, w_kv, mode="nn", out_dtype=BF16, name="mm_kv", tm=s, tn=1024)
    (o_att, lse), (g_b, g_o, g_up) = _attn_fwd(q_rot, kv_all, kr, heads, "attn_fwd",
                                               comm=gather(["w_branch_b", "w_out", "w_up"]))
    w_b, w_o, w_upf = rows4(whole("w_branch_b", g_b)), rows4(whole("w_out", g_o)), whole("w_up", g_up)
    y_a = _matmul(a_act, w_a, mode="nn", out_dtype=F32, name="mm_y_a", tm=s)
    y_b = _matmul(o_att, w_b, mode="nn", out_dtype=F32, name="mm_y_b", tm=s)
    merged = _merge(z_big, y_a, y_b, "merge")
    y1 = _matmul(merged, w_o, mode="nn", out_dtype=F32, name="mm_y1", tm=s)
    x1, h2 = _post_pre(x2d, y1, gate1, g_post1, g_pre2, scale2, shift2, "post1_pre2")

    up_pre, (g_dn,) = _matmul(h2, w_upf, mode="nn", out_dtype=BF16, name="mm_up", tm=s, tn=1408,
                              comm=gather(["w_down"]))
    w_dn = rows4(whole("w_down", g_dn))
    act = _conv_fwd(up_pre, conv_wf, conv_bf, "conv_fwd")
    ffn = _matmul(act, w_dn, mode="nn", out_dtype=F32, name="mm_ffn", tm=s, tk=1408)

    dffn, dgate2, g_post2_grad, dx2, loss_part = _post_bwd(ffn, gate2, g_post2, "post2_bwd", xin=x1, target=tgt)
    loss = lax.psum(loss_part[0, 0], ("x", "y", "c"))
    place = jnp.stack([ic, chip]).astype(jnp.int32)
    rows_of = lambda g: g.reshape(N_CHIPS, g.shape[0] // N_CHIPS, g.shape[1])
    add_sibling = lambda names, gs, r1s: [_add_sibling(g, r1, place, "rs_add_sibling_" + n)
                                          for n, g, r1 in zip(names, gs, r1s)]
    add_chips = lambda names, s1s, r2s: [_add_chips(s1, r2, place, "rs_add_chips_" + n)
                                         for n, s1, r2 in zip(names, s1s, r2s)]
    dact = _matmul(dffn, w_dn, mode="nt", out_dtype=BF16, name="mm_dact", tm=s)
    gp_down = [rows_of(_matmul(act, dffn, mode="tn", out_dtype=BF16, name="mm_gw_down", tn=1024, tk=s))]
    (dup, gcw_g, gcw_v, gcb_g, gcb_v), r1_down = _conv_bwd(up_pre, dact, conv_wf, conv_bf, "conv_bwd",
                                                            comm=_swap_comm(gp_down))
    s1_down = add_sibling(["w_down"], gp_down, r1_down)
    dh2, r2_down = _matmul(dup, w_upf, mode="nt", out_dtype=F32, name="mm_dh2", tm=s, tk=1408,
                           comm=_exchange_comm(s1_down))
    half_down = add_chips(["w_down"], s1_down, r2_down)
    gw_up = _matmul(h2, dup, mode="tn", out_dtype=BF16, name="mm_gw_up", tn=1408, tk=s, out_groups=N_CHIPS)
    dx1, dshift2, dscale2, g_pre2_grad = _prenorm_bwd(x1, dh2, dx2, g_pre2, scale2, "prenorm2_bwd")

    dy1, dgate1, g_post1_grad = _post_bwd(y1, gate1, g_post1, "post1_bwd", dxo=dx1)
    dmerged = _matmul(dy1, w_o, mode="nt", out_dtype=F32, name="mm_dmerged", tm=s)
    gw_out = _matmul(merged, dy1, mode="tn", out_dtype=BF16, name="mm_gw_out", tn=1024, tk=s)
    dy_a, dy_b, dz_big = _merge_bwd(dmerged, z_big, y_a, y_b, "merge_bwd")
    da = _matmul(dy_a, w_a, mode="nt", out_dtype=F32, name="mm_da", tm=s)
    gw_a = _matmul(a_act, dy_a, mode="tn", out_dtype=BF16, name="mm_gw_a", tn=1024, tk=s)
    do = _matmul(dy_b, w_b, mode="nt", out_dtype=BF16, name="mm_do", tm=s)
    gw_b = _matmul(o_att, dy_b, mode="tn", out_dtype=BF16, name="mm_gw_b", tn=1024, tk=s)
    mid = ["w_up", "w_out", "w_branch_a", "w_branch_b"]
    gp_mid = [gw_up, rows_of(gw_out), rows_of(gw_a), rows_of(gw_b)]
    (dz_big, g_ws, g_bs_t, g_ln_g, g_ln_b), r1_mid = _gmlp_bwd(z_big, da, dz_big, ln_g, ln_b, gm_w_s, b_s_t,
                                                                "gmlp_bwd", comm=_swap_comm(gp_mid))
    s1_mid = add_sibling(mid, gp_mid, r1_mid)
    (dq, dk, dv), r2_up_out = _attn_bwd(q_rot, kv_all, kr, o_att, do, lse, heads, "attn_bwd",
                                        comm=_exchange_comm(s1_mid[:2]))
    dq_big, dkv, dkk = _mla_bwd_mid(dq, dk, dv, rope_q, rope_k, heads, "mla_bwd_mid")
    gw_q = _matmul(qn, dq_big, mode="tn", out_dtype=F32, name="mm_gw_q", tn=1024, tk=s)
    dqn = _matmul(dq_big, w_q, mode="nt", out_dtype=F32, name="mm_dqn", tm=s, tk=1024)
    gw_kv = _matmul(kvn, dkv, mode="tn", out_dtype=BF16, name="mm_gw_kv", tn=1024, tk=s)
    dkvn = _matmul(dkv, w_kv, mode="nt", out_dtype=F32, name="mm_dkvn", tm=s, tk=1024)
    dz_lat, g_q, g_kv = _mla_bwd_post(z_lat, dqn, dkvn, dkk, q_g, kv_g, "mla_bwd_post")
    dh1, r2_a_b = _matmul(dz_big, w_in_big, mode="nt", out_dtype=F32, name="mm_dh1_big", tm=s,
                          comm=_exchange_comm(s1_mid[2:]))
    half_mid = add_chips(mid, s1_mid, list(r2_up_out) + list(r2_a_b))
    dh1 = _matmul(dz_lat, w_in_lat, mode="nt", out_dtype=F32, name="mm_dh1_lat", tm=s, tk=1024, add=dh1)
    gw_in_big, shared = _matmul(h1, dz_big, mode="tn", out_dtype=BF16, name="mm_gw_in_big", tn=1024, tk=s,
                                comm=_share_comm(half_down + half_mid))
    grads = dict(zip(["w_down"] + mid, shared))
    gw_in_lat = _matmul(h1, dz_lat, mode="tn", out_dtype=F32, name="mm_gw_in_lat", tn=1024, tk=s)

    gq = gw_q.reshape(ql, heads, HEAD_W)
    gq_pe = gq[:, :, NOPE:NOPE + ROPE] + _quarter_turn_back(gq[:, :, NOPE + ROPE:])
    g_pe = gw_in_lat[:, ql + kvl:ql + kvl + ROPE] + _quarter_turn_back(gw_in_lat[:, ql + kvl + ROPE:])
    last = ["w_in", "w_uq", "w_ukv"]
    gp_last = [
        _split_cols(jnp.concatenate([gw_in_big[:, :o_q], gw_in_lat[:, :ql + kvl].astype(BF16), g_pe.astype(BF16),
                                     gw_in_big[:, o_q:]], axis=1)),
        _split_cols(jnp.concatenate([gq[:, :, :NOPE], gq_pe], axis=2).reshape(ql, heads * (NOPE + ROPE)).astype(BF16)),
        _split_cols(gw_kv.reshape(kvl, 2, heads, NOPE).transpose(0, 2, 1, 3).reshape(kvl, heads * 2 * NOPE)),
    ]
    (grad_x, dshift1, dscale1, g_pre1_grad), r1_last = _prenorm_bwd(x2d, dh1, dx1, g_pre1, scale1, "prenorm1_bwd",
                                                                    comm=_swap_comm(gp_last))
    s1_last = add_sibling(last, gp_last, r1_last)

    dmod = jnp.concatenate([dshift1, dscale1, dgate1, dshift2, dscale2, dgate2], axis=1)
    dmod_all = _all_gather(jnp.pad(dmod, ((0, SUBLANES - 1), (0, 0))), "gather_dmod")
    dmod_all = dmod_all.reshape(N_DEV, SUBLANES, N_MOD * d)[:, 0]
    grad_b_ada = _sum_leading(dmod_all.reshape(N_DEV, 1, N_MOD * d), "sum_b_ada")[0]
    dmod_mine = lax.dynamic_slice(dmod_all, (0, chip * na), (N_DEV, na))
    grads["w_ada"] = _ada_bwd(c_all.T, dmod_mine, "ada_bwd")

    partial = {
        "pre_norm1_g": g_pre1_grad, "gm_ln_g": g_ln_g, "gm_ln_b": g_ln_b, "gm_w_s": g_ws, "gm_b_s": g_bs_t[:, :gm_b_s.shape[0]].T,
        "q_norm_g": g_q, "kv_norm_g": g_kv, "post_norm1_g": g_post1_grad, "pre_norm2_g": g_pre2_grad,
        "conv_w": jnp.concatenate([gcw_g, gcw_v], axis=1), "conv_b": jnp.concatenate([gcb_g, gcb_v], axis=1),
        "post_norm2_g": g_post2_grad,
    }
    flat = jnp.concatenate([partial[n].reshape(-1) for n in SMALL_PARTIAL])
    n_small = flat.shape[0]
    rows_small = -(-n_small // (LANES * SUBLANES)) * SUBLANES
    flat = jnp.pad(flat, (0, rows_small * LANES - n_small)).reshape(rows_small, LANES)
    small_sum = _sum_leading(_all_gather(flat, "gather_small").reshape(N_DEV, rows_small, LANES), "sum_small")
    small_sum = small_sum.reshape(-1)
    off = 0
    for n in SMALL_PARTIAL:
        shape = (CONV_TAPS, 2 * ff) if n == "conv_w" else given[n].shape
        size = partial[n].size
        grads[n] = small_sum[off:off + size].reshape(shape)
        off += size
    grads["conv_w"] = lax.dynamic_slice(grads["conv_w"], (0, chip * conv_w.shape[1]), conv_w.shape)
    grads["b_ada"] = grad_b_ada

    half_last = add_chips(last, s1_last, _run_comm(_exchange_comm(s1_last), "rs_exchange_last"))
    grads.update(zip(last, _run_comm(_share_comm(half_last), "rs_share_last")))

    delta, new_m, new_v = {}, {}, {}
    for n in ("w_ada",) + BIG:
        turn = (lambda a: a.T) if n == "w_in" else (lambda a: a)
        g_t = turn(grads[n])
        outs = _adamw(turn(given[n]), g_t, turn(given["m_" + n]), turn(given["v_" + n]), "adamw_" + n)
        grads[n] = turn(g_t)
        delta[n], new_m[n], new_v[n] = (turn(o) for o in outs)

    def small_pack(prefix, source):
        v = jnp.concatenate([source[prefix + n].reshape(-1) for n in SMALL])
        rows = -(-v.shape[0] // (LANES * SUBLANES)) * SUBLANES
        return jnp.pad(v, (0, rows * LANES - v.shape[0])).reshape(rows, LANES)

    outs = _adamw(small_pack("", given), small_pack("", grads), small_pack("m_", given), small_pack("v_", given),
                  "adamw_small")
    off = 0
    for n in SMALL:
        size = given[n].size
        for store, packed_out in zip((delta, new_m, new_v), outs):
            store[n] = packed_out.reshape(-1)[off:off + size].reshape(given[n].shape)
        off += size

    return (loss, grad_x[None], *[grads[n] for n in WEIGHTS], *[delta[n] for n in WEIGHTS],
            *[new_m[n] for n in WEIGHTS], *[new_v[n] for n in WEIGHTS])
```

```python
import functools

import jax
import jax.numpy as jnp
from jax import lax
from jax.experimental import pallas as pl
from jax.experimental.pallas import tpu as pltpu

F32 = jnp.float32
BF16 = jnp.bfloat16
MESH = pl.DeviceIdType.MESH
HBM = pltpu.HBM

EPS = 1e-6
NOPE, ROPE, VHEAD = 128, 64, 128
HEAD_W = NOPE + 2 * ROPE
ROPE_THETA = 10000.0
CONV_TAPS = 3
N_MOD = 6
N_CHIPS, N_CORES, N_DEV = 4, 2, 8
ADAM_LR, ADAM_B1, ADAM_B2, ADAM_EPS, ADAM_WD, ADAM_STEP = 0.001, 0.9, 0.999, 1e-08, 0.01, 10

LANES = 128
SUBLANES = 8
VMEM_LIMIT = 56 * 2**20

BIG = ("w_in", "w_branch_a", "w_uq", "w_ukv", "w_branch_b", "w_out", "w_up", "w_down")
WEIGHTS = ("w_ada", "b_ada", "pre_norm1_g", "w_in", "gm_ln_g", "gm_ln_b", "gm_w_s", "gm_b_s", "w_branch_a",
           "q_norm_g", "w_uq", "kv_norm_g", "w_ukv", "w_branch_b", "w_out", "post_norm1_g", "pre_norm2_g",
           "w_up", "conv_w", "conv_b", "w_down", "post_norm2_g")
SMALL_PARTIAL = ("pre_norm1_g", "gm_ln_g", "gm_ln_b", "gm_w_s", "gm_b_s", "q_norm_g", "kv_norm_g", "post_norm1_g",
                 "pre_norm2_g", "conv_w", "conv_b", "post_norm2_g")
SMALL = ("b_ada",) + SMALL_PARTIAL


def _div_tile(n, cap, mult=LANES):
    t = (min(cap, n) // mult) * mult
    while t >= mult:
        if n % t == 0:
            return t
        t -= mult
    return n


def _params(**kw):
    return pltpu.CompilerParams(vmem_limit_bytes=VMEM_LIMIT, **kw)


def _row_spec(width):
    return pl.BlockSpec((1, width), lambda *_: (0, 0))


def _gelu(x):
    k = 0.7978845608028654
    return 0.5 * x * (1.0 + jnp.tanh(k * (x + 0.044715 * x * x * x)))


def _gelu_grad(x):
    k = 0.7978845608028654
    t = jnp.tanh(k * (x + 0.044715 * x * x * x))
    return 0.5 * (1.0 + t) + 0.5 * x * (1.0 - t * t) * k * (1.0 + 3.0 * 0.044715 * x * x)


def _sigmoid(x):
    return 1.0 / (1.0 + jnp.exp(-x))


def _dot(a, b, dims):
    return lax.dot_general(a, b, (dims, ((), ())), preferred_element_type=F32)


NN = ((1,), (0,))
NT = ((1,), (1,))
TN = ((0,), (0,))


def _logical(arr):
    if arr.ndim == 2:
        return arr.shape[0], arr.shape[1], arr.shape[1]
    return arr.shape[1], arr.shape[0] * arr.shape[2], arr.shape[2]


def _tile_spec(ndim, group_w, blk_rows, blk_cols, row_of, col_of):
    if ndim == 2:
        return pl.BlockSpec((blk_rows, blk_cols), lambda i, j, k: (row_of(i, j, k), col_of(i, j, k)))
    per = group_w // blk_cols
    return pl.BlockSpec((None, blk_rows, blk_cols),
                        lambda i, j, k: (col_of(i, j, k) // per, row_of(i, j, k), col_of(i, j, k) % per))


def _matmul(a, b, *, mode, out_dtype, name, tm=512, tn=512, tk=2048, mul=None, add=None, out_groups=None, comm=None):
    ar, ac, agw = _logical(a)
    br, bc, bgw = _logical(b)
    if mode == "nn":
        m, kd, n = ar, ac, bc
        m_w, k_w, n_w = (), (agw,), (bgw,)
    elif mode == "nt":
        m, kd, n = ar, ac, br
        m_w, k_w, n_w = (), (agw, bgw), ()
    else:
        m, kd, n = ac, ar, bc
        m_w, k_w, n_w = (agw,), (), (bgw,)
    if out_groups is not None:
        n_w = n_w + (n // out_groups,)
    tm = _div_tile(min((m,) + m_w), tm, SUBLANES)
    tn = _div_tile(min((n,) + n_w), tn)
    tk = _div_tile(min((kd,) + k_w), tk)
    assert all(w % tn == 0 for w in n_w) and all(w % tk == 0 for w in k_w) and all(w % tm == 0 for w in m_w)
    nk = kd // tk
    dims = {"nn": NN, "nt": NT, "tn": TN}[mode]
    gi, gj, gk = (lambda i, j, k: i), (lambda i, j, k: j), (lambda i, j, k: k)
    if mode == "nn":
        a_spec = _tile_spec(a.ndim, agw, tm, tk, gi, gk)
        b_spec = _tile_spec(b.ndim, bgw, tk, tn, gk, gj)
    elif mode == "nt":
        a_spec = _tile_spec(a.ndim, agw, tm, tk, gi, gk)
        b_spec = _tile_spec(b.ndim, bgw, tn, tk, gj, gk)
    else:
        a_spec = _tile_spec(a.ndim, agw, tk, tm, gk, gi)
        b_spec = _tile_spec(b.ndim, bgw, tk, tn, gk, gj)
    in_specs, operands = [a_spec, b_spec], [a, b]
    if mul is not None:
        assert mul.shape == (m, tn)
        in_specs.append(pl.BlockSpec((tm, tn), lambda i, j, k: (i, 0)))
        operands.append(mul)
    if add is not None:
        in_specs.append(pl.BlockSpec((tm, tn), lambda i, j, k: (i, j)))
        operands.append(add)

    def body(*refs):
        a_ref, b_ref = refs[0], refs[1]
        pos = 2
        mul_ref = add_ref = None
        if mul is not None:
            mul_ref, pos = refs[pos], pos + 1
        if add is not None:
            add_ref, pos = refs[pos], pos + 1
        o_ref = refs[pos]

        def finish(r):
            if mul_ref is not None:
                r = r * mul_ref[...]
            if add_ref is not None:
                r = r + add_ref[...]
            o_ref[...] = r.astype(out_dtype)

        part = _dot(a_ref[...], b_ref[...], dims)
        if nk == 1:
            finish(part)
        else:
            acc_ref = refs[pos + 1]
            k = pl.program_id(2)

            @pl.when(k == 0)
            def _():
                acc_ref[...] = part

            @pl.when(k > 0)
            def _():
                acc_ref[...] += part

            @pl.when(k == nk - 1)
            def _():
                finish(acc_ref[...])

    if out_groups is None:
        out_spec, out_dims = _tile_spec(2, n, tm, tn, gi, gj), (m, n)
    else:
        out_spec, out_dims = _tile_spec(3, n // out_groups, tm, tn, gi, gj), (out_groups, m, n // out_groups)
    return _call(body, operands, comm, name=name, grid=(m // tm, n // tn, nk), in_specs=in_specs, out_specs=out_spec,
                 out_shape=jax.ShapeDtypeStruct(out_dims, out_dtype),
                 scratch_shapes=[] if nk == 1 else [pltpu.VMEM((tm, tn), F32)])


def _accumulate(ref, value):
    @pl.when(pl.program_id(0) == 0)
    def _():
        ref[...] = value

    @pl.when(pl.program_id(0) > 0)
    def _():
        ref[...] += value


def _colsum(v):
    return jnp.sum(v, axis=0, keepdims=True)


def _rowmean(v):
    return jnp.mean(v, axis=-1, keepdims=True)


def _prenorm(x, g, scale, shift, name):
    s, d = x.shape
    tb = _div_tile(s, 256, SUBLANES)

    def body(x_ref, g_ref, sc_ref, sh_ref, h_ref):
        xv = x_ref[...]
        r = lax.rsqrt(_rowmean(xv * xv) + EPS)
        h_ref[...] = ((xv * r) * g_ref[...] * (1.0 + sc_ref[...]) + sh_ref[...]).astype(BF16)

    blk = pl.BlockSpec((tb, d), lambda i: (i, 0))
    return pl.pallas_call(
        body, name=name, grid=(s // tb,), in_specs=[blk, _row_spec(d), _row_spec(d), _row_spec(d)],
        out_specs=blk, out_shape=jax.ShapeDtypeStruct((s, d), BF16), compiler_params=_params(),
    )(x, g, scale, shift)


def _post_pre(x, y, gate, pg, g2, scale2, shift2, name):
    s, d = x.shape
    tb = _div_tile(s, 256, SUBLANES)

    def body(x_ref, y_ref, gate_ref, pg_ref, g2_ref, sc_ref, sh_ref, x1_ref, h2_ref):
        yv = y_ref[...]
        rp = lax.rsqrt(_rowmean(yv * yv) + EPS)
        x1 = x_ref[...] + gate_ref[...] * ((yv * rp) * pg_ref[...])
        x1_ref[...] = x1
        r2 = lax.rsqrt(_rowmean(x1 * x1) + EPS)
        h2_ref[...] = ((x1 * r2) * g2_ref[...] * (1.0 + sc_ref[...]) + sh_ref[...]).astype(BF16)

    blk = pl.BlockSpec((tb, d), lambda i: (i, 0))
    return pl.pallas_call(
        body, name=name, grid=(s // tb,), in_specs=[blk, blk] + [_row_spec(d)] * 5,
        out_specs=[blk, blk],
        out_shape=[jax.ShapeDtypeStruct((s, d), F32), jax.ShapeDtypeStruct((s, d), BF16)],
        compiler_params=_params(),
    )(x, y, gate, pg, g2, scale2, shift2)


def _post_bwd(y, gate, pg, name, *, dxo=None, xin=None, target=None):
    s, d = y.shape
    tb = _div_tile(s, 256, SUBLANES)
    from_loss = target is not None

    def body(*refs):
        if from_loss:
            y_ref, gate_ref, pg_ref, xin_ref, t_ref, dy_ref, dgate_ref, dpg_ref, dxo_ref, loss_ref = refs
        else:
            y_ref, gate_ref, pg_ref, dxo_in_ref, dy_ref, dgate_ref, dpg_ref = refs
        yv = y_ref[...]
        rp = lax.rsqrt(_rowmean(yv * yv) + EPS)
        yh = yv * rp
        fn = yh * pg_ref[...]
        gate = gate_ref[...]
        if from_loss:
            err = xin_ref[...] + gate * fn - t_ref[...]
            dxo = err * (1.0 / d)
            dxo_ref[...] = dxo
            part = 0.5 * jnp.sum(_rowmean(err * err), axis=0, keepdims=True)
            _accumulate(loss_ref, jnp.broadcast_to(part, loss_ref.shape))
        else:
            dxo = dxo_in_ref[...]
        _accumulate(dgate_ref, _colsum(dxo * fn))
        dfn = dxo * gate
        _accumulate(dpg_ref, _colsum(dfn * yh))
        dyh = dfn * pg_ref[...]
        dy_ref[...] = (rp * (dyh - yh * _rowmean(dyh * yh))).astype(BF16)

    blk = pl.BlockSpec((tb, d), lambda i: (i, 0))
    in_specs = [blk, _row_spec(d), _row_spec(d)]
    out_specs = [blk, _row_spec(d), _row_spec(d)]
    out_shape = [jax.ShapeDtypeStruct((s, d), BF16), jax.ShapeDtypeStruct((1, d), F32),
                 jax.ShapeDtypeStruct((1, d), F32)]
    if from_loss:
        operands = (y, gate, pg, xin, target)
        in_specs += [blk, blk]
        out_specs += [blk, _row_spec(LANES)]
        out_shape += [jax.ShapeDtypeStruct((s, d), F32), jax.ShapeDtypeStruct((1, LANES), F32)]
    else:
        operands = (y, gate, pg, dxo)
        in_specs += [blk]
    return pl.pallas_call(
        body, name=name, grid=(s // tb,), in_specs=in_specs, out_specs=out_specs, out_shape=out_shape,
        compiler_params=_params(),
    )(*operands)


def _prenorm_bwd(xin, dh, dres, g, scale, name, comm=None):
    s, d = xin.shape
    tb = _div_tile(s, 256, SUBLANES)

    def body(x_ref, dh_ref, dres_ref, g_ref, sc_ref, dx_ref, dshift_ref, dscale_ref, dg_ref):
        xv = x_ref[...]
        r = lax.rsqrt(_rowmean(xv * xv) + EPS)
        xn = xv * r
        dh = dh_ref[...]
        g1 = g_ref[...]
        s1 = 1.0 + sc_ref[...]
        _accumulate(dshift_ref, _colsum(dh))
        _accumulate(dscale_ref, _colsum(dh * xn * g1))
        _accumulate(dg_ref, _colsum(dh * xn * s1))
        dxn = dh * g1 * s1
        dx_ref[...] = dres_ref[...] + r * (dxn - xn * _rowmean(dxn * xn))

    blk = pl.BlockSpec((tb, d), lambda i: (i, 0))
    return _call(
        body, (xin, dh, dres, g, scale), comm, name=name, grid=(s // tb,),
        in_specs=[blk, blk, blk, _row_spec(d), _row_spec(d)],
        out_specs=[blk, _row_spec(d), _row_spec(d), _row_spec(d)],
        out_shape=[jax.ShapeDtypeStruct((s, d), F32)] + [jax.ShapeDtypeStruct((1, d), F32)] * 3)


def _merge(z_big, y_a, y_b, name):
    s, d = y_a.shape
    tb = _div_tile(s, 256, SUBLANES)

    def body(zg_ref, ya_ref, yb_ref, o_ref):
        o_ref[...] = (_sigmoid(zg_ref[:, :d]) * ya_ref[...] + _sigmoid(zg_ref[:, d:]) * yb_ref[...]).astype(BF16)

    blk = pl.BlockSpec((tb, d), lambda i: (i, 0))
    return pl.pallas_call(
        body, name=name, grid=(s // tb,), in_specs=[pl.BlockSpec((tb, 2 * d), lambda i: (i, 1)), blk, blk],
        out_specs=blk, out_shape=jax.ShapeDtypeStruct((s, d), BF16), compiler_params=_params(),
    )(z_big, y_a, y_b)


def _merge_bwd(dmerged, z_big, y_a, y_b, name):
    s, d = y_a.shape
    tb = _div_tile(s, 256, SUBLANES)

    def body(dm_ref, zg_ref, ya_ref, yb_ref, dya_ref, dyb_ref, dz_ref):
        dm = dm_ref[...]
        sa, sb = _sigmoid(zg_ref[:, :d]), _sigmoid(zg_ref[:, d:])
        dya_ref[...] = (dm * sa).astype(BF16)
        dyb_ref[...] = (dm * sb).astype(BF16)
        dz_ref[:, :d] = (dm * ya_ref[...] * sa * (1.0 - sa)).astype(BF16)
        dz_ref[:, d:] = (dm * yb_ref[...] * sb * (1.0 - sb)).astype(BF16)

    blk = pl.BlockSpec((tb, d), lambda i: (i, 0))
    wide = pl.BlockSpec((tb, 2 * d), lambda i: (i, 1))
    return pl.pallas_call(
        body, name=name, grid=(s // tb,), in_specs=[blk, wide, blk, blk], out_specs=[blk, blk, wide],
        out_shape=[jax.ShapeDtypeStruct((s, d), BF16), jax.ShapeDtypeStruct((s, d), BF16),
                   jax.ShapeDtypeStruct((s, 4 * d), BF16)],
        compiler_params=_params(),
    )(dmerged, z_big, y_a, y_b)


def _causal_mask(ch):
    q = lax.broadcasted_iota(jnp.int32, (ch, ch), 0)
    p = lax.broadcasted_iota(jnp.int32, (ch, ch), 1)
    return (p <= q).astype(F32)


def _gmlp_norm(zc, lng, lnb, gw):
    u_pre, v_pre = zc[:, :gw], zc[:, gw:]
    vg = _gelu(v_pre)
    mu = _rowmean(vg)
    cen = vg - mu
    rstd = lax.rsqrt(_rowmean(cen * cen) + EPS)
    vhat = cen * rstd
    return u_pre, v_pre, _gelu(u_pre), vhat, rstd, vhat * lng + lnb


def _gmlp_fwd(z_big, ln_g, ln_b, w_s, b_s_t, name):
    s = z_big.shape[0]
    groups, ch, _ = w_s.shape
    gw = ln_g.shape[1]
    gd = gw // groups

    def body(z_ref, lng_ref, lnb_ref, ws_ref, bt_ref, a_ref):
        _, _, u, _, _, vn = _gmlp_norm(z_ref[...], lng_ref[...], lnb_ref[...], gw)
        mask = _causal_mask(ch)
        for g in range(groups):
            cols = slice(g * gd, (g + 1) * gd)
            wm = (ws_ref[g] * mask).astype(BF16)
            mixed = _dot(wm, vn[:, cols].astype(BF16), NN) + bt_ref[:, g:g + 1]
            a_ref[:, cols] = (u[:, cols] * mixed).astype(BF16)

    return pl.pallas_call(
        body, name=name, grid=(s // ch,),
        in_specs=[pl.BlockSpec((ch, 2 * gw), lambda n: (n, 0)), _row_spec(gw), _row_spec(gw),
                  pl.BlockSpec((groups, ch, ch), lambda n: (0, 0, 0)), pl.BlockSpec((ch, groups), lambda n: (0, 0))],
        out_specs=pl.BlockSpec((ch, gw), lambda n: (n, 0)),
        out_shape=jax.ShapeDtypeStruct((s, gw), BF16), compiler_params=_params(),
    )(z_big, ln_g, ln_b, w_s, b_s_t)


def _gmlp_bwd(z_big, da, dz_big, ln_g, ln_b, w_s, b_s_t, name, comm=None):
    s = z_big.shape[0]
    groups, ch, _ = w_s.shape
    gw = ln_g.shape[1]
    gd = gw // groups

    def body(z_ref, da_ref, dzin_ref, lng_ref, lnb_ref, ws_ref, bt_ref, dz_ref, gws_ref, gbt_ref, glng_ref, glnb_ref):
        del dzin_ref
        lng = lng_ref[...]
        u_pre, v_pre, u, vhat, rstd, vn = _gmlp_norm(z_ref[...], lng, lnb_ref[...], gw)
        da = da_ref[...]
        mask = _causal_mask(ch)
        first = pl.program_id(0) == 0
        dvn_parts = []
        lane = lax.broadcasted_iota(jnp.int32, (ch, LANES), 1)
        gb = jnp.zeros((ch, LANES), F32)
        for g in range(groups):
            cols = slice(g * gd, (g + 1) * gd)
            wm = (ws_ref[g] * mask).astype(BF16)
            vn_g = vn[:, cols].astype(BF16)
            mixed = _dot(wm, vn_g, NN) + bt_ref[:, g:g + 1]
            dz_ref[:, cols] = (da[:, cols] * mixed * _gelu_grad(u_pre[:, cols])).astype(BF16)
            dmixed = da[:, cols] * u[:, cols]
            dm16 = dmixed.astype(BF16)
            dvn_parts.append(_dot(wm, dm16, TN))
            gws = _dot(dm16, vn_g, NT) * mask

            @pl.when(first)
            def _(g=g, gws=gws):
                gws_ref[g] = gws

            @pl.when(jnp.logical_not(first))
            def _(g=g, gws=gws):
                gws_ref[g] += gws

            gb = gb + jnp.where(lane == g, jnp.sum(dmixed, axis=1, keepdims=True), 0.0)
        _accumulate(gbt_ref, gb)
        dvn = jnp.concatenate(dvn_parts, axis=1)
        _accumulate(glnb_ref, _colsum(dvn))
        _accumulate(glng_ref, _colsum(dvn * vhat))
        dvh = dvn * lng
        dvg = rstd * (dvh - _rowmean(dvh) - vhat * _rowmean(dvh * vhat))
        dz_ref[:, gw:] = (dvg * _gelu_grad(v_pre)).astype(BF16)

    zspec = pl.BlockSpec((ch, 2 * gw), lambda n: (n, 0))
    return _call(
        body, (z_big, da, dz_big, ln_g, ln_b, w_s, b_s_t), comm, name=name, grid=(s // ch,),
        in_specs=[zspec, pl.BlockSpec((ch, gw), lambda n: (n, 0)), pl.BlockSpec(memory_space=HBM),
                  _row_spec(gw), _row_spec(gw), pl.BlockSpec((groups, ch, ch), lambda n: (0, 0, 0)),
                  pl.BlockSpec((ch, groups), lambda n: (0, 0))],
        out_specs=[zspec, pl.BlockSpec((groups, ch, ch), lambda n: (0, 0, 0)),
                   pl.BlockSpec((ch, LANES), lambda n: (0, 0)), _row_spec(gw), _row_spec(gw)],
        out_shape=[jax.ShapeDtypeStruct(dz_big.shape, BF16), jax.ShapeDtypeStruct((groups, ch, ch), F32),
                   jax.ShapeDtypeStruct((ch, LANES), F32), jax.ShapeDtypeStruct((1, gw), F32),
                   jax.ShapeDtypeStruct((1, gw), F32)],
        input_output_aliases={2: 0})


def _mla_prep(z_lat, q_g, kv_g, rope_k, name):
    s, latw = z_lat.shape
    ql, kvl = q_g.shape[1], kv_g.shape[1]
    tb = _div_tile(s, 256, SUBLANES)

    def body(z_ref, qg_ref, kvg_ref, t_ref, qn_ref, kvn_ref, kr_ref):
        q = z_ref[:, :ql]
        qn_ref[...] = ((q * lax.rsqrt(_rowmean(q * q) + EPS)) * qg_ref[...]).astype(BF16)
        kv = z_ref[:, ql:ql + kvl]
        kvn_ref[...] = ((kv * lax.rsqrt(_rowmean(kv * kv) + EPS)) * kvg_ref[...]).astype(BF16)
        kk = z_ref[:, ql + kvl:] * t_ref[...]
        kr_ref[...] = (kk + pltpu.roll(kk, ROPE, axis=1)).astype(BF16)

    return pl.pallas_call(
        body, name=name, grid=(s // tb,),
        in_specs=[pl.BlockSpec((tb, latw), lambda i: (i, 0)), _row_spec(ql), _row_spec(kvl),
                  pl.BlockSpec((tb, 2 * ROPE), lambda i: (i, 0))],
        out_specs=[pl.BlockSpec((tb, ql), lambda i: (i, 0)), pl.BlockSpec((tb, kvl), lambda i: (i, 0)),
                   pl.BlockSpec((tb, 2 * ROPE), lambda i: (i, 0))],
        out_shape=[jax.ShapeDtypeStruct((s, ql), BF16), jax.ShapeDtypeStruct((s, kvl), BF16),
                   jax.ShapeDtypeStruct((s, 2 * ROPE), BF16)],
        compiler_params=_params(),
    )(z_lat, q_g, kv_g, rope_k)


def _scores(q, k, kr, on_diagonal):
    s = _dot(q[:, :NOPE], k, NT) + _dot(q[:, NOPE:], kr, NT)
    if not on_diagonal:
        return s
    rows = lax.broadcasted_iota(jnp.int32, s.shape, 0)
    cols = lax.broadcasted_iota(jnp.int32, s.shape, 1)
    return jnp.where(cols <= rows, s, -1e30)


def _attn_fwd(q, kv, kr, heads, name, comm=None):
    s = q.shape[0]
    t = _div_tile(s, 512)
    nb = s // t
    hp = 2 if heads % 2 == 0 else 1

    def body(q_ref, k_ref, kr_ref, v_ref, o_ref, lse_ref, m_ref, l_ref, acc_ref):
        i, j = pl.program_id(1), pl.program_id(2)

        @pl.when(j == 0)
        def _():
            m_ref[...] = jnp.full(m_ref.shape, -1e30, F32)
            l_ref[...] = jnp.zeros(l_ref.shape, F32)
            acc_ref[...] = jnp.zeros(acc_ref.shape, F32)

        def step(on_diagonal):
            krv = kr_ref[...]
            for h in range(hp):
                vc = slice(h * VHEAD, (h + 1) * VHEAD)
                sc = _scores(q_ref[:, h * HEAD_W:(h + 1) * HEAD_W], k_ref[:, h * NOPE:(h + 1) * NOPE], krv, on_diagonal)
                m_old = m_ref[h]
                m_new = jnp.maximum(m_old, jnp.max(sc, axis=-1, keepdims=True))
                p = jnp.exp(sc - m_new)
                alpha = jnp.exp(m_old - m_new)
                l_new = alpha * l_ref[h] + jnp.sum(p, axis=-1, keepdims=True)
                acc = alpha * acc_ref[:, vc] + _dot(p.astype(BF16), v_ref[:, vc], NN)
                if on_diagonal:
                    o_ref[:, vc] = (acc / l_new).astype(BF16)
                    lse_ref[h] = jnp.broadcast_to(m_new + jnp.log(l_new), (t, LANES))
                else:
                    m_ref[h], l_ref[h], acc_ref[:, vc] = m_new, l_new, acc

        pl.when(j < i)(lambda: step(False))
        pl.when(j == i)(lambda: step(True))

    kidx = lambda off: (lambda h, i, j: (jnp.minimum(i, j), off(h)))
    return _call(
        body, (q, kv, kr, kv), comm, name=name, grid=(heads // hp, nb, nb),
        in_specs=[pl.BlockSpec((t, hp * HEAD_W), lambda h, i, j: (i, h)),
                  pl.BlockSpec((t, hp * NOPE), kidx(lambda h: h)),
                  pl.BlockSpec((t, 2 * ROPE), kidx(lambda h: 0)),
                  pl.BlockSpec((t, hp * VHEAD), kidx(lambda h: heads // hp + h))],
        out_specs=[pl.BlockSpec((t, hp * VHEAD), lambda h, i, j: (i, h)),
                   pl.BlockSpec((hp, t, LANES), lambda h, i, j: (h, i, 0))],
        out_shape=[jax.ShapeDtypeStruct((s, heads * VHEAD), BF16), jax.ShapeDtypeStruct((heads, s, LANES), F32)],
        scratch_shapes=[pltpu.VMEM((hp, t, 1), F32), pltpu.VMEM((hp, t, 1), F32), pltpu.VMEM((t, hp * VHEAD), F32)])


def _attn_bwd(q, kv, kr, o, do, lse, heads, name, comm=None):
    s = q.shape[0]
    t = _div_tile(s, 512)
    nb = s // t
    hp = 2 if heads % 2 == 0 else 1

    def body(q_ref, k_ref, kr_ref, v_ref, o_ref, do_ref, lse_ref, dq_ref, dk_ref, dv_ref, dk_acc, dv_acc):
        j, i = pl.program_id(1), pl.program_id(2)

        @pl.when(jnp.logical_and(j == 0, i == 0))
        def _():
            dq_ref[...] = jnp.zeros(dq_ref.shape, F32)

        def step(on_diagonal):
            krv = kr_ref[...]
            rows = pl.ds(pl.multiple_of(i * t, t), t)
            for h in range(hp):
                qc, kc, vc = (slice(h * w, (h + 1) * w) for w in (HEAD_W, NOPE, VHEAD))
                qv, kn, do_v = q_ref[:, qc], k_ref[:, kc], do_ref[:, vc]
                p = jnp.exp(_scores(qv, kn, krv, on_diagonal) - lse_ref[h][:, :1])
                dp = _dot(do_v, v_ref[:, vc], NT)
                delta = jnp.sum(do_v.astype(F32) * o_ref[:, vc].astype(F32), axis=-1, keepdims=True)
                ds = (p * (dp - delta)).astype(BF16)
                dq_ref[rows, h * HEAD_W:h * HEAD_W + NOPE] += _dot(ds, kn, NN)
                dq_ref[rows, h * HEAD_W + NOPE:(h + 1) * HEAD_W] += _dot(ds, krv, NN)
                dv_part, dk_part = _dot(p.astype(BF16), do_v, TN), _dot(ds, qv, TN)
                if on_diagonal:
                    dv_acc[:, vc], dk_acc[:, qc] = dv_part, dk_part
                else:
                    dv_acc[:, vc] += dv_part
                    dk_acc[:, qc] += dk_part

        pl.when(i == j)(lambda: step(True))
        pl.when(i > j)(lambda: step(False))

        @pl.when(i == nb - 1)
        def _():
            dk_ref[...] = dk_acc[...].astype(BF16)
            dv_ref[...] = dv_acc[...].astype(BF16)

    qidx = lambda h, j, i: (jnp.maximum(i, j), h)
    return _call(
        body, (q, kv, kr, kv, o, do, lse), comm, name=name, grid=(heads // hp, nb, nb),
        in_specs=[pl.BlockSpec((t, hp * HEAD_W), qidx),
                  pl.BlockSpec((t, hp * NOPE), lambda h, j, i: (j, h)),
                  pl.BlockSpec((t, 2 * ROPE), lambda h, j, i: (j, 0)),
                  pl.BlockSpec((t, hp * VHEAD), lambda h, j, i: (j, heads // hp + h)),
                  pl.BlockSpec((t, hp * VHEAD), qidx), pl.BlockSpec((t, hp * VHEAD), qidx),
                  pl.BlockSpec((hp, t, LANES), lambda h, j, i: (h, jnp.maximum(i, j), 0))],
        out_specs=[pl.BlockSpec((s, hp * HEAD_W), lambda h, j, i: (0, h)),
                   pl.BlockSpec((t, hp * HEAD_W), lambda h, j, i: (j, h)),
                   pl.BlockSpec((t, hp * VHEAD), lambda h, j, i: (j, h))],
        out_shape=[jax.ShapeDtypeStruct((s, heads * HEAD_W), F32), jax.ShapeDtypeStruct((s, heads * HEAD_W), BF16),
                   jax.ShapeDtypeStruct((s, heads * VHEAD), BF16)],
        scratch_shapes=[pltpu.VMEM((t, hp * HEAD_W), F32), pltpu.VMEM((t, hp * VHEAD), F32)])


def _mla_bwd_mid(dq, dk, dv, rope_q, rope_k, heads, name):
    s = dq.shape[0]
    tb = _div_tile(s, 256, SUBLANES)

    def body(dq_ref, dk_ref, dv_ref, tq_ref, tk_ref, dqb_ref, dkv_ref, dkk_ref):
        tq = tq_ref[...]
        dkr = jnp.zeros((tb, 2 * ROPE), F32)
        for h in range(heads):
            cols = slice(h * HEAD_W, (h + 1) * HEAD_W)
            dqb_ref[:, cols] = (dq_ref[:, cols] * tq).astype(BF16)
            dkv_ref[:, h * NOPE:(h + 1) * NOPE] = dk_ref[:, h * HEAD_W:h * HEAD_W + NOPE]
            dkr = dkr + dk_ref[:, h * HEAD_W + NOPE:(h + 1) * HEAD_W].astype(F32)
        dkv_ref[:, heads * NOPE:] = dv_ref[...]
        dkk_ref[...] = (dkr + pltpu.roll(dkr, ROPE, axis=1)) * tk_ref[...]

    wq, wv = heads * HEAD_W, heads * VHEAD
    return pl.pallas_call(
        body, name=name, grid=(s // tb,),
        in_specs=[pl.BlockSpec((tb, wq), lambda i: (i, 0)), pl.BlockSpec((tb, wq), lambda i: (i, 0)),
                  pl.BlockSpec((tb, wv), lambda i: (i, 0)), pl.BlockSpec((tb, HEAD_W), lambda i: (i, 0)),
                  pl.BlockSpec((tb, 2 * ROPE), lambda i: (i, 0))],
        out_specs=[pl.BlockSpec((tb, wq), lambda i: (i, 0)), pl.BlockSpec((tb, heads * NOPE + wv), lambda i: (i, 0)),
                   pl.BlockSpec((tb, 2 * ROPE), lambda i: (i, 0))],
        out_shape=[jax.ShapeDtypeStruct((s, wq), BF16), jax.ShapeDtypeStruct((s, heads * NOPE + wv), BF16),
                   jax.ShapeDtypeStruct((s, 2 * ROPE), F32)],
        compiler_params=_params(),
    )(dq, dk, dv, rope_q, rope_k)


def _mla_bwd_post(z_lat, dqn, dkvn, dkk, q_g, kv_g, name):
    s, latw = z_lat.shape
    ql, kvl = q_g.shape[1], kv_g.shape[1]
    tb = _div_tile(s, 256, SUBLANES)

    def norm_bwd(xv, dn, g, dg_ref):
        r = lax.rsqrt(_rowmean(xv * xv) + EPS)
        xh = xv * r
        _accumulate(dg_ref, _colsum(dn * xh))
        dxh = dn * g
        return r * (dxh - xh * _rowmean(dxh * xh))

    def body(z_ref, dqn_ref, dkvn_ref, dkk_ref, qg_ref, kvg_ref, dz_ref, gq_ref, gkv_ref):
        dz_ref[:, :ql] = norm_bwd(z_ref[:, :ql], dqn_ref[...], qg_ref[...], gq_ref).astype(BF16)
        dz_ref[:, ql:ql + kvl] = norm_bwd(z_ref[:, ql:ql + kvl], dkvn_ref[...], kvg_ref[...], gkv_ref).astype(BF16)
        dz_ref[:, ql + kvl:] = dkk_ref[...].astype(BF16)

    return pl.pallas_call(
        body, name=name, grid=(s // tb,),
        in_specs=[pl.BlockSpec((tb, latw), lambda i: (i, 0)), pl.BlockSpec((tb, ql), lambda i: (i, 0)),
                  pl.BlockSpec((tb, kvl), lambda i: (i, 0)), pl.BlockSpec((tb, 2 * ROPE), lambda i: (i, 0)),
                  _row_spec(ql), _row_spec(kvl)],
        out_specs=[pl.BlockSpec((tb, latw), lambda i: (i, 0)), _row_spec(ql), _row_spec(kvl)],
        out_shape=[jax.ShapeDtypeStruct((s, latw), BF16), jax.ShapeDtypeStruct((1, ql), F32),
                   jax.ShapeDtypeStruct((1, kvl), F32)],
        compiler_params=_params(),
    )(z_lat, dqn, dkvn, dkk, q_g, kv_g)


def _shift_down(x, n):
    rows = lax.broadcasted_iota(jnp.int32, x.shape, 0)
    return jnp.where(rows >= n, pltpu.roll(x, n, axis=0), 0.0)


def _shift_up(x, n):
    s = x.shape[0]
    rows = lax.broadcasted_iota(jnp.int32, x.shape, 0)
    return jnp.where(rows < s - n, pltpu.roll(x, s - n, axis=0), 0.0)


def _conv(pre, w_ref, b_ref):
    return (w_ref[2:3, :] * pre + w_ref[1:2, :] * _shift_down(pre, 1) + w_ref[0:1, :] * _shift_down(pre, 2)
            + b_ref[...])


def _conv_fwd(up_pre, conv_w, conv_b, name):
    s, ff2 = up_pre.shape
    ff = ff2 // 2
    tc = _div_tile(ff, 256)
    nb = ff // tc

    def body(pg_ref, pv_ref, wg_ref, wv_ref, bg_ref, bv_ref, act_ref):
        gate = _conv(pg_ref[...].astype(F32), wg_ref, bg_ref)
        val = _conv(pv_ref[...].astype(F32), wv_ref, bv_ref)
        act_ref[...] = (gate * _sigmoid(gate) * val).astype(BF16)

    def col(rows, off):
        return pl.BlockSpec((rows, tc), lambda j: (0, j + off))

    return pl.pallas_call(
        body, name=name, grid=(nb,),
        in_specs=[col(s, 0), col(s, nb), col(CONV_TAPS, 0), col(CONV_TAPS, nb), col(1, 0), col(1, nb)],
        out_specs=col(s, 0), out_shape=jax.ShapeDtypeStruct((s, ff), BF16), compiler_params=_params(),
    )(up_pre, up_pre, conv_w, conv_w, conv_b, conv_b)


def _conv_bwd(up_pre, dact, conv_w, conv_b, name, comm=None):
    s, ff2 = up_pre.shape
    ff = ff2 // 2
    tc = _div_tile(ff, 256)
    nb = ff // tc

    def half(pre, dx, w_ref, dpre_ref, gw_ref, gb_ref):
        gb_ref[...] = _colsum(dx)
        gw_ref[0:1, :] = _colsum(dx * _shift_down(pre, 2))
        gw_ref[1:2, :] = _colsum(dx * _shift_down(pre, 1))
        gw_ref[2:3, :] = _colsum(dx * pre)
        dpre_ref[...] = (w_ref[2:3, :] * dx + w_ref[1:2, :] * _shift_up(dx, 1)
                         + w_ref[0:1, :] * _shift_up(dx, 2)).astype(BF16)

    def body(pg_ref, pv_ref, da_ref, wg_ref, wv_ref, bg_ref, bv_ref, dup_ref, gwg_ref, gwv_ref, gbg_ref, gbv_ref):
        pre_g, pre_v = pg_ref[...].astype(F32), pv_ref[...].astype(F32)
        gate = _conv(pre_g, wg_ref, bg_ref)
        val = _conv(pre_v, wv_ref, bv_ref)
        da = da_ref[...].astype(F32)
        sg = _sigmoid(gate)
        half(pre_v, da * gate * sg, wv_ref, dup_ref.at[1], gwv_ref, gbv_ref)
        half(pre_g, da * val * sg * (1.0 + gate * (1.0 - sg)), wg_ref, dup_ref.at[0], gwg_ref, gbg_ref)

    def col(rows, off):
        return pl.BlockSpec((rows, tc), lambda j: (0, j + off))

    return _call(
        body, (up_pre, up_pre, dact, conv_w, conv_w, conv_b, conv_b), comm, name=name, grid=(nb,),
        in_specs=[col(s, 0), col(s, nb), col(s, 0), col(CONV_TAPS, 0), col(CONV_TAPS, nb), col(1, 0), col(1, nb)],
        out_specs=[pl.BlockSpec((2, s, tc), lambda j: (0, 0, j)), col(CONV_TAPS, 0), col(CONV_TAPS, 0),
                   col(1, 0), col(1, 0)],
        out_shape=[jax.ShapeDtypeStruct((2, s, ff), BF16)] + [jax.ShapeDtypeStruct((CONV_TAPS, ff), F32)] * 2
        + [jax.ShapeDtypeStruct((1, ff), F32)] * 2)


def _ada_fwd(c_all, w, b, name):
    nseq, d = c_all.shape
    na = w.shape[1]
    tn = _div_tile(na, 512)

    def body(c_ref, w_ref, b_ref, o_ref):
        cv = c_ref[...]
        sc = cv * _sigmoid(cv)
        o_ref[...] = jnp.dot(sc, w_ref[...], preferred_element_type=F32, precision=lax.Precision.HIGHEST) + b_ref[...]

    return pl.pallas_call(
        body, name=name, grid=(na // tn,),
        in_specs=[pl.BlockSpec((nseq, d), lambda j: (0, 0)), pl.BlockSpec((d, tn), lambda j: (0, j)),
                  pl.BlockSpec((1, tn), lambda j: (0, j))],
        out_specs=pl.BlockSpec((nseq, tn), lambda j: (0, j)),
        out_shape=jax.ShapeDtypeStruct((nseq, na), F32), compiler_params=_params(),
    )(c_all, w, b)


def _ada_bwd(c_all_t, dmod, name):
    d, nseq = c_all_t.shape
    na = dmod.shape[1]
    tm, tn = _div_tile(d, 256, SUBLANES), _div_tile(na, 512)

    def body(c_ref, dm_ref, o_ref):
        cv = c_ref[...]
        sc = cv * _sigmoid(cv)
        acc = sc[:, 0:1] * dm_ref[0:1, :]
        for bi in range(1, nseq):
            acc = acc + sc[:, bi:bi + 1] * dm_ref[bi:bi + 1, :]
        o_ref[...] = acc

    return pl.pallas_call(
        body, name=name, grid=(d // tm, na // tn),
        in_specs=[pl.BlockSpec((tm, nseq), lambda i, j: (i, 0)), pl.BlockSpec((nseq, tn), lambda i, j: (0, j))],
        out_specs=pl.BlockSpec((tm, tn), lambda i, j: (i, j)),
        out_shape=jax.ShapeDtypeStruct((d, na), F32), compiler_params=_params(),
    )(c_all_t, dmod)


def _adamw(w, g, m, v, name, comm=None, after=None):
    rows, cols = w.shape
    tb = _div_tile(rows, max(SUBLANES, (256 * 1024) // cols // SUBLANES * SUBLANES), SUBLANES)
    c1 = 1.0 / (1.0 - ADAM_B1 ** ADAM_STEP)
    c2 = 1.0 / (1.0 - ADAM_B2 ** ADAM_STEP)

    def body(*refs):
        w_ref, g_ref, m_ref, v_ref = refs[:4]
        d_ref, nm_ref, nv_ref = refs[-3:]
        gv = g_ref[...]
        nm = ADAM_B1 * m_ref[...] + (1.0 - ADAM_B1) * gv
        nv = ADAM_B2 * v_ref[...] + (1.0 - ADAM_B2) * (gv * gv)
        nm_ref[...] = nm
        nv_ref[...] = nv
        d_ref[...] = -ADAM_LR * ((nm * c1) / (jnp.sqrt(nv * c2) + ADAM_EPS) + ADAM_WD * w_ref[...])

    blk = pl.BlockSpec((tb, cols), lambda i: (i, 0))
    operands, in_specs = (w, g, m, v), [blk] * 4
    if after is not None:
        operands, in_specs = operands + (after,), in_specs + [pl.BlockSpec(after.shape, lambda i: (0, 0))]
    return _call(body, operands, comm, name=name, grid=(rows // tb,), in_specs=in_specs, out_specs=[blk] * 3,
                 out_shape=[jax.ShapeDtypeStruct((rows, cols), F32)] * 3)


def _sum_leading(parts, name):
    n, rows, cols = parts.shape
    tb = _div_tile(rows, 512, SUBLANES)

    def body(p_ref, o_ref):
        acc = p_ref[0]
        for k in range(1, n):
            acc = acc + p_ref[k]
        o_ref[...] = acc

    return pl.pallas_call(
        body, name=name, grid=(rows // tb,), in_specs=[pl.BlockSpec((n, tb, cols), lambda i: (0, i, 0))],
        out_specs=pl.BlockSpec((tb, cols), lambda i: (i, 0)),
        out_shape=jax.ShapeDtypeStruct((rows, cols), F32), compiler_params=_params(),
    )(parts)


def _place():
    x, y, c = lax.axis_index("x"), lax.axis_index("y"), lax.axis_index("c")
    return x, y, c, [(1 - x, y), (x, 1 - y), (1 - x, 1 - y)]


def _all_gather(block, name):
    m_per, n = block.shape

    def body(x_ref, out_ref, send_sems, recv_sems, local_sem):
        x, y, c, chips = _place()
        me, sibling = (x, y, c), (x, y, 1 - c)

        def rows(px, py, pc):
            return out_ref.at[pl.ds((4 * px + 2 * py + pc) * m_per, m_per), :]

        def copy(k, blk, to, src=None):
            return pltpu.make_async_remote_copy(
                src_ref=rows(*blk) if src is None else src, dst_ref=rows(*blk), send_sem=send_sems.at[k],
                recv_sem=recv_sems.at[k], device_id=to, device_id_type=MESH)

        mine = pltpu.make_async_copy(x_ref, rows(*me), local_sem)
        mine.start()
        first = [copy(0, me, sibling, src=x_ref)]
        first += [copy(1 + j, me, (*chip, c), src=x_ref) for j, chip in enumerate(chips)]
        for cp in first:
            cp.start()
        passed = [copy(4 + j, (*chip, c), sibling) for j, chip in enumerate(chips)]
        for j, chip in enumerate(chips):
            copy(1 + j, (*chip, c), me).wait_recv()
            passed[j].start()
        copy(0, sibling, me).wait_recv()
        for j, chip in enumerate(chips):
            copy(4 + j, (*chip, 1 - c), me).wait_recv()
        for cp in first + passed:
            cp.wait_send()
        mine.wait()

    return pl.pallas_call(
        body, name=name, out_shape=jax.ShapeDtypeStruct((N_DEV * m_per, n), block.dtype),
        in_specs=[pl.BlockSpec(memory_space=pltpu.VMEM)], out_specs=pl.BlockSpec(memory_space=pltpu.VMEM),
        scratch_shapes=[pltpu.SemaphoreType.DMA((7,)), pltpu.SemaphoreType.DMA((7,)), pltpu.SemaphoreType.DMA],
        compiler_params=_params(),
    )(block)


def _hbm_specs(n):
    return [pl.BlockSpec(memory_space=HBM)] * n


def _half_rows(ref, half, lead=None):
    h = ref.shape[-2] // 2
    rows = pl.ds(pl.multiple_of(half * h, 2 * SUBLANES), h)
    return ref.at[rows, :] if lead is None else ref.at[lead, rows, :]


class _Comm:
    def __init__(self, operands, out_shape, sem_dims, build, aliases=None):
        self.operands, self.out_shape, self.sem_dims = list(operands), list(out_shape), list(sem_dims)
        self.scratch = [pltpu.SemaphoreType.DMA(d) for d in sem_dims]
        self.build, self.aliases = build, dict(aliases or {})


class _SemGrid:
    def __init__(self, sems, dims):
        self.sems, self.dims, self.at = list(sems), tuple(dims), self

    def __getitem__(self, index):
        index = index if isinstance(index, tuple) else (index,)
        flat = 0
        for i, d in zip(index, self.dims):
            flat = flat * d + i
        return self.sems[flat]


def _call(body, operands, comm=None, *, name, grid, in_specs, out_specs, out_shape, scratch_shapes=(),
          input_output_aliases=None):
    aliases = dict(input_output_aliases or {})
    if comm is None:
        return pl.pallas_call(
            body, name=name, grid=grid, in_specs=in_specs, out_specs=out_specs, out_shape=out_shape,
            scratch_shapes=list(scratch_shapes), input_output_aliases=aliases, compiler_params=_params())(*operands)
    single = not isinstance(out_shape, (list, tuple))
    outs = [out_shape] if single else list(out_shape)
    ospecs = [out_specs] if single else list(out_specs)
    n_in, n_out, n_scr = len(operands), len(outs), len(scratch_shapes)
    c_in, c_out = len(comm.operands), len(comm.out_shape)
    for i, o in comm.aliases.items():
        aliases[n_in + i] = n_out + o

    def hosted(*refs):
        ins, c_ins = refs[:n_in], refs[n_in:n_in + c_in]
        o0 = n_in + c_in
        o_refs, c_outs = refs[o0:o0 + n_out], refs[o0 + n_out:o0 + n_out + c_out]
        s0 = o0 + n_out + c_out
        scr, sems = refs[s0:s0 + n_scr], refs[s0 + n_scr:]
        start, finish = comm.build(c_ins, c_outs, sems)
        first = last = None
        for dim, size in enumerate(grid):
            at0, at1 = pl.program_id(dim) == 0, pl.program_id(dim) == size - 1
            first = at0 if first is None else jnp.logical_and(first, at0)
            last = at1 if last is None else jnp.logical_and(last, at1)
        pl.when(first)(start)
        body(*ins, *o_refs, *scr)
        pl.when(last)(finish)

    res = pl.pallas_call(
        hosted, name=name, grid=grid, in_specs=list(in_specs) + _hbm_specs(c_in),
        out_specs=ospecs + _hbm_specs(c_out), out_shape=outs + comm.out_shape,
        scratch_shapes=list(scratch_shapes) + comm.scratch, input_output_aliases=aliases,
        compiler_params=_params())(*operands, *comm.operands)
    return (res[0] if single else res[:n_out]), res[n_out:]


def _run_comm(comm, name):
    c_in, c_out = len(comm.operands), len(comm.out_shape)

    def body(*refs):
        start, finish = comm.build(refs[:c_in], refs[c_in:c_in + c_out], refs[c_in + c_out:])
        start()
        finish()

    return pl.pallas_call(
        body, name=name, in_specs=_hbm_specs(c_in), out_specs=_hbm_specs(c_out), out_shape=comm.out_shape,
        scratch_shapes=comm.scratch, input_output_aliases=comm.aliases, compiler_params=_params())(*comm.operands)


def _gather_comm(shards):
    nw = len(shards)

    def build(in_refs, out_refs, sems):
        send_sems, recv_sems = sems
        x, y, c, chips = _place()
        me, sibling = (x, y, c), (x, y, 1 - c)

        def copy(w, k, block, half, to, src=None):
            dst = _half_rows(out_refs[w], half, 2 * block[0] + block[1])
            return pltpu.make_async_remote_copy(
                src_ref=dst if src is None else src, dst_ref=dst, send_sem=send_sems.at[w, k],
                recv_sem=recv_sems.at[w, k], device_id=to, device_id_type=MESH)

        first = [copy(w, j, (x, y), c, (*chip, c), src=_half_rows(in_refs[w], c))
                 for w in range(nw) for j, chip in enumerate(chips)]

        def start():
            for cp in first:
                cp.start()

        def finish():
            passed = []
            for w in range(nw):
                for j, chip in enumerate(chips):
                    copy(w, j, chip, c, me).wait_recv()
                    passed.append(copy(w, 3 + j, chip, c, sibling))
                    passed[-1].start()
            for w in range(nw):
                for j, chip in enumerate(chips):
                    copy(w, 3 + j, chip, 1 - c, me).wait_recv()
            for cp in first + passed:
                cp.wait_send()

        return start, finish

    return _Comm(shards, [jax.ShapeDtypeStruct((N_CHIPS,) + w.shape, w.dtype) for w in shards],
                 [(nw, 6), (nw, 6)], build)


def _swap_comm(gs):
    nw = len(gs)

    def build(in_refs, out_refs, sems):
        send_sems, recv_sems = sems
        x, y, c, _ = _place()
        cps = []
        for w in range(nw):
            h = in_refs[w].shape[1] // 2
            src = in_refs[w].at[:, pl.ds(pl.multiple_of((1 - c) * h, 2 * SUBLANES), h), :]
            cps.append(pltpu.make_async_remote_copy(
                src_ref=src, dst_ref=out_refs[w], send_sem=send_sems.at[w], recv_sem=recv_sems.at[w],
                device_id=(x, y, 1 - c), device_id_type=MESH))

        def start():
            for cp in cps:
                cp.start()

        def finish():
            for cp in cps:
                cp.wait()

        return start, finish

    return _Comm(gs, [jax.ShapeDtypeStruct((N_CHIPS, g.shape[1] // 2, g.shape[2]), g.dtype) for g in gs],
                 [(nw,), (nw,)], build)


def _exchange_comm(s1s):
    nw = len(s1s)

    def build(in_refs, out_refs, sems):
        send_sems, recv_sems = sems
        x, y, c, chips = _place()
        cps = [pltpu.make_async_remote_copy(
            src_ref=in_refs[w].at[2 * chip[0] + chip[1]], dst_ref=out_refs[w].at[j], send_sem=send_sems.at[w, j],
            recv_sem=recv_sems.at[w, j], device_id=(*chip, c), device_id_type=MESH)
            for w in range(nw) for j, chip in enumerate(chips)]

        def start():
            for cp in cps:
                cp.start()

        def finish():
            for cp in cps:
                cp.wait()

        return start, finish

    return _Comm(s1s, [jax.ShapeDtypeStruct((N_CHIPS - 1,) + s.shape[1:], s.dtype) for s in s1s],
                 [(nw, 3), (nw, 3)], build)


def _size(dims):
    n = 1
    for d in dims:
        n *= d
    return n


def _sem_grids(comm, sem_refs):
    grids, pos = [], 0
    for dims in comm.sem_dims:
        grids.append(_SemGrid(sem_refs[pos:pos + _size(dims)], dims))
        pos += _size(dims)
    return grids


def _comm_split_start(comm, name):
    c_in, c_out = len(comm.operands), len(comm.out_shape)
    counts = [_size(d) for d in comm.sem_dims]
    n_sem = sum(counts)
    assert not comm.aliases

    def body(*refs):
        srcs, lands = refs[:c_in], refs[c_in:c_in + c_out]
        start, _ = comm.build(srcs, lands, _sem_grids(comm, refs[c_in + c_out:c_in + c_out + n_sem]))
        start()
        refs[-1][...] = jnp.zeros(refs[-1].shape, refs[-1].dtype)

    lands = [pltpu.with_memory_space_constraint(lax.empty(o.shape, o.dtype), HBM) for o in comm.out_shape]
    srcs = [pltpu.with_memory_space_constraint(a, HBM) for a in comm.operands]
    res = pl.pallas_call(
        body, name=name, in_specs=_hbm_specs(c_in + c_out),
        out_specs=[pl.BlockSpec(memory_space=pltpu.SEMAPHORE)] * n_sem + _hbm_specs(c_in + c_out)
        + [pl.BlockSpec(memory_space=pltpu.VMEM)],
        out_shape=[pltpu.SemaphoreType.DMA(())] * n_sem + [pltpu.HBM(a.shape, a.dtype) for a in comm.operands]
        + [pltpu.HBM(o.shape, o.dtype) for o in comm.out_shape] + [jax.ShapeDtypeStruct((SUBLANES, LANES), F32)],
        input_output_aliases={i: n_sem + i for i in range(c_in + c_out)},
        compiler_params=_params(has_side_effects=pltpu.SideEffectType.DATAFLOW_SIDE_EFFECTING))(*srcs, *lands)
    return res[:-1], res[-1]


def _comm_split_wait(comm, state, after, name):
    c_in, c_out, n_sem = len(comm.operands), len(comm.out_shape), sum(_size(d) for d in comm.sem_dims)
    sems, srcs, lands = state[:n_sem], state[n_sem:n_sem + c_in], state[n_sem + c_in:]

    def body(*refs):
        src_refs, land_refs = refs[:c_in], refs[c_in:c_in + c_out]
        _, finish = comm.build(src_refs, land_refs, _sem_grids(comm, refs[c_in + c_out:c_in + c_out + n_sem]))
        finish()

    sem_spec = pl.BlockSpec(memory_space=pltpu.SEMAPHORE)
    res = pl.pallas_call(
        body, name=name, in_specs=_hbm_specs(c_in + c_out) + [sem_spec] * n_sem + [pl.BlockSpec(memory_space=pl.ANY)],
        out_specs=_hbm_specs(c_in + c_out),
        out_shape=[pltpu.HBM(a.shape, a.dtype) for a in srcs] + [pltpu.HBM(o.shape, o.dtype) for o in lands],
        input_output_aliases={i: i for i in range(c_in + c_out)},
        compiler_params=_params(has_side_effects=pltpu.SideEffectType.DATAFLOW_SIDE_EFFECTING),
    )(*srcs, *lands, *sems, after)
    return res[:c_in], res[c_in:]


def _share_comm(fs):
    nw = len(fs)

    def build(in_refs, out_refs, sems):
        del in_refs
        send_sems, recv_sems = sems
        x, y, c, _ = _place()

        def copy(w, half):
            rows = _half_rows(out_refs[w], half)
            return pltpu.make_async_remote_copy(
                src_ref=rows, dst_ref=rows, send_sem=send_sems.at[w], recv_sem=recv_sems.at[w],
                device_id=(x, y, 1 - c), device_id_type=MESH)

        sends = [copy(w, c) for w in range(nw)]

        def start():
            for cp in sends:
                cp.start()

        def finish():
            for w in range(nw):
                copy(w, 1 - c).wait_recv()
            for cp in sends:
                cp.wait_send()

        return start, finish

    return _Comm(fs, [jax.ShapeDtypeStruct(f.shape, f.dtype) for f in fs],
                 [(nw,), (nw,)], build,
                 aliases={w: w for w in range(nw)})


def _add_sibling(g, r1, place, name):
    nch, h, cols = r1.shape
    tr = _div_tile(h, 256, 2 * SUBLANES)
    nb = h // tr

    def body(place_ref, g_ref, r_ref, o_ref):
        del place_ref
        o_ref[...] = (g_ref[...].astype(F32) + r_ref[...].astype(F32)).astype(BF16)

    spec = pltpu.PrefetchScalarGridSpec(
        num_scalar_prefetch=1, grid=(nch, nb),
        in_specs=[pl.BlockSpec((None, tr, cols), lambda k, i, p: (k, p[0] * nb + i, 0)),
                  pl.BlockSpec((None, tr, cols), lambda k, i, p: (k, i, 0))],
        out_specs=pl.BlockSpec((None, tr, cols), lambda k, i, p: (k, i, 0)))
    return pl.pallas_call(body, name=name, grid_spec=spec, out_shape=jax.ShapeDtypeStruct((nch, h, cols), BF16),
                          compiler_params=_params())(place, g, r1)


def _add_chips(s1, r2, place, name):
    _, h, cols = s1.shape
    tr = _div_tile(h, 256, 2 * SUBLANES)
    nb = h // tr

    def body(place_ref, s_ref, r_ref, o_ref):
        del place_ref
        acc = s_ref[...].astype(F32)
        for j in range(N_CHIPS - 1):
            acc = acc + r_ref[j].astype(F32)
        o_ref[...] = acc

    spec = pltpu.PrefetchScalarGridSpec(
        num_scalar_prefetch=1, grid=(nb,),
        in_specs=[pl.BlockSpec((None, tr, cols), lambda i, p: (p[1], i, 0)),
                  pl.BlockSpec((N_CHIPS - 1, tr, cols), lambda i, p: (0, i, 0))],
        out_specs=pl.BlockSpec((tr, cols), lambda i, p: (p[0] * nb + i, 0)))
    return pl.pallas_call(body, name=name, grid_spec=spec, out_shape=jax.ShapeDtypeStruct((2 * h, cols), F32),
                          compiler_params=_params())(place, s1, r2)


def _quarter_turn(m):
    h = m.shape[-1] // 2
    return jnp.concatenate([-m[..., h:], m[..., :h]], axis=-1)


def _quarter_turn_back(m):
    h = m.shape[-1] // 2
    return jnp.concatenate([m[..., h:], -m[..., :h]], axis=-1)


def _join_cols(sh):
    return jnp.concatenate([sh[k] for k in range(N_CHIPS)], axis=1)


def _split_cols(full):
    c = full.shape[1] // N_CHIPS
    return jnp.stack([full[:, k * c:(k + 1) * c] for k in range(N_CHIPS)])


def kernel(x, c, positions, w_ada, b_ada, pre_norm1_g, w_in, gm_ln_g, gm_ln_b, gm_w_s, gm_b_s, w_branch_a, q_norm_g, w_uq, kv_norm_g, w_ukv, w_branch_b, w_out, post_norm1_g, pre_norm2_g, w_up, conv_w, conv_b, w_down, post_norm2_g, loss_target, m_w_ada, m_b_ada, m_pre_norm1_g, m_w_in, m_gm_ln_g, m_gm_ln_b, m_gm_w_s, m_gm_b_s, m_w_branch_a, m_q_norm_g, m_w_uq, m_kv_norm_g, m_w_ukv, m_w_branch_b, m_w_out, m_post_norm1_g, m_pre_norm2_g, m_w_up, m_conv_w, m_conv_b, m_w_down, m_post_norm2_g, v_w_ada, v_b_ada, v_pre_norm1_g, v_w_in, v_gm_ln_g, v_gm_ln_b, v_gm_w_s, v_gm_b_s, v_w_branch_a, v_q_norm_g, v_w_uq, v_kv_norm_g, v_w_ukv, v_w_branch_b, v_w_out, v_post_norm1_g, v_pre_norm2_g, v_w_up, v_conv_w, v_conv_b, v_w_down, v_post_norm2_g):
    given = dict(locals())
    s, d = x.shape[1], x.shape[2]
    gw = gm_ln_g.shape[0]
    ql, kvl = q_norm_g.shape[0], kv_norm_g.shape[0]
    heads = N_CHIPS * w_uq.shape[1] // (NOPE + ROPE)
    ff = N_CHIPS * w_down.shape[0]
    assert gw == d and N_CHIPS * w_ukv.shape[1] == heads * (NOPE + VHEAD)
    ix, iy, ic = lax.axis_index("x"), lax.axis_index("y"), lax.axis_index("c")
    chip = 2 * ix + iy
    dev = 2 * chip + ic
    row = lambda v: v.reshape(1, -1)

    c_all = _all_gather(jnp.pad(c, ((0, SUBLANES - 1), (0, 0))), "gather_c").reshape(N_DEV, SUBLANES, d)[:, 0]
    na = w_ada.shape[1]
    b_ada_mine = lax.dynamic_slice(b_ada, (chip * na,), (na,))
    mod_cols = _ada_fwd(c_all, w_ada, row(b_ada_mine), "ada_fwd")
    mod_all = _all_gather(mod_cols, "gather_mod").reshape(N_CHIPS, N_CORES, N_DEV, na)[:, 0]
    mod = lax.dynamic_index_in_dim(mod_all, dev, axis=1, keepdims=False).reshape(N_MOD, d)
    shift1, scale1, gate1, shift2, scale2, gate2 = (mod[i:i + 1] for i in range(N_MOD))

    mine = {n: given[n].astype(BF16) for n in BIG}
    gather = lambda names: _gather_comm([mine[n] for n in names])
    whole = lambda n, g: lax.dynamic_update_slice(g, mine[n][None], (chip, 0, 0))
    rows4 = lambda sh4: sh4.reshape(-1, sh4.shape[2])
    wi = _join_cols(whole("w_in", _run_comm(gather(["w_in"]), "gather_w_in")[0]))
    o_q, o_kv, o_pe, o_ga = 2 * gw, 2 * gw + ql, 2 * gw + ql + kvl, 2 * gw + ql + kvl + ROPE
    w_in_big = jnp.concatenate([wi[:, :o_q], wi[:, o_ga:]], axis=1)
    w_in_lat = jnp.concatenate([wi[:, o_q:o_ga], _quarter_turn(wi[:, o_pe:o_ga])], axis=1)

    inv = ROPE_THETA ** (-jnp.arange(0, ROPE, 2, dtype=F32) / ROPE)
    ang = positions[0].astype(F32)[:, None] * inv
    cos, sin = jnp.cos(ang), jnp.sin(ang)
    rope_k = jnp.concatenate([cos, cos, sin, sin], axis=1)
    softmax_scale = float(NOPE + ROPE) ** -0.5
    rope_q = jnp.concatenate([jnp.ones((s, NOPE), F32), rope_k], axis=1) * softmax_scale

    x2d, tgt = x[0], loss_target[0]
    g_pre1, g_post1, g_pre2, g_post2 = row(pre_norm1_g), row(post_norm1_g), row(pre_norm2_g), row(post_norm2_g)
    ln_g, ln_b, q_g, kv_g = row(gm_ln_g), row(gm_ln_b), row(q_norm_g), row(kv_norm_g)
    b_s_t = gm_b_s.T
    conv_wf = _all_gather(jnp.pad(conv_w, ((0, SUBLANES - CONV_TAPS), (0, 0))), "gather_conv_w")
    conv_wf = conv_wf.reshape(N_CHIPS, N_CORES, SUBLANES, conv_w.shape[1])[:, 0, :CONV_TAPS]
    conv_wf = conv_wf.transpose(1, 0, 2).reshape(CONV_TAPS, 2 * ff)
    conv_bf = row(conv_b)

    h1 = _prenorm(x2d, g_pre1, scale1, shift1, "prenorm1")
    z_big, (g_uq, g_ukv, g_a) = _matmul(h1, w_in_big, mode="nn", out_dtype=F32, name="mm_z_big", tm=s,
                                        comm=gather(["w_uq", "w_ukv", "w_branch_a"]))
    wq = _join_cols(whole("w_uq", g_uq)).reshape(ql, heads, NOPE + ROPE)
    w_q = jnp.concatenate([wq, _quarter_turn(wq[:, :, NOPE:])], axis=2).reshape(ql, heads * HEAD_W)
    w_kv = _join_cols(whole("w_ukv", g_ukv)).reshape(kvl, heads, 2, NOPE).transpose(0, 2, 1, 3)
    w_kv = w_kv.reshape(kvl, 2 * heads * NOPE)
    w_a = rows4(whole("w_branch_a", g_a))
    z_lat = _matmul(h1, w_in_lat, mode="nn", out_dtype=F32, name="mm_z_lat", tm=s, tn=1024)
    a_act = _gmlp_fwd(z_big, ln_g, ln_b, gm_w_s, b_s_t, "gmlp_fwd")
    qn, kvn, kr = _mla_prep(z_lat, q_g, kv_g, rope_k, "mla_prep")
    q_rot = _matmul(qn, w_q, mode="nn", out_dtype=BF16, name="mm_q", tm=s, tn=HEAD_W, mul=rope_q)
    kv_all = _matmul(kvn, w_kv, mode="nn", out_dtype=BF16, name="mm_kv", tm=s, tn=1024)
    (o_att, lse), (g_b, g_o, g_up) = _attn_fwd(q_rot, kv_all, kr, heads, "attn_fwd",
                                               comm=gather(["w_branch_b", "w_out", "w_up"]))
    w_b, w_o, w_upf = rows4(whole("w_branch_b", g_b)), rows4(whole("w_out", g_o)), whole("w_up", g_up)
    y_a = _matmul(a_act, w_a, mode="nn", out_dtype=F32, name="mm_y_a", tm=s)
    y_b = _matmul(o_att, w_b, mode="nn", out_dtype=F32, name="mm_y_b", tm=s)
    merged = _merge(z_big, y_a, y_b, "merge")
    y1 = _matmul(merged, w_o, mode="nn", out_dtype=F32, name="mm_y1", tm=s)
    x1, h2 = _post_pre(x2d, y1, gate1, g_post1, g_pre2, scale2, shift2, "post1_pre2")

    up_pre, (g_dn,) = _matmul(h2, w_upf, mode="nn", out_dtype=BF16, name="mm_up", tm=s, tn=1408,
                              comm=gather(["w_down"]))
    w_dn = rows4(whole("w_down", g_dn))
    act = _conv_fwd(up_pre, conv_wf, conv_bf, "conv_fwd")
    ffn = _matmul(act, w_dn, mode="nn", out_dtype=F32, name="mm_ffn", tm=s, tk=1408)

    dffn, dgate2, g_post2_grad, dx2, loss_part = _post_bwd(ffn, gate2, g_post2, "post2_bwd", xin=x1, target=tgt)
    loss = lax.psum(loss_part[0, 0], ("x", "y", "c"))
    place = jnp.stack([ic, chip]).astype(jnp.int32)
    rows_of = lambda g: g.reshape(N_CHIPS, g.shape[0] // N_CHIPS, g.shape[1])
    add_sibling = lambda names, gs, r1s: [_add_sibling(g, r1, place, "rs_add_sibling_" + n)
                                          for n, g, r1 in zip(names, gs, r1s)]
    add_chips = lambda names, s1s, r2s: [_add_chips(s1, r2, place, "rs_add_chips_" + n)
                                         for n, s1, r2 in zip(names, s1s, r2s)]
    dact = _matmul(dffn, w_dn, mode="nt", out_dtype=BF16, name="mm_dact", tm=s)
    gp_down = [rows_of(_matmul(act, dffn, mode="tn", out_dtype=BF16, name="mm_gw_down", tn=1024, tk=s))]
    (dup, gcw_g, gcw_v, gcb_g, gcb_v), r1_down = _conv_bwd(up_pre, dact, conv_wf, conv_bf, "conv_bwd",
                                                            comm=_swap_comm(gp_down))
    s1_down = add_sibling(["w_down"], gp_down, r1_down)
    dh2, r2_down = _matmul(dup, w_upf, mode="nt", out_dtype=F32, name="mm_dh2", tm=s, tk=1408,
                           comm=_exchange_comm(s1_down))
    half_down = add_chips(["w_down"], s1_down, r2_down)
    gw_up = _matmul(h2, dup, mode="tn", out_dtype=BF16, name="mm_gw_up", tn=1408, tk=s, out_groups=N_CHIPS)
    dx1, dshift2, dscale2, g_pre2_grad = _prenorm_bwd(x1, dh2, dx2, g_pre2, scale2, "prenorm2_bwd")

    dy1, dgate1, g_post1_grad = _post_bwd(y1, gate1, g_post1, "post1_bwd", dxo=dx1)
    dmerged = _matmul(dy1, w_o, mode="nt", out_dtype=F32, name="mm_dmerged", tm=s)
    gw_out = _matmul(merged, dy1, mode="tn", out_dtype=BF16, name="mm_gw_out", tn=1024, tk=s)
    dy_a, dy_b, dz_big = _merge_bwd(dmerged, z_big, y_a, y_b, "merge_bwd")
    da = _matmul(dy_a, w_a, mode="nt", out_dtype=F32, name="mm_da", tm=s)
    gw_a = _matmul(a_act, dy_a, mode="tn", out_dtype=BF16, name="mm_gw_a", tn=1024, tk=s)
    do = _matmul(dy_b, w_b, mode="nt", out_dtype=BF16, name="mm_do", tm=s)
    gw_b = _matmul(o_att, dy_b, mode="tn", out_dtype=BF16, name="mm_gw_b", tn=1024, tk=s)
    mid = ["w_up", "w_out", "w_branch_a", "w_branch_b"]
    gp_mid = [gw_up, rows_of(gw_out), rows_of(gw_a), rows_of(gw_b)]
    (dz_big, g_ws, g_bs_t, g_ln_g, g_ln_b), r1_mid = _gmlp_bwd(z_big, da, dz_big, ln_g, ln_b, gm_w_s, b_s_t,
                                                                "gmlp_bwd", comm=_swap_comm(gp_mid))
    s1_mid = add_sibling(mid, gp_mid, r1_mid)
    (dq, dk, dv), r2_up_out = _attn_bwd(q_rot, kv_all, kr, o_att, do, lse, heads, "attn_bwd",
                                        comm=_exchange_comm(s1_mid[:2]))
    dq_big, dkv, dkk = _mla_bwd_mid(dq, dk, dv, rope_q, rope_k, heads, "mla_bwd_mid")
    gw_q = _matmul(qn, dq_big, mode="tn", out_dtype=F32, name="mm_gw_q", tn=1024, tk=s)
    dqn = _matmul(dq_big, w_q, mode="nt", out_dtype=F32, name="mm_dqn", tm=s, tk=1024)
    gw_kv = _matmul(kvn, dkv, mode="tn", out_dtype=BF16, name="mm_gw_kv", tn=1024, tk=s)
    dkvn = _matmul(dkv, w_kv, mode="nt", out_dtype=F32, name="mm_dkvn", tm=s, tk=1024)
    dz_lat, g_q, g_kv = _mla_bwd_post(z_lat, dqn, dkvn, dkk, q_g, kv_g, "mla_bwd_post")
    dh1, r2_a_b = _matmul(dz_big, w_in_big, mode="nt", out_dtype=F32, name="mm_dh1_big", tm=s,
                          comm=_exchange_comm(s1_mid[2:]))
    half_mid = add_chips(mid, s1_mid, list(r2_up_out) + list(r2_a_b))
    dh1 = _matmul(dz_lat, w_in_lat, mode="nt", out_dtype=F32, name="mm_dh1_lat", tm=s, tk=1024, add=dh1)
    gw_in_big, shared = _matmul(h1, dz_big, mode="tn", out_dtype=BF16, name="mm_gw_in_big", tn=1024, tk=s,
                                comm=_share_comm(half_down + half_mid))
    grads = dict(zip(["w_down"] + mid, shared))
    gw_in_lat = _matmul(h1, dz_lat, mode="tn", out_dtype=F32, name="mm_gw_in_lat", tn=1024, tk=s)

    gq = gw_q.reshape(ql, heads, HEAD_W)
    gq_pe = gq[:, :, NOPE:NOPE + ROPE] + _quarter_turn_back(gq[:, :, NOPE + ROPE:])
    g_pe = gw_in_lat[:, ql + kvl:ql + kvl + ROPE] + _quarter_turn_back(gw_in_lat[:, ql + kvl + ROPE:])
    last = ["w_in", "w_uq", "w_ukv"]
    gp_last = [
        _split_cols(jnp.concatenate([gw_in_big[:, :o_q], gw_in_lat[:, :ql + kvl].astype(BF16), g_pe.astype(BF16),
                                     gw_in_big[:, o_q:]], axis=1)),
        _split_cols(jnp.concatenate([gq[:, :, :NOPE], gq_pe], axis=2).reshape(ql, heads * (NOPE + ROPE)).astype(BF16)),
        _split_cols(gw_kv.reshape(kvl, 2, heads, NOPE).transpose(0, 2, 1, 3).reshape(kvl, heads * 2 * NOPE)),
    ]
    (grad_x, dshift1, dscale1, g_pre1_grad), r1_last = _prenorm_bwd(x2d, dh1, dx1, g_pre1, scale1, "prenorm1_bwd",
                                                                    comm=_swap_comm(gp_last))
    s1_last = add_sibling(last, gp_last, r1_last)

    dmod = jnp.concatenate([dshift1, dscale1, dgate1, dshift2, dscale2, dgate2], axis=1)
    dmod_all = _all_gather(jnp.pad(dmod, ((0, SUBLANES - 1), (0, 0))), "gather_dmod")
    dmod_all = dmod_all.reshape(N_DEV, SUBLANES, N_MOD * d)[:, 0]
    grad_b_ada = _sum_leading(dmod_all.reshape(N_DEV, 1, N_MOD * d), "sum_b_ada")[0]
    dmod_mine = lax.dynamic_slice(dmod_all, (0, chip * na), (N_DEV, na))
    grads["w_ada"] = _ada_bwd(c_all.T, dmod_mine, "ada_bwd")

    partial = {
        "pre_norm1_g": g_pre1_grad, "gm_ln_g": g_ln_g, "gm_ln_b": g_ln_b, "gm_w_s": g_ws, "gm_b_s": g_bs_t[:, :gm_b_s.shape[0]].T,
        "q_norm_g": g_q, "kv_norm_g": g_kv, "post_norm1_g": g_post1_grad, "pre_norm2_g": g_pre2_grad,
        "conv_w": jnp.concatenate([gcw_g, gcw_v], axis=1), "conv_b": jnp.concatenate([gcb_g, gcb_v], axis=1),
        "post_norm2_g": g_post2_grad,
    }
    flat = jnp.concatenate([partial[n].reshape(-1) for n in SMALL_PARTIAL])
    n_small = flat.shape[0]
    rows_small = -(-n_small // (LANES * SUBLANES)) * SUBLANES
    flat = jnp.pad(flat, (0, rows_small * LANES - n_small)).reshape(rows_small, LANES)
    small_sum = _sum_leading(_all_gather(flat, "gather_small").reshape(N_DEV, rows_small, LANES), "sum_small")
    small_sum = small_sum.reshape(-1)
    off = 0
    for n in SMALL_PARTIAL:
        shape = (CONV_TAPS, 2 * ff) if n == "conv_w" else given[n].shape
        size = partial[n].size
        grads[n] = small_sum[off:off + size].reshape(shape)
        off += size
    grads["conv_w"] = lax.dynamic_slice(grads["conv_w"], (0, chip * conv_w.shape[1]), conv_w.shape)
    grads["b_ada"] = grad_b_ada

    delta, new_m, new_v = {}, {}, {}

    def adamw(n, after=None):
        turn = (lambda a: a.T) if n == "w_in" else (lambda a: a)
        g_t = turn(grads[n])
        outs = _adamw(turn(given[n]), g_t, turn(given["m_" + n]), turn(given["v_" + n]), "adamw_" + n, after=after)
        grads[n] = turn(g_t)
        delta[n], new_m[n], new_v[n] = (turn(o) for o in outs)

    exchange_last = _exchange_comm(s1_last)
    in_flight, token = _comm_split_start(exchange_last, "rs_exchange_last_start")
    for n in ["w_ada", "w_down"] + mid:
        adamw(n, after=token)
    s1_last, r2_last = _comm_split_wait(exchange_last, in_flight, delta[mid[-1]], "rs_exchange_last_wait")
    half_last = add_chips(last, s1_last, r2_last)
    grads.update(zip(last, _run_comm(_share_comm(half_last), "rs_share_last")))
    for n in last:
        adamw(n)

    def small_pack(prefix, source):
        v = jnp.concatenate([source[prefix + n].reshape(-1) for n in SMALL])
        rows = -(-v.shape[0] // (LANES * SUBLANES)) * SUBLANES
        return jnp.pad(v, (0, rows * LANES - v.shape[0])).reshape(rows, LANES)

    outs = _adamw(small_pack("", given), small_pack("", grads), small_pack("m_", given), small_pack("v_", given),
                  "adamw_small")
    off = 0
    for n in SMALL:
        size = given[n].size
        for store, packed_out in zip((delta, new_m, new_v), outs):
            store[n] = packed_out.reshape(-1)[off:off + size].reshape(given[n].shape)
        off += size

    return (loss, grad_x[None], *[grads[n] for n in WEIGHTS], *[delta[n] for n in WEIGHTS],
            *[new_m[n] for n in WEIGHTS], *[new_v[n] for n in WEIGHTS])
```

```python
import functools

import jax
import jax.numpy as jnp
from jax import lax
from jax.experimental import pallas as pl
from jax.experimental.pallas import tpu as pltpu

F32 = jnp.float32
BF16 = jnp.bfloat16
MESH = pl.DeviceIdType.MESH
HBM = pltpu.HBM

EPS = 1e-6
NOPE, ROPE, VHEAD = 128, 64, 128
HEAD_W = NOPE + 2 * ROPE
ROPE_THETA = 10000.0
CONV_TAPS = 3
N_MOD = 6
N_CHIPS, N_CORES, N_DEV = 4, 2, 8
ADAM_LR, ADAM_B1, ADAM_B2, ADAM_EPS, ADAM_WD, ADAM_STEP = 0.001, 0.9, 0.999, 1e-08, 0.01, 10

LANES = 128
SUBLANES = 8
VMEM_LIMIT = 56 * 2**20

BIG = ("w_in", "w_branch_a", "w_uq", "w_ukv", "w_branch_b", "w_out", "w_up", "w_down")
WEIGHTS = ("w_ada", "b_ada", "pre_norm1_g", "w_in", "gm_ln_g", "gm_ln_b", "gm_w_s", "gm_b_s", "w_branch_a",
           "q_norm_g", "w_uq", "kv_norm_g", "w_ukv", "w_branch_b", "w_out", "post_norm1_g", "pre_norm2_g",
           "w_up", "conv_w", "conv_b", "w_down", "post_norm2_g")
SMALL_PARTIAL = ("gm_ln_g", "gm_ln_b", "gm_w_s", "gm_b_s", "q_norm_g", "kv_norm_g", "post_norm1_g",
                 "pre_norm2_g", "conv_w", "conv_b", "post_norm2_g")
SMALL = ("b_ada", "pre_norm1_g") + SMALL_PARTIAL


def _div_tile(n, cap, mult=LANES):
    t = (min(cap, n) // mult) * mult
    while t >= mult:
        if n % t == 0:
            return t
        t -= mult
    return n


def _params(**kw):
    return pltpu.CompilerParams(vmem_limit_bytes=VMEM_LIMIT, **kw)


def _row_spec(width):
    return pl.BlockSpec((1, width), lambda *_: (0, 0))


def _gelu(x):
    k = 0.7978845608028654
    return 0.5 * x * (1.0 + jnp.tanh(k * (x + 0.044715 * x * x * x)))


def _gelu_grad(x):
    k = 0.7978845608028654
    t = jnp.tanh(k * (x + 0.044715 * x * x * x))
    return 0.5 * (1.0 + t) + 0.5 * x * (1.0 - t * t) * k * (1.0 + 3.0 * 0.044715 * x * x)


def _sigmoid(x):
    return 1.0 / (1.0 + jnp.exp(-x))


def _dot(a, b, dims):
    return lax.dot_general(a, b, (dims, ((), ())), preferred_element_type=F32)


NN = ((1,), (0,))
NT = ((1,), (1,))
TN = ((0,), (0,))


def _logical(arr):
    if arr.ndim == 2:
        return arr.shape[0], arr.shape[1], arr.shape[1]
    return arr.shape[1], arr.shape[0] * arr.shape[2], arr.shape[2]


def _tile_spec(ndim, group_w, blk_rows, blk_cols, row_of, col_of):
    if ndim == 2:
        return pl.BlockSpec((blk_rows, blk_cols), lambda i, j, k: (row_of(i, j, k), col_of(i, j, k)))
    per = group_w // blk_cols
    return pl.BlockSpec((None, blk_rows, blk_cols),
                        lambda i, j, k: (col_of(i, j, k) // per, row_of(i, j, k), col_of(i, j, k) % per))


def _matmul(a, b, *, mode, out_dtype, name, tm=512, tn=512, tk=2048, mul=None, add=None, out_groups=None, comm=None):
    ar, ac, agw = _logical(a)
    br, bc, bgw = _logical(b)
    if mode == "nn":
        m, kd, n = ar, ac, bc
        m_w, k_w, n_w = (), (agw,), (bgw,)
    elif mode == "nt":
        m, kd, n = ar, ac, br
        m_w, k_w, n_w = (), (agw, bgw), ()
    else:
        m, kd, n = ac, ar, bc
        m_w, k_w, n_w = (agw,), (), (bgw,)
    if out_groups is not None:
        n_w = n_w + (n // out_groups,)
    tm = _div_tile(min((m,) + m_w), tm, SUBLANES)
    tn = _div_tile(min((n,) + n_w), tn)
    tk = _div_tile(min((kd,) + k_w), tk)
    assert all(w % tn == 0 for w in n_w) and all(w % tk == 0 for w in k_w) and all(w % tm == 0 for w in m_w)
    nk = kd // tk
    dims = {"nn": NN, "nt": NT, "tn": TN}[mode]
    gi, gj, gk = (lambda i, j, k: i), (lambda i, j, k: j), (lambda i, j, k: k)
    if mode == "nn":
        a_spec = _tile_spec(a.ndim, agw, tm, tk, gi, gk)
        b_spec = _tile_spec(b.ndim, bgw, tk, tn, gk, gj)
    elif mode == "nt":
        a_spec = _tile_spec(a.ndim, agw, tm, tk, gi, gk)
        b_spec = _tile_spec(b.ndim, bgw, tn, tk, gj, gk)
    else:
        a_spec = _tile_spec(a.ndim, agw, tk, tm, gk, gi)
        b_spec = _tile_spec(b.ndim, bgw, tk, tn, gk, gj)
    in_specs, operands = [a_spec, b_spec], [a, b]
    if mul is not None:
        assert mul.shape == (m, tn)
        in_specs.append(pl.BlockSpec((tm, tn), lambda i, j, k: (i, 0)))
        operands.append(mul)
    if add is not None:
        in_specs.append(pl.BlockSpec((tm, tn), lambda i, j, k: (i, j)))
        operands.append(add)

    def body(*refs):
        a_ref, b_ref = refs[0], refs[1]
        pos = 2
        mul_ref = add_ref = None
        if mul is not None:
            mul_ref, pos = refs[pos], pos + 1
        if add is not None:
            add_ref, pos = refs[pos], pos + 1
        o_ref = refs[pos]

        def finish(r):
            if mul_ref is not None:
                r = r * mul_ref[...]
            if add_ref is not None:
                r = r + add_ref[...]
            o_ref[...] = r.astype(out_dtype)

        part = _dot(a_ref[...], b_ref[...], dims)
        if nk == 1:
            finish(part)
        else:
            acc_ref = refs[pos + 1]
            k = pl.program_id(2)

            @pl.when(k == 0)
            def _():
                acc_ref[...] = part

            @pl.when(k > 0)
            def _():
                acc_ref[...] += part

            @pl.when(k == nk - 1)
            def _():
                finish(acc_ref[...])

    if out_groups is None:
        out_spec, out_dims = _tile_spec(2, n, tm, tn, gi, gj), (m, n)
    else:
        out_spec, out_dims = _tile_spec(3, n // out_groups, tm, tn, gi, gj), (out_groups, m, n // out_groups)
    return _call(body, operands, comm, name=name, grid=(m // tm, n // tn, nk), in_specs=in_specs, out_specs=out_spec,
                 out_shape=jax.ShapeDtypeStruct(out_dims, out_dtype),
                 scratch_shapes=[] if nk == 1 else [pltpu.VMEM((tm, tn), F32)])


def _accumulate(ref, value):
    @pl.when(pl.program_id(0) == 0)
    def _():
        ref[...] = value

    @pl.when(pl.program_id(0) > 0)
    def _():
        ref[...] += value


def _colsum(v):
    return jnp.sum(v, axis=0, keepdims=True)


def _rowmean(v):
    return jnp.mean(v, axis=-1, keepdims=True)


def _prenorm(x, g, scale, shift, name):
    s, d = x.shape
    tb = _div_tile(s, 256, SUBLANES)

    def body(x_ref, g_ref, sc_ref, sh_ref, h_ref):
        xv = x_ref[...]
        r = lax.rsqrt(_rowmean(xv * xv) + EPS)
        h_ref[...] = ((xv * r) * g_ref[...] * (1.0 + sc_ref[...]) + sh_ref[...]).astype(BF16)

    blk = pl.BlockSpec((tb, d), lambda i: (i, 0))
    return pl.pallas_call(
        body, name=name, grid=(s // tb,), in_specs=[blk, _row_spec(d), _row_spec(d), _row_spec(d)],
        out_specs=blk, out_shape=jax.ShapeDtypeStruct((s, d), BF16), compiler_params=_params(),
    )(x, g, scale, shift)


def _post_pre(x, y, gate, pg, g2, scale2, shift2, name):
    s, d = x.shape
    tb = _div_tile(s, 256, SUBLANES)

    def body(x_ref, y_ref, gate_ref, pg_ref, g2_ref, sc_ref, sh_ref, x1_ref, h2_ref):
        yv = y_ref[...]
        rp = lax.rsqrt(_rowmean(yv * yv) + EPS)
        x1 = x_ref[...] + gate_ref[...] * ((yv * rp) * pg_ref[...])
        x1_ref[...] = x1
        r2 = lax.rsqrt(_rowmean(x1 * x1) + EPS)
        h2_ref[...] = ((x1 * r2) * g2_ref[...] * (1.0 + sc_ref[...]) + sh_ref[...]).astype(BF16)

    blk = pl.BlockSpec((tb, d), lambda i: (i, 0))
    return pl.pallas_call(
        body, name=name, grid=(s // tb,), in_specs=[blk, blk] + [_row_spec(d)] * 5,
        out_specs=[blk, blk],
        out_shape=[jax.ShapeDtypeStruct((s, d), F32), jax.ShapeDtypeStruct((s, d), BF16)],
        compiler_params=_params(),
    )(x, y, gate, pg, g2, scale2, shift2)


def _post_bwd(y, gate, pg, name, *, dxo=None, xin=None, target=None):
    s, d = y.shape
    tb = _div_tile(s, 256, SUBLANES)
    from_loss = target is not None

    def body(*refs):
        if from_loss:
            y_ref, gate_ref, pg_ref, xin_ref, t_ref, dy_ref, dgate_ref, dpg_ref, dxo_ref, loss_ref = refs
        else:
            y_ref, gate_ref, pg_ref, dxo_in_ref, dy_ref, dgate_ref, dpg_ref = refs
        yv = y_ref[...]
        rp = lax.rsqrt(_rowmean(yv * yv) + EPS)
        yh = yv * rp
        fn = yh * pg_ref[...]
        gate = gate_ref[...]
        if from_loss:
            err = xin_ref[...] + gate * fn - t_ref[...]
            dxo = err * (1.0 / d)
            dxo_ref[...] = dxo
            part = 0.5 * jnp.sum(_rowmean(err * err), axis=0, keepdims=True)
            _accumulate(loss_ref, jnp.broadcast_to(part, loss_ref.shape))
        else:
            dxo = dxo_in_ref[...]
        _accumulate(dgate_ref, _colsum(dxo * fn))
        dfn = dxo * gate
        _accumulate(dpg_ref, _colsum(dfn * yh))
        dyh = dfn * pg_ref[...]
        dy_ref[...] = (rp * (dyh - yh * _rowmean(dyh * yh))).astype(BF16)

    blk = pl.BlockSpec((tb, d), lambda i: (i, 0))
    in_specs = [blk, _row_spec(d), _row_spec(d)]
    out_specs = [blk, _row_spec(d), _row_spec(d)]
    out_shape = [jax.ShapeDtypeStruct((s, d), BF16), jax.ShapeDtypeStruct((1, d), F32),
                 jax.ShapeDtypeStruct((1, d), F32)]
    if from_loss:
        operands = (y, gate, pg, xin, target)
        in_specs += [blk, blk]
        out_specs += [blk, _row_spec(LANES)]
        out_shape += [jax.ShapeDtypeStruct((s, d), F32), jax.ShapeDtypeStruct((1, LANES), F32)]
    else:
        operands = (y, gate, pg, dxo)
        in_specs += [blk]
    return pl.pallas_call(
        body, name=name, grid=(s // tb,), in_specs=in_specs, out_specs=out_specs, out_shape=out_shape,
        compiler_params=_params(),
    )(*operands)


def _prenorm_bwd(xin, dh, dres, g, scale, name, comm=None):
    s, d = xin.shape
    tb = _div_tile(s, 256, SUBLANES)

    def body(x_ref, dh_ref, dres_ref, g_ref, sc_ref, dx_ref, dshift_ref, dscale_ref, dg_ref):
        xv = x_ref[...]
        r = lax.rsqrt(_rowmean(xv * xv) + EPS)
        xn = xv * r
        dh = dh_ref[...]
        g1 = g_ref[...]
        s1 = 1.0 + sc_ref[...]
        _accumulate(dshift_ref, _colsum(dh))
        _accumulate(dscale_ref, _colsum(dh * xn * g1))
        _accumulate(dg_ref, _colsum(dh * xn * s1))
        dxn = dh * g1 * s1
        dx_ref[...] = dres_ref[...] + r * (dxn - xn * _rowmean(dxn * xn))

    blk = pl.BlockSpec((tb, d), lambda i: (i, 0))
    return _call(
        body, (xin, dh, dres, g, scale), comm, name=name, grid=(s // tb,),
        in_specs=[blk, blk, blk, _row_spec(d), _row_spec(d)],
        out_specs=[blk, _row_spec(d), _row_spec(d), _row_spec(d)],
        out_shape=[jax.ShapeDtypeStruct((s, d), F32)] + [jax.ShapeDtypeStruct((1, d), F32)] * 3)


def _merge(z_big, y_a, y_b, name):
    s, d = y_a.shape
    tb = _div_tile(s, 256, SUBLANES)

    def body(zg_ref, ya_ref, yb_ref, o_ref):
        o_ref[...] = (_sigmoid(zg_ref[:, :d]) * ya_ref[...] + _sigmoid(zg_ref[:, d:]) * yb_ref[...]).astype(BF16)

    blk = pl.BlockSpec((tb, d), lambda i: (i, 0))
    return pl.pallas_call(
        body, name=name, grid=(s // tb,), in_specs=[pl.BlockSpec((tb, 2 * d), lambda i: (i, 1)), blk, blk],
        out_specs=blk, out_shape=jax.ShapeDtypeStruct((s, d), BF16), compiler_params=_params(),
    )(z_big, y_a, y_b)


def _merge_bwd(dmerged, z_big, y_a, y_b, name):
    s, d = y_a.shape
    tb = _div_tile(s, 256, SUBLANES)

    def body(dm_ref, zg_ref, ya_ref, yb_ref, dya_ref, dyb_ref, dz_ref):
        dm = dm_ref[...]
        sa, sb = _sigmoid(zg_ref[:, :d]), _sigmoid(zg_ref[:, d:])
        dya_ref[...] = (dm * sa).astype(BF16)
        dyb_ref[...] = (dm * sb).astype(BF16)
        dz_ref[:, :d] = (dm * ya_ref[...] * sa * (1.0 - sa)).astype(BF16)
        dz_ref[:, d:] = (dm * yb_ref[...] * sb * (1.0 - sb)).astype(BF16)

    blk = pl.BlockSpec((tb, d), lambda i: (i, 0))
    wide = pl.BlockSpec((tb, 2 * d), lambda i: (i, 1))
    return pl.pallas_call(
        body, name=name, grid=(s // tb,), in_specs=[blk, wide, blk, blk], out_specs=[blk, blk, wide],
        out_shape=[jax.ShapeDtypeStruct((s, d), BF16), jax.ShapeDtypeStruct((s, d), BF16),
                   jax.ShapeDtypeStruct((s, 4 * d), BF16)],
        compiler_params=_params(),
    )(dmerged, z_big, y_a, y_b)


def _causal_mask(ch):
    q = lax.broadcasted_iota(jnp.int32, (ch, ch), 0)
    p = lax.broadcasted_iota(jnp.int32, (ch, ch), 1)
    return (p <= q).astype(F32)


def _gmlp_norm(zc, lng, lnb, gw):
    u_pre, v_pre = zc[:, :gw], zc[:, gw:]
    vg = _gelu(v_pre)
    mu = _rowmean(vg)
    cen = vg - mu
    rstd = lax.rsqrt(_rowmean(cen * cen) + EPS)
    vhat = cen * rstd
    return u_pre, v_pre, _gelu(u_pre), vhat, rstd, vhat * lng + lnb


def _gmlp_fwd(z_big, ln_g, ln_b, w_s, b_s_t, name):
    s = z_big.shape[0]
    groups, ch, _ = w_s.shape
    gw = ln_g.shape[1]
    gd = gw // groups

    def body(z_ref, lng_ref, lnb_ref, ws_ref, bt_ref, a_ref):
        _, _, u, _, _, vn = _gmlp_norm(z_ref[...], lng_ref[...], lnb_ref[...], gw)
        mask = _causal_mask(ch)
        for g in range(groups):
            cols = slice(g * gd, (g + 1) * gd)
            wm = (ws_ref[g] * mask).astype(BF16)
            mixed = _dot(wm, vn[:, cols].astype(BF16), NN) + bt_ref[:, g:g + 1]
            a_ref[:, cols] = (u[:, cols] * mixed).astype(BF16)

    return pl.pallas_call(
        body, name=name, grid=(s // ch,),
        in_specs=[pl.BlockSpec((ch, 2 * gw), lambda n: (n, 0)), _row_spec(gw), _row_spec(gw),
                  pl.BlockSpec((groups, ch, ch), lambda n: (0, 0, 0)), pl.BlockSpec((ch, groups), lambda n: (0, 0))],
        out_specs=pl.BlockSpec((ch, gw), lambda n: (n, 0)),
        out_shape=jax.ShapeDtypeStruct((s, gw), BF16), compiler_params=_params(),
    )(z_big, ln_g, ln_b, w_s, b_s_t)


def _gmlp_bwd(z_big, da, dz_big, ln_g, ln_b, w_s, b_s_t, name, comm=None):
    s = z_big.shape[0]
    groups, ch, _ = w_s.shape
    gw = ln_g.shape[1]
    gd = gw // groups

    def body(z_ref, da_ref, dzin_ref, lng_ref, lnb_ref, ws_ref, bt_ref, dz_ref, gws_ref, gbt_ref, glng_ref, glnb_ref):
        del dzin_ref
        lng = lng_ref[...]
        u_pre, v_pre, u, vhat, rstd, vn = _gmlp_norm(z_ref[...], lng, lnb_ref[...], gw)
        da = da_ref[...]
        mask = _causal_mask(ch)
        first = pl.program_id(0) == 0
        dvn_parts = []
        lane = lax.broadcasted_iota(jnp.int32, (ch, LANES), 1)
        gb = jnp.zeros((ch, LANES), F32)
        for g in range(groups):
            cols = slice(g * gd, (g + 1) * gd)
            wm = (ws_ref[g] * mask).astype(BF16)
            vn_g = vn[:, cols].astype(BF16)
            mixed = _dot(wm, vn_g, NN) + bt_ref[:, g:g + 1]
            dz_ref[:, cols] = (da[:, cols] * mixed * _gelu_grad(u_pre[:, cols])).astype(BF16)
            dmixed = da[:, cols] * u[:, cols]
            dm16 = dmixed.astype(BF16)
            dvn_parts.append(_dot(wm, dm16, TN))
            gws = _dot(dm16, vn_g, NT) * mask

            @pl.when(first)
            def _(g=g, gws=gws):
                gws_ref[g] = gws

            @pl.when(jnp.logical_not(first))
            def _(g=g, gws=gws):
                gws_ref[g] += gws

            gb = gb + jnp.where(lane == g, jnp.sum(dmixed, axis=1, keepdims=True), 0.0)
        _accumulate(gbt_ref, gb)
        dvn = jnp.concatenate(dvn_parts, axis=1)
        _accumulate(glnb_ref, _colsum(dvn))
        _accumulate(glng_ref, _colsum(dvn * vhat))
        dvh = dvn * lng
        dvg = rstd * (dvh - _rowmean(dvh) - vhat * _rowmean(dvh * vhat))
        dz_ref[:, gw:] = (dvg * _gelu_grad(v_pre)).astype(BF16)

    zspec = pl.BlockSpec((ch, 2 * gw), lambda n: (n, 0))
    return _call(
        body, (z_big, da, dz_big, ln_g, ln_b, w_s, b_s_t), comm, name=name, grid=(s // ch,),
        in_specs=[zspec, pl.BlockSpec((ch, gw), lambda n: (n, 0)), pl.BlockSpec(memory_space=HBM),
                  _row_spec(gw), _row_spec(gw), pl.BlockSpec((groups, ch, ch), lambda n: (0, 0, 0)),
                  pl.BlockSpec((ch, groups), lambda n: (0, 0))],
        out_specs=[zspec, pl.BlockSpec((groups, ch, ch), lambda n: (0, 0, 0)),
                   pl.BlockSpec((ch, LANES), lambda n: (0, 0)), _row_spec(gw), _row_spec(gw)],
        out_shape=[jax.ShapeDtypeStruct(dz_big.shape, BF16), jax.ShapeDtypeStruct((groups, ch, ch), F32),
                   jax.ShapeDtypeStruct((ch, LANES), F32), jax.ShapeDtypeStruct((1, gw), F32),
                   jax.ShapeDtypeStruct((1, gw), F32)],
        input_output_aliases={2: 0})


def _mla_prep(z_lat, q_g, kv_g, rope_k, name):
    s, latw = z_lat.shape
    ql, kvl = q_g.shape[1], kv_g.shape[1]
    tb = _div_tile(s, 256, SUBLANES)

    def body(z_ref, qg_ref, kvg_ref, t_ref, qn_ref, kvn_ref, kr_ref):
        q = z_ref[:, :ql]
        qn_ref[...] = ((q * lax.rsqrt(_rowmean(q * q) + EPS)) * qg_ref[...]).astype(BF16)
        kv = z_ref[:, ql:ql + kvl]
        kvn_ref[...] = ((kv * lax.rsqrt(_rowmean(kv * kv) + EPS)) * kvg_ref[...]).astype(BF16)
        kk = z_ref[:, ql + kvl:] * t_ref[...]
        kr_ref[...] = (kk + pltpu.roll(kk, ROPE, axis=1)).astype(BF16)

    return pl.pallas_call(
        body, name=name, grid=(s // tb,),
        in_specs=[pl.BlockSpec((tb, latw), lambda i: (i, 0)), _row_spec(ql), _row_spec(kvl),
                  pl.BlockSpec((tb, 2 * ROPE), lambda i: (i, 0))],
        out_specs=[pl.BlockSpec((tb, ql), lambda i: (i, 0)), pl.BlockSpec((tb, kvl), lambda i: (i, 0)),
                   pl.BlockSpec((tb, 2 * ROPE), lambda i: (i, 0))],
        out_shape=[jax.ShapeDtypeStruct((s, ql), BF16), jax.ShapeDtypeStruct((s, kvl), BF16),
                   jax.ShapeDtypeStruct((s, 2 * ROPE), BF16)],
        compiler_params=_params(),
    )(z_lat, q_g, kv_g, rope_k)


def _scores(q, k, kr, on_diagonal):
    s = _dot(q[:, :NOPE], k, NT) + _dot(q[:, NOPE:], kr, NT)
    if not on_diagonal:
        return s
    rows = lax.broadcasted_iota(jnp.int32, s.shape, 0)
    cols = lax.broadcasted_iota(jnp.int32, s.shape, 1)
    return jnp.where(cols <= rows, s, -1e30)


def _attn_fwd(q, kv, kr, heads, name, comm=None):
    s = q.shape[0]
    t = _div_tile(s, 512)
    nb = s // t
    hp = 2 if heads % 2 == 0 else 1

    def body(q_ref, k_ref, kr_ref, v_ref, o_ref, lse_ref, m_ref, l_ref, acc_ref):
        i, j = pl.program_id(1), pl.program_id(2)

        @pl.when(j == 0)
        def _():
            m_ref[...] = jnp.full(m_ref.shape, -1e30, F32)
            l_ref[...] = jnp.zeros(l_ref.shape, F32)
            acc_ref[...] = jnp.zeros(acc_ref.shape, F32)

        def step(on_diagonal):
            krv = kr_ref[...]
            for h in range(hp):
                vc = slice(h * VHEAD, (h + 1) * VHEAD)
                sc = _scores(q_ref[:, h * HEAD_W:(h + 1) * HEAD_W], k_ref[:, h * NOPE:(h + 1) * NOPE], krv, on_diagonal)
                m_old = m_ref[h]
                m_new = jnp.maximum(m_old, jnp.max(sc, axis=-1, keepdims=True))
                p = jnp.exp(sc - m_new)
                alpha = jnp.exp(m_old - m_new)
                l_new = alpha * l_ref[h] + jnp.sum(p, axis=-1, keepdims=True)
                acc = alpha * acc_ref[:, vc] + _dot(p.astype(BF16), v_ref[:, vc], NN)
                if on_diagonal:
                    o_ref[:, vc] = (acc / l_new).astype(BF16)
                    lse_ref[h] = jnp.broadcast_to(m_new + jnp.log(l_new), (t, LANES))
                else:
                    m_ref[h], l_ref[h], acc_ref[:, vc] = m_new, l_new, acc

        pl.when(j < i)(lambda: step(False))
        pl.when(j == i)(lambda: step(True))

    kidx = lambda off: (lambda h, i, j: (jnp.minimum(i, j), off(h)))
    return _call(
        body, (q, kv, kr, kv), comm, name=name, grid=(heads // hp, nb, nb),
        in_specs=[pl.BlockSpec((t, hp * HEAD_W), lambda h, i, j: (i, h)),
                  pl.BlockSpec((t, hp * NOPE), kidx(lambda h: h)),
                  pl.BlockSpec((t, 2 * ROPE), kidx(lambda h: 0)),
                  pl.BlockSpec((t, hp * VHEAD), kidx(lambda h: heads // hp + h))],
        out_specs=[pl.BlockSpec((t, hp * VHEAD), lambda h, i, j: (i, h)),
                   pl.BlockSpec((hp, t, LANES), lambda h, i, j: (h, i, 0))],
        out_shape=[jax.ShapeDtypeStruct((s, heads * VHEAD), BF16), jax.ShapeDtypeStruct((heads, s, LANES), F32)],
        scratch_shapes=[pltpu.VMEM((hp, t, 1), F32), pltpu.VMEM((hp, t, 1), F32), pltpu.VMEM((t, hp * VHEAD), F32)])


def _attn_bwd(q, kv, kr, o, do, lse, heads, name, comm=None):
    s = q.shape[0]
    t = _div_tile(s, 512)
    nb = s // t
    hp = 2 if heads % 2 == 0 else 1

    def body(q_ref, k_ref, kr_ref, v_ref, o_ref, do_ref, lse_ref, dq_ref, dk_ref, dv_ref, dk_acc, dv_acc):
        j, i = pl.program_id(1), pl.program_id(2)

        @pl.when(jnp.logical_and(j == 0, i == 0))
        def _():
            dq_ref[...] = jnp.zeros(dq_ref.shape, F32)

        def step(on_diagonal):
            krv = kr_ref[...]
            rows = pl.ds(pl.multiple_of(i * t, t), t)
            for h in range(hp):
                qc, kc, vc = (slice(h * w, (h + 1) * w) for w in (HEAD_W, NOPE, VHEAD))
                qv, kn, do_v = q_ref[:, qc], k_ref[:, kc], do_ref[:, vc]
                p = jnp.exp(_scores(qv, kn, krv, on_diagonal) - lse_ref[h][:, :1])
                dp = _dot(do_v, v_ref[:, vc], NT)
                delta = jnp.sum(do_v.astype(F32) * o_ref[:, vc].astype(F32), axis=-1, keepdims=True)
                ds = (p * (dp - delta)).astype(BF16)
                dq_ref[rows, h * HEAD_W:h * HEAD_W + NOPE] += _dot(ds, kn, NN)
                dq_ref[rows, h * HEAD_W + NOPE:(h + 1) * HEAD_W] += _dot(ds, krv, NN)
                dv_part, dk_part = _dot(p.astype(BF16), do_v, TN), _dot(ds, qv, TN)
                if on_diagonal:
                    dv_acc[:, vc], dk_acc[:, qc] = dv_part, dk_part
                else:
                    dv_acc[:, vc] += dv_part
                    dk_acc[:, qc] += dk_part

        pl.when(i == j)(lambda: step(True))
        pl.when(i > j)(lambda: step(False))

        @pl.when(i == nb - 1)
        def _():
            dk_ref[...] = dk_acc[...].astype(BF16)
            dv_ref[...] = dv_acc[...].astype(BF16)

    qidx = lambda h, j, i: (jnp.maximum(i, j), h)
    return _call(
        body, (q, kv, kr, kv, o, do, lse), comm, name=name, grid=(heads // hp, nb, nb),
        in_specs=[pl.BlockSpec((t, hp * HEAD_W), qidx),
                  pl.BlockSpec((t, hp * NOPE), lambda h, j, i: (j, h)),
                  pl.BlockSpec((t, 2 * ROPE), lambda h, j, i: (j, 0)),
                  pl.BlockSpec((t, hp * VHEAD), lambda h, j, i: (j, heads // hp + h)),
                  pl.BlockSpec((t, hp * VHEAD), qidx), pl.BlockSpec((t, hp * VHEAD), qidx),
                  pl.BlockSpec((hp, t, LANES), lambda h, j, i: (h, jnp.maximum(i, j), 0))],
        out_specs=[pl.BlockSpec((s, hp * HEAD_W), lambda h, j, i: (0, h)),
                   pl.BlockSpec((t, hp * HEAD_W), lambda h, j, i: (j, h)),
                   pl.BlockSpec((t, hp * VHEAD), lambda h, j, i: (j, h))],
        out_shape=[jax.ShapeDtypeStruct((s, heads * HEAD_W), F32), jax.ShapeDtypeStruct((s, heads * HEAD_W), BF16),
                   jax.ShapeDtypeStruct((s, heads * VHEAD), BF16)],
        scratch_shapes=[pltpu.VMEM((t, hp * HEAD_W), F32), pltpu.VMEM((t, hp * VHEAD), F32)])


def _mla_bwd_mid(dq, dk, dv, rope_q, rope_k, heads, name):
    s = dq.shape[0]
    tb = _div_tile(s, 256, SUBLANES)

    def body(dq_ref, dk_ref, dv_ref, tq_ref, tk_ref, dqb_ref, dkv_ref, dkk_ref):
        tq = tq_ref[...]
        dkr = jnp.zeros((tb, 2 * ROPE), F32)
        for h in range(heads):
            cols = slice(h * HEAD_W, (h + 1) * HEAD_W)
            dqb_ref[:, cols] = (dq_ref[:, cols] * tq).astype(BF16)
            dkv_ref[:, h * NOPE:(h + 1) * NOPE] = dk_ref[:, h * HEAD_W:h * HEAD_W + NOPE]
            dkr = dkr + dk_ref[:, h * HEAD_W + NOPE:(h + 1) * HEAD_W].astype(F32)
        dkv_ref[:, heads * NOPE:] = dv_ref[...]
        dkk_ref[...] = (dkr + pltpu.roll(dkr, ROPE, axis=1)) * tk_ref[...]

    wq, wv = heads * HEAD_W, heads * VHEAD
    return pl.pallas_call(
        body, name=name, grid=(s // tb,),
        in_specs=[pl.BlockSpec((tb, wq), lambda i: (i, 0)), pl.BlockSpec((tb, wq), lambda i: (i, 0)),
                  pl.BlockSpec((tb, wv), lambda i: (i, 0)), pl.BlockSpec((tb, HEAD_W), lambda i: (i, 0)),
                  pl.BlockSpec((tb, 2 * ROPE), lambda i: (i, 0))],
        out_specs=[pl.BlockSpec((tb, wq), lambda i: (i, 0)), pl.BlockSpec((tb, heads * NOPE + wv), lambda i: (i, 0)),
                   pl.BlockSpec((tb, 2 * ROPE), lambda i: (i, 0))],
        out_shape=[jax.ShapeDtypeStruct((s, wq), BF16), jax.ShapeDtypeStruct((s, heads * NOPE + wv), BF16),
                   jax.ShapeDtypeStruct((s, 2 * ROPE), F32)],
        compiler_params=_params(),
    )(dq, dk, dv, rope_q, rope_k)


def _mla_bwd_post(z_lat, dqn, dkvn, dkk, q_g, kv_g, name):
    s, latw = z_lat.shape
    ql, kvl = q_g.shape[1], kv_g.shape[1]
    tb = _div_tile(s, 256, SUBLANES)

    def norm_bwd(xv, dn, g, dg_ref):
        r = lax.rsqrt(_rowmean(xv * xv) + EPS)
        xh = xv * r
        _accumulate(dg_ref, _colsum(dn * xh))
        dxh = dn * g
        return r * (dxh - xh * _rowmean(dxh * xh))

    def body(z_ref, dqn_ref, dkvn_ref, dkk_ref, qg_ref, kvg_ref, dz_ref, gq_ref, gkv_ref):
        dz_ref[:, :ql] = norm_bwd(z_ref[:, :ql], dqn_ref[...], qg_ref[...], gq_ref).astype(BF16)
        dz_ref[:, ql:ql + kvl] = norm_bwd(z_ref[:, ql:ql + kvl], dkvn_ref[...], kvg_ref[...], gkv_ref).astype(BF16)
        dz_ref[:, ql + kvl:] = dkk_ref[...].astype(BF16)

    return pl.pallas_call(
        body, name=name, grid=(s // tb,),
        in_specs=[pl.BlockSpec((tb, latw), lambda i: (i, 0)), pl.BlockSpec((tb, ql), lambda i: (i, 0)),
                  pl.BlockSpec((tb, kvl), lambda i: (i, 0)), pl.BlockSpec((tb, 2 * ROPE), lambda i: (i, 0)),
                  _row_spec(ql), _row_spec(kvl)],
        out_specs=[pl.BlockSpec((tb, latw), lambda i: (i, 0)), _row_spec(ql), _row_spec(kvl)],
        out_shape=[jax.ShapeDtypeStruct((s, latw), BF16), jax.ShapeDtypeStruct((1, ql), F32),
                   jax.ShapeDtypeStruct((1, kvl), F32)],
        compiler_params=_params(),
    )(z_lat, dqn, dkvn, dkk, q_g, kv_g)


def _shift_down(x, n):
    rows = lax.broadcasted_iota(jnp.int32, x.shape, 0)
    return jnp.where(rows >= n, pltpu.roll(x, n, axis=0), 0.0)


def _shift_up(x, n):
    s = x.shape[0]
    rows = lax.broadcasted_iota(jnp.int32, x.shape, 0)
    return jnp.where(rows < s - n, pltpu.roll(x, s - n, axis=0), 0.0)


def _conv(pre, w_ref, b_ref):
    return (w_ref[2:3, :] * pre + w_ref[1:2, :] * _shift_down(pre, 1) + w_ref[0:1, :] * _shift_down(pre, 2)
            + b_ref[...])


def _conv_fwd(up_pre, conv_w, conv_b, name):
    s, ff2 = up_pre.shape
    ff = ff2 // 2
    tc = _div_tile(ff, 256)
    nb = ff // tc

    def body(pg_ref, pv_ref, wg_ref, wv_ref, bg_ref, bv_ref, act_ref):
        gate = _conv(pg_ref[...].astype(F32), wg_ref, bg_ref)
        val = _conv(pv_ref[...].astype(F32), wv_ref, bv_ref)
        act_ref[...] = (gate * _sigmoid(gate) * val).astype(BF16)

    def col(rows, off):
        return pl.BlockSpec((rows, tc), lambda j: (0, j + off))

    return pl.pallas_call(
        body, name=name, grid=(nb,),
        in_specs=[col(s, 0), col(s, nb), col(CONV_TAPS, 0), col(CONV_TAPS, nb), col(1, 0), col(1, nb)],
        out_specs=col(s, 0), out_shape=jax.ShapeDtypeStruct((s, ff), BF16), compiler_params=_params(),
    )(up_pre, up_pre, conv_w, conv_w, conv_b, conv_b)


def _conv_bwd(up_pre, dact, conv_w, conv_b, name, comm=None):
    s, ff2 = up_pre.shape
    ff = ff2 // 2
    tc = _div_tile(ff, 256)
    nb = ff // tc

    def half(pre, dx, w_ref, dpre_ref, gw_ref, gb_ref):
        gb_ref[...] = _colsum(dx)
        gw_ref[0:1, :] = _colsum(dx * _shift_down(pre, 2))
        gw_ref[1:2, :] = _colsum(dx * _shift_down(pre, 1))
        gw_ref[2:3, :] = _colsum(dx * pre)
        dpre_ref[...] = (w_ref[2:3, :] * dx + w_ref[1:2, :] * _shift_up(dx, 1)
                         + w_ref[0:1, :] * _shift_up(dx, 2)).astype(BF16)

    def body(pg_ref, pv_ref, da_ref, wg_ref, wv_ref, bg_ref, bv_ref, dup_ref, gwg_ref, gwv_ref, gbg_ref, gbv_ref):
        pre_g, pre_v = pg_ref[...].astype(F32), pv_ref[...].astype(F32)
        gate = _conv(pre_g, wg_ref, bg_ref)
        val = _conv(pre_v, wv_ref, bv_ref)
        da = da_ref[...].astype(F32)
        sg = _sigmoid(gate)
        half(pre_v, da * gate * sg, wv_ref, dup_ref.at[1], gwv_ref, gbv_ref)
        half(pre_g, da * val * sg * (1.0 + gate * (1.0 - sg)), wg_ref, dup_ref.at[0], gwg_ref, gbg_ref)

    def col(rows, off):
        return pl.BlockSpec((rows, tc), lambda j: (0, j + off))

    return _call(
        body, (up_pre, up_pre, dact, conv_w, conv_w, conv_b, conv_b), comm, name=name, grid=(nb,),
        in_specs=[col(s, 0), col(s, nb), col(s, 0), col(CONV_TAPS, 0), col(CONV_TAPS, nb), col(1, 0), col(1, nb)],
        out_specs=[pl.BlockSpec((2, s, tc), lambda j: (0, 0, j)), col(CONV_TAPS, 0), col(CONV_TAPS, 0),
                   col(1, 0), col(1, 0)],
        out_shape=[jax.ShapeDtypeStruct((2, s, ff), BF16)] + [jax.ShapeDtypeStruct((CONV_TAPS, ff), F32)] * 2
        + [jax.ShapeDtypeStruct((1, ff), F32)] * 2)


def _ada_fwd(c_all, w, b, name):
    nseq, d = c_all.shape
    na = w.shape[1]
    tn = _div_tile(na, 512)

    def body(c_ref, w_ref, b_ref, o_ref):
        cv = c_ref[...]
        sc = cv * _sigmoid(cv)
        o_ref[...] = jnp.dot(sc, w_ref[...], preferred_element_type=F32, precision=lax.Precision.HIGHEST) + b_ref[...]

    return pl.pallas_call(
        body, name=name, grid=(na // tn,),
        in_specs=[pl.BlockSpec((nseq, d), lambda j: (0, 0)), pl.BlockSpec((d, tn), lambda j: (0, j)),
                  pl.BlockSpec((1, tn), lambda j: (0, j))],
        out_specs=pl.BlockSpec((nseq, tn), lambda j: (0, j)),
        out_shape=jax.ShapeDtypeStruct((nseq, na), F32), compiler_params=_params(),
    )(c_all, w, b)


def _ada_bwd(c_all_t, dmod, name):
    d, nseq = c_all_t.shape
    na = dmod.shape[1]
    tm, tn = _div_tile(d, 256, SUBLANES), _div_tile(na, 512)

    def body(c_ref, dm_ref, o_ref):
        cv = c_ref[...]
        sc = cv * _sigmoid(cv)
        acc = sc[:, 0:1] * dm_ref[0:1, :]
        for bi in range(1, nseq):
            acc = acc + sc[:, bi:bi + 1] * dm_ref[bi:bi + 1, :]
        o_ref[...] = acc

    return pl.pallas_call(
        body, name=name, grid=(d // tm, na // tn),
        in_specs=[pl.BlockSpec((tm, nseq), lambda i, j: (i, 0)), pl.BlockSpec((nseq, tn), lambda i, j: (0, j))],
        out_specs=pl.BlockSpec((tm, tn), lambda i, j: (i, j)),
        out_shape=jax.ShapeDtypeStruct((d, na), F32), compiler_params=_params(),
    )(c_all_t, dmod)


def _adamw(w, g, m, v, name, comm=None, after=None):
    rows, cols = w.shape
    tb = _div_tile(rows, max(SUBLANES, (256 * 1024) // cols // SUBLANES * SUBLANES), SUBLANES)
    c1 = 1.0 / (1.0 - ADAM_B1 ** ADAM_STEP)
    c2 = 1.0 / (1.0 - ADAM_B2 ** ADAM_STEP)

    def body(*refs):
        w_ref, g_ref, m_ref, v_ref = refs[:4]
        d_ref, nm_ref, nv_ref = refs[-3:]
        gv = g_ref[...]
        nm = ADAM_B1 * m_ref[...] + (1.0 - ADAM_B1) * gv
        nv = ADAM_B2 * v_ref[...] + (1.0 - ADAM_B2) * (gv * gv)
        nm_ref[...] = nm
        nv_ref[...] = nv
        d_ref[...] = -ADAM_LR * ((nm * c1) / (jnp.sqrt(nv * c2) + ADAM_EPS) + ADAM_WD * w_ref[...])

    blk = pl.BlockSpec((tb, cols), lambda i: (i, 0))
    operands, in_specs = (w, g, m, v), [blk] * 4
    if after is not None:
        operands, in_specs = operands + (after,), in_specs + [pl.BlockSpec(after.shape, lambda i: (0, 0))]
    return _call(body, operands, comm, name=name, grid=(rows // tb,), in_specs=in_specs, out_specs=[blk] * 3,
                 out_shape=[jax.ShapeDtypeStruct((rows, cols), F32)] * 3)


def _sum_leading(parts, name):
    n, rows, cols = parts.shape
    tb = _div_tile(rows, 512, SUBLANES)

    def body(p_ref, o_ref):
        acc = p_ref[0]
        for k in range(1, n):
            acc = acc + p_ref[k]
        o_ref[...] = acc

    return pl.pallas_call(
        body, name=name, grid=(rows // tb,), in_specs=[pl.BlockSpec((n, tb, cols), lambda i: (0, i, 0))],
        out_specs=pl.BlockSpec((tb, cols), lambda i: (i, 0)),
        out_shape=jax.ShapeDtypeStruct((rows, cols), F32), compiler_params=_params(),
    )(parts)


def _place():
    x, y, c = lax.axis_index("x"), lax.axis_index("y"), lax.axis_index("c")
    return x, y, c, [(1 - x, y), (x, 1 - y), (1 - x, 1 - y)]


def _all_gather(block, name):
    m_per, n = block.shape

    def body(x_ref, out_ref, send_sems, recv_sems, local_sem):
        x, y, c, chips = _place()
        me, sibling = (x, y, c), (x, y, 1 - c)

        def rows(px, py, pc):
            return out_ref.at[pl.ds((4 * px + 2 * py + pc) * m_per, m_per), :]

        def copy(k, blk, to, src=None):
            return pltpu.make_async_remote_copy(
                src_ref=rows(*blk) if src is None else src, dst_ref=rows(*blk), send_sem=send_sems.at[k],
                recv_sem=recv_sems.at[k], device_id=to, device_id_type=MESH)

        mine = pltpu.make_async_copy(x_ref, rows(*me), local_sem)
        mine.start()
        first = [copy(0, me, sibling, src=x_ref)]
        first += [copy(1 + j, me, (*chip, c), src=x_ref) for j, chip in enumerate(chips)]
        for cp in first:
            cp.start()
        passed = [copy(4 + j, (*chip, c), sibling) for j, chip in enumerate(chips)]
        for j, chip in enumerate(chips):
            copy(1 + j, (*chip, c), me).wait_recv()
            passed[j].start()
        copy(0, sibling, me).wait_recv()
        for j, chip in enumerate(chips):
            copy(4 + j, (*chip, 1 - c), me).wait_recv()
        for cp in first + passed:
            cp.wait_send()
        mine.wait()

    return pl.pallas_call(
        body, name=name, out_shape=jax.ShapeDtypeStruct((N_DEV * m_per, n), block.dtype),
        in_specs=[pl.BlockSpec(memory_space=pltpu.VMEM)], out_specs=pl.BlockSpec(memory_space=pltpu.VMEM),
        scratch_shapes=[pltpu.SemaphoreType.DMA((7,)), pltpu.SemaphoreType.DMA((7,)), pltpu.SemaphoreType.DMA],
        compiler_params=_params(),
    )(block)


def _hbm_specs(n):
    return [pl.BlockSpec(memory_space=HBM)] * n


def _half_rows(ref, half, lead=None):
    h = ref.shape[-2] // 2
    rows = pl.ds(pl.multiple_of(half * h, 2 * SUBLANES), h)
    return ref.at[rows, :] if lead is None else ref.at[lead, rows, :]


class _Comm:
    def __init__(self, operands, out_shape, sem_dims, build, aliases=None):
        self.operands, self.out_shape, self.sem_dims = list(operands), list(out_shape), list(sem_dims)
        self.scratch = [pltpu.SemaphoreType.DMA(d) for d in sem_dims]
        self.build, self.aliases = build, dict(aliases or {})


class _SemGrid:
    def __init__(self, sems, dims):
        self.sems, self.dims, self.at = list(sems), tuple(dims), self

    def __getitem__(self, index):
        index = index if isinstance(index, tuple) else (index,)
        flat = 0
        for i, d in zip(index, self.dims):
            flat = flat * d + i
        return self.sems[flat]


def _call(body, operands, comm=None, *, name, grid, in_specs, out_specs, out_shape, scratch_shapes=(),
          input_output_aliases=None):
    aliases = dict(input_output_aliases or {})
    if comm is None:
        return pl.pallas_call(
            body, name=name, grid=grid, in_specs=in_specs, out_specs=out_specs, out_shape=out_shape,
            scratch_shapes=list(scratch_shapes), input_output_aliases=aliases, compiler_params=_params())(*operands)
    single = not isinstance(out_shape, (list, tuple))
    outs = [out_shape] if single else list(out_shape)
    ospecs = [out_specs] if single else list(out_specs)
    n_in, n_out, n_scr = len(operands), len(outs), len(scratch_shapes)
    c_in, c_out = len(comm.operands), len(comm.out_shape)
    for i, o in comm.aliases.items():
        aliases[n_in + i] = n_out + o

    def hosted(*refs):
        ins, c_ins = refs[:n_in], refs[n_in:n_in + c_in]
        o0 = n_in + c_in
        o_refs, c_outs = refs[o0:o0 + n_out], refs[o0 + n_out:o0 + n_out + c_out]
        s0 = o0 + n_out + c_out
        scr, sems = refs[s0:s0 + n_scr], refs[s0 + n_scr:]
        start, finish = comm.build(c_ins, c_outs, sems)
        first = last = None
        for dim, size in enumerate(grid):
            at0, at1 = pl.program_id(dim) == 0, pl.program_id(dim) == size - 1
            first = at0 if first is None else jnp.logical_and(first, at0)
            last = at1 if last is None else jnp.logical_and(last, at1)
        pl.when(first)(start)
        body(*ins, *o_refs, *scr)
        pl.when(last)(finish)

    res = pl.pallas_call(
        hosted, name=name, grid=grid, in_specs=list(in_specs) + _hbm_specs(c_in),
        out_specs=ospecs + _hbm_specs(c_out), out_shape=outs + comm.out_shape,
        scratch_shapes=list(scratch_shapes) + comm.scratch, input_output_aliases=aliases,
        compiler_params=_params())(*operands, *comm.operands)
    return (res[0] if single else res[:n_out]), res[n_out:]


def _run_comm(comm, name):
    c_in, c_out = len(comm.operands), len(comm.out_shape)

    def body(*refs):
        start, finish = comm.build(refs[:c_in], refs[c_in:c_in + c_out], refs[c_in + c_out:])
        start()
        finish()

    return pl.pallas_call(
        body, name=name, in_specs=_hbm_specs(c_in), out_specs=_hbm_specs(c_out), out_shape=comm.out_shape,
        scratch_shapes=comm.scratch, input_output_aliases=comm.aliases, compiler_params=_params())(*comm.operands)


def _gather_comm(shards):
    nw = len(shards)

    def build(in_refs, out_refs, sems):
        send_sems, recv_sems = sems
        x, y, c, chips = _place()
        me, sibling = (x, y, c), (x, y, 1 - c)

        def copy(w, k, block, half, to, src=None):
            dst = _half_rows(out_refs[w], half, 2 * block[0] + block[1])
            return pltpu.make_async_remote_copy(
                src_ref=dst if src is None else src, dst_ref=dst, send_sem=send_sems.at[w, k],
                recv_sem=recv_sems.at[w, k], device_id=to, device_id_type=MESH)

        first = [copy(w, j, (x, y), c, (*chip, c), src=_half_rows(in_refs[w], c))
                 for w in range(nw) for j, chip in enumerate(chips)]

        def start():
            for cp in first:
                cp.start()

        def finish():
            passed = []
            for w in range(nw):
                for j, chip in enumerate(chips):
                    copy(w, j, chip, c, me).wait_recv()
                    passed.append(copy(w, 3 + j, chip, c, sibling))
                    passed[-1].start()
            for w in range(nw):
                for j, chip in enumerate(chips):
                    copy(w, 3 + j, chip, 1 - c, me).wait_recv()
            for cp in first + passed:
                cp.wait_send()

        return start, finish

    return _Comm(shards, [jax.ShapeDtypeStruct((N_CHIPS,) + w.shape, w.dtype) for w in shards],
                 [(nw, 6), (nw, 6)], build)


def _swap_comm(gs):
    nw = len(gs)

    def build(in_refs, out_refs, sems):
        send_sems, recv_sems = sems
        x, y, c, _ = _place()
        cps = []
        for w in range(nw):
            h = in_refs[w].shape[1] // 2
            src = in_refs[w].at[:, pl.ds(pl.multiple_of((1 - c) * h, 2 * SUBLANES), h), :]
            cps.append(pltpu.make_async_remote_copy(
                src_ref=src, dst_ref=out_refs[w], send_sem=send_sems.at[w], recv_sem=recv_sems.at[w],
                device_id=(x, y, 1 - c), device_id_type=MESH))

        def start():
            for cp in cps:
                cp.start()

        def finish():
            for cp in cps:
                cp.wait()

        return start, finish

    return _Comm(gs, [jax.ShapeDtypeStruct((N_CHIPS, g.shape[1] // 2, g.shape[2]), g.dtype) for g in gs],
                 [(nw,), (nw,)], build)


def _exchange_comm(s1s):
    nw = len(s1s)

    def build(in_refs, out_refs, sems):
        send_sems, recv_sems = sems
        x, y, c, chips = _place()
        cps = [pltpu.make_async_remote_copy(
            src_ref=in_refs[w].at[2 * chip[0] + chip[1]], dst_ref=out_refs[w].at[j], send_sem=send_sems.at[w, j],
            recv_sem=recv_sems.at[w, j], device_id=(*chip, c), device_id_type=MESH)
            for w in range(nw) for j, chip in enumerate(chips)]

        def start():
            for cp in cps:
                cp.start()

        def finish():
            for cp in cps:
                cp.wait()

        return start, finish

    return _Comm(s1s, [jax.ShapeDtypeStruct((N_CHIPS - 1,) + s.shape[1:], s.dtype) for s in s1s],
                 [(nw, 3), (nw, 3)], build)


def _size(dims):
    n = 1
    for d in dims:
        n *= d
    return n


def _sem_grids(comm, sem_refs):
    grids, pos = [], 0
    for dims in comm.sem_dims:
        grids.append(_SemGrid(sem_refs[pos:pos + _size(dims)], dims))
        pos += _size(dims)
    return grids


def _comm_split_start(comm, name, after=()):
    c_in, c_out = len(comm.operands), len(comm.out_shape)
    counts = [_size(d) for d in comm.sem_dims]
    n_sem = sum(counts)
    assert not comm.aliases

    def body(*refs):
        srcs, lands = refs[:c_in], refs[c_in:c_in + c_out]
        first_sem = c_in + c_out + len(after)
        start, _ = comm.build(srcs, lands, _sem_grids(comm, refs[first_sem:first_sem + n_sem]))
        start()
        refs[-1][...] = jnp.zeros(refs[-1].shape, refs[-1].dtype)

    lands = [pltpu.with_memory_space_constraint(lax.empty(o.shape, o.dtype), HBM) for o in comm.out_shape]
    srcs = [pltpu.with_memory_space_constraint(a, HBM) for a in comm.operands]
    res = pl.pallas_call(
        body, name=name, in_specs=_hbm_specs(c_in + c_out) + [pl.BlockSpec(memory_space=pl.ANY)] * len(after),
        out_specs=[pl.BlockSpec(memory_space=pltpu.SEMAPHORE)] * n_sem + _hbm_specs(c_in + c_out)
        + [pl.BlockSpec(memory_space=pltpu.VMEM)],
        out_shape=[pltpu.SemaphoreType.DMA(())] * n_sem + [pltpu.HBM(a.shape, a.dtype) for a in comm.operands]
        + [pltpu.HBM(o.shape, o.dtype) for o in comm.out_shape] + [jax.ShapeDtypeStruct((SUBLANES, LANES), F32)],
        input_output_aliases={i: n_sem + i for i in range(c_in + c_out)},
        compiler_params=_params(has_side_effects=pltpu.SideEffectType.DATAFLOW_SIDE_EFFECTING))(*srcs, *lands, *after)
    return res[:-1], res[-1]


def _comm_split_wait(comm, state, after, name):
    c_in, c_out, n_sem = len(comm.operands), len(comm.out_shape), sum(_size(d) for d in comm.sem_dims)
    sems, srcs, lands = state[:n_sem], state[n_sem:n_sem + c_in], state[n_sem + c_in:]

    def body(*refs):
        src_refs, land_refs = refs[:c_in], refs[c_in:c_in + c_out]
        _, finish = comm.build(src_refs, land_refs, _sem_grids(comm, refs[c_in + c_out:c_in + c_out + n_sem]))
        finish()

    sem_spec = pl.BlockSpec(memory_space=pltpu.SEMAPHORE)
    res = pl.pallas_call(
        body, name=name, in_specs=_hbm_specs(c_in + c_out) + [sem_spec] * n_sem + [pl.BlockSpec(memory_space=pl.ANY)],
        out_specs=_hbm_specs(c_in + c_out),
        out_shape=[pltpu.HBM(a.shape, a.dtype) for a in srcs] + [pltpu.HBM(o.shape, o.dtype) for o in lands],
        input_output_aliases={i: i for i in range(c_in + c_out)},
        compiler_params=_params(has_side_effects=pltpu.SideEffectType.DATAFLOW_SIDE_EFFECTING),
    )(*srcs, *lands, *sems, after)
    return res[:c_in], res[c_in:]


def _share_comm(fs):
    nw = len(fs)

    def build(in_refs, out_refs, sems):
        del in_refs
        send_sems, recv_sems = sems
        x, y, c, _ = _place()

        def copy(w, half):
            rows = _half_rows(out_refs[w], half)
            return pltpu.make_async_remote_copy(
                src_ref=rows, dst_ref=rows, send_sem=send_sems.at[w], recv_sem=recv_sems.at[w],
                device_id=(x, y, 1 - c), device_id_type=MESH)

        sends = [copy(w, c) for w in range(nw)]

        def start():
            for cp in sends:
                cp.start()

        def finish():
            for w in range(nw):
                copy(w, 1 - c).wait_recv()
            for cp in sends:
                cp.wait_send()

        return start, finish

    return _Comm(fs, [jax.ShapeDtypeStruct(f.shape, f.dtype) for f in fs],
                 [(nw,), (nw,)], build,
                 aliases={w: w for w in range(nw)})


def _add_sibling(g, r1, place, name):
    nch, h, cols = r1.shape
    tr = _div_tile(h, 256, 2 * SUBLANES)
    nb = h // tr

    def body(place_ref, g_ref, r_ref, o_ref):
        del place_ref
        o_ref[...] = (g_ref[...].astype(F32) + r_ref[...].astype(F32)).astype(BF16)

    spec = pltpu.PrefetchScalarGridSpec(
        num_scalar_prefetch=1, grid=(nch, nb),
        in_specs=[pl.BlockSpec((None, tr, cols), lambda k, i, p: (k, p[0] * nb + i, 0)),
                  pl.BlockSpec((None, tr, cols), lambda k, i, p: (k, i, 0))],
        out_specs=pl.BlockSpec((None, tr, cols), lambda k, i, p: (k, i, 0)))
    return pl.pallas_call(body, name=name, grid_spec=spec, out_shape=jax.ShapeDtypeStruct((nch, h, cols), BF16),
                          compiler_params=_params())(place, g, r1)


def _add_chips(s1, r2, place, name):
    _, h, cols = s1.shape
    tr = _div_tile(h, 256, 2 * SUBLANES)
    nb = h // tr

    def body(place_ref, s_ref, r_ref, o_ref):
        del place_ref
        acc = s_ref[...].astype(F32)
        for j in range(N_CHIPS - 1):
            acc = acc + r_ref[j].astype(F32)
        o_ref[...] = acc

    spec = pltpu.PrefetchScalarGridSpec(
        num_scalar_prefetch=1, grid=(nb,),
        in_specs=[pl.BlockSpec((None, tr, cols), lambda i, p: (p[1], i, 0)),
                  pl.BlockSpec((N_CHIPS - 1, tr, cols), lambda i, p: (0, i, 0))],
        out_specs=pl.BlockSpec((tr, cols), lambda i, p: (p[0] * nb + i, 0)))
    return pl.pallas_call(body, name=name, grid_spec=spec, out_shape=jax.ShapeDtypeStruct((2 * h, cols), F32),
                          compiler_params=_params())(place, s1, r2)


def _quarter_turn(m):
    h = m.shape[-1] // 2
    return jnp.concatenate([-m[..., h:], m[..., :h]], axis=-1)


def _quarter_turn_back(m):
    h = m.shape[-1] // 2
    return jnp.concatenate([m[..., h:], -m[..., :h]], axis=-1)


def _join_cols(sh):
    return jnp.concatenate([sh[k] for k in range(N_CHIPS)], axis=1)


def _split_cols(full):
    c = full.shape[1] // N_CHIPS
    return jnp.stack([full[:, k * c:(k + 1) * c] for k in range(N_CHIPS)])


def kernel(x, c, positions, w_ada, b_ada, pre_norm1_g, w_in, gm_ln_g, gm_ln_b, gm_w_s, gm_b_s, w_branch_a, q_norm_g, w_uq, kv_norm_g, w_ukv, w_branch_b, w_out, post_norm1_g, pre_norm2_g, w_up, conv_w, conv_b, w_down, post_norm2_g, loss_target, m_w_ada, m_b_ada, m_pre_norm1_g, m_w_in, m_gm_ln_g, m_gm_ln_b, m_gm_w_s, m_gm_b_s, m_w_branch_a, m_q_norm_g, m_w_uq, m_kv_norm_g, m_w_ukv, m_w_branch_b, m_w_out, m_post_norm1_g, m_pre_norm2_g, m_w_up, m_conv_w, m_conv_b, m_w_down, m_post_norm2_g, v_w_ada, v_b_ada, v_pre_norm1_g, v_w_in, v_gm_ln_g, v_gm_ln_b, v_gm_w_s, v_gm_b_s, v_w_branch_a, v_q_norm_g, v_w_uq, v_kv_norm_g, v_w_ukv, v_w_branch_b, v_w_out, v_post_norm1_g, v_pre_norm2_g, v_w_up, v_conv_w, v_conv_b, v_w_down, v_post_norm2_g):
    given = dict(locals())
    s, d = x.shape[1], x.shape[2]
    gw = gm_ln_g.shape[0]
    ql, kvl = q_norm_g.shape[0], kv_norm_g.shape[0]
    heads = N_CHIPS * w_uq.shape[1] // (NOPE + ROPE)
    ff = N_CHIPS * w_down.shape[0]
    assert gw == d and N_CHIPS * w_ukv.shape[1] == heads * (NOPE + VHEAD)
    ix, iy, ic = lax.axis_index("x"), lax.axis_index("y"), lax.axis_index("c")
    chip = 2 * ix + iy
    dev = 2 * chip + ic
    row = lambda v: v.reshape(1, -1)

    c_all = _all_gather(jnp.pad(c, ((0, SUBLANES - 1), (0, 0))), "gather_c").reshape(N_DEV, SUBLANES, d)[:, 0]
    na = w_ada.shape[1]
    b_ada_mine = lax.dynamic_slice(b_ada, (chip * na,), (na,))
    mod_cols = _ada_fwd(c_all, w_ada, row(b_ada_mine), "ada_fwd")
    mod_all = _all_gather(mod_cols, "gather_mod").reshape(N_CHIPS, N_CORES, N_DEV, na)[:, 0]
    mod = lax.dynamic_index_in_dim(mod_all, dev, axis=1, keepdims=False).reshape(N_MOD, d)
    shift1, scale1, gate1, shift2, scale2, gate2 = (mod[i:i + 1] for i in range(N_MOD))

    mine = {n: given[n].astype(BF16) for n in BIG}
    gather = lambda names: _gather_comm([mine[n] for n in names])
    whole = lambda n, g: lax.dynamic_update_slice(g, mine[n][None], (chip, 0, 0))
    rows4 = lambda sh4: sh4.reshape(-1, sh4.shape[2])
    wi = _join_cols(whole("w_in", _run_comm(gather(["w_in"]), "gather_w_in")[0]))
    o_q, o_kv, o_pe, o_ga = 2 * gw, 2 * gw + ql, 2 * gw + ql + kvl, 2 * gw + ql + kvl + ROPE
    w_in_big = jnp.concatenate([wi[:, :o_q], wi[:, o_ga:]], axis=1)
    w_in_lat = jnp.concatenate([wi[:, o_q:o_ga], _quarter_turn(wi[:, o_pe:o_ga])], axis=1)

    inv = ROPE_THETA ** (-jnp.arange(0, ROPE, 2, dtype=F32) / ROPE)
    ang = positions[0].astype(F32)[:, None] * inv
    cos, sin = jnp.cos(ang), jnp.sin(ang)
    rope_k = jnp.concatenate([cos, cos, sin, sin], axis=1)
    softmax_scale = float(NOPE + ROPE) ** -0.5
    rope_q = jnp.concatenate([jnp.ones((s, NOPE), F32), rope_k], axis=1) * softmax_scale

    x2d, tgt = x[0], loss_target[0]
    g_pre1, g_post1, g_pre2, g_post2 = row(pre_norm1_g), row(post_norm1_g), row(pre_norm2_g), row(post_norm2_g)
    ln_g, ln_b, q_g, kv_g = row(gm_ln_g), row(gm_ln_b), row(q_norm_g), row(kv_norm_g)
    b_s_t = gm_b_s.T
    conv_wf = _all_gather(jnp.pad(conv_w, ((0, SUBLANES - CONV_TAPS), (0, 0))), "gather_conv_w")
    conv_wf = conv_wf.reshape(N_CHIPS, N_CORES, SUBLANES, conv_w.shape[1])[:, 0, :CONV_TAPS]
    conv_wf = conv_wf.transpose(1, 0, 2).reshape(CONV_TAPS, 2 * ff)
    conv_bf = row(conv_b)

    h1 = _prenorm(x2d, g_pre1, scale1, shift1, "prenorm1")
    z_big, (g_uq, g_ukv, g_a) = _matmul(h1, w_in_big, mode="nn", out_dtype=F32, name="mm_z_big", tm=s,
                                        comm=gather(["w_uq", "w_ukv", "w_branch_a"]))
    wq = _join_cols(whole("w_uq", g_uq)).reshape(ql, heads, NOPE + ROPE)
    w_q = jnp.concatenate([wq, _quarter_turn(wq[:, :, NOPE:])], axis=2).reshape(ql, heads * HEAD_W)
    w_kv = _join_cols(whole("w_ukv", g_ukv)).reshape(kvl, heads, 2, NOPE).transpose(0, 2, 1, 3)
    w_kv = w_kv.reshape(kvl, 2 * heads * NOPE)
    w_a = rows4(whole("w_branch_a", g_a))
    z_lat = _matmul(h1, w_in_lat, mode="nn", out_dtype=F32, name="mm_z_lat", tm=s, tn=1024)
    a_act = _gmlp_fwd(z_big, ln_g, ln_b, gm_w_s, b_s_t, "gmlp_fwd")
    qn, kvn, kr = _mla_prep(z_lat, q_g, kv_g, rope_k, "mla_prep")
    q_rot = _matmul(qn, w_q, mode="nn", out_dtype=BF16, name="mm_q", tm=s, tn=HEAD_W, mul=rope_q)
    kv_all = _matmul(kvn, w_kv, mode="nn", out_dtype=BF16, name="mm_kv", tm=s, tn=1024)
    (o_att, lse), (g_b, g_o, g_up) = _attn_fwd(q_rot, kv_all, kr, heads, "attn_fwd",
                                               comm=gather(["w_branch_b", "w_out", "w_up"]))
    w_b, w_o, w_upf = rows4(whole("w_branch_b", g_b)), rows4(whole("w_out", g_o)), whole("w_up", g_up)
    y_a = _matmul(a_act, w_a, mode="nn", out_dtype=F32, name="mm_y_a", tm=s)
    y_b = _matmul(o_att, w_b, mode="nn", out_dtype=F32, name="mm_y_b", tm=s)
    merged = _merge(z_big, y_a, y_b, "merge")
    y1 = _matmul(merged, w_o, mode="nn", out_dtype=F32, name="mm_y1", tm=s)
    x1, h2 = _post_pre(x2d, y1, gate1, g_post1, g_pre2, scale2, shift2, "post1_pre2")

    up_pre, (g_dn,) = _matmul(h2, w_upf, mode="nn", out_dtype=BF16, name="mm_up", tm=s, tn=1408,
                              comm=gather(["w_down"]))
    w_dn = rows4(whole("w_down", g_dn))
    act = _conv_fwd(up_pre, conv_wf, conv_bf, "conv_fwd")
    ffn = _matmul(act, w_dn, mode="nn", out_dtype=F32, name="mm_ffn", tm=s, tk=1408)

    dffn, dgate2, g_post2_grad, dx2, loss_part = _post_bwd(ffn, gate2, g_post2, "post2_bwd", xin=x1, target=tgt)
    loss = lax.psum(loss_part[0, 0], ("x", "y", "c"))
    place = jnp.stack([ic, chip]).astype(jnp.int32)
    rows_of = lambda g: g.reshape(N_CHIPS, g.shape[0] // N_CHIPS, g.shape[1])
    add_sibling = lambda names, gs, r1s: [_add_sibling(g, r1, place, "rs_add_sibling_" + n)
                                          for n, g, r1 in zip(names, gs, r1s)]
    add_chips = lambda names, s1s, r2s: [_add_chips(s1, r2, place, "rs_add_chips_" + n)
                                         for n, s1, r2 in zip(names, s1s, r2s)]
    dact = _matmul(dffn, w_dn, mode="nt", out_dtype=BF16, name="mm_dact", tm=s)
    gp_down = [rows_of(_matmul(act, dffn, mode="tn", out_dtype=BF16, name="mm_gw_down", tn=1024, tk=s))]
    (dup, gcw_g, gcw_v, gcb_g, gcb_v), r1_down = _conv_bwd(up_pre, dact, conv_wf, conv_bf, "conv_bwd",
                                                            comm=_swap_comm(gp_down))
    s1_down = add_sibling(["w_down"], gp_down, r1_down)
    dh2, r2_down = _matmul(dup, w_upf, mode="nt", out_dtype=F32, name="mm_dh2", tm=s, tk=1408,
                           comm=_exchange_comm(s1_down))
    half_down = add_chips(["w_down"], s1_down, r2_down)
    gw_up = _matmul(h2, dup, mode="tn", out_dtype=BF16, name="mm_gw_up", tn=1408, tk=s, out_groups=N_CHIPS)
    dx1, dshift2, dscale2, g_pre2_grad = _prenorm_bwd(x1, dh2, dx2, g_pre2, scale2, "prenorm2_bwd")

    dy1, dgate1, g_post1_grad = _post_bwd(y1, gate1, g_post1, "post1_bwd", dxo=dx1)
    dmerged = _matmul(dy1, w_o, mode="nt", out_dtype=F32, name="mm_dmerged", tm=s)
    gw_out = _matmul(merged, dy1, mode="tn", out_dtype=BF16, name="mm_gw_out", tn=1024, tk=s)
    dy_a, dy_b, dz_big = _merge_bwd(dmerged, z_big, y_a, y_b, "merge_bwd")
    da = _matmul(dy_a, w_a, mode="nt", out_dtype=F32, name="mm_da", tm=s)
    gw_a = _matmul(a_act, dy_a, mode="tn", out_dtype=BF16, name="mm_gw_a", tn=1024, tk=s)
    do = _matmul(dy_b, w_b, mode="nt", out_dtype=BF16, name="mm_do", tm=s)
    gw_b = _matmul(o_att, dy_b, mode="tn", out_dtype=BF16, name="mm_gw_b", tn=1024, tk=s)
    mid = ["w_up", "w_out", "w_branch_a", "w_branch_b"]
    gp_mid = [gw_up, rows_of(gw_out), rows_of(gw_a), rows_of(gw_b)]
    (dz_big, g_ws, g_bs_t, g_ln_g, g_ln_b), r1_mid = _gmlp_bwd(z_big, da, dz_big, ln_g, ln_b, gm_w_s, b_s_t,
                                                                "gmlp_bwd", comm=_swap_comm(gp_mid))
    s1_mid = add_sibling(mid, gp_mid, r1_mid)
    (dq, dk, dv), r2_up_out = _attn_bwd(q_rot, kv_all, kr, o_att, do, lse, heads, "attn_bwd",
                                        comm=_exchange_comm(s1_mid[:2]))
    dq_big, dkv, dkk = _mla_bwd_mid(dq, dk, dv, rope_q, rope_k, heads, "mla_bwd_mid")
    gw_q = _matmul(qn, dq_big, mode="tn", out_dtype=F32, name="mm_gw_q", tn=1024, tk=s)
    dqn = _matmul(dq_big, w_q, mode="nt", out_dtype=F32, name="mm_dqn", tm=s, tk=1024)
    gw_kv = _matmul(kvn, dkv, mode="tn", out_dtype=BF16, name="mm_gw_kv", tn=1024, tk=s)
    dkvn = _matmul(dkv, w_kv, mode="nt", out_dtype=F32, name="mm_dkvn", tm=s, tk=1024)
    dz_lat, g_q, g_kv = _mla_bwd_post(z_lat, dqn, dkvn, dkk, q_g, kv_g, "mla_bwd_post")

    partial = {
        "gm_ln_g": g_ln_g, "gm_ln_b": g_ln_b, "gm_w_s": g_ws, "gm_b_s": g_bs_t[:, :gm_b_s.shape[0]].T,
        "q_norm_g": g_q, "kv_norm_g": g_kv, "post_norm1_g": g_post1_grad, "pre_norm2_g": g_pre2_grad,
        "conv_w": jnp.concatenate([gcw_g, gcw_v], axis=1), "conv_b": jnp.concatenate([gcb_g, gcb_v], axis=1),
        "post_norm2_g": g_post2_grad,
    }
    flat = jnp.concatenate([partial[n].reshape(-1) for n in SMALL_PARTIAL])
    n_small = flat.shape[0]
    rows_small = -(-n_small // (LANES * SUBLANES)) * SUBLANES
    flat = jnp.pad(flat, (0, rows_small * LANES - n_small)).reshape(rows_small, LANES)
    small_sum = _sum_leading(_all_gather(flat, "gather_small").reshape(N_DEV, rows_small, LANES), "sum_small")
    small_sum = small_sum.reshape(-1)
    small_grads, off = {}, 0
    for n in SMALL_PARTIAL:
        shape = (CONV_TAPS, 2 * ff) if n == "conv_w" else given[n].shape
        small_grads[n] = small_sum[off:off + partial[n].size].reshape(shape)
        off += partial[n].size
    small_grads["conv_w"] = lax.dynamic_slice(small_grads["conv_w"], (0, chip * conv_w.shape[1]), conv_w.shape)

    dh1, r2_a_b = _matmul(dz_big, w_in_big, mode="nt", out_dtype=F32, name="mm_dh1_big", tm=s,
                          comm=_exchange_comm(s1_mid[2:]))
    half_mid = add_chips(mid, s1_mid, list(r2_up_out) + list(r2_a_b))
    dh1 = _matmul(dz_lat, w_in_lat, mode="nt", out_dtype=F32, name="mm_dh1_lat", tm=s, tk=1024, add=dh1)
    gw_in_big, shared = _matmul(h1, dz_big, mode="tn", out_dtype=BF16, name="mm_gw_in_big", tn=1024, tk=s,
                                comm=_share_comm(half_down + half_mid))
    grads = dict(zip(["w_down"] + mid, shared), **small_grads)
    gw_in_lat = _matmul(h1, dz_lat, mode="tn", out_dtype=F32, name="mm_gw_in_lat", tn=1024, tk=s)

    gq = gw_q.reshape(ql, heads, HEAD_W)
    gq_pe = gq[:, :, NOPE:NOPE + ROPE] + _quarter_turn_back(gq[:, :, NOPE + ROPE:])
    g_pe = gw_in_lat[:, ql + kvl:ql + kvl + ROPE] + _quarter_turn_back(gw_in_lat[:, ql + kvl + ROPE:])
    last = ["w_in", "w_uq", "w_ukv"]
    gp_last = [
        _split_cols(jnp.concatenate([gw_in_big[:, :o_q], gw_in_lat[:, :ql + kvl].astype(BF16), g_pe.astype(BF16),
                                     gw_in_big[:, o_q:]], axis=1)),
        _split_cols(jnp.concatenate([gq[:, :, :NOPE], gq_pe], axis=2).reshape(ql, heads * (NOPE + ROPE)).astype(BF16)),
        _split_cols(gw_kv.reshape(kvl, 2, heads, NOPE).transpose(0, 2, 1, 3).reshape(kvl, heads * 2 * NOPE)),
    ]
    (grad_x, dshift1, dscale1, g_pre1_grad), r1_last = _prenorm_bwd(x2d, dh1, dx1, g_pre1, scale1, "prenorm1_bwd",
                                                                    comm=_swap_comm(gp_last))
    s1_last = add_sibling(last, gp_last, r1_last)

    dmod = jnp.concatenate([dshift1, dscale1, dgate1, dshift2, dscale2, dgate2, g_pre1_grad], axis=1)
    dmod_all = _all_gather(jnp.pad(dmod, ((0, SUBLANES - 1), (0, 0))), "gather_dmod")
    dmod_all = dmod_all.reshape(N_DEV, SUBLANES, (N_MOD + 1) * d)[:, 0]
    dmod_sum = _sum_leading(dmod_all.reshape(N_DEV, 1, (N_MOD + 1) * d), "sum_dmod")[0]
    grads["b_ada"], grads["pre_norm1_g"] = dmod_sum[:N_MOD * d], dmod_sum[N_MOD * d:]
    dmod_mine = lax.dynamic_slice(dmod_all, (0, chip * na), (N_DEV, na))
    grads["w_ada"] = _ada_bwd(c_all.T, dmod_mine, "ada_bwd")

    delta, new_m, new_v = {}, {}, {}

    def adamw(n, after=None):
        turn = (lambda a: a.T) if n == "w_in" else (lambda a: a)
        g_t = turn(grads[n])
        outs = _adamw(turn(given[n]), g_t, turn(given["m_" + n]), turn(given["v_" + n]), "adamw_" + n, after=after)
        grads[n] = turn(g_t)
        delta[n], new_m[n], new_v[n] = (turn(o) for o in outs)

    exchange_last = _exchange_comm(s1_last)
    in_flight, token = _comm_split_start(exchange_last, "rs_exchange_last_start", after=[dmod_sum, small_sum])
    for n in ["w_ada", "w_down"] + mid:
        adamw(n, after=token)
    s1_last, r2_last = _comm_split_wait(exchange_last, in_flight, delta[mid[-1]], "rs_exchange_last_wait")
    half_last = add_chips(last, s1_last, r2_last)
    grads.update(zip(last, _run_comm(_share_comm(half_last), "rs_share_last")))
    for n in last:
        adamw(n)

    def small_pack(prefix, source):
        v = jnp.concatenate([source[prefix + n].reshape(-1) for n in SMALL])
        rows = -(-v.shape[0] // (LANES * SUBLANES)) * SUBLANES
        return jnp.pad(v, (0, rows * LANES - v.shape[0])).reshape(rows, LANES)

    outs = _adamw(small_pack("", given), small_pack("", grads), small_pack("m_", given), small_pack("v_", given),
                  "adamw_small")
    off = 0
    for n in SMALL:
        size = given[n].size
        for store, packed_out in zip((delta, new_m, new_v), outs):
            store[n] = packed_out.reshape(-1)[off:off + size].reshape(given[n].shape)
        off += size

    return (loss, grad_x[None], *[grads[n] for n in WEIGHTS], *[delta[n] for n in WEIGHTS],
            *[new_m[n] for n in WEIGHTS], *[new_v[n] for n in WEIGHTS])
```

```python
import functools

import jax
import jax.numpy as jnp
from jax import lax
from jax.experimental import pallas as pl
from jax.experimental.pallas import tpu as pltpu

F32 = jnp.float32
BF16 = jnp.bfloat16
MESH = pl.DeviceIdType.MESH
HBM = pltpu.HBM

EPS = 1e-6
NOPE, ROPE, VHEAD = 128, 64, 128
HEAD_W = NOPE + 2 * ROPE
ROPE_THETA = 10000.0
CONV_TAPS = 3
N_MOD = 6
N_CHIPS, N_CORES, N_DEV = 4, 2, 8
ADAM_LR, ADAM_B1, ADAM_B2, ADAM_EPS, ADAM_WD, ADAM_STEP = 0.001, 0.9, 0.999, 1e-08, 0.01, 10

LANES = 128
SUBLANES = 8
VMEM_LIMIT = 56 * 2**20
MIDDLE_STAGE_AT = 70

BIG = ("w_in", "w_branch_a", "w_uq", "w_ukv", "w_branch_b", "w_out", "w_up", "w_down")
WEIGHTS = ("w_ada", "b_ada", "pre_norm1_g", "w_in", "gm_ln_g", "gm_ln_b", "gm_w_s", "gm_b_s", "w_branch_a",
           "q_norm_g", "w_uq", "kv_norm_g", "w_ukv", "w_branch_b", "w_out", "post_norm1_g", "pre_norm2_g",
           "w_up", "conv_w", "conv_b", "w_down", "post_norm2_g")
SMALL_PARTIAL = ("gm_ln_g", "gm_ln_b", "gm_w_s", "gm_b_s", "q_norm_g", "kv_norm_g", "post_norm1_g",
                 "pre_norm2_g", "conv_w", "conv_b", "post_norm2_g")
SMALL = ("b_ada", "pre_norm1_g") + SMALL_PARTIAL


def _div_tile(n, cap, mult=LANES):
    t = (min(cap, n) // mult) * mult
    while t >= mult:
        if n % t == 0:
            return t
        t -= mult
    return n


def _params(**kw):
    return pltpu.CompilerParams(vmem_limit_bytes=VMEM_LIMIT, **kw)


def _row_spec(width):
    return pl.BlockSpec((1, width), lambda *_: (0, 0))


def _gelu(x):
    k = 0.7978845608028654
    return 0.5 * x * (1.0 + jnp.tanh(k * (x + 0.044715 * x * x * x)))


def _gelu_grad(x):
    k = 0.7978845608028654
    t = jnp.tanh(k * (x + 0.044715 * x * x * x))
    return 0.5 * (1.0 + t) + 0.5 * x * (1.0 - t * t) * k * (1.0 + 3.0 * 0.044715 * x * x)


def _sigmoid(x):
    return 1.0 / (1.0 + jnp.exp(-x))


def _dot(a, b, dims):
    return lax.dot_general(a, b, (dims, ((), ())), preferred_element_type=F32)


NN = ((1,), (0,))
NT = ((1,), (1,))
TN = ((0,), (0,))


def _logical(arr):
    if arr.ndim == 2:
        return arr.shape[0], arr.shape[1], arr.shape[1]
    return arr.shape[1], arr.shape[0] * arr.shape[2], arr.shape[2]


def _tile_spec(ndim, group_w, blk_rows, blk_cols, row_of, col_of):
    if ndim == 2:
        return pl.BlockSpec((blk_rows, blk_cols), lambda i, j, k: (row_of(i, j, k), col_of(i, j, k)))
    per = group_w // blk_cols
    return pl.BlockSpec((None, blk_rows, blk_cols),
                        lambda i, j, k: (col_of(i, j, k) // per, row_of(i, j, k), col_of(i, j, k) % per))


def _matmul(a, b, *, mode, out_dtype, name, tm=512, tn=512, tk=2048, mul=None, add=None, out_groups=None, comm=None):
    ar, ac, agw = _logical(a)
    br, bc, bgw = _logical(b)
    if mode == "nn":
        m, kd, n = ar, ac, bc
        m_w, k_w, n_w = (), (agw,), (bgw,)
    elif mode == "nt":
        m, kd, n = ar, ac, br
        m_w, k_w, n_w = (), (agw, bgw), ()
    else:
        m, kd, n = ac, ar, bc
        m_w, k_w, n_w = (agw,), (), (bgw,)
    if out_groups is not None:
        n_w = n_w + (n // out_groups,)
    tm = _div_tile(min((m,) + m_w), tm, SUBLANES)
    tn = _div_tile(min((n,) + n_w), tn)
    tk = _div_tile(min((kd,) + k_w), tk)
    assert all(w % tn == 0 for w in n_w) and all(w % tk == 0 for w in k_w) and all(w % tm == 0 for w in m_w)
    nk = kd // tk
    dims = {"nn": NN, "nt": NT, "tn": TN}[mode]
    gi, gj, gk = (lambda i, j, k: i), (lambda i, j, k: j), (lambda i, j, k: k)
    if mode == "nn":
        a_spec = _tile_spec(a.ndim, agw, tm, tk, gi, gk)
        b_spec = _tile_spec(b.ndim, bgw, tk, tn, gk, gj)
    elif mode == "nt":
        a_spec = _tile_spec(a.ndim, agw, tm, tk, gi, gk)
        b_spec = _tile_spec(b.ndim, bgw, tn, tk, gj, gk)
    else:
        a_spec = _tile_spec(a.ndim, agw, tk, tm, gk, gi)
        b_spec = _tile_spec(b.ndim, bgw, tk, tn, gk, gj)
    in_specs, operands = [a_spec, b_spec], [a, b]
    if mul is not None:
        assert mul.shape == (m, tn)
        in_specs.append(pl.BlockSpec((tm, tn), lambda i, j, k: (i, 0)))
        operands.append(mul)
    if add is not None:
        in_specs.append(pl.BlockSpec((tm, tn), lambda i, j, k: (i, j)))
        operands.append(add)

    def body(*refs):
        a_ref, b_ref = refs[0], refs[1]
        pos = 2
        mul_ref = add_ref = None
        if mul is not None:
            mul_ref, pos = refs[pos], pos + 1
        if add is not None:
            add_ref, pos = refs[pos], pos + 1
        o_ref = refs[pos]

        def finish(r):
            if mul_ref is not None:
                r = r * mul_ref[...]
            if add_ref is not None:
                r = r + add_ref[...]
            o_ref[...] = r.astype(out_dtype)

        part = _dot(a_ref[...], b_ref[...], dims)
        if nk == 1:
            finish(part)
        else:
            acc_ref = refs[pos + 1]
            k = pl.program_id(2)

            @pl.when(k == 0)
            def _():
                acc_ref[...] = part

            @pl.when(k > 0)
            def _():
                acc_ref[...] += part

            @pl.when(k == nk - 1)
            def _():
                finish(acc_ref[...])

    if out_groups is None:
        out_spec, out_dims = _tile_spec(2, n, tm, tn, gi, gj), (m, n)
    else:
        out_spec, out_dims = _tile_spec(3, n // out_groups, tm, tn, gi, gj), (out_groups, m, n // out_groups)
    return _call(body, operands, comm, name=name, grid=(m // tm, n // tn, nk), in_specs=in_specs, out_specs=out_spec,
                 out_shape=jax.ShapeDtypeStruct(out_dims, out_dtype),
                 scratch_shapes=[] if nk == 1 else [pltpu.VMEM((tm, tn), F32)])


def _accumulate(ref, value):
    @pl.when(pl.program_id(0) == 0)
    def _():
        ref[...] = value

    @pl.when(pl.program_id(0) > 0)
    def _():
        ref[...] += value


def _colsum(v):
    return jnp.sum(v, axis=0, keepdims=True)


def _rowmean(v):
    return jnp.mean(v, axis=-1, keepdims=True)


def _prenorm(x, g, scale, shift, name):
    s, d = x.shape
    tb = _div_tile(s, 256, SUBLANES)

    def body(x_ref, g_ref, sc_ref, sh_ref, h_ref):
        xv = x_ref[...]
        r = lax.rsqrt(_rowmean(xv * xv) + EPS)
        h_ref[...] = ((xv * r) * g_ref[...] * (1.0 + sc_ref[...]) + sh_ref[...]).astype(BF16)

    blk = pl.BlockSpec((tb, d), lambda i: (i, 0))
    return pl.pallas_call(
        body, name=name, grid=(s // tb,), in_specs=[blk, _row_spec(d), _row_spec(d), _row_spec(d)],
        out_specs=blk, out_shape=jax.ShapeDtypeStruct((s, d), BF16), compiler_params=_params(),
    )(x, g, scale, shift)


def _post_pre(x, y, gate, pg, g2, scale2, shift2, name):
    s, d = x.shape
    tb = _div_tile(s, 256, SUBLANES)

    def body(x_ref, y_ref, gate_ref, pg_ref, g2_ref, sc_ref, sh_ref, x1_ref, h2_ref):
        yv = y_ref[...]
        rp = lax.rsqrt(_rowmean(yv * yv) + EPS)
        x1 = x_ref[...] + gate_ref[...] * ((yv * rp) * pg_ref[...])
        x1_ref[...] = x1
        r2 = lax.rsqrt(_rowmean(x1 * x1) + EPS)
        h2_ref[...] = ((x1 * r2) * g2_ref[...] * (1.0 + sc_ref[...]) + sh_ref[...]).astype(BF16)

    blk = pl.BlockSpec((tb, d), lambda i: (i, 0))
    return pl.pallas_call(
        body, name=name, grid=(s // tb,), in_specs=[blk, blk] + [_row_spec(d)] * 5,
        out_specs=[blk, blk],
        out_shape=[jax.ShapeDtypeStruct((s, d), F32), jax.ShapeDtypeStruct((s, d), BF16)],
        compiler_params=_params(),
    )(x, y, gate, pg, g2, scale2, shift2)


def _post_bwd(y, gate, pg, name, *, dxo=None, xin=None, target=None):
    s, d = y.shape
    tb = _div_tile(s, 256, SUBLANES)
    from_loss = target is not None

    def body(*refs):
        if from_loss:
            y_ref, gate_ref, pg_ref, xin_ref, t_ref, dy_ref, dgate_ref, dpg_ref, dxo_ref, loss_ref = refs
        else:
            y_ref, gate_ref, pg_ref, dxo_in_ref, dy_ref, dgate_ref, dpg_ref = refs
        yv = y_ref[...]
        rp = lax.rsqrt(_rowmean(yv * yv) + EPS)
        yh = yv * rp
        fn = yh * pg_ref[...]
        gate = gate_ref[...]
        if from_loss:
            err = xin_ref[...] + gate * fn - t_ref[...]
            dxo = err * (1.0 / d)
            dxo_ref[...] = dxo
            part = 0.5 * jnp.sum(_rowmean(err * err), axis=0, keepdims=True)
            _accumulate(loss_ref, jnp.broadcast_to(part, loss_ref.shape))
        else:
            dxo = dxo_in_ref[...]
        _accumulate(dgate_ref, _colsum(dxo * fn))
        dfn = dxo * gate
        _accumulate(dpg_ref, _colsum(dfn * yh))
        dyh = dfn * pg_ref[...]
        dy_ref[...] = (rp * (dyh - yh * _rowmean(dyh * yh))).astype(BF16)

    blk = pl.BlockSpec((tb, d), lambda i: (i, 0))
    in_specs = [blk, _row_spec(d), _row_spec(d)]
    out_specs = [blk, _row_spec(d), _row_spec(d)]
    out_shape = [jax.ShapeDtypeStruct((s, d), BF16), jax.ShapeDtypeStruct((1, d), F32),
                 jax.ShapeDtypeStruct((1, d), F32)]
    if from_loss:
        operands = (y, gate, pg, xin, target)
        in_specs += [blk, blk]
        out_specs += [blk, _row_spec(LANES)]
        out_shape += [jax.ShapeDtypeStruct((s, d), F32), jax.ShapeDtypeStruct((1, LANES), F32)]
    else:
        operands = (y, gate, pg, dxo)
        in_specs += [blk]
    return pl.pallas_call(
        body, name=name, grid=(s // tb,), in_specs=in_specs, out_specs=out_specs, out_shape=out_shape,
        compiler_params=_params(),
    )(*operands)


def _prenorm_bwd(xin, dh, dres, g, scale, name, comm=None):
    s, d = xin.shape
    tb = _div_tile(s, 256, SUBLANES)

    def body(x_ref, dh_ref, dres_ref, g_ref, sc_ref, dx_ref, dshift_ref, dscale_ref, dg_ref):
        xv = x_ref[...]
        r = lax.rsqrt(_rowmean(xv * xv) + EPS)
        xn = xv * r
        dh = dh_ref[...]
        g1 = g_ref[...]
        s1 = 1.0 + sc_ref[...]
        _accumulate(dshift_ref, _colsum(dh))
        _accumulate(dscale_ref, _colsum(dh * xn * g1))
        _accumulate(dg_ref, _colsum(dh * xn * s1))
        dxn = dh * g1 * s1
        dx_ref[...] = dres_ref[...] + r * (dxn - xn * _rowmean(dxn * xn))

    blk = pl.BlockSpec((tb, d), lambda i: (i, 0))
    return _call(
        body, (xin, dh, dres, g, scale), comm, name=name, grid=(s // tb,),
        in_specs=[blk, blk, blk, _row_spec(d), _row_spec(d)],
        out_specs=[blk, _row_spec(d), _row_spec(d), _row_spec(d)],
        out_shape=[jax.ShapeDtypeStruct((s, d), F32)] + [jax.ShapeDtypeStruct((1, d), F32)] * 3)


def _merge(z_big, y_a, y_b, name):
    s, d = y_a.shape
    tb = _div_tile(s, 256, SUBLANES)

    def body(zg_ref, ya_ref, yb_ref, o_ref):
        o_ref[...] = (_sigmoid(zg_ref[:, :d]) * ya_ref[...] + _sigmoid(zg_ref[:, d:]) * yb_ref[...]).astype(BF16)

    blk = pl.BlockSpec((tb, d), lambda i: (i, 0))
    return pl.pallas_call(
        body, name=name, grid=(s // tb,), in_specs=[pl.BlockSpec((tb, 2 * d), lambda i: (i, 1)), blk, blk],
        out_specs=blk, out_shape=jax.ShapeDtypeStruct((s, d), BF16), compiler_params=_params(),
    )(z_big, y_a, y_b)


def _merge_bwd(dmerged, z_big, y_a, y_b, name):
    s, d = y_a.shape
    tb = _div_tile(s, 256, SUBLANES)

    def body(dm_ref, zg_ref, ya_ref, yb_ref, dya_ref, dyb_ref, dz_ref):
        dm = dm_ref[...]
        sa, sb = _sigmoid(zg_ref[:, :d]), _sigmoid(zg_ref[:, d:])
        dya_ref[...] = (dm * sa).astype(BF16)
        dyb_ref[...] = (dm * sb).astype(BF16)
        dz_ref[:, :d] = (dm * ya_ref[...] * sa * (1.0 - sa)).astype(BF16)
        dz_ref[:, d:] = (dm * yb_ref[...] * sb * (1.0 - sb)).astype(BF16)

    blk = pl.BlockSpec((tb, d), lambda i: (i, 0))
    wide = pl.BlockSpec((tb, 2 * d), lambda i: (i, 1))
    return pl.pallas_call(
        body, name=name, grid=(s // tb,), in_specs=[blk, wide, blk, blk], out_specs=[blk, blk, wide],
        out_shape=[jax.ShapeDtypeStruct((s, d), BF16), jax.ShapeDtypeStruct((s, d), BF16),
                   jax.ShapeDtypeStruct((s, 4 * d), BF16)],
        compiler_params=_params(),
    )(dmerged, z_big, y_a, y_b)


def _causal_mask(ch):
    q = lax.broadcasted_iota(jnp.int32, (ch, ch), 0)
    p = lax.broadcasted_iota(jnp.int32, (ch, ch), 1)
    return (p <= q).astype(F32)


def _gmlp_norm(zc, lng, lnb, gw):
    u_pre, v_pre = zc[:, :gw], zc[:, gw:]
    vg = _gelu(v_pre)
    mu = _rowmean(vg)
    cen = vg - mu
    rstd = lax.rsqrt(_rowmean(cen * cen) + EPS)
    vhat = cen * rstd
    return u_pre, v_pre, _gelu(u_pre), vhat, rstd, vhat * lng + lnb


def _gmlp_fwd(z_big, ln_g, ln_b, w_s, b_s_t, name):
    s = z_big.shape[0]
    groups, ch, _ = w_s.shape
    gw = ln_g.shape[1]
    gd = gw // groups

    def body(z_ref, lng_ref, lnb_ref, ws_ref, bt_ref, a_ref):
        _, _, u, _, _, vn = _gmlp_norm(z_ref[...], lng_ref[...], lnb_ref[...], gw)
        mask = _causal_mask(ch)
        for g in range(groups):
            cols = slice(g * gd, (g + 1) * gd)
            wm = (ws_ref[g] * mask).astype(BF16)
            mixed = _dot(wm, vn[:, cols].astype(BF16), NN) + bt_ref[:, g:g + 1]
            a_ref[:, cols] = (u[:, cols] * mixed).astype(BF16)

    return pl.pallas_call(
        body, name=name, grid=(s // ch,),
        in_specs=[pl.BlockSpec((ch, 2 * gw), lambda n: (n, 0)), _row_spec(gw), _row_spec(gw),
                  pl.BlockSpec((groups, ch, ch), lambda n: (0, 0, 0)), pl.BlockSpec((ch, groups), lambda n: (0, 0))],
        out_specs=pl.BlockSpec((ch, gw), lambda n: (n, 0)),
        out_shape=jax.ShapeDtypeStruct((s, gw), BF16), compiler_params=_params(),
    )(z_big, ln_g, ln_b, w_s, b_s_t)


def _gmlp_bwd(z_big, da, dz_big, ln_g, ln_b, w_s, b_s_t, name, comm=None):
    s = z_big.shape[0]
    groups, ch, _ = w_s.shape
    gw = ln_g.shape[1]
    gd = gw // groups

    def body(z_ref, da_ref, dzin_ref, lng_ref, lnb_ref, ws_ref, bt_ref, dz_ref, gws_ref, gbt_ref, glng_ref, glnb_ref):
        del dzin_ref
        lng = lng_ref[...]
        u_pre, v_pre, u, vhat, rstd, vn = _gmlp_norm(z_ref[...], lng, lnb_ref[...], gw)
        da = da_ref[...]
        mask = _causal_mask(ch)
        first = pl.program_id(0) == 0
        dvn_parts = []
        lane = lax.broadcasted_iota(jnp.int32, (ch, LANES), 1)
        gb = jnp.zeros((ch, LANES), F32)
        for g in range(groups):
            cols = slice(g * gd, (g + 1) * gd)
            wm = (ws_ref[g] * mask).astype(BF16)
            vn_g = vn[:, cols].astype(BF16)
            mixed = _dot(wm, vn_g, NN) + bt_ref[:, g:g + 1]
            dz_ref[:, cols] = (da[:, cols] * mixed * _gelu_grad(u_pre[:, cols])).astype(BF16)
            dmixed = da[:, cols] * u[:, cols]
            dm16 = dmixed.astype(BF16)
            dvn_parts.append(_dot(wm, dm16, TN))
            gws = _dot(dm16, vn_g, NT) * mask

            @pl.when(first)
            def _(g=g, gws=gws):
                gws_ref[g] = gws

            @pl.when(jnp.logical_not(first))
            def _(g=g, gws=gws):
                gws_ref[g] += gws

            gb = gb + jnp.where(lane == g, jnp.sum(dmixed, axis=1, keepdims=True), 0.0)
        _accumulate(gbt_ref, gb)
        dvn = jnp.concatenate(dvn_parts, axis=1)
        _accumulate(glnb_ref, _colsum(dvn))
        _accumulate(glng_ref, _colsum(dvn * vhat))
        dvh = dvn * lng
        dvg = rstd * (dvh - _rowmean(dvh) - vhat * _rowmean(dvh * vhat))
        dz_ref[:, gw:] = (dvg * _gelu_grad(v_pre)).astype(BF16)

    zspec = pl.BlockSpec((ch, 2 * gw), lambda n: (n, 0))
    return _call(
        body, (z_big, da, dz_big, ln_g, ln_b, w_s, b_s_t), comm, name=name, grid=(s // ch,),
        in_specs=[zspec, pl.BlockSpec((ch, gw), lambda n: (n, 0)), pl.BlockSpec(memory_space=HBM),
                  _row_spec(gw), _row_spec(gw), pl.BlockSpec((groups, ch, ch), lambda n: (0, 0, 0)),
                  pl.BlockSpec((ch, groups), lambda n: (0, 0))],
        out_specs=[zspec, pl.BlockSpec((groups, ch, ch), lambda n: (0, 0, 0)),
                   pl.BlockSpec((ch, LANES), lambda n: (0, 0)), _row_spec(gw), _row_spec(gw)],
        out_shape=[jax.ShapeDtypeStruct(dz_big.shape, BF16), jax.ShapeDtypeStruct((groups, ch, ch), F32),
                   jax.ShapeDtypeStruct((ch, LANES), F32), jax.ShapeDtypeStruct((1, gw), F32),
                   jax.ShapeDtypeStruct((1, gw), F32)],
        input_output_aliases={2: 0})


def _mla_prep(z_lat, q_g, kv_g, rope_k, name):
    s, latw = z_lat.shape
    ql, kvl = q_g.shape[1], kv_g.shape[1]
    tb = _div_tile(s, 256, SUBLANES)

    def body(z_ref, qg_ref, kvg_ref, t_ref, qn_ref, kvn_ref, kr_ref):
        q = z_ref[:, :ql]
        qn_ref[...] = ((q * lax.rsqrt(_rowmean(q * q) + EPS)) * qg_ref[...]).astype(BF16)
        kv = z_ref[:, ql:ql + kvl]
        kvn_ref[...] = ((kv * lax.rsqrt(_rowmean(kv * kv) + EPS)) * kvg_ref[...]).astype(BF16)
        kk = z_ref[:, ql + kvl:] * t_ref[...]
        kr_ref[...] = (kk + pltpu.roll(kk, ROPE, axis=1)).astype(BF16)

    return pl.pallas_call(
        body, name=name, grid=(s // tb,),
        in_specs=[pl.BlockSpec((tb, latw), lambda i: (i, 0)), _row_spec(ql), _row_spec(kvl),
                  pl.BlockSpec((tb, 2 * ROPE), lambda i: (i, 0))],
        out_specs=[pl.BlockSpec((tb, ql), lambda i: (i, 0)), pl.BlockSpec((tb, kvl), lambda i: (i, 0)),
                   pl.BlockSpec((tb, 2 * ROPE), lambda i: (i, 0))],
        out_shape=[jax.ShapeDtypeStruct((s, ql), BF16), jax.ShapeDtypeStruct((s, kvl), BF16),
                   jax.ShapeDtypeStruct((s, 2 * ROPE), BF16)],
        compiler_params=_params(),
    )(z_lat, q_g, kv_g, rope_k)


def _scores(q, k, kr, on_diagonal):
    s = _dot(q[:, :NOPE], k, NT) + _dot(q[:, NOPE:], kr, NT)
    if not on_diagonal:
        return s
    rows = lax.broadcasted_iota(jnp.int32, s.shape, 0)
    cols = lax.broadcasted_iota(jnp.int32, s.shape, 1)
    return jnp.where(cols <= rows, s, -1e30)


def _attn_fwd(q, kv, kr, heads, name, comm=None):
    s = q.shape[0]
    t = _div_tile(s, 512)
    nb = s // t
    hp = 2 if heads % 2 == 0 else 1

    def body(q_ref, k_ref, kr_ref, v_ref, o_ref, lse_ref, m_ref, l_ref, acc_ref):
        i, j = pl.program_id(1), pl.program_id(2)

        @pl.when(j == 0)
        def _():
            m_ref[...] = jnp.full(m_ref.shape, -1e30, F32)
            l_ref[...] = jnp.zeros(l_ref.shape, F32)
            acc_ref[...] = jnp.zeros(acc_ref.shape, F32)

        def step(on_diagonal):
            krv = kr_ref[...]
            for h in range(hp):
                vc = slice(h * VHEAD, (h + 1) * VHEAD)
                sc = _scores(q_ref[:, h * HEAD_W:(h + 1) * HEAD_W], k_ref[:, h * NOPE:(h + 1) * NOPE], krv, on_diagonal)
                m_old = m_ref[h]
                m_new = jnp.maximum(m_old, jnp.max(sc, axis=-1, keepdims=True))
                p = jnp.exp(sc - m_new)
                alpha = jnp.exp(m_old - m_new)
                l_new = alpha * l_ref[h] + jnp.sum(p, axis=-1, keepdims=True)
                acc = alpha * acc_ref[:, vc] + _dot(p.astype(BF16), v_ref[:, vc], NN)
                if on_diagonal:
                    o_ref[:, vc] = (acc / l_new).astype(BF16)
                    lse_ref[h] = jnp.broadcast_to(m_new + jnp.log(l_new), (t, LANES))
                else:
                    m_ref[h], l_ref[h], acc_ref[:, vc] = m_new, l_new, acc

        pl.when(j < i)(lambda: step(False))
        pl.when(j == i)(lambda: step(True))

    kidx = lambda off: (lambda h, i, j: (jnp.minimum(i, j), off(h)))
    return _call(
        body, (q, kv, kr, kv), comm, name=name, grid=(heads // hp, nb, nb),
        in_specs=[pl.BlockSpec((t, hp * HEAD_W), lambda h, i, j: (i, h)),
                  pl.BlockSpec((t, hp * NOPE), kidx(lambda h: h)),
                  pl.BlockSpec((t, 2 * ROPE), kidx(lambda h: 0)),
                  pl.BlockSpec((t, hp * VHEAD), kidx(lambda h: heads // hp + h))],
        out_specs=[pl.BlockSpec((t, hp * VHEAD), lambda h, i, j: (i, h)),
                   pl.BlockSpec((hp, t, LANES), lambda h, i, j: (h, i, 0))],
        out_shape=[jax.ShapeDtypeStruct((s, heads * VHEAD), BF16), jax.ShapeDtypeStruct((heads, s, LANES), F32)],
        scratch_shapes=[pltpu.VMEM((hp, t, 1), F32), pltpu.VMEM((hp, t, 1), F32), pltpu.VMEM((t, hp * VHEAD), F32)])


def _attn_bwd(q, kv, kr, o, do, lse, heads, name, comm=None):
    s = q.shape[0]
    t = _div_tile(s, 512)
    nb = s // t
    hp = 2 if heads % 2 == 0 else 1

    def body(q_ref, k_ref, kr_ref, v_ref, o_ref, do_ref, lse_ref, dq_ref, dk_ref, dv_ref, dk_acc, dv_acc):
        j, i = pl.program_id(1), pl.program_id(2)

        @pl.when(jnp.logical_and(j == 0, i == 0))
        def _():
            dq_ref[...] = jnp.zeros(dq_ref.shape, F32)

        def step(on_diagonal):
            krv = kr_ref[...]
            rows = pl.ds(pl.multiple_of(i * t, t), t)
            for h in range(hp):
                qc, kc, vc = (slice(h * w, (h + 1) * w) for w in (HEAD_W, NOPE, VHEAD))
                qv, kn, do_v = q_ref[:, qc], k_ref[:, kc], do_ref[:, vc]
                p = jnp.exp(_scores(qv, kn, krv, on_diagonal) - lse_ref[h][:, :1])
                dp = _dot(do_v, v_ref[:, vc], NT)
                delta = jnp.sum(do_v.astype(F32) * o_ref[:, vc].astype(F32), axis=-1, keepdims=True)
                ds = (p * (dp - delta)).astype(BF16)
                dq_ref[rows, h * HEAD_W:h * HEAD_W + NOPE] += _dot(ds, kn, NN)
                dq_ref[rows, h * HEAD_W + NOPE:(h + 1) * HEAD_W] += _dot(ds, krv, NN)
                dv_part, dk_part = _dot(p.astype(BF16), do_v, TN), _dot(ds, qv, TN)
                if on_diagonal:
                    dv_acc[:, vc], dk_acc[:, qc] = dv_part, dk_part
                else:
                    dv_acc[:, vc] += dv_part
                    dk_acc[:, qc] += dk_part

        pl.when(i == j)(lambda: step(True))
        pl.when(i > j)(lambda: step(False))

        @pl.when(i == nb - 1)
        def _():
            dk_ref[...] = dk_acc[...].astype(BF16)
            dv_ref[...] = dv_acc[...].astype(BF16)

    qidx = lambda h, j, i: (jnp.maximum(i, j), h)
    return _call(
        body, (q, kv, kr, kv, o, do, lse), comm, name=name, grid=(heads // hp, nb, nb),
        in_specs=[pl.BlockSpec((t, hp * HEAD_W), qidx),
                  pl.BlockSpec((t, hp * NOPE), lambda h, j, i: (j, h)),
                  pl.BlockSpec((t, 2 * ROPE), lambda h, j, i: (j, 0)),
                  pl.BlockSpec((t, hp * VHEAD), lambda h, j, i: (j, heads // hp + h)),
                  pl.BlockSpec((t, hp * VHEAD), qidx), pl.BlockSpec((t, hp * VHEAD), qidx),
                  pl.BlockSpec((hp, t, LANES), lambda h, j, i: (h, jnp.maximum(i, j), 0))],
        out_specs=[pl.BlockSpec((s, hp * HEAD_W), lambda h, j, i: (0, h)),
                   pl.BlockSpec((t, hp * HEAD_W), lambda h, j, i: (j, h)),
                   pl.BlockSpec((t, hp * VHEAD), lambda h, j, i: (j, h))],
        out_shape=[jax.ShapeDtypeStruct((s, heads * HEAD_W), F32), jax.ShapeDtypeStruct((s, heads * HEAD_W), BF16),
                   jax.ShapeDtypeStruct((s, heads * VHEAD), BF16)],
        scratch_shapes=[pltpu.VMEM((t, hp * HEAD_W), F32), pltpu.VMEM((t, hp * VHEAD), F32)])


def _mla_bwd_mid(dq, dk, dv, rope_q, rope_k, heads, name):
    s = dq.shape[0]
    tb = _div_tile(s, 256, SUBLANES)

    def body(dq_ref, dk_ref, dv_ref, tq_ref, tk_ref, dqb_ref, dkv_ref, dkk_ref):
        tq = tq_ref[...]
        dkr = jnp.zeros((tb, 2 * ROPE), F32)
        for h in range(heads):
            cols = slice(h * HEAD_W, (h + 1) * HEAD_W)
            dqb_ref[:, cols] = (dq_ref[:, cols] * tq).astype(BF16)
            dkv_ref[:, h * NOPE:(h + 1) * NOPE] = dk_ref[:, h * HEAD_W:h * HEAD_W + NOPE]
            dkr = dkr + dk_ref[:, h * HEAD_W + NOPE:(h + 1) * HEAD_W].astype(F32)
        dkv_ref[:, heads * NOPE:] = dv_ref[...]
        dkk_ref[...] = (dkr + pltpu.roll(dkr, ROPE, axis=1)) * tk_ref[...]

    wq, wv = heads * HEAD_W, heads * VHEAD
    return pl.pallas_call(
        body, name=name, grid=(s // tb,),
        in_specs=[pl.BlockSpec((tb, wq), lambda i: (i, 0)), pl.BlockSpec((tb, wq), lambda i: (i, 0)),
                  pl.BlockSpec((tb, wv), lambda i: (i, 0)), pl.BlockSpec((tb, HEAD_W), lambda i: (i, 0)),
                  pl.BlockSpec((tb, 2 * ROPE), lambda i: (i, 0))],
        out_specs=[pl.BlockSpec((tb, wq), lambda i: (i, 0)), pl.BlockSpec((tb, heads * NOPE + wv), lambda i: (i, 0)),
                   pl.BlockSpec((tb, 2 * ROPE), lambda i: (i, 0))],
        out_shape=[jax.ShapeDtypeStruct((s, wq), BF16), jax.ShapeDtypeStruct((s, heads * NOPE + wv), BF16),
                   jax.ShapeDtypeStruct((s, 2 * ROPE), F32)],
        compiler_params=_params(),
    )(dq, dk, dv, rope_q, rope_k)


def _mla_bwd_post(z_lat, dqn, dkvn, dkk, q_g, kv_g, name):
    s, latw = z_lat.shape
    ql, kvl = q_g.shape[1], kv_g.shape[1]
    tb = _div_tile(s, 256, SUBLANES)

    def norm_bwd(xv, dn, g, dg_ref):
        r = lax.rsqrt(_rowmean(xv * xv) + EPS)
        xh = xv * r
        _accumulate(dg_ref, _colsum(dn * xh))
        dxh = dn * g
        return r * (dxh - xh * _rowmean(dxh * xh))

    def body(z_ref, dqn_ref, dkvn_ref, dkk_ref, qg_ref, kvg_ref, dz_ref, gq_ref, gkv_ref):
        dz_ref[:, :ql] = norm_bwd(z_ref[:, :ql], dqn_ref[...], qg_ref[...], gq_ref).astype(BF16)
        dz_ref[:, ql:ql + kvl] = norm_bwd(z_ref[:, ql:ql + kvl], dkvn_ref[...], kvg_ref[...], gkv_ref).astype(BF16)
        dz_ref[:, ql + kvl:] = dkk_ref[...].astype(BF16)

    return pl.pallas_call(
        body, name=name, grid=(s // tb,),
        in_specs=[pl.BlockSpec((tb, latw), lambda i: (i, 0)), pl.BlockSpec((tb, ql), lambda i: (i, 0)),
                  pl.BlockSpec((tb, kvl), lambda i: (i, 0)), pl.BlockSpec((tb, 2 * ROPE), lambda i: (i, 0)),
                  _row_spec(ql), _row_spec(kvl)],
        out_specs=[pl.BlockSpec((tb, latw), lambda i: (i, 0)), _row_spec(ql), _row_spec(kvl)],
        out_shape=[jax.ShapeDtypeStruct((s, latw), BF16), jax.ShapeDtypeStruct((1, ql), F32),
                   jax.ShapeDtypeStruct((1, kvl), F32)],
        compiler_params=_params(),
    )(z_lat, dqn, dkvn, dkk, q_g, kv_g)


def _shift_down(x, n):
    rows = lax.broadcasted_iota(jnp.int32, x.shape, 0)
    return jnp.where(rows >= n, pltpu.roll(x, n, axis=0), 0.0)


def _shift_up(x, n):
    s = x.shape[0]
    rows = lax.broadcasted_iota(jnp.int32, x.shape, 0)
    return jnp.where(rows < s - n, pltpu.roll(x, s - n, axis=0), 0.0)


def _conv(pre, w_ref, b_ref):
    return (w_ref[2:3, :] * pre + w_ref[1:2, :] * _shift_down(pre, 1) + w_ref[0:1, :] * _shift_down(pre, 2)
            + b_ref[...])


def _conv_fwd(up_pre, conv_w, conv_b, name):
    s, ff2 = up_pre.shape
    ff = ff2 // 2
    tc = _div_tile(ff, 256)
    nb = ff // tc

    def body(pg_ref, pv_ref, wg_ref, wv_ref, bg_ref, bv_ref, act_ref):
        gate = _conv(pg_ref[...].astype(F32), wg_ref, bg_ref)
        val = _conv(pv_ref[...].astype(F32), wv_ref, bv_ref)
        act_ref[...] = (gate * _sigmoid(gate) * val).astype(BF16)

    def col(rows, off):
        return pl.BlockSpec((rows, tc), lambda j: (0, j + off))

    return pl.pallas_call(
        body, name=name, grid=(nb,),
        in_specs=[col(s, 0), col(s, nb), col(CONV_TAPS, 0), col(CONV_TAPS, nb), col(1, 0), col(1, nb)],
        out_specs=col(s, 0), out_shape=jax.ShapeDtypeStruct((s, ff), BF16), compiler_params=_params(),
    )(up_pre, up_pre, conv_w, conv_w, conv_b, conv_b)


def _conv_bwd(up_pre, dact, conv_w, conv_b, name, comm=None):
    s, ff2 = up_pre.shape
    ff = ff2 // 2
    tc = _div_tile(ff, 256)
    nb = ff // tc

    def half(pre, dx, w_ref, dpre_ref, gw_ref, gb_ref):
        gb_ref[...] = _colsum(dx)
        gw_ref[0:1, :] = _colsum(dx * _shift_down(pre, 2))
        gw_ref[1:2, :] = _colsum(dx * _shift_down(pre, 1))
        gw_ref[2:3, :] = _colsum(dx * pre)
        dpre_ref[...] = (w_ref[2:3, :] * dx + w_ref[1:2, :] * _shift_up(dx, 1)
                         + w_ref[0:1, :] * _shift_up(dx, 2)).astype(BF16)

    def body(pg_ref, pv_ref, da_ref, wg_ref, wv_ref, bg_ref, bv_ref, dup_ref, gwg_ref, gwv_ref, gbg_ref, gbv_ref):
        pre_g, pre_v = pg_ref[...].astype(F32), pv_ref[...].astype(F32)
        gate = _conv(pre_g, wg_ref, bg_ref)
        val = _conv(pre_v, wv_ref, bv_ref)
        da = da_ref[...].astype(F32)
        sg = _sigmoid(gate)
        half(pre_v, da * gate * sg, wv_ref, dup_ref.at[1], gwv_ref, gbv_ref)
        half(pre_g, da * val * sg * (1.0 + gate * (1.0 - sg)), wg_ref, dup_ref.at[0], gwg_ref, gbg_ref)

    def col(rows, off):
        return pl.BlockSpec((rows, tc), lambda j: (0, j + off))

    return _call(
        body, (up_pre, up_pre, dact, conv_w, conv_w, conv_b, conv_b), comm, name=name, grid=(nb,),
        in_specs=[col(s, 0), col(s, nb), col(s, 0), col(CONV_TAPS, 0), col(CONV_TAPS, nb), col(1, 0), col(1, nb)],
        out_specs=[pl.BlockSpec((2, s, tc), lambda j: (0, 0, j)), col(CONV_TAPS, 0), col(CONV_TAPS, 0),
                   col(1, 0), col(1, 0)],
        out_shape=[jax.ShapeDtypeStruct((2, s, ff), BF16)] + [jax.ShapeDtypeStruct((CONV_TAPS, ff), F32)] * 2
        + [jax.ShapeDtypeStruct((1, ff), F32)] * 2)


def _ada_fwd(c_all, w, b, name):
    nseq, d = c_all.shape
    na = w.shape[1]
    tn = _div_tile(na, 512)

    def body(c_ref, w_ref, b_ref, o_ref):
        cv = c_ref[...]
        sc = cv * _sigmoid(cv)
        o_ref[...] = jnp.dot(sc, w_ref[...], preferred_element_type=F32, precision=lax.Precision.HIGHEST) + b_ref[...]

    return pl.pallas_call(
        body, name=name, grid=(na // tn,),
        in_specs=[pl.BlockSpec((nseq, d), lambda j: (0, 0)), pl.BlockSpec((d, tn), lambda j: (0, j)),
                  pl.BlockSpec((1, tn), lambda j: (0, j))],
        out_specs=pl.BlockSpec((nseq, tn), lambda j: (0, j)),
        out_shape=jax.ShapeDtypeStruct((nseq, na), F32), compiler_params=_params(),
    )(c_all, w, b)


def _ada_bwd(c_all_t, dmod, name):
    d, nseq = c_all_t.shape
    na = dmod.shape[1]
    tm, tn = _div_tile(d, 256, SUBLANES), _div_tile(na, 512)

    def body(c_ref, dm_ref, o_ref):
        cv = c_ref[...]
        sc = cv * _sigmoid(cv)
        acc = sc[:, 0:1] * dm_ref[0:1, :]
        for bi in range(1, nseq):
            acc = acc + sc[:, bi:bi + 1] * dm_ref[bi:bi + 1, :]
        o_ref[...] = acc

    return pl.pallas_call(
        body, name=name, grid=(d // tm, na // tn),
        in_specs=[pl.BlockSpec((tm, nseq), lambda i, j: (i, 0)), pl.BlockSpec((nseq, tn), lambda i, j: (0, j))],
        out_specs=pl.BlockSpec((tm, tn), lambda i, j: (i, j)),
        out_shape=jax.ShapeDtypeStruct((d, na), F32), compiler_params=_params(),
    )(c_all_t, dmod)


def _adamw(w, g, m, v, name, comm=None, after=None):
    rows, cols = w.shape
    tb = _div_tile(rows, max(SUBLANES, (256 * 1024) // cols // SUBLANES * SUBLANES), SUBLANES)
    c1 = 1.0 / (1.0 - ADAM_B1 ** ADAM_STEP)
    c2 = 1.0 / (1.0 - ADAM_B2 ** ADAM_STEP)

    def body(*refs):
        w_ref, g_ref, m_ref, v_ref = refs[:4]
        d_ref, nm_ref, nv_ref = refs[-3:]
        gv = g_ref[...]
        nm = ADAM_B1 * m_ref[...] + (1.0 - ADAM_B1) * gv
        nv = ADAM_B2 * v_ref[...] + (1.0 - ADAM_B2) * (gv * gv)
        nm_ref[...] = nm
        nv_ref[...] = nv
        d_ref[...] = -ADAM_LR * ((nm * c1) / (jnp.sqrt(nv * c2) + ADAM_EPS) + ADAM_WD * w_ref[...])

    blk = pl.BlockSpec((tb, cols), lambda i: (i, 0))
    operands, in_specs = (w, g, m, v), [blk] * 4
    if after is not None:
        operands, in_specs = operands + (after,), in_specs + [pl.BlockSpec(after.shape, lambda i: (0, 0))]
    return _call(body, operands, comm, name=name, grid=(rows // tb,), in_specs=in_specs, out_specs=[blk] * 3,
                 out_shape=[jax.ShapeDtypeStruct((rows, cols), F32)] * 3)


def _sum_leading(parts, name):
    n, rows, cols = parts.shape
    tb = _div_tile(rows, 512, SUBLANES)

    def body(p_ref, o_ref):
        acc = p_ref[0]
        for k in range(1, n):
            acc = acc + p_ref[k]
        o_ref[...] = acc

    return pl.pallas_call(
        body, name=name, grid=(rows // tb,), in_specs=[pl.BlockSpec((n, tb, cols), lambda i: (0, i, 0))],
        out_specs=pl.BlockSpec((tb, cols), lambda i: (i, 0)),
        out_shape=jax.ShapeDtypeStruct((rows, cols), F32), compiler_params=_params(),
    )(parts)


def _place():
    x, y, c = lax.axis_index("x"), lax.axis_index("y"), lax.axis_index("c")
    return x, y, c, [(1 - x, y), (x, 1 - y), (1 - x, 1 - y)]


def _all_gather(block, name):
    m_per, n = block.shape

    def body(x_ref, out_ref, send_sems, recv_sems, local_sem):
        x, y, c, chips = _place()
        me, sibling = (x, y, c), (x, y, 1 - c)

        def rows(px, py, pc):
            return out_ref.at[pl.ds((4 * px + 2 * py + pc) * m_per, m_per), :]

        def copy(k, blk, to, src=None):
            return pltpu.make_async_remote_copy(
                src_ref=rows(*blk) if src is None else src, dst_ref=rows(*blk), send_sem=send_sems.at[k],
                recv_sem=recv_sems.at[k], device_id=to, device_id_type=MESH)

        mine = pltpu.make_async_copy(x_ref, rows(*me), local_sem)
        mine.start()
        first = [copy(0, me, sibling, src=x_ref)]
        first += [copy(1 + j, me, (*chip, c), src=x_ref) for j, chip in enumerate(chips)]
        for cp in first:
            cp.start()
        passed = [copy(4 + j, (*chip, c), sibling) for j, chip in enumerate(chips)]
        for j, chip in enumerate(chips):
            copy(1 + j, (*chip, c), me).wait_recv()
            passed[j].start()
        copy(0, sibling, me).wait_recv()
        for j, chip in enumerate(chips):
            copy(4 + j, (*chip, 1 - c), me).wait_recv()
        for cp in first + passed:
            cp.wait_send()
        mine.wait()

    return pl.pallas_call(
        body, name=name, out_shape=jax.ShapeDtypeStruct((N_DEV * m_per, n), block.dtype),
        in_specs=[pl.BlockSpec(memory_space=pltpu.VMEM)], out_specs=pl.BlockSpec(memory_space=pltpu.VMEM),
        scratch_shapes=[pltpu.SemaphoreType.DMA((7,)), pltpu.SemaphoreType.DMA((7,)), pltpu.SemaphoreType.DMA],
        compiler_params=_params(),
    )(block)


def _hbm_specs(n):
    return [pl.BlockSpec(memory_space=HBM)] * n


def _half_rows(ref, half, lead=None):
    h = ref.shape[-2] // 2
    rows = pl.ds(pl.multiple_of(half * h, 2 * SUBLANES), h)
    return ref.at[rows, :] if lead is None else ref.at[lead, rows, :]


class _Comm:
    def __init__(self, operands, out_shape, sem_dims, build, aliases=None):
        self.operands, self.out_shape, self.sem_dims = list(operands), list(out_shape), list(sem_dims)
        self.scratch = [pltpu.SemaphoreType.DMA(d) for d in sem_dims]
        self.build, self.aliases = build, dict(aliases or {})


class _SemGrid:
    def __init__(self, sems, dims):
        self.sems, self.dims, self.at = list(sems), tuple(dims), self

    def __getitem__(self, index):
        index = index if isinstance(index, tuple) else (index,)
        flat = 0
        for i, d in zip(index, self.dims):
            flat = flat * d + i
        return self.sems[flat]


def _call(body, operands, comm=None, *, name, grid, in_specs, out_specs, out_shape, scratch_shapes=(),
          input_output_aliases=None):
    aliases = dict(input_output_aliases or {})
    if comm is None:
        return pl.pallas_call(
            body, name=name, grid=grid, in_specs=in_specs, out_specs=out_specs, out_shape=out_shape,
            scratch_shapes=list(scratch_shapes), input_output_aliases=aliases, compiler_params=_params())(*operands)
    single = not isinstance(out_shape, (list, tuple))
    outs = [out_shape] if single else list(out_shape)
    ospecs = [out_specs] if single else list(out_specs)
    n_in, n_out, n_scr = len(operands), len(outs), len(scratch_shapes)
    c_in, c_out = len(comm.operands), len(comm.out_shape)
    for i, o in comm.aliases.items():
        aliases[n_in + i] = n_out + o

    def hosted(*refs):
        ins, c_ins = refs[:n_in], refs[n_in:n_in + c_in]
        o0 = n_in + c_in
        o_refs, c_outs = refs[o0:o0 + n_out], refs[o0 + n_out:o0 + n_out + c_out]
        s0 = o0 + n_out + c_out
        scr, sems = refs[s0:s0 + n_scr], refs[s0 + n_scr:]
        stages = comm.build(c_ins, c_outs, sems)
        step, n_steps = 0, 1
        for dim, size in enumerate(grid):
            step, n_steps = step * size + pl.program_id(dim), n_steps * size
        pl.when(step == 0)(stages[0])
        body(*ins, *o_refs, *scr)
        for stage in stages[1:-1]:
            pl.when(step == (n_steps * MIDDLE_STAGE_AT) // 100)(stage)
        pl.when(step == n_steps - 1)(stages[-1])

    res = pl.pallas_call(
        hosted, name=name, grid=grid, in_specs=list(in_specs) + _hbm_specs(c_in),
        out_specs=ospecs + _hbm_specs(c_out), out_shape=outs + comm.out_shape,
        scratch_shapes=list(scratch_shapes) + comm.scratch, input_output_aliases=aliases,
        compiler_params=_params())(*operands, *comm.operands)
    return (res[0] if single else res[:n_out]), res[n_out:]


def _run_comm(comm, name):
    c_in, c_out = len(comm.operands), len(comm.out_shape)

    def body(*refs):
        for stage in comm.build(refs[:c_in], refs[c_in:c_in + c_out], refs[c_in + c_out:]):
            stage()

    return pl.pallas_call(
        body, name=name, in_specs=_hbm_specs(c_in), out_specs=_hbm_specs(c_out), out_shape=comm.out_shape,
        scratch_shapes=comm.scratch, input_output_aliases=comm.aliases, compiler_params=_params())(*comm.operands)


def _gather_comm(shards):
    nw = len(shards)

    def build(in_refs, out_refs, sems):
        send_sems, recv_sems = sems
        x, y, c, chips = _place()
        me, sibling = (x, y, c), (x, y, 1 - c)
        across_x, across_y, diagonal = chips

        def copy(w, k, block, part, to, src=None):
            rows = out_refs[w].shape[1]
            size = rows // 2 if part[0] == 0 else rows // 4
            first_row = part[1] * (rows // 2) + (part[2] * size if part[0] else 0)
            dst = out_refs[w].at[2 * block[0] + block[1], pl.ds(pl.multiple_of(first_row, 2 * SUBLANES), size), :]
            return pltpu.make_async_remote_copy(
                src_ref=dst if src is None else src, dst_ref=dst, send_sem=send_sems.at[w, k],
                recv_sem=recv_sems.at[w, k], device_id=to, device_id_type=MESH)

        first = [copy(w, j, (x, y), (0, c), (*chip, c), src=_half_rows(in_refs[w], c))
                 for w in range(nw) for j, chip in enumerate((across_x, across_y))]
        passed = [[copy(w, 2, across_x, (1, c, 0), (*across_y, c)), copy(w, 3, across_y, (1, c, 1), (*across_x, c)),
                   copy(w, 4, across_x, (0, c), sibling), copy(w, 5, across_y, (0, c), sibling)] for w in range(nw)]
        last = [[copy(w, 6, diagonal, (1, c, 0), sibling), copy(w, 7, diagonal, (1, c, 1), sibling)]
                for w in range(nw)]

        def start():
            for cp in first:
                cp.start()

        def middle():
            for w in range(nw):
                copy(w, 0, across_x, (0, c), me).wait_recv()
                copy(w, 1, across_y, (0, c), me).wait_recv()
                for cp in passed[w]:
                    cp.start()

        def finish():
            for w in range(nw):
                copy(w, 2, diagonal, (1, c, 0), me).wait_recv()
                copy(w, 3, diagonal, (1, c, 1), me).wait_recv()
                for cp in last[w]:
                    cp.start()
            for w in range(nw):
                for k, block, part in ((4, across_x, (0, 1 - c)), (5, across_y, (0, 1 - c)),
                                       (6, diagonal, (1, 1 - c, 0)), (7, diagonal, (1, 1 - c, 1))):
                    copy(w, k, block, part, me).wait_recv()
            for cp in first + sum(passed, []) + sum(last, []):
                cp.wait_send()

        return start, middle, finish

    return _Comm(shards, [jax.ShapeDtypeStruct((N_CHIPS,) + w.shape, w.dtype) for w in shards],
                 [(nw, 8), (nw, 8)], build)


def _swap_comm(gs):
    nw = len(gs)

    def build(in_refs, out_refs, sems):
        send_sems, recv_sems = sems
        x, y, c, _ = _place()
        cps = []
        for w in range(nw):
            h = in_refs[w].shape[1] // 2
            src = in_refs[w].at[:, pl.ds(pl.multiple_of((1 - c) * h, 2 * SUBLANES), h), :]
            cps.append(pltpu.make_async_remote_copy(
                src_ref=src, dst_ref=out_refs[w], send_sem=send_sems.at[w], recv_sem=recv_sems.at[w],
                device_id=(x, y, 1 - c), device_id_type=MESH))

        def start():
            for cp in cps:
                cp.start()

        def finish():
            for cp in cps:
                cp.wait()

        return start, finish

    return _Comm(gs, [jax.ShapeDtypeStruct((N_CHIPS, g.shape[1] // 2, g.shape[2]), g.dtype) for g in gs],
                 [(nw,), (nw,)], build)


def _exchange_comm(s1s):
    nw = len(s1s)

    def build(in_refs, out_refs, sems):
        send_sems, recv_sems = sems
        x, y, c, chips = _place()
        cps = [pltpu.make_async_remote_copy(
            src_ref=in_refs[w].at[2 * chip[0] + chip[1]], dst_ref=out_refs[w].at[j], send_sem=send_sems.at[w, j],
            recv_sem=recv_sems.at[w, j], device_id=(*chip, c), device_id_type=MESH)
            for w in range(nw) for j, chip in enumerate(chips)]

        def start():
            for cp in cps:
                cp.start()

        def finish():
            for cp in cps:
                cp.wait()

        return start, finish

    return _Comm(s1s, [jax.ShapeDtypeStruct((N_CHIPS - 1,) + s.shape[1:], s.dtype) for s in s1s],
                 [(nw, 3), (nw, 3)], build)


def _size(dims):
    n = 1
    for d in dims:
        n *= d
    return n


def _sem_grids(comm, sem_refs):
    grids, pos = [], 0
    for dims in comm.sem_dims:
        grids.append(_SemGrid(sem_refs[pos:pos + _size(dims)], dims))
        pos += _size(dims)
    return grids


def _comm_split_start(comm, name, after=()):
    c_in, c_out = len(comm.operands), len(comm.out_shape)
    counts = [_size(d) for d in comm.sem_dims]
    n_sem = sum(counts)
    assert not comm.aliases

    def body(*refs):
        srcs, lands = refs[:c_in], refs[c_in:c_in + c_out]
        first_sem = c_in + c_out + len(after)
        start, _ = comm.build(srcs, lands, _sem_grids(comm, refs[first_sem:first_sem + n_sem]))
        start()
        refs[-1][...] = jnp.zeros(refs[-1].shape, refs[-1].dtype)

    lands = [pltpu.with_memory_space_constraint(lax.empty(o.shape, o.dtype), HBM) for o in comm.out_shape]
    srcs = [pltpu.with_memory_space_constraint(a, HBM) for a in comm.operands]
    res = pl.pallas_call(
        body, name=name, in_specs=_hbm_specs(c_in + c_out) + [pl.BlockSpec(memory_space=pl.ANY)] * len(after),
        out_specs=[pl.BlockSpec(memory_space=pltpu.SEMAPHORE)] * n_sem + _hbm_specs(c_in + c_out)
        + [pl.BlockSpec(memory_space=pltpu.VMEM)],
        out_shape=[pltpu.SemaphoreType.DMA(())] * n_sem + [pltpu.HBM(a.shape, a.dtype) for a in comm.operands]
        + [pltpu.HBM(o.shape, o.dtype) for o in comm.out_shape] + [jax.ShapeDtypeStruct((SUBLANES, LANES), F32)],
        input_output_aliases={i: n_sem + i for i in range(c_in + c_out)},
        compiler_params=_params(has_side_effects=pltpu.SideEffectType.DATAFLOW_SIDE_EFFECTING))(*srcs, *lands, *after)
    return res[:-1], res[-1]


def _comm_split_wait(comm, state, after, name):
    c_in, c_out, n_sem = len(comm.operands), len(comm.out_shape), sum(_size(d) for d in comm.sem_dims)
    sems, srcs, lands = state[:n_sem], state[n_sem:n_sem + c_in], state[n_sem + c_in:]

    def body(*refs):
        src_refs, land_refs = refs[:c_in], refs[c_in:c_in + c_out]
        _, finish = comm.build(src_refs, land_refs, _sem_grids(comm, refs[c_in + c_out:c_in + c_out + n_sem]))
        finish()

    sem_spec = pl.BlockSpec(memory_space=pltpu.SEMAPHORE)
    res = pl.pallas_call(
        body, name=name, in_specs=_hbm_specs(c_in + c_out) + [sem_spec] * n_sem + [pl.BlockSpec(memory_space=pl.ANY)],
        out_specs=_hbm_specs(c_in + c_out),
        out_shape=[pltpu.HBM(a.shape, a.dtype) for a in srcs] + [pltpu.HBM(o.shape, o.dtype) for o in lands],
        input_output_aliases={i: i for i in range(c_in + c_out)},
        compiler_params=_params(has_side_effects=pltpu.SideEffectType.DATAFLOW_SIDE_EFFECTING),
    )(*srcs, *lands, *sems, after)
    return res[:c_in], res[c_in:]


def _share_comm(fs):
    nw = len(fs)

    def build(in_refs, out_refs, sems):
        del in_refs
        send_sems, recv_sems = sems
        x, y, c, _ = _place()

        def copy(w, half):
            rows = _half_rows(out_refs[w], half)
            return pltpu.make_async_remote_copy(
                src_ref=rows, dst_ref=rows, send_sem=send_sems.at[w], recv_sem=recv_sems.at[w],
                device_id=(x, y, 1 - c), device_id_type=MESH)

        sends = [copy(w, c) for w in range(nw)]

        def start():
            for cp in sends:
                cp.start()

        def finish():
            for w in range(nw):
                copy(w, 1 - c).wait_recv()
            for cp in sends:
                cp.wait_send()

        return start, finish

    return _Comm(fs, [jax.ShapeDtypeStruct(f.shape, f.dtype) for f in fs],
                 [(nw,), (nw,)], build,
                 aliases={w: w for w in range(nw)})


def _add_sibling(g, r1, place, name):
    nch, h, cols = r1.shape
    tr = _div_tile(h, 256, 2 * SUBLANES)
    nb = h // tr

    def body(place_ref, g_ref, r_ref, o_ref):
        del place_ref
        o_ref[...] = (g_ref[...].astype(F32) + r_ref[...].astype(F32)).astype(BF16)

    spec = pltpu.PrefetchScalarGridSpec(
        num_scalar_prefetch=1, grid=(nch, nb),
        in_specs=[pl.BlockSpec((None, tr, cols), lambda k, i, p: (k, p[0] * nb + i, 0)),
                  pl.BlockSpec((None, tr, cols), lambda k, i, p: (k, i, 0))],
        out_specs=pl.BlockSpec((None, tr, cols), lambda k, i, p: (k, i, 0)))
    return pl.pallas_call(body, name=name, grid_spec=spec, out_shape=jax.ShapeDtypeStruct((nch, h, cols), BF16),
                          compiler_params=_params())(place, g, r1)


def _add_chips(s1, r2, place, name):
    _, h, cols = s1.shape
    tr = _div_tile(h, 256, 2 * SUBLANES)
    nb = h // tr

    def body(place_ref, s_ref, r_ref, o_ref):
        del place_ref
        acc = s_ref[...].astype(F32)
        for j in range(N_CHIPS - 1):
            acc = acc + r_ref[j].astype(F32)
        o_ref[...] = acc

    spec = pltpu.PrefetchScalarGridSpec(
        num_scalar_prefetch=1, grid=(nb,),
        in_specs=[pl.BlockSpec((None, tr, cols), lambda i, p: (p[1], i, 0)),
                  pl.BlockSpec((N_CHIPS - 1, tr, cols), lambda i, p: (0, i, 0))],
        out_specs=pl.BlockSpec((tr, cols), lambda i, p: (p[0] * nb + i, 0)))
    return pl.pallas_call(body, name=name, grid_spec=spec, out_shape=jax.ShapeDtypeStruct((2 * h, cols), F32),
                          compiler_params=_params())(place, s1, r2)


def _quarter_turn(m):
    h = m.shape[-1] // 2
    return jnp.concatenate([-m[..., h:], m[..., :h]], axis=-1)


def _quarter_turn_back(m):
    h = m.shape[-1] // 2
    return jnp.concatenate([m[..., h:], -m[..., :h]], axis=-1)


def _join_cols(sh):
    return jnp.concatenate([sh[k] for k in range(N_CHIPS)], axis=1)


def _split_cols(full):
    c = full.shape[1] // N_CHIPS
    return jnp.stack([full[:, k * c:(k + 1) * c] for k in range(N_CHIPS)])


def kernel(x, c, positions, w_ada, b_ada, pre_norm1_g, w_in, gm_ln_g, gm_ln_b, gm_w_s, gm_b_s, w_branch_a, q_norm_g, w_uq, kv_norm_g, w_ukv, w_branch_b, w_out, post_norm1_g, pre_norm2_g, w_up, conv_w, conv_b, w_down, post_norm2_g, loss_target, m_w_ada, m_b_ada, m_pre_norm1_g, m_w_in, m_gm_ln_g, m_gm_ln_b, m_gm_w_s, m_gm_b_s, m_w_branch_a, m_q_norm_g, m_w_uq, m_kv_norm_g, m_w_ukv, m_w_branch_b, m_w_out, m_post_norm1_g, m_pre_norm2_g, m_w_up, m_conv_w, m_conv_b, m_w_down, m_post_norm2_g, v_w_ada, v_b_ada, v_pre_norm1_g, v_w_in, v_gm_ln_g, v_gm_ln_b, v_gm_w_s, v_gm_b_s, v_w_branch_a, v_q_norm_g, v_w_uq, v_kv_norm_g, v_w_ukv, v_w_branch_b, v_w_out, v_post_norm1_g, v_pre_norm2_g, v_w_up, v_conv_w, v_conv_b, v_w_down, v_post_norm2_g):
    given = dict(locals())
    s, d = x.shape[1], x.shape[2]
    gw = gm_ln_g.shape[0]
    ql, kvl = q_norm_g.shape[0], kv_norm_g.shape[0]
    heads = N_CHIPS * w_uq.shape[1] // (NOPE + ROPE)
    ff = N_CHIPS * w_down.shape[0]
    assert gw == d and N_CHIPS * w_ukv.shape[1] == heads * (NOPE + VHEAD)
    ix, iy, ic = lax.axis_index("x"), lax.axis_index("y"), lax.axis_index("c")
    chip = 2 * ix + iy
    dev = 2 * chip + ic
    row = lambda v: v.reshape(1, -1)

    c_all = _all_gather(jnp.pad(c, ((0, SUBLANES - 1), (0, 0))), "gather_c").reshape(N_DEV, SUBLANES, d)[:, 0]
    na = w_ada.shape[1]
    b_ada_mine = lax.dynamic_slice(b_ada, (chip * na,), (na,))
    mod_cols = _ada_fwd(c_all, w_ada, row(b_ada_mine), "ada_fwd")
    mod_all = _all_gather(mod_cols, "gather_mod").reshape(N_CHIPS, N_CORES, N_DEV, na)[:, 0]
    mod = lax.dynamic_index_in_dim(mod_all, dev, axis=1, keepdims=False).reshape(N_MOD, d)
    shift1, scale1, gate1, shift2, scale2, gate2 = (mod[i:i + 1] for i in range(N_MOD))

    mine = {n: given[n].astype(BF16) for n in BIG}
    gather = lambda names: _gather_comm([mine[n] for n in names])
    whole = lambda n, g: lax.dynamic_update_slice(g, mine[n][None], (chip, 0, 0))
    rows4 = lambda sh4: sh4.reshape(-1, sh4.shape[2])
    wi = _join_cols(whole("w_in", _run_comm(gather(["w_in"]), "gather_w_in")[0]))
    o_q, o_kv, o_pe, o_ga = 2 * gw, 2 * gw + ql, 2 * gw + ql + kvl, 2 * gw + ql + kvl + ROPE
    w_in_big = jnp.concatenate([wi[:, :o_q], wi[:, o_ga:]], axis=1)
    w_in_lat = jnp.concatenate([wi[:, o_q:o_ga], _quarter_turn(wi[:, o_pe:o_ga])], axis=1)

    inv = ROPE_THETA ** (-jnp.arange(0, ROPE, 2, dtype=F32) / ROPE)
    ang = positions[0].astype(F32)[:, None] * inv
    cos, sin = jnp.cos(ang), jnp.sin(ang)
    rope_k = jnp.concatenate([cos, cos, sin, sin], axis=1)
    softmax_scale = float(NOPE + ROPE) ** -0.5
    rope_q = jnp.concatenate([jnp.ones((s, NOPE), F32), rope_k], axis=1) * softmax_scale

    x2d, tgt = x[0], loss_target[0]
    g_pre1, g_post1, g_pre2, g_post2 = row(pre_norm1_g), row(post_norm1_g), row(pre_norm2_g), row(post_norm2_g)
    ln_g, ln_b, q_g, kv_g = row(gm_ln_g), row(gm_ln_b), row(q_norm_g), row(kv_norm_g)
    b_s_t = gm_b_s.T
    conv_wf = _all_gather(jnp.pad(conv_w, ((0, SUBLANES - CONV_TAPS), (0, 0))), "gather_conv_w")
    conv_wf = conv_wf.reshape(N_CHIPS, N_CORES, SUBLANES, conv_w.shape[1])[:, 0, :CONV_TAPS]
    conv_wf = conv_wf.transpose(1, 0, 2).reshape(CONV_TAPS, 2 * ff)
    conv_bf = row(conv_b)

    h1 = _prenorm(x2d, g_pre1, scale1, shift1, "prenorm1")
    z_big, (g_uq, g_ukv, g_a) = _matmul(h1, w_in_big, mode="nn", out_dtype=F32, name="mm_z_big", tm=s,
                                        comm=gather(["w_uq", "w_ukv", "w_branch_a"]))
    wq = _join_cols(whole("w_uq", g_uq)).reshape(ql, heads, NOPE + ROPE)
    w_q = jnp.concatenate([wq, _quarter_turn(wq[:, :, NOPE:])], axis=2).reshape(ql, heads * HEAD_W)
    w_kv = _join_cols(whole("w_ukv", g_ukv)).reshape(kvl, heads, 2, NOPE).transpose(0, 2, 1, 3)
    w_kv = w_kv.reshape(kvl, 2 * heads * NOPE)
    w_a = rows4(whole("w_branch_a", g_a))
    z_lat = _matmul(h1, w_in_lat, mode="nn", out_dtype=F32, name="mm_z_lat", tm=s, tn=1024)
    a_act = _gmlp_fwd(z_big, ln_g, ln_b, gm_w_s, b_s_t, "gmlp_fwd")
    qn, kvn, kr = _mla_prep(z_lat, q_g, kv_g, rope_k, "mla_prep")
    q_rot = _matmul(qn, w_q, mode="nn", out_dtype=BF16, name="mm_q", tm=s, tn=HEAD_W, mul=rope_q)
    kv_all = _matmul(kvn, w_kv, mode="nn", out_dtype=BF16, name="mm_kv", tm=s, tn=1024)
    (o_att, lse), (g_b, g_o, g_up) = _attn_fwd(q_rot, kv_all, kr, heads, "attn_fwd",
                                               comm=gather(["w_branch_b", "w_out", "w_up"]))
    w_b, w_o, w_upf = rows4(whole("w_branch_b", g_b)), rows4(whole("w_out", g_o)), whole("w_up", g_up)
    y_a = _matmul(a_act, w_a, mode="nn", out_dtype=F32, name="mm_y_a", tm=s)
    y_b = _matmul(o_att, w_b, mode="nn", out_dtype=F32, name="mm_y_b", tm=s)
    merged = _merge(z_big, y_a, y_b, "merge")
    y1 = _matmul(merged, w_o, mode="nn", out_dtype=F32, name="mm_y1", tm=s)
    x1, h2 = _post_pre(x2d, y1, gate1, g_post1, g_pre2, scale2, shift2, "post1_pre2")

    up_pre, (g_dn,) = _matmul(h2, w_upf, mode="nn", out_dtype=BF16, name="mm_up", tm=s, tn=1408,
                              comm=gather(["w_down"]))
    w_dn = rows4(whole("w_down", g_dn))
    act = _conv_fwd(up_pre, conv_wf, conv_bf, "conv_fwd")
    ffn = _matmul(act, w_dn, mode="nn", out_dtype=F32, name="mm_ffn", tm=s, tk=1408)

    dffn, dgate2, g_post2_grad, dx2, loss_part = _post_bwd(ffn, gate2, g_post2, "post2_bwd", xin=x1, target=tgt)
    loss = lax.psum(loss_part[0, 0], ("x", "y", "c"))
    place = jnp.stack([ic, chip]).astype(jnp.int32)
    rows_of = lambda g: g.reshape(N_CHIPS, g.shape[0] // N_CHIPS, g.shape[1])
    add_sibling = lambda names, gs, r1s: [_add_sibling(g, r1, place, "rs_add_sibling_" + n)
                                          for n, g, r1 in zip(names, gs, r1s)]
    add_chips = lambda names, s1s, r2s: [_add_chips(s1, r2, place, "rs_add_chips_" + n)
                                         for n, s1, r2 in zip(names, s1s, r2s)]
    dact = _matmul(dffn, w_dn, mode="nt", out_dtype=BF16, name="mm_dact", tm=s)
    gp_down = [rows_of(_matmul(act, dffn, mode="tn", out_dtype=BF16, name="mm_gw_down", tn=1024, tk=s))]
    (dup, gcw_g, gcw_v, gcb_g, gcb_v), r1_down = _conv_bwd(up_pre, dact, conv_wf, conv_bf, "conv_bwd",
                                                            comm=_swap_comm(gp_down))
    s1_down = add_sibling(["w_down"], gp_down, r1_down)
    dh2, r2_down = _matmul(dup, w_upf, mode="nt", out_dtype=F32, name="mm_dh2", tm=s, tk=1408,
                           comm=_exchange_comm(s1_down))
    half_down = add_chips(["w_down"], s1_down, r2_down)
    gw_up = _matmul(h2, dup, mode="tn", out_dtype=BF16, name="mm_gw_up", tn=1408, tk=s, out_groups=N_CHIPS)
    dx1, dshift2, dscale2, g_pre2_grad = _prenorm_bwd(x1, dh2, dx2, g_pre2, scale2, "prenorm2_bwd")

    dy1, dgate1, g_post1_grad = _post_bwd(y1, gate1, g_post1, "post1_bwd", dxo=dx1)
    dmerged = _matmul(dy1, w_o, mode="nt", out_dtype=F32, name="mm_dmerged", tm=s)
    gw_out = _matmul(merged, dy1, mode="tn", out_dtype=BF16, name="mm_gw_out", tn=1024, tk=s)
    dy_a, dy_b, dz_big = _merge_bwd(dmerged, z_big, y_a, y_b, "merge_bwd")
    da = _matmul(dy_a, w_a, mode="nt", out_dtype=F32, name="mm_da", tm=s)
    gw_a = _matmul(a_act, dy_a, mode="tn", out_dtype=BF16, name="mm_gw_a", tn=1024, tk=s)
    do = _matmul(dy_b, w_b, mode="nt", out_dtype=BF16, name="mm_do", tm=s)
    gw_b = _matmul(o_att, dy_b, mode="tn", out_dtype=BF16, name="mm_gw_b", tn=1024, tk=s)
    mid = ["w_up", "w_out", "w_branch_a", "w_branch_b"]
    gp_mid = [gw_up, rows_of(gw_out), rows_of(gw_a), rows_of(gw_b)]
    (dz_big, g_ws, g_bs_t, g_ln_g, g_ln_b), r1_mid = _gmlp_bwd(z_big, da, dz_big, ln_g, ln_b, gm_w_s, b_s_t,
                                                                "gmlp_bwd", comm=_swap_comm(gp_mid))
    s1_mid = add_sibling(mid, gp_mid, r1_mid)
    (dq, dk, dv), r2_up_out = _attn_bwd(q_rot, kv_all, kr, o_att, do, lse, heads, "attn_bwd",
                                        comm=_exchange_comm(s1_mid[:2]))
    dq_big, dkv, dkk = _mla_bwd_mid(dq, dk, dv, rope_q, rope_k, heads, "mla_bwd_mid")
    gw_q = _matmul(qn, dq_big, mode="tn", out_dtype=F32, name="mm_gw_q", tn=1024, tk=s)
    dqn = _matmul(dq_big, w_q, mode="nt", out_dtype=F32, name="mm_dqn", tm=s, tk=1024)
    gw_kv = _matmul(kvn, dkv, mode="tn", out_dtype=BF16, name="mm_gw_kv", tn=1024, tk=s)
    dkvn = _matmul(dkv, w_kv, mode="nt", out_dtype=F32, name="mm_dkvn", tm=s, tk=1024)
    dz_lat, g_q, g_kv = _mla_bwd_post(z_lat, dqn, dkvn, dkk, q_g, kv_g, "mla_bwd_post")

    partial = {
        "gm_ln_g": g_ln_g, "gm_ln_b": g_ln_b, "gm_w_s": g_ws, "gm_b_s": g_bs_t[:, :gm_b_s.shape[0]].T,
        "q_norm_g": g_q, "kv_norm_g": g_kv, "post_norm1_g": g_post1_grad, "pre_norm2_g": g_pre2_grad,
        "conv_w": jnp.concatenate([gcw_g, gcw_v], axis=1), "conv_b": jnp.concatenate([gcb_g, gcb_v], axis=1),
        "post_norm2_g": g_post2_grad,
    }
    flat = jnp.concatenate([partial[n].reshape(-1) for n in SMALL_PARTIAL])
    n_small = flat.shape[0]
    rows_small = -(-n_small // (LANES * SUBLANES)) * SUBLANES
    flat = jnp.pad(flat, (0, rows_small * LANES - n_small)).reshape(rows_small, LANES)
    small_sum = _sum_leading(_all_gather(flat, "gather_small").reshape(N_DEV, rows_small, LANES), "sum_small")
    small_sum = small_sum.reshape(-1)
    small_grads, off = {}, 0
    for n in SMALL_PARTIAL:
        shape = (CONV_TAPS, 2 * ff) if n == "conv_w" else given[n].shape
        small_grads[n] = small_sum[off:off + partial[n].size].reshape(shape)
        off += partial[n].size
    small_grads["conv_w"] = lax.dynamic_slice(small_grads["conv_w"], (0, chip * conv_w.shape[1]), conv_w.shape)

    dh1, r2_a_b = _matmul(dz_big, w_in_big, mode="nt", out_dtype=F32, name="mm_dh1_big", tm=s,
                          comm=_exchange_comm(s1_mid[2:]))
    half_mid = add_chips(mid, s1_mid, list(r2_up_out) + list(r2_a_b))
    dh1 = _matmul(dz_lat, w_in_lat, mode="nt", out_dtype=F32, name="mm_dh1_lat", tm=s, tk=1024, add=dh1)
    gw_in_big, shared = _matmul(h1, dz_big, mode="tn", out_dtype=BF16, name="mm_gw_in_big", tn=1024, tk=s,
                                comm=_share_comm(half_down + half_mid))
    grads = dict(zip(["w_down"] + mid, shared), **small_grads)
    gw_in_lat = _matmul(h1, dz_lat, mode="tn", out_dtype=F32, name="mm_gw_in_lat", tn=1024, tk=s)

    gq = gw_q.reshape(ql, heads, HEAD_W)
    gq_pe = gq[:, :, NOPE:NOPE + ROPE] + _quarter_turn_back(gq[:, :, NOPE + ROPE:])
    g_pe = gw_in_lat[:, ql + kvl:ql + kvl + ROPE] + _quarter_turn_back(gw_in_lat[:, ql + kvl + ROPE:])
    last = ["w_in", "w_uq", "w_ukv"]
    gp_last = [
        _split_cols(jnp.concatenate([gw_in_big[:, :o_q], gw_in_lat[:, :ql + kvl].astype(BF16), g_pe.astype(BF16),
                                     gw_in_big[:, o_q:]], axis=1)),
        _split_cols(jnp.concatenate([gq[:, :, :NOPE], gq_pe], axis=2).reshape(ql, heads * (NOPE + ROPE)).astype(BF16)),
        _split_cols(gw_kv.reshape(kvl, 2, heads, NOPE).transpose(0, 2, 1, 3).reshape(kvl, heads * 2 * NOPE)),
    ]
    (grad_x, dshift1, dscale1, g_pre1_grad), r1_last = _prenorm_bwd(x2d, dh1, dx1, g_pre1, scale1, "prenorm1_bwd",
                                                                    comm=_swap_comm(gp_last))
    s1_last = add_sibling(last, gp_last, r1_last)

    dmod = jnp.concatenate([dshift1, dscale1, dgate1, dshift2, dscale2, dgate2, g_pre1_grad], axis=1)
    dmod_all = _all_gather(jnp.pad(dmod, ((0, SUBLANES - 1), (0, 0))), "gather_dmod")
    dmod_all = dmod_all.reshape(N_DEV, SUBLANES, (N_MOD + 1) * d)[:, 0]
    dmod_sum = _sum_leading(dmod_all.reshape(N_DEV, 1, (N_MOD + 1) * d), "sum_dmod")[0]
    grads["b_ada"], grads["pre_norm1_g"] = dmod_sum[:N_MOD * d], dmod_sum[N_MOD * d:]
    dmod_mine = lax.dynamic_slice(dmod_all, (0, chip * na), (N_DEV, na))
    grads["w_ada"] = _ada_bwd(c_all.T, dmod_mine, "ada_bwd")

    delta, new_m, new_v = {}, {}, {}

    def adamw(n, after=None):
        turn = (lambda a: a.T) if n == "w_in" else (lambda a: a)
        g_t = turn(grads[n])
        outs = _adamw(turn(given[n]), g_t, turn(given["m_" + n]), turn(given["v_" + n]), "adamw_" + n, after=after)
        grads[n] = turn(g_t)
        delta[n], new_m[n], new_v[n] = (turn(o) for o in outs)

    exchange_last = _exchange_comm(s1_last)
    in_flight, token = _comm_split_start(exchange_last, "rs_exchange_last_start", after=[dmod_sum, small_sum])
    for n in ["w_ada", "w_down"] + mid:
        adamw(n, after=token)
    s1_last, r2_last = _comm_split_wait(exchange_last, in_flight, delta[mid[-1]], "rs_exchange_last_wait")
    half_last = add_chips(last, s1_last, r2_last)
    grads.update(zip(last, _run_comm(_share_comm(half_last), "rs_share_last")))
    for n in last:
        adamw(n)

    def small_pack(prefix, source):
        v = jnp.concatenate([source[prefix + n].reshape(-1) for n in SMALL])
        rows = -(-v.shape[0] // (LANES * SUBLANES)) * SUBLANES
        return jnp.pad(v, (0, rows * LANES - v.shape[0])).reshape(rows, LANES)

    outs = _adamw(small_pack("", given), small_pack("", grads), small_pack("m_", given), small_pack("v_", given),
                  "adamw_small")
    off = 0
    for n in SMALL:
        size = given[n].size
        for store, packed_out in zip((delta, new_m, new_v), outs):
            store[n] = packed_out.reshape(-1)[off:off + size].reshape(given[n].shape)
        off += size

    return (loss, grad_x[None], *[grads[n] for n in WEIGHTS], *[delta[n] for n in WEIGHTS],
            *[new_m[n] for n in WEIGHTS], *[new_v[n] for n in WEIGHTS])
```

```python
import functools

import jax
import jax.numpy as jnp
from jax import lax
from jax.experimental import pallas as pl
from jax.experimental.pallas import tpu as pltpu

F32 = jnp.float32
BF16 = jnp.bfloat16
MESH = pl.DeviceIdType.MESH
HBM = pltpu.HBM

EPS = 1e-6
NOPE, ROPE, VHEAD = 128, 64, 128
HEAD_W = NOPE + 2 * ROPE
ROPE_THETA = 10000.0
CONV_TAPS = 3
N_MOD = 6
N_CHIPS, N_CORES, N_DEV = 4, 2, 8
ADAM_LR, ADAM_B1, ADAM_B2, ADAM_EPS, ADAM_WD, ADAM_STEP = 0.001, 0.9, 0.999, 1e-08, 0.01, 10

LANES = 128
SUBLANES = 8
VMEM_LIMIT = 56 * 2**20
MIDDLE_STAGE_AT = 70
SMALL_ROW_TILE = 256

BIG = ("w_in", "w_branch_a", "w_uq", "w_ukv", "w_branch_b", "w_out", "w_up", "w_down")
WEIGHTS = ("w_ada", "b_ada", "pre_norm1_g", "w_in", "gm_ln_g", "gm_ln_b", "gm_w_s", "gm_b_s", "w_branch_a",
           "q_norm_g", "w_uq", "kv_norm_g", "w_ukv", "w_branch_b", "w_out", "post_norm1_g", "pre_norm2_g",
           "w_up", "conv_w", "conv_b", "w_down", "post_norm2_g")
SMALL_PARTIAL = ("gm_ln_g", "gm_ln_b", "gm_w_s", "gm_b_s", "q_norm_g", "kv_norm_g", "post_norm1_g",
                 "pre_norm2_g", "conv_w", "conv_b", "post_norm2_g")
SMALL = ("b_ada", "pre_norm1_g") + SMALL_PARTIAL


def _div_tile(n, cap, mult=LANES):
    t = (min(cap, n) // mult) * mult
    while t >= mult:
        if n % t == 0:
            return t
        t -= mult
    return n


def _params(**kw):
    return pltpu.CompilerParams(vmem_limit_bytes=VMEM_LIMIT, **kw)


def _row_spec(width):
    return pl.BlockSpec((1, width), lambda *_: (0, 0))


def _gelu(x):
    k = 0.7978845608028654
    return 0.5 * x * (1.0 + jnp.tanh(k * (x + 0.044715 * x * x * x)))


def _gelu_grad(x):
    k = 0.7978845608028654
    t = jnp.tanh(k * (x + 0.044715 * x * x * x))
    return 0.5 * (1.0 + t) + 0.5 * x * (1.0 - t * t) * k * (1.0 + 3.0 * 0.044715 * x * x)


def _sigmoid(x):
    return 0.5 * jnp.tanh(0.5 * x) + 0.5


def _dot(a, b, dims):
    return lax.dot_general(a, b, (dims, ((), ())), preferred_element_type=F32)


NN = ((1,), (0,))
NT = ((1,), (1,))
TN = ((0,), (0,))


def _logical(arr):
    if arr.ndim == 2:
        return arr.shape[0], arr.shape[1], arr.shape[1]
    return arr.shape[1], arr.shape[0] * arr.shape[2], arr.shape[2]


def _tile_spec(ndim, group_w, blk_rows, blk_cols, row_of, col_of):
    if ndim == 2:
        return pl.BlockSpec((blk_rows, blk_cols), lambda i, j, k: (row_of(i, j, k), col_of(i, j, k)))
    per = group_w // blk_cols
    return pl.BlockSpec((None, blk_rows, blk_cols),
                        lambda i, j, k: (col_of(i, j, k) // per, row_of(i, j, k), col_of(i, j, k) % per))


def _matmul(a, b, *, mode, out_dtype, name, tm=512, tn=512, tk=2048, mul=None, add=None, out_groups=None, comm=None):
    ar, ac, agw = _logical(a)
    br, bc, bgw = _logical(b)
    if mode == "nn":
        m, kd, n = ar, ac, bc
        m_w, k_w, n_w = (), (agw,), (bgw,)
    elif mode == "nt":
        m, kd, n = ar, ac, br
        m_w, k_w, n_w = (), (agw, bgw), ()
    else:
        m, kd, n = ac, ar, bc
        m_w, k_w, n_w = (agw,), (), (bgw,)
    if out_groups is not None:
        n_w = n_w + (n // out_groups,)
    tm = _div_tile(min((m,) + m_w), tm, SUBLANES)
    tn = _div_tile(min((n,) + n_w), tn)
    tk = _div_tile(min((kd,) + k_w), tk)
    assert all(w % tn == 0 for w in n_w) and all(w % tk == 0 for w in k_w) and all(w % tm == 0 for w in m_w)
    nk = kd // tk
    dims = {"nn": NN, "nt": NT, "tn": TN}[mode]
    gi, gj, gk = (lambda i, j, k: i), (lambda i, j, k: j), (lambda i, j, k: k)
    if mode == "nn":
        a_spec = _tile_spec(a.ndim, agw, tm, tk, gi, gk)
        b_spec = _tile_spec(b.ndim, bgw, tk, tn, gk, gj)
    elif mode == "nt":
        a_spec = _tile_spec(a.ndim, agw, tm, tk, gi, gk)
        b_spec = _tile_spec(b.ndim, bgw, tn, tk, gj, gk)
    else:
        a_spec = _tile_spec(a.ndim, agw, tk, tm, gk, gi)
        b_spec = _tile_spec(b.ndim, bgw, tk, tn, gk, gj)
    in_specs, operands = [a_spec, b_spec], [a, b]
    if mul is not None:
        assert mul.shape == (m, tn)
        in_specs.append(pl.BlockSpec((tm, tn), lambda i, j, k: (i, 0)))
        operands.append(mul)
    if add is not None:
        in_specs.append(pl.BlockSpec((tm, tn), lambda i, j, k: (i, j)))
        operands.append(add)

    def body(*refs):
        a_ref, b_ref = refs[0], refs[1]
        pos = 2
        mul_ref = add_ref = None
        if mul is not None:
            mul_ref, pos = refs[pos], pos + 1
        if add is not None:
            add_ref, pos = refs[pos], pos + 1
        o_ref = refs[pos]

        def finish(r):
            if mul_ref is not None:
                r = r * mul_ref[...]
            if add_ref is not None:
                r = r + add_ref[...]
            o_ref[...] = r.astype(out_dtype)

        part = _dot(a_ref[...], b_ref[...], dims)
        if nk == 1:
            finish(part)
        else:
            acc_ref = refs[pos + 1]
            k = pl.program_id(2)

            @pl.when(k == 0)
            def _():
                acc_ref[...] = part

            @pl.when(k > 0)
            def _():
                acc_ref[...] += part

            @pl.when(k == nk - 1)
            def _():
                finish(acc_ref[...])

    if out_groups is None:
        out_spec, out_dims = _tile_spec(2, n, tm, tn, gi, gj), (m, n)
    else:
        out_spec, out_dims = _tile_spec(3, n // out_groups, tm, tn, gi, gj), (out_groups, m, n // out_groups)
    return _call(body, operands, comm, name=name, grid=(m // tm, n // tn, nk), in_specs=in_specs, out_specs=out_spec,
                 out_shape=jax.ShapeDtypeStruct(out_dims, out_dtype),
                 scratch_shapes=[] if nk == 1 else [pltpu.VMEM((tm, tn), F32)])


def _accumulate(ref, value):
    @pl.when(pl.program_id(0) == 0)
    def _():
        ref[...] = value

    @pl.when(pl.program_id(0) > 0)
    def _():
        ref[...] += value


def _colsum(v):
    return jnp.sum(v, axis=0, keepdims=True)


def _rowmean(v):
    return jnp.mean(v, axis=-1, keepdims=True)


def _prenorm(x, g, scale, shift, name):
    s, d = x.shape
    tb = _div_tile(s, 256, SUBLANES)

    def body(x_ref, g_ref, sc_ref, sh_ref, h_ref):
        xv = x_ref[...]
        r = lax.rsqrt(_rowmean(xv * xv) + EPS)
        h_ref[...] = ((xv * r) * g_ref[...] * (1.0 + sc_ref[...]) + sh_ref[...]).astype(BF16)

    blk = pl.BlockSpec((tb, d), lambda i: (i, 0))
    return pl.pallas_call(
        body, name=name, grid=(s // tb,), in_specs=[blk, _row_spec(d), _row_spec(d), _row_spec(d)],
        out_specs=blk, out_shape=jax.ShapeDtypeStruct((s, d), BF16), compiler_params=_params(),
    )(x, g, scale, shift)


def _post_pre(x, y, gate, pg, g2, scale2, shift2, name):
    s, d = x.shape
    tb = _div_tile(s, 256, SUBLANES)

    def body(x_ref, y_ref, gate_ref, pg_ref, g2_ref, sc_ref, sh_ref, x1_ref, h2_ref):
        yv = y_ref[...]
        rp = lax.rsqrt(_rowmean(yv * yv) + EPS)
        x1 = x_ref[...] + gate_ref[...] * ((yv * rp) * pg_ref[...])
        x1_ref[...] = x1
        r2 = lax.rsqrt(_rowmean(x1 * x1) + EPS)
        h2_ref[...] = ((x1 * r2) * g2_ref[...] * (1.0 + sc_ref[...]) + sh_ref[...]).astype(BF16)

    blk = pl.BlockSpec((tb, d), lambda i: (i, 0))
    return pl.pallas_call(
        body, name=name, grid=(s // tb,), in_specs=[blk, blk] + [_row_spec(d)] * 5,
        out_specs=[blk, blk],
        out_shape=[jax.ShapeDtypeStruct((s, d), F32), jax.ShapeDtypeStruct((s, d), BF16)],
        compiler_params=_params(),
    )(x, y, gate, pg, g2, scale2, shift2)


def _post_bwd(y, gate, pg, name, *, dxo=None, xin=None, target=None):
    s, d = y.shape
    tb = _div_tile(s, 256, SUBLANES)
    from_loss = target is not None

    def body(*refs):
        if from_loss:
            y_ref, gate_ref, pg_ref, xin_ref, t_ref, dy_ref, dgate_ref, dpg_ref, dxo_ref, loss_ref = refs
        else:
            y_ref, gate_ref, pg_ref, dxo_in_ref, dy_ref, dgate_ref, dpg_ref = refs
        yv = y_ref[...]
        rp = lax.rsqrt(_rowmean(yv * yv) + EPS)
        yh = yv * rp
        fn = yh * pg_ref[...]
        gate = gate_ref[...]
        if from_loss:
            err = xin_ref[...] + gate * fn - t_ref[...]
            dxo = err * (1.0 / d)
            dxo_ref[...] = dxo
            part = 0.5 * jnp.sum(_rowmean(err * err), axis=0, keepdims=True)
            _accumulate(loss_ref, jnp.broadcast_to(part, loss_ref.shape))
        else:
            dxo = dxo_in_ref[...]
        _accumulate(dgate_ref, _colsum(dxo * fn))
        dfn = dxo * gate
        _accumulate(dpg_ref, _colsum(dfn * yh))
        dyh = dfn * pg_ref[...]
        dy_ref[...] = (rp * (dyh - yh * _rowmean(dyh * yh))).astype(BF16)

    blk = pl.BlockSpec((tb, d), lambda i: (i, 0))
    in_specs = [blk, _row_spec(d), _row_spec(d)]
    out_specs = [blk, _row_spec(d), _row_spec(d)]
    out_shape = [jax.ShapeDtypeStruct((s, d), BF16), jax.ShapeDtypeStruct((1, d), F32),
                 jax.ShapeDtypeStruct((1, d), F32)]
    if from_loss:
        operands = (y, gate, pg, xin, target)
        in_specs += [blk, blk]
        out_specs += [blk, _row_spec(LANES)]
        out_shape += [jax.ShapeDtypeStruct((s, d), F32), jax.ShapeDtypeStruct((1, LANES), F32)]
    else:
        operands = (y, gate, pg, dxo)
        in_specs += [blk]
    return pl.pallas_call(
        body, name=name, grid=(s // tb,), in_specs=in_specs, out_specs=out_specs, out_shape=out_shape,
        compiler_params=_params(),
    )(*operands)


def _prenorm_bwd(xin, dh, dres, g, scale, name, comm=None):
    s, d = xin.shape
    tb = _div_tile(s, 256, SUBLANES)

    def body(x_ref, dh_ref, dres_ref, g_ref, sc_ref, dx_ref, dshift_ref, dscale_ref, dg_ref):
        xv = x_ref[...]
        r = lax.rsqrt(_rowmean(xv * xv) + EPS)
        xn = xv * r
        dh = dh_ref[...]
        g1 = g_ref[...]
        s1 = 1.0 + sc_ref[...]
        _accumulate(dshift_ref, _colsum(dh))
        _accumulate(dscale_ref, _colsum(dh * xn * g1))
        _accumulate(dg_ref, _colsum(dh * xn * s1))
        dxn = dh * g1 * s1
        dx_ref[...] = dres_ref[...] + r * (dxn - xn * _rowmean(dxn * xn))

    blk = pl.BlockSpec((tb, d), lambda i: (i, 0))
    return _call(
        body, (xin, dh, dres, g, scale), comm, name=name, grid=(s // tb,),
        in_specs=[blk, blk, blk, _row_spec(d), _row_spec(d)],
        out_specs=[blk, _row_spec(d), _row_spec(d), _row_spec(d)],
        out_shape=[jax.ShapeDtypeStruct((s, d), F32)] + [jax.ShapeDtypeStruct((1, d), F32)] * 3)


def _merge(z_big, y_a, y_b, name):
    s, d = y_a.shape
    tb = _div_tile(s, 256, SUBLANES)

    def body(zg_ref, ya_ref, yb_ref, o_ref):
        o_ref[...] = (_sigmoid(zg_ref[:, :d]) * ya_ref[...] + _sigmoid(zg_ref[:, d:]) * yb_ref[...]).astype(BF16)

    blk = pl.BlockSpec((tb, d), lambda i: (i, 0))
    return pl.pallas_call(
        body, name=name, grid=(s // tb,), in_specs=[pl.BlockSpec((tb, 2 * d), lambda i: (i, 1)), blk, blk],
        out_specs=blk, out_shape=jax.ShapeDtypeStruct((s, d), BF16), compiler_params=_params(),
    )(z_big, y_a, y_b)


def _merge_bwd(dmerged, z_big, y_a, y_b, name):
    s, d = y_a.shape
    tb = _div_tile(s, 256, SUBLANES)

    def body(dm_ref, zg_ref, ya_ref, yb_ref, dya_ref, dyb_ref, dz_ref):
        dm = dm_ref[...]
        sa, sb = _sigmoid(zg_ref[:, :d]), _sigmoid(zg_ref[:, d:])
        dya_ref[...] = (dm * sa).astype(BF16)
        dyb_ref[...] = (dm * sb).astype(BF16)
        dz_ref[:, :d] = (dm * ya_ref[...] * sa * (1.0 - sa)).astype(BF16)
        dz_ref[:, d:] = (dm * yb_ref[...] * sb * (1.0 - sb)).astype(BF16)

    blk = pl.BlockSpec((tb, d), lambda i: (i, 0))
    wide = pl.BlockSpec((tb, 2 * d), lambda i: (i, 1))
    return pl.pallas_call(
        body, name=name, grid=(s // tb,), in_specs=[blk, wide, blk, blk], out_specs=[blk, blk, wide],
        out_shape=[jax.ShapeDtypeStruct((s, d), BF16), jax.ShapeDtypeStruct((s, d), BF16),
                   jax.ShapeDtypeStruct((s, 4 * d), BF16)],
        compiler_params=_params(),
    )(dmerged, z_big, y_a, y_b)


def _causal_mask(ch):
    q = lax.broadcasted_iota(jnp.int32, (ch, ch), 0)
    p = lax.broadcasted_iota(jnp.int32, (ch, ch), 1)
    return (p <= q).astype(F32)


def _gmlp_norm(zc, lng, lnb, gw):
    u_pre, v_pre = zc[:, :gw], zc[:, gw:]
    vg = _gelu(v_pre)
    mu = _rowmean(vg)
    cen = vg - mu
    rstd = lax.rsqrt(_rowmean(cen * cen) + EPS)
    vhat = cen * rstd
    return u_pre, v_pre, _gelu(u_pre), vhat, rstd, vhat * lng + lnb


def _gmlp_fwd(z_big, ln_g, ln_b, w_s, b_s_t, name):
    s = z_big.shape[0]
    groups, ch, _ = w_s.shape
    gw = ln_g.shape[1]
    gd = gw // groups

    def body(z_ref, lng_ref, lnb_ref, ws_ref, bt_ref, a_ref):
        _, _, u, _, _, vn = _gmlp_norm(z_ref[...], lng_ref[...], lnb_ref[...], gw)
        mask = _causal_mask(ch)
        for g in range(groups):
            cols = slice(g * gd, (g + 1) * gd)
            wm = (ws_ref[g] * mask).astype(BF16)
            mixed = _dot(wm, vn[:, cols].astype(BF16), NN) + bt_ref[:, g:g + 1]
            a_ref[:, cols] = (u[:, cols] * mixed).astype(BF16)

    return pl.pallas_call(
        body, name=name, grid=(s // ch,),
        in_specs=[pl.BlockSpec((ch, 2 * gw), lambda n: (n, 0)), _row_spec(gw), _row_spec(gw),
                  pl.BlockSpec((groups, ch, ch), lambda n: (0, 0, 0)), pl.BlockSpec((ch, groups), lambda n: (0, 0))],
        out_specs=pl.BlockSpec((ch, gw), lambda n: (n, 0)),
        out_shape=jax.ShapeDtypeStruct((s, gw), BF16), compiler_params=_params(),
    )(z_big, ln_g, ln_b, w_s, b_s_t)


def _gmlp_bwd(z_big, da, dz_big, ln_g, ln_b, w_s, b_s_t, name, comm=None):
    s = z_big.shape[0]
    groups, ch, _ = w_s.shape
    gw = ln_g.shape[1]
    gd = gw // groups

    def body(z_ref, da_ref, dzin_ref, lng_ref, lnb_ref, ws_ref, bt_ref, dz_ref, gws_ref, gbt_ref, glng_ref, glnb_ref):
        del dzin_ref
        lng = lng_ref[...]
        u_pre, v_pre, u, vhat, rstd, vn = _gmlp_norm(z_ref[...], lng, lnb_ref[...], gw)
        da = da_ref[...]
        mask = _causal_mask(ch)
        first = pl.program_id(0) == 0
        dvn_parts = []
        lane = lax.broadcasted_iota(jnp.int32, (ch, LANES), 1)
        gb = jnp.zeros((ch, LANES), F32)
        for g in range(groups):
            cols = slice(g * gd, (g + 1) * gd)
            wm = (ws_ref[g] * mask).astype(BF16)
            vn_g = vn[:, cols].astype(BF16)
            mixed = _dot(wm, vn_g, NN) + bt_ref[:, g:g + 1]
            dz_ref[:, cols] = (da[:, cols] * mixed * _gelu_grad(u_pre[:, cols])).astype(BF16)
            dmixed = da[:, cols] * u[:, cols]
            dm16 = dmixed.astype(BF16)
            dvn_parts.append(_dot(wm, dm16, TN))
            gws = _dot(dm16, vn_g, NT) * mask

            @pl.when(first)
            def _(g=g, gws=gws):
                gws_ref[g] = gws

            @pl.when(jnp.logical_not(first))
            def _(g=g, gws=gws):
                gws_ref[g] += gws

            gb = gb + jnp.where(lane == g, jnp.sum(dmixed, axis=1, keepdims=True), 0.0)
        _accumulate(gbt_ref, gb)
        dvn = jnp.concatenate(dvn_parts, axis=1)
        _accumulate(glnb_ref, _colsum(dvn))
        _accumulate(glng_ref, _colsum(dvn * vhat))
        dvh = dvn * lng
        dvg = rstd * (dvh - _rowmean(dvh) - vhat * _rowmean(dvh * vhat))
        dz_ref[:, gw:] = (dvg * _gelu_grad(v_pre)).astype(BF16)

    zspec = pl.BlockSpec((ch, 2 * gw), lambda n: (n, 0))
    return _call(
        body, (z_big, da, dz_big, ln_g, ln_b, w_s, b_s_t), comm, name=name, grid=(s // ch,),
        in_specs=[zspec, pl.BlockSpec((ch, gw), lambda n: (n, 0)), pl.BlockSpec(memory_space=HBM),
                  _row_spec(gw), _row_spec(gw), pl.BlockSpec((groups, ch, ch), lambda n: (0, 0, 0)),
                  pl.BlockSpec((ch, groups), lambda n: (0, 0))],
        out_specs=[zspec, pl.BlockSpec((groups, ch, ch), lambda n: (0, 0, 0)),
                   pl.BlockSpec((ch, LANES), lambda n: (0, 0)), _row_spec(gw), _row_spec(gw)],
        out_shape=[jax.ShapeDtypeStruct(dz_big.shape, BF16), jax.ShapeDtypeStruct((groups, ch, ch), F32),
                   jax.ShapeDtypeStruct((ch, LANES), F32), jax.ShapeDtypeStruct((1, gw), F32),
                   jax.ShapeDtypeStruct((1, gw), F32)],
        input_output_aliases={2: 0})


def _mla_prep(z_lat, q_g, kv_g, rope_k, name):
    s, latw = z_lat.shape
    ql, kvl = q_g.shape[1], kv_g.shape[1]
    tb = _div_tile(s, 256, SUBLANES)

    def body(z_ref, qg_ref, kvg_ref, t_ref, qn_ref, kvn_ref, kr_ref):
        q = z_ref[:, :ql]
        qn_ref[...] = ((q * lax.rsqrt(_rowmean(q * q) + EPS)) * qg_ref[...]).astype(BF16)
        kv = z_ref[:, ql:ql + kvl]
        kvn_ref[...] = ((kv * lax.rsqrt(_rowmean(kv * kv) + EPS)) * kvg_ref[...]).astype(BF16)
        kk = z_ref[:, ql + kvl:] * t_ref[...]
        kr_ref[...] = (kk + pltpu.roll(kk, ROPE, axis=1)).astype(BF16)

    return pl.pallas_call(
        body, name=name, grid=(s // tb,),
        in_specs=[pl.BlockSpec((tb, latw), lambda i: (i, 0)), _row_spec(ql), _row_spec(kvl),
                  pl.BlockSpec((tb, 2 * ROPE), lambda i: (i, 0))],
        out_specs=[pl.BlockSpec((tb, ql), lambda i: (i, 0)), pl.BlockSpec((tb, kvl), lambda i: (i, 0)),
                   pl.BlockSpec((tb, 2 * ROPE), lambda i: (i, 0))],
        out_shape=[jax.ShapeDtypeStruct((s, ql), BF16), jax.ShapeDtypeStruct((s, kvl), BF16),
                   jax.ShapeDtypeStruct((s, 2 * ROPE), BF16)],
        compiler_params=_params(),
    )(z_lat, q_g, kv_g, rope_k)


def _scores(q, k, kr, on_diagonal):
    s = _dot(q[:, :NOPE], k, NT) + _dot(q[:, NOPE:], kr, NT)
    if not on_diagonal:
        return s
    rows = lax.broadcasted_iota(jnp.int32, s.shape, 0)
    cols = lax.broadcasted_iota(jnp.int32, s.shape, 1)
    return jnp.where(cols <= rows, s, -1e30)


def _attn_fwd(q, kv, kr, heads, name, comm=None):
    s = q.shape[0]
    t = _div_tile(s, 512)
    nb = s // t
    hp = 2 if heads % 2 == 0 else 1

    def body(q_ref, k_ref, kr_ref, v_ref, o_ref, lse_ref, m_ref, l_ref, acc_ref):
        i, j = pl.program_id(1), pl.program_id(2)

        @pl.when(j == 0)
        def _():
            m_ref[...] = jnp.full(m_ref.shape, -1e30, F32)
            l_ref[...] = jnp.zeros(l_ref.shape, F32)
            acc_ref[...] = jnp.zeros(acc_ref.shape, F32)

        def step(on_diagonal):
            krv = kr_ref[...]
            for h in range(hp):
                vc = slice(h * VHEAD, (h + 1) * VHEAD)
                sc = _scores(q_ref[:, h * HEAD_W:(h + 1) * HEAD_W], k_ref[:, h * NOPE:(h + 1) * NOPE], krv, on_diagonal)
                m_old = m_ref[h]
                m_new = jnp.maximum(m_old, jnp.max(sc, axis=-1, keepdims=True))
                p = jnp.exp(sc - m_new)
                alpha = jnp.exp(m_old - m_new)
                l_new = alpha * l_ref[h] + jnp.sum(p, axis=-1, keepdims=True)
                acc = alpha * acc_ref[:, vc] + _dot(p.astype(BF16), v_ref[:, vc], NN)
                if on_diagonal:
                    o_ref[:, vc] = (acc / l_new).astype(BF16)
                    lse_ref[h] = jnp.broadcast_to(m_new + jnp.log(l_new), (t, LANES))
                else:
                    m_ref[h], l_ref[h], acc_ref[:, vc] = m_new, l_new, acc

        pl.when(j < i)(lambda: step(False))
        pl.when(j == i)(lambda: step(True))

    kidx = lambda off: (lambda h, i, j: (jnp.minimum(i, j), off(h)))
    return _call(
        body, (q, kv, kr, kv), comm, name=name, grid=(heads // hp, nb, nb),
        in_specs=[pl.BlockSpec((t, hp * HEAD_W), lambda h, i, j: (i, h)),
                  pl.BlockSpec((t, hp * NOPE), kidx(lambda h: h)),
                  pl.BlockSpec((t, 2 * ROPE), kidx(lambda h: 0)),
                  pl.BlockSpec((t, hp * VHEAD), kidx(lambda h: heads // hp + h))],
        out_specs=[pl.BlockSpec((t, hp * VHEAD), lambda h, i, j: (i, h)),
                   pl.BlockSpec((hp, t, LANES), lambda h, i, j: (h, i, 0))],
        out_shape=[jax.ShapeDtypeStruct((s, heads * VHEAD), BF16), jax.ShapeDtypeStruct((heads, s, LANES), F32)],
        scratch_shapes=[pltpu.VMEM((hp, t, 1), F32), pltpu.VMEM((hp, t, 1), F32), pltpu.VMEM((t, hp * VHEAD), F32)])


def _attn_bwd(q, kv, kr, o, do, lse, heads, name, comm=None):
    s = q.shape[0]
    t = _div_tile(s, 512)
    nb = s // t
    hp = 2 if heads % 2 == 0 else 1

    def body(q_ref, k_ref, kr_ref, v_ref, o_ref, do_ref, lse_ref, dq_ref, dk_ref, dv_ref, dk_acc, dv_acc):
        j, i = pl.program_id(1), pl.program_id(2)

        @pl.when(jnp.logical_and(j == 0, i == 0))
        def _():
            dq_ref[...] = jnp.zeros(dq_ref.shape, F32)

        def step(on_diagonal):
            krv = kr_ref[...]
            rows = pl.ds(pl.multiple_of(i * t, t), t)
            for h in range(hp):
                qc, kc, vc = (slice(h * w, (h + 1) * w) for w in (HEAD_W, NOPE, VHEAD))
                qv, kn, do_v = q_ref[:, qc], k_ref[:, kc], do_ref[:, vc]
                p = jnp.exp(_scores(qv, kn, krv, on_diagonal) - lse_ref[h][:, :1])
                dp = _dot(do_v, v_ref[:, vc], NT)
                delta = jnp.sum(do_v.astype(F32) * o_ref[:, vc].astype(F32), axis=-1, keepdims=True)
                ds = (p * (dp - delta)).astype(BF16)
                dq_ref[rows, h * HEAD_W:h * HEAD_W + NOPE] += _dot(ds, kn, NN)
                dq_ref[rows, h * HEAD_W + NOPE:(h + 1) * HEAD_W] += _dot(ds, krv, NN)
                dv_part, dk_part = _dot(p.astype(BF16), do_v, TN), _dot(ds, qv, TN)
                if on_diagonal:
                    dv_acc[:, vc], dk_acc[:, qc] = dv_part, dk_part
                else:
                    dv_acc[:, vc] += dv_part
                    dk_acc[:, qc] += dk_part

        pl.when(i == j)(lambda: step(True))
        pl.when(i > j)(lambda: step(False))

        @pl.when(i == nb - 1)
        def _():
            dk_ref[...] = dk_acc[...].astype(BF16)
            dv_ref[...] = dv_acc[...].astype(BF16)

    qidx = lambda h, j, i: (jnp.maximum(i, j), h)
    return _call(
        body, (q, kv, kr, kv, o, do, lse), comm, name=name, grid=(heads // hp, nb, nb),
        in_specs=[pl.BlockSpec((t, hp * HEAD_W), qidx),
                  pl.BlockSpec((t, hp * NOPE), lambda h, j, i: (j, h)),
                  pl.BlockSpec((t, 2 * ROPE), lambda h, j, i: (j, 0)),
                  pl.BlockSpec((t, hp * VHEAD), lambda h, j, i: (j, heads // hp + h)),
                  pl.BlockSpec((t, hp * VHEAD), qidx), pl.BlockSpec((t, hp * VHEAD), qidx),
                  pl.BlockSpec((hp, t, LANES), lambda h, j, i: (h, jnp.maximum(i, j), 0))],
        out_specs=[pl.BlockSpec((s, hp * HEAD_W), lambda h, j, i: (0, h)),
                   pl.BlockSpec((t, hp * HEAD_W), lambda h, j, i: (j, h)),
                   pl.BlockSpec((t, hp * VHEAD), lambda h, j, i: (j, h))],
        out_shape=[jax.ShapeDtypeStruct((s, heads * HEAD_W), F32), jax.ShapeDtypeStruct((s, heads * HEAD_W), BF16),
                   jax.ShapeDtypeStruct((s, heads * VHEAD), BF16)],
        scratch_shapes=[pltpu.VMEM((t, hp * HEAD_W), F32), pltpu.VMEM((t, hp * VHEAD), F32)])


def _mla_bwd_mid(dq, dk, dv, rope_q, rope_k, heads, name):
    s = dq.shape[0]
    tb = _div_tile(s, 256, SUBLANES)

    def body(dq_ref, dk_ref, dv_ref, tq_ref, tk_ref, dqb_ref, dkv_ref, dkk_ref):
        tq = tq_ref[...]
        dkr = jnp.zeros((tb, 2 * ROPE), F32)
        for h in range(heads):
            cols = slice(h * HEAD_W, (h + 1) * HEAD_W)
            dqb_ref[:, cols] = (dq_ref[:, cols] * tq).astype(BF16)
            dkv_ref[:, h * NOPE:(h + 1) * NOPE] = dk_ref[:, h * HEAD_W:h * HEAD_W + NOPE]
            dkr = dkr + dk_ref[:, h * HEAD_W + NOPE:(h + 1) * HEAD_W].astype(F32)
        dkv_ref[:, heads * NOPE:] = dv_ref[...]
        dkk_ref[...] = (dkr + pltpu.roll(dkr, ROPE, axis=1)) * tk_ref[...]

    wq, wv = heads * HEAD_W, heads * VHEAD
    return pl.pallas_call(
        body, name=name, grid=(s // tb,),
        in_specs=[pl.BlockSpec((tb, wq), lambda i: (i, 0)), pl.BlockSpec((tb, wq), lambda i: (i, 0)),
                  pl.BlockSpec((tb, wv), lambda i: (i, 0)), pl.BlockSpec((tb, HEAD_W), lambda i: (i, 0)),
                  pl.BlockSpec((tb, 2 * ROPE), lambda i: (i, 0))],
        out_specs=[pl.BlockSpec((tb, wq), lambda i: (i, 0)), pl.BlockSpec((tb, heads * NOPE + wv), lambda i: (i, 0)),
                   pl.BlockSpec((tb, 2 * ROPE), lambda i: (i, 0))],
        out_shape=[jax.ShapeDtypeStruct((s, wq), BF16), jax.ShapeDtypeStruct((s, heads * NOPE + wv), BF16),
                   jax.ShapeDtypeStruct((s, 2 * ROPE), F32)],
        compiler_params=_params(),
    )(dq, dk, dv, rope_q, rope_k)


def _mla_bwd_post(z_lat, dqn, dkvn, dkk, q_g, kv_g, name):
    s, latw = z_lat.shape
    ql, kvl = q_g.shape[1], kv_g.shape[1]
    tb = _div_tile(s, 256, SUBLANES)

    def norm_bwd(xv, dn, g, dg_ref):
        r = lax.rsqrt(_rowmean(xv * xv) + EPS)
        xh = xv * r
        _accumulate(dg_ref, _colsum(dn * xh))
        dxh = dn * g
        return r * (dxh - xh * _rowmean(dxh * xh))

    def body(z_ref, dqn_ref, dkvn_ref, dkk_ref, qg_ref, kvg_ref, dz_ref, gq_ref, gkv_ref):
        dz_ref[:, :ql] = norm_bwd(z_ref[:, :ql], dqn_ref[...], qg_ref[...], gq_ref).astype(BF16)
        dz_ref[:, ql:ql + kvl] = norm_bwd(z_ref[:, ql:ql + kvl], dkvn_ref[...], kvg_ref[...], gkv_ref).astype(BF16)
        dz_ref[:, ql + kvl:] = dkk_ref[...].astype(BF16)

    return pl.pallas_call(
        body, name=name, grid=(s // tb,),
        in_specs=[pl.BlockSpec((tb, latw), lambda i: (i, 0)), pl.BlockSpec((tb, ql), lambda i: (i, 0)),
                  pl.BlockSpec((tb, kvl), lambda i: (i, 0)), pl.BlockSpec((tb, 2 * ROPE), lambda i: (i, 0)),
                  _row_spec(ql), _row_spec(kvl)],
        out_specs=[pl.BlockSpec((tb, latw), lambda i: (i, 0)), _row_spec(ql), _row_spec(kvl)],
        out_shape=[jax.ShapeDtypeStruct((s, latw), BF16), jax.ShapeDtypeStruct((1, ql), F32),
                   jax.ShapeDtypeStruct((1, kvl), F32)],
        compiler_params=_params(),
    )(z_lat, dqn, dkvn, dkk, q_g, kv_g)


def _shift_down(x, n):
    rows = lax.broadcasted_iota(jnp.int32, x.shape, 0)
    return jnp.where(rows >= n, pltpu.roll(x, n, axis=0), 0.0)


def _shift_up(x, n):
    s = x.shape[0]
    rows = lax.broadcasted_iota(jnp.int32, x.shape, 0)
    return jnp.where(rows < s - n, pltpu.roll(x, s - n, axis=0), 0.0)


def _conv(pre, w_ref, b_ref):
    return (w_ref[2:3, :] * pre + w_ref[1:2, :] * _shift_down(pre, 1) + w_ref[0:1, :] * _shift_down(pre, 2)
            + b_ref[...])


def _conv_fwd(up_pre, conv_w, conv_b, name):
    s, ff2 = up_pre.shape
    ff = ff2 // 2
    tc = _div_tile(ff, 256)
    nb = ff // tc

    def body(pg_ref, pv_ref, wg_ref, wv_ref, bg_ref, bv_ref, act_ref):
        gate = _conv(pg_ref[...].astype(F32), wg_ref, bg_ref)
        val = _conv(pv_ref[...].astype(F32), wv_ref, bv_ref)
        act_ref[...] = (gate * _sigmoid(gate) * val).astype(BF16)

    def col(rows, off):
        return pl.BlockSpec((rows, tc), lambda j: (0, j + off))

    return pl.pallas_call(
        body, name=name, grid=(nb,),
        in_specs=[col(s, 0), col(s, nb), col(CONV_TAPS, 0), col(CONV_TAPS, nb), col(1, 0), col(1, nb)],
        out_specs=col(s, 0), out_shape=jax.ShapeDtypeStruct((s, ff), BF16), compiler_params=_params(),
    )(up_pre, up_pre, conv_w, conv_w, conv_b, conv_b)


def _conv_bwd(up_pre, dact, conv_w, conv_b, name, comm=None):
    s, ff2 = up_pre.shape
    ff = ff2 // 2
    tc = _div_tile(ff, 256)
    nb = ff // tc

    def half(pre, dx, w_ref, dpre_ref, gw_ref, gb_ref):
        gb_ref[...] = _colsum(dx)
        gw_ref[0:1, :] = _colsum(dx * _shift_down(pre, 2))
        gw_ref[1:2, :] = _colsum(dx * _shift_down(pre, 1))
        gw_ref[2:3, :] = _colsum(dx * pre)
        dpre_ref[...] = (w_ref[2:3, :] * dx + w_ref[1:2, :] * _shift_up(dx, 1)
                         + w_ref[0:1, :] * _shift_up(dx, 2)).astype(BF16)

    def body(pg_ref, pv_ref, da_ref, wg_ref, wv_ref, bg_ref, bv_ref, dup_ref, gwg_ref, gwv_ref, gbg_ref, gbv_ref):
        pre_g, pre_v = pg_ref[...].astype(F32), pv_ref[...].astype(F32)
        gate = _conv(pre_g, wg_ref, bg_ref)
        val = _conv(pre_v, wv_ref, bv_ref)
        da = da_ref[...].astype(F32)
        sg = _sigmoid(gate)
        half(pre_v, da * gate * sg, wv_ref, dup_ref.at[1], gwv_ref, gbv_ref)
        half(pre_g, da * val * sg * (1.0 + gate * (1.0 - sg)), wg_ref, dup_ref.at[0], gwg_ref, gbg_ref)

    def col(rows, off):
        return pl.BlockSpec((rows, tc), lambda j: (0, j + off))

    return _call(
        body, (up_pre, up_pre, dact, conv_w, conv_w, conv_b, conv_b), comm, name=name, grid=(nb,),
        in_specs=[col(s, 0), col(s, nb), col(s, 0), col(CONV_TAPS, 0), col(CONV_TAPS, nb), col(1, 0), col(1, nb)],
        out_specs=[pl.BlockSpec((2, s, tc), lambda j: (0, 0, j)), col(CONV_TAPS, 0), col(CONV_TAPS, 0),
                   col(1, 0), col(1, 0)],
        out_shape=[jax.ShapeDtypeStruct((2, s, ff), BF16)] + [jax.ShapeDtypeStruct((CONV_TAPS, ff), F32)] * 2
        + [jax.ShapeDtypeStruct((1, ff), F32)] * 2)


def _ada_fwd(c_all, w, b, name):
    nseq, d = c_all.shape
    na = w.shape[1]
    tn = _div_tile(na, 512)

    def body(c_ref, w_ref, b_ref, o_ref):
        cv = c_ref[...]
        sc = cv * _sigmoid(cv)
        o_ref[...] = jnp.dot(sc, w_ref[...], preferred_element_type=F32, precision=lax.Precision.HIGHEST) + b_ref[...]

    return pl.pallas_call(
        body, name=name, grid=(na // tn,),
        in_specs=[pl.BlockSpec((nseq, d), lambda j: (0, 0)), pl.BlockSpec((d, tn), lambda j: (0, j)),
                  pl.BlockSpec((1, tn), lambda j: (0, j))],
        out_specs=pl.BlockSpec((nseq, tn), lambda j: (0, j)),
        out_shape=jax.ShapeDtypeStruct((nseq, na), F32), compiler_params=_params(),
    )(c_all, w, b)


def _ada_bwd(c_all_t, dmod, name):
    d, nseq = c_all_t.shape
    na = dmod.shape[1]
    tm, tn = _div_tile(d, 256, SUBLANES), _div_tile(na, 512)

    def body(c_ref, dm_ref, o_ref):
        cv = c_ref[...]
        sc = cv * _sigmoid(cv)
        acc = sc[:, 0:1] * dm_ref[0:1, :]
        for bi in range(1, nseq):
            acc = acc + sc[:, bi:bi + 1] * dm_ref[bi:bi + 1, :]
        o_ref[...] = acc

    return pl.pallas_call(
        body, name=name, grid=(d // tm, na // tn),
        in_specs=[pl.BlockSpec((tm, nseq), lambda i, j: (i, 0)), pl.BlockSpec((nseq, tn), lambda i, j: (0, j))],
        out_specs=pl.BlockSpec((tm, tn), lambda i, j: (i, j)),
        out_shape=jax.ShapeDtypeStruct((d, na), F32), compiler_params=_params(),
    )(c_all_t, dmod)


def _adamw(w, g, m, v, name, comm=None, after=None):
    rows, cols = w.shape
    tb = _div_tile(rows, max(SUBLANES, (256 * 1024) // cols // SUBLANES * SUBLANES), SUBLANES)
    c1 = 1.0 / (1.0 - ADAM_B1 ** ADAM_STEP)
    c2 = 1.0 / (1.0 - ADAM_B2 ** ADAM_STEP)

    def body(*refs):
        w_ref, g_ref, m_ref, v_ref = refs[:4]
        d_ref, nm_ref, nv_ref = refs[-3:]
        gv = g_ref[...]
        nm = ADAM_B1 * m_ref[...] + (1.0 - ADAM_B1) * gv
        nv = ADAM_B2 * v_ref[...] + (1.0 - ADAM_B2) * (gv * gv)
        nm_ref[...] = nm
        nv_ref[...] = nv
        d_ref[...] = -ADAM_LR * ((nm * c1) / (jnp.sqrt(nv * c2) + ADAM_EPS) + ADAM_WD * w_ref[...])

    blk = pl.BlockSpec((tb, cols), lambda i: (i, 0))
    operands, in_specs = (w, g, m, v), [blk] * 4
    if after is not None:
        operands, in_specs = operands + (after,), in_specs + [pl.BlockSpec(after.shape, lambda i: (0, 0))]
    return _call(body, operands, comm, name=name, grid=(rows // tb,), in_specs=in_specs, out_specs=[blk] * 3,
                 out_shape=[jax.ShapeDtypeStruct((rows, cols), F32)] * 3)


def _sum_leading(parts, name):
    n, rows, cols = parts.shape
    tb = _div_tile(rows, 512, SUBLANES)

    def body(p_ref, o_ref):
        acc = p_ref[0]
        for k in range(1, n):
            acc = acc + p_ref[k]
        o_ref[...] = acc

    return pl.pallas_call(
        body, name=name, grid=(rows // tb,), in_specs=[pl.BlockSpec((n, tb, cols), lambda i: (0, i, 0))],
        out_specs=pl.BlockSpec((tb, cols), lambda i: (i, 0)),
        out_shape=jax.ShapeDtypeStruct((rows, cols), F32), compiler_params=_params(),
    )(parts)


def _place():
    x, y, c = lax.axis_index("x"), lax.axis_index("y"), lax.axis_index("c")
    return x, y, c, [(1 - x, y), (x, 1 - y), (1 - x, 1 - y)]


def _all_gather(block, name):
    m_per, n = block.shape

    def body(x_ref, out_ref, send_sems, recv_sems, local_sem):
        x, y, c, chips = _place()
        me, sibling = (x, y, c), (x, y, 1 - c)

        def rows(px, py, pc):
            return out_ref.at[pl.ds((4 * px + 2 * py + pc) * m_per, m_per), :]

        def copy(k, blk, to, src=None):
            return pltpu.make_async_remote_copy(
                src_ref=rows(*blk) if src is None else src, dst_ref=rows(*blk), send_sem=send_sems.at[k],
                recv_sem=recv_sems.at[k], device_id=to, device_id_type=MESH)

        mine = pltpu.make_async_copy(x_ref, rows(*me), local_sem)
        mine.start()
        first = [copy(0, me, sibling, src=x_ref)]
        first += [copy(1 + j, me, (*chip, c), src=x_ref) for j, chip in enumerate(chips)]
        for cp in first:
            cp.start()
        passed = [copy(4 + j, (*chip, c), sibling) for j, chip in enumerate(chips)]
        for j, chip in enumerate(chips):
            copy(1 + j, (*chip, c), me).wait_recv()
            passed[j].start()
        copy(0, sibling, me).wait_recv()
        for j, chip in enumerate(chips):
            copy(4 + j, (*chip, 1 - c), me).wait_recv()
        for cp in first + passed:
            cp.wait_send()
        mine.wait()

    return pl.pallas_call(
        body, name=name, out_shape=jax.ShapeDtypeStruct((N_DEV * m_per, n), block.dtype),
        in_specs=[pl.BlockSpec(memory_space=pltpu.VMEM)], out_specs=pl.BlockSpec(memory_space=pltpu.VMEM),
        scratch_shapes=[pltpu.SemaphoreType.DMA((7,)), pltpu.SemaphoreType.DMA((7,)), pltpu.SemaphoreType.DMA],
        compiler_params=_params(),
    )(block)


def _hbm_specs(n):
    return [pl.BlockSpec(memory_space=HBM)] * n


def _half_rows(ref, half, lead=None):
    h = ref.shape[-2] // 2
    rows = pl.ds(pl.multiple_of(half * h, 2 * SUBLANES), h)
    return ref.at[rows, :] if lead is None else ref.at[lead, rows, :]


class _Comm:
    def __init__(self, operands, out_shape, sem_dims, build, aliases=None):
        self.operands, self.out_shape, self.sem_dims = list(operands), list(out_shape), list(sem_dims)
        self.scratch = [pltpu.SemaphoreType.DMA(d) for d in sem_dims]
        self.build, self.aliases = build, dict(aliases or {})


class _SemGrid:
    def __init__(self, sems, dims):
        self.sems, self.dims, self.at = list(sems), tuple(dims), self

    def __getitem__(self, index):
        index = index if isinstance(index, tuple) else (index,)
        flat = 0
        for i, d in zip(index, self.dims):
            flat = flat * d + i
        return self.sems[flat]


def _call(body, operands, comm=None, *, name, grid, in_specs, out_specs, out_shape, scratch_shapes=(),
          input_output_aliases=None):
    aliases = dict(input_output_aliases or {})
    if comm is None:
        return pl.pallas_call(
            body, name=name, grid=grid, in_specs=in_specs, out_specs=out_specs, out_shape=out_shape,
            scratch_shapes=list(scratch_shapes), input_output_aliases=aliases, compiler_params=_params())(*operands)
    single = not isinstance(out_shape, (list, tuple))
    outs = [out_shape] if single else list(out_shape)
    ospecs = [out_specs] if single else list(out_specs)
    n_in, n_out, n_scr = len(operands), len(outs), len(scratch_shapes)
    c_in, c_out = len(comm.operands), len(comm.out_shape)
    for i, o in comm.aliases.items():
        aliases[n_in + i] = n_out + o

    def hosted(*refs):
        ins, c_ins = refs[:n_in], refs[n_in:n_in + c_in]
        o0 = n_in + c_in
        o_refs, c_outs = refs[o0:o0 + n_out], refs[o0 + n_out:o0 + n_out + c_out]
        s0 = o0 + n_out + c_out
        scr, sems = refs[s0:s0 + n_scr], refs[s0 + n_scr:]
        stages = comm.build(c_ins, c_outs, sems)
        step, n_steps = 0, 1
        for dim, size in enumerate(grid):
            step, n_steps = step * size + pl.program_id(dim), n_steps * size
        pl.when(step == 0)(stages[0])
        body(*ins, *o_refs, *scr)
        for stage in stages[1:-1]:
            pl.when(step == (n_steps * MIDDLE_STAGE_AT) // 100)(stage)
        pl.when(step == n_steps - 1)(stages[-1])

    res = pl.pallas_call(
        hosted, name=name, grid=grid, in_specs=list(in_specs) + _hbm_specs(c_in),
        out_specs=ospecs + _hbm_specs(c_out), out_shape=outs + comm.out_shape,
        scratch_shapes=list(scratch_shapes) + comm.scratch, input_output_aliases=aliases,
        compiler_params=_params())(*operands, *comm.operands)
    return (res[0] if single else res[:n_out]), res[n_out:]


def _run_comm(comm, name):
    c_in, c_out = len(comm.operands), len(comm.out_shape)

    def body(*refs):
        for stage in comm.build(refs[:c_in], refs[c_in:c_in + c_out], refs[c_in + c_out:]):
            stage()

    return pl.pallas_call(
        body, name=name, in_specs=_hbm_specs(c_in), out_specs=_hbm_specs(c_out), out_shape=comm.out_shape,
        scratch_shapes=comm.scratch, input_output_aliases=comm.aliases, compiler_params=_params())(*comm.operands)


def _gather_comm(shards):
    nw = len(shards)

    def build(in_refs, out_refs, sems):
        send_sems, recv_sems = sems
        x, y, c, chips = _place()
        me, sibling = (x, y, c), (x, y, 1 - c)
        across_x, across_y, diagonal = chips

        def copy(w, k, block, part, to, src=None):
            rows = out_refs[w].shape[1]
            size = rows // 2 if part[0] == 0 else rows // 4
            first_row = part[1] * (rows // 2) + (part[2] * size if part[0] else 0)
            dst = out_refs[w].at[2 * block[0] + block[1], pl.ds(pl.multiple_of(first_row, 2 * SUBLANES), size), :]
            return pltpu.make_async_remote_copy(
                src_ref=dst if src is None else src, dst_ref=dst, send_sem=send_sems.at[w, k],
                recv_sem=recv_sems.at[w, k], device_id=to, device_id_type=MESH)

        first = [copy(w, j, (x, y), (0, c), (*chip, c), src=_half_rows(in_refs[w], c))
                 for w in range(nw) for j, chip in enumerate((across_x, across_y))]
        passed = [[copy(w, 2, across_x, (1, c, 0), (*across_y, c)), copy(w, 3, across_y, (1, c, 1), (*across_x, c)),
                   copy(w, 4, across_x, (0, c), sibling), copy(w, 5, across_y, (0, c), sibling)] for w in range(nw)]
        last = [[copy(w, 6, diagonal, (1, c, 0), sibling), copy(w, 7, diagonal, (1, c, 1), sibling)]
                for w in range(nw)]

        def start():
            for cp in first:
                cp.start()

        def middle():
            for w in range(nw):
                copy(w, 0, across_x, (0, c), me).wait_recv()
                copy(w, 1, across_y, (0, c), me).wait_recv()
                for cp in passed[w]:
                    cp.start()

        def finish():
            for w in range(nw):
                copy(w, 2, diagonal, (1, c, 0), me).wait_recv()
                copy(w, 3, diagonal, (1, c, 1), me).wait_recv()
                for cp in last[w]:
                    cp.start()
            for w in range(nw):
                for k, block, part in ((4, across_x, (0, 1 - c)), (5, across_y, (0, 1 - c)),
                                       (6, diagonal, (1, 1 - c, 0)), (7, diagonal, (1, 1 - c, 1))):
                    copy(w, k, block, part, me).wait_recv()
            for cp in first + sum(passed, []) + sum(last, []):
                cp.wait_send()

        return start, middle, finish

    return _Comm(shards, [jax.ShapeDtypeStruct((N_CHIPS,) + w.shape, w.dtype) for w in shards],
                 [(nw, 8), (nw, 8)], build)


def _swap_comm(gs):
    nw = len(gs)

    def build(in_refs, out_refs, sems):
        send_sems, recv_sems = sems
        x, y, c, _ = _place()
        cps = []
        for w in range(nw):
            h = in_refs[w].shape[1] // 2
            src = in_refs[w].at[:, pl.ds(pl.multiple_of((1 - c) * h, 2 * SUBLANES), h), :]
            cps.append(pltpu.make_async_remote_copy(
                src_ref=src, dst_ref=out_refs[w], send_sem=send_sems.at[w], recv_sem=recv_sems.at[w],
                device_id=(x, y, 1 - c), device_id_type=MESH))

        def start():
            for cp in cps:
                cp.start()

        def finish():
            for cp in cps:
                cp.wait()

        return start, finish

    return _Comm(gs, [jax.ShapeDtypeStruct((N_CHIPS, g.shape[1] // 2, g.shape[2]), g.dtype) for g in gs],
                 [(nw,), (nw,)], build)


def _exchange_comm(s1s):
    nw = len(s1s)

    def build(in_refs, out_refs, sems):
        send_sems, recv_sems = sems
        x, y, c, chips = _place()
        cps = [pltpu.make_async_remote_copy(
            src_ref=in_refs[w].at[2 * chip[0] + chip[1]], dst_ref=out_refs[w].at[j], send_sem=send_sems.at[w, j],
            recv_sem=recv_sems.at[w, j], device_id=(*chip, c), device_id_type=MESH)
            for w in range(nw) for j, chip in enumerate(chips)]

        def start():
            for cp in cps:
                cp.start()

        def finish():
            for cp in cps:
                cp.wait()

        return start, finish

    return _Comm(s1s, [jax.ShapeDtypeStruct((N_CHIPS - 1,) + s.shape[1:], s.dtype) for s in s1s],
                 [(nw, 3), (nw, 3)], build)


def _size(dims):
    n = 1
    for d in dims:
        n *= d
    return n


def _sem_grids(comm, sem_refs):
    grids, pos = [], 0
    for dims in comm.sem_dims:
        grids.append(_SemGrid(sem_refs[pos:pos + _size(dims)], dims))
        pos += _size(dims)
    return grids


def _comm_split_start(comm, name, after=()):
    c_in, c_out = len(comm.operands), len(comm.out_shape)
    counts = [_size(d) for d in comm.sem_dims]
    n_sem = sum(counts)
    assert not comm.aliases

    def body(*refs):
        srcs, lands = refs[:c_in], refs[c_in:c_in + c_out]
        first_sem = c_in + c_out + len(after)
        start, _ = comm.build(srcs, lands, _sem_grids(comm, refs[first_sem:first_sem + n_sem]))
        start()
        refs[-1][...] = jnp.zeros(refs[-1].shape, refs[-1].dtype)

    lands = [pltpu.with_memory_space_constraint(lax.empty(o.shape, o.dtype), HBM) for o in comm.out_shape]
    srcs = [pltpu.with_memory_space_constraint(a, HBM) for a in comm.operands]
    res = pl.pallas_call(
        body, name=name, in_specs=_hbm_specs(c_in + c_out) + [pl.BlockSpec(memory_space=pl.ANY)] * len(after),
        out_specs=[pl.BlockSpec(memory_space=pltpu.SEMAPHORE)] * n_sem + _hbm_specs(c_in + c_out)
        + [pl.BlockSpec(memory_space=pltpu.VMEM)],
        out_shape=[pltpu.SemaphoreType.DMA(())] * n_sem + [pltpu.HBM(a.shape, a.dtype) for a in comm.operands]
        + [pltpu.HBM(o.shape, o.dtype) for o in comm.out_shape] + [jax.ShapeDtypeStruct((SUBLANES, LANES), F32)],
        input_output_aliases={i: n_sem + i for i in range(c_in + c_out)},
        compiler_params=_params(has_side_effects=pltpu.SideEffectType.DATAFLOW_SIDE_EFFECTING))(*srcs, *lands, *after)
    return res[:-1], res[-1]


def _comm_split_wait(comm, state, after, name):
    c_in, c_out, n_sem = len(comm.operands), len(comm.out_shape), sum(_size(d) for d in comm.sem_dims)
    sems, srcs, lands = state[:n_sem], state[n_sem:n_sem + c_in], state[n_sem + c_in:]

    def body(*refs):
        src_refs, land_refs = refs[:c_in], refs[c_in:c_in + c_out]
        _, finish = comm.build(src_refs, land_refs, _sem_grids(comm, refs[c_in + c_out:c_in + c_out + n_sem]))
        finish()

    sem_spec = pl.BlockSpec(memory_space=pltpu.SEMAPHORE)
    res = pl.pallas_call(
        body, name=name, in_specs=_hbm_specs(c_in + c_out) + [sem_spec] * n_sem + [pl.BlockSpec(memory_space=pl.ANY)],
        out_specs=_hbm_specs(c_in + c_out),
        out_shape=[pltpu.HBM(a.shape, a.dtype) for a in srcs] + [pltpu.HBM(o.shape, o.dtype) for o in lands],
        input_output_aliases={i: i for i in range(c_in + c_out)},
        compiler_params=_params(has_side_effects=pltpu.SideEffectType.DATAFLOW_SIDE_EFFECTING),
    )(*srcs, *lands, *sems, after)
    return res[:c_in], res[c_in:]


def _share_comm(fs):
    nw = len(fs)

    def build(in_refs, out_refs, sems):
        del in_refs
        send_sems, recv_sems = sems
        x, y, c, _ = _place()

        def copy(w, half):
            rows = _half_rows(out_refs[w], half)
            return pltpu.make_async_remote_copy(
                src_ref=rows, dst_ref=rows, send_sem=send_sems.at[w], recv_sem=recv_sems.at[w],
                device_id=(x, y, 1 - c), device_id_type=MESH)

        sends = [copy(w, c) for w in range(nw)]

        def start():
            for cp in sends:
                cp.start()

        def finish():
            for w in range(nw):
                copy(w, 1 - c).wait_recv()
            for cp in sends:
                cp.wait_send()

        return start, finish

    return _Comm(fs, [jax.ShapeDtypeStruct(f.shape, f.dtype) for f in fs],
                 [(nw,), (nw,)], build,
                 aliases={w: w for w in range(nw)})


def _add_sibling(g, r1, place, name):
    nch, h, cols = r1.shape
    tr = _div_tile(h, 256, 2 * SUBLANES)
    nb = h // tr

    def body(place_ref, g_ref, r_ref, o_ref):
        del place_ref
        o_ref[...] = (g_ref[...].astype(F32) + r_ref[...].astype(F32)).astype(BF16)

    spec = pltpu.PrefetchScalarGridSpec(
        num_scalar_prefetch=1, grid=(nch, nb),
        in_specs=[pl.BlockSpec((None, tr, cols), lambda k, i, p: (k, p[0] * nb + i, 0)),
                  pl.BlockSpec((None, tr, cols), lambda k, i, p: (k, i, 0))],
        out_specs=pl.BlockSpec((None, tr, cols), lambda k, i, p: (k, i, 0)))
    return pl.pallas_call(body, name=name, grid_spec=spec, out_shape=jax.ShapeDtypeStruct((nch, h, cols), BF16),
                          compiler_params=_params())(place, g, r1)


def _add_chips(s1, r2, place, name):
    _, h, cols = s1.shape
    tr = _div_tile(h, 256, 2 * SUBLANES)
    nb = h // tr

    def body(place_ref, s_ref, r_ref, o_ref):
        del place_ref
        acc = s_ref[...].astype(F32)
        for j in range(N_CHIPS - 1):
            acc = acc + r_ref[j].astype(F32)
        o_ref[...] = acc

    spec = pltpu.PrefetchScalarGridSpec(
        num_scalar_prefetch=1, grid=(nb,),
        in_specs=[pl.BlockSpec((None, tr, cols), lambda i, p: (p[1], i, 0)),
                  pl.BlockSpec((N_CHIPS - 1, tr, cols), lambda i, p: (0, i, 0))],
        out_specs=pl.BlockSpec((tr, cols), lambda i, p: (p[0] * nb + i, 0)))
    return pl.pallas_call(body, name=name, grid_spec=spec, out_shape=jax.ShapeDtypeStruct((2 * h, cols), F32),
                          compiler_params=_params())(place, s1, r2)


def _quarter_turn(m):
    h = m.shape[-1] // 2
    return jnp.concatenate([-m[..., h:], m[..., :h]], axis=-1)


def _quarter_turn_back(m):
    h = m.shape[-1] // 2
    return jnp.concatenate([m[..., h:], -m[..., :h]], axis=-1)


def _join_cols(sh):
    return jnp.concatenate([sh[k] for k in range(N_CHIPS)], axis=1)


def _split_cols(full):
    c = full.shape[1] // N_CHIPS
    return jnp.stack([full[:, k * c:(k + 1) * c] for k in range(N_CHIPS)])


def kernel(x, c, positions, w_ada, b_ada, pre_norm1_g, w_in, gm_ln_g, gm_ln_b, gm_w_s, gm_b_s, w_branch_a, q_norm_g, w_uq, kv_norm_g, w_ukv, w_branch_b, w_out, post_norm1_g, pre_norm2_g, w_up, conv_w, conv_b, w_down, post_norm2_g, loss_target, m_w_ada, m_b_ada, m_pre_norm1_g, m_w_in, m_gm_ln_g, m_gm_ln_b, m_gm_w_s, m_gm_b_s, m_w_branch_a, m_q_norm_g, m_w_uq, m_kv_norm_g, m_w_ukv, m_w_branch_b, m_w_out, m_post_norm1_g, m_pre_norm2_g, m_w_up, m_conv_w, m_conv_b, m_w_down, m_post_norm2_g, v_w_ada, v_b_ada, v_pre_norm1_g, v_w_in, v_gm_ln_g, v_gm_ln_b, v_gm_w_s, v_gm_b_s, v_w_branch_a, v_q_norm_g, v_w_uq, v_kv_norm_g, v_w_ukv, v_w_branch_b, v_w_out, v_post_norm1_g, v_pre_norm2_g, v_w_up, v_conv_w, v_conv_b, v_w_down, v_post_norm2_g):
    given = dict(locals())
    s, d = x.shape[1], x.shape[2]
    gw = gm_ln_g.shape[0]
    ql, kvl = q_norm_g.shape[0], kv_norm_g.shape[0]
    heads = N_CHIPS * w_uq.shape[1] // (NOPE + ROPE)
    ff = N_CHIPS * w_down.shape[0]
    assert gw == d and N_CHIPS * w_ukv.shape[1] == heads * (NOPE + VHEAD)
    ix, iy, ic = lax.axis_index("x"), lax.axis_index("y"), lax.axis_index("c")
    chip = 2 * ix + iy
    dev = 2 * chip + ic
    row = lambda v: v.reshape(1, -1)

    c_all = _all_gather(jnp.pad(c, ((0, SUBLANES - 1), (0, 0))), "gather_c").reshape(N_DEV, SUBLANES, d)[:, 0]
    na = w_ada.shape[1]
    b_ada_mine = lax.dynamic_slice(b_ada, (chip * na,), (na,))
    mod_cols = _ada_fwd(c_all, w_ada, row(b_ada_mine), "ada_fwd")
    mod_all = _all_gather(mod_cols, "gather_mod").reshape(N_CHIPS, N_CORES, N_DEV, na)[:, 0]
    mod = lax.dynamic_index_in_dim(mod_all, dev, axis=1, keepdims=False).reshape(N_MOD, d)
    shift1, scale1, gate1, shift2, scale2, gate2 = (mod[i:i + 1] for i in range(N_MOD))

    mine = {n: given[n].astype(BF16) for n in BIG}
    gather = lambda names: _gather_comm([mine[n] for n in names])
    whole = lambda n, g: lax.dynamic_update_slice(g, mine[n][None], (chip, 0, 0))
    rows4 = lambda sh4: sh4.reshape(-1, sh4.shape[2])
    wi = _join_cols(whole("w_in", _run_comm(gather(["w_in"]), "gather_w_in")[0]))
    o_q, o_kv, o_pe, o_ga = 2 * gw, 2 * gw + ql, 2 * gw + ql + kvl, 2 * gw + ql + kvl + ROPE
    w_in_big = jnp.concatenate([wi[:, :o_q], wi[:, o_ga:]], axis=1)
    w_in_lat = jnp.concatenate([wi[:, o_q:o_ga], _quarter_turn(wi[:, o_pe:o_ga])], axis=1)

    inv = ROPE_THETA ** (-jnp.arange(0, ROPE, 2, dtype=F32) / ROPE)
    ang = positions[0].astype(F32)[:, None] * inv
    cos, sin = jnp.cos(ang), jnp.sin(ang)
    rope_k = jnp.concatenate([cos, cos, sin, sin], axis=1)
    softmax_scale = float(NOPE + ROPE) ** -0.5
    rope_q = jnp.concatenate([jnp.ones((s, NOPE), F32), rope_k], axis=1) * softmax_scale

    x2d, tgt = x[0], loss_target[0]
    g_pre1, g_post1, g_pre2, g_post2 = row(pre_norm1_g), row(post_norm1_g), row(pre_norm2_g), row(post_norm2_g)
    ln_g, ln_b, q_g, kv_g = row(gm_ln_g), row(gm_ln_b), row(q_norm_g), row(kv_norm_g)
    b_s_t = gm_b_s.T
    conv_wf = _all_gather(jnp.pad(conv_w, ((0, SUBLANES - CONV_TAPS), (0, 0))), "gather_conv_w")
    conv_wf = conv_wf.reshape(N_CHIPS, N_CORES, SUBLANES, conv_w.shape[1])[:, 0, :CONV_TAPS]
    conv_wf = conv_wf.transpose(1, 0, 2).reshape(CONV_TAPS, 2 * ff)
    conv_bf = row(conv_b)

    h1 = _prenorm(x2d, g_pre1, scale1, shift1, "prenorm1")
    z_big, (g_uq, g_ukv, g_a) = _matmul(h1, w_in_big, mode="nn", out_dtype=F32, name="mm_z_big", tm=s,
                                        comm=gather(["w_uq", "w_ukv", "w_branch_a"]))
    wq = _join_cols(whole("w_uq", g_uq)).reshape(ql, heads, NOPE + ROPE)
    w_q = jnp.concatenate([wq, _quarter_turn(wq[:, :, NOPE:])], axis=2).reshape(ql, heads * HEAD_W)
    w_kv = _join_cols(whole("w_ukv", g_ukv)).reshape(kvl, heads, 2, NOPE).transpose(0, 2, 1, 3)
    w_kv = w_kv.reshape(kvl, 2 * heads * NOPE)
    w_a = rows4(whole("w_branch_a", g_a))
    z_lat = _matmul(h1, w_in_lat, mode="nn", out_dtype=F32, name="mm_z_lat", tm=s, tn=1024)
    a_act = _gmlp_fwd(z_big, ln_g, ln_b, gm_w_s, b_s_t, "gmlp_fwd")
    qn, kvn, kr = _mla_prep(z_lat, q_g, kv_g, rope_k, "mla_prep")
    q_rot = _matmul(qn, w_q, mode="nn", out_dtype=BF16, name="mm_q", tm=s, tn=HEAD_W, mul=rope_q)
    kv_all = _matmul(kvn, w_kv, mode="nn", out_dtype=BF16, name="mm_kv", tm=s, tn=1024)
    (o_att, lse), (g_b, g_o, g_up) = _attn_fwd(q_rot, kv_all, kr, heads, "attn_fwd",
                                               comm=gather(["w_branch_b", "w_out", "w_up"]))
    w_b, w_o, w_upf = rows4(whole("w_branch_b", g_b)), rows4(whole("w_out", g_o)), whole("w_up", g_up)
    y_a = _matmul(a_act, w_a, mode="nn", out_dtype=F32, name="mm_y_a", tm=s)
    y_b = _matmul(o_att, w_b, mode="nn", out_dtype=F32, name="mm_y_b", tm=s)
    merged = _merge(z_big, y_a, y_b, "merge")
    y1 = _matmul(merged, w_o, mode="nn", out_dtype=F32, name="mm_y1", tm=s)
    x1, h2 = _post_pre(x2d, y1, gate1, g_post1, g_pre2, scale2, shift2, "post1_pre2")

    up_pre, (g_dn,) = _matmul(h2, w_upf, mode="nn", out_dtype=BF16, name="mm_up", tm=s, tn=1408,
                              comm=gather(["w_down"]))
    w_dn = rows4(whole("w_down", g_dn))
    act = _conv_fwd(up_pre, conv_wf, conv_bf, "conv_fwd")
    ffn = _matmul(act, w_dn, mode="nn", out_dtype=F32, name="mm_ffn", tm=s, tk=1408)

    dffn, dgate2, g_post2_grad, dx2, loss_part = _post_bwd(ffn, gate2, g_post2, "post2_bwd", xin=x1, target=tgt)
    loss = lax.psum(loss_part[0, 0], ("x", "y", "c"))
    place = jnp.stack([ic, chip]).astype(jnp.int32)
    rows_of = lambda g: g.reshape(N_CHIPS, g.shape[0] // N_CHIPS, g.shape[1])
    add_sibling = lambda names, gs, r1s: [_add_sibling(g, r1, place, "rs_add_sibling_" + n)
                                          for n, g, r1 in zip(names, gs, r1s)]
    add_chips = lambda names, s1s, r2s: [_add_chips(s1, r2, place, "rs_add_chips_" + n)
                                         for n, s1, r2 in zip(names, s1s, r2s)]
    dact = _matmul(dffn, w_dn, mode="nt", out_dtype=BF16, name="mm_dact", tm=s)
    gp_down = [rows_of(_matmul(act, dffn, mode="tn", out_dtype=BF16, name="mm_gw_down", tn=1024, tk=s))]
    (dup, gcw_g, gcw_v, gcb_g, gcb_v), r1_down = _conv_bwd(up_pre, dact, conv_wf, conv_bf, "conv_bwd",
                                                            comm=_swap_comm(gp_down))
    s1_down = add_sibling(["w_down"], gp_down, r1_down)
    dh2, r2_down = _matmul(dup, w_upf, mode="nt", out_dtype=F32, name="mm_dh2", tm=s, tk=1408,
                           comm=_exchange_comm(s1_down))
    half_down = add_chips(["w_down"], s1_down, r2_down)
    gw_up = _matmul(h2, dup, mode="tn", out_dtype=BF16, name="mm_gw_up", tn=1408, tk=s, out_groups=N_CHIPS)
    dx1, dshift2, dscale2, g_pre2_grad = _prenorm_bwd(x1, dh2, dx2, g_pre2, scale2, "prenorm2_bwd")

    dy1, dgate1, g_post1_grad = _post_bwd(y1, gate1, g_post1, "post1_bwd", dxo=dx1)
    dmerged = _matmul(dy1, w_o, mode="nt", out_dtype=F32, name="mm_dmerged", tm=s)
    gw_out = _matmul(merged, dy1, mode="tn", out_dtype=BF16, name="mm_gw_out", tn=1024, tk=s)
    dy_a, dy_b, dz_big = _merge_bwd(dmerged, z_big, y_a, y_b, "merge_bwd")
    da = _matmul(dy_a, w_a, mode="nt", out_dtype=F32, name="mm_da", tm=s)
    gw_a = _matmul(a_act, dy_a, mode="tn", out_dtype=BF16, name="mm_gw_a", tn=1024, tk=s)
    do = _matmul(dy_b, w_b, mode="nt", out_dtype=BF16, name="mm_do", tm=s)
    gw_b = _matmul(o_att, dy_b, mode="tn", out_dtype=BF16, name="mm_gw_b", tn=1024, tk=s)
    mid = ["w_up", "w_out", "w_branch_a", "w_branch_b"]
    gp_mid = [gw_up, rows_of(gw_out), rows_of(gw_a), rows_of(gw_b)]
    (dz_big, g_ws, g_bs_t, g_ln_g, g_ln_b), r1_mid = _gmlp_bwd(z_big, da, dz_big, ln_g, ln_b, gm_w_s, b_s_t,
                                                                "gmlp_bwd", comm=_swap_comm(gp_mid))
    s1_mid = add_sibling(mid, gp_mid, r1_mid)
    (dq, dk, dv), r2_up_out = _attn_bwd(q_rot, kv_all, kr, o_att, do, lse, heads, "attn_bwd",
                                        comm=_exchange_comm(s1_mid[:2]))
    dq_big, dkv, dkk = _mla_bwd_mid(dq, dk, dv, rope_q, rope_k, heads, "mla_bwd_mid")
    gw_q = _matmul(qn, dq_big, mode="tn", out_dtype=F32, name="mm_gw_q", tn=1024, tk=s)
    dqn = _matmul(dq_big, w_q, mode="nt", out_dtype=F32, name="mm_dqn", tm=s, tk=1024)
    gw_kv = _matmul(kvn, dkv, mode="tn", out_dtype=BF16, name="mm_gw_kv", tn=1024, tk=s)
    dkvn = _matmul(dkv, w_kv, mode="nt", out_dtype=F32, name="mm_dkvn", tm=s, tk=1024)
    dz_lat, g_q, g_kv = _mla_bwd_post(z_lat, dqn, dkvn, dkk, q_g, kv_g, "mla_bwd_post")

    partial = {
        "gm_ln_g": g_ln_g, "gm_ln_b": g_ln_b, "gm_w_s": g_ws, "gm_b_s": g_bs_t[:, :gm_b_s.shape[0]].T,
        "q_norm_g": g_q, "kv_norm_g": g_kv, "post_norm1_g": g_post1_grad, "pre_norm2_g": g_pre2_grad,
        "conv_w": jnp.concatenate([gcw_g, gcw_v], axis=1), "conv_b": jnp.concatenate([gcb_g, gcb_v], axis=1),
        "post_norm2_g": g_post2_grad,
    }
    flat = jnp.concatenate([partial[n].reshape(-1) for n in SMALL_PARTIAL])
    n_small = flat.shape[0]
    rows_small = -(-n_small // (LANES * SMALL_ROW_TILE)) * SMALL_ROW_TILE
    flat = jnp.pad(flat, (0, rows_small * LANES - n_small)).reshape(rows_small, LANES)
    small_sum = _sum_leading(_all_gather(flat, "gather_small").reshape(N_DEV, rows_small, LANES), "sum_small")
    small_sum = small_sum.reshape(-1)
    small_grads, off = {}, 0
    for n in SMALL_PARTIAL:
        shape = (CONV_TAPS, 2 * ff) if n == "conv_w" else given[n].shape
        small_grads[n] = small_sum[off:off + partial[n].size].reshape(shape)
        off += partial[n].size
    small_grads["conv_w"] = lax.dynamic_slice(small_grads["conv_w"], (0, chip * conv_w.shape[1]), conv_w.shape)

    dh1, r2_a_b = _matmul(dz_big, w_in_big, mode="nt", out_dtype=F32, name="mm_dh1_big", tm=s,
                          comm=_exchange_comm(s1_mid[2:]))
    half_mid = add_chips(mid, s1_mid, list(r2_up_out) + list(r2_a_b))
    dh1 = _matmul(dz_lat, w_in_lat, mode="nt", out_dtype=F32, name="mm_dh1_lat", tm=s, tk=1024, add=dh1)
    gw_in_big, shared = _matmul(h1, dz_big, mode="tn", out_dtype=BF16, name="mm_gw_in_big", tn=1024, tk=s,
                                comm=_share_comm(half_down + half_mid))
    grads = dict(zip(["w_down"] + mid, shared), **small_grads)
    gw_in_lat = _matmul(h1, dz_lat, mode="tn", out_dtype=F32, name="mm_gw_in_lat", tn=1024, tk=s)

    gq = gw_q.reshape(ql, heads, HEAD_W)
    gq_pe = gq[:, :, NOPE:NOPE + ROPE] + _quarter_turn_back(gq[:, :, NOPE + ROPE:])
    g_pe = gw_in_lat[:, ql + kvl:ql + kvl + ROPE] + _quarter_turn_back(gw_in_lat[:, ql + kvl + ROPE:])
    last = ["w_in", "w_uq", "w_ukv"]
    gp_last = [
        _split_cols(jnp.concatenate([gw_in_big[:, :o_q], gw_in_lat[:, :ql + kvl].astype(BF16), g_pe.astype(BF16),
                                     gw_in_big[:, o_q:]], axis=1)),
        _split_cols(jnp.concatenate([gq[:, :, :NOPE], gq_pe], axis=2).reshape(ql, heads * (NOPE + ROPE)).astype(BF16)),
        _split_cols(gw_kv.reshape(kvl, 2, heads, NOPE).transpose(0, 2, 1, 3).reshape(kvl, heads * 2 * NOPE)),
    ]
    (grad_x, dshift1, dscale1, g_pre1_grad), r1_last = _prenorm_bwd(x2d, dh1, dx1, g_pre1, scale1, "prenorm1_bwd",
                                                                    comm=_swap_comm(gp_last))
    s1_last = add_sibling(last, gp_last, r1_last)

    dmod = jnp.concatenate([dshift1, dscale1, dgate1, dshift2, dscale2, dgate2, g_pre1_grad], axis=1)
    dmod_all = _all_gather(jnp.pad(dmod, ((0, SUBLANES - 1), (0, 0))), "gather_dmod")
    dmod_all = dmod_all.reshape(N_DEV, SUBLANES, (N_MOD + 1) * d)[:, 0]
    dmod_sum = _sum_leading(dmod_all.reshape(N_DEV, 1, (N_MOD + 1) * d), "sum_dmod")[0]
    grads["b_ada"], grads["pre_norm1_g"] = dmod_sum[:N_MOD * d], dmod_sum[N_MOD * d:]
    dmod_mine = lax.dynamic_slice(dmod_all, (0, chip * na), (N_DEV, na))
    grads["w_ada"] = _ada_bwd(c_all.T, dmod_mine, "ada_bwd")

    delta, new_m, new_v = {}, {}, {}

    def adamw(n, after=None):
        turn = (lambda a: a.T) if n == "w_in" else (lambda a: a)
        g_t = turn(grads[n])
        outs = _adamw(turn(given[n]), g_t, turn(given["m_" + n]), turn(given["v_" + n]), "adamw_" + n, after=after)
        grads[n] = turn(g_t)
        delta[n], new_m[n], new_v[n] = (turn(o) for o in outs)

    exchange_last = _exchange_comm(s1_last)
    in_flight, token = _comm_split_start(exchange_last, "rs_exchange_last_start", after=[dmod_sum, small_sum])
    for n in ["w_ada", "w_down"] + mid:
        adamw(n, after=token)
    s1_last, r2_last = _comm_split_wait(exchange_last, in_flight, delta[mid[-1]], "rs_exchange_last_wait")
    half_last = add_chips(last, s1_last, r2_last)
    grads.update(zip(last, _run_comm(_share_comm(half_last), "rs_share_last")))
    for n in last:
        adamw(n)

    def small_pack(prefix, source):
        v = jnp.concatenate([source[prefix + n].reshape(-1) for n in SMALL])
        rows = -(-v.shape[0] // (LANES * SUBLANES)) * SUBLANES
        return jnp.pad(v, (0, rows * LANES - v.shape[0])).reshape(rows, LANES)

    outs = _adamw(small_pack("", given), small_pack("", grads), small_pack("m_", given), small_pack("v_", given),
                  "adamw_small")
    off = 0
    for n in SMALL:
        size = given[n].size
        for store, packed_out in zip((delta, new_m, new_v), outs):
            store[n] = packed_out.reshape(-1)[off:off + size].reshape(given[n].shape)
        off += size

    return (loss, grad_x[None], *[grads[n] for n in WEIGHTS], *[delta[n] for n in WEIGHTS],
            *[new_m[n] for n in WEIGHTS], *[new_v[n] for n in WEIGHTS])
```

```python
import functools

import jax
import jax.numpy as jnp
from jax import lax
from jax.experimental import pallas as pl
from jax.experimental.pallas import tpu as pltpu

F32 = jnp.float32
BF16 = jnp.bfloat16
MESH = pl.DeviceIdType.MESH
HBM = pltpu.HBM

EPS = 1e-6
NOPE, ROPE, VHEAD = 128, 64, 128
HEAD_W = NOPE + 2 * ROPE
ROPE_THETA = 10000.0
CONV_TAPS = 3
N_MOD = 6
N_CHIPS, N_CORES, N_DEV = 4, 2, 8
ADAM_LR, ADAM_B1, ADAM_B2, ADAM_EPS, ADAM_WD, ADAM_STEP = 0.001, 0.9, 0.999, 1e-08, 0.01, 10

LANES = 128
SUBLANES = 8
VMEM_LIMIT = 56 * 2**20
MIDDLE_STAGE_AT = 70
SMALL_ROW_TILE = 256

BIG = ("w_in", "w_branch_a", "w_uq", "w_ukv", "w_branch_b", "w_out", "w_up", "w_down")
WEIGHTS = ("w_ada", "b_ada", "pre_norm1_g", "w_in", "gm_ln_g", "gm_ln_b", "gm_w_s", "gm_b_s", "w_branch_a",
           "q_norm_g", "w_uq", "kv_norm_g", "w_ukv", "w_branch_b", "w_out", "post_norm1_g", "pre_norm2_g",
           "w_up", "conv_w", "conv_b", "w_down", "post_norm2_g")
SMALL_PARTIAL = ("gm_ln_g", "gm_ln_b", "gm_w_s", "gm_b_s", "q_norm_g", "kv_norm_g", "post_norm1_g",
                 "pre_norm2_g", "conv_w", "conv_b", "post_norm2_g")
SMALL = ("b_ada", "pre_norm1_g") + SMALL_PARTIAL


def _div_tile(n, cap, mult=LANES):
    t = (min(cap, n) // mult) * mult
    while t >= mult:
        if n % t == 0:
            return t
        t -= mult
    return n


def _params(**kw):
    return pltpu.CompilerParams(vmem_limit_bytes=VMEM_LIMIT, **kw)


def _row_spec(width):
    return pl.BlockSpec((1, width), lambda *_: (0, 0))


def _gelu(x):
    k = 0.7978845608028654
    return 0.5 * x * (1.0 + jnp.tanh(k * (x + 0.044715 * x * x * x)))


def _gelu_grad(x):
    k = 0.7978845608028654
    t = jnp.tanh(k * (x + 0.044715 * x * x * x))
    return 0.5 * (1.0 + t) + 0.5 * x * (1.0 - t * t) * k * (1.0 + 3.0 * 0.044715 * x * x)


def _sigmoid(x):
    return 0.5 * jnp.tanh(0.5 * x) + 0.5


def _dot(a, b, dims):
    return lax.dot_general(a, b, (dims, ((), ())), preferred_element_type=F32)


NN = ((1,), (0,))
NT = ((1,), (1,))
TN = ((0,), (0,))


def _logical(arr):
    if arr.ndim == 2:
        return arr.shape[0], arr.shape[1], arr.shape[1]
    return arr.shape[1], arr.shape[0] * arr.shape[2], arr.shape[2]


def _tile_spec(ndim, group_w, blk_rows, blk_cols, row_of, col_of):
    if ndim == 2:
        return pl.BlockSpec((blk_rows, blk_cols), lambda i, j, k: (row_of(i, j, k), col_of(i, j, k)))
    per = group_w // blk_cols
    return pl.BlockSpec((None, blk_rows, blk_cols),
                        lambda i, j, k: (col_of(i, j, k) // per, row_of(i, j, k), col_of(i, j, k) % per))


def _matmul(a, b, *, mode, out_dtype, name, tm=512, tn=512, tk=2048, mul=None, add=None, out_groups=None, comm=None):
    ar, ac, agw = _logical(a)
    br, bc, bgw = _logical(b)
    if mode == "nn":
        m, kd, n = ar, ac, bc
        m_w, k_w, n_w = (), (agw,), (bgw,)
    elif mode == "nt":
        m, kd, n = ar, ac, br
        m_w, k_w, n_w = (), (agw, bgw), ()
    else:
        m, kd, n = ac, ar, bc
        m_w, k_w, n_w = (agw,), (), (bgw,)
    if out_groups is not None:
        n_w = n_w + (n // out_groups,)
    tm = _div_tile(min((m,) + m_w), tm, LANES if mode == "tn" else SUBLANES)
    tn = _div_tile(min((n,) + n_w), tn)
    tk = _div_tile(min((kd,) + k_w), tk)
    assert all(w % tn == 0 for w in n_w) and all(w % tk == 0 for w in k_w) and all(w % tm == 0 for w in m_w)
    nk = kd // tk
    dims = {"nn": NN, "nt": NT, "tn": TN}[mode]
    gi, gj, gk = (lambda i, j, k: i), (lambda i, j, k: j), (lambda i, j, k: k)
    if mode == "nn":
        a_spec = _tile_spec(a.ndim, agw, tm, tk, gi, gk)
        b_spec = _tile_spec(b.ndim, bgw, tk, tn, gk, gj)
    elif mode == "nt":
        a_spec = _tile_spec(a.ndim, agw, tm, tk, gi, gk)
        b_spec = _tile_spec(b.ndim, bgw, tn, tk, gj, gk)
    else:
        a_spec = _tile_spec(a.ndim, agw, tk, tm, gk, gi)
        b_spec = _tile_spec(b.ndim, bgw, tk, tn, gk, gj)
    in_specs, operands = [a_spec, b_spec], [a, b]
    if mul is not None:
        assert mul.shape == (m, tn)
        in_specs.append(pl.BlockSpec((tm, tn), lambda i, j, k: (i, 0)))
        operands.append(mul)
    if add is not None:
        in_specs.append(pl.BlockSpec((tm, tn), lambda i, j, k: (i, j)))
        operands.append(add)

    def body(*refs):
        a_ref, b_ref = refs[0], refs[1]
        pos = 2
        mul_ref = add_ref = None
        if mul is not None:
            mul_ref, pos = refs[pos], pos + 1
        if add is not None:
            add_ref, pos = refs[pos], pos + 1
        o_ref = refs[pos]

        def finish(r):
            if mul_ref is not None:
                r = r * mul_ref[...]
            if add_ref is not None:
                r = r + add_ref[...]
            o_ref[...] = r.astype(out_dtype)

        part = _dot(a_ref[...], b_ref[...], dims)
        if nk == 1:
            finish(part)
        else:
            acc_ref = refs[pos + 1]
            k = pl.program_id(2)

            @pl.when(k == 0)
            def _():
                acc_ref[...] = part

            @pl.when(k > 0)
            def _():
                acc_ref[...] += part

            @pl.when(k == nk - 1)
            def _():
                finish(acc_ref[...])

    if out_groups is None:
        out_spec, out_dims = _tile_spec(2, n, tm, tn, gi, gj), (m, n)
    else:
        out_spec, out_dims = _tile_spec(3, n // out_groups, tm, tn, gi, gj), (out_groups, m, n // out_groups)
    return _call(body, operands, comm, name=name, grid=(m // tm, n // tn, nk), in_specs=in_specs, out_specs=out_spec,
                 out_shape=jax.ShapeDtypeStruct(out_dims, out_dtype),
                 scratch_shapes=[] if nk == 1 else [pltpu.VMEM((tm, tn), F32)])


def _accumulate(ref, value):
    @pl.when(pl.program_id(0) == 0)
    def _():
        ref[...] = value

    @pl.when(pl.program_id(0) > 0)
    def _():
        ref[...] += value


def _colsum(v):
    return jnp.sum(v, axis=0, keepdims=True)


def _rowmean(v):
    return jnp.mean(v, axis=-1, keepdims=True)


def _prenorm(x, g, scale, shift, name):
    s, d = x.shape
    tb = _div_tile(s, 256, SUBLANES)

    def body(x_ref, g_ref, sc_ref, sh_ref, h_ref):
        xv = x_ref[...]
        r = lax.rsqrt(_rowmean(xv * xv) + EPS)
        h_ref[...] = ((xv * r) * g_ref[...] * (1.0 + sc_ref[...]) + sh_ref[...]).astype(BF16)

    blk = pl.BlockSpec((tb, d), lambda i: (i, 0))
    return pl.pallas_call(
        body, name=name, grid=(s // tb,), in_specs=[blk, _row_spec(d), _row_spec(d), _row_spec(d)],
        out_specs=blk, out_shape=jax.ShapeDtypeStruct((s, d), BF16), compiler_params=_params(),
    )(x, g, scale, shift)


def _post_pre(x, y, gate, pg, g2, scale2, shift2, name):
    s, d = x.shape
    tb = _div_tile(s, 256, SUBLANES)

    def body(x_ref, y_ref, gate_ref, pg_ref, g2_ref, sc_ref, sh_ref, x1_ref, h2_ref):
        yv = y_ref[...]
        rp = lax.rsqrt(_rowmean(yv * yv) + EPS)
        x1 = x_ref[...] + gate_ref[...] * ((yv * rp) * pg_ref[...])
        x1_ref[...] = x1
        r2 = lax.rsqrt(_rowmean(x1 * x1) + EPS)
        h2_ref[...] = ((x1 * r2) * g2_ref[...] * (1.0 + sc_ref[...]) + sh_ref[...]).astype(BF16)

    blk = pl.BlockSpec((tb, d), lambda i: (i, 0))
    return pl.pallas_call(
        body, name=name, grid=(s // tb,), in_specs=[blk, blk] + [_row_spec(d)] * 5,
        out_specs=[blk, blk],
        out_shape=[jax.ShapeDtypeStruct((s, d), F32), jax.ShapeDtypeStruct((s, d), BF16)],
        compiler_params=_params(),
    )(x, y, gate, pg, g2, scale2, shift2)


def _post_bwd(y, gate, pg, name, *, dxo=None, xin=None, target=None):
    s, d = y.shape
    tb = _div_tile(s, 256, SUBLANES)
    from_loss = target is not None

    def body(*refs):
        if from_loss:
            y_ref, gate_ref, pg_ref, xin_ref, t_ref, dy_ref, dgate_ref, dpg_ref, dxo_ref, loss_ref = refs
        else:
            y_ref, gate_ref, pg_ref, dxo_in_ref, dy_ref, dgate_ref, dpg_ref = refs
        yv = y_ref[...]
        rp = lax.rsqrt(_rowmean(yv * yv) + EPS)
        yh = yv * rp
        fn = yh * pg_ref[...]
        gate = gate_ref[...]
        if from_loss:
            err = xin_ref[...] + gate * fn - t_ref[...]
            dxo = err * (1.0 / d)
            dxo_ref[...] = dxo
            part = 0.5 * jnp.sum(_rowmean(err * err), axis=0, keepdims=True)
            _accumulate(loss_ref, jnp.broadcast_to(part, loss_ref.shape))
        else:
            dxo = dxo_in_ref[...]
        _accumulate(dgate_ref, _colsum(dxo * fn))
        dfn = dxo * gate
        _accumulate(dpg_ref, _colsum(dfn * yh))
        dyh = dfn * pg_ref[...]
        dy_ref[...] = (rp * (dyh - yh * _rowmean(dyh * yh))).astype(BF16)

    blk = pl.BlockSpec((tb, d), lambda i: (i, 0))
    in_specs = [blk, _row_spec(d), _row_spec(d)]
    out_specs = [blk, _row_spec(d), _row_spec(d)]
    out_shape = [jax.ShapeDtypeStruct((s, d), BF16), jax.ShapeDtypeStruct((1, d), F32),
                 jax.ShapeDtypeStruct((1, d), F32)]
    if from_loss:
        operands = (y, gate, pg, xin, target)
        in_specs += [blk, blk]
        out_specs += [blk, _row_spec(LANES)]
        out_shape += [jax.ShapeDtypeStruct((s, d), F32), jax.ShapeDtypeStruct((1, LANES), F32)]
    else:
        operands = (y, gate, pg, dxo)
        in_specs += [blk]
    return pl.pallas_call(
        body, name=name, grid=(s // tb,), in_specs=in_specs, out_specs=out_specs, out_shape=out_shape,
        compiler_params=_params(),
    )(*operands)


def _prenorm_bwd(xin, dh, dres, g, scale, name, comm=None):
    s, d = xin.shape
    tb = _div_tile(s, 256, SUBLANES)

    def body(x_ref, dh_ref, dres_ref, g_ref, sc_ref, dx_ref, dshift_ref, dscale_ref, dg_ref):
        xv = x_ref[...]
        r = lax.rsqrt(_rowmean(xv * xv) + EPS)
        xn = xv * r
        dh = dh_ref[...]
        g1 = g_ref[...]
        s1 = 1.0 + sc_ref[...]
        _accumulate(dshift_ref, _colsum(dh))
        _accumulate(dscale_ref, _colsum(dh * xn * g1))
        _accumulate(dg_ref, _colsum(dh * xn * s1))
        dxn = dh * g1 * s1
        dx_ref[...] = dres_ref[...] + r * (dxn - xn * _rowmean(dxn * xn))

    blk = pl.BlockSpec((tb, d), lambda i: (i, 0))
    return _call(
        body, (xin, dh, dres, g, scale), comm, name=name, grid=(s // tb,),
        in_specs=[blk, blk, blk, _row_spec(d), _row_spec(d)],
        out_specs=[blk, _row_spec(d), _row_spec(d), _row_spec(d)],
        out_shape=[jax.ShapeDtypeStruct((s, d), F32)] + [jax.ShapeDtypeStruct((1, d), F32)] * 3)


def _merge(z_big, y_a, y_b, name):
    s, d = y_a.shape
    tb = _div_tile(s, 256, SUBLANES)

    def body(zg_ref, ya_ref, yb_ref, o_ref):
        o_ref[...] = (_sigmoid(zg_ref[:, :d]) * ya_ref[...] + _sigmoid(zg_ref[:, d:]) * yb_ref[...]).astype(BF16)

    blk = pl.BlockSpec((tb, d), lambda i: (i, 0))
    return pl.pallas_call(
        body, name=name, grid=(s // tb,), in_specs=[pl.BlockSpec((tb, 2 * d), lambda i: (i, 1)), blk, blk],
        out_specs=blk, out_shape=jax.ShapeDtypeStruct((s, d), BF16), compiler_params=_params(),
    )(z_big, y_a, y_b)


def _merge_bwd(dmerged, z_big, y_a, y_b, name):
    s, d = y_a.shape
    tb = _div_tile(s, 256, SUBLANES)

    def body(dm_ref, zg_ref, ya_ref, yb_ref, dya_ref, dyb_ref, dz_ref):
        dm = dm_ref[...]
        sa, sb = _sigmoid(zg_ref[:, :d]), _sigmoid(zg_ref[:, d:])
        dya_ref[...] = (dm * sa).astype(BF16)
        dyb_ref[...] = (dm * sb).astype(BF16)
        dz_ref[:, :d] = (dm * ya_ref[...] * sa * (1.0 - sa)).astype(BF16)
        dz_ref[:, d:] = (dm * yb_ref[...] * sb * (1.0 - sb)).astype(BF16)

    blk = pl.BlockSpec((tb, d), lambda i: (i, 0))
    wide = pl.BlockSpec((tb, 2 * d), lambda i: (i, 1))
    return pl.pallas_call(
        body, name=name, grid=(s // tb,), in_specs=[blk, wide, blk, blk], out_specs=[blk, blk, wide],
        out_shape=[jax.ShapeDtypeStruct((s, d), BF16), jax.ShapeDtypeStruct((s, d), BF16),
                   jax.ShapeDtypeStruct((s, 4 * d), BF16)],
        compiler_params=_params(),
    )(dmerged, z_big, y_a, y_b)


def _causal_mask(ch):
    q = lax.broadcasted_iota(jnp.int32, (ch, ch), 0)
    p = lax.broadcasted_iota(jnp.int32, (ch, ch), 1)
    return (p <= q).astype(F32)


def _gmlp_norm(zc, lng, lnb, gw):
    u_pre, v_pre = zc[:, :gw], zc[:, gw:]
    vg = _gelu(v_pre)
    mu = _rowmean(vg)
    cen = vg - mu
    rstd = lax.rsqrt(_rowmean(cen * cen) + EPS)
    vhat = cen * rstd
    return u_pre, v_pre, _gelu(u_pre), vhat, rstd, vhat * lng + lnb


def _gmlp_fwd(z_big, ln_g, ln_b, w_s, b_s_t, name):
    s = z_big.shape[0]
    groups, ch, _ = w_s.shape
    gw = ln_g.shape[1]
    gd = gw // groups

    def body(z_ref, lng_ref, lnb_ref, ws_ref, bt_ref, a_ref):
        _, _, u, _, _, vn = _gmlp_norm(z_ref[...], lng_ref[...], lnb_ref[...], gw)
        mask = _causal_mask(ch)
        for g in range(groups):
            cols = slice(g * gd, (g + 1) * gd)
            wm = (ws_ref[g] * mask).astype(BF16)
            mixed = _dot(wm, vn[:, cols].astype(BF16), NN) + bt_ref[:, g:g + 1]
            a_ref[:, cols] = (u[:, cols] * mixed).astype(BF16)

    return pl.pallas_call(
        body, name=name, grid=(s // ch,),
        in_specs=[pl.BlockSpec((ch, 2 * gw), lambda n: (n, 0)), _row_spec(gw), _row_spec(gw),
                  pl.BlockSpec((groups, ch, ch), lambda n: (0, 0, 0)), pl.BlockSpec((ch, groups), lambda n: (0, 0))],
        out_specs=pl.BlockSpec((ch, gw), lambda n: (n, 0)),
        out_shape=jax.ShapeDtypeStruct((s, gw), BF16), compiler_params=_params(),
    )(z_big, ln_g, ln_b, w_s, b_s_t)


def _gmlp_bwd(z_big, da, dz_big, ln_g, ln_b, w_s, b_s_t, name, comm=None):
    s = z_big.shape[0]
    groups, ch, _ = w_s.shape
    gw = ln_g.shape[1]
    gd = gw // groups

    def body(z_ref, da_ref, dzin_ref, lng_ref, lnb_ref, ws_ref, bt_ref, dz_ref, gws_ref, gbt_ref, glng_ref, glnb_ref):
        del dzin_ref
        lng = lng_ref[...]
        u_pre, v_pre, u, vhat, rstd, vn = _gmlp_norm(z_ref[...], lng, lnb_ref[...], gw)
        da = da_ref[...]
        mask = _causal_mask(ch)
        first = pl.program_id(0) == 0
        dvn_parts = []
        lane = lax.broadcasted_iota(jnp.int32, (ch, LANES), 1)
        gb = jnp.zeros((ch, LANES), F32)
        for g in range(groups):
            cols = slice(g * gd, (g + 1) * gd)
            wm = (ws_ref[g] * mask).astype(BF16)
            vn_g = vn[:, cols].astype(BF16)
            mixed = _dot(wm, vn_g, NN) + bt_ref[:, g:g + 1]
            dz_ref[:, cols] = (da[:, cols] * mixed * _gelu_grad(u_pre[:, cols])).astype(BF16)
            dmixed = da[:, cols] * u[:, cols]
            dm16 = dmixed.astype(BF16)
            dvn_parts.append(_dot(wm, dm16, TN))
            gws = _dot(dm16, vn_g, NT) * mask

            @pl.when(first)
            def _(g=g, gws=gws):
                gws_ref[g] = gws

            @pl.when(jnp.logical_not(first))
            def _(g=g, gws=gws):
                gws_ref[g] += gws

            gb = gb + jnp.where(lane == g, jnp.sum(dmixed, axis=1, keepdims=True), 0.0)
        _accumulate(gbt_ref, gb)
        dvn = jnp.concatenate(dvn_parts, axis=1)
        _accumulate(glnb_ref, _colsum(dvn))
        _accumulate(glng_ref, _colsum(dvn * vhat))
        dvh = dvn * lng
        dvg = rstd * (dvh - _rowmean(dvh) - vhat * _rowmean(dvh * vhat))
        dz_ref[:, gw:] = (dvg * _gelu_grad(v_pre)).astype(BF16)

    zspec = pl.BlockSpec((ch, 2 * gw), lambda n: (n, 0))
    return _call(
        body, (z_big, da, dz_big, ln_g, ln_b, w_s, b_s_t), comm, name=name, grid=(s // ch,),
        in_specs=[zspec, pl.BlockSpec((ch, gw), lambda n: (n, 0)), pl.BlockSpec(memory_space=HBM),
                  _row_spec(gw), _row_spec(gw), pl.BlockSpec((groups, ch, ch), lambda n: (0, 0, 0)),
                  pl.BlockSpec((ch, groups), lambda n: (0, 0))],
        out_specs=[zspec, pl.BlockSpec((groups, ch, ch), lambda n: (0, 0, 0)),
                   pl.BlockSpec((ch, LANES), lambda n: (0, 0)), _row_spec(gw), _row_spec(gw)],
        out_shape=[jax.ShapeDtypeStruct(dz_big.shape, BF16), jax.ShapeDtypeStruct((groups, ch, ch), F32),
                   jax.ShapeDtypeStruct((ch, LANES), F32), jax.ShapeDtypeStruct((1, gw), F32),
                   jax.ShapeDtypeStruct((1, gw), F32)],
        input_output_aliases={2: 0})


def _mla_prep(z_lat, q_g, kv_g, rope_k, name):
    s, latw = z_lat.shape
    ql, kvl = q_g.shape[1], kv_g.shape[1]
    tb = _div_tile(s, 256, SUBLANES)

    def body(z_ref, qg_ref, kvg_ref, t_ref, qn_ref, kvn_ref, kr_ref):
        q = z_ref[:, :ql]
        qn_ref[...] = ((q * lax.rsqrt(_rowmean(q * q) + EPS)) * qg_ref[...]).astype(BF16)
        kv = z_ref[:, ql:ql + kvl]
        kvn_ref[...] = ((kv * lax.rsqrt(_rowmean(kv * kv) + EPS)) * kvg_ref[...]).astype(BF16)
        kk = z_ref[:, ql + kvl:] * t_ref[...]
        kr_ref[...] = (kk + pltpu.roll(kk, ROPE, axis=1)).astype(BF16)

    return pl.pallas_call(
        body, name=name, grid=(s // tb,),
        in_specs=[pl.BlockSpec((tb, latw), lambda i: (i, 0)), _row_spec(ql), _row_spec(kvl),
                  pl.BlockSpec((tb, 2 * ROPE), lambda i: (i, 0))],
        out_specs=[pl.BlockSpec((tb, ql), lambda i: (i, 0)), pl.BlockSpec((tb, kvl), lambda i: (i, 0)),
                   pl.BlockSpec((tb, 2 * ROPE), lambda i: (i, 0))],
        out_shape=[jax.ShapeDtypeStruct((s, ql), BF16), jax.ShapeDtypeStruct((s, kvl), BF16),
                   jax.ShapeDtypeStruct((s, 2 * ROPE), BF16)],
        compiler_params=_params(),
    )(z_lat, q_g, kv_g, rope_k)


def _scores(q, k, kr, on_diagonal):
    s = _dot(q[:, :NOPE], k, NT) + _dot(q[:, NOPE:], kr, NT)
    if not on_diagonal:
        return s
    rows = lax.broadcasted_iota(jnp.int32, s.shape, 0)
    cols = lax.broadcasted_iota(jnp.int32, s.shape, 1)
    return jnp.where(cols <= rows, s, -1e30)


def _attn_fwd(q, kv, kr, heads, name, comm=None):
    s = q.shape[0]
    t = _div_tile(s, 512)
    nb = s // t
    hp = 2 if heads % 2 == 0 else 1

    def body(q_ref, k_ref, kr_ref, v_ref, o_ref, lse_ref, m_ref, l_ref, acc_ref):
        i, j = pl.program_id(1), pl.program_id(2)

        @pl.when(j == 0)
        def _():
            m_ref[...] = jnp.full(m_ref.shape, -1e30, F32)
            l_ref[...] = jnp.zeros(l_ref.shape, F32)
            acc_ref[...] = jnp.zeros(acc_ref.shape, F32)

        def step(on_diagonal):
            krv = kr_ref[...]
            for h in range(hp):
                vc = slice(h * VHEAD, (h + 1) * VHEAD)
                sc = _scores(q_ref[:, h * HEAD_W:(h + 1) * HEAD_W], k_ref[:, h * NOPE:(h + 1) * NOPE], krv, on_diagonal)
                m_old = m_ref[h]
                m_new = jnp.maximum(m_old, jnp.max(sc, axis=-1, keepdims=True))
                p = jnp.exp(sc - m_new)
                alpha = jnp.exp(m_old - m_new)
                l_new = alpha * l_ref[h] + jnp.sum(p, axis=-1, keepdims=True)
                acc = alpha * acc_ref[:, vc] + _dot(p.astype(BF16), v_ref[:, vc], NN)
                if on_diagonal:
                    o_ref[:, vc] = (acc / l_new).astype(BF16)
                    lse_ref[h] = jnp.broadcast_to(m_new + jnp.log(l_new), (t, LANES))
                else:
                    m_ref[h], l_ref[h], acc_ref[:, vc] = m_new, l_new, acc

        pl.when(j < i)(lambda: step(False))
        pl.when(j == i)(lambda: step(True))

    kidx = lambda off: (lambda h, i, j: (jnp.minimum(i, j), off(h)))
    return _call(
        body, (q, kv, kr, kv), comm, name=name, grid=(heads // hp, nb, nb),
        in_specs=[pl.BlockSpec((t, hp * HEAD_W), lambda h, i, j: (i, h)),
                  pl.BlockSpec((t, hp * NOPE), kidx(lambda h: h)),
                  pl.BlockSpec((t, 2 * ROPE), kidx(lambda h: 0)),
                  pl.BlockSpec((t, hp * VHEAD), kidx(lambda h: heads // hp + h))],
        out_specs=[pl.BlockSpec((t, hp * VHEAD), lambda h, i, j: (i, h)),
                   pl.BlockSpec((hp, t, LANES), lambda h, i, j: (h, i, 0))],
        out_shape=[jax.ShapeDtypeStruct((s, heads * VHEAD), BF16), jax.ShapeDtypeStruct((heads, s, LANES), F32)],
        scratch_shapes=[pltpu.VMEM((hp, t, 1), F32), pltpu.VMEM((hp, t, 1), F32), pltpu.VMEM((t, hp * VHEAD), F32)])


def _attn_bwd(q, kv, kr, o, do, lse, heads, name, comm=None):
    s = q.shape[0]
    t = _div_tile(s, 512)
    nb = s // t
    hp = 2 if heads % 2 == 0 else 1

    def body(q_ref, k_ref, kr_ref, v_ref, o_ref, do_ref, lse_ref, dq_ref, dk_ref, dv_ref, dk_acc, dv_acc):
        j, i = pl.program_id(1), pl.program_id(2)

        @pl.when(jnp.logical_and(j == 0, i == 0))
        def _():
            dq_ref[...] = jnp.zeros(dq_ref.shape, F32)

        def step(on_diagonal):
            krv = kr_ref[...]
            rows = pl.ds(pl.multiple_of(i * t, t), t)
            for h in range(hp):
                qc, kc, vc = (slice(h * w, (h + 1) * w) for w in (HEAD_W, NOPE, VHEAD))
                qv, kn, do_v = q_ref[:, qc], k_ref[:, kc], do_ref[:, vc]
                p = jnp.exp(_scores(qv, kn, krv, on_diagonal) - lse_ref[h][:, :1])
                dp = _dot(do_v, v_ref[:, vc], NT)
                delta = jnp.sum(do_v.astype(F32) * o_ref[:, vc].astype(F32), axis=-1, keepdims=True)
                ds = (p * (dp - delta)).astype(BF16)
                dq_ref[rows, h * HEAD_W:h * HEAD_W + NOPE] += _dot(ds, kn, NN)
                dq_ref[rows, h * HEAD_W + NOPE:(h + 1) * HEAD_W] += _dot(ds, krv, NN)
                dv_part, dk_part = _dot(p.astype(BF16), do_v, TN), _dot(ds, qv, TN)
                if on_diagonal:
                    dv_acc[:, vc], dk_acc[:, qc] = dv_part, dk_part
                else:
                    dv_acc[:, vc] += dv_part
                    dk_acc[:, qc] += dk_part

        pl.when(i == j)(lambda: step(True))
        pl.when(i > j)(lambda: step(False))

        @pl.when(i == nb - 1)
        def _():
            dk_ref[...] = dk_acc[...].astype(BF16)
            dv_ref[...] = dv_acc[...].astype(BF16)

    qidx = lambda h, j, i: (jnp.maximum(i, j), h)
    return _call(
        body, (q, kv, kr, kv, o, do, lse), comm, name=name, grid=(heads // hp, nb, nb),
        in_specs=[pl.BlockSpec((t, hp * HEAD_W), qidx),
                  pl.BlockSpec((t, hp * NOPE), lambda h, j, i: (j, h)),
                  pl.BlockSpec((t, 2 * ROPE), lambda h, j, i: (j, 0)),
                  pl.BlockSpec((t, hp * VHEAD), lambda h, j, i: (j, heads // hp + h)),
                  pl.BlockSpec((t, hp * VHEAD), qidx), pl.BlockSpec((t, hp * VHEAD), qidx),
                  pl.BlockSpec((hp, t, LANES), lambda h, j, i: (h, jnp.maximum(i, j), 0))],
        out_specs=[pl.BlockSpec((s, hp * HEAD_W), lambda h, j, i: (0, h)),
                   pl.BlockSpec((t, hp * HEAD_W), lambda h, j, i: (j, h)),
                   pl.BlockSpec((t, hp * VHEAD), lambda h, j, i: (j, h))],
        out_shape=[jax.ShapeDtypeStruct((s, heads * HEAD_W), F32), jax.ShapeDtypeStruct((s, heads * HEAD_W), BF16),
                   jax.ShapeDtypeStruct((s, heads * VHEAD), BF16)],
        scratch_shapes=[pltpu.VMEM((t, hp * HEAD_W), F32), pltpu.VMEM((t, hp * VHEAD), F32)])


def _mla_bwd_mid(dq, dk, dv, rope_q, rope_k, heads, name):
    s = dq.shape[0]
    tb = _div_tile(s, 256, SUBLANES)

    def body(dq_ref, dk_ref, dv_ref, tq_ref, tk_ref, dqb_ref, dkv_ref, dkk_ref):
        tq = tq_ref[...]
        dkr = jnp.zeros((tb, 2 * ROPE), F32)
        for h in range(heads):
            cols = slice(h * HEAD_W, (h + 1) * HEAD_W)
            dqb_ref[:, cols] = (dq_ref[:, cols] * tq).astype(BF16)
            dkv_ref[:, h * NOPE:(h + 1) * NOPE] = dk_ref[:, h * HEAD_W:h * HEAD_W + NOPE]
            dkr = dkr + dk_ref[:, h * HEAD_W + NOPE:(h + 1) * HEAD_W].astype(F32)
        dkv_ref[:, heads * NOPE:] = dv_ref[...]
        dkk_ref[...] = (dkr + pltpu.roll(dkr, ROPE, axis=1)) * tk_ref[...]

    wq, wv = heads * HEAD_W, heads * VHEAD
    return pl.pallas_call(
        body, name=name, grid=(s // tb,),
        in_specs=[pl.BlockSpec((tb, wq), lambda i: (i, 0)), pl.BlockSpec((tb, wq), lambda i: (i, 0)),
                  pl.BlockSpec((tb, wv), lambda i: (i, 0)), pl.BlockSpec((tb, HEAD_W), lambda i: (i, 0)),
                  pl.BlockSpec((tb, 2 * ROPE), lambda i: (i, 0))],
        out_specs=[pl.BlockSpec((tb, wq), lambda i: (i, 0)), pl.BlockSpec((tb, heads * NOPE + wv), lambda i: (i, 0)),
                   pl.BlockSpec((tb, 2 * ROPE), lambda i: (i, 0))],
        out_shape=[jax.ShapeDtypeStruct((s, wq), BF16), jax.ShapeDtypeStruct((s, heads * NOPE + wv), BF16),
                   jax.ShapeDtypeStruct((s, 2 * ROPE), F32)],
        compiler_params=_params(),
    )(dq, dk, dv, rope_q, rope_k)


def _mla_bwd_post(z_lat, dqn, dkvn, dkk, q_g, kv_g, name):
    s, latw = z_lat.shape
    ql, kvl = q_g.shape[1], kv_g.shape[1]
    tb = _div_tile(s, 256, SUBLANES)

    def norm_bwd(xv, dn, g, dg_ref):
        r = lax.rsqrt(_rowmean(xv * xv) + EPS)
        xh = xv * r
        _accumulate(dg_ref, _colsum(dn * xh))
        dxh = dn * g
        return r * (dxh - xh * _rowmean(dxh * xh))

    def body(z_ref, dqn_ref, dkvn_ref, dkk_ref, qg_ref, kvg_ref, dz_ref, gq_ref, gkv_ref):
        dz_ref[:, :ql] = norm_bwd(z_ref[:, :ql], dqn_ref[...], qg_ref[...], gq_ref).astype(BF16)
        dz_ref[:, ql:ql + kvl] = norm_bwd(z_ref[:, ql:ql + kvl], dkvn_ref[...], kvg_ref[...], gkv_ref).astype(BF16)
        dz_ref[:, ql + kvl:] = dkk_ref[...].astype(BF16)

    return pl.pallas_call(
        body, name=name, grid=(s // tb,),
        in_specs=[pl.BlockSpec((tb, latw), lambda i: (i, 0)), pl.BlockSpec((tb, ql), lambda i: (i, 0)),
                  pl.BlockSpec((tb, kvl), lambda i: (i, 0)), pl.BlockSpec((tb, 2 * ROPE), lambda i: (i, 0)),
                  _row_spec(ql), _row_spec(kvl)],
        out_specs=[pl.BlockSpec((tb, latw), lambda i: (i, 0)), _row_spec(ql), _row_spec(kvl)],
        out_shape=[jax.ShapeDtypeStruct((s, latw), BF16), jax.ShapeDtypeStruct((1, ql), F32),
                   jax.ShapeDtypeStruct((1, kvl), F32)],
        compiler_params=_params(),
    )(z_lat, dqn, dkvn, dkk, q_g, kv_g)


def _shift_down(x, n):
    rows = lax.broadcasted_iota(jnp.int32, x.shape, 0)
    return jnp.where(rows >= n, pltpu.roll(x, n, axis=0), 0.0)


def _shift_up(x, n):
    s = x.shape[0]
    rows = lax.broadcasted_iota(jnp.int32, x.shape, 0)
    return jnp.where(rows < s - n, pltpu.roll(x, s - n, axis=0), 0.0)


def _conv(pre, w_ref, b_ref):
    return (w_ref[2:3, :] * pre + w_ref[1:2, :] * _shift_down(pre, 1) + w_ref[0:1, :] * _shift_down(pre, 2)
            + b_ref[...])


def _conv_fwd(up_pre, conv_w, conv_b, name):
    s, ff2 = up_pre.shape
    ff = ff2 // 2
    tc = _div_tile(ff, 256)
    nb = ff // tc

    def body(pg_ref, pv_ref, wg_ref, wv_ref, bg_ref, bv_ref, act_ref):
        gate = _conv(pg_ref[...].astype(F32), wg_ref, bg_ref)
        val = _conv(pv_ref[...].astype(F32), wv_ref, bv_ref)
        act_ref[...] = (gate * _sigmoid(gate) * val).astype(BF16)

    def col(rows, off):
        return pl.BlockSpec((rows, tc), lambda j: (0, j + off))

    return pl.pallas_call(
        body, name=name, grid=(nb,),
        in_specs=[col(s, 0), col(s, nb), col(CONV_TAPS, 0), col(CONV_TAPS, nb), col(1, 0), col(1, nb)],
        out_specs=col(s, 0), out_shape=jax.ShapeDtypeStruct((s, ff), BF16), compiler_params=_params(),
    )(up_pre, up_pre, conv_w, conv_w, conv_b, conv_b)


def _conv_bwd(up_pre, dact, conv_w, conv_b, name, comm=None):
    s, ff2 = up_pre.shape
    ff = ff2 // 2
    tc = _div_tile(ff, 256)
    nb = ff // tc

    def half(pre, dx, w_ref, dpre_ref, gw_ref, gb_ref):
        gb_ref[...] = _colsum(dx)
        gw_ref[0:1, :] = _colsum(dx * _shift_down(pre, 2))
        gw_ref[1:2, :] = _colsum(dx * _shift_down(pre, 1))
        gw_ref[2:3, :] = _colsum(dx * pre)
        dpre_ref[...] = (w_ref[2:3, :] * dx + w_ref[1:2, :] * _shift_up(dx, 1)
                         + w_ref[0:1, :] * _shift_up(dx, 2)).astype(BF16)

    def body(pg_ref, pv_ref, da_ref, wg_ref, wv_ref, bg_ref, bv_ref, dup_ref, gwg_ref, gwv_ref, gbg_ref, gbv_ref):
        pre_g, pre_v = pg_ref[...].astype(F32), pv_ref[...].astype(F32)
        gate = _conv(pre_g, wg_ref, bg_ref)
        val = _conv(pre_v, wv_ref, bv_ref)
        da = da_ref[...].astype(F32)
        sg = _sigmoid(gate)
        half(pre_v, da * gate * sg, wv_ref, dup_ref.at[1], gwv_ref, gbv_ref)
        half(pre_g, da * val * sg * (1.0 + gate * (1.0 - sg)), wg_ref, dup_ref.at[0], gwg_ref, gbg_ref)

    def col(rows, off):
        return pl.BlockSpec((rows, tc), lambda j: (0, j + off))

    return _call(
        body, (up_pre, up_pre, dact, conv_w, conv_w, conv_b, conv_b), comm, name=name, grid=(nb,),
        in_specs=[col(s, 0), col(s, nb), col(s, 0), col(CONV_TAPS, 0), col(CONV_TAPS, nb), col(1, 0), col(1, nb)],
        out_specs=[pl.BlockSpec((2, s, tc), lambda j: (0, 0, j)), col(CONV_TAPS, 0), col(CONV_TAPS, 0),
                   col(1, 0), col(1, 0)],
        out_shape=[jax.ShapeDtypeStruct((2, s, ff), BF16)] + [jax.ShapeDtypeStruct((CONV_TAPS, ff), F32)] * 2
        + [jax.ShapeDtypeStruct((1, ff), F32)] * 2)


def _ada_fwd(c_all, w, b, name):
    nseq, d = c_all.shape
    na = w.shape[1]
    tn = _div_tile(na, 512)

    def body(c_ref, w_ref, b_ref, o_ref):
        cv = c_ref[...]
        sc = cv * _sigmoid(cv)
        o_ref[...] = jnp.dot(sc, w_ref[...], preferred_element_type=F32, precision=lax.Precision.HIGHEST) + b_ref[...]

    return pl.pallas_call(
        body, name=name, grid=(na // tn,),
        in_specs=[pl.BlockSpec((nseq, d), lambda j: (0, 0)), pl.BlockSpec((d, tn), lambda j: (0, j)),
                  pl.BlockSpec((1, tn), lambda j: (0, j))],
        out_specs=pl.BlockSpec((nseq, tn), lambda j: (0, j)),
        out_shape=jax.ShapeDtypeStruct((nseq, na), F32), compiler_params=_params(),
    )(c_all, w, b)


def _ada_bwd(c_all_t, dmod, name):
    d, nseq = c_all_t.shape
    na = dmod.shape[1]
    tm, tn = _div_tile(d, 256, SUBLANES), _div_tile(na, 512)

    def body(c_ref, dm_ref, o_ref):
        cv = c_ref[...]
        sc = cv * _sigmoid(cv)
        acc = sc[:, 0:1] * dm_ref[0:1, :]
        for bi in range(1, nseq):
            acc = acc + sc[:, bi:bi + 1] * dm_ref[bi:bi + 1, :]
        o_ref[...] = acc

    return pl.pallas_call(
        body, name=name, grid=(d // tm, na // tn),
        in_specs=[pl.BlockSpec((tm, nseq), lambda i, j: (i, 0)), pl.BlockSpec((nseq, tn), lambda i, j: (0, j))],
        out_specs=pl.BlockSpec((tm, tn), lambda i, j: (i, j)),
        out_shape=jax.ShapeDtypeStruct((d, na), F32), compiler_params=_params(),
    )(c_all_t, dmod)


def _adamw(w, g, m, v, name, comm=None, after=None):
    rows, cols = w.shape
    tb = _div_tile(rows, max(SUBLANES, (256 * 1024) // cols // SUBLANES * SUBLANES), SUBLANES)
    c1 = 1.0 / (1.0 - ADAM_B1 ** ADAM_STEP)
    c2 = 1.0 / (1.0 - ADAM_B2 ** ADAM_STEP)

    def body(*refs):
        w_ref, g_ref, m_ref, v_ref = refs[:4]
        d_ref, nm_ref, nv_ref = refs[-3:]
        gv = g_ref[...]
        nm = ADAM_B1 * m_ref[...] + (1.0 - ADAM_B1) * gv
        nv = ADAM_B2 * v_ref[...] + (1.0 - ADAM_B2) * (gv * gv)
        nm_ref[...] = nm
        nv_ref[...] = nv
        d_ref[...] = -ADAM_LR * ((nm * c1) / (jnp.sqrt(nv * c2) + ADAM_EPS) + ADAM_WD * w_ref[...])

    blk = pl.BlockSpec((tb, cols), lambda i: (i, 0))
    operands, in_specs = (w, g, m, v), [blk] * 4
    if after is not None:
        operands, in_specs = operands + (after,), in_specs + [pl.BlockSpec(after.shape, lambda i: (0, 0))]
    return _call(body, operands, comm, name=name, grid=(rows // tb,), in_specs=in_specs, out_specs=[blk] * 3,
                 out_shape=[jax.ShapeDtypeStruct((rows, cols), F32)] * 3)


def _sum_leading(parts, name):
    n, rows, cols = parts.shape
    tb = _div_tile(rows, 512, SUBLANES)

    def body(p_ref, o_ref):
        acc = p_ref[0]
        for k in range(1, n):
            acc = acc + p_ref[k]
        o_ref[...] = acc

    return pl.pallas_call(
        body, name=name, grid=(rows // tb,), in_specs=[pl.BlockSpec((n, tb, cols), lambda i: (0, i, 0))],
        out_specs=pl.BlockSpec((tb, cols), lambda i: (i, 0)),
        out_shape=jax.ShapeDtypeStruct((rows, cols), F32), compiler_params=_params(),
    )(parts)


def _place():
    x, y, c = lax.axis_index("x"), lax.axis_index("y"), lax.axis_index("c")
    return x, y, c, [(1 - x, y), (x, 1 - y), (1 - x, 1 - y)]


def _all_gather(block, name):
    m_per, n = block.shape

    def body(x_ref, out_ref, send_sems, recv_sems, local_sem):
        x, y, c, chips = _place()
        me, sibling = (x, y, c), (x, y, 1 - c)

        def rows(px, py, pc):
            return out_ref.at[pl.ds((4 * px + 2 * py + pc) * m_per, m_per), :]

        def copy(k, blk, to, src=None):
            return pltpu.make_async_remote_copy(
                src_ref=rows(*blk) if src is None else src, dst_ref=rows(*blk), send_sem=send_sems.at[k],
                recv_sem=recv_sems.at[k], device_id=to, device_id_type=MESH)

        mine = pltpu.make_async_copy(x_ref, rows(*me), local_sem)
        mine.start()
        first = [copy(0, me, sibling, src=x_ref)]
        first += [copy(1 + j, me, (*chip, c), src=x_ref) for j, chip in enumerate(chips)]
        for cp in first:
            cp.start()
        passed = [copy(4 + j, (*chip, c), sibling) for j, chip in enumerate(chips)]
        for j, chip in enumerate(chips):
            copy(1 + j, (*chip, c), me).wait_recv()
            passed[j].start()
        copy(0, sibling, me).wait_recv()
        for j, chip in enumerate(chips):
            copy(4 + j, (*chip, 1 - c), me).wait_recv()
        for cp in first + passed:
            cp.wait_send()
        mine.wait()

    return pl.pallas_call(
        body, name=name, out_shape=jax.ShapeDtypeStruct((N_DEV * m_per, n), block.dtype),
        in_specs=[pl.BlockSpec(memory_space=pltpu.VMEM)], out_specs=pl.BlockSpec(memory_space=pltpu.VMEM),
        scratch_shapes=[pltpu.SemaphoreType.DMA((7,)), pltpu.SemaphoreType.DMA((7,)), pltpu.SemaphoreType.DMA],
        compiler_params=_params(),
    )(block)


def _hbm_specs(n):
    return [pl.BlockSpec(memory_space=HBM)] * n


def _part(ref, by_cols, half, quarter=None, lead=None):
    extent = ref.shape[-1] if by_cols else ref.shape[-2]
    size = extent // 2 if quarter is None else extent // 4
    first = half * (extent // 2) + (0 if quarter is None else quarter * size)
    tile = LANES if by_cols else 2 * SUBLANES
    span = pl.ds(pl.multiple_of(first, tile) if size % tile == 0 else first, size)
    index = (slice(None), span) if by_cols else (span, slice(None))
    return ref.at[index] if lead is None else ref.at[(lead,) + index]


def _half_rows(ref, half, lead=None):
    return _part(ref, False, half, lead=lead)


class _Comm:
    def __init__(self, operands, out_shape, sem_dims, build, aliases=None):
        self.operands, self.out_shape, self.sem_dims = list(operands), list(out_shape), list(sem_dims)
        self.scratch = [pltpu.SemaphoreType.DMA(d) for d in sem_dims]
        self.build, self.aliases = build, dict(aliases or {})


class _SemGrid:
    def __init__(self, sems, dims):
        self.sems, self.dims, self.at = list(sems), tuple(dims), self

    def __getitem__(self, index):
        index = index if isinstance(index, tuple) else (index,)
        flat = 0
        for i, d in zip(index, self.dims):
            flat = flat * d + i
        return self.sems[flat]


def _call(body, operands, comm=None, *, name, grid, in_specs, out_specs, out_shape, scratch_shapes=(),
          input_output_aliases=None):
    aliases = dict(input_output_aliases or {})
    if comm is None:
        return pl.pallas_call(
            body, name=name, grid=grid, in_specs=in_specs, out_specs=out_specs, out_shape=out_shape,
            scratch_shapes=list(scratch_shapes), input_output_aliases=aliases, compiler_params=_params())(*operands)
    single = not isinstance(out_shape, (list, tuple))
    outs = [out_shape] if single else list(out_shape)
    ospecs = [out_specs] if single else list(out_specs)
    n_in, n_out, n_scr = len(operands), len(outs), len(scratch_shapes)
    c_in, c_out = len(comm.operands), len(comm.out_shape)
    for i, o in comm.aliases.items():
        aliases[n_in + i] = n_out + o

    def hosted(*refs):
        ins, c_ins = refs[:n_in], refs[n_in:n_in + c_in]
        o0 = n_in + c_in
        o_refs, c_outs = refs[o0:o0 + n_out], refs[o0 + n_out:o0 + n_out + c_out]
        s0 = o0 + n_out + c_out
        scr, sems = refs[s0:s0 + n_scr], refs[s0 + n_scr:]
        stages = comm.build(c_ins, c_outs, sems)
        step, n_steps = 0, 1
        for dim, size in enumerate(grid):
            step, n_steps = step * size + pl.program_id(dim), n_steps * size
        pl.when(step == 0)(stages[0])
        body(*ins, *o_refs, *scr)
        for stage in stages[1:-1]:
            pl.when(step == (n_steps * MIDDLE_STAGE_AT) // 100)(stage)
        pl.when(step == n_steps - 1)(stages[-1])

    res = pl.pallas_call(
        hosted, name=name, grid=grid, in_specs=list(in_specs) + _hbm_specs(c_in),
        out_specs=ospecs + _hbm_specs(c_out), out_shape=outs + comm.out_shape,
        scratch_shapes=list(scratch_shapes) + comm.scratch, input_output_aliases=aliases,
        compiler_params=_params())(*operands, *comm.operands)
    return (res[0] if single else res[:n_out]), res[n_out:]


def _run_comm(comm, name):
    c_in, c_out = len(comm.operands), len(comm.out_shape)

    def body(*refs):
        for stage in comm.build(refs[:c_in], refs[c_in:c_in + c_out], refs[c_in + c_out:]):
            stage()

    return pl.pallas_call(
        body, name=name, in_specs=_hbm_specs(c_in), out_specs=_hbm_specs(c_out), out_shape=comm.out_shape,
        scratch_shapes=comm.scratch, input_output_aliases=comm.aliases, compiler_params=_params())(*comm.operands)


def _gather_comm(shards, by_cols=()):
    nw = len(shards)

    def build(in_refs, out_refs, sems):
        send_sems, recv_sems = sems
        x, y, c, chips = _place()
        me, sibling = (x, y, c), (x, y, 1 - c)
        across_x, across_y, diagonal = chips

        def copy(w, k, block, part, to, src=None):
            dst = _part(out_refs[w], w in by_cols, part[1], part[2] if part[0] else None, 2 * block[0] + block[1])
            return pltpu.make_async_remote_copy(
                src_ref=dst if src is None else src, dst_ref=dst, send_sem=send_sems.at[w, k],
                recv_sem=recv_sems.at[w, k], device_id=to, device_id_type=MESH)

        first = [copy(w, j, (x, y), (0, c), (*chip, c), src=_part(in_refs[w], w in by_cols, c))
                 for w in range(nw) for j, chip in enumerate((across_x, across_y))]
        passed = [[copy(w, 2, across_x, (1, c, 0), (*across_y, c)), copy(w, 3, across_y, (1, c, 1), (*across_x, c)),
                   copy(w, 4, across_x, (0, c), sibling), copy(w, 5, across_y, (0, c), sibling)] for w in range(nw)]
        last = [[copy(w, 6, diagonal, (1, c, 0), sibling), copy(w, 7, diagonal, (1, c, 1), sibling)]
                for w in range(nw)]

        def start():
            for cp in first:
                cp.start()

        def middle():
            for w in range(nw):
                copy(w, 0, across_x, (0, c), me).wait_recv()
                copy(w, 1, across_y, (0, c), me).wait_recv()
                for cp in passed[w]:
                    cp.start()

        def finish():
            for w in range(nw):
                copy(w, 2, diagonal, (1, c, 0), me).wait_recv()
                copy(w, 3, diagonal, (1, c, 1), me).wait_recv()
                for cp in last[w]:
                    cp.start()
            for w in range(nw):
                for k, block, part in ((4, across_x, (0, 1 - c)), (5, across_y, (0, 1 - c)),
                                       (6, diagonal, (1, 1 - c, 0)), (7, diagonal, (1, 1 - c, 1))):
                    copy(w, k, block, part, me).wait_recv()
            for cp in first + sum(passed, []) + sum(last, []):
                cp.wait_send()

        return start, middle, finish

    return _Comm(shards, [jax.ShapeDtypeStruct((N_CHIPS,) + w.shape, w.dtype) for w in shards],
                 [(nw, 8), (nw, 8)], build)


def _halved(shape, by_cols):
    return shape[:-1] + (shape[-1] // 2,) if by_cols else shape[:-2] + (shape[-2] // 2, shape[-1])


def _swap_comm(gs, by_cols=()):
    nw = len(gs)

    def build(in_refs, out_refs, sems):
        send_sems, recv_sems = sems
        x, y, c, _ = _place()
        cps = []
        for w in range(nw):
            cps.append(pltpu.make_async_remote_copy(
                src_ref=_part(in_refs[w], w in by_cols, 1 - c, lead=slice(None)), dst_ref=out_refs[w],
                send_sem=send_sems.at[w], recv_sem=recv_sems.at[w], device_id=(x, y, 1 - c), device_id_type=MESH))

        def start():
            for cp in cps:
                cp.start()

        def finish():
            for cp in cps:
                cp.wait()

        return start, finish

    return _Comm(gs, [jax.ShapeDtypeStruct(_halved(g.shape, w in by_cols), g.dtype) for w, g in enumerate(gs)],
                 [(nw,), (nw,)], build)


def _exchange_comm(s1s):
    nw = len(s1s)

    def build(in_refs, out_refs, sems):
        send_sems, recv_sems = sems
        x, y, c, chips = _place()
        cps = [pltpu.make_async_remote_copy(
            src_ref=in_refs[w].at[2 * chip[0] + chip[1]], dst_ref=out_refs[w].at[j], send_sem=send_sems.at[w, j],
            recv_sem=recv_sems.at[w, j], device_id=(*chip, c), device_id_type=MESH)
            for w in range(nw) for j, chip in enumerate(chips)]

        def start():
            for cp in cps:
                cp.start()

        def finish():
            for cp in cps:
                cp.wait()

        return start, finish

    return _Comm(s1s, [jax.ShapeDtypeStruct((N_CHIPS - 1,) + s.shape[1:], s.dtype) for s in s1s],
                 [(nw, 3), (nw, 3)], build)


def _size(dims):
    n = 1
    for d in dims:
        n *= d
    return n


def _sem_grids(comm, sem_refs):
    grids, pos = [], 0
    for dims in comm.sem_dims:
        grids.append(_SemGrid(sem_refs[pos:pos + _size(dims)], dims))
        pos += _size(dims)
    return grids


def _comm_split_start(comm, name, after=()):
    c_in, c_out = len(comm.operands), len(comm.out_shape)
    counts = [_size(d) for d in comm.sem_dims]
    n_sem = sum(counts)
    assert not comm.aliases

    def body(*refs):
        srcs, lands = refs[:c_in], refs[c_in:c_in + c_out]
        first_sem = c_in + c_out + len(after)
        start, _ = comm.build(srcs, lands, _sem_grids(comm, refs[first_sem:first_sem + n_sem]))
        start()
        refs[-1][...] = jnp.zeros(refs[-1].shape, refs[-1].dtype)

    lands = [pltpu.with_memory_space_constraint(lax.empty(o.shape, o.dtype), HBM) for o in comm.out_shape]
    srcs = [pltpu.with_memory_space_constraint(a, HBM) for a in comm.operands]
    res = pl.pallas_call(
        body, name=name, in_specs=_hbm_specs(c_in + c_out) + [pl.BlockSpec(memory_space=pl.ANY)] * len(after),
        out_specs=[pl.BlockSpec(memory_space=pltpu.SEMAPHORE)] * n_sem + _hbm_specs(c_in + c_out)
        + [pl.BlockSpec(memory_space=pltpu.VMEM)],
        out_shape=[pltpu.SemaphoreType.DMA(())] * n_sem + [pltpu.HBM(a.shape, a.dtype) for a in comm.operands]
        + [pltpu.HBM(o.shape, o.dtype) for o in comm.out_shape] + [jax.ShapeDtypeStruct((SUBLANES, LANES), F32)],
        input_output_aliases={i: n_sem + i for i in range(c_in + c_out)},
        compiler_params=_params(has_side_effects=pltpu.SideEffectType.DATAFLOW_SIDE_EFFECTING))(*srcs, *lands, *after)
    return res[:-1], res[-1]


def _comm_split_wait(comm, state, after, name):
    c_in, c_out, n_sem = len(comm.operands), len(comm.out_shape), sum(_size(d) for d in comm.sem_dims)
    sems, srcs, lands = state[:n_sem], state[n_sem:n_sem + c_in], state[n_sem + c_in:]

    def body(*refs):
        src_refs, land_refs = refs[:c_in], refs[c_in:c_in + c_out]
        _, finish = comm.build(src_refs, land_refs, _sem_grids(comm, refs[c_in + c_out:c_in + c_out + n_sem]))
        finish()

    sem_spec = pl.BlockSpec(memory_space=pltpu.SEMAPHORE)
    res = pl.pallas_call(
        body, name=name, in_specs=_hbm_specs(c_in + c_out) + [sem_spec] * n_sem + [pl.BlockSpec(memory_space=pl.ANY)],
        out_specs=_hbm_specs(c_in + c_out),
        out_shape=[pltpu.HBM(a.shape, a.dtype) for a in srcs] + [pltpu.HBM(o.shape, o.dtype) for o in lands],
        input_output_aliases={i: i for i in range(c_in + c_out)},
        compiler_params=_params(has_side_effects=pltpu.SideEffectType.DATAFLOW_SIDE_EFFECTING),
    )(*srcs, *lands, *sems, after)
    return res[:c_in], res[c_in:]


def _share_comm(fs, by_cols=()):
    nw = len(fs)

    def build(in_refs, out_refs, sems):
        del in_refs
        send_sems, recv_sems = sems
        x, y, c, _ = _place()

        def copy(w, half):
            part = _part(out_refs[w], w in by_cols, half)
            return pltpu.make_async_remote_copy(
                src_ref=part, dst_ref=part, send_sem=send_sems.at[w], recv_sem=recv_sems.at[w],
                device_id=(x, y, 1 - c), device_id_type=MESH)

        sends = [copy(w, c) for w in range(nw)]

        def start():
            for cp in sends:
                cp.start()

        def finish():
            for w in range(nw):
                copy(w, 1 - c).wait_recv()
            for cp in sends:
                cp.wait_send()

        return start, finish

    return _Comm(fs, [jax.ShapeDtypeStruct(f.shape, f.dtype) for f in fs],
                 [(nw,), (nw,)], build,
                 aliases={w: w for w in range(nw)})


def _add_sibling(g, r1, place, name, by_cols=False):
    nch, h, cols = r1.shape
    tr = _div_tile(h, 1024 if by_cols else 256, 2 * SUBLANES)
    nb = h // tr
    mine = (lambda k, i, p: (k, i, p[0])) if by_cols else (lambda k, i, p: (k, p[0] * nb + i, 0))

    def body(place_ref, g_ref, r_ref, o_ref):
        del place_ref
        o_ref[...] = (g_ref[...].astype(F32) + r_ref[...].astype(F32)).astype(BF16)

    spec = pltpu.PrefetchScalarGridSpec(
        num_scalar_prefetch=1, grid=(nch, nb),
        in_specs=[pl.BlockSpec((None, tr, cols), mine), pl.BlockSpec((None, tr, cols), lambda k, i, p: (k, i, 0))],
        out_specs=pl.BlockSpec((None, tr, cols), lambda k, i, p: (k, i, 0)))
    return pl.pallas_call(body, name=name, grid_spec=spec, out_shape=jax.ShapeDtypeStruct((nch, h, cols), BF16),
                          compiler_params=_params())(place, g, r1)


def _add_chips(s1, r2, place, name, by_cols=False):
    _, h, cols = s1.shape
    tr = _div_tile(h, 1024 if by_cols else 256, 2 * SUBLANES)
    nb = h // tr
    mine = (lambda i, p: (i, p[0])) if by_cols else (lambda i, p: (p[0] * nb + i, 0))
    whole = (h, 2 * cols) if by_cols else (2 * h, cols)

    def body(place_ref, s_ref, r_ref, o_ref):
        del place_ref
        acc = s_ref[...].astype(F32)
        for j in range(N_CHIPS - 1):
            acc = acc + r_ref[j].astype(F32)
        o_ref[...] = acc

    spec = pltpu.PrefetchScalarGridSpec(
        num_scalar_prefetch=1, grid=(nb,),
        in_specs=[pl.BlockSpec((None, tr, cols), lambda i, p: (p[1], i, 0)),
                  pl.BlockSpec((N_CHIPS - 1, tr, cols), lambda i, p: (0, i, 0))],
        out_specs=pl.BlockSpec((tr, cols), mine))
    return pl.pallas_call(body, name=name, grid_spec=spec, out_shape=jax.ShapeDtypeStruct(whole, F32),
                          compiler_params=_params())(place, s1, r2)


def _quarter_turn(m):
    h = m.shape[-1] // 2
    return jnp.concatenate([-m[..., h:], m[..., :h]], axis=-1)


def _quarter_turn_back(m):
    h = m.shape[-1] // 2
    return jnp.concatenate([m[..., h:], -m[..., :h]], axis=-1)


def _join_cols(sh):
    return jnp.concatenate([sh[k] for k in range(N_CHIPS)], axis=1)


def _split_cols(full):
    c = full.shape[1] // N_CHIPS
    return jnp.stack([full[:, k * c:(k + 1) * c] for k in range(N_CHIPS)])


def kernel(x, c, positions, w_ada, b_ada, pre_norm1_g, w_in, gm_ln_g, gm_ln_b, gm_w_s, gm_b_s, w_branch_a, q_norm_g, w_uq, kv_norm_g, w_ukv, w_branch_b, w_out, post_norm1_g, pre_norm2_g, w_up, conv_w, conv_b, w_down, post_norm2_g, loss_target, m_w_ada, m_b_ada, m_pre_norm1_g, m_w_in, m_gm_ln_g, m_gm_ln_b, m_gm_w_s, m_gm_b_s, m_w_branch_a, m_q_norm_g, m_w_uq, m_kv_norm_g, m_w_ukv, m_w_branch_b, m_w_out, m_post_norm1_g, m_pre_norm2_g, m_w_up, m_conv_w, m_conv_b, m_w_down, m_post_norm2_g, v_w_ada, v_b_ada, v_pre_norm1_g, v_w_in, v_gm_ln_g, v_gm_ln_b, v_gm_w_s, v_gm_b_s, v_w_branch_a, v_q_norm_g, v_w_uq, v_kv_norm_g, v_w_ukv, v_w_branch_b, v_w_out, v_post_norm1_g, v_pre_norm2_g, v_w_up, v_conv_w, v_conv_b, v_w_down, v_post_norm2_g):
    given = dict(locals())
    s, d = x.shape[1], x.shape[2]
    gw = gm_ln_g.shape[0]
    ql, kvl = q_norm_g.shape[0], kv_norm_g.shape[0]
    heads = N_CHIPS * w_uq.shape[1] // (NOPE + ROPE)
    ff = N_CHIPS * w_down.shape[0]
    assert gw == d and N_CHIPS * w_ukv.shape[1] == heads * (NOPE + VHEAD)
    ix, iy, ic = lax.axis_index("x"), lax.axis_index("y"), lax.axis_index("c")
    chip = 2 * ix + iy
    dev = 2 * chip + ic
    row = lambda v: v.reshape(1, -1)

    c_all = _all_gather(jnp.pad(c, ((0, SUBLANES - 1), (0, 0))), "gather_c").reshape(N_DEV, SUBLANES, d)[:, 0]
    na = w_ada.shape[1]
    b_ada_mine = lax.dynamic_slice(b_ada, (chip * na,), (na,))
    mod_cols = _ada_fwd(c_all, w_ada, row(b_ada_mine), "ada_fwd")
    mod_all = _all_gather(mod_cols, "gather_mod").reshape(N_CHIPS, N_CORES, N_DEV, na)[:, 0]
    mod = lax.dynamic_index_in_dim(mod_all, dev, axis=1, keepdims=False).reshape(N_MOD, d)
    shift1, scale1, gate1, shift2, scale2, gate2 = (mod[i:i + 1] for i in range(N_MOD))

    mine = {n: (given[n].T if n == "w_in" else given[n]).astype(BF16) for n in BIG}
    gather = lambda names: _gather_comm([mine[n] for n in names], [i for i, n in enumerate(names) if n == "w_in"])
    whole = lambda n, g: lax.dynamic_update_slice(g, mine[n][None], (chip, 0, 0))
    rows4 = lambda sh4: sh4.reshape(-1, sh4.shape[2])
    wi_t = rows4(whole("w_in", _run_comm(gather(["w_in"]), "gather_w_in")[0]))
    o_q, o_kv, o_pe, o_ga = 2 * gw, 2 * gw + ql, 2 * gw + ql + kvl, 2 * gw + ql + kvl + ROPE
    w_in_big_t = jnp.concatenate([wi_t[:o_q], wi_t[o_ga:]], axis=0)
    w_in_lat_t = jnp.concatenate([wi_t[o_q:o_ga], _quarter_turn(wi_t[o_pe:o_ga].T).T], axis=0)

    inv = ROPE_THETA ** (-jnp.arange(0, ROPE, 2, dtype=F32) / ROPE)
    ang = positions[0].astype(F32)[:, None] * inv
    cos, sin = jnp.cos(ang), jnp.sin(ang)
    rope_k = jnp.concatenate([cos, cos, sin, sin], axis=1)
    softmax_scale = float(NOPE + ROPE) ** -0.5
    rope_q = jnp.concatenate([jnp.ones((s, NOPE), F32), rope_k], axis=1) * softmax_scale

    x2d, tgt = x[0], loss_target[0]
    g_pre1, g_post1, g_pre2, g_post2 = row(pre_norm1_g), row(post_norm1_g), row(pre_norm2_g), row(post_norm2_g)
    ln_g, ln_b, q_g, kv_g = row(gm_ln_g), row(gm_ln_b), row(q_norm_g), row(kv_norm_g)
    b_s_t = gm_b_s.T
    conv_wf = _all_gather(jnp.pad(conv_w, ((0, SUBLANES - CONV_TAPS), (0, 0))), "gather_conv_w")
    conv_wf = conv_wf.reshape(N_CHIPS, N_CORES, SUBLANES, conv_w.shape[1])[:, 0, :CONV_TAPS]
    conv_wf = conv_wf.transpose(1, 0, 2).reshape(CONV_TAPS, 2 * ff)
    conv_bf = row(conv_b)

    h1 = _prenorm(x2d, g_pre1, scale1, shift1, "prenorm1")
    z_big, (g_uq, g_ukv, g_a) = _matmul(h1, w_in_big_t, mode="nt", out_dtype=F32, name="mm_z_big", tm=s,
                                        comm=gather(["w_uq", "w_ukv", "w_branch_a"]))
    wq = _join_cols(whole("w_uq", g_uq)).reshape(ql, heads, NOPE + ROPE)
    w_q = jnp.concatenate([wq, _quarter_turn(wq[:, :, NOPE:])], axis=2).reshape(ql, heads * HEAD_W)
    w_kv = _join_cols(whole("w_ukv", g_ukv)).reshape(kvl, heads, 2, NOPE).transpose(0, 2, 1, 3)
    w_kv = w_kv.reshape(kvl, 2 * heads * NOPE)
    w_a = rows4(whole("w_branch_a", g_a))
    z_lat = _matmul(h1, w_in_lat_t, mode="nt", out_dtype=F32, name="mm_z_lat", tm=s, tn=1024)
    a_act = _gmlp_fwd(z_big, ln_g, ln_b, gm_w_s, b_s_t, "gmlp_fwd")
    qn, kvn, kr = _mla_prep(z_lat, q_g, kv_g, rope_k, "mla_prep")
    q_rot = _matmul(qn, w_q, mode="nn", out_dtype=BF16, name="mm_q", tm=s, tn=HEAD_W, mul=rope_q)
    kv_all = _matmul(kvn, w_kv, mode="nn", out_dtype=BF16, name="mm_kv", tm=s, tn=1024)
    (o_att, lse), (g_b, g_o, g_up) = _attn_fwd(q_rot, kv_all, kr, heads, "attn_fwd",
                                               comm=gather(["w_branch_b", "w_out", "w_up"]))
    w_b, w_o, w_upf = rows4(whole("w_branch_b", g_b)), rows4(whole("w_out", g_o)), whole("w_up", g_up)
    y_a = _matmul(a_act, w_a, mode="nn", out_dtype=F32, name="mm_y_a", tm=s)
    y_b = _matmul(o_att, w_b, mode="nn", out_dtype=F32, name="mm_y_b", tm=s)
    merged = _merge(z_big, y_a, y_b, "merge")
    y1 = _matmul(merged, w_o, mode="nn", out_dtype=F32, name="mm_y1", tm=s)
    x1, h2 = _post_pre(x2d, y1, gate1, g_post1, g_pre2, scale2, shift2, "post1_pre2")

    up_pre, (g_dn,) = _matmul(h2, w_upf, mode="nn", out_dtype=BF16, name="mm_up", tm=s, tn=1408,
                              comm=gather(["w_down"]))
    w_dn = rows4(whole("w_down", g_dn))
    act = _conv_fwd(up_pre, conv_wf, conv_bf, "conv_fwd")
    ffn = _matmul(act, w_dn, mode="nn", out_dtype=F32, name="mm_ffn", tm=s, tk=1408)

    dffn, dgate2, g_post2_grad, dx2, loss_part = _post_bwd(ffn, gate2, g_post2, "post2_bwd", xin=x1, target=tgt)
    loss = lax.psum(loss_part[0, 0], ("x", "y", "c"))
    place = jnp.stack([ic, chip]).astype(jnp.int32)
    rows_of = lambda g: g.reshape(N_CHIPS, g.shape[0] // N_CHIPS, g.shape[1])
    add_sibling = lambda names, gs, r1s: [_add_sibling(g, r1, place, "rs_add_sibling_" + n, by_cols=n == "w_in")
                                          for n, g, r1 in zip(names, gs, r1s)]
    add_chips = lambda names, s1s, r2s: [_add_chips(s1, r2, place, "rs_add_chips_" + n, by_cols=n == "w_in")
                                         for n, s1, r2 in zip(names, s1s, r2s)]
    dact = _matmul(dffn, w_dn, mode="nt", out_dtype=BF16, name="mm_dact", tm=s)
    gp_down = [rows_of(_matmul(act, dffn, mode="tn", out_dtype=BF16, name="mm_gw_down", tn=1024, tk=s))]
    (dup, gcw_g, gcw_v, gcb_g, gcb_v), r1_down = _conv_bwd(up_pre, dact, conv_wf, conv_bf, "conv_bwd",
                                                            comm=_swap_comm(gp_down))
    s1_down = add_sibling(["w_down"], gp_down, r1_down)
    dh2, r2_down = _matmul(dup, w_upf, mode="nt", out_dtype=F32, name="mm_dh2", tm=s, tk=1408,
                           comm=_exchange_comm(s1_down))
    half_down = add_chips(["w_down"], s1_down, r2_down)
    gw_up = _matmul(h2, dup, mode="tn", out_dtype=BF16, name="mm_gw_up", tn=1408, tk=s, out_groups=N_CHIPS)
    dx1, dshift2, dscale2, g_pre2_grad = _prenorm_bwd(x1, dh2, dx2, g_pre2, scale2, "prenorm2_bwd")

    dy1, dgate1, g_post1_grad = _post_bwd(y1, gate1, g_post1, "post1_bwd", dxo=dx1)
    dmerged = _matmul(dy1, w_o, mode="nt", out_dtype=F32, name="mm_dmerged", tm=s)
    gw_out = _matmul(merged, dy1, mode="tn", out_dtype=BF16, name="mm_gw_out", tn=1024, tk=s)
    dy_a, dy_b, dz_big = _merge_bwd(dmerged, z_big, y_a, y_b, "merge_bwd")
    da = _matmul(dy_a, w_a, mode="nt", out_dtype=F32, name="mm_da", tm=s)
    gw_a = _matmul(a_act, dy_a, mode="tn", out_dtype=BF16, name="mm_gw_a", tn=1024, tk=s)
    do = _matmul(dy_b, w_b, mode="nt", out_dtype=BF16, name="mm_do", tm=s)
    gw_b = _matmul(o_att, dy_b, mode="tn", out_dtype=BF16, name="mm_gw_b", tn=1024, tk=s)
    mid = ["w_up", "w_out", "w_branch_a", "w_branch_b"]
    gp_mid = [gw_up, rows_of(gw_out), rows_of(gw_a), rows_of(gw_b)]
    (dz_big, g_ws, g_bs_t, g_ln_g, g_ln_b), r1_mid = _gmlp_bwd(z_big, da, dz_big, ln_g, ln_b, gm_w_s, b_s_t,
                                                                "gmlp_bwd", comm=_swap_comm(gp_mid))
    s1_mid = add_sibling(mid, gp_mid, r1_mid)
    (dq, dk, dv), r2_up_out = _attn_bwd(q_rot, kv_all, kr, o_att, do, lse, heads, "attn_bwd",
                                        comm=_exchange_comm(s1_mid[:2]))
    dq_big, dkv, dkk = _mla_bwd_mid(dq, dk, dv, rope_q, rope_k, heads, "mla_bwd_mid")
    gw_q = _matmul(qn, dq_big, mode="tn", out_dtype=F32, name="mm_gw_q", tn=1024, tk=s)
    dqn = _matmul(dq_big, w_q, mode="nt", out_dtype=F32, name="mm_dqn", tm=s, tk=1024)
    gw_kv = _matmul(kvn, dkv, mode="tn", out_dtype=BF16, name="mm_gw_kv", tn=1024, tk=s)
    dkvn = _matmul(dkv, w_kv, mode="nt", out_dtype=F32, name="mm_dkvn", tm=s, tk=1024)
    dz_lat, g_q, g_kv = _mla_bwd_post(z_lat, dqn, dkvn, dkk, q_g, kv_g, "mla_bwd_post")

    partial = {
        "gm_ln_g": g_ln_g, "gm_ln_b": g_ln_b, "gm_w_s": g_ws, "gm_b_s": g_bs_t[:, :gm_b_s.shape[0]].T,
        "q_norm_g": g_q, "kv_norm_g": g_kv, "post_norm1_g": g_post1_grad, "pre_norm2_g": g_pre2_grad,
        "conv_w": jnp.concatenate([gcw_g, gcw_v], axis=1), "conv_b": jnp.concatenate([gcb_g, gcb_v], axis=1),
        "post_norm2_g": g_post2_grad,
    }
    flat = jnp.concatenate([partial[n].reshape(-1) for n in SMALL_PARTIAL])
    n_small = flat.shape[0]
    rows_small = -(-n_small // (LANES * SMALL_ROW_TILE)) * SMALL_ROW_TILE
    flat = jnp.pad(flat, (0, rows_small * LANES - n_small)).reshape(rows_small, LANES)
    small_sum = _sum_leading(_all_gather(flat, "gather_small").reshape(N_DEV, rows_small, LANES), "sum_small")
    small_sum = small_sum.reshape(-1)
    small_grads, off = {}, 0
    for n in SMALL_PARTIAL:
        shape = (CONV_TAPS, 2 * ff) if n == "conv_w" else given[n].shape
        small_grads[n] = small_sum[off:off + partial[n].size].reshape(shape)
        off += partial[n].size
    small_grads["conv_w"] = lax.dynamic_slice(small_grads["conv_w"], (0, chip * conv_w.shape[1]), conv_w.shape)

    dh1, r2_a_b = _matmul(dz_big, w_in_big_t, mode="nn", out_dtype=F32, name="mm_dh1_big", tm=s, tk=1024,
                          comm=_exchange_comm(s1_mid[2:]))
    half_mid = add_chips(mid, s1_mid, list(r2_up_out) + list(r2_a_b))
    dh1 = _matmul(dz_lat, w_in_lat_t, mode="nn", out_dtype=F32, name="mm_dh1_lat", tm=s, tk=1024, add=dh1)
    gw_big_t, shared = _matmul(dz_big, h1, mode="tn", out_dtype=BF16, name="mm_gw_in_big", tn=2048, tk=s,
                               comm=_share_comm(half_down + half_mid))
    grads = dict(zip(["w_down"] + mid, shared), **small_grads)
    gw_lat_t = _matmul(dz_lat, h1, mode="tn", out_dtype=F32, name="mm_gw_in_lat", tm=1024, tn=1024, tk=s)

    gq = gw_q.reshape(ql, heads, HEAD_W)
    gq_pe = gq[:, :, NOPE:NOPE + ROPE] + _quarter_turn_back(gq[:, :, NOPE + ROPE:])
    g_pe_t = gw_lat_t[ql + kvl:ql + kvl + ROPE] + _quarter_turn_back(gw_lat_t[ql + kvl + ROPE:].T).T
    last = ["w_in", "w_uq", "w_ukv"]
    gw_in_t = jnp.concatenate([gw_big_t[:o_q], gw_lat_t[:ql + kvl].astype(BF16), g_pe_t.astype(BF16),
                               gw_big_t[o_q:]], axis=0)
    gp_last = [
        gw_in_t.reshape(N_CHIPS, gw_in_t.shape[0] // N_CHIPS, d),
        _split_cols(jnp.concatenate([gq[:, :, :NOPE], gq_pe], axis=2).reshape(ql, heads * (NOPE + ROPE)).astype(BF16)),
        _split_cols(gw_kv.reshape(kvl, 2, heads, NOPE).transpose(0, 2, 1, 3).reshape(kvl, heads * 2 * NOPE)),
    ]
    (grad_x, dshift1, dscale1, g_pre1_grad), r1_last = _prenorm_bwd(x2d, dh1, dx1, g_pre1, scale1, "prenorm1_bwd",
                                                                    comm=_swap_comm(gp_last, by_cols=[0]))
    s1_last = add_sibling(last, gp_last, r1_last)

    dmod = jnp.concatenate([dshift1, dscale1, dgate1, dshift2, dscale2, dgate2, g_pre1_grad], axis=1)
    dmod_all = _all_gather(jnp.pad(dmod, ((0, SUBLANES - 1), (0, 0))), "gather_dmod")
    dmod_all = dmod_all.reshape(N_DEV, SUBLANES, (N_MOD + 1) * d)[:, 0]
    dmod_sum = _sum_leading(dmod_all.reshape(N_DEV, 1, (N_MOD + 1) * d), "sum_dmod")[0]
    grads["b_ada"], grads["pre_norm1_g"] = dmod_sum[:N_MOD * d], dmod_sum[N_MOD * d:]
    dmod_mine = lax.dynamic_slice(dmod_all, (0, chip * na), (N_DEV, na))
    grads["w_ada"] = _ada_bwd(c_all.T, dmod_mine, "ada_bwd")

    delta, new_m, new_v = {}, {}, {}

    def adamw(n, after=None):
        turn = (lambda a: a.T) if n == "w_in" else (lambda a: a)
        outs = _adamw(turn(given[n]), grads[n], turn(given["m_" + n]), turn(given["v_" + n]), "adamw_" + n,
                      after=after)
        grads[n] = turn(grads[n])
        delta[n], new_m[n], new_v[n] = (turn(o) for o in outs)

    exchange_last = _exchange_comm(s1_last)
    in_flight, token = _comm_split_start(exchange_last, "rs_exchange_last_start", after=[dmod_sum, small_sum])
    for n in ["w_ada", "w_down"] + mid:
        adamw(n, after=token)
    s1_last, r2_last = _comm_split_wait(exchange_last, in_flight, delta[mid[-1]], "rs_exchange_last_wait")
    half_last = add_chips(last, s1_last, r2_last)
    grads.update(zip(last, _run_comm(_share_comm(half_last, by_cols=[0]), "rs_share_last")))
    for n in last:
        adamw(n)

    def small_pack(prefix, source):
        v = jnp.concatenate([source[prefix + n].reshape(-1) for n in SMALL])
        rows = -(-v.shape[0] // (LANES * SUBLANES)) * SUBLANES
        return jnp.pad(v, (0, rows * LANES - v.shape[0])).reshape(rows, LANES)

    outs = _adamw(small_pack("", given), small_pack("", grads), small_pack("m_", given), small_pack("v_", given),
                  "adamw_small")
    off = 0
    for n in SMALL:
        size = given[n].size
        for store, packed_out in zip((delta, new_m, new_v), outs):
            store[n] = packed_out.reshape(-1)[off:off + size].reshape(given[n].shape)
        off += size

    return (loss, grad_x[None], *[grads[n] for n in WEIGHTS], *[delta[n] for n in WEIGHTS],
            *[new_m[n] for n in WEIGHTS], *[new_v[n] for n in WEIGHTS])
```

```python
import functools

import jax
import jax.numpy as jnp
from jax import lax
from jax.experimental import pallas as pl
from jax.experimental.pallas import tpu as pltpu

F32 = jnp.float32
BF16 = jnp.bfloat16
MESH = pl.DeviceIdType.MESH
HBM = pltpu.HBM

EPS = 1e-6
NOPE, ROPE, VHEAD = 128, 64, 128
HEAD_W = NOPE + 2 * ROPE
ROPE_THETA = 10000.0
CONV_TAPS = 3
N_MOD = 6
N_CHIPS, N_CORES, N_DEV = 4, 2, 8
ADAM_LR, ADAM_B1, ADAM_B2, ADAM_EPS, ADAM_WD, ADAM_STEP = 0.001, 0.9, 0.999, 1e-08, 0.01, 10

LANES = 128
SUBLANES = 8
VMEM_LIMIT = 56 * 2**20
MIDDLE_STAGE_AT = 70
SMALL_ROW_TILE = 256

BIG = ("w_in", "w_branch_a", "w_uq", "w_ukv", "w_branch_b", "w_out", "w_up", "w_down")
WEIGHTS = ("w_ada", "b_ada", "pre_norm1_g", "w_in", "gm_ln_g", "gm_ln_b", "gm_w_s", "gm_b_s", "w_branch_a",
           "q_norm_g", "w_uq", "kv_norm_g", "w_ukv", "w_branch_b", "w_out", "post_norm1_g", "pre_norm2_g",
           "w_up", "conv_w", "conv_b", "w_down", "post_norm2_g")
SMALL_PARTIAL = ("gm_ln_g", "gm_ln_b", "gm_w_s", "gm_b_s", "q_norm_g", "kv_norm_g", "post_norm1_g",
                 "pre_norm2_g", "conv_w", "conv_b", "post_norm2_g")
SMALL = ("b_ada", "pre_norm1_g") + SMALL_PARTIAL


def _div_tile(n, cap, mult=LANES):
    t = (min(cap, n) // mult) * mult
    while t >= mult:
        if n % t == 0:
            return t
        t -= mult
    return n


def _params(**kw):
    return pltpu.CompilerParams(vmem_limit_bytes=VMEM_LIMIT, **kw)


def _row_spec(width):
    return pl.BlockSpec((1, width), lambda *_: (0, 0))


def _gelu(x):
    k = 0.7978845608028654
    return 0.5 * x * (1.0 + jnp.tanh(k * (x + 0.044715 * x * x * x)))


def _gelu_grad(x):
    k = 0.7978845608028654
    t = jnp.tanh(k * (x + 0.044715 * x * x * x))
    return 0.5 * (1.0 + t) + 0.5 * x * (1.0 - t * t) * k * (1.0 + 3.0 * 0.044715 * x * x)


def _sigmoid(x):
    return 0.5 * jnp.tanh(0.5 * x) + 0.5


def _dot(a, b, dims):
    return lax.dot_general(a, b, (dims, ((), ())), preferred_element_type=F32)


NN = ((1,), (0,))
NT = ((1,), (1,))
TN = ((0,), (0,))


def _logical(arr):
    if arr.ndim == 2:
        return arr.shape[0], arr.shape[1], arr.shape[1]
    return arr.shape[1], arr.shape[0] * arr.shape[2], arr.shape[2]


def _tile_spec(ndim, group_w, blk_rows, blk_cols, row_of, col_of):
    if ndim == 2:
        return pl.BlockSpec((blk_rows, blk_cols), lambda i, j, k: (row_of(i, j, k), col_of(i, j, k)))
    per = group_w // blk_cols
    return pl.BlockSpec((None, blk_rows, blk_cols),
                        lambda i, j, k: (col_of(i, j, k) // per, row_of(i, j, k), col_of(i, j, k) % per))


def _matmul(a, b, *, mode, out_dtype, name, tm=512, tn=512, tk=2048, mul=None, add=None, out_groups=None, comm=None):
    ar, ac, agw = _logical(a)
    br, bc, bgw = _logical(b)
    if mode == "nn":
        m, kd, n = ar, ac, bc
        m_w, k_w, n_w = (), (agw,), (bgw,)
    elif mode == "nt":
        m, kd, n = ar, ac, br
        m_w, k_w, n_w = (), (agw, bgw), ()
    else:
        m, kd, n = ac, ar, bc
        m_w, k_w, n_w = (agw,), (), (bgw,)
    if out_groups is not None:
        n_w = n_w + (n // out_groups,)
    tm = _div_tile(min((m,) + m_w), tm, LANES if mode == "tn" else SUBLANES)
    tn = _div_tile(min((n,) + n_w), tn)
    tk = _div_tile(min((kd,) + k_w), tk)
    assert all(w % tn == 0 for w in n_w) and all(w % tk == 0 for w in k_w) and all(w % tm == 0 for w in m_w)
    nk = kd // tk
    dims = {"nn": NN, "nt": NT, "tn": TN}[mode]
    gi, gj, gk = (lambda i, j, k: i), (lambda i, j, k: j), (lambda i, j, k: k)
    if mode == "nn":
        a_spec = _tile_spec(a.ndim, agw, tm, tk, gi, gk)
        b_spec = _tile_spec(b.ndim, bgw, tk, tn, gk, gj)
    elif mode == "nt":
        a_spec = _tile_spec(a.ndim, agw, tm, tk, gi, gk)
        b_spec = _tile_spec(b.ndim, bgw, tn, tk, gj, gk)
    else:
        a_spec = _tile_spec(a.ndim, agw, tk, tm, gk, gi)
        b_spec = _tile_spec(b.ndim, bgw, tk, tn, gk, gj)
    in_specs, operands = [a_spec, b_spec], [a, b]
    if mul is not None:
        assert mul.shape == (m, tn)
        in_specs.append(pl.BlockSpec((tm, tn), lambda i, j, k: (i, 0)))
        operands.append(mul)
    if add is not None:
        in_specs.append(pl.BlockSpec((tm, tn), lambda i, j, k: (i, j)))
        operands.append(add)

    def body(*refs):
        a_ref, b_ref = refs[0], refs[1]
        pos = 2
        mul_ref = add_ref = None
        if mul is not None:
            mul_ref, pos = refs[pos], pos + 1
        if add is not None:
            add_ref, pos = refs[pos], pos + 1
        o_ref = refs[pos]

        def finish(r):
            if mul_ref is not None:
                r = r * mul_ref[...]
            if add_ref is not None:
                r = r + add_ref[...]
            o_ref[...] = r.astype(out_dtype)

        part = _dot(a_ref[...], b_ref[...], dims)
        if nk == 1:
            finish(part)
        else:
            acc_ref = refs[pos + 1]
            k = pl.program_id(2)

            @pl.when(k == 0)
            def _():
                acc_ref[...] = part

            @pl.when(k > 0)
            def _():
                acc_ref[...] += part

            @pl.when(k == nk - 1)
            def _():
                finish(acc_ref[...])

    if out_groups is None:
        out_spec, out_dims = _tile_spec(2, n, tm, tn, gi, gj), (m, n)
    else:
        out_spec, out_dims = _tile_spec(3, n // out_groups, tm, tn, gi, gj), (out_groups, m, n // out_groups)
    return _call(body, operands, comm, name=name, grid=(m // tm, n // tn, nk), in_specs=in_specs, out_specs=out_spec,
                 out_shape=jax.ShapeDtypeStruct(out_dims, out_dtype),
                 scratch_shapes=[] if nk == 1 else [pltpu.VMEM((tm, tn), F32)])


def _accumulate(ref, value):
    @pl.when(pl.program_id(0) == 0)
    def _():
        ref[...] = value

    @pl.when(pl.program_id(0) > 0)
    def _():
        ref[...] += value


def _colsum(v):
    return jnp.sum(v, axis=0, keepdims=True)


def _rowmean(v):
    return jnp.mean(v, axis=-1, keepdims=True)


def _prenorm(x, g, scale, shift, name):
    s, d = x.shape
    tb = _div_tile(s, 256, SUBLANES)

    def body(x_ref, g_ref, sc_ref, sh_ref, h_ref):
        xv = x_ref[...]
        r = lax.rsqrt(_rowmean(xv * xv) + EPS)
        h_ref[...] = ((xv * r) * g_ref[...] * (1.0 + sc_ref[...]) + sh_ref[...]).astype(BF16)

    blk = pl.BlockSpec((tb, d), lambda i: (i, 0))
    return pl.pallas_call(
        body, name=name, grid=(s // tb,), in_specs=[blk, _row_spec(d), _row_spec(d), _row_spec(d)],
        out_specs=blk, out_shape=jax.ShapeDtypeStruct((s, d), BF16), compiler_params=_params(),
    )(x, g, scale, shift)


def _post_pre(x, y, gate, pg, g2, scale2, shift2, name):
    s, d = x.shape
    tb = _div_tile(s, 256, SUBLANES)

    def body(x_ref, y_ref, gate_ref, pg_ref, g2_ref, sc_ref, sh_ref, x1_ref, h2_ref):
        yv = y_ref[...]
        rp = lax.rsqrt(_rowmean(yv * yv) + EPS)
        x1 = x_ref[...] + gate_ref[...] * ((yv * rp) * pg_ref[...])
        x1_ref[...] = x1
        r2 = lax.rsqrt(_rowmean(x1 * x1) + EPS)
        h2_ref[...] = ((x1 * r2) * g2_ref[...] * (1.0 + sc_ref[...]) + sh_ref[...]).astype(BF16)

    blk = pl.BlockSpec((tb, d), lambda i: (i, 0))
    return pl.pallas_call(
        body, name=name, grid=(s // tb,), in_specs=[blk, blk] + [_row_spec(d)] * 5,
        out_specs=[blk, blk],
        out_shape=[jax.ShapeDtypeStruct((s, d), F32), jax.ShapeDtypeStruct((s, d), BF16)],
        compiler_params=_params(),
    )(x, y, gate, pg, g2, scale2, shift2)


def _post_bwd(y, gate, pg, name, *, dxo=None, xin=None, target=None):
    s, d = y.shape
    tb = _div_tile(s, 256, SUBLANES)
    from_loss = target is not None

    def body(*refs):
        if from_loss:
            y_ref, gate_ref, pg_ref, xin_ref, t_ref, dy_ref, dgate_ref, dpg_ref, dxo_ref, loss_ref = refs
        else:
            y_ref, gate_ref, pg_ref, dxo_in_ref, dy_ref, dgate_ref, dpg_ref = refs
        yv = y_ref[...]
        rp = lax.rsqrt(_rowmean(yv * yv) + EPS)
        yh = yv * rp
        fn = yh * pg_ref[...]
        gate = gate_ref[...]
        if from_loss:
            err = xin_ref[...] + gate * fn - t_ref[...]
            dxo = err * (1.0 / d)
            dxo_ref[...] = dxo
            part = 0.5 * jnp.sum(_rowmean(err * err), axis=0, keepdims=True)
            _accumulate(loss_ref, jnp.broadcast_to(part, loss_ref.shape))
        else:
            dxo = dxo_in_ref[...]
        _accumulate(dgate_ref, _colsum(dxo * fn))
        dfn = dxo * gate
        _accumulate(dpg_ref, _colsum(dfn * yh))
        dyh = dfn * pg_ref[...]
        dy_ref[...] = (rp * (dyh - yh * _rowmean(dyh * yh))).astype(BF16)

    blk = pl.BlockSpec((tb, d), lambda i: (i, 0))
    in_specs = [blk, _row_spec(d), _row_spec(d)]
    out_specs = [blk, _row_spec(d), _row_spec(d)]
    out_shape = [jax.ShapeDtypeStruct((s, d), BF16), jax.ShapeDtypeStruct((1, d), F32),
                 jax.ShapeDtypeStruct((1, d), F32)]
    if from_loss:
        operands = (y, gate, pg, xin, target)
        in_specs += [blk, blk]
        out_specs += [blk, _row_spec(LANES)]
        out_shape += [jax.ShapeDtypeStruct((s, d), F32), jax.ShapeDtypeStruct((1, LANES), F32)]
    else:
        operands = (y, gate, pg, dxo)
        in_specs += [blk]
    return pl.pallas_call(
        body, name=name, grid=(s // tb,), in_specs=in_specs, out_specs=out_specs, out_shape=out_shape,
        compiler_params=_params(),
    )(*operands)


def _prenorm_bwd(xin, dh, dres, g, scale, name, comm=None):
    s, d = xin.shape
    tb = _div_tile(s, 256, SUBLANES)

    def body(x_ref, dh_ref, dres_ref, g_ref, sc_ref, dx_ref, dshift_ref, dscale_ref, dg_ref):
        xv = x_ref[...]
        r = lax.rsqrt(_rowmean(xv * xv) + EPS)
        xn = xv * r
        dh = dh_ref[...]
        g1 = g_ref[...]
        s1 = 1.0 + sc_ref[...]
        _accumulate(dshift_ref, _colsum(dh))
        _accumulate(dscale_ref, _colsum(dh * xn * g1))
        _accumulate(dg_ref, _colsum(dh * xn * s1))
        dxn = dh * g1 * s1
        dx_ref[...] = dres_ref[...] + r * (dxn - xn * _rowmean(dxn * xn))

    blk = pl.BlockSpec((tb, d), lambda i: (i, 0))
    return _call(
        body, (xin, dh, dres, g, scale), comm, name=name, grid=(s // tb,),
        in_specs=[blk, blk, blk, _row_spec(d), _row_spec(d)],
        out_specs=[blk, _row_spec(d), _row_spec(d), _row_spec(d)],
        out_shape=[jax.ShapeDtypeStruct((s, d), F32)] + [jax.ShapeDtypeStruct((1, d), F32)] * 3)


def _merge(z_big, y_a, y_b, name):
    s, d = y_a.shape
    tb = _div_tile(s, 256, SUBLANES)

    def body(zg_ref, ya_ref, yb_ref, o_ref):
        o_ref[...] = (_sigmoid(zg_ref[:, :d]) * ya_ref[...] + _sigmoid(zg_ref[:, d:]) * yb_ref[...]).astype(BF16)

    blk = pl.BlockSpec((tb, d), lambda i: (i, 0))
    return pl.pallas_call(
        body, name=name, grid=(s // tb,), in_specs=[pl.BlockSpec((tb, 2 * d), lambda i: (i, 1)), blk, blk],
        out_specs=blk, out_shape=jax.ShapeDtypeStruct((s, d), BF16), compiler_params=_params(),
    )(z_big, y_a, y_b)


def _merge_bwd(dmerged, z_big, y_a, y_b, name):
    s, d = y_a.shape
    tb = _div_tile(s, 256, SUBLANES)

    def body(dm_ref, zg_ref, ya_ref, yb_ref, dya_ref, dyb_ref, dz_ref):
        dm = dm_ref[...]
        sa, sb = _sigmoid(zg_ref[:, :d]), _sigmoid(zg_ref[:, d:])
        dya_ref[...] = (dm * sa).astype(BF16)
        dyb_ref[...] = (dm * sb).astype(BF16)
        dz_ref[:, :d] = (dm * ya_ref[...] * sa * (1.0 - sa)).astype(BF16)
        dz_ref[:, d:] = (dm * yb_ref[...] * sb * (1.0 - sb)).astype(BF16)

    blk = pl.BlockSpec((tb, d), lambda i: (i, 0))
    wide = pl.BlockSpec((tb, 2 * d), lambda i: (i, 1))
    return pl.pallas_call(
        body, name=name, grid=(s // tb,), in_specs=[blk, wide, blk, blk], out_specs=[blk, blk, wide],
        out_shape=[jax.ShapeDtypeStruct((s, d), BF16), jax.ShapeDtypeStruct((s, d), BF16),
                   jax.ShapeDtypeStruct((s, 4 * d), BF16)],
        compiler_params=_params(),
    )(dmerged, z_big, y_a, y_b)


def _causal_mask(ch):
    q = lax.broadcasted_iota(jnp.int32, (ch, ch), 0)
    p = lax.broadcasted_iota(jnp.int32, (ch, ch), 1)
    return (p <= q).astype(F32)


def _gmlp_norm(zc, lng, lnb, gw):
    u_pre, v_pre = zc[:, :gw], zc[:, gw:]
    vg = _gelu(v_pre)
    mu = _rowmean(vg)
    cen = vg - mu
    rstd = lax.rsqrt(_rowmean(cen * cen) + EPS)
    vhat = cen * rstd
    return u_pre, v_pre, _gelu(u_pre), vhat, rstd, vhat * lng + lnb


def _gmlp_fwd(z_big, ln_g, ln_b, w_s, b_s_t, name):
    s = z_big.shape[0]
    groups, ch, _ = w_s.shape
    gw = ln_g.shape[1]
    gd = gw // groups

    def body(z_ref, lng_ref, lnb_ref, ws_ref, bt_ref, a_ref):
        _, _, u, _, _, vn = _gmlp_norm(z_ref[...], lng_ref[...], lnb_ref[...], gw)
        mask = _causal_mask(ch)
        for g in range(groups):
            cols = slice(g * gd, (g + 1) * gd)
            wm = (ws_ref[g] * mask).astype(BF16)
            mixed = _dot(wm, vn[:, cols].astype(BF16), NN) + bt_ref[:, g:g + 1]
            a_ref[:, cols] = (u[:, cols] * mixed).astype(BF16)

    return pl.pallas_call(
        body, name=name, grid=(s // ch,),
        in_specs=[pl.BlockSpec((ch, 2 * gw), lambda n: (n, 0)), _row_spec(gw), _row_spec(gw),
                  pl.BlockSpec((groups, ch, ch), lambda n: (0, 0, 0)), pl.BlockSpec((ch, groups), lambda n: (0, 0))],
        out_specs=pl.BlockSpec((ch, gw), lambda n: (n, 0)),
        out_shape=jax.ShapeDtypeStruct((s, gw), BF16), compiler_params=_params(),
    )(z_big, ln_g, ln_b, w_s, b_s_t)


def _gmlp_bwd(z_big, da, dz_big, ln_g, ln_b, w_s, b_s_t, name, comm=None):
    s = z_big.shape[0]
    groups, ch, _ = w_s.shape
    gw = ln_g.shape[1]
    gd = gw // groups

    def body(z_ref, da_ref, dzin_ref, lng_ref, lnb_ref, ws_ref, bt_ref, dz_ref, gws_ref, gbt_ref, glng_ref, glnb_ref):
        del dzin_ref
        lng = lng_ref[...]
        u_pre, v_pre, u, vhat, rstd, vn = _gmlp_norm(z_ref[...], lng, lnb_ref[...], gw)
        da = da_ref[...]
        mask = _causal_mask(ch)
        first = pl.program_id(0) == 0
        dvn_parts = []
        lane = lax.broadcasted_iota(jnp.int32, (ch, LANES), 1)
        gb = jnp.zeros((ch, LANES), F32)
        for g in range(groups):
            cols = slice(g * gd, (g + 1) * gd)
            wm = (ws_ref[g] * mask).astype(BF16)
            vn_g = vn[:, cols].astype(BF16)
            mixed = _dot(wm, vn_g, NN) + bt_ref[:, g:g + 1]
            dz_ref[:, cols] = (da[:, cols] * mixed * _gelu_grad(u_pre[:, cols])).astype(BF16)
            dmixed = da[:, cols] * u[:, cols]
            dm16 = dmixed.astype(BF16)
            dvn_parts.append(_dot(wm, dm16, TN))
            gws = _dot(dm16, vn_g, NT) * mask

            @pl.when(first)
            def _(g=g, gws=gws):
                gws_ref[g] = gws

            @pl.when(jnp.logical_not(first))
            def _(g=g, gws=gws):
                gws_ref[g] += gws

            gb = gb + jnp.where(lane == g, jnp.sum(dmixed, axis=1, keepdims=True), 0.0)
        _accumulate(gbt_ref, gb)
        dvn = jnp.concatenate(dvn_parts, axis=1)
        _accumulate(glnb_ref, _colsum(dvn))
        _accumulate(glng_ref, _colsum(dvn * vhat))
        dvh = dvn * lng
        dvg = rstd * (dvh - _rowmean(dvh) - vhat * _rowmean(dvh * vhat))
        dz_ref[:, gw:] = (dvg * _gelu_grad(v_pre)).astype(BF16)

    zspec = pl.BlockSpec((ch, 2 * gw), lambda n: (n, 0))
    return _call(
        body, (z_big, da, dz_big, ln_g, ln_b, w_s, b_s_t), comm, name=name, grid=(s // ch,),
        in_specs=[zspec, pl.BlockSpec((ch, gw), lambda n: (n, 0)), pl.BlockSpec(memory_space=HBM),
                  _row_spec(gw), _row_spec(gw), pl.BlockSpec((groups, ch, ch), lambda n: (0, 0, 0)),
                  pl.BlockSpec((ch, groups), lambda n: (0, 0))],
        out_specs=[zspec, pl.BlockSpec((groups, ch, ch), lambda n: (0, 0, 0)),
                   pl.BlockSpec((ch, LANES), lambda n: (0, 0)), _row_spec(gw), _row_spec(gw)],
        out_shape=[jax.ShapeDtypeStruct(dz_big.shape, BF16), jax.ShapeDtypeStruct((groups, ch, ch), F32),
                   jax.ShapeDtypeStruct((ch, LANES), F32), jax.ShapeDtypeStruct((1, gw), F32),
                   jax.ShapeDtypeStruct((1, gw), F32)],
        input_output_aliases={2: 0})


def _mla_prep(z_lat, q_g, kv_g, rope_k, name):
    s, latw = z_lat.shape
    ql, kvl = q_g.shape[1], kv_g.shape[1]
    tb = _div_tile(s, 256, SUBLANES)

    def body(z_ref, qg_ref, kvg_ref, t_ref, qn_ref, kvn_ref, kr_ref):
        q = z_ref[:, :ql]
        qn_ref[...] = ((q * lax.rsqrt(_rowmean(q * q) + EPS)) * qg_ref[...]).astype(BF16)
        kv = z_ref[:, ql:ql + kvl]
        kvn_ref[...] = ((kv * lax.rsqrt(_rowmean(kv * kv) + EPS)) * kvg_ref[...]).astype(BF16)
        kk = z_ref[:, ql + kvl:] * t_ref[...]
        kr_ref[...] = (kk + pltpu.roll(kk, ROPE, axis=1)).astype(BF16)

    return pl.pallas_call(
        body, name=name, grid=(s // tb,),
        in_specs=[pl.BlockSpec((tb, latw), lambda i: (i, 0)), _row_spec(ql), _row_spec(kvl),
                  pl.BlockSpec((tb, 2 * ROPE), lambda i: (i, 0))],
        out_specs=[pl.BlockSpec((tb, ql), lambda i: (i, 0)), pl.BlockSpec((tb, kvl), lambda i: (i, 0)),
                   pl.BlockSpec((tb, 2 * ROPE), lambda i: (i, 0))],
        out_shape=[jax.ShapeDtypeStruct((s, ql), BF16), jax.ShapeDtypeStruct((s, kvl), BF16),
                   jax.ShapeDtypeStruct((s, 2 * ROPE), BF16)],
        compiler_params=_params(),
    )(z_lat, q_g, kv_g, rope_k)


def _scores(q, k, kr, on_diagonal):
    s = _dot(q[:, :NOPE], k, NT) + _dot(q[:, NOPE:], kr, NT)
    if not on_diagonal:
        return s
    rows = lax.broadcasted_iota(jnp.int32, s.shape, 0)
    cols = lax.broadcasted_iota(jnp.int32, s.shape, 1)
    return jnp.where(cols <= rows, s, -1e30)


def _attn_fwd(q, kv, kr, heads, name, comm=None):
    s = q.shape[0]
    t = _div_tile(s, 512)
    nb = s // t
    hp = 2 if heads % 2 == 0 else 1

    def body(q_ref, k_ref, kr_ref, v_ref, o_ref, lse_ref, m_ref, l_ref, acc_ref):
        i, j = pl.program_id(1), pl.program_id(2)

        @pl.when(j == 0)
        def _():
            m_ref[...] = jnp.full(m_ref.shape, -1e30, F32)
            l_ref[...] = jnp.zeros(l_ref.shape, F32)
            acc_ref[...] = jnp.zeros(acc_ref.shape, F32)

        def step(on_diagonal):
            krv = kr_ref[...]
            for h in range(hp):
                vc = slice(h * VHEAD, (h + 1) * VHEAD)
                sc = _scores(q_ref[:, h * HEAD_W:(h + 1) * HEAD_W], k_ref[:, h * NOPE:(h + 1) * NOPE], krv, on_diagonal)
                m_old = m_ref[h]
                m_new = jnp.maximum(m_old, jnp.max(sc, axis=-1, keepdims=True))
                p = jnp.exp(sc - m_new)
                alpha = jnp.exp(m_old - m_new)
                l_new = alpha * l_ref[h] + jnp.sum(p, axis=-1, keepdims=True)
                acc = alpha * acc_ref[:, vc] + _dot(p.astype(BF16), v_ref[:, vc], NN)
                if on_diagonal:
                    o_ref[:, vc] = (acc / l_new).astype(BF16)
                    lse_ref[h] = jnp.broadcast_to(m_new + jnp.log(l_new), (t, LANES))
                else:
                    m_ref[h], l_ref[h], acc_ref[:, vc] = m_new, l_new, acc

        pl.when(j < i)(lambda: step(False))
        pl.when(j == i)(lambda: step(True))

    kidx = lambda off: (lambda h, i, j: (jnp.minimum(i, j), off(h)))
    return _call(
        body, (q, kv, kr, kv), comm, name=name, grid=(heads // hp, nb, nb),
        in_specs=[pl.BlockSpec((t, hp * HEAD_W), lambda h, i, j: (i, h)),
                  pl.BlockSpec((t, hp * NOPE), kidx(lambda h: h)),
                  pl.BlockSpec((t, 2 * ROPE), kidx(lambda h: 0)),
                  pl.BlockSpec((t, hp * VHEAD), kidx(lambda h: heads // hp + h))],
        out_specs=[pl.BlockSpec((t, hp * VHEAD), lambda h, i, j: (i, h)),
                   pl.BlockSpec((hp, t, LANES), lambda h, i, j: (h, i, 0))],
        out_shape=[jax.ShapeDtypeStruct((s, heads * VHEAD), BF16), jax.ShapeDtypeStruct((heads, s, LANES), F32)],
        scratch_shapes=[pltpu.VMEM((hp, t, 1), F32), pltpu.VMEM((hp, t, 1), F32), pltpu.VMEM((t, hp * VHEAD), F32)])


def _attn_bwd(q, kv, kr, o, do, lse, heads, name, comm=None):
    s = q.shape[0]
    t = _div_tile(s, 512)
    nb = s // t
    hp = 2 if heads % 2 == 0 else 1

    def body(q_ref, k_ref, kr_ref, v_ref, o_ref, do_ref, lse_ref, dq_ref, dk_ref, dv_ref, dk_acc, dv_acc):
        j, i = pl.program_id(1), pl.program_id(2)

        @pl.when(jnp.logical_and(j == 0, i == 0))
        def _():
            dq_ref[...] = jnp.zeros(dq_ref.shape, F32)

        def step(on_diagonal):
            krv = kr_ref[...]
            rows = pl.ds(pl.multiple_of(i * t, t), t)
            for h in range(hp):
                qc, kc, vc = (slice(h * w, (h + 1) * w) for w in (HEAD_W, NOPE, VHEAD))
                qv, kn, do_v = q_ref[:, qc], k_ref[:, kc], do_ref[:, vc]
                p = jnp.exp(_scores(qv, kn, krv, on_diagonal) - lse_ref[h][:, :1])
                dp = _dot(do_v, v_ref[:, vc], NT)
                delta = jnp.sum(do_v.astype(F32) * o_ref[:, vc].astype(F32), axis=-1, keepdims=True)
                ds = (p * (dp - delta)).astype(BF16)
                dq_ref[rows, h * HEAD_W:h * HEAD_W + NOPE] += _dot(ds, kn, NN)
                dq_ref[rows, h * HEAD_W + NOPE:(h + 1) * HEAD_W] += _dot(ds, krv, NN)
                dv_part, dk_part = _dot(p.astype(BF16), do_v, TN), _dot(ds, qv, TN)
                if on_diagonal:
                    dv_acc[:, vc], dk_acc[:, qc] = dv_part, dk_part
                else:
                    dv_acc[:, vc] += dv_part
                    dk_acc[:, qc] += dk_part

        pl.when(i == j)(lambda: step(True))
        pl.when(i > j)(lambda: step(False))

        @pl.when(i == nb - 1)
        def _():
            dk_ref[...] = dk_acc[...].astype(BF16)
            dv_ref[...] = dv_acc[...].astype(BF16)

    qidx = lambda h, j, i: (jnp.maximum(i, j), h)
    return _call(
        body, (q, kv, kr, kv, o, do, lse), comm, name=name, grid=(heads // hp, nb, nb),
        in_specs=[pl.BlockSpec((t, hp * HEAD_W), qidx),
                  pl.BlockSpec((t, hp * NOPE), lambda h, j, i: (j, h)),
                  pl.BlockSpec((t, 2 * ROPE), lambda h, j, i: (j, 0)),
                  pl.BlockSpec((t, hp * VHEAD), lambda h, j, i: (j, heads // hp + h)),
                  pl.BlockSpec((t, hp * VHEAD), qidx), pl.BlockSpec((t, hp * VHEAD), qidx),
                  pl.BlockSpec((hp, t, LANES), lambda h, j, i: (h, jnp.maximum(i, j), 0))],
        out_specs=[pl.BlockSpec((s, hp * HEAD_W), lambda h, j, i: (0, h)),
                   pl.BlockSpec((t, hp * HEAD_W), lambda h, j, i: (j, h)),
                   pl.BlockSpec((t, hp * VHEAD), lambda h, j, i: (j, h))],
        out_shape=[jax.ShapeDtypeStruct((s, heads * HEAD_W), F32), jax.ShapeDtypeStruct((s, heads * HEAD_W), BF16),
                   jax.ShapeDtypeStruct((s, heads * VHEAD), BF16)],
        scratch_shapes=[pltpu.VMEM((t, hp * HEAD_W), F32), pltpu.VMEM((t, hp * VHEAD), F32)])


def _mla_bwd_mid(dq, dk, dv, rope_q, rope_k, heads, name):
    s = dq.shape[0]
    tb = _div_tile(s, 256, SUBLANES)

    def body(dq_ref, dk_ref, dv_ref, tq_ref, tk_ref, dqb_ref, dkv_ref, dkk_ref):
        tq = tq_ref[...]
        dkr = jnp.zeros((tb, 2 * ROPE), F32)
        for h in range(heads):
            cols = slice(h * HEAD_W, (h + 1) * HEAD_W)
            dqb_ref[:, cols] = (dq_ref[:, cols] * tq).astype(BF16)
            dkv_ref[:, h * NOPE:(h + 1) * NOPE] = dk_ref[:, h * HEAD_W:h * HEAD_W + NOPE]
            dkr = dkr + dk_ref[:, h * HEAD_W + NOPE:(h + 1) * HEAD_W].astype(F32)
        dkv_ref[:, heads * NOPE:] = dv_ref[...]
        dkk_ref[...] = (dkr + pltpu.roll(dkr, ROPE, axis=1)) * tk_ref[...]

    wq, wv = heads * HEAD_W, heads * VHEAD
    return pl.pallas_call(
        body, name=name, grid=(s // tb,),
        in_specs=[pl.BlockSpec((tb, wq), lambda i: (i, 0)), pl.BlockSpec((tb, wq), lambda i: (i, 0)),
                  pl.BlockSpec((tb, wv), lambda i: (i, 0)), pl.BlockSpec((tb, HEAD_W), lambda i: (i, 0)),
                  pl.BlockSpec((tb, 2 * ROPE), lambda i: (i, 0))],
        out_specs=[pl.BlockSpec((tb, wq), lambda i: (i, 0)), pl.BlockSpec((tb, heads * NOPE + wv), lambda i: (i, 0)),
                   pl.BlockSpec((tb, 2 * ROPE), lambda i: (i, 0))],
        out_shape=[jax.ShapeDtypeStruct((s, wq), BF16), jax.ShapeDtypeStruct((s, heads * NOPE + wv), BF16),
                   jax.ShapeDtypeStruct((s, 2 * ROPE), F32)],
        compiler_params=_params(),
    )(dq, dk, dv, rope_q, rope_k)


def _mla_bwd_post(z_lat, dqn, dkvn, dkk, q_g, kv_g, name):
    s, latw = z_lat.shape
    ql, kvl = q_g.shape[1], kv_g.shape[1]
    tb = _div_tile(s, 256, SUBLANES)

    def norm_bwd(xv, dn, g, dg_ref):
        r = lax.rsqrt(_rowmean(xv * xv) + EPS)
        xh = xv * r
        _accumulate(dg_ref, _colsum(dn * xh))
        dxh = dn * g
        return r * (dxh - xh * _rowmean(dxh * xh))

    def body(z_ref, dqn_ref, dkvn_ref, dkk_ref, qg_ref, kvg_ref, dz_ref, gq_ref, gkv_ref):
        dz_ref[:, :ql] = norm_bwd(z_ref[:, :ql], dqn_ref[...], qg_ref[...], gq_ref).astype(BF16)
        dz_ref[:, ql:ql + kvl] = norm_bwd(z_ref[:, ql:ql + kvl], dkvn_ref[...], kvg_ref[...], gkv_ref).astype(BF16)
        dz_ref[:, ql + kvl:] = dkk_ref[...].astype(BF16)

    return pl.pallas_call(
        body, name=name, grid=(s // tb,),
        in_specs=[pl.BlockSpec((tb, latw), lambda i: (i, 0)), pl.BlockSpec((tb, ql), lambda i: (i, 0)),
                  pl.BlockSpec((tb, kvl), lambda i: (i, 0)), pl.BlockSpec((tb, 2 * ROPE), lambda i: (i, 0)),
                  _row_spec(ql), _row_spec(kvl)],
        out_specs=[pl.BlockSpec((tb, latw), lambda i: (i, 0)), _row_spec(ql), _row_spec(kvl)],
        out_shape=[jax.ShapeDtypeStruct((s, latw), BF16), jax.ShapeDtypeStruct((1, ql), F32),
                   jax.ShapeDtypeStruct((1, kvl), F32)],
        compiler_params=_params(),
    )(z_lat, dqn, dkvn, dkk, q_g, kv_g)


def _shift_down(x, n):
    rows = lax.broadcasted_iota(jnp.int32, x.shape, 0)
    return jnp.where(rows >= n, pltpu.roll(x, n, axis=0), 0.0)


def _shift_up(x, n):
    s = x.shape[0]
    rows = lax.broadcasted_iota(jnp.int32, x.shape, 0)
    return jnp.where(rows < s - n, pltpu.roll(x, s - n, axis=0), 0.0)


def _conv(pre, w_ref, b_ref):
    return (w_ref[2:3, :] * pre + w_ref[1:2, :] * _shift_down(pre, 1) + w_ref[0:1, :] * _shift_down(pre, 2)
            + b_ref[...])


def _conv_fwd(up_pre, conv_w, conv_b, name):
    s, ff2 = up_pre.shape
    ff = ff2 // 2
    tc = _div_tile(ff, 256)
    nb = ff // tc

    def body(pg_ref, pv_ref, wg_ref, wv_ref, bg_ref, bv_ref, act_ref):
        gate = _conv(pg_ref[...].astype(F32), wg_ref, bg_ref)
        val = _conv(pv_ref[...].astype(F32), wv_ref, bv_ref)
        act_ref[...] = (gate * _sigmoid(gate) * val).astype(BF16)

    def col(rows, off):
        return pl.BlockSpec((rows, tc), lambda j: (0, j + off))

    return pl.pallas_call(
        body, name=name, grid=(nb,),
        in_specs=[col(s, 0), col(s, nb), col(CONV_TAPS, 0), col(CONV_TAPS, nb), col(1, 0), col(1, nb)],
        out_specs=col(s, 0), out_shape=jax.ShapeDtypeStruct((s, ff), BF16), compiler_params=_params(),
    )(up_pre, up_pre, conv_w, conv_w, conv_b, conv_b)


def _conv_bwd(up_pre, dact, conv_w, conv_b, name, comm=None):
    s, ff2 = up_pre.shape
    ff = ff2 // 2
    tc = _div_tile(ff, 256)
    nb = ff // tc

    def half(pre, dx, w_ref, dpre_ref, gw_ref, gb_ref):
        gb_ref[...] = _colsum(dx)
        gw_ref[0:1, :] = _colsum(dx * _shift_down(pre, 2))
        gw_ref[1:2, :] = _colsum(dx * _shift_down(pre, 1))
        gw_ref[2:3, :] = _colsum(dx * pre)
        dpre_ref[...] = (w_ref[2:3, :] * dx + w_ref[1:2, :] * _shift_up(dx, 1)
                         + w_ref[0:1, :] * _shift_up(dx, 2)).astype(BF16)

    def body(pg_ref, pv_ref, da_ref, wg_ref, wv_ref, bg_ref, bv_ref, dup_ref, gwg_ref, gwv_ref, gbg_ref, gbv_ref):
        pre_g, pre_v = pg_ref[...].astype(F32), pv_ref[...].astype(F32)
        gate = _conv(pre_g, wg_ref, bg_ref)
        val = _conv(pre_v, wv_ref, bv_ref)
        da = da_ref[...].astype(F32)
        sg = _sigmoid(gate)
        half(pre_v, da * gate * sg, wv_ref, dup_ref.at[1], gwv_ref, gbv_ref)
        half(pre_g, da * val * sg * (1.0 + gate * (1.0 - sg)), wg_ref, dup_ref.at[0], gwg_ref, gbg_ref)

    def col(rows, off):
        return pl.BlockSpec((rows, tc), lambda j: (0, j + off))

    return _call(
        body, (up_pre, up_pre, dact, conv_w, conv_w, conv_b, conv_b), comm, name=name, grid=(nb,),
        in_specs=[col(s, 0), col(s, nb), col(s, 0), col(CONV_TAPS, 0), col(CONV_TAPS, nb), col(1, 0), col(1, nb)],
        out_specs=[pl.BlockSpec((2, s, tc), lambda j: (0, 0, j)), col(CONV_TAPS, 0), col(CONV_TAPS, 0),
                   col(1, 0), col(1, 0)],
        out_shape=[jax.ShapeDtypeStruct((2, s, ff), BF16)] + [jax.ShapeDtypeStruct((CONV_TAPS, ff), F32)] * 2
        + [jax.ShapeDtypeStruct((1, ff), F32)] * 2)


def _ada_fwd(c_all, w, b, name):
    nseq, d = c_all.shape
    na = w.shape[1]
    tn = _div_tile(na, 512)

    def body(c_ref, w_ref, b_ref, o_ref):
        cv = c_ref[...]
        sc = cv * _sigmoid(cv)
        o_ref[...] = jnp.dot(sc, w_ref[...], preferred_element_type=F32, precision=lax.Precision.HIGHEST) + b_ref[...]

    return pl.pallas_call(
        body, name=name, grid=(na // tn,),
        in_specs=[pl.BlockSpec((nseq, d), lambda j: (0, 0)), pl.BlockSpec((d, tn), lambda j: (0, j)),
                  pl.BlockSpec((1, tn), lambda j: (0, j))],
        out_specs=pl.BlockSpec((nseq, tn), lambda j: (0, j)),
        out_shape=jax.ShapeDtypeStruct((nseq, na), F32), compiler_params=_params(),
    )(c_all, w, b)


def _ada_bwd(c_all_t, dmod, name):
    d, nseq = c_all_t.shape
    na = dmod.shape[1]
    tm, tn = _div_tile(d, 256, SUBLANES), _div_tile(na, 512)

    def body(c_ref, dm_ref, o_ref):
        cv = c_ref[...]
        sc = cv * _sigmoid(cv)
        acc = sc[:, 0:1] * dm_ref[0:1, :]
        for bi in range(1, nseq):
            acc = acc + sc[:, bi:bi + 1] * dm_ref[bi:bi + 1, :]
        o_ref[...] = acc

    return pl.pallas_call(
        body, name=name, grid=(d // tm, na // tn),
        in_specs=[pl.BlockSpec((tm, nseq), lambda i, j: (i, 0)), pl.BlockSpec((nseq, tn), lambda i, j: (0, j))],
        out_specs=pl.BlockSpec((tm, tn), lambda i, j: (i, j)),
        out_shape=jax.ShapeDtypeStruct((d, na), F32), compiler_params=_params(),
    )(c_all_t, dmod)


def _adamw(w, g, m, v, name, comm=None, after=None):
    rows, cols = w.shape
    tb = _div_tile(rows, max(SUBLANES, (256 * 1024) // cols // SUBLANES * SUBLANES), SUBLANES)
    c1 = 1.0 / (1.0 - ADAM_B1 ** ADAM_STEP)
    c2 = 1.0 / (1.0 - ADAM_B2 ** ADAM_STEP)

    def body(*refs):
        w_ref, g_ref, m_ref, v_ref = refs[:4]
        d_ref, nm_ref, nv_ref = refs[-3:]
        gv = g_ref[...]
        nm = ADAM_B1 * m_ref[...] + (1.0 - ADAM_B1) * gv
        nv = ADAM_B2 * v_ref[...] + (1.0 - ADAM_B2) * (gv * gv)
        nm_ref[...] = nm
        nv_ref[...] = nv
        d_ref[...] = -ADAM_LR * ((nm * c1) / (jnp.sqrt(nv * c2) + ADAM_EPS) + ADAM_WD * w_ref[...])

    blk = pl.BlockSpec((tb, cols), lambda i: (i, 0))
    operands, in_specs = (w, g, m, v), [blk] * 4
    if after is not None:
        operands, in_specs = operands + (after,), in_specs + [pl.BlockSpec(after.shape, lambda i: (0, 0))]
    return _call(body, operands, comm, name=name, grid=(rows // tb,), in_specs=in_specs, out_specs=[blk] * 3,
                 out_shape=[jax.ShapeDtypeStruct((rows, cols), F32)] * 3)


def _sum_leading(parts, name, after=()):
    n, rows, cols = parts.shape
    tb = _div_tile(rows, 512, SUBLANES)

    def body(p_ref, *rest):
        o_ref = rest[-1]
        acc = p_ref[0]
        for k in range(1, n):
            acc = acc + p_ref[k]
        o_ref[...] = acc

    return pl.pallas_call(
        body, name=name, grid=(rows // tb,),
        in_specs=[pl.BlockSpec((n, tb, cols), lambda i: (0, i, 0))] + [pl.BlockSpec(memory_space=pl.ANY)] * len(after),
        out_specs=pl.BlockSpec((tb, cols), lambda i: (i, 0)),
        out_shape=jax.ShapeDtypeStruct((rows, cols), F32), compiler_params=_params(),
    )(parts, *after)


def _place():
    x, y, c = lax.axis_index("x"), lax.axis_index("y"), lax.axis_index("c")
    return x, y, c, [(1 - x, y), (x, 1 - y), (1 - x, 1 - y)]


def _all_gather(block, name):
    m_per, n = block.shape

    def body(x_ref, out_ref, send_sems, recv_sems, local_sem):
        x, y, c, chips = _place()
        me, sibling = (x, y, c), (x, y, 1 - c)

        def rows(px, py, pc):
            return out_ref.at[pl.ds((4 * px + 2 * py + pc) * m_per, m_per), :]

        def copy(k, blk, to, src=None):
            return pltpu.make_async_remote_copy(
                src_ref=rows(*blk) if src is None else src, dst_ref=rows(*blk), send_sem=send_sems.at[k],
                recv_sem=recv_sems.at[k], device_id=to, device_id_type=MESH)

        mine = pltpu.make_async_copy(x_ref, rows(*me), local_sem)
        mine.start()
        first = [copy(0, me, sibling, src=x_ref)]
        first += [copy(1 + j, me, (*chip, c), src=x_ref) for j, chip in enumerate(chips)]
        for cp in first:
            cp.start()
        passed = [copy(4 + j, (*chip, c), sibling) for j, chip in enumerate(chips)]
        for j, chip in enumerate(chips):
            copy(1 + j, (*chip, c), me).wait_recv()
            passed[j].start()
        copy(0, sibling, me).wait_recv()
        for j, chip in enumerate(chips):
            copy(4 + j, (*chip, 1 - c), me).wait_recv()
        for cp in first + passed:
            cp.wait_send()
        mine.wait()

    return pl.pallas_call(
        body, name=name, out_shape=jax.ShapeDtypeStruct((N_DEV * m_per, n), block.dtype),
        in_specs=[pl.BlockSpec(memory_space=pltpu.VMEM)], out_specs=pl.BlockSpec(memory_space=pltpu.VMEM),
        scratch_shapes=[pltpu.SemaphoreType.DMA((7,)), pltpu.SemaphoreType.DMA((7,)), pltpu.SemaphoreType.DMA],
        compiler_params=_params(),
    )(block)


def _hbm_specs(n):
    return [pl.BlockSpec(memory_space=HBM)] * n


def _part(ref, by_cols, half, quarter=None, lead=None):
    extent = ref.shape[-1] if by_cols else ref.shape[-2]
    size = extent // 2 if quarter is None else extent // 4
    first = half * (extent // 2) + (0 if quarter is None else quarter * size)
    tile = LANES if by_cols else 2 * SUBLANES
    span = pl.ds(pl.multiple_of(first, tile) if size % tile == 0 else first, size)
    index = (slice(None), span) if by_cols else (span, slice(None))
    return ref.at[index] if lead is None else ref.at[(lead,) + index]


def _half_rows(ref, half, lead=None):
    return _part(ref, False, half, lead=lead)


class _Comm:
    def __init__(self, operands, out_shape, sem_dims, build, aliases=None):
        self.operands, self.out_shape, self.sem_dims = list(operands), list(out_shape), list(sem_dims)
        self.scratch = [pltpu.SemaphoreType.DMA(d) for d in sem_dims]
        self.build, self.aliases = build, dict(aliases or {})


class _SemGrid:
    def __init__(self, sems, dims):
        self.sems, self.dims, self.at = list(sems), tuple(dims), self

    def __getitem__(self, index):
        index = index if isinstance(index, tuple) else (index,)
        flat = 0
        for i, d in zip(index, self.dims):
            flat = flat * d + i
        return self.sems[flat]


def _call(body, operands, comm=None, *, name, grid, in_specs, out_specs, out_shape, scratch_shapes=(),
          input_output_aliases=None):
    aliases = dict(input_output_aliases or {})
    if comm is None:
        return pl.pallas_call(
            body, name=name, grid=grid, in_specs=in_specs, out_specs=out_specs, out_shape=out_shape,
            scratch_shapes=list(scratch_shapes), input_output_aliases=aliases, compiler_params=_params())(*operands)
    single = not isinstance(out_shape, (list, tuple))
    outs = [out_shape] if single else list(out_shape)
    ospecs = [out_specs] if single else list(out_specs)
    n_in, n_out, n_scr = len(operands), len(outs), len(scratch_shapes)
    c_in, c_out = len(comm.operands), len(comm.out_shape)
    for i, o in comm.aliases.items():
        aliases[n_in + i] = n_out + o

    def hosted(*refs):
        ins, c_ins = refs[:n_in], refs[n_in:n_in + c_in]
        o0 = n_in + c_in
        o_refs, c_outs = refs[o0:o0 + n_out], refs[o0 + n_out:o0 + n_out + c_out]
        s0 = o0 + n_out + c_out
        scr, sems = refs[s0:s0 + n_scr], refs[s0 + n_scr:]
        stages = comm.build(c_ins, c_outs, sems)
        step, n_steps = 0, 1
        for dim, size in enumerate(grid):
            step, n_steps = step * size + pl.program_id(dim), n_steps * size
        pl.when(step == 0)(stages[0])
        body(*ins, *o_refs, *scr)
        for stage in stages[1:-1]:
            pl.when(step == (n_steps * MIDDLE_STAGE_AT) // 100)(stage)
        pl.when(step == n_steps - 1)(stages[-1])

    res = pl.pallas_call(
        hosted, name=name, grid=grid, in_specs=list(in_specs) + _hbm_specs(c_in),
        out_specs=ospecs + _hbm_specs(c_out), out_shape=outs + comm.out_shape,
        scratch_shapes=list(scratch_shapes) + comm.scratch, input_output_aliases=aliases,
        compiler_params=_params())(*operands, *comm.operands)
    return (res[0] if single else res[:n_out]), res[n_out:]


def _run_comm(comm, name):
    c_in, c_out = len(comm.operands), len(comm.out_shape)

    def body(*refs):
        for stage in comm.build(refs[:c_in], refs[c_in:c_in + c_out], refs[c_in + c_out:]):
            stage()

    return pl.pallas_call(
        body, name=name, in_specs=_hbm_specs(c_in), out_specs=_hbm_specs(c_out), out_shape=comm.out_shape,
        scratch_shapes=comm.scratch, input_output_aliases=comm.aliases, compiler_params=_params())(*comm.operands)


def _gather_comm(shards, by_cols=()):
    nw = len(shards)

    def build(in_refs, out_refs, sems):
        send_sems, recv_sems = sems
        x, y, c, chips = _place()
        me, sibling = (x, y, c), (x, y, 1 - c)
        across_x, across_y, diagonal = chips

        def copy(w, k, block, part, to, src=None):
            dst = _part(out_refs[w], w in by_cols, part[1], part[2] if part[0] else None, 2 * block[0] + block[1])
            return pltpu.make_async_remote_copy(
                src_ref=dst if src is None else src, dst_ref=dst, send_sem=send_sems.at[w, k],
                recv_sem=recv_sems.at[w, k], device_id=to, device_id_type=MESH)

        first = [copy(w, j, (x, y), (0, c), (*chip, c), src=_part(in_refs[w], w in by_cols, c))
                 for w in range(nw) for j, chip in enumerate((across_x, across_y))]
        passed = [[copy(w, 2, across_x, (1, c, 0), (*across_y, c)), copy(w, 3, across_y, (1, c, 1), (*across_x, c)),
                   copy(w, 4, across_x, (0, c), sibling), copy(w, 5, across_y, (0, c), sibling)] for w in range(nw)]
        last = [[copy(w, 6, diagonal, (1, c, 0), sibling), copy(w, 7, diagonal, (1, c, 1), sibling)]
                for w in range(nw)]

        def start():
            for cp in first:
                cp.start()

        def middle():
            for w in range(nw):
                copy(w, 0, across_x, (0, c), me).wait_recv()
                copy(w, 1, across_y, (0, c), me).wait_recv()
                for cp in passed[w]:
                    cp.start()

        def finish():
            for w in range(nw):
                copy(w, 2, diagonal, (1, c, 0), me).wait_recv()
                copy(w, 3, diagonal, (1, c, 1), me).wait_recv()
                for cp in last[w]:
                    cp.start()
            for w in range(nw):
                for k, block, part in ((4, across_x, (0, 1 - c)), (5, across_y, (0, 1 - c)),
                                       (6, diagonal, (1, 1 - c, 0)), (7, diagonal, (1, 1 - c, 1))):
                    copy(w, k, block, part, me).wait_recv()
            for cp in first + sum(passed, []) + sum(last, []):
                cp.wait_send()

        return start, middle, finish

    return _Comm(shards, [jax.ShapeDtypeStruct((N_CHIPS,) + w.shape, w.dtype) for w in shards],
                 [(nw, 8), (nw, 8)], build)


def _halved(shape, by_cols):
    return shape[:-1] + (shape[-1] // 2,) if by_cols else shape[:-2] + (shape[-2] // 2, shape[-1])


def _swap_comm(gs, by_cols=()):
    nw = len(gs)

    def build(in_refs, out_refs, sems):
        send_sems, recv_sems = sems
        x, y, c, _ = _place()
        cps = []
        for w in range(nw):
            cps.append(pltpu.make_async_remote_copy(
                src_ref=_part(in_refs[w], w in by_cols, 1 - c, lead=slice(None)), dst_ref=out_refs[w],
                send_sem=send_sems.at[w], recv_sem=recv_sems.at[w], device_id=(x, y, 1 - c), device_id_type=MESH))

        def start():
            for cp in cps:
                cp.start()

        def finish():
            for cp in cps:
                cp.wait()

        return start, finish

    return _Comm(gs, [jax.ShapeDtypeStruct(_halved(g.shape, w in by_cols), g.dtype) for w, g in enumerate(gs)],
                 [(nw,), (nw,)], build)


def _exchange_comm(s1s):
    nw = len(s1s)

    def build(in_refs, out_refs, sems):
        send_sems, recv_sems = sems
        x, y, c, chips = _place()
        cps = [pltpu.make_async_remote_copy(
            src_ref=in_refs[w].at[2 * chip[0] + chip[1]], dst_ref=out_refs[w].at[j], send_sem=send_sems.at[w, j],
            recv_sem=recv_sems.at[w, j], device_id=(*chip, c), device_id_type=MESH)
            for w in range(nw) for j, chip in enumerate(chips)]

        def start():
            for cp in cps:
                cp.start()

        def finish():
            for cp in cps:
                cp.wait()

        return start, finish

    return _Comm(s1s, [jax.ShapeDtypeStruct((N_CHIPS - 1,) + s.shape[1:], s.dtype) for s in s1s],
                 [(nw, 3), (nw, 3)], build)


def _size(dims):
    n = 1
    for d in dims:
        n *= d
    return n


def _sem_grids(comm, sem_refs):
    grids, pos = [], 0
    for dims in comm.sem_dims:
        grids.append(_SemGrid(sem_refs[pos:pos + _size(dims)], dims))
        pos += _size(dims)
    return grids


def _comm_split_start(comm, name, after=()):
    c_in, c_out = len(comm.operands), len(comm.out_shape)
    counts = [_size(d) for d in comm.sem_dims]
    n_sem = sum(counts)
    assert not comm.aliases

    def body(*refs):
        srcs, lands = refs[:c_in], refs[c_in:c_in + c_out]
        first_sem = c_in + c_out + len(after)
        start, _ = comm.build(srcs, lands, _sem_grids(comm, refs[first_sem:first_sem + n_sem]))
        start()
        refs[-1][...] = jnp.zeros(refs[-1].shape, refs[-1].dtype)

    lands = [pltpu.with_memory_space_constraint(lax.empty(o.shape, o.dtype), HBM) for o in comm.out_shape]
    srcs = [pltpu.with_memory_space_constraint(a, HBM) for a in comm.operands]
    res = pl.pallas_call(
        body, name=name, in_specs=_hbm_specs(c_in + c_out) + [pl.BlockSpec(memory_space=pl.ANY)] * len(after),
        out_specs=[pl.BlockSpec(memory_space=pltpu.SEMAPHORE)] * n_sem + _hbm_specs(c_in + c_out)
        + [pl.BlockSpec(memory_space=pltpu.VMEM)],
        out_shape=[pltpu.SemaphoreType.DMA(())] * n_sem + [pltpu.HBM(a.shape, a.dtype) for a in comm.operands]
        + [pltpu.HBM(o.shape, o.dtype) for o in comm.out_shape] + [jax.ShapeDtypeStruct((SUBLANES, LANES), F32)],
        input_output_aliases={i: n_sem + i for i in range(c_in + c_out)},
        compiler_params=_params(has_side_effects=pltpu.SideEffectType.DATAFLOW_SIDE_EFFECTING))(*srcs, *lands, *after)
    return res[:-1], res[-1]


def _comm_split_wait(comm, state, after, name):
    c_in, c_out, n_sem = len(comm.operands), len(comm.out_shape), sum(_size(d) for d in comm.sem_dims)
    sems, srcs, lands = state[:n_sem], state[n_sem:n_sem + c_in], state[n_sem + c_in:]

    def body(*refs):
        src_refs, land_refs = refs[:c_in], refs[c_in:c_in + c_out]
        _, finish = comm.build(src_refs, land_refs, _sem_grids(comm, refs[c_in + c_out:c_in + c_out + n_sem]))
        finish()

    sem_spec = pl.BlockSpec(memory_space=pltpu.SEMAPHORE)
    res = pl.pallas_call(
        body, name=name, in_specs=_hbm_specs(c_in + c_out) + [sem_spec] * n_sem + [pl.BlockSpec(memory_space=pl.ANY)],
        out_specs=_hbm_specs(c_in + c_out),
        out_shape=[pltpu.HBM(a.shape, a.dtype) for a in srcs] + [pltpu.HBM(o.shape, o.dtype) for o in lands],
        input_output_aliases={i: i for i in range(c_in + c_out)},
        compiler_params=_params(has_side_effects=pltpu.SideEffectType.DATAFLOW_SIDE_EFFECTING),
    )(*srcs, *lands, *sems, after)
    return res[:c_in], res[c_in:]


def _share_comm(fs, by_cols=()):
    nw = len(fs)

    def build(in_refs, out_refs, sems):
        del in_refs
        send_sems, recv_sems = sems
        x, y, c, _ = _place()

        def copy(w, half):
            part = _part(out_refs[w], w in by_cols, half)
            return pltpu.make_async_remote_copy(
                src_ref=part, dst_ref=part, send_sem=send_sems.at[w], recv_sem=recv_sems.at[w],
                device_id=(x, y, 1 - c), device_id_type=MESH)

        sends = [copy(w, c) for w in range(nw)]

        def start():
            for cp in sends:
                cp.start()

        def finish():
            for w in range(nw):
                copy(w, 1 - c).wait_recv()
            for cp in sends:
                cp.wait_send()

        return start, finish

    return _Comm(fs, [jax.ShapeDtypeStruct(f.shape, f.dtype) for f in fs],
                 [(nw,), (nw,)], build,
                 aliases={w: w for w in range(nw)})


def _add_sibling(g, r1, place, name, by_cols=False):
    nch, h, cols = r1.shape
    tr = _div_tile(h, 1024 if by_cols else 256, 2 * SUBLANES)
    nb = h // tr
    mine = (lambda k, i, p: (k, i, p[0])) if by_cols else (lambda k, i, p: (k, p[0] * nb + i, 0))

    def body(place_ref, g_ref, r_ref, o_ref):
        del place_ref
        o_ref[...] = (g_ref[...].astype(F32) + r_ref[...].astype(F32)).astype(BF16)

    spec = pltpu.PrefetchScalarGridSpec(
        num_scalar_prefetch=1, grid=(nch, nb),
        in_specs=[pl.BlockSpec((None, tr, cols), mine), pl.BlockSpec((None, tr, cols), lambda k, i, p: (k, i, 0))],
        out_specs=pl.BlockSpec((None, tr, cols), lambda k, i, p: (k, i, 0)))
    return pl.pallas_call(body, name=name, grid_spec=spec, out_shape=jax.ShapeDtypeStruct((nch, h, cols), BF16),
                          compiler_params=_params())(place, g, r1)


def _add_chips(s1, r2, place, name, by_cols=False):
    _, h, cols = s1.shape
    tr = _div_tile(h, 1024 if by_cols else 256, 2 * SUBLANES)
    nb = h // tr
    mine = (lambda i, p: (i, p[0])) if by_cols else (lambda i, p: (p[0] * nb + i, 0))
    whole = (h, 2 * cols) if by_cols else (2 * h, cols)

    def body(place_ref, s_ref, r_ref, o_ref):
        del place_ref
        acc = s_ref[...].astype(F32)
        for j in range(N_CHIPS - 1):
            acc = acc + r_ref[j].astype(F32)
        o_ref[...] = acc

    spec = pltpu.PrefetchScalarGridSpec(
        num_scalar_prefetch=1, grid=(nb,),
        in_specs=[pl.BlockSpec((None, tr, cols), lambda i, p: (p[1], i, 0)),
                  pl.BlockSpec((N_CHIPS - 1, tr, cols), lambda i, p: (0, i, 0))],
        out_specs=pl.BlockSpec((tr, cols), mine))
    return pl.pallas_call(body, name=name, grid_spec=spec, out_shape=jax.ShapeDtypeStruct(whole, F32),
                          compiler_params=_params())(place, s1, r2)


def _quarter_turn(m):
    h = m.shape[-1] // 2
    return jnp.concatenate([-m[..., h:], m[..., :h]], axis=-1)


def _quarter_turn_back(m):
    h = m.shape[-1] // 2
    return jnp.concatenate([m[..., h:], -m[..., :h]], axis=-1)


def _stack_rows(parts):
    out = lax.empty((sum(p.shape[0] for p in parts),) + parts[0].shape[1:], parts[0].dtype)
    row = 0
    for p in parts:
        out = lax.dynamic_update_slice(out, p, (row, 0))
        row += p.shape[0]
    return out


def _join_cols(sh):
    return jnp.concatenate([sh[k] for k in range(N_CHIPS)], axis=1)


def _split_cols(full):
    c = full.shape[1] // N_CHIPS
    return jnp.stack([full[:, k * c:(k + 1) * c] for k in range(N_CHIPS)])


def kernel(x, c, positions, w_ada, b_ada, pre_norm1_g, w_in, gm_ln_g, gm_ln_b, gm_w_s, gm_b_s, w_branch_a, q_norm_g, w_uq, kv_norm_g, w_ukv, w_branch_b, w_out, post_norm1_g, pre_norm2_g, w_up, conv_w, conv_b, w_down, post_norm2_g, loss_target, m_w_ada, m_b_ada, m_pre_norm1_g, m_w_in, m_gm_ln_g, m_gm_ln_b, m_gm_w_s, m_gm_b_s, m_w_branch_a, m_q_norm_g, m_w_uq, m_kv_norm_g, m_w_ukv, m_w_branch_b, m_w_out, m_post_norm1_g, m_pre_norm2_g, m_w_up, m_conv_w, m_conv_b, m_w_down, m_post_norm2_g, v_w_ada, v_b_ada, v_pre_norm1_g, v_w_in, v_gm_ln_g, v_gm_ln_b, v_gm_w_s, v_gm_b_s, v_w_branch_a, v_q_norm_g, v_w_uq, v_kv_norm_g, v_w_ukv, v_w_branch_b, v_w_out, v_post_norm1_g, v_pre_norm2_g, v_w_up, v_conv_w, v_conv_b, v_w_down, v_post_norm2_g):
    given = dict(locals())
    s, d = x.shape[1], x.shape[2]
    gw = gm_ln_g.shape[0]
    ql, kvl = q_norm_g.shape[0], kv_norm_g.shape[0]
    heads = N_CHIPS * w_uq.shape[1] // (NOPE + ROPE)
    ff = N_CHIPS * w_down.shape[0]
    assert gw == d and N_CHIPS * w_ukv.shape[1] == heads * (NOPE + VHEAD)
    ix, iy, ic = lax.axis_index("x"), lax.axis_index("y"), lax.axis_index("c")
    chip = 2 * ix + iy
    dev = 2 * chip + ic
    row = lambda v: v.reshape(1, -1)

    c_all = _all_gather(jnp.pad(c, ((0, SUBLANES - 1), (0, 0))), "gather_c").reshape(N_DEV, SUBLANES, d)[:, 0]
    na = w_ada.shape[1]
    b_ada_mine = lax.dynamic_slice(b_ada, (chip * na,), (na,))
    mod_cols = _ada_fwd(c_all, w_ada, row(b_ada_mine), "ada_fwd")
    mod_all = _all_gather(mod_cols, "gather_mod").reshape(N_CHIPS, N_CORES, N_DEV, na)[:, 0]
    mod = lax.dynamic_index_in_dim(mod_all, dev, axis=1, keepdims=False).reshape(N_MOD, d)
    shift1, scale1, gate1, shift2, scale2, gate2 = (mod[i:i + 1] for i in range(N_MOD))

    mine = {n: (given[n].T if n == "w_in" else given[n]).astype(BF16) for n in BIG}
    gather = lambda names: _gather_comm([mine[n] for n in names], [i for i, n in enumerate(names) if n == "w_in"])
    whole = lambda n, g: lax.dynamic_update_slice(g, mine[n][None], (chip, 0, 0))
    rows4 = lambda sh4: sh4.reshape(-1, sh4.shape[2])
    wi_t = rows4(whole("w_in", _run_comm(gather(["w_in"]), "gather_w_in")[0]))
    o_q, o_kv, o_pe, o_ga = 2 * gw, 2 * gw + ql, 2 * gw + ql + kvl, 2 * gw + ql + kvl + ROPE
    w_in_big_t = _stack_rows([wi_t[:o_q], wi_t[o_ga:]])
    w_in_lat_t = _stack_rows([wi_t[o_q:o_ga], _quarter_turn(wi_t[o_pe:o_ga].T).T])

    inv = ROPE_THETA ** (-jnp.arange(0, ROPE, 2, dtype=F32) / ROPE)
    ang = positions[0].astype(F32)[:, None] * inv
    cos, sin = jnp.cos(ang), jnp.sin(ang)
    rope_k = jnp.concatenate([cos, cos, sin, sin], axis=1)
    softmax_scale = float(NOPE + ROPE) ** -0.5
    rope_q = jnp.concatenate([jnp.ones((s, NOPE), F32), rope_k], axis=1) * softmax_scale

    x2d, tgt = x[0], loss_target[0]
    g_pre1, g_post1, g_pre2, g_post2 = row(pre_norm1_g), row(post_norm1_g), row(pre_norm2_g), row(post_norm2_g)
    ln_g, ln_b, q_g, kv_g = row(gm_ln_g), row(gm_ln_b), row(q_norm_g), row(kv_norm_g)
    b_s_t = gm_b_s.T
    conv_wf = _all_gather(jnp.pad(conv_w, ((0, SUBLANES - CONV_TAPS), (0, 0))), "gather_conv_w")
    conv_wf = conv_wf.reshape(N_CHIPS, N_CORES, SUBLANES, conv_w.shape[1])[:, 0, :CONV_TAPS]
    conv_wf = conv_wf.transpose(1, 0, 2).reshape(CONV_TAPS, 2 * ff)
    conv_bf = row(conv_b)

    h1 = _prenorm(x2d, g_pre1, scale1, shift1, "prenorm1")
    z_big, (g_uq, g_ukv, g_a) = _matmul(h1, w_in_big_t, mode="nt", out_dtype=F32, name="mm_z_big", tm=s,
                                        comm=gather(["w_uq", "w_ukv", "w_branch_a"]))
    wq = _join_cols(whole("w_uq", g_uq)).reshape(ql, heads, NOPE + ROPE)
    w_q = jnp.concatenate([wq, _quarter_turn(wq[:, :, NOPE:])], axis=2).reshape(ql, heads * HEAD_W)
    w_kv = _join_cols(whole("w_ukv", g_ukv)).reshape(kvl, heads, 2, NOPE).transpose(0, 2, 1, 3)
    w_kv = w_kv.reshape(kvl, 2 * heads * NOPE)
    w_a = rows4(whole("w_branch_a", g_a))
    z_lat = _matmul(h1, w_in_lat_t, mode="nt", out_dtype=F32, name="mm_z_lat", tm=s, tn=1024)
    a_act = _gmlp_fwd(z_big, ln_g, ln_b, gm_w_s, b_s_t, "gmlp_fwd")
    qn, kvn, kr = _mla_prep(z_lat, q_g, kv_g, rope_k, "mla_prep")
    q_rot = _matmul(qn, w_q, mode="nn", out_dtype=BF16, name="mm_q", tm=s, tn=HEAD_W, mul=rope_q)
    kv_all = _matmul(kvn, w_kv, mode="nn", out_dtype=BF16, name="mm_kv", tm=s, tn=1024)
    (o_att, lse), (g_b, g_o, g_up) = _attn_fwd(q_rot, kv_all, kr, heads, "attn_fwd",
                                               comm=gather(["w_branch_b", "w_out", "w_up"]))
    w_b, w_o, w_upf = rows4(whole("w_branch_b", g_b)), rows4(whole("w_out", g_o)), whole("w_up", g_up)
    y_a = _matmul(a_act, w_a, mode="nn", out_dtype=F32, name="mm_y_a", tm=s)
    y_b = _matmul(o_att, w_b, mode="nn", out_dtype=F32, name="mm_y_b", tm=s)
    merged = _merge(z_big, y_a, y_b, "merge")
    y1 = _matmul(merged, w_o, mode="nn", out_dtype=F32, name="mm_y1", tm=s)
    x1, h2 = _post_pre(x2d, y1, gate1, g_post1, g_pre2, scale2, shift2, "post1_pre2")

    up_pre, (g_dn,) = _matmul(h2, w_upf, mode="nn", out_dtype=BF16, name="mm_up", tm=s, tn=1408,
                              comm=gather(["w_down"]))
    w_dn = rows4(whole("w_down", g_dn))
    act = _conv_fwd(up_pre, conv_wf, conv_bf, "conv_fwd")
    ffn = _matmul(act, w_dn, mode="nn", out_dtype=F32, name="mm_ffn", tm=s, tk=1408)

    dffn, dgate2, g_post2_grad, dx2, loss_part = _post_bwd(ffn, gate2, g_post2, "post2_bwd", xin=x1, target=tgt)
    loss = lax.psum(loss_part[0, 0], ("x", "y", "c"))
    place = jnp.stack([ic, chip]).astype(jnp.int32)
    rows_of = lambda g: g.reshape(N_CHIPS, g.shape[0] // N_CHIPS, g.shape[1])
    add_sibling = lambda names, gs, r1s: [_add_sibling(g, r1, place, "rs_add_sibling_" + n, by_cols=n == "w_in")
                                          for n, g, r1 in zip(names, gs, r1s)]
    add_chips = lambda names, s1s, r2s: [_add_chips(s1, r2, place, "rs_add_chips_" + n, by_cols=n == "w_in")
                                         for n, s1, r2 in zip(names, s1s, r2s)]
    dact = _matmul(dffn, w_dn, mode="nt", out_dtype=BF16, name="mm_dact", tm=s)
    gp_down = [rows_of(_matmul(act, dffn, mode="tn", out_dtype=BF16, name="mm_gw_down", tn=1024, tk=s))]
    (dup, gcw_g, gcw_v, gcb_g, gcb_v), r1_down = _conv_bwd(up_pre, dact, conv_wf, conv_bf, "conv_bwd",
                                                            comm=_swap_comm(gp_down))
    s1_down = add_sibling(["w_down"], gp_down, r1_down)
    dh2, r2_down = _matmul(dup, w_upf, mode="nt", out_dtype=F32, name="mm_dh2", tm=s, tk=1408,
                           comm=_exchange_comm(s1_down))
    half_down = add_chips(["w_down"], s1_down, r2_down)
    gw_up = _matmul(h2, dup, mode="tn", out_dtype=BF16, name="mm_gw_up", tn=1408, tk=s, out_groups=N_CHIPS)
    dx1, dshift2, dscale2, g_pre2_grad = _prenorm_bwd(x1, dh2, dx2, g_pre2, scale2, "prenorm2_bwd")

    dy1, dgate1, g_post1_grad = _post_bwd(y1, gate1, g_post1, "post1_bwd", dxo=dx1)
    dmerged = _matmul(dy1, w_o, mode="nt", out_dtype=F32, name="mm_dmerged", tm=s)
    gw_out = _matmul(merged, dy1, mode="tn", out_dtype=BF16, name="mm_gw_out", tn=1024, tk=s)
    dy_a, dy_b, dz_big = _merge_bwd(dmerged, z_big, y_a, y_b, "merge_bwd")
    da = _matmul(dy_a, w_a, mode="nt", out_dtype=F32, name="mm_da", tm=s)
    gw_a = _matmul(a_act, dy_a, mode="tn", out_dtype=BF16, name="mm_gw_a", tn=1024, tk=s)
    do = _matmul(dy_b, w_b, mode="nt", out_dtype=BF16, name="mm_do", tm=s)
    gw_b = _matmul(o_att, dy_b, mode="tn", out_dtype=BF16, name="mm_gw_b", tn=1024, tk=s)
    mid = ["w_up", "w_out", "w_branch_a", "w_branch_b"]
    gp_mid = [gw_up, rows_of(gw_out), rows_of(gw_a), rows_of(gw_b)]
    (dz_big, g_ws, g_bs_t, g_ln_g, g_ln_b), r1_mid = _gmlp_bwd(z_big, da, dz_big, ln_g, ln_b, gm_w_s, b_s_t,
                                                                "gmlp_bwd", comm=_swap_comm(gp_mid))
    s1_mid = add_sibling(mid, gp_mid, r1_mid)
    (dq, dk, dv), r2_up_out = _attn_bwd(q_rot, kv_all, kr, o_att, do, lse, heads, "attn_bwd",
                                        comm=_exchange_comm(s1_mid[:2]))
    dq_big, dkv, dkk = _mla_bwd_mid(dq, dk, dv, rope_q, rope_k, heads, "mla_bwd_mid")
    gw_q = _matmul(qn, dq_big, mode="tn", out_dtype=F32, name="mm_gw_q", tn=1024, tk=s)
    dqn = _matmul(dq_big, w_q, mode="nt", out_dtype=F32, name="mm_dqn", tm=s, tk=1024)
    gw_kv = _matmul(kvn, dkv, mode="tn", out_dtype=BF16, name="mm_gw_kv", tn=1024, tk=s)
    dkvn = _matmul(dkv, w_kv, mode="nt", out_dtype=F32, name="mm_dkvn", tm=s, tk=1024)
    dz_lat, g_q, g_kv = _mla_bwd_post(z_lat, dqn, dkvn, dkk, q_g, kv_g, "mla_bwd_post")

    partial = {
        "gm_ln_g": g_ln_g, "gm_ln_b": g_ln_b, "gm_w_s": g_ws, "gm_b_s": g_bs_t[:, :gm_b_s.shape[0]].T,
        "q_norm_g": g_q, "kv_norm_g": g_kv, "post_norm1_g": g_post1_grad, "pre_norm2_g": g_pre2_grad,
        "conv_w": jnp.concatenate([gcw_g, gcw_v], axis=1), "conv_b": jnp.concatenate([gcb_g, gcb_v], axis=1),
        "post_norm2_g": g_post2_grad,
    }
    flat = jnp.concatenate([partial[n].reshape(-1) for n in SMALL_PARTIAL])
    n_small = flat.shape[0]
    rows_small = -(-n_small // (LANES * SMALL_ROW_TILE)) * SMALL_ROW_TILE
    flat = jnp.pad(flat, (0, rows_small * LANES - n_small)).reshape(rows_small, LANES)
    def small_pack(prefix, source):
        v = jnp.concatenate([source[prefix + n].reshape(-1) for n in SMALL])
        rows = -(-v.shape[0] // (LANES * SUBLANES)) * SUBLANES
        return jnp.pad(v, (0, rows * LANES - v.shape[0])).reshape(rows, LANES)

    small_state = [small_pack(prefix, given) for prefix in ("", "m_", "v_")]
    small_sum = _sum_leading(_all_gather(flat, "gather_small").reshape(N_DEV, rows_small, LANES), "sum_small",
                             after=small_state)
    small_sum = small_sum.reshape(-1)
    small_grads, off = {}, 0
    for n in SMALL_PARTIAL:
        shape = (CONV_TAPS, 2 * ff) if n == "conv_w" else given[n].shape
        small_grads[n] = small_sum[off:off + partial[n].size].reshape(shape)
        off += partial[n].size
    small_grads["conv_w"] = lax.dynamic_slice(small_grads["conv_w"], (0, chip * conv_w.shape[1]), conv_w.shape)

    dh1, r2_a_b = _matmul(dz_big, w_in_big_t, mode="nn", out_dtype=F32, name="mm_dh1_big", tm=s, tk=1024,
                          comm=_exchange_comm(s1_mid[2:]))
    half_mid = add_chips(mid, s1_mid, list(r2_up_out) + list(r2_a_b))
    dh1 = _matmul(dz_lat, w_in_lat_t, mode="nn", out_dtype=F32, name="mm_dh1_lat", tm=s, tk=1024, add=dh1)
    gw_big_t, shared = _matmul(dz_big, h1, mode="tn", out_dtype=BF16, name="mm_gw_in_big", tn=2048, tk=s,
                               comm=_share_comm(half_down + half_mid))
    grads = dict(zip(["w_down"] + mid, shared), **small_grads)
    gw_lat_t = _matmul(dz_lat, h1, mode="tn", out_dtype=F32, name="mm_gw_in_lat", tm=1024, tn=1024, tk=s)

    gq = gw_q.reshape(ql, heads, HEAD_W)
    gq_pe = gq[:, :, NOPE:NOPE + ROPE] + _quarter_turn_back(gq[:, :, NOPE + ROPE:])
    g_pe_t = gw_lat_t[ql + kvl:ql + kvl + ROPE] + _quarter_turn_back(gw_lat_t[ql + kvl + ROPE:].T).T
    last = ["w_in", "w_uq", "w_ukv"]
    gw_in_t = _stack_rows([gw_big_t[:o_q], gw_lat_t[:ql + kvl].astype(BF16), g_pe_t.astype(BF16), gw_big_t[o_q:]])
    gp_last = [
        gw_in_t.reshape(N_CHIPS, gw_in_t.shape[0] // N_CHIPS, d),
        _split_cols(jnp.concatenate([gq[:, :, :NOPE], gq_pe], axis=2).reshape(ql, heads * (NOPE + ROPE)).astype(BF16)),
        _split_cols(gw_kv.reshape(kvl, 2, heads, NOPE).transpose(0, 2, 1, 3).reshape(kvl, heads * 2 * NOPE)),
    ]
    (grad_x, dshift1, dscale1, g_pre1_grad), r1_last = _prenorm_bwd(x2d, dh1, dx1, g_pre1, scale1, "prenorm1_bwd",
                                                                    comm=_swap_comm(gp_last, by_cols=[0]))
    s1_last = add_sibling(last, gp_last, r1_last)

    dmod = jnp.concatenate([dshift1, dscale1, dgate1, dshift2, dscale2, dgate2, g_pre1_grad], axis=1)
    dmod_all = _all_gather(jnp.pad(dmod, ((0, SUBLANES - 1), (0, 0))), "gather_dmod")
    dmod_all = dmod_all.reshape(N_DEV, SUBLANES, (N_MOD + 1) * d)[:, 0]
    dmod_sum = _sum_leading(dmod_all.reshape(N_DEV, 1, (N_MOD + 1) * d), "sum_dmod")[0]
    grads["b_ada"], grads["pre_norm1_g"] = dmod_sum[:N_MOD * d], dmod_sum[N_MOD * d:]
    dmod_mine = lax.dynamic_slice(dmod_all, (0, chip * na), (N_DEV, na))
    grads["w_ada"] = _ada_bwd(c_all.T, dmod_mine, "ada_bwd")

    delta, new_m, new_v = {}, {}, {}

    def adamw(n, after=None):
        turn = (lambda a: a.T) if n == "w_in" else (lambda a: a)
        outs = _adamw(turn(given[n]), grads[n], turn(given["m_" + n]), turn(given["v_" + n]), "adamw_" + n,
                      after=after)
        grads[n] = turn(grads[n])
        delta[n], new_m[n], new_v[n] = (turn(o) for o in outs)

    exchange_last = _exchange_comm(s1_last)
    in_flight, token = _comm_split_start(exchange_last, "rs_exchange_last_start", after=[dmod_sum, small_sum])
    for n in ["w_ada", "w_down"] + mid:
        adamw(n, after=token)
    s1_last, r2_last = _comm_split_wait(exchange_last, in_flight, delta[mid[-1]], "rs_exchange_last_wait")
    half_last = add_chips(last, s1_last, r2_last)
    grads.update(zip(last, _run_comm(_share_comm(half_last, by_cols=[0]), "rs_share_last")))
    for n in last:
        adamw(n)

    outs = _adamw(small_state[0], small_pack("", grads), small_state[1], small_state[2], "adamw_small")
    off = 0
    for n in SMALL:
        size = given[n].size
        for store, packed_out in zip((delta, new_m, new_v), outs):
            store[n] = packed_out.reshape(-1)[off:off + size].reshape(given[n].shape)
        off += size

    return (loss, grad_x[None], *[grads[n] for n in WEIGHTS], *[delta[n] for n in WEIGHTS],
            *[new_m[n] for n in WEIGHTS], *[new_v[n] for n in WEIGHTS])
```

```python
import functools

import jax
import jax.numpy as jnp
from jax import lax
from jax.experimental import pallas as pl
from jax.experimental.pallas import tpu as pltpu

F32 = jnp.float32
BF16 = jnp.bfloat16
MESH = pl.DeviceIdType.MESH
HBM = pltpu.HBM

EPS = 1e-6
NOPE, ROPE, VHEAD = 128, 64, 128
HEAD_W = NOPE + 2 * ROPE
ROPE_THETA = 10000.0
CONV_TAPS = 3
N_MOD = 6
N_CHIPS, N_CORES, N_DEV = 4, 2, 8
ADAM_LR, ADAM_B1, ADAM_B2, ADAM_EPS, ADAM_WD, ADAM_STEP = 0.001, 0.9, 0.999, 1e-08, 0.01, 10

LANES = 128
SUBLANES = 8
VMEM_LIMIT = 56 * 2**20
MIDDLE_STAGE_AT = 70
SMALL_ROW_TILE = 256

BIG = ("w_in", "w_branch_a", "w_uq", "w_ukv", "w_branch_b", "w_out", "w_up", "w_down")
WEIGHTS = ("w_ada", "b_ada", "pre_norm1_g", "w_in", "gm_ln_g", "gm_ln_b", "gm_w_s", "gm_b_s", "w_branch_a",
           "q_norm_g", "w_uq", "kv_norm_g", "w_ukv", "w_branch_b", "w_out", "post_norm1_g", "pre_norm2_g",
           "w_up", "conv_w", "conv_b", "w_down", "post_norm2_g")
SMALL_PARTIAL = ("gm_ln_g", "gm_ln_b", "gm_w_s", "gm_b_s", "q_norm_g", "kv_norm_g", "post_norm1_g",
                 "pre_norm2_g", "conv_w", "conv_b", "post_norm2_g")
SMALL = ("b_ada", "pre_norm1_g") + SMALL_PARTIAL


def _div_tile(n, cap, mult=LANES):
    t = (min(cap, n) // mult) * mult
    while t >= mult:
        if n % t == 0:
            return t
        t -= mult
    return n


def _params(**kw):
    return pltpu.CompilerParams(vmem_limit_bytes=VMEM_LIMIT, **kw)


def _row_spec(width):
    return pl.BlockSpec((1, width), lambda *_: (0, 0))


def _gelu(x):
    k = 0.7978845608028654
    return 0.5 * x * (1.0 + jnp.tanh(k * (x + 0.044715 * x * x * x)))


def _gelu_grad(x):
    k = 0.7978845608028654
    t = jnp.tanh(k * (x + 0.044715 * x * x * x))
    return 0.5 * (1.0 + t) + 0.5 * x * (1.0 - t * t) * k * (1.0 + 3.0 * 0.044715 * x * x)


def _sigmoid(x):
    return 0.5 * jnp.tanh(0.5 * x) + 0.5


def _dot(a, b, dims):
    return lax.dot_general(a, b, (dims, ((), ())), preferred_element_type=F32)


NN = ((1,), (0,))
NT = ((1,), (1,))
TN = ((0,), (0,))


def _logical(arr):
    if arr.ndim == 2:
        return arr.shape[0], arr.shape[1], arr.shape[1]
    return arr.shape[1], arr.shape[0] * arr.shape[2], arr.shape[2]


def _tile_spec(ndim, group_w, blk_rows, blk_cols, row_of, col_of):
    if ndim == 2:
        return pl.BlockSpec((blk_rows, blk_cols), lambda i, j, k: (row_of(i, j, k), col_of(i, j, k)))
    per = group_w // blk_cols
    return pl.BlockSpec((None, blk_rows, blk_cols),
                        lambda i, j, k: (col_of(i, j, k) // per, row_of(i, j, k), col_of(i, j, k) % per))


def _matmul(a, b, *, mode, out_dtype, name, tm=512, tn=512, tk=2048, mul=None, add=None, out_groups=None, comm=None):
    ar, ac, agw = _logical(a)
    br, bc, bgw = _logical(b)
    if mode == "nn":
        m, kd, n = ar, ac, bc
        m_w, k_w, n_w = (), (agw,), (bgw,)
    elif mode == "nt":
        m, kd, n = ar, ac, br
        m_w, k_w, n_w = (), (agw, bgw), ()
    else:
        m, kd, n = ac, ar, bc
        m_w, k_w, n_w = (agw,), (), (bgw,)
    if out_groups is not None:
        n_w = n_w + (n // out_groups,)
    tm = _div_tile(min((m,) + m_w), tm, LANES if mode == "tn" else SUBLANES)
    tn = _div_tile(min((n,) + n_w), tn)
    tk = _div_tile(min((kd,) + k_w), tk)
    assert all(w % tn == 0 for w in n_w) and all(w % tk == 0 for w in k_w) and all(w % tm == 0 for w in m_w)
    nk = kd // tk
    dims = {"nn": NN, "nt": NT, "tn": TN}[mode]
    gi, gj, gk = (lambda i, j, k: i), (lambda i, j, k: j), (lambda i, j, k: k)
    if mode == "nn":
        a_spec = _tile_spec(a.ndim, agw, tm, tk, gi, gk)
        b_spec = _tile_spec(b.ndim, bgw, tk, tn, gk, gj)
    elif mode == "nt":
        a_spec = _tile_spec(a.ndim, agw, tm, tk, gi, gk)
        b_spec = _tile_spec(b.ndim, bgw, tn, tk, gj, gk)
    else:
        a_spec = _tile_spec(a.ndim, agw, tk, tm, gk, gi)
        b_spec = _tile_spec(b.ndim, bgw, tk, tn, gk, gj)
    in_specs, operands = [a_spec, b_spec], [a, b]
    if mul is not None:
        assert mul.shape == (m, tn)
        in_specs.append(pl.BlockSpec((tm, tn), lambda i, j, k: (i, 0)))
        operands.append(mul)
    if add is not None:
        in_specs.append(pl.BlockSpec((tm, tn), lambda i, j, k: (i, j)))
        operands.append(add)

    def body(*refs):
        a_ref, b_ref = refs[0], refs[1]
        pos = 2
        mul_ref = add_ref = None
        if mul is not None:
            mul_ref, pos = refs[pos], pos + 1
        if add is not None:
            add_ref, pos = refs[pos], pos + 1
        o_ref = refs[pos]

        def finish(r):
            if mul_ref is not None:
                r = r * mul_ref[...]
            if add_ref is not None:
                r = r + add_ref[...]
            o_ref[...] = r.astype(out_dtype)

        part = _dot(a_ref[...], b_ref[...], dims)
        if nk == 1:
            finish(part)
        else:
            acc_ref = refs[pos + 1]
            k = pl.program_id(2)

            @pl.when(k == 0)
            def _():
                acc_ref[...] = part

            @pl.when(k > 0)
            def _():
                acc_ref[...] += part

            @pl.when(k == nk - 1)
            def _():
                finish(acc_ref[...])

    if out_groups is None:
        out_spec, out_dims = _tile_spec(2, n, tm, tn, gi, gj), (m, n)
    else:
        out_spec, out_dims = _tile_spec(3, n // out_groups, tm, tn, gi, gj), (out_groups, m, n // out_groups)
    return _call(body, operands, comm, name=name, grid=(m // tm, n // tn, nk), in_specs=in_specs, out_specs=out_spec,
                 out_shape=jax.ShapeDtypeStruct(out_dims, out_dtype),
                 scratch_shapes=[] if nk == 1 else [pltpu.VMEM((tm, tn), F32)])


def _accumulate(ref, value):
    @pl.when(pl.program_id(0) == 0)
    def _():
        ref[...] = value

    @pl.when(pl.program_id(0) > 0)
    def _():
        ref[...] += value


def _colsum(v):
    return jnp.sum(v, axis=0, keepdims=True)


def _rowmean(v):
    return jnp.mean(v, axis=-1, keepdims=True)


def _prenorm(x, g, scale, shift, name):
    s, d = x.shape
    tb = _div_tile(s, 256, SUBLANES)

    def body(x_ref, g_ref, sc_ref, sh_ref, h_ref):
        xv = x_ref[...]
        r = lax.rsqrt(_rowmean(xv * xv) + EPS)
        h_ref[...] = ((xv * r) * g_ref[...] * (1.0 + sc_ref[...]) + sh_ref[...]).astype(BF16)

    blk = pl.BlockSpec((tb, d), lambda i: (i, 0))
    return pl.pallas_call(
        body, name=name, grid=(s // tb,), in_specs=[blk, _row_spec(d), _row_spec(d), _row_spec(d)],
        out_specs=blk, out_shape=jax.ShapeDtypeStruct((s, d), BF16), compiler_params=_params(),
    )(x, g, scale, shift)


def _post_pre(x, y, gate, pg, g2, scale2, shift2, name):
    s, d = x.shape
    tb = _div_tile(s, 256, SUBLANES)

    def body(x_ref, y_ref, gate_ref, pg_ref, g2_ref, sc_ref, sh_ref, x1_ref, h2_ref):
        yv = y_ref[...]
        rp = lax.rsqrt(_rowmean(yv * yv) + EPS)
        x1 = x_ref[...] + gate_ref[...] * ((yv * rp) * pg_ref[...])
        x1_ref[...] = x1
        r2 = lax.rsqrt(_rowmean(x1 * x1) + EPS)
        h2_ref[...] = ((x1 * r2) * g2_ref[...] * (1.0 + sc_ref[...]) + sh_ref[...]).astype(BF16)

    blk = pl.BlockSpec((tb, d), lambda i: (i, 0))
    return pl.pallas_call(
        body, name=name, grid=(s // tb,), in_specs=[blk, blk] + [_row_spec(d)] * 5,
        out_specs=[blk, blk],
        out_shape=[jax.ShapeDtypeStruct((s, d), F32), jax.ShapeDtypeStruct((s, d), BF16)],
        compiler_params=_params(),
    )(x, y, gate, pg, g2, scale2, shift2)


def _post_bwd(y, gate, pg, name, *, dxo=None, xin=None, target=None):
    s, d = y.shape
    tb = _div_tile(s, 256, SUBLANES)
    from_loss = target is not None

    def body(*refs):
        if from_loss:
            y_ref, gate_ref, pg_ref, xin_ref, t_ref, dy_ref, dgate_ref, dpg_ref, dxo_ref, loss_ref = refs
        else:
            y_ref, gate_ref, pg_ref, dxo_in_ref, dy_ref, dgate_ref, dpg_ref = refs
        yv = y_ref[...]
        rp = lax.rsqrt(_rowmean(yv * yv) + EPS)
        yh = yv * rp
        fn = yh * pg_ref[...]
        gate = gate_ref[...]
        if from_loss:
            err = xin_ref[...] + gate * fn - t_ref[...]
            dxo = err * (1.0 / d)
            dxo_ref[...] = dxo
            part = 0.5 * jnp.sum(_rowmean(err * err), axis=0, keepdims=True)
            _accumulate(loss_ref, jnp.broadcast_to(part, loss_ref.shape))
        else:
            dxo = dxo_in_ref[...]
        _accumulate(dgate_ref, _colsum(dxo * fn))
        dfn = dxo * gate
        _accumulate(dpg_ref, _colsum(dfn * yh))
        dyh = dfn * pg_ref[...]
        dy_ref[...] = (rp * (dyh - yh * _rowmean(dyh * yh))).astype(BF16)

    blk = pl.BlockSpec((tb, d), lambda i: (i, 0))
    in_specs = [blk, _row_spec(d), _row_spec(d)]
    out_specs = [blk, _row_spec(d), _row_spec(d)]
    out_shape = [jax.ShapeDtypeStruct((s, d), BF16), jax.ShapeDtypeStruct((1, d), F32),
                 jax.ShapeDtypeStruct((1, d), F32)]
    if from_loss:
        operands = (y, gate, pg, xin, target)
        in_specs += [blk, blk]
        out_specs += [blk, _row_spec(LANES)]
        out_shape += [jax.ShapeDtypeStruct((s, d), F32), jax.ShapeDtypeStruct((1, LANES), F32)]
    else:
        operands = (y, gate, pg, dxo)
        in_specs += [blk]
    return pl.pallas_call(
        body, name=name, grid=(s // tb,), in_specs=in_specs, out_specs=out_specs, out_shape=out_shape,
        compiler_params=_params(),
    )(*operands)


def _prenorm_bwd(xin, dh, dres, g, scale, name, comm=None):
    s, d = xin.shape
    tb = _div_tile(s, 256, SUBLANES)

    def body(x_ref, dh_ref, dres_ref, g_ref, sc_ref, dx_ref, dshift_ref, dscale_ref, dg_ref):
        xv = x_ref[...]
        r = lax.rsqrt(_rowmean(xv * xv) + EPS)
        xn = xv * r
        dh = dh_ref[...]
        g1 = g_ref[...]
        s1 = 1.0 + sc_ref[...]
        _accumulate(dshift_ref, _colsum(dh))
        _accumulate(dscale_ref, _colsum(dh * xn * g1))
        _accumulate(dg_ref, _colsum(dh * xn * s1))
        dxn = dh * g1 * s1
        dx_ref[...] = dres_ref[...] + r * (dxn - xn * _rowmean(dxn * xn))

    blk = pl.BlockSpec((tb, d), lambda i: (i, 0))
    return _call(
        body, (xin, dh, dres, g, scale), comm, name=name, grid=(s // tb,),
        in_specs=[blk, blk, blk, _row_spec(d), _row_spec(d)],
        out_specs=[blk, _row_spec(d), _row_spec(d), _row_spec(d)],
        out_shape=[jax.ShapeDtypeStruct((s, d), F32)] + [jax.ShapeDtypeStruct((1, d), F32)] * 3)


def _merge(z_big, y_a, y_b, name):
    s, d = y_a.shape
    tb = _div_tile(s, 256, SUBLANES)

    def body(zg_ref, ya_ref, yb_ref, o_ref):
        o_ref[...] = (_sigmoid(zg_ref[:, :d]) * ya_ref[...] + _sigmoid(zg_ref[:, d:]) * yb_ref[...]).astype(BF16)

    blk = pl.BlockSpec((tb, d), lambda i: (i, 0))
    return pl.pallas_call(
        body, name=name, grid=(s // tb,), in_specs=[pl.BlockSpec((tb, 2 * d), lambda i: (i, 1)), blk, blk],
        out_specs=blk, out_shape=jax.ShapeDtypeStruct((s, d), BF16), compiler_params=_params(),
    )(z_big, y_a, y_b)


def _merge_bwd(dmerged, z_big, y_a, y_b, name):
    s, d = y_a.shape
    tb = _div_tile(s, 256, SUBLANES)

    def body(dm_ref, zg_ref, ya_ref, yb_ref, dya_ref, dyb_ref, dz_ref):
        dm = dm_ref[...]
        sa, sb = _sigmoid(zg_ref[:, :d]), _sigmoid(zg_ref[:, d:])
        dya_ref[...] = (dm * sa).astype(BF16)
        dyb_ref[...] = (dm * sb).astype(BF16)
        dz_ref[:, :d] = (dm * ya_ref[...] * sa * (1.0 - sa)).astype(BF16)
        dz_ref[:, d:] = (dm * yb_ref[...] * sb * (1.0 - sb)).astype(BF16)

    blk = pl.BlockSpec((tb, d), lambda i: (i, 0))
    wide = pl.BlockSpec((tb, 2 * d), lambda i: (i, 1))
    return pl.pallas_call(
        body, name=name, grid=(s // tb,), in_specs=[blk, wide, blk, blk], out_specs=[blk, blk, wide],
        out_shape=[jax.ShapeDtypeStruct((s, d), BF16), jax.ShapeDtypeStruct((s, d), BF16),
                   jax.ShapeDtypeStruct((s, 4 * d), BF16)],
        compiler_params=_params(),
    )(dmerged, z_big, y_a, y_b)


def _causal_mask(ch):
    q = lax.broadcasted_iota(jnp.int32, (ch, ch), 0)
    p = lax.broadcasted_iota(jnp.int32, (ch, ch), 1)
    return (p <= q).astype(F32)


def _gmlp_norm(zc, lng, lnb, gw):
    u_pre, v_pre = zc[:, :gw], zc[:, gw:]
    vg = _gelu(v_pre)
    mu = _rowmean(vg)
    cen = vg - mu
    rstd = lax.rsqrt(_rowmean(cen * cen) + EPS)
    vhat = cen * rstd
    return u_pre, v_pre, _gelu(u_pre), vhat, rstd, vhat * lng + lnb


def _gmlp_fwd(z_big, ln_g, ln_b, w_s, b_s_t, name):
    s = z_big.shape[0]
    groups, ch, _ = w_s.shape
    gw = ln_g.shape[1]
    gd = gw // groups

    def body(z_ref, lng_ref, lnb_ref, ws_ref, bt_ref, a_ref):
        _, _, u, _, _, vn = _gmlp_norm(z_ref[...], lng_ref[...], lnb_ref[...], gw)
        mask = _causal_mask(ch)
        for g in range(groups):
            cols = slice(g * gd, (g + 1) * gd)
            wm = (ws_ref[g] * mask).astype(BF16)
            mixed = _dot(wm, vn[:, cols].astype(BF16), NN) + bt_ref[:, g:g + 1]
            a_ref[:, cols] = (u[:, cols] * mixed).astype(BF16)

    return pl.pallas_call(
        body, name=name, grid=(s // ch,),
        in_specs=[pl.BlockSpec((ch, 2 * gw), lambda n: (n, 0)), _row_spec(gw), _row_spec(gw),
                  pl.BlockSpec((groups, ch, ch), lambda n: (0, 0, 0)), pl.BlockSpec((ch, groups), lambda n: (0, 0))],
        out_specs=pl.BlockSpec((ch, gw), lambda n: (n, 0)),
        out_shape=jax.ShapeDtypeStruct((s, gw), BF16), compiler_params=_params(),
    )(z_big, ln_g, ln_b, w_s, b_s_t)


def _gmlp_bwd(z_big, da, dz_big, ln_g, ln_b, w_s, b_s_t, name, comm=None):
    s = z_big.shape[0]
    groups, ch, _ = w_s.shape
    gw = ln_g.shape[1]
    gd = gw // groups

    def body(z_ref, da_ref, dzin_ref, lng_ref, lnb_ref, ws_ref, bt_ref, dz_ref, gws_ref, gbt_ref, glng_ref, glnb_ref):
        del dzin_ref
        lng = lng_ref[...]
        u_pre, v_pre, u, vhat, rstd, vn = _gmlp_norm(z_ref[...], lng, lnb_ref[...], gw)
        da = da_ref[...]
        mask = _causal_mask(ch)
        first = pl.program_id(0) == 0
        dvn_parts = []
        lane = lax.broadcasted_iota(jnp.int32, (ch, LANES), 1)
        gb = jnp.zeros((ch, LANES), F32)
        for g in range(groups):
            cols = slice(g * gd, (g + 1) * gd)
            wm = (ws_ref[g] * mask).astype(BF16)
            vn_g = vn[:, cols].astype(BF16)
            mixed = _dot(wm, vn_g, NN) + bt_ref[:, g:g + 1]
            dz_ref[:, cols] = (da[:, cols] * mixed * _gelu_grad(u_pre[:, cols])).astype(BF16)
            dmixed = da[:, cols] * u[:, cols]
            dm16 = dmixed.astype(BF16)
            dvn_parts.append(_dot(wm, dm16, TN))
            gws = _dot(dm16, vn_g, NT) * mask

            @pl.when(first)
            def _(g=g, gws=gws):
                gws_ref[g] = gws

            @pl.when(jnp.logical_not(first))
            def _(g=g, gws=gws):
                gws_ref[g] += gws

            gb = gb + jnp.where(lane == g, jnp.sum(dmixed, axis=1, keepdims=True), 0.0)
        _accumulate(gbt_ref, gb)
        dvn = jnp.concatenate(dvn_parts, axis=1)
        _accumulate(glnb_ref, _colsum(dvn))
        _accumulate(glng_ref, _colsum(dvn * vhat))
        dvh = dvn * lng
        dvg = rstd * (dvh - _rowmean(dvh) - vhat * _rowmean(dvh * vhat))
        dz_ref[:, gw:] = (dvg * _gelu_grad(v_pre)).astype(BF16)

    zspec = pl.BlockSpec((ch, 2 * gw), lambda n: (n, 0))
    return _call(
        body, (z_big, da, dz_big, ln_g, ln_b, w_s, b_s_t), comm, name=name, grid=(s // ch,),
        in_specs=[zspec, pl.BlockSpec((ch, gw), lambda n: (n, 0)), pl.BlockSpec(memory_space=HBM),
                  _row_spec(gw), _row_spec(gw), pl.BlockSpec((groups, ch, ch), lambda n: (0, 0, 0)),
                  pl.BlockSpec((ch, groups), lambda n: (0, 0))],
        out_specs=[zspec, pl.BlockSpec((groups, ch, ch), lambda n: (0, 0, 0)),
                   pl.BlockSpec((ch, LANES), lambda n: (0, 0)), _row_spec(gw), _row_spec(gw)],
        out_shape=[jax.ShapeDtypeStruct(dz_big.shape, BF16), jax.ShapeDtypeStruct((groups, ch, ch), F32),
                   jax.ShapeDtypeStruct((ch, LANES), F32), jax.ShapeDtypeStruct((1, gw), F32),
                   jax.ShapeDtypeStruct((1, gw), F32)],
        input_output_aliases={2: 0})


def _mla_prep(z_lat, q_g, kv_g, rope_k, name):
    s, latw = z_lat.shape
    ql, kvl = q_g.shape[1], kv_g.shape[1]
    tb = _div_tile(s, 256, SUBLANES)

    def body(z_ref, qg_ref, kvg_ref, t_ref, qn_ref, kvn_ref, kr_ref):
        q = z_ref[:, :ql]
        qn_ref[...] = ((q * lax.rsqrt(_rowmean(q * q) + EPS)) * qg_ref[...]).astype(BF16)
        kv = z_ref[:, ql:ql + kvl]
        kvn_ref[...] = ((kv * lax.rsqrt(_rowmean(kv * kv) + EPS)) * kvg_ref[...]).astype(BF16)
        kk = z_ref[:, ql + kvl:] * t_ref[...]
        kr_ref[...] = (kk + pltpu.roll(kk, ROPE, axis=1)).astype(BF16)

    return pl.pallas_call(
        body, name=name, grid=(s // tb,),
        in_specs=[pl.BlockSpec((tb, latw), lambda i: (i, 0)), _row_spec(ql), _row_spec(kvl),
                  pl.BlockSpec((tb, 2 * ROPE), lambda i: (i, 0))],
        out_specs=[pl.BlockSpec((tb, ql), lambda i: (i, 0)), pl.BlockSpec((tb, kvl), lambda i: (i, 0)),
                   pl.BlockSpec((tb, 2 * ROPE), lambda i: (i, 0))],
        out_shape=[jax.ShapeDtypeStruct((s, ql), BF16), jax.ShapeDtypeStruct((s, kvl), BF16),
                   jax.ShapeDtypeStruct((s, 2 * ROPE), BF16)],
        compiler_params=_params(),
    )(z_lat, q_g, kv_g, rope_k)


def _scores(q, k, kr, on_diagonal):
    s = _dot(q[:, :NOPE], k, NT) + _dot(q[:, NOPE:], kr, NT)
    if not on_diagonal:
        return s
    rows = lax.broadcasted_iota(jnp.int32, s.shape, 0)
    cols = lax.broadcasted_iota(jnp.int32, s.shape, 1)
    return jnp.where(cols <= rows, s, -1e30)


def _attn_fwd(q, kv, kr, heads, name, comm=None):
    s = q.shape[0]
    t = _div_tile(s, 512)
    nb = s // t
    hp = 2 if heads % 2 == 0 else 1

    def body(q_ref, k_ref, kr_ref, v_ref, o_ref, lse_ref, m_ref, l_ref, acc_ref):
        i, j = pl.program_id(1), pl.program_id(2)

        @pl.when(j == 0)
        def _():
            m_ref[...] = jnp.full(m_ref.shape, -1e30, F32)
            l_ref[...] = jnp.zeros(l_ref.shape, F32)
            acc_ref[...] = jnp.zeros(acc_ref.shape, F32)

        def step(on_diagonal):
            krv = kr_ref[...]
            for h in range(hp):
                vc = slice(h * VHEAD, (h + 1) * VHEAD)
                sc = _scores(q_ref[:, h * HEAD_W:(h + 1) * HEAD_W], k_ref[:, h * NOPE:(h + 1) * NOPE], krv, on_diagonal)
                m_old = m_ref[h]
                m_new = jnp.maximum(m_old, jnp.max(sc, axis=-1, keepdims=True))
                p = jnp.exp(sc - m_new)
                alpha = jnp.exp(m_old - m_new)
                l_new = alpha * l_ref[h] + jnp.sum(p, axis=-1, keepdims=True)
                acc = alpha * acc_ref[:, vc] + _dot(p.astype(BF16), v_ref[:, vc], NN)
                if on_diagonal:
                    o_ref[:, vc] = (acc / l_new).astype(BF16)
                    lse_ref[h] = jnp.broadcast_to(m_new + jnp.log(l_new), (t, LANES))
                else:
                    m_ref[h], l_ref[h], acc_ref[:, vc] = m_new, l_new, acc

        pl.when(j < i)(lambda: step(False))
        pl.when(j == i)(lambda: step(True))

    kidx = lambda off: (lambda h, i, j: (jnp.minimum(i, j), off(h)))
    return _call(
        body, (q, kv, kr, kv), comm, name=name, grid=(heads // hp, nb, nb),
        in_specs=[pl.BlockSpec((t, hp * HEAD_W), lambda h, i, j: (i, h)),
                  pl.BlockSpec((t, hp * NOPE), kidx(lambda h: h)),
                  pl.BlockSpec((t, 2 * ROPE), kidx(lambda h: 0)),
                  pl.BlockSpec((t, hp * VHEAD), kidx(lambda h: heads // hp + h))],
        out_specs=[pl.BlockSpec((t, hp * VHEAD), lambda h, i, j: (i, h)),
                   pl.BlockSpec((hp, t, LANES), lambda h, i, j: (h, i, 0))],
        out_shape=[jax.ShapeDtypeStruct((s, heads * VHEAD), BF16), jax.ShapeDtypeStruct((heads, s, LANES), F32)],
        scratch_shapes=[pltpu.VMEM((hp, t, 1), F32), pltpu.VMEM((hp, t, 1), F32), pltpu.VMEM((t, hp * VHEAD), F32)])


def _attn_bwd(q, kv, kr, o, do, lse, heads, name, comm=None):
    s = q.shape[0]
    t = _div_tile(s, 512)
    nb = s // t
    hp = 2 if heads % 2 == 0 else 1

    def body(q_ref, k_ref, kr_ref, v_ref, o_ref, do_ref, lse_ref, dq_ref, dk_ref, dv_ref, dk_acc, dv_acc):
        j, i = pl.program_id(1), pl.program_id(2)

        @pl.when(jnp.logical_and(j == 0, i == 0))
        def _():
            dq_ref[...] = jnp.zeros(dq_ref.shape, F32)

        def step(on_diagonal):
            krv = kr_ref[...]
            rows = pl.ds(pl.multiple_of(i * t, t), t)
            for h in range(hp):
                qc, kc, vc = (slice(h * w, (h + 1) * w) for w in (HEAD_W, NOPE, VHEAD))
                qv, kn, do_v = q_ref[:, qc], k_ref[:, kc], do_ref[:, vc]
                p = jnp.exp(_scores(qv, kn, krv, on_diagonal) - lse_ref[h][:, :1])
                dp = _dot(do_v, v_ref[:, vc], NT)
                delta = jnp.sum(do_v.astype(F32) * o_ref[:, vc].astype(F32), axis=-1, keepdims=True)
                ds = (p * (dp - delta)).astype(BF16)
                dq_ref[rows, h * HEAD_W:h * HEAD_W + NOPE] += _dot(ds, kn, NN)
                dq_ref[rows, h * HEAD_W + NOPE:(h + 1) * HEAD_W] += _dot(ds, krv, NN)
                dv_part, dk_part = _dot(p.astype(BF16), do_v, TN), _dot(ds, qv, TN)
                if on_diagonal:
                    dv_acc[:, vc], dk_acc[:, qc] = dv_part, dk_part
                else:
                    dv_acc[:, vc] += dv_part
                    dk_acc[:, qc] += dk_part

        pl.when(i == j)(lambda: step(True))
        pl.when(i > j)(lambda: step(False))

        @pl.when(i == nb - 1)
        def _():
            dk_ref[...] = dk_acc[...].astype(BF16)
            dv_ref[...] = dv_acc[...].astype(BF16)

    qidx = lambda h, j, i: (jnp.maximum(i, j), h)
    return _call(
        body, (q, kv, kr, kv, o, do, lse), comm, name=name, grid=(heads // hp, nb, nb),
        in_specs=[pl.BlockSpec((t, hp * HEAD_W), qidx),
                  pl.BlockSpec((t, hp * NOPE), lambda h, j, i: (j, h)),
                  pl.BlockSpec((t, 2 * ROPE), lambda h, j, i: (j, 0)),
                  pl.BlockSpec((t, hp * VHEAD), lambda h, j, i: (j, heads // hp + h)),
                  pl.BlockSpec((t, hp * VHEAD), qidx), pl.BlockSpec((t, hp * VHEAD), qidx),
                  pl.BlockSpec((hp, t, LANES), lambda h, j, i: (h, jnp.maximum(i, j), 0))],
        out_specs=[pl.BlockSpec((s, hp * HEAD_W), lambda h, j, i: (0, h)),
                   pl.BlockSpec((t, hp * HEAD_W), lambda h, j, i: (j, h)),
                   pl.BlockSpec((t, hp * VHEAD), lambda h, j, i: (j, h))],
        out_shape=[jax.ShapeDtypeStruct((s, heads * HEAD_W), F32), jax.ShapeDtypeStruct((s, heads * HEAD_W), BF16),
                   jax.ShapeDtypeStruct((s, heads * VHEAD), BF16)],
        scratch_shapes=[pltpu.VMEM((t, hp * HEAD_W), F32), pltpu.VMEM((t, hp * VHEAD), F32)])


def _mla_bwd_mid(dq, dk, dv, rope_q, rope_k, heads, name):
    s = dq.shape[0]
    tb = _div_tile(s, 256, SUBLANES)

    def body(dq_ref, dk_ref, dv_ref, tq_ref, tk_ref, dqb_ref, dkv_ref, dkk_ref):
        tq = tq_ref[...]
        dkr = jnp.zeros((tb, 2 * ROPE), F32)
        for h in range(heads):
            cols = slice(h * HEAD_W, (h + 1) * HEAD_W)
            dqb_ref[:, cols] = (dq_ref[:, cols] * tq).astype(BF16)
            dkv_ref[:, h * NOPE:(h + 1) * NOPE] = dk_ref[:, h * HEAD_W:h * HEAD_W + NOPE]
            dkr = dkr + dk_ref[:, h * HEAD_W + NOPE:(h + 1) * HEAD_W].astype(F32)
        dkv_ref[:, heads * NOPE:] = dv_ref[...]
        dkk_ref[...] = (dkr + pltpu.roll(dkr, ROPE, axis=1)) * tk_ref[...]

    wq, wv = heads * HEAD_W, heads * VHEAD
    return pl.pallas_call(
        body, name=name, grid=(s // tb,),
        in_specs=[pl.BlockSpec((tb, wq), lambda i: (i, 0)), pl.BlockSpec((tb, wq), lambda i: (i, 0)),
                  pl.BlockSpec((tb, wv), lambda i: (i, 0)), pl.BlockSpec((tb, HEAD_W), lambda i: (i, 0)),
                  pl.BlockSpec((tb, 2 * ROPE), lambda i: (i, 0))],
        out_specs=[pl.BlockSpec((tb, wq), lambda i: (i, 0)), pl.BlockSpec((tb, heads * NOPE + wv), lambda i: (i, 0)),
                   pl.BlockSpec((tb, 2 * ROPE), lambda i: (i, 0))],
        out_shape=[jax.ShapeDtypeStruct((s, wq), BF16), jax.ShapeDtypeStruct((s, heads * NOPE + wv), BF16),
                   jax.ShapeDtypeStruct((s, 2 * ROPE), F32)],
        compiler_params=_params(),
    )(dq, dk, dv, rope_q, rope_k)


def _mla_bwd_post(z_lat, dqn, dkvn, dkk, q_g, kv_g, name):
    s, latw = z_lat.shape
    ql, kvl = q_g.shape[1], kv_g.shape[1]
    tb = _div_tile(s, 256, SUBLANES)

    def norm_bwd(xv, dn, g, dg_ref):
        r = lax.rsqrt(_rowmean(xv * xv) + EPS)
        xh = xv * r
        _accumulate(dg_ref, _colsum(dn * xh))
        dxh = dn * g
        return r * (dxh - xh * _rowmean(dxh * xh))

    def body(z_ref, dqn_ref, dkvn_ref, dkk_ref, qg_ref, kvg_ref, dz_ref, gq_ref, gkv_ref):
        dz_ref[:, :ql] = norm_bwd(z_ref[:, :ql], dqn_ref[...], qg_ref[...], gq_ref).astype(BF16)
        dz_ref[:, ql:ql + kvl] = norm_bwd(z_ref[:, ql:ql + kvl], dkvn_ref[...], kvg_ref[...], gkv_ref).astype(BF16)
        dz_ref[:, ql + kvl:] = dkk_ref[...].astype(BF16)

    return pl.pallas_call(
        body, name=name, grid=(s // tb,),
        in_specs=[pl.BlockSpec((tb, latw), lambda i: (i, 0)), pl.BlockSpec((tb, ql), lambda i: (i, 0)),
                  pl.BlockSpec((tb, kvl), lambda i: (i, 0)), pl.BlockSpec((tb, 2 * ROPE), lambda i: (i, 0)),
                  _row_spec(ql), _row_spec(kvl)],
        out_specs=[pl.BlockSpec((tb, latw), lambda i: (i, 0)), _row_spec(ql), _row_spec(kvl)],
        out_shape=[jax.ShapeDtypeStruct((s, latw), BF16), jax.ShapeDtypeStruct((1, ql), F32),
                   jax.ShapeDtypeStruct((1, kvl), F32)],
        compiler_params=_params(),
    )(z_lat, dqn, dkvn, dkk, q_g, kv_g)


def _shift_down(x, n):
    rows = lax.broadcasted_iota(jnp.int32, x.shape, 0)
    return jnp.where(rows >= n, pltpu.roll(x, n, axis=0), 0.0)


def _shift_up(x, n):
    s = x.shape[0]
    rows = lax.broadcasted_iota(jnp.int32, x.shape, 0)
    return jnp.where(rows < s - n, pltpu.roll(x, s - n, axis=0), 0.0)


def _conv(pre, w_ref, b_ref):
    return (w_ref[2:3, :] * pre + w_ref[1:2, :] * _shift_down(pre, 1) + w_ref[0:1, :] * _shift_down(pre, 2)
            + b_ref[...])


def _conv_fwd(up_pre, conv_w, conv_b, name):
    s, ff2 = up_pre.shape
    ff = ff2 // 2
    tc = _div_tile(ff, 256)
    nb = ff // tc

    def body(pg_ref, pv_ref, wg_ref, wv_ref, bg_ref, bv_ref, act_ref):
        gate = _conv(pg_ref[...].astype(F32), wg_ref, bg_ref)
        val = _conv(pv_ref[...].astype(F32), wv_ref, bv_ref)
        act_ref[...] = (gate * _sigmoid(gate) * val).astype(BF16)

    def col(rows, off):
        return pl.BlockSpec((rows, tc), lambda j: (0, j + off))

    return pl.pallas_call(
        body, name=name, grid=(nb,),
        in_specs=[col(s, 0), col(s, nb), col(CONV_TAPS, 0), col(CONV_TAPS, nb), col(1, 0), col(1, nb)],
        out_specs=col(s, 0), out_shape=jax.ShapeDtypeStruct((s, ff), BF16), compiler_params=_params(),
    )(up_pre, up_pre, conv_w, conv_w, conv_b, conv_b)


def _conv_bwd(up_pre, dact, conv_w, conv_b, name, comm=None):
    s, ff2 = up_pre.shape
    ff = ff2 // 2
    tc = _div_tile(ff, 256)
    nb = ff // tc

    def half(pre, dx, w_ref, dpre_ref, gw_ref, gb_ref):
        gb_ref[...] = _colsum(dx)
        gw_ref[0:1, :] = _colsum(dx * _shift_down(pre, 2))
        gw_ref[1:2, :] = _colsum(dx * _shift_down(pre, 1))
        gw_ref[2:3, :] = _colsum(dx * pre)
        dpre_ref[...] = (w_ref[2:3, :] * dx + w_ref[1:2, :] * _shift_up(dx, 1)
                         + w_ref[0:1, :] * _shift_up(dx, 2)).astype(BF16)

    def body(pg_ref, pv_ref, da_ref, wg_ref, wv_ref, bg_ref, bv_ref, dup_ref, gwg_ref, gwv_ref, gbg_ref, gbv_ref):
        pre_g, pre_v = pg_ref[...].astype(F32), pv_ref[...].astype(F32)
        gate = _conv(pre_g, wg_ref, bg_ref)
        val = _conv(pre_v, wv_ref, bv_ref)
        da = da_ref[...].astype(F32)
        sg = _sigmoid(gate)
        half(pre_v, da * gate * sg, wv_ref, dup_ref.at[1], gwv_ref, gbv_ref)
        half(pre_g, da * val * sg * (1.0 + gate * (1.0 - sg)), wg_ref, dup_ref.at[0], gwg_ref, gbg_ref)

    def col(rows, off):
        return pl.BlockSpec((rows, tc), lambda j: (0, j + off))

    return _call(
        body, (up_pre, up_pre, dact, conv_w, conv_w, conv_b, conv_b), comm, name=name, grid=(nb,),
        in_specs=[col(s, 0), col(s, nb), col(s, 0), col(CONV_TAPS, 0), col(CONV_TAPS, nb), col(1, 0), col(1, nb)],
        out_specs=[pl.BlockSpec((2, s, tc), lambda j: (0, 0, j)), col(CONV_TAPS, 0), col(CONV_TAPS, 0),
                   col(1, 0), col(1, 0)],
        out_shape=[jax.ShapeDtypeStruct((2, s, ff), BF16)] + [jax.ShapeDtypeStruct((CONV_TAPS, ff), F32)] * 2
        + [jax.ShapeDtypeStruct((1, ff), F32)] * 2)


def _ada_fwd(c_all, w, b, name):
    nseq, d = c_all.shape
    na = w.shape[1]
    tn = _div_tile(na, 512)

    def body(c_ref, w_ref, b_ref, o_ref):
        cv = c_ref[...]
        sc = cv * _sigmoid(cv)
        o_ref[...] = jnp.dot(sc, w_ref[...], preferred_element_type=F32, precision=lax.Precision.HIGHEST) + b_ref[...]

    return pl.pallas_call(
        body, name=name, grid=(na // tn,),
        in_specs=[pl.BlockSpec((nseq, d), lambda j: (0, 0)), pl.BlockSpec((d, tn), lambda j: (0, j)),
                  pl.BlockSpec((1, tn), lambda j: (0, j))],
        out_specs=pl.BlockSpec((nseq, tn), lambda j: (0, j)),
        out_shape=jax.ShapeDtypeStruct((nseq, na), F32), compiler_params=_params(),
    )(c_all, w, b)


def _ada_bwd(c_all_t, dmod, name):
    d, nseq = c_all_t.shape
    na = dmod.shape[1]
    tm, tn = _div_tile(d, 256, SUBLANES), _div_tile(na, 512)

    def body(c_ref, dm_ref, o_ref):
        cv = c_ref[...]
        sc = cv * _sigmoid(cv)
        acc = sc[:, 0:1] * dm_ref[0:1, :]
        for bi in range(1, nseq):
            acc = acc + sc[:, bi:bi + 1] * dm_ref[bi:bi + 1, :]
        o_ref[...] = acc

    return pl.pallas_call(
        body, name=name, grid=(d // tm, na // tn),
        in_specs=[pl.BlockSpec((tm, nseq), lambda i, j: (i, 0)), pl.BlockSpec((nseq, tn), lambda i, j: (0, j))],
        out_specs=pl.BlockSpec((tm, tn), lambda i, j: (i, j)),
        out_shape=jax.ShapeDtypeStruct((d, na), F32), compiler_params=_params(),
    )(c_all_t, dmod)


def _adamw(w, g, m, v, name, comm=None, after=None):
    rows, cols = w.shape
    tb = _div_tile(rows, max(SUBLANES, (256 * 1024) // cols // SUBLANES * SUBLANES), SUBLANES)
    c1 = 1.0 / (1.0 - ADAM_B1 ** ADAM_STEP)
    c2 = 1.0 / (1.0 - ADAM_B2 ** ADAM_STEP)

    def body(*refs):
        w_ref, g_ref, m_ref, v_ref = refs[:4]
        d_ref, nm_ref, nv_ref = refs[-3:]
        gv = g_ref[...]
        nm = ADAM_B1 * m_ref[...] + (1.0 - ADAM_B1) * gv
        nv = ADAM_B2 * v_ref[...] + (1.0 - ADAM_B2) * (gv * gv)
        nm_ref[...] = nm
        nv_ref[...] = nv
        d_ref[...] = -ADAM_LR * ((nm * c1) / (jnp.sqrt(nv * c2) + ADAM_EPS) + ADAM_WD * w_ref[...])

    blk = pl.BlockSpec((tb, cols), lambda i: (i, 0))
    operands, in_specs = (w, g, m, v), [blk] * 4
    if after is not None:
        operands, in_specs = operands + (after,), in_specs + [pl.BlockSpec(after.shape, lambda i: (0, 0))]
    return _call(body, operands, comm, name=name, grid=(rows // tb,), in_specs=in_specs, out_specs=[blk] * 3,
                 out_shape=[jax.ShapeDtypeStruct((rows, cols), F32)] * 3)


def _sum_leading(parts, name, after=()):
    n, rows, cols = parts.shape
    tb = _div_tile(rows, 512, SUBLANES)

    def body(p_ref, *rest):
        o_ref = rest[-1]
        acc = p_ref[0]
        for k in range(1, n):
            acc = acc + p_ref[k]
        o_ref[...] = acc

    return pl.pallas_call(
        body, name=name, grid=(rows // tb,),
        in_specs=[pl.BlockSpec((n, tb, cols), lambda i: (0, i, 0))] + [pl.BlockSpec(memory_space=pl.ANY)] * len(after),
        out_specs=pl.BlockSpec((tb, cols), lambda i: (i, 0)),
        out_shape=jax.ShapeDtypeStruct((rows, cols), F32), compiler_params=_params(),
    )(parts, *after)


def _place():
    x, y, c = lax.axis_index("x"), lax.axis_index("y"), lax.axis_index("c")
    return x, y, c, [(1 - x, y), (x, 1 - y), (1 - x, 1 - y)]


def _all_gather(block, name):
    m_per, n = block.shape

    def body(x_ref, out_ref, send_sems, recv_sems, local_sem):
        x, y, c, chips = _place()
        me, sibling = (x, y, c), (x, y, 1 - c)

        def rows(px, py, pc):
            return out_ref.at[pl.ds((4 * px + 2 * py + pc) * m_per, m_per), :]

        def copy(k, blk, to, src=None):
            return pltpu.make_async_remote_copy(
                src_ref=rows(*blk) if src is None else src, dst_ref=rows(*blk), send_sem=send_sems.at[k],
                recv_sem=recv_sems.at[k], device_id=to, device_id_type=MESH)

        mine = pltpu.make_async_copy(x_ref, rows(*me), local_sem)
        mine.start()
        first = [copy(0, me, sibling, src=x_ref)]
        first += [copy(1 + j, me, (*chip, c), src=x_ref) for j, chip in enumerate(chips)]
        for cp in first:
            cp.start()
        passed = [copy(4 + j, (*chip, c), sibling) for j, chip in enumerate(chips)]
        for j, chip in enumerate(chips):
            copy(1 + j, (*chip, c), me).wait_recv()
            passed[j].start()
        copy(0, sibling, me).wait_recv()
        for j, chip in enumerate(chips):
            copy(4 + j, (*chip, 1 - c), me).wait_recv()
        for cp in first + passed:
            cp.wait_send()
        mine.wait()

    return pl.pallas_call(
        body, name=name, out_shape=jax.ShapeDtypeStruct((N_DEV * m_per, n), block.dtype),
        in_specs=[pl.BlockSpec(memory_space=pltpu.VMEM)], out_specs=pl.BlockSpec(memory_space=pltpu.VMEM),
        scratch_shapes=[pltpu.SemaphoreType.DMA((7,)), pltpu.SemaphoreType.DMA((7,)), pltpu.SemaphoreType.DMA],
        compiler_params=_params(),
    )(block)


def _hbm_specs(n):
    return [pl.BlockSpec(memory_space=HBM)] * n


def _part(ref, by_cols, half, quarter=None, lead=None):
    extent = ref.shape[-1] if by_cols else ref.shape[-2]
    size = extent // 2 if quarter is None else extent // 4
    first = half * (extent // 2) + (0 if quarter is None else quarter * size)
    tile = LANES if by_cols else 2 * SUBLANES
    span = pl.ds(pl.multiple_of(first, tile) if size % tile == 0 else first, size)
    index = (slice(None), span) if by_cols else (span, slice(None))
    return ref.at[index] if lead is None else ref.at[(lead,) + index]


def _half_rows(ref, half, lead=None):
    return _part(ref, False, half, lead=lead)


class _Comm:
    def __init__(self, operands, out_shape, sem_dims, build, aliases=None):
        self.operands, self.out_shape, self.sem_dims = list(operands), list(out_shape), list(sem_dims)
        self.scratch = [pltpu.SemaphoreType.DMA(d) for d in sem_dims]
        self.build, self.aliases = build, dict(aliases or {})


class _SemGrid:
    def __init__(self, sems, dims):
        self.sems, self.dims, self.at = list(sems), tuple(dims), self

    def __getitem__(self, index):
        index = index if isinstance(index, tuple) else (index,)
        flat = 0
        for i, d in zip(index, self.dims):
            flat = flat * d + i
        return self.sems[flat]


def _call(body, operands, comm=None, *, name, grid, in_specs, out_specs, out_shape, scratch_shapes=(),
          input_output_aliases=None):
    aliases = dict(input_output_aliases or {})
    if comm is None:
        return pl.pallas_call(
            body, name=name, grid=grid, in_specs=in_specs, out_specs=out_specs, out_shape=out_shape,
            scratch_shapes=list(scratch_shapes), input_output_aliases=aliases, compiler_params=_params())(*operands)
    single = not isinstance(out_shape, (list, tuple))
    outs = [out_shape] if single else list(out_shape)
    ospecs = [out_specs] if single else list(out_specs)
    n_in, n_out, n_scr = len(operands), len(outs), len(scratch_shapes)
    c_in, c_out = len(comm.operands), len(comm.out_shape)
    for i, o in comm.aliases.items():
        aliases[n_in + i] = n_out + o

    def hosted(*refs):
        ins, c_ins = refs[:n_in], refs[n_in:n_in + c_in]
        o0 = n_in + c_in
        o_refs, c_outs = refs[o0:o0 + n_out], refs[o0 + n_out:o0 + n_out + c_out]
        s0 = o0 + n_out + c_out
        scr, sems = refs[s0:s0 + n_scr], refs[s0 + n_scr:]
        stages = comm.build(c_ins, c_outs, sems)
        step, n_steps = 0, 1
        for dim, size in enumerate(grid):
            step, n_steps = step * size + pl.program_id(dim), n_steps * size
        pl.when(step == 0)(stages[0])
        body(*ins, *o_refs, *scr)
        for stage in stages[1:-1]:
            pl.when(step == (n_steps * MIDDLE_STAGE_AT) // 100)(stage)
        pl.when(step == n_steps - 1)(stages[-1])

    res = pl.pallas_call(
        hosted, name=name, grid=grid, in_specs=list(in_specs) + _hbm_specs(c_in),
        out_specs=ospecs + _hbm_specs(c_out), out_shape=outs + comm.out_shape,
        scratch_shapes=list(scratch_shapes) + comm.scratch, input_output_aliases=aliases,
        compiler_params=_params())(*operands, *comm.operands)
    return (res[0] if single else res[:n_out]), res[n_out:]


def _run_comm(comm, name):
    c_in, c_out = len(comm.operands), len(comm.out_shape)

    def body(*refs):
        for stage in comm.build(refs[:c_in], refs[c_in:c_in + c_out], refs[c_in + c_out:]):
            stage()

    return pl.pallas_call(
        body, name=name, in_specs=_hbm_specs(c_in), out_specs=_hbm_specs(c_out), out_shape=comm.out_shape,
        scratch_shapes=comm.scratch, input_output_aliases=comm.aliases, compiler_params=_params())(*comm.operands)


def _join_comms(comms):
    def build(in_refs, out_refs, sems):
        staged, i, o, k = [], 0, 0, 0
        for cm in comms:
            ni, no, ns = len(cm.operands), len(cm.out_shape), len(cm.sem_dims)
            staged.append(cm.build(in_refs[i:i + ni], out_refs[o:o + no], sems[k:k + ns]))
            i, o, k = i + ni, o + no, k + ns
        def run(fns):
            def stage():
                for fn in fns:
                    fn()
            return stage

        return (run([st[0] for st in staged]), run([fn for st in staged for fn in st[1:-1]]),
                run([st[-1] for st in staged]))

    aliases, i, o = {}, 0, 0
    for cm in comms:
        aliases.update({i + a: o + b for a, b in cm.aliases.items()})
        i, o = i + len(cm.operands), o + len(cm.out_shape)
    return _Comm(sum((cm.operands for cm in comms), []), sum((cm.out_shape for cm in comms), []),
                 sum((cm.sem_dims for cm in comms), []), build, aliases)


def _gather8_comm(block):
    def build(in_refs, out_refs, sems):
        (src,), (out,), (send_sems, recv_sems) = in_refs, out_refs, sems
        x, y, c, chips = _place()
        me, sibling = (x, y, c), (x, y, 1 - c)

        def copy(k, blk, to, own=False):
            dst = out.at[4 * blk[0] + 2 * blk[1] + blk[2]]
            return pltpu.make_async_remote_copy(
                src_ref=src if own else dst, dst_ref=dst, send_sem=send_sems.at[k], recv_sem=recv_sems.at[k],
                device_id=to, device_id_type=MESH)

        first = [copy(0, me, sibling, own=True)] + [copy(1 + j, me, (*chip, c), own=True)
                                                     for j, chip in enumerate(chips)]
        passed = [copy(4 + j, (*chip, c), sibling) for j, chip in enumerate(chips)]

        def start():
            for cp in first:
                cp.start()

        def middle():
            for j, chip in enumerate(chips):
                copy(1 + j, (*chip, c), me).wait_recv()
                passed[j].start()

        def finish():
            copy(0, sibling, me).wait_recv()
            for j, chip in enumerate(chips):
                copy(4 + j, (*chip, 1 - c), me).wait_recv()
            for cp in first + passed:
                cp.wait_send()

        return start, middle, finish

    return _Comm([block], [jax.ShapeDtypeStruct((N_DEV,) + block.shape, block.dtype)], [(7,), (7,)], build)


def _gather_comm(shards, by_cols=()):
    nw = len(shards)

    def build(in_refs, out_refs, sems):
        send_sems, recv_sems = sems
        x, y, c, chips = _place()
        me, sibling = (x, y, c), (x, y, 1 - c)
        across_x, across_y, diagonal = chips

        def copy(w, k, block, part, to, src=None):
            dst = _part(out_refs[w], w in by_cols, part[1], part[2] if part[0] else None, 2 * block[0] + block[1])
            return pltpu.make_async_remote_copy(
                src_ref=dst if src is None else src, dst_ref=dst, send_sem=send_sems.at[w, k],
                recv_sem=recv_sems.at[w, k], device_id=to, device_id_type=MESH)

        first = [copy(w, j, (x, y), (0, c), (*chip, c), src=_part(in_refs[w], w in by_cols, c))
                 for w in range(nw) for j, chip in enumerate((across_x, across_y))]
        passed = [[copy(w, 2, across_x, (1, c, 0), (*across_y, c)), copy(w, 3, across_y, (1, c, 1), (*across_x, c)),
                   copy(w, 4, across_x, (0, c), sibling), copy(w, 5, across_y, (0, c), sibling)] for w in range(nw)]
        last = [[copy(w, 6, diagonal, (1, c, 0), sibling), copy(w, 7, diagonal, (1, c, 1), sibling)]
                for w in range(nw)]

        def start():
            for cp in first:
                cp.start()

        def middle():
            for w in range(nw):
                copy(w, 0, across_x, (0, c), me).wait_recv()
                copy(w, 1, across_y, (0, c), me).wait_recv()
                for cp in passed[w]:
                    cp.start()

        def finish():
            for w in range(nw):
                copy(w, 2, diagonal, (1, c, 0), me).wait_recv()
                copy(w, 3, diagonal, (1, c, 1), me).wait_recv()
                for cp in last[w]:
                    cp.start()
            for w in range(nw):
                for k, block, part in ((4, across_x, (0, 1 - c)), (5, across_y, (0, 1 - c)),
                                       (6, diagonal, (1, 1 - c, 0)), (7, diagonal, (1, 1 - c, 1))):
                    copy(w, k, block, part, me).wait_recv()
            for cp in first + sum(passed, []) + sum(last, []):
                cp.wait_send()

        return start, middle, finish

    return _Comm(shards, [jax.ShapeDtypeStruct((N_CHIPS,) + w.shape, w.dtype) for w in shards],
                 [(nw, 8), (nw, 8)], build)


def _halved(shape, by_cols):
    return shape[:-1] + (shape[-1] // 2,) if by_cols else shape[:-2] + (shape[-2] // 2, shape[-1])


def _swap_comm(gs, by_cols=()):
    nw = len(gs)

    def build(in_refs, out_refs, sems):
        send_sems, recv_sems = sems
        x, y, c, _ = _place()
        cps = []
        for w in range(nw):
            cps.append(pltpu.make_async_remote_copy(
                src_ref=_part(in_refs[w], w in by_cols, 1 - c, lead=slice(None)), dst_ref=out_refs[w],
                send_sem=send_sems.at[w], recv_sem=recv_sems.at[w], device_id=(x, y, 1 - c), device_id_type=MESH))

        def start():
            for cp in cps:
                cp.start()

        def finish():
            for cp in cps:
                cp.wait()

        return start, finish

    return _Comm(gs, [jax.ShapeDtypeStruct(_halved(g.shape, w in by_cols), g.dtype) for w, g in enumerate(gs)],
                 [(nw,), (nw,)], build)


def _exchange_comm(s1s):
    nw = len(s1s)

    def build(in_refs, out_refs, sems):
        send_sems, recv_sems = sems
        x, y, c, chips = _place()
        cps = [pltpu.make_async_remote_copy(
            src_ref=in_refs[w].at[2 * chip[0] + chip[1]], dst_ref=out_refs[w].at[j], send_sem=send_sems.at[w, j],
            recv_sem=recv_sems.at[w, j], device_id=(*chip, c), device_id_type=MESH)
            for w in range(nw) for j, chip in enumerate(chips)]

        def start():
            for cp in cps:
                cp.start()

        def finish():
            for cp in cps:
                cp.wait()

        return start, finish

    return _Comm(s1s, [jax.ShapeDtypeStruct((N_CHIPS - 1,) + s.shape[1:], s.dtype) for s in s1s],
                 [(nw, 3), (nw, 3)], build)


def _size(dims):
    n = 1
    for d in dims:
        n *= d
    return n


def _sem_grids(comm, sem_refs):
    grids, pos = [], 0
    for dims in comm.sem_dims:
        grids.append(_SemGrid(sem_refs[pos:pos + _size(dims)], dims))
        pos += _size(dims)
    return grids


def _comm_split_start(comm, name, after=()):
    c_in, c_out = len(comm.operands), len(comm.out_shape)
    counts = [_size(d) for d in comm.sem_dims]
    n_sem = sum(counts)
    assert not comm.aliases

    def body(*refs):
        srcs, lands = refs[:c_in], refs[c_in:c_in + c_out]
        first_sem = c_in + c_out + len(after)
        start, _ = comm.build(srcs, lands, _sem_grids(comm, refs[first_sem:first_sem + n_sem]))
        start()
        refs[-1][...] = jnp.zeros(refs[-1].shape, refs[-1].dtype)

    lands = [pltpu.with_memory_space_constraint(lax.empty(o.shape, o.dtype), HBM) for o in comm.out_shape]
    srcs = [pltpu.with_memory_space_constraint(a, HBM) for a in comm.operands]
    res = pl.pallas_call(
        body, name=name, in_specs=_hbm_specs(c_in + c_out) + [pl.BlockSpec(memory_space=pl.ANY)] * len(after),
        out_specs=[pl.BlockSpec(memory_space=pltpu.SEMAPHORE)] * n_sem + _hbm_specs(c_in + c_out)
        + [pl.BlockSpec(memory_space=pltpu.VMEM)],
        out_shape=[pltpu.SemaphoreType.DMA(())] * n_sem + [pltpu.HBM(a.shape, a.dtype) for a in comm.operands]
        + [pltpu.HBM(o.shape, o.dtype) for o in comm.out_shape] + [jax.ShapeDtypeStruct((SUBLANES, LANES), F32)],
        input_output_aliases={i: n_sem + i for i in range(c_in + c_out)},
        compiler_params=_params(has_side_effects=pltpu.SideEffectType.DATAFLOW_SIDE_EFFECTING))(*srcs, *lands, *after)
    return res[:-1], res[-1]


def _comm_split_wait(comm, state, after, name):
    c_in, c_out, n_sem = len(comm.operands), len(comm.out_shape), sum(_size(d) for d in comm.sem_dims)
    sems, srcs, lands = state[:n_sem], state[n_sem:n_sem + c_in], state[n_sem + c_in:]

    def body(*refs):
        src_refs, land_refs = refs[:c_in], refs[c_in:c_in + c_out]
        _, finish = comm.build(src_refs, land_refs, _sem_grids(comm, refs[c_in + c_out:c_in + c_out + n_sem]))
        finish()

    sem_spec = pl.BlockSpec(memory_space=pltpu.SEMAPHORE)
    res = pl.pallas_call(
        body, name=name, in_specs=_hbm_specs(c_in + c_out) + [sem_spec] * n_sem + [pl.BlockSpec(memory_space=pl.ANY)],
        out_specs=_hbm_specs(c_in + c_out),
        out_shape=[pltpu.HBM(a.shape, a.dtype) for a in srcs] + [pltpu.HBM(o.shape, o.dtype) for o in lands],
        input_output_aliases={i: i for i in range(c_in + c_out)},
        compiler_params=_params(has_side_effects=pltpu.SideEffectType.DATAFLOW_SIDE_EFFECTING),
    )(*srcs, *lands, *sems, after)
    return res[:c_in], res[c_in:]


def _share_comm(fs, by_cols=()):
    nw = len(fs)

    def build(in_refs, out_refs, sems):
        del in_refs
        send_sems, recv_sems = sems
        x, y, c, _ = _place()

        def copy(w, half):
            part = _part(out_refs[w], w in by_cols, half)
            return pltpu.make_async_remote_copy(
                src_ref=part, dst_ref=part, send_sem=send_sems.at[w], recv_sem=recv_sems.at[w],
                device_id=(x, y, 1 - c), device_id_type=MESH)

        sends = [copy(w, c) for w in range(nw)]

        def start():
            for cp in sends:
                cp.start()

        def finish():
            for w in range(nw):
                copy(w, 1 - c).wait_recv()
            for cp in sends:
                cp.wait_send()

        return start, finish

    return _Comm(fs, [jax.ShapeDtypeStruct(f.shape, f.dtype) for f in fs],
                 [(nw,), (nw,)], build,
                 aliases={w: w for w in range(nw)})


def _add_sibling(g, r1, place, name, by_cols=False):
    nch, h, cols = r1.shape
    tr = _div_tile(h, 1024 if by_cols else 512, 2 * SUBLANES)
    nb = h // tr
    mine = (lambda k, i, p: (k, i, p[0])) if by_cols else (lambda k, i, p: (k, p[0] * nb + i, 0))

    def body(place_ref, g_ref, r_ref, o_ref):
        del place_ref
        o_ref[...] = (g_ref[...].astype(F32) + r_ref[...].astype(F32)).astype(BF16)

    spec = pltpu.PrefetchScalarGridSpec(
        num_scalar_prefetch=1, grid=(nch, nb),
        in_specs=[pl.BlockSpec((None, tr, cols), mine), pl.BlockSpec((None, tr, cols), lambda k, i, p: (k, i, 0))],
        out_specs=pl.BlockSpec((None, tr, cols), lambda k, i, p: (k, i, 0)))
    return pl.pallas_call(body, name=name, grid_spec=spec, out_shape=jax.ShapeDtypeStruct((nch, h, cols), BF16),
                          compiler_params=_params())(place, g, r1)


def _add_chips(s1, r2, place, name, by_cols=False):
    _, h, cols = s1.shape
    tr = _div_tile(h, 1024 if by_cols else 512, 2 * SUBLANES)
    nb = h // tr
    mine = (lambda i, p: (i, p[0])) if by_cols else (lambda i, p: (p[0] * nb + i, 0))
    whole = (h, 2 * cols) if by_cols else (2 * h, cols)

    def body(place_ref, s_ref, r_ref, o_ref):
        del place_ref
        acc = s_ref[...].astype(F32)
        for j in range(N_CHIPS - 1):
            acc = acc + r_ref[j].astype(F32)
        o_ref[...] = acc

    spec = pltpu.PrefetchScalarGridSpec(
        num_scalar_prefetch=1, grid=(nb,),
        in_specs=[pl.BlockSpec((None, tr, cols), lambda i, p: (p[1], i, 0)),
                  pl.BlockSpec((N_CHIPS - 1, tr, cols), lambda i, p: (0, i, 0))],
        out_specs=pl.BlockSpec((tr, cols), mine))
    return pl.pallas_call(body, name=name, grid_spec=spec, out_shape=jax.ShapeDtypeStruct(whole, F32),
                          compiler_params=_params())(place, s1, r2)


def _quarter_turn(m):
    h = m.shape[-1] // 2
    return jnp.concatenate([-m[..., h:], m[..., :h]], axis=-1)


def _quarter_turn_back(m):
    h = m.shape[-1] // 2
    return jnp.concatenate([m[..., h:], -m[..., :h]], axis=-1)


def _stack_rows(parts):
    out = lax.empty((sum(p.shape[0] for p in parts),) + parts[0].shape[1:], parts[0].dtype)
    row = 0
    for p in parts:
        out = lax.dynamic_update_slice(out, p, (row, 0))
        row += p.shape[0]
    return out


def _join_cols(sh):
    return jnp.concatenate([sh[k] for k in range(N_CHIPS)], axis=1)


def _split_cols(full):
    c = full.shape[1] // N_CHIPS
    return jnp.stack([full[:, k * c:(k + 1) * c] for k in range(N_CHIPS)])


def kernel(x, c, positions, w_ada, b_ada, pre_norm1_g, w_in, gm_ln_g, gm_ln_b, gm_w_s, gm_b_s, w_branch_a, q_norm_g, w_uq, kv_norm_g, w_ukv, w_branch_b, w_out, post_norm1_g, pre_norm2_g, w_up, conv_w, conv_b, w_down, post_norm2_g, loss_target, m_w_ada, m_b_ada, m_pre_norm1_g, m_w_in, m_gm_ln_g, m_gm_ln_b, m_gm_w_s, m_gm_b_s, m_w_branch_a, m_q_norm_g, m_w_uq, m_kv_norm_g, m_w_ukv, m_w_branch_b, m_w_out, m_post_norm1_g, m_pre_norm2_g, m_w_up, m_conv_w, m_conv_b, m_w_down, m_post_norm2_g, v_w_ada, v_b_ada, v_pre_norm1_g, v_w_in, v_gm_ln_g, v_gm_ln_b, v_gm_w_s, v_gm_b_s, v_w_branch_a, v_q_norm_g, v_w_uq, v_kv_norm_g, v_w_ukv, v_w_branch_b, v_w_out, v_post_norm1_g, v_pre_norm2_g, v_w_up, v_conv_w, v_conv_b, v_w_down, v_post_norm2_g):
    given = dict(locals())
    s, d = x.shape[1], x.shape[2]
    gw = gm_ln_g.shape[0]
    ql, kvl = q_norm_g.shape[0], kv_norm_g.shape[0]
    heads = N_CHIPS * w_uq.shape[1] // (NOPE + ROPE)
    ff = N_CHIPS * w_down.shape[0]
    assert gw == d and N_CHIPS * w_ukv.shape[1] == heads * (NOPE + VHEAD)
    ix, iy, ic = lax.axis_index("x"), lax.axis_index("y"), lax.axis_index("c")
    chip = 2 * ix + iy
    dev = 2 * chip + ic
    row = lambda v: v.reshape(1, -1)

    c_all = _all_gather(jnp.pad(c, ((0, SUBLANES - 1), (0, 0))), "gather_c").reshape(N_DEV, SUBLANES, d)[:, 0]
    na = w_ada.shape[1]
    b_ada_mine = lax.dynamic_slice(b_ada, (chip * na,), (na,))
    mod_cols = _ada_fwd(c_all, w_ada, row(b_ada_mine), "ada_fwd")
    mod_all = _all_gather(mod_cols, "gather_mod").reshape(N_CHIPS, N_CORES, N_DEV, na)[:, 0]
    mod = lax.dynamic_index_in_dim(mod_all, dev, axis=1, keepdims=False).reshape(N_MOD, d)
    shift1, scale1, gate1, shift2, scale2, gate2 = (mod[i:i + 1] for i in range(N_MOD))

    mine = {n: (given[n].T if n == "w_in" else given[n]).astype(BF16) for n in BIG}
    gather = lambda names: _gather_comm([mine[n] for n in names], [i for i, n in enumerate(names) if n == "w_in"])
    whole = lambda n, g: lax.dynamic_update_slice(g, mine[n][None], (chip, 0, 0))
    rows4 = lambda sh4: sh4.reshape(-1, sh4.shape[2])
    wi_t = rows4(whole("w_in", _run_comm(gather(["w_in"]), "gather_w_in")[0]))
    o_q, o_kv, o_pe, o_ga = 2 * gw, 2 * gw + ql, 2 * gw + ql + kvl, 2 * gw + ql + kvl + ROPE
    w_in_big_t = _stack_rows([wi_t[:o_q], wi_t[o_ga:]])
    w_in_lat_t = _stack_rows([wi_t[o_q:o_ga], _quarter_turn(wi_t[o_pe:o_ga].T).T])

    inv = ROPE_THETA ** (-jnp.arange(0, ROPE, 2, dtype=F32) / ROPE)
    ang = positions[0].astype(F32)[:, None] * inv
    cos, sin = jnp.cos(ang), jnp.sin(ang)
    rope_k = jnp.concatenate([cos, cos, sin, sin], axis=1)
    softmax_scale = float(NOPE + ROPE) ** -0.5
    rope_q = jnp.concatenate([jnp.ones((s, NOPE), F32), rope_k], axis=1) * softmax_scale

    x2d, tgt = x[0], loss_target[0]
    g_pre1, g_post1, g_pre2, g_post2 = row(pre_norm1_g), row(post_norm1_g), row(pre_norm2_g), row(post_norm2_g)
    ln_g, ln_b, q_g, kv_g = row(gm_ln_g), row(gm_ln_b), row(q_norm_g), row(kv_norm_g)
    b_s_t = gm_b_s.T
    conv_wf = _all_gather(jnp.pad(conv_w, ((0, SUBLANES - CONV_TAPS), (0, 0))), "gather_conv_w")
    conv_wf = conv_wf.reshape(N_CHIPS, N_CORES, SUBLANES, conv_w.shape[1])[:, 0, :CONV_TAPS]
    conv_wf = conv_wf.transpose(1, 0, 2).reshape(CONV_TAPS, 2 * ff)
    conv_bf = row(conv_b)

    h1 = _prenorm(x2d, g_pre1, scale1, shift1, "prenorm1")
    z_big, (g_uq, g_ukv, g_a) = _matmul(h1, w_in_big_t, mode="nt", out_dtype=F32, name="mm_z_big", tm=s,
                                        comm=gather(["w_uq", "w_ukv", "w_branch_a"]))
    wq = _join_cols(whole("w_uq", g_uq)).reshape(ql, heads, NOPE + ROPE)
    w_q = jnp.concatenate([wq, _quarter_turn(wq[:, :, NOPE:])], axis=2).reshape(ql, heads * HEAD_W)
    w_kv = _join_cols(whole("w_ukv", g_ukv)).reshape(kvl, heads, 2, NOPE).transpose(0, 2, 1, 3)
    w_kv = w_kv.reshape(kvl, 2 * heads * NOPE)
    w_a = rows4(whole("w_branch_a", g_a))
    z_lat = _matmul(h1, w_in_lat_t, mode="nt", out_dtype=F32, name="mm_z_lat", tm=s, tn=1024)
    a_act = _gmlp_fwd(z_big, ln_g, ln_b, gm_w_s, b_s_t, "gmlp_fwd")
    qn, kvn, kr = _mla_prep(z_lat, q_g, kv_g, rope_k, "mla_prep")
    q_rot = _matmul(qn, w_q, mode="nn", out_dtype=BF16, name="mm_q", tm=s, tn=HEAD_W, mul=rope_q)
    kv_all = _matmul(kvn, w_kv, mode="nn", out_dtype=BF16, name="mm_kv", tm=s, tn=1024)
    (o_att, lse), (g_b, g_o, g_up) = _attn_fwd(q_rot, kv_all, kr, heads, "attn_fwd",
                                               comm=gather(["w_branch_b", "w_out", "w_up"]))
    w_b, w_o, w_upf = rows4(whole("w_branch_b", g_b)), rows4(whole("w_out", g_o)), whole("w_up", g_up)
    y_a = _matmul(a_act, w_a, mode="nn", out_dtype=F32, name="mm_y_a", tm=s)
    y_b = _matmul(o_att, w_b, mode="nn", out_dtype=F32, name="mm_y_b", tm=s)
    merged = _merge(z_big, y_a, y_b, "merge")
    y1 = _matmul(merged, w_o, mode="nn", out_dtype=F32, name="mm_y1", tm=s)
    x1, h2 = _post_pre(x2d, y1, gate1, g_post1, g_pre2, scale2, shift2, "post1_pre2")

    up_pre, (g_dn,) = _matmul(h2, w_upf, mode="nn", out_dtype=BF16, name="mm_up", tm=s, tn=1408,
                              comm=gather(["w_down"]))
    w_dn = rows4(whole("w_down", g_dn))
    act = _conv_fwd(up_pre, conv_wf, conv_bf, "conv_fwd")
    ffn = _matmul(act, w_dn, mode="nn", out_dtype=F32, name="mm_ffn", tm=s, tk=1408)

    dffn, dgate2, g_post2_grad, dx2, loss_part = _post_bwd(ffn, gate2, g_post2, "post2_bwd", xin=x1, target=tgt)
    loss = lax.psum(loss_part[0, 0], ("x", "y", "c"))
    place = jnp.stack([ic, chip]).astype(jnp.int32)
    rows_of = lambda g: g.reshape(N_CHIPS, g.shape[0] // N_CHIPS, g.shape[1])
    add_sibling = lambda names, gs, r1s: [_add_sibling(g, r1, place, "rs_add_sibling_" + n, by_cols=n == "w_in")
                                          for n, g, r1 in zip(names, gs, r1s)]
    add_chips = lambda names, s1s, r2s: [_add_chips(s1, r2, place, "rs_add_chips_" + n, by_cols=n == "w_in")
                                         for n, s1, r2 in zip(names, s1s, r2s)]
    dact = _matmul(dffn, w_dn, mode="nt", out_dtype=BF16, name="mm_dact", tm=s)
    gp_down = [rows_of(_matmul(act, dffn, mode="tn", out_dtype=BF16, name="mm_gw_down", tn=1024, tk=s))]
    (dup, gcw_g, gcw_v, gcb_g, gcb_v), r1_down = _conv_bwd(up_pre, dact, conv_wf, conv_bf, "conv_bwd",
                                                            comm=_swap_comm(gp_down))
    s1_down = add_sibling(["w_down"], gp_down, r1_down)
    dh2, r2_down = _matmul(dup, w_upf, mode="nt", out_dtype=F32, name="mm_dh2", tm=s, tk=1408,
                           comm=_exchange_comm(s1_down))
    half_down = add_chips(["w_down"], s1_down, r2_down)
    gw_up = _matmul(h2, dup, mode="tn", out_dtype=BF16, name="mm_gw_up", tn=1408, tk=s, out_groups=N_CHIPS)
    dx1, dshift2, dscale2, g_pre2_grad = _prenorm_bwd(x1, dh2, dx2, g_pre2, scale2, "prenorm2_bwd")

    dy1, dgate1, g_post1_grad = _post_bwd(y1, gate1, g_post1, "post1_bwd", dxo=dx1)
    dmerged = _matmul(dy1, w_o, mode="nt", out_dtype=F32, name="mm_dmerged", tm=s)
    gw_out = _matmul(merged, dy1, mode="tn", out_dtype=BF16, name="mm_gw_out", tn=1024, tk=s)
    dy_a, dy_b, dz_big = _merge_bwd(dmerged, z_big, y_a, y_b, "merge_bwd")
    da = _matmul(dy_a, w_a, mode="nt", out_dtype=F32, name="mm_da", tm=s)
    gw_a = _matmul(a_act, dy_a, mode="tn", out_dtype=BF16, name="mm_gw_a", tn=1024, tk=s)
    do = _matmul(dy_b, w_b, mode="nt", out_dtype=BF16, name="mm_do", tm=s)
    gw_b = _matmul(o_att, dy_b, mode="tn", out_dtype=BF16, name="mm_gw_b", tn=1024, tk=s)
    mid = ["w_up", "w_out", "w_branch_a", "w_branch_b"]
    gp_mid = [gw_up, rows_of(gw_out), rows_of(gw_a), rows_of(gw_b)]
    (dz_big, g_ws, g_bs_t, g_ln_g, g_ln_b), r1_mid = _gmlp_bwd(z_big, da, dz_big, ln_g, ln_b, gm_w_s, b_s_t,
                                                                "gmlp_bwd", comm=_swap_comm(gp_mid))
    s1_mid = add_sibling(mid, gp_mid, r1_mid)
    (dq, dk, dv), r2_up_out = _attn_bwd(q_rot, kv_all, kr, o_att, do, lse, heads, "attn_bwd",
                                        comm=_exchange_comm(s1_mid[:2]))
    dq_big, dkv, dkk = _mla_bwd_mid(dq, dk, dv, rope_q, rope_k, heads, "mla_bwd_mid")
    gw_q = _matmul(qn, dq_big, mode="tn", out_dtype=F32, name="mm_gw_q", tn=1024, tk=s)
    dqn = _matmul(dq_big, w_q, mode="nt", out_dtype=F32, name="mm_dqn", tm=s, tk=1024)
    gw_kv = _matmul(kvn, dkv, mode="tn", out_dtype=BF16, name="mm_gw_kv", tn=1024, tk=s)
    dkvn = _matmul(dkv, w_kv, mode="nt", out_dtype=F32, name="mm_dkvn", tm=s, tk=1024)
    dz_lat, g_q, g_kv = _mla_bwd_post(z_lat, dqn, dkvn, dkk, q_g, kv_g, "mla_bwd_post")

    partial = {
        "gm_ln_g": g_ln_g, "gm_ln_b": g_ln_b, "gm_w_s": g_ws, "gm_b_s": g_bs_t[:, :gm_b_s.shape[0]].T,
        "q_norm_g": g_q, "kv_norm_g": g_kv, "post_norm1_g": g_post1_grad, "pre_norm2_g": g_pre2_grad,
        "conv_w": jnp.concatenate([gcw_g, gcw_v], axis=1), "conv_b": jnp.concatenate([gcb_g, gcb_v], axis=1),
        "post_norm2_g": g_post2_grad,
    }
    flat = jnp.concatenate([partial[n].reshape(-1) for n in SMALL_PARTIAL])
    n_small = flat.shape[0]
    rows_small = -(-n_small // (LANES * SMALL_ROW_TILE)) * SMALL_ROW_TILE
    flat = jnp.pad(flat, (0, rows_small * LANES - n_small)).reshape(rows_small, LANES)

    def small_pack(prefix, source):
        v = jnp.concatenate([source[prefix + n].reshape(-1) for n in SMALL])
        rows = -(-v.shape[0] // (LANES * SUBLANES)) * SUBLANES
        return jnp.pad(v, (0, rows * LANES - v.shape[0])).reshape(rows, LANES)

    small_state = [small_pack(prefix, given) for prefix in ("", "m_", "v_")]

    dh1, r2_a_b = _matmul(dz_big, w_in_big_t, mode="nn", out_dtype=F32, name="mm_dh1_big", tm=s, tk=1024,
                          comm=_exchange_comm(s1_mid[2:]))
    half_mid = add_chips(mid, s1_mid, list(r2_up_out) + list(r2_a_b))
    dh1 = _matmul(dz_lat, w_in_lat_t, mode="nn", out_dtype=F32, name="mm_dh1_lat", tm=s, tk=1024, add=dh1)
    gw_big_t, hosted = _matmul(dz_big, h1, mode="tn", out_dtype=BF16, name="mm_gw_in_big", tn=2048, tk=s,
                               comm=_join_comms([_share_comm(half_down + half_mid), _gather8_comm(flat)]))
    shared, small_all = hosted[:-1], lax.dynamic_update_slice(hosted[-1], flat[None], (dev, 0, 0))
    small_sum = _sum_leading(small_all, "sum_small", after=small_state).reshape(-1)
    small_grads, off = {}, 0
    for n in SMALL_PARTIAL:
        shape = (CONV_TAPS, 2 * ff) if n == "conv_w" else given[n].shape
        small_grads[n] = small_sum[off:off + partial[n].size].reshape(shape)
        off += partial[n].size
    small_grads["conv_w"] = lax.dynamic_slice(small_grads["conv_w"], (0, chip * conv_w.shape[1]), conv_w.shape)
    grads = dict(zip(["w_down"] + mid, shared), **small_grads)
    gw_lat_t = _matmul(dz_lat, h1, mode="tn", out_dtype=F32, name="mm_gw_in_lat", tm=1024, tn=1024, tk=s)

    gq = gw_q.reshape(ql, heads, HEAD_W)
    gq_pe = gq[:, :, NOPE:NOPE + ROPE] + _quarter_turn_back(gq[:, :, NOPE + ROPE:])
    g_pe_t = gw_lat_t[ql + kvl:ql + kvl + ROPE] + _quarter_turn_back(gw_lat_t[ql + kvl + ROPE:].T).T
    last = ["w_in", "w_uq", "w_ukv"]
    gw_in_t = _stack_rows([gw_big_t[:o_q], gw_lat_t[:ql + kvl].astype(BF16), g_pe_t.astype(BF16), gw_big_t[o_q:]])
    gp_last = [
        gw_in_t.reshape(N_CHIPS, gw_in_t.shape[0] // N_CHIPS, d),
        _split_cols(jnp.concatenate([gq[:, :, :NOPE], gq_pe], axis=2).reshape(ql, heads * (NOPE + ROPE)).astype(BF16)),
        _split_cols(gw_kv.reshape(kvl, 2, heads, NOPE).transpose(0, 2, 1, 3).reshape(kvl, heads * 2 * NOPE)),
    ]
    (grad_x, dshift1, dscale1, g_pre1_grad), r1_last = _prenorm_bwd(x2d, dh1, dx1, g_pre1, scale1, "prenorm1_bwd",
                                                                    comm=_swap_comm(gp_last, by_cols=[0]))
    s1_last = add_sibling(last, gp_last, r1_last)

    dmod = jnp.concatenate([dshift1, dscale1, dgate1, dshift2, dscale2, dgate2, g_pre1_grad], axis=1)
    dmod_all = _all_gather(jnp.pad(dmod, ((0, SUBLANES - 1), (0, 0))), "gather_dmod")
    dmod_all = dmod_all.reshape(N_DEV, SUBLANES, (N_MOD + 1) * d)[:, 0]
    dmod_sum = _sum_leading(dmod_all.reshape(N_DEV, 1, (N_MOD + 1) * d), "sum_dmod")[0]
    grads["b_ada"], grads["pre_norm1_g"] = dmod_sum[:N_MOD * d], dmod_sum[N_MOD * d:]
    dmod_mine = lax.dynamic_slice(dmod_all, (0, chip * na), (N_DEV, na))
    grads["w_ada"] = _ada_bwd(c_all.T, dmod_mine, "ada_bwd")

    delta, new_m, new_v = {}, {}, {}

    def adamw(n, after=None):
        turn = (lambda a: a.T) if n == "w_in" else (lambda a: a)
        outs = _adamw(turn(given[n]), grads[n], turn(given["m_" + n]), turn(given["v_" + n]), "adamw_" + n,
                      after=after)
        grads[n] = turn(grads[n])
        delta[n], new_m[n], new_v[n] = (turn(o) for o in outs)

    exchange_last = _exchange_comm(s1_last)
    in_flight, token = _comm_split_start(exchange_last, "rs_exchange_last_start", after=[dmod_sum, small_sum])
    for n in ["w_ada", "w_down"] + mid:
        adamw(n, after=token)
    s1_last, r2_last = _comm_split_wait(exchange_last, in_flight, delta[mid[-1]], "rs_exchange_last_wait")
    half_last = add_chips(last, s1_last, r2_last)
    grads.update(zip(last, _run_comm(_share_comm(half_last, by_cols=[0]), "rs_share_last")))
    for n in last:
        adamw(n)

    outs = _adamw(small_state[0], small_pack("", grads), small_state[1], small_state[2], "adamw_small")
    off = 0
    for n in SMALL:
        size = given[n].size
        for store, packed_out in zip((delta, new_m, new_v), outs):
            store[n] = packed_out.reshape(-1)[off:off + size].reshape(given[n].shape)
        off += size

    return (loss, grad_x[None], *[grads[n] for n in WEIGHTS], *[delta[n] for n in WEIGHTS],
            *[new_m[n] for n in WEIGHTS], *[new_v[n] for n in WEIGHTS])
```

```python
import functools

import jax
import jax.numpy as jnp
from jax import lax
from jax.experimental import pallas as pl
from jax.experimental.pallas import tpu as pltpu

F32 = jnp.float32
BF16 = jnp.bfloat16
MESH = pl.DeviceIdType.MESH
HBM = pltpu.HBM

EPS = 1e-6
NOPE, ROPE, VHEAD = 128, 64, 128
HEAD_W = NOPE + 2 * ROPE
ROPE_THETA = 10000.0
CONV_TAPS = 3
N_MOD = 6
N_CHIPS, N_CORES, N_DEV = 4, 2, 8
ADAM_LR, ADAM_B1, ADAM_B2, ADAM_EPS, ADAM_WD, ADAM_STEP = 0.001, 0.9, 0.999, 1e-08, 0.01, 10

LANES = 128
SUBLANES = 8
VMEM_LIMIT = 56 * 2**20
MIDDLE_STAGE_AT = 70
SMALL_ROW_TILE = 256

BIG = ("w_in", "w_branch_a", "w_uq", "w_ukv", "w_branch_b", "w_out", "w_up", "w_down")
WEIGHTS = ("w_ada", "b_ada", "pre_norm1_g", "w_in", "gm_ln_g", "gm_ln_b", "gm_w_s", "gm_b_s", "w_branch_a",
           "q_norm_g", "w_uq", "kv_norm_g", "w_ukv", "w_branch_b", "w_out", "post_norm1_g", "pre_norm2_g",
           "w_up", "conv_w", "conv_b", "w_down", "post_norm2_g")
SMALL_PARTIAL = ("gm_ln_g", "gm_ln_b", "gm_w_s", "gm_b_s", "q_norm_g", "kv_norm_g", "post_norm1_g",
                 "pre_norm2_g", "conv_w", "conv_b", "post_norm2_g")
SMALL = ("b_ada", "pre_norm1_g") + SMALL_PARTIAL


def _div_tile(n, cap, mult=LANES):
    t = (min(cap, n) // mult) * mult
    while t >= mult:
        if n % t == 0:
            return t
        t -= mult
    return n


def _params(**kw):
    return pltpu.CompilerParams(vmem_limit_bytes=VMEM_LIMIT, **kw)


def _row_spec(width):
    return pl.BlockSpec((1, width), lambda *_: (0, 0))


def _gelu(x):
    k = 0.7978845608028654
    return 0.5 * x * (1.0 + jnp.tanh(k * (x + 0.044715 * x * x * x)))


def _gelu_grad(x):
    k = 0.7978845608028654
    t = jnp.tanh(k * (x + 0.044715 * x * x * x))
    return 0.5 * (1.0 + t) + 0.5 * x * (1.0 - t * t) * k * (1.0 + 3.0 * 0.044715 * x * x)


def _sigmoid(x):
    return 0.5 * jnp.tanh(0.5 * x) + 0.5


def _dot(a, b, dims):
    return lax.dot_general(a, b, (dims, ((), ())), preferred_element_type=F32)


NN = ((1,), (0,))
NT = ((1,), (1,))
TN = ((0,), (0,))


def _logical(arr):
    if arr.ndim == 2:
        return arr.shape[0], arr.shape[1], arr.shape[1]
    return arr.shape[1], arr.shape[0] * arr.shape[2], arr.shape[2]


def _tile_spec(ndim, group_w, blk_rows, blk_cols, row_of, col_of):
    if ndim == 2:
        return pl.BlockSpec((blk_rows, blk_cols), lambda i, j, k: (row_of(i, j, k), col_of(i, j, k)))
    per = group_w // blk_cols
    return pl.BlockSpec((None, blk_rows, blk_cols),
                        lambda i, j, k: (col_of(i, j, k) // per, row_of(i, j, k), col_of(i, j, k) % per))


def _matmul(a, b, *, mode, out_dtype, name, tm=512, tn=512, tk=2048, mul=None, add=None, out_groups=None, comm=None):
    ar, ac, agw = _logical(a)
    br, bc, bgw = _logical(b)
    if mode == "nn":
        m, kd, n = ar, ac, bc
        m_w, k_w, n_w = (), (agw,), (bgw,)
    elif mode == "nt":
        m, kd, n = ar, ac, br
        m_w, k_w, n_w = (), (agw, bgw), ()
    else:
        m, kd, n = ac, ar, bc
        m_w, k_w, n_w = (agw,), (), (bgw,)
    if out_groups is not None:
        n_w = n_w + (n // out_groups,)
    tm = _div_tile(min((m,) + m_w), tm, LANES if mode == "tn" else SUBLANES)
    tn = _div_tile(min((n,) + n_w), tn)
    tk = _div_tile(min((kd,) + k_w), tk)
    assert all(w % tn == 0 for w in n_w) and all(w % tk == 0 for w in k_w) and all(w % tm == 0 for w in m_w)
    nk = kd // tk
    dims = {"nn": NN, "nt": NT, "tn": TN}[mode]
    gi, gj, gk = (lambda i, j, k: i), (lambda i, j, k: j), (lambda i, j, k: k)
    if mode == "nn":
        a_spec = _tile_spec(a.ndim, agw, tm, tk, gi, gk)
        b_spec = _tile_spec(b.ndim, bgw, tk, tn, gk, gj)
    elif mode == "nt":
        a_spec = _tile_spec(a.ndim, agw, tm, tk, gi, gk)
        b_spec = _tile_spec(b.ndim, bgw, tn, tk, gj, gk)
    else:
        a_spec = _tile_spec(a.ndim, agw, tk, tm, gk, gi)
        b_spec = _tile_spec(b.ndim, bgw, tk, tn, gk, gj)
    in_specs, operands = [a_spec, b_spec], [a, b]
    if mul is not None:
        assert mul.shape == (m, tn)
        in_specs.append(pl.BlockSpec((tm, tn), lambda i, j, k: (i, 0)))
        operands.append(mul)
    if add is not None:
        in_specs.append(pl.BlockSpec((tm, tn), lambda i, j, k: (i, j)))
        operands.append(add)

    def body(*refs):
        a_ref, b_ref = refs[0], refs[1]
        pos = 2
        mul_ref = add_ref = None
        if mul is not None:
            mul_ref, pos = refs[pos], pos + 1
        if add is not None:
            add_ref, pos = refs[pos], pos + 1
        o_ref = refs[pos]

        def finish(r):
            if mul_ref is not None:
                r = r * mul_ref[...]
            if add_ref is not None:
                r = r + add_ref[...]
            o_ref[...] = r.astype(out_dtype)

        part = _dot(a_ref[...], b_ref[...], dims)
        if nk == 1:
            finish(part)
        else:
            acc_ref = refs[pos + 1]
            k = pl.program_id(2)

            @pl.when(k == 0)
            def _():
                acc_ref[...] = part

            @pl.when(k > 0)
            def _():
                acc_ref[...] += part

            @pl.when(k == nk - 1)
            def _():
                finish(acc_ref[...])

    if out_groups is None:
        out_spec, out_dims = _tile_spec(2, n, tm, tn, gi, gj), (m, n)
    else:
        out_spec, out_dims = _tile_spec(3, n // out_groups, tm, tn, gi, gj), (out_groups, m, n // out_groups)
    return _call(body, operands, comm, name=name, grid=(m // tm, n // tn, nk), in_specs=in_specs, out_specs=out_spec,
                 out_shape=jax.ShapeDtypeStruct(out_dims, out_dtype),
                 scratch_shapes=[] if nk == 1 else [pltpu.VMEM((tm, tn), F32)])


def _accumulate(ref, value):
    @pl.when(pl.program_id(0) == 0)
    def _():
        ref[...] = value

    @pl.when(pl.program_id(0) > 0)
    def _():
        ref[...] += value


def _colsum(v):
    return jnp.sum(v, axis=0, keepdims=True)


def _rowmean(v):
    return jnp.mean(v, axis=-1, keepdims=True)


def _prenorm(x, g, scale, shift, name):
    s, d = x.shape
    tb = _div_tile(s, 256, SUBLANES)

    def body(x_ref, g_ref, sc_ref, sh_ref, h_ref):
        xv = x_ref[...]
        r = lax.rsqrt(_rowmean(xv * xv) + EPS)
        h_ref[...] = ((xv * r) * g_ref[...] * (1.0 + sc_ref[...]) + sh_ref[...]).astype(BF16)

    blk = pl.BlockSpec((tb, d), lambda i: (i, 0))
    return pl.pallas_call(
        body, name=name, grid=(s // tb,), in_specs=[blk, _row_spec(d), _row_spec(d), _row_spec(d)],
        out_specs=blk, out_shape=jax.ShapeDtypeStruct((s, d), BF16), compiler_params=_params(),
    )(x, g, scale, shift)


def _post_pre(x, y, gate, pg, g2, scale2, shift2, name):
    s, d = x.shape
    tb = _div_tile(s, 256, SUBLANES)

    def body(x_ref, y_ref, gate_ref, pg_ref, g2_ref, sc_ref, sh_ref, x1_ref, h2_ref):
        yv = y_ref[...]
        rp = lax.rsqrt(_rowmean(yv * yv) + EPS)
        x1 = x_ref[...] + gate_ref[...] * ((yv * rp) * pg_ref[...])
        x1_ref[...] = x1
        r2 = lax.rsqrt(_rowmean(x1 * x1) + EPS)
        h2_ref[...] = ((x1 * r2) * g2_ref[...] * (1.0 + sc_ref[...]) + sh_ref[...]).astype(BF16)

    blk = pl.BlockSpec((tb, d), lambda i: (i, 0))
    return pl.pallas_call(
        body, name=name, grid=(s // tb,), in_specs=[blk, blk] + [_row_spec(d)] * 5,
        out_specs=[blk, blk],
        out_shape=[jax.ShapeDtypeStruct((s, d), F32), jax.ShapeDtypeStruct((s, d), BF16)],
        compiler_params=_params(),
    )(x, y, gate, pg, g2, scale2, shift2)


def _post_bwd(y, gate, pg, name, *, dxo=None, xin=None, target=None):
    s, d = y.shape
    tb = _div_tile(s, 256, SUBLANES)
    from_loss = target is not None

    def body(*refs):
        if from_loss:
            y_ref, gate_ref, pg_ref, xin_ref, t_ref, dy_ref, dgate_ref, dpg_ref, dxo_ref, loss_ref = refs
        else:
            y_ref, gate_ref, pg_ref, dxo_in_ref, dy_ref, dgate_ref, dpg_ref = refs
        yv = y_ref[...]
        rp = lax.rsqrt(_rowmean(yv * yv) + EPS)
        yh = yv * rp
        fn = yh * pg_ref[...]
        gate = gate_ref[...]
        if from_loss:
            err = xin_ref[...] + gate * fn - t_ref[...]
            dxo = err * (1.0 / d)
            dxo_ref[...] = dxo
            part = 0.5 * jnp.sum(_rowmean(err * err), axis=0, keepdims=True)
            _accumulate(loss_ref, jnp.broadcast_to(part, loss_ref.shape))
        else:
            dxo = dxo_in_ref[...]
        _accumulate(dgate_ref, _colsum(dxo * fn))
        dfn = dxo * gate
        _accumulate(dpg_ref, _colsum(dfn * yh))
        dyh = dfn * pg_ref[...]
        dy_ref[...] = (rp * (dyh - yh * _rowmean(dyh * yh))).astype(BF16)

    blk = pl.BlockSpec((tb, d), lambda i: (i, 0))
    in_specs = [blk, _row_spec(d), _row_spec(d)]
    out_specs = [blk, _row_spec(d), _row_spec(d)]
    out_shape = [jax.ShapeDtypeStruct((s, d), BF16), jax.ShapeDtypeStruct((1, d), F32),
                 jax.ShapeDtypeStruct((1, d), F32)]
    if from_loss:
        operands = (y, gate, pg, xin, target)
        in_specs += [blk, blk]
        out_specs += [blk, _row_spec(LANES)]
        out_shape += [jax.ShapeDtypeStruct((s, d), F32), jax.ShapeDtypeStruct((1, LANES), F32)]
    else:
        operands = (y, gate, pg, dxo)
        in_specs += [blk]
    return pl.pallas_call(
        body, name=name, grid=(s // tb,), in_specs=in_specs, out_specs=out_specs, out_shape=out_shape,
        compiler_params=_params(),
    )(*operands)


def _prenorm_bwd(xin, dh, dres, g, scale, name, comm=None):
    s, d = xin.shape
    tb = _div_tile(s, 256, SUBLANES)

    def body(x_ref, dh_ref, dres_ref, g_ref, sc_ref, dx_ref, dshift_ref, dscale_ref, dg_ref):
        xv = x_ref[...]
        r = lax.rsqrt(_rowmean(xv * xv) + EPS)
        xn = xv * r
        dh = dh_ref[...]
        g1 = g_ref[...]
        s1 = 1.0 + sc_ref[...]
        _accumulate(dshift_ref, _colsum(dh))
        _accumulate(dscale_ref, _colsum(dh * xn * g1))
        _accumulate(dg_ref, _colsum(dh * xn * s1))
        dxn = dh * g1 * s1
        dx_ref[...] = dres_ref[...] + r * (dxn - xn * _rowmean(dxn * xn))

    blk = pl.BlockSpec((tb, d), lambda i: (i, 0))
    return _call(
        body, (xin, dh, dres, g, scale), comm, name=name, grid=(s // tb,),
        in_specs=[blk, blk, blk, _row_spec(d), _row_spec(d)],
        out_specs=[blk, _row_spec(d), _row_spec(d), _row_spec(d)],
        out_shape=[jax.ShapeDtypeStruct((s, d), F32)] + [jax.ShapeDtypeStruct((1, d), F32)] * 3)


def _merge(z_big, y_a, y_b, name):
    s, d = y_a.shape
    tb = _div_tile(s, 256, SUBLANES)

    def body(zg_ref, ya_ref, yb_ref, o_ref):
        o_ref[...] = (_sigmoid(zg_ref[:, :d]) * ya_ref[...] + _sigmoid(zg_ref[:, d:]) * yb_ref[...]).astype(BF16)

    blk = pl.BlockSpec((tb, d), lambda i: (i, 0))
    return pl.pallas_call(
        body, name=name, grid=(s // tb,), in_specs=[pl.BlockSpec((tb, 2 * d), lambda i: (i, 1)), blk, blk],
        out_specs=blk, out_shape=jax.ShapeDtypeStruct((s, d), BF16), compiler_params=_params(),
    )(z_big, y_a, y_b)


def _merge_bwd(dmerged, z_big, y_a, y_b, name):
    s, d = y_a.shape
    tb = _div_tile(s, 256, SUBLANES)

    def body(dm_ref, zg_ref, ya_ref, yb_ref, dya_ref, dyb_ref, dz_ref):
        dm = dm_ref[...]
        sa, sb = _sigmoid(zg_ref[:, :d]), _sigmoid(zg_ref[:, d:])
        dya_ref[...] = (dm * sa).astype(BF16)
        dyb_ref[...] = (dm * sb).astype(BF16)
        dz_ref[:, :d] = (dm * ya_ref[...] * sa * (1.0 - sa)).astype(BF16)
        dz_ref[:, d:] = (dm * yb_ref[...] * sb * (1.0 - sb)).astype(BF16)

    blk = pl.BlockSpec((tb, d), lambda i: (i, 0))
    wide = pl.BlockSpec((tb, 2 * d), lambda i: (i, 1))
    return pl.pallas_call(
        body, name=name, grid=(s // tb,), in_specs=[blk, wide, blk, blk], out_specs=[blk, blk, wide],
        out_shape=[jax.ShapeDtypeStruct((s, d), BF16), jax.ShapeDtypeStruct((s, d), BF16),
                   jax.ShapeDtypeStruct((s, 4 * d), BF16)],
        compiler_params=_params(),
    )(dmerged, z_big, y_a, y_b)


def _causal_mask(ch):
    q = lax.broadcasted_iota(jnp.int32, (ch, ch), 0)
    p = lax.broadcasted_iota(jnp.int32, (ch, ch), 1)
    return (p <= q).astype(F32)


def _gmlp_norm(zc, lng, lnb, gw):
    u_pre, v_pre = zc[:, :gw], zc[:, gw:]
    vg = _gelu(v_pre)
    mu = _rowmean(vg)
    cen = vg - mu
    rstd = lax.rsqrt(_rowmean(cen * cen) + EPS)
    vhat = cen * rstd
    return u_pre, v_pre, _gelu(u_pre), vhat, rstd, vhat * lng + lnb


def _gmlp_fwd(z_big, ln_g, ln_b, w_s, b_s_t, name):
    s = z_big.shape[0]
    groups, ch, _ = w_s.shape
    gw = ln_g.shape[1]
    gd = gw // groups

    def body(z_ref, lng_ref, lnb_ref, ws_ref, bt_ref, a_ref):
        _, _, u, _, _, vn = _gmlp_norm(z_ref[...], lng_ref[...], lnb_ref[...], gw)
        mask = _causal_mask(ch)
        for g in range(groups):
            cols = slice(g * gd, (g + 1) * gd)
            wm = (ws_ref[g] * mask).astype(BF16)
            mixed = _dot(wm, vn[:, cols].astype(BF16), NN) + bt_ref[:, g:g + 1]
            a_ref[:, cols] = (u[:, cols] * mixed).astype(BF16)

    return pl.pallas_call(
        body, name=name, grid=(s // ch,),
        in_specs=[pl.BlockSpec((ch, 2 * gw), lambda n: (n, 0)), _row_spec(gw), _row_spec(gw),
                  pl.BlockSpec((groups, ch, ch), lambda n: (0, 0, 0)), pl.BlockSpec((ch, groups), lambda n: (0, 0))],
        out_specs=pl.BlockSpec((ch, gw), lambda n: (n, 0)),
        out_shape=jax.ShapeDtypeStruct((s, gw), BF16), compiler_params=_params(),
    )(z_big, ln_g, ln_b, w_s, b_s_t)


def _gmlp_bwd(z_big, da, dz_big, ln_g, ln_b, w_s, b_s_t, name, comm=None):
    s = z_big.shape[0]
    groups, ch, _ = w_s.shape
    gw = ln_g.shape[1]
    gd = gw // groups

    def body(z_ref, da_ref, dzin_ref, lng_ref, lnb_ref, ws_ref, bt_ref, dz_ref, gws_ref, gbt_ref, glng_ref, glnb_ref):
        del dzin_ref
        lng = lng_ref[...]
        u_pre, v_pre, u, vhat, rstd, vn = _gmlp_norm(z_ref[...], lng, lnb_ref[...], gw)
        da = da_ref[...]
        mask = _causal_mask(ch)
        first = pl.program_id(0) == 0
        dvn_parts = []
        lane = lax.broadcasted_iota(jnp.int32, (ch, LANES), 1)
        gb = jnp.zeros((ch, LANES), F32)
        for g in range(groups):
            cols = slice(g * gd, (g + 1) * gd)
            wm = (ws_ref[g] * mask).astype(BF16)
            vn_g = vn[:, cols].astype(BF16)
            mixed = _dot(wm, vn_g, NN) + bt_ref[:, g:g + 1]
            dz_ref[:, cols] = (da[:, cols] * mixed * _gelu_grad(u_pre[:, cols])).astype(BF16)
            dmixed = da[:, cols] * u[:, cols]
            dm16 = dmixed.astype(BF16)
            dvn_parts.append(_dot(wm, dm16, TN))
            gws = _dot(dm16, vn_g, NT) * mask

            @pl.when(first)
            def _(g=g, gws=gws):
                gws_ref[g] = gws

            @pl.when(jnp.logical_not(first))
            def _(g=g, gws=gws):
                gws_ref[g] += gws

            gb = gb + jnp.where(lane == g, jnp.sum(dmixed, axis=1, keepdims=True), 0.0)
        _accumulate(gbt_ref, gb)
        dvn = jnp.concatenate(dvn_parts, axis=1)
        _accumulate(glnb_ref, _colsum(dvn))
        _accumulate(glng_ref, _colsum(dvn * vhat))
        dvh = dvn * lng
        dvg = rstd * (dvh - _rowmean(dvh) - vhat * _rowmean(dvh * vhat))
        dz_ref[:, gw:] = (dvg * _gelu_grad(v_pre)).astype(BF16)

    zspec = pl.BlockSpec((ch, 2 * gw), lambda n: (n, 0))
    return _call(
        body, (z_big, da, dz_big, ln_g, ln_b, w_s, b_s_t), comm, name=name, grid=(s // ch,),
        in_specs=[zspec, pl.BlockSpec((ch, gw), lambda n: (n, 0)), pl.BlockSpec(memory_space=HBM),
                  _row_spec(gw), _row_spec(gw), pl.BlockSpec((groups, ch, ch), lambda n: (0, 0, 0)),
                  pl.BlockSpec((ch, groups), lambda n: (0, 0))],
        out_specs=[zspec, pl.BlockSpec((groups, ch, ch), lambda n: (0, 0, 0)),
                   pl.BlockSpec((ch, LANES), lambda n: (0, 0)), _row_spec(gw), _row_spec(gw)],
        out_shape=[jax.ShapeDtypeStruct(dz_big.shape, BF16), jax.ShapeDtypeStruct((groups, ch, ch), F32),
                   jax.ShapeDtypeStruct((ch, LANES), F32), jax.ShapeDtypeStruct((1, gw), F32),
                   jax.ShapeDtypeStruct((1, gw), F32)],
        input_output_aliases={2: 0})


def _mla_prep(z_lat, q_g, kv_g, rope_k, name):
    s, latw = z_lat.shape
    ql, kvl = q_g.shape[1], kv_g.shape[1]
    tb = _div_tile(s, 256, SUBLANES)

    def body(z_ref, qg_ref, kvg_ref, t_ref, qn_ref, kvn_ref, kr_ref):
        q = z_ref[:, :ql]
        qn_ref[...] = ((q * lax.rsqrt(_rowmean(q * q) + EPS)) * qg_ref[...]).astype(BF16)
        kv = z_ref[:, ql:ql + kvl]
        kvn_ref[...] = ((kv * lax.rsqrt(_rowmean(kv * kv) + EPS)) * kvg_ref[...]).astype(BF16)
        kk = z_ref[:, ql + kvl:] * t_ref[...]
        kr_ref[...] = (kk + pltpu.roll(kk, ROPE, axis=1)).astype(BF16)

    return pl.pallas_call(
        body, name=name, grid=(s // tb,),
        in_specs=[pl.BlockSpec((tb, latw), lambda i: (i, 0)), _row_spec(ql), _row_spec(kvl),
                  pl.BlockSpec((tb, 2 * ROPE), lambda i: (i, 0))],
        out_specs=[pl.BlockSpec((tb, ql), lambda i: (i, 0)), pl.BlockSpec((tb, kvl), lambda i: (i, 0)),
                   pl.BlockSpec((tb, 2 * ROPE), lambda i: (i, 0))],
        out_shape=[jax.ShapeDtypeStruct((s, ql), BF16), jax.ShapeDtypeStruct((s, kvl), BF16),
                   jax.ShapeDtypeStruct((s, 2 * ROPE), BF16)],
        compiler_params=_params(),
    )(z_lat, q_g, kv_g, rope_k)


def _scores(q, k_full, on_diagonal):
    s = _dot(q, k_full, NT)
    if not on_diagonal:
        return s
    rows = lax.broadcasted_iota(jnp.int32, s.shape, 0)
    cols = lax.broadcasted_iota(jnp.int32, s.shape, 1)
    return jnp.where(cols <= rows, s, -1e30)


def _attn_fwd(q, kv, kr, heads, name, comm=None):
    s = q.shape[0]
    t = _div_tile(s, 512)
    nb = s // t
    hp = 2 if heads % 2 == 0 else 1

    def body(q_ref, k_ref, kr_ref, v_ref, o_ref, lse_ref, m_ref, l_ref, acc_ref):
        i, j = pl.program_id(1), pl.program_id(2)

        @pl.when(j == 0)
        def _():
            m_ref[...] = jnp.full(m_ref.shape, -1e30, F32)
            l_ref[...] = jnp.zeros(l_ref.shape, F32)
            acc_ref[...] = jnp.zeros(acc_ref.shape, F32)

        def step(on_diagonal):
            krv = kr_ref[...]
            for h in range(hp):
                vc = slice(h * VHEAD, (h + 1) * VHEAD)
                k_full = jnp.concatenate([k_ref[:, h * NOPE:(h + 1) * NOPE], krv], axis=1)
                sc = _scores(q_ref[:, h * HEAD_W:(h + 1) * HEAD_W], k_full, on_diagonal)
                m_old = m_ref[h]
                m_new = jnp.maximum(m_old, jnp.max(sc, axis=-1, keepdims=True))
                p = jnp.exp(sc - m_new)
                alpha = jnp.exp(m_old - m_new)
                l_new = alpha * l_ref[h] + jnp.sum(p, axis=-1, keepdims=True)
                acc = alpha * acc_ref[:, vc] + _dot(p.astype(BF16), v_ref[:, vc], NN)
                if on_diagonal:
                    o_ref[:, vc] = (acc / l_new).astype(BF16)
                    lse_ref[h] = jnp.broadcast_to(m_new + jnp.log(l_new), (t, LANES))
                else:
                    m_ref[h], l_ref[h], acc_ref[:, vc] = m_new, l_new, acc

        pl.when(j < i)(lambda: step(False))
        pl.when(j == i)(lambda: step(True))

    kidx = lambda off: (lambda h, i, j: (jnp.minimum(i, j), off(h)))
    return _call(
        body, (q, kv, kr, kv), comm, name=name, grid=(heads // hp, nb, nb),
        in_specs=[pl.BlockSpec((t, hp * HEAD_W), lambda h, i, j: (i, h)),
                  pl.BlockSpec((t, hp * NOPE), kidx(lambda h: h)),
                  pl.BlockSpec((t, 2 * ROPE), kidx(lambda h: 0)),
                  pl.BlockSpec((t, hp * VHEAD), kidx(lambda h: heads // hp + h))],
        out_specs=[pl.BlockSpec((t, hp * VHEAD), lambda h, i, j: (i, h)),
                   pl.BlockSpec((hp, t, LANES), lambda h, i, j: (h, i, 0))],
        out_shape=[jax.ShapeDtypeStruct((s, heads * VHEAD), BF16), jax.ShapeDtypeStruct((heads, s, LANES), F32)],
        scratch_shapes=[pltpu.VMEM((hp, t, 1), F32), pltpu.VMEM((hp, t, 1), F32), pltpu.VMEM((t, hp * VHEAD), F32)])


def _attn_bwd(q, kv, kr, o, do, lse, heads, name, comm=None):
    s = q.shape[0]
    t = _div_tile(s, 512)
    nb = s // t
    hp = 2 if heads % 2 == 0 else 1

    def body(q_ref, k_ref, kr_ref, v_ref, o_ref, do_ref, lse_ref, dq_ref, dk_ref, dv_ref, dk_acc, dv_acc):
        j, i = pl.program_id(1), pl.program_id(2)

        @pl.when(jnp.logical_and(j == 0, i == 0))
        def _():
            dq_ref[...] = jnp.zeros(dq_ref.shape, F32)

        def step(on_diagonal):
            krv = kr_ref[...]
            rows = pl.ds(pl.multiple_of(i * t, t), t)
            for h in range(hp):
                qc, kc, vc = (slice(h * w, (h + 1) * w) for w in (HEAD_W, NOPE, VHEAD))
                qv, do_v = q_ref[:, qc], do_ref[:, vc]
                k_full = jnp.concatenate([k_ref[:, kc], krv], axis=1)
                p = jnp.exp(_scores(qv, k_full, on_diagonal) - lse_ref[h][:, :1])
                dp = _dot(do_v, v_ref[:, vc], NT)
                delta = jnp.sum(do_v.astype(F32) * o_ref[:, vc].astype(F32), axis=-1, keepdims=True)
                ds = (p * (dp - delta)).astype(BF16)
                dq_ref[rows, qc] += _dot(ds, k_full, NN)
                dv_part, dk_part = _dot(p.astype(BF16), do_v, TN), _dot(ds, qv, TN)
                if on_diagonal:
                    dv_acc[:, vc], dk_acc[:, qc] = dv_part, dk_part
                else:
                    dv_acc[:, vc] += dv_part
                    dk_acc[:, qc] += dk_part

        pl.when(i == j)(lambda: step(True))
        pl.when(i > j)(lambda: step(False))

        @pl.when(i == nb - 1)
        def _():
            dk_ref[...] = dk_acc[...].astype(BF16)
            dv_ref[...] = dv_acc[...].astype(BF16)

    qidx = lambda h, j, i: (jnp.maximum(i, j), h)
    return _call(
        body, (q, kv, kr, kv, o, do, lse), comm, name=name, grid=(heads // hp, nb, nb),
        in_specs=[pl.BlockSpec((t, hp * HEAD_W), qidx),
                  pl.BlockSpec((t, hp * NOPE), lambda h, j, i: (j, h)),
                  pl.BlockSpec((t, 2 * ROPE), lambda h, j, i: (j, 0)),
                  pl.BlockSpec((t, hp * VHEAD), lambda h, j, i: (j, heads // hp + h)),
                  pl.BlockSpec((t, hp * VHEAD), qidx), pl.BlockSpec((t, hp * VHEAD), qidx),
                  pl.BlockSpec((hp, t, LANES), lambda h, j, i: (h, jnp.maximum(i, j), 0))],
        out_specs=[pl.BlockSpec((s, hp * HEAD_W), lambda h, j, i: (0, h)),
                   pl.BlockSpec((t, hp * HEAD_W), lambda h, j, i: (j, h)),
                   pl.BlockSpec((t, hp * VHEAD), lambda h, j, i: (j, h))],
        out_shape=[jax.ShapeDtypeStruct((s, heads * HEAD_W), F32), jax.ShapeDtypeStruct((s, heads * HEAD_W), BF16),
                   jax.ShapeDtypeStruct((s, heads * VHEAD), BF16)],
        scratch_shapes=[pltpu.VMEM((t, hp * HEAD_W), F32), pltpu.VMEM((t, hp * VHEAD), F32)])


def _mla_bwd_mid(dq, dk, dv, rope_q, rope_k, heads, name):
    s = dq.shape[0]
    tb = _div_tile(s, 256, SUBLANES)

    def body(dq_ref, dk_ref, dv_ref, tq_ref, tk_ref, dqb_ref, dkv_ref, dkk_ref):
        tq = tq_ref[...]
        dkr = jnp.zeros((tb, 2 * ROPE), F32)
        for h in range(heads):
            cols = slice(h * HEAD_W, (h + 1) * HEAD_W)
            dqb_ref[:, cols] = (dq_ref[:, cols] * tq).astype(BF16)
            dkv_ref[:, h * NOPE:(h + 1) * NOPE] = dk_ref[:, h * HEAD_W:h * HEAD_W + NOPE]
            dkr = dkr + dk_ref[:, h * HEAD_W + NOPE:(h + 1) * HEAD_W].astype(F32)
        dkv_ref[:, heads * NOPE:] = dv_ref[...]
        dkk_ref[...] = (dkr + pltpu.roll(dkr, ROPE, axis=1)) * tk_ref[...]

    wq, wv = heads * HEAD_W, heads * VHEAD
    return pl.pallas_call(
        body, name=name, grid=(s // tb,),
        in_specs=[pl.BlockSpec((tb, wq), lambda i: (i, 0)), pl.BlockSpec((tb, wq), lambda i: (i, 0)),
                  pl.BlockSpec((tb, wv), lambda i: (i, 0)), pl.BlockSpec((tb, HEAD_W), lambda i: (i, 0)),
                  pl.BlockSpec((tb, 2 * ROPE), lambda i: (i, 0))],
        out_specs=[pl.BlockSpec((tb, wq), lambda i: (i, 0)), pl.BlockSpec((tb, heads * NOPE + wv), lambda i: (i, 0)),
                   pl.BlockSpec((tb, 2 * ROPE), lambda i: (i, 0))],
        out_shape=[jax.ShapeDtypeStruct((s, wq), BF16), jax.ShapeDtypeStruct((s, heads * NOPE + wv), BF16),
                   jax.ShapeDtypeStruct((s, 2 * ROPE), F32)],
        compiler_params=_params(),
    )(dq, dk, dv, rope_q, rope_k)


def _mla_bwd_post(z_lat, dqn, dkvn, dkk, q_g, kv_g, name):
    s, latw = z_lat.shape
    ql, kvl = q_g.shape[1], kv_g.shape[1]
    tb = _div_tile(s, 256, SUBLANES)

    def norm_bwd(xv, dn, g, dg_ref):
        r = lax.rsqrt(_rowmean(xv * xv) + EPS)
        xh = xv * r
        _accumulate(dg_ref, _colsum(dn * xh))
        dxh = dn * g
        return r * (dxh - xh * _rowmean(dxh * xh))

    def body(z_ref, dqn_ref, dkvn_ref, dkk_ref, qg_ref, kvg_ref, dz_ref, gq_ref, gkv_ref):
        dz_ref[:, :ql] = norm_bwd(z_ref[:, :ql], dqn_ref[...], qg_ref[...], gq_ref).astype(BF16)
        dz_ref[:, ql:ql + kvl] = norm_bwd(z_ref[:, ql:ql + kvl], dkvn_ref[...], kvg_ref[...], gkv_ref).astype(BF16)
        dz_ref[:, ql + kvl:] = dkk_ref[...].astype(BF16)

    return pl.pallas_call(
        body, name=name, grid=(s // tb,),
        in_specs=[pl.BlockSpec((tb, latw), lambda i: (i, 0)), pl.BlockSpec((tb, ql), lambda i: (i, 0)),
                  pl.BlockSpec((tb, kvl), lambda i: (i, 0)), pl.BlockSpec((tb, 2 * ROPE), lambda i: (i, 0)),
                  _row_spec(ql), _row_spec(kvl)],
        out_specs=[pl.BlockSpec((tb, latw), lambda i: (i, 0)), _row_spec(ql), _row_spec(kvl)],
        out_shape=[jax.ShapeDtypeStruct((s, latw), BF16), jax.ShapeDtypeStruct((1, ql), F32),
                   jax.ShapeDtypeStruct((1, kvl), F32)],
        compiler_params=_params(),
    )(z_lat, dqn, dkvn, dkk, q_g, kv_g)


def _shift_down(x, n):
    rows = lax.broadcasted_iota(jnp.int32, x.shape, 0)
    return jnp.where(rows >= n, pltpu.roll(x, n, axis=0), 0.0)


def _shift_up(x, n):
    s = x.shape[0]
    rows = lax.broadcasted_iota(jnp.int32, x.shape, 0)
    return jnp.where(rows < s - n, pltpu.roll(x, s - n, axis=0), 0.0)


def _conv(pre, w_ref, b_ref):
    return (w_ref[2:3, :] * pre + w_ref[1:2, :] * _shift_down(pre, 1) + w_ref[0:1, :] * _shift_down(pre, 2)
            + b_ref[...])


def _conv_fwd(up_pre, conv_w, conv_b, name):
    s, ff2 = up_pre.shape
    ff = ff2 // 2
    tc = _div_tile(ff, 256)
    nb = ff // tc

    def body(pg_ref, pv_ref, wg_ref, wv_ref, bg_ref, bv_ref, act_ref):
        gate = _conv(pg_ref[...].astype(F32), wg_ref, bg_ref)
        val = _conv(pv_ref[...].astype(F32), wv_ref, bv_ref)
        act_ref[...] = (gate * _sigmoid(gate) * val).astype(BF16)

    def col(rows, off):
        return pl.BlockSpec((rows, tc), lambda j: (0, j + off))

    return pl.pallas_call(
        body, name=name, grid=(nb,),
        in_specs=[col(s, 0), col(s, nb), col(CONV_TAPS, 0), col(CONV_TAPS, nb), col(1, 0), col(1, nb)],
        out_specs=col(s, 0), out_shape=jax.ShapeDtypeStruct((s, ff), BF16), compiler_params=_params(),
    )(up_pre, up_pre, conv_w, conv_w, conv_b, conv_b)


def _conv_bwd(up_pre, dact, conv_w, conv_b, name, comm=None):
    s, ff2 = up_pre.shape
    ff = ff2 // 2
    tc = _div_tile(ff, 256)
    nb = ff // tc

    def half(pre, dx, w_ref, dpre_ref, gw_ref, gb_ref):
        gb_ref[...] = _colsum(dx)
        gw_ref[0:1, :] = _colsum(dx * _shift_down(pre, 2))
        gw_ref[1:2, :] = _colsum(dx * _shift_down(pre, 1))
        gw_ref[2:3, :] = _colsum(dx * pre)
        dpre_ref[...] = (w_ref[2:3, :] * dx + w_ref[1:2, :] * _shift_up(dx, 1)
                         + w_ref[0:1, :] * _shift_up(dx, 2)).astype(BF16)

    def body(pg_ref, pv_ref, da_ref, wg_ref, wv_ref, bg_ref, bv_ref, dup_ref, gwg_ref, gwv_ref, gbg_ref, gbv_ref):
        pre_g, pre_v = pg_ref[...].astype(F32), pv_ref[...].astype(F32)
        gate = _conv(pre_g, wg_ref, bg_ref)
        val = _conv(pre_v, wv_ref, bv_ref)
        da = da_ref[...].astype(F32)
        sg = _sigmoid(gate)
        half(pre_v, da * gate * sg, wv_ref, dup_ref.at[1], gwv_ref, gbv_ref)
        half(pre_g, da * val * sg * (1.0 + gate * (1.0 - sg)), wg_ref, dup_ref.at[0], gwg_ref, gbg_ref)

    def col(rows, off):
        return pl.BlockSpec((rows, tc), lambda j: (0, j + off))

    return _call(
        body, (up_pre, up_pre, dact, conv_w, conv_w, conv_b, conv_b), comm, name=name, grid=(nb,),
        in_specs=[col(s, 0), col(s, nb), col(s, 0), col(CONV_TAPS, 0), col(CONV_TAPS, nb), col(1, 0), col(1, nb)],
        out_specs=[pl.BlockSpec((2, s, tc), lambda j: (0, 0, j)), col(CONV_TAPS, 0), col(CONV_TAPS, 0),
                   col(1, 0), col(1, 0)],
        out_shape=[jax.ShapeDtypeStruct((2, s, ff), BF16)] + [jax.ShapeDtypeStruct((CONV_TAPS, ff), F32)] * 2
        + [jax.ShapeDtypeStruct((1, ff), F32)] * 2)


def _ada_fwd(c_all, w, b, name):
    nseq, d = c_all.shape
    na = w.shape[1]
    tn = _div_tile(na, 512)

    def body(c_ref, w_ref, b_ref, o_ref):
        cv = c_ref[...]
        sc = cv * _sigmoid(cv)
        o_ref[...] = jnp.dot(sc, w_ref[...], preferred_element_type=F32, precision=lax.Precision.HIGHEST) + b_ref[...]

    return pl.pallas_call(
        body, name=name, grid=(na // tn,),
        in_specs=[pl.BlockSpec((nseq, d), lambda j: (0, 0)), pl.BlockSpec((d, tn), lambda j: (0, j)),
                  pl.BlockSpec((1, tn), lambda j: (0, j))],
        out_specs=pl.BlockSpec((nseq, tn), lambda j: (0, j)),
        out_shape=jax.ShapeDtypeStruct((nseq, na), F32), compiler_params=_params(),
    )(c_all, w, b)


def _ada_bwd(c_all_t, dmod, name):
    d, nseq = c_all_t.shape
    na = dmod.shape[1]
    tm, tn = _div_tile(d, 256, SUBLANES), _div_tile(na, 512)

    def body(c_ref, dm_ref, o_ref):
        cv = c_ref[...]
        sc = cv * _sigmoid(cv)
        acc = sc[:, 0:1] * dm_ref[0:1, :]
        for bi in range(1, nseq):
            acc = acc + sc[:, bi:bi + 1] * dm_ref[bi:bi + 1, :]
        o_ref[...] = acc

    return pl.pallas_call(
        body, name=name, grid=(d // tm, na // tn),
        in_specs=[pl.BlockSpec((tm, nseq), lambda i, j: (i, 0)), pl.BlockSpec((nseq, tn), lambda i, j: (0, j))],
        out_specs=pl.BlockSpec((tm, tn), lambda i, j: (i, j)),
        out_shape=jax.ShapeDtypeStruct((d, na), F32), compiler_params=_params(),
    )(c_all_t, dmod)


def _adamw(w, g, m, v, name, comm=None, after=None):
    rows, cols = w.shape
    tb = _div_tile(rows, max(SUBLANES, (256 * 1024) // cols // SUBLANES * SUBLANES), SUBLANES)
    c1 = 1.0 / (1.0 - ADAM_B1 ** ADAM_STEP)
    c2 = 1.0 / (1.0 - ADAM_B2 ** ADAM_STEP)

    def body(*refs):
        w_ref, g_ref, m_ref, v_ref = refs[:4]
        d_ref, nm_ref, nv_ref = refs[-3:]
        gv = g_ref[...]
        nm = ADAM_B1 * m_ref[...] + (1.0 - ADAM_B1) * gv
        nv = ADAM_B2 * v_ref[...] + (1.0 - ADAM_B2) * (gv * gv)
        nm_ref[...] = nm
        nv_ref[...] = nv
        d_ref[...] = -ADAM_LR * ((nm * c1) / (jnp.sqrt(nv * c2) + ADAM_EPS) + ADAM_WD * w_ref[...])

    blk = pl.BlockSpec((tb, cols), lambda i: (i, 0))
    operands, in_specs = (w, g, m, v), [blk] * 4
    if after is not None:
        operands, in_specs = operands + (after,), in_specs + [pl.BlockSpec(after.shape, lambda i: (0, 0))]
    return _call(body, operands, comm, name=name, grid=(rows // tb,), in_specs=in_specs, out_specs=[blk] * 3,
                 out_shape=[jax.ShapeDtypeStruct((rows, cols), F32)] * 3)


def _sum_leading(parts, name, after=()):
    n, rows, cols = parts.shape
    tb = _div_tile(rows, 512, SUBLANES)

    def body(p_ref, *rest):
        o_ref = rest[-1]
        acc = p_ref[0]
        for k in range(1, n):
            acc = acc + p_ref[k]
        o_ref[...] = acc

    return pl.pallas_call(
        body, name=name, grid=(rows // tb,),
        in_specs=[pl.BlockSpec((n, tb, cols), lambda i: (0, i, 0))] + [pl.BlockSpec(memory_space=pl.ANY)] * len(after),
        out_specs=pl.BlockSpec((tb, cols), lambda i: (i, 0)),
        out_shape=jax.ShapeDtypeStruct((rows, cols), F32), compiler_params=_params(),
    )(parts, *after)


def _place():
    x, y, c = lax.axis_index("x"), lax.axis_index("y"), lax.axis_index("c")
    return x, y, c, [(1 - x, y), (x, 1 - y), (1 - x, 1 - y)]


def _all_gather(block, name):
    m_per, n = block.shape

    def body(x_ref, out_ref, send_sems, recv_sems, local_sem):
        x, y, c, chips = _place()
        me, sibling = (x, y, c), (x, y, 1 - c)

        def rows(px, py, pc):
            return out_ref.at[pl.ds((4 * px + 2 * py + pc) * m_per, m_per), :]

        def copy(k, blk, to, src=None):
            return pltpu.make_async_remote_copy(
                src_ref=rows(*blk) if src is None else src, dst_ref=rows(*blk), send_sem=send_sems.at[k],
                recv_sem=recv_sems.at[k], device_id=to, device_id_type=MESH)

        mine = pltpu.make_async_copy(x_ref, rows(*me), local_sem)
        mine.start()
        first = [copy(0, me, sibling, src=x_ref)]
        first += [copy(1 + j, me, (*chip, c), src=x_ref) for j, chip in enumerate(chips)]
        for cp in first:
            cp.start()
        passed = [copy(4 + j, (*chip, c), sibling) for j, chip in enumerate(chips)]
        for j, chip in enumerate(chips):
            copy(1 + j, (*chip, c), me).wait_recv()
            passed[j].start()
        copy(0, sibling, me).wait_recv()
        for j, chip in enumerate(chips):
            copy(4 + j, (*chip, 1 - c), me).wait_recv()
        for cp in first + passed:
            cp.wait_send()
        mine.wait()

    return pl.pallas_call(
        body, name=name, out_shape=jax.ShapeDtypeStruct((N_DEV * m_per, n), block.dtype),
        in_specs=[pl.BlockSpec(memory_space=pltpu.VMEM)], out_specs=pl.BlockSpec(memory_space=pltpu.VMEM),
        scratch_shapes=[pltpu.SemaphoreType.DMA((7,)), pltpu.SemaphoreType.DMA((7,)), pltpu.SemaphoreType.DMA],
        compiler_params=_params(),
    )(block)


def _hbm_specs(n):
    return [pl.BlockSpec(memory_space=HBM)] * n


def _part(ref, by_cols, half, quarter=None, lead=None):
    extent = ref.shape[-1] if by_cols else ref.shape[-2]
    size = extent // 2 if quarter is None else extent // 4
    first = half * (extent // 2) + (0 if quarter is None else quarter * size)
    tile = LANES if by_cols else 2 * SUBLANES
    span = pl.ds(pl.multiple_of(first, tile) if size % tile == 0 else first, size)
    index = (slice(None), span) if by_cols else (span, slice(None))
    return ref.at[index] if lead is None else ref.at[(lead,) + index]


def _half_rows(ref, half, lead=None):
    return _part(ref, False, half, lead=lead)


class _Comm:
    def __init__(self, operands, out_shape, sem_dims, build, aliases=None):
        self.operands, self.out_shape, self.sem_dims = list(operands), list(out_shape), list(sem_dims)
        self.scratch = [pltpu.SemaphoreType.DMA(d) for d in sem_dims]
        self.build, self.aliases = build, dict(aliases or {})


class _SemGrid:
    def __init__(self, sems, dims):
        self.sems, self.dims, self.at = list(sems), tuple(dims), self

    def __getitem__(self, index):
        index = index if isinstance(index, tuple) else (index,)
        flat = 0
        for i, d in zip(index, self.dims):
            flat = flat * d + i
        return self.sems[flat]


def _call(body, operands, comm=None, *, name, grid, in_specs, out_specs, out_shape, scratch_shapes=(),
          input_output_aliases=None):
    aliases = dict(input_output_aliases or {})
    if comm is None:
        return pl.pallas_call(
            body, name=name, grid=grid, in_specs=in_specs, out_specs=out_specs, out_shape=out_shape,
            scratch_shapes=list(scratch_shapes), input_output_aliases=aliases, compiler_params=_params())(*operands)
    single = not isinstance(out_shape, (list, tuple))
    outs = [out_shape] if single else list(out_shape)
    ospecs = [out_specs] if single else list(out_specs)
    n_in, n_out, n_scr = len(operands), len(outs), len(scratch_shapes)
    c_in, c_out = len(comm.operands), len(comm.out_shape)
    for i, o in comm.aliases.items():
        aliases[n_in + i] = n_out + o

    def hosted(*refs):
        ins, c_ins = refs[:n_in], refs[n_in:n_in + c_in]
        o0 = n_in + c_in
        o_refs, c_outs = refs[o0:o0 + n_out], refs[o0 + n_out:o0 + n_out + c_out]
        s0 = o0 + n_out + c_out
        scr, sems = refs[s0:s0 + n_scr], refs[s0 + n_scr:]
        stages = comm.build(c_ins, c_outs, sems)
        step, n_steps = 0, 1
        for dim, size in enumerate(grid):
            step, n_steps = step * size + pl.program_id(dim), n_steps * size
        pl.when(step == 0)(stages[0])
        body(*ins, *o_refs, *scr)
        for stage in stages[1:-1]:
            pl.when(step == (n_steps * MIDDLE_STAGE_AT) // 100)(stage)
        pl.when(step == n_steps - 1)(stages[-1])

    res = pl.pallas_call(
        hosted, name=name, grid=grid, in_specs=list(in_specs) + _hbm_specs(c_in),
        out_specs=ospecs + _hbm_specs(c_out), out_shape=outs + comm.out_shape,
        scratch_shapes=list(scratch_shapes) + comm.scratch, input_output_aliases=aliases,
        compiler_params=_params())(*operands, *comm.operands)
    return (res[0] if single else res[:n_out]), res[n_out:]


def _run_comm(comm, name):
    c_in, c_out = len(comm.operands), len(comm.out_shape)

    def body(*refs):
        for stage in comm.build(refs[:c_in], refs[c_in:c_in + c_out], refs[c_in + c_out:]):
            stage()

    return pl.pallas_call(
        body, name=name, in_specs=_hbm_specs(c_in), out_specs=_hbm_specs(c_out), out_shape=comm.out_shape,
        scratch_shapes=comm.scratch, input_output_aliases=comm.aliases, compiler_params=_params())(*comm.operands)


def _join_comms(comms):
    def build(in_refs, out_refs, sems):
        staged, i, o, k = [], 0, 0, 0
        for cm in comms:
            ni, no, ns = len(cm.operands), len(cm.out_shape), len(cm.sem_dims)
            staged.append(cm.build(in_refs[i:i + ni], out_refs[o:o + no], sems[k:k + ns]))
            i, o, k = i + ni, o + no, k + ns
        def run(fns):
            def stage():
                for fn in fns:
                    fn()
            return stage

        return (run([st[0] for st in staged]), run([fn for st in staged for fn in st[1:-1]]),
                run([st[-1] for st in staged]))

    aliases, i, o = {}, 0, 0
    for cm in comms:
        aliases.update({i + a: o + b for a, b in cm.aliases.items()})
        i, o = i + len(cm.operands), o + len(cm.out_shape)
    return _Comm(sum((cm.operands for cm in comms), []), sum((cm.out_shape for cm in comms), []),
                 sum((cm.sem_dims for cm in comms), []), build, aliases)


def _gather8_comm(block):
    def build(in_refs, out_refs, sems):
        (src,), (out,), (send_sems, recv_sems) = in_refs, out_refs, sems
        x, y, c, chips = _place()
        me, sibling = (x, y, c), (x, y, 1 - c)

        def copy(k, blk, to, own=False):
            dst = out.at[4 * blk[0] + 2 * blk[1] + blk[2]]
            return pltpu.make_async_remote_copy(
                src_ref=src if own else dst, dst_ref=dst, send_sem=send_sems.at[k], recv_sem=recv_sems.at[k],
                device_id=to, device_id_type=MESH)

        first = [copy(0, me, sibling, own=True)] + [copy(1 + j, me, (*chip, c), own=True)
                                                     for j, chip in enumerate(chips)]
        passed = [copy(4 + j, (*chip, c), sibling) for j, chip in enumerate(chips)]

        def start():
            for cp in first:
                cp.start()

        def middle():
            for j, chip in enumerate(chips):
                copy(1 + j, (*chip, c), me).wait_recv()
                passed[j].start()

        def finish():
            copy(0, sibling, me).wait_recv()
            for j, chip in enumerate(chips):
                copy(4 + j, (*chip, 1 - c), me).wait_recv()
            for cp in first + passed:
                cp.wait_send()

        return start, middle, finish

    return _Comm([block], [jax.ShapeDtypeStruct((N_DEV,) + block.shape, block.dtype)], [(7,), (7,)], build)


def _gather_comm(shards, by_cols=()):
    nw = len(shards)

    def build(in_refs, out_refs, sems):
        send_sems, recv_sems = sems
        x, y, c, chips = _place()
        me, sibling = (x, y, c), (x, y, 1 - c)
        across_x, across_y, diagonal = chips

        def copy(w, k, block, part, to, src=None):
            dst = _part(out_refs[w], w in by_cols, part[1], part[2] if part[0] else None, 2 * block[0] + block[1])
            return pltpu.make_async_remote_copy(
                src_ref=dst if src is None else src, dst_ref=dst, send_sem=send_sems.at[w, k],
                recv_sem=recv_sems.at[w, k], device_id=to, device_id_type=MESH)

        first = [copy(w, j, (x, y), (0, c), (*chip, c), src=_part(in_refs[w], w in by_cols, c))
                 for w in range(nw) for j, chip in enumerate((across_x, across_y))]
        passed = [[copy(w, 2, across_x, (1, c, 0), (*across_y, c)), copy(w, 3, across_y, (1, c, 1), (*across_x, c)),
                   copy(w, 4, across_x, (0, c), sibling), copy(w, 5, across_y, (0, c), sibling)] for w in range(nw)]
        last = [[copy(w, 6, diagonal, (1, c, 0), sibling), copy(w, 7, diagonal, (1, c, 1), sibling)]
                for w in range(nw)]

        def start():
            for cp in first:
                cp.start()

        def middle():
            for w in range(nw):
                copy(w, 0, across_x, (0, c), me).wait_recv()
                copy(w, 1, across_y, (0, c), me).wait_recv()
                for cp in passed[w]:
                    cp.start()

        def finish():
            for w in range(nw):
                copy(w, 2, diagonal, (1, c, 0), me).wait_recv()
                copy(w, 3, diagonal, (1, c, 1), me).wait_recv()
                for cp in last[w]:
                    cp.start()
            for w in range(nw):
                for k, block, part in ((4, across_x, (0, 1 - c)), (5, across_y, (0, 1 - c)),
                                       (6, diagonal, (1, 1 - c, 0)), (7, diagonal, (1, 1 - c, 1))):
                    copy(w, k, block, part, me).wait_recv()
            for cp in first + sum(passed, []) + sum(last, []):
                cp.wait_send()

        return start, middle, finish

    return _Comm(shards, [jax.ShapeDtypeStruct((N_CHIPS,) + w.shape, w.dtype) for w in shards],
                 [(nw, 8), (nw, 8)], build)


def _halved(shape, by_cols):
    return shape[:-1] + (shape[-1] // 2,) if by_cols else shape[:-2] + (shape[-2] // 2, shape[-1])


def _swap_comm(gs, by_cols=()):
    nw = len(gs)

    def build(in_refs, out_refs, sems):
        send_sems, recv_sems = sems
        x, y, c, _ = _place()
        cps = []
        for w in range(nw):
            cps.append(pltpu.make_async_remote_copy(
                src_ref=_part(in_refs[w], w in by_cols, 1 - c, lead=slice(None)), dst_ref=out_refs[w],
                send_sem=send_sems.at[w], recv_sem=recv_sems.at[w], device_id=(x, y, 1 - c), device_id_type=MESH))

        def start():
            for cp in cps:
                cp.start()

        def finish():
            for cp in cps:
                cp.wait()

        return start, finish

    return _Comm(gs, [jax.ShapeDtypeStruct(_halved(g.shape, w in by_cols), g.dtype) for w, g in enumerate(gs)],
                 [(nw,), (nw,)], build)


def _exchange_comm(s1s):
    nw = len(s1s)

    def build(in_refs, out_refs, sems):
        send_sems, recv_sems = sems
        x, y, c, chips = _place()
        cps = [pltpu.make_async_remote_copy(
            src_ref=in_refs[w].at[2 * chip[0] + chip[1]], dst_ref=out_refs[w].at[j], send_sem=send_sems.at[w, j],
            recv_sem=recv_sems.at[w, j], device_id=(*chip, c), device_id_type=MESH)
            for w in range(nw) for j, chip in enumerate(chips)]

        def start():
            for cp in cps:
                cp.start()

        def finish():
            for cp in cps:
                cp.wait()

        return start, finish

    return _Comm(s1s, [jax.ShapeDtypeStruct((N_CHIPS - 1,) + s.shape[1:], s.dtype) for s in s1s],
                 [(nw, 3), (nw, 3)], build)


def _size(dims):
    n = 1
    for d in dims:
        n *= d
    return n


def _sem_grids(comm, sem_refs):
    grids, pos = [], 0
    for dims in comm.sem_dims:
        grids.append(_SemGrid(sem_refs[pos:pos + _size(dims)], dims))
        pos += _size(dims)
    return grids


def _comm_split_start(comm, name, after=()):
    c_in, c_out = len(comm.operands), len(comm.out_shape)
    counts = [_size(d) for d in comm.sem_dims]
    n_sem = sum(counts)
    assert not comm.aliases

    def body(*refs):
        srcs, lands = refs[:c_in], refs[c_in:c_in + c_out]
        first_sem = c_in + c_out + len(after)
        start, _ = comm.build(srcs, lands, _sem_grids(comm, refs[first_sem:first_sem + n_sem]))
        start()
        refs[-1][...] = jnp.zeros(refs[-1].shape, refs[-1].dtype)

    lands = [pltpu.with_memory_space_constraint(lax.empty(o.shape, o.dtype), HBM) for o in comm.out_shape]
    srcs = [pltpu.with_memory_space_constraint(a, HBM) for a in comm.operands]
    res = pl.pallas_call(
        body, name=name, in_specs=_hbm_specs(c_in + c_out) + [pl.BlockSpec(memory_space=pl.ANY)] * len(after),
        out_specs=[pl.BlockSpec(memory_space=pltpu.SEMAPHORE)] * n_sem + _hbm_specs(c_in + c_out)
        + [pl.BlockSpec(memory_space=pltpu.VMEM)],
        out_shape=[pltpu.SemaphoreType.DMA(())] * n_sem + [pltpu.HBM(a.shape, a.dtype) for a in comm.operands]
        + [pltpu.HBM(o.shape, o.dtype) for o in comm.out_shape] + [jax.ShapeDtypeStruct((SUBLANES, LANES), F32)],
        input_output_aliases={i: n_sem + i for i in range(c_in + c_out)},
        compiler_params=_params(has_side_effects=pltpu.SideEffectType.DATAFLOW_SIDE_EFFECTING))(*srcs, *lands, *after)
    return res[:-1], res[-1]


def _comm_split_wait(comm, state, after, name):
    c_in, c_out, n_sem = len(comm.operands), len(comm.out_shape), sum(_size(d) for d in comm.sem_dims)
    sems, srcs, lands = state[:n_sem], state[n_sem:n_sem + c_in], state[n_sem + c_in:]

    def body(*refs):
        src_refs, land_refs = refs[:c_in], refs[c_in:c_in + c_out]
        _, finish = comm.build(src_refs, land_refs, _sem_grids(comm, refs[c_in + c_out:c_in + c_out + n_sem]))
        finish()

    sem_spec = pl.BlockSpec(memory_space=pltpu.SEMAPHORE)
    res = pl.pallas_call(
        body, name=name, in_specs=_hbm_specs(c_in + c_out) + [sem_spec] * n_sem + [pl.BlockSpec(memory_space=pl.ANY)],
        out_specs=_hbm_specs(c_in + c_out),
        out_shape=[pltpu.HBM(a.shape, a.dtype) for a in srcs] + [pltpu.HBM(o.shape, o.dtype) for o in lands],
        input_output_aliases={i: i for i in range(c_in + c_out)},
        compiler_params=_params(has_side_effects=pltpu.SideEffectType.DATAFLOW_SIDE_EFFECTING),
    )(*srcs, *lands, *sems, after)
    return res[:c_in], res[c_in:]


def _share_comm(fs, by_cols=()):
    nw = len(fs)

    def build(in_refs, out_refs, sems):
        del in_refs
        send_sems, recv_sems = sems
        x, y, c, _ = _place()

        def copy(w, half):
            part = _part(out_refs[w], w in by_cols, half)
            return pltpu.make_async_remote_copy(
                src_ref=part, dst_ref=part, send_sem=send_sems.at[w], recv_sem=recv_sems.at[w],
                device_id=(x, y, 1 - c), device_id_type=MESH)

        sends = [copy(w, c) for w in range(nw)]

        def start():
            for cp in sends:
                cp.start()

        def finish():
            for w in range(nw):
                copy(w, 1 - c).wait_recv()
            for cp in sends:
                cp.wait_send()

        return start, finish

    return _Comm(fs, [jax.ShapeDtypeStruct(f.shape, f.dtype) for f in fs],
                 [(nw,), (nw,)], build,
                 aliases={w: w for w in range(nw)})


def _add_sibling(g, r1, place, name, by_cols=False):
    nch, h, cols = r1.shape
    tr = _div_tile(h, 1024 if by_cols else 512, 2 * SUBLANES)
    nb = h // tr
    mine = (lambda k, i, p: (k, i, p[0])) if by_cols else (lambda k, i, p: (k, p[0] * nb + i, 0))

    def body(place_ref, g_ref, r_ref, o_ref):
        del place_ref
        o_ref[...] = (g_ref[...].astype(F32) + r_ref[...].astype(F32)).astype(BF16)

    spec = pltpu.PrefetchScalarGridSpec(
        num_scalar_prefetch=1, grid=(nch, nb),
        in_specs=[pl.BlockSpec((None, tr, cols), mine), pl.BlockSpec((None, tr, cols), lambda k, i, p: (k, i, 0))],
        out_specs=pl.BlockSpec((None, tr, cols), lambda k, i, p: (k, i, 0)))
    return pl.pallas_call(body, name=name, grid_spec=spec, out_shape=jax.ShapeDtypeStruct((nch, h, cols), BF16),
                          compiler_params=_params())(place, g, r1)


def _add_chips(s1, r2, place, name, by_cols=False):
    _, h, cols = s1.shape
    tr = _div_tile(h, 1024 if by_cols else 512, 2 * SUBLANES)
    nb = h // tr
    mine = (lambda i, p: (i, p[0])) if by_cols else (lambda i, p: (p[0] * nb + i, 0))
    whole = (h, 2 * cols) if by_cols else (2 * h, cols)

    def body(place_ref, s_ref, r_ref, o_ref):
        del place_ref
        acc = s_ref[...].astype(F32)
        for j in range(N_CHIPS - 1):
            acc = acc + r_ref[j].astype(F32)
        o_ref[...] = acc

    spec = pltpu.PrefetchScalarGridSpec(
        num_scalar_prefetch=1, grid=(nb,),
        in_specs=[pl.BlockSpec((None, tr, cols), lambda i, p: (p[1], i, 0)),
                  pl.BlockSpec((N_CHIPS - 1, tr, cols), lambda i, p: (0, i, 0))],
        out_specs=pl.BlockSpec((tr, cols), mine))
    return pl.pallas_call(body, name=name, grid_spec=spec, out_shape=jax.ShapeDtypeStruct(whole, F32),
                          compiler_params=_params())(place, s1, r2)


def _quarter_turn(m):
    h = m.shape[-1] // 2
    return jnp.concatenate([-m[..., h:], m[..., :h]], axis=-1)


def _quarter_turn_back(m):
    h = m.shape[-1] // 2
    return jnp.concatenate([m[..., h:], -m[..., :h]], axis=-1)


def _stack_rows(parts):
    out = lax.empty((sum(p.shape[0] for p in parts),) + parts[0].shape[1:], parts[0].dtype)
    row = 0
    for p in parts:
        out = lax.dynamic_update_slice(out, p, (row, 0))
        row += p.shape[0]
    return out


def _join_cols(sh):
    return jnp.concatenate([sh[k] for k in range(N_CHIPS)], axis=1)


def _split_cols(full):
    c = full.shape[1] // N_CHIPS
    return jnp.stack([full[:, k * c:(k + 1) * c] for k in range(N_CHIPS)])


def kernel(x, c, positions, w_ada, b_ada, pre_norm1_g, w_in, gm_ln_g, gm_ln_b, gm_w_s, gm_b_s, w_branch_a, q_norm_g, w_uq, kv_norm_g, w_ukv, w_branch_b, w_out, post_norm1_g, pre_norm2_g, w_up, conv_w, conv_b, w_down, post_norm2_g, loss_target, m_w_ada, m_b_ada, m_pre_norm1_g, m_w_in, m_gm_ln_g, m_gm_ln_b, m_gm_w_s, m_gm_b_s, m_w_branch_a, m_q_norm_g, m_w_uq, m_kv_norm_g, m_w_ukv, m_w_branch_b, m_w_out, m_post_norm1_g, m_pre_norm2_g, m_w_up, m_conv_w, m_conv_b, m_w_down, m_post_norm2_g, v_w_ada, v_b_ada, v_pre_norm1_g, v_w_in, v_gm_ln_g, v_gm_ln_b, v_gm_w_s, v_gm_b_s, v_w_branch_a, v_q_norm_g, v_w_uq, v_kv_norm_g, v_w_ukv, v_w_branch_b, v_w_out, v_post_norm1_g, v_pre_norm2_g, v_w_up, v_conv_w, v_conv_b, v_w_down, v_post_norm2_g):
    given = dict(locals())
    s, d = x.shape[1], x.shape[2]
    gw = gm_ln_g.shape[0]
    ql, kvl = q_norm_g.shape[0], kv_norm_g.shape[0]
    heads = N_CHIPS * w_uq.shape[1] // (NOPE + ROPE)
    ff = N_CHIPS * w_down.shape[0]
    assert gw == d and N_CHIPS * w_ukv.shape[1] == heads * (NOPE + VHEAD)
    ix, iy, ic = lax.axis_index("x"), lax.axis_index("y"), lax.axis_index("c")
    chip = 2 * ix + iy
    dev = 2 * chip + ic
    row = lambda v: v.reshape(1, -1)

    c_all = _all_gather(jnp.pad(c, ((0, SUBLANES - 1), (0, 0))), "gather_c").reshape(N_DEV, SUBLANES, d)[:, 0]
    na = w_ada.shape[1]
    b_ada_mine = lax.dynamic_slice(b_ada, (chip * na,), (na,))
    mod_cols = _ada_fwd(c_all, w_ada, row(b_ada_mine), "ada_fwd")
    mod_all = _all_gather(mod_cols, "gather_mod").reshape(N_CHIPS, N_CORES, N_DEV, na)[:, 0]
    mod = lax.dynamic_index_in_dim(mod_all, dev, axis=1, keepdims=False).reshape(N_MOD, d)
    shift1, scale1, gate1, shift2, scale2, gate2 = (mod[i:i + 1] for i in range(N_MOD))

    mine = {n: (given[n].T if n == "w_in" else given[n]).astype(BF16) for n in BIG}
    gather = lambda names: _gather_comm([mine[n] for n in names], [i for i, n in enumerate(names) if n == "w_in"])
    whole = lambda n, g: lax.dynamic_update_slice(g, mine[n][None], (chip, 0, 0))
    rows4 = lambda sh4: sh4.reshape(-1, sh4.shape[2])
    wi_t = rows4(whole("w_in", _run_comm(gather(["w_in"]), "gather_w_in")[0]))
    o_q, o_kv, o_pe, o_ga = 2 * gw, 2 * gw + ql, 2 * gw + ql + kvl, 2 * gw + ql + kvl + ROPE
    w_in_big_t = _stack_rows([wi_t[:o_q], wi_t[o_ga:]])
    w_in_lat_t = _stack_rows([wi_t[o_q:o_ga], _quarter_turn(wi_t[o_pe:o_ga].T).T])

    inv = ROPE_THETA ** (-jnp.arange(0, ROPE, 2, dtype=F32) / ROPE)
    ang = positions[0].astype(F32)[:, None] * inv
    cos, sin = jnp.cos(ang), jnp.sin(ang)
    rope_k = jnp.concatenate([cos, cos, sin, sin], axis=1)
    softmax_scale = float(NOPE + ROPE) ** -0.5
    rope_q = jnp.concatenate([jnp.ones((s, NOPE), F32), rope_k], axis=1) * softmax_scale

    x2d, tgt = x[0], loss_target[0]
    g_pre1, g_post1, g_pre2, g_post2 = row(pre_norm1_g), row(post_norm1_g), row(pre_norm2_g), row(post_norm2_g)
    ln_g, ln_b, q_g, kv_g = row(gm_ln_g), row(gm_ln_b), row(q_norm_g), row(kv_norm_g)
    b_s_t = gm_b_s.T
    conv_wf = _all_gather(jnp.pad(conv_w, ((0, SUBLANES - CONV_TAPS), (0, 0))), "gather_conv_w")
    conv_wf = conv_wf.reshape(N_CHIPS, N_CORES, SUBLANES, conv_w.shape[1])[:, 0, :CONV_TAPS]
    conv_wf = conv_wf.transpose(1, 0, 2).reshape(CONV_TAPS, 2 * ff)
    conv_bf = row(conv_b)

    h1 = _prenorm(x2d, g_pre1, scale1, shift1, "prenorm1")
    z_big, (g_uq, g_ukv, g_a) = _matmul(h1, w_in_big_t, mode="nt", out_dtype=F32, name="mm_z_big", tm=s,
                                        comm=gather(["w_uq", "w_ukv", "w_branch_a"]))
    wq = _join_cols(whole("w_uq", g_uq)).reshape(ql, heads, NOPE + ROPE)
    w_q = jnp.concatenate([wq, _quarter_turn(wq[:, :, NOPE:])], axis=2).reshape(ql, heads * HEAD_W)
    w_kv = _join_cols(whole("w_ukv", g_ukv)).reshape(kvl, heads, 2, NOPE).transpose(0, 2, 1, 3)
    w_kv = w_kv.reshape(kvl, 2 * heads * NOPE)
    w_a = rows4(whole("w_branch_a", g_a))
    z_lat = _matmul(h1, w_in_lat_t, mode="nt", out_dtype=F32, name="mm_z_lat", tm=s, tn=1024)
    a_act = _gmlp_fwd(z_big, ln_g, ln_b, gm_w_s, b_s_t, "gmlp_fwd")
    qn, kvn, kr = _mla_prep(z_lat, q_g, kv_g, rope_k, "mla_prep")
    q_rot = _matmul(qn, w_q, mode="nn", out_dtype=BF16, name="mm_q", tm=s, tn=HEAD_W, mul=rope_q)
    kv_all = _matmul(kvn, w_kv, mode="nn", out_dtype=BF16, name="mm_kv", tm=s, tn=1024)
    (o_att, lse), (g_b, g_o, g_up) = _attn_fwd(q_rot, kv_all, kr, heads, "attn_fwd",
                                               comm=gather(["w_branch_b", "w_out", "w_up"]))
    w_b, w_o, w_upf = rows4(whole("w_branch_b", g_b)), rows4(whole("w_out", g_o)), whole("w_up", g_up)
    y_a = _matmul(a_act, w_a, mode="nn", out_dtype=F32, name="mm_y_a", tm=s)
    y_b = _matmul(o_att, w_b, mode="nn", out_dtype=F32, name="mm_y_b", tm=s)
    merged = _merge(z_big, y_a, y_b, "merge")
    y1 = _matmul(merged, w_o, mode="nn", out_dtype=F32, name="mm_y1", tm=s)
    x1, h2 = _post_pre(x2d, y1, gate1, g_post1, g_pre2, scale2, shift2, "post1_pre2")

    up_pre, (g_dn,) = _matmul(h2, w_upf, mode="nn", out_dtype=BF16, name="mm_up", tm=s, tn=1408,
                              comm=gather(["w_down"]))
    w_dn = rows4(whole("w_down", g_dn))
    act = _conv_fwd(up_pre, conv_wf, conv_bf, "conv_fwd")
    ffn = _matmul(act, w_dn, mode="nn", out_dtype=F32, name="mm_ffn", tm=s, tk=1408)

    dffn, dgate2, g_post2_grad, dx2, loss_part = _post_bwd(ffn, gate2, g_post2, "post2_bwd", xin=x1, target=tgt)
    loss = lax.psum(loss_part[0, 0], ("x", "y", "c"))
    place = jnp.stack([ic, chip]).astype(jnp.int32)
    rows_of = lambda g: g.reshape(N_CHIPS, g.shape[0] // N_CHIPS, g.shape[1])
    add_sibling = lambda names, gs, r1s: [_add_sibling(g, r1, place, "rs_add_sibling_" + n, by_cols=n == "w_in")
                                          for n, g, r1 in zip(names, gs, r1s)]
    add_chips = lambda names, s1s, r2s: [_add_chips(s1, r2, place, "rs_add_chips_" + n, by_cols=n == "w_in")
                                         for n, s1, r2 in zip(names, s1s, r2s)]
    gp_down = [rows_of(_matmul(act, dffn, mode="tn", out_dtype=BF16, name="mm_gw_down", tn=1024, tk=s))]
    dact, r1_down = _matmul(dffn, w_dn, mode="nt", out_dtype=BF16, name="mm_dact", tm=s, comm=_swap_comm(gp_down))
    s1_down = add_sibling(["w_down"], gp_down, r1_down)
    (dup, gcw_g, gcw_v, gcb_g, gcb_v), r2_down = _conv_bwd(up_pre, dact, conv_wf, conv_bf, "conv_bwd",
                                                            comm=_exchange_comm(s1_down))
    half_down = add_chips(["w_down"], s1_down, r2_down)
    dh2 = _matmul(dup, w_upf, mode="nt", out_dtype=F32, name="mm_dh2", tm=s, tk=1408)
    gw_up = _matmul(h2, dup, mode="tn", out_dtype=BF16, name="mm_gw_up", tn=1408, tk=s, out_groups=N_CHIPS)
    dx1, dshift2, dscale2, g_pre2_grad = _prenorm_bwd(x1, dh2, dx2, g_pre2, scale2, "prenorm2_bwd")

    dy1, dgate1, g_post1_grad = _post_bwd(y1, gate1, g_post1, "post1_bwd", dxo=dx1)
    dmerged = _matmul(dy1, w_o, mode="nt", out_dtype=F32, name="mm_dmerged", tm=s)
    gw_out = _matmul(merged, dy1, mode="tn", out_dtype=BF16, name="mm_gw_out", tn=1024, tk=s)
    dy_a, dy_b, dz_big = _merge_bwd(dmerged, z_big, y_a, y_b, "merge_bwd")
    da = _matmul(dy_a, w_a, mode="nt", out_dtype=F32, name="mm_da", tm=s)
    gw_a = _matmul(a_act, dy_a, mode="tn", out_dtype=BF16, name="mm_gw_a", tn=1024, tk=s)
    do = _matmul(dy_b, w_b, mode="nt", out_dtype=BF16, name="mm_do", tm=s)
    gw_b = _matmul(o_att, dy_b, mode="tn", out_dtype=BF16, name="mm_gw_b", tn=1024, tk=s)
    mid = ["w_up", "w_out", "w_branch_a", "w_branch_b"]
    gp_mid = [gw_up, rows_of(gw_out), rows_of(gw_a), rows_of(gw_b)]
    (dz_big, g_ws, g_bs_t, g_ln_g, g_ln_b), r1_mid = _gmlp_bwd(z_big, da, dz_big, ln_g, ln_b, gm_w_s, b_s_t,
                                                                "gmlp_bwd", comm=_swap_comm(gp_mid))
    s1_mid = add_sibling(mid, gp_mid, r1_mid)
    (dq, dk, dv), r2_up_out = _attn_bwd(q_rot, kv_all, kr, o_att, do, lse, heads, "attn_bwd",
                                        comm=_exchange_comm(s1_mid[:2]))
    dq_big, dkv, dkk = _mla_bwd_mid(dq, dk, dv, rope_q, rope_k, heads, "mla_bwd_mid")
    gw_q = _matmul(qn, dq_big, mode="tn", out_dtype=F32, name="mm_gw_q", tn=1024, tk=s)
    dqn = _matmul(dq_big, w_q, mode="nt", out_dtype=F32, name="mm_dqn", tm=s, tk=1024)
    gw_kv = _matmul(kvn, dkv, mode="tn", out_dtype=BF16, name="mm_gw_kv", tn=1024, tk=s)
    dkvn = _matmul(dkv, w_kv, mode="nt", out_dtype=F32, name="mm_dkvn", tm=s, tk=1024)
    dz_lat, g_q, g_kv = _mla_bwd_post(z_lat, dqn, dkvn, dkk, q_g, kv_g, "mla_bwd_post")

    partial = {
        "gm_ln_g": g_ln_g, "gm_ln_b": g_ln_b, "gm_w_s": g_ws, "gm_b_s": g_bs_t[:, :gm_b_s.shape[0]].T,
        "q_norm_g": g_q, "kv_norm_g": g_kv, "post_norm1_g": g_post1_grad, "pre_norm2_g": g_pre2_grad,
        "conv_w": jnp.concatenate([gcw_g, gcw_v], axis=1), "conv_b": jnp.concatenate([gcb_g, gcb_v], axis=1),
        "post_norm2_g": g_post2_grad,
    }
    flat = jnp.concatenate([partial[n].reshape(-1) for n in SMALL_PARTIAL])
    n_small = flat.shape[0]
    rows_small = -(-n_small // (LANES * SMALL_ROW_TILE)) * SMALL_ROW_TILE
    flat = jnp.pad(flat, (0, rows_small * LANES - n_small)).reshape(rows_small, LANES)

    def small_pack(prefix, source):
        v = jnp.concatenate([source[prefix + n].reshape(-1) for n in SMALL])
        rows = -(-v.shape[0] // (LANES * SUBLANES)) * SUBLANES
        return jnp.pad(v, (0, rows * LANES - v.shape[0])).reshape(rows, LANES)

    small_state = [small_pack(prefix, given) for prefix in ("", "m_", "v_")]

    dh1, r2_a_b = _matmul(dz_big, w_in_big_t, mode="nn", out_dtype=F32, name="mm_dh1_big", tm=s, tk=1024,
                          comm=_exchange_comm(s1_mid[2:]))
    half_mid = add_chips(mid, s1_mid, list(r2_up_out) + list(r2_a_b))
    dh1 = _matmul(dz_lat, w_in_lat_t, mode="nn", out_dtype=F32, name="mm_dh1_lat", tm=s, tk=1024, add=dh1)
    gw_big_t, hosted = _matmul(dz_big, h1, mode="tn", out_dtype=BF16, name="mm_gw_in_big", tn=2048, tk=s,
                               comm=_join_comms([_share_comm(half_down + half_mid), _gather8_comm(flat)]))
    shared, small_all = hosted[:-1], lax.dynamic_update_slice(hosted[-1], flat[None], (dev, 0, 0))
    small_sum = _sum_leading(small_all, "sum_small", after=small_state).reshape(-1)
    small_grads, off = {}, 0
    for n in SMALL_PARTIAL:
        shape = (CONV_TAPS, 2 * ff) if n == "conv_w" else given[n].shape
        small_grads[n] = small_sum[off:off + partial[n].size].reshape(shape)
        off += partial[n].size
    small_grads["conv_w"] = lax.dynamic_slice(small_grads["conv_w"], (0, chip * conv_w.shape[1]), conv_w.shape)
    grads = dict(zip(["w_down"] + mid, shared), **small_grads)
    gw_lat_t = _matmul(dz_lat, h1, mode="tn", out_dtype=F32, name="mm_gw_in_lat", tm=1024, tn=1024, tk=s)

    gq = gw_q.reshape(ql, heads, HEAD_W)
    gq_pe = gq[:, :, NOPE:NOPE + ROPE] + _quarter_turn_back(gq[:, :, NOPE + ROPE:])
    g_pe_t = gw_lat_t[ql + kvl:ql + kvl + ROPE] + _quarter_turn_back(gw_lat_t[ql + kvl + ROPE:].T).T
    last = ["w_in", "w_uq", "w_ukv"]
    gw_in_t = _stack_rows([gw_big_t[:o_q], gw_lat_t[:ql + kvl].astype(BF16), g_pe_t.astype(BF16), gw_big_t[o_q:]])
    gp_last = [
        gw_in_t.reshape(N_CHIPS, gw_in_t.shape[0] // N_CHIPS, d),
        _split_cols(jnp.concatenate([gq[:, :, :NOPE], gq_pe], axis=2).reshape(ql, heads * (NOPE + ROPE)).astype(BF16)),
        _split_cols(gw_kv.reshape(kvl, 2, heads, NOPE).transpose(0, 2, 1, 3).reshape(kvl, heads * 2 * NOPE)),
    ]
    (grad_x, dshift1, dscale1, g_pre1_grad), r1_last = _prenorm_bwd(x2d, dh1, dx1, g_pre1, scale1, "prenorm1_bwd",
                                                                    comm=_swap_comm(gp_last, by_cols=[0]))
    s1_last = add_sibling(last, gp_last, r1_last)

    dmod = jnp.concatenate([dshift1, dscale1, dgate1, dshift2, dscale2, dgate2, g_pre1_grad], axis=1)
    dmod_all = _all_gather(jnp.pad(dmod, ((0, SUBLANES - 1), (0, 0))), "gather_dmod")
    dmod_all = dmod_all.reshape(N_DEV, SUBLANES, (N_MOD + 1) * d)[:, 0]
    dmod_sum = _sum_leading(dmod_all.reshape(N_DEV, 1, (N_MOD + 1) * d), "sum_dmod")[0]
    grads["b_ada"], grads["pre_norm1_g"] = dmod_sum[:N_MOD * d], dmod_sum[N_MOD * d:]
    dmod_mine = lax.dynamic_slice(dmod_all, (0, chip * na), (N_DEV, na))
    grads["w_ada"] = _ada_bwd(c_all.T, dmod_mine, "ada_bwd")

    delta, new_m, new_v = {}, {}, {}

    def adamw(n, after=None):
        turn = (lambda a: a.T) if n == "w_in" else (lambda a: a)
        outs = _adamw(turn(given[n]), grads[n], turn(given["m_" + n]), turn(given["v_" + n]), "adamw_" + n,
                      after=after)
        grads[n] = turn(grads[n])
        delta[n], new_m[n], new_v[n] = (turn(o) for o in outs)

    exchange_last = _exchange_comm(s1_last)
    in_flight, token = _comm_split_start(exchange_last, "rs_exchange_last_start", after=[dmod_sum, small_sum])
    for n in ["w_ada", "w_down"] + mid:
        adamw(n, after=token)
    s1_last, r2_last = _comm_split_wait(exchange_last, in_flight, delta[mid[-1]], "rs_exchange_last_wait")
    half_last = add_chips(last, s1_last, r2_last)
    grads.update(zip(last, _run_comm(_share_comm(half_last, by_cols=[0]), "rs_share_last")))
    for n in last:
        adamw(n)

    outs = _adamw(small_state[0], small_pack("", grads), small_state[1], small_state[2], "adamw_small")
    off = 0
    for n in SMALL:
        size = given[n].size
        for store, packed_out in zip((delta, new_m, new_v), outs):
            store[n] = packed_out.reshape(-1)[off:off + size].reshape(given[n].shape)
        off += size

    return (loss, grad_x[None], *[grads[n] for n in WEIGHTS], *[delta[n] for n in WEIGHTS],
            *[new_m[n] for n in WEIGHTS], *[new_v[n] for n in WEIGHTS])
```

```python
import functools

import jax
import jax.numpy as jnp
from jax import lax
from jax.experimental import pallas as pl
from jax.experimental.pallas import tpu as pltpu

F32 = jnp.float32
BF16 = jnp.bfloat16
MESH = pl.DeviceIdType.MESH
HBM = pltpu.HBM

EPS = 1e-6
NOPE, ROPE, VHEAD = 128, 64, 128
HEAD_W = NOPE + 2 * ROPE
ROPE_THETA = 10000.0
CONV_TAPS = 3
N_MOD = 6
N_CHIPS, N_CORES, N_DEV = 4, 2, 8
ADAM_LR, ADAM_B1, ADAM_B2, ADAM_EPS, ADAM_WD, ADAM_STEP = 0.001, 0.9, 0.999, 1e-08, 0.01, 10

LANES = 128
SUBLANES = 8
VMEM_LIMIT = 56 * 2**20
MIDDLE_STAGE_AT = 70
SMALL_ROW_TILE = 256

BIG = ("w_in", "w_branch_a", "w_uq", "w_ukv", "w_branch_b", "w_out", "w_up", "w_down")
WEIGHTS = ("w_ada", "b_ada", "pre_norm1_g", "w_in", "gm_ln_g", "gm_ln_b", "gm_w_s", "gm_b_s", "w_branch_a",
           "q_norm_g", "w_uq", "kv_norm_g", "w_ukv", "w_branch_b", "w_out", "post_norm1_g", "pre_norm2_g",
           "w_up", "conv_w", "conv_b", "w_down", "post_norm2_g")
SMALL_PARTIAL = ("gm_ln_g", "gm_ln_b", "gm_w_s", "gm_b_s", "q_norm_g", "kv_norm_g", "post_norm1_g",
                 "pre_norm2_g", "conv_w", "conv_b", "post_norm2_g")
SMALL = ("b_ada", "pre_norm1_g") + SMALL_PARTIAL


def _div_tile(n, cap, mult=LANES):
    t = (min(cap, n) // mult) * mult
    while t >= mult:
        if n % t == 0:
            return t
        t -= mult
    return n


def _params(**kw):
    return pltpu.CompilerParams(vmem_limit_bytes=VMEM_LIMIT, **kw)


def _row_spec(width):
    return pl.BlockSpec((1, width), lambda *_: (0, 0))


def _gelu(x):
    k = 0.7978845608028654
    return 0.5 * x * (1.0 + jnp.tanh(k * (x + 0.044715 * x * x * x)))


def _gelu_grad(x):
    k = 0.7978845608028654
    t = jnp.tanh(k * (x + 0.044715 * x * x * x))
    return 0.5 * (1.0 + t) + 0.5 * x * (1.0 - t * t) * k * (1.0 + 3.0 * 0.044715 * x * x)


def _sigmoid(x):
    return 0.5 * jnp.tanh(0.5 * x) + 0.5


def _dot(a, b, dims):
    return lax.dot_general(a, b, (dims, ((), ())), preferred_element_type=F32)


NN = ((1,), (0,))
NT = ((1,), (1,))
TN = ((0,), (0,))


def _logical(arr):
    if arr.ndim == 2:
        return arr.shape[0], arr.shape[1], arr.shape[1]
    return arr.shape[1], arr.shape[0] * arr.shape[2], arr.shape[2]


def _tile_spec(ndim, group_w, blk_rows, blk_cols, row_of, col_of):
    if ndim == 2:
        return pl.BlockSpec((blk_rows, blk_cols), lambda i, j, k: (row_of(i, j, k), col_of(i, j, k)))
    per = group_w // blk_cols
    return pl.BlockSpec((None, blk_rows, blk_cols),
                        lambda i, j, k: (col_of(i, j, k) // per, row_of(i, j, k), col_of(i, j, k) % per))


def _matmul(a, b, *, mode, out_dtype, name, tm=512, tn=512, tk=2048, mul=None, add=None, out_groups=None, comm=None):
    ar, ac, agw = _logical(a)
    br, bc, bgw = _logical(b)
    if mode == "nn":
        m, kd, n = ar, ac, bc
        m_w, k_w, n_w = (), (agw,), (bgw,)
    elif mode == "nt":
        m, kd, n = ar, ac, br
        m_w, k_w, n_w = (), (agw, bgw), ()
    else:
        m, kd, n = ac, ar, bc
        m_w, k_w, n_w = (agw,), (), (bgw,)
    if out_groups is not None:
        n_w = n_w + (n // out_groups,)
    tm = _div_tile(min((m,) + m_w), tm, LANES if mode == "tn" else SUBLANES)
    tn = _div_tile(min((n,) + n_w), tn)
    tk = _div_tile(min((kd,) + k_w), tk)
    assert all(w % tn == 0 for w in n_w) and all(w % tk == 0 for w in k_w) and all(w % tm == 0 for w in m_w)
    nk = kd // tk
    dims = {"nn": NN, "nt": NT, "tn": TN}[mode]
    gi, gj, gk = (lambda i, j, k: i), (lambda i, j, k: j), (lambda i, j, k: k)
    if mode == "nn":
        a_spec = _tile_spec(a.ndim, agw, tm, tk, gi, gk)
        b_spec = _tile_spec(b.ndim, bgw, tk, tn, gk, gj)
    elif mode == "nt":
        a_spec = _tile_spec(a.ndim, agw, tm, tk, gi, gk)
        b_spec = _tile_spec(b.ndim, bgw, tn, tk, gj, gk)
    else:
        a_spec = _tile_spec(a.ndim, agw, tk, tm, gk, gi)
        b_spec = _tile_spec(b.ndim, bgw, tk, tn, gk, gj)
    in_specs, operands = [a_spec, b_spec], [a, b]
    if mul is not None:
        assert mul.shape == (m, tn)
        in_specs.append(pl.BlockSpec((tm, tn), lambda i, j, k: (i, 0)))
        operands.append(mul)
    if add is not None:
        in_specs.append(pl.BlockSpec((tm, tn), lambda i, j, k: (i, j)))
        operands.append(add)

    def body(*refs):
        a_ref, b_ref = refs[0], refs[1]
        pos = 2
        mul_ref = add_ref = None
        if mul is not None:
            mul_ref, pos = refs[pos], pos + 1
        if add is not None:
            add_ref, pos = refs[pos], pos + 1
        o_ref = refs[pos]

        def finish(r):
            if mul_ref is not None:
                r = r * mul_ref[...]
            if add_ref is not None:
                r = r + add_ref[...]
            o_ref[...] = r.astype(out_dtype)

        part = _dot(a_ref[...], b_ref[...], dims)
        if nk == 1:
            finish(part)
        else:
            acc_ref = refs[pos + 1]
            k = pl.program_id(2)

            @pl.when(k == 0)
            def _():
                acc_ref[...] = part

            @pl.when(k > 0)
            def _():
                acc_ref[...] += part

            @pl.when(k == nk - 1)
            def _():
                finish(acc_ref[...])

    if out_groups is None:
        out_spec, out_dims = _tile_spec(2, n, tm, tn, gi, gj), (m, n)
    else:
        out_spec, out_dims = _tile_spec(3, n // out_groups, tm, tn, gi, gj), (out_groups, m, n // out_groups)
    return _call(body, operands, comm, name=name, grid=(m // tm, n // tn, nk), in_specs=in_specs, out_specs=out_spec,
                 out_shape=jax.ShapeDtypeStruct(out_dims, out_dtype),
                 scratch_shapes=[] if nk == 1 else [pltpu.VMEM((tm, tn), F32)])


def _accumulate(ref, value):
    @pl.when(pl.program_id(0) == 0)
    def _():
        ref[...] = value

    @pl.when(pl.program_id(0) > 0)
    def _():
        ref[...] += value


def _colsum(v):
    return jnp.sum(v, axis=0, keepdims=True)


def _rowmean(v):
    return jnp.mean(v, axis=-1, keepdims=True)


def _prenorm(x, g, scale, shift, name):
    s, d = x.shape
    tb = _div_tile(s, 256, SUBLANES)

    def body(x_ref, g_ref, sc_ref, sh_ref, h_ref):
        xv = x_ref[...]
        r = lax.rsqrt(_rowmean(xv * xv) + EPS)
        h_ref[...] = ((xv * r) * g_ref[...] * (1.0 + sc_ref[...]) + sh_ref[...]).astype(BF16)

    blk = pl.BlockSpec((tb, d), lambda i: (i, 0))
    return pl.pallas_call(
        body, name=name, grid=(s // tb,), in_specs=[blk, _row_spec(d), _row_spec(d), _row_spec(d)],
        out_specs=blk, out_shape=jax.ShapeDtypeStruct((s, d), BF16), compiler_params=_params(),
    )(x, g, scale, shift)


def _post_pre(x, y, gate, pg, g2, scale2, shift2, name):
    s, d = x.shape
    tb = _div_tile(s, 256, SUBLANES)

    def body(x_ref, y_ref, gate_ref, pg_ref, g2_ref, sc_ref, sh_ref, x1_ref, h2_ref):
        yv = y_ref[...]
        rp = lax.rsqrt(_rowmean(yv * yv) + EPS)
        x1 = x_ref[...] + gate_ref[...] * ((yv * rp) * pg_ref[...])
        x1_ref[...] = x1
        r2 = lax.rsqrt(_rowmean(x1 * x1) + EPS)
        h2_ref[...] = ((x1 * r2) * g2_ref[...] * (1.0 + sc_ref[...]) + sh_ref[...]).astype(BF16)

    blk = pl.BlockSpec((tb, d), lambda i: (i, 0))
    return pl.pallas_call(
        body, name=name, grid=(s // tb,), in_specs=[blk, blk] + [_row_spec(d)] * 5,
        out_specs=[blk, blk],
        out_shape=[jax.ShapeDtypeStruct((s, d), F32), jax.ShapeDtypeStruct((s, d), BF16)],
        compiler_params=_params(),
    )(x, y, gate, pg, g2, scale2, shift2)


def _post_bwd(y, gate, pg, name, *, dxo=None, xin=None, target=None):
    s, d = y.shape
    tb = _div_tile(s, 256, SUBLANES)
    from_loss = target is not None

    def body(*refs):
        if from_loss:
            y_ref, gate_ref, pg_ref, xin_ref, t_ref, dy_ref, dgate_ref, dpg_ref, dxo_ref, loss_ref = refs
        else:
            y_ref, gate_ref, pg_ref, dxo_in_ref, dy_ref, dgate_ref, dpg_ref = refs
        yv = y_ref[...]
        rp = lax.rsqrt(_rowmean(yv * yv) + EPS)
        yh = yv * rp
        fn = yh * pg_ref[...]
        gate = gate_ref[...]
        if from_loss:
            err = xin_ref[...] + gate * fn - t_ref[...]
            dxo = err * (1.0 / d)
            dxo_ref[...] = dxo
            part = 0.5 * jnp.sum(_rowmean(err * err), axis=0, keepdims=True)
            _accumulate(loss_ref, jnp.broadcast_to(part, loss_ref.shape))
        else:
            dxo = dxo_in_ref[...]
        _accumulate(dgate_ref, _colsum(dxo * fn))
        dfn = dxo * gate
        _accumulate(dpg_ref, _colsum(dfn * yh))
        dyh = dfn * pg_ref[...]
        dy_ref[...] = (rp * (dyh - yh * _rowmean(dyh * yh))).astype(BF16)

    blk = pl.BlockSpec((tb, d), lambda i: (i, 0))
    in_specs = [blk, _row_spec(d), _row_spec(d)]
    out_specs = [blk, _row_spec(d), _row_spec(d)]
    out_shape = [jax.ShapeDtypeStruct((s, d), BF16), jax.ShapeDtypeStruct((1, d), F32),
                 jax.ShapeDtypeStruct((1, d), F32)]
    if from_loss:
        operands = (y, gate, pg, xin, target)
        in_specs += [blk, blk]
        out_specs += [blk, _row_spec(LANES)]
        out_shape += [jax.ShapeDtypeStruct((s, d), F32), jax.ShapeDtypeStruct((1, LANES), F32)]
    else:
        operands = (y, gate, pg, dxo)
        in_specs += [blk]
    return pl.pallas_call(
        body, name=name, grid=(s // tb,), in_specs=in_specs, out_specs=out_specs, out_shape=out_shape,
        compiler_params=_params(),
    )(*operands)


def _prenorm_bwd(xin, dh, dres, g, scale, name, comm=None):
    s, d = xin.shape
    tb = _div_tile(s, 256, SUBLANES)

    def body(x_ref, dh_ref, dres_ref, g_ref, sc_ref, dx_ref, dshift_ref, dscale_ref, dg_ref):
        xv = x_ref[...]
        r = lax.rsqrt(_rowmean(xv * xv) + EPS)
        xn = xv * r
        dh = dh_ref[...]
        g1 = g_ref[...]
        s1 = 1.0 + sc_ref[...]
        _accumulate(dshift_ref, _colsum(dh))
        _accumulate(dscale_ref, _colsum(dh * xn * g1))
        _accumulate(dg_ref, _colsum(dh * xn * s1))
        dxn = dh * g1 * s1
        dx_ref[...] = dres_ref[...] + r * (dxn - xn * _rowmean(dxn * xn))

    blk = pl.BlockSpec((tb, d), lambda i: (i, 0))
    return _call(
        body, (xin, dh, dres, g, scale), comm, name=name, grid=(s // tb,),
        in_specs=[blk, blk, blk, _row_spec(d), _row_spec(d)],
        out_specs=[blk, _row_spec(d), _row_spec(d), _row_spec(d)],
        out_shape=[jax.ShapeDtypeStruct((s, d), F32)] + [jax.ShapeDtypeStruct((1, d), F32)] * 3)


def _merge(z_big, y_a, y_b, name):
    s, d = y_a.shape
    tb = _div_tile(s, 256, SUBLANES)

    def body(zg_ref, ya_ref, yb_ref, o_ref):
        ga, gb = zg_ref[:, :d].astype(F32), zg_ref[:, d:].astype(F32)
        o_ref[...] = (_sigmoid(ga) * ya_ref[...].astype(F32) + _sigmoid(gb) * yb_ref[...].astype(F32)).astype(BF16)

    blk = pl.BlockSpec((tb, d), lambda i: (i, 0))
    return pl.pallas_call(
        body, name=name, grid=(s // tb,), in_specs=[pl.BlockSpec((tb, 2 * d), lambda i: (i, 1)), blk, blk],
        out_specs=blk, out_shape=jax.ShapeDtypeStruct((s, d), BF16), compiler_params=_params(),
    )(z_big, y_a, y_b)


def _merge_bwd(dmerged, z_big, y_a, y_b, name):
    s, d = y_a.shape
    tb = _div_tile(s, 256, SUBLANES)

    def body(dm_ref, zg_ref, ya_ref, yb_ref, dya_ref, dyb_ref, dz_ref):
        dm = dm_ref[...].astype(F32)
        sa, sb = _sigmoid(zg_ref[:, :d].astype(F32)), _sigmoid(zg_ref[:, d:].astype(F32))
        dya_ref[...] = (dm * sa).astype(BF16)
        dyb_ref[...] = (dm * sb).astype(BF16)
        dz_ref[:, :d] = (dm * ya_ref[...].astype(F32) * sa * (1.0 - sa)).astype(BF16)
        dz_ref[:, d:] = (dm * yb_ref[...].astype(F32) * sb * (1.0 - sb)).astype(BF16)

    blk = pl.BlockSpec((tb, d), lambda i: (i, 0))
    wide = pl.BlockSpec((tb, 2 * d), lambda i: (i, 1))
    return pl.pallas_call(
        body, name=name, grid=(s // tb,), in_specs=[blk, wide, blk, blk], out_specs=[blk, blk, wide],
        out_shape=[jax.ShapeDtypeStruct((s, d), BF16), jax.ShapeDtypeStruct((s, d), BF16),
                   jax.ShapeDtypeStruct((s, 4 * d), BF16)],
        compiler_params=_params(),
    )(dmerged, z_big, y_a, y_b)


def _causal_mask(ch):
    q = lax.broadcasted_iota(jnp.int32, (ch, ch), 0)
    p = lax.broadcasted_iota(jnp.int32, (ch, ch), 1)
    return (p <= q).astype(F32)


def _gmlp_norm(zc, lng, lnb, gw):
    u_pre, v_pre = zc[:, :gw], zc[:, gw:]
    vg = _gelu(v_pre)
    mu = _rowmean(vg)
    cen = vg - mu
    rstd = lax.rsqrt(_rowmean(cen * cen) + EPS)
    vhat = cen * rstd
    return u_pre, v_pre, _gelu(u_pre), vhat, rstd, vhat * lng + lnb


def _gmlp_fwd(z_big, ln_g, ln_b, w_s, b_s_t, name):
    s = z_big.shape[0]
    groups, ch, _ = w_s.shape
    gw = ln_g.shape[1]
    gd = gw // groups

    def body(z_ref, lng_ref, lnb_ref, ws_ref, bt_ref, a_ref):
        _, _, u, _, _, vn = _gmlp_norm(z_ref[...].astype(F32), lng_ref[...], lnb_ref[...], gw)
        mask = _causal_mask(ch)
        for g in range(groups):
            cols = slice(g * gd, (g + 1) * gd)
            wm = (ws_ref[g] * mask).astype(BF16)
            mixed = _dot(wm, vn[:, cols].astype(BF16), NN) + bt_ref[:, g:g + 1]
            a_ref[:, cols] = (u[:, cols] * mixed).astype(BF16)

    return pl.pallas_call(
        body, name=name, grid=(s // ch,),
        in_specs=[pl.BlockSpec((ch, 2 * gw), lambda n: (n, 0)), _row_spec(gw), _row_spec(gw),
                  pl.BlockSpec((groups, ch, ch), lambda n: (0, 0, 0)), pl.BlockSpec((ch, groups), lambda n: (0, 0))],
        out_specs=pl.BlockSpec((ch, gw), lambda n: (n, 0)),
        out_shape=jax.ShapeDtypeStruct((s, gw), BF16), compiler_params=_params(),
    )(z_big, ln_g, ln_b, w_s, b_s_t)


def _gmlp_bwd(z_big, da, dz_big, ln_g, ln_b, w_s, b_s_t, name, comm=None):
    s = z_big.shape[0]
    groups, ch, _ = w_s.shape
    gw = ln_g.shape[1]
    gd = gw // groups

    def body(z_ref, da_ref, dzin_ref, lng_ref, lnb_ref, ws_ref, bt_ref, dz_ref, gws_ref, gbt_ref, glng_ref, glnb_ref):
        del dzin_ref
        lng = lng_ref[...]
        u_pre, v_pre, u, vhat, rstd, vn = _gmlp_norm(z_ref[...].astype(F32), lng, lnb_ref[...], gw)
        da = da_ref[...].astype(F32)
        mask = _causal_mask(ch)
        first = pl.program_id(0) == 0
        dvn_parts = []
        lane = lax.broadcasted_iota(jnp.int32, (ch, LANES), 1)
        gb = jnp.zeros((ch, LANES), F32)
        for g in range(groups):
            cols = slice(g * gd, (g + 1) * gd)
            wm = (ws_ref[g] * mask).astype(BF16)
            vn_g = vn[:, cols].astype(BF16)
            mixed = _dot(wm, vn_g, NN) + bt_ref[:, g:g + 1]
            dz_ref[:, cols] = (da[:, cols] * mixed * _gelu_grad(u_pre[:, cols])).astype(BF16)
            dmixed = da[:, cols] * u[:, cols]
            dm16 = dmixed.astype(BF16)
            dvn_parts.append(_dot(wm, dm16, TN))
            gws = _dot(dm16, vn_g, NT) * mask

            @pl.when(first)
            def _(g=g, gws=gws):
                gws_ref[g] = gws

            @pl.when(jnp.logical_not(first))
            def _(g=g, gws=gws):
                gws_ref[g] += gws

            gb = gb + jnp.where(lane == g, jnp.sum(dmixed, axis=1, keepdims=True), 0.0)
        _accumulate(gbt_ref, gb)
        dvn = jnp.concatenate(dvn_parts, axis=1)
        _accumulate(glnb_ref, _colsum(dvn))
        _accumulate(glng_ref, _colsum(dvn * vhat))
        dvh = dvn * lng
        dvg = rstd * (dvh - _rowmean(dvh) - vhat * _rowmean(dvh * vhat))
        dz_ref[:, gw:] = (dvg * _gelu_grad(v_pre)).astype(BF16)

    zspec = pl.BlockSpec((ch, 2 * gw), lambda n: (n, 0))
    return _call(
        body, (z_big, da, dz_big, ln_g, ln_b, w_s, b_s_t), comm, name=name, grid=(s // ch,),
        in_specs=[zspec, pl.BlockSpec((ch, gw), lambda n: (n, 0)), pl.BlockSpec(memory_space=HBM),
                  _row_spec(gw), _row_spec(gw), pl.BlockSpec((groups, ch, ch), lambda n: (0, 0, 0)),
                  pl.BlockSpec((ch, groups), lambda n: (0, 0))],
        out_specs=[zspec, pl.BlockSpec((groups, ch, ch), lambda n: (0, 0, 0)),
                   pl.BlockSpec((ch, LANES), lambda n: (0, 0)), _row_spec(gw), _row_spec(gw)],
        out_shape=[jax.ShapeDtypeStruct(dz_big.shape, BF16), jax.ShapeDtypeStruct((groups, ch, ch), F32),
                   jax.ShapeDtypeStruct((ch, LANES), F32), jax.ShapeDtypeStruct((1, gw), F32),
                   jax.ShapeDtypeStruct((1, gw), F32)],
        input_output_aliases={2: 0})


def _mla_prep(z_lat, q_g, kv_g, rope_k, name):
    s, latw = z_lat.shape
    ql, kvl = q_g.shape[1], kv_g.shape[1]
    tb = _div_tile(s, 256, SUBLANES)

    def body(z_ref, qg_ref, kvg_ref, t_ref, qn_ref, kvn_ref, kr_ref):
        q = z_ref[:, :ql]
        qn_ref[...] = ((q * lax.rsqrt(_rowmean(q * q) + EPS)) * qg_ref[...]).astype(BF16)
        kv = z_ref[:, ql:ql + kvl]
        kvn_ref[...] = ((kv * lax.rsqrt(_rowmean(kv * kv) + EPS)) * kvg_ref[...]).astype(BF16)
        kk = z_ref[:, ql + kvl:] * t_ref[...]
        kr_ref[...] = (kk + pltpu.roll(kk, ROPE, axis=1)).astype(BF16)

    return pl.pallas_call(
        body, name=name, grid=(s // tb,),
        in_specs=[pl.BlockSpec((tb, latw), lambda i: (i, 0)), _row_spec(ql), _row_spec(kvl),
                  pl.BlockSpec((tb, 2 * ROPE), lambda i: (i, 0))],
        out_specs=[pl.BlockSpec((tb, ql), lambda i: (i, 0)), pl.BlockSpec((tb, kvl), lambda i: (i, 0)),
                   pl.BlockSpec((tb, 2 * ROPE), lambda i: (i, 0))],
        out_shape=[jax.ShapeDtypeStruct((s, ql), BF16), jax.ShapeDtypeStruct((s, kvl), BF16),
                   jax.ShapeDtypeStruct((s, 2 * ROPE), BF16)],
        compiler_params=_params(),
    )(z_lat, q_g, kv_g, rope_k)


def _scores(q, k_full, on_diagonal):
    s = _dot(q, k_full, NT)
    if not on_diagonal:
        return s
    rows = lax.broadcasted_iota(jnp.int32, s.shape, 0)
    cols = lax.broadcasted_iota(jnp.int32, s.shape, 1)
    return jnp.where(cols <= rows, s, -1e30)


def _attn_fwd(q, kv, kr, heads, name, comm=None):
    s = q.shape[0]
    t = _div_tile(s, 512)
    nb = s // t
    hp = 2 if heads % 2 == 0 else 1

    def body(q_ref, k_ref, kr_ref, v_ref, o_ref, lse_ref, m_ref, l_ref, acc_ref):
        i, j = pl.program_id(1), pl.program_id(2)

        @pl.when(j == 0)
        def _():
            m_ref[...] = jnp.full(m_ref.shape, -1e30, F32)
            l_ref[...] = jnp.zeros(l_ref.shape, F32)
            acc_ref[...] = jnp.zeros(acc_ref.shape, F32)

        def step(on_diagonal):
            krv = kr_ref[...]
            for h in range(hp):
                vc = slice(h * VHEAD, (h + 1) * VHEAD)
                k_full = jnp.concatenate([k_ref[:, h * NOPE:(h + 1) * NOPE], krv], axis=1)
                sc = _scores(q_ref[:, h * HEAD_W:(h + 1) * HEAD_W], k_full, on_diagonal)
                m_old = m_ref[h]
                m_new = jnp.maximum(m_old, jnp.max(sc, axis=-1, keepdims=True))
                p = jnp.exp(sc - m_new)
                alpha = jnp.exp(m_old - m_new)
                l_new = alpha * l_ref[h] + jnp.sum(p, axis=-1, keepdims=True)
                acc = alpha * acc_ref[:, vc] + _dot(p.astype(BF16), v_ref[:, vc], NN)
                if on_diagonal:
                    o_ref[:, vc] = (acc / l_new).astype(BF16)
                    lse_ref[h] = jnp.broadcast_to(m_new + jnp.log(l_new), (t, LANES))
                else:
                    m_ref[h], l_ref[h], acc_ref[:, vc] = m_new, l_new, acc

        pl.when(j < i)(lambda: step(False))
        pl.when(j == i)(lambda: step(True))

    kidx = lambda off: (lambda h, i, j: (jnp.minimum(i, j), off(h)))
    return _call(
        body, (q, kv, kr, kv), comm, name=name, grid=(heads // hp, nb, nb),
        in_specs=[pl.BlockSpec((t, hp * HEAD_W), lambda h, i, j: (i, h)),
                  pl.BlockSpec((t, hp * NOPE), kidx(lambda h: h)),
                  pl.BlockSpec((t, 2 * ROPE), kidx(lambda h: 0)),
                  pl.BlockSpec((t, hp * VHEAD), kidx(lambda h: heads // hp + h))],
        out_specs=[pl.BlockSpec((t, hp * VHEAD), lambda h, i, j: (i, h)),
                   pl.BlockSpec((hp, t, LANES), lambda h, i, j: (h, i, 0))],
        out_shape=[jax.ShapeDtypeStruct((s, heads * VHEAD), BF16), jax.ShapeDtypeStruct((heads, s, LANES), F32)],
        scratch_shapes=[pltpu.VMEM((hp, t, 1), F32), pltpu.VMEM((hp, t, 1), F32), pltpu.VMEM((t, hp * VHEAD), F32)])


def _attn_bwd(q, kv, kr, o, do, lse, heads, name, comm=None):
    s = q.shape[0]
    t = _div_tile(s, 512)
    nb = s // t
    hp = 2 if heads % 2 == 0 else 1

    def body(q_ref, k_ref, kr_ref, v_ref, o_ref, do_ref, lse_ref, dq_ref, dk_ref, dv_ref, dk_acc, dv_acc):
        j, i = pl.program_id(1), pl.program_id(2)

        @pl.when(jnp.logical_and(j == 0, i == 0))
        def _():
            dq_ref[...] = jnp.zeros(dq_ref.shape, F32)

        def step(on_diagonal):
            krv = kr_ref[...]
            rows = pl.ds(pl.multiple_of(i * t, t), t)
            for h in range(hp):
                qc, kc, vc = (slice(h * w, (h + 1) * w) for w in (HEAD_W, NOPE, VHEAD))
                qv, do_v = q_ref[:, qc], do_ref[:, vc]
                k_full = jnp.concatenate([k_ref[:, kc], krv], axis=1)
                p = jnp.exp(_scores(qv, k_full, on_diagonal) - lse_ref[h][:, :1])
                dp = _dot(do_v, v_ref[:, vc], NT)
                delta = jnp.sum(do_v.astype(F32) * o_ref[:, vc].astype(F32), axis=-1, keepdims=True)
                ds = (p * (dp - delta)).astype(BF16)
                dq_ref[rows, qc] += _dot(ds, k_full, NN)
                dv_part, dk_part = _dot(p.astype(BF16), do_v, TN), _dot(ds, qv, TN)
                if on_diagonal:
                    dv_acc[:, vc], dk_acc[:, qc] = dv_part, dk_part
                else:
                    dv_acc[:, vc] += dv_part
                    dk_acc[:, qc] += dk_part

        pl.when(i == j)(lambda: step(True))
        pl.when(i > j)(lambda: step(False))

        @pl.when(i == nb - 1)
        def _():
            dk_ref[...] = dk_acc[...].astype(BF16)
            dv_ref[...] = dv_acc[...].astype(BF16)

    qidx = lambda h, j, i: (jnp.maximum(i, j), h)
    return _call(
        body, (q, kv, kr, kv, o, do, lse), comm, name=name, grid=(heads // hp, nb, nb),
        in_specs=[pl.BlockSpec((t, hp * HEAD_W), qidx),
                  pl.BlockSpec((t, hp * NOPE), lambda h, j, i: (j, h)),
                  pl.BlockSpec((t, 2 * ROPE), lambda h, j, i: (j, 0)),
                  pl.BlockSpec((t, hp * VHEAD), lambda h, j, i: (j, heads // hp + h)),
                  pl.BlockSpec((t, hp * VHEAD), qidx), pl.BlockSpec((t, hp * VHEAD), qidx),
                  pl.BlockSpec((hp, t, LANES), lambda h, j, i: (h, jnp.maximum(i, j), 0))],
        out_specs=[pl.BlockSpec((s, hp * HEAD_W), lambda h, j, i: (0, h)),
                   pl.BlockSpec((t, hp * HEAD_W), lambda h, j, i: (j, h)),
                   pl.BlockSpec((t, hp * VHEAD), lambda h, j, i: (j, h))],
        out_shape=[jax.ShapeDtypeStruct((s, heads * HEAD_W), F32), jax.ShapeDtypeStruct((s, heads * HEAD_W), BF16),
                   jax.ShapeDtypeStruct((s, heads * VHEAD), BF16)],
        scratch_shapes=[pltpu.VMEM((t, hp * HEAD_W), F32), pltpu.VMEM((t, hp * VHEAD), F32)])


def _mla_bwd_mid(dq, dk, dv, rope_q, rope_k, heads, name):
    s = dq.shape[0]
    tb = _div_tile(s, 256, SUBLANES)

    def body(dq_ref, dk_ref, dv_ref, tq_ref, tk_ref, dqb_ref, dkv_ref, dkk_ref):
        tq = tq_ref[...]
        dkr = jnp.zeros((tb, 2 * ROPE), F32)
        for h in range(heads):
            cols = slice(h * HEAD_W, (h + 1) * HEAD_W)
            dqb_ref[:, cols] = (dq_ref[:, cols] * tq).astype(BF16)
            dkv_ref[:, h * NOPE:(h + 1) * NOPE] = dk_ref[:, h * HEAD_W:h * HEAD_W + NOPE]
            dkr = dkr + dk_ref[:, h * HEAD_W + NOPE:(h + 1) * HEAD_W].astype(F32)
        dkv_ref[:, heads * NOPE:] = dv_ref[...]
        dkk_ref[...] = (dkr + pltpu.roll(dkr, ROPE, axis=1)) * tk_ref[...]

    wq, wv = heads * HEAD_W, heads * VHEAD
    return pl.pallas_call(
        body, name=name, grid=(s // tb,),
        in_specs=[pl.BlockSpec((tb, wq), lambda i: (i, 0)), pl.BlockSpec((tb, wq), lambda i: (i, 0)),
                  pl.BlockSpec((tb, wv), lambda i: (i, 0)), pl.BlockSpec((tb, HEAD_W), lambda i: (i, 0)),
                  pl.BlockSpec((tb, 2 * ROPE), lambda i: (i, 0))],
        out_specs=[pl.BlockSpec((tb, wq), lambda i: (i, 0)), pl.BlockSpec((tb, heads * NOPE + wv), lambda i: (i, 0)),
                   pl.BlockSpec((tb, 2 * ROPE), lambda i: (i, 0))],
        out_shape=[jax.ShapeDtypeStruct((s, wq), BF16), jax.ShapeDtypeStruct((s, heads * NOPE + wv), BF16),
                   jax.ShapeDtypeStruct((s, 2 * ROPE), F32)],
        compiler_params=_params(),
    )(dq, dk, dv, rope_q, rope_k)


def _mla_bwd_post(z_lat, dqn, dkvn, dkk, q_g, kv_g, name):
    s, latw = z_lat.shape
    ql, kvl = q_g.shape[1], kv_g.shape[1]
    tb = _div_tile(s, 256, SUBLANES)

    def norm_bwd(xv, dn, g, dg_ref):
        r = lax.rsqrt(_rowmean(xv * xv) + EPS)
        xh = xv * r
        _accumulate(dg_ref, _colsum(dn * xh))
        dxh = dn * g
        return r * (dxh - xh * _rowmean(dxh * xh))

    def body(z_ref, dqn_ref, dkvn_ref, dkk_ref, qg_ref, kvg_ref, dz_ref, gq_ref, gkv_ref):
        dz_ref[:, :ql] = norm_bwd(z_ref[:, :ql], dqn_ref[...], qg_ref[...], gq_ref).astype(BF16)
        dz_ref[:, ql:ql + kvl] = norm_bwd(z_ref[:, ql:ql + kvl], dkvn_ref[...], kvg_ref[...], gkv_ref).astype(BF16)
        dz_ref[:, ql + kvl:] = dkk_ref[...].astype(BF16)

    return pl.pallas_call(
        body, name=name, grid=(s // tb,),
        in_specs=[pl.BlockSpec((tb, latw), lambda i: (i, 0)), pl.BlockSpec((tb, ql), lambda i: (i, 0)),
                  pl.BlockSpec((tb, kvl), lambda i: (i, 0)), pl.BlockSpec((tb, 2 * ROPE), lambda i: (i, 0)),
                  _row_spec(ql), _row_spec(kvl)],
        out_specs=[pl.BlockSpec((tb, latw), lambda i: (i, 0)), _row_spec(ql), _row_spec(kvl)],
        out_shape=[jax.ShapeDtypeStruct((s, latw), BF16), jax.ShapeDtypeStruct((1, ql), F32),
                   jax.ShapeDtypeStruct((1, kvl), F32)],
        compiler_params=_params(),
    )(z_lat, dqn, dkvn, dkk, q_g, kv_g)


def _shift_down(x, n):
    rows = lax.broadcasted_iota(jnp.int32, x.shape, 0)
    return jnp.where(rows >= n, pltpu.roll(x, n, axis=0), 0.0)


def _shift_up(x, n):
    s = x.shape[0]
    rows = lax.broadcasted_iota(jnp.int32, x.shape, 0)
    return jnp.where(rows < s - n, pltpu.roll(x, s - n, axis=0), 0.0)


def _conv(pre, w_ref, b_ref):
    return (w_ref[2:3, :] * pre + w_ref[1:2, :] * _shift_down(pre, 1) + w_ref[0:1, :] * _shift_down(pre, 2)
            + b_ref[...])


def _conv_fwd(up_pre, conv_w, conv_b, name):
    s, ff2 = up_pre.shape
    ff = ff2 // 2
    tc = _div_tile(ff, 256)
    nb = ff // tc

    def body(pg_ref, pv_ref, wg_ref, wv_ref, bg_ref, bv_ref, act_ref):
        gate = _conv(pg_ref[...].astype(F32), wg_ref, bg_ref)
        val = _conv(pv_ref[...].astype(F32), wv_ref, bv_ref)
        act_ref[...] = (gate * _sigmoid(gate) * val).astype(BF16)

    def col(rows, off):
        return pl.BlockSpec((rows, tc), lambda j: (0, j + off))

    return pl.pallas_call(
        body, name=name, grid=(nb,),
        in_specs=[col(s, 0), col(s, nb), col(CONV_TAPS, 0), col(CONV_TAPS, nb), col(1, 0), col(1, nb)],
        out_specs=col(s, 0), out_shape=jax.ShapeDtypeStruct((s, ff), BF16), compiler_params=_params(),
    )(up_pre, up_pre, conv_w, conv_w, conv_b, conv_b)


def _conv_bwd(up_pre, dact, conv_w, conv_b, name, comm=None):
    s, ff2 = up_pre.shape
    ff = ff2 // 2
    tc = _div_tile(ff, 256)
    nb = ff // tc

    def half(pre, dx, w_ref, dpre_ref, gw_ref, gb_ref):
        gb_ref[...] = _colsum(dx)
        gw_ref[0:1, :] = _colsum(dx * _shift_down(pre, 2))
        gw_ref[1:2, :] = _colsum(dx * _shift_down(pre, 1))
        gw_ref[2:3, :] = _colsum(dx * pre)
        dpre_ref[...] = (w_ref[2:3, :] * dx + w_ref[1:2, :] * _shift_up(dx, 1)
                         + w_ref[0:1, :] * _shift_up(dx, 2)).astype(BF16)

    def body(pg_ref, pv_ref, da_ref, wg_ref, wv_ref, bg_ref, bv_ref, dup_ref, gwg_ref, gwv_ref, gbg_ref, gbv_ref):
        pre_g, pre_v = pg_ref[...].astype(F32), pv_ref[...].astype(F32)
        gate = _conv(pre_g, wg_ref, bg_ref)
        val = _conv(pre_v, wv_ref, bv_ref)
        da = da_ref[...].astype(F32)
        sg = _sigmoid(gate)
        half(pre_v, da * gate * sg, wv_ref, dup_ref.at[1], gwv_ref, gbv_ref)
        half(pre_g, da * val * sg * (1.0 + gate * (1.0 - sg)), wg_ref, dup_ref.at[0], gwg_ref, gbg_ref)

    def col(rows, off):
        return pl.BlockSpec((rows, tc), lambda j: (0, j + off))

    return _call(
        body, (up_pre, up_pre, dact, conv_w, conv_w, conv_b, conv_b), comm, name=name, grid=(nb,),
        in_specs=[col(s, 0), col(s, nb), col(s, 0), col(CONV_TAPS, 0), col(CONV_TAPS, nb), col(1, 0), col(1, nb)],
        out_specs=[pl.BlockSpec((2, s, tc), lambda j: (0, 0, j)), col(CONV_TAPS, 0), col(CONV_TAPS, 0),
                   col(1, 0), col(1, 0)],
        out_shape=[jax.ShapeDtypeStruct((2, s, ff), BF16)] + [jax.ShapeDtypeStruct((CONV_TAPS, ff), F32)] * 2
        + [jax.ShapeDtypeStruct((1, ff), F32)] * 2)


def _ada_fwd(c_all, w, b, name):
    nseq, d = c_all.shape
    na = w.shape[1]
    tn = _div_tile(na, 512)

    def body(c_ref, w_ref, b_ref, o_ref):
        cv = c_ref[...]
        sc = cv * _sigmoid(cv)
        o_ref[...] = jnp.dot(sc, w_ref[...], preferred_element_type=F32, precision=lax.Precision.HIGHEST) + b_ref[...]

    return pl.pallas_call(
        body, name=name, grid=(na // tn,),
        in_specs=[pl.BlockSpec((nseq, d), lambda j: (0, 0)), pl.BlockSpec((d, tn), lambda j: (0, j)),
                  pl.BlockSpec((1, tn), lambda j: (0, j))],
        out_specs=pl.BlockSpec((nseq, tn), lambda j: (0, j)),
        out_shape=jax.ShapeDtypeStruct((nseq, na), F32), compiler_params=_params(),
    )(c_all, w, b)


def _ada_bwd(c_all_t, dmod, name):
    d, nseq = c_all_t.shape
    na = dmod.shape[1]
    tm, tn = _div_tile(d, 256, SUBLANES), _div_tile(na, 512)

    def body(c_ref, dm_ref, o_ref):
        cv = c_ref[...]
        sc = cv * _sigmoid(cv)
        acc = sc[:, 0:1] * dm_ref[0:1, :]
        for bi in range(1, nseq):
            acc = acc + sc[:, bi:bi + 1] * dm_ref[bi:bi + 1, :]
        o_ref[...] = acc

    return pl.pallas_call(
        body, name=name, grid=(d // tm, na // tn),
        in_specs=[pl.BlockSpec((tm, nseq), lambda i, j: (i, 0)), pl.BlockSpec((nseq, tn), lambda i, j: (0, j))],
        out_specs=pl.BlockSpec((tm, tn), lambda i, j: (i, j)),
        out_shape=jax.ShapeDtypeStruct((d, na), F32), compiler_params=_params(),
    )(c_all_t, dmod)


def _adamw(w, g, m, v, name, comm=None, after=None):
    rows, cols = w.shape
    tb = _div_tile(rows, max(SUBLANES, (256 * 1024) // cols // SUBLANES * SUBLANES), SUBLANES)
    c1 = 1.0 / (1.0 - ADAM_B1 ** ADAM_STEP)
    c2 = 1.0 / (1.0 - ADAM_B2 ** ADAM_STEP)

    def body(*refs):
        w_ref, g_ref, m_ref, v_ref = refs[:4]
        d_ref, nm_ref, nv_ref = refs[-3:]
        gv = g_ref[...]
        nm = ADAM_B1 * m_ref[...] + (1.0 - ADAM_B1) * gv
        nv = ADAM_B2 * v_ref[...] + (1.0 - ADAM_B2) * (gv * gv)
        nm_ref[...] = nm
        nv_ref[...] = nv
        d_ref[...] = -ADAM_LR * ((nm * c1) / (jnp.sqrt(nv * c2) + ADAM_EPS) + ADAM_WD * w_ref[...])

    blk = pl.BlockSpec((tb, cols), lambda i: (i, 0))
    operands, in_specs = (w, g, m, v), [blk] * 4
    if after is not None:
        operands, in_specs = operands + (after,), in_specs + [pl.BlockSpec(after.shape, lambda i: (0, 0))]
    return _call(body, operands, comm, name=name, grid=(rows // tb,), in_specs=in_specs, out_specs=[blk] * 3,
                 out_shape=[jax.ShapeDtypeStruct((rows, cols), F32)] * 3)


def _sum_leading(parts, name, after=()):
    n, rows, cols = parts.shape
    tb = _div_tile(rows, 512, SUBLANES)

    def body(p_ref, *rest):
        o_ref = rest[-1]
        acc = p_ref[0]
        for k in range(1, n):
            acc = acc + p_ref[k]
        o_ref[...] = acc

    return pl.pallas_call(
        body, name=name, grid=(rows // tb,),
        in_specs=[pl.BlockSpec((n, tb, cols), lambda i: (0, i, 0))] + [pl.BlockSpec(memory_space=pl.ANY)] * len(after),
        out_specs=pl.BlockSpec((tb, cols), lambda i: (i, 0)),
        out_shape=jax.ShapeDtypeStruct((rows, cols), F32), compiler_params=_params(),
    )(parts, *after)


def _place():
    x, y, c = lax.axis_index("x"), lax.axis_index("y"), lax.axis_index("c")
    return x, y, c, [(1 - x, y), (x, 1 - y), (1 - x, 1 - y)]


def _all_gather(block, name):
    m_per, n = block.shape

    def body(x_ref, out_ref, send_sems, recv_sems, local_sem):
        x, y, c, chips = _place()
        me, sibling = (x, y, c), (x, y, 1 - c)

        def rows(px, py, pc):
            return out_ref.at[pl.ds((4 * px + 2 * py + pc) * m_per, m_per), :]

        def copy(k, blk, to, src=None):
            return pltpu.make_async_remote_copy(
                src_ref=rows(*blk) if src is None else src, dst_ref=rows(*blk), send_sem=send_sems.at[k],
                recv_sem=recv_sems.at[k], device_id=to, device_id_type=MESH)

        mine = pltpu.make_async_copy(x_ref, rows(*me), local_sem)
        mine.start()
        first = [copy(0, me, sibling, src=x_ref)]
        first += [copy(1 + j, me, (*chip, c), src=x_ref) for j, chip in enumerate(chips)]
        for cp in first:
            cp.start()
        passed = [copy(4 + j, (*chip, c), sibling) for j, chip in enumerate(chips)]
        for j, chip in enumerate(chips):
            copy(1 + j, (*chip, c), me).wait_recv()
            passed[j].start()
        copy(0, sibling, me).wait_recv()
        for j, chip in enumerate(chips):
            copy(4 + j, (*chip, 1 - c), me).wait_recv()
        for cp in first + passed:
            cp.wait_send()
        mine.wait()

    return pl.pallas_call(
        body, name=name, out_shape=jax.ShapeDtypeStruct((N_DEV * m_per, n), block.dtype),
        in_specs=[pl.BlockSpec(memory_space=pltpu.VMEM)], out_specs=pl.BlockSpec(memory_space=pltpu.VMEM),
        scratch_shapes=[pltpu.SemaphoreType.DMA((7,)), pltpu.SemaphoreType.DMA((7,)), pltpu.SemaphoreType.DMA],
        compiler_params=_params(),
    )(block)


def _hbm_specs(n):
    return [pl.BlockSpec(memory_space=HBM)] * n


def _part(ref, by_cols, half, quarter=None, lead=None):
    extent = ref.shape[-1] if by_cols else ref.shape[-2]
    size = extent // 2 if quarter is None else extent // 4
    first = half * (extent // 2) + (0 if quarter is None else quarter * size)
    tile = LANES if by_cols else 2 * SUBLANES
    span = pl.ds(pl.multiple_of(first, tile) if size % tile == 0 else first, size)
    index = (slice(None), span) if by_cols else (span, slice(None))
    return ref.at[index] if lead is None else ref.at[(lead,) + index]


def _half_rows(ref, half, lead=None):
    return _part(ref, False, half, lead=lead)


class _Comm:
    def __init__(self, operands, out_shape, sem_dims, build, aliases=None):
        self.operands, self.out_shape, self.sem_dims = list(operands), list(out_shape), list(sem_dims)
        self.scratch = [pltpu.SemaphoreType.DMA(d) for d in sem_dims]
        self.build, self.aliases = build, dict(aliases or {})


class _SemGrid:
    def __init__(self, sems, dims):
        self.sems, self.dims, self.at = list(sems), tuple(dims), self

    def __getitem__(self, index):
        index = index if isinstance(index, tuple) else (index,)
        flat = 0
        for i, d in zip(index, self.dims):
            flat = flat * d + i
        return self.sems[flat]


def _call(body, operands, comm=None, *, name, grid, in_specs, out_specs, out_shape, scratch_shapes=(),
          input_output_aliases=None):
    aliases = dict(input_output_aliases or {})
    if comm is None:
        return pl.pallas_call(
            body, name=name, grid=grid, in_specs=in_specs, out_specs=out_specs, out_shape=out_shape,
            scratch_shapes=list(scratch_shapes), input_output_aliases=aliases, compiler_params=_params())(*operands)
    single = not isinstance(out_shape, (list, tuple))
    outs = [out_shape] if single else list(out_shape)
    ospecs = [out_specs] if single else list(out_specs)
    n_in, n_out, n_scr = len(operands), len(outs), len(scratch_shapes)
    c_in, c_out = len(comm.operands), len(comm.out_shape)
    for i, o in comm.aliases.items():
        aliases[n_in + i] = n_out + o

    def hosted(*refs):
        ins, c_ins = refs[:n_in], refs[n_in:n_in + c_in]
        o0 = n_in + c_in
        o_refs, c_outs = refs[o0:o0 + n_out], refs[o0 + n_out:o0 + n_out + c_out]
        s0 = o0 + n_out + c_out
        scr, sems = refs[s0:s0 + n_scr], refs[s0 + n_scr:]
        stages = comm.build(c_ins, c_outs, sems)
        step, n_steps = 0, 1
        for dim, size in enumerate(grid):
            step, n_steps = step * size + pl.program_id(dim), n_steps * size
        pl.when(step == 0)(stages[0])
        body(*ins, *o_refs, *scr)
        for stage in stages[1:-1]:
            pl.when(step == (n_steps * MIDDLE_STAGE_AT) // 100)(stage)
        pl.when(step == n_steps - 1)(stages[-1])

    res = pl.pallas_call(
        hosted, name=name, grid=grid, in_specs=list(in_specs) + _hbm_specs(c_in),
        out_specs=ospecs + _hbm_specs(c_out), out_shape=outs + comm.out_shape,
        scratch_shapes=list(scratch_shapes) + comm.scratch, input_output_aliases=aliases,
        compiler_params=_params())(*operands, *comm.operands)
    return (res[0] if single else res[:n_out]), res[n_out:]


def _run_comm(comm, name):
    c_in, c_out = len(comm.operands), len(comm.out_shape)

    def body(*refs):
        for stage in comm.build(refs[:c_in], refs[c_in:c_in + c_out], refs[c_in + c_out:]):
            stage()

    return pl.pallas_call(
        body, name=name, in_specs=_hbm_specs(c_in), out_specs=_hbm_specs(c_out), out_shape=comm.out_shape,
        scratch_shapes=comm.scratch, input_output_aliases=comm.aliases, compiler_params=_params())(*comm.operands)


def _join_comms(comms):
    def build(in_refs, out_refs, sems):
        staged, i, o, k = [], 0, 0, 0
        for cm in comms:
            ni, no, ns = len(cm.operands), len(cm.out_shape), len(cm.sem_dims)
            staged.append(cm.build(in_refs[i:i + ni], out_refs[o:o + no], sems[k:k + ns]))
            i, o, k = i + ni, o + no, k + ns
        def run(fns):
            def stage():
                for fn in fns:
                    fn()
            return stage

        return (run([st[0] for st in staged]), run([fn for st in staged for fn in st[1:-1]]),
                run([st[-1] for st in staged]))

    aliases, i, o = {}, 0, 0
    for cm in comms:
        aliases.update({i + a: o + b for a, b in cm.aliases.items()})
        i, o = i + len(cm.operands), o + len(cm.out_shape)
    return _Comm(sum((cm.operands for cm in comms), []), sum((cm.out_shape for cm in comms), []),
                 sum((cm.sem_dims for cm in comms), []), build, aliases)


def _gather8_comm(block):
    def build(in_refs, out_refs, sems):
        (src,), (out,), (send_sems, recv_sems) = in_refs, out_refs, sems
        x, y, c, chips = _place()
        me, sibling = (x, y, c), (x, y, 1 - c)

        def copy(k, blk, to, own=False):
            dst = out.at[4 * blk[0] + 2 * blk[1] + blk[2]]
            return pltpu.make_async_remote_copy(
                src_ref=src if own else dst, dst_ref=dst, send_sem=send_sems.at[k], recv_sem=recv_sems.at[k],
                device_id=to, device_id_type=MESH)

        first = [copy(0, me, sibling, own=True)] + [copy(1 + j, me, (*chip, c), own=True)
                                                     for j, chip in enumerate(chips)]
        passed = [copy(4 + j, (*chip, c), sibling) for j, chip in enumerate(chips)]

        def start():
            for cp in first:
                cp.start()

        def middle():
            for j, chip in enumerate(chips):
                copy(1 + j, (*chip, c), me).wait_recv()
                passed[j].start()

        def finish():
            copy(0, sibling, me).wait_recv()
            for j, chip in enumerate(chips):
                copy(4 + j, (*chip, 1 - c), me).wait_recv()
            for cp in first + passed:
                cp.wait_send()

        return start, middle, finish

    return _Comm([block], [jax.ShapeDtypeStruct((N_DEV,) + block.shape, block.dtype)], [(7,), (7,)], build)


def _gather_comm(shards, by_cols=()):
    nw = len(shards)

    def build(in_refs, out_refs, sems):
        send_sems, recv_sems = sems
        x, y, c, chips = _place()
        me, sibling = (x, y, c), (x, y, 1 - c)
        across_x, across_y, diagonal = chips

        def copy(w, k, block, part, to, src=None):
            dst = _part(out_refs[w], w in by_cols, part[1], part[2] if part[0] else None, 2 * block[0] + block[1])
            return pltpu.make_async_remote_copy(
                src_ref=dst if src is None else src, dst_ref=dst, send_sem=send_sems.at[w, k],
                recv_sem=recv_sems.at[w, k], device_id=to, device_id_type=MESH)

        first = [copy(w, j, (x, y), (0, c), (*chip, c), src=_part(in_refs[w], w in by_cols, c))
                 for w in range(nw) for j, chip in enumerate((across_x, across_y))]
        passed = [[copy(w, 2, across_x, (1, c, 0), (*across_y, c)), copy(w, 3, across_y, (1, c, 1), (*across_x, c)),
                   copy(w, 4, across_x, (0, c), sibling), copy(w, 5, across_y, (0, c), sibling)] for w in range(nw)]
        last = [[copy(w, 6, diagonal, (1, c, 0), sibling), copy(w, 7, diagonal, (1, c, 1), sibling)]
                for w in range(nw)]

        def start():
            for cp in first:
                cp.start()

        def middle():
            for w in range(nw):
                copy(w, 0, across_x, (0, c), me).wait_recv()
                copy(w, 1, across_y, (0, c), me).wait_recv()
                for cp in passed[w]:
                    cp.start()

        def finish():
            for w in range(nw):
                copy(w, 2, diagonal, (1, c, 0), me).wait_recv()
                copy(w, 3, diagonal, (1, c, 1), me).wait_recv()
                for cp in last[w]:
                    cp.start()
            for w in range(nw):
                for k, block, part in ((4, across_x, (0, 1 - c)), (5, across_y, (0, 1 - c)),
                                       (6, diagonal, (1, 1 - c, 0)), (7, diagonal, (1, 1 - c, 1))):
                    copy(w, k, block, part, me).wait_recv()
            for cp in first + sum(passed, []) + sum(last, []):
                cp.wait_send()

        return start, middle, finish

    return _Comm(shards, [jax.ShapeDtypeStruct((N_CHIPS,) + w.shape, w.dtype) for w in shards],
                 [(nw, 8), (nw, 8)], build)


def _halved(shape, by_cols):
    return shape[:-1] + (shape[-1] // 2,) if by_cols else shape[:-2] + (shape[-2] // 2, shape[-1])


def _swap_comm(gs, by_cols=()):
    nw = len(gs)

    def build(in_refs, out_refs, sems):
        send_sems, recv_sems = sems
        x, y, c, _ = _place()
        cps = []
        for w in range(nw):
            cps.append(pltpu.make_async_remote_copy(
                src_ref=_part(in_refs[w], w in by_cols, 1 - c, lead=slice(None)), dst_ref=out_refs[w],
                send_sem=send_sems.at[w], recv_sem=recv_sems.at[w], device_id=(x, y, 1 - c), device_id_type=MESH))

        def start():
            for cp in cps:
                cp.start()

        def finish():
            for cp in cps:
                cp.wait()

        return start, finish

    return _Comm(gs, [jax.ShapeDtypeStruct(_halved(g.shape, w in by_cols), g.dtype) for w, g in enumerate(gs)],
                 [(nw,), (nw,)], build)


def _exchange_comm(s1s):
    nw = len(s1s)

    def build(in_refs, out_refs, sems):
        send_sems, recv_sems = sems
        x, y, c, chips = _place()
        cps = [pltpu.make_async_remote_copy(
            src_ref=in_refs[w].at[2 * chip[0] + chip[1]], dst_ref=out_refs[w].at[j], send_sem=send_sems.at[w, j],
            recv_sem=recv_sems.at[w, j], device_id=(*chip, c), device_id_type=MESH)
            for w in range(nw) for j, chip in enumerate(chips)]

        def start():
            for cp in cps:
                cp.start()

        def finish():
            for cp in cps:
                cp.wait()

        return start, finish

    return _Comm(s1s, [jax.ShapeDtypeStruct((N_CHIPS - 1,) + s.shape[1:], s.dtype) for s in s1s],
                 [(nw, 3), (nw, 3)], build)


def _size(dims):
    n = 1
    for d in dims:
        n *= d
    return n


def _sem_grids(comm, sem_refs):
    grids, pos = [], 0
    for dims in comm.sem_dims:
        grids.append(_SemGrid(sem_refs[pos:pos + _size(dims)], dims))
        pos += _size(dims)
    return grids


def _comm_split_start(comm, name, after=()):
    c_in, c_out = len(comm.operands), len(comm.out_shape)
    counts = [_size(d) for d in comm.sem_dims]
    n_sem = sum(counts)
    assert not comm.aliases

    def body(*refs):
        srcs, lands = refs[:c_in], refs[c_in:c_in + c_out]
        first_sem = c_in + c_out + len(after)
        start, _ = comm.build(srcs, lands, _sem_grids(comm, refs[first_sem:first_sem + n_sem]))
        start()
        refs[-1][...] = jnp.zeros(refs[-1].shape, refs[-1].dtype)

    lands = [pltpu.with_memory_space_constraint(lax.empty(o.shape, o.dtype), HBM) for o in comm.out_shape]
    srcs = [pltpu.with_memory_space_constraint(a, HBM) for a in comm.operands]
    res = pl.pallas_call(
        body, name=name, in_specs=_hbm_specs(c_in + c_out) + [pl.BlockSpec(memory_space=pl.ANY)] * len(after),
        out_specs=[pl.BlockSpec(memory_space=pltpu.SEMAPHORE)] * n_sem + _hbm_specs(c_in + c_out)
        + [pl.BlockSpec(memory_space=pltpu.VMEM)],
        out_shape=[pltpu.SemaphoreType.DMA(())] * n_sem + [pltpu.HBM(a.shape, a.dtype) for a in comm.operands]
        + [pltpu.HBM(o.shape, o.dtype) for o in comm.out_shape] + [jax.ShapeDtypeStruct((SUBLANES, LANES), F32)],
        input_output_aliases={i: n_sem + i for i in range(c_in + c_out)},
        compiler_params=_params(has_side_effects=pltpu.SideEffectType.DATAFLOW_SIDE_EFFECTING))(*srcs, *lands, *after)
    return res[:-1], res[-1]


def _comm_split_wait(comm, state, after, name):
    c_in, c_out, n_sem = len(comm.operands), len(comm.out_shape), sum(_size(d) for d in comm.sem_dims)
    sems, srcs, lands = state[:n_sem], state[n_sem:n_sem + c_in], state[n_sem + c_in:]

    def body(*refs):
        src_refs, land_refs = refs[:c_in], refs[c_in:c_in + c_out]
        _, finish = comm.build(src_refs, land_refs, _sem_grids(comm, refs[c_in + c_out:c_in + c_out + n_sem]))
        finish()

    sem_spec = pl.BlockSpec(memory_space=pltpu.SEMAPHORE)
    res = pl.pallas_call(
        body, name=name, in_specs=_hbm_specs(c_in + c_out) + [sem_spec] * n_sem + [pl.BlockSpec(memory_space=pl.ANY)],
        out_specs=_hbm_specs(c_in + c_out),
        out_shape=[pltpu.HBM(a.shape, a.dtype) for a in srcs] + [pltpu.HBM(o.shape, o.dtype) for o in lands],
        input_output_aliases={i: i for i in range(c_in + c_out)},
        compiler_params=_params(has_side_effects=pltpu.SideEffectType.DATAFLOW_SIDE_EFFECTING),
    )(*srcs, *lands, *sems, after)
    return res[:c_in], res[c_in:]


def _share_comm(fs, by_cols=()):
    nw = len(fs)

    def build(in_refs, out_refs, sems):
        del in_refs
        send_sems, recv_sems = sems
        x, y, c, _ = _place()

        def copy(w, half):
            part = _part(out_refs[w], w in by_cols, half)
            return pltpu.make_async_remote_copy(
                src_ref=part, dst_ref=part, send_sem=send_sems.at[w], recv_sem=recv_sems.at[w],
                device_id=(x, y, 1 - c), device_id_type=MESH)

        sends = [copy(w, c) for w in range(nw)]

        def start():
            for cp in sends:
                cp.start()

        def finish():
            for w in range(nw):
                copy(w, 1 - c).wait_recv()
            for cp in sends:
                cp.wait_send()

        return start, finish

    return _Comm(fs, [jax.ShapeDtypeStruct(f.shape, f.dtype) for f in fs],
                 [(nw,), (nw,)], build,
                 aliases={w: w for w in range(nw)})


def _add_sibling(g, r1, place, name, by_cols=False):
    nch, h, cols = r1.shape
    tr = _div_tile(h, 1024 if by_cols else 512, 2 * SUBLANES)
    nb = h // tr
    mine = (lambda k, i, p: (k, i, p[0])) if by_cols else (lambda k, i, p: (k, p[0] * nb + i, 0))

    def body(place_ref, g_ref, r_ref, o_ref):
        del place_ref
        o_ref[...] = (g_ref[...].astype(F32) + r_ref[...].astype(F32)).astype(BF16)

    spec = pltpu.PrefetchScalarGridSpec(
        num_scalar_prefetch=1, grid=(nch, nb),
        in_specs=[pl.BlockSpec((None, tr, cols), mine), pl.BlockSpec((None, tr, cols), lambda k, i, p: (k, i, 0))],
        out_specs=pl.BlockSpec((None, tr, cols), lambda k, i, p: (k, i, 0)))
    return pl.pallas_call(body, name=name, grid_spec=spec, out_shape=jax.ShapeDtypeStruct((nch, h, cols), BF16),
                          compiler_params=_params())(place, g, r1)


def _add_chips(s1, r2, place, name, by_cols=False):
    _, h, cols = s1.shape
    tr = _div_tile(h, 1024 if by_cols else 512, 2 * SUBLANES)
    nb = h // tr
    mine = (lambda i, p: (i, p[0])) if by_cols else (lambda i, p: (p[0] * nb + i, 0))
    whole = (h, 2 * cols) if by_cols else (2 * h, cols)

    def body(place_ref, s_ref, r_ref, o_ref):
        del place_ref
        acc = s_ref[...].astype(F32)
        for j in range(N_CHIPS - 1):
            acc = acc + r_ref[j].astype(F32)
        o_ref[...] = acc

    spec = pltpu.PrefetchScalarGridSpec(
        num_scalar_prefetch=1, grid=(nb,),
        in_specs=[pl.BlockSpec((None, tr, cols), lambda i, p: (p[1], i, 0)),
                  pl.BlockSpec((N_CHIPS - 1, tr, cols), lambda i, p: (0, i, 0))],
        out_specs=pl.BlockSpec((tr, cols), mine))
    return pl.pallas_call(body, name=name, grid_spec=spec, out_shape=jax.ShapeDtypeStruct(whole, F32),
                          compiler_params=_params())(place, s1, r2)


def _quarter_turn(m):
    h = m.shape[-1] // 2
    return jnp.concatenate([-m[..., h:], m[..., :h]], axis=-1)


def _quarter_turn_back(m):
    h = m.shape[-1] // 2
    return jnp.concatenate([m[..., h:], -m[..., :h]], axis=-1)


def _stack_rows(parts):
    out = lax.empty((sum(p.shape[0] for p in parts),) + parts[0].shape[1:], parts[0].dtype)
    row = 0
    for p in parts:
        out = lax.dynamic_update_slice(out, p, (row, 0))
        row += p.shape[0]
    return out


def _join_cols(sh):
    return jnp.concatenate([sh[k] for k in range(N_CHIPS)], axis=1)


def _split_cols(full):
    c = full.shape[1] // N_CHIPS
    return jnp.stack([full[:, k * c:(k + 1) * c] for k in range(N_CHIPS)])


def kernel(x, c, positions, w_ada, b_ada, pre_norm1_g, w_in, gm_ln_g, gm_ln_b, gm_w_s, gm_b_s, w_branch_a, q_norm_g, w_uq, kv_norm_g, w_ukv, w_branch_b, w_out, post_norm1_g, pre_norm2_g, w_up, conv_w, conv_b, w_down, post_norm2_g, loss_target, m_w_ada, m_b_ada, m_pre_norm1_g, m_w_in, m_gm_ln_g, m_gm_ln_b, m_gm_w_s, m_gm_b_s, m_w_branch_a, m_q_norm_g, m_w_uq, m_kv_norm_g, m_w_ukv, m_w_branch_b, m_w_out, m_post_norm1_g, m_pre_norm2_g, m_w_up, m_conv_w, m_conv_b, m_w_down, m_post_norm2_g, v_w_ada, v_b_ada, v_pre_norm1_g, v_w_in, v_gm_ln_g, v_gm_ln_b, v_gm_w_s, v_gm_b_s, v_w_branch_a, v_q_norm_g, v_w_uq, v_kv_norm_g, v_w_ukv, v_w_branch_b, v_w_out, v_post_norm1_g, v_pre_norm2_g, v_w_up, v_conv_w, v_conv_b, v_w_down, v_post_norm2_g):
    given = dict(locals())
    s, d = x.shape[1], x.shape[2]
    gw = gm_ln_g.shape[0]
    ql, kvl = q_norm_g.shape[0], kv_norm_g.shape[0]
    heads = N_CHIPS * w_uq.shape[1] // (NOPE + ROPE)
    ff = N_CHIPS * w_down.shape[0]
    assert gw == d and N_CHIPS * w_ukv.shape[1] == heads * (NOPE + VHEAD)
    ix, iy, ic = lax.axis_index("x"), lax.axis_index("y"), lax.axis_index("c")
    chip = 2 * ix + iy
    dev = 2 * chip + ic
    row = lambda v: v.reshape(1, -1)

    c_all = _all_gather(jnp.pad(c, ((0, SUBLANES - 1), (0, 0))), "gather_c").reshape(N_DEV, SUBLANES, d)[:, 0]
    na = w_ada.shape[1]
    b_ada_mine = lax.dynamic_slice(b_ada, (chip * na,), (na,))
    mod_cols = _ada_fwd(c_all, w_ada, row(b_ada_mine), "ada_fwd")
    mod_all = _all_gather(mod_cols, "gather_mod").reshape(N_CHIPS, N_CORES, N_DEV, na)[:, 0]
    mod = lax.dynamic_index_in_dim(mod_all, dev, axis=1, keepdims=False).reshape(N_MOD, d)
    shift1, scale1, gate1, shift2, scale2, gate2 = (mod[i:i + 1] for i in range(N_MOD))

    mine = {n: (given[n].T if n == "w_in" else given[n]).astype(BF16) for n in BIG}
    gather = lambda names: _gather_comm([mine[n] for n in names], [i for i, n in enumerate(names) if n == "w_in"])
    whole = lambda n, g: lax.dynamic_update_slice(g, mine[n][None], (chip, 0, 0))
    rows4 = lambda sh4: sh4.reshape(-1, sh4.shape[2])
    wi_t = rows4(whole("w_in", _run_comm(gather(["w_in"]), "gather_w_in")[0]))
    o_q, o_kv, o_pe, o_ga = 2 * gw, 2 * gw + ql, 2 * gw + ql + kvl, 2 * gw + ql + kvl + ROPE
    w_in_big_t = _stack_rows([wi_t[:o_q], wi_t[o_ga:]])
    w_in_lat_t = _stack_rows([wi_t[o_q:o_ga], _quarter_turn(wi_t[o_pe:o_ga].T).T])

    inv = ROPE_THETA ** (-jnp.arange(0, ROPE, 2, dtype=F32) / ROPE)
    ang = positions[0].astype(F32)[:, None] * inv
    cos, sin = jnp.cos(ang), jnp.sin(ang)
    rope_k = jnp.concatenate([cos, cos, sin, sin], axis=1)
    softmax_scale = float(NOPE + ROPE) ** -0.5
    rope_q = jnp.concatenate([jnp.ones((s, NOPE), F32), rope_k], axis=1) * softmax_scale

    x2d, tgt = x[0], loss_target[0]
    g_pre1, g_post1, g_pre2, g_post2 = row(pre_norm1_g), row(post_norm1_g), row(pre_norm2_g), row(post_norm2_g)
    ln_g, ln_b, q_g, kv_g = row(gm_ln_g), row(gm_ln_b), row(q_norm_g), row(kv_norm_g)
    b_s_t = gm_b_s.T
    conv_wf = _all_gather(jnp.pad(conv_w, ((0, SUBLANES - CONV_TAPS), (0, 0))), "gather_conv_w")
    conv_wf = conv_wf.reshape(N_CHIPS, N_CORES, SUBLANES, conv_w.shape[1])[:, 0, :CONV_TAPS]
    conv_wf = conv_wf.transpose(1, 0, 2).reshape(CONV_TAPS, 2 * ff)
    conv_bf = row(conv_b)

    h1 = _prenorm(x2d, g_pre1, scale1, shift1, "prenorm1")
    z_big, (g_uq, g_ukv, g_a) = _matmul(h1, w_in_big_t, mode="nt", out_dtype=BF16, name="mm_z_big", tm=s,
                                        comm=gather(["w_uq", "w_ukv", "w_branch_a"]))
    wq = _join_cols(whole("w_uq", g_uq)).reshape(ql, heads, NOPE + ROPE)
    w_q = jnp.concatenate([wq, _quarter_turn(wq[:, :, NOPE:])], axis=2).reshape(ql, heads * HEAD_W)
    w_kv = _join_cols(whole("w_ukv", g_ukv)).reshape(kvl, heads, 2, NOPE).transpose(0, 2, 1, 3)
    w_kv = w_kv.reshape(kvl, 2 * heads * NOPE)
    w_a = rows4(whole("w_branch_a", g_a))
    z_lat = _matmul(h1, w_in_lat_t, mode="nt", out_dtype=F32, name="mm_z_lat", tm=s, tn=1024)
    a_act = _gmlp_fwd(z_big, ln_g, ln_b, gm_w_s, b_s_t, "gmlp_fwd")
    qn, kvn, kr = _mla_prep(z_lat, q_g, kv_g, rope_k, "mla_prep")
    q_rot = _matmul(qn, w_q, mode="nn", out_dtype=BF16, name="mm_q", tm=s, tn=HEAD_W, mul=rope_q)
    kv_all = _matmul(kvn, w_kv, mode="nn", out_dtype=BF16, name="mm_kv", tm=s, tn=1024)
    (o_att, lse), (g_b, g_o, g_up) = _attn_fwd(q_rot, kv_all, kr, heads, "attn_fwd",
                                               comm=gather(["w_branch_b", "w_out", "w_up"]))
    w_b, w_o, w_upf = rows4(whole("w_branch_b", g_b)), rows4(whole("w_out", g_o)), whole("w_up", g_up)
    y_a = _matmul(a_act, w_a, mode="nn", out_dtype=BF16, name="mm_y_a", tm=s)
    y_b = _matmul(o_att, w_b, mode="nn", out_dtype=BF16, name="mm_y_b", tm=s)
    merged = _merge(z_big, y_a, y_b, "merge")
    y1 = _matmul(merged, w_o, mode="nn", out_dtype=F32, name="mm_y1", tm=s)
    x1, h2 = _post_pre(x2d, y1, gate1, g_post1, g_pre2, scale2, shift2, "post1_pre2")

    up_pre, (g_dn,) = _matmul(h2, w_upf, mode="nn", out_dtype=BF16, name="mm_up", tm=s, tn=1408,
                              comm=gather(["w_down"]))
    w_dn = rows4(whole("w_down", g_dn))
    act = _conv_fwd(up_pre, conv_wf, conv_bf, "conv_fwd")
    ffn = _matmul(act, w_dn, mode="nn", out_dtype=F32, name="mm_ffn", tm=s, tn=1024, tk=1408)

    dffn, dgate2, g_post2_grad, dx2, loss_part = _post_bwd(ffn, gate2, g_post2, "post2_bwd", xin=x1, target=tgt)
    loss = lax.psum(loss_part[0, 0], ("x", "y", "c"))
    place = jnp.stack([ic, chip]).astype(jnp.int32)
    rows_of = lambda g: g.reshape(N_CHIPS, g.shape[0] // N_CHIPS, g.shape[1])
    add_sibling = lambda names, gs, r1s: [_add_sibling(g, r1, place, "rs_add_sibling_" + n, by_cols=n == "w_in")
                                          for n, g, r1 in zip(names, gs, r1s)]
    add_chips = lambda names, s1s, r2s: [_add_chips(s1, r2, place, "rs_add_chips_" + n, by_cols=n == "w_in")
                                         for n, s1, r2 in zip(names, s1s, r2s)]
    gp_down = [rows_of(_matmul(act, dffn, mode="tn", out_dtype=BF16, name="mm_gw_down", tn=2048, tk=s))]
    dact, r1_down = _matmul(dffn, w_dn, mode="nt", out_dtype=BF16, name="mm_dact", tm=s, comm=_swap_comm(gp_down))
    s1_down = add_sibling(["w_down"], gp_down, r1_down)
    (dup, gcw_g, gcw_v, gcb_g, gcb_v), r2_down = _conv_bwd(up_pre, dact, conv_wf, conv_bf, "conv_bwd",
                                                            comm=_exchange_comm(s1_down))
    half_down = add_chips(["w_down"], s1_down, r2_down)
    dh2 = _matmul(dup, w_upf, mode="nt", out_dtype=F32, name="mm_dh2", tm=s, tn=1024, tk=1408)
    gw_up = _matmul(h2, dup, mode="tn", out_dtype=BF16, name="mm_gw_up", tm=1024, tn=1408, tk=s, out_groups=N_CHIPS)
    dx1, dshift2, dscale2, g_pre2_grad = _prenorm_bwd(x1, dh2, dx2, g_pre2, scale2, "prenorm2_bwd")

    dy1, dgate1, g_post1_grad = _post_bwd(y1, gate1, g_post1, "post1_bwd", dxo=dx1)
    dmerged = _matmul(dy1, w_o, mode="nt", out_dtype=BF16, name="mm_dmerged", tm=s)
    gw_out = _matmul(merged, dy1, mode="tn", out_dtype=BF16, name="mm_gw_out", tn=1024, tk=s)
    dy_a, dy_b, dz_big = _merge_bwd(dmerged, z_big, y_a, y_b, "merge_bwd")
    da = _matmul(dy_a, w_a, mode="nt", out_dtype=BF16, name="mm_da", tm=s)
    gw_a = _matmul(a_act, dy_a, mode="tn", out_dtype=BF16, name="mm_gw_a", tn=1024, tk=s)
    do = _matmul(dy_b, w_b, mode="nt", out_dtype=BF16, name="mm_do", tm=s)
    gw_b = _matmul(o_att, dy_b, mode="tn", out_dtype=BF16, name="mm_gw_b", tn=1024, tk=s)
    mid = ["w_up", "w_out", "w_branch_a", "w_branch_b"]
    gp_mid = [gw_up, rows_of(gw_out), rows_of(gw_a), rows_of(gw_b)]
    (dz_big, g_ws, g_bs_t, g_ln_g, g_ln_b), r1_mid = _gmlp_bwd(z_big, da, dz_big, ln_g, ln_b, gm_w_s, b_s_t,
                                                                "gmlp_bwd", comm=_swap_comm(gp_mid))
    s1_mid = add_sibling(mid, gp_mid, r1_mid)
    (dq, dk, dv), r2_up_out = _attn_bwd(q_rot, kv_all, kr, o_att, do, lse, heads, "attn_bwd",
                                        comm=_exchange_comm(s1_mid[:2]))
    dq_big, dkv, dkk = _mla_bwd_mid(dq, dk, dv, rope_q, rope_k, heads, "mla_bwd_mid")
    gw_q = _matmul(qn, dq_big, mode="tn", out_dtype=F32, name="mm_gw_q", tn=1024, tk=s)
    dqn = _matmul(dq_big, w_q, mode="nt", out_dtype=F32, name="mm_dqn", tm=s, tk=1024)
    gw_kv = _matmul(kvn, dkv, mode="tn", out_dtype=BF16, name="mm_gw_kv", tn=1024, tk=s)
    dkvn = _matmul(dkv, w_kv, mode="nt", out_dtype=F32, name="mm_dkvn", tm=s, tk=1024)
    dz_lat, g_q, g_kv = _mla_bwd_post(z_lat, dqn, dkvn, dkk, q_g, kv_g, "mla_bwd_post")

    partial = {
        "gm_ln_g": g_ln_g, "gm_ln_b": g_ln_b, "gm_w_s": g_ws, "gm_b_s": g_bs_t[:, :gm_b_s.shape[0]].T,
        "q_norm_g": g_q, "kv_norm_g": g_kv, "post_norm1_g": g_post1_grad, "pre_norm2_g": g_pre2_grad,
        "conv_w": jnp.concatenate([gcw_g, gcw_v], axis=1), "conv_b": jnp.concatenate([gcb_g, gcb_v], axis=1),
        "post_norm2_g": g_post2_grad,
    }
    flat = jnp.concatenate([partial[n].reshape(-1) for n in SMALL_PARTIAL])
    n_small = flat.shape[0]
    rows_small = -(-n_small // (LANES * SMALL_ROW_TILE)) * SMALL_ROW_TILE
    flat = jnp.pad(flat, (0, rows_small * LANES - n_small)).reshape(rows_small, LANES)

    def small_pack(prefix, source):
        v = jnp.concatenate([source[prefix + n].reshape(-1) for n in SMALL])
        rows = -(-v.shape[0] // (LANES * SUBLANES)) * SUBLANES
        return jnp.pad(v, (0, rows * LANES - v.shape[0])).reshape(rows, LANES)

    small_state = [small_pack(prefix, given) for prefix in ("", "m_", "v_")]

    dh1, r2_a_b = _matmul(dz_big, w_in_big_t, mode="nn", out_dtype=F32, name="mm_dh1_big", tm=s, tk=1024,
                          comm=_exchange_comm(s1_mid[2:]))
    half_mid = add_chips(mid, s1_mid, list(r2_up_out) + list(r2_a_b))
    dh1 = _matmul(dz_lat, w_in_lat_t, mode="nn", out_dtype=F32, name="mm_dh1_lat", tm=s, tk=1024, add=dh1)
    gw_big_t, hosted = _matmul(dz_big, h1, mode="tn", out_dtype=BF16, name="mm_gw_in_big", tn=2048, tk=s,
                               comm=_join_comms([_share_comm(half_down + half_mid), _gather8_comm(flat)]))
    shared, small_all = hosted[:-1], lax.dynamic_update_slice(hosted[-1], flat[None], (dev, 0, 0))
    small_sum = _sum_leading(small_all, "sum_small", after=small_state).reshape(-1)
    small_grads, off = {}, 0
    for n in SMALL_PARTIAL:
        shape = (CONV_TAPS, 2 * ff) if n == "conv_w" else given[n].shape
        small_grads[n] = small_sum[off:off + partial[n].size].reshape(shape)
        off += partial[n].size
    small_grads["conv_w"] = lax.dynamic_slice(small_grads["conv_w"], (0, chip * conv_w.shape[1]), conv_w.shape)
    grads = dict(zip(["w_down"] + mid, shared), **small_grads)
    gw_lat_t = _matmul(dz_lat, h1, mode="tn", out_dtype=F32, name="mm_gw_in_lat", tm=1024, tn=1024, tk=s)

    gq = gw_q.reshape(ql, heads, HEAD_W)
    gq_pe = gq[:, :, NOPE:NOPE + ROPE] + _quarter_turn_back(gq[:, :, NOPE + ROPE:])
    g_pe_t = gw_lat_t[ql + kvl:ql + kvl + ROPE] + _quarter_turn_back(gw_lat_t[ql + kvl + ROPE:].T).T
    last = ["w_in", "w_uq", "w_ukv"]
    gw_in_t = _stack_rows([gw_big_t[:o_q], gw_lat_t[:ql + kvl].astype(BF16), g_pe_t.astype(BF16), gw_big_t[o_q:]])
    gp_last = [
        gw_in_t.reshape(N_CHIPS, gw_in_t.shape[0] // N_CHIPS, d),
        _split_cols(jnp.concatenate([gq[:, :, :NOPE], gq_pe], axis=2).reshape(ql, heads * (NOPE + ROPE)).astype(BF16)),
        _split_cols(gw_kv.reshape(kvl, 2, heads, NOPE).transpose(0, 2, 1, 3).reshape(kvl, heads * 2 * NOPE)),
    ]
    (grad_x, dshift1, dscale1, g_pre1_grad), r1_last = _prenorm_bwd(x2d, dh1, dx1, g_pre1, scale1, "prenorm1_bwd",
                                                                    comm=_swap_comm(gp_last, by_cols=[0]))
    s1_last = add_sibling(last, gp_last, r1_last)

    dmod = jnp.concatenate([dshift1, dscale1, dgate1, dshift2, dscale2, dgate2, g_pre1_grad], axis=1)
    dmod_all = _all_gather(jnp.pad(dmod, ((0, SUBLANES - 1), (0, 0))), "gather_dmod")
    dmod_all = dmod_all.reshape(N_DEV, SUBLANES, (N_MOD + 1) * d)[:, 0]
    dmod_sum = _sum_leading(dmod_all.reshape(N_DEV, 1, (N_MOD + 1) * d), "sum_dmod")[0]
    grads["b_ada"], grads["pre_norm1_g"] = dmod_sum[:N_MOD * d], dmod_sum[N_MOD * d:]
    dmod_mine = lax.dynamic_slice(dmod_all, (0, chip * na), (N_DEV, na))
    grads["w_ada"] = _ada_bwd(c_all.T, dmod_mine, "ada_bwd")

    delta, new_m, new_v = {}, {}, {}

    def adamw(n, after=None):
        turn = (lambda a: a.T) if n == "w_in" else (lambda a: a)
        outs = _adamw(turn(given[n]), grads[n], turn(given["m_" + n]), turn(given["v_" + n]), "adamw_" + n,
                      after=after)
        grads[n] = turn(grads[n])
        delta[n], new_m[n], new_v[n] = (turn(o) for o in outs)

    exchange_last = _exchange_comm(s1_last)
    in_flight, token = _comm_split_start(exchange_last, "rs_exchange_last_start", after=[dmod_sum, small_sum])
    for n in ["w_ada", "w_down"] + mid:
        adamw(n, after=token)
    s1_last, r2_last = _comm_split_wait(exchange_last, in_flight, delta[mid[-1]], "rs_exchange_last_wait")
    half_last = add_chips(last, s1_last, r2_last)
    grads.update(zip(last, _run_comm(_share_comm(half_last, by_cols=[0]), "rs_share_last")))
    for n in last:
        adamw(n)

    outs = _adamw(small_state[0], small_pack("", grads), small_state[1], small_state[2], "adamw_small")
    off = 0
    for n in SMALL:
        size = given[n].size
        for store, packed_out in zip((delta, new_m, new_v), outs):
            store[n] = packed_out.reshape(-1)[off:off + size].reshape(given[n].shape)
        off += size

    return (loss, grad_x[None], *[grads[n] for n in WEIGHTS], *[delta[n] for n in WEIGHTS],
            *[new_m[n] for n in WEIGHTS], *[new_v[n] for n in WEIGHTS])
```

```python
import functools

import jax
import jax.numpy as jnp
from jax import lax
from jax.experimental import pallas as pl
from jax.experimental.pallas import tpu as pltpu

F32 = jnp.float32
BF16 = jnp.bfloat16
MESH = pl.DeviceIdType.MESH
HBM = pltpu.HBM

EPS = 1e-6
NOPE, ROPE, VHEAD = 128, 64, 128
HEAD_W = NOPE + 2 * ROPE
ROPE_THETA = 10000.0
CONV_TAPS = 3
N_MOD = 6
N_CHIPS, N_CORES, N_DEV = 4, 2, 8
ADAM_LR, ADAM_B1, ADAM_B2, ADAM_EPS, ADAM_WD, ADAM_STEP = 0.001, 0.9, 0.999, 1e-08, 0.01, 10

LANES = 128
SUBLANES = 8
VMEM_LIMIT = 56 * 2**20
MIDDLE_STAGE_AT = 70
SMALL_ROW_TILE = 256

BIG = ("w_in", "w_branch_a", "w_uq", "w_ukv", "w_branch_b", "w_out", "w_up", "w_down")
WEIGHTS = ("w_ada", "b_ada", "pre_norm1_g", "w_in", "gm_ln_g", "gm_ln_b", "gm_w_s", "gm_b_s", "w_branch_a",
           "q_norm_g", "w_uq", "kv_norm_g", "w_ukv", "w_branch_b", "w_out", "post_norm1_g", "pre_norm2_g",
           "w_up", "conv_w", "conv_b", "w_down", "post_norm2_g")
SMALL_PARTIAL = ("gm_ln_g", "gm_ln_b", "gm_w_s", "gm_b_s", "q_norm_g", "kv_norm_g", "post_norm1_g",
                 "pre_norm2_g", "conv_w", "conv_b", "post_norm2_g")
SMALL = ("b_ada", "pre_norm1_g") + SMALL_PARTIAL


def _div_tile(n, cap, mult=LANES):
    t = (min(cap, n) // mult) * mult
    while t >= mult:
        if n % t == 0:
            return t
        t -= mult
    return n


def _params(**kw):
    return pltpu.CompilerParams(vmem_limit_bytes=VMEM_LIMIT, **kw)


def _row_spec(width):
    return pl.BlockSpec((1, width), lambda *_: (0, 0))


def _gelu(x):
    k = 0.7978845608028654
    return 0.5 * x * (1.0 + jnp.tanh(k * (x + 0.044715 * x * x * x)))


def _gelu_grad(x):
    k = 0.7978845608028654
    t = jnp.tanh(k * (x + 0.044715 * x * x * x))
    return 0.5 * (1.0 + t) + 0.5 * x * (1.0 - t * t) * k * (1.0 + 3.0 * 0.044715 * x * x)


def _sigmoid(x):
    return 0.5 * jnp.tanh(0.5 * x) + 0.5


def _dot(a, b, dims):
    return lax.dot_general(a, b, (dims, ((), ())), preferred_element_type=F32)


NN = ((1,), (0,))
NT = ((1,), (1,))
TN = ((0,), (0,))


def _logical(arr):
    if arr.ndim == 2:
        return arr.shape[0], arr.shape[1], arr.shape[1]
    return arr.shape[1], arr.shape[0] * arr.shape[2], arr.shape[2]


def _tile_spec(ndim, group_w, blk_rows, blk_cols, row_of, col_of):
    if ndim == 2:
        return pl.BlockSpec((blk_rows, blk_cols), lambda i, j, k: (row_of(i, j, k), col_of(i, j, k)))
    per = group_w // blk_cols
    return pl.BlockSpec((None, blk_rows, blk_cols),
                        lambda i, j, k: (col_of(i, j, k) // per, row_of(i, j, k), col_of(i, j, k) % per))


def _matmul(a, b, *, mode, out_dtype, name, tm=512, tn=512, tk=2048, mul=None, add=None, out_groups=None, comm=None):
    ar, ac, agw = _logical(a)
    br, bc, bgw = _logical(b)
    if mode == "nn":
        m, kd, n = ar, ac, bc
        m_w, k_w, n_w = (), (agw,), (bgw,)
    elif mode == "nt":
        m, kd, n = ar, ac, br
        m_w, k_w, n_w = (), (agw, bgw), ()
    else:
        m, kd, n = ac, ar, bc
        m_w, k_w, n_w = (agw,), (), (bgw,)
    if out_groups is not None:
        n_w = n_w + (n // out_groups,)
    tm = _div_tile(min((m,) + m_w), tm, LANES if mode == "tn" else SUBLANES)
    tn = _div_tile(min((n,) + n_w), tn)
    tk = _div_tile(min((kd,) + k_w), tk)
    assert all(w % tn == 0 for w in n_w) and all(w % tk == 0 for w in k_w) and all(w % tm == 0 for w in m_w)
    nk = kd // tk
    dims = {"nn": NN, "nt": NT, "tn": TN}[mode]
    gi, gj, gk = (lambda i, j, k: i), (lambda i, j, k: j), (lambda i, j, k: k)
    if mode == "nn":
        a_spec = _tile_spec(a.ndim, agw, tm, tk, gi, gk)
        b_spec = _tile_spec(b.ndim, bgw, tk, tn, gk, gj)
    elif mode == "nt":
        a_spec = _tile_spec(a.ndim, agw, tm, tk, gi, gk)
        b_spec = _tile_spec(b.ndim, bgw, tn, tk, gj, gk)
    else:
        a_spec = _tile_spec(a.ndim, agw, tk, tm, gk, gi)
        b_spec = _tile_spec(b.ndim, bgw, tk, tn, gk, gj)
    in_specs, operands = [a_spec, b_spec], [a, b]
    if mul is not None:
        assert mul.shape == (m, tn)
        in_specs.append(pl.BlockSpec((tm, tn), lambda i, j, k: (i, 0)))
        operands.append(mul)
    if add is not None:
        in_specs.append(pl.BlockSpec((tm, tn), lambda i, j, k: (i, j)))
        operands.append(add)

    def body(*refs):
        a_ref, b_ref = refs[0], refs[1]
        pos = 2
        mul_ref = add_ref = None
        if mul is not None:
            mul_ref, pos = refs[pos], pos + 1
        if add is not None:
            add_ref, pos = refs[pos], pos + 1
        o_ref = refs[pos]

        def finish(r):
            if mul_ref is not None:
                r = r * mul_ref[...]
            if add_ref is not None:
                r = r + add_ref[...]
            o_ref[...] = r.astype(out_dtype)

        part = _dot(a_ref[...], b_ref[...], dims)
        if nk == 1:
            finish(part)
        else:
            acc_ref = refs[pos + 1]
            k = pl.program_id(2)

            @pl.when(k == 0)
            def _():
                acc_ref[...] = part

            @pl.when(k > 0)
            def _():
                acc_ref[...] += part

            @pl.when(k == nk - 1)
            def _():
                finish(acc_ref[...])

    if out_groups is None:
        out_spec, out_dims = _tile_spec(2, n, tm, tn, gi, gj), (m, n)
    else:
        out_spec, out_dims = _tile_spec(3, n // out_groups, tm, tn, gi, gj), (out_groups, m, n // out_groups)
    return _call(body, operands, comm, name=name, grid=(m // tm, n // tn, nk), in_specs=in_specs, out_specs=out_spec,
                 out_shape=jax.ShapeDtypeStruct(out_dims, out_dtype),
                 scratch_shapes=[] if nk == 1 else [pltpu.VMEM((tm, tn), F32)])


def _accumulate(ref, value):
    @pl.when(pl.program_id(0) == 0)
    def _():
        ref[...] = value

    @pl.when(pl.program_id(0) > 0)
    def _():
        ref[...] += value


def _colsum(v):
    return jnp.sum(v, axis=0, keepdims=True)


def _rowmean(v):
    return jnp.mean(v, axis=-1, keepdims=True)


def _prenorm(x, g, scale, shift, name):
    s, d = x.shape
    tb = _div_tile(s, 256, SUBLANES)

    def body(x_ref, g_ref, sc_ref, sh_ref, h_ref):
        xv = x_ref[...]
        r = lax.rsqrt(_rowmean(xv * xv) + EPS)
        h_ref[...] = ((xv * r) * g_ref[...] * (1.0 + sc_ref[...]) + sh_ref[...]).astype(BF16)

    blk = pl.BlockSpec((tb, d), lambda i: (i, 0))
    return pl.pallas_call(
        body, name=name, grid=(s // tb,), in_specs=[blk, _row_spec(d), _row_spec(d), _row_spec(d)],
        out_specs=blk, out_shape=jax.ShapeDtypeStruct((s, d), BF16), compiler_params=_params(),
    )(x, g, scale, shift)


def _post_pre(x, y, gate, pg, g2, scale2, shift2, name):
    s, d = x.shape
    tb = _div_tile(s, 256, SUBLANES)

    def body(x_ref, y_ref, gate_ref, pg_ref, g2_ref, sc_ref, sh_ref, x1_ref, h2_ref):
        yv = y_ref[...]
        rp = lax.rsqrt(_rowmean(yv * yv) + EPS)
        x1 = x_ref[...] + gate_ref[...] * ((yv * rp) * pg_ref[...])
        x1_ref[...] = x1
        r2 = lax.rsqrt(_rowmean(x1 * x1) + EPS)
        h2_ref[...] = ((x1 * r2) * g2_ref[...] * (1.0 + sc_ref[...]) + sh_ref[...]).astype(BF16)

    blk = pl.BlockSpec((tb, d), lambda i: (i, 0))
    return pl.pallas_call(
        body, name=name, grid=(s // tb,), in_specs=[blk, blk] + [_row_spec(d)] * 5,
        out_specs=[blk, blk],
        out_shape=[jax.ShapeDtypeStruct((s, d), F32), jax.ShapeDtypeStruct((s, d), BF16)],
        compiler_params=_params(),
    )(x, y, gate, pg, g2, scale2, shift2)


def _post_bwd(y, gate, pg, name, *, dxo=None, xin=None, target=None):
    s, d = y.shape
    tb = _div_tile(s, 256, SUBLANES)
    from_loss = target is not None

    def body(*refs):
        if from_loss:
            y_ref, gate_ref, pg_ref, xin_ref, t_ref, dy_ref, dgate_ref, dpg_ref, dxo_ref, loss_ref = refs
        else:
            y_ref, gate_ref, pg_ref, dxo_in_ref, dy_ref, dgate_ref, dpg_ref = refs
        yv = y_ref[...]
        rp = lax.rsqrt(_rowmean(yv * yv) + EPS)
        yh = yv * rp
        fn = yh * pg_ref[...]
        gate = gate_ref[...]
        if from_loss:
            err = xin_ref[...] + gate * fn - t_ref[...]
            dxo = err * (1.0 / d)
            dxo_ref[...] = dxo
            part = 0.5 * jnp.sum(_rowmean(err * err), axis=0, keepdims=True)
            _accumulate(loss_ref, jnp.broadcast_to(part, loss_ref.shape))
        else:
            dxo = dxo_in_ref[...]
        _accumulate(dgate_ref, _colsum(dxo * fn))
        dfn = dxo * gate
        _accumulate(dpg_ref, _colsum(dfn * yh))
        dyh = dfn * pg_ref[...]
        dy_ref[...] = (rp * (dyh - yh * _rowmean(dyh * yh))).astype(BF16)

    blk = pl.BlockSpec((tb, d), lambda i: (i, 0))
    in_specs = [blk, _row_spec(d), _row_spec(d)]
    out_specs = [blk, _row_spec(d), _row_spec(d)]
    out_shape = [jax.ShapeDtypeStruct((s, d), BF16), jax.ShapeDtypeStruct((1, d), F32),
                 jax.ShapeDtypeStruct((1, d), F32)]
    if from_loss:
        operands = (y, gate, pg, xin, target)
        in_specs += [blk, blk]
        out_specs += [blk, _row_spec(LANES)]
        out_shape += [jax.ShapeDtypeStruct((s, d), F32), jax.ShapeDtypeStruct((1, LANES), F32)]
    else:
        operands = (y, gate, pg, dxo)
        in_specs += [blk]
    return pl.pallas_call(
        body, name=name, grid=(s // tb,), in_specs=in_specs, out_specs=out_specs, out_shape=out_shape,
        compiler_params=_params(),
    )(*operands)


def _prenorm_bwd(xin, dh, dres, g, scale, name, comm=None):
    s, d = xin.shape
    tb = _div_tile(s, 256, SUBLANES)

    def body(x_ref, dh_ref, dres_ref, g_ref, sc_ref, dx_ref, dshift_ref, dscale_ref, dg_ref):
        xv = x_ref[...]
        r = lax.rsqrt(_rowmean(xv * xv) + EPS)
        xn = xv * r
        dh = dh_ref[...]
        g1 = g_ref[...]
        s1 = 1.0 + sc_ref[...]
        _accumulate(dshift_ref, _colsum(dh))
        _accumulate(dscale_ref, _colsum(dh * xn * g1))
        _accumulate(dg_ref, _colsum(dh * xn * s1))
        dxn = dh * g1 * s1
        dx_ref[...] = dres_ref[...] + r * (dxn - xn * _rowmean(dxn * xn))

    blk = pl.BlockSpec((tb, d), lambda i: (i, 0))
    return _call(
        body, (xin, dh, dres, g, scale), comm, name=name, grid=(s // tb,),
        in_specs=[blk, blk, blk, _row_spec(d), _row_spec(d)],
        out_specs=[blk, _row_spec(d), _row_spec(d), _row_spec(d)],
        out_shape=[jax.ShapeDtypeStruct((s, d), F32)] + [jax.ShapeDtypeStruct((1, d), F32)] * 3)


def _merge(z_big, y_a, y_b, name):
    s, d = y_a.shape
    tb = _div_tile(s, 256, SUBLANES)

    def body(zg_ref, ya_ref, yb_ref, o_ref):
        ga, gb = zg_ref[:, :d].astype(F32), zg_ref[:, d:].astype(F32)
        o_ref[...] = (_sigmoid(ga) * ya_ref[...].astype(F32) + _sigmoid(gb) * yb_ref[...].astype(F32)).astype(BF16)

    blk = pl.BlockSpec((tb, d), lambda i: (i, 0))
    return pl.pallas_call(
        body, name=name, grid=(s // tb,), in_specs=[pl.BlockSpec((tb, 2 * d), lambda i: (i, 1)), blk, blk],
        out_specs=blk, out_shape=jax.ShapeDtypeStruct((s, d), BF16), compiler_params=_params(),
    )(z_big, y_a, y_b)


def _merge_bwd(dmerged, z_big, y_a, y_b, name):
    s, d = y_a.shape
    tb = _div_tile(s, 256, SUBLANES)

    def body(dm_ref, zg_ref, ya_ref, yb_ref, dya_ref, dyb_ref, dz_ref):
        dm = dm_ref[...].astype(F32)
        sa, sb = _sigmoid(zg_ref[:, :d].astype(F32)), _sigmoid(zg_ref[:, d:].astype(F32))
        dya_ref[...] = (dm * sa).astype(BF16)
        dyb_ref[...] = (dm * sb).astype(BF16)
        dz_ref[:, :d] = (dm * ya_ref[...].astype(F32) * sa * (1.0 - sa)).astype(BF16)
        dz_ref[:, d:] = (dm * yb_ref[...].astype(F32) * sb * (1.0 - sb)).astype(BF16)

    blk = pl.BlockSpec((tb, d), lambda i: (i, 0))
    wide = pl.BlockSpec((tb, 2 * d), lambda i: (i, 1))
    return pl.pallas_call(
        body, name=name, grid=(s // tb,), in_specs=[blk, wide, blk, blk], out_specs=[blk, blk, wide],
        out_shape=[jax.ShapeDtypeStruct((s, d), BF16), jax.ShapeDtypeStruct((s, d), BF16),
                   jax.ShapeDtypeStruct((s, 4 * d), BF16)],
        compiler_params=_params(),
    )(dmerged, z_big, y_a, y_b)


def _causal_mask(ch):
    q = lax.broadcasted_iota(jnp.int32, (ch, ch), 0)
    p = lax.broadcasted_iota(jnp.int32, (ch, ch), 1)
    return (p <= q).astype(F32)


def _gmlp_norm(zc, lng, lnb, gw):
    u_pre, v_pre = zc[:, :gw], zc[:, gw:]
    vg = _gelu(v_pre)
    mu = _rowmean(vg)
    cen = vg - mu
    rstd = lax.rsqrt(_rowmean(cen * cen) + EPS)
    vhat = cen * rstd
    return u_pre, v_pre, _gelu(u_pre), vhat, rstd, vhat * lng + lnb


def _gmlp_fwd(z_big, ln_g, ln_b, w_s, b_s_t, name):
    s = z_big.shape[0]
    groups, ch, _ = w_s.shape
    gw = ln_g.shape[1]
    gd = gw // groups

    def body(z_ref, lng_ref, lnb_ref, ws_ref, bt_ref, a_ref):
        _, _, u, _, _, vn = _gmlp_norm(z_ref[...].astype(F32), lng_ref[...], lnb_ref[...], gw)
        mask = _causal_mask(ch)
        for g in range(groups):
            cols = slice(g * gd, (g + 1) * gd)
            wm = (ws_ref[g] * mask).astype(BF16)
            mixed = _dot(wm, vn[:, cols].astype(BF16), NN) + bt_ref[:, g:g + 1]
            a_ref[:, cols] = (u[:, cols] * mixed).astype(BF16)

    return pl.pallas_call(
        body, name=name, grid=(s // ch,),
        in_specs=[pl.BlockSpec((ch, 2 * gw), lambda n: (n, 0)), _row_spec(gw), _row_spec(gw),
                  pl.BlockSpec((groups, ch, ch), lambda n: (0, 0, 0)), pl.BlockSpec((ch, groups), lambda n: (0, 0))],
        out_specs=pl.BlockSpec((ch, gw), lambda n: (n, 0)),
        out_shape=jax.ShapeDtypeStruct((s, gw), BF16), compiler_params=_params(),
    )(z_big, ln_g, ln_b, w_s, b_s_t)


def _gmlp_bwd(z_big, da, dz_big, ln_g, ln_b, w_s, b_s_t, name, comm=None):
    s = z_big.shape[0]
    groups, ch, _ = w_s.shape
    gw = ln_g.shape[1]
    gd = gw // groups

    def body(z_ref, da_ref, dzin_ref, lng_ref, lnb_ref, ws_ref, bt_ref, dz_ref, gws_ref, gbt_ref, glng_ref, glnb_ref):
        del dzin_ref
        lng = lng_ref[...]
        u_pre, v_pre, u, vhat, rstd, vn = _gmlp_norm(z_ref[...].astype(F32), lng, lnb_ref[...], gw)
        da = da_ref[...].astype(F32)
        mask = _causal_mask(ch)
        first = pl.program_id(0) == 0
        dvn_parts = []
        lane = lax.broadcasted_iota(jnp.int32, (ch, LANES), 1)
        gb = jnp.zeros((ch, LANES), F32)
        for g in range(groups):
            cols = slice(g * gd, (g + 1) * gd)
            wm = (ws_ref[g] * mask).astype(BF16)
            vn_g = vn[:, cols].astype(BF16)
            mixed = _dot(wm, vn_g, NN) + bt_ref[:, g:g + 1]
            dz_ref[:, cols] = (da[:, cols] * mixed * _gelu_grad(u_pre[:, cols])).astype(BF16)
            dmixed = da[:, cols] * u[:, cols]
            dm16 = dmixed.astype(BF16)
            dvn_parts.append(_dot(wm, dm16, TN))
            gws = _dot(dm16, vn_g, NT) * mask

            @pl.when(first)
            def _(g=g, gws=gws):
                gws_ref[g] = gws

            @pl.when(jnp.logical_not(first))
            def _(g=g, gws=gws):
                gws_ref[g] += gws

            gb = gb + jnp.where(lane == g, jnp.sum(dmixed, axis=1, keepdims=True), 0.0)
        _accumulate(gbt_ref, gb)
        dvn = jnp.concatenate(dvn_parts, axis=1)
        _accumulate(glnb_ref, _colsum(dvn))
        _accumulate(glng_ref, _colsum(dvn * vhat))
        dvh = dvn * lng
        dvg = rstd * (dvh - _rowmean(dvh) - vhat * _rowmean(dvh * vhat))
        dz_ref[:, gw:] = (dvg * _gelu_grad(v_pre)).astype(BF16)

    zspec = pl.BlockSpec((ch, 2 * gw), lambda n: (n, 0))
    return _call(
        body, (z_big, da, dz_big, ln_g, ln_b, w_s, b_s_t), comm, name=name, grid=(s // ch,),
        in_specs=[zspec, pl.BlockSpec((ch, gw), lambda n: (n, 0)), pl.BlockSpec(memory_space=HBM),
                  _row_spec(gw), _row_spec(gw), pl.BlockSpec((groups, ch, ch), lambda n: (0, 0, 0)),
                  pl.BlockSpec((ch, groups), lambda n: (0, 0))],
        out_specs=[zspec, pl.BlockSpec((groups, ch, ch), lambda n: (0, 0, 0)),
                   pl.BlockSpec((ch, LANES), lambda n: (0, 0)), _row_spec(gw), _row_spec(gw)],
        out_shape=[jax.ShapeDtypeStruct(dz_big.shape, BF16), jax.ShapeDtypeStruct((groups, ch, ch), F32),
                   jax.ShapeDtypeStruct((ch, LANES), F32), jax.ShapeDtypeStruct((1, gw), F32),
                   jax.ShapeDtypeStruct((1, gw), F32)],
        input_output_aliases={2: 0})


def _mla_prep(z_lat, q_g, kv_g, rope_k, name):
    s, latw = z_lat.shape
    ql, kvl = q_g.shape[1], kv_g.shape[1]
    tb = _div_tile(s, 256, SUBLANES)

    def body(z_ref, qg_ref, kvg_ref, t_ref, qn_ref, kvn_ref, kr_ref):
        q = z_ref[:, :ql]
        qn_ref[...] = ((q * lax.rsqrt(_rowmean(q * q) + EPS)) * qg_ref[...]).astype(BF16)
        kv = z_ref[:, ql:ql + kvl]
        kvn_ref[...] = ((kv * lax.rsqrt(_rowmean(kv * kv) + EPS)) * kvg_ref[...]).astype(BF16)
        kk = z_ref[:, ql + kvl:] * t_ref[...]
        kr_ref[...] = (kk + pltpu.roll(kk, ROPE, axis=1)).astype(BF16)

    return pl.pallas_call(
        body, name=name, grid=(s // tb,),
        in_specs=[pl.BlockSpec((tb, latw), lambda i: (i, 0)), _row_spec(ql), _row_spec(kvl),
                  pl.BlockSpec((tb, 2 * ROPE), lambda i: (i, 0))],
        out_specs=[pl.BlockSpec((tb, ql), lambda i: (i, 0)), pl.BlockSpec((tb, kvl), lambda i: (i, 0)),
                   pl.BlockSpec((tb, 2 * ROPE), lambda i: (i, 0))],
        out_shape=[jax.ShapeDtypeStruct((s, ql), BF16), jax.ShapeDtypeStruct((s, kvl), BF16),
                   jax.ShapeDtypeStruct((s, 2 * ROPE), BF16)],
        compiler_params=_params(),
    )(z_lat, q_g, kv_g, rope_k)


def _scores(q, k_full, on_diagonal):
    s = _dot(q, k_full, NT)
    if not on_diagonal:
        return s
    rows = lax.broadcasted_iota(jnp.int32, s.shape, 0)
    cols = lax.broadcasted_iota(jnp.int32, s.shape, 1)
    return jnp.where(cols <= rows, s, -1e30)


def _attn_fwd(q, kv, kr, heads, name, comm=None):
    s = q.shape[0]
    t = _div_tile(s, 512)
    nb = s // t
    hp = 2 if heads % 2 == 0 else 1

    def body(q_ref, k_ref, kr_ref, v_ref, o_ref, lse_ref, m_ref, l_ref, acc_ref):
        i, j = pl.program_id(1), pl.program_id(2)

        @pl.when(j == 0)
        def _():
            m_ref[...] = jnp.full(m_ref.shape, -1e30, F32)
            l_ref[...] = jnp.zeros(l_ref.shape, F32)
            acc_ref[...] = jnp.zeros(acc_ref.shape, F32)

        def step(on_diagonal):
            krv = kr_ref[...]
            for h in range(hp):
                vc = slice(h * VHEAD, (h + 1) * VHEAD)
                k_full = jnp.concatenate([k_ref[:, h * NOPE:(h + 1) * NOPE], krv], axis=1)
                sc = _scores(q_ref[:, h * HEAD_W:(h + 1) * HEAD_W], k_full, on_diagonal)
                m_old = m_ref[h]
                m_new = jnp.maximum(m_old, jnp.max(sc, axis=-1, keepdims=True))
                p = jnp.exp(sc - m_new)
                alpha = jnp.exp(m_old - m_new)
                l_new = alpha * l_ref[h] + jnp.sum(p, axis=-1, keepdims=True)
                acc = alpha * acc_ref[:, vc] + _dot(p.astype(BF16), v_ref[:, vc], NN)
                if on_diagonal:
                    o_ref[:, vc] = (acc / l_new).astype(BF16)
                    lse_ref[h] = jnp.broadcast_to(m_new + jnp.log(l_new), (t, LANES))
                else:
                    m_ref[h], l_ref[h], acc_ref[:, vc] = m_new, l_new, acc

        pl.when(j < i)(lambda: step(False))
        pl.when(j == i)(lambda: step(True))

    kidx = lambda off: (lambda h, i, j: (jnp.minimum(i, j), off(h)))
    return _call(
        body, (q, kv, kr, kv), comm, name=name, grid=(heads // hp, nb, nb),
        in_specs=[pl.BlockSpec((t, hp * HEAD_W), lambda h, i, j: (i, h)),
                  pl.BlockSpec((t, hp * NOPE), kidx(lambda h: h)),
                  pl.BlockSpec((t, 2 * ROPE), kidx(lambda h: 0)),
                  pl.BlockSpec((t, hp * VHEAD), kidx(lambda h: heads // hp + h))],
        out_specs=[pl.BlockSpec((t, hp * VHEAD), lambda h, i, j: (i, h)),
                   pl.BlockSpec((hp, t, LANES), lambda h, i, j: (h, i, 0))],
        out_shape=[jax.ShapeDtypeStruct((s, heads * VHEAD), BF16), jax.ShapeDtypeStruct((heads, s, LANES), F32)],
        scratch_shapes=[pltpu.VMEM((hp, t, 1), F32), pltpu.VMEM((hp, t, 1), F32), pltpu.VMEM((t, hp * VHEAD), F32)])


def _attn_bwd(q, kv, kr, o, do, lse, heads, name, comm=None):
    s = q.shape[0]
    t = _div_tile(s, 512)
    nb = s // t
    hp = 2 if heads % 2 == 0 else 1

    def body(q_ref, k_ref, kr_ref, v_ref, o_ref, do_ref, lse_ref, dq_ref, dk_ref, dv_ref, dk_acc, dv_acc):
        j, i = pl.program_id(1), pl.program_id(2)

        @pl.when(jnp.logical_and(j == 0, i == 0))
        def _():
            dq_ref[...] = jnp.zeros(dq_ref.shape, F32)

        def step(on_diagonal):
            krv = kr_ref[...]
            rows = pl.ds(pl.multiple_of(i * t, t), t)
            for h in range(hp):
                qc, kc, vc = (slice(h * w, (h + 1) * w) for w in (HEAD_W, NOPE, VHEAD))
                qv, do_v = q_ref[:, qc], do_ref[:, vc]
                k_full = jnp.concatenate([k_ref[:, kc], krv], axis=1)
                p = jnp.exp(_scores(qv, k_full, on_diagonal) - lse_ref[h][:, :1])
                dp = _dot(do_v, v_ref[:, vc], NT)
                delta = jnp.sum(do_v.astype(F32) * o_ref[:, vc].astype(F32), axis=-1, keepdims=True)
                ds = (p * (dp - delta)).astype(BF16)
                dq_ref[rows, qc] += _dot(ds, k_full, NN)
                dv_part, dk_part = _dot(p.astype(BF16), do_v, TN), _dot(ds, qv, TN)
                if on_diagonal:
                    dv_acc[:, vc], dk_acc[:, qc] = dv_part, dk_part
                else:
                    dv_acc[:, vc] += dv_part
                    dk_acc[:, qc] += dk_part

        pl.when(i == j)(lambda: step(True))
        pl.when(i > j)(lambda: step(False))

        @pl.when(i == nb - 1)
        def _():
            dk_ref[...] = dk_acc[...].astype(BF16)
            dv_ref[...] = dv_acc[...].astype(BF16)

    qidx = lambda h, j, i: (jnp.maximum(i, j), h)
    return _call(
        body, (q, kv, kr, kv, o, do, lse), comm, name=name, grid=(heads // hp, nb, nb),
        in_specs=[pl.BlockSpec((t, hp * HEAD_W), qidx),
                  pl.BlockSpec((t, hp * NOPE), lambda h, j, i: (j, h)),
                  pl.BlockSpec((t, 2 * ROPE), lambda h, j, i: (j, 0)),
                  pl.BlockSpec((t, hp * VHEAD), lambda h, j, i: (j, heads // hp + h)),
                  pl.BlockSpec((t, hp * VHEAD), qidx), pl.BlockSpec((t, hp * VHEAD), qidx),
                  pl.BlockSpec((hp, t, LANES), lambda h, j, i: (h, jnp.maximum(i, j), 0))],
        out_specs=[pl.BlockSpec((s, hp * HEAD_W), lambda h, j, i: (0, h)),
                   pl.BlockSpec((t, hp * HEAD_W), lambda h, j, i: (j, h)),
                   pl.BlockSpec((t, hp * VHEAD), lambda h, j, i: (j, h))],
        out_shape=[jax.ShapeDtypeStruct((s, heads * HEAD_W), F32), jax.ShapeDtypeStruct((s, heads * HEAD_W), BF16),
                   jax.ShapeDtypeStruct((s, heads * VHEAD), BF16)],
        scratch_shapes=[pltpu.VMEM((t, hp * HEAD_W), F32), pltpu.VMEM((t, hp * VHEAD), F32)])


def _mla_bwd_mid(dq, dk, dv, rope_q, rope_k, heads, name):
    s = dq.shape[0]
    tb = _div_tile(s, 256, SUBLANES)

    def body(dq_ref, dk_ref, dv_ref, tq_ref, tk_ref, dqb_ref, dkv_ref, dkk_ref):
        tq = tq_ref[...]
        dkr = jnp.zeros((tb, 2 * ROPE), F32)
        for h in range(heads):
            cols = slice(h * HEAD_W, (h + 1) * HEAD_W)
            dqb_ref[:, cols] = (dq_ref[:, cols] * tq).astype(BF16)
            dkv_ref[:, h * NOPE:(h + 1) * NOPE] = dk_ref[:, h * HEAD_W:h * HEAD_W + NOPE]
            dkr = dkr + dk_ref[:, h * HEAD_W + NOPE:(h + 1) * HEAD_W].astype(F32)
        dkv_ref[:, heads * NOPE:] = dv_ref[...]
        dkk_ref[...] = (dkr + pltpu.roll(dkr, ROPE, axis=1)) * tk_ref[...]

    wq, wv = heads * HEAD_W, heads * VHEAD
    return pl.pallas_call(
        body, name=name, grid=(s // tb,),
        in_specs=[pl.BlockSpec((tb, wq), lambda i: (i, 0)), pl.BlockSpec((tb, wq), lambda i: (i, 0)),
                  pl.BlockSpec((tb, wv), lambda i: (i, 0)), pl.BlockSpec((tb, HEAD_W), lambda i: (i, 0)),
                  pl.BlockSpec((tb, 2 * ROPE), lambda i: (i, 0))],
        out_specs=[pl.BlockSpec((tb, wq), lambda i: (i, 0)), pl.BlockSpec((tb, heads * NOPE + wv), lambda i: (i, 0)),
                   pl.BlockSpec((tb, 2 * ROPE), lambda i: (i, 0))],
        out_shape=[jax.ShapeDtypeStruct((s, wq), BF16), jax.ShapeDtypeStruct((s, heads * NOPE + wv), BF16),
                   jax.ShapeDtypeStruct((s, 2 * ROPE), F32)],
        compiler_params=_params(),
    )(dq, dk, dv, rope_q, rope_k)


def _mla_bwd_post(z_lat, dqn, dkvn, dkk, q_g, kv_g, name):
    s, latw = z_lat.shape
    ql, kvl = q_g.shape[1], kv_g.shape[1]
    tb = _div_tile(s, 256, SUBLANES)

    def norm_bwd(xv, dn, g, dg_ref):
        r = lax.rsqrt(_rowmean(xv * xv) + EPS)
        xh = xv * r
        _accumulate(dg_ref, _colsum(dn * xh))
        dxh = dn * g
        return r * (dxh - xh * _rowmean(dxh * xh))

    def body(z_ref, dqn_ref, dkvn_ref, dkk_ref, qg_ref, kvg_ref, dz_ref, gq_ref, gkv_ref):
        dz_ref[:, :ql] = norm_bwd(z_ref[:, :ql], dqn_ref[...], qg_ref[...], gq_ref).astype(BF16)
        dz_ref[:, ql:ql + kvl] = norm_bwd(z_ref[:, ql:ql + kvl], dkvn_ref[...], kvg_ref[...], gkv_ref).astype(BF16)
        dz_ref[:, ql + kvl:] = dkk_ref[...].astype(BF16)

    return pl.pallas_call(
        body, name=name, grid=(s // tb,),
        in_specs=[pl.BlockSpec((tb, latw), lambda i: (i, 0)), pl.BlockSpec((tb, ql), lambda i: (i, 0)),
                  pl.BlockSpec((tb, kvl), lambda i: (i, 0)), pl.BlockSpec((tb, 2 * ROPE), lambda i: (i, 0)),
                  _row_spec(ql), _row_spec(kvl)],
        out_specs=[pl.BlockSpec((tb, latw), lambda i: (i, 0)), _row_spec(ql), _row_spec(kvl)],
        out_shape=[jax.ShapeDtypeStruct((s, latw), BF16), jax.ShapeDtypeStruct((1, ql), F32),
                   jax.ShapeDtypeStruct((1, kvl), F32)],
        compiler_params=_params(),
    )(z_lat, dqn, dkvn, dkk, q_g, kv_g)


CONV_ROWS = 128
CONV_HALO = 16


def _row_steps(n_rows, step):
    step(0, True)
    if n_rows > CONV_ROWS:
        def later(i, carry):
            step(pl.multiple_of(i * CONV_ROWS, CONV_ROWS), False)
            return carry
        lax.fori_loop(1, n_rows // CONV_ROWS, later, 0)


def _conv_taps(pre_ref, r0, first):
    if first:
        win = jnp.concatenate([jnp.zeros((CONV_HALO, pre_ref.shape[1]), F32), pre_ref[0:CONV_ROWS, :].astype(F32)])
    else:
        win = pre_ref[pl.ds(pl.multiple_of(r0 - CONV_HALO, CONV_HALO), CONV_ROWS + CONV_HALO), :].astype(F32)
    return win[CONV_HALO:], pltpu.roll(win, 1, axis=0)[CONV_HALO:], pltpu.roll(win, 2, axis=0)[CONV_HALO:]


def _conv(taps, w_ref, b_ref):
    return w_ref[2:3, :] * taps[0] + w_ref[1:2, :] * taps[1] + w_ref[0:1, :] * taps[2] + b_ref[...]


def _conv_fwd(up_pre, conv_w, conv_b, name):
    s, ff2 = up_pre.shape
    ff = ff2 // 2
    tc = _div_tile(ff, 256)
    nb = ff // tc
    assert s % CONV_ROWS == 0

    def body(pg_ref, pv_ref, wg_ref, wv_ref, bg_ref, bv_ref, act_ref):
        def step(r0, first):
            gate = _conv(_conv_taps(pg_ref, r0, first), wg_ref, bg_ref)
            val = _conv(_conv_taps(pv_ref, r0, first), wv_ref, bv_ref)
            act_ref[pl.ds(r0, CONV_ROWS), :] = (gate * _sigmoid(gate) * val).astype(BF16)

        _row_steps(s, step)

    def col(rows, off):
        return pl.BlockSpec((rows, tc), lambda j: (0, j + off))

    return pl.pallas_call(
        body, name=name, grid=(nb,),
        in_specs=[col(s, 0), col(s, nb), col(CONV_TAPS, 0), col(CONV_TAPS, nb), col(1, 0), col(1, nb)],
        out_specs=col(s, 0), out_shape=jax.ShapeDtypeStruct((s, ff), BF16), compiler_params=_params(),
    )(up_pre, up_pre, conv_w, conv_w, conv_b, conv_b)


def _conv_bwd(up_pre, dact, conv_w, conv_b, name, comm=None):
    s, ff2 = up_pre.shape
    ff = ff2 // 2
    tc = _div_tile(ff, 256)
    nb = ff // tc
    assert s % CONV_ROWS == 0

    def body(pg_ref, pv_ref, da_ref, wg_ref, wv_ref, bg_ref, bv_ref, dup_ref, gwg_ref, gwv_ref, gbg_ref, gbv_ref,
             dxg_ref, dxv_ref):
        for ref in (gwg_ref, gwv_ref, gbg_ref, gbv_ref):
            ref[...] = jnp.zeros(ref.shape, F32)
        for ref in (dxg_ref, dxv_ref):
            ref[s:s + SUBLANES, :] = jnp.zeros((SUBLANES, tc), F32)

        def sums(taps, dx, gw_ref, gb_ref):
            gb_ref[...] += _colsum(dx)
            for k in range(CONV_TAPS):
                gw_ref[k:k + 1, :] += _colsum(dx * taps[CONV_TAPS - 1 - k])

        def forward(r0, first):
            rows = pl.ds(r0, CONV_ROWS)
            taps_g, taps_v = _conv_taps(pg_ref, r0, first), _conv_taps(pv_ref, r0, first)
            gate, val = _conv(taps_g, wg_ref, bg_ref), _conv(taps_v, wv_ref, bv_ref)
            da = da_ref[rows, :].astype(F32)
            sg = _sigmoid(gate)
            dxv, dxg = da * gate * sg, da * val * sg * (1.0 + gate * (1.0 - sg))
            dxv_ref[rows, :], dxg_ref[rows, :] = dxv, dxg
            sums(taps_v, dxv, gwv_ref, gbv_ref)
            sums(taps_g, dxg, gwg_ref, gbg_ref)

        def backward(r0, first):
            del first
            n = CONV_ROWS + SUBLANES
            for dx_ref, w_ref, out_ref in ((dxg_ref, wg_ref, dup_ref.at[0]), (dxv_ref, wv_ref, dup_ref.at[1])):
                win = dx_ref[pl.ds(r0, n), :]
                ahead1 = pltpu.roll(win, n - 1, axis=0)[:CONV_ROWS]
                ahead2 = pltpu.roll(win, n - 2, axis=0)[:CONV_ROWS]
                out_ref[pl.ds(r0, CONV_ROWS), :] = (w_ref[2:3, :] * win[:CONV_ROWS] + w_ref[1:2, :] * ahead1
                                                    + w_ref[0:1, :] * ahead2).astype(BF16)

        _row_steps(s, forward)
        _row_steps(s, backward)

    def col(rows, off):
        return pl.BlockSpec((rows, tc), lambda j: (0, j + off))

    return _call(
        body, (up_pre, up_pre, dact, conv_w, conv_w, conv_b, conv_b), comm, name=name, grid=(nb,),
        in_specs=[col(s, 0), col(s, nb), col(s, 0), col(CONV_TAPS, 0), col(CONV_TAPS, nb), col(1, 0), col(1, nb)],
        out_specs=[pl.BlockSpec((2, s, tc), lambda j: (0, 0, j)), col(CONV_TAPS, 0), col(CONV_TAPS, 0),
                   col(1, 0), col(1, 0)],
        out_shape=[jax.ShapeDtypeStruct((2, s, ff), BF16)] + [jax.ShapeDtypeStruct((CONV_TAPS, ff), F32)] * 2
        + [jax.ShapeDtypeStruct((1, ff), F32)] * 2,
        scratch_shapes=[pltpu.VMEM((s + SUBLANES, tc), F32)] * 2)


def _ada_fwd(c_all, w, b, name):
    nseq, d = c_all.shape
    na = w.shape[1]
    tn = _div_tile(na, 512)

    def body(c_ref, w_ref, b_ref, o_ref):
        cv = c_ref[...]
        sc = cv * _sigmoid(cv)
        o_ref[...] = jnp.dot(sc, w_ref[...], preferred_element_type=F32, precision=lax.Precision.HIGHEST) + b_ref[...]

    return pl.pallas_call(
        body, name=name, grid=(na // tn,),
        in_specs=[pl.BlockSpec((nseq, d), lambda j: (0, 0)), pl.BlockSpec((d, tn), lambda j: (0, j)),
                  pl.BlockSpec((1, tn), lambda j: (0, j))],
        out_specs=pl.BlockSpec((nseq, tn), lambda j: (0, j)),
        out_shape=jax.ShapeDtypeStruct((nseq, na), F32), compiler_params=_params(),
    )(c_all, w, b)


def _ada_bwd(c_all_t, dmod, name):
    d, nseq = c_all_t.shape
    na = dmod.shape[1]
    tm, tn = _div_tile(d, 256, SUBLANES), _div_tile(na, 512)

    def body(c_ref, dm_ref, o_ref):
        cv = c_ref[...]
        sc = cv * _sigmoid(cv)
        acc = sc[:, 0:1] * dm_ref[0:1, :]
        for bi in range(1, nseq):
            acc = acc + sc[:, bi:bi + 1] * dm_ref[bi:bi + 1, :]
        o_ref[...] = acc

    return pl.pallas_call(
        body, name=name, grid=(d // tm, na // tn),
        in_specs=[pl.BlockSpec((tm, nseq), lambda i, j: (i, 0)), pl.BlockSpec((nseq, tn), lambda i, j: (0, j))],
        out_specs=pl.BlockSpec((tm, tn), lambda i, j: (i, j)),
        out_shape=jax.ShapeDtypeStruct((d, na), F32), compiler_params=_params(),
    )(c_all_t, dmod)


def _adamw(w, g, m, v, name, comm=None, after=None):
    rows, cols = w.shape
    tb = _div_tile(rows, max(SUBLANES, (256 * 1024) // cols // SUBLANES * SUBLANES), SUBLANES)
    c1 = 1.0 / (1.0 - ADAM_B1 ** ADAM_STEP)
    c2 = 1.0 / (1.0 - ADAM_B2 ** ADAM_STEP)

    def body(*refs):
        w_ref, g_ref, m_ref, v_ref = refs[:4]
        d_ref, nm_ref, nv_ref = refs[-3:]
        gv = g_ref[...]
        nm = ADAM_B1 * m_ref[...] + (1.0 - ADAM_B1) * gv
        nv = ADAM_B2 * v_ref[...] + (1.0 - ADAM_B2) * (gv * gv)
        nm_ref[...] = nm
        nv_ref[...] = nv
        d_ref[...] = -ADAM_LR * ((nm * c1) / (jnp.sqrt(nv * c2) + ADAM_EPS) + ADAM_WD * w_ref[...])

    blk = pl.BlockSpec((tb, cols), lambda i: (i, 0))
    operands, in_specs = (w, g, m, v), [blk] * 4
    if after is not None:
        operands, in_specs = operands + (after,), in_specs + [pl.BlockSpec(after.shape, lambda i: (0, 0))]
    return _call(body, operands, comm, name=name, grid=(rows // tb,), in_specs=in_specs, out_specs=[blk] * 3,
                 out_shape=[jax.ShapeDtypeStruct((rows, cols), F32)] * 3)


def _sum_leading(parts, name, after=()):
    n, rows, cols = parts.shape
    tb = _div_tile(rows, 512, SUBLANES)

    def body(p_ref, *rest):
        o_ref = rest[-1]
        acc = p_ref[0]
        for k in range(1, n):
            acc = acc + p_ref[k]
        o_ref[...] = acc

    return pl.pallas_call(
        body, name=name, grid=(rows // tb,),
        in_specs=[pl.BlockSpec((n, tb, cols), lambda i: (0, i, 0))] + [pl.BlockSpec(memory_space=pl.ANY)] * len(after),
        out_specs=pl.BlockSpec((tb, cols), lambda i: (i, 0)),
        out_shape=jax.ShapeDtypeStruct((rows, cols), F32), compiler_params=_params(),
    )(parts, *after)


def _place():
    x, y, c = lax.axis_index("x"), lax.axis_index("y"), lax.axis_index("c")
    return x, y, c, [(1 - x, y), (x, 1 - y), (1 - x, 1 - y)]


def _all_gather(block, name):
    m_per, n = block.shape

    def body(x_ref, out_ref, send_sems, recv_sems, local_sem):
        x, y, c, chips = _place()
        me, sibling = (x, y, c), (x, y, 1 - c)

        def rows(px, py, pc):
            return out_ref.at[pl.ds((4 * px + 2 * py + pc) * m_per, m_per), :]

        def copy(k, blk, to, src=None):
            return pltpu.make_async_remote_copy(
                src_ref=rows(*blk) if src is None else src, dst_ref=rows(*blk), send_sem=send_sems.at[k],
                recv_sem=recv_sems.at[k], device_id=to, device_id_type=MESH)

        mine = pltpu.make_async_copy(x_ref, rows(*me), local_sem)
        mine.start()
        first = [copy(0, me, sibling, src=x_ref)]
        first += [copy(1 + j, me, (*chip, c), src=x_ref) for j, chip in enumerate(chips)]
        for cp in first:
            cp.start()
        passed = [copy(4 + j, (*chip, c), sibling) for j, chip in enumerate(chips)]
        for j, chip in enumerate(chips):
            copy(1 + j, (*chip, c), me).wait_recv()
            passed[j].start()
        copy(0, sibling, me).wait_recv()
        for j, chip in enumerate(chips):
            copy(4 + j, (*chip, 1 - c), me).wait_recv()
        for cp in first + passed:
            cp.wait_send()
        mine.wait()

    return pl.pallas_call(
        body, name=name, out_shape=jax.ShapeDtypeStruct((N_DEV * m_per, n), block.dtype),
        in_specs=[pl.BlockSpec(memory_space=pltpu.VMEM)], out_specs=pl.BlockSpec(memory_space=pltpu.VMEM),
        scratch_shapes=[pltpu.SemaphoreType.DMA((7,)), pltpu.SemaphoreType.DMA((7,)), pltpu.SemaphoreType.DMA],
        compiler_params=_params(),
    )(block)


def _hbm_specs(n):
    return [pl.BlockSpec(memory_space=HBM)] * n


def _part(ref, by_cols, half, quarter=None, lead=None):
    extent = ref.shape[-1] if by_cols else ref.shape[-2]
    size = extent // 2 if quarter is None else extent // 4
    first = half * (extent // 2) + (0 if quarter is None else quarter * size)
    tile = LANES if by_cols else 2 * SUBLANES
    span = pl.ds(pl.multiple_of(first, tile) if size % tile == 0 else first, size)
    index = (slice(None), span) if by_cols else (span, slice(None))
    return ref.at[index] if lead is None else ref.at[(lead,) + index]


def _half_rows(ref, half, lead=None):
    return _part(ref, False, half, lead=lead)


class _Comm:
    def __init__(self, operands, out_shape, sem_dims, build, aliases=None):
        self.operands, self.out_shape, self.sem_dims = list(operands), list(out_shape), list(sem_dims)
        self.scratch = [pltpu.SemaphoreType.DMA(d) for d in sem_dims]
        self.build, self.aliases = build, dict(aliases or {})


class _SemGrid:
    def __init__(self, sems, dims):
        self.sems, self.dims, self.at = list(sems), tuple(dims), self

    def __getitem__(self, index):
        index = index if isinstance(index, tuple) else (index,)
        flat = 0
        for i, d in zip(index, self.dims):
            flat = flat * d + i
        return self.sems[flat]


def _call(body, operands, comm=None, *, name, grid, in_specs, out_specs, out_shape, scratch_shapes=(),
          input_output_aliases=None):
    aliases = dict(input_output_aliases or {})
    if comm is None:
        return pl.pallas_call(
            body, name=name, grid=grid, in_specs=in_specs, out_specs=out_specs, out_shape=out_shape,
            scratch_shapes=list(scratch_shapes), input_output_aliases=aliases, compiler_params=_params())(*operands)
    single = not isinstance(out_shape, (list, tuple))
    outs = [out_shape] if single else list(out_shape)
    ospecs = [out_specs] if single else list(out_specs)
    n_in, n_out, n_scr = len(operands), len(outs), len(scratch_shapes)
    c_in, c_out = len(comm.operands), len(comm.out_shape)
    for i, o in comm.aliases.items():
        aliases[n_in + i] = n_out + o

    def hosted(*refs):
        ins, c_ins = refs[:n_in], refs[n_in:n_in + c_in]
        o0 = n_in + c_in
        o_refs, c_outs = refs[o0:o0 + n_out], refs[o0 + n_out:o0 + n_out + c_out]
        s0 = o0 + n_out + c_out
        scr, sems = refs[s0:s0 + n_scr], refs[s0 + n_scr:]
        stages = comm.build(c_ins, c_outs, sems)
        step, n_steps = 0, 1
        for dim, size in enumerate(grid):
            step, n_steps = step * size + pl.program_id(dim), n_steps * size
        pl.when(step == 0)(stages[0])
        body(*ins, *o_refs, *scr)
        for stage in stages[1:-1]:
            pl.when(step == (n_steps * MIDDLE_STAGE_AT) // 100)(stage)
        pl.when(step == n_steps - 1)(stages[-1])

    res = pl.pallas_call(
        hosted, name=name, grid=grid, in_specs=list(in_specs) + _hbm_specs(c_in),
        out_specs=ospecs + _hbm_specs(c_out), out_shape=outs + comm.out_shape,
        scratch_shapes=list(scratch_shapes) + comm.scratch, input_output_aliases=aliases,
        compiler_params=_params())(*operands, *comm.operands)
    return (res[0] if single else res[:n_out]), res[n_out:]


def _run_comm(comm, name):
    c_in, c_out = len(comm.operands), len(comm.out_shape)

    def body(*refs):
        for stage in comm.build(refs[:c_in], refs[c_in:c_in + c_out], refs[c_in + c_out:]):
            stage()

    return pl.pallas_call(
        body, name=name, in_specs=_hbm_specs(c_in), out_specs=_hbm_specs(c_out), out_shape=comm.out_shape,
        scratch_shapes=comm.scratch, input_output_aliases=comm.aliases, compiler_params=_params())(*comm.operands)


def _join_comms(comms):
    def build(in_refs, out_refs, sems):
        staged, i, o, k = [], 0, 0, 0
        for cm in comms:
            ni, no, ns = len(cm.operands), len(cm.out_shape), len(cm.sem_dims)
            staged.append(cm.build(in_refs[i:i + ni], out_refs[o:o + no], sems[k:k + ns]))
            i, o, k = i + ni, o + no, k + ns
        def run(fns):
            def stage():
                for fn in fns:
                    fn()
            return stage

        return (run([st[0] for st in staged]), run([fn for st in staged for fn in st[1:-1]]),
                run([st[-1] for st in staged]))

    aliases, i, o = {}, 0, 0
    for cm in comms:
        aliases.update({i + a: o + b for a, b in cm.aliases.items()})
        i, o = i + len(cm.operands), o + len(cm.out_shape)
    return _Comm(sum((cm.operands for cm in comms), []), sum((cm.out_shape for cm in comms), []),
                 sum((cm.sem_dims for cm in comms), []), build, aliases)


def _gather8_comm(block):
    def build(in_refs, out_refs, sems):
        (src,), (out,), (send_sems, recv_sems) = in_refs, out_refs, sems
        x, y, c, chips = _place()
        me, sibling = (x, y, c), (x, y, 1 - c)

        def copy(k, blk, to, own=False):
            dst = out.at[4 * blk[0] + 2 * blk[1] + blk[2]]
            return pltpu.make_async_remote_copy(
                src_ref=src if own else dst, dst_ref=dst, send_sem=send_sems.at[k], recv_sem=recv_sems.at[k],
                device_id=to, device_id_type=MESH)

        first = [copy(0, me, sibling, own=True)] + [copy(1 + j, me, (*chip, c), own=True)
                                                     for j, chip in enumerate(chips)]
        passed = [copy(4 + j, (*chip, c), sibling) for j, chip in enumerate(chips)]

        def start():
            for cp in first:
                cp.start()

        def middle():
            for j, chip in enumerate(chips):
                copy(1 + j, (*chip, c), me).wait_recv()
                passed[j].start()

        def finish():
            copy(0, sibling, me).wait_recv()
            for j, chip in enumerate(chips):
                copy(4 + j, (*chip, 1 - c), me).wait_recv()
            for cp in first + passed:
                cp.wait_send()

        return start, middle, finish

    return _Comm([block], [jax.ShapeDtypeStruct((N_DEV,) + block.shape, block.dtype)], [(7,), (7,)], build)


def _gather_comm(shards, by_cols=()):
    nw = len(shards)

    def build(in_refs, out_refs, sems):
        send_sems, recv_sems = sems
        x, y, c, chips = _place()
        me, sibling = (x, y, c), (x, y, 1 - c)
        across_x, across_y, diagonal = chips

        def copy(w, k, block, part, to, src=None):
            dst = _part(out_refs[w], w in by_cols, part[1], part[2] if part[0] else None, 2 * block[0] + block[1])
            return pltpu.make_async_remote_copy(
                src_ref=dst if src is None else src, dst_ref=dst, send_sem=send_sems.at[w, k],
                recv_sem=recv_sems.at[w, k], device_id=to, device_id_type=MESH)

        first = [copy(w, j, (x, y), (0, c), (*chip, c), src=_part(in_refs[w], w in by_cols, c))
                 for w in range(nw) for j, chip in enumerate((across_x, across_y))]
        passed = [[copy(w, 2, across_x, (1, c, 0), (*across_y, c)), copy(w, 3, across_y, (1, c, 1), (*across_x, c)),
                   copy(w, 4, across_x, (0, c), sibling), copy(w, 5, across_y, (0, c), sibling)] for w in range(nw)]
        last = [[copy(w, 6, diagonal, (1, c, 0), sibling), copy(w, 7, diagonal, (1, c, 1), sibling)]
                for w in range(nw)]

        def start():
            for cp in first:
                cp.start()

        def middle():
            for w in range(nw):
                copy(w, 0, across_x, (0, c), me).wait_recv()
                copy(w, 1, across_y, (0, c), me).wait_recv()
                for cp in passed[w]:
                    cp.start()

        def finish():
            for w in range(nw):
                copy(w, 2, diagonal, (1, c, 0), me).wait_recv()
                copy(w, 3, diagonal, (1, c, 1), me).wait_recv()
                for cp in last[w]:
                    cp.start()
            for w in range(nw):
                for k, block, part in ((4, across_x, (0, 1 - c)), (5, across_y, (0, 1 - c)),
                                       (6, diagonal, (1, 1 - c, 0)), (7, diagonal, (1, 1 - c, 1))):
                    copy(w, k, block, part, me).wait_recv()
            for cp in first + sum(passed, []) + sum(last, []):
                cp.wait_send()

        return start, middle, finish

    return _Comm(shards, [jax.ShapeDtypeStruct((N_CHIPS,) + w.shape, w.dtype) for w in shards],
                 [(nw, 8), (nw, 8)], build)


def _halved(shape, by_cols):
    return shape[:-1] + (shape[-1] // 2,) if by_cols else shape[:-2] + (shape[-2] // 2, shape[-1])


def _swap_comm(gs, by_cols=()):
    nw = len(gs)

    def build(in_refs, out_refs, sems):
        send_sems, recv_sems = sems
        x, y, c, _ = _place()
        cps = []
        for w in range(nw):
            cps.append(pltpu.make_async_remote_copy(
                src_ref=_part(in_refs[w], w in by_cols, 1 - c, lead=slice(None)), dst_ref=out_refs[w],
                send_sem=send_sems.at[w], recv_sem=recv_sems.at[w], device_id=(x, y, 1 - c), device_id_type=MESH))

        def start():
            for cp in cps:
                cp.start()

        def finish():
            for cp in cps:
                cp.wait()

        return start, finish

    return _Comm(gs, [jax.ShapeDtypeStruct(_halved(g.shape, w in by_cols), g.dtype) for w, g in enumerate(gs)],
                 [(nw,), (nw,)], build)


def _exchange_comm(s1s):
    nw = len(s1s)

    def build(in_refs, out_refs, sems):
        send_sems, recv_sems = sems
        x, y, c, chips = _place()
        cps = [pltpu.make_async_remote_copy(
            src_ref=in_refs[w].at[2 * chip[0] + chip[1]], dst_ref=out_refs[w].at[j], send_sem=send_sems.at[w, j],
            recv_sem=recv_sems.at[w, j], device_id=(*chip, c), device_id_type=MESH)
            for w in range(nw) for j, chip in enumerate(chips)]

        def start():
            for cp in cps:
                cp.start()

        def finish():
            for cp in cps:
                cp.wait()

        return start, finish

    return _Comm(s1s, [jax.ShapeDtypeStruct((N_CHIPS - 1,) + s.shape[1:], s.dtype) for s in s1s],
                 [(nw, 3), (nw, 3)], build)


def _size(dims):
    n = 1
    for d in dims:
        n *= d
    return n


def _sem_grids(comm, sem_refs):
    grids, pos = [], 0
    for dims in comm.sem_dims:
        grids.append(_SemGrid(sem_refs[pos:pos + _size(dims)], dims))
        pos += _size(dims)
    return grids


def _comm_split_start(comm, name, after=()):
    c_in, c_out = len(comm.operands), len(comm.out_shape)
    counts = [_size(d) for d in comm.sem_dims]
    n_sem = sum(counts)
    assert not comm.aliases

    def body(*refs):
        srcs, lands = refs[:c_in], refs[c_in:c_in + c_out]
        first_sem = c_in + c_out + len(after)
        start, _ = comm.build(srcs, lands, _sem_grids(comm, refs[first_sem:first_sem + n_sem]))
        start()
        refs[-1][...] = jnp.zeros(refs[-1].shape, refs[-1].dtype)

    lands = [pltpu.with_memory_space_constraint(lax.empty(o.shape, o.dtype), HBM) for o in comm.out_shape]
    srcs = [pltpu.with_memory_space_constraint(a, HBM) for a in comm.operands]
    res = pl.pallas_call(
        body, name=name, in_specs=_hbm_specs(c_in + c_out) + [pl.BlockSpec(memory_space=pl.ANY)] * len(after),
        out_specs=[pl.BlockSpec(memory_space=pltpu.SEMAPHORE)] * n_sem + _hbm_specs(c_in + c_out)
        + [pl.BlockSpec(memory_space=pltpu.VMEM)],
        out_shape=[pltpu.SemaphoreType.DMA(())] * n_sem + [pltpu.HBM(a.shape, a.dtype) for a in comm.operands]
        + [pltpu.HBM(o.shape, o.dtype) for o in comm.out_shape] + [jax.ShapeDtypeStruct((SUBLANES, LANES), F32)],
        input_output_aliases={i: n_sem + i for i in range(c_in + c_out)},
        compiler_params=_params(has_side_effects=pltpu.SideEffectType.DATAFLOW_SIDE_EFFECTING))(*srcs, *lands, *after)
    return res[:-1], res[-1]


def _comm_split_wait(comm, state, after, name):
    c_in, c_out, n_sem = len(comm.operands), len(comm.out_shape), sum(_size(d) for d in comm.sem_dims)
    sems, srcs, lands = state[:n_sem], state[n_sem:n_sem + c_in], state[n_sem + c_in:]

    def body(*refs):
        src_refs, land_refs = refs[:c_in], refs[c_in:c_in + c_out]
        _, finish = comm.build(src_refs, land_refs, _sem_grids(comm, refs[c_in + c_out:c_in + c_out + n_sem]))
        finish()

    sem_spec = pl.BlockSpec(memory_space=pltpu.SEMAPHORE)
    res = pl.pallas_call(
        body, name=name, in_specs=_hbm_specs(c_in + c_out) + [sem_spec] * n_sem + [pl.BlockSpec(memory_space=pl.ANY)],
        out_specs=_hbm_specs(c_in + c_out),
        out_shape=[pltpu.HBM(a.shape, a.dtype) for a in srcs] + [pltpu.HBM(o.shape, o.dtype) for o in lands],
        input_output_aliases={i: i for i in range(c_in + c_out)},
        compiler_params=_params(has_side_effects=pltpu.SideEffectType.DATAFLOW_SIDE_EFFECTING),
    )(*srcs, *lands, *sems, after)
    return res[:c_in], res[c_in:]


def _share_comm(fs, by_cols=()):
    nw = len(fs)

    def build(in_refs, out_refs, sems):
        del in_refs
        send_sems, recv_sems = sems
        x, y, c, _ = _place()

        def copy(w, half):
            part = _part(out_refs[w], w in by_cols, half)
            return pltpu.make_async_remote_copy(
                src_ref=part, dst_ref=part, send_sem=send_sems.at[w], recv_sem=recv_sems.at[w],
                device_id=(x, y, 1 - c), device_id_type=MESH)

        sends = [copy(w, c) for w in range(nw)]

        def start():
            for cp in sends:
                cp.start()

        def finish():
            for w in range(nw):
                copy(w, 1 - c).wait_recv()
            for cp in sends:
                cp.wait_send()

        return start, finish

    return _Comm(fs, [jax.ShapeDtypeStruct(f.shape, f.dtype) for f in fs],
                 [(nw,), (nw,)], build,
                 aliases={w: w for w in range(nw)})


def _add_sibling(g, r1, place, name, by_cols=False):
    nch, h, cols = r1.shape
    tr = _div_tile(h, 1024 if by_cols else 512, 2 * SUBLANES)
    nb = h // tr
    mine = (lambda k, i, p: (k, i, p[0])) if by_cols else (lambda k, i, p: (k, p[0] * nb + i, 0))

    def body(place_ref, g_ref, r_ref, o_ref):
        del place_ref
        o_ref[...] = (g_ref[...].astype(F32) + r_ref[...].astype(F32)).astype(BF16)

    spec = pltpu.PrefetchScalarGridSpec(
        num_scalar_prefetch=1, grid=(nch, nb),
        in_specs=[pl.BlockSpec((None, tr, cols), mine), pl.BlockSpec((None, tr, cols), lambda k, i, p: (k, i, 0))],
        out_specs=pl.BlockSpec((None, tr, cols), lambda k, i, p: (k, i, 0)))
    return pl.pallas_call(body, name=name, grid_spec=spec, out_shape=jax.ShapeDtypeStruct((nch, h, cols), BF16),
                          compiler_params=_params())(place, g, r1)


def _add_chips(s1, r2, place, name, by_cols=False):
    _, h, cols = s1.shape
    tr = _div_tile(h, 1024 if by_cols else 512, 2 * SUBLANES)
    nb = h // tr
    mine = (lambda i, p: (i, p[0])) if by_cols else (lambda i, p: (p[0] * nb + i, 0))
    whole = (h, 2 * cols) if by_cols else (2 * h, cols)

    def body(place_ref, s_ref, r_ref, o_ref):
        del place_ref
        acc = s_ref[...].astype(F32)
        for j in range(N_CHIPS - 1):
            acc = acc + r_ref[j].astype(F32)
        o_ref[...] = acc

    spec = pltpu.PrefetchScalarGridSpec(
        num_scalar_prefetch=1, grid=(nb,),
        in_specs=[pl.BlockSpec((None, tr, cols), lambda i, p: (p[1], i, 0)),
                  pl.BlockSpec((N_CHIPS - 1, tr, cols), lambda i, p: (0, i, 0))],
        out_specs=pl.BlockSpec((tr, cols), mine))
    return pl.pallas_call(body, name=name, grid_spec=spec, out_shape=jax.ShapeDtypeStruct(whole, F32),
                          compiler_params=_params())(place, s1, r2)


def _quarter_turn(m):
    h = m.shape[-1] // 2
    return jnp.concatenate([-m[..., h:], m[..., :h]], axis=-1)


def _quarter_turn_back(m):
    h = m.shape[-1] // 2
    return jnp.concatenate([m[..., h:], -m[..., :h]], axis=-1)


def _stack_rows(parts):
    out = lax.empty((sum(p.shape[0] for p in parts),) + parts[0].shape[1:], parts[0].dtype)
    row = 0
    for p in parts:
        out = lax.dynamic_update_slice(out, p, (row, 0))
        row += p.shape[0]
    return out


def _join_cols(sh):
    return jnp.concatenate([sh[k] for k in range(N_CHIPS)], axis=1)


def _split_cols(full):
    c = full.shape[1] // N_CHIPS
    return jnp.stack([full[:, k * c:(k + 1) * c] for k in range(N_CHIPS)])


def kernel(x, c, positions, w_ada, b_ada, pre_norm1_g, w_in, gm_ln_g, gm_ln_b, gm_w_s, gm_b_s, w_branch_a, q_norm_g, w_uq, kv_norm_g, w_ukv, w_branch_b, w_out, post_norm1_g, pre_norm2_g, w_up, conv_w, conv_b, w_down, post_norm2_g, loss_target, m_w_ada, m_b_ada, m_pre_norm1_g, m_w_in, m_gm_ln_g, m_gm_ln_b, m_gm_w_s, m_gm_b_s, m_w_branch_a, m_q_norm_g, m_w_uq, m_kv_norm_g, m_w_ukv, m_w_branch_b, m_w_out, m_post_norm1_g, m_pre_norm2_g, m_w_up, m_conv_w, m_conv_b, m_w_down, m_post_norm2_g, v_w_ada, v_b_ada, v_pre_norm1_g, v_w_in, v_gm_ln_g, v_gm_ln_b, v_gm_w_s, v_gm_b_s, v_w_branch_a, v_q_norm_g, v_w_uq, v_kv_norm_g, v_w_ukv, v_w_branch_b, v_w_out, v_post_norm1_g, v_pre_norm2_g, v_w_up, v_conv_w, v_conv_b, v_w_down, v_post_norm2_g):
    given = dict(locals())
    s, d = x.shape[1], x.shape[2]
    gw = gm_ln_g.shape[0]
    ql, kvl = q_norm_g.shape[0], kv_norm_g.shape[0]
    heads = N_CHIPS * w_uq.shape[1] // (NOPE + ROPE)
    ff = N_CHIPS * w_down.shape[0]
    assert gw == d and N_CHIPS * w_ukv.shape[1] == heads * (NOPE + VHEAD)
    ix, iy, ic = lax.axis_index("x"), lax.axis_index("y"), lax.axis_index("c")
    chip = 2 * ix + iy
    dev = 2 * chip + ic
    row = lambda v: v.reshape(1, -1)

    c_all = _all_gather(jnp.pad(c, ((0, SUBLANES - 1), (0, 0))), "gather_c").reshape(N_DEV, SUBLANES, d)[:, 0]
    na = w_ada.shape[1]
    b_ada_mine = lax.dynamic_slice(b_ada, (chip * na,), (na,))
    mod_cols = _ada_fwd(c_all, w_ada, row(b_ada_mine), "ada_fwd")
    mod_all = _all_gather(mod_cols, "gather_mod").reshape(N_CHIPS, N_CORES, N_DEV, na)[:, 0]
    mod = lax.dynamic_index_in_dim(mod_all, dev, axis=1, keepdims=False).reshape(N_MOD, d)
    shift1, scale1, gate1, shift2, scale2, gate2 = (mod[i:i + 1] for i in range(N_MOD))

    mine = {n: (given[n].T if n == "w_in" else given[n]).astype(BF16) for n in BIG}
    gather = lambda names: _gather_comm([mine[n] for n in names], [i for i, n in enumerate(names) if n == "w_in"])
    whole = lambda n, g: lax.dynamic_update_slice(g, mine[n][None], (chip, 0, 0))
    rows4 = lambda sh4: sh4.reshape(-1, sh4.shape[2])
    wi_t = rows4(whole("w_in", _run_comm(gather(["w_in"]), "gather_w_in")[0]))
    o_q, o_kv, o_pe, o_ga = 2 * gw, 2 * gw + ql, 2 * gw + ql + kvl, 2 * gw + ql + kvl + ROPE
    w_in_big_t = _stack_rows([wi_t[:o_q], wi_t[o_ga:]])
    w_in_lat_t = _stack_rows([wi_t[o_q:o_ga], _quarter_turn(wi_t[o_pe:o_ga].T).T])

    inv = ROPE_THETA ** (-jnp.arange(0, ROPE, 2, dtype=F32) / ROPE)
    ang = positions[0].astype(F32)[:, None] * inv
    cos, sin = jnp.cos(ang), jnp.sin(ang)
    rope_k = jnp.concatenate([cos, cos, sin, sin], axis=1)
    softmax_scale = float(NOPE + ROPE) ** -0.5
    rope_q = jnp.concatenate([jnp.ones((s, NOPE), F32), rope_k], axis=1) * softmax_scale

    x2d, tgt = x[0], loss_target[0]
    g_pre1, g_post1, g_pre2, g_post2 = row(pre_norm1_g), row(post_norm1_g), row(pre_norm2_g), row(post_norm2_g)
    ln_g, ln_b, q_g, kv_g = row(gm_ln_g), row(gm_ln_b), row(q_norm_g), row(kv_norm_g)
    b_s_t = gm_b_s.T
    conv_wf = _all_gather(jnp.pad(conv_w, ((0, SUBLANES - CONV_TAPS), (0, 0))), "gather_conv_w")
    conv_wf = conv_wf.reshape(N_CHIPS, N_CORES, SUBLANES, conv_w.shape[1])[:, 0, :CONV_TAPS]
    conv_wf = conv_wf.transpose(1, 0, 2).reshape(CONV_TAPS, 2 * ff)
    conv_bf = row(conv_b)

    h1 = _prenorm(x2d, g_pre1, scale1, shift1, "prenorm1")
    z_big, (g_uq, g_ukv, g_a) = _matmul(h1, w_in_big_t, mode="nt", out_dtype=BF16, name="mm_z_big", tm=s,
                                        comm=gather(["w_uq", "w_ukv", "w_branch_a"]))
    wq = _join_cols(whole("w_uq", g_uq)).reshape(ql, heads, NOPE + ROPE)
    w_q = jnp.concatenate([wq, _quarter_turn(wq[:, :, NOPE:])], axis=2).reshape(ql, heads * HEAD_W)
    w_kv = _join_cols(whole("w_ukv", g_ukv)).reshape(kvl, heads, 2, NOPE).transpose(0, 2, 1, 3)
    w_kv = w_kv.reshape(kvl, 2 * heads * NOPE)
    w_a = rows4(whole("w_branch_a", g_a))
    z_lat = _matmul(h1, w_in_lat_t, mode="nt", out_dtype=F32, name="mm_z_lat", tm=s, tn=1024)
    a_act = _gmlp_fwd(z_big, ln_g, ln_b, gm_w_s, b_s_t, "gmlp_fwd")
    qn, kvn, kr = _mla_prep(z_lat, q_g, kv_g, rope_k, "mla_prep")
    q_rot = _matmul(qn, w_q, mode="nn", out_dtype=BF16, name="mm_q", tm=s, tn=HEAD_W, mul=rope_q)
    kv_all = _matmul(kvn, w_kv, mode="nn", out_dtype=BF16, name="mm_kv", tm=s, tn=1024)
    (o_att, lse), (g_b, g_o, g_up) = _attn_fwd(q_rot, kv_all, kr, heads, "attn_fwd",
                                               comm=gather(["w_branch_b", "w_out", "w_up"]))
    w_b, w_o, w_upf = rows4(whole("w_branch_b", g_b)), rows4(whole("w_out", g_o)), whole("w_up", g_up)
    y_a = _matmul(a_act, w_a, mode="nn", out_dtype=BF16, name="mm_y_a", tm=s)
    y_b = _matmul(o_att, w_b, mode="nn", out_dtype=BF16, name="mm_y_b", tm=s)
    merged = _merge(z_big, y_a, y_b, "merge")
    y1 = _matmul(merged, w_o, mode="nn", out_dtype=F32, name="mm_y1", tm=s)
    x1, h2 = _post_pre(x2d, y1, gate1, g_post1, g_pre2, scale2, shift2, "post1_pre2")

    up_pre, (g_dn,) = _matmul(h2, w_upf, mode="nn", out_dtype=BF16, name="mm_up", tm=s, tn=1408,
                              comm=gather(["w_down"]))
    w_dn = rows4(whole("w_down", g_dn))
    act = _conv_fwd(up_pre, conv_wf, conv_bf, "conv_fwd")
    ffn = _matmul(act, w_dn, mode="nn", out_dtype=F32, name="mm_ffn", tm=s, tn=1024, tk=1408)

    dffn, dgate2, g_post2_grad, dx2, loss_part = _post_bwd(ffn, gate2, g_post2, "post2_bwd", xin=x1, target=tgt)
    loss = lax.psum(loss_part[0, 0], ("x", "y", "c"))
    place = jnp.stack([ic, chip]).astype(jnp.int32)
    rows_of = lambda g: g.reshape(N_CHIPS, g.shape[0] // N_CHIPS, g.shape[1])
    add_sibling = lambda names, gs, r1s: [_add_sibling(g, r1, place, "rs_add_sibling_" + n, by_cols=n == "w_in")
                                          for n, g, r1 in zip(names, gs, r1s)]
    add_chips = lambda names, s1s, r2s: [_add_chips(s1, r2, place, "rs_add_chips_" + n, by_cols=n == "w_in")
                                         for n, s1, r2 in zip(names, s1s, r2s)]
    gp_down = [rows_of(_matmul(act, dffn, mode="tn", out_dtype=BF16, name="mm_gw_down", tn=2048, tk=s))]
    dact, r1_down = _matmul(dffn, w_dn, mode="nt", out_dtype=BF16, name="mm_dact", tm=s, comm=_swap_comm(gp_down))
    s1_down = add_sibling(["w_down"], gp_down, r1_down)
    (dup, gcw_g, gcw_v, gcb_g, gcb_v), r2_down = _conv_bwd(up_pre, dact, conv_wf, conv_bf, "conv_bwd",
                                                            comm=_exchange_comm(s1_down))
    half_down = add_chips(["w_down"], s1_down, r2_down)
    dh2 = _matmul(dup, w_upf, mode="nt", out_dtype=F32, name="mm_dh2", tm=s, tn=1024, tk=1408)
    gw_up = _matmul(h2, dup, mode="tn", out_dtype=BF16, name="mm_gw_up", tm=1024, tn=1408, tk=s, out_groups=N_CHIPS)
    dx1, dshift2, dscale2, g_pre2_grad = _prenorm_bwd(x1, dh2, dx2, g_pre2, scale2, "prenorm2_bwd")

    dy1, dgate1, g_post1_grad = _post_bwd(y1, gate1, g_post1, "post1_bwd", dxo=dx1)
    dmerged = _matmul(dy1, w_o, mode="nt", out_dtype=BF16, name="mm_dmerged", tm=s)
    gw_out = _matmul(merged, dy1, mode="tn", out_dtype=BF16, name="mm_gw_out", tn=1024, tk=s)
    dy_a, dy_b, dz_big = _merge_bwd(dmerged, z_big, y_a, y_b, "merge_bwd")
    da = _matmul(dy_a, w_a, mode="nt", out_dtype=BF16, name="mm_da", tm=s)
    gw_a = _matmul(a_act, dy_a, mode="tn", out_dtype=BF16, name="mm_gw_a", tn=1024, tk=s)
    do = _matmul(dy_b, w_b, mode="nt", out_dtype=BF16, name="mm_do", tm=s)
    gw_b = _matmul(o_att, dy_b, mode="tn", out_dtype=BF16, name="mm_gw_b", tn=1024, tk=s)
    mid = ["w_up", "w_out", "w_branch_a", "w_branch_b"]
    gp_mid = [gw_up, rows_of(gw_out), rows_of(gw_a), rows_of(gw_b)]
    (dz_big, g_ws, g_bs_t, g_ln_g, g_ln_b), r1_mid = _gmlp_bwd(z_big, da, dz_big, ln_g, ln_b, gm_w_s, b_s_t,
                                                                "gmlp_bwd", comm=_swap_comm(gp_mid))
    s1_mid = add_sibling(mid, gp_mid, r1_mid)
    (dq, dk, dv), r2_up_out = _attn_bwd(q_rot, kv_all, kr, o_att, do, lse, heads, "attn_bwd",
                                        comm=_exchange_comm(s1_mid[:2]))
    dq_big, dkv, dkk = _mla_bwd_mid(dq, dk, dv, rope_q, rope_k, heads, "mla_bwd_mid")
    gw_q = _matmul(qn, dq_big, mode="tn", out_dtype=F32, name="mm_gw_q", tn=1024, tk=s)
    dqn = _matmul(dq_big, w_q, mode="nt", out_dtype=F32, name="mm_dqn", tm=s, tk=1024)
    gw_kv = _matmul(kvn, dkv, mode="tn", out_dtype=BF16, name="mm_gw_kv", tn=1024, tk=s)
    dkvn = _matmul(dkv, w_kv, mode="nt", out_dtype=F32, name="mm_dkvn", tm=s, tk=1024)
    dz_lat, g_q, g_kv = _mla_bwd_post(z_lat, dqn, dkvn, dkk, q_g, kv_g, "mla_bwd_post")

    partial = {
        "gm_ln_g": g_ln_g, "gm_ln_b": g_ln_b, "gm_w_s": g_ws, "gm_b_s": g_bs_t[:, :gm_b_s.shape[0]].T,
        "q_norm_g": g_q, "kv_norm_g": g_kv, "post_norm1_g": g_post1_grad, "pre_norm2_g": g_pre2_grad,
        "conv_w": jnp.concatenate([gcw_g, gcw_v], axis=1), "conv_b": jnp.concatenate([gcb_g, gcb_v], axis=1),
        "post_norm2_g": g_post2_grad,
    }
    flat = jnp.concatenate([partial[n].reshape(-1) for n in SMALL_PARTIAL])
    n_small = flat.shape[0]
    rows_small = -(-n_small // (LANES * SMALL_ROW_TILE)) * SMALL_ROW_TILE
    flat = jnp.pad(flat, (0, rows_small * LANES - n_small)).reshape(rows_small, LANES)

    def small_pack(prefix, source):
        v = jnp.concatenate([source[prefix + n].reshape(-1) for n in SMALL])
        rows = -(-v.shape[0] // (LANES * SUBLANES)) * SUBLANES
        return jnp.pad(v, (0, rows * LANES - v.shape[0])).reshape(rows, LANES)

    small_state = [small_pack(prefix, given) for prefix in ("", "m_", "v_")]

    dh1, r2_a_b = _matmul(dz_big, w_in_big_t, mode="nn", out_dtype=F32, name="mm_dh1_big", tm=s, tk=1024,
                          comm=_exchange_comm(s1_mid[2:]))
    half_mid = add_chips(mid, s1_mid, list(r2_up_out) + list(r2_a_b))
    dh1 = _matmul(dz_lat, w_in_lat_t, mode="nn", out_dtype=F32, name="mm_dh1_lat", tm=s, tk=1024, add=dh1)
    gw_big_t, hosted = _matmul(dz_big, h1, mode="tn", out_dtype=BF16, name="mm_gw_in_big", tn=2048, tk=s,
                               comm=_join_comms([_share_comm(half_down + half_mid), _gather8_comm(flat)]))
    shared, small_all = hosted[:-1], lax.dynamic_update_slice(hosted[-1], flat[None], (dev, 0, 0))
    small_sum = _sum_leading(small_all, "sum_small", after=small_state).reshape(-1)
    small_grads, off = {}, 0
    for n in SMALL_PARTIAL:
        shape = (CONV_TAPS, 2 * ff) if n == "conv_w" else given[n].shape
        small_grads[n] = small_sum[off:off + partial[n].size].reshape(shape)
        off += partial[n].size
    small_grads["conv_w"] = lax.dynamic_slice(small_grads["conv_w"], (0, chip * conv_w.shape[1]), conv_w.shape)
    grads = dict(zip(["w_down"] + mid, shared), **small_grads)
    gw_lat_t = _matmul(dz_lat, h1, mode="tn", out_dtype=F32, name="mm_gw_in_lat", tm=1024, tn=1024, tk=s)

    gq = gw_q.reshape(ql, heads, HEAD_W)
    gq_pe = gq[:, :, NOPE:NOPE + ROPE] + _quarter_turn_back(gq[:, :, NOPE + ROPE:])
    g_pe_t = gw_lat_t[ql + kvl:ql + kvl + ROPE] + _quarter_turn_back(gw_lat_t[ql + kvl + ROPE:].T).T
    last = ["w_in", "w_uq", "w_ukv"]
    gw_in_t = _stack_rows([gw_big_t[:o_q], gw_lat_t[:ql + kvl].astype(BF16), g_pe_t.astype(BF16), gw_big_t[o_q:]])
    gp_last = [
        gw_in_t.reshape(N_CHIPS, gw_in_t.shape[0] // N_CHIPS, d),
        _split_cols(jnp.concatenate([gq[:, :, :NOPE], gq_pe], axis=2).reshape(ql, heads * (NOPE + ROPE)).astype(BF16)),
        _split_cols(gw_kv.reshape(kvl, 2, heads, NOPE).transpose(0, 2, 1, 3).reshape(kvl, heads * 2 * NOPE)),
    ]
    (grad_x, dshift1, dscale1, g_pre1_grad), r1_last = _prenorm_bwd(x2d, dh1, dx1, g_pre1, scale1, "prenorm1_bwd",
                                                                    comm=_swap_comm(gp_last, by_cols=[0]))
    s1_last = add_sibling(last, gp_last, r1_last)

    dmod = jnp.concatenate([dshift1, dscale1, dgate1, dshift2, dscale2, dgate2, g_pre1_grad], axis=1)
    dmod_all = _all_gather(jnp.pad(dmod, ((0, SUBLANES - 1), (0, 0))), "gather_dmod")
    dmod_all = dmod_all.reshape(N_DEV, SUBLANES, (N_MOD + 1) * d)[:, 0]
    dmod_sum = _sum_leading(dmod_all.reshape(N_DEV, 1, (N_MOD + 1) * d), "sum_dmod")[0]
    grads["b_ada"], grads["pre_norm1_g"] = dmod_sum[:N_MOD * d], dmod_sum[N_MOD * d:]
    dmod_mine = lax.dynamic_slice(dmod_all, (0, chip * na), (N_DEV, na))
    grads["w_ada"] = _ada_bwd(c_all.T, dmod_mine, "ada_bwd")

    delta, new_m, new_v = {}, {}, {}

    def adamw(n, after=None):
        turn = (lambda a: a.T) if n == "w_in" else (lambda a: a)
        outs = _adamw(turn(given[n]), grads[n], turn(given["m_" + n]), turn(given["v_" + n]), "adamw_" + n,
                      after=after)
        grads[n] = turn(grads[n])
        delta[n], new_m[n], new_v[n] = (turn(o) for o in outs)

    exchange_last = _exchange_comm(s1_last)
    in_flight, token = _comm_split_start(exchange_last, "rs_exchange_last_start", after=[dmod_sum, small_sum])
    for n in ["w_ada", "w_down"] + mid:
        adamw(n, after=token)
    s1_last, r2_last = _comm_split_wait(exchange_last, in_flight, delta[mid[-1]], "rs_exchange_last_wait")
    half_last = add_chips(last, s1_last, r2_last)
    grads.update(zip(last, _run_comm(_share_comm(half_last, by_cols=[0]), "rs_share_last")))
    for n in last:
        adamw(n)

    outs = _adamw(small_state[0], small_pack("", grads), small_state[1], small_state[2], "adamw_small")
    off = 0
    for n in SMALL:
        size = given[n].size
        for store, packed_out in zip((delta, new_m, new_v), outs):
            store[n] = packed_out.reshape(-1)[off:off + size].reshape(given[n].shape)
        off += size

    return (loss, grad_x[None], *[grads[n] for n in WEIGHTS], *[delta[n] for n in WEIGHTS],
            *[new_m[n] for n in WEIGHTS], *[new_v[n] for n in WEIGHTS])
```

```python
import functools

import jax
import jax.numpy as jnp
from jax import lax
from jax.experimental import pallas as pl
from jax.experimental.pallas import tpu as pltpu

F32 = jnp.float32
BF16 = jnp.bfloat16
MESH = pl.DeviceIdType.MESH
HBM = pltpu.HBM

EPS = 1e-6
NOPE, ROPE, VHEAD = 128, 64, 128
HEAD_W = NOPE + 2 * ROPE
ROPE_THETA = 10000.0
CONV_TAPS = 3
N_MOD = 6
N_CHIPS, N_CORES, N_DEV = 4, 2, 8
ADAM_LR, ADAM_B1, ADAM_B2, ADAM_EPS, ADAM_WD, ADAM_STEP = 0.001, 0.9, 0.999, 1e-08, 0.01, 10

LANES = 128
SUBLANES = 8
VMEM_LIMIT = 56 * 2**20
MIDDLE_STAGE_AT = 70
SMALL_ROW_TILE = 256

BIG = ("w_in", "w_branch_a", "w_uq", "w_ukv", "w_branch_b", "w_out", "w_up", "w_down")
WEIGHTS = ("w_ada", "b_ada", "pre_norm1_g", "w_in", "gm_ln_g", "gm_ln_b", "gm_w_s", "gm_b_s", "w_branch_a",
           "q_norm_g", "w_uq", "kv_norm_g", "w_ukv", "w_branch_b", "w_out", "post_norm1_g", "pre_norm2_g",
           "w_up", "conv_w", "conv_b", "w_down", "post_norm2_g")
SMALL_PARTIAL = ("gm_ln_g", "gm_ln_b", "gm_w_s", "gm_b_s", "q_norm_g", "kv_norm_g", "post_norm1_g",
                 "pre_norm2_g", "conv_w", "conv_b", "post_norm2_g")
SMALL = ("b_ada", "pre_norm1_g") + SMALL_PARTIAL


def _div_tile(n, cap, mult=LANES):
    t = (min(cap, n) // mult) * mult
    while t >= mult:
        if n % t == 0:
            return t
        t -= mult
    return n


def _params(**kw):
    return pltpu.CompilerParams(vmem_limit_bytes=VMEM_LIMIT, **kw)


def _row_spec(width):
    return pl.BlockSpec((1, width), lambda *_: (0, 0))


def _gelu(x):
    k = 0.7978845608028654
    return 0.5 * x * (1.0 + jnp.tanh(k * (x + 0.044715 * x * x * x)))


def _gelu_grad(x):
    k = 0.7978845608028654
    t = jnp.tanh(k * (x + 0.044715 * x * x * x))
    return 0.5 * (1.0 + t) + 0.5 * x * (1.0 - t * t) * k * (1.0 + 3.0 * 0.044715 * x * x)


def _sigmoid(x):
    return 0.5 * jnp.tanh(0.5 * x) + 0.5


def _dot(a, b, dims):
    return lax.dot_general(a, b, (dims, ((), ())), preferred_element_type=F32)


NN = ((1,), (0,))
NT = ((1,), (1,))
TN = ((0,), (0,))


def _logical(arr):
    if arr.ndim == 2:
        return arr.shape[0], arr.shape[1], arr.shape[1]
    return arr.shape[1], arr.shape[0] * arr.shape[2], arr.shape[2]


def _tile_spec(ndim, group_w, blk_rows, blk_cols, row_of, col_of):
    if ndim == 2:
        return pl.BlockSpec((blk_rows, blk_cols), lambda i, j, k: (row_of(i, j, k), col_of(i, j, k)))
    per = group_w // blk_cols
    return pl.BlockSpec((None, blk_rows, blk_cols),
                        lambda i, j, k: (col_of(i, j, k) // per, row_of(i, j, k), col_of(i, j, k) % per))


def _matmul(a, b, *, mode, out_dtype, name, tm=512, tn=512, tk=2048, mul=None, add=None, out_groups=None, comm=None):
    ar, ac, agw = _logical(a)
    br, bc, bgw = _logical(b)
    if mode == "nn":
        m, kd, n = ar, ac, bc
        m_w, k_w, n_w = (), (agw,), (bgw,)
    elif mode == "nt":
        m, kd, n = ar, ac, br
        m_w, k_w, n_w = (), (agw, bgw), ()
    else:
        m, kd, n = ac, ar, bc
        m_w, k_w, n_w = (agw,), (), (bgw,)
    if out_groups is not None:
        n_w = n_w + (n // out_groups,)
    tm = _div_tile(min((m,) + m_w), tm, LANES if mode == "tn" else SUBLANES)
    tn = _div_tile(min((n,) + n_w), tn)
    tk = _div_tile(min((kd,) + k_w), tk)
    assert all(w % tn == 0 for w in n_w) and all(w % tk == 0 for w in k_w) and all(w % tm == 0 for w in m_w)
    nk = kd // tk
    dims = {"nn": NN, "nt": NT, "tn": TN}[mode]
    gi, gj, gk = (lambda i, j, k: i), (lambda i, j, k: j), (lambda i, j, k: k)
    if mode == "nn":
        a_spec = _tile_spec(a.ndim, agw, tm, tk, gi, gk)
        b_spec = _tile_spec(b.ndim, bgw, tk, tn, gk, gj)
    elif mode == "nt":
        a_spec = _tile_spec(a.ndim, agw, tm, tk, gi, gk)
        b_spec = _tile_spec(b.ndim, bgw, tn, tk, gj, gk)
    else:
        a_spec = _tile_spec(a.ndim, agw, tk, tm, gk, gi)
        b_spec = _tile_spec(b.ndim, bgw, tk, tn, gk, gj)
    in_specs, operands = [a_spec, b_spec], [a, b]
    if mul is not None:
        assert mul.shape == (m, tn)
        in_specs.append(pl.BlockSpec((tm, tn), lambda i, j, k: (i, 0)))
        operands.append(mul)
    if add is not None:
        in_specs.append(pl.BlockSpec((tm, tn), lambda i, j, k: (i, j)))
        operands.append(add)

    def body(*refs):
        a_ref, b_ref = refs[0], refs[1]
        pos = 2
        mul_ref = add_ref = None
        if mul is not None:
            mul_ref, pos = refs[pos], pos + 1
        if add is not None:
            add_ref, pos = refs[pos], pos + 1
        o_ref = refs[pos]

        def finish(r):
            if mul_ref is not None:
                r = r * mul_ref[...]
            if add_ref is not None:
                r = r + add_ref[...]
            o_ref[...] = r.astype(out_dtype)

        part = _dot(a_ref[...], b_ref[...], dims)
        if nk == 1:
            finish(part)
        else:
            acc_ref = refs[pos + 1]
            k = pl.program_id(2)

            @pl.when(k == 0)
            def _():
                acc_ref[...] = part

            @pl.when(k > 0)
            def _():
                acc_ref[...] += part

            @pl.when(k == nk - 1)
            def _():
                finish(acc_ref[...])

    if out_groups is None:
        out_spec, out_dims = _tile_spec(2, n, tm, tn, gi, gj), (m, n)
    else:
        out_spec, out_dims = _tile_spec(3, n // out_groups, tm, tn, gi, gj), (out_groups, m, n // out_groups)
    return _call(body, operands, comm, name=name, grid=(m // tm, n // tn, nk), in_specs=in_specs, out_specs=out_spec,
                 out_shape=jax.ShapeDtypeStruct(out_dims, out_dtype),
                 scratch_shapes=[] if nk == 1 else [pltpu.VMEM((tm, tn), F32)])


def _accumulate(ref, value):
    @pl.when(pl.program_id(0) == 0)
    def _():
        ref[...] = value

    @pl.when(pl.program_id(0) > 0)
    def _():
        ref[...] += value


def _colsum(v):
    return jnp.sum(v, axis=0, keepdims=True)


def _rowmean(v):
    return jnp.mean(v, axis=-1, keepdims=True)


def _prenorm(x, g, scale, shift, name):
    s, d = x.shape
    tb = _div_tile(s, 256, SUBLANES)

    def body(x_ref, g_ref, sc_ref, sh_ref, h_ref):
        xv = x_ref[...]
        r = lax.rsqrt(_rowmean(xv * xv) + EPS)
        h_ref[...] = ((xv * r) * g_ref[...] * (1.0 + sc_ref[...]) + sh_ref[...]).astype(BF16)

    blk = pl.BlockSpec((tb, d), lambda i: (i, 0))
    return pl.pallas_call(
        body, name=name, grid=(s // tb,), in_specs=[blk, _row_spec(d), _row_spec(d), _row_spec(d)],
        out_specs=blk, out_shape=jax.ShapeDtypeStruct((s, d), BF16), compiler_params=_params(),
    )(x, g, scale, shift)


def _post_pre(x, y, gate, pg, g2, scale2, shift2, name):
    s, d = x.shape
    tb = _div_tile(s, 256, SUBLANES)

    def body(x_ref, y_ref, gate_ref, pg_ref, g2_ref, sc_ref, sh_ref, x1_ref, h2_ref):
        yv = y_ref[...]
        rp = lax.rsqrt(_rowmean(yv * yv) + EPS)
        x1 = x_ref[...] + gate_ref[...] * ((yv * rp) * pg_ref[...])
        x1_ref[...] = x1
        r2 = lax.rsqrt(_rowmean(x1 * x1) + EPS)
        h2_ref[...] = ((x1 * r2) * g2_ref[...] * (1.0 + sc_ref[...]) + sh_ref[...]).astype(BF16)

    blk = pl.BlockSpec((tb, d), lambda i: (i, 0))
    return pl.pallas_call(
        body, name=name, grid=(s // tb,), in_specs=[blk, blk] + [_row_spec(d)] * 5,
        out_specs=[blk, blk],
        out_shape=[jax.ShapeDtypeStruct((s, d), F32), jax.ShapeDtypeStruct((s, d), BF16)],
        compiler_params=_params(),
    )(x, y, gate, pg, g2, scale2, shift2)


def _post_bwd(y, gate, pg, name, *, dxo=None, xin=None, target=None):
    s, d = y.shape
    tb = _div_tile(s, 256, SUBLANES)
    from_loss = target is not None

    def body(*refs):
        if from_loss:
            y_ref, gate_ref, pg_ref, xin_ref, t_ref, dy_ref, dgate_ref, dpg_ref, dxo_ref, loss_ref = refs
        else:
            y_ref, gate_ref, pg_ref, dxo_in_ref, dy_ref, dgate_ref, dpg_ref = refs
        yv = y_ref[...]
        rp = lax.rsqrt(_rowmean(yv * yv) + EPS)
        yh = yv * rp
        fn = yh * pg_ref[...]
        gate = gate_ref[...]
        if from_loss:
            err = xin_ref[...] + gate * fn - t_ref[...]
            dxo = err * (1.0 / d)
            dxo_ref[...] = dxo
            part = 0.5 * jnp.sum(_rowmean(err * err), axis=0, keepdims=True)
            _accumulate(loss_ref, jnp.broadcast_to(part, loss_ref.shape))
        else:
            dxo = dxo_in_ref[...]
        _accumulate(dgate_ref, _colsum(dxo * fn))
        dfn = dxo * gate
        _accumulate(dpg_ref, _colsum(dfn * yh))
        dyh = dfn * pg_ref[...]
        dy_ref[...] = (rp * (dyh - yh * _rowmean(dyh * yh))).astype(BF16)

    blk = pl.BlockSpec((tb, d), lambda i: (i, 0))
    in_specs = [blk, _row_spec(d), _row_spec(d)]
    out_specs = [blk, _row_spec(d), _row_spec(d)]
    out_shape = [jax.ShapeDtypeStruct((s, d), BF16), jax.ShapeDtypeStruct((1, d), F32),
                 jax.ShapeDtypeStruct((1, d), F32)]
    if from_loss:
        operands = (y, gate, pg, xin, target)
        in_specs += [blk, blk]
        out_specs += [blk, _row_spec(LANES)]
        out_shape += [jax.ShapeDtypeStruct((s, d), F32), jax.ShapeDtypeStruct((1, LANES), F32)]
    else:
        operands = (y, gate, pg, dxo)
        in_specs += [blk]
    return pl.pallas_call(
        body, name=name, grid=(s // tb,), in_specs=in_specs, out_specs=out_specs, out_shape=out_shape,
        compiler_params=_params(),
    )(*operands)


def _prenorm_bwd(xin, dh, dres, g, scale, name, comm=None):
    s, d = xin.shape
    tb = _div_tile(s, 256, SUBLANES)

    def body(x_ref, dh_ref, dres_ref, g_ref, sc_ref, dx_ref, dshift_ref, dscale_ref, dg_ref):
        xv = x_ref[...]
        r = lax.rsqrt(_rowmean(xv * xv) + EPS)
        xn = xv * r
        dh = dh_ref[...]
        g1 = g_ref[...]
        s1 = 1.0 + sc_ref[...]
        _accumulate(dshift_ref, _colsum(dh))
        _accumulate(dscale_ref, _colsum(dh * xn * g1))
        _accumulate(dg_ref, _colsum(dh * xn * s1))
        dxn = dh * g1 * s1
        dx_ref[...] = dres_ref[...] + r * (dxn - xn * _rowmean(dxn * xn))

    blk = pl.BlockSpec((tb, d), lambda i: (i, 0))
    return _call(
        body, (xin, dh, dres, g, scale), comm, name=name, grid=(s // tb,),
        in_specs=[blk, blk, blk, _row_spec(d), _row_spec(d)],
        out_specs=[blk, _row_spec(d), _row_spec(d), _row_spec(d)],
        out_shape=[jax.ShapeDtypeStruct((s, d), F32)] + [jax.ShapeDtypeStruct((1, d), F32)] * 3)


def _merge(z_big, y_a, y_b, name):
    s, d = y_a.shape
    tb = _div_tile(s, 256, SUBLANES)

    def body(zg_ref, ya_ref, yb_ref, o_ref):
        ga, gb = zg_ref[:, :d].astype(F32), zg_ref[:, d:].astype(F32)
        o_ref[...] = (_sigmoid(ga) * ya_ref[...].astype(F32) + _sigmoid(gb) * yb_ref[...].astype(F32)).astype(BF16)

    blk = pl.BlockSpec((tb, d), lambda i: (i, 0))
    return pl.pallas_call(
        body, name=name, grid=(s // tb,), in_specs=[pl.BlockSpec((tb, 2 * d), lambda i: (i, 1)), blk, blk],
        out_specs=blk, out_shape=jax.ShapeDtypeStruct((s, d), BF16), compiler_params=_params(),
    )(z_big, y_a, y_b)


def _merge_bwd(dmerged, z_big, y_a, y_b, name):
    s, d = y_a.shape
    tb = _div_tile(s, 256, SUBLANES)

    def body(dm_ref, zg_ref, ya_ref, yb_ref, dya_ref, dyb_ref, dz_ref):
        dm = dm_ref[...].astype(F32)
        sa, sb = _sigmoid(zg_ref[:, :d].astype(F32)), _sigmoid(zg_ref[:, d:].astype(F32))
        dya_ref[...] = (dm * sa).astype(BF16)
        dyb_ref[...] = (dm * sb).astype(BF16)
        dz_ref[:, :d] = (dm * ya_ref[...].astype(F32) * sa * (1.0 - sa)).astype(BF16)
        dz_ref[:, d:] = (dm * yb_ref[...].astype(F32) * sb * (1.0 - sb)).astype(BF16)

    blk = pl.BlockSpec((tb, d), lambda i: (i, 0))
    wide = pl.BlockSpec((tb, 2 * d), lambda i: (i, 1))
    return pl.pallas_call(
        body, name=name, grid=(s // tb,), in_specs=[blk, wide, blk, blk], out_specs=[blk, blk, wide],
        out_shape=[jax.ShapeDtypeStruct((s, d), BF16), jax.ShapeDtypeStruct((s, d), BF16),
                   jax.ShapeDtypeStruct((s, 4 * d), BF16)],
        compiler_params=_params(),
    )(dmerged, z_big, y_a, y_b)


def _causal_mask(ch):
    q = lax.broadcasted_iota(jnp.int32, (ch, ch), 0)
    p = lax.broadcasted_iota(jnp.int32, (ch, ch), 1)
    return (p <= q).astype(F32)


def _gmlp_norm(zc, lng, lnb, gw):
    u_pre, v_pre = zc[:, :gw], zc[:, gw:]
    vg = _gelu(v_pre)
    mu = _rowmean(vg)
    cen = vg - mu
    rstd = lax.rsqrt(_rowmean(cen * cen) + EPS)
    vhat = cen * rstd
    return u_pre, v_pre, _gelu(u_pre), vhat, rstd, vhat * lng + lnb


def _gmlp_fwd(z_big, ln_g, ln_b, w_s, b_s_t, name):
    s = z_big.shape[0]
    groups, ch, _ = w_s.shape
    gw = ln_g.shape[1]
    gd = gw // groups

    def body(z_ref, lng_ref, lnb_ref, ws_ref, bt_ref, a_ref):
        _, _, u, _, _, vn = _gmlp_norm(z_ref[...].astype(F32), lng_ref[...], lnb_ref[...], gw)
        mask = _causal_mask(ch)
        for g in range(groups):
            cols = slice(g * gd, (g + 1) * gd)
            wm = (ws_ref[g] * mask).astype(BF16)
            mixed = _dot(wm, vn[:, cols].astype(BF16), NN) + bt_ref[:, g:g + 1]
            a_ref[:, cols] = (u[:, cols] * mixed).astype(BF16)

    return pl.pallas_call(
        body, name=name, grid=(s // ch,),
        in_specs=[pl.BlockSpec((ch, 2 * gw), lambda n: (n, 0)), _row_spec(gw), _row_spec(gw),
                  pl.BlockSpec((groups, ch, ch), lambda n: (0, 0, 0)), pl.BlockSpec((ch, groups), lambda n: (0, 0))],
        out_specs=pl.BlockSpec((ch, gw), lambda n: (n, 0)),
        out_shape=jax.ShapeDtypeStruct((s, gw), BF16), compiler_params=_params(),
    )(z_big, ln_g, ln_b, w_s, b_s_t)


def _gmlp_bwd(z_big, da, dz_big, ln_g, ln_b, w_s, b_s_t, name, comm=None):
    s = z_big.shape[0]
    groups, ch, _ = w_s.shape
    gw = ln_g.shape[1]
    gd = gw // groups

    def body(z_ref, da_ref, dzin_ref, lng_ref, lnb_ref, ws_ref, bt_ref, dz_ref, gws_ref, gbt_ref, glng_ref, glnb_ref):
        del dzin_ref
        lng = lng_ref[...]
        u_pre, v_pre, u, vhat, rstd, vn = _gmlp_norm(z_ref[...].astype(F32), lng, lnb_ref[...], gw)
        da = da_ref[...].astype(F32)
        mask = _causal_mask(ch)
        first = pl.program_id(0) == 0
        dvn_parts = []
        lane = lax.broadcasted_iota(jnp.int32, (ch, LANES), 1)
        gb = jnp.zeros((ch, LANES), F32)
        for g in range(groups):
            cols = slice(g * gd, (g + 1) * gd)
            wm = (ws_ref[g] * mask).astype(BF16)
            vn_g = vn[:, cols].astype(BF16)
            mixed = _dot(wm, vn_g, NN) + bt_ref[:, g:g + 1]
            dz_ref[:, cols] = (da[:, cols] * mixed * _gelu_grad(u_pre[:, cols])).astype(BF16)
            dmixed = da[:, cols] * u[:, cols]
            dm16 = dmixed.astype(BF16)
            dvn_parts.append(_dot(wm, dm16, TN))
            gws = _dot(dm16, vn_g, NT) * mask

            @pl.when(first)
            def _(g=g, gws=gws):
                gws_ref[g] = gws

            @pl.when(jnp.logical_not(first))
            def _(g=g, gws=gws):
                gws_ref[g] += gws

            gb = gb + jnp.where(lane == g, jnp.sum(dmixed, axis=1, keepdims=True), 0.0)
        _accumulate(gbt_ref, gb)
        dvn = jnp.concatenate(dvn_parts, axis=1)
        _accumulate(glnb_ref, _colsum(dvn))
        _accumulate(glng_ref, _colsum(dvn * vhat))
        dvh = dvn * lng
        dvg = rstd * (dvh - _rowmean(dvh) - vhat * _rowmean(dvh * vhat))
        dz_ref[:, gw:] = (dvg * _gelu_grad(v_pre)).astype(BF16)

    zspec = pl.BlockSpec((ch, 2 * gw), lambda n: (n, 0))
    return _call(
        body, (z_big, da, dz_big, ln_g, ln_b, w_s, b_s_t), comm, name=name, grid=(s // ch,),
        in_specs=[zspec, pl.BlockSpec((ch, gw), lambda n: (n, 0)), pl.BlockSpec(memory_space=HBM),
                  _row_spec(gw), _row_spec(gw), pl.BlockSpec((groups, ch, ch), lambda n: (0, 0, 0)),
                  pl.BlockSpec((ch, groups), lambda n: (0, 0))],
        out_specs=[zspec, pl.BlockSpec((groups, ch, ch), lambda n: (0, 0, 0)),
                   pl.BlockSpec((ch, LANES), lambda n: (0, 0)), _row_spec(gw), _row_spec(gw)],
        out_shape=[jax.ShapeDtypeStruct(dz_big.shape, BF16), jax.ShapeDtypeStruct((groups, ch, ch), F32),
                   jax.ShapeDtypeStruct((ch, LANES), F32), jax.ShapeDtypeStruct((1, gw), F32),
                   jax.ShapeDtypeStruct((1, gw), F32)],
        input_output_aliases={2: 0})


def _mla_prep(z_lat, q_g, kv_g, rope_k, name):
    s, latw = z_lat.shape
    ql, kvl = q_g.shape[1], kv_g.shape[1]
    tb = _div_tile(s, 256, SUBLANES)

    def body(z_ref, qg_ref, kvg_ref, t_ref, qn_ref, kvn_ref, kr_ref):
        q = z_ref[:, :ql]
        qn_ref[...] = ((q * lax.rsqrt(_rowmean(q * q) + EPS)) * qg_ref[...]).astype(BF16)
        kv = z_ref[:, ql:ql + kvl]
        kvn_ref[...] = ((kv * lax.rsqrt(_rowmean(kv * kv) + EPS)) * kvg_ref[...]).astype(BF16)
        kk = z_ref[:, ql + kvl:] * t_ref[...]
        kr_ref[...] = (kk + pltpu.roll(kk, ROPE, axis=1)).astype(BF16)

    return pl.pallas_call(
        body, name=name, grid=(s // tb,),
        in_specs=[pl.BlockSpec((tb, latw), lambda i: (i, 0)), _row_spec(ql), _row_spec(kvl),
                  pl.BlockSpec((tb, 2 * ROPE), lambda i: (i, 0))],
        out_specs=[pl.BlockSpec((tb, ql), lambda i: (i, 0)), pl.BlockSpec((tb, kvl), lambda i: (i, 0)),
                   pl.BlockSpec((tb, 2 * ROPE), lambda i: (i, 0))],
        out_shape=[jax.ShapeDtypeStruct((s, ql), BF16), jax.ShapeDtypeStruct((s, kvl), BF16),
                   jax.ShapeDtypeStruct((s, 2 * ROPE), BF16)],
        compiler_params=_params(),
    )(z_lat, q_g, kv_g, rope_k)


def _scores(q, k_full, on_diagonal):
    s = _dot(q, k_full, NT)
    if not on_diagonal:
        return s
    rows = lax.broadcasted_iota(jnp.int32, s.shape, 0)
    cols = lax.broadcasted_iota(jnp.int32, s.shape, 1)
    return jnp.where(cols <= rows, s, -1e30)


def _attn_fwd(q, kv, kr, heads, name, comm=None):
    s = q.shape[0]
    t = _div_tile(s, 512)
    nb = s // t
    hp = 2 if heads % 2 == 0 else 1

    def body(q_ref, k_ref, kr_ref, v_ref, o_ref, lse_ref, m_ref, l_ref, acc_ref):
        i, j = pl.program_id(1), pl.program_id(2)

        @pl.when(j == 0)
        def _():
            m_ref[...] = jnp.full(m_ref.shape, -1e30, F32)
            l_ref[...] = jnp.zeros(l_ref.shape, F32)
            acc_ref[...] = jnp.zeros(acc_ref.shape, F32)

        def update(h, rows, n_keys, on_diagonal):
            vc = slice(h * VHEAD, (h + 1) * VHEAD)
            k_full = jnp.concatenate([k_ref[:n_keys, h * NOPE:(h + 1) * NOPE], kr_ref[:n_keys, :]], axis=1)
            sc = _dot(q_ref[rows, h * HEAD_W:(h + 1) * HEAD_W], k_full, NT)
            if on_diagonal:
                row_pos = rows.start + lax.broadcasted_iota(jnp.int32, sc.shape, 0)
                sc = jnp.where(lax.broadcasted_iota(jnp.int32, sc.shape, 1) <= row_pos, sc, -1e30)
            m_old = m_ref[h, rows, :]
            m_new = jnp.maximum(m_old, jnp.max(sc, axis=-1, keepdims=True))
            p = jnp.exp(sc - m_new)
            alpha = jnp.exp(m_old - m_new)
            l_new = alpha * l_ref[h, rows, :] + jnp.sum(p, axis=-1, keepdims=True)
            acc = alpha * acc_ref[rows, vc] + _dot(p.astype(BF16), v_ref[:n_keys, vc], NN)
            if on_diagonal:
                o_ref[rows, vc] = (acc / l_new).astype(BF16)
                lse_ref[h, rows, :] = jnp.broadcast_to(m_new + jnp.log(l_new), (rows.stop - rows.start, LANES))
            else:
                m_ref[h, rows, :], l_ref[h, rows, :], acc_ref[rows, vc] = m_new, l_new, acc

        def below_diagonal():
            for h in range(hp):
                update(h, slice(0, t), t, False)

        def on_diagonal():
            for h in range(hp):
                update(h, slice(0, t // 2), t // 2, True)
                update(h, slice(t // 2, t), t, True)

        pl.when(j < i)(below_diagonal)
        pl.when(j == i)(on_diagonal)

    kidx = lambda off: (lambda h, i, j: (jnp.minimum(i, j), off(h)))
    return _call(
        body, (q, kv, kr, kv), comm, name=name, grid=(heads // hp, nb, nb),
        in_specs=[pl.BlockSpec((t, hp * HEAD_W), lambda h, i, j: (i, h)),
                  pl.BlockSpec((t, hp * NOPE), kidx(lambda h: h)),
                  pl.BlockSpec((t, 2 * ROPE), kidx(lambda h: 0)),
                  pl.BlockSpec((t, hp * VHEAD), kidx(lambda h: heads // hp + h))],
        out_specs=[pl.BlockSpec((t, hp * VHEAD), lambda h, i, j: (i, h)),
                   pl.BlockSpec((hp, t, LANES), lambda h, i, j: (h, i, 0))],
        out_shape=[jax.ShapeDtypeStruct((s, heads * VHEAD), BF16), jax.ShapeDtypeStruct((heads, s, LANES), F32)],
        scratch_shapes=[pltpu.VMEM((hp, t, 1), F32), pltpu.VMEM((hp, t, 1), F32), pltpu.VMEM((t, hp * VHEAD), F32)])


def _attn_bwd(q, kv, kr, o, do, lse, heads, name, comm=None):
    s = q.shape[0]
    t = _div_tile(s, 512)
    nb = s // t
    hp = 2 if heads % 2 == 0 else 1

    def body(q_ref, k_ref, kr_ref, v_ref, o_ref, do_ref, lse_ref, dq_ref, dk_ref, dv_ref, dk_acc, dv_acc):
        j, i = pl.program_id(1), pl.program_id(2)

        @pl.when(jnp.logical_and(j == 0, i == 0))
        def _():
            dq_ref[...] = jnp.zeros(dq_ref.shape, F32)

        def step(on_diagonal):
            krv = kr_ref[...]
            rows = pl.ds(pl.multiple_of(i * t, t), t)
            for h in range(hp):
                qc, kc, vc = (slice(h * w, (h + 1) * w) for w in (HEAD_W, NOPE, VHEAD))
                qv, do_v = q_ref[:, qc], do_ref[:, vc]
                k_full = jnp.concatenate([k_ref[:, kc], krv], axis=1)
                p = jnp.exp(_scores(qv, k_full, on_diagonal) - lse_ref[h][:, :1])
                dp = _dot(do_v, v_ref[:, vc], NT)
                delta = jnp.sum(do_v.astype(F32) * o_ref[:, vc].astype(F32), axis=-1, keepdims=True)
                ds = (p * (dp - delta)).astype(BF16)
                dq_ref[rows, qc] += _dot(ds, k_full, NN)
                dv_part, dk_part = _dot(p.astype(BF16), do_v, TN), _dot(ds, qv, TN)
                if on_diagonal:
                    dv_acc[:, vc], dk_acc[:, qc] = dv_part, dk_part
                else:
                    dv_acc[:, vc] += dv_part
                    dk_acc[:, qc] += dk_part

        pl.when(i == j)(lambda: step(True))
        pl.when(i > j)(lambda: step(False))

        @pl.when(i == nb - 1)
        def _():
            dk_ref[...] = dk_acc[...].astype(BF16)
            dv_ref[...] = dv_acc[...].astype(BF16)

    qidx = lambda h, j, i: (jnp.maximum(i, j), h)
    return _call(
        body, (q, kv, kr, kv, o, do, lse), comm, name=name, grid=(heads // hp, nb, nb),
        in_specs=[pl.BlockSpec((t, hp * HEAD_W), qidx),
                  pl.BlockSpec((t, hp * NOPE), lambda h, j, i: (j, h)),
                  pl.BlockSpec((t, 2 * ROPE), lambda h, j, i: (j, 0)),
                  pl.BlockSpec((t, hp * VHEAD), lambda h, j, i: (j, heads // hp + h)),
                  pl.BlockSpec((t, hp * VHEAD), qidx), pl.BlockSpec((t, hp * VHEAD), qidx),
                  pl.BlockSpec((hp, t, LANES), lambda h, j, i: (h, jnp.maximum(i, j), 0))],
        out_specs=[pl.BlockSpec((s, hp * HEAD_W), lambda h, j, i: (0, h)),
                   pl.BlockSpec((t, hp * HEAD_W), lambda h, j, i: (j, h)),
                   pl.BlockSpec((t, hp * VHEAD), lambda h, j, i: (j, h))],
        out_shape=[jax.ShapeDtypeStruct((s, heads * HEAD_W), F32), jax.ShapeDtypeStruct((s, heads * HEAD_W), BF16),
                   jax.ShapeDtypeStruct((s, heads * VHEAD), BF16)],
        scratch_shapes=[pltpu.VMEM((t, hp * HEAD_W), F32), pltpu.VMEM((t, hp * VHEAD), F32)])


def _mla_bwd_mid(dq, dk, dv, rope_q, rope_k, heads, name):
    s = dq.shape[0]
    tb = _div_tile(s, 256, SUBLANES)

    def body(dq_ref, dk_ref, dv_ref, tq_ref, tk_ref, dqb_ref, dkv_ref, dkk_ref):
        tq = tq_ref[...]
        dkr = jnp.zeros((tb, 2 * ROPE), F32)
        for h in range(heads):
            cols = slice(h * HEAD_W, (h + 1) * HEAD_W)
            dqb_ref[:, cols] = (dq_ref[:, cols] * tq).astype(BF16)
            dkv_ref[:, h * NOPE:(h + 1) * NOPE] = dk_ref[:, h * HEAD_W:h * HEAD_W + NOPE]
            dkr = dkr + dk_ref[:, h * HEAD_W + NOPE:(h + 1) * HEAD_W].astype(F32)
        dkv_ref[:, heads * NOPE:] = dv_ref[...]
        dkk_ref[...] = (dkr + pltpu.roll(dkr, ROPE, axis=1)) * tk_ref[...]

    wq, wv = heads * HEAD_W, heads * VHEAD
    return pl.pallas_call(
        body, name=name, grid=(s // tb,),
        in_specs=[pl.BlockSpec((tb, wq), lambda i: (i, 0)), pl.BlockSpec((tb, wq), lambda i: (i, 0)),
                  pl.BlockSpec((tb, wv), lambda i: (i, 0)), pl.BlockSpec((tb, HEAD_W), lambda i: (i, 0)),
                  pl.BlockSpec((tb, 2 * ROPE), lambda i: (i, 0))],
        out_specs=[pl.BlockSpec((tb, wq), lambda i: (i, 0)), pl.BlockSpec((tb, heads * NOPE + wv), lambda i: (i, 0)),
                   pl.BlockSpec((tb, 2 * ROPE), lambda i: (i, 0))],
        out_shape=[jax.ShapeDtypeStruct((s, wq), BF16), jax.ShapeDtypeStruct((s, heads * NOPE + wv), BF16),
                   jax.ShapeDtypeStruct((s, 2 * ROPE), F32)],
        compiler_params=_params(),
    )(dq, dk, dv, rope_q, rope_k)


def _mla_bwd_post(z_lat, dqn, dkvn, dkk, q_g, kv_g, name):
    s, latw = z_lat.shape
    ql, kvl = q_g.shape[1], kv_g.shape[1]
    tb = _div_tile(s, 256, SUBLANES)

    def norm_bwd(xv, dn, g, dg_ref):
        r = lax.rsqrt(_rowmean(xv * xv) + EPS)
        xh = xv * r
        _accumulate(dg_ref, _colsum(dn * xh))
        dxh = dn * g
        return r * (dxh - xh * _rowmean(dxh * xh))

    def body(z_ref, dqn_ref, dkvn_ref, dkk_ref, qg_ref, kvg_ref, dz_ref, gq_ref, gkv_ref):
        dz_ref[:, :ql] = norm_bwd(z_ref[:, :ql], dqn_ref[...], qg_ref[...], gq_ref).astype(BF16)
        dz_ref[:, ql:ql + kvl] = norm_bwd(z_ref[:, ql:ql + kvl], dkvn_ref[...], kvg_ref[...], gkv_ref).astype(BF16)
        dz_ref[:, ql + kvl:] = dkk_ref[...].astype(BF16)

    return pl.pallas_call(
        body, name=name, grid=(s // tb,),
        in_specs=[pl.BlockSpec((tb, latw), lambda i: (i, 0)), pl.BlockSpec((tb, ql), lambda i: (i, 0)),
                  pl.BlockSpec((tb, kvl), lambda i: (i, 0)), pl.BlockSpec((tb, 2 * ROPE), lambda i: (i, 0)),
                  _row_spec(ql), _row_spec(kvl)],
        out_specs=[pl.BlockSpec((tb, latw), lambda i: (i, 0)), _row_spec(ql), _row_spec(kvl)],
        out_shape=[jax.ShapeDtypeStruct((s, latw), BF16), jax.ShapeDtypeStruct((1, ql), F32),
                   jax.ShapeDtypeStruct((1, kvl), F32)],
        compiler_params=_params(),
    )(z_lat, dqn, dkvn, dkk, q_g, kv_g)


CONV_ROWS = 128
CONV_HALO = 16


def _row_steps(n_rows, step):
    step(0, True)
    if n_rows > CONV_ROWS:
        def later(i, carry):
            step(pl.multiple_of(i * CONV_ROWS, CONV_ROWS), False)
            return carry
        lax.fori_loop(1, n_rows // CONV_ROWS, later, 0)


def _conv_taps(pre_ref, r0, first):
    if first:
        win = jnp.concatenate([jnp.zeros((CONV_HALO, pre_ref.shape[1]), F32), pre_ref[0:CONV_ROWS, :].astype(F32)])
    else:
        win = pre_ref[pl.ds(pl.multiple_of(r0 - CONV_HALO, CONV_HALO), CONV_ROWS + CONV_HALO), :].astype(F32)
    return win[CONV_HALO:], pltpu.roll(win, 1, axis=0)[CONV_HALO:], pltpu.roll(win, 2, axis=0)[CONV_HALO:]


def _conv(taps, w_ref, b_ref):
    return w_ref[2:3, :] * taps[0] + w_ref[1:2, :] * taps[1] + w_ref[0:1, :] * taps[2] + b_ref[...]


def _conv_fwd(up_pre, conv_w, conv_b, name):
    s, ff2 = up_pre.shape
    ff = ff2 // 2
    tc = _div_tile(ff, 256)
    nb = ff // tc
    assert s % CONV_ROWS == 0

    def body(pg_ref, pv_ref, wg_ref, wv_ref, bg_ref, bv_ref, act_ref):
        def step(r0, first):
            gate = _conv(_conv_taps(pg_ref, r0, first), wg_ref, bg_ref)
            val = _conv(_conv_taps(pv_ref, r0, first), wv_ref, bv_ref)
            act_ref[pl.ds(r0, CONV_ROWS), :] = (gate * _sigmoid(gate) * val).astype(BF16)

        _row_steps(s, step)

    def col(rows, off):
        return pl.BlockSpec((rows, tc), lambda j: (0, j + off))

    return pl.pallas_call(
        body, name=name, grid=(nb,),
        in_specs=[col(s, 0), col(s, nb), col(CONV_TAPS, 0), col(CONV_TAPS, nb), col(1, 0), col(1, nb)],
        out_specs=col(s, 0), out_shape=jax.ShapeDtypeStruct((s, ff), BF16), compiler_params=_params(),
    )(up_pre, up_pre, conv_w, conv_w, conv_b, conv_b)


def _conv_bwd(up_pre, dact, conv_w, conv_b, name, comm=None):
    s, ff2 = up_pre.shape
    ff = ff2 // 2
    tc = _div_tile(ff, 256)
    nb = ff // tc
    assert s % CONV_ROWS == 0

    def body(pg_ref, pv_ref, da_ref, wg_ref, wv_ref, bg_ref, bv_ref, dup_ref, gwg_ref, gwv_ref, gbg_ref, gbv_ref,
             dxg_ref, dxv_ref):
        for ref in (gwg_ref, gwv_ref, gbg_ref, gbv_ref):
            ref[...] = jnp.zeros(ref.shape, F32)
        for ref in (dxg_ref, dxv_ref):
            ref[s:s + SUBLANES, :] = jnp.zeros((SUBLANES, tc), F32)

        def sums(taps, dx, gw_ref, gb_ref):
            gb_ref[...] += _colsum(dx)
            for k in range(CONV_TAPS):
                gw_ref[k:k + 1, :] += _colsum(dx * taps[CONV_TAPS - 1 - k])

        def forward(r0, first):
            rows = pl.ds(r0, CONV_ROWS)
            taps_g, taps_v = _conv_taps(pg_ref, r0, first), _conv_taps(pv_ref, r0, first)
            gate, val = _conv(taps_g, wg_ref, bg_ref), _conv(taps_v, wv_ref, bv_ref)
            da = da_ref[rows, :].astype(F32)
            sg = _sigmoid(gate)
            dxv, dxg = da * gate * sg, da * val * sg * (1.0 + gate * (1.0 - sg))
            dxv_ref[rows, :], dxg_ref[rows, :] = dxv, dxg
            sums(taps_v, dxv, gwv_ref, gbv_ref)
            sums(taps_g, dxg, gwg_ref, gbg_ref)

        def backward(r0, first):
            del first
            n = CONV_ROWS + SUBLANES
            for dx_ref, w_ref, out_ref in ((dxg_ref, wg_ref, dup_ref.at[0]), (dxv_ref, wv_ref, dup_ref.at[1])):
                win = dx_ref[pl.ds(r0, n), :]
                ahead1 = pltpu.roll(win, n - 1, axis=0)[:CONV_ROWS]
                ahead2 = pltpu.roll(win, n - 2, axis=0)[:CONV_ROWS]
                out_ref[pl.ds(r0, CONV_ROWS), :] = (w_ref[2:3, :] * win[:CONV_ROWS] + w_ref[1:2, :] * ahead1
                                                    + w_ref[0:1, :] * ahead2).astype(BF16)

        _row_steps(s, forward)
        _row_steps(s, backward)

    def col(rows, off):
        return pl.BlockSpec((rows, tc), lambda j: (0, j + off))

    return _call(
        body, (up_pre, up_pre, dact, conv_w, conv_w, conv_b, conv_b), comm, name=name, grid=(nb,),
        in_specs=[col(s, 0), col(s, nb), col(s, 0), col(CONV_TAPS, 0), col(CONV_TAPS, nb), col(1, 0), col(1, nb)],
        out_specs=[pl.BlockSpec((2, s, tc), lambda j: (0, 0, j)), col(CONV_TAPS, 0), col(CONV_TAPS, 0),
                   col(1, 0), col(1, 0)],
        out_shape=[jax.ShapeDtypeStruct((2, s, ff), BF16)] + [jax.ShapeDtypeStruct((CONV_TAPS, ff), F32)] * 2
        + [jax.ShapeDtypeStruct((1, ff), F32)] * 2,
        scratch_shapes=[pltpu.VMEM((s + SUBLANES, tc), F32)] * 2)


def _ada_fwd(c_all, w, b, name):
    nseq, d = c_all.shape
    na = w.shape[1]
    tn = _div_tile(na, 512)

    def body(c_ref, w_ref, b_ref, o_ref):
        cv = c_ref[...]
        sc = cv * _sigmoid(cv)
        o_ref[...] = jnp.dot(sc, w_ref[...], preferred_element_type=F32, precision=lax.Precision.HIGHEST) + b_ref[...]

    return pl.pallas_call(
        body, name=name, grid=(na // tn,),
        in_specs=[pl.BlockSpec((nseq, d), lambda j: (0, 0)), pl.BlockSpec((d, tn), lambda j: (0, j)),
                  pl.BlockSpec((1, tn), lambda j: (0, j))],
        out_specs=pl.BlockSpec((nseq, tn), lambda j: (0, j)),
        out_shape=jax.ShapeDtypeStruct((nseq, na), F32), compiler_params=_params(),
    )(c_all, w, b)


def _ada_bwd(c_all_t, dmod, name):
    d, nseq = c_all_t.shape
    na = dmod.shape[1]
    tm, tn = _div_tile(d, 256, SUBLANES), _div_tile(na, 512)

    def body(c_ref, dm_ref, o_ref):
        cv = c_ref[...]
        sc = cv * _sigmoid(cv)
        acc = sc[:, 0:1] * dm_ref[0:1, :]
        for bi in range(1, nseq):
            acc = acc + sc[:, bi:bi + 1] * dm_ref[bi:bi + 1, :]
        o_ref[...] = acc

    return pl.pallas_call(
        body, name=name, grid=(d // tm, na // tn),
        in_specs=[pl.BlockSpec((tm, nseq), lambda i, j: (i, 0)), pl.BlockSpec((nseq, tn), lambda i, j: (0, j))],
        out_specs=pl.BlockSpec((tm, tn), lambda i, j: (i, j)),
        out_shape=jax.ShapeDtypeStruct((d, na), F32), compiler_params=_params(),
    )(c_all_t, dmod)


def _adamw(w, g, m, v, name, comm=None, after=None):
    rows, cols = w.shape
    tb = _div_tile(rows, max(SUBLANES, (256 * 1024) // cols // SUBLANES * SUBLANES), SUBLANES)
    c1 = 1.0 / (1.0 - ADAM_B1 ** ADAM_STEP)
    c2 = 1.0 / (1.0 - ADAM_B2 ** ADAM_STEP)

    def body(*refs):
        w_ref, g_ref, m_ref, v_ref = refs[:4]
        d_ref, nm_ref, nv_ref = refs[-3:]
        gv = g_ref[...]
        nm = ADAM_B1 * m_ref[...] + (1.0 - ADAM_B1) * gv
        nv = ADAM_B2 * v_ref[...] + (1.0 - ADAM_B2) * (gv * gv)
        nm_ref[...] = nm
        nv_ref[...] = nv
        d_ref[...] = -ADAM_LR * ((nm * c1) / (jnp.sqrt(nv * c2) + ADAM_EPS) + ADAM_WD * w_ref[...])

    blk = pl.BlockSpec((tb, cols), lambda i: (i, 0))
    operands, in_specs = (w, g, m, v), [blk] * 4
    if after is not None:
        operands, in_specs = operands + (after,), in_specs + [pl.BlockSpec(after.shape, lambda i: (0, 0))]
    return _call(body, operands, comm, name=name, grid=(rows // tb,), in_specs=in_specs, out_specs=[blk] * 3,
                 out_shape=[jax.ShapeDtypeStruct((rows, cols), F32)] * 3)


def _sum_leading(parts, name, after=()):
    n, rows, cols = parts.shape
    tb = _div_tile(rows, 512, SUBLANES)

    def body(p_ref, *rest):
        o_ref = rest[-1]
        acc = p_ref[0]
        for k in range(1, n):
            acc = acc + p_ref[k]
        o_ref[...] = acc

    return pl.pallas_call(
        body, name=name, grid=(rows // tb,),
        in_specs=[pl.BlockSpec((n, tb, cols), lambda i: (0, i, 0))] + [pl.BlockSpec(memory_space=pl.ANY)] * len(after),
        out_specs=pl.BlockSpec((tb, cols), lambda i: (i, 0)),
        out_shape=jax.ShapeDtypeStruct((rows, cols), F32), compiler_params=_params(),
    )(parts, *after)


def _place():
    x, y, c = lax.axis_index("x"), lax.axis_index("y"), lax.axis_index("c")
    return x, y, c, [(1 - x, y), (x, 1 - y), (1 - x, 1 - y)]


def _all_gather(block, name):
    m_per, n = block.shape

    def body(x_ref, out_ref, send_sems, recv_sems, local_sem):
        x, y, c, chips = _place()
        me, sibling = (x, y, c), (x, y, 1 - c)

        def rows(px, py, pc):
            return out_ref.at[pl.ds((4 * px + 2 * py + pc) * m_per, m_per), :]

        def copy(k, blk, to, src=None):
            return pltpu.make_async_remote_copy(
                src_ref=rows(*blk) if src is None else src, dst_ref=rows(*blk), send_sem=send_sems.at[k],
                recv_sem=recv_sems.at[k], device_id=to, device_id_type=MESH)

        mine = pltpu.make_async_copy(x_ref, rows(*me), local_sem)
        mine.start()
        first = [copy(0, me, sibling, src=x_ref)]
        first += [copy(1 + j, me, (*chip, c), src=x_ref) for j, chip in enumerate(chips)]
        for cp in first:
            cp.start()
        passed = [copy(4 + j, (*chip, c), sibling) for j, chip in enumerate(chips)]
        for j, chip in enumerate(chips):
            copy(1 + j, (*chip, c), me).wait_recv()
            passed[j].start()
        copy(0, sibling, me).wait_recv()
        for j, chip in enumerate(chips):
            copy(4 + j, (*chip, 1 - c), me).wait_recv()
        for cp in first + passed:
            cp.wait_send()
        mine.wait()

    return pl.pallas_call(
        body, name=name, out_shape=jax.ShapeDtypeStruct((N_DEV * m_per, n), block.dtype),
        in_specs=[pl.BlockSpec(memory_space=pltpu.VMEM)], out_specs=pl.BlockSpec(memory_space=pltpu.VMEM),
        scratch_shapes=[pltpu.SemaphoreType.DMA((7,)), pltpu.SemaphoreType.DMA((7,)), pltpu.SemaphoreType.DMA],
        compiler_params=_params(),
    )(block)


def _hbm_specs(n):
    return [pl.BlockSpec(memory_space=HBM)] * n


def _part(ref, by_cols, half, quarter=None, lead=None):
    extent = ref.shape[-1] if by_cols else ref.shape[-2]
    size = extent // 2 if quarter is None else extent // 4
    first = half * (extent // 2) + (0 if quarter is None else quarter * size)
    tile = LANES if by_cols else 2 * SUBLANES
    span = pl.ds(pl.multiple_of(first, tile) if size % tile == 0 else first, size)
    index = (slice(None), span) if by_cols else (span, slice(None))
    return ref.at[index] if lead is None else ref.at[(lead,) + index]


def _half_rows(ref, half, lead=None):
    return _part(ref, False, half, lead=lead)


class _Comm:
    def __init__(self, operands, out_shape, sem_dims, build, aliases=None):
        self.operands, self.out_shape, self.sem_dims = list(operands), list(out_shape), list(sem_dims)
        self.scratch = [pltpu.SemaphoreType.DMA(d) for d in sem_dims]
        self.build, self.aliases = build, dict(aliases or {})


class _SemGrid:
    def __init__(self, sems, dims):
        self.sems, self.dims, self.at = list(sems), tuple(dims), self

    def __getitem__(self, index):
        index = index if isinstance(index, tuple) else (index,)
        flat = 0
        for i, d in zip(index, self.dims):
            flat = flat * d + i
        return self.sems[flat]


def _call(body, operands, comm=None, *, name, grid, in_specs, out_specs, out_shape, scratch_shapes=(),
          input_output_aliases=None):
    aliases = dict(input_output_aliases or {})
    if comm is None:
        return pl.pallas_call(
            body, name=name, grid=grid, in_specs=in_specs, out_specs=out_specs, out_shape=out_shape,
            scratch_shapes=list(scratch_shapes), input_output_aliases=aliases, compiler_params=_params())(*operands)
    single = not isinstance(out_shape, (list, tuple))
    outs = [out_shape] if single else list(out_shape)
    ospecs = [out_specs] if single else list(out_specs)
    n_in, n_out, n_scr = len(operands), len(outs), len(scratch_shapes)
    c_in, c_out = len(comm.operands), len(comm.out_shape)
    for i, o in comm.aliases.items():
        aliases[n_in + i] = n_out + o

    def hosted(*refs):
        ins, c_ins = refs[:n_in], refs[n_in:n_in + c_in]
        o0 = n_in + c_in
        o_refs, c_outs = refs[o0:o0 + n_out], refs[o0 + n_out:o0 + n_out + c_out]
        s0 = o0 + n_out + c_out
        scr, sems = refs[s0:s0 + n_scr], refs[s0 + n_scr:]
        stages = comm.build(c_ins, c_outs, sems)
        step, n_steps = 0, 1
        for dim, size in enumerate(grid):
            step, n_steps = step * size + pl.program_id(dim), n_steps * size
        pl.when(step == 0)(stages[0])
        body(*ins, *o_refs, *scr)
        for stage in stages[1:-1]:
            pl.when(step == (n_steps * MIDDLE_STAGE_AT) // 100)(stage)
        pl.when(step == n_steps - 1)(stages[-1])

    res = pl.pallas_call(
        hosted, name=name, grid=grid, in_specs=list(in_specs) + _hbm_specs(c_in),
        out_specs=ospecs + _hbm_specs(c_out), out_shape=outs + comm.out_shape,
        scratch_shapes=list(scratch_shapes) + comm.scratch, input_output_aliases=aliases,
        compiler_params=_params())(*operands, *comm.operands)
    return (res[0] if single else res[:n_out]), res[n_out:]


def _run_comm(comm, name):
    c_in, c_out = len(comm.operands), len(comm.out_shape)

    def body(*refs):
        for stage in comm.build(refs[:c_in], refs[c_in:c_in + c_out], refs[c_in + c_out:]):
            stage()

    return pl.pallas_call(
        body, name=name, in_specs=_hbm_specs(c_in), out_specs=_hbm_specs(c_out), out_shape=comm.out_shape,
        scratch_shapes=comm.scratch, input_output_aliases=comm.aliases, compiler_params=_params())(*comm.operands)


def _join_comms(comms):
    def build(in_refs, out_refs, sems):
        staged, i, o, k = [], 0, 0, 0
        for cm in comms:
            ni, no, ns = len(cm.operands), len(cm.out_shape), len(cm.sem_dims)
            staged.append(cm.build(in_refs[i:i + ni], out_refs[o:o + no], sems[k:k + ns]))
            i, o, k = i + ni, o + no, k + ns
        def run(fns):
            def stage():
                for fn in fns:
                    fn()
            return stage

        return (run([st[0] for st in staged]), run([fn for st in staged for fn in st[1:-1]]),
                run([st[-1] for st in staged]))

    aliases, i, o = {}, 0, 0
    for cm in comms:
        aliases.update({i + a: o + b for a, b in cm.aliases.items()})
        i, o = i + len(cm.operands), o + len(cm.out_shape)
    return _Comm(sum((cm.operands for cm in comms), []), sum((cm.out_shape for cm in comms), []),
                 sum((cm.sem_dims for cm in comms), []), build, aliases)


def _gather8_comm(block):
    def build(in_refs, out_refs, sems):
        (src,), (out,), (send_sems, recv_sems) = in_refs, out_refs, sems
        x, y, c, chips = _place()
        me, sibling = (x, y, c), (x, y, 1 - c)

        def copy(k, blk, to, own=False):
            dst = out.at[4 * blk[0] + 2 * blk[1] + blk[2]]
            return pltpu.make_async_remote_copy(
                src_ref=src if own else dst, dst_ref=dst, send_sem=send_sems.at[k], recv_sem=recv_sems.at[k],
                device_id=to, device_id_type=MESH)

        first = [copy(0, me, sibling, own=True)] + [copy(1 + j, me, (*chip, c), own=True)
                                                     for j, chip in enumerate(chips)]
        passed = [copy(4 + j, (*chip, c), sibling) for j, chip in enumerate(chips)]

        def start():
            for cp in first:
                cp.start()

        def middle():
            for j, chip in enumerate(chips):
                copy(1 + j, (*chip, c), me).wait_recv()
                passed[j].start()

        def finish():
            copy(0, sibling, me).wait_recv()
            for j, chip in enumerate(chips):
                copy(4 + j, (*chip, 1 - c), me).wait_recv()
            for cp in first + passed:
                cp.wait_send()

        return start, middle, finish

    return _Comm([block], [jax.ShapeDtypeStruct((N_DEV,) + block.shape, block.dtype)], [(7,), (7,)], build)


def _gather_comm(shards, by_cols=()):
    nw = len(shards)

    def build(in_refs, out_refs, sems):
        send_sems, recv_sems = sems
        x, y, c, chips = _place()
        me, sibling = (x, y, c), (x, y, 1 - c)
        across_x, across_y, diagonal = chips

        def copy(w, k, block, part, to, src=None):
            dst = _part(out_refs[w], w in by_cols, part[1], part[2] if part[0] else None, 2 * block[0] + block[1])
            return pltpu.make_async_remote_copy(
                src_ref=dst if src is None else src, dst_ref=dst, send_sem=send_sems.at[w, k],
                recv_sem=recv_sems.at[w, k], device_id=to, device_id_type=MESH)

        first = [copy(w, j, (x, y), (0, c), (*chip, c), src=_part(in_refs[w], w in by_cols, c))
                 for w in range(nw) for j, chip in enumerate((across_x, across_y))]
        passed = [[copy(w, 2, across_x, (1, c, 0), (*across_y, c)), copy(w, 3, across_y, (1, c, 1), (*across_x, c)),
                   copy(w, 4, across_x, (0, c), sibling), copy(w, 5, across_y, (0, c), sibling)] for w in range(nw)]
        last = [[copy(w, 6, diagonal, (1, c, 0), sibling), copy(w, 7, diagonal, (1, c, 1), sibling)]
                for w in range(nw)]

        def start():
            for cp in first:
                cp.start()

        def middle():
            for w in range(nw):
                copy(w, 0, across_x, (0, c), me).wait_recv()
                copy(w, 1, across_y, (0, c), me).wait_recv()
                for cp in passed[w]:
                    cp.start()

        def finish():
            for w in range(nw):
                copy(w, 2, diagonal, (1, c, 0), me).wait_recv()
                copy(w, 3, diagonal, (1, c, 1), me).wait_recv()
                for cp in last[w]:
                    cp.start()
            for w in range(nw):
                for k, block, part in ((4, across_x, (0, 1 - c)), (5, across_y, (0, 1 - c)),
                                       (6, diagonal, (1, 1 - c, 0)), (7, diagonal, (1, 1 - c, 1))):
                    copy(w, k, block, part, me).wait_recv()
            for cp in first + sum(passed, []) + sum(last, []):
                cp.wait_send()

        return start, middle, finish

    return _Comm(shards, [jax.ShapeDtypeStruct((N_CHIPS,) + w.shape, w.dtype) for w in shards],
                 [(nw, 8), (nw, 8)], build)


def _halved(shape, by_cols):
    return shape[:-1] + (shape[-1] // 2,) if by_cols else shape[:-2] + (shape[-2] // 2, shape[-1])


def _swap_comm(gs, by_cols=()):
    nw = len(gs)

    def build(in_refs, out_refs, sems):
        send_sems, recv_sems = sems
        x, y, c, _ = _place()
        cps = []
        for w in range(nw):
            cps.append(pltpu.make_async_remote_copy(
                src_ref=_part(in_refs[w], w in by_cols, 1 - c, lead=slice(None)), dst_ref=out_refs[w],
                send_sem=send_sems.at[w], recv_sem=recv_sems.at[w], device_id=(x, y, 1 - c), device_id_type=MESH))

        def start():
            for cp in cps:
                cp.start()

        def finish():
            for cp in cps:
                cp.wait()

        return start, finish

    return _Comm(gs, [jax.ShapeDtypeStruct(_halved(g.shape, w in by_cols), g.dtype) for w, g in enumerate(gs)],
                 [(nw,), (nw,)], build)


def _exchange_comm(s1s):
    nw = len(s1s)

    def build(in_refs, out_refs, sems):
        send_sems, recv_sems = sems
        x, y, c, chips = _place()
        cps = [pltpu.make_async_remote_copy(
            src_ref=in_refs[w].at[2 * chip[0] + chip[1]], dst_ref=out_refs[w].at[j], send_sem=send_sems.at[w, j],
            recv_sem=recv_sems.at[w, j], device_id=(*chip, c), device_id_type=MESH)
            for w in range(nw) for j, chip in enumerate(chips)]

        def start():
            for cp in cps:
                cp.start()

        def finish():
            for cp in cps:
                cp.wait()

        return start, finish

    return _Comm(s1s, [jax.ShapeDtypeStruct((N_CHIPS - 1,) + s.shape[1:], s.dtype) for s in s1s],
                 [(nw, 3), (nw, 3)], build)


def _size(dims):
    n = 1
    for d in dims:
        n *= d
    return n


def _sem_grids(comm, sem_refs):
    grids, pos = [], 0
    for dims in comm.sem_dims:
        grids.append(_SemGrid(sem_refs[pos:pos + _size(dims)], dims))
        pos += _size(dims)
    return grids


def _comm_split_start(comm, name, after=()):
    c_in, c_out = len(comm.operands), len(comm.out_shape)
    counts = [_size(d) for d in comm.sem_dims]
    n_sem = sum(counts)
    assert not comm.aliases

    def body(*refs):
        srcs, lands = refs[:c_in], refs[c_in:c_in + c_out]
        first_sem = c_in + c_out + len(after)
        start, _ = comm.build(srcs, lands, _sem_grids(comm, refs[first_sem:first_sem + n_sem]))
        start()
        refs[-1][...] = jnp.zeros(refs[-1].shape, refs[-1].dtype)

    lands = [pltpu.with_memory_space_constraint(lax.empty(o.shape, o.dtype), HBM) for o in comm.out_shape]
    srcs = [pltpu.with_memory_space_constraint(a, HBM) for a in comm.operands]
    res = pl.pallas_call(
        body, name=name, in_specs=_hbm_specs(c_in + c_out) + [pl.BlockSpec(memory_space=pl.ANY)] * len(after),
        out_specs=[pl.BlockSpec(memory_space=pltpu.SEMAPHORE)] * n_sem + _hbm_specs(c_in + c_out)
        + [pl.BlockSpec(memory_space=pltpu.VMEM)],
        out_shape=[pltpu.SemaphoreType.DMA(())] * n_sem + [pltpu.HBM(a.shape, a.dtype) for a in comm.operands]
        + [pltpu.HBM(o.shape, o.dtype) for o in comm.out_shape] + [jax.ShapeDtypeStruct((SUBLANES, LANES), F32)],
        input_output_aliases={i: n_sem + i for i in range(c_in + c_out)},
        compiler_params=_params(has_side_effects=pltpu.SideEffectType.DATAFLOW_SIDE_EFFECTING))(*srcs, *lands, *after)
    return res[:-1], res[-1]


def _comm_split_wait(comm, state, after, name):
    c_in, c_out, n_sem = len(comm.operands), len(comm.out_shape), sum(_size(d) for d in comm.sem_dims)
    sems, srcs, lands = state[:n_sem], state[n_sem:n_sem + c_in], state[n_sem + c_in:]

    def body(*refs):
        src_refs, land_refs = refs[:c_in], refs[c_in:c_in + c_out]
        _, finish = comm.build(src_refs, land_refs, _sem_grids(comm, refs[c_in + c_out:c_in + c_out + n_sem]))
        finish()

    sem_spec = pl.BlockSpec(memory_space=pltpu.SEMAPHORE)
    res = pl.pallas_call(
        body, name=name, in_specs=_hbm_specs(c_in + c_out) + [sem_spec] * n_sem + [pl.BlockSpec(memory_space=pl.ANY)],
        out_specs=_hbm_specs(c_in + c_out),
        out_shape=[pltpu.HBM(a.shape, a.dtype) for a in srcs] + [pltpu.HBM(o.shape, o.dtype) for o in lands],
        input_output_aliases={i: i for i in range(c_in + c_out)},
        compiler_params=_params(has_side_effects=pltpu.SideEffectType.DATAFLOW_SIDE_EFFECTING),
    )(*srcs, *lands, *sems, after)
    return res[:c_in], res[c_in:]


def _share_comm(fs, by_cols=()):
    nw = len(fs)

    def build(in_refs, out_refs, sems):
        del in_refs
        send_sems, recv_sems = sems
        x, y, c, _ = _place()

        def copy(w, half):
            part = _part(out_refs[w], w in by_cols, half)
            return pltpu.make_async_remote_copy(
                src_ref=part, dst_ref=part, send_sem=send_sems.at[w], recv_sem=recv_sems.at[w],
                device_id=(x, y, 1 - c), device_id_type=MESH)

        sends = [copy(w, c) for w in range(nw)]

        def start():
            for cp in sends:
                cp.start()

        def finish():
            for w in range(nw):
                copy(w, 1 - c).wait_recv()
            for cp in sends:
                cp.wait_send()

        return start, finish

    return _Comm(fs, [jax.ShapeDtypeStruct(f.shape, f.dtype) for f in fs],
                 [(nw,), (nw,)], build,
                 aliases={w: w for w in range(nw)})


def _add_sibling(g, r1, place, name, by_cols=False):
    nch, h, cols = r1.shape
    tr = _div_tile(h, 1024 if by_cols else 512, 2 * SUBLANES)
    nb = h // tr
    mine = (lambda k, i, p: (k, i, p[0])) if by_cols else (lambda k, i, p: (k, p[0] * nb + i, 0))

    def body(place_ref, g_ref, r_ref, o_ref):
        del place_ref
        o_ref[...] = (g_ref[...].astype(F32) + r_ref[...].astype(F32)).astype(BF16)

    spec = pltpu.PrefetchScalarGridSpec(
        num_scalar_prefetch=1, grid=(nch, nb),
        in_specs=[pl.BlockSpec((None, tr, cols), mine), pl.BlockSpec((None, tr, cols), lambda k, i, p: (k, i, 0))],
        out_specs=pl.BlockSpec((None, tr, cols), lambda k, i, p: (k, i, 0)))
    return pl.pallas_call(body, name=name, grid_spec=spec, out_shape=jax.ShapeDtypeStruct((nch, h, cols), BF16),
                          compiler_params=_params())(place, g, r1)


def _add_chips(s1, r2, place, name, by_cols=False):
    _, h, cols = s1.shape
    tr = _div_tile(h, 1024 if by_cols else 512, 2 * SUBLANES)
    nb = h // tr
    mine = (lambda i, p: (i, p[0])) if by_cols else (lambda i, p: (p[0] * nb + i, 0))
    whole = (h, 2 * cols) if by_cols else (2 * h, cols)

    def body(place_ref, s_ref, r_ref, o_ref):
        del place_ref
        acc = s_ref[...].astype(F32)
        for j in range(N_CHIPS - 1):
            acc = acc + r_ref[j].astype(F32)
        o_ref[...] = acc

    spec = pltpu.PrefetchScalarGridSpec(
        num_scalar_prefetch=1, grid=(nb,),
        in_specs=[pl.BlockSpec((None, tr, cols), lambda i, p: (p[1], i, 0)),
                  pl.BlockSpec((N_CHIPS - 1, tr, cols), lambda i, p: (0, i, 0))],
        out_specs=pl.BlockSpec((tr, cols), mine))
    return pl.pallas_call(body, name=name, grid_spec=spec, out_shape=jax.ShapeDtypeStruct(whole, F32),
                          compiler_params=_params())(place, s1, r2)


def _quarter_turn(m):
    h = m.shape[-1] // 2
    return jnp.concatenate([-m[..., h:], m[..., :h]], axis=-1)


def _quarter_turn_back(m):
    h = m.shape[-1] // 2
    return jnp.concatenate([m[..., h:], -m[..., :h]], axis=-1)


def _stack_rows(parts):
    out = lax.empty((sum(p.shape[0] for p in parts),) + parts[0].shape[1:], parts[0].dtype)
    row = 0
    for p in parts:
        out = lax.dynamic_update_slice(out, p, (row, 0))
        row += p.shape[0]
    return out


def _join_cols(sh):
    return jnp.concatenate([sh[k] for k in range(N_CHIPS)], axis=1)


def _split_cols(full):
    c = full.shape[1] // N_CHIPS
    return jnp.stack([full[:, k * c:(k + 1) * c] for k in range(N_CHIPS)])


def kernel(x, c, positions, w_ada, b_ada, pre_norm1_g, w_in, gm_ln_g, gm_ln_b, gm_w_s, gm_b_s, w_branch_a, q_norm_g, w_uq, kv_norm_g, w_ukv, w_branch_b, w_out, post_norm1_g, pre_norm2_g, w_up, conv_w, conv_b, w_down, post_norm2_g, loss_target, m_w_ada, m_b_ada, m_pre_norm1_g, m_w_in, m_gm_ln_g, m_gm_ln_b, m_gm_w_s, m_gm_b_s, m_w_branch_a, m_q_norm_g, m_w_uq, m_kv_norm_g, m_w_ukv, m_w_branch_b, m_w_out, m_post_norm1_g, m_pre_norm2_g, m_w_up, m_conv_w, m_conv_b, m_w_down, m_post_norm2_g, v_w_ada, v_b_ada, v_pre_norm1_g, v_w_in, v_gm_ln_g, v_gm_ln_b, v_gm_w_s, v_gm_b_s, v_w_branch_a, v_q_norm_g, v_w_uq, v_kv_norm_g, v_w_ukv, v_w_branch_b, v_w_out, v_post_norm1_g, v_pre_norm2_g, v_w_up, v_conv_w, v_conv_b, v_w_down, v_post_norm2_g):
    given = dict(locals())
    s, d = x.shape[1], x.shape[2]
    gw = gm_ln_g.shape[0]
    ql, kvl = q_norm_g.shape[0], kv_norm_g.shape[0]
    heads = N_CHIPS * w_uq.shape[1] // (NOPE + ROPE)
    ff = N_CHIPS * w_down.shape[0]
    assert gw == d and N_CHIPS * w_ukv.shape[1] == heads * (NOPE + VHEAD)
    ix, iy, ic = lax.axis_index("x"), lax.axis_index("y"), lax.axis_index("c")
    chip = 2 * ix + iy
    dev = 2 * chip + ic
    row = lambda v: v.reshape(1, -1)

    first = _all_gather(jnp.concatenate([jnp.pad(c, ((0, SUBLANES - 1), (0, 0))),
                                         jnp.pad(conv_w, ((0, SUBLANES - CONV_TAPS), (0, 0)))], axis=1), "gather_c")
    first = first.reshape(N_DEV, SUBLANES, d + conv_w.shape[1])
    c_all = first[:, 0, :d]
    conv_wf = first[::N_CORES, :CONV_TAPS, d:].transpose(1, 0, 2).reshape(CONV_TAPS, N_CHIPS * conv_w.shape[1])
    na = w_ada.shape[1]
    b_ada_mine = lax.dynamic_slice(b_ada, (chip * na,), (na,))
    mod_cols = _ada_fwd(c_all, w_ada, row(b_ada_mine), "ada_fwd")
    mod_all = _all_gather(mod_cols, "gather_mod").reshape(N_CHIPS, N_CORES, N_DEV, na)[:, 0]
    mod = lax.dynamic_index_in_dim(mod_all, dev, axis=1, keepdims=False).reshape(N_MOD, d)
    shift1, scale1, gate1, shift2, scale2, gate2 = (mod[i:i + 1] for i in range(N_MOD))

    mine = {n: (given[n].T if n == "w_in" else given[n]).astype(BF16) for n in BIG}
    gather = lambda names: _gather_comm([mine[n] for n in names], [i for i, n in enumerate(names) if n == "w_in"])
    whole = lambda n, g: lax.dynamic_update_slice(g, mine[n][None], (chip, 0, 0))
    rows4 = lambda sh4: sh4.reshape(-1, sh4.shape[2])
    wi_t = rows4(whole("w_in", _run_comm(gather(["w_in"]), "gather_w_in")[0]))
    o_q, o_kv, o_pe, o_ga = 2 * gw, 2 * gw + ql, 2 * gw + ql + kvl, 2 * gw + ql + kvl + ROPE
    w_in_big_t = _stack_rows([wi_t[:o_q], wi_t[o_ga:]])
    w_in_lat_t = _stack_rows([wi_t[o_q:o_ga], _quarter_turn(wi_t[o_pe:o_ga].T).T])

    inv = ROPE_THETA ** (-jnp.arange(0, ROPE, 2, dtype=F32) / ROPE)
    ang = positions[0].astype(F32)[:, None] * inv
    cos, sin = jnp.cos(ang), jnp.sin(ang)
    rope_k = jnp.concatenate([cos, cos, sin, sin], axis=1)
    softmax_scale = float(NOPE + ROPE) ** -0.5
    rope_q = jnp.concatenate([jnp.ones((s, NOPE), F32), rope_k], axis=1) * softmax_scale

    x2d, tgt = x[0], loss_target[0]
    g_pre1, g_post1, g_pre2, g_post2 = row(pre_norm1_g), row(post_norm1_g), row(pre_norm2_g), row(post_norm2_g)
    ln_g, ln_b, q_g, kv_g = row(gm_ln_g), row(gm_ln_b), row(q_norm_g), row(kv_norm_g)
    b_s_t = gm_b_s.T
    conv_bf = row(conv_b)

    h1 = _prenorm(x2d, g_pre1, scale1, shift1, "prenorm1")
    z_big, (g_uq, g_ukv, g_a) = _matmul(h1, w_in_big_t, mode="nt", out_dtype=BF16, name="mm_z_big", tm=s,
                                        comm=gather(["w_uq", "w_ukv", "w_branch_a"]))
    wq = _join_cols(whole("w_uq", g_uq)).reshape(ql, heads, NOPE + ROPE)
    w_q = jnp.concatenate([wq, _quarter_turn(wq[:, :, NOPE:])], axis=2).reshape(ql, heads * HEAD_W)
    w_kv = _join_cols(whole("w_ukv", g_ukv)).reshape(kvl, heads, 2, NOPE).transpose(0, 2, 1, 3)
    w_kv = w_kv.reshape(kvl, 2 * heads * NOPE)
    w_a = rows4(whole("w_branch_a", g_a))
    z_lat = _matmul(h1, w_in_lat_t, mode="nt", out_dtype=F32, name="mm_z_lat", tm=s, tn=1024)
    a_act = _gmlp_fwd(z_big, ln_g, ln_b, gm_w_s, b_s_t, "gmlp_fwd")
    qn, kvn, kr = _mla_prep(z_lat, q_g, kv_g, rope_k, "mla_prep")
    q_rot = _matmul(qn, w_q, mode="nn", out_dtype=BF16, name="mm_q", tm=s, tn=HEAD_W, mul=rope_q)
    kv_all = _matmul(kvn, w_kv, mode="nn", out_dtype=BF16, name="mm_kv", tm=s, tn=1024)
    (o_att, lse), (g_b, g_o, g_up) = _attn_fwd(q_rot, kv_all, kr, heads, "attn_fwd",
                                               comm=gather(["w_branch_b", "w_out", "w_up"]))
    w_b, w_o, w_upf = rows4(whole("w_branch_b", g_b)), rows4(whole("w_out", g_o)), whole("w_up", g_up)
    y_a = _matmul(a_act, w_a, mode="nn", out_dtype=BF16, name="mm_y_a", tm=s)
    y_b = _matmul(o_att, w_b, mode="nn", out_dtype=BF16, name="mm_y_b", tm=s)
    merged = _merge(z_big, y_a, y_b, "merge")
    y1 = _matmul(merged, w_o, mode="nn", out_dtype=F32, name="mm_y1", tm=s)
    x1, h2 = _post_pre(x2d, y1, gate1, g_post1, g_pre2, scale2, shift2, "post1_pre2")

    up_pre, (g_dn,) = _matmul(h2, w_upf, mode="nn", out_dtype=BF16, name="mm_up", tm=s, tn=1408,
                              comm=gather(["w_down"]))
    w_dn = rows4(whole("w_down", g_dn))
    act = _conv_fwd(up_pre, conv_wf, conv_bf, "conv_fwd")
    ffn = _matmul(act, w_dn, mode="nn", out_dtype=F32, name="mm_ffn", tm=s, tn=1024, tk=1408)

    dffn, dgate2, g_post2_grad, dx2, loss_part = _post_bwd(ffn, gate2, g_post2, "post2_bwd", xin=x1, target=tgt)
    loss = lax.psum(loss_part[0, 0], ("x", "y", "c"))
    place = jnp.stack([ic, chip]).astype(jnp.int32)
    rows_of = lambda g: g.reshape(N_CHIPS, g.shape[0] // N_CHIPS, g.shape[1])
    add_sibling = lambda names, gs, r1s: [_add_sibling(g, r1, place, "rs_add_sibling_" + n, by_cols=n == "w_in")
                                          for n, g, r1 in zip(names, gs, r1s)]
    add_chips = lambda names, s1s, r2s: [_add_chips(s1, r2, place, "rs_add_chips_" + n, by_cols=n == "w_in")
                                         for n, s1, r2 in zip(names, s1s, r2s)]
    gp_down = [rows_of(_matmul(act, dffn, mode="tn", out_dtype=BF16, name="mm_gw_down", tn=2048, tk=s))]
    dact, r1_down = _matmul(dffn, w_dn, mode="nt", out_dtype=BF16, name="mm_dact", tm=s, comm=_swap_comm(gp_down))
    s1_down = add_sibling(["w_down"], gp_down, r1_down)
    (dup, gcw_g, gcw_v, gcb_g, gcb_v), r2_down = _conv_bwd(up_pre, dact, conv_wf, conv_bf, "conv_bwd",
                                                            comm=_exchange_comm(s1_down))
    half_down = add_chips(["w_down"], s1_down, r2_down)
    dh2 = _matmul(dup, w_upf, mode="nt", out_dtype=F32, name="mm_dh2", tm=s, tn=1024, tk=1408)
    gw_up = _matmul(h2, dup, mode="tn", out_dtype=BF16, name="mm_gw_up", tm=1024, tn=1408, tk=s, out_groups=N_CHIPS)
    dx1, dshift2, dscale2, g_pre2_grad = _prenorm_bwd(x1, dh2, dx2, g_pre2, scale2, "prenorm2_bwd")

    dy1, dgate1, g_post1_grad = _post_bwd(y1, gate1, g_post1, "post1_bwd", dxo=dx1)
    dmerged = _matmul(dy1, w_o, mode="nt", out_dtype=BF16, name="mm_dmerged", tm=s)
    gw_out = _matmul(merged, dy1, mode="tn", out_dtype=BF16, name="mm_gw_out", tn=1024, tk=s)
    dy_a, dy_b, dz_big = _merge_bwd(dmerged, z_big, y_a, y_b, "merge_bwd")
    da = _matmul(dy_a, w_a, mode="nt", out_dtype=BF16, name="mm_da", tm=s)
    gw_a = _matmul(a_act, dy_a, mode="tn", out_dtype=BF16, name="mm_gw_a", tn=1024, tk=s)
    do = _matmul(dy_b, w_b, mode="nt", out_dtype=BF16, name="mm_do", tm=s)
    gw_b = _matmul(o_att, dy_b, mode="tn", out_dtype=BF16, name="mm_gw_b", tn=1024, tk=s)
    mid = ["w_up", "w_out", "w_branch_a", "w_branch_b"]
    gp_mid = [gw_up, rows_of(gw_out), rows_of(gw_a), rows_of(gw_b)]
    (dz_big, g_ws, g_bs_t, g_ln_g, g_ln_b), r1_mid = _gmlp_bwd(z_big, da, dz_big, ln_g, ln_b, gm_w_s, b_s_t,
                                                                "gmlp_bwd", comm=_swap_comm(gp_mid))
    s1_mid = add_sibling(mid, gp_mid, r1_mid)
    (dq, dk, dv), r2_up_out = _attn_bwd(q_rot, kv_all, kr, o_att, do, lse, heads, "attn_bwd",
                                        comm=_exchange_comm(s1_mid[:2]))
    dq_big, dkv, dkk = _mla_bwd_mid(dq, dk, dv, rope_q, rope_k, heads, "mla_bwd_mid")
    gw_q = _matmul(qn, dq_big, mode="tn", out_dtype=F32, name="mm_gw_q", tn=1024, tk=s)
    dqn = _matmul(dq_big, w_q, mode="nt", out_dtype=F32, name="mm_dqn", tm=s, tk=1024)
    gw_kv = _matmul(kvn, dkv, mode="tn", out_dtype=BF16, name="mm_gw_kv", tn=1024, tk=s)
    dkvn = _matmul(dkv, w_kv, mode="nt", out_dtype=F32, name="mm_dkvn", tm=s, tk=1024)
    dz_lat, g_q, g_kv = _mla_bwd_post(z_lat, dqn, dkvn, dkk, q_g, kv_g, "mla_bwd_post")

    partial = {
        "gm_ln_g": g_ln_g, "gm_ln_b": g_ln_b, "gm_w_s": g_ws, "gm_b_s": g_bs_t[:, :gm_b_s.shape[0]].T,
        "q_norm_g": g_q, "kv_norm_g": g_kv, "post_norm1_g": g_post1_grad, "pre_norm2_g": g_pre2_grad,
        "conv_w": jnp.concatenate([gcw_g, gcw_v], axis=1), "conv_b": jnp.concatenate([gcb_g, gcb_v], axis=1),
        "post_norm2_g": g_post2_grad,
    }
    flat = jnp.concatenate([partial[n].reshape(-1) for n in SMALL_PARTIAL])
    n_small = flat.shape[0]
    rows_small = -(-n_small // (LANES * SMALL_ROW_TILE)) * SMALL_ROW_TILE
    flat = jnp.pad(flat, (0, rows_small * LANES - n_small)).reshape(rows_small, LANES)

    def small_pack(prefix, source):
        v = jnp.concatenate([source[prefix + n].reshape(-1) for n in SMALL])
        rows = -(-v.shape[0] // (LANES * SUBLANES)) * SUBLANES
        return jnp.pad(v, (0, rows * LANES - v.shape[0])).reshape(rows, LANES)

    small_state = [small_pack(prefix, given) for prefix in ("", "m_", "v_")]

    dh1, r2_a_b = _matmul(dz_big, w_in_big_t, mode="nn", out_dtype=F32, name="mm_dh1_big", tm=s, tk=1024,
                          comm=_exchange_comm(s1_mid[2:]))
    half_mid = add_chips(mid, s1_mid, list(r2_up_out) + list(r2_a_b))
    dh1 = _matmul(dz_lat, w_in_lat_t, mode="nn", out_dtype=F32, name="mm_dh1_lat", tm=s, tk=1024, add=dh1)
    gw_big_t, hosted = _matmul(dz_big, h1, mode="tn", out_dtype=BF16, name="mm_gw_in_big", tn=2048, tk=s,
                               comm=_join_comms([_share_comm(half_down + half_mid), _gather8_comm(flat)]))
    shared, small_all = hosted[:-1], lax.dynamic_update_slice(hosted[-1], flat[None], (dev, 0, 0))
    small_sum = _sum_leading(small_all, "sum_small", after=small_state + [loss.reshape(1, 1)]).reshape(-1)
    small_grads, off = {}, 0
    for n in SMALL_PARTIAL:
        shape = (CONV_TAPS, 2 * ff) if n == "conv_w" else given[n].shape
        small_grads[n] = small_sum[off:off + partial[n].size].reshape(shape)
        off += partial[n].size
    small_grads["conv_w"] = lax.dynamic_slice(small_grads["conv_w"], (0, chip * conv_w.shape[1]), conv_w.shape)
    grads = dict(zip(["w_down"] + mid, shared), **small_grads)
    gw_lat_t = _matmul(dz_lat, h1, mode="tn", out_dtype=F32, name="mm_gw_in_lat", tm=1024, tn=1024, tk=s)

    gq = gw_q.reshape(ql, heads, HEAD_W)
    gq_pe = gq[:, :, NOPE:NOPE + ROPE] + _quarter_turn_back(gq[:, :, NOPE + ROPE:])
    g_pe_t = gw_lat_t[ql + kvl:ql + kvl + ROPE] + _quarter_turn_back(gw_lat_t[ql + kvl + ROPE:].T).T
    last = ["w_in", "w_uq", "w_ukv"]
    gw_in_t = _stack_rows([gw_big_t[:o_q], gw_lat_t[:ql + kvl].astype(BF16), g_pe_t.astype(BF16), gw_big_t[o_q:]])
    gp_last = [
        gw_in_t.reshape(N_CHIPS, gw_in_t.shape[0] // N_CHIPS, d),
        _split_cols(jnp.concatenate([gq[:, :, :NOPE], gq_pe], axis=2).reshape(ql, heads * (NOPE + ROPE)).astype(BF16)),
        _split_cols(gw_kv.reshape(kvl, 2, heads, NOPE).transpose(0, 2, 1, 3).reshape(kvl, heads * 2 * NOPE)),
    ]
    (grad_x, dshift1, dscale1, g_pre1_grad), r1_last = _prenorm_bwd(x2d, dh1, dx1, g_pre1, scale1, "prenorm1_bwd",
                                                                    comm=_swap_comm(gp_last, by_cols=[0]))
    s1_last = add_sibling(last, gp_last, r1_last)

    dmod = jnp.concatenate([dshift1, dscale1, dgate1, dshift2, dscale2, dgate2, g_pre1_grad], axis=1)
    dmod_all = _all_gather(jnp.pad(dmod, ((0, SUBLANES - 1), (0, 0))), "gather_dmod")
    dmod_all = dmod_all.reshape(N_DEV, SUBLANES, (N_MOD + 1) * d)[:, 0]
    dmod_sum = _sum_leading(dmod_all.reshape(N_DEV, 1, (N_MOD + 1) * d), "sum_dmod")[0]
    grads["b_ada"], grads["pre_norm1_g"] = dmod_sum[:N_MOD * d], dmod_sum[N_MOD * d:]
    dmod_mine = lax.dynamic_slice(dmod_all, (0, chip * na), (N_DEV, na))
    grads["w_ada"] = _ada_bwd(c_all.T, dmod_mine, "ada_bwd")

    delta, new_m, new_v = {}, {}, {}

    def adamw(n, after=None):
        turn = (lambda a: a.T) if n == "w_in" else (lambda a: a)
        outs = _adamw(turn(given[n]), grads[n], turn(given["m_" + n]), turn(given["v_" + n]), "adamw_" + n,
                      after=after)
        grads[n] = turn(grads[n])
        delta[n], new_m[n], new_v[n] = (turn(o) for o in outs)

    exchange_last = _exchange_comm(s1_last)
    in_flight, token = _comm_split_start(exchange_last, "rs_exchange_last_start", after=[dmod_sum, small_sum])
    for n in ["w_ada", "w_down"] + mid:
        adamw(n, after=token)
    s1_last, r2_last = _comm_split_wait(exchange_last, in_flight, delta[mid[-1]], "rs_exchange_last_wait")
    half_last = add_chips(last, s1_last, r2_last)
    grads.update(zip(last, _run_comm(_share_comm(half_last, by_cols=[0]), "rs_share_last")))
    for n in last:
        adamw(n)

    outs = _adamw(small_state[0], small_pack("", grads), small_state[1], small_state[2], "adamw_small")
    off = 0
    for n in SMALL:
        size = given[n].size
        for store, packed_out in zip((delta, new_m, new_v), outs):
            store[n] = packed_out.reshape(-1)[off:off + size].reshape(given[n].shape)
        off += size

    return (loss, grad_x[None], *[grads[n] for n in WEIGHTS], *[delta[n] for n in WEIGHTS],
            *[new_m[n] for n in WEIGHTS], *[new_v[n] for n in WEIGHTS])
```

```python
import functools

import jax
import jax.numpy as jnp
from jax import lax
from jax.experimental import pallas as pl
from jax.experimental.pallas import tpu as pltpu

F32 = jnp.float32
BF16 = jnp.bfloat16
MESH = pl.DeviceIdType.MESH
HBM = pltpu.HBM

EPS = 1e-6
NOPE, ROPE, VHEAD = 128, 64, 128
HEAD_W = NOPE + 2 * ROPE
ROPE_THETA = 10000.0
CONV_TAPS = 3
N_MOD = 6
N_CHIPS, N_CORES, N_DEV = 4, 2, 8
ADAM_LR, ADAM_B1, ADAM_B2, ADAM_EPS, ADAM_WD, ADAM_STEP = 0.001, 0.9, 0.999, 1e-08, 0.01, 10

LANES = 128
SUBLANES = 8
VMEM_LIMIT = 56 * 2**20
MIDDLE_STAGE_AT = 70
SMALL_ROW_TILE = 256

BIG = ("w_in", "w_branch_a", "w_uq", "w_ukv", "w_branch_b", "w_out", "w_up", "w_down")
WEIGHTS = ("w_ada", "b_ada", "pre_norm1_g", "w_in", "gm_ln_g", "gm_ln_b", "gm_w_s", "gm_b_s", "w_branch_a",
           "q_norm_g", "w_uq", "kv_norm_g", "w_ukv", "w_branch_b", "w_out", "post_norm1_g", "pre_norm2_g",
           "w_up", "conv_w", "conv_b", "w_down", "post_norm2_g")
SMALL_PARTIAL = ("gm_ln_g", "gm_ln_b", "gm_w_s", "gm_b_s", "q_norm_g", "kv_norm_g", "post_norm1_g",
                 "pre_norm2_g", "conv_w", "conv_b", "post_norm2_g")
SMALL = ("b_ada", "pre_norm1_g") + SMALL_PARTIAL


def _div_tile(n, cap, mult=LANES):
    t = (min(cap, n) // mult) * mult
    while t >= mult:
        if n % t == 0:
            return t
        t -= mult
    return n


def _params(**kw):
    return pltpu.CompilerParams(vmem_limit_bytes=VMEM_LIMIT, **kw)


def _row_spec(width):
    return pl.BlockSpec((1, width), lambda *_: (0, 0))


def _gelu(x):
    k = 0.7978845608028654
    return 0.5 * x * (1.0 + jnp.tanh(k * (x + 0.044715 * x * x * x)))


def _gelu_grad(x):
    k = 0.7978845608028654
    t = jnp.tanh(k * (x + 0.044715 * x * x * x))
    return 0.5 * (1.0 + t) + 0.5 * x * (1.0 - t * t) * k * (1.0 + 3.0 * 0.044715 * x * x)


def _sigmoid(x):
    return 0.5 * jnp.tanh(0.5 * x) + 0.5


def _dot(a, b, dims):
    return lax.dot_general(a, b, (dims, ((), ())), preferred_element_type=F32)


NN = ((1,), (0,))
NT = ((1,), (1,))
TN = ((0,), (0,))


def _logical(arr):
    if arr.ndim == 2:
        return arr.shape[0], arr.shape[1], arr.shape[1]
    return arr.shape[1], arr.shape[0] * arr.shape[2], arr.shape[2]


def _tile_spec(ndim, group_w, blk_rows, blk_cols, row_of, col_of):
    if ndim == 2:
        return pl.BlockSpec((blk_rows, blk_cols), lambda i, j, k: (row_of(i, j, k), col_of(i, j, k)))
    per = group_w // blk_cols
    return pl.BlockSpec((None, blk_rows, blk_cols),
                        lambda i, j, k: (col_of(i, j, k) // per, row_of(i, j, k), col_of(i, j, k) % per))


def _matmul(a, b, *, mode, out_dtype, name, tm=512, tn=512, tk=2048, mul=None, add=None, out_groups=None, comm=None):
    ar, ac, agw = _logical(a)
    br, bc, bgw = _logical(b)
    if mode == "nn":
        m, kd, n = ar, ac, bc
        m_w, k_w, n_w = (), (agw,), (bgw,)
    elif mode == "nt":
        m, kd, n = ar, ac, br
        m_w, k_w, n_w = (), (agw, bgw), ()
    else:
        m, kd, n = ac, ar, bc
        m_w, k_w, n_w = (agw,), (), (bgw,)
    if out_groups is not None:
        n_w = n_w + (n // out_groups,)
    tm = _div_tile(min((m,) + m_w), tm, LANES if mode == "tn" else SUBLANES)
    tn = _div_tile(min((n,) + n_w), tn)
    tk = _div_tile(min((kd,) + k_w), tk)
    assert all(w % tn == 0 for w in n_w) and all(w % tk == 0 for w in k_w) and all(w % tm == 0 for w in m_w)
    nk = kd // tk
    dims = {"nn": NN, "nt": NT, "tn": TN}[mode]
    gi, gj, gk = (lambda i, j, k: i), (lambda i, j, k: j), (lambda i, j, k: k)
    if mode == "nn":
        a_spec = _tile_spec(a.ndim, agw, tm, tk, gi, gk)
        b_spec = _tile_spec(b.ndim, bgw, tk, tn, gk, gj)
    elif mode == "nt":
        a_spec = _tile_spec(a.ndim, agw, tm, tk, gi, gk)
        b_spec = _tile_spec(b.ndim, bgw, tn, tk, gj, gk)
    else:
        a_spec = _tile_spec(a.ndim, agw, tk, tm, gk, gi)
        b_spec = _tile_spec(b.ndim, bgw, tk, tn, gk, gj)
    in_specs, operands = [a_spec, b_spec], [a, b]
    if mul is not None:
        assert mul.shape == (m, tn)
        in_specs.append(pl.BlockSpec((tm, tn), lambda i, j, k: (i, 0)))
        operands.append(mul)
    if add is not None:
        in_specs.append(pl.BlockSpec((tm, tn), lambda i, j, k: (i, j)))
        operands.append(add)

    def body(*refs):
        a_ref, b_ref = refs[0], refs[1]
        pos = 2
        mul_ref = add_ref = None
        if mul is not None:
            mul_ref, pos = refs[pos], pos + 1
        if add is not None:
            add_ref, pos = refs[pos], pos + 1
        o_ref = refs[pos]

        def finish(r):
            if mul_ref is not None:
                r = r * mul_ref[...]
            if add_ref is not None:
                r = r + add_ref[...]
            o_ref[...] = r.astype(out_dtype)

        part = _dot(a_ref[...], b_ref[...], dims)
        if nk == 1:
            finish(part)
        else:
            acc_ref = refs[pos + 1]
            k = pl.program_id(2)

            @pl.when(k == 0)
            def _():
                acc_ref[...] = part

            @pl.when(k > 0)
            def _():
                acc_ref[...] += part

            @pl.when(k == nk - 1)
            def _():
                finish(acc_ref[...])

    if out_groups is None:
        out_spec, out_dims = _tile_spec(2, n, tm, tn, gi, gj), (m, n)
    else:
        out_spec, out_dims = _tile_spec(3, n // out_groups, tm, tn, gi, gj), (out_groups, m, n // out_groups)
    return _call(body, operands, comm, name=name, grid=(m // tm, n // tn, nk), in_specs=in_specs, out_specs=out_spec,
                 out_shape=jax.ShapeDtypeStruct(out_dims, out_dtype),
                 scratch_shapes=[] if nk == 1 else [pltpu.VMEM((tm, tn), F32)])


def _accumulate(ref, value):
    @pl.when(pl.program_id(0) == 0)
    def _():
        ref[...] = value

    @pl.when(pl.program_id(0) > 0)
    def _():
        ref[...] += value


def _colsum(v):
    return jnp.sum(v, axis=0, keepdims=True)


def _rowmean(v):
    return jnp.mean(v, axis=-1, keepdims=True)


def _prenorm(x, g, scale, shift, name):
    s, d = x.shape
    tb = _div_tile(s, 256, SUBLANES)

    def body(x_ref, g_ref, sc_ref, sh_ref, h_ref):
        xv = x_ref[...]
        r = lax.rsqrt(_rowmean(xv * xv) + EPS)
        h_ref[...] = ((xv * r) * g_ref[...] * (1.0 + sc_ref[...]) + sh_ref[...]).astype(BF16)

    blk = pl.BlockSpec((tb, d), lambda i: (i, 0))
    return pl.pallas_call(
        body, name=name, grid=(s // tb,), in_specs=[blk, _row_spec(d), _row_spec(d), _row_spec(d)],
        out_specs=blk, out_shape=jax.ShapeDtypeStruct((s, d), BF16), compiler_params=_params(),
    )(x, g, scale, shift)


def _post_pre(x, y, gate, pg, g2, scale2, shift2, name):
    s, d = x.shape
    tb = _div_tile(s, 256, SUBLANES)

    def body(x_ref, y_ref, gate_ref, pg_ref, g2_ref, sc_ref, sh_ref, x1_ref, h2_ref):
        yv = y_ref[...]
        rp = lax.rsqrt(_rowmean(yv * yv) + EPS)
        x1 = x_ref[...] + gate_ref[...] * ((yv * rp) * pg_ref[...])
        x1_ref[...] = x1
        r2 = lax.rsqrt(_rowmean(x1 * x1) + EPS)
        h2_ref[...] = ((x1 * r2) * g2_ref[...] * (1.0 + sc_ref[...]) + sh_ref[...]).astype(BF16)

    blk = pl.BlockSpec((tb, d), lambda i: (i, 0))
    return pl.pallas_call(
        body, name=name, grid=(s // tb,), in_specs=[blk, blk] + [_row_spec(d)] * 5,
        out_specs=[blk, blk],
        out_shape=[jax.ShapeDtypeStruct((s, d), F32), jax.ShapeDtypeStruct((s, d), BF16)],
        compiler_params=_params(),
    )(x, y, gate, pg, g2, scale2, shift2)


def _post_bwd(y, gate, pg, name, *, dxo=None, xin=None, target=None):
    s, d = y.shape
    tb = _div_tile(s, 256, SUBLANES)
    from_loss = target is not None

    def body(*refs):
        if from_loss:
            y_ref, gate_ref, pg_ref, xin_ref, t_ref, dy_ref, dgate_ref, dpg_ref, dxo_ref, loss_ref = refs
        else:
            y_ref, gate_ref, pg_ref, dxo_in_ref, dy_ref, dgate_ref, dpg_ref = refs
        yv = y_ref[...]
        rp = lax.rsqrt(_rowmean(yv * yv) + EPS)
        yh = yv * rp
        fn = yh * pg_ref[...]
        gate = gate_ref[...]
        if from_loss:
            err = xin_ref[...] + gate * fn - t_ref[...]
            dxo = err * (1.0 / d)
            dxo_ref[...] = dxo
            part = 0.5 * jnp.sum(_rowmean(err * err), axis=0, keepdims=True)
            _accumulate(loss_ref, jnp.broadcast_to(part, loss_ref.shape))
        else:
            dxo = dxo_in_ref[...]
        _accumulate(dgate_ref, _colsum(dxo * fn))
        dfn = dxo * gate
        _accumulate(dpg_ref, _colsum(dfn * yh))
        dyh = dfn * pg_ref[...]
        dy_ref[...] = (rp * (dyh - yh * _rowmean(dyh * yh))).astype(BF16)

    blk = pl.BlockSpec((tb, d), lambda i: (i, 0))
    in_specs = [blk, _row_spec(d), _row_spec(d)]
    out_specs = [blk, _row_spec(d), _row_spec(d)]
    out_shape = [jax.ShapeDtypeStruct((s, d), BF16), jax.ShapeDtypeStruct((1, d), F32),
                 jax.ShapeDtypeStruct((1, d), F32)]
    if from_loss:
        operands = (y, gate, pg, xin, target)
        in_specs += [blk, blk]
        out_specs += [blk, _row_spec(LANES)]
        out_shape += [jax.ShapeDtypeStruct((s, d), F32), jax.ShapeDtypeStruct((1, LANES), F32)]
    else:
        operands = (y, gate, pg, dxo)
        in_specs += [blk]
    return pl.pallas_call(
        body, name=name, grid=(s // tb,), in_specs=in_specs, out_specs=out_specs, out_shape=out_shape,
        compiler_params=_params(),
    )(*operands)


def _prenorm_bwd(xin, dh, dres, g, scale, name, comm=None):
    s, d = xin.shape
    tb = _div_tile(s, 256, SUBLANES)

    def body(x_ref, dh_ref, dres_ref, g_ref, sc_ref, dx_ref, dshift_ref, dscale_ref, dg_ref):
        xv = x_ref[...]
        r = lax.rsqrt(_rowmean(xv * xv) + EPS)
        xn = xv * r
        dh = dh_ref[...]
        g1 = g_ref[...]
        s1 = 1.0 + sc_ref[...]
        _accumulate(dshift_ref, _colsum(dh))
        _accumulate(dscale_ref, _colsum(dh * xn * g1))
        _accumulate(dg_ref, _colsum(dh * xn * s1))
        dxn = dh * g1 * s1
        dx_ref[...] = dres_ref[...] + r * (dxn - xn * _rowmean(dxn * xn))

    blk = pl.BlockSpec((tb, d), lambda i: (i, 0))
    return _call(
        body, (xin, dh, dres, g, scale), comm, name=name, grid=(s // tb,),
        in_specs=[blk, blk, blk, _row_spec(d), _row_spec(d)],
        out_specs=[blk, _row_spec(d), _row_spec(d), _row_spec(d)],
        out_shape=[jax.ShapeDtypeStruct((s, d), F32)] + [jax.ShapeDtypeStruct((1, d), F32)] * 3)


def _merge(z_big, y_a, y_b, name):
    s, d = y_a.shape
    tb = _div_tile(s, 256, SUBLANES)

    def body(zg_ref, ya_ref, yb_ref, o_ref):
        ga, gb = zg_ref[:, :d].astype(F32), zg_ref[:, d:].astype(F32)
        o_ref[...] = (_sigmoid(ga) * ya_ref[...].astype(F32) + _sigmoid(gb) * yb_ref[...].astype(F32)).astype(BF16)

    blk = pl.BlockSpec((tb, d), lambda i: (i, 0))
    return pl.pallas_call(
        body, name=name, grid=(s // tb,), in_specs=[pl.BlockSpec((tb, 2 * d), lambda i: (i, 1)), blk, blk],
        out_specs=blk, out_shape=jax.ShapeDtypeStruct((s, d), BF16), compiler_params=_params(),
    )(z_big, y_a, y_b)


def _merge_bwd(dmerged, z_big, y_a, y_b, name):
    s, d = y_a.shape
    tb = _div_tile(s, 256, SUBLANES)

    def body(dm_ref, zg_ref, ya_ref, yb_ref, dya_ref, dyb_ref, dz_ref):
        dm = dm_ref[...].astype(F32)
        sa, sb = _sigmoid(zg_ref[:, :d].astype(F32)), _sigmoid(zg_ref[:, d:].astype(F32))
        dya_ref[...] = (dm * sa).astype(BF16)
        dyb_ref[...] = (dm * sb).astype(BF16)
        dz_ref[:, :d] = (dm * ya_ref[...].astype(F32) * sa * (1.0 - sa)).astype(BF16)
        dz_ref[:, d:] = (dm * yb_ref[...].astype(F32) * sb * (1.0 - sb)).astype(BF16)

    blk = pl.BlockSpec((tb, d), lambda i: (i, 0))
    wide = pl.BlockSpec((tb, 2 * d), lambda i: (i, 1))
    return pl.pallas_call(
        body, name=name, grid=(s // tb,), in_specs=[blk, wide, blk, blk], out_specs=[blk, blk, wide],
        out_shape=[jax.ShapeDtypeStruct((s, d), BF16), jax.ShapeDtypeStruct((s, d), BF16),
                   jax.ShapeDtypeStruct((s, 4 * d), BF16)],
        compiler_params=_params(),
    )(dmerged, z_big, y_a, y_b)


def _causal_mask(ch):
    q = lax.broadcasted_iota(jnp.int32, (ch, ch), 0)
    p = lax.broadcasted_iota(jnp.int32, (ch, ch), 1)
    return (p <= q).astype(F32)


def _gmlp_norm(zc, lng, lnb, gw):
    u_pre, v_pre = zc[:, :gw], zc[:, gw:]
    vg = _gelu(v_pre)
    mu = _rowmean(vg)
    cen = vg - mu
    rstd = lax.rsqrt(_rowmean(cen * cen) + EPS)
    vhat = cen * rstd
    return u_pre, v_pre, _gelu(u_pre), vhat, rstd, vhat * lng + lnb


def _gmlp_fwd(z_big, ln_g, ln_b, w_s, b_s_t, name):
    s = z_big.shape[0]
    groups, ch, _ = w_s.shape
    gw = ln_g.shape[1]
    gd = gw // groups

    def body(z_ref, lng_ref, lnb_ref, ws_ref, bt_ref, a_ref):
        _, _, u, _, _, vn = _gmlp_norm(z_ref[...].astype(F32), lng_ref[...], lnb_ref[...], gw)
        mask = _causal_mask(ch)
        for g in range(groups):
            cols = slice(g * gd, (g + 1) * gd)
            wm = (ws_ref[g] * mask).astype(BF16)
            mixed = _dot(wm, vn[:, cols].astype(BF16), NN) + bt_ref[:, g:g + 1]
            a_ref[:, cols] = (u[:, cols] * mixed).astype(BF16)

    return pl.pallas_call(
        body, name=name, grid=(s // ch,),
        in_specs=[pl.BlockSpec((ch, 2 * gw), lambda n: (n, 0)), _row_spec(gw), _row_spec(gw),
                  pl.BlockSpec((groups, ch, ch), lambda n: (0, 0, 0)), pl.BlockSpec((ch, groups), lambda n: (0, 0))],
        out_specs=pl.BlockSpec((ch, gw), lambda n: (n, 0)),
        out_shape=jax.ShapeDtypeStruct((s, gw), BF16), compiler_params=_params(),
    )(z_big, ln_g, ln_b, w_s, b_s_t)


def _gmlp_bwd(z_big, da, dz_big, ln_g, ln_b, w_s, b_s_t, name, comm=None):
    s = z_big.shape[0]
    groups, ch, _ = w_s.shape
    gw = ln_g.shape[1]
    gd = gw // groups

    def body(z_ref, da_ref, dzin_ref, lng_ref, lnb_ref, ws_ref, bt_ref, dz_ref, gws_ref, gbt_ref, glng_ref, glnb_ref):
        del dzin_ref
        lng = lng_ref[...]
        u_pre, v_pre, u, vhat, rstd, vn = _gmlp_norm(z_ref[...].astype(F32), lng, lnb_ref[...], gw)
        da = da_ref[...].astype(F32)
        mask = _causal_mask(ch)
        first = pl.program_id(0) == 0
        dvn_parts = []
        lane = lax.broadcasted_iota(jnp.int32, (ch, LANES), 1)
        gb = jnp.zeros((ch, LANES), F32)
        for g in range(groups):
            cols = slice(g * gd, (g + 1) * gd)
            wm = (ws_ref[g] * mask).astype(BF16)
            vn_g = vn[:, cols].astype(BF16)
            mixed = _dot(wm, vn_g, NN) + bt_ref[:, g:g + 1]
            dz_ref[:, cols] = (da[:, cols] * mixed * _gelu_grad(u_pre[:, cols])).astype(BF16)
            dmixed = da[:, cols] * u[:, cols]
            dm16 = dmixed.astype(BF16)
            dvn_parts.append(_dot(wm, dm16, TN))
            gws = _dot(dm16, vn_g, NT) * mask

            @pl.when(first)
            def _(g=g, gws=gws):
                gws_ref[g] = gws

            @pl.when(jnp.logical_not(first))
            def _(g=g, gws=gws):
                gws_ref[g] += gws

            gb = gb + jnp.where(lane == g, jnp.sum(dmixed, axis=1, keepdims=True), 0.0)
        _accumulate(gbt_ref, gb)
        dvn = jnp.concatenate(dvn_parts, axis=1)
        _accumulate(glnb_ref, _colsum(dvn))
        _accumulate(glng_ref, _colsum(dvn * vhat))
        dvh = dvn * lng
        dvg = rstd * (dvh - _rowmean(dvh) - vhat * _rowmean(dvh * vhat))
        dz_ref[:, gw:] = (dvg * _gelu_grad(v_pre)).astype(BF16)

    zspec = pl.BlockSpec((ch, 2 * gw), lambda n: (n, 0))
    return _call(
        body, (z_big, da, dz_big, ln_g, ln_b, w_s, b_s_t), comm, name=name, grid=(s // ch,),
        in_specs=[zspec, pl.BlockSpec((ch, gw), lambda n: (n, 0)), pl.BlockSpec(memory_space=HBM),
                  _row_spec(gw), _row_spec(gw), pl.BlockSpec((groups, ch, ch), lambda n: (0, 0, 0)),
                  pl.BlockSpec((ch, groups), lambda n: (0, 0))],
        out_specs=[zspec, pl.BlockSpec((groups, ch, ch), lambda n: (0, 0, 0)),
                   pl.BlockSpec((ch, LANES), lambda n: (0, 0)), _row_spec(gw), _row_spec(gw)],
        out_shape=[jax.ShapeDtypeStruct(dz_big.shape, BF16), jax.ShapeDtypeStruct((groups, ch, ch), F32),
                   jax.ShapeDtypeStruct((ch, LANES), F32), jax.ShapeDtypeStruct((1, gw), F32),
                   jax.ShapeDtypeStruct((1, gw), F32)],
        input_output_aliases={2: 0})


def _mla_prep(z_lat, q_g, kv_g, rope_k, name):
    s, latw = z_lat.shape
    ql, kvl = q_g.shape[1], kv_g.shape[1]
    tb = _div_tile(s, 256, SUBLANES)

    def body(z_ref, qg_ref, kvg_ref, t_ref, qn_ref, kvn_ref, kr_ref):
        q = z_ref[:, :ql]
        qn_ref[...] = ((q * lax.rsqrt(_rowmean(q * q) + EPS)) * qg_ref[...]).astype(BF16)
        kv = z_ref[:, ql:ql + kvl]
        kvn_ref[...] = ((kv * lax.rsqrt(_rowmean(kv * kv) + EPS)) * kvg_ref[...]).astype(BF16)
        kk = z_ref[:, ql + kvl:] * t_ref[...]
        kr_ref[...] = (kk + pltpu.roll(kk, ROPE, axis=1)).astype(BF16)

    return pl.pallas_call(
        body, name=name, grid=(s // tb,),
        in_specs=[pl.BlockSpec((tb, latw), lambda i: (i, 0)), _row_spec(ql), _row_spec(kvl),
                  pl.BlockSpec((tb, 2 * ROPE), lambda i: (i, 0))],
        out_specs=[pl.BlockSpec((tb, ql), lambda i: (i, 0)), pl.BlockSpec((tb, kvl), lambda i: (i, 0)),
                   pl.BlockSpec((tb, 2 * ROPE), lambda i: (i, 0))],
        out_shape=[jax.ShapeDtypeStruct((s, ql), BF16), jax.ShapeDtypeStruct((s, kvl), BF16),
                   jax.ShapeDtypeStruct((s, 2 * ROPE), BF16)],
        compiler_params=_params(),
    )(z_lat, q_g, kv_g, rope_k)


def _attn_fwd(q, kv, kr, heads, name, comm=None):
    s = q.shape[0]
    t = _div_tile(s, 512)
    nb = s // t
    hp = 2 if heads % 2 == 0 else 1

    def body(q_ref, k_ref, kr_ref, v_ref, o_ref, lse_ref, m_ref, l_ref, acc_ref):
        i, j = pl.program_id(1), pl.program_id(2)

        @pl.when(j == 0)
        def _():
            m_ref[...] = jnp.full(m_ref.shape, -1e30, F32)
            l_ref[...] = jnp.zeros(l_ref.shape, F32)
            acc_ref[...] = jnp.zeros(acc_ref.shape, F32)

        def update(h, rows, n_keys, on_diagonal):
            vc = slice(h * VHEAD, (h + 1) * VHEAD)
            k_full = jnp.concatenate([k_ref[:n_keys, h * NOPE:(h + 1) * NOPE], kr_ref[:n_keys, :]], axis=1)
            sc = _dot(q_ref[rows, h * HEAD_W:(h + 1) * HEAD_W], k_full, NT)
            if on_diagonal:
                row_pos = rows.start + lax.broadcasted_iota(jnp.int32, sc.shape, 0)
                sc = jnp.where(lax.broadcasted_iota(jnp.int32, sc.shape, 1) <= row_pos, sc, -1e30)
            m_old = m_ref[h, rows, :]
            m_new = jnp.maximum(m_old, jnp.max(sc, axis=-1, keepdims=True))
            p = jnp.exp(sc - m_new)
            alpha = jnp.exp(m_old - m_new)
            l_new = alpha * l_ref[h, rows, :] + jnp.sum(p, axis=-1, keepdims=True)
            acc = alpha * acc_ref[rows, vc] + _dot(p.astype(BF16), v_ref[:n_keys, vc], NN)
            if on_diagonal:
                o_ref[rows, vc] = (acc / l_new).astype(BF16)
                lse_ref[h, rows, :] = jnp.broadcast_to(m_new + jnp.log(l_new), (rows.stop - rows.start, LANES))
            else:
                m_ref[h, rows, :], l_ref[h, rows, :], acc_ref[rows, vc] = m_new, l_new, acc

        def below_diagonal():
            for h in range(hp):
                update(h, slice(0, t), t, False)

        def on_diagonal():
            for h in range(hp):
                update(h, slice(0, t // 2), t // 2, True)
                update(h, slice(t // 2, t), t, True)

        pl.when(j < i)(below_diagonal)
        pl.when(j == i)(on_diagonal)

    kidx = lambda off: (lambda h, i, j: (jnp.minimum(i, j), off(h)))
    return _call(
        body, (q, kv, kr, kv), comm, name=name, grid=(heads // hp, nb, nb),
        in_specs=[pl.BlockSpec((t, hp * HEAD_W), lambda h, i, j: (i, h)),
                  pl.BlockSpec((t, hp * NOPE), kidx(lambda h: h)),
                  pl.BlockSpec((t, 2 * ROPE), kidx(lambda h: 0)),
                  pl.BlockSpec((t, hp * VHEAD), kidx(lambda h: heads // hp + h))],
        out_specs=[pl.BlockSpec((t, hp * VHEAD), lambda h, i, j: (i, h)),
                   pl.BlockSpec((hp, t, LANES), lambda h, i, j: (h, i, 0))],
        out_shape=[jax.ShapeDtypeStruct((s, heads * VHEAD), BF16), jax.ShapeDtypeStruct((heads, s, LANES), F32)],
        scratch_shapes=[pltpu.VMEM((hp, t, 1), F32), pltpu.VMEM((hp, t, 1), F32), pltpu.VMEM((t, hp * VHEAD), F32)])


def _attn_bwd(q, kv, kr, o, do, lse, heads, name, comm=None):
    s = q.shape[0]
    t = _div_tile(s, 512)
    nb = s // t
    hp = 2 if heads % 2 == 0 else 1

    def body(q_ref, k_ref, kr_ref, v_ref, o_ref, do_ref, lse_ref, dq_ref, dk_ref, dv_ref, dk_acc, dv_acc):
        j, i = pl.program_id(1), pl.program_id(2)

        @pl.when(jnp.logical_and(j == 0, i == 0))
        def _():
            dq_ref[...] = jnp.zeros(dq_ref.shape, F32)

        def update(h, rows, n_keys, on_diagonal, assign):
            qc, kc, vc = (slice(h * w, (h + 1) * w) for w in (HEAD_W, NOPE, VHEAD))
            n_rows = rows.stop - rows.start
            qv, do_v = q_ref[rows, qc], do_ref[rows, vc]
            k_full = jnp.concatenate([k_ref[:n_keys, kc], kr_ref[:n_keys, :]], axis=1)
            sc = _dot(qv, k_full, NT)
            if on_diagonal:
                row_pos = rows.start + lax.broadcasted_iota(jnp.int32, sc.shape, 0)
                sc = jnp.where(lax.broadcasted_iota(jnp.int32, sc.shape, 1) <= row_pos, sc, -1e30)
            p = jnp.exp(sc - lse_ref[h, rows, :1])
            dp = _dot(do_v, v_ref[:n_keys, vc], NT)
            delta = jnp.sum(do_v.astype(F32) * o_ref[rows, vc].astype(F32), axis=-1, keepdims=True)
            ds = (p * (dp - delta)).astype(BF16)
            dq_ref[pl.ds(pl.multiple_of(i * t + rows.start, n_rows), n_rows), qc] += _dot(ds, k_full, NN)
            dv_part, dk_part = _dot(p.astype(BF16), do_v, TN), _dot(ds, qv, TN)
            if assign:
                dv_acc[:n_keys, vc], dk_acc[:n_keys, qc] = dv_part, dk_part
            else:
                dv_acc[:n_keys, vc] += dv_part
                dk_acc[:n_keys, qc] += dk_part

        def on_diagonal():
            for h in range(hp):
                update(h, slice(t // 2, t), t, True, True)
                update(h, slice(0, t // 2), t // 2, True, False)

        def below_diagonal():
            for h in range(hp):
                update(h, slice(0, t), t, False, False)

        pl.when(i == j)(on_diagonal)
        pl.when(i > j)(below_diagonal)

        @pl.when(i == nb - 1)
        def _():
            dk_ref[...] = dk_acc[...].astype(BF16)
            dv_ref[...] = dv_acc[...].astype(BF16)

    qidx = lambda h, j, i: (jnp.maximum(i, j), h)
    return _call(
        body, (q, kv, kr, kv, o, do, lse), comm, name=name, grid=(heads // hp, nb, nb),
        in_specs=[pl.BlockSpec((t, hp * HEAD_W), qidx),
                  pl.BlockSpec((t, hp * NOPE), lambda h, j, i: (j, h)),
                  pl.BlockSpec((t, 2 * ROPE), lambda h, j, i: (j, 0)),
                  pl.BlockSpec((t, hp * VHEAD), lambda h, j, i: (j, heads // hp + h)),
                  pl.BlockSpec((t, hp * VHEAD), qidx), pl.BlockSpec((t, hp * VHEAD), qidx),
                  pl.BlockSpec((hp, t, LANES), lambda h, j, i: (h, jnp.maximum(i, j), 0))],
        out_specs=[pl.BlockSpec((s, hp * HEAD_W), lambda h, j, i: (0, h)),
                   pl.BlockSpec((t, hp * HEAD_W), lambda h, j, i: (j, h)),
                   pl.BlockSpec((t, hp * VHEAD), lambda h, j, i: (j, h))],
        out_shape=[jax.ShapeDtypeStruct((s, heads * HEAD_W), F32), jax.ShapeDtypeStruct((s, heads * HEAD_W), BF16),
                   jax.ShapeDtypeStruct((s, heads * VHEAD), BF16)],
        scratch_shapes=[pltpu.VMEM((t, hp * HEAD_W), F32), pltpu.VMEM((t, hp * VHEAD), F32)])


def _mla_bwd_mid(dq, dk, dv, rope_q, rope_k, heads, name):
    s = dq.shape[0]
    tb = _div_tile(s, 256, SUBLANES)

    def body(dq_ref, dk_ref, dv_ref, tq_ref, tk_ref, dqb_ref, dkv_ref, dkk_ref):
        tq = tq_ref[...]
        dkr = jnp.zeros((tb, 2 * ROPE), F32)
        for h in range(heads):
            cols = slice(h * HEAD_W, (h + 1) * HEAD_W)
            dqb_ref[:, cols] = (dq_ref[:, cols] * tq).astype(BF16)
            dkv_ref[:, h * NOPE:(h + 1) * NOPE] = dk_ref[:, h * HEAD_W:h * HEAD_W + NOPE]
            dkr = dkr + dk_ref[:, h * HEAD_W + NOPE:(h + 1) * HEAD_W].astype(F32)
        dkv_ref[:, heads * NOPE:] = dv_ref[...]
        dkk_ref[...] = (dkr + pltpu.roll(dkr, ROPE, axis=1)) * tk_ref[...]

    wq, wv = heads * HEAD_W, heads * VHEAD
    return pl.pallas_call(
        body, name=name, grid=(s // tb,),
        in_specs=[pl.BlockSpec((tb, wq), lambda i: (i, 0)), pl.BlockSpec((tb, wq), lambda i: (i, 0)),
                  pl.BlockSpec((tb, wv), lambda i: (i, 0)), pl.BlockSpec((tb, HEAD_W), lambda i: (i, 0)),
                  pl.BlockSpec((tb, 2 * ROPE), lambda i: (i, 0))],
        out_specs=[pl.BlockSpec((tb, wq), lambda i: (i, 0)), pl.BlockSpec((tb, heads * NOPE + wv), lambda i: (i, 0)),
                   pl.BlockSpec((tb, 2 * ROPE), lambda i: (i, 0))],
        out_shape=[jax.ShapeDtypeStruct((s, wq), BF16), jax.ShapeDtypeStruct((s, heads * NOPE + wv), BF16),
                   jax.ShapeDtypeStruct((s, 2 * ROPE), F32)],
        compiler_params=_params(),
    )(dq, dk, dv, rope_q, rope_k)


def _mla_bwd_post(z_lat, dqn, dkvn, dkk, q_g, kv_g, name):
    s, latw = z_lat.shape
    ql, kvl = q_g.shape[1], kv_g.shape[1]
    tb = _div_tile(s, 256, SUBLANES)

    def norm_bwd(xv, dn, g, dg_ref):
        r = lax.rsqrt(_rowmean(xv * xv) + EPS)
        xh = xv * r
        _accumulate(dg_ref, _colsum(dn * xh))
        dxh = dn * g
        return r * (dxh - xh * _rowmean(dxh * xh))

    def body(z_ref, dqn_ref, dkvn_ref, dkk_ref, qg_ref, kvg_ref, dz_ref, gq_ref, gkv_ref):
        dz_ref[:, :ql] = norm_bwd(z_ref[:, :ql], dqn_ref[...], qg_ref[...], gq_ref).astype(BF16)
        dz_ref[:, ql:ql + kvl] = norm_bwd(z_ref[:, ql:ql + kvl], dkvn_ref[...], kvg_ref[...], gkv_ref).astype(BF16)
        dz_ref[:, ql + kvl:] = dkk_ref[...].astype(BF16)

    return pl.pallas_call(
        body, name=name, grid=(s // tb,),
        in_specs=[pl.BlockSpec((tb, latw), lambda i: (i, 0)), pl.BlockSpec((tb, ql), lambda i: (i, 0)),
                  pl.BlockSpec((tb, kvl), lambda i: (i, 0)), pl.BlockSpec((tb, 2 * ROPE), lambda i: (i, 0)),
                  _row_spec(ql), _row_spec(kvl)],
        out_specs=[pl.BlockSpec((tb, latw), lambda i: (i, 0)), _row_spec(ql), _row_spec(kvl)],
        out_shape=[jax.ShapeDtypeStruct((s, latw), BF16), jax.ShapeDtypeStruct((1, ql), F32),
                   jax.ShapeDtypeStruct((1, kvl), F32)],
        compiler_params=_params(),
    )(z_lat, dqn, dkvn, dkk, q_g, kv_g)


CONV_ROWS = 128
CONV_HALO = 16


def _row_steps(n_rows, step):
    step(0, True)
    if n_rows > CONV_ROWS:
        def later(i, carry):
            step(pl.multiple_of(i * CONV_ROWS, CONV_ROWS), False)
            return carry
        lax.fori_loop(1, n_rows // CONV_ROWS, later, 0)


def _conv_taps(pre_ref, r0, first):
    if first:
        win = jnp.concatenate([jnp.zeros((CONV_HALO, pre_ref.shape[1]), F32), pre_ref[0:CONV_ROWS, :].astype(F32)])
    else:
        win = pre_ref[pl.ds(pl.multiple_of(r0 - CONV_HALO, CONV_HALO), CONV_ROWS + CONV_HALO), :].astype(F32)
    return win[CONV_HALO:], pltpu.roll(win, 1, axis=0)[CONV_HALO:], pltpu.roll(win, 2, axis=0)[CONV_HALO:]


def _conv(taps, w_ref, b_ref):
    return w_ref[2:3, :] * taps[0] + w_ref[1:2, :] * taps[1] + w_ref[0:1, :] * taps[2] + b_ref[...]


def _conv_fwd(up_pre, conv_w, conv_b, name):
    s, ff2 = up_pre.shape
    ff = ff2 // 2
    tc = _div_tile(ff, 256)
    nb = ff // tc
    assert s % CONV_ROWS == 0

    def body(pg_ref, pv_ref, wg_ref, wv_ref, bg_ref, bv_ref, act_ref):
        def step(r0, first):
            gate = _conv(_conv_taps(pg_ref, r0, first), wg_ref, bg_ref)
            val = _conv(_conv_taps(pv_ref, r0, first), wv_ref, bv_ref)
            act_ref[pl.ds(r0, CONV_ROWS), :] = (gate * _sigmoid(gate) * val).astype(BF16)

        _row_steps(s, step)

    def col(rows, off):
        return pl.BlockSpec((rows, tc), lambda j: (0, j + off))

    return pl.pallas_call(
        body, name=name, grid=(nb,),
        in_specs=[col(s, 0), col(s, nb), col(CONV_TAPS, 0), col(CONV_TAPS, nb), col(1, 0), col(1, nb)],
        out_specs=col(s, 0), out_shape=jax.ShapeDtypeStruct((s, ff), BF16), compiler_params=_params(),
    )(up_pre, up_pre, conv_w, conv_w, conv_b, conv_b)


def _conv_bwd(up_pre, dact, conv_w, conv_b, name, comm=None):
    s, ff2 = up_pre.shape
    ff = ff2 // 2
    tc = _div_tile(ff, 256)
    nb = ff // tc
    assert s % CONV_ROWS == 0

    def body(pg_ref, pv_ref, da_ref, wg_ref, wv_ref, bg_ref, bv_ref, dup_ref, gwg_ref, gwv_ref, gbg_ref, gbv_ref,
             dxg_ref, dxv_ref):
        for ref in (gwg_ref, gwv_ref, gbg_ref, gbv_ref):
            ref[...] = jnp.zeros(ref.shape, F32)
        for ref in (dxg_ref, dxv_ref):
            ref[s:s + SUBLANES, :] = jnp.zeros((SUBLANES, tc), F32)

        def sums(taps, dx, gw_ref, gb_ref):
            gb_ref[...] += _colsum(dx)
            for k in range(CONV_TAPS):
                gw_ref[k:k + 1, :] += _colsum(dx * taps[CONV_TAPS - 1 - k])

        def forward(r0, first):
            rows = pl.ds(r0, CONV_ROWS)
            taps_g, taps_v = _conv_taps(pg_ref, r0, first), _conv_taps(pv_ref, r0, first)
            gate, val = _conv(taps_g, wg_ref, bg_ref), _conv(taps_v, wv_ref, bv_ref)
            da = da_ref[rows, :].astype(F32)
            sg = _sigmoid(gate)
            dxv, dxg = da * gate * sg, da * val * sg * (1.0 + gate * (1.0 - sg))
            dxv_ref[rows, :], dxg_ref[rows, :] = dxv, dxg
            sums(taps_v, dxv, gwv_ref, gbv_ref)
            sums(taps_g, dxg, gwg_ref, gbg_ref)

        def backward(r0, first):
            del first
            n = CONV_ROWS + SUBLANES
            for dx_ref, w_ref, out_ref in ((dxg_ref, wg_ref, dup_ref.at[0]), (dxv_ref, wv_ref, dup_ref.at[1])):
                win = dx_ref[pl.ds(r0, n), :]
                ahead1 = pltpu.roll(win, n - 1, axis=0)[:CONV_ROWS]
                ahead2 = pltpu.roll(win, n - 2, axis=0)[:CONV_ROWS]
                out_ref[pl.ds(r0, CONV_ROWS), :] = (w_ref[2:3, :] * win[:CONV_ROWS] + w_ref[1:2, :] * ahead1
                                                    + w_ref[0:1, :] * ahead2).astype(BF16)

        _row_steps(s, forward)
        _row_steps(s, backward)

    def col(rows, off):
        return pl.BlockSpec((rows, tc), lambda j: (0, j + off))

    return _call(
        body, (up_pre, up_pre, dact, conv_w, conv_w, conv_b, conv_b), comm, name=name, grid=(nb,),
        in_specs=[col(s, 0), col(s, nb), col(s, 0), col(CONV_TAPS, 0), col(CONV_TAPS, nb), col(1, 0), col(1, nb)],
        out_specs=[pl.BlockSpec((2, s, tc), lambda j: (0, 0, j)), col(CONV_TAPS, 0), col(CONV_TAPS, 0),
                   col(1, 0), col(1, 0)],
        out_shape=[jax.ShapeDtypeStruct((2, s, ff), BF16)] + [jax.ShapeDtypeStruct((CONV_TAPS, ff), F32)] * 2
        + [jax.ShapeDtypeStruct((1, ff), F32)] * 2,
        scratch_shapes=[pltpu.VMEM((s + SUBLANES, tc), F32)] * 2)


def _ada_fwd(c_all, w, b, name):
    nseq, d = c_all.shape
    na = w.shape[1]
    tn = _div_tile(na, 512)

    def body(c_ref, w_ref, b_ref, o_ref):
        cv = c_ref[...]
        sc = cv * _sigmoid(cv)
        o_ref[...] = jnp.dot(sc, w_ref[...], preferred_element_type=F32, precision=lax.Precision.HIGHEST) + b_ref[...]

    return pl.pallas_call(
        body, name=name, grid=(na // tn,),
        in_specs=[pl.BlockSpec((nseq, d), lambda j: (0, 0)), pl.BlockSpec((d, tn), lambda j: (0, j)),
                  pl.BlockSpec((1, tn), lambda j: (0, j))],
        out_specs=pl.BlockSpec((nseq, tn), lambda j: (0, j)),
        out_shape=jax.ShapeDtypeStruct((nseq, na), F32), compiler_params=_params(),
    )(c_all, w, b)


def _ada_bwd(c_all_t, dmod, name):
    d, nseq = c_all_t.shape
    na = dmod.shape[1]
    tm, tn = _div_tile(d, 256, SUBLANES), _div_tile(na, 512)

    def body(c_ref, dm_ref, o_ref):
        cv = c_ref[...]
        sc = cv * _sigmoid(cv)
        acc = sc[:, 0:1] * dm_ref[0:1, :]
        for bi in range(1, nseq):
            acc = acc + sc[:, bi:bi + 1] * dm_ref[bi:bi + 1, :]
        o_ref[...] = acc

    return pl.pallas_call(
        body, name=name, grid=(d // tm, na // tn),
        in_specs=[pl.BlockSpec((tm, nseq), lambda i, j: (i, 0)), pl.BlockSpec((nseq, tn), lambda i, j: (0, j))],
        out_specs=pl.BlockSpec((tm, tn), lambda i, j: (i, j)),
        out_shape=jax.ShapeDtypeStruct((d, na), F32), compiler_params=_params(),
    )(c_all_t, dmod)


def _adamw(w, g, m, v, name, comm=None, after=None):
    rows, cols = w.shape
    tb = _div_tile(rows, max(SUBLANES, (256 * 1024) // cols // SUBLANES * SUBLANES), SUBLANES)
    c1 = 1.0 / (1.0 - ADAM_B1 ** ADAM_STEP)
    c2 = 1.0 / (1.0 - ADAM_B2 ** ADAM_STEP)

    def body(*refs):
        w_ref, g_ref, m_ref, v_ref = refs[:4]
        d_ref, nm_ref, nv_ref = refs[-3:]
        gv = g_ref[...]
        nm = ADAM_B1 * m_ref[...] + (1.0 - ADAM_B1) * gv
        nv = ADAM_B2 * v_ref[...] + (1.0 - ADAM_B2) * (gv * gv)
        nm_ref[...] = nm
        nv_ref[...] = nv
        d_ref[...] = -ADAM_LR * ((nm * c1) / (jnp.sqrt(nv * c2) + ADAM_EPS) + ADAM_WD * w_ref[...])

    blk = pl.BlockSpec((tb, cols), lambda i: (i, 0))
    operands, in_specs = (w, g, m, v), [blk] * 4
    if after is not None:
        operands, in_specs = operands + (after,), in_specs + [pl.BlockSpec(after.shape, lambda i: (0, 0))]
    return _call(body, operands, comm, name=name, grid=(rows // tb,), in_specs=in_specs, out_specs=[blk] * 3,
                 out_shape=[jax.ShapeDtypeStruct((rows, cols), F32)] * 3)


def _sum_leading(parts, name, after=()):
    n, rows, cols = parts.shape
    tb = _div_tile(rows, 512, SUBLANES)

    def body(p_ref, *rest):
        o_ref = rest[-1]
        acc = p_ref[0]
        for k in range(1, n):
            acc = acc + p_ref[k]
        o_ref[...] = acc

    return pl.pallas_call(
        body, name=name, grid=(rows // tb,),
        in_specs=[pl.BlockSpec((n, tb, cols), lambda i: (0, i, 0))] + [pl.BlockSpec(memory_space=pl.ANY)] * len(after),
        out_specs=pl.BlockSpec((tb, cols), lambda i: (i, 0)),
        out_shape=jax.ShapeDtypeStruct((rows, cols), F32), compiler_params=_params(),
    )(parts, *after)


def _place():
    x, y, c = lax.axis_index("x"), lax.axis_index("y"), lax.axis_index("c")
    return x, y, c, [(1 - x, y), (x, 1 - y), (1 - x, 1 - y)]


def _all_gather(block, name):
    m_per, n = block.shape

    def body(x_ref, out_ref, send_sems, recv_sems, local_sem):
        x, y, c, chips = _place()
        me, sibling = (x, y, c), (x, y, 1 - c)

        def rows(px, py, pc):
            return out_ref.at[pl.ds((4 * px + 2 * py + pc) * m_per, m_per), :]

        def copy(k, blk, to, src=None):
            return pltpu.make_async_remote_copy(
                src_ref=rows(*blk) if src is None else src, dst_ref=rows(*blk), send_sem=send_sems.at[k],
                recv_sem=recv_sems.at[k], device_id=to, device_id_type=MESH)

        mine = pltpu.make_async_copy(x_ref, rows(*me), local_sem)
        mine.start()
        first = [copy(0, me, sibling, src=x_ref)]
        first += [copy(1 + j, me, (*chip, c), src=x_ref) for j, chip in enumerate(chips)]
        for cp in first:
            cp.start()
        passed = [copy(4 + j, (*chip, c), sibling) for j, chip in enumerate(chips)]
        for j, chip in enumerate(chips):
            copy(1 + j, (*chip, c), me).wait_recv()
            passed[j].start()
        copy(0, sibling, me).wait_recv()
        for j, chip in enumerate(chips):
            copy(4 + j, (*chip, 1 - c), me).wait_recv()
        for cp in first + passed:
            cp.wait_send()
        mine.wait()

    return pl.pallas_call(
        body, name=name, out_shape=jax.ShapeDtypeStruct((N_DEV * m_per, n), block.dtype),
        in_specs=[pl.BlockSpec(memory_space=pltpu.VMEM)], out_specs=pl.BlockSpec(memory_space=pltpu.VMEM),
        scratch_shapes=[pltpu.SemaphoreType.DMA((7,)), pltpu.SemaphoreType.DMA((7,)), pltpu.SemaphoreType.DMA],
        compiler_params=_params(),
    )(block)


def _hbm_specs(n):
    return [pl.BlockSpec(memory_space=HBM)] * n


def _part(ref, by_cols, half, quarter=None, lead=None):
    extent = ref.shape[-1] if by_cols else ref.shape[-2]
    size = extent // 2 if quarter is None else extent // 4
    first = half * (extent // 2) + (0 if quarter is None else quarter * size)
    tile = LANES if by_cols else 2 * SUBLANES
    span = pl.ds(pl.multiple_of(first, tile) if size % tile == 0 else first, size)
    index = (slice(None), span) if by_cols else (span, slice(None))
    return ref.at[index] if lead is None else ref.at[(lead,) + index]


def _half_rows(ref, half, lead=None):
    return _part(ref, False, half, lead=lead)


class _Comm:
    def __init__(self, operands, out_shape, sem_dims, build, aliases=None):
        self.operands, self.out_shape, self.sem_dims = list(operands), list(out_shape), list(sem_dims)
        self.scratch = [pltpu.SemaphoreType.DMA(d) for d in sem_dims]
        self.build, self.aliases = build, dict(aliases or {})


class _SemGrid:
    def __init__(self, sems, dims):
        self.sems, self.dims, self.at = list(sems), tuple(dims), self

    def __getitem__(self, index):
        index = index if isinstance(index, tuple) else (index,)
        flat = 0
        for i, d in zip(index, self.dims):
            flat = flat * d + i
        return self.sems[flat]


def _call(body, operands, comm=None, *, name, grid, in_specs, out_specs, out_shape, scratch_shapes=(),
          input_output_aliases=None):
    aliases = dict(input_output_aliases or {})
    if comm is None:
        return pl.pallas_call(
            body, name=name, grid=grid, in_specs=in_specs, out_specs=out_specs, out_shape=out_shape,
            scratch_shapes=list(scratch_shapes), input_output_aliases=aliases, compiler_params=_params())(*operands)
    single = not isinstance(out_shape, (list, tuple))
    outs = [out_shape] if single else list(out_shape)
    ospecs = [out_specs] if single else list(out_specs)
    n_in, n_out, n_scr = len(operands), len(outs), len(scratch_shapes)
    c_in, c_out = len(comm.operands), len(comm.out_shape)
    for i, o in comm.aliases.items():
        aliases[n_in + i] = n_out + o

    def hosted(*refs):
        ins, c_ins = refs[:n_in], refs[n_in:n_in + c_in]
        o0 = n_in + c_in
        o_refs, c_outs = refs[o0:o0 + n_out], refs[o0 + n_out:o0 + n_out + c_out]
        s0 = o0 + n_out + c_out
        scr, sems = refs[s0:s0 + n_scr], refs[s0 + n_scr:]
        stages = comm.build(c_ins, c_outs, sems)
        step, n_steps = 0, 1
        for dim, size in enumerate(grid):
            step, n_steps = step * size + pl.program_id(dim), n_steps * size
        pl.when(step == 0)(stages[0])
        body(*ins, *o_refs, *scr)
        for stage in stages[1:-1]:
            pl.when(step == (n_steps * MIDDLE_STAGE_AT) // 100)(stage)
        pl.when(step == n_steps - 1)(stages[-1])

    res = pl.pallas_call(
        hosted, name=name, grid=grid, in_specs=list(in_specs) + _hbm_specs(c_in),
        out_specs=ospecs + _hbm_specs(c_out), out_shape=outs + comm.out_shape,
        scratch_shapes=list(scratch_shapes) + comm.scratch, input_output_aliases=aliases,
        compiler_params=_params())(*operands, *comm.operands)
    return (res[0] if single else res[:n_out]), res[n_out:]


def _run_comm(comm, name):
    c_in, c_out = len(comm.operands), len(comm.out_shape)

    def body(*refs):
        for stage in comm.build(refs[:c_in], refs[c_in:c_in + c_out], refs[c_in + c_out:]):
            stage()

    return pl.pallas_call(
        body, name=name, in_specs=_hbm_specs(c_in), out_specs=_hbm_specs(c_out), out_shape=comm.out_shape,
        scratch_shapes=comm.scratch, input_output_aliases=comm.aliases, compiler_params=_params())(*comm.operands)


def _join_comms(comms):
    def build(in_refs, out_refs, sems):
        staged, i, o, k = [], 0, 0, 0
        for cm in comms:
            ni, no, ns = len(cm.operands), len(cm.out_shape), len(cm.sem_dims)
            staged.append(cm.build(in_refs[i:i + ni], out_refs[o:o + no], sems[k:k + ns]))
            i, o, k = i + ni, o + no, k + ns
        def run(fns):
            def stage():
                for fn in fns:
                    fn()
            return stage

        return (run([st[0] for st in staged]), run([fn for st in staged for fn in st[1:-1]]),
                run([st[-1] for st in staged]))

    aliases, i, o = {}, 0, 0
    for cm in comms:
        aliases.update({i + a: o + b for a, b in cm.aliases.items()})
        i, o = i + len(cm.operands), o + len(cm.out_shape)
    return _Comm(sum((cm.operands for cm in comms), []), sum((cm.out_shape for cm in comms), []),
                 sum((cm.sem_dims for cm in comms), []), build, aliases)


def _gather8_comm(block):
    def build(in_refs, out_refs, sems):
        (src,), (out,), (send_sems, recv_sems) = in_refs, out_refs, sems
        x, y, c, chips = _place()
        me, sibling = (x, y, c), (x, y, 1 - c)

        def copy(k, blk, to, own=False):
            dst = out.at[4 * blk[0] + 2 * blk[1] + blk[2]]
            return pltpu.make_async_remote_copy(
                src_ref=src if own else dst, dst_ref=dst, send_sem=send_sems.at[k], recv_sem=recv_sems.at[k],
                device_id=to, device_id_type=MESH)

        first = [copy(0, me, sibling, own=True)] + [copy(1 + j, me, (*chip, c), own=True)
                                                     for j, chip in enumerate(chips)]
        passed = [copy(4 + j, (*chip, c), sibling) for j, chip in enumerate(chips)]

        def start():
            for cp in first:
                cp.start()

        def middle():
            for j, chip in enumerate(chips):
                copy(1 + j, (*chip, c), me).wait_recv()
                passed[j].start()

        def finish():
            copy(0, sibling, me).wait_recv()
            for j, chip in enumerate(chips):
                copy(4 + j, (*chip, 1 - c), me).wait_recv()
            for cp in first + passed:
                cp.wait_send()

        return start, middle, finish

    return _Comm([block], [jax.ShapeDtypeStruct((N_DEV,) + block.shape, block.dtype)], [(7,), (7,)], build)


def _gather_comm(shards, by_cols=()):
    nw = len(shards)

    def build(in_refs, out_refs, sems):
        send_sems, recv_sems = sems
        x, y, c, chips = _place()
        me, sibling = (x, y, c), (x, y, 1 - c)
        across_x, across_y, diagonal = chips

        def copy(w, k, block, part, to, src=None):
            dst = _part(out_refs[w], w in by_cols, part[1], part[2] if part[0] else None, 2 * block[0] + block[1])
            return pltpu.make_async_remote_copy(
                src_ref=dst if src is None else src, dst_ref=dst, send_sem=send_sems.at[w, k],
                recv_sem=recv_sems.at[w, k], device_id=to, device_id_type=MESH)

        first = [copy(w, j, (x, y), (0, c), (*chip, c), src=_part(in_refs[w], w in by_cols, c))
                 for w in range(nw) for j, chip in enumerate((across_x, across_y))]
        passed = [[copy(w, 2, across_x, (1, c, 0), (*across_y, c)), copy(w, 3, across_y, (1, c, 1), (*across_x, c)),
                   copy(w, 4, across_x, (0, c), sibling), copy(w, 5, across_y, (0, c), sibling)] for w in range(nw)]
        last = [[copy(w, 6, diagonal, (1, c, 0), sibling), copy(w, 7, diagonal, (1, c, 1), sibling)]
                for w in range(nw)]

        def start():
            for cp in first:
                cp.start()

        def middle():
            for w in range(nw):
                copy(w, 0, across_x, (0, c), me).wait_recv()
                copy(w, 1, across_y, (0, c), me).wait_recv()
                for cp in passed[w]:
                    cp.start()

        def finish():
            for w in range(nw):
                copy(w, 2, diagonal, (1, c, 0), me).wait_recv()
                copy(w, 3, diagonal, (1, c, 1), me).wait_recv()
                for cp in last[w]:
                    cp.start()
            for w in range(nw):
                for k, block, part in ((4, across_x, (0, 1 - c)), (5, across_y, (0, 1 - c)),
                                       (6, diagonal, (1, 1 - c, 0)), (7, diagonal, (1, 1 - c, 1))):
                    copy(w, k, block, part, me).wait_recv()
            for cp in first + sum(passed, []) + sum(last, []):
                cp.wait_send()

        return start, middle, finish

    return _Comm(shards, [jax.ShapeDtypeStruct((N_CHIPS,) + w.shape, w.dtype) for w in shards],
                 [(nw, 8), (nw, 8)], build)


def _halved(shape, by_cols):
    return shape[:-1] + (shape[-1] // 2,) if by_cols else shape[:-2] + (shape[-2] // 2, shape[-1])


def _swap_comm(gs, by_cols=()):
    nw = len(gs)

    def build(in_refs, out_refs, sems):
        send_sems, recv_sems = sems
        x, y, c, _ = _place()
        cps = []
        for w in range(nw):
            cps.append(pltpu.make_async_remote_copy(
                src_ref=_part(in_refs[w], w in by_cols, 1 - c, lead=slice(None)), dst_ref=out_refs[w],
                send_sem=send_sems.at[w], recv_sem=recv_sems.at[w], device_id=(x, y, 1 - c), device_id_type=MESH))

        def start():
            for cp in cps:
                cp.start()

        def finish():
            for cp in cps:
                cp.wait()

        return start, finish

    return _Comm(gs, [jax.ShapeDtypeStruct(_halved(g.shape, w in by_cols), g.dtype) for w, g in enumerate(gs)],
                 [(nw,), (nw,)], build)


def _exchange_comm(s1s):
    nw = len(s1s)

    def build(in_refs, out_refs, sems):
        send_sems, recv_sems = sems
        x, y, c, chips = _place()
        cps = [pltpu.make_async_remote_copy(
            src_ref=in_refs[w].at[2 * chip[0] + chip[1]], dst_ref=out_refs[w].at[j], send_sem=send_sems.at[w, j],
            recv_sem=recv_sems.at[w, j], device_id=(*chip, c), device_id_type=MESH)
            for w in range(nw) for j, chip in enumerate(chips)]

        def start():
            for cp in cps:
                cp.start()

        def finish():
            for cp in cps:
                cp.wait()

        return start, finish

    return _Comm(s1s, [jax.ShapeDtypeStruct((N_CHIPS - 1,) + s.shape[1:], s.dtype) for s in s1s],
                 [(nw, 3), (nw, 3)], build)


def _size(dims):
    n = 1
    for d in dims:
        n *= d
    return n


def _sem_grids(comm, sem_refs):
    grids, pos = [], 0
    for dims in comm.sem_dims:
        grids.append(_SemGrid(sem_refs[pos:pos + _size(dims)], dims))
        pos += _size(dims)
    return grids


def _comm_split_start(comm, name, after=()):
    c_in, c_out = len(comm.operands), len(comm.out_shape)
    counts = [_size(d) for d in comm.sem_dims]
    n_sem = sum(counts)
    assert not comm.aliases

    def body(*refs):
        srcs, lands = refs[:c_in], refs[c_in:c_in + c_out]
        first_sem = c_in + c_out + len(after)
        start, _ = comm.build(srcs, lands, _sem_grids(comm, refs[first_sem:first_sem + n_sem]))
        start()
        refs[-1][...] = jnp.zeros(refs[-1].shape, refs[-1].dtype)

    lands = [pltpu.with_memory_space_constraint(lax.empty(o.shape, o.dtype), HBM) for o in comm.out_shape]
    srcs = [pltpu.with_memory_space_constraint(a, HBM) for a in comm.operands]
    res = pl.pallas_call(
        body, name=name, in_specs=_hbm_specs(c_in + c_out) + [pl.BlockSpec(memory_space=pl.ANY)] * len(after),
        out_specs=[pl.BlockSpec(memory_space=pltpu.SEMAPHORE)] * n_sem + _hbm_specs(c_in + c_out)
        + [pl.BlockSpec(memory_space=pltpu.VMEM)],
        out_shape=[pltpu.SemaphoreType.DMA(())] * n_sem + [pltpu.HBM(a.shape, a.dtype) for a in comm.operands]
        + [pltpu.HBM(o.shape, o.dtype) for o in comm.out_shape] + [jax.ShapeDtypeStruct((SUBLANES, LANES), F32)],
        input_output_aliases={i: n_sem + i for i in range(c_in + c_out)},
        compiler_params=_params(has_side_effects=pltpu.SideEffectType.DATAFLOW_SIDE_EFFECTING))(*srcs, *lands, *after)
    return res[:-1], res[-1]


def _comm_split_wait(comm, state, after, name):
    c_in, c_out, n_sem = len(comm.operands), len(comm.out_shape), sum(_size(d) for d in comm.sem_dims)
    sems, srcs, lands = state[:n_sem], state[n_sem:n_sem + c_in], state[n_sem + c_in:]

    def body(*refs):
        src_refs, land_refs = refs[:c_in], refs[c_in:c_in + c_out]
        _, finish = comm.build(src_refs, land_refs, _sem_grids(comm, refs[c_in + c_out:c_in + c_out + n_sem]))
        finish()

    sem_spec = pl.BlockSpec(memory_space=pltpu.SEMAPHORE)
    res = pl.pallas_call(
        body, name=name, in_specs=_hbm_specs(c_in + c_out) + [sem_spec] * n_sem + [pl.BlockSpec(memory_space=pl.ANY)],
        out_specs=_hbm_specs(c_in + c_out),
        out_shape=[pltpu.HBM(a.shape, a.dtype) for a in srcs] + [pltpu.HBM(o.shape, o.dtype) for o in lands],
        input_output_aliases={i: i for i in range(c_in + c_out)},
        compiler_params=_params(has_side_effects=pltpu.SideEffectType.DATAFLOW_SIDE_EFFECTING),
    )(*srcs, *lands, *sems, after)
    return res[:c_in], res[c_in:]


def _share_comm(fs, by_cols=()):
    nw = len(fs)

    def build(in_refs, out_refs, sems):
        del in_refs
        send_sems, recv_sems = sems
        x, y, c, _ = _place()

        def copy(w, half):
            part = _part(out_refs[w], w in by_cols, half)
            return pltpu.make_async_remote_copy(
                src_ref=part, dst_ref=part, send_sem=send_sems.at[w], recv_sem=recv_sems.at[w],
                device_id=(x, y, 1 - c), device_id_type=MESH)

        sends = [copy(w, c) for w in range(nw)]

        def start():
            for cp in sends:
                cp.start()

        def finish():
            for w in range(nw):
                copy(w, 1 - c).wait_recv()
            for cp in sends:
                cp.wait_send()

        return start, finish

    return _Comm(fs, [jax.ShapeDtypeStruct(f.shape, f.dtype) for f in fs],
                 [(nw,), (nw,)], build,
                 aliases={w: w for w in range(nw)})


def _add_sibling(g, r1, place, name, by_cols=False):
    nch, h, cols = r1.shape
    tr = _div_tile(h, 1024 if by_cols else 512, 2 * SUBLANES)
    nb = h // tr
    mine = (lambda k, i, p: (k, i, p[0])) if by_cols else (lambda k, i, p: (k, p[0] * nb + i, 0))

    def body(place_ref, g_ref, r_ref, o_ref):
        del place_ref
        o_ref[...] = (g_ref[...].astype(F32) + r_ref[...].astype(F32)).astype(BF16)

    spec = pltpu.PrefetchScalarGridSpec(
        num_scalar_prefetch=1, grid=(nch, nb),
        in_specs=[pl.BlockSpec((None, tr, cols), mine), pl.BlockSpec((None, tr, cols), lambda k, i, p: (k, i, 0))],
        out_specs=pl.BlockSpec((None, tr, cols), lambda k, i, p: (k, i, 0)))
    return pl.pallas_call(body, name=name, grid_spec=spec, out_shape=jax.ShapeDtypeStruct((nch, h, cols), BF16),
                          compiler_params=_params())(place, g, r1)


def _add_chips(s1, r2, place, name, by_cols=False):
    _, h, cols = s1.shape
    tr = _div_tile(h, 1024 if by_cols else 512, 2 * SUBLANES)
    nb = h // tr
    mine = (lambda i, p: (i, p[0])) if by_cols else (lambda i, p: (p[0] * nb + i, 0))
    whole = (h, 2 * cols) if by_cols else (2 * h, cols)

    def body(place_ref, s_ref, r_ref, o_ref):
        del place_ref
        acc = s_ref[...].astype(F32)
        for j in range(N_CHIPS - 1):
            acc = acc + r_ref[j].astype(F32)
        o_ref[...] = acc

    spec = pltpu.PrefetchScalarGridSpec(
        num_scalar_prefetch=1, grid=(nb,),
        in_specs=[pl.BlockSpec((None, tr, cols), lambda i, p: (p[1], i, 0)),
                  pl.BlockSpec((N_CHIPS - 1, tr, cols), lambda i, p: (0, i, 0))],
        out_specs=pl.BlockSpec((tr, cols), mine))
    return pl.pallas_call(body, name=name, grid_spec=spec, out_shape=jax.ShapeDtypeStruct(whole, F32),
                          compiler_params=_params())(place, s1, r2)


def _quarter_turn(m):
    h = m.shape[-1] // 2
    return jnp.concatenate([-m[..., h:], m[..., :h]], axis=-1)


def _quarter_turn_back(m):
    h = m.shape[-1] // 2
    return jnp.concatenate([m[..., h:], -m[..., :h]], axis=-1)


def _stack_rows(parts):
    out = lax.empty((sum(p.shape[0] for p in parts),) + parts[0].shape[1:], parts[0].dtype)
    row = 0
    for p in parts:
        out = lax.dynamic_update_slice(out, p, (row, 0))
        row += p.shape[0]
    return out


def _join_cols(sh):
    return jnp.concatenate([sh[k] for k in range(N_CHIPS)], axis=1)


def _split_cols(full):
    c = full.shape[1] // N_CHIPS
    return jnp.stack([full[:, k * c:(k + 1) * c] for k in range(N_CHIPS)])


def kernel(x, c, positions, w_ada, b_ada, pre_norm1_g, w_in, gm_ln_g, gm_ln_b, gm_w_s, gm_b_s, w_branch_a, q_norm_g, w_uq, kv_norm_g, w_ukv, w_branch_b, w_out, post_norm1_g, pre_norm2_g, w_up, conv_w, conv_b, w_down, post_norm2_g, loss_target, m_w_ada, m_b_ada, m_pre_norm1_g, m_w_in, m_gm_ln_g, m_gm_ln_b, m_gm_w_s, m_gm_b_s, m_w_branch_a, m_q_norm_g, m_w_uq, m_kv_norm_g, m_w_ukv, m_w_branch_b, m_w_out, m_post_norm1_g, m_pre_norm2_g, m_w_up, m_conv_w, m_conv_b, m_w_down, m_post_norm2_g, v_w_ada, v_b_ada, v_pre_norm1_g, v_w_in, v_gm_ln_g, v_gm_ln_b, v_gm_w_s, v_gm_b_s, v_w_branch_a, v_q_norm_g, v_w_uq, v_kv_norm_g, v_w_ukv, v_w_branch_b, v_w_out, v_post_norm1_g, v_pre_norm2_g, v_w_up, v_conv_w, v_conv_b, v_w_down, v_post_norm2_g):
    given = dict(locals())
    s, d = x.shape[1], x.shape[2]
    gw = gm_ln_g.shape[0]
    ql, kvl = q_norm_g.shape[0], kv_norm_g.shape[0]
    heads = N_CHIPS * w_uq.shape[1] // (NOPE + ROPE)
    ff = N_CHIPS * w_down.shape[0]
    assert gw == d and N_CHIPS * w_ukv.shape[1] == heads * (NOPE + VHEAD)
    ix, iy, ic = lax.axis_index("x"), lax.axis_index("y"), lax.axis_index("c")
    chip = 2 * ix + iy
    dev = 2 * chip + ic
    row = lambda v: v.reshape(1, -1)

    first = _all_gather(jnp.concatenate([jnp.pad(c, ((0, SUBLANES - 1), (0, 0))),
                                         jnp.pad(conv_w, ((0, SUBLANES - CONV_TAPS), (0, 0)))], axis=1), "gather_c")
    first = first.reshape(N_DEV, SUBLANES, d + conv_w.shape[1])
    c_all = first[:, 0, :d]
    conv_wf = first[::N_CORES, :CONV_TAPS, d:].transpose(1, 0, 2).reshape(CONV_TAPS, N_CHIPS * conv_w.shape[1])
    na = w_ada.shape[1]
    b_ada_mine = lax.dynamic_slice(b_ada, (chip * na,), (na,))
    mod_cols = _ada_fwd(c_all, w_ada, row(b_ada_mine), "ada_fwd")
    mod_all = _all_gather(mod_cols, "gather_mod").reshape(N_CHIPS, N_CORES, N_DEV, na)[:, 0]
    mod = lax.dynamic_index_in_dim(mod_all, dev, axis=1, keepdims=False).reshape(N_MOD, d)
    shift1, scale1, gate1, shift2, scale2, gate2 = (mod[i:i + 1] for i in range(N_MOD))

    mine = {n: (given[n].T if n == "w_in" else given[n]).astype(BF16) for n in BIG}
    gather = lambda names: _gather_comm([mine[n] for n in names], [i for i, n in enumerate(names) if n == "w_in"])
    whole = lambda n, g: lax.dynamic_update_slice(g, mine[n][None], (chip, 0, 0))
    rows4 = lambda sh4: sh4.reshape(-1, sh4.shape[2])
    wi_t = rows4(whole("w_in", _run_comm(gather(["w_in"]), "gather_w_in")[0]))
    o_q, o_kv, o_pe, o_ga = 2 * gw, 2 * gw + ql, 2 * gw + ql + kvl, 2 * gw + ql + kvl + ROPE
    w_in_big_t = _stack_rows([wi_t[:o_q], wi_t[o_ga:]])
    w_in_lat_t = _stack_rows([wi_t[o_q:o_ga], _quarter_turn(wi_t[o_pe:o_ga].T).T])

    inv = ROPE_THETA ** (-jnp.arange(0, ROPE, 2, dtype=F32) / ROPE)
    ang = positions[0].astype(F32)[:, None] * inv
    cos, sin = jnp.cos(ang), jnp.sin(ang)
    rope_k = jnp.concatenate([cos, cos, sin, sin], axis=1)
    softmax_scale = float(NOPE + ROPE) ** -0.5
    rope_q = jnp.concatenate([jnp.ones((s, NOPE), F32), rope_k], axis=1) * softmax_scale

    x2d, tgt = x[0], loss_target[0]
    g_pre1, g_post1, g_pre2, g_post2 = row(pre_norm1_g), row(post_norm1_g), row(pre_norm2_g), row(post_norm2_g)
    ln_g, ln_b, q_g, kv_g = row(gm_ln_g), row(gm_ln_b), row(q_norm_g), row(kv_norm_g)
    b_s_t = gm_b_s.T
    conv_bf = row(conv_b)

    h1 = _prenorm(x2d, g_pre1, scale1, shift1, "prenorm1")
    z_big, (g_uq, g_ukv, g_a) = _matmul(h1, w_in_big_t, mode="nt", out_dtype=BF16, name="mm_z_big", tm=s,
                                        comm=gather(["w_uq", "w_ukv", "w_branch_a"]))
    wq = _join_cols(whole("w_uq", g_uq)).reshape(ql, heads, NOPE + ROPE)
    w_q = jnp.concatenate([wq, _quarter_turn(wq[:, :, NOPE:])], axis=2).reshape(ql, heads * HEAD_W)
    w_kv = _join_cols(whole("w_ukv", g_ukv)).reshape(kvl, heads, 2, NOPE).transpose(0, 2, 1, 3)
    w_kv = w_kv.reshape(kvl, 2 * heads * NOPE)
    w_a = rows4(whole("w_branch_a", g_a))
    z_lat = _matmul(h1, w_in_lat_t, mode="nt", out_dtype=F32, name="mm_z_lat", tm=s, tn=1024)
    a_act = _gmlp_fwd(z_big, ln_g, ln_b, gm_w_s, b_s_t, "gmlp_fwd")
    qn, kvn, kr = _mla_prep(z_lat, q_g, kv_g, rope_k, "mla_prep")
    q_rot = _matmul(qn, w_q, mode="nn", out_dtype=BF16, name="mm_q", tm=s, tn=HEAD_W, mul=rope_q)
    kv_all = _matmul(kvn, w_kv, mode="nn", out_dtype=BF16, name="mm_kv", tm=s, tn=1024)
    (o_att, lse), (g_b, g_o, g_up) = _attn_fwd(q_rot, kv_all, kr, heads, "attn_fwd",
                                               comm=gather(["w_branch_b", "w_out", "w_up"]))
    w_b, w_o, w_upf = rows4(whole("w_branch_b", g_b)), rows4(whole("w_out", g_o)), whole("w_up", g_up)
    y_a = _matmul(a_act, w_a, mode="nn", out_dtype=BF16, name="mm_y_a", tm=s)
    y_b = _matmul(o_att, w_b, mode="nn", out_dtype=BF16, name="mm_y_b", tm=s)
    merged = _merge(z_big, y_a, y_b, "merge")
    y1 = _matmul(merged, w_o, mode="nn", out_dtype=F32, name="mm_y1", tm=s)
    x1, h2 = _post_pre(x2d, y1, gate1, g_post1, g_pre2, scale2, shift2, "post1_pre2")

    up_pre, (g_dn,) = _matmul(h2, w_upf, mode="nn", out_dtype=BF16, name="mm_up", tm=s, tn=1408,
                              comm=gather(["w_down"]))
    w_dn = rows4(whole("w_down", g_dn))
    act = _conv_fwd(up_pre, conv_wf, conv_bf, "conv_fwd")
    ffn = _matmul(act, w_dn, mode="nn", out_dtype=F32, name="mm_ffn", tm=s, tn=1024, tk=1408)

    dffn, dgate2, g_post2_grad, dx2, loss_part = _post_bwd(ffn, gate2, g_post2, "post2_bwd", xin=x1, target=tgt)
    loss = lax.psum(loss_part[0, 0], ("x", "y", "c"))
    place = jnp.stack([ic, chip]).astype(jnp.int32)
    rows_of = lambda g: g.reshape(N_CHIPS, g.shape[0] // N_CHIPS, g.shape[1])
    add_sibling = lambda names, gs, r1s: [_add_sibling(g, r1, place, "rs_add_sibling_" + n, by_cols=n == "w_in")
                                          for n, g, r1 in zip(names, gs, r1s)]
    add_chips = lambda names, s1s, r2s: [_add_chips(s1, r2, place, "rs_add_chips_" + n, by_cols=n == "w_in")
                                         for n, s1, r2 in zip(names, s1s, r2s)]
    gp_down = [rows_of(_matmul(act, dffn, mode="tn", out_dtype=BF16, name="mm_gw_down", tn=2048, tk=s))]
    dact, r1_down = _matmul(dffn, w_dn, mode="nt", out_dtype=BF16, name="mm_dact", tm=s, comm=_swap_comm(gp_down))
    s1_down = add_sibling(["w_down"], gp_down, r1_down)
    (dup, gcw_g, gcw_v, gcb_g, gcb_v), r2_down = _conv_bwd(up_pre, dact, conv_wf, conv_bf, "conv_bwd",
                                                            comm=_exchange_comm(s1_down))
    half_down = add_chips(["w_down"], s1_down, r2_down)
    dh2 = _matmul(dup, w_upf, mode="nt", out_dtype=F32, name="mm_dh2", tm=s, tn=1024, tk=1408)
    gw_up = _matmul(h2, dup, mode="tn", out_dtype=BF16, name="mm_gw_up", tm=1024, tn=1408, tk=s, out_groups=N_CHIPS)
    dx1, dshift2, dscale2, g_pre2_grad = _prenorm_bwd(x1, dh2, dx2, g_pre2, scale2, "prenorm2_bwd")

    dy1, dgate1, g_post1_grad = _post_bwd(y1, gate1, g_post1, "post1_bwd", dxo=dx1)
    dmerged = _matmul(dy1, w_o, mode="nt", out_dtype=BF16, name="mm_dmerged", tm=s)
    gw_out = _matmul(merged, dy1, mode="tn", out_dtype=BF16, name="mm_gw_out", tn=1024, tk=s)
    dy_a, dy_b, dz_big = _merge_bwd(dmerged, z_big, y_a, y_b, "merge_bwd")
    da = _matmul(dy_a, w_a, mode="nt", out_dtype=BF16, name="mm_da", tm=s)
    gw_a = _matmul(a_act, dy_a, mode="tn", out_dtype=BF16, name="mm_gw_a", tn=1024, tk=s)
    do = _matmul(dy_b, w_b, mode="nt", out_dtype=BF16, name="mm_do", tm=s)
    gw_b = _matmul(o_att, dy_b, mode="tn", out_dtype=BF16, name="mm_gw_b", tn=1024, tk=s)
    mid = ["w_up", "w_branch_a", "w_branch_b", "w_out"]
    gp_mid = [gw_up, rows_of(gw_a), rows_of(gw_b), rows_of(gw_out)]
    (dz_big, g_ws, g_bs_t, g_ln_g, g_ln_b), r1_mid = _gmlp_bwd(z_big, da, dz_big, ln_g, ln_b, gm_w_s, b_s_t,
                                                                "gmlp_bwd", comm=_swap_comm(gp_mid))
    s1_mid = add_sibling(mid, gp_mid, r1_mid)
    (dq, dk, dv), r2_up = _attn_bwd(q_rot, kv_all, kr, o_att, do, lse, heads, "attn_bwd",
                                    comm=_exchange_comm(s1_mid[:1]))
    dq_big, dkv, dkk = _mla_bwd_mid(dq, dk, dv, rope_q, rope_k, heads, "mla_bwd_mid")
    gw_q = _matmul(qn, dq_big, mode="tn", out_dtype=F32, name="mm_gw_q", tn=1024, tk=s)
    dqn = _matmul(dq_big, w_q, mode="nt", out_dtype=F32, name="mm_dqn", tm=s, tk=1024)
    gw_kv = _matmul(kvn, dkv, mode="tn", out_dtype=BF16, name="mm_gw_kv", tn=1024, tk=s)
    dkvn = _matmul(dkv, w_kv, mode="nt", out_dtype=F32, name="mm_dkvn", tm=s, tk=1024)
    dz_lat, g_q, g_kv = _mla_bwd_post(z_lat, dqn, dkvn, dkk, q_g, kv_g, "mla_bwd_post")

    partial = {
        "gm_ln_g": g_ln_g, "gm_ln_b": g_ln_b, "gm_w_s": g_ws, "gm_b_s": g_bs_t[:, :gm_b_s.shape[0]].T,
        "q_norm_g": g_q, "kv_norm_g": g_kv, "post_norm1_g": g_post1_grad, "pre_norm2_g": g_pre2_grad,
        "conv_w": jnp.concatenate([gcw_g, gcw_v], axis=1), "conv_b": jnp.concatenate([gcb_g, gcb_v], axis=1),
        "post_norm2_g": g_post2_grad,
    }
    flat = jnp.concatenate([partial[n].reshape(-1) for n in SMALL_PARTIAL])
    n_small = flat.shape[0]
    rows_small = -(-n_small // (LANES * SMALL_ROW_TILE)) * SMALL_ROW_TILE
    flat = jnp.pad(flat, (0, rows_small * LANES - n_small)).reshape(rows_small, LANES)

    def small_pack(prefix, source):
        v = jnp.concatenate([source[prefix + n].reshape(-1) for n in SMALL])
        rows = -(-v.shape[0] // (LANES * SUBLANES)) * SUBLANES
        return jnp.pad(v, (0, rows * LANES - v.shape[0])).reshape(rows, LANES)

    small_state = [small_pack(prefix, given) for prefix in ("", "m_", "v_")]

    dh1, r2_a_b = _matmul(dz_big, w_in_big_t, mode="nn", out_dtype=F32, name="mm_dh1_big", tm=s, tk=1024,
                          comm=_exchange_comm(s1_mid[1:3]))
    half_mid = add_chips(mid[:3], s1_mid[:3], list(r2_up) + list(r2_a_b))
    dh1 = _matmul(dz_lat, w_in_lat_t, mode="nn", out_dtype=F32, name="mm_dh1_lat", tm=s, tk=1024, add=dh1)
    gw_big_t, hosted = _matmul(dz_big, h1, mode="tn", out_dtype=BF16, name="mm_gw_in_big", tn=2048, tk=s,
                               comm=_join_comms([_share_comm(half_down + half_mid), _gather8_comm(flat),
                                                 _exchange_comm(s1_mid[3:])]))
    shared, small_all = hosted[:4], lax.dynamic_update_slice(hosted[4], flat[None], (dev, 0, 0))
    half_out = add_chips(mid[3:], s1_mid[3:], hosted[5:])
    small_sum = _sum_leading(small_all, "sum_small", after=small_state + [loss.reshape(1, 1)]).reshape(-1)
    small_grads, off = {}, 0
    for n in SMALL_PARTIAL:
        shape = (CONV_TAPS, 2 * ff) if n == "conv_w" else given[n].shape
        small_grads[n] = small_sum[off:off + partial[n].size].reshape(shape)
        off += partial[n].size
    small_grads["conv_w"] = lax.dynamic_slice(small_grads["conv_w"], (0, chip * conv_w.shape[1]), conv_w.shape)
    grads = dict(zip(["w_down"] + mid[:3], shared), **small_grads)
    gw_lat_t = _matmul(dz_lat, h1, mode="tn", out_dtype=F32, name="mm_gw_in_lat", tm=1024, tn=1024, tk=s)

    gq = gw_q.reshape(ql, heads, HEAD_W)
    gq_pe = gq[:, :, NOPE:NOPE + ROPE] + _quarter_turn_back(gq[:, :, NOPE + ROPE:])
    g_pe_t = gw_lat_t[ql + kvl:ql + kvl + ROPE] + _quarter_turn_back(gw_lat_t[ql + kvl + ROPE:].T).T
    last = ["w_in", "w_uq", "w_ukv"]
    gw_in_t = _stack_rows([gw_big_t[:o_q], gw_lat_t[:ql + kvl].astype(BF16), g_pe_t.astype(BF16), gw_big_t[o_q:]])
    gp_last = [
        gw_in_t.reshape(N_CHIPS, gw_in_t.shape[0] // N_CHIPS, d),
        _split_cols(jnp.concatenate([gq[:, :, :NOPE], gq_pe], axis=2).reshape(ql, heads * (NOPE + ROPE)).astype(BF16)),
        _split_cols(gw_kv.reshape(kvl, 2, heads, NOPE).transpose(0, 2, 1, 3).reshape(kvl, heads * 2 * NOPE)),
    ]
    (grad_x, dshift1, dscale1, g_pre1_grad), r1_last = _prenorm_bwd(x2d, dh1, dx1, g_pre1, scale1, "prenorm1_bwd",
                                                                    comm=_swap_comm(gp_last, by_cols=[0]))
    s1_last = add_sibling(last, gp_last, r1_last)

    dmod = jnp.concatenate([dshift1, dscale1, dgate1, dshift2, dscale2, dgate2, g_pre1_grad], axis=1)
    dmod_all = _all_gather(jnp.pad(dmod, ((0, SUBLANES - 1), (0, 0))), "gather_dmod")
    dmod_all = dmod_all.reshape(N_DEV, SUBLANES, (N_MOD + 1) * d)[:, 0]
    dmod_sum = _sum_leading(dmod_all.reshape(N_DEV, 1, (N_MOD + 1) * d), "sum_dmod")[0]
    grads["b_ada"], grads["pre_norm1_g"] = dmod_sum[:N_MOD * d], dmod_sum[N_MOD * d:]
    dmod_mine = lax.dynamic_slice(dmod_all, (0, chip * na), (N_DEV, na))
    grads["w_ada"] = _ada_bwd(c_all.T, dmod_mine, "ada_bwd")

    delta, new_m, new_v = {}, {}, {}

    def adamw(n, after=None):
        turn = (lambda a: a.T) if n == "w_in" else (lambda a: a)
        outs = _adamw(turn(given[n]), grads[n], turn(given["m_" + n]), turn(given["v_" + n]), "adamw_" + n,
                      after=after)
        grads[n] = turn(grads[n])
        delta[n], new_m[n], new_v[n] = (turn(o) for o in outs)

    exchange_last = _exchange_comm(s1_last)
    in_flight, token = _comm_split_start(exchange_last, "rs_exchange_last_start", after=[dmod_sum, small_sum])
    for n in ["w_ada", "w_down"] + mid[:3]:
        adamw(n, after=token)
    s1_last, r2_last = _comm_split_wait(exchange_last, in_flight, delta[mid[2]], "rs_exchange_last_wait")
    half_last = add_chips(last, s1_last, r2_last)
    late = last + mid[3:]
    grads.update(zip(late, _run_comm(_share_comm(half_last + half_out, by_cols=[0]), "rs_share_last")))
    for n in late:
        adamw(n)

    outs = _adamw(small_state[0], small_pack("", grads), small_state[1], small_state[2], "adamw_small")
    off = 0
    for n in SMALL:
        size = given[n].size
        for store, packed_out in zip((delta, new_m, new_v), outs):
            store[n] = packed_out.reshape(-1)[off:off + size].reshape(given[n].shape)
        off += size

    return (loss, grad_x[None], *[grads[n] for n in WEIGHTS], *[delta[n] for n in WEIGHTS],
            *[new_m[n] for n in WEIGHTS], *[new_v[n] for n in WEIGHTS])
```

```python
import functools

import jax
import jax.numpy as jnp
from jax import lax
from jax.experimental import pallas as pl
from jax.experimental.pallas import tpu as pltpu

F32 = jnp.float32
BF16 = jnp.bfloat16
MESH = pl.DeviceIdType.MESH
HBM = pltpu.HBM

EPS = 1e-6
NOPE, ROPE, VHEAD = 128, 64, 128
HEAD_W = NOPE + 2 * ROPE
ROPE_THETA = 10000.0
CONV_TAPS = 3
N_MOD = 6
N_CHIPS, N_CORES, N_DEV = 4, 2, 8
ADAM_LR, ADAM_B1, ADAM_B2, ADAM_EPS, ADAM_WD, ADAM_STEP = 0.001, 0.9, 0.999, 1e-08, 0.01, 10

LANES = 128
SUBLANES = 8
VMEM_LIMIT = 56 * 2**20
MIDDLE_STAGE_AT = 70
SMALL_ROW_TILE = 256

BIG = ("w_in", "w_branch_a", "w_uq", "w_ukv", "w_branch_b", "w_out", "w_up", "w_down")
WEIGHTS = ("w_ada", "b_ada", "pre_norm1_g", "w_in", "gm_ln_g", "gm_ln_b", "gm_w_s", "gm_b_s", "w_branch_a",
           "q_norm_g", "w_uq", "kv_norm_g", "w_ukv", "w_branch_b", "w_out", "post_norm1_g", "pre_norm2_g",
           "w_up", "conv_w", "conv_b", "w_down", "post_norm2_g")
SMALL_PARTIAL = ("gm_ln_g", "gm_ln_b", "gm_w_s", "gm_b_s", "q_norm_g", "kv_norm_g", "post_norm1_g",
                 "pre_norm2_g", "conv_w", "conv_b", "post_norm2_g")
SMALL = ("b_ada", "pre_norm1_g") + SMALL_PARTIAL


def _div_tile(n, cap, mult=LANES):
    t = (min(cap, n) // mult) * mult
    while t >= mult:
        if n % t == 0:
            return t
        t -= mult
    return n


def _params(**kw):
    return pltpu.CompilerParams(vmem_limit_bytes=VMEM_LIMIT, **kw)


def _row_spec(width):
    return pl.BlockSpec((1, width), lambda *_: (0, 0))


def _gelu(x):
    k = 0.7978845608028654
    return 0.5 * x * (1.0 + jnp.tanh(k * (x + 0.044715 * x * x * x)))


def _gelu_grad(x):
    k = 0.7978845608028654
    t = jnp.tanh(k * (x + 0.044715 * x * x * x))
    return 0.5 * (1.0 + t) + 0.5 * x * (1.0 - t * t) * k * (1.0 + 3.0 * 0.044715 * x * x)


def _sigmoid(x):
    return 0.5 * jnp.tanh(0.5 * x) + 0.5


def _dot(a, b, dims):
    return lax.dot_general(a, b, (dims, ((), ())), preferred_element_type=F32)


NN = ((1,), (0,))
NT = ((1,), (1,))
TN = ((0,), (0,))


def _logical(arr):
    if arr.ndim == 2:
        return arr.shape[0], arr.shape[1], arr.shape[1]
    return arr.shape[1], arr.shape[0] * arr.shape[2], arr.shape[2]


def _tile_spec(ndim, group_w, blk_rows, blk_cols, row_of, col_of):
    if ndim == 2:
        return pl.BlockSpec((blk_rows, blk_cols), lambda i, j, k: (row_of(i, j, k), col_of(i, j, k)))
    per = group_w // blk_cols
    return pl.BlockSpec((None, blk_rows, blk_cols),
                        lambda i, j, k: (col_of(i, j, k) // per, row_of(i, j, k), col_of(i, j, k) % per))


def _matmul(a, b, *, mode, out_dtype, name, tm=512, tn=512, tk=2048, mul=None, add=None, out_groups=None, comm=None):
    ar, ac, agw = _logical(a)
    br, bc, bgw = _logical(b)
    if mode == "nn":
        m, kd, n = ar, ac, bc
        m_w, k_w, n_w = (), (agw,), (bgw,)
    elif mode == "nt":
        m, kd, n = ar, ac, br
        m_w, k_w, n_w = (), (agw, bgw), ()
    else:
        m, kd, n = ac, ar, bc
        m_w, k_w, n_w = (agw,), (), (bgw,)
    if out_groups is not None:
        n_w = n_w + (n // out_groups,)
    tm = _div_tile(min((m,) + m_w), tm, LANES if mode == "tn" else SUBLANES)
    tn = _div_tile(min((n,) + n_w), tn)
    tk = _div_tile(min((kd,) + k_w), tk)
    assert all(w % tn == 0 for w in n_w) and all(w % tk == 0 for w in k_w) and all(w % tm == 0 for w in m_w)
    nk = kd // tk
    dims = {"nn": NN, "nt": NT, "tn": TN}[mode]
    gi, gj, gk = (lambda i, j, k: i), (lambda i, j, k: j), (lambda i, j, k: k)
    if mode == "nn":
        a_spec = _tile_spec(a.ndim, agw, tm, tk, gi, gk)
        b_spec = _tile_spec(b.ndim, bgw, tk, tn, gk, gj)
    elif mode == "nt":
        a_spec = _tile_spec(a.ndim, agw, tm, tk, gi, gk)
        b_spec = _tile_spec(b.ndim, bgw, tn, tk, gj, gk)
    else:
        a_spec = _tile_spec(a.ndim, agw, tk, tm, gk, gi)
        b_spec = _tile_spec(b.ndim, bgw, tk, tn, gk, gj)
    in_specs, operands = [a_spec, b_spec], [a, b]
    if mul is not None:
        assert mul.shape == (m, tn)
        in_specs.append(pl.BlockSpec((tm, tn), lambda i, j, k: (i, 0)))
        operands.append(mul)
    if add is not None:
        in_specs.append(pl.BlockSpec((tm, tn), lambda i, j, k: (i, j)))
        operands.append(add)

    def body(*refs):
        a_ref, b_ref = refs[0], refs[1]
        pos = 2
        mul_ref = add_ref = None
        if mul is not None:
            mul_ref, pos = refs[pos], pos + 1
        if add is not None:
            add_ref, pos = refs[pos], pos + 1
        o_ref = refs[pos]

        def finish(r):
            if mul_ref is not None:
                r = r * mul_ref[...]
            if add_ref is not None:
                r = r + add_ref[...]
            o_ref[...] = r.astype(out_dtype)

        part = _dot(a_ref[...], b_ref[...], dims)
        if nk == 1:
            finish(part)
        else:
            acc_ref = refs[pos + 1]
            k = pl.program_id(2)

            @pl.when(k == 0)
            def _():
                acc_ref[...] = part

            @pl.when(k > 0)
            def _():
                acc_ref[...] += part

            @pl.when(k == nk - 1)
            def _():
                finish(acc_ref[...])

    if out_groups is None:
        out_spec, out_dims = _tile_spec(2, n, tm, tn, gi, gj), (m, n)
    else:
        out_spec, out_dims = _tile_spec(3, n // out_groups, tm, tn, gi, gj), (out_groups, m, n // out_groups)
    return _call(body, operands, comm, name=name, grid=(m // tm, n // tn, nk), in_specs=in_specs, out_specs=out_spec,
                 out_shape=jax.ShapeDtypeStruct(out_dims, out_dtype),
                 scratch_shapes=[] if nk == 1 else [pltpu.VMEM((tm, tn), F32)])


def _accumulate(ref, value):
    @pl.when(pl.program_id(0) == 0)
    def _():
        ref[...] = value

    @pl.when(pl.program_id(0) > 0)
    def _():
        ref[...] += value


def _colsum(v):
    return jnp.sum(v, axis=0, keepdims=True)


def _rowmean(v):
    return jnp.mean(v, axis=-1, keepdims=True)


def _prenorm(x, g, scale, shift, name):
    s, d = x.shape
    tb = _div_tile(s, 256, SUBLANES)

    def body(x_ref, g_ref, sc_ref, sh_ref, h_ref):
        xv = x_ref[...]
        r = lax.rsqrt(_rowmean(xv * xv) + EPS)
        h_ref[...] = ((xv * r) * g_ref[...] * (1.0 + sc_ref[...]) + sh_ref[...]).astype(BF16)

    blk = pl.BlockSpec((tb, d), lambda i: (i, 0))
    return pl.pallas_call(
        body, name=name, grid=(s // tb,), in_specs=[blk, _row_spec(d), _row_spec(d), _row_spec(d)],
        out_specs=blk, out_shape=jax.ShapeDtypeStruct((s, d), BF16), compiler_params=_params(),
    )(x, g, scale, shift)


def _post_pre(x, y, gate, pg, g2, scale2, shift2, name):
    s, d = x.shape
    tb = _div_tile(s, 256, SUBLANES)

    def body(x_ref, y_ref, gate_ref, pg_ref, g2_ref, sc_ref, sh_ref, x1_ref, h2_ref):
        yv = y_ref[...]
        rp = lax.rsqrt(_rowmean(yv * yv) + EPS)
        x1 = x_ref[...] + gate_ref[...] * ((yv * rp) * pg_ref[...])
        x1_ref[...] = x1
        r2 = lax.rsqrt(_rowmean(x1 * x1) + EPS)
        h2_ref[...] = ((x1 * r2) * g2_ref[...] * (1.0 + sc_ref[...]) + sh_ref[...]).astype(BF16)

    blk = pl.BlockSpec((tb, d), lambda i: (i, 0))
    return pl.pallas_call(
        body, name=name, grid=(s // tb,), in_specs=[blk, blk] + [_row_spec(d)] * 5,
        out_specs=[blk, blk],
        out_shape=[jax.ShapeDtypeStruct((s, d), F32), jax.ShapeDtypeStruct((s, d), BF16)],
        compiler_params=_params(),
    )(x, y, gate, pg, g2, scale2, shift2)


def _post_bwd(y, gate, pg, name, *, dxo=None, xin=None, target=None):
    s, d = y.shape
    tb = _div_tile(s, 256, SUBLANES)
    from_loss = target is not None

    def body(*refs):
        if from_loss:
            y_ref, gate_ref, pg_ref, xin_ref, t_ref, dy_ref, dgate_ref, dpg_ref, dxo_ref, loss_ref = refs
        else:
            y_ref, gate_ref, pg_ref, dxo_in_ref, dy_ref, dgate_ref, dpg_ref = refs
        yv = y_ref[...]
        rp = lax.rsqrt(_rowmean(yv * yv) + EPS)
        yh = yv * rp
        fn = yh * pg_ref[...]
        gate = gate_ref[...]
        if from_loss:
            err = xin_ref[...] + gate * fn - t_ref[...]
            dxo = err * (1.0 / d)
            dxo_ref[...] = dxo
            part = 0.5 * jnp.sum(_rowmean(err * err), axis=0, keepdims=True)
            _accumulate(loss_ref, jnp.broadcast_to(part, loss_ref.shape))
        else:
            dxo = dxo_in_ref[...]
        _accumulate(dgate_ref, _colsum(dxo * fn))
        dfn = dxo * gate
        _accumulate(dpg_ref, _colsum(dfn * yh))
        dyh = dfn * pg_ref[...]
        dy_ref[...] = (rp * (dyh - yh * _rowmean(dyh * yh))).astype(BF16)

    blk = pl.BlockSpec((tb, d), lambda i: (i, 0))
    in_specs = [blk, _row_spec(d), _row_spec(d)]
    out_specs = [blk, _row_spec(d), _row_spec(d)]
    out_shape = [jax.ShapeDtypeStruct((s, d), BF16), jax.ShapeDtypeStruct((1, d), F32),
                 jax.ShapeDtypeStruct((1, d), F32)]
    if from_loss:
        operands = (y, gate, pg, xin, target)
        in_specs += [blk, blk]
        out_specs += [blk, _row_spec(LANES)]
        out_shape += [jax.ShapeDtypeStruct((s, d), F32), jax.ShapeDtypeStruct((1, LANES), F32)]
    else:
        operands = (y, gate, pg, dxo)
        in_specs += [blk]
    return pl.pallas_call(
        body, name=name, grid=(s // tb,), in_specs=in_specs, out_specs=out_specs, out_shape=out_shape,
        compiler_params=_params(),
    )(*operands)


def _prenorm_bwd(xin, dh, dres, g, scale, name, comm=None):
    s, d = xin.shape
    tb = _div_tile(s, 256, SUBLANES)

    def body(x_ref, dh_ref, dres_ref, g_ref, sc_ref, dx_ref, dshift_ref, dscale_ref, dg_ref):
        xv = x_ref[...]
        r = lax.rsqrt(_rowmean(xv * xv) + EPS)
        xn = xv * r
        dh = dh_ref[...]
        g1 = g_ref[...]
        s1 = 1.0 + sc_ref[...]
        _accumulate(dshift_ref, _colsum(dh))
        _accumulate(dscale_ref, _colsum(dh * xn * g1))
        _accumulate(dg_ref, _colsum(dh * xn * s1))
        dxn = dh * g1 * s1
        dx_ref[...] = dres_ref[...] + r * (dxn - xn * _rowmean(dxn * xn))

    blk = pl.BlockSpec((tb, d), lambda i: (i, 0))
    return _call(
        body, (xin, dh, dres, g, scale), comm, name=name, grid=(s // tb,),
        in_specs=[blk, blk, blk, _row_spec(d), _row_spec(d)],
        out_specs=[blk, _row_spec(d), _row_spec(d), _row_spec(d)],
        out_shape=[jax.ShapeDtypeStruct((s, d), F32)] + [jax.ShapeDtypeStruct((1, d), F32)] * 3)


def _merge(z_big, y_a, y_b, name):
    s, d = y_a.shape
    tb = _div_tile(s, 256, SUBLANES)

    def body(zg_ref, ya_ref, yb_ref, o_ref):
        ga, gb = zg_ref[:, :d].astype(F32), zg_ref[:, d:].astype(F32)
        o_ref[...] = (_sigmoid(ga) * ya_ref[...].astype(F32) + _sigmoid(gb) * yb_ref[...].astype(F32)).astype(BF16)

    blk = pl.BlockSpec((tb, d), lambda i: (i, 0))
    return pl.pallas_call(
        body, name=name, grid=(s // tb,), in_specs=[pl.BlockSpec((tb, 2 * d), lambda i: (i, 1)), blk, blk],
        out_specs=blk, out_shape=jax.ShapeDtypeStruct((s, d), BF16), compiler_params=_params(),
    )(z_big, y_a, y_b)


def _merge_bwd(dmerged, z_big, y_a, y_b, name):
    s, d = y_a.shape
    tb = _div_tile(s, 256, SUBLANES)

    def body(dm_ref, zg_ref, ya_ref, yb_ref, dya_ref, dyb_ref, dz_ref):
        dm = dm_ref[...].astype(F32)
        sa, sb = _sigmoid(zg_ref[:, :d].astype(F32)), _sigmoid(zg_ref[:, d:].astype(F32))
        dya_ref[...] = (dm * sa).astype(BF16)
        dyb_ref[...] = (dm * sb).astype(BF16)
        dz_ref[:, :d] = (dm * ya_ref[...].astype(F32) * sa * (1.0 - sa)).astype(BF16)
        dz_ref[:, d:] = (dm * yb_ref[...].astype(F32) * sb * (1.0 - sb)).astype(BF16)

    blk = pl.BlockSpec((tb, d), lambda i: (i, 0))
    wide = pl.BlockSpec((tb, 2 * d), lambda i: (i, 1))
    return pl.pallas_call(
        body, name=name, grid=(s // tb,), in_specs=[blk, wide, blk, blk], out_specs=[blk, blk, wide],
        out_shape=[jax.ShapeDtypeStruct((s, d), BF16), jax.ShapeDtypeStruct((s, d), BF16),
                   jax.ShapeDtypeStruct((s, 4 * d), BF16)],
        compiler_params=_params(),
    )(dmerged, z_big, y_a, y_b)


def _causal_mask(ch):
    q = lax.broadcasted_iota(jnp.int32, (ch, ch), 0)
    p = lax.broadcasted_iota(jnp.int32, (ch, ch), 1)
    return (p <= q).astype(F32)


def _gmlp_norm(zc, lng, lnb, gw):
    u_pre, v_pre = zc[:, :gw], zc[:, gw:]
    vg = _gelu(v_pre)
    mu = _rowmean(vg)
    cen = vg - mu
    rstd = lax.rsqrt(_rowmean(cen * cen) + EPS)
    vhat = cen * rstd
    return u_pre, v_pre, _gelu(u_pre), vhat, rstd, vhat * lng + lnb


def _gmlp_fwd(z_big, ln_g, ln_b, w_s, b_s_t, name):
    s = z_big.shape[0]
    groups, ch, _ = w_s.shape
    gw = ln_g.shape[1]
    gd = gw // groups

    def body(z_ref, lng_ref, lnb_ref, ws_ref, bt_ref, a_ref):
        _, _, u, _, _, vn = _gmlp_norm(z_ref[...].astype(F32), lng_ref[...], lnb_ref[...], gw)
        mask = _causal_mask(ch)
        for g in range(groups):
            cols = slice(g * gd, (g + 1) * gd)
            wm = (ws_ref[g] * mask).astype(BF16)
            mixed = _dot(wm, vn[:, cols].astype(BF16), NN) + bt_ref[:, g:g + 1]
            a_ref[:, cols] = (u[:, cols] * mixed).astype(BF16)

    return pl.pallas_call(
        body, name=name, grid=(s // ch,),
        in_specs=[pl.BlockSpec((ch, 2 * gw), lambda n: (n, 0)), _row_spec(gw), _row_spec(gw),
                  pl.BlockSpec((groups, ch, ch), lambda n: (0, 0, 0)), pl.BlockSpec((ch, groups), lambda n: (0, 0))],
        out_specs=pl.BlockSpec((ch, gw), lambda n: (n, 0)),
        out_shape=jax.ShapeDtypeStruct((s, gw), BF16), compiler_params=_params(),
    )(z_big, ln_g, ln_b, w_s, b_s_t)


def _gmlp_bwd(z_big, da, dz_big, ln_g, ln_b, w_s, b_s_t, name, comm=None):
    s = z_big.shape[0]
    groups, ch, _ = w_s.shape
    gw = ln_g.shape[1]
    gd = gw // groups

    def body(z_ref, da_ref, dzin_ref, lng_ref, lnb_ref, ws_ref, bt_ref, dz_ref, gws_ref, gbt_ref, glng_ref, glnb_ref):
        del dzin_ref
        lng = lng_ref[...]
        u_pre, v_pre, u, vhat, rstd, vn = _gmlp_norm(z_ref[...].astype(F32), lng, lnb_ref[...], gw)
        da = da_ref[...].astype(F32)
        mask = _causal_mask(ch)
        first = pl.program_id(0) == 0
        dvn_parts = []
        lane = lax.broadcasted_iota(jnp.int32, (ch, LANES), 1)
        gb = jnp.zeros((ch, LANES), F32)
        for g in range(groups):
            cols = slice(g * gd, (g + 1) * gd)
            wm = (ws_ref[g] * mask).astype(BF16)
            vn_g = vn[:, cols].astype(BF16)
            mixed = _dot(wm, vn_g, NN) + bt_ref[:, g:g + 1]
            dz_ref[:, cols] = (da[:, cols] * mixed * _gelu_grad(u_pre[:, cols])).astype(BF16)
            dmixed = da[:, cols] * u[:, cols]
            dm16 = dmixed.astype(BF16)
            dvn_parts.append(_dot(wm, dm16, TN))
            gws = _dot(dm16, vn_g, NT) * mask

            @pl.when(first)
            def _(g=g, gws=gws):
                gws_ref[g] = gws

            @pl.when(jnp.logical_not(first))
            def _(g=g, gws=gws):
                gws_ref[g] += gws

            gb = gb + jnp.where(lane == g, jnp.sum(dmixed, axis=1, keepdims=True), 0.0)
        _accumulate(gbt_ref, gb)
        dvn = jnp.concatenate(dvn_parts, axis=1)
        _accumulate(glnb_ref, _colsum(dvn))
        _accumulate(glng_ref, _colsum(dvn * vhat))
        dvh = dvn * lng
        dvg = rstd * (dvh - _rowmean(dvh) - vhat * _rowmean(dvh * vhat))
        dz_ref[:, gw:] = (dvg * _gelu_grad(v_pre)).astype(BF16)

    zspec = pl.BlockSpec((ch, 2 * gw), lambda n: (n, 0))
    return _call(
        body, (z_big, da, dz_big, ln_g, ln_b, w_s, b_s_t), comm, name=name, grid=(s // ch,),
        in_specs=[zspec, pl.BlockSpec((ch, gw), lambda n: (n, 0)), pl.BlockSpec(memory_space=HBM),
                  _row_spec(gw), _row_spec(gw), pl.BlockSpec((groups, ch, ch), lambda n: (0, 0, 0)),
                  pl.BlockSpec((ch, groups), lambda n: (0, 0))],
        out_specs=[zspec, pl.BlockSpec((groups, ch, ch), lambda n: (0, 0, 0)),
                   pl.BlockSpec((ch, LANES), lambda n: (0, 0)), _row_spec(gw), _row_spec(gw)],
        out_shape=[jax.ShapeDtypeStruct(dz_big.shape, BF16), jax.ShapeDtypeStruct((groups, ch, ch), F32),
                   jax.ShapeDtypeStruct((ch, LANES), F32), jax.ShapeDtypeStruct((1, gw), F32),
                   jax.ShapeDtypeStruct((1, gw), F32)],
        input_output_aliases={2: 0})


def _mla_prep(z_lat, q_g, kv_g, rope_k, name):
    s, latw = z_lat.shape
    ql, kvl = q_g.shape[1], kv_g.shape[1]
    tb = _div_tile(s, 256, SUBLANES)

    def body(z_ref, qg_ref, kvg_ref, t_ref, qn_ref, kvn_ref, kr_ref):
        q = z_ref[:, :ql]
        qn_ref[...] = ((q * lax.rsqrt(_rowmean(q * q) + EPS)) * qg_ref[...]).astype(BF16)
        kv = z_ref[:, ql:ql + kvl]
        kvn_ref[...] = ((kv * lax.rsqrt(_rowmean(kv * kv) + EPS)) * kvg_ref[...]).astype(BF16)
        kk = z_ref[:, ql + kvl:] * t_ref[...]
        kr_ref[...] = (kk + pltpu.roll(kk, ROPE, axis=1)).astype(BF16)

    return pl.pallas_call(
        body, name=name, grid=(s // tb,),
        in_specs=[pl.BlockSpec((tb, latw), lambda i: (i, 0)), _row_spec(ql), _row_spec(kvl),
                  pl.BlockSpec((tb, 2 * ROPE), lambda i: (i, 0))],
        out_specs=[pl.BlockSpec((tb, ql), lambda i: (i, 0)), pl.BlockSpec((tb, kvl), lambda i: (i, 0)),
                   pl.BlockSpec((tb, 2 * ROPE), lambda i: (i, 0))],
        out_shape=[jax.ShapeDtypeStruct((s, ql), BF16), jax.ShapeDtypeStruct((s, kvl), BF16),
                   jax.ShapeDtypeStruct((s, 2 * ROPE), BF16)],
        compiler_params=_params(),
    )(z_lat, q_g, kv_g, rope_k)


def _attn_fwd(q, kv, kr, heads, name, comm=None):
    s = q.shape[0]
    t = _div_tile(s, 512)
    nb = s // t
    hp = 2 if heads % 2 == 0 else 1

    def body(q_ref, k_ref, kr_ref, v_ref, o_ref, lse_ref, m_ref, l_ref, acc_ref):
        i, j = pl.program_id(1), pl.program_id(2)

        @pl.when(j == 0)
        def _():
            m_ref[...] = jnp.full(m_ref.shape, -1e30, F32)
            l_ref[...] = jnp.zeros(l_ref.shape, F32)
            acc_ref[...] = jnp.zeros(acc_ref.shape, F32)

        def update(h, rows, n_keys, on_diagonal):
            vc = slice(h * VHEAD, (h + 1) * VHEAD)
            k_full = jnp.concatenate([k_ref[:n_keys, h * NOPE:(h + 1) * NOPE], kr_ref[:n_keys, :]], axis=1)
            sc = _dot(q_ref[rows, h * HEAD_W:(h + 1) * HEAD_W], k_full, NT)
            if on_diagonal:
                row_pos = rows.start + lax.broadcasted_iota(jnp.int32, sc.shape, 0)
                sc = jnp.where(lax.broadcasted_iota(jnp.int32, sc.shape, 1) <= row_pos, sc, -1e30)
            m_old = m_ref[h, rows, :]
            m_new = jnp.maximum(m_old, jnp.max(sc, axis=-1, keepdims=True))
            p = jnp.exp(sc - m_new)
            alpha = jnp.exp(m_old - m_new)
            l_new = alpha * l_ref[h, rows, :] + jnp.sum(p, axis=-1, keepdims=True)
            acc = alpha * acc_ref[rows, vc] + _dot(p.astype(BF16), v_ref[:n_keys, vc], NN)
            if on_diagonal:
                o_ref[rows, vc] = (acc / l_new).astype(BF16)
                lse_ref[h, rows, :] = jnp.broadcast_to(m_new + jnp.log(l_new), (rows.stop - rows.start, LANES))
            else:
                m_ref[h, rows, :], l_ref[h, rows, :], acc_ref[rows, vc] = m_new, l_new, acc

        def below_diagonal():
            for h in range(hp):
                update(h, slice(0, t), t, False)

        def on_diagonal():
            for h in range(hp):
                update(h, slice(0, t // 2), t // 2, True)
                update(h, slice(t // 2, t), t, True)

        pl.when(j < i)(below_diagonal)
        pl.when(j == i)(on_diagonal)

    kidx = lambda off: (lambda h, i, j: (jnp.minimum(i, j), off(h)))
    return _call(
        body, (q, kv, kr, kv), comm, name=name, grid=(heads // hp, nb, nb),
        in_specs=[pl.BlockSpec((t, hp * HEAD_W), lambda h, i, j: (i, h)),
                  pl.BlockSpec((t, hp * NOPE), kidx(lambda h: h)),
                  pl.BlockSpec((t, 2 * ROPE), kidx(lambda h: 0)),
                  pl.BlockSpec((t, hp * VHEAD), kidx(lambda h: heads // hp + h))],
        out_specs=[pl.BlockSpec((t, hp * VHEAD), lambda h, i, j: (i, h)),
                   pl.BlockSpec((hp, t, LANES), lambda h, i, j: (h, i, 0))],
        out_shape=[jax.ShapeDtypeStruct((s, heads * VHEAD), BF16), jax.ShapeDtypeStruct((heads, s, LANES), F32)],
        scratch_shapes=[pltpu.VMEM((hp, t, 1), F32), pltpu.VMEM((hp, t, 1), F32), pltpu.VMEM((t, hp * VHEAD), F32)])


def _attn_bwd(q, kv, kr, o, do, lse, heads, name, comm=None):
    s = q.shape[0]
    t = _div_tile(s, 512)
    nb = s // t
    hp = 2 if heads % 2 == 0 else 1

    def body(q_ref, k_ref, kr_ref, v_ref, o_ref, do_ref, lse_ref, dq_ref, dk_ref, dv_ref, dk_acc, dv_acc):
        j, i = pl.program_id(1), pl.program_id(2)

        @pl.when(jnp.logical_and(j == 0, i == 0))
        def _():
            dq_ref[...] = jnp.zeros(dq_ref.shape, F32)

        def update(h, rows, n_keys, on_diagonal, assign):
            qc, kc, vc = (slice(h * w, (h + 1) * w) for w in (HEAD_W, NOPE, VHEAD))
            n_rows = rows.stop - rows.start
            qv, do_v = q_ref[rows, qc], do_ref[rows, vc]
            k_full = jnp.concatenate([k_ref[:n_keys, kc], kr_ref[:n_keys, :]], axis=1)
            sc = _dot(qv, k_full, NT)
            if on_diagonal:
                row_pos = rows.start + lax.broadcasted_iota(jnp.int32, sc.shape, 0)
                sc = jnp.where(lax.broadcasted_iota(jnp.int32, sc.shape, 1) <= row_pos, sc, -1e30)
            p = jnp.exp(sc - lse_ref[h, rows, :1])
            dp = _dot(do_v, v_ref[:n_keys, vc], NT)
            delta = jnp.sum(do_v.astype(F32) * o_ref[rows, vc].astype(F32), axis=-1, keepdims=True)
            ds = (p * (dp - delta)).astype(BF16)
            dq_ref[pl.ds(pl.multiple_of(i * t + rows.start, n_rows), n_rows), qc] += _dot(ds, k_full, NN)
            dv_part, dk_part = _dot(p.astype(BF16), do_v, TN), _dot(ds, qv, TN)
            if assign:
                dv_acc[:n_keys, vc], dk_acc[:n_keys, qc] = dv_part, dk_part
            else:
                dv_acc[:n_keys, vc] += dv_part
                dk_acc[:n_keys, qc] += dk_part

        def on_diagonal():
            for h in range(hp):
                update(h, slice(t // 2, t), t, True, True)
                update(h, slice(0, t // 2), t // 2, True, False)

        def below_diagonal():
            for h in range(hp):
                update(h, slice(0, t), t, False, False)

        pl.when(i == j)(on_diagonal)
        pl.when(i > j)(below_diagonal)

        @pl.when(i == nb - 1)
        def _():
            dk_ref[...] = dk_acc[...].astype(BF16)
            dv_ref[...] = dv_acc[...].astype(BF16)

    qidx = lambda h, j, i: (jnp.maximum(i, j), h)
    return _call(
        body, (q, kv, kr, kv, o, do, lse), comm, name=name, grid=(heads // hp, nb, nb),
        in_specs=[pl.BlockSpec((t, hp * HEAD_W), qidx),
                  pl.BlockSpec((t, hp * NOPE), lambda h, j, i: (j, h)),
                  pl.BlockSpec((t, 2 * ROPE), lambda h, j, i: (j, 0)),
                  pl.BlockSpec((t, hp * VHEAD), lambda h, j, i: (j, heads // hp + h)),
                  pl.BlockSpec((t, hp * VHEAD), qidx), pl.BlockSpec((t, hp * VHEAD), qidx),
                  pl.BlockSpec((hp, t, LANES), lambda h, j, i: (h, jnp.maximum(i, j), 0))],
        out_specs=[pl.BlockSpec((s, hp * HEAD_W), lambda h, j, i: (0, h)),
                   pl.BlockSpec((t, hp * HEAD_W), lambda h, j, i: (j, h)),
                   pl.BlockSpec((t, hp * VHEAD), lambda h, j, i: (j, h))],
        out_shape=[jax.ShapeDtypeStruct((s, heads * HEAD_W), F32), jax.ShapeDtypeStruct((s, heads * HEAD_W), BF16),
                   jax.ShapeDtypeStruct((s, heads * VHEAD), BF16)],
        scratch_shapes=[pltpu.VMEM((t, hp * HEAD_W), F32), pltpu.VMEM((t, hp * VHEAD), F32)])


def _mla_bwd_mid(dq, dk, dv, rope_q, rope_k, heads, name):
    s = dq.shape[0]
    tb = _div_tile(s, 256, SUBLANES)

    def body(dq_ref, dk_ref, dv_ref, tq_ref, tk_ref, dqb_ref, dkv_ref, dkk_ref):
        tq = tq_ref[...]
        dkr = jnp.zeros((tb, 2 * ROPE), F32)
        for h in range(heads):
            cols = slice(h * HEAD_W, (h + 1) * HEAD_W)
            dqb_ref[:, cols] = (dq_ref[:, cols] * tq).astype(BF16)
            dkv_ref[:, h * NOPE:(h + 1) * NOPE] = dk_ref[:, h * HEAD_W:h * HEAD_W + NOPE]
            dkr = dkr + dk_ref[:, h * HEAD_W + NOPE:(h + 1) * HEAD_W].astype(F32)
        dkv_ref[:, heads * NOPE:] = dv_ref[...]
        dkk_ref[...] = (dkr + pltpu.roll(dkr, ROPE, axis=1)) * tk_ref[...]

    wq, wv = heads * HEAD_W, heads * VHEAD
    return pl.pallas_call(
        body, name=name, grid=(s // tb,),
        in_specs=[pl.BlockSpec((tb, wq), lambda i: (i, 0)), pl.BlockSpec((tb, wq), lambda i: (i, 0)),
                  pl.BlockSpec((tb, wv), lambda i: (i, 0)), pl.BlockSpec((tb, HEAD_W), lambda i: (i, 0)),
                  pl.BlockSpec((tb, 2 * ROPE), lambda i: (i, 0))],
        out_specs=[pl.BlockSpec((tb, wq), lambda i: (i, 0)), pl.BlockSpec((tb, heads * NOPE + wv), lambda i: (i, 0)),
                   pl.BlockSpec((tb, 2 * ROPE), lambda i: (i, 0))],
        out_shape=[jax.ShapeDtypeStruct((s, wq), BF16), jax.ShapeDtypeStruct((s, heads * NOPE + wv), BF16),
                   jax.ShapeDtypeStruct((s, 2 * ROPE), F32)],
        compiler_params=_params(),
    )(dq, dk, dv, rope_q, rope_k)


def _mla_bwd_post(z_lat, dqn, dkvn, dkk, q_g, kv_g, name):
    s, latw = z_lat.shape
    ql, kvl = q_g.shape[1], kv_g.shape[1]
    tb = _div_tile(s, 256, SUBLANES)

    def norm_bwd(xv, dn, g, dg_ref):
        r = lax.rsqrt(_rowmean(xv * xv) + EPS)
        xh = xv * r
        _accumulate(dg_ref, _colsum(dn * xh))
        dxh = dn * g
        return r * (dxh - xh * _rowmean(dxh * xh))

    def body(z_ref, dqn_ref, dkvn_ref, dkk_ref, qg_ref, kvg_ref, dz_ref, gq_ref, gkv_ref):
        dz_ref[:, :ql] = norm_bwd(z_ref[:, :ql], dqn_ref[...], qg_ref[...], gq_ref).astype(BF16)
        dz_ref[:, ql:ql + kvl] = norm_bwd(z_ref[:, ql:ql + kvl], dkvn_ref[...], kvg_ref[...], gkv_ref).astype(BF16)
        dz_ref[:, ql + kvl:] = dkk_ref[...].astype(BF16)

    return pl.pallas_call(
        body, name=name, grid=(s // tb,),
        in_specs=[pl.BlockSpec((tb, latw), lambda i: (i, 0)), pl.BlockSpec((tb, ql), lambda i: (i, 0)),
                  pl.BlockSpec((tb, kvl), lambda i: (i, 0)), pl.BlockSpec((tb, 2 * ROPE), lambda i: (i, 0)),
                  _row_spec(ql), _row_spec(kvl)],
        out_specs=[pl.BlockSpec((tb, latw), lambda i: (i, 0)), _row_spec(ql), _row_spec(kvl)],
        out_shape=[jax.ShapeDtypeStruct((s, latw), BF16), jax.ShapeDtypeStruct((1, ql), F32),
                   jax.ShapeDtypeStruct((1, kvl), F32)],
        compiler_params=_params(),
    )(z_lat, dqn, dkvn, dkk, q_g, kv_g)


CONV_ROWS = 128
CONV_HALO = 16


def _row_steps(n_rows, step):
    step(0, True)
    if n_rows > CONV_ROWS:
        def later(i, carry):
            step(pl.multiple_of(i * CONV_ROWS, CONV_ROWS), False)
            return carry
        lax.fori_loop(1, n_rows // CONV_ROWS, later, 0)


def _conv_taps(pre_ref, r0, first):
    if first:
        win = jnp.concatenate([jnp.zeros((CONV_HALO, pre_ref.shape[1]), F32), pre_ref[0:CONV_ROWS, :].astype(F32)])
    else:
        win = pre_ref[pl.ds(pl.multiple_of(r0 - CONV_HALO, CONV_HALO), CONV_ROWS + CONV_HALO), :].astype(F32)
    return win[CONV_HALO:], pltpu.roll(win, 1, axis=0)[CONV_HALO:], pltpu.roll(win, 2, axis=0)[CONV_HALO:]


def _conv(taps, w_ref, b_ref):
    return w_ref[2:3, :] * taps[0] + w_ref[1:2, :] * taps[1] + w_ref[0:1, :] * taps[2] + b_ref[...]


def _conv_fwd(up_pre, conv_w, conv_b, name):
    s, ff2 = up_pre.shape
    ff = ff2 // 2
    tc = _div_tile(ff, 256)
    nb = ff // tc
    assert s % CONV_ROWS == 0

    def body(pg_ref, pv_ref, wg_ref, wv_ref, bg_ref, bv_ref, act_ref):
        def step(r0, first):
            gate = _conv(_conv_taps(pg_ref, r0, first), wg_ref, bg_ref)
            val = _conv(_conv_taps(pv_ref, r0, first), wv_ref, bv_ref)
            act_ref[pl.ds(r0, CONV_ROWS), :] = (gate * _sigmoid(gate) * val).astype(BF16)

        _row_steps(s, step)

    def col(rows, off):
        return pl.BlockSpec((rows, tc), lambda j: (0, j + off))

    return pl.pallas_call(
        body, name=name, grid=(nb,),
        in_specs=[col(s, 0), col(s, nb), col(CONV_TAPS, 0), col(CONV_TAPS, nb), col(1, 0), col(1, nb)],
        out_specs=col(s, 0), out_shape=jax.ShapeDtypeStruct((s, ff), BF16), compiler_params=_params(),
    )(up_pre, up_pre, conv_w, conv_w, conv_b, conv_b)


def _conv_bwd(up_pre, dact, conv_w, conv_b, name, comm=None):
    s, ff2 = up_pre.shape
    ff = ff2 // 2
    tc = _div_tile(ff, 256)
    nb = ff // tc
    assert s % CONV_ROWS == 0

    def body(pg_ref, pv_ref, da_ref, wg_ref, wv_ref, bg_ref, bv_ref, dup_ref, gwg_ref, gwv_ref, gbg_ref, gbv_ref,
             dxg_ref, dxv_ref):
        for ref in (gwg_ref, gwv_ref, gbg_ref, gbv_ref):
            ref[...] = jnp.zeros(ref.shape, F32)
        for ref in (dxg_ref, dxv_ref):
            ref[s:s + SUBLANES, :] = jnp.zeros((SUBLANES, tc), F32)

        def sums(taps, dx, gw_ref, gb_ref):
            gb_ref[...] += _colsum(dx)
            for k in range(CONV_TAPS):
                gw_ref[k:k + 1, :] += _colsum(dx * taps[CONV_TAPS - 1 - k])

        def forward(r0, first):
            rows = pl.ds(r0, CONV_ROWS)
            taps_g, taps_v = _conv_taps(pg_ref, r0, first), _conv_taps(pv_ref, r0, first)
            gate, val = _conv(taps_g, wg_ref, bg_ref), _conv(taps_v, wv_ref, bv_ref)
            da = da_ref[rows, :].astype(F32)
            sg = _sigmoid(gate)
            dxv, dxg = da * gate * sg, da * val * sg * (1.0 + gate * (1.0 - sg))
            dxv_ref[rows, :], dxg_ref[rows, :] = dxv, dxg
            sums(taps_v, dxv, gwv_ref, gbv_ref)
            sums(taps_g, dxg, gwg_ref, gbg_ref)

        def backward(r0, first):
            del first
            n = CONV_ROWS + SUBLANES
            for dx_ref, w_ref, out_ref in ((dxg_ref, wg_ref, dup_ref.at[0]), (dxv_ref, wv_ref, dup_ref.at[1])):
                win = dx_ref[pl.ds(r0, n), :]
                ahead1 = pltpu.roll(win, n - 1, axis=0)[:CONV_ROWS]
                ahead2 = pltpu.roll(win, n - 2, axis=0)[:CONV_ROWS]
                out_ref[pl.ds(r0, CONV_ROWS), :] = (w_ref[2:3, :] * win[:CONV_ROWS] + w_ref[1:2, :] * ahead1
                                                    + w_ref[0:1, :] * ahead2).astype(BF16)

        _row_steps(s, forward)
        _row_steps(s, backward)

    def col(rows, off):
        return pl.BlockSpec((rows, tc), lambda j: (0, j + off))

    return _call(
        body, (up_pre, up_pre, dact, conv_w, conv_w, conv_b, conv_b), comm, name=name, grid=(nb,),
        in_specs=[col(s, 0), col(s, nb), col(s, 0), col(CONV_TAPS, 0), col(CONV_TAPS, nb), col(1, 0), col(1, nb)],
        out_specs=[pl.BlockSpec((2, s, tc), lambda j: (0, 0, j)), col(CONV_TAPS, 0), col(CONV_TAPS, 0),
                   col(1, 0), col(1, 0)],
        out_shape=[jax.ShapeDtypeStruct((2, s, ff), BF16)] + [jax.ShapeDtypeStruct((CONV_TAPS, ff), F32)] * 2
        + [jax.ShapeDtypeStruct((1, ff), F32)] * 2,
        scratch_shapes=[pltpu.VMEM((s + SUBLANES, tc), F32)] * 2)


def _ada_fwd(c_all, w, b, name):
    nseq, d = c_all.shape
    na = w.shape[1]
    tn = _div_tile(na, 512)

    def body(c_ref, w_ref, b_ref, o_ref):
        cv = c_ref[...]
        sc = cv * _sigmoid(cv)
        o_ref[...] = jnp.dot(sc, w_ref[...], preferred_element_type=F32, precision=lax.Precision.HIGHEST) + b_ref[...]

    return pl.pallas_call(
        body, name=name, grid=(na // tn,),
        in_specs=[pl.BlockSpec((nseq, d), lambda j: (0, 0)), pl.BlockSpec((d, tn), lambda j: (0, j)),
                  pl.BlockSpec((1, tn), lambda j: (0, j))],
        out_specs=pl.BlockSpec((nseq, tn), lambda j: (0, j)),
        out_shape=jax.ShapeDtypeStruct((nseq, na), F32), compiler_params=_params(),
    )(c_all, w, b)


def _ada_bwd(c_all_t, dmod, name):
    d, nseq = c_all_t.shape
    na = dmod.shape[1]
    tm, tn = _div_tile(d, 256, SUBLANES), _div_tile(na, 512)

    def body(c_ref, dm_ref, o_ref):
        cv = c_ref[...]
        sc = cv * _sigmoid(cv)
        acc = sc[:, 0:1] * dm_ref[0:1, :]
        for bi in range(1, nseq):
            acc = acc + sc[:, bi:bi + 1] * dm_ref[bi:bi + 1, :]
        o_ref[...] = acc

    return pl.pallas_call(
        body, name=name, grid=(d // tm, na // tn),
        in_specs=[pl.BlockSpec((tm, nseq), lambda i, j: (i, 0)), pl.BlockSpec((nseq, tn), lambda i, j: (0, j))],
        out_specs=pl.BlockSpec((tm, tn), lambda i, j: (i, j)),
        out_shape=jax.ShapeDtypeStruct((d, na), F32), compiler_params=_params(),
    )(c_all_t, dmod)


def _adamw(w, g, m, v, name, comm=None, after=None):
    rows, cols = w.shape
    tb = _div_tile(rows, max(SUBLANES, (256 * 1024) // cols // SUBLANES * SUBLANES), SUBLANES)
    c1 = 1.0 / (1.0 - ADAM_B1 ** ADAM_STEP)
    c2 = 1.0 / (1.0 - ADAM_B2 ** ADAM_STEP)

    def body(*refs):
        w_ref, g_ref, m_ref, v_ref = refs[:4]
        d_ref, nm_ref, nv_ref = refs[-3:]
        gv = g_ref[...]
        nm = ADAM_B1 * m_ref[...] + (1.0 - ADAM_B1) * gv
        nv = ADAM_B2 * v_ref[...] + (1.0 - ADAM_B2) * (gv * gv)
        nm_ref[...] = nm
        nv_ref[...] = nv
        d_ref[...] = -ADAM_LR * ((nm * c1) / (jnp.sqrt(nv * c2) + ADAM_EPS) + ADAM_WD * w_ref[...])

    blk = pl.BlockSpec((tb, cols), lambda i: (i, 0))
    operands, in_specs = (w, g, m, v), [blk] * 4
    if after is not None:
        operands, in_specs = operands + (after,), in_specs + [pl.BlockSpec(after.shape, lambda i: (0, 0))]
    return _call(body, operands, comm, name=name, grid=(rows // tb,), in_specs=in_specs, out_specs=[blk] * 3,
                 out_shape=[jax.ShapeDtypeStruct((rows, cols), F32)] * 3)


def _sum_leading(parts, name, after=()):
    n, rows, cols = parts.shape
    tb = _div_tile(rows, 512, SUBLANES)

    def body(p_ref, *rest):
        o_ref = rest[-1]
        acc = p_ref[0]
        for k in range(1, n):
            acc = acc + p_ref[k]
        o_ref[...] = acc

    return pl.pallas_call(
        body, name=name, grid=(rows // tb,),
        in_specs=[pl.BlockSpec((n, tb, cols), lambda i: (0, i, 0))] + [pl.BlockSpec(memory_space=pl.ANY)] * len(after),
        out_specs=pl.BlockSpec((tb, cols), lambda i: (i, 0)),
        out_shape=jax.ShapeDtypeStruct((rows, cols), F32), compiler_params=_params(),
    )(parts, *after)


def _place():
    x, y, c = lax.axis_index("x"), lax.axis_index("y"), lax.axis_index("c")
    return x, y, c, [(1 - x, y), (x, 1 - y), (1 - x, 1 - y)]


def _all_gather(block, name):
    m_per, n = block.shape

    def body(x_ref, out_ref, send_sems, recv_sems, local_sem):
        x, y, c, chips = _place()
        me, sibling = (x, y, c), (x, y, 1 - c)

        def rows(px, py, pc):
            return out_ref.at[pl.ds((4 * px + 2 * py + pc) * m_per, m_per), :]

        def copy(k, blk, to, src=None):
            return pltpu.make_async_remote_copy(
                src_ref=rows(*blk) if src is None else src, dst_ref=rows(*blk), send_sem=send_sems.at[k],
                recv_sem=recv_sems.at[k], device_id=to, device_id_type=MESH)

        mine = pltpu.make_async_copy(x_ref, rows(*me), local_sem)
        mine.start()
        first = [copy(0, me, sibling, src=x_ref)]
        first += [copy(1 + j, me, (*chip, c), src=x_ref) for j, chip in enumerate(chips)]
        for cp in first:
            cp.start()
        passed = [copy(4 + j, (*chip, c), sibling) for j, chip in enumerate(chips)]
        for j, chip in enumerate(chips):
            copy(1 + j, (*chip, c), me).wait_recv()
            passed[j].start()
        copy(0, sibling, me).wait_recv()
        for j, chip in enumerate(chips):
            copy(4 + j, (*chip, 1 - c), me).wait_recv()
        for cp in first + passed:
            cp.wait_send()
        mine.wait()

    return pl.pallas_call(
        body, name=name, out_shape=jax.ShapeDtypeStruct((N_DEV * m_per, n), block.dtype),
        in_specs=[pl.BlockSpec(memory_space=pltpu.VMEM)], out_specs=pl.BlockSpec(memory_space=pltpu.VMEM),
        scratch_shapes=[pltpu.SemaphoreType.DMA((7,)), pltpu.SemaphoreType.DMA((7,)), pltpu.SemaphoreType.DMA],
        compiler_params=_params(),
    )(block)


def _hbm_specs(n):
    return [pl.BlockSpec(memory_space=HBM)] * n


def _part(ref, by_cols, half, quarter=None, lead=None):
    extent = ref.shape[-1] if by_cols else ref.shape[-2]
    size = extent // 2 if quarter is None else extent // 4
    first = half * (extent // 2) + (0 if quarter is None else quarter * size)
    tile = LANES if by_cols else 2 * SUBLANES
    span = pl.ds(pl.multiple_of(first, tile) if size % tile == 0 else first, size)
    index = (slice(None), span) if by_cols else (span, slice(None))
    return ref.at[index] if lead is None else ref.at[(lead,) + index]


def _half_rows(ref, half, lead=None):
    return _part(ref, False, half, lead=lead)


class _Comm:
    def __init__(self, operands, out_shape, sem_dims, build, aliases=None):
        self.operands, self.out_shape, self.sem_dims = list(operands), list(out_shape), list(sem_dims)
        self.scratch = [pltpu.SemaphoreType.DMA(d) for d in sem_dims]
        self.build, self.aliases = build, dict(aliases or {})


class _SemGrid:
    def __init__(self, sems, dims):
        self.sems, self.dims, self.at = list(sems), tuple(dims), self

    def __getitem__(self, index):
        index = index if isinstance(index, tuple) else (index,)
        flat = 0
        for i, d in zip(index, self.dims):
            flat = flat * d + i
        return self.sems[flat]


def _call(body, operands, comm=None, *, name, grid, in_specs, out_specs, out_shape, scratch_shapes=(),
          input_output_aliases=None):
    aliases = dict(input_output_aliases or {})
    if comm is None:
        return pl.pallas_call(
            body, name=name, grid=grid, in_specs=in_specs, out_specs=out_specs, out_shape=out_shape,
            scratch_shapes=list(scratch_shapes), input_output_aliases=aliases, compiler_params=_params())(*operands)
    single = not isinstance(out_shape, (list, tuple))
    outs = [out_shape] if single else list(out_shape)
    ospecs = [out_specs] if single else list(out_specs)
    n_in, n_out, n_scr = len(operands), len(outs), len(scratch_shapes)
    c_in, c_out = len(comm.operands), len(comm.out_shape)
    for i, o in comm.aliases.items():
        aliases[n_in + i] = n_out + o

    def hosted(*refs):
        ins, c_ins = refs[:n_in], refs[n_in:n_in + c_in]
        o0 = n_in + c_in
        o_refs, c_outs = refs[o0:o0 + n_out], refs[o0 + n_out:o0 + n_out + c_out]
        s0 = o0 + n_out + c_out
        scr, sems = refs[s0:s0 + n_scr], refs[s0 + n_scr:]
        stages = comm.build(c_ins, c_outs, sems)
        step, n_steps = 0, 1
        for dim, size in enumerate(grid):
            step, n_steps = step * size + pl.program_id(dim), n_steps * size
        pl.when(step == 0)(stages[0])
        body(*ins, *o_refs, *scr)
        for stage in stages[1:-1]:
            pl.when(step == (n_steps * MIDDLE_STAGE_AT) // 100)(stage)
        pl.when(step == n_steps - 1)(stages[-1])

    res = pl.pallas_call(
        hosted, name=name, grid=grid, in_specs=list(in_specs) + _hbm_specs(c_in),
        out_specs=ospecs + _hbm_specs(c_out), out_shape=outs + comm.out_shape,
        scratch_shapes=list(scratch_shapes) + comm.scratch, input_output_aliases=aliases,
        compiler_params=_params())(*operands, *comm.operands)
    return (res[0] if single else res[:n_out]), res[n_out:]


def _run_comm(comm, name):
    c_in, c_out = len(comm.operands), len(comm.out_shape)

    def body(*refs):
        for stage in comm.build(refs[:c_in], refs[c_in:c_in + c_out], refs[c_in + c_out:]):
            stage()

    return pl.pallas_call(
        body, name=name, in_specs=_hbm_specs(c_in), out_specs=_hbm_specs(c_out), out_shape=comm.out_shape,
        scratch_shapes=comm.scratch, input_output_aliases=comm.aliases, compiler_params=_params())(*comm.operands)


def _join_comms(comms):
    def build(in_refs, out_refs, sems):
        staged, i, o, k = [], 0, 0, 0
        for cm in comms:
            ni, no, ns = len(cm.operands), len(cm.out_shape), len(cm.sem_dims)
            staged.append(cm.build(in_refs[i:i + ni], out_refs[o:o + no], sems[k:k + ns]))
            i, o, k = i + ni, o + no, k + ns
        def run(fns):
            def stage():
                for fn in fns:
                    fn()
            return stage

        return (run([st[0] for st in staged]), run([fn for st in staged for fn in st[1:-1]]),
                run([st[-1] for st in staged]))

    aliases, i, o = {}, 0, 0
    for cm in comms:
        aliases.update({i + a: o + b for a, b in cm.aliases.items()})
        i, o = i + len(cm.operands), o + len(cm.out_shape)
    return _Comm(sum((cm.operands for cm in comms), []), sum((cm.out_shape for cm in comms), []),
                 sum((cm.sem_dims for cm in comms), []), build, aliases)


def _gather8_comm(block):
    def build(in_refs, out_refs, sems):
        (src,), (out,), (send_sems, recv_sems) = in_refs, out_refs, sems
        x, y, c, chips = _place()
        me, sibling = (x, y, c), (x, y, 1 - c)

        def copy(k, blk, to, own=False):
            dst = out.at[4 * blk[0] + 2 * blk[1] + blk[2]]
            return pltpu.make_async_remote_copy(
                src_ref=src if own else dst, dst_ref=dst, send_sem=send_sems.at[k], recv_sem=recv_sems.at[k],
                device_id=to, device_id_type=MESH)

        first = [copy(0, me, sibling, own=True)] + [copy(1 + j, me, (*chip, c), own=True)
                                                     for j, chip in enumerate(chips)]
        passed = [copy(4 + j, (*chip, c), sibling) for j, chip in enumerate(chips)]

        def start():
            for cp in first:
                cp.start()

        def middle():
            for j, chip in enumerate(chips):
                copy(1 + j, (*chip, c), me).wait_recv()
                passed[j].start()

        def finish():
            copy(0, sibling, me).wait_recv()
            for j, chip in enumerate(chips):
                copy(4 + j, (*chip, 1 - c), me).wait_recv()
            for cp in first + passed:
                cp.wait_send()

        return start, middle, finish

    return _Comm([block], [jax.ShapeDtypeStruct((N_DEV,) + block.shape, block.dtype)], [(7,), (7,)], build)


def _gather_comm(shards, by_cols=()):
    nw = len(shards)

    def build(in_refs, out_refs, sems):
        send_sems, recv_sems = sems
        x, y, c, chips = _place()
        me, sibling = (x, y, c), (x, y, 1 - c)
        across_x, across_y, diagonal = chips

        def copy(w, k, block, part, to, src=None):
            dst = _part(out_refs[w], w in by_cols, part[1], part[2] if part[0] else None, 2 * block[0] + block[1])
            return pltpu.make_async_remote_copy(
                src_ref=dst if src is None else src, dst_ref=dst, send_sem=send_sems.at[w, k],
                recv_sem=recv_sems.at[w, k], device_id=to, device_id_type=MESH)

        first = [copy(w, j, (x, y), (0, c), (*chip, c), src=_part(in_refs[w], w in by_cols, c))
                 for w in range(nw) for j, chip in enumerate((across_x, across_y))]
        passed = [[copy(w, 2, across_x, (1, c, 0), (*across_y, c)), copy(w, 3, across_y, (1, c, 1), (*across_x, c)),
                   copy(w, 4, across_x, (0, c), sibling), copy(w, 5, across_y, (0, c), sibling)] for w in range(nw)]
        last = [[copy(w, 6, diagonal, (1, c, 0), sibling), copy(w, 7, diagonal, (1, c, 1), sibling)]
                for w in range(nw)]

        def start():
            for cp in first:
                cp.start()

        def middle():
            for w in range(nw):
                copy(w, 0, across_x, (0, c), me).wait_recv()
                copy(w, 1, across_y, (0, c), me).wait_recv()
                for cp in passed[w]:
                    cp.start()

        def finish():
            for w in range(nw):
                copy(w, 2, diagonal, (1, c, 0), me).wait_recv()
                copy(w, 3, diagonal, (1, c, 1), me).wait_recv()
                for cp in last[w]:
                    cp.start()
            for w in range(nw):
                for k, block, part in ((4, across_x, (0, 1 - c)), (5, across_y, (0, 1 - c)),
                                       (6, diagonal, (1, 1 - c, 0)), (7, diagonal, (1, 1 - c, 1))):
                    copy(w, k, block, part, me).wait_recv()
            for cp in first + sum(passed, []) + sum(last, []):
                cp.wait_send()

        return start, middle, finish

    return _Comm(shards, [jax.ShapeDtypeStruct((N_CHIPS,) + w.shape, w.dtype) for w in shards],
                 [(nw, 8), (nw, 8)], build)


def _halved(shape, by_cols):
    return shape[:-1] + (shape[-1] // 2,) if by_cols else shape[:-2] + (shape[-2] // 2, shape[-1])


def _swap_comm(gs, by_cols=()):
    nw = len(gs)

    def build(in_refs, out_refs, sems):
        send_sems, recv_sems = sems
        x, y, c, _ = _place()
        cps = []
        for w in range(nw):
            cps.append(pltpu.make_async_remote_copy(
                src_ref=_part(in_refs[w], w in by_cols, 1 - c, lead=slice(None)), dst_ref=out_refs[w],
                send_sem=send_sems.at[w], recv_sem=recv_sems.at[w], device_id=(x, y, 1 - c), device_id_type=MESH))

        def start():
            for cp in cps:
                cp.start()

        def finish():
            for cp in cps:
                cp.wait()

        return start, finish

    return _Comm(gs, [jax.ShapeDtypeStruct(_halved(g.shape, w in by_cols), g.dtype) for w, g in enumerate(gs)],
                 [(nw,), (nw,)], build)


def _exchange_comm(s1s):
    nw = len(s1s)

    def build(in_refs, out_refs, sems):
        send_sems, recv_sems = sems
        x, y, c, chips = _place()
        cps = [pltpu.make_async_remote_copy(
            src_ref=in_refs[w].at[2 * chip[0] + chip[1]], dst_ref=out_refs[w].at[j], send_sem=send_sems.at[w, j],
            recv_sem=recv_sems.at[w, j], device_id=(*chip, c), device_id_type=MESH)
            for w in range(nw) for j, chip in enumerate(chips)]

        def start():
            for cp in cps:
                cp.start()

        def finish():
            for cp in cps:
                cp.wait()

        return start, finish

    return _Comm(s1s, [jax.ShapeDtypeStruct((N_CHIPS - 1,) + s.shape[1:], s.dtype) for s in s1s],
                 [(nw, 3), (nw, 3)], build)


def _size(dims):
    n = 1
    for d in dims:
        n *= d
    return n


def _sem_grids(comm, sem_refs):
    grids, pos = [], 0
    for dims in comm.sem_dims:
        grids.append(_SemGrid(sem_refs[pos:pos + _size(dims)], dims))
        pos += _size(dims)
    return grids


def _comm_split_start(comm, name, after=()):
    c_in, c_out = len(comm.operands), len(comm.out_shape)
    counts = [_size(d) for d in comm.sem_dims]
    n_sem = sum(counts)
    assert not comm.aliases

    def body(*refs):
        srcs, lands = refs[:c_in], refs[c_in:c_in + c_out]
        first_sem = c_in + c_out + len(after)
        start, _ = comm.build(srcs, lands, _sem_grids(comm, refs[first_sem:first_sem + n_sem]))
        start()
        refs[-1][...] = jnp.zeros(refs[-1].shape, refs[-1].dtype)

    lands = [pltpu.with_memory_space_constraint(lax.empty(o.shape, o.dtype), HBM) for o in comm.out_shape]
    srcs = [pltpu.with_memory_space_constraint(a, HBM) for a in comm.operands]
    res = pl.pallas_call(
        body, name=name, in_specs=_hbm_specs(c_in + c_out) + [pl.BlockSpec(memory_space=pl.ANY)] * len(after),
        out_specs=[pl.BlockSpec(memory_space=pltpu.SEMAPHORE)] * n_sem + _hbm_specs(c_in + c_out)
        + [pl.BlockSpec(memory_space=pltpu.VMEM)],
        out_shape=[pltpu.SemaphoreType.DMA(())] * n_sem + [pltpu.HBM(a.shape, a.dtype) for a in comm.operands]
        + [pltpu.HBM(o.shape, o.dtype) for o in comm.out_shape] + [jax.ShapeDtypeStruct((SUBLANES, LANES), F32)],
        input_output_aliases={i: n_sem + i for i in range(c_in + c_out)},
        compiler_params=_params(has_side_effects=pltpu.SideEffectType.DATAFLOW_SIDE_EFFECTING))(*srcs, *lands, *after)
    return res[:-1], res[-1]


def _comm_split_wait(comm, state, after, name):
    c_in, c_out, n_sem = len(comm.operands), len(comm.out_shape), sum(_size(d) for d in comm.sem_dims)
    sems, srcs, lands = state[:n_sem], state[n_sem:n_sem + c_in], state[n_sem + c_in:]

    def body(*refs):
        src_refs, land_refs = refs[:c_in], refs[c_in:c_in + c_out]
        _, finish = comm.build(src_refs, land_refs, _sem_grids(comm, refs[c_in + c_out:c_in + c_out + n_sem]))
        finish()

    sem_spec = pl.BlockSpec(memory_space=pltpu.SEMAPHORE)
    res = pl.pallas_call(
        body, name=name, in_specs=_hbm_specs(c_in + c_out) + [sem_spec] * n_sem + [pl.BlockSpec(memory_space=pl.ANY)],
        out_specs=_hbm_specs(c_in + c_out),
        out_shape=[pltpu.HBM(a.shape, a.dtype) for a in srcs] + [pltpu.HBM(o.shape, o.dtype) for o in lands],
        input_output_aliases={i: i for i in range(c_in + c_out)},
        compiler_params=_params(has_side_effects=pltpu.SideEffectType.DATAFLOW_SIDE_EFFECTING),
    )(*srcs, *lands, *sems, after)
    return res[:c_in], res[c_in:]


def _share_comm(fs, by_cols=()):
    nw = len(fs)

    def build(in_refs, out_refs, sems):
        del in_refs
        send_sems, recv_sems = sems
        x, y, c, _ = _place()

        def copy(w, half):
            part = _part(out_refs[w], w in by_cols, half)
            return pltpu.make_async_remote_copy(
                src_ref=part, dst_ref=part, send_sem=send_sems.at[w], recv_sem=recv_sems.at[w],
                device_id=(x, y, 1 - c), device_id_type=MESH)

        sends = [copy(w, c) for w in range(nw)]

        def start():
            for cp in sends:
                cp.start()

        def finish():
            for w in range(nw):
                copy(w, 1 - c).wait_recv()
            for cp in sends:
                cp.wait_send()

        return start, finish

    return _Comm(fs, [jax.ShapeDtypeStruct(f.shape, f.dtype) for f in fs],
                 [(nw,), (nw,)], build,
                 aliases={w: w for w in range(nw)})


def _add_sibling(g, r1, place, name, by_cols=False):
    nch, h, cols = r1.shape
    tr = _div_tile(h, 1024 if by_cols else 512, 2 * SUBLANES)
    nb = h // tr
    mine = (lambda k, i, p: (k, i, p[0])) if by_cols else (lambda k, i, p: (k, p[0] * nb + i, 0))

    def body(place_ref, g_ref, r_ref, o_ref):
        del place_ref
        o_ref[...] = (g_ref[...].astype(F32) + r_ref[...].astype(F32)).astype(BF16)

    spec = pltpu.PrefetchScalarGridSpec(
        num_scalar_prefetch=1, grid=(nch, nb),
        in_specs=[pl.BlockSpec((None, tr, cols), mine), pl.BlockSpec((None, tr, cols), lambda k, i, p: (k, i, 0))],
        out_specs=pl.BlockSpec((None, tr, cols), lambda k, i, p: (k, i, 0)))
    return pl.pallas_call(body, name=name, grid_spec=spec, out_shape=jax.ShapeDtypeStruct((nch, h, cols), BF16),
                          compiler_params=_params())(place, g, r1)


def _add_chips(s1, r2, place, name, by_cols=False):
    _, h, cols = s1.shape
    tr = _div_tile(h, 1024 if by_cols else 512, 2 * SUBLANES)
    nb = h // tr
    mine = (lambda i, p: (i, p[0])) if by_cols else (lambda i, p: (p[0] * nb + i, 0))
    whole = (h, 2 * cols) if by_cols else (2 * h, cols)

    def body(place_ref, s_ref, r_ref, o_ref):
        del place_ref
        acc = s_ref[...].astype(F32)
        for j in range(N_CHIPS - 1):
            acc = acc + r_ref[j].astype(F32)
        o_ref[...] = acc

    spec = pltpu.PrefetchScalarGridSpec(
        num_scalar_prefetch=1, grid=(nb,),
        in_specs=[pl.BlockSpec((None, tr, cols), lambda i, p: (p[1], i, 0)),
                  pl.BlockSpec((N_CHIPS - 1, tr, cols), lambda i, p: (0, i, 0))],
        out_specs=pl.BlockSpec((tr, cols), mine))
    return pl.pallas_call(body, name=name, grid_spec=spec, out_shape=jax.ShapeDtypeStruct(whole, F32),
                          compiler_params=_params())(place, s1, r2)


def _quarter_turn(m):
    h = m.shape[-1] // 2
    return jnp.concatenate([-m[..., h:], m[..., :h]], axis=-1)


def _quarter_turn_back(m):
    h = m.shape[-1] // 2
    return jnp.concatenate([m[..., h:], -m[..., :h]], axis=-1)


def _stack_rows(parts):
    out = lax.empty((sum(p.shape[0] for p in parts),) + parts[0].shape[1:], parts[0].dtype)
    row = 0
    for p in parts:
        out = lax.dynamic_update_slice(out, p, (row, 0))
        row += p.shape[0]
    return out


def _join_cols(sh):
    return jnp.concatenate([sh[k] for k in range(N_CHIPS)], axis=1)


def _split_cols(full):
    c = full.shape[1] // N_CHIPS
    return jnp.stack([full[:, k * c:(k + 1) * c] for k in range(N_CHIPS)])


def kernel(x, c, positions, w_ada, b_ada, pre_norm1_g, w_in, gm_ln_g, gm_ln_b, gm_w_s, gm_b_s, w_branch_a, q_norm_g, w_uq, kv_norm_g, w_ukv, w_branch_b, w_out, post_norm1_g, pre_norm2_g, w_up, conv_w, conv_b, w_down, post_norm2_g, loss_target, m_w_ada, m_b_ada, m_pre_norm1_g, m_w_in, m_gm_ln_g, m_gm_ln_b, m_gm_w_s, m_gm_b_s, m_w_branch_a, m_q_norm_g, m_w_uq, m_kv_norm_g, m_w_ukv, m_w_branch_b, m_w_out, m_post_norm1_g, m_pre_norm2_g, m_w_up, m_conv_w, m_conv_b, m_w_down, m_post_norm2_g, v_w_ada, v_b_ada, v_pre_norm1_g, v_w_in, v_gm_ln_g, v_gm_ln_b, v_gm_w_s, v_gm_b_s, v_w_branch_a, v_q_norm_g, v_w_uq, v_kv_norm_g, v_w_ukv, v_w_branch_b, v_w_out, v_post_norm1_g, v_pre_norm2_g, v_w_up, v_conv_w, v_conv_b, v_w_down, v_post_norm2_g):
    given = dict(locals())
    s, d = x.shape[1], x.shape[2]
    gw = gm_ln_g.shape[0]
    ql, kvl = q_norm_g.shape[0], kv_norm_g.shape[0]
    heads = N_CHIPS * w_uq.shape[1] // (NOPE + ROPE)
    ff = N_CHIPS * w_down.shape[0]
    assert gw == d and N_CHIPS * w_ukv.shape[1] == heads * (NOPE + VHEAD)
    ix, iy, ic = lax.axis_index("x"), lax.axis_index("y"), lax.axis_index("c")
    chip = 2 * ix + iy
    dev = 2 * chip + ic
    row = lambda v: v.reshape(1, -1)

    first = _all_gather(jnp.concatenate([jnp.pad(c, ((0, SUBLANES - 1), (0, 0))),
                                         jnp.pad(conv_w, ((0, SUBLANES - CONV_TAPS), (0, 0)))], axis=1), "gather_c")
    first = first.reshape(N_DEV, SUBLANES, d + conv_w.shape[1])
    c_all = first[:, 0, :d]
    conv_wf = first[::N_CORES, :CONV_TAPS, d:].transpose(1, 0, 2).reshape(CONV_TAPS, N_CHIPS * conv_w.shape[1])
    na = w_ada.shape[1]
    b_ada_mine = lax.dynamic_slice(b_ada, (chip * na,), (na,))
    mod_cols = _ada_fwd(c_all, w_ada, row(b_ada_mine), "ada_fwd")
    mod_all = _all_gather(mod_cols, "gather_mod").reshape(N_CHIPS, N_CORES, N_DEV, na)[:, 0]
    mod = lax.dynamic_index_in_dim(mod_all, dev, axis=1, keepdims=False).reshape(N_MOD, d)
    shift1, scale1, gate1, shift2, scale2, gate2 = (mod[i:i + 1] for i in range(N_MOD))

    mine = {n: (given[n].T if n == "w_in" else given[n]).astype(BF16) for n in BIG}
    gather = lambda names: _gather_comm([mine[n] for n in names], [i for i, n in enumerate(names) if n == "w_in"])
    whole = lambda n, g: lax.dynamic_update_slice(g, mine[n][None], (chip, 0, 0))
    rows4 = lambda sh4: sh4.reshape(-1, sh4.shape[2])
    wi_t = rows4(whole("w_in", _run_comm(gather(["w_in"]), "gather_w_in")[0]))
    o_q, o_kv, o_pe, o_ga = 2 * gw, 2 * gw + ql, 2 * gw + ql + kvl, 2 * gw + ql + kvl + ROPE
    w_in_big_t = _stack_rows([wi_t[:o_q], wi_t[o_ga:]])
    w_in_lat_t = _stack_rows([wi_t[o_q:o_ga], _quarter_turn(wi_t[o_pe:o_ga].T).T])

    inv = ROPE_THETA ** (-jnp.arange(0, ROPE, 2, dtype=F32) / ROPE)
    ang = positions[0].astype(F32)[:, None] * inv
    cos, sin = jnp.cos(ang), jnp.sin(ang)
    rope_k = jnp.concatenate([cos, cos, sin, sin], axis=1)
    softmax_scale = float(NOPE + ROPE) ** -0.5
    rope_q = jnp.concatenate([jnp.ones((s, NOPE), F32), rope_k], axis=1) * softmax_scale

    x2d, tgt = x[0], loss_target[0]
    g_pre1, g_post1, g_pre2, g_post2 = row(pre_norm1_g), row(post_norm1_g), row(pre_norm2_g), row(post_norm2_g)
    ln_g, ln_b, q_g, kv_g = row(gm_ln_g), row(gm_ln_b), row(q_norm_g), row(kv_norm_g)
    b_s_t = gm_b_s.T
    conv_bf = row(conv_b)

    h1 = _prenorm(x2d, g_pre1, scale1, shift1, "prenorm1")
    z_big, (g_uq, g_ukv, g_a) = _matmul(h1, w_in_big_t, mode="nt", out_dtype=BF16, name="mm_z_big", tm=s,
                                        comm=gather(["w_uq", "w_ukv", "w_branch_a"]))
    wq = _join_cols(whole("w_uq", g_uq)).reshape(ql, heads, NOPE + ROPE)
    w_q = jnp.concatenate([wq, _quarter_turn(wq[:, :, NOPE:])], axis=2).reshape(ql, heads * HEAD_W)
    w_kv = _join_cols(whole("w_ukv", g_ukv)).reshape(kvl, heads, 2, NOPE).transpose(0, 2, 1, 3)
    w_kv = w_kv.reshape(kvl, 2 * heads * NOPE)
    w_a = rows4(whole("w_branch_a", g_a))
    z_lat = _matmul(h1, w_in_lat_t, mode="nt", out_dtype=F32, name="mm_z_lat", tm=s, tn=1024)
    a_act = _gmlp_fwd(z_big, ln_g, ln_b, gm_w_s, b_s_t, "gmlp_fwd")
    qn, kvn, kr = _mla_prep(z_lat, q_g, kv_g, rope_k, "mla_prep")
    q_rot = _matmul(qn, w_q, mode="nn", out_dtype=BF16, name="mm_q", tm=s, tn=HEAD_W, mul=rope_q)
    kv_all = _matmul(kvn, w_kv, mode="nn", out_dtype=BF16, name="mm_kv", tm=s, tn=1024)
    (o_att, lse), (g_b, g_o, g_up) = _attn_fwd(q_rot, kv_all, kr, heads, "attn_fwd",
                                               comm=gather(["w_branch_b", "w_out", "w_up"]))
    w_b, w_o, w_upf = rows4(whole("w_branch_b", g_b)), rows4(whole("w_out", g_o)), whole("w_up", g_up)
    y_a = _matmul(a_act, w_a, mode="nn", out_dtype=BF16, name="mm_y_a", tm=s)
    y_b = _matmul(o_att, w_b, mode="nn", out_dtype=BF16, name="mm_y_b", tm=s)
    merged = _merge(z_big, y_a, y_b, "merge")
    y1 = _matmul(merged, w_o, mode="nn", out_dtype=F32, name="mm_y1", tm=s)
    x1, h2 = _post_pre(x2d, y1, gate1, g_post1, g_pre2, scale2, shift2, "post1_pre2")

    up_pre, (g_dn,) = _matmul(h2, w_upf, mode="nn", out_dtype=BF16, name="mm_up", tm=s, tn=1408,
                              comm=gather(["w_down"]))
    w_dn = rows4(whole("w_down", g_dn))
    act = _conv_fwd(up_pre, conv_wf, conv_bf, "conv_fwd")
    ffn = _matmul(act, w_dn, mode="nn", out_dtype=F32, name="mm_ffn", tm=s, tn=1024, tk=1408)

    dffn, dgate2, g_post2_grad, dx2, loss_part = _post_bwd(ffn, gate2, g_post2, "post2_bwd", xin=x1, target=tgt)
    loss = lax.psum(loss_part[0, 0], ("x", "y", "c"))
    place = jnp.stack([ic, chip]).astype(jnp.int32)
    rows_of = lambda g: g.reshape(N_CHIPS, g.shape[0] // N_CHIPS, g.shape[1])
    add_sibling = lambda names, gs, r1s: [_add_sibling(g, r1, place, "rs_add_sibling_" + n, by_cols=n == "w_in")
                                          for n, g, r1 in zip(names, gs, r1s)]
    add_chips = lambda names, s1s, r2s: [_add_chips(s1, r2, place, "rs_add_chips_" + n, by_cols=n == "w_in")
                                         for n, s1, r2 in zip(names, s1s, r2s)]
    gp_down = [rows_of(_matmul(act, dffn, mode="tn", out_dtype=BF16, name="mm_gw_down", tn=2048, tk=s))]
    dact, r1_down = _matmul(dffn, w_dn, mode="nt", out_dtype=BF16, name="mm_dact", tm=s, comm=_swap_comm(gp_down))
    s1_down = add_sibling(["w_down"], gp_down, r1_down)
    (dup, gcw_g, gcw_v, gcb_g, gcb_v), r2_down = _conv_bwd(up_pre, dact, conv_wf, conv_bf, "conv_bwd",
                                                            comm=_exchange_comm(s1_down))
    half_down = add_chips(["w_down"], s1_down, r2_down)
    dh2 = _matmul(dup, w_upf, mode="nt", out_dtype=F32, name="mm_dh2", tm=s, tn=1024, tk=1408)
    dx1, dshift2, dscale2, g_pre2_grad = _prenorm_bwd(x1, dh2, dx2, g_pre2, scale2, "prenorm2_bwd")

    dy1, dgate1, g_post1_grad = _post_bwd(y1, gate1, g_post1, "post1_bwd", dxo=dx1)
    dmerged = _matmul(dy1, w_o, mode="nt", out_dtype=BF16, name="mm_dmerged", tm=s)
    gw_out = _matmul(merged, dy1, mode="tn", out_dtype=BF16, name="mm_gw_out", tn=1024, tk=s)
    dy_a, dy_b, dz_big = _merge_bwd(dmerged, z_big, y_a, y_b, "merge_bwd")
    gw_a = _matmul(a_act, dy_a, mode="tn", out_dtype=BF16, name="mm_gw_a", tn=1024, tk=s)
    gw_b = _matmul(o_att, dy_b, mode="tn", out_dtype=BF16, name="mm_gw_b", tn=1024, tk=s)
    mid = ["w_up", "w_out", "w_branch_a", "w_branch_b"]
    gp_oab = [rows_of(gw_out), rows_of(gw_a), rows_of(gw_b)]
    da, r1_oab = _matmul(dy_a, w_a, mode="nt", out_dtype=BF16, name="mm_da", tm=s, comm=_swap_comm(gp_oab))
    s1_oab = add_sibling(mid[1:], gp_oab, r1_oab)
    gw_up, r2_oa = _matmul(h2, dup, mode="tn", out_dtype=BF16, name="mm_gw_up", tm=1024, tn=1408, tk=s,
                           out_groups=N_CHIPS, comm=_exchange_comm(s1_oab[:2]))
    do = _matmul(dy_b, w_b, mode="nt", out_dtype=BF16, name="mm_do", tm=s)
    (dz_big, g_ws, g_bs_t, g_ln_g, g_ln_b), r1_up = _gmlp_bwd(z_big, da, dz_big, ln_g, ln_b, gm_w_s, b_s_t,
                                                               "gmlp_bwd", comm=_swap_comm([gw_up]))
    s1_mid = add_sibling(mid[:1], [gw_up], r1_up) + s1_oab
    (dq, dk, dv), r2_up = _attn_bwd(q_rot, kv_all, kr, o_att, do, lse, heads, "attn_bwd",
                                    comm=_exchange_comm(s1_mid[:1]))
    dq_big, dkv, dkk = _mla_bwd_mid(dq, dk, dv, rope_q, rope_k, heads, "mla_bwd_mid")
    gw_q = _matmul(qn, dq_big, mode="tn", out_dtype=F32, name="mm_gw_q", tn=1024, tk=s)
    dqn = _matmul(dq_big, w_q, mode="nt", out_dtype=F32, name="mm_dqn", tm=s, tk=1024)
    gw_kv = _matmul(kvn, dkv, mode="tn", out_dtype=BF16, name="mm_gw_kv", tn=1024, tk=s)
    dkvn = _matmul(dkv, w_kv, mode="nt", out_dtype=F32, name="mm_dkvn", tm=s, tk=1024)
    dz_lat, g_q, g_kv = _mla_bwd_post(z_lat, dqn, dkvn, dkk, q_g, kv_g, "mla_bwd_post")

    partial = {
        "gm_ln_g": g_ln_g, "gm_ln_b": g_ln_b, "gm_w_s": g_ws, "gm_b_s": g_bs_t[:, :gm_b_s.shape[0]].T,
        "q_norm_g": g_q, "kv_norm_g": g_kv, "post_norm1_g": g_post1_grad, "pre_norm2_g": g_pre2_grad,
        "conv_w": jnp.concatenate([gcw_g, gcw_v], axis=1), "conv_b": jnp.concatenate([gcb_g, gcb_v], axis=1),
        "post_norm2_g": g_post2_grad,
    }
    flat = jnp.concatenate([partial[n].reshape(-1) for n in SMALL_PARTIAL])
    n_small = flat.shape[0]
    rows_small = -(-n_small // (LANES * SMALL_ROW_TILE)) * SMALL_ROW_TILE
    flat = jnp.pad(flat, (0, rows_small * LANES - n_small)).reshape(rows_small, LANES)

    def small_pack(prefix, source):
        v = jnp.concatenate([source[prefix + n].reshape(-1) for n in SMALL])
        rows = -(-v.shape[0] // (LANES * SUBLANES)) * SUBLANES
        return jnp.pad(v, (0, rows * LANES - v.shape[0])).reshape(rows, LANES)

    small_state = [small_pack(prefix, given) for prefix in ("", "m_", "v_")]

    dh1, r2_a_b = _matmul(dz_big, w_in_big_t, mode="nn", out_dtype=F32, name="mm_dh1_big", tm=s, tk=1024,
                          comm=_exchange_comm(s1_mid[3:]))
    half_mid = add_chips(mid, s1_mid, list(r2_up) + list(r2_oa) + list(r2_a_b))
    dh1 = _matmul(dz_lat, w_in_lat_t, mode="nn", out_dtype=F32, name="mm_dh1_lat", tm=s, tk=1024, add=dh1)
    gw_big_t, hosted = _matmul(dz_big, h1, mode="tn", out_dtype=BF16, name="mm_gw_in_big", tn=2048, tk=s,
                               comm=_join_comms([_share_comm(half_down + half_mid), _gather8_comm(flat)]))
    shared, small_all = hosted[:-1], lax.dynamic_update_slice(hosted[-1], flat[None], (dev, 0, 0))
    small_sum = _sum_leading(small_all, "sum_small", after=small_state + [loss.reshape(1, 1)]).reshape(-1)
    small_grads, off = {}, 0
    for n in SMALL_PARTIAL:
        shape = (CONV_TAPS, 2 * ff) if n == "conv_w" else given[n].shape
        small_grads[n] = small_sum[off:off + partial[n].size].reshape(shape)
        off += partial[n].size
    small_grads["conv_w"] = lax.dynamic_slice(small_grads["conv_w"], (0, chip * conv_w.shape[1]), conv_w.shape)
    grads = dict(zip(["w_down"] + mid, shared), **small_grads)
    gw_lat_t = _matmul(dz_lat, h1, mode="tn", out_dtype=F32, name="mm_gw_in_lat", tm=1024, tn=1024, tk=s)

    gq = gw_q.reshape(ql, heads, HEAD_W)
    gq_pe = gq[:, :, NOPE:NOPE + ROPE] + _quarter_turn_back(gq[:, :, NOPE + ROPE:])
    g_pe_t = gw_lat_t[ql + kvl:ql + kvl + ROPE] + _quarter_turn_back(gw_lat_t[ql + kvl + ROPE:].T).T
    last = ["w_in", "w_uq", "w_ukv"]
    gw_in_t = _stack_rows([gw_big_t[:o_q], gw_lat_t[:ql + kvl].astype(BF16), g_pe_t.astype(BF16), gw_big_t[o_q:]])
    gp_last = [
        gw_in_t.reshape(N_CHIPS, gw_in_t.shape[0] // N_CHIPS, d),
        _split_cols(jnp.concatenate([gq[:, :, :NOPE], gq_pe], axis=2).reshape(ql, heads * (NOPE + ROPE)).astype(BF16)),
        _split_cols(gw_kv.reshape(kvl, 2, heads, NOPE).transpose(0, 2, 1, 3).reshape(kvl, heads * 2 * NOPE)),
    ]
    (grad_x, dshift1, dscale1, g_pre1_grad), r1_last = _prenorm_bwd(x2d, dh1, dx1, g_pre1, scale1, "prenorm1_bwd",
                                                                    comm=_swap_comm(gp_last, by_cols=[0]))
    s1_last = add_sibling(last, gp_last, r1_last)

    dmod = jnp.concatenate([dshift1, dscale1, dgate1, dshift2, dscale2, dgate2, g_pre1_grad], axis=1)
    dmod_all = _all_gather(jnp.pad(dmod, ((0, SUBLANES - 1), (0, 0))), "gather_dmod")
    dmod_all = dmod_all.reshape(N_DEV, SUBLANES, (N_MOD + 1) * d)[:, 0]
    dmod_sum = _sum_leading(dmod_all.reshape(N_DEV, 1, (N_MOD + 1) * d), "sum_dmod")[0]
    grads["b_ada"], grads["pre_norm1_g"] = dmod_sum[:N_MOD * d], dmod_sum[N_MOD * d:]
    dmod_mine = lax.dynamic_slice(dmod_all, (0, chip * na), (N_DEV, na))
    grads["w_ada"] = _ada_bwd(c_all.T, dmod_mine, "ada_bwd")

    delta, new_m, new_v = {}, {}, {}

    def adamw(n, after=None):
        turn = (lambda a: a.T) if n == "w_in" else (lambda a: a)
        outs = _adamw(turn(given[n]), grads[n], turn(given["m_" + n]), turn(given["v_" + n]), "adamw_" + n,
                      after=after)
        grads[n] = turn(grads[n])
        delta[n], new_m[n], new_v[n] = (turn(o) for o in outs)

    exchange_last = _exchange_comm(s1_last)
    in_flight, token = _comm_split_start(exchange_last, "rs_exchange_last_start", after=[dmod_sum, small_sum])
    for n in ["w_ada", "w_down"] + mid:
        adamw(n, after=token)
    s1_last, r2_last = _comm_split_wait(exchange_last, in_flight, delta[mid[-1]], "rs_exchange_last_wait")
    half_last = add_chips(last, s1_last, r2_last)
    grads.update(zip(last, _run_comm(_share_comm(half_last, by_cols=[0]), "rs_share_last")))
    for n in last:
        adamw(n)

    outs = _adamw(small_state[0], small_pack("", grads), small_state[1], small_state[2], "adamw_small")
    off = 0
    for n in SMALL:
        size = given[n].size
        for store, packed_out in zip((delta, new_m, new_v), outs):
            store[n] = packed_out.reshape(-1)[off:off + size].reshape(given[n].shape)
        off += size

    return (loss, grad_x[None], *[grads[n] for n in WEIGHTS], *[delta[n] for n in WEIGHTS],
            *[new_m[n] for n in WEIGHTS], *[new_v[n] for n in WEIGHTS])
```

```python
import functools

import jax
import jax.numpy as jnp
from jax import lax
from jax.experimental import pallas as pl
from jax.experimental.pallas import tpu as pltpu

F32 = jnp.float32
BF16 = jnp.bfloat16
MESH = pl.DeviceIdType.MESH
HBM = pltpu.HBM

EPS = 1e-6
NOPE, ROPE, VHEAD = 128, 64, 128
HEAD_W = NOPE + 2 * ROPE
ROPE_THETA = 10000.0
CONV_TAPS = 3
N_MOD = 6
N_CHIPS, N_CORES, N_DEV = 4, 2, 8
ADAM_LR, ADAM_B1, ADAM_B2, ADAM_EPS, ADAM_WD, ADAM_STEP = 0.001, 0.9, 0.999, 1e-08, 0.01, 10

LANES = 128
SUBLANES = 8
VMEM_LIMIT = 56 * 2**20
MIDDLE_STAGE_AT = 70
SMALL_ROW_TILE = 256

BIG = ("w_in", "w_branch_a", "w_uq", "w_ukv", "w_branch_b", "w_out", "w_up", "w_down")
WEIGHTS = ("w_ada", "b_ada", "pre_norm1_g", "w_in", "gm_ln_g", "gm_ln_b", "gm_w_s", "gm_b_s", "w_branch_a",
           "q_norm_g", "w_uq", "kv_norm_g", "w_ukv", "w_branch_b", "w_out", "post_norm1_g", "pre_norm2_g",
           "w_up", "conv_w", "conv_b", "w_down", "post_norm2_g")
SMALL_PARTIAL = ("gm_ln_g", "gm_ln_b", "gm_w_s", "gm_b_s", "q_norm_g", "kv_norm_g", "post_norm1_g",
                 "pre_norm2_g", "conv_w", "conv_b", "post_norm2_g")
SMALL = ("b_ada", "pre_norm1_g") + SMALL_PARTIAL


def _div_tile(n, cap, mult=LANES):
    t = (min(cap, n) // mult) * mult
    while t >= mult:
        if n % t == 0:
            return t
        t -= mult
    return n


def _params(**kw):
    return pltpu.CompilerParams(vmem_limit_bytes=VMEM_LIMIT, **kw)


def _row_spec(width):
    return pl.BlockSpec((1, width), lambda *_: (0, 0))


def _gelu(x):
    k = 0.7978845608028654
    return 0.5 * x * (1.0 + jnp.tanh(k * (x + 0.044715 * x * x * x)))


def _gelu_grad(x):
    k = 0.7978845608028654
    t = jnp.tanh(k * (x + 0.044715 * x * x * x))
    return 0.5 * (1.0 + t) + 0.5 * x * (1.0 - t * t) * k * (1.0 + 3.0 * 0.044715 * x * x)


def _sigmoid(x):
    return 0.5 * jnp.tanh(0.5 * x) + 0.5


def _dot(a, b, dims):
    return lax.dot_general(a, b, (dims, ((), ())), preferred_element_type=F32)


NN = ((1,), (0,))
NT = ((1,), (1,))
TN = ((0,), (0,))


def _logical(arr):
    if arr.ndim == 2:
        return arr.shape[0], arr.shape[1], arr.shape[1]
    return arr.shape[1], arr.shape[0] * arr.shape[2], arr.shape[2]


def _tile_spec(ndim, group_w, blk_rows, blk_cols, row_of, col_of):
    if ndim == 2:
        return pl.BlockSpec((blk_rows, blk_cols), lambda i, j, k: (row_of(i, j, k), col_of(i, j, k)))
    per = group_w // blk_cols
    return pl.BlockSpec((None, blk_rows, blk_cols),
                        lambda i, j, k: (col_of(i, j, k) // per, row_of(i, j, k), col_of(i, j, k) % per))


def _matmul(a, b, *, mode, out_dtype, name, tm=512, tn=512, tk=2048, mul=None, add=None, out_groups=None, comm=None):
    ar, ac, agw = _logical(a)
    br, bc, bgw = _logical(b)
    if mode == "nn":
        m, kd, n = ar, ac, bc
        m_w, k_w, n_w = (), (agw,), (bgw,)
    elif mode == "nt":
        m, kd, n = ar, ac, br
        m_w, k_w, n_w = (), (agw, bgw), ()
    else:
        m, kd, n = ac, ar, bc
        m_w, k_w, n_w = (agw,), (), (bgw,)
    if out_groups is not None:
        n_w = n_w + (n // out_groups,)
    tm = _div_tile(min((m,) + m_w), tm, LANES if mode == "tn" else SUBLANES)
    tn = _div_tile(min((n,) + n_w), tn)
    tk = _div_tile(min((kd,) + k_w), tk)
    assert all(w % tn == 0 for w in n_w) and all(w % tk == 0 for w in k_w) and all(w % tm == 0 for w in m_w)
    nk = kd // tk
    dims = {"nn": NN, "nt": NT, "tn": TN}[mode]
    gi, gj, gk = (lambda i, j, k: i), (lambda i, j, k: j), (lambda i, j, k: k)
    if mode == "nn":
        a_spec = _tile_spec(a.ndim, agw, tm, tk, gi, gk)
        b_spec = _tile_spec(b.ndim, bgw, tk, tn, gk, gj)
    elif mode == "nt":
        a_spec = _tile_spec(a.ndim, agw, tm, tk, gi, gk)
        b_spec = _tile_spec(b.ndim, bgw, tn, tk, gj, gk)
    else:
        a_spec = _tile_spec(a.ndim, agw, tk, tm, gk, gi)
        b_spec = _tile_spec(b.ndim, bgw, tk, tn, gk, gj)
    in_specs, operands = [a_spec, b_spec], [a, b]
    if mul is not None:
        assert mul.shape == (m, tn)
        in_specs.append(pl.BlockSpec((tm, tn), lambda i, j, k: (i, 0)))
        operands.append(mul)
    if add is not None:
        in_specs.append(pl.BlockSpec((tm, tn), lambda i, j, k: (i, j)))
        operands.append(add)

    def body(*refs):
        a_ref, b_ref = refs[0], refs[1]
        pos = 2
        mul_ref = add_ref = None
        if mul is not None:
            mul_ref, pos = refs[pos], pos + 1
        if add is not None:
            add_ref, pos = refs[pos], pos + 1
        o_ref = refs[pos]

        def finish(r):
            if mul_ref is not None:
                r = r * mul_ref[...]
            if add_ref is not None:
                r = r + add_ref[...]
            o_ref[...] = r.astype(out_dtype)

        part = _dot(a_ref[...], b_ref[...], dims)
        if nk == 1:
            finish(part)
        else:
            acc_ref = refs[pos + 1]
            k = pl.program_id(2)

            @pl.when(k == 0)
            def _():
                acc_ref[...] = part

            @pl.when(k > 0)
            def _():
                acc_ref[...] += part

            @pl.when(k == nk - 1)
            def _():
                finish(acc_ref[...])

    if out_groups is None:
        out_spec, out_dims = _tile_spec(2, n, tm, tn, gi, gj), (m, n)
    else:
        out_spec, out_dims = _tile_spec(3, n // out_groups, tm, tn, gi, gj), (out_groups, m, n // out_groups)
    return _call(body, operands, comm, name=name, grid=(m // tm, n // tn, nk), in_specs=in_specs, out_specs=out_spec,
                 out_shape=jax.ShapeDtypeStruct(out_dims, out_dtype),
                 scratch_shapes=[] if nk == 1 else [pltpu.VMEM((tm, tn), F32)])


def _accumulate(ref, value):
    @pl.when(pl.program_id(0) == 0)
    def _():
        ref[...] = value

    @pl.when(pl.program_id(0) > 0)
    def _():
        ref[...] += value


def _colsum(v):
    return jnp.sum(v, axis=0, keepdims=True)


def _rowmean(v):
    return jnp.mean(v, axis=-1, keepdims=True)


def _prenorm(x, g, scale, shift, name):
    s, d = x.shape
    tb = _div_tile(s, 256, SUBLANES)

    def body(x_ref, g_ref, sc_ref, sh_ref, h_ref):
        xv = x_ref[...]
        r = lax.rsqrt(_rowmean(xv * xv) + EPS)
        h_ref[...] = ((xv * r) * g_ref[...] * (1.0 + sc_ref[...]) + sh_ref[...]).astype(BF16)

    blk = pl.BlockSpec((tb, d), lambda i: (i, 0))
    return pl.pallas_call(
        body, name=name, grid=(s // tb,), in_specs=[blk, _row_spec(d), _row_spec(d), _row_spec(d)],
        out_specs=blk, out_shape=jax.ShapeDtypeStruct((s, d), BF16), compiler_params=_params(),
    )(x, g, scale, shift)


def _post_pre(x, y, gate, pg, g2, scale2, shift2, name):
    s, d = x.shape
    tb = _div_tile(s, 256, SUBLANES)

    def body(x_ref, y_ref, gate_ref, pg_ref, g2_ref, sc_ref, sh_ref, x1_ref, h2_ref):
        yv = y_ref[...]
        rp = lax.rsqrt(_rowmean(yv * yv) + EPS)
        x1 = x_ref[...] + gate_ref[...] * ((yv * rp) * pg_ref[...])
        x1_ref[...] = x1
        r2 = lax.rsqrt(_rowmean(x1 * x1) + EPS)
        h2_ref[...] = ((x1 * r2) * g2_ref[...] * (1.0 + sc_ref[...]) + sh_ref[...]).astype(BF16)

    blk = pl.BlockSpec((tb, d), lambda i: (i, 0))
    return pl.pallas_call(
        body, name=name, grid=(s // tb,), in_specs=[blk, blk] + [_row_spec(d)] * 5,
        out_specs=[blk, blk],
        out_shape=[jax.ShapeDtypeStruct((s, d), F32), jax.ShapeDtypeStruct((s, d), BF16)],
        compiler_params=_params(),
    )(x, y, gate, pg, g2, scale2, shift2)


def _post_bwd(y, gate, pg, name, *, dxo=None, xin=None, target=None):
    s, d = y.shape
    tb = _div_tile(s, 256, SUBLANES)
    from_loss = target is not None

    def body(*refs):
        if from_loss:
            y_ref, gate_ref, pg_ref, xin_ref, t_ref, dy_ref, dgate_ref, dpg_ref, dxo_ref, loss_ref = refs
        else:
            y_ref, gate_ref, pg_ref, dxo_in_ref, dy_ref, dgate_ref, dpg_ref = refs
        yv = y_ref[...]
        rp = lax.rsqrt(_rowmean(yv * yv) + EPS)
        yh = yv * rp
        fn = yh * pg_ref[...]
        gate = gate_ref[...]
        if from_loss:
            err = xin_ref[...] + gate * fn - t_ref[...]
            dxo = err * (1.0 / d)
            dxo_ref[...] = dxo
            part = 0.5 * jnp.sum(_rowmean(err * err), axis=0, keepdims=True)
            _accumulate(loss_ref, jnp.broadcast_to(part, loss_ref.shape))
        else:
            dxo = dxo_in_ref[...]
        _accumulate(dgate_ref, _colsum(dxo * fn))
        dfn = dxo * gate
        _accumulate(dpg_ref, _colsum(dfn * yh))
        dyh = dfn * pg_ref[...]
        dy_ref[...] = (rp * (dyh - yh * _rowmean(dyh * yh))).astype(BF16)

    blk = pl.BlockSpec((tb, d), lambda i: (i, 0))
    in_specs = [blk, _row_spec(d), _row_spec(d)]
    out_specs = [blk, _row_spec(d), _row_spec(d)]
    out_shape = [jax.ShapeDtypeStruct((s, d), BF16), jax.ShapeDtypeStruct((1, d), F32),
                 jax.ShapeDtypeStruct((1, d), F32)]
    if from_loss:
        operands = (y, gate, pg, xin, target)
        in_specs += [blk, blk]
        out_specs += [blk, _row_spec(LANES)]
        out_shape += [jax.ShapeDtypeStruct((s, d), F32), jax.ShapeDtypeStruct((1, LANES), F32)]
    else:
        operands = (y, gate, pg, dxo)
        in_specs += [blk]
    return pl.pallas_call(
        body, name=name, grid=(s // tb,), in_specs=in_specs, out_specs=out_specs, out_shape=out_shape,
        compiler_params=_params(),
    )(*operands)


def _prenorm_bwd(xin, dh, dres, g, scale, name, comm=None):
    s, d = xin.shape
    tb = _div_tile(s, 256, SUBLANES)

    def body(x_ref, dh_ref, dres_ref, g_ref, sc_ref, dx_ref, dshift_ref, dscale_ref, dg_ref):
        xv = x_ref[...]
        r = lax.rsqrt(_rowmean(xv * xv) + EPS)
        xn = xv * r
        dh = dh_ref[...]
        g1 = g_ref[...]
        s1 = 1.0 + sc_ref[...]
        _accumulate(dshift_ref, _colsum(dh))
        _accumulate(dscale_ref, _colsum(dh * xn * g1))
        _accumulate(dg_ref, _colsum(dh * xn * s1))
        dxn = dh * g1 * s1
        dx_ref[...] = dres_ref[...] + r * (dxn - xn * _rowmean(dxn * xn))

    blk = pl.BlockSpec((tb, d), lambda i: (i, 0))
    return _call(
        body, (xin, dh, dres, g, scale), comm, name=name, grid=(s // tb,),
        in_specs=[blk, blk, blk, _row_spec(d), _row_spec(d)],
        out_specs=[blk, _row_spec(d), _row_spec(d), _row_spec(d)],
        out_shape=[jax.ShapeDtypeStruct((s, d), F32)] + [jax.ShapeDtypeStruct((1, d), F32)] * 3)


def _merge(z_big, y_a, y_b, name):
    s, d = y_a.shape
    tb = _div_tile(s, 256, SUBLANES)

    def body(zg_ref, ya_ref, yb_ref, o_ref):
        ga, gb = zg_ref[:, :d].astype(F32), zg_ref[:, d:].astype(F32)
        o_ref[...] = (_sigmoid(ga) * ya_ref[...].astype(F32) + _sigmoid(gb) * yb_ref[...].astype(F32)).astype(BF16)

    blk = pl.BlockSpec((tb, d), lambda i: (i, 0))
    return pl.pallas_call(
        body, name=name, grid=(s // tb,), in_specs=[pl.BlockSpec((tb, 2 * d), lambda i: (i, 1)), blk, blk],
        out_specs=blk, out_shape=jax.ShapeDtypeStruct((s, d), BF16), compiler_params=_params(),
    )(z_big, y_a, y_b)


def _merge_bwd(dmerged, z_big, y_a, y_b, name):
    s, d = y_a.shape
    tb = _div_tile(s, 256, SUBLANES)

    def body(dm_ref, zg_ref, ya_ref, yb_ref, dya_ref, dyb_ref, dz_ref):
        dm = dm_ref[...].astype(F32)
        sa, sb = _sigmoid(zg_ref[:, :d].astype(F32)), _sigmoid(zg_ref[:, d:].astype(F32))
        dya_ref[...] = (dm * sa).astype(BF16)
        dyb_ref[...] = (dm * sb).astype(BF16)
        dz_ref[:, :d] = (dm * ya_ref[...].astype(F32) * sa * (1.0 - sa)).astype(BF16)
        dz_ref[:, d:] = (dm * yb_ref[...].astype(F32) * sb * (1.0 - sb)).astype(BF16)

    blk = pl.BlockSpec((tb, d), lambda i: (i, 0))
    wide = pl.BlockSpec((tb, 2 * d), lambda i: (i, 1))
    return pl.pallas_call(
        body, name=name, grid=(s // tb,), in_specs=[blk, wide, blk, blk], out_specs=[blk, blk, wide],
        out_shape=[jax.ShapeDtypeStruct((s, d), BF16), jax.ShapeDtypeStruct((s, d), BF16),
                   jax.ShapeDtypeStruct((s, 4 * d), BF16)],
        compiler_params=_params(),
    )(dmerged, z_big, y_a, y_b)


def _causal_mask(ch):
    q = lax.broadcasted_iota(jnp.int32, (ch, ch), 0)
    p = lax.broadcasted_iota(jnp.int32, (ch, ch), 1)
    return (p <= q).astype(F32)


def _gmlp_norm(zc, lng, lnb, gw):
    u_pre, v_pre = zc[:, :gw], zc[:, gw:]
    vg = _gelu(v_pre)
    mu = _rowmean(vg)
    cen = vg - mu
    rstd = lax.rsqrt(_rowmean(cen * cen) + EPS)
    vhat = cen * rstd
    return u_pre, v_pre, _gelu(u_pre), vhat, rstd, vhat * lng + lnb


def _gmlp_fwd(z_big, ln_g, ln_b, w_s, b_s_t, name):
    s = z_big.shape[0]
    groups, ch, _ = w_s.shape
    gw = ln_g.shape[1]
    gd = gw // groups

    def body(z_ref, lng_ref, lnb_ref, ws_ref, bt_ref, a_ref):
        _, _, u, _, _, vn = _gmlp_norm(z_ref[...].astype(F32), lng_ref[...], lnb_ref[...], gw)
        mask = _causal_mask(ch)
        for g in range(groups):
            cols = slice(g * gd, (g + 1) * gd)
            wm = (ws_ref[g] * mask).astype(BF16)
            mixed = _dot(wm, vn[:, cols].astype(BF16), NN) + bt_ref[:, g:g + 1]
            a_ref[:, cols] = (u[:, cols] * mixed).astype(BF16)

    return pl.pallas_call(
        body, name=name, grid=(s // ch,),
        in_specs=[pl.BlockSpec((ch, 2 * gw), lambda n: (n, 0)), _row_spec(gw), _row_spec(gw),
                  pl.BlockSpec((groups, ch, ch), lambda n: (0, 0, 0)), pl.BlockSpec((ch, groups), lambda n: (0, 0))],
        out_specs=pl.BlockSpec((ch, gw), lambda n: (n, 0)),
        out_shape=jax.ShapeDtypeStruct((s, gw), BF16), compiler_params=_params(),
    )(z_big, ln_g, ln_b, w_s, b_s_t)


def _gmlp_bwd(z_big, da, dz_big, ln_g, ln_b, w_s, b_s_t, name, comm=None):
    s = z_big.shape[0]
    groups, ch, _ = w_s.shape
    gw = ln_g.shape[1]
    gd = gw // groups

    def body(z_ref, da_ref, dzin_ref, lng_ref, lnb_ref, ws_ref, bt_ref, dz_ref, gws_ref, gbt_ref, glng_ref, glnb_ref):
        del dzin_ref
        lng = lng_ref[...]
        u_pre, v_pre, u, vhat, rstd, vn = _gmlp_norm(z_ref[...].astype(F32), lng, lnb_ref[...], gw)
        da = da_ref[...].astype(F32)
        mask = _causal_mask(ch)
        first = pl.program_id(0) == 0
        dvn_parts = []
        lane = lax.broadcasted_iota(jnp.int32, (ch, LANES), 1)
        gb = jnp.zeros((ch, LANES), F32)
        for g in range(groups):
            cols = slice(g * gd, (g + 1) * gd)
            wm = (ws_ref[g] * mask).astype(BF16)
            vn_g = vn[:, cols].astype(BF16)
            mixed = _dot(wm, vn_g, NN) + bt_ref[:, g:g + 1]
            dz_ref[:, cols] = (da[:, cols] * mixed * _gelu_grad(u_pre[:, cols])).astype(BF16)
            dmixed = da[:, cols] * u[:, cols]
            dm16 = dmixed.astype(BF16)
            dvn_parts.append(_dot(wm, dm16, TN))
            gws = _dot(dm16, vn_g, NT) * mask

            @pl.when(first)
            def _(g=g, gws=gws):
                gws_ref[g] = gws

            @pl.when(jnp.logical_not(first))
            def _(g=g, gws=gws):
                gws_ref[g] += gws

            gb = gb + jnp.where(lane == g, jnp.sum(dmixed, axis=1, keepdims=True), 0.0)
        _accumulate(gbt_ref, gb)
        dvn = jnp.concatenate(dvn_parts, axis=1)
        _accumulate(glnb_ref, _colsum(dvn))
        _accumulate(glng_ref, _colsum(dvn * vhat))
        dvh = dvn * lng
        dvg = rstd * (dvh - _rowmean(dvh) - vhat * _rowmean(dvh * vhat))
        dz_ref[:, gw:] = (dvg * _gelu_grad(v_pre)).astype(BF16)

    zspec = pl.BlockSpec((ch, 2 * gw), lambda n: (n, 0))
    return _call(
        body, (z_big, da, dz_big, ln_g, ln_b, w_s, b_s_t), comm, name=name, grid=(s // ch,),
        in_specs=[zspec, pl.BlockSpec((ch, gw), lambda n: (n, 0)), pl.BlockSpec(memory_space=HBM),
                  _row_spec(gw), _row_spec(gw), pl.BlockSpec((groups, ch, ch), lambda n: (0, 0, 0)),
                  pl.BlockSpec((ch, groups), lambda n: (0, 0))],
        out_specs=[zspec, pl.BlockSpec((groups, ch, ch), lambda n: (0, 0, 0)),
                   pl.BlockSpec((ch, LANES), lambda n: (0, 0)), _row_spec(gw), _row_spec(gw)],
        out_shape=[jax.ShapeDtypeStruct(dz_big.shape, BF16), jax.ShapeDtypeStruct((groups, ch, ch), F32),
                   jax.ShapeDtypeStruct((ch, LANES), F32), jax.ShapeDtypeStruct((1, gw), F32),
                   jax.ShapeDtypeStruct((1, gw), F32)],
        input_output_aliases={2: 0})


def _mla_prep(z_lat, q_g, kv_g, rope_k, name):
    s, latw = z_lat.shape
    ql, kvl = q_g.shape[1], kv_g.shape[1]
    tb = _div_tile(s, 256, SUBLANES)

    def body(z_ref, qg_ref, kvg_ref, t_ref, qn_ref, kvn_ref, kr_ref):
        q = z_ref[:, :ql]
        qn_ref[...] = ((q * lax.rsqrt(_rowmean(q * q) + EPS)) * qg_ref[...]).astype(BF16)
        kv = z_ref[:, ql:ql + kvl]
        kvn_ref[...] = ((kv * lax.rsqrt(_rowmean(kv * kv) + EPS)) * kvg_ref[...]).astype(BF16)
        kk = z_ref[:, ql + kvl:] * t_ref[...]
        kr_ref[...] = (kk + pltpu.roll(kk, ROPE, axis=1)).astype(BF16)

    return pl.pallas_call(
        body, name=name, grid=(s // tb,),
        in_specs=[pl.BlockSpec((tb, latw), lambda i: (i, 0)), _row_spec(ql), _row_spec(kvl),
                  pl.BlockSpec((tb, 2 * ROPE), lambda i: (i, 0))],
        out_specs=[pl.BlockSpec((tb, ql), lambda i: (i, 0)), pl.BlockSpec((tb, kvl), lambda i: (i, 0)),
                   pl.BlockSpec((tb, 2 * ROPE), lambda i: (i, 0))],
        out_shape=[jax.ShapeDtypeStruct((s, ql), BF16), jax.ShapeDtypeStruct((s, kvl), BF16),
                   jax.ShapeDtypeStruct((s, 2 * ROPE), BF16)],
        compiler_params=_params(),
    )(z_lat, q_g, kv_g, rope_k)


def _attn_fwd(q, kv, kr, heads, name, comm=None):
    s = q.shape[0]
    t = _div_tile(s, 512)
    nb = s // t
    hp = 2 if heads % 2 == 0 else 1

    def body(q_ref, k_ref, kr_ref, v_ref, o_ref, lse_ref, m_ref, l_ref, acc_ref):
        i, j = pl.program_id(1), pl.program_id(2)

        @pl.when(j == 0)
        def _():
            m_ref[...] = jnp.full(m_ref.shape, -1e30, F32)
            l_ref[...] = jnp.zeros(l_ref.shape, F32)
            acc_ref[...] = jnp.zeros(acc_ref.shape, F32)

        def update(h, rows, n_keys, on_diagonal):
            vc = slice(h * VHEAD, (h + 1) * VHEAD)
            k_full = jnp.concatenate([k_ref[:n_keys, h * NOPE:(h + 1) * NOPE], kr_ref[:n_keys, :]], axis=1)
            sc = _dot(q_ref[rows, h * HEAD_W:(h + 1) * HEAD_W], k_full, NT)
            if on_diagonal:
                row_pos = rows.start + lax.broadcasted_iota(jnp.int32, sc.shape, 0)
                sc = jnp.where(lax.broadcasted_iota(jnp.int32, sc.shape, 1) <= row_pos, sc, -1e30)
            m_old = m_ref[h, rows, :]
            m_new = jnp.maximum(m_old, jnp.max(sc, axis=-1, keepdims=True))
            p = jnp.exp(sc - m_new)
            alpha = jnp.exp(m_old - m_new)
            l_new = alpha * l_ref[h, rows, :] + jnp.sum(p, axis=-1, keepdims=True)
            acc = alpha * acc_ref[rows, vc] + _dot(p.astype(BF16), v_ref[:n_keys, vc], NN)
            if on_diagonal:
                o_ref[rows, vc] = (acc / l_new).astype(BF16)
                lse_ref[h, rows, :] = jnp.broadcast_to(m_new + jnp.log(l_new), (rows.stop - rows.start, LANES))
            else:
                m_ref[h, rows, :], l_ref[h, rows, :], acc_ref[rows, vc] = m_new, l_new, acc

        def below_diagonal():
            for h in range(hp):
                update(h, slice(0, t), t, False)

        def on_diagonal():
            for h in range(hp):
                update(h, slice(0, t // 2), t // 2, True)
                update(h, slice(t // 2, t), t, True)

        pl.when(j < i)(below_diagonal)
        pl.when(j == i)(on_diagonal)

    kidx = lambda off: (lambda h, i, j: (jnp.minimum(i, j), off(h)))
    return _call(
        body, (q, kv, kr, kv), comm, name=name, grid=(heads // hp, nb, nb),
        in_specs=[pl.BlockSpec((t, hp * HEAD_W), lambda h, i, j: (i, h)),
                  pl.BlockSpec((t, hp * NOPE), kidx(lambda h: h)),
                  pl.BlockSpec((t, 2 * ROPE), kidx(lambda h: 0)),
                  pl.BlockSpec((t, hp * VHEAD), kidx(lambda h: heads // hp + h))],
        out_specs=[pl.BlockSpec((t, hp * VHEAD), lambda h, i, j: (i, h)),
                   pl.BlockSpec((hp, t, LANES), lambda h, i, j: (h, i, 0))],
        out_shape=[jax.ShapeDtypeStruct((s, heads * VHEAD), BF16), jax.ShapeDtypeStruct((heads, s, LANES), F32)],
        scratch_shapes=[pltpu.VMEM((hp, t, 1), F32), pltpu.VMEM((hp, t, 1), F32), pltpu.VMEM((t, hp * VHEAD), F32)])


def _attn_bwd(q, kv, kr, o, do, lse, heads, name, comm=None):
    s = q.shape[0]
    t = _div_tile(s, 512)
    nb = s // t
    hp = 2 if heads % 2 == 0 else 1

    def body(q_ref, k_ref, kr_ref, v_ref, o_ref, do_ref, lse_ref, dq_ref, dk_ref, dv_ref, dk_acc, dv_acc):
        j, i = pl.program_id(1), pl.program_id(2)

        @pl.when(jnp.logical_and(j == 0, i == 0))
        def _():
            dq_ref[...] = jnp.zeros(dq_ref.shape, F32)

        def update(h, rows, n_keys, on_diagonal, assign):
            qc, kc, vc = (slice(h * w, (h + 1) * w) for w in (HEAD_W, NOPE, VHEAD))
            n_rows = rows.stop - rows.start
            qv, do_v = q_ref[rows, qc], do_ref[rows, vc]
            k_full = jnp.concatenate([k_ref[:n_keys, kc], kr_ref[:n_keys, :]], axis=1)
            sc = _dot(qv, k_full, NT)
            if on_diagonal:
                row_pos = rows.start + lax.broadcasted_iota(jnp.int32, sc.shape, 0)
                sc = jnp.where(lax.broadcasted_iota(jnp.int32, sc.shape, 1) <= row_pos, sc, -1e30)
            p = jnp.exp(sc - lse_ref[h, rows, :1])
            dp = _dot(do_v, v_ref[:n_keys, vc], NT)
            delta = jnp.sum(do_v.astype(F32) * o_ref[rows, vc].astype(F32), axis=-1, keepdims=True)
            ds = (p * (dp - delta)).astype(BF16)
            dq_ref[pl.ds(pl.multiple_of(i * t + rows.start, n_rows), n_rows), qc] += _dot(ds, k_full, NN)
            dv_part, dk_part = _dot(p.astype(BF16), do_v, TN), _dot(ds, qv, TN)
            if assign:
                dv_acc[:n_keys, vc], dk_acc[:n_keys, qc] = dv_part, dk_part
            else:
                dv_acc[:n_keys, vc] += dv_part
                dk_acc[:n_keys, qc] += dk_part

        def on_diagonal():
            for h in range(hp):
                update(h, slice(t // 2, t), t, True, True)
                update(h, slice(0, t // 2), t // 2, True, False)

        def below_diagonal():
            for h in range(hp):
                update(h, slice(0, t), t, False, False)

        pl.when(i == j)(on_diagonal)
        pl.when(i > j)(below_diagonal)

        @pl.when(i == nb - 1)
        def _():
            dk_ref[...] = dk_acc[...].astype(BF16)
            dv_ref[...] = dv_acc[...].astype(BF16)

    qidx = lambda h, j, i: (jnp.maximum(i, j), h)
    return _call(
        body, (q, kv, kr, kv, o, do, lse), comm, name=name, grid=(heads // hp, nb, nb),
        in_specs=[pl.BlockSpec((t, hp * HEAD_W), qidx),
                  pl.BlockSpec((t, hp * NOPE), lambda h, j, i: (j, h)),
                  pl.BlockSpec((t, 2 * ROPE), lambda h, j, i: (j, 0)),
                  pl.BlockSpec((t, hp * VHEAD), lambda h, j, i: (j, heads // hp + h)),
                  pl.BlockSpec((t, hp * VHEAD), qidx), pl.BlockSpec((t, hp * VHEAD), qidx),
                  pl.BlockSpec((hp, t, LANES), lambda h, j, i: (h, jnp.maximum(i, j), 0))],
        out_specs=[pl.BlockSpec((s, hp * HEAD_W), lambda h, j, i: (0, h)),
                   pl.BlockSpec((t, hp * HEAD_W), lambda h, j, i: (j, h)),
                   pl.BlockSpec((t, hp * VHEAD), lambda h, j, i: (j, h))],
        out_shape=[jax.ShapeDtypeStruct((s, heads * HEAD_W), F32), jax.ShapeDtypeStruct((s, heads * HEAD_W), BF16),
                   jax.ShapeDtypeStruct((s, heads * VHEAD), BF16)],
        scratch_shapes=[pltpu.VMEM((t, hp * HEAD_W), F32), pltpu.VMEM((t, hp * VHEAD), F32)])


def _mla_bwd_mid(dq, dk, dv, rope_q, rope_k, heads, name):
    s = dq.shape[0]
    tb = _div_tile(s, 256, SUBLANES)

    def body(dq_ref, dk_ref, dv_ref, tq_ref, tk_ref, dqb_ref, dkv_ref, dkk_ref):
        tq = tq_ref[...]
        dkr = jnp.zeros((tb, 2 * ROPE), F32)
        for h in range(heads):
            cols = slice(h * HEAD_W, (h + 1) * HEAD_W)
            dqb_ref[:, cols] = (dq_ref[:, cols] * tq).astype(BF16)
            dkv_ref[:, h * NOPE:(h + 1) * NOPE] = dk_ref[:, h * HEAD_W:h * HEAD_W + NOPE]
            dkr = dkr + dk_ref[:, h * HEAD_W + NOPE:(h + 1) * HEAD_W].astype(F32)
        dkv_ref[:, heads * NOPE:] = dv_ref[...]
        dkk_ref[...] = (dkr + pltpu.roll(dkr, ROPE, axis=1)) * tk_ref[...]

    wq, wv = heads * HEAD_W, heads * VHEAD
    return pl.pallas_call(
        body, name=name, grid=(s // tb,),
        in_specs=[pl.BlockSpec((tb, wq), lambda i: (i, 0)), pl.BlockSpec((tb, wq), lambda i: (i, 0)),
                  pl.BlockSpec((tb, wv), lambda i: (i, 0)), pl.BlockSpec((tb, HEAD_W), lambda i: (i, 0)),
                  pl.BlockSpec((tb, 2 * ROPE), lambda i: (i, 0))],
        out_specs=[pl.BlockSpec((tb, wq), lambda i: (i, 0)), pl.BlockSpec((tb, heads * NOPE + wv), lambda i: (i, 0)),
                   pl.BlockSpec((tb, 2 * ROPE), lambda i: (i, 0))],
        out_shape=[jax.ShapeDtypeStruct((s, wq), BF16), jax.ShapeDtypeStruct((s, heads * NOPE + wv), BF16),
                   jax.ShapeDtypeStruct((s, 2 * ROPE), F32)],
        compiler_params=_params(),
    )(dq, dk, dv, rope_q, rope_k)


def _mla_bwd_post(z_lat, dqn, dkvn, dkk, q_g, kv_g, name):
    s, latw = z_lat.shape
    ql, kvl = q_g.shape[1], kv_g.shape[1]
    tb = _div_tile(s, 256, SUBLANES)

    def norm_bwd(xv, dn, g, dg_ref):
        r = lax.rsqrt(_rowmean(xv * xv) + EPS)
        xh = xv * r
        _accumulate(dg_ref, _colsum(dn * xh))
        dxh = dn * g
        return r * (dxh - xh * _rowmean(dxh * xh))

    def body(z_ref, dqn_ref, dkvn_ref, dkk_ref, qg_ref, kvg_ref, dz_ref, gq_ref, gkv_ref):
        dz_ref[:, :ql] = norm_bwd(z_ref[:, :ql], dqn_ref[...], qg_ref[...], gq_ref).astype(BF16)
        dz_ref[:, ql:ql + kvl] = norm_bwd(z_ref[:, ql:ql + kvl], dkvn_ref[...], kvg_ref[...], gkv_ref).astype(BF16)
        dz_ref[:, ql + kvl:] = dkk_ref[...].astype(BF16)

    return pl.pallas_call(
        body, name=name, grid=(s // tb,),
        in_specs=[pl.BlockSpec((tb, latw), lambda i: (i, 0)), pl.BlockSpec((tb, ql), lambda i: (i, 0)),
                  pl.BlockSpec((tb, kvl), lambda i: (i, 0)), pl.BlockSpec((tb, 2 * ROPE), lambda i: (i, 0)),
                  _row_spec(ql), _row_spec(kvl)],
        out_specs=[pl.BlockSpec((tb, latw), lambda i: (i, 0)), _row_spec(ql), _row_spec(kvl)],
        out_shape=[jax.ShapeDtypeStruct((s, latw), BF16), jax.ShapeDtypeStruct((1, ql), F32),
                   jax.ShapeDtypeStruct((1, kvl), F32)],
        compiler_params=_params(),
    )(z_lat, dqn, dkvn, dkk, q_g, kv_g)


CONV_ROWS = 128
CONV_HALO = 16


def _row_steps(n_rows, step):
    step(0, True)
    if n_rows > CONV_ROWS:
        def later(i, carry):
            step(pl.multiple_of(i * CONV_ROWS, CONV_ROWS), False)
            return carry
        lax.fori_loop(1, n_rows // CONV_ROWS, later, 0)


def _conv_taps(pre_ref, r0, first):
    if first:
        win = jnp.concatenate([jnp.zeros((CONV_HALO, pre_ref.shape[1]), F32), pre_ref[0:CONV_ROWS, :].astype(F32)])
    else:
        win = pre_ref[pl.ds(pl.multiple_of(r0 - CONV_HALO, CONV_HALO), CONV_ROWS + CONV_HALO), :].astype(F32)
    return win[CONV_HALO:], pltpu.roll(win, 1, axis=0)[CONV_HALO:], pltpu.roll(win, 2, axis=0)[CONV_HALO:]


def _conv(taps, w_ref, b_ref):
    return w_ref[2:3, :] * taps[0] + w_ref[1:2, :] * taps[1] + w_ref[0:1, :] * taps[2] + b_ref[...]


def _conv_fwd(up_pre, conv_w, conv_b, name):
    s, ff2 = up_pre.shape
    ff = ff2 // 2
    tc = _div_tile(ff, 256)
    nb = ff // tc
    assert s % CONV_ROWS == 0

    def body(pg_ref, pv_ref, wg_ref, wv_ref, bg_ref, bv_ref, act_ref):
        def step(r0, first):
            gate = _conv(_conv_taps(pg_ref, r0, first), wg_ref, bg_ref)
            val = _conv(_conv_taps(pv_ref, r0, first), wv_ref, bv_ref)
            act_ref[pl.ds(r0, CONV_ROWS), :] = (gate * _sigmoid(gate) * val).astype(BF16)

        _row_steps(s, step)

    def col(rows, off):
        return pl.BlockSpec((rows, tc), lambda j: (0, j + off))

    return pl.pallas_call(
        body, name=name, grid=(nb,),
        in_specs=[col(s, 0), col(s, nb), col(CONV_TAPS, 0), col(CONV_TAPS, nb), col(1, 0), col(1, nb)],
        out_specs=col(s, 0), out_shape=jax.ShapeDtypeStruct((s, ff), BF16), compiler_params=_params(),
    )(up_pre, up_pre, conv_w, conv_w, conv_b, conv_b)


def _conv_bwd(up_pre, dact, conv_w, conv_b, name, comm=None):
    s, ff2 = up_pre.shape
    ff = ff2 // 2
    tc = _div_tile(ff, 256)
    nb = ff // tc
    assert s % CONV_ROWS == 0

    def body(pg_ref, pv_ref, da_ref, wg_ref, wv_ref, bg_ref, bv_ref, dup_ref, gwg_ref, gwv_ref, gbg_ref, gbv_ref,
             dxg_ref, dxv_ref):
        for ref in (gwg_ref, gwv_ref, gbg_ref, gbv_ref):
            ref[...] = jnp.zeros(ref.shape, F32)
        for ref in (dxg_ref, dxv_ref):
            ref[s:s + SUBLANES, :] = jnp.zeros((SUBLANES, tc), F32)

        def sums(taps, dx, gw_ref, gb_ref):
            gb_ref[...] += _colsum(dx)
            for k in range(CONV_TAPS):
                gw_ref[k:k + 1, :] += _colsum(dx * taps[CONV_TAPS - 1 - k])

        def forward(r0, first):
            rows = pl.ds(r0, CONV_ROWS)
            taps_g, taps_v = _conv_taps(pg_ref, r0, first), _conv_taps(pv_ref, r0, first)
            gate, val = _conv(taps_g, wg_ref, bg_ref), _conv(taps_v, wv_ref, bv_ref)
            da = da_ref[rows, :].astype(F32)
            sg = _sigmoid(gate)
            dxv, dxg = da * gate * sg, da * val * sg * (1.0 + gate * (1.0 - sg))
            dxv_ref[rows, :], dxg_ref[rows, :] = dxv, dxg
            sums(taps_v, dxv, gwv_ref, gbv_ref)
            sums(taps_g, dxg, gwg_ref, gbg_ref)

        def backward(r0, first):
            del first
            n = CONV_ROWS + SUBLANES
            for dx_ref, w_ref, out_ref in ((dxg_ref, wg_ref, dup_ref.at[0]), (dxv_ref, wv_ref, dup_ref.at[1])):
                win = dx_ref[pl.ds(r0, n), :]
                ahead1 = pltpu.roll(win, n - 1, axis=0)[:CONV_ROWS]
                ahead2 = pltpu.roll(win, n - 2, axis=0)[:CONV_ROWS]
                out_ref[pl.ds(r0, CONV_ROWS), :] = (w_ref[2:3, :] * win[:CONV_ROWS] + w_ref[1:2, :] * ahead1
                                                    + w_ref[0:1, :] * ahead2).astype(BF16)

        _row_steps(s, forward)
        _row_steps(s, backward)

    def col(rows, off):
        return pl.BlockSpec((rows, tc), lambda j: (0, j + off))

    return _call(
        body, (up_pre, up_pre, dact, conv_w, conv_w, conv_b, conv_b), comm, name=name, grid=(nb,),
        in_specs=[col(s, 0), col(s, nb), col(s, 0), col(CONV_TAPS, 0), col(CONV_TAPS, nb), col(1, 0), col(1, nb)],
        out_specs=[pl.BlockSpec((2, s, tc), lambda j: (0, 0, j)), col(CONV_TAPS, 0), col(CONV_TAPS, 0),
                   col(1, 0), col(1, 0)],
        out_shape=[jax.ShapeDtypeStruct((2, s, ff), BF16)] + [jax.ShapeDtypeStruct((CONV_TAPS, ff), F32)] * 2
        + [jax.ShapeDtypeStruct((1, ff), F32)] * 2,
        scratch_shapes=[pltpu.VMEM((s + SUBLANES, tc), F32)] * 2)


def _ada_fwd(c_all, w, b, name):
    nseq, d = c_all.shape
    na = w.shape[1]
    tn = _div_tile(na, 512)

    def body(c_ref, w_ref, b_ref, o_ref):
        cv = c_ref[...]
        sc = cv * _sigmoid(cv)
        o_ref[...] = jnp.dot(sc, w_ref[...], preferred_element_type=F32, precision=lax.Precision.HIGHEST) + b_ref[...]

    return pl.pallas_call(
        body, name=name, grid=(na // tn,),
        in_specs=[pl.BlockSpec((nseq, d), lambda j: (0, 0)), pl.BlockSpec((d, tn), lambda j: (0, j)),
                  pl.BlockSpec((1, tn), lambda j: (0, j))],
        out_specs=pl.BlockSpec((nseq, tn), lambda j: (0, j)),
        out_shape=jax.ShapeDtypeStruct((nseq, na), F32), compiler_params=_params(),
    )(c_all, w, b)


def _ada_bwd(c_all_t, dmod, name):
    d, nseq = c_all_t.shape
    na = dmod.shape[1]
    tm, tn = _div_tile(d, 256, SUBLANES), _div_tile(na, 512)

    def body(c_ref, dm_ref, o_ref):
        cv = c_ref[...]
        sc = cv * _sigmoid(cv)
        acc = sc[:, 0:1] * dm_ref[0:1, :]
        for bi in range(1, nseq):
            acc = acc + sc[:, bi:bi + 1] * dm_ref[bi:bi + 1, :]
        o_ref[...] = acc

    return pl.pallas_call(
        body, name=name, grid=(d // tm, na // tn),
        in_specs=[pl.BlockSpec((tm, nseq), lambda i, j: (i, 0)), pl.BlockSpec((nseq, tn), lambda i, j: (0, j))],
        out_specs=pl.BlockSpec((tm, tn), lambda i, j: (i, j)),
        out_shape=jax.ShapeDtypeStruct((d, na), F32), compiler_params=_params(),
    )(c_all_t, dmod)


def _adamw(w, g, m, v, name, comm=None, after=None):
    rows, cols = w.shape
    tb = _div_tile(rows, max(SUBLANES, (256 * 1024) // cols // SUBLANES * SUBLANES), SUBLANES)
    c1 = 1.0 / (1.0 - ADAM_B1 ** ADAM_STEP)
    c2 = 1.0 / (1.0 - ADAM_B2 ** ADAM_STEP)

    def body(*refs):
        w_ref, g_ref, m_ref, v_ref = refs[:4]
        d_ref, nm_ref, nv_ref = refs[-3:]
        gv = g_ref[...]
        nm = ADAM_B1 * m_ref[...] + (1.0 - ADAM_B1) * gv
        nv = ADAM_B2 * v_ref[...] + (1.0 - ADAM_B2) * (gv * gv)
        nm_ref[...] = nm
        nv_ref[...] = nv
        d_ref[...] = -ADAM_LR * ((nm * c1) / (jnp.sqrt(nv * c2) + ADAM_EPS) + ADAM_WD * w_ref[...])

    blk = pl.BlockSpec((tb, cols), lambda i: (i, 0))
    operands, in_specs = (w, g, m, v), [blk] * 4
    if after is not None:
        operands, in_specs = operands + (after,), in_specs + [pl.BlockSpec(after.shape, lambda i: (0, 0))]
    return _call(body, operands, comm, name=name, grid=(rows // tb,), in_specs=in_specs, out_specs=[blk] * 3,
                 out_shape=[jax.ShapeDtypeStruct((rows, cols), F32)] * 3)


def _sum_leading(parts, name, after=()):
    n, rows, cols = parts.shape
    tb = _div_tile(rows, 512, SUBLANES)

    def body(p_ref, *rest):
        o_ref = rest[-1]
        acc = p_ref[0]
        for k in range(1, n):
            acc = acc + p_ref[k]
        o_ref[...] = acc

    return pl.pallas_call(
        body, name=name, grid=(rows // tb,),
        in_specs=[pl.BlockSpec((n, tb, cols), lambda i: (0, i, 0))] + [pl.BlockSpec(memory_space=pl.ANY)] * len(after),
        out_specs=pl.BlockSpec((tb, cols), lambda i: (i, 0)),
        out_shape=jax.ShapeDtypeStruct((rows, cols), F32), compiler_params=_params(),
    )(parts, *after)


def _place():
    x, y, c = lax.axis_index("x"), lax.axis_index("y"), lax.axis_index("c")
    return x, y, c, [(1 - x, y), (x, 1 - y), (1 - x, 1 - y)]


def _all_gather(block, name):
    m_per, n = block.shape

    def body(x_ref, out_ref, send_sems, recv_sems, local_sem):
        x, y, c, chips = _place()
        me, sibling = (x, y, c), (x, y, 1 - c)

        def rows(px, py, pc):
            return out_ref.at[pl.ds((4 * px + 2 * py + pc) * m_per, m_per), :]

        def copy(k, blk, to, src=None):
            return pltpu.make_async_remote_copy(
                src_ref=rows(*blk) if src is None else src, dst_ref=rows(*blk), send_sem=send_sems.at[k],
                recv_sem=recv_sems.at[k], device_id=to, device_id_type=MESH)

        mine = pltpu.make_async_copy(x_ref, rows(*me), local_sem)
        mine.start()
        first = [copy(0, me, sibling, src=x_ref)]
        first += [copy(1 + j, me, (*chip, c), src=x_ref) for j, chip in enumerate(chips)]
        for cp in first:
            cp.start()
        passed = [copy(4 + j, (*chip, c), sibling) for j, chip in enumerate(chips)]
        for j, chip in enumerate(chips):
            copy(1 + j, (*chip, c), me).wait_recv()
            passed[j].start()
        copy(0, sibling, me).wait_recv()
        for j, chip in enumerate(chips):
            copy(4 + j, (*chip, 1 - c), me).wait_recv()
        for cp in first + passed:
            cp.wait_send()
        mine.wait()

    return pl.pallas_call(
        body, name=name, out_shape=jax.ShapeDtypeStruct((N_DEV * m_per, n), block.dtype),
        in_specs=[pl.BlockSpec(memory_space=pltpu.VMEM)], out_specs=pl.BlockSpec(memory_space=pltpu.VMEM),
        scratch_shapes=[pltpu.SemaphoreType.DMA((7,)), pltpu.SemaphoreType.DMA((7,)), pltpu.SemaphoreType.DMA],
        compiler_params=_params(),
    )(block)


def _hbm_specs(n):
    return [pl.BlockSpec(memory_space=HBM)] * n


def _part(ref, by_cols, half, quarter=None, lead=None):
    extent = ref.shape[-1] if by_cols else ref.shape[-2]
    size = extent // 2 if quarter is None else extent // 4
    first = half * (extent // 2) + (0 if quarter is None else quarter * size)
    tile = LANES if by_cols else 2 * SUBLANES
    span = pl.ds(pl.multiple_of(first, tile) if size % tile == 0 else first, size)
    index = (slice(None), span) if by_cols else (span, slice(None))
    return ref.at[index] if lead is None else ref.at[(lead,) + index]


def _half_rows(ref, half, lead=None):
    return _part(ref, False, half, lead=lead)


class _Comm:
    def __init__(self, operands, out_shape, sem_dims, build, aliases=None):
        self.operands, self.out_shape, self.sem_dims = list(operands), list(out_shape), list(sem_dims)
        self.scratch = [pltpu.SemaphoreType.DMA(d) for d in sem_dims]
        self.build, self.aliases = build, dict(aliases or {})


class _SemGrid:
    def __init__(self, sems, dims):
        self.sems, self.dims, self.at = list(sems), tuple(dims), self

    def __getitem__(self, index):
        index = index if isinstance(index, tuple) else (index,)
        flat = 0
        for i, d in zip(index, self.dims):
            flat = flat * d + i
        return self.sems[flat]


def _call(body, operands, comm=None, *, name, grid, in_specs, out_specs, out_shape, scratch_shapes=(),
          input_output_aliases=None):
    aliases = dict(input_output_aliases or {})
    if comm is None:
        return pl.pallas_call(
            body, name=name, grid=grid, in_specs=in_specs, out_specs=out_specs, out_shape=out_shape,
            scratch_shapes=list(scratch_shapes), input_output_aliases=aliases, compiler_params=_params())(*operands)
    single = not isinstance(out_shape, (list, tuple))
    outs = [out_shape] if single else list(out_shape)
    ospecs = [out_specs] if single else list(out_specs)
    n_in, n_out, n_scr = len(operands), len(outs), len(scratch_shapes)
    c_in, c_out = len(comm.operands), len(comm.out_shape)
    for i, o in comm.aliases.items():
        aliases[n_in + i] = n_out + o

    def hosted(*refs):
        ins, c_ins = refs[:n_in], refs[n_in:n_in + c_in]
        o0 = n_in + c_in
        o_refs, c_outs = refs[o0:o0 + n_out], refs[o0 + n_out:o0 + n_out + c_out]
        s0 = o0 + n_out + c_out
        scr, sems = refs[s0:s0 + n_scr], refs[s0 + n_scr:]
        stages = comm.build(c_ins, c_outs, sems)
        step, n_steps = 0, 1
        for dim, size in enumerate(grid):
            step, n_steps = step * size + pl.program_id(dim), n_steps * size
        pl.when(step == 0)(stages[0])
        body(*ins, *o_refs, *scr)
        for stage in stages[1:-1]:
            pl.when(step == (n_steps * MIDDLE_STAGE_AT) // 100)(stage)
        pl.when(step == n_steps - 1)(stages[-1])

    res = pl.pallas_call(
        hosted, name=name, grid=grid, in_specs=list(in_specs) + _hbm_specs(c_in),
        out_specs=ospecs + _hbm_specs(c_out), out_shape=outs + comm.out_shape,
        scratch_shapes=list(scratch_shapes) + comm.scratch, input_output_aliases=aliases,
        compiler_params=_params())(*operands, *comm.operands)
    return (res[0] if single else res[:n_out]), res[n_out:]


def _run_comm(comm, name):
    c_in, c_out = len(comm.operands), len(comm.out_shape)

    def body(*refs):
        for stage in comm.build(refs[:c_in], refs[c_in:c_in + c_out], refs[c_in + c_out:]):
            stage()

    return pl.pallas_call(
        body, name=name, in_specs=_hbm_specs(c_in), out_specs=_hbm_specs(c_out), out_shape=comm.out_shape,
        scratch_shapes=comm.scratch, input_output_aliases=comm.aliases, compiler_params=_params())(*comm.operands)


def _join_comms(comms):
    def build(in_refs, out_refs, sems):
        staged, i, o, k = [], 0, 0, 0
        for cm in comms:
            ni, no, ns = len(cm.operands), len(cm.out_shape), len(cm.sem_dims)
            staged.append(cm.build(in_refs[i:i + ni], out_refs[o:o + no], sems[k:k + ns]))
            i, o, k = i + ni, o + no, k + ns
        def run(fns):
            def stage():
                for fn in fns:
                    fn()
            return stage

        return (run([st[0] for st in staged]), run([fn for st in staged for fn in st[1:-1]]),
                run([st[-1] for st in staged]))

    aliases, i, o = {}, 0, 0
    for cm in comms:
        aliases.update({i + a: o + b for a, b in cm.aliases.items()})
        i, o = i + len(cm.operands), o + len(cm.out_shape)
    return _Comm(sum((cm.operands for cm in comms), []), sum((cm.out_shape for cm in comms), []),
                 sum((cm.sem_dims for cm in comms), []), build, aliases)


def _gather8_comm(block):
    def build(in_refs, out_refs, sems):
        (src,), (out,), (send_sems, recv_sems) = in_refs, out_refs, sems
        x, y, c, chips = _place()
        me, sibling = (x, y, c), (x, y, 1 - c)

        def copy(k, blk, to, own=False):
            dst = out.at[4 * blk[0] + 2 * blk[1] + blk[2]]
            return pltpu.make_async_remote_copy(
                src_ref=src if own else dst, dst_ref=dst, send_sem=send_sems.at[k], recv_sem=recv_sems.at[k],
                device_id=to, device_id_type=MESH)

        first = [copy(0, me, sibling, own=True)] + [copy(1 + j, me, (*chip, c), own=True)
                                                     for j, chip in enumerate(chips)]
        passed = [copy(4 + j, (*chip, c), sibling) for j, chip in enumerate(chips)]

        def start():
            for cp in first:
                cp.start()

        def middle():
            for j, chip in enumerate(chips):
                copy(1 + j, (*chip, c), me).wait_recv()
                passed[j].start()

        def finish():
            copy(0, sibling, me).wait_recv()
            for j, chip in enumerate(chips):
                copy(4 + j, (*chip, 1 - c), me).wait_recv()
            for cp in first + passed:
                cp.wait_send()

        return start, middle, finish

    return _Comm([block], [jax.ShapeDtypeStruct((N_DEV,) + block.shape, block.dtype)], [(7,), (7,)], build)


def _gather_comm(shards, by_cols=()):
    nw = len(shards)

    def build(in_refs, out_refs, sems):
        send_sems, recv_sems = sems
        x, y, c, chips = _place()
        me, sibling = (x, y, c), (x, y, 1 - c)
        across_x, across_y, diagonal = chips

        def copy(w, k, block, part, to, src=None):
            dst = _part(out_refs[w], w in by_cols, part[1], part[2] if part[0] else None, 2 * block[0] + block[1])
            return pltpu.make_async_remote_copy(
                src_ref=dst if src is None else src, dst_ref=dst, send_sem=send_sems.at[w, k],
                recv_sem=recv_sems.at[w, k], device_id=to, device_id_type=MESH)

        first = [copy(w, j, (x, y), (0, c), (*chip, c), src=_part(in_refs[w], w in by_cols, c))
                 for w in range(nw) for j, chip in enumerate((across_x, across_y))]
        passed = [[copy(w, 2, across_x, (1, c, 0), (*across_y, c)), copy(w, 3, across_y, (1, c, 1), (*across_x, c)),
                   copy(w, 4, across_x, (0, c), sibling), copy(w, 5, across_y, (0, c), sibling)] for w in range(nw)]
        last = [[copy(w, 6, diagonal, (1, c, 0), sibling), copy(w, 7, diagonal, (1, c, 1), sibling)]
                for w in range(nw)]

        def start():
            for cp in first:
                cp.start()

        def middle():
            for w in range(nw):
                copy(w, 0, across_x, (0, c), me).wait_recv()
                copy(w, 1, across_y, (0, c), me).wait_recv()
                for cp in passed[w]:
                    cp.start()

        def finish():
            for w in range(nw):
                copy(w, 2, diagonal, (1, c, 0), me).wait_recv()
                copy(w, 3, diagonal, (1, c, 1), me).wait_recv()
                for cp in last[w]:
                    cp.start()
            for w in range(nw):
                for k, block, part in ((4, across_x, (0, 1 - c)), (5, across_y, (0, 1 - c)),
                                       (6, diagonal, (1, 1 - c, 0)), (7, diagonal, (1, 1 - c, 1))):
                    copy(w, k, block, part, me).wait_recv()
            for cp in first + sum(passed, []) + sum(last, []):
                cp.wait_send()

        return start, middle, finish

    return _Comm(shards, [jax.ShapeDtypeStruct((N_CHIPS,) + w.shape, w.dtype) for w in shards],
                 [(nw, 8), (nw, 8)], build)


def _halved(shape, by_cols):
    return shape[:-1] + (shape[-1] // 2,) if by_cols else shape[:-2] + (shape[-2] // 2, shape[-1])


def _swap_comm(gs, by_cols=()):
    nw = len(gs)

    def build(in_refs, out_refs, sems):
        send_sems, recv_sems = sems
        x, y, c, _ = _place()
        cps = []
        for w in range(nw):
            cps.append(pltpu.make_async_remote_copy(
                src_ref=_part(in_refs[w], w in by_cols, 1 - c, lead=slice(None)), dst_ref=out_refs[w],
                send_sem=send_sems.at[w], recv_sem=recv_sems.at[w], device_id=(x, y, 1 - c), device_id_type=MESH))

        def start():
            for cp in cps:
                cp.start()

        def finish():
            for cp in cps:
                cp.wait()

        return start, finish

    return _Comm(gs, [jax.ShapeDtypeStruct(_halved(g.shape, w in by_cols), g.dtype) for w, g in enumerate(gs)],
                 [(nw,), (nw,)], build)


def _exchange_comm(s1s):
    nw = len(s1s)

    def build(in_refs, out_refs, sems):
        send_sems, recv_sems = sems
        x, y, c, chips = _place()
        cps = [pltpu.make_async_remote_copy(
            src_ref=in_refs[w].at[2 * chip[0] + chip[1]], dst_ref=out_refs[w].at[j], send_sem=send_sems.at[w, j],
            recv_sem=recv_sems.at[w, j], device_id=(*chip, c), device_id_type=MESH)
            for w in range(nw) for j, chip in enumerate(chips)]

        def start():
            for cp in cps:
                cp.start()

        def finish():
            for cp in cps:
                cp.wait()

        return start, finish

    return _Comm(s1s, [jax.ShapeDtypeStruct((N_CHIPS - 1,) + s.shape[1:], s.dtype) for s in s1s],
                 [(nw, 3), (nw, 3)], build)


def _size(dims):
    n = 1
    for d in dims:
        n *= d
    return n


def _sem_grids(comm, sem_refs):
    grids, pos = [], 0
    for dims in comm.sem_dims:
        grids.append(_SemGrid(sem_refs[pos:pos + _size(dims)], dims))
        pos += _size(dims)
    return grids


def _comm_split_start(comm, name, after=()):
    c_in, c_out = len(comm.operands), len(comm.out_shape)
    counts = [_size(d) for d in comm.sem_dims]
    n_sem = sum(counts)
    assert not comm.aliases

    def body(*refs):
        srcs, lands = refs[:c_in], refs[c_in:c_in + c_out]
        first_sem = c_in + c_out + len(after)
        start, _ = comm.build(srcs, lands, _sem_grids(comm, refs[first_sem:first_sem + n_sem]))
        start()
        refs[-1][...] = jnp.zeros(refs[-1].shape, refs[-1].dtype)

    lands = [pltpu.with_memory_space_constraint(lax.empty(o.shape, o.dtype), HBM) for o in comm.out_shape]
    srcs = [pltpu.with_memory_space_constraint(a, HBM) for a in comm.operands]
    res = pl.pallas_call(
        body, name=name, in_specs=_hbm_specs(c_in + c_out) + [pl.BlockSpec(memory_space=pl.ANY)] * len(after),
        out_specs=[pl.BlockSpec(memory_space=pltpu.SEMAPHORE)] * n_sem + _hbm_specs(c_in + c_out)
        + [pl.BlockSpec(memory_space=pltpu.VMEM)],
        out_shape=[pltpu.SemaphoreType.DMA(())] * n_sem + [pltpu.HBM(a.shape, a.dtype) for a in comm.operands]
        + [pltpu.HBM(o.shape, o.dtype) for o in comm.out_shape] + [jax.ShapeDtypeStruct((SUBLANES, LANES), F32)],
        input_output_aliases={i: n_sem + i for i in range(c_in + c_out)},
        compiler_params=_params(has_side_effects=pltpu.SideEffectType.DATAFLOW_SIDE_EFFECTING))(*srcs, *lands, *after)
    return res[:-1], res[-1]


def _comm_split_wait(comm, state, after, name):
    c_in, c_out, n_sem = len(comm.operands), len(comm.out_shape), sum(_size(d) for d in comm.sem_dims)
    sems, srcs, lands = state[:n_sem], state[n_sem:n_sem + c_in], state[n_sem + c_in:]

    def body(*refs):
        src_refs, land_refs = refs[:c_in], refs[c_in:c_in + c_out]
        _, finish = comm.build(src_refs, land_refs, _sem_grids(comm, refs[c_in + c_out:c_in + c_out + n_sem]))
        finish()

    sem_spec = pl.BlockSpec(memory_space=pltpu.SEMAPHORE)
    res = pl.pallas_call(
        body, name=name, in_specs=_hbm_specs(c_in + c_out) + [sem_spec] * n_sem + [pl.BlockSpec(memory_space=pl.ANY)],
        out_specs=_hbm_specs(c_in + c_out),
        out_shape=[pltpu.HBM(a.shape, a.dtype) for a in srcs] + [pltpu.HBM(o.shape, o.dtype) for o in lands],
        input_output_aliases={i: i for i in range(c_in + c_out)},
        compiler_params=_params(has_side_effects=pltpu.SideEffectType.DATAFLOW_SIDE_EFFECTING),
    )(*srcs, *lands, *sems, after)
    return res[:c_in], res[c_in:]


def _share_comm(fs, by_cols=()):
    nw = len(fs)

    def build(in_refs, out_refs, sems):
        del in_refs
        send_sems, recv_sems = sems
        x, y, c, _ = _place()

        def copy(w, half):
            part = _part(out_refs[w], w in by_cols, half)
            return pltpu.make_async_remote_copy(
                src_ref=part, dst_ref=part, send_sem=send_sems.at[w], recv_sem=recv_sems.at[w],
                device_id=(x, y, 1 - c), device_id_type=MESH)

        sends = [copy(w, c) for w in range(nw)]

        def start():
            for cp in sends:
                cp.start()

        def finish():
            for w in range(nw):
                copy(w, 1 - c).wait_recv()
            for cp in sends:
                cp.wait_send()

        return start, finish

    return _Comm(fs, [jax.ShapeDtypeStruct(f.shape, f.dtype) for f in fs],
                 [(nw,), (nw,)], build,
                 aliases={w: w for w in range(nw)})


def _add_sibling(g, r1, place, name, by_cols=False):
    nch, h, cols = r1.shape
    tr = _div_tile(h, 1024 if by_cols else 512, 2 * SUBLANES)
    nb = h // tr
    mine = (lambda k, i, p: (k, i, p[0])) if by_cols else (lambda k, i, p: (k, p[0] * nb + i, 0))

    def body(place_ref, g_ref, r_ref, o_ref):
        del place_ref
        o_ref[...] = (g_ref[...].astype(F32) + r_ref[...].astype(F32)).astype(BF16)

    spec = pltpu.PrefetchScalarGridSpec(
        num_scalar_prefetch=1, grid=(nch, nb),
        in_specs=[pl.BlockSpec((None, tr, cols), mine), pl.BlockSpec((None, tr, cols), lambda k, i, p: (k, i, 0))],
        out_specs=pl.BlockSpec((None, tr, cols), lambda k, i, p: (k, i, 0)))
    return pl.pallas_call(body, name=name, grid_spec=spec, out_shape=jax.ShapeDtypeStruct((nch, h, cols), BF16),
                          compiler_params=_params())(place, g, r1)


def _add_chips(s1, r2, place, name, by_cols=False):
    _, h, cols = s1.shape
    tr = _div_tile(h, 1024 if by_cols else 512, 2 * SUBLANES)
    nb = h // tr
    mine = (lambda i, p: (i, p[0])) if by_cols else (lambda i, p: (p[0] * nb + i, 0))
    whole = (h, 2 * cols) if by_cols else (2 * h, cols)

    def body(place_ref, s_ref, r_ref, o_ref):
        del place_ref
        acc = s_ref[...].astype(F32)
        for j in range(N_CHIPS - 1):
            acc = acc + r_ref[j].astype(F32)
        o_ref[...] = acc

    spec = pltpu.PrefetchScalarGridSpec(
        num_scalar_prefetch=1, grid=(nb,),
        in_specs=[pl.BlockSpec((None, tr, cols), lambda i, p: (p[1], i, 0)),
                  pl.BlockSpec((N_CHIPS - 1, tr, cols), lambda i, p: (0, i, 0))],
        out_specs=pl.BlockSpec((tr, cols), mine))
    return pl.pallas_call(body, name=name, grid_spec=spec, out_shape=jax.ShapeDtypeStruct(whole, F32),
                          compiler_params=_params())(place, s1, r2)


def _quarter_turn(m):
    h = m.shape[-1] // 2
    return jnp.concatenate([-m[..., h:], m[..., :h]], axis=-1)


def _quarter_turn_back(m):
    h = m.shape[-1] // 2
    return jnp.concatenate([m[..., h:], -m[..., :h]], axis=-1)


def _stack_rows(parts):
    out = lax.empty((sum(p.shape[0] for p in parts),) + parts[0].shape[1:], parts[0].dtype)
    row = 0
    for p in parts:
        out = lax.dynamic_update_slice(out, p, (row, 0))
        row += p.shape[0]
    return out


def _join_cols(sh):
    return jnp.concatenate([sh[k] for k in range(N_CHIPS)], axis=1)


def _split_cols(full):
    c = full.shape[1] // N_CHIPS
    return jnp.stack([full[:, k * c:(k + 1) * c] for k in range(N_CHIPS)])


def kernel(x, c, positions, w_ada, b_ada, pre_norm1_g, w_in, gm_ln_g, gm_ln_b, gm_w_s, gm_b_s, w_branch_a, q_norm_g, w_uq, kv_norm_g, w_ukv, w_branch_b, w_out, post_norm1_g, pre_norm2_g, w_up, conv_w, conv_b, w_down, post_norm2_g, loss_target, m_w_ada, m_b_ada, m_pre_norm1_g, m_w_in, m_gm_ln_g, m_gm_ln_b, m_gm_w_s, m_gm_b_s, m_w_branch_a, m_q_norm_g, m_w_uq, m_kv_norm_g, m_w_ukv, m_w_branch_b, m_w_out, m_post_norm1_g, m_pre_norm2_g, m_w_up, m_conv_w, m_conv_b, m_w_down, m_post_norm2_g, v_w_ada, v_b_ada, v_pre_norm1_g, v_w_in, v_gm_ln_g, v_gm_ln_b, v_gm_w_s, v_gm_b_s, v_w_branch_a, v_q_norm_g, v_w_uq, v_kv_norm_g, v_w_ukv, v_w_branch_b, v_w_out, v_post_norm1_g, v_pre_norm2_g, v_w_up, v_conv_w, v_conv_b, v_w_down, v_post_norm2_g):
    given = dict(locals())
    s, d = x.shape[1], x.shape[2]
    gw = gm_ln_g.shape[0]
    ql, kvl = q_norm_g.shape[0], kv_norm_g.shape[0]
    heads = N_CHIPS * w_uq.shape[1] // (NOPE + ROPE)
    ff = N_CHIPS * w_down.shape[0]
    assert gw == d and N_CHIPS * w_ukv.shape[1] == heads * (NOPE + VHEAD)
    ix, iy, ic = lax.axis_index("x"), lax.axis_index("y"), lax.axis_index("c")
    chip = 2 * ix + iy
    dev = 2 * chip + ic
    row = lambda v: v.reshape(1, -1)

    first = _all_gather(jnp.concatenate([jnp.pad(c, ((0, SUBLANES - 1), (0, 0))),
                                         jnp.pad(conv_w, ((0, SUBLANES - CONV_TAPS), (0, 0)))], axis=1), "gather_c")
    first = first.reshape(N_DEV, SUBLANES, d + conv_w.shape[1])
    c_all = first[:, 0, :d]
    conv_wf = first[::N_CORES, :CONV_TAPS, d:].transpose(1, 0, 2).reshape(CONV_TAPS, N_CHIPS * conv_w.shape[1])
    na = w_ada.shape[1]
    b_ada_mine = lax.dynamic_slice(b_ada, (chip * na,), (na,))
    mod_cols = _ada_fwd(c_all, w_ada, row(b_ada_mine), "ada_fwd")
    mod_all = _all_gather(mod_cols, "gather_mod").reshape(N_CHIPS, N_CORES, N_DEV, na)[:, 0]
    mod = lax.dynamic_index_in_dim(mod_all, dev, axis=1, keepdims=False).reshape(N_MOD, d)
    shift1, scale1, gate1, shift2, scale2, gate2 = (mod[i:i + 1] for i in range(N_MOD))

    mine = {n: (given[n].T if n == "w_in" else given[n]).astype(BF16) for n in BIG}
    gather = lambda names: _gather_comm([mine[n] for n in names], [i for i, n in enumerate(names) if n == "w_in"])
    whole = lambda n, g: lax.dynamic_update_slice(g, mine[n][None], (chip, 0, 0))
    rows4 = lambda sh4: sh4.reshape(-1, sh4.shape[2])
    wi_t = rows4(whole("w_in", _run_comm(gather(["w_in"]), "gather_w_in")[0]))
    o_q, o_kv, o_pe, o_ga = 2 * gw, 2 * gw + ql, 2 * gw + ql + kvl, 2 * gw + ql + kvl + ROPE
    w_in_big_t = _stack_rows([wi_t[:o_q], wi_t[o_ga:]])
    w_in_lat_t = _stack_rows([wi_t[o_q:o_ga], _quarter_turn(wi_t[o_pe:o_ga].T).T])

    inv = ROPE_THETA ** (-jnp.arange(0, ROPE, 2, dtype=F32) / ROPE)
    ang = positions[0].astype(F32)[:, None] * inv
    cos, sin = jnp.cos(ang), jnp.sin(ang)
    rope_k = jnp.concatenate([cos, cos, sin, sin], axis=1)
    softmax_scale = float(NOPE + ROPE) ** -0.5
    rope_q = jnp.concatenate([jnp.ones((s, NOPE), F32), rope_k], axis=1) * softmax_scale

    x2d, tgt = x[0], loss_target[0]
    g_pre1, g_post1, g_pre2, g_post2 = row(pre_norm1_g), row(post_norm1_g), row(pre_norm2_g), row(post_norm2_g)
    ln_g, ln_b, q_g, kv_g = row(gm_ln_g), row(gm_ln_b), row(q_norm_g), row(kv_norm_g)
    b_s_t = gm_b_s.T
    conv_bf = row(conv_b)

    h1 = _prenorm(x2d, g_pre1, scale1, shift1, "prenorm1")
    z_big, (g_uq, g_ukv, g_a) = _matmul(h1, w_in_big_t, mode="nt", out_dtype=BF16, name="mm_z_big", tm=s,
                                        comm=gather(["w_uq", "w_ukv", "w_branch_a"]))
    wq = _join_cols(whole("w_uq", g_uq)).reshape(ql, heads, NOPE + ROPE)
    w_q = jnp.concatenate([wq, _quarter_turn(wq[:, :, NOPE:])], axis=2).reshape(ql, heads * HEAD_W)
    w_kv = _join_cols(whole("w_ukv", g_ukv)).reshape(kvl, heads, 2, NOPE).transpose(0, 2, 1, 3)
    w_kv = w_kv.reshape(kvl, 2 * heads * NOPE)
    w_a = rows4(whole("w_branch_a", g_a))
    z_lat = _matmul(h1, w_in_lat_t, mode="nt", out_dtype=F32, name="mm_z_lat", tm=s, tn=1024)
    a_act = _gmlp_fwd(z_big, ln_g, ln_b, gm_w_s, b_s_t, "gmlp_fwd")
    qn, kvn, kr = _mla_prep(z_lat, q_g, kv_g, rope_k, "mla_prep")
    q_rot = _matmul(qn, w_q, mode="nn", out_dtype=BF16, name="mm_q", tm=s, tn=HEAD_W, mul=rope_q)
    kv_all = _matmul(kvn, w_kv, mode="nn", out_dtype=BF16, name="mm_kv", tm=s, tn=1024)
    (o_att, lse), (g_b, g_o, g_up) = _attn_fwd(q_rot, kv_all, kr, heads, "attn_fwd",
                                               comm=gather(["w_branch_b", "w_out", "w_up"]))
    w_b, w_o, w_upf = rows4(whole("w_branch_b", g_b)), rows4(whole("w_out", g_o)), whole("w_up", g_up)
    y_a = _matmul(a_act, w_a, mode="nn", out_dtype=BF16, name="mm_y_a", tm=s)
    y_b = _matmul(o_att, w_b, mode="nn", out_dtype=BF16, name="mm_y_b", tm=s)
    merged = _merge(z_big, y_a, y_b, "merge")
    y1 = _matmul(merged, w_o, mode="nn", out_dtype=F32, name="mm_y1", tm=s)
    x1, h2 = _post_pre(x2d, y1, gate1, g_post1, g_pre2, scale2, shift2, "post1_pre2")

    up_pre, (g_dn,) = _matmul(h2, w_upf, mode="nn", out_dtype=BF16, name="mm_up", tm=s, tn=1408,
                              comm=gather(["w_down"]))
    w_dn = rows4(whole("w_down", g_dn))
    act = _conv_fwd(up_pre, conv_wf, conv_bf, "conv_fwd")
    ffn = _matmul(act, w_dn, mode="nn", out_dtype=F32, name="mm_ffn", tm=s, tn=1024, tk=1408)

    dffn, dgate2, g_post2_grad, dx2, loss_part = _post_bwd(ffn, gate2, g_post2, "post2_bwd", xin=x1, target=tgt)
    loss = lax.psum(loss_part[0, 0], ("x", "y", "c"))
    place = jnp.stack([ic, chip]).astype(jnp.int32)
    rows_of = lambda g: g.reshape(N_CHIPS, g.shape[0] // N_CHIPS, g.shape[1])
    add_sibling = lambda names, gs, r1s: [_add_sibling(g, r1, place, "rs_add_sibling_" + n, by_cols=n == "w_in")
                                          for n, g, r1 in zip(names, gs, r1s)]
    add_chips = lambda names, s1s, r2s: [_add_chips(s1, r2, place, "rs_add_chips_" + n, by_cols=n == "w_in")
                                         for n, s1, r2 in zip(names, s1s, r2s)]
    gp_down = [rows_of(_matmul(act, dffn, mode="tn", out_dtype=BF16, name="mm_gw_down", tn=2048, tk=s))]
    dact, r1_down = _matmul(dffn, w_dn, mode="nt", out_dtype=BF16, name="mm_dact", tm=s, comm=_swap_comm(gp_down))
    s1_down = add_sibling(["w_down"], gp_down, r1_down)
    (dup, gcw_g, gcw_v, gcb_g, gcb_v), r2_down = _conv_bwd(up_pre, dact, conv_wf, conv_bf, "conv_bwd",
                                                            comm=_exchange_comm(s1_down))
    half_down = add_chips(["w_down"], s1_down, r2_down)
    dh2 = _matmul(dup, w_upf, mode="nt", out_dtype=F32, name="mm_dh2", tm=s, tn=1024, tk=1408)
    dx1, dshift2, dscale2, g_pre2_grad = _prenorm_bwd(x1, dh2, dx2, g_pre2, scale2, "prenorm2_bwd")

    dy1, dgate1, g_post1_grad = _post_bwd(y1, gate1, g_post1, "post1_bwd", dxo=dx1)
    dmerged = _matmul(dy1, w_o, mode="nt", out_dtype=BF16, name="mm_dmerged", tm=s)
    gw_out = _matmul(merged, dy1, mode="tn", out_dtype=BF16, name="mm_gw_out", tn=1024, tk=s)
    dy_a, dy_b, dz_big = _merge_bwd(dmerged, z_big, y_a, y_b, "merge_bwd")
    gw_a = _matmul(a_act, dy_a, mode="tn", out_dtype=BF16, name="mm_gw_a", tn=1024, tk=s)
    gw_b = _matmul(o_att, dy_b, mode="tn", out_dtype=BF16, name="mm_gw_b", tn=1024, tk=s)
    mid = ["w_up", "w_out", "w_branch_a", "w_branch_b"]
    gp_oab = [rows_of(gw_out), rows_of(gw_a), rows_of(gw_b)]
    da, r1_oab = _matmul(dy_a, w_a, mode="nt", out_dtype=BF16, name="mm_da", tm=s, comm=_swap_comm(gp_oab))
    s1_oab = add_sibling(mid[1:], gp_oab, r1_oab)
    gw_up, r2_oa = _matmul(h2, dup, mode="tn", out_dtype=BF16, name="mm_gw_up", tm=1024, tn=1408, tk=s,
                           out_groups=N_CHIPS, comm=_exchange_comm(s1_oab[:2]))
    do = _matmul(dy_b, w_b, mode="nt", out_dtype=BF16, name="mm_do", tm=s)
    (dz_big, g_ws, g_bs_t, g_ln_g, g_ln_b), r1_up = _gmlp_bwd(z_big, da, dz_big, ln_g, ln_b, gm_w_s, b_s_t,
                                                               "gmlp_bwd", comm=_swap_comm([gw_up]))
    s1_mid = add_sibling(mid[:1], [gw_up], r1_up) + s1_oab
    (dq, dk, dv), r2_up = _attn_bwd(q_rot, kv_all, kr, o_att, do, lse, heads, "attn_bwd",
                                    comm=_exchange_comm(s1_mid[:1]))
    dq_big, dkv, dkk = _mla_bwd_mid(dq, dk, dv, rope_q, rope_k, heads, "mla_bwd_mid")
    gw_q = _matmul(qn, dq_big, mode="tn", out_dtype=F32, name="mm_gw_q", tn=1024, tk=s)
    dqn = _matmul(dq_big, w_q, mode="nt", out_dtype=F32, name="mm_dqn", tm=s, tk=1024)
    gw_kv = _matmul(kvn, dkv, mode="tn", out_dtype=BF16, name="mm_gw_kv", tn=1024, tk=s)
    dkvn = _matmul(dkv, w_kv, mode="nt", out_dtype=F32, name="mm_dkvn", tm=s, tk=1024)
    dz_lat, g_q, g_kv = _mla_bwd_post(z_lat, dqn, dkvn, dkk, q_g, kv_g, "mla_bwd_post")

    partial = {
        "gm_ln_g": g_ln_g, "gm_ln_b": g_ln_b, "gm_w_s": g_ws, "gm_b_s": g_bs_t[:, :gm_b_s.shape[0]].T,
        "q_norm_g": g_q, "kv_norm_g": g_kv, "post_norm1_g": g_post1_grad, "pre_norm2_g": g_pre2_grad,
        "conv_w": jnp.concatenate([gcw_g, gcw_v], axis=1), "conv_b": jnp.concatenate([gcb_g, gcb_v], axis=1),
        "post_norm2_g": g_post2_grad,
    }
    flat = jnp.concatenate([partial[n].reshape(-1) for n in SMALL_PARTIAL])
    n_small = flat.shape[0]
    rows_small = -(-n_small // (LANES * SMALL_ROW_TILE)) * SMALL_ROW_TILE
    flat = jnp.pad(flat, (0, rows_small * LANES - n_small)).reshape(rows_small, LANES)

    def small_pack(prefix, source):
        v = jnp.concatenate([source[prefix + n].reshape(-1) for n in SMALL])
        rows = -(-v.shape[0] // (LANES * SUBLANES)) * SUBLANES
        return jnp.pad(v, (0, rows * LANES - v.shape[0])).reshape(rows, LANES)

    small_state = [small_pack(prefix, given) for prefix in ("", "m_", "v_")]

    dh1, r2_a_b = _matmul(dz_big, w_in_big_t, mode="nn", out_dtype=F32, name="mm_dh1_big", tm=s, tn=1024, tk=1024,
                          comm=_exchange_comm(s1_mid[3:]))
    half_mid = add_chips(mid, s1_mid, list(r2_up) + list(r2_oa) + list(r2_a_b))
    dh1 = _matmul(dz_lat, w_in_lat_t, mode="nn", out_dtype=F32, name="mm_dh1_lat", tm=s, tk=1024, add=dh1)
    gw_big_t, hosted = _matmul(dz_big, h1, mode="tn", out_dtype=BF16, name="mm_gw_in_big", tn=2048, tk=s,
                               comm=_join_comms([_share_comm(half_down + half_mid), _gather8_comm(flat)]))
    shared, small_all = hosted[:-1], lax.dynamic_update_slice(hosted[-1], flat[None], (dev, 0, 0))
    small_sum = _sum_leading(small_all, "sum_small", after=small_state + [loss.reshape(1, 1)]).reshape(-1)
    small_grads, off = {}, 0
    for n in SMALL_PARTIAL:
        shape = (CONV_TAPS, 2 * ff) if n == "conv_w" else given[n].shape
        small_grads[n] = small_sum[off:off + partial[n].size].reshape(shape)
        off += partial[n].size
    small_grads["conv_w"] = lax.dynamic_slice(small_grads["conv_w"], (0, chip * conv_w.shape[1]), conv_w.shape)
    grads = dict(zip(["w_down"] + mid, shared), **small_grads)
    gw_lat_t = _matmul(dz_lat, h1, mode="tn", out_dtype=F32, name="mm_gw_in_lat", tm=1024, tn=1024, tk=s)

    gq = gw_q.reshape(ql, heads, HEAD_W)
    gq_pe = gq[:, :, NOPE:NOPE + ROPE] + _quarter_turn_back(gq[:, :, NOPE + ROPE:])
    g_pe_t = gw_lat_t[ql + kvl:ql + kvl + ROPE] + _quarter_turn_back(gw_lat_t[ql + kvl + ROPE:].T).T
    last = ["w_in", "w_uq", "w_ukv"]
    gw_in_t = _stack_rows([gw_big_t[:o_q], gw_lat_t[:ql + kvl].astype(BF16), g_pe_t.astype(BF16), gw_big_t[o_q:]])
    gp_last = [
        gw_in_t.reshape(N_CHIPS, gw_in_t.shape[0] // N_CHIPS, d),
        _split_cols(jnp.concatenate([gq[:, :, :NOPE], gq_pe], axis=2).reshape(ql, heads * (NOPE + ROPE)).astype(BF16)),
        _split_cols(gw_kv.reshape(kvl, 2, heads, NOPE).transpose(0, 2, 1, 3).reshape(kvl, heads * 2 * NOPE)),
    ]
    (grad_x, dshift1, dscale1, g_pre1_grad), r1_last = _prenorm_bwd(x2d, dh1, dx1, g_pre1, scale1, "prenorm1_bwd",
                                                                    comm=_swap_comm(gp_last, by_cols=[0]))
    s1_last = add_sibling(last, gp_last, r1_last)

    dmod = jnp.concatenate([dshift1, dscale1, dgate1, dshift2, dscale2, dgate2, g_pre1_grad], axis=1)
    dmod_all = _all_gather(jnp.pad(dmod, ((0, SUBLANES - 1), (0, 0))), "gather_dmod")
    dmod_all = dmod_all.reshape(N_DEV, SUBLANES, (N_MOD + 1) * d)[:, 0]
    dmod_sum = _sum_leading(dmod_all.reshape(N_DEV, 1, (N_MOD + 1) * d), "sum_dmod")[0]
    grads["b_ada"], grads["pre_norm1_g"] = dmod_sum[:N_MOD * d], dmod_sum[N_MOD * d:]
    dmod_mine = lax.dynamic_slice(dmod_all, (0, chip * na), (N_DEV, na))
    grads["w_ada"] = _ada_bwd(c_all.T, dmod_mine, "ada_bwd")

    delta, new_m, new_v = {}, {}, {}

    def adamw(n, after=None):
        turn = (lambda a: a.T) if n == "w_in" else (lambda a: a)
        outs = _adamw(turn(given[n]), grads[n], turn(given["m_" + n]), turn(given["v_" + n]), "adamw_" + n,
                      after=after)
        grads[n] = turn(grads[n])
        delta[n], new_m[n], new_v[n] = (turn(o) for o in outs)

    exchange_last = _exchange_comm(s1_last)
    in_flight, token = _comm_split_start(exchange_last, "rs_exchange_last_start", after=[dmod_sum, small_sum])
    for n in ["w_ada", "w_down"] + mid:
        adamw(n, after=token)
    s1_last, r2_last = _comm_split_wait(exchange_last, in_flight, delta[mid[-1]], "rs_exchange_last_wait")
    half_last = add_chips(last, s1_last, r2_last)
    grads.update(zip(last, _run_comm(_share_comm(half_last, by_cols=[0]), "rs_share_last")))
    for n in last:
        adamw(n)

    outs = _adamw(small_state[0], small_pack("", grads), small_state[1], small_state[2], "adamw_small")
    off = 0
    for n in SMALL:
        size = given[n].size
        for store, packed_out in zip((delta, new_m, new_v), outs):
            store[n] = packed_out.reshape(-1)[off:off + size].reshape(given[n].shape)
        off += size

    return (loss, grad_x[None], *[grads[n] for n in WEIGHTS], *[delta[n] for n in WEIGHTS],
            *[new_m[n] for n in WEIGHTS], *[new_v[n] for n in WEIGHTS])
```

```python
import functools

import jax
import jax.numpy as jnp
from jax import lax
from jax.experimental import pallas as pl
from jax.experimental.pallas import tpu as pltpu

F32 = jnp.float32
BF16 = jnp.bfloat16
MESH = pl.DeviceIdType.MESH
HBM = pltpu.HBM

EPS = 1e-6
NOPE, ROPE, VHEAD = 128, 64, 128
HEAD_W = NOPE + 2 * ROPE
ROPE_THETA = 10000.0
CONV_TAPS = 3
N_MOD = 6
N_CHIPS, N_CORES, N_DEV = 4, 2, 8
ADAM_LR, ADAM_B1, ADAM_B2, ADAM_EPS, ADAM_WD, ADAM_STEP = 0.001, 0.9, 0.999, 1e-08, 0.01, 10

LANES = 128
SUBLANES = 8
VMEM_LIMIT = 56 * 2**20
MIDDLE_STAGE_AT = 70
SMALL_ROW_TILE = 256

BIG = ("w_in", "w_branch_a", "w_uq", "w_ukv", "w_branch_b", "w_out", "w_up", "w_down")
WEIGHTS = ("w_ada", "b_ada", "pre_norm1_g", "w_in", "gm_ln_g", "gm_ln_b", "gm_w_s", "gm_b_s", "w_branch_a",
           "q_norm_g", "w_uq", "kv_norm_g", "w_ukv", "w_branch_b", "w_out", "post_norm1_g", "pre_norm2_g",
           "w_up", "conv_w", "conv_b", "w_down", "post_norm2_g")
SMALL_PARTIAL = ("gm_ln_g", "gm_ln_b", "gm_w_s", "gm_b_s", "q_norm_g", "kv_norm_g", "post_norm1_g",
                 "pre_norm2_g", "conv_w", "conv_b", "post_norm2_g")
SMALL = ("b_ada", "pre_norm1_g") + SMALL_PARTIAL


def _div_tile(n, cap, mult=LANES):
    t = (min(cap, n) // mult) * mult
    while t >= mult:
        if n % t == 0:
            return t
        t -= mult
    return n


def _params(**kw):
    return pltpu.CompilerParams(vmem_limit_bytes=VMEM_LIMIT, **kw)


def _row_spec(width):
    return pl.BlockSpec((1, width), lambda *_: (0, 0))


def _gelu(x):
    k = 0.7978845608028654
    return 0.5 * x * (1.0 + jnp.tanh(k * (x + 0.044715 * x * x * x)))


def _gelu_grad(x):
    k = 0.7978845608028654
    t = jnp.tanh(k * (x + 0.044715 * x * x * x))
    return 0.5 * (1.0 + t) + 0.5 * x * (1.0 - t * t) * k * (1.0 + 3.0 * 0.044715 * x * x)


def _sigmoid(x):
    return 0.5 * jnp.tanh(0.5 * x) + 0.5


def _dot(a, b, dims):
    return lax.dot_general(a, b, (dims, ((), ())), preferred_element_type=F32)


NN = ((1,), (0,))
NT = ((1,), (1,))
TN = ((0,), (0,))


def _logical(arr):
    if arr.ndim == 2:
        return arr.shape[0], arr.shape[1], arr.shape[1]
    return arr.shape[1], arr.shape[0] * arr.shape[2], arr.shape[2]


def _tile_spec(ndim, group_w, blk_rows, blk_cols, row_of, col_of):
    if ndim == 2:
        return pl.BlockSpec((blk_rows, blk_cols), lambda i, j, k: (row_of(i, j, k), col_of(i, j, k)))
    per = group_w // blk_cols
    return pl.BlockSpec((None, blk_rows, blk_cols),
                        lambda i, j, k: (col_of(i, j, k) // per, row_of(i, j, k), col_of(i, j, k) % per))


def _matmul(a, b, *, mode, out_dtype, name, tm=512, tn=512, tk=2048, mul=None, add=None, out_groups=None, comm=None):
    ar, ac, agw = _logical(a)
    br, bc, bgw = _logical(b)
    if mode == "nn":
        m, kd, n = ar, ac, bc
        m_w, k_w, n_w = (), (agw,), (bgw,)
    elif mode == "nt":
        m, kd, n = ar, ac, br
        m_w, k_w, n_w = (), (agw, bgw), ()
    else:
        m, kd, n = ac, ar, bc
        m_w, k_w, n_w = (agw,), (), (bgw,)
    if out_groups is not None:
        n_w = n_w + (n // out_groups,)
    tm = _div_tile(min((m,) + m_w), tm, LANES if mode == "tn" else SUBLANES)
    tn = _div_tile(min((n,) + n_w), tn)
    tk = _div_tile(min((kd,) + k_w), tk)
    assert all(w % tn == 0 for w in n_w) and all(w % tk == 0 for w in k_w) and all(w % tm == 0 for w in m_w)
    nk = kd // tk
    dims = {"nn": NN, "nt": NT, "tn": TN}[mode]
    gi, gj, gk = (lambda i, j, k: i), (lambda i, j, k: j), (lambda i, j, k: k)
    if mode == "nn":
        a_spec = _tile_spec(a.ndim, agw, tm, tk, gi, gk)
        b_spec = _tile_spec(b.ndim, bgw, tk, tn, gk, gj)
    elif mode == "nt":
        a_spec = _tile_spec(a.ndim, agw, tm, tk, gi, gk)
        b_spec = _tile_spec(b.ndim, bgw, tn, tk, gj, gk)
    else:
        a_spec = _tile_spec(a.ndim, agw, tk, tm, gk, gi)
        b_spec = _tile_spec(b.ndim, bgw, tk, tn, gk, gj)
    in_specs, operands = [a_spec, b_spec], [a, b]
    if mul is not None:
        assert mul.shape == (m, tn)
        in_specs.append(pl.BlockSpec((tm, tn), lambda i, j, k: (i, 0)))
        operands.append(mul)
    if add is not None:
        in_specs.append(pl.BlockSpec((tm, tn), lambda i, j, k: (i, j)))
        operands.append(add)

    def body(*refs):
        a_ref, b_ref = refs[0], refs[1]
        pos = 2
        mul_ref = add_ref = None
        if mul is not None:
            mul_ref, pos = refs[pos], pos + 1
        if add is not None:
            add_ref, pos = refs[pos], pos + 1
        o_ref = refs[pos]

        def finish(r):
            if mul_ref is not None:
                r = r * mul_ref[...]
            if add_ref is not None:
                r = r + add_ref[...]
            o_ref[...] = r.astype(out_dtype)

        part = _dot(a_ref[...], b_ref[...], dims)
        if nk == 1:
            finish(part)
        else:
            acc_ref = refs[pos + 1]
            k = pl.program_id(2)

            @pl.when(k == 0)
            def _():
                acc_ref[...] = part

            @pl.when(k > 0)
            def _():
                acc_ref[...] += part

            @pl.when(k == nk - 1)
            def _():
                finish(acc_ref[...])

    if out_groups is None:
        out_spec, out_dims = _tile_spec(2, n, tm, tn, gi, gj), (m, n)
    else:
        out_spec, out_dims = _tile_spec(3, n // out_groups, tm, tn, gi, gj), (out_groups, m, n // out_groups)
    return _call(body, operands, comm, name=name, grid=(m // tm, n // tn, nk), in_specs=in_specs, out_specs=out_spec,
                 out_shape=jax.ShapeDtypeStruct(out_dims, out_dtype),
                 scratch_shapes=[] if nk == 1 else [pltpu.VMEM((tm, tn), F32)])


def _accumulate(ref, value):
    @pl.when(pl.program_id(0) == 0)
    def _():
        ref[...] = value

    @pl.when(pl.program_id(0) > 0)
    def _():
        ref[...] += value


def _colsum(v):
    return jnp.sum(v, axis=0, keepdims=True)


def _rowmean(v):
    return jnp.mean(v, axis=-1, keepdims=True)


def _prenorm(x, g, scale, shift, name):
    s, d = x.shape
    tb = _div_tile(s, 256, SUBLANES)

    def body(x_ref, g_ref, sc_ref, sh_ref, h_ref):
        xv = x_ref[...]
        r = lax.rsqrt(_rowmean(xv * xv) + EPS)
        h_ref[...] = ((xv * r) * g_ref[...] * (1.0 + sc_ref[...]) + sh_ref[...]).astype(BF16)

    blk = pl.BlockSpec((tb, d), lambda i: (i, 0))
    return pl.pallas_call(
        body, name=name, grid=(s // tb,), in_specs=[blk, _row_spec(d), _row_spec(d), _row_spec(d)],
        out_specs=blk, out_shape=jax.ShapeDtypeStruct((s, d), BF16), compiler_params=_params(),
    )(x, g, scale, shift)


def _post_pre(x, y, gate, pg, g2, scale2, shift2, name):
    s, d = x.shape
    tb = _div_tile(s, 256, SUBLANES)

    def body(x_ref, y_ref, gate_ref, pg_ref, g2_ref, sc_ref, sh_ref, x1_ref, h2_ref):
        yv = y_ref[...]
        rp = lax.rsqrt(_rowmean(yv * yv) + EPS)
        x1 = x_ref[...] + gate_ref[...] * ((yv * rp) * pg_ref[...])
        x1_ref[...] = x1
        r2 = lax.rsqrt(_rowmean(x1 * x1) + EPS)
        h2_ref[...] = ((x1 * r2) * g2_ref[...] * (1.0 + sc_ref[...]) + sh_ref[...]).astype(BF16)

    blk = pl.BlockSpec((tb, d), lambda i: (i, 0))
    return pl.pallas_call(
        body, name=name, grid=(s // tb,), in_specs=[blk, blk] + [_row_spec(d)] * 5,
        out_specs=[blk, blk],
        out_shape=[jax.ShapeDtypeStruct((s, d), F32), jax.ShapeDtypeStruct((s, d), BF16)],
        compiler_params=_params(),
    )(x, y, gate, pg, g2, scale2, shift2)


def _post_bwd(y, gate, pg, name, *, dxo=None, xin=None, target=None):
    s, d = y.shape
    tb = _div_tile(s, 256, SUBLANES)
    from_loss = target is not None

    def body(*refs):
        if from_loss:
            y_ref, gate_ref, pg_ref, xin_ref, t_ref, dy_ref, dgate_ref, dpg_ref, dxo_ref, loss_ref = refs
        else:
            y_ref, gate_ref, pg_ref, dxo_in_ref, dy_ref, dgate_ref, dpg_ref = refs
        yv = y_ref[...]
        rp = lax.rsqrt(_rowmean(yv * yv) + EPS)
        yh = yv * rp
        fn = yh * pg_ref[...]
        gate = gate_ref[...]
        if from_loss:
            err = xin_ref[...] + gate * fn - t_ref[...]
            dxo = err * (1.0 / d)
            dxo_ref[...] = dxo
            part = 0.5 * jnp.sum(_rowmean(err * err), axis=0, keepdims=True)
            _accumulate(loss_ref, jnp.broadcast_to(part, loss_ref.shape))
        else:
            dxo = dxo_in_ref[...]
        _accumulate(dgate_ref, _colsum(dxo * fn))
        dfn = dxo * gate
        _accumulate(dpg_ref, _colsum(dfn * yh))
        dyh = dfn * pg_ref[...]
        dy_ref[...] = (rp * (dyh - yh * _rowmean(dyh * yh))).astype(BF16)

    blk = pl.BlockSpec((tb, d), lambda i: (i, 0))
    in_specs = [blk, _row_spec(d), _row_spec(d)]
    out_specs = [blk, _row_spec(d), _row_spec(d)]
    out_shape = [jax.ShapeDtypeStruct((s, d), BF16), jax.ShapeDtypeStruct((1, d), F32),
                 jax.ShapeDtypeStruct((1, d), F32)]
    if from_loss:
        operands = (y, gate, pg, xin, target)
        in_specs += [blk, blk]
        out_specs += [blk, _row_spec(LANES)]
        out_shape += [jax.ShapeDtypeStruct((s, d), F32), jax.ShapeDtypeStruct((1, LANES), F32)]
    else:
        operands = (y, gate, pg, dxo)
        in_specs += [blk]
    return pl.pallas_call(
        body, name=name, grid=(s // tb,), in_specs=in_specs, out_specs=out_specs, out_shape=out_shape,
        compiler_params=_params(),
    )(*operands)


def _prenorm_bwd(xin, dh, dres, g, scale, name, comm=None):
    s, d = xin.shape
    tb = _div_tile(s, 256, SUBLANES)

    def body(x_ref, dh_ref, dres_ref, g_ref, sc_ref, dx_ref, dshift_ref, dscale_ref, dg_ref):
        xv = x_ref[...]
        r = lax.rsqrt(_rowmean(xv * xv) + EPS)
        xn = xv * r
        dh = dh_ref[...]
        g1 = g_ref[...]
        s1 = 1.0 + sc_ref[...]
        _accumulate(dshift_ref, _colsum(dh))
        _accumulate(dscale_ref, _colsum(dh * xn * g1))
        _accumulate(dg_ref, _colsum(dh * xn * s1))
        dxn = dh * g1 * s1
        dx_ref[...] = dres_ref[...] + r * (dxn - xn * _rowmean(dxn * xn))

    blk = pl.BlockSpec((tb, d), lambda i: (i, 0))
    return _call(
        body, (xin, dh, dres, g, scale), comm, name=name, grid=(s // tb,),
        in_specs=[blk, blk, blk, _row_spec(d), _row_spec(d)],
        out_specs=[blk, _row_spec(d), _row_spec(d), _row_spec(d)],
        out_shape=[jax.ShapeDtypeStruct((s, d), F32)] + [jax.ShapeDtypeStruct((1, d), F32)] * 3)


def _merge(z_big, y_a, y_b, name):
    s, d = y_a.shape
    tb = _div_tile(s, 256, SUBLANES)

    def body(zg_ref, ya_ref, yb_ref, o_ref):
        ga, gb = zg_ref[:, :d].astype(F32), zg_ref[:, d:].astype(F32)
        o_ref[...] = (_sigmoid(ga) * ya_ref[...].astype(F32) + _sigmoid(gb) * yb_ref[...].astype(F32)).astype(BF16)

    blk = pl.BlockSpec((tb, d), lambda i: (i, 0))
    return pl.pallas_call(
        body, name=name, grid=(s // tb,), in_specs=[pl.BlockSpec((tb, 2 * d), lambda i: (i, 1)), blk, blk],
        out_specs=blk, out_shape=jax.ShapeDtypeStruct((s, d), BF16), compiler_params=_params(),
    )(z_big, y_a, y_b)


def _merge_bwd(dmerged, z_big, y_a, y_b, name):
    s, d = y_a.shape
    tb = _div_tile(s, 256, SUBLANES)

    def body(dm_ref, zg_ref, ya_ref, yb_ref, dya_ref, dyb_ref, dz_ref):
        dm = dm_ref[...].astype(F32)
        sa, sb = _sigmoid(zg_ref[:, :d].astype(F32)), _sigmoid(zg_ref[:, d:].astype(F32))
        dya_ref[...] = (dm * sa).astype(BF16)
        dyb_ref[...] = (dm * sb).astype(BF16)
        dz_ref[:, :d] = (dm * ya_ref[...].astype(F32) * sa * (1.0 - sa)).astype(BF16)
        dz_ref[:, d:] = (dm * yb_ref[...].astype(F32) * sb * (1.0 - sb)).astype(BF16)

    blk = pl.BlockSpec((tb, d), lambda i: (i, 0))
    wide = pl.BlockSpec((tb, 2 * d), lambda i: (i, 1))
    return pl.pallas_call(
        body, name=name, grid=(s // tb,), in_specs=[blk, wide, blk, blk], out_specs=[blk, blk, wide],
        out_shape=[jax.ShapeDtypeStruct((s, d), BF16), jax.ShapeDtypeStruct((s, d), BF16),
                   jax.ShapeDtypeStruct((s, 4 * d), BF16)],
        compiler_params=_params(),
    )(dmerged, z_big, y_a, y_b)


def _causal_mask(ch):
    q = lax.broadcasted_iota(jnp.int32, (ch, ch), 0)
    p = lax.broadcasted_iota(jnp.int32, (ch, ch), 1)
    return (p <= q).astype(F32)


def _gmlp_norm(zc, lng, lnb, gw):
    u_pre, v_pre = zc[:, :gw], zc[:, gw:]
    vg = _gelu(v_pre)
    mu = _rowmean(vg)
    cen = vg - mu
    rstd = lax.rsqrt(_rowmean(cen * cen) + EPS)
    vhat = cen * rstd
    return u_pre, v_pre, _gelu(u_pre), vhat, rstd, vhat * lng + lnb


def _gmlp_fwd(z_big, ln_g, ln_b, w_s, b_s_t, name):
    s = z_big.shape[0]
    groups, ch, _ = w_s.shape
    gw = ln_g.shape[1]
    gd = gw // groups

    def body(z_ref, lng_ref, lnb_ref, ws_ref, bt_ref, a_ref):
        _, _, u, _, _, vn = _gmlp_norm(z_ref[...].astype(F32), lng_ref[...], lnb_ref[...], gw)
        mask = _causal_mask(ch)
        for g in range(groups):
            cols = slice(g * gd, (g + 1) * gd)
            wm = (ws_ref[g] * mask).astype(BF16)
            mixed = _dot(wm, vn[:, cols].astype(BF16), NN) + bt_ref[:, g:g + 1]
            a_ref[:, cols] = (u[:, cols] * mixed).astype(BF16)

    return pl.pallas_call(
        body, name=name, grid=(s // ch,),
        in_specs=[pl.BlockSpec((ch, 2 * gw), lambda n: (n, 0)), _row_spec(gw), _row_spec(gw),
                  pl.BlockSpec((groups, ch, ch), lambda n: (0, 0, 0)), pl.BlockSpec((ch, groups), lambda n: (0, 0))],
        out_specs=pl.BlockSpec((ch, gw), lambda n: (n, 0)),
        out_shape=jax.ShapeDtypeStruct((s, gw), BF16), compiler_params=_params(),
    )(z_big, ln_g, ln_b, w_s, b_s_t)


def _gmlp_bwd(z_big, da, dz_big, ln_g, ln_b, w_s, b_s_t, name, comm=None):
    s = z_big.shape[0]
    groups, ch, _ = w_s.shape
    gw = ln_g.shape[1]
    gd = gw // groups

    def body(z_ref, da_ref, dzin_ref, lng_ref, lnb_ref, ws_ref, bt_ref, dz_ref, gws_ref, gbt_ref, glng_ref, glnb_ref,
             vg_ref, dvh_ref):
        del dzin_ref
        mask = _causal_mask(ch)
        lane = lax.broadcasted_iota(jnp.int32, (ch, LANES), 1)
        group_cols = [slice(g * gd, (g + 1) * gd) for g in range(groups)]
        rowsum = lambda v: jnp.sum(v, axis=1, keepdims=True)

        @pl.when(pl.program_id(0) == 0)
        def _():
            for ref in (gws_ref, gbt_ref, glng_ref, glnb_ref):
                ref[...] = jnp.zeros(ref.shape, F32)

        total = jnp.zeros((ch, 1), F32)
        for cols in group_cols:
            vg = _gelu(z_ref[:, gw + cols.start:gw + cols.stop].astype(F32))
            vg_ref[:, cols] = vg
            total = total + rowsum(vg)
        mu = total * (1.0 / gw)
        total = jnp.zeros((ch, 1), F32)
        for cols in group_cols:
            cen = vg_ref[:, cols] - mu
            total = total + rowsum(cen * cen)
        rstd = lax.rsqrt(total * (1.0 / gw) + EPS)
        m1, m2, gb = jnp.zeros((ch, 1), F32), jnp.zeros((ch, 1), F32), jnp.zeros((ch, LANES), F32)
        for g, cols in enumerate(group_cols):
            vhat = (vg_ref[:, cols] - mu) * rstd
            vn_g = (vhat * lng_ref[:, cols] + lnb_ref[:, cols]).astype(BF16)
            wm = (ws_ref[g] * mask).astype(BF16)
            mixed = _dot(wm, vn_g, NN) + bt_ref[:, g:g + 1]
            u_pre, da_g = z_ref[:, cols].astype(F32), da_ref[:, cols].astype(F32)
            dz_ref[:, cols] = (da_g * mixed * _gelu_grad(u_pre)).astype(BF16)
            dmixed = da_g * _gelu(u_pre)
            dm16 = dmixed.astype(BF16)
            dvn = _dot(wm, dm16, TN)
            gws_ref[g] += _dot(dm16, vn_g, NT) * mask
            gb = gb + jnp.where(lane == g, rowsum(dmixed), 0.0)
            glnb_ref[:, cols] += _colsum(dvn)
            glng_ref[:, cols] += _colsum(dvn * vhat)
            dvh = dvn * lng_ref[:, cols]
            dvh_ref[:, cols] = dvh
            m1, m2 = m1 + rowsum(dvh), m2 + rowsum(dvh * vhat)
        gbt_ref[...] += gb
        m1, m2 = m1 * (1.0 / gw), m2 * (1.0 / gw)
        for cols in group_cols:
            vhat = (vg_ref[:, cols] - mu) * rstd
            dvg = rstd * (dvh_ref[:, cols] - m1 - vhat * m2)
            v_pre = z_ref[:, gw + cols.start:gw + cols.stop].astype(F32)
            dz_ref[:, gw + cols.start:gw + cols.stop] = (dvg * _gelu_grad(v_pre)).astype(BF16)

    zspec = pl.BlockSpec((ch, 2 * gw), lambda n: (n, 0))
    return _call(
        body, (z_big, da, dz_big, ln_g, ln_b, w_s, b_s_t), comm, name=name, grid=(s // ch,),
        in_specs=[zspec, pl.BlockSpec((ch, gw), lambda n: (n, 0)), pl.BlockSpec(memory_space=HBM),
                  _row_spec(gw), _row_spec(gw), pl.BlockSpec((groups, ch, ch), lambda n: (0, 0, 0)),
                  pl.BlockSpec((ch, groups), lambda n: (0, 0))],
        out_specs=[zspec, pl.BlockSpec((groups, ch, ch), lambda n: (0, 0, 0)),
                   pl.BlockSpec((ch, LANES), lambda n: (0, 0)), _row_spec(gw), _row_spec(gw)],
        out_shape=[jax.ShapeDtypeStruct(dz_big.shape, BF16), jax.ShapeDtypeStruct((groups, ch, ch), F32),
                   jax.ShapeDtypeStruct((ch, LANES), F32), jax.ShapeDtypeStruct((1, gw), F32),
                   jax.ShapeDtypeStruct((1, gw), F32)],
        scratch_shapes=[pltpu.VMEM((ch, gw), F32)] * 2, input_output_aliases={2: 0})


def _mla_prep(z_lat, q_g, kv_g, rope_k, name):
    s, latw = z_lat.shape
    ql, kvl = q_g.shape[1], kv_g.shape[1]
    tb = _div_tile(s, 256, SUBLANES)

    def body(z_ref, qg_ref, kvg_ref, t_ref, qn_ref, kvn_ref, kr_ref):
        q = z_ref[:, :ql]
        qn_ref[...] = ((q * lax.rsqrt(_rowmean(q * q) + EPS)) * qg_ref[...]).astype(BF16)
        kv = z_ref[:, ql:ql + kvl]
        kvn_ref[...] = ((kv * lax.rsqrt(_rowmean(kv * kv) + EPS)) * kvg_ref[...]).astype(BF16)
        kk = z_ref[:, ql + kvl:] * t_ref[...]
        kr_ref[...] = (kk + pltpu.roll(kk, ROPE, axis=1)).astype(BF16)

    return pl.pallas_call(
        body, name=name, grid=(s // tb,),
        in_specs=[pl.BlockSpec((tb, latw), lambda i: (i, 0)), _row_spec(ql), _row_spec(kvl),
                  pl.BlockSpec((tb, 2 * ROPE), lambda i: (i, 0))],
        out_specs=[pl.BlockSpec((tb, ql), lambda i: (i, 0)), pl.BlockSpec((tb, kvl), lambda i: (i, 0)),
                   pl.BlockSpec((tb, 2 * ROPE), lambda i: (i, 0))],
        out_shape=[jax.ShapeDtypeStruct((s, ql), BF16), jax.ShapeDtypeStruct((s, kvl), BF16),
                   jax.ShapeDtypeStruct((s, 2 * ROPE), BF16)],
        compiler_params=_params(),
    )(z_lat, q_g, kv_g, rope_k)


def _attn_fwd(q, kv, kr, heads, name, comm=None):
    s = q.shape[0]
    t = _div_tile(s, 512)
    nb = s // t
    hp = 2 if heads % 2 == 0 else 1

    def body(q_ref, k_ref, kr_ref, v_ref, o_ref, lse_ref, m_ref, l_ref, acc_ref):
        i, j = pl.program_id(1), pl.program_id(2)

        @pl.when(j == 0)
        def _():
            m_ref[...] = jnp.full(m_ref.shape, -1e30, F32)
            l_ref[...] = jnp.zeros(l_ref.shape, F32)
            acc_ref[...] = jnp.zeros(acc_ref.shape, F32)

        def update(h, rows, n_keys, on_diagonal):
            vc = slice(h * VHEAD, (h + 1) * VHEAD)
            k_full = jnp.concatenate([k_ref[:n_keys, h * NOPE:(h + 1) * NOPE], kr_ref[:n_keys, :]], axis=1)
            sc = _dot(q_ref[rows, h * HEAD_W:(h + 1) * HEAD_W], k_full, NT)
            if on_diagonal:
                row_pos = rows.start + lax.broadcasted_iota(jnp.int32, sc.shape, 0)
                sc = jnp.where(lax.broadcasted_iota(jnp.int32, sc.shape, 1) <= row_pos, sc, -1e30)
            m_old = m_ref[h, rows, :]
            m_new = jnp.maximum(m_old, jnp.max(sc, axis=-1, keepdims=True))
            p = jnp.exp(sc - m_new)
            alpha = jnp.exp(m_old - m_new)
            l_new = alpha * l_ref[h, rows, :] + jnp.sum(p, axis=-1, keepdims=True)
            acc = alpha * acc_ref[rows, vc] + _dot(p.astype(BF16), v_ref[:n_keys, vc], NN)
            if on_diagonal:
                o_ref[rows, vc] = (acc / l_new).astype(BF16)
                lse_ref[h, rows, :] = jnp.broadcast_to(m_new + jnp.log(l_new), (rows.stop - rows.start, LANES))
            else:
                m_ref[h, rows, :], l_ref[h, rows, :], acc_ref[rows, vc] = m_new, l_new, acc

        def below_diagonal():
            for h in range(hp):
                update(h, slice(0, t), t, False)

        def on_diagonal():
            for h in range(hp):
                update(h, slice(0, t // 2), t // 2, True)
                update(h, slice(t // 2, t), t, True)

        pl.when(j < i)(below_diagonal)
        pl.when(j == i)(on_diagonal)

    kidx = lambda off: (lambda h, i, j: (jnp.minimum(i, j), off(h)))
    return _call(
        body, (q, kv, kr, kv), comm, name=name, grid=(heads // hp, nb, nb),
        in_specs=[pl.BlockSpec((t, hp * HEAD_W), lambda h, i, j: (i, h)),
                  pl.BlockSpec((t, hp * NOPE), kidx(lambda h: h)),
                  pl.BlockSpec((t, 2 * ROPE), kidx(lambda h: 0)),
                  pl.BlockSpec((t, hp * VHEAD), kidx(lambda h: heads // hp + h))],
        out_specs=[pl.BlockSpec((t, hp * VHEAD), lambda h, i, j: (i, h)),
                   pl.BlockSpec((hp, t, LANES), lambda h, i, j: (h, i, 0))],
        out_shape=[jax.ShapeDtypeStruct((s, heads * VHEAD), BF16), jax.ShapeDtypeStruct((heads, s, LANES), F32)],
        scratch_shapes=[pltpu.VMEM((hp, t, 1), F32), pltpu.VMEM((hp, t, 1), F32), pltpu.VMEM((t, hp * VHEAD), F32)])


def _attn_bwd(q, kv, kr, o, do, lse, heads, name, comm=None):
    s = q.shape[0]
    t = _div_tile(s, 512)
    nb = s // t
    hp = 2 if heads % 2 == 0 else 1

    def body(q_ref, k_ref, kr_ref, v_ref, o_ref, do_ref, lse_ref, dq_ref, dk_ref, dv_ref, dk_acc, dv_acc):
        j, i = pl.program_id(1), pl.program_id(2)

        @pl.when(jnp.logical_and(j == 0, i == 0))
        def _():
            dq_ref[...] = jnp.zeros(dq_ref.shape, F32)

        def update(h, rows, n_keys, on_diagonal, assign):
            qc, kc, vc = (slice(h * w, (h + 1) * w) for w in (HEAD_W, NOPE, VHEAD))
            n_rows = rows.stop - rows.start
            qv, do_v = q_ref[rows, qc], do_ref[rows, vc]
            k_full = jnp.concatenate([k_ref[:n_keys, kc], kr_ref[:n_keys, :]], axis=1)
            sc = _dot(qv, k_full, NT)
            if on_diagonal:
                row_pos = rows.start + lax.broadcasted_iota(jnp.int32, sc.shape, 0)
                sc = jnp.where(lax.broadcasted_iota(jnp.int32, sc.shape, 1) <= row_pos, sc, -1e30)
            p = jnp.exp(sc - lse_ref[h, rows, :1])
            dp = _dot(do_v, v_ref[:n_keys, vc], NT)
            delta = jnp.sum(do_v.astype(F32) * o_ref[rows, vc].astype(F32), axis=-1, keepdims=True)
            ds = (p * (dp - delta)).astype(BF16)
            dq_ref[pl.ds(pl.multiple_of(i * t + rows.start, n_rows), n_rows), qc] += _dot(ds, k_full, NN)
            dv_part, dk_part = _dot(p.astype(BF16), do_v, TN), _dot(ds, qv, TN)
            if assign:
                dv_acc[:n_keys, vc], dk_acc[:n_keys, qc] = dv_part, dk_part
            else:
                dv_acc[:n_keys, vc] += dv_part
                dk_acc[:n_keys, qc] += dk_part

        def on_diagonal():
            for h in range(hp):
                update(h, slice(t // 2, t), t, True, True)
                update(h, slice(0, t // 2), t // 2, True, False)

        def below_diagonal():
            for h in range(hp):
                update(h, slice(0, t), t, False, False)

        pl.when(i == j)(on_diagonal)
        pl.when(i > j)(below_diagonal)

        @pl.when(i == nb - 1)
        def _():
            dk_ref[...] = dk_acc[...].astype(BF16)
            dv_ref[...] = dv_acc[...].astype(BF16)

    qidx = lambda h, j, i: (jnp.maximum(i, j), h)
    return _call(
        body, (q, kv, kr, kv, o, do, lse), comm, name=name, grid=(heads // hp, nb, nb),
        in_specs=[pl.BlockSpec((t, hp * HEAD_W), qidx),
                  pl.BlockSpec((t, hp * NOPE), lambda h, j, i: (j, h)),
                  pl.BlockSpec((t, 2 * ROPE), lambda h, j, i: (j, 0)),
                  pl.BlockSpec((t, hp * VHEAD), lambda h, j, i: (j, heads // hp + h)),
                  pl.BlockSpec((t, hp * VHEAD), qidx), pl.BlockSpec((t, hp * VHEAD), qidx),
                  pl.BlockSpec((hp, t, LANES), lambda h, j, i: (h, jnp.maximum(i, j), 0))],
        out_specs=[pl.BlockSpec((s, hp * HEAD_W), lambda h, j, i: (0, h)),
                   pl.BlockSpec((t, hp * HEAD_W), lambda h, j, i: (j, h)),
                   pl.BlockSpec((t, hp * VHEAD), lambda h, j, i: (j, h))],
        out_shape=[jax.ShapeDtypeStruct((s, heads * HEAD_W), F32), jax.ShapeDtypeStruct((s, heads * HEAD_W), BF16),
                   jax.ShapeDtypeStruct((s, heads * VHEAD), BF16)],
        scratch_shapes=[pltpu.VMEM((t, hp * HEAD_W), F32), pltpu.VMEM((t, hp * VHEAD), F32)])


def _mla_bwd_mid(dq, dk, dv, rope_q, rope_k, heads, name):
    s = dq.shape[0]
    tb = _div_tile(s, 256, SUBLANES)

    def body(dq_ref, dk_ref, dv_ref, tq_ref, tk_ref, dqb_ref, dkv_ref, dkk_ref):
        tq = tq_ref[...]
        dkr = jnp.zeros((tb, 2 * ROPE), F32)
        for h in range(heads):
            cols = slice(h * HEAD_W, (h + 1) * HEAD_W)
            dqb_ref[:, cols] = (dq_ref[:, cols] * tq).astype(BF16)
            dkv_ref[:, h * NOPE:(h + 1) * NOPE] = dk_ref[:, h * HEAD_W:h * HEAD_W + NOPE]
            dkr = dkr + dk_ref[:, h * HEAD_W + NOPE:(h + 1) * HEAD_W].astype(F32)
        dkv_ref[:, heads * NOPE:] = dv_ref[...]
        dkk_ref[...] = (dkr + pltpu.roll(dkr, ROPE, axis=1)) * tk_ref[...]

    wq, wv = heads * HEAD_W, heads * VHEAD
    return pl.pallas_call(
        body, name=name, grid=(s // tb,),
        in_specs=[pl.BlockSpec((tb, wq), lambda i: (i, 0)), pl.BlockSpec((tb, wq), lambda i: (i, 0)),
                  pl.BlockSpec((tb, wv), lambda i: (i, 0)), pl.BlockSpec((tb, HEAD_W), lambda i: (i, 0)),
                  pl.BlockSpec((tb, 2 * ROPE), lambda i: (i, 0))],
        out_specs=[pl.BlockSpec((tb, wq), lambda i: (i, 0)), pl.BlockSpec((tb, heads * NOPE + wv), lambda i: (i, 0)),
                   pl.BlockSpec((tb, 2 * ROPE), lambda i: (i, 0))],
        out_shape=[jax.ShapeDtypeStruct((s, wq), BF16), jax.ShapeDtypeStruct((s, heads * NOPE + wv), BF16),
                   jax.ShapeDtypeStruct((s, 2 * ROPE), F32)],
        compiler_params=_params(),
    )(dq, dk, dv, rope_q, rope_k)


def _mla_bwd_post(z_lat, dqn, dkvn, dkk, q_g, kv_g, name):
    s, latw = z_lat.shape
    ql, kvl = q_g.shape[1], kv_g.shape[1]
    tb = _div_tile(s, 256, SUBLANES)

    def norm_bwd(xv, dn, g, dg_ref):
        r = lax.rsqrt(_rowmean(xv * xv) + EPS)
        xh = xv * r
        _accumulate(dg_ref, _colsum(dn * xh))
        dxh = dn * g
        return r * (dxh - xh * _rowmean(dxh * xh))

    def body(z_ref, dqn_ref, dkvn_ref, dkk_ref, qg_ref, kvg_ref, dz_ref, gq_ref, gkv_ref):
        dz_ref[:, :ql] = norm_bwd(z_ref[:, :ql], dqn_ref[...], qg_ref[...], gq_ref).astype(BF16)
        dz_ref[:, ql:ql + kvl] = norm_bwd(z_ref[:, ql:ql + kvl], dkvn_ref[...], kvg_ref[...], gkv_ref).astype(BF16)
        dz_ref[:, ql + kvl:] = dkk_ref[...].astype(BF16)

    return pl.pallas_call(
        body, name=name, grid=(s // tb,),
        in_specs=[pl.BlockSpec((tb, latw), lambda i: (i, 0)), pl.BlockSpec((tb, ql), lambda i: (i, 0)),
                  pl.BlockSpec((tb, kvl), lambda i: (i, 0)), pl.BlockSpec((tb, 2 * ROPE), lambda i: (i, 0)),
                  _row_spec(ql), _row_spec(kvl)],
        out_specs=[pl.BlockSpec((tb, latw), lambda i: (i, 0)), _row_spec(ql), _row_spec(kvl)],
        out_shape=[jax.ShapeDtypeStruct((s, latw), BF16), jax.ShapeDtypeStruct((1, ql), F32),
                   jax.ShapeDtypeStruct((1, kvl), F32)],
        compiler_params=_params(),
    )(z_lat, dqn, dkvn, dkk, q_g, kv_g)


CONV_ROWS = 128
CONV_HALO = 16


def _row_steps(n_rows, step):
    step(0, True)
    if n_rows > CONV_ROWS:
        def later(i, carry):
            step(pl.multiple_of(i * CONV_ROWS, CONV_ROWS), False)
            return carry
        lax.fori_loop(1, n_rows // CONV_ROWS, later, 0)


def _conv_taps(pre_ref, r0, first):
    if first:
        win = jnp.concatenate([jnp.zeros((CONV_HALO, pre_ref.shape[1]), F32), pre_ref[0:CONV_ROWS, :].astype(F32)])
    else:
        win = pre_ref[pl.ds(pl.multiple_of(r0 - CONV_HALO, CONV_HALO), CONV_ROWS + CONV_HALO), :].astype(F32)
    return win[CONV_HALO:], pltpu.roll(win, 1, axis=0)[CONV_HALO:], pltpu.roll(win, 2, axis=0)[CONV_HALO:]


def _conv(taps, w_ref, b_ref):
    return w_ref[2:3, :] * taps[0] + w_ref[1:2, :] * taps[1] + w_ref[0:1, :] * taps[2] + b_ref[...]


def _conv_fwd(up_pre, conv_w, conv_b, name):
    s, ff2 = up_pre.shape
    ff = ff2 // 2
    tc = _div_tile(ff, 256)
    nb = ff // tc
    assert s % CONV_ROWS == 0

    def body(pg_ref, pv_ref, wg_ref, wv_ref, bg_ref, bv_ref, act_ref):
        def step(r0, first):
            gate = _conv(_conv_taps(pg_ref, r0, first), wg_ref, bg_ref)
            val = _conv(_conv_taps(pv_ref, r0, first), wv_ref, bv_ref)
            act_ref[pl.ds(r0, CONV_ROWS), :] = (gate * _sigmoid(gate) * val).astype(BF16)

        _row_steps(s, step)

    def col(rows, off):
        return pl.BlockSpec((rows, tc), lambda j: (0, j + off))

    return pl.pallas_call(
        body, name=name, grid=(nb,),
        in_specs=[col(s, 0), col(s, nb), col(CONV_TAPS, 0), col(CONV_TAPS, nb), col(1, 0), col(1, nb)],
        out_specs=col(s, 0), out_shape=jax.ShapeDtypeStruct((s, ff), BF16), compiler_params=_params(),
    )(up_pre, up_pre, conv_w, conv_w, conv_b, conv_b)


def _conv_bwd(up_pre, dact, conv_w, conv_b, name, comm=None):
    s, ff2 = up_pre.shape
    ff = ff2 // 2
    tc = _div_tile(ff, 256)
    nb = ff // tc
    assert s % CONV_ROWS == 0

    def body(pg_ref, pv_ref, da_ref, wg_ref, wv_ref, bg_ref, bv_ref, dup_ref, gwg_ref, gwv_ref, gbg_ref, gbv_ref,
             dxg_ref, dxv_ref):
        for ref in (gwg_ref, gwv_ref, gbg_ref, gbv_ref):
            ref[...] = jnp.zeros(ref.shape, F32)
        for ref in (dxg_ref, dxv_ref):
            ref[s:s + SUBLANES, :] = jnp.zeros((SUBLANES, tc), F32)

        def sums(taps, dx, gw_ref, gb_ref):
            gb_ref[...] += _colsum(dx)
            for k in range(CONV_TAPS):
                gw_ref[k:k + 1, :] += _colsum(dx * taps[CONV_TAPS - 1 - k])

        def forward(r0, first):
            rows = pl.ds(r0, CONV_ROWS)
            taps_g, taps_v = _conv_taps(pg_ref, r0, first), _conv_taps(pv_ref, r0, first)
            gate, val = _conv(taps_g, wg_ref, bg_ref), _conv(taps_v, wv_ref, bv_ref)
            da = da_ref[rows, :].astype(F32)
            sg = _sigmoid(gate)
            dxv, dxg = da * gate * sg, da * val * sg * (1.0 + gate * (1.0 - sg))
            dxv_ref[rows, :], dxg_ref[rows, :] = dxv, dxg
            sums(taps_v, dxv, gwv_ref, gbv_ref)
            sums(taps_g, dxg, gwg_ref, gbg_ref)

        def backward(r0, first):
            del first
            n = CONV_ROWS + SUBLANES
            for dx_ref, w_ref, out_ref in ((dxg_ref, wg_ref, dup_ref.at[0]), (dxv_ref, wv_ref, dup_ref.at[1])):
                win = dx_ref[pl.ds(r0, n), :]
                ahead1 = pltpu.roll(win, n - 1, axis=0)[:CONV_ROWS]
                ahead2 = pltpu.roll(win, n - 2, axis=0)[:CONV_ROWS]
                out_ref[pl.ds(r0, CONV_ROWS), :] = (w_ref[2:3, :] * win[:CONV_ROWS] + w_ref[1:2, :] * ahead1
                                                    + w_ref[0:1, :] * ahead2).astype(BF16)

        _row_steps(s, forward)
        _row_steps(s, backward)

    def col(rows, off):
        return pl.BlockSpec((rows, tc), lambda j: (0, j + off))

    return _call(
        body, (up_pre, up_pre, dact, conv_w, conv_w, conv_b, conv_b), comm, name=name, grid=(nb,),
        in_specs=[col(s, 0), col(s, nb), col(s, 0), col(CONV_TAPS, 0), col(CONV_TAPS, nb), col(1, 0), col(1, nb)],
        out_specs=[pl.BlockSpec((2, s, tc), lambda j: (0, 0, j)), col(CONV_TAPS, 0), col(CONV_TAPS, 0),
                   col(1, 0), col(1, 0)],
        out_shape=[jax.ShapeDtypeStruct((2, s, ff), BF16)] + [jax.ShapeDtypeStruct((CONV_TAPS, ff), F32)] * 2
        + [jax.ShapeDtypeStruct((1, ff), F32)] * 2,
        scratch_shapes=[pltpu.VMEM((s + SUBLANES, tc), F32)] * 2)


def _ada_fwd(c_all, w, b, name):
    nseq, d = c_all.shape
    na = w.shape[1]
    tn = _div_tile(na, 512)

    def body(c_ref, w_ref, b_ref, o_ref):
        cv = c_ref[...]
        sc = cv * _sigmoid(cv)
        o_ref[...] = jnp.dot(sc, w_ref[...], preferred_element_type=F32, precision=lax.Precision.HIGHEST) + b_ref[...]

    return pl.pallas_call(
        body, name=name, grid=(na // tn,),
        in_specs=[pl.BlockSpec((nseq, d), lambda j: (0, 0)), pl.BlockSpec((d, tn), lambda j: (0, j)),
                  pl.BlockSpec((1, tn), lambda j: (0, j))],
        out_specs=pl.BlockSpec((nseq, tn), lambda j: (0, j)),
        out_shape=jax.ShapeDtypeStruct((nseq, na), F32), compiler_params=_params(),
    )(c_all, w, b)


def _ada_bwd(c_all_t, dmod, name):
    d, nseq = c_all_t.shape
    na = dmod.shape[1]
    tm, tn = _div_tile(d, 256, SUBLANES), _div_tile(na, 512)

    def body(c_ref, dm_ref, o_ref):
        cv = c_ref[...]
        sc = cv * _sigmoid(cv)
        acc = sc[:, 0:1] * dm_ref[0:1, :]
        for bi in range(1, nseq):
            acc = acc + sc[:, bi:bi + 1] * dm_ref[bi:bi + 1, :]
        o_ref[...] = acc

    return pl.pallas_call(
        body, name=name, grid=(d // tm, na // tn),
        in_specs=[pl.BlockSpec((tm, nseq), lambda i, j: (i, 0)), pl.BlockSpec((nseq, tn), lambda i, j: (0, j))],
        out_specs=pl.BlockSpec((tm, tn), lambda i, j: (i, j)),
        out_shape=jax.ShapeDtypeStruct((d, na), F32), compiler_params=_params(),
    )(c_all_t, dmod)


def _adamw(w, g, m, v, name, comm=None, after=None):
    rows, cols = w.shape
    tb = _div_tile(rows, max(SUBLANES, (256 * 1024) // cols // SUBLANES * SUBLANES), SUBLANES)
    c1 = 1.0 / (1.0 - ADAM_B1 ** ADAM_STEP)
    c2 = 1.0 / (1.0 - ADAM_B2 ** ADAM_STEP)

    def body(*refs):
        w_ref, g_ref, m_ref, v_ref = refs[:4]
        d_ref, nm_ref, nv_ref = refs[-3:]
        gv = g_ref[...]
        nm = ADAM_B1 * m_ref[...] + (1.0 - ADAM_B1) * gv
        nv = ADAM_B2 * v_ref[...] + (1.0 - ADAM_B2) * (gv * gv)
        nm_ref[...] = nm
        nv_ref[...] = nv
        d_ref[...] = -ADAM_LR * ((nm * c1) / (jnp.sqrt(nv * c2) + ADAM_EPS) + ADAM_WD * w_ref[...])

    blk = pl.BlockSpec((tb, cols), lambda i: (i, 0))
    operands, in_specs = (w, g, m, v), [blk] * 4
    if after is not None:
        operands, in_specs = operands + (after,), in_specs + [pl.BlockSpec(after.shape, lambda i: (0, 0))]
    return _call(body, operands, comm, name=name, grid=(rows // tb,), in_specs=in_specs, out_specs=[blk] * 3,
                 out_shape=[jax.ShapeDtypeStruct((rows, cols), F32)] * 3)


def _sum_leading(parts, name, after=()):
    n, rows, cols = parts.shape
    tb = _div_tile(rows, 512, SUBLANES)

    def body(p_ref, *rest):
        o_ref = rest[-1]
        acc = p_ref[0]
        for k in range(1, n):
            acc = acc + p_ref[k]
        o_ref[...] = acc

    return pl.pallas_call(
        body, name=name, grid=(rows // tb,),
        in_specs=[pl.BlockSpec((n, tb, cols), lambda i: (0, i, 0))] + [pl.BlockSpec(memory_space=pl.ANY)] * len(after),
        out_specs=pl.BlockSpec((tb, cols), lambda i: (i, 0)),
        out_shape=jax.ShapeDtypeStruct((rows, cols), F32), compiler_params=_params(),
    )(parts, *after)


def _place():
    x, y, c = lax.axis_index("x"), lax.axis_index("y"), lax.axis_index("c")
    return x, y, c, [(1 - x, y), (x, 1 - y), (1 - x, 1 - y)]


def _all_gather(block, name):
    m_per, n = block.shape

    def body(x_ref, out_ref, send_sems, recv_sems, local_sem):
        x, y, c, chips = _place()
        me, sibling = (x, y, c), (x, y, 1 - c)

        def rows(px, py, pc):
            return out_ref.at[pl.ds((4 * px + 2 * py + pc) * m_per, m_per), :]

        def copy(k, blk, to, src=None):
            return pltpu.make_async_remote_copy(
                src_ref=rows(*blk) if src is None else src, dst_ref=rows(*blk), send_sem=send_sems.at[k],
                recv_sem=recv_sems.at[k], device_id=to, device_id_type=MESH)

        mine = pltpu.make_async_copy(x_ref, rows(*me), local_sem)
        mine.start()
        first = [copy(0, me, sibling, src=x_ref)]
        first += [copy(1 + j, me, (*chip, c), src=x_ref) for j, chip in enumerate(chips)]
        for cp in first:
            cp.start()
        passed = [copy(4 + j, (*chip, c), sibling) for j, chip in enumerate(chips)]
        for j, chip in enumerate(chips):
            copy(1 + j, (*chip, c), me).wait_recv()
            passed[j].start()
        copy(0, sibling, me).wait_recv()
        for j, chip in enumerate(chips):
            copy(4 + j, (*chip, 1 - c), me).wait_recv()
        for cp in first + passed:
            cp.wait_send()
        mine.wait()

    return pl.pallas_call(
        body, name=name, out_shape=jax.ShapeDtypeStruct((N_DEV * m_per, n), block.dtype),
        in_specs=[pl.BlockSpec(memory_space=pltpu.VMEM)], out_specs=pl.BlockSpec(memory_space=pltpu.VMEM),
        scratch_shapes=[pltpu.SemaphoreType.DMA((7,)), pltpu.SemaphoreType.DMA((7,)), pltpu.SemaphoreType.DMA],
        compiler_params=_params(),
    )(block)


def _hbm_specs(n):
    return [pl.BlockSpec(memory_space=HBM)] * n


def _part(ref, by_cols, half, quarter=None, lead=None):
    extent = ref.shape[-1] if by_cols else ref.shape[-2]
    size = extent // 2 if quarter is None else extent // 4
    first = half * (extent // 2) + (0 if quarter is None else quarter * size)
    tile = LANES if by_cols else 2 * SUBLANES
    span = pl.ds(pl.multiple_of(first, tile) if size % tile == 0 else first, size)
    index = (slice(None), span) if by_cols else (span, slice(None))
    return ref.at[index] if lead is None else ref.at[(lead,) + index]


def _half_rows(ref, half, lead=None):
    return _part(ref, False, half, lead=lead)


class _Comm:
    def __init__(self, operands, out_shape, sem_dims, build, aliases=None):
        self.operands, self.out_shape, self.sem_dims = list(operands), list(out_shape), list(sem_dims)
        self.scratch = [pltpu.SemaphoreType.DMA(d) for d in sem_dims]
        self.build, self.aliases = build, dict(aliases or {})


class _SemGrid:
    def __init__(self, sems, dims):
        self.sems, self.dims, self.at = list(sems), tuple(dims), self

    def __getitem__(self, index):
        index = index if isinstance(index, tuple) else (index,)
        flat = 0
        for i, d in zip(index, self.dims):
            flat = flat * d + i
        return self.sems[flat]


def _call(body, operands, comm=None, *, name, grid, in_specs, out_specs, out_shape, scratch_shapes=(),
          input_output_aliases=None):
    aliases = dict(input_output_aliases or {})
    if comm is None:
        return pl.pallas_call(
            body, name=name, grid=grid, in_specs=in_specs, out_specs=out_specs, out_shape=out_shape,
            scratch_shapes=list(scratch_shapes), input_output_aliases=aliases, compiler_params=_params())(*operands)
    single = not isinstance(out_shape, (list, tuple))
    outs = [out_shape] if single else list(out_shape)
    ospecs = [out_specs] if single else list(out_specs)
    n_in, n_out, n_scr = len(operands), len(outs), len(scratch_shapes)
    c_in, c_out = len(comm.operands), len(comm.out_shape)
    for i, o in comm.aliases.items():
        aliases[n_in + i] = n_out + o

    def hosted(*refs):
        ins, c_ins = refs[:n_in], refs[n_in:n_in + c_in]
        o0 = n_in + c_in
        o_refs, c_outs = refs[o0:o0 + n_out], refs[o0 + n_out:o0 + n_out + c_out]
        s0 = o0 + n_out + c_out
        scr, sems = refs[s0:s0 + n_scr], refs[s0 + n_scr:]
        stages = comm.build(c_ins, c_outs, sems)
        step, n_steps = 0, 1
        for dim, size in enumerate(grid):
            step, n_steps = step * size + pl.program_id(dim), n_steps * size
        pl.when(step == 0)(stages[0])
        body(*ins, *o_refs, *scr)
        for stage in stages[1:-1]:
            pl.when(step == (n_steps * MIDDLE_STAGE_AT) // 100)(stage)
        pl.when(step == n_steps - 1)(stages[-1])

    res = pl.pallas_call(
        hosted, name=name, grid=grid, in_specs=list(in_specs) + _hbm_specs(c_in),
        out_specs=ospecs + _hbm_specs(c_out), out_shape=outs + comm.out_shape,
        scratch_shapes=list(scratch_shapes) + comm.scratch, input_output_aliases=aliases,
        compiler_params=_params())(*operands, *comm.operands)
    return (res[0] if single else res[:n_out]), res[n_out:]


def _run_comm(comm, name):
    c_in, c_out = len(comm.operands), len(comm.out_shape)

    def body(*refs):
        for stage in comm.build(refs[:c_in], refs[c_in:c_in + c_out], refs[c_in + c_out:]):
            stage()

    return pl.pallas_call(
        body, name=name, in_specs=_hbm_specs(c_in), out_specs=_hbm_specs(c_out), out_shape=comm.out_shape,
        scratch_shapes=comm.scratch, input_output_aliases=comm.aliases, compiler_params=_params())(*comm.operands)


def _join_comms(comms):
    def build(in_refs, out_refs, sems):
        staged, i, o, k = [], 0, 0, 0
        for cm in comms:
            ni, no, ns = len(cm.operands), len(cm.out_shape), len(cm.sem_dims)
            staged.append(cm.build(in_refs[i:i + ni], out_refs[o:o + no], sems[k:k + ns]))
            i, o, k = i + ni, o + no, k + ns
        def run(fns):
            def stage():
                for fn in fns:
                    fn()
            return stage

        return (run([st[0] for st in staged]), run([fn for st in staged for fn in st[1:-1]]),
                run([st[-1] for st in staged]))

    aliases, i, o = {}, 0, 0
    for cm in comms:
        aliases.update({i + a: o + b for a, b in cm.aliases.items()})
        i, o = i + len(cm.operands), o + len(cm.out_shape)
    return _Comm(sum((cm.operands for cm in comms), []), sum((cm.out_shape for cm in comms), []),
                 sum((cm.sem_dims for cm in comms), []), build, aliases)


def _gather8_comm(block):
    def build(in_refs, out_refs, sems):
        (src,), (out,), (send_sems, recv_sems) = in_refs, out_refs, sems
        x, y, c, chips = _place()
        me, sibling = (x, y, c), (x, y, 1 - c)

        def copy(k, blk, to, own=False):
            dst = out.at[4 * blk[0] + 2 * blk[1] + blk[2]]
            return pltpu.make_async_remote_copy(
                src_ref=src if own else dst, dst_ref=dst, send_sem=send_sems.at[k], recv_sem=recv_sems.at[k],
                device_id=to, device_id_type=MESH)

        first = [copy(0, me, sibling, own=True)] + [copy(1 + j, me, (*chip, c), own=True)
                                                     for j, chip in enumerate(chips)]
        passed = [copy(4 + j, (*chip, c), sibling) for j, chip in enumerate(chips)]

        def start():
            for cp in first:
                cp.start()

        def middle():
            for j, chip in enumerate(chips):
                copy(1 + j, (*chip, c), me).wait_recv()
                passed[j].start()

        def finish():
            copy(0, sibling, me).wait_recv()
            for j, chip in enumerate(chips):
                copy(4 + j, (*chip, 1 - c), me).wait_recv()
            for cp in first + passed:
                cp.wait_send()

        return start, middle, finish

    return _Comm([block], [jax.ShapeDtypeStruct((N_DEV,) + block.shape, block.dtype)], [(7,), (7,)], build)


def _gather_comm(shards, by_cols=()):
    nw = len(shards)

    def build(in_refs, out_refs, sems):
        send_sems, recv_sems = sems
        x, y, c, chips = _place()
        me, sibling = (x, y, c), (x, y, 1 - c)
        across_x, across_y, diagonal = chips

        def copy(w, k, block, part, to, src=None):
            dst = _part(out_refs[w], w in by_cols, part[1], part[2] if part[0] else None, 2 * block[0] + block[1])
            return pltpu.make_async_remote_copy(
                src_ref=dst if src is None else src, dst_ref=dst, send_sem=send_sems.at[w, k],
                recv_sem=recv_sems.at[w, k], device_id=to, device_id_type=MESH)

        first = [copy(w, j, (x, y), (0, c), (*chip, c), src=_part(in_refs[w], w in by_cols, c))
                 for w in range(nw) for j, chip in enumerate((across_x, across_y))]
        passed = [[copy(w, 2, across_x, (1, c, 0), (*across_y, c)), copy(w, 3, across_y, (1, c, 1), (*across_x, c)),
                   copy(w, 4, across_x, (0, c), sibling), copy(w, 5, across_y, (0, c), sibling)] for w in range(nw)]
        last = [[copy(w, 6, diagonal, (1, c, 0), sibling), copy(w, 7, diagonal, (1, c, 1), sibling)]
                for w in range(nw)]

        def start():
            for cp in first:
                cp.start()

        def middle():
            for w in range(nw):
                copy(w, 0, across_x, (0, c), me).wait_recv()
                copy(w, 1, across_y, (0, c), me).wait_recv()
                for cp in passed[w]:
                    cp.start()

        def finish():
            for w in range(nw):
                copy(w, 2, diagonal, (1, c, 0), me).wait_recv()
                copy(w, 3, diagonal, (1, c, 1), me).wait_recv()
                for cp in last[w]:
                    cp.start()
            for w in range(nw):
                for k, block, part in ((4, across_x, (0, 1 - c)), (5, across_y, (0, 1 - c)),
                                       (6, diagonal, (1, 1 - c, 0)), (7, diagonal, (1, 1 - c, 1))):
                    copy(w, k, block, part, me).wait_recv()
            for cp in first + sum(passed, []) + sum(last, []):
                cp.wait_send()

        return start, middle, finish

    return _Comm(shards, [jax.ShapeDtypeStruct((N_CHIPS,) + w.shape, w.dtype) for w in shards],
                 [(nw, 8), (nw, 8)], build)


def _halved(shape, by_cols):
    return shape[:-1] + (shape[-1] // 2,) if by_cols else shape[:-2] + (shape[-2] // 2, shape[-1])


def _swap_comm(gs, by_cols=()):
    nw = len(gs)

    def build(in_refs, out_refs, sems):
        send_sems, recv_sems = sems
        x, y, c, _ = _place()
        cps = []
        for w in range(nw):
            cps.append(pltpu.make_async_remote_copy(
                src_ref=_part(in_refs[w], w in by_cols, 1 - c, lead=slice(None)), dst_ref=out_refs[w],
                send_sem=send_sems.at[w], recv_sem=recv_sems.at[w], device_id=(x, y, 1 - c), device_id_type=MESH))

        def start():
            for cp in cps:
                cp.start()

        def finish():
            for cp in cps:
                cp.wait()

        return start, finish

    return _Comm(gs, [jax.ShapeDtypeStruct(_halved(g.shape, w in by_cols), g.dtype) for w, g in enumerate(gs)],
                 [(nw,), (nw,)], build)


def _exchange_comm(s1s):
    nw = len(s1s)

    def build(in_refs, out_refs, sems):
        send_sems, recv_sems = sems
        x, y, c, chips = _place()
        cps = [pltpu.make_async_remote_copy(
            src_ref=in_refs[w].at[2 * chip[0] + chip[1]], dst_ref=out_refs[w].at[j], send_sem=send_sems.at[w, j],
            recv_sem=recv_sems.at[w, j], device_id=(*chip, c), device_id_type=MESH)
            for w in range(nw) for j, chip in enumerate(chips)]

        def start():
            for cp in cps:
                cp.start()

        def finish():
            for cp in cps:
                cp.wait()

        return start, finish

    return _Comm(s1s, [jax.ShapeDtypeStruct((N_CHIPS - 1,) + s.shape[1:], s.dtype) for s in s1s],
                 [(nw, 3), (nw, 3)], build)


def _size(dims):
    n = 1
    for d in dims:
        n *= d
    return n


def _sem_grids(comm, sem_refs):
    grids, pos = [], 0
    for dims in comm.sem_dims:
        grids.append(_SemGrid(sem_refs[pos:pos + _size(dims)], dims))
        pos += _size(dims)
    return grids


def _comm_split_start(comm, name, after=()):
    c_in, c_out = len(comm.operands), len(comm.out_shape)
    counts = [_size(d) for d in comm.sem_dims]
    n_sem = sum(counts)
    assert not comm.aliases

    def body(*refs):
        srcs, lands = refs[:c_in], refs[c_in:c_in + c_out]
        first_sem = c_in + c_out + len(after)
        start, _ = comm.build(srcs, lands, _sem_grids(comm, refs[first_sem:first_sem + n_sem]))
        start()
        refs[-1][...] = jnp.zeros(refs[-1].shape, refs[-1].dtype)

    lands = [pltpu.with_memory_space_constraint(lax.empty(o.shape, o.dtype), HBM) for o in comm.out_shape]
    srcs = [pltpu.with_memory_space_constraint(a, HBM) for a in comm.operands]
    res = pl.pallas_call(
        body, name=name, in_specs=_hbm_specs(c_in + c_out) + [pl.BlockSpec(memory_space=pl.ANY)] * len(after),
        out_specs=[pl.BlockSpec(memory_space=pltpu.SEMAPHORE)] * n_sem + _hbm_specs(c_in + c_out)
        + [pl.BlockSpec(memory_space=pltpu.VMEM)],
        out_shape=[pltpu.SemaphoreType.DMA(())] * n_sem + [pltpu.HBM(a.shape, a.dtype) for a in comm.operands]
        + [pltpu.HBM(o.shape, o.dtype) for o in comm.out_shape] + [jax.ShapeDtypeStruct((SUBLANES, LANES), F32)],
        input_output_aliases={i: n_sem + i for i in range(c_in + c_out)},
        compiler_params=_params(has_side_effects=pltpu.SideEffectType.DATAFLOW_SIDE_EFFECTING))(*srcs, *lands, *after)
    return res[:-1], res[-1]


def _comm_split_wait(comm, state, after, name):
    c_in, c_out, n_sem = len(comm.operands), len(comm.out_shape), sum(_size(d) for d in comm.sem_dims)
    sems, srcs, lands = state[:n_sem], state[n_sem:n_sem + c_in], state[n_sem + c_in:]

    def body(*refs):
        src_refs, land_refs = refs[:c_in], refs[c_in:c_in + c_out]
        _, finish = comm.build(src_refs, land_refs, _sem_grids(comm, refs[c_in + c_out:c_in + c_out + n_sem]))
        finish()

    sem_spec = pl.BlockSpec(memory_space=pltpu.SEMAPHORE)
    res = pl.pallas_call(
        body, name=name, in_specs=_hbm_specs(c_in + c_out) + [sem_spec] * n_sem + [pl.BlockSpec(memory_space=pl.ANY)],
        out_specs=_hbm_specs(c_in + c_out),
        out_shape=[pltpu.HBM(a.shape, a.dtype) for a in srcs] + [pltpu.HBM(o.shape, o.dtype) for o in lands],
        input_output_aliases={i: i for i in range(c_in + c_out)},
        compiler_params=_params(has_side_effects=pltpu.SideEffectType.DATAFLOW_SIDE_EFFECTING),
    )(*srcs, *lands, *sems, after)
    return res[:c_in], res[c_in:]


def _share_comm(fs, by_cols=()):
    nw = len(fs)

    def build(in_refs, out_refs, sems):
        del in_refs
        send_sems, recv_sems = sems
        x, y, c, _ = _place()

        def copy(w, half):
            part = _part(out_refs[w], w in by_cols, half)
            return pltpu.make_async_remote_copy(
                src_ref=part, dst_ref=part, send_sem=send_sems.at[w], recv_sem=recv_sems.at[w],
                device_id=(x, y, 1 - c), device_id_type=MESH)

        sends = [copy(w, c) for w in range(nw)]

        def start():
            for cp in sends:
                cp.start()

        def finish():
            for w in range(nw):
                copy(w, 1 - c).wait_recv()
            for cp in sends:
                cp.wait_send()

        return start, finish

    return _Comm(fs, [jax.ShapeDtypeStruct(f.shape, f.dtype) for f in fs],
                 [(nw,), (nw,)], build,
                 aliases={w: w for w in range(nw)})


def _add_sibling(g, r1, place, name, by_cols=False):
    nch, h, cols = r1.shape
    tr = _div_tile(h, 1024 if by_cols else 512, 2 * SUBLANES)
    nb = h // tr
    mine = (lambda k, i, p: (k, i, p[0])) if by_cols else (lambda k, i, p: (k, p[0] * nb + i, 0))

    def body(place_ref, g_ref, r_ref, o_ref):
        del place_ref
        o_ref[...] = (g_ref[...].astype(F32) + r_ref[...].astype(F32)).astype(BF16)

    spec = pltpu.PrefetchScalarGridSpec(
        num_scalar_prefetch=1, grid=(nch, nb),
        in_specs=[pl.BlockSpec((None, tr, cols), mine), pl.BlockSpec((None, tr, cols), lambda k, i, p: (k, i, 0))],
        out_specs=pl.BlockSpec((None, tr, cols), lambda k, i, p: (k, i, 0)))
    return pl.pallas_call(body, name=name, grid_spec=spec, out_shape=jax.ShapeDtypeStruct((nch, h, cols), BF16),
                          compiler_params=_params())(place, g, r1)


def _add_chips(s1, r2, place, name, by_cols=False):
    _, h, cols = s1.shape
    tr = _div_tile(h, 1024 if by_cols else 512, 2 * SUBLANES)
    nb = h // tr
    mine = (lambda i, p: (i, p[0])) if by_cols else (lambda i, p: (p[0] * nb + i, 0))
    whole = (h, 2 * cols) if by_cols else (2 * h, cols)

    def body(place_ref, s_ref, r_ref, o_ref):
        del place_ref
        acc = s_ref[...].astype(F32)
        for j in range(N_CHIPS - 1):
            acc = acc + r_ref[j].astype(F32)
        o_ref[...] = acc

    spec = pltpu.PrefetchScalarGridSpec(
        num_scalar_prefetch=1, grid=(nb,),
        in_specs=[pl.BlockSpec((None, tr, cols), lambda i, p: (p[1], i, 0)),
                  pl.BlockSpec((N_CHIPS - 1, tr, cols), lambda i, p: (0, i, 0))],
        out_specs=pl.BlockSpec((tr, cols), mine))
    return pl.pallas_call(body, name=name, grid_spec=spec, out_shape=jax.ShapeDtypeStruct(whole, F32),
                          compiler_params=_params())(place, s1, r2)


def _quarter_turn(m):
    h = m.shape[-1] // 2
    return jnp.concatenate([-m[..., h:], m[..., :h]], axis=-1)


def _quarter_turn_back(m):
    h = m.shape[-1] // 2
    return jnp.concatenate([m[..., h:], -m[..., :h]], axis=-1)


def _stack_rows(parts):
    out = lax.empty((sum(p.shape[0] for p in parts),) + parts[0].shape[1:], parts[0].dtype)
    row = 0
    for p in parts:
        out = lax.dynamic_update_slice(out, p, (row, 0))
        row += p.shape[0]
    return out


def _join_cols(sh):
    return jnp.concatenate([sh[k] for k in range(N_CHIPS)], axis=1)


def _split_cols(full):
    c = full.shape[1] // N_CHIPS
    return jnp.stack([full[:, k * c:(k + 1) * c] for k in range(N_CHIPS)])


def kernel(x, c, positions, w_ada, b_ada, pre_norm1_g, w_in, gm_ln_g, gm_ln_b, gm_w_s, gm_b_s, w_branch_a, q_norm_g, w_uq, kv_norm_g, w_ukv, w_branch_b, w_out, post_norm1_g, pre_norm2_g, w_up, conv_w, conv_b, w_down, post_norm2_g, loss_target, m_w_ada, m_b_ada, m_pre_norm1_g, m_w_in, m_gm_ln_g, m_gm_ln_b, m_gm_w_s, m_gm_b_s, m_w_branch_a, m_q_norm_g, m_w_uq, m_kv_norm_g, m_w_ukv, m_w_branch_b, m_w_out, m_post_norm1_g, m_pre_norm2_g, m_w_up, m_conv_w, m_conv_b, m_w_down, m_post_norm2_g, v_w_ada, v_b_ada, v_pre_norm1_g, v_w_in, v_gm_ln_g, v_gm_ln_b, v_gm_w_s, v_gm_b_s, v_w_branch_a, v_q_norm_g, v_w_uq, v_kv_norm_g, v_w_ukv, v_w_branch_b, v_w_out, v_post_norm1_g, v_pre_norm2_g, v_w_up, v_conv_w, v_conv_b, v_w_down, v_post_norm2_g):
    given = dict(locals())
    s, d = x.shape[1], x.shape[2]
    gw = gm_ln_g.shape[0]
    ql, kvl = q_norm_g.shape[0], kv_norm_g.shape[0]
    heads = N_CHIPS * w_uq.shape[1] // (NOPE + ROPE)
    ff = N_CHIPS * w_down.shape[0]
    assert gw == d and N_CHIPS * w_ukv.shape[1] == heads * (NOPE + VHEAD)
    ix, iy, ic = lax.axis_index("x"), lax.axis_index("y"), lax.axis_index("c")
    chip = 2 * ix + iy
    dev = 2 * chip + ic
    row = lambda v: v.reshape(1, -1)

    first = _all_gather(jnp.concatenate([jnp.pad(c, ((0, SUBLANES - 1), (0, 0))),
                                         jnp.pad(conv_w, ((0, SUBLANES - CONV_TAPS), (0, 0)))], axis=1), "gather_c")
    first = first.reshape(N_DEV, SUBLANES, d + conv_w.shape[1])
    c_all = first[:, 0, :d]
    conv_wf = first[::N_CORES, :CONV_TAPS, d:].transpose(1, 0, 2).reshape(CONV_TAPS, N_CHIPS * conv_w.shape[1])
    na = w_ada.shape[1]
    b_ada_mine = lax.dynamic_slice(b_ada, (chip * na,), (na,))
    mod_cols = _ada_fwd(c_all, w_ada, row(b_ada_mine), "ada_fwd")
    mod_all = _all_gather(mod_cols, "gather_mod").reshape(N_CHIPS, N_CORES, N_DEV, na)[:, 0]
    mod = lax.dynamic_index_in_dim(mod_all, dev, axis=1, keepdims=False).reshape(N_MOD, d)
    shift1, scale1, gate1, shift2, scale2, gate2 = (mod[i:i + 1] for i in range(N_MOD))

    mine = {n: (given[n].T if n == "w_in" else given[n]).astype(BF16) for n in BIG}
    gather = lambda names: _gather_comm([mine[n] for n in names], [i for i, n in enumerate(names) if n == "w_in"])
    whole = lambda n, g: lax.dynamic_update_slice(g, mine[n][None], (chip, 0, 0))
    rows4 = lambda sh4: sh4.reshape(-1, sh4.shape[2])
    wi_t = rows4(whole("w_in", _run_comm(gather(["w_in"]), "gather_w_in")[0]))
    o_q, o_kv, o_pe, o_ga = 2 * gw, 2 * gw + ql, 2 * gw + ql + kvl, 2 * gw + ql + kvl + ROPE
    w_in_big_t = _stack_rows([wi_t[:o_q], wi_t[o_ga:]])
    w_in_lat_t = _stack_rows([wi_t[o_q:o_ga], _quarter_turn(wi_t[o_pe:o_ga].T).T])

    inv = ROPE_THETA ** (-jnp.arange(0, ROPE, 2, dtype=F32) / ROPE)
    ang = positions[0].astype(F32)[:, None] * inv
    cos, sin = jnp.cos(ang), jnp.sin(ang)
    rope_k = jnp.concatenate([cos, cos, sin, sin], axis=1)
    softmax_scale = float(NOPE + ROPE) ** -0.5
    rope_q = jnp.concatenate([jnp.ones((s, NOPE), F32), rope_k], axis=1) * softmax_scale

    x2d, tgt = x[0], loss_target[0]
    g_pre1, g_post1, g_pre2, g_post2 = row(pre_norm1_g), row(post_norm1_g), row(pre_norm2_g), row(post_norm2_g)
    ln_g, ln_b, q_g, kv_g = row(gm_ln_g), row(gm_ln_b), row(q_norm_g), row(kv_norm_g)
    b_s_t = gm_b_s.T
    conv_bf = row(conv_b)

    h1 = _prenorm(x2d, g_pre1, scale1, shift1, "prenorm1")
    z_big, (g_uq, g_ukv, g_a) = _matmul(h1, w_in_big_t, mode="nt", out_dtype=BF16, name="mm_z_big", tm=s,
                                        comm=gather(["w_uq", "w_ukv", "w_branch_a"]))
    wq = _join_cols(whole("w_uq", g_uq)).reshape(ql, heads, NOPE + ROPE)
    w_q = jnp.concatenate([wq, _quarter_turn(wq[:, :, NOPE:])], axis=2).reshape(ql, heads * HEAD_W)
    w_kv = _join_cols(whole("w_ukv", g_ukv)).reshape(kvl, heads, 2, NOPE).transpose(0, 2, 1, 3)
    w_kv = w_kv.reshape(kvl, 2 * heads * NOPE)
    w_a = rows4(whole("w_branch_a", g_a))
    z_lat = _matmul(h1, w_in_lat_t, mode="nt", out_dtype=F32, name="mm_z_lat", tm=s, tn=1024)
    a_act = _gmlp_fwd(z_big, ln_g, ln_b, gm_w_s, b_s_t, "gmlp_fwd")
    qn, kvn, kr = _mla_prep(z_lat, q_g, kv_g, rope_k, "mla_prep")
    q_rot = _matmul(qn, w_q, mode="nn", out_dtype=BF16, name="mm_q", tm=s, tn=HEAD_W, mul=rope_q)
    kv_all = _matmul(kvn, w_kv, mode="nn", out_dtype=BF16, name="mm_kv", tm=s, tn=1024)
    (o_att, lse), (g_b, g_o, g_up) = _attn_fwd(q_rot, kv_all, kr, heads, "attn_fwd",
                                               comm=gather(["w_branch_b", "w_out", "w_up"]))
    w_b, w_o, w_upf = rows4(whole("w_branch_b", g_b)), rows4(whole("w_out", g_o)), whole("w_up", g_up)
    y_a = _matmul(a_act, w_a, mode="nn", out_dtype=BF16, name="mm_y_a", tm=s)
    y_b = _matmul(o_att, w_b, mode="nn", out_dtype=BF16, name="mm_y_b", tm=s)
    merged = _merge(z_big, y_a, y_b, "merge")
    y1 = _matmul(merged, w_o, mode="nn", out_dtype=F32, name="mm_y1", tm=s)
    x1, h2 = _post_pre(x2d, y1, gate1, g_post1, g_pre2, scale2, shift2, "post1_pre2")

    up_pre, (g_dn,) = _matmul(h2, w_upf, mode="nn", out_dtype=BF16, name="mm_up", tm=s, tn=1408,
                              comm=gather(["w_down"]))
    w_dn = rows4(whole("w_down", g_dn))
    act = _conv_fwd(up_pre, conv_wf, conv_bf, "conv_fwd")
    ffn = _matmul(act, w_dn, mode="nn", out_dtype=F32, name="mm_ffn", tm=s, tn=1024, tk=1408)

    dffn, dgate2, g_post2_grad, dx2, loss_part = _post_bwd(ffn, gate2, g_post2, "post2_bwd", xin=x1, target=tgt)
    loss = lax.psum(loss_part[0, 0], ("x", "y", "c"))
    place = jnp.stack([ic, chip]).astype(jnp.int32)
    rows_of = lambda g: g.reshape(N_CHIPS, g.shape[0] // N_CHIPS, g.shape[1])
    add_sibling = lambda names, gs, r1s: [_add_sibling(g, r1, place, "rs_add_sibling_" + n, by_cols=n == "w_in")
                                          for n, g, r1 in zip(names, gs, r1s)]
    add_chips = lambda names, s1s, r2s: [_add_chips(s1, r2, place, "rs_add_chips_" + n, by_cols=n == "w_in")
                                         for n, s1, r2 in zip(names, s1s, r2s)]
    gp_down = [rows_of(_matmul(act, dffn, mode="tn", out_dtype=BF16, name="mm_gw_down", tn=2048, tk=s))]
    dact, r1_down = _matmul(dffn, w_dn, mode="nt", out_dtype=BF16, name="mm_dact", tm=s, comm=_swap_comm(gp_down))
    s1_down = add_sibling(["w_down"], gp_down, r1_down)
    (dup, gcw_g, gcw_v, gcb_g, gcb_v), r2_down = _conv_bwd(up_pre, dact, conv_wf, conv_bf, "conv_bwd",
                                                            comm=_exchange_comm(s1_down))
    half_down = add_chips(["w_down"], s1_down, r2_down)
    dh2 = _matmul(dup, w_upf, mode="nt", out_dtype=F32, name="mm_dh2", tm=s, tn=1024, tk=1408)
    dx1, dshift2, dscale2, g_pre2_grad = _prenorm_bwd(x1, dh2, dx2, g_pre2, scale2, "prenorm2_bwd")

    dy1, dgate1, g_post1_grad = _post_bwd(y1, gate1, g_post1, "post1_bwd", dxo=dx1)
    dmerged = _matmul(dy1, w_o, mode="nt", out_dtype=BF16, name="mm_dmerged", tm=s)
    gw_out = _matmul(merged, dy1, mode="tn", out_dtype=BF16, name="mm_gw_out", tn=1024, tk=s)
    dy_a, dy_b, dz_big = _merge_bwd(dmerged, z_big, y_a, y_b, "merge_bwd")
    gw_a = _matmul(a_act, dy_a, mode="tn", out_dtype=BF16, name="mm_gw_a", tn=1024, tk=s)
    gw_b = _matmul(o_att, dy_b, mode="tn", out_dtype=BF16, name="mm_gw_b", tn=1024, tk=s)
    mid = ["w_up", "w_out", "w_branch_a", "w_branch_b"]
    gp_oab = [rows_of(gw_out), rows_of(gw_a), rows_of(gw_b)]
    da, r1_oab = _matmul(dy_a, w_a, mode="nt", out_dtype=BF16, name="mm_da", tm=s, comm=_swap_comm(gp_oab))
    s1_oab = add_sibling(mid[1:], gp_oab, r1_oab)
    gw_up, r2_oa = _matmul(h2, dup, mode="tn", out_dtype=BF16, name="mm_gw_up", tm=1024, tn=1408, tk=s,
                           out_groups=N_CHIPS, comm=_exchange_comm(s1_oab[:2]))
    do = _matmul(dy_b, w_b, mode="nt", out_dtype=BF16, name="mm_do", tm=s)
    (dz_big, g_ws, g_bs_t, g_ln_g, g_ln_b), r1_up = _gmlp_bwd(z_big, da, dz_big, ln_g, ln_b, gm_w_s, b_s_t,
                                                               "gmlp_bwd", comm=_swap_comm([gw_up]))
    s1_mid = add_sibling(mid[:1], [gw_up], r1_up) + s1_oab
    (dq, dk, dv), r2_up = _attn_bwd(q_rot, kv_all, kr, o_att, do, lse, heads, "attn_bwd",
                                    comm=_exchange_comm(s1_mid[:1]))
    dq_big, dkv, dkk = _mla_bwd_mid(dq, dk, dv, rope_q, rope_k, heads, "mla_bwd_mid")
    gw_q = _matmul(qn, dq_big, mode="tn", out_dtype=F32, name="mm_gw_q", tn=1024, tk=s)
    dqn = _matmul(dq_big, w_q, mode="nt", out_dtype=F32, name="mm_dqn", tm=s, tk=1024)
    gw_kv = _matmul(kvn, dkv, mode="tn", out_dtype=BF16, name="mm_gw_kv", tn=1024, tk=s)
    dkvn = _matmul(dkv, w_kv, mode="nt", out_dtype=F32, name="mm_dkvn", tm=s, tk=1024)
    dz_lat, g_q, g_kv = _mla_bwd_post(z_lat, dqn, dkvn, dkk, q_g, kv_g, "mla_bwd_post")

    partial = {
        "gm_ln_g": g_ln_g, "gm_ln_b": g_ln_b, "gm_w_s": g_ws, "gm_b_s": g_bs_t[:, :gm_b_s.shape[0]].T,
        "q_norm_g": g_q, "kv_norm_g": g_kv, "post_norm1_g": g_post1_grad, "pre_norm2_g": g_pre2_grad,
        "conv_w": jnp.concatenate([gcw_g, gcw_v], axis=1), "conv_b": jnp.concatenate([gcb_g, gcb_v], axis=1),
        "post_norm2_g": g_post2_grad,
    }
    flat = jnp.concatenate([partial[n].reshape(-1) for n in SMALL_PARTIAL])
    n_small = flat.shape[0]
    rows_small = -(-n_small // (LANES * SMALL_ROW_TILE)) * SMALL_ROW_TILE
    flat = jnp.pad(flat, (0, rows_small * LANES - n_small)).reshape(rows_small, LANES)

    def small_pack(prefix, source):
        v = jnp.concatenate([source[prefix + n].reshape(-1) for n in SMALL])
        rows = -(-v.shape[0] // (LANES * SUBLANES)) * SUBLANES
        return jnp.pad(v, (0, rows * LANES - v.shape[0])).reshape(rows, LANES)

    small_state = [small_pack(prefix, given) for prefix in ("", "m_", "v_")]

    dh1, r2_a_b = _matmul(dz_big, w_in_big_t, mode="nn", out_dtype=F32, name="mm_dh1_big", tm=s, tn=1024, tk=1024,
                          comm=_exchange_comm(s1_mid[3:]))
    half_mid = add_chips(mid, s1_mid, list(r2_up) + list(r2_oa) + list(r2_a_b))
    dh1 = _matmul(dz_lat, w_in_lat_t, mode="nn", out_dtype=F32, name="mm_dh1_lat", tm=s, tk=1024, add=dh1)
    gw_big_t, hosted = _matmul(dz_big, h1, mode="tn", out_dtype=BF16, name="mm_gw_in_big", tn=2048, tk=s,
                               comm=_join_comms([_share_comm(half_down + half_mid), _gather8_comm(flat)]))
    shared, small_all = hosted[:-1], lax.dynamic_update_slice(hosted[-1], flat[None], (dev, 0, 0))
    small_sum = _sum_leading(small_all, "sum_small", after=small_state + [loss.reshape(1, 1)]).reshape(-1)
    small_grads, off = {}, 0
    for n in SMALL_PARTIAL:
        shape = (CONV_TAPS, 2 * ff) if n == "conv_w" else given[n].shape
        small_grads[n] = small_sum[off:off + partial[n].size].reshape(shape)
        off += partial[n].size
    small_grads["conv_w"] = lax.dynamic_slice(small_grads["conv_w"], (0, chip * conv_w.shape[1]), conv_w.shape)
    grads = dict(zip(["w_down"] + mid, shared), **small_grads)
    gw_lat_t = _matmul(dz_lat, h1, mode="tn", out_dtype=F32, name="mm_gw_in_lat", tm=1024, tn=1024, tk=s)

    gq = gw_q.reshape(ql, heads, HEAD_W)
    gq_pe = gq[:, :, NOPE:NOPE + ROPE] + _quarter_turn_back(gq[:, :, NOPE + ROPE:])
    g_pe_t = gw_lat_t[ql + kvl:ql + kvl + ROPE] + _quarter_turn_back(gw_lat_t[ql + kvl + ROPE:].T).T
    last = ["w_in", "w_uq", "w_ukv"]
    gw_in_t = _stack_rows([gw_big_t[:o_q], gw_lat_t[:ql + kvl].astype(BF16), g_pe_t.astype(BF16), gw_big_t[o_q:]])
    gp_last = [
        gw_in_t.reshape(N_CHIPS, gw_in_t.shape[0] // N_CHIPS, d),
        _split_cols(jnp.concatenate([gq[:, :, :NOPE], gq_pe], axis=2).reshape(ql, heads * (NOPE + ROPE)).astype(BF16)),
        _split_cols(gw_kv.reshape(kvl, 2, heads, NOPE).transpose(0, 2, 1, 3).reshape(kvl, heads * 2 * NOPE)),
    ]
    (grad_x, dshift1, dscale1, g_pre1_grad), r1_last = _prenorm_bwd(x2d, dh1, dx1, g_pre1, scale1, "prenorm1_bwd",
                                                                    comm=_swap_comm(gp_last, by_cols=[0]))
    s1_last = add_sibling(last, gp_last, r1_last)

    dmod = jnp.concatenate([dshift1, dscale1, dgate1, dshift2, dscale2, dgate2, g_pre1_grad], axis=1)
    dmod_all = _all_gather(jnp.pad(dmod, ((0, SUBLANES - 1), (0, 0))), "gather_dmod")
    dmod_all = dmod_all.reshape(N_DEV, SUBLANES, (N_MOD + 1) * d)[:, 0]
    dmod_sum = _sum_leading(dmod_all.reshape(N_DEV, 1, (N_MOD + 1) * d), "sum_dmod")[0]
    grads["b_ada"], grads["pre_norm1_g"] = dmod_sum[:N_MOD * d], dmod_sum[N_MOD * d:]
    dmod_mine = lax.dynamic_slice(dmod_all, (0, chip * na), (N_DEV, na))
    grads["w_ada"] = _ada_bwd(c_all.T, dmod_mine, "ada_bwd")

    delta, new_m, new_v = {}, {}, {}

    def adamw(n, after=None):
        turn = (lambda a: a.T) if n == "w_in" else (lambda a: a)
        outs = _adamw(turn(given[n]), grads[n], turn(given["m_" + n]), turn(given["v_" + n]), "adamw_" + n,
                      after=after)
        grads[n] = turn(grads[n])
        delta[n], new_m[n], new_v[n] = (turn(o) for o in outs)

    exchange_last = _exchange_comm(s1_last)
    in_flight, token = _comm_split_start(exchange_last, "rs_exchange_last_start", after=[dmod_sum, small_sum])
    for n in ["w_ada", "w_down"] + mid:
        adamw(n, after=token)
    s1_last, r2_last = _comm_split_wait(exchange_last, in_flight, delta[mid[-1]], "rs_exchange_last_wait")
    half_last = add_chips(last, s1_last, r2_last)
    grads.update(zip(last, _run_comm(_share_comm(half_last, by_cols=[0]), "rs_share_last")))
    for n in last:
        adamw(n)

    outs = _adamw(small_state[0], small_pack("", grads), small_state[1], small_state[2], "adamw_small")
    off = 0
    for n in SMALL:
        size = given[n].size
        for store, packed_out in zip((delta, new_m, new_v), outs):
            store[n] = packed_out.reshape(-1)[off:off + size].reshape(given[n].shape)
        off += size

    return (loss, grad_x[None], *[grads[n] for n in WEIGHTS], *[delta[n] for n in WEIGHTS],
            *[new_m[n] for n in WEIGHTS], *[new_v[n] for n in WEIGHTS])
```

```python
import functools

import jax
import jax.numpy as jnp
from jax import lax
from jax.experimental import pallas as pl
from jax.experimental.pallas import tpu as pltpu

F32 = jnp.float32
BF16 = jnp.bfloat16
MESH = pl.DeviceIdType.MESH
HBM = pltpu.HBM

EPS = 1e-6
NOPE, ROPE, VHEAD = 128, 64, 128
HEAD_W = NOPE + 2 * ROPE
ROPE_THETA = 10000.0
CONV_TAPS = 3
N_MOD = 6
N_CHIPS, N_CORES, N_DEV = 4, 2, 8
ADAM_LR, ADAM_B1, ADAM_B2, ADAM_EPS, ADAM_WD, ADAM_STEP = 0.001, 0.9, 0.999, 1e-08, 0.01, 10

LANES = 128
SUBLANES = 8
VMEM_LIMIT = 56 * 2**20
MIDDLE_STAGE_AT = 70
SMALL_ROW_TILE = 256

BIG = ("w_in", "w_branch_a", "w_uq", "w_ukv", "w_branch_b", "w_out", "w_up", "w_down")
WEIGHTS = ("w_ada", "b_ada", "pre_norm1_g", "w_in", "gm_ln_g", "gm_ln_b", "gm_w_s", "gm_b_s", "w_branch_a",
           "q_norm_g", "w_uq", "kv_norm_g", "w_ukv", "w_branch_b", "w_out", "post_norm1_g", "pre_norm2_g",
           "w_up", "conv_w", "conv_b", "w_down", "post_norm2_g")
SMALL_PARTIAL = ("gm_ln_g", "gm_ln_b", "gm_w_s", "gm_b_s", "q_norm_g", "kv_norm_g", "post_norm1_g",
                 "pre_norm2_g", "conv_w", "conv_b", "post_norm2_g")
SMALL = ("b_ada", "pre_norm1_g") + SMALL_PARTIAL


def _div_tile(n, cap, mult=LANES):
    t = (min(cap, n) // mult) * mult
    while t >= mult:
        if n % t == 0:
            return t
        t -= mult
    return n


def _params(**kw):
    return pltpu.CompilerParams(vmem_limit_bytes=VMEM_LIMIT, **kw)


def _row_spec(width):
    return pl.BlockSpec((1, width), lambda *_: (0, 0))


def _gelu(x):
    k = 0.7978845608028654
    return 0.5 * x * (1.0 + jnp.tanh(k * (x + 0.044715 * x * x * x)))


def _gelu_grad(x):
    k = 0.7978845608028654
    t = jnp.tanh(k * (x + 0.044715 * x * x * x))
    return 0.5 * (1.0 + t) + 0.5 * x * (1.0 - t * t) * k * (1.0 + 3.0 * 0.044715 * x * x)


def _sigmoid(x):
    return 0.5 * jnp.tanh(0.5 * x) + 0.5


def _dot(a, b, dims):
    return lax.dot_general(a, b, (dims, ((), ())), preferred_element_type=F32)


NN = ((1,), (0,))
NT = ((1,), (1,))
TN = ((0,), (0,))


def _logical(arr):
    if arr.ndim == 2:
        return arr.shape[0], arr.shape[1], arr.shape[1]
    return arr.shape[1], arr.shape[0] * arr.shape[2], arr.shape[2]


def _tile_spec(ndim, group_w, blk_rows, blk_cols, row_of, col_of):
    if ndim == 2:
        return pl.BlockSpec((blk_rows, blk_cols), lambda i, j, k: (row_of(i, j, k), col_of(i, j, k)))
    per = group_w // blk_cols
    return pl.BlockSpec((None, blk_rows, blk_cols),
                        lambda i, j, k: (col_of(i, j, k) // per, row_of(i, j, k), col_of(i, j, k) % per))


def _matmul(a, b, *, mode, out_dtype, name, tm=512, tn=512, tk=2048, mul=None, add=None, out_groups=None, comm=None):
    ar, ac, agw = _logical(a)
    br, bc, bgw = _logical(b)
    if mode == "nn":
        m, kd, n = ar, ac, bc
        m_w, k_w, n_w = (), (agw,), (bgw,)
    elif mode == "nt":
        m, kd, n = ar, ac, br
        m_w, k_w, n_w = (), (agw, bgw), ()
    else:
        m, kd, n = ac, ar, bc
        m_w, k_w, n_w = (agw,), (), (bgw,)
    if out_groups is not None:
        n_w = n_w + (n // out_groups,)
    tm = _div_tile(min((m,) + m_w), tm, LANES if mode == "tn" else SUBLANES)
    tn = _div_tile(min((n,) + n_w), tn)
    tk = _div_tile(min((kd,) + k_w), tk)
    assert all(w % tn == 0 for w in n_w) and all(w % tk == 0 for w in k_w) and all(w % tm == 0 for w in m_w)
    nk = kd // tk
    dims = {"nn": NN, "nt": NT, "tn": TN}[mode]
    gi, gj, gk = (lambda i, j, k: i), (lambda i, j, k: j), (lambda i, j, k: k)
    if mode == "nn":
        a_spec = _tile_spec(a.ndim, agw, tm, tk, gi, gk)
        b_spec = _tile_spec(b.ndim, bgw, tk, tn, gk, gj)
    elif mode == "nt":
        a_spec = _tile_spec(a.ndim, agw, tm, tk, gi, gk)
        b_spec = _tile_spec(b.ndim, bgw, tn, tk, gj, gk)
    else:
        a_spec = _tile_spec(a.ndim, agw, tk, tm, gk, gi)
        b_spec = _tile_spec(b.ndim, bgw, tk, tn, gk, gj)
    in_specs, operands = [a_spec, b_spec], [a, b]
    if mul is not None:
        assert mul.shape == (m, tn)
        in_specs.append(pl.BlockSpec((tm, tn), lambda i, j, k: (i, 0)))
        operands.append(mul)
    if add is not None:
        in_specs.append(pl.BlockSpec((tm, tn), lambda i, j, k: (i, j)))
        operands.append(add)

    def body(*refs):
        a_ref, b_ref = refs[0], refs[1]
        pos = 2
        mul_ref = add_ref = None
        if mul is not None:
            mul_ref, pos = refs[pos], pos + 1
        if add is not None:
            add_ref, pos = refs[pos], pos + 1
        o_ref = refs[pos]

        def finish(r):
            if mul_ref is not None:
                r = r * mul_ref[...]
            if add_ref is not None:
                r = r + add_ref[...]
            o_ref[...] = r.astype(out_dtype)

        part = _dot(a_ref[...], b_ref[...], dims)
        if nk == 1:
            finish(part)
        else:
            acc_ref = refs[pos + 1]
            k = pl.program_id(2)

            @pl.when(k == 0)
            def _():
                acc_ref[...] = part

            @pl.when(k > 0)
            def _():
                acc_ref[...] += part

            @pl.when(k == nk - 1)
            def _():
                finish(acc_ref[...])

    if out_groups is None:
        out_spec, out_dims = _tile_spec(2, n, tm, tn, gi, gj), (m, n)
    else:
        out_spec, out_dims = _tile_spec(3, n // out_groups, tm, tn, gi, gj), (out_groups, m, n // out_groups)
    return _call(body, operands, comm, name=name, grid=(m // tm, n // tn, nk), in_specs=in_specs, out_specs=out_spec,
                 out_shape=jax.ShapeDtypeStruct(out_dims, out_dtype),
                 scratch_shapes=[] if nk == 1 else [pltpu.VMEM((tm, tn), F32)])


def _accumulate(ref, value):
    @pl.when(pl.program_id(0) == 0)
    def _():
        ref[...] = value

    @pl.when(pl.program_id(0) > 0)
    def _():
        ref[...] += value


def _colsum(v):
    return jnp.sum(v, axis=0, keepdims=True)


def _rowmean(v):
    return jnp.mean(v, axis=-1, keepdims=True)


def _prenorm(x, g, scale, shift, name):
    s, d = x.shape
    tb = _div_tile(s, 256, SUBLANES)

    def body(x_ref, g_ref, sc_ref, sh_ref, h_ref):
        xv = x_ref[...]
        r = lax.rsqrt(_rowmean(xv * xv) + EPS)
        h_ref[...] = ((xv * r) * g_ref[...] * (1.0 + sc_ref[...]) + sh_ref[...]).astype(BF16)

    blk = pl.BlockSpec((tb, d), lambda i: (i, 0))
    return pl.pallas_call(
        body, name=name, grid=(s // tb,), in_specs=[blk, _row_spec(d), _row_spec(d), _row_spec(d)],
        out_specs=blk, out_shape=jax.ShapeDtypeStruct((s, d), BF16), compiler_params=_params(),
    )(x, g, scale, shift)


def _post_pre(x, y, gate, pg, g2, scale2, shift2, name):
    s, d = x.shape
    tb = _div_tile(s, 256, SUBLANES)

    def body(x_ref, y_ref, gate_ref, pg_ref, g2_ref, sc_ref, sh_ref, x1_ref, h2_ref):
        yv = y_ref[...]
        rp = lax.rsqrt(_rowmean(yv * yv) + EPS)
        x1 = x_ref[...] + gate_ref[...] * ((yv * rp) * pg_ref[...])
        x1_ref[...] = x1
        r2 = lax.rsqrt(_rowmean(x1 * x1) + EPS)
        h2_ref[...] = ((x1 * r2) * g2_ref[...] * (1.0 + sc_ref[...]) + sh_ref[...]).astype(BF16)

    blk = pl.BlockSpec((tb, d), lambda i: (i, 0))
    return pl.pallas_call(
        body, name=name, grid=(s // tb,), in_specs=[blk, blk] + [_row_spec(d)] * 5,
        out_specs=[blk, blk],
        out_shape=[jax.ShapeDtypeStruct((s, d), F32), jax.ShapeDtypeStruct((s, d), BF16)],
        compiler_params=_params(),
    )(x, y, gate, pg, g2, scale2, shift2)


def _post_bwd(y, gate, pg, name, *, dxo=None, xin=None, target=None):
    s, d = y.shape
    tb = _div_tile(s, 256, SUBLANES)
    from_loss = target is not None

    def body(*refs):
        if from_loss:
            y_ref, gate_ref, pg_ref, xin_ref, t_ref, dy_ref, dgate_ref, dpg_ref, dxo_ref, loss_ref = refs
        else:
            y_ref, gate_ref, pg_ref, dxo_in_ref, dy_ref, dgate_ref, dpg_ref = refs
        yv = y_ref[...]
        rp = lax.rsqrt(_rowmean(yv * yv) + EPS)
        yh = yv * rp
        fn = yh * pg_ref[...]
        gate = gate_ref[...]
        if from_loss:
            err = xin_ref[...] + gate * fn - t_ref[...]
            dxo = err * (1.0 / d)
            dxo_ref[...] = dxo
            part = 0.5 * jnp.sum(_rowmean(err * err), axis=0, keepdims=True)
            _accumulate(loss_ref, jnp.broadcast_to(part, loss_ref.shape))
        else:
            dxo = dxo_in_ref[...]
        _accumulate(dgate_ref, _colsum(dxo * fn))
        dfn = dxo * gate
        _accumulate(dpg_ref, _colsum(dfn * yh))
        dyh = dfn * pg_ref[...]
        dy_ref[...] = (rp * (dyh - yh * _rowmean(dyh * yh))).astype(BF16)

    blk = pl.BlockSpec((tb, d), lambda i: (i, 0))
    in_specs = [blk, _row_spec(d), _row_spec(d)]
    out_specs = [blk, _row_spec(d), _row_spec(d)]
    out_shape = [jax.ShapeDtypeStruct((s, d), BF16), jax.ShapeDtypeStruct((1, d), F32),
                 jax.ShapeDtypeStruct((1, d), F32)]
    if from_loss:
        operands = (y, gate, pg, xin, target)
        in_specs += [blk, blk]
        out_specs += [blk, _row_spec(LANES)]
        out_shape += [jax.ShapeDtypeStruct((s, d), F32), jax.ShapeDtypeStruct((1, LANES), F32)]
    else:
        operands = (y, gate, pg, dxo)
        in_specs += [blk]
    return pl.pallas_call(
        body, name=name, grid=(s // tb,), in_specs=in_specs, out_specs=out_specs, out_shape=out_shape,
        compiler_params=_params(),
    )(*operands)


def _prenorm_bwd(xin, dh, dres, g, scale, name, comm=None):
    s, d = xin.shape
    tb = _div_tile(s, 256, SUBLANES)

    def body(x_ref, dh_ref, dres_ref, g_ref, sc_ref, dx_ref, dshift_ref, dscale_ref, dg_ref):
        xv = x_ref[...]
        r = lax.rsqrt(_rowmean(xv * xv) + EPS)
        xn = xv * r
        dh = dh_ref[...]
        g1 = g_ref[...]
        s1 = 1.0 + sc_ref[...]
        _accumulate(dshift_ref, _colsum(dh))
        _accumulate(dscale_ref, _colsum(dh * xn * g1))
        _accumulate(dg_ref, _colsum(dh * xn * s1))
        dxn = dh * g1 * s1
        dx_ref[...] = dres_ref[...] + r * (dxn - xn * _rowmean(dxn * xn))

    blk = pl.BlockSpec((tb, d), lambda i: (i, 0))
    return _call(
        body, (xin, dh, dres, g, scale), comm, name=name, grid=(s // tb,),
        in_specs=[blk, blk, blk, _row_spec(d), _row_spec(d)],
        out_specs=[blk, _row_spec(d), _row_spec(d), _row_spec(d)],
        out_shape=[jax.ShapeDtypeStruct((s, d), F32)] + [jax.ShapeDtypeStruct((1, d), F32)] * 3)


def _merge(z_big, y_a, y_b, name):
    s, d = y_a.shape
    tb = _div_tile(s, 256, SUBLANES)

    def body(zg_ref, ya_ref, yb_ref, o_ref):
        ga, gb = zg_ref[:, :d].astype(F32), zg_ref[:, d:].astype(F32)
        o_ref[...] = (_sigmoid(ga) * ya_ref[...].astype(F32) + _sigmoid(gb) * yb_ref[...].astype(F32)).astype(BF16)

    blk = pl.BlockSpec((tb, d), lambda i: (i, 0))
    return pl.pallas_call(
        body, name=name, grid=(s // tb,), in_specs=[pl.BlockSpec((tb, 2 * d), lambda i: (i, 1)), blk, blk],
        out_specs=blk, out_shape=jax.ShapeDtypeStruct((s, d), BF16), compiler_params=_params(),
    )(z_big, y_a, y_b)


def _merge_bwd(dmerged, z_big, y_a, y_b, name):
    s, d = y_a.shape
    tb = _div_tile(s, 256, SUBLANES)

    def body(dm_ref, zg_ref, ya_ref, yb_ref, dya_ref, dyb_ref, dz_ref):
        dm = dm_ref[...].astype(F32)
        sa, sb = _sigmoid(zg_ref[:, :d].astype(F32)), _sigmoid(zg_ref[:, d:].astype(F32))
        dya_ref[...] = (dm * sa).astype(BF16)
        dyb_ref[...] = (dm * sb).astype(BF16)
        dz_ref[:, :d] = (dm * ya_ref[...].astype(F32) * sa * (1.0 - sa)).astype(BF16)
        dz_ref[:, d:] = (dm * yb_ref[...].astype(F32) * sb * (1.0 - sb)).astype(BF16)

    blk = pl.BlockSpec((tb, d), lambda i: (i, 0))
    wide = pl.BlockSpec((tb, 2 * d), lambda i: (i, 1))
    return pl.pallas_call(
        body, name=name, grid=(s // tb,), in_specs=[blk, wide, blk, blk], out_specs=[blk, blk, wide],
        out_shape=[jax.ShapeDtypeStruct((s, d), BF16), jax.ShapeDtypeStruct((s, d), BF16),
                   jax.ShapeDtypeStruct((s, 4 * d), BF16)],
        compiler_params=_params(),
    )(dmerged, z_big, y_a, y_b)


def _causal_mask(ch):
    q = lax.broadcasted_iota(jnp.int32, (ch, ch), 0)
    p = lax.broadcasted_iota(jnp.int32, (ch, ch), 1)
    return (p <= q).astype(F32)


def _gmlp_norm(zc, lng, lnb, gw):
    u_pre, v_pre = zc[:, :gw], zc[:, gw:]
    vg = _gelu(v_pre)
    mu = _rowmean(vg)
    cen = vg - mu
    rstd = lax.rsqrt(_rowmean(cen * cen) + EPS)
    vhat = cen * rstd
    return u_pre, v_pre, _gelu(u_pre), vhat, rstd, vhat * lng + lnb


def _gmlp_fwd(z_big, ln_g, ln_b, w_s, b_s_t, name):
    s = z_big.shape[0]
    groups, ch, _ = w_s.shape
    gw = ln_g.shape[1]
    gd = gw // groups

    def body(z_ref, lng_ref, lnb_ref, ws_ref, bt_ref, a_ref):
        _, _, u, _, _, vn = _gmlp_norm(z_ref[...].astype(F32), lng_ref[...], lnb_ref[...], gw)
        mask = _causal_mask(ch)
        for g in range(groups):
            cols = slice(g * gd, (g + 1) * gd)
            wm = (ws_ref[g] * mask).astype(BF16)
            mixed = _dot(wm, vn[:, cols].astype(BF16), NN) + bt_ref[:, g:g + 1]
            a_ref[:, cols] = (u[:, cols] * mixed).astype(BF16)

    return pl.pallas_call(
        body, name=name, grid=(s // ch,),
        in_specs=[pl.BlockSpec((ch, 2 * gw), lambda n: (n, 0)), _row_spec(gw), _row_spec(gw),
                  pl.BlockSpec((groups, ch, ch), lambda n: (0, 0, 0)), pl.BlockSpec((ch, groups), lambda n: (0, 0))],
        out_specs=pl.BlockSpec((ch, gw), lambda n: (n, 0)),
        out_shape=jax.ShapeDtypeStruct((s, gw), BF16), compiler_params=_params(),
    )(z_big, ln_g, ln_b, w_s, b_s_t)


def _gmlp_bwd(z_big, da, dz_big, ln_g, ln_b, w_s, b_s_t, name, comm=None):
    s = z_big.shape[0]
    groups, ch, _ = w_s.shape
    gw = ln_g.shape[1]
    gd = gw // groups

    def body(z_ref, da_ref, dzin_ref, lng_ref, lnb_ref, ws_ref, bt_ref, dz_ref, gws_ref, gbt_ref, glng_ref, glnb_ref,
             vg_ref, dvh_ref):
        del dzin_ref
        mask = _causal_mask(ch)
        lane = lax.broadcasted_iota(jnp.int32, (ch, LANES), 1)
        group_cols = [slice(g * gd, (g + 1) * gd) for g in range(groups)]
        rowsum = lambda v: jnp.sum(v, axis=1, keepdims=True)

        @pl.when(pl.program_id(0) == 0)
        def _():
            for ref in (gws_ref, gbt_ref, glng_ref, glnb_ref):
                ref[...] = jnp.zeros(ref.shape, F32)

        total = jnp.zeros((ch, 1), F32)
        for cols in group_cols:
            vg = _gelu(z_ref[:, gw + cols.start:gw + cols.stop].astype(F32))
            vg_ref[:, cols] = vg
            total = total + rowsum(vg)
        mu = total * (1.0 / gw)
        total = jnp.zeros((ch, 1), F32)
        for cols in group_cols:
            cen = vg_ref[:, cols] - mu
            total = total + rowsum(cen * cen)
        rstd = lax.rsqrt(total * (1.0 / gw) + EPS)
        m1, m2, gb = jnp.zeros((ch, 1), F32), jnp.zeros((ch, 1), F32), jnp.zeros((ch, LANES), F32)
        for g, cols in enumerate(group_cols):
            vhat = (vg_ref[:, cols] - mu) * rstd
            vn_g = (vhat * lng_ref[:, cols] + lnb_ref[:, cols]).astype(BF16)
            wm = (ws_ref[g] * mask).astype(BF16)
            mixed = _dot(wm, vn_g, NN) + bt_ref[:, g:g + 1]
            u_pre, da_g = z_ref[:, cols].astype(F32), da_ref[:, cols].astype(F32)
            dz_ref[:, cols] = (da_g * mixed * _gelu_grad(u_pre)).astype(BF16)
            dmixed = da_g * _gelu(u_pre)
            dm16 = dmixed.astype(BF16)
            dvn = _dot(wm, dm16, TN)
            gws_ref[g] += _dot(dm16, vn_g, NT) * mask
            gb = gb + jnp.where(lane == g, rowsum(dmixed), 0.0)
            glnb_ref[:, cols] += _colsum(dvn)
            glng_ref[:, cols] += _colsum(dvn * vhat)
            dvh = dvn * lng_ref[:, cols]
            dvh_ref[:, cols] = dvh
            m1, m2 = m1 + rowsum(dvh), m2 + rowsum(dvh * vhat)
        gbt_ref[...] += gb
        m1, m2 = m1 * (1.0 / gw), m2 * (1.0 / gw)
        for cols in group_cols:
            vhat = (vg_ref[:, cols] - mu) * rstd
            dvg = rstd * (dvh_ref[:, cols] - m1 - vhat * m2)
            v_pre = z_ref[:, gw + cols.start:gw + cols.stop].astype(F32)
            dz_ref[:, gw + cols.start:gw + cols.stop] = (dvg * _gelu_grad(v_pre)).astype(BF16)

    zspec = pl.BlockSpec((ch, 2 * gw), lambda n: (n, 0))
    return _call(
        body, (z_big, da, dz_big, ln_g, ln_b, w_s, b_s_t), comm, name=name, grid=(s // ch,),
        in_specs=[zspec, pl.BlockSpec((ch, gw), lambda n: (n, 0)), pl.BlockSpec(memory_space=HBM),
                  _row_spec(gw), _row_spec(gw), pl.BlockSpec((groups, ch, ch), lambda n: (0, 0, 0)),
                  pl.BlockSpec((ch, groups), lambda n: (0, 0))],
        out_specs=[zspec, pl.BlockSpec((groups, ch, ch), lambda n: (0, 0, 0)),
                   pl.BlockSpec((ch, LANES), lambda n: (0, 0)), _row_spec(gw), _row_spec(gw)],
        out_shape=[jax.ShapeDtypeStruct(dz_big.shape, BF16), jax.ShapeDtypeStruct((groups, ch, ch), F32),
                   jax.ShapeDtypeStruct((ch, LANES), F32), jax.ShapeDtypeStruct((1, gw), F32),
                   jax.ShapeDtypeStruct((1, gw), F32)],
        scratch_shapes=[pltpu.VMEM((ch, gw), F32)] * 2, input_output_aliases={2: 0})


def _mla_prep(z_lat, q_g, kv_g, rope_k, name):
    s, latw = z_lat.shape
    ql, kvl = q_g.shape[1], kv_g.shape[1]
    tb = _div_tile(s, 256, SUBLANES)

    def body(z_ref, qg_ref, kvg_ref, t_ref, qn_ref, kvn_ref, kr_ref):
        q = z_ref[:, :ql]
        qn_ref[...] = ((q * lax.rsqrt(_rowmean(q * q) + EPS)) * qg_ref[...]).astype(BF16)
        kv = z_ref[:, ql:ql + kvl]
        kvn_ref[...] = ((kv * lax.rsqrt(_rowmean(kv * kv) + EPS)) * kvg_ref[...]).astype(BF16)
        kk = z_ref[:, ql + kvl:] * t_ref[...]
        kr_ref[...] = (kk + pltpu.roll(kk, ROPE, axis=1)).astype(BF16)

    return pl.pallas_call(
        body, name=name, grid=(s // tb,),
        in_specs=[pl.BlockSpec((tb, latw), lambda i: (i, 0)), _row_spec(ql), _row_spec(kvl),
                  pl.BlockSpec((tb, 2 * ROPE), lambda i: (i, 0))],
        out_specs=[pl.BlockSpec((tb, ql), lambda i: (i, 0)), pl.BlockSpec((tb, kvl), lambda i: (i, 0)),
                   pl.BlockSpec((tb, 2 * ROPE), lambda i: (i, 0))],
        out_shape=[jax.ShapeDtypeStruct((s, ql), BF16), jax.ShapeDtypeStruct((s, kvl), BF16),
                   jax.ShapeDtypeStruct((s, 2 * ROPE), BF16)],
        compiler_params=_params(),
    )(z_lat, q_g, kv_g, rope_k)


def _attn_fwd(q, kv, kr, heads, name, comm=None):
    s = q.shape[0]
    t = _div_tile(s, 512)
    nb = s // t
    hp = 2 if heads % 2 == 0 else 1

    def body(q_ref, k_ref, kr_ref, v_ref, o_ref, lse_ref, m_ref, l_ref, acc_ref):
        i, j = pl.program_id(1), pl.program_id(2)

        @pl.when(j == 0)
        def _():
            m_ref[...] = jnp.full(m_ref.shape, -1e30, F32)
            l_ref[...] = jnp.zeros(l_ref.shape, F32)
            acc_ref[...] = jnp.zeros(acc_ref.shape, F32)

        def update(h, rows, n_keys, on_diagonal):
            vc = slice(h * VHEAD, (h + 1) * VHEAD)
            k_full = jnp.concatenate([k_ref[:n_keys, h * NOPE:(h + 1) * NOPE], kr_ref[:n_keys, :]], axis=1)
            sc = _dot(q_ref[rows, h * HEAD_W:(h + 1) * HEAD_W], k_full, NT)
            if on_diagonal:
                row_pos = rows.start + lax.broadcasted_iota(jnp.int32, sc.shape, 0)
                sc = jnp.where(lax.broadcasted_iota(jnp.int32, sc.shape, 1) <= row_pos, sc, -1e30)
            m_old = m_ref[h, rows, :]
            m_new = jnp.maximum(m_old, jnp.max(sc, axis=-1, keepdims=True))
            p = jnp.exp(sc - m_new)
            alpha = jnp.exp(m_old - m_new)
            l_new = alpha * l_ref[h, rows, :] + jnp.sum(p, axis=-1, keepdims=True)
            acc = alpha * acc_ref[rows, vc] + _dot(p.astype(BF16), v_ref[:n_keys, vc], NN)
            if on_diagonal:
                o_ref[rows, vc] = (acc / l_new).astype(BF16)
                lse_ref[h, rows, :] = jnp.broadcast_to(m_new + jnp.log(l_new), (rows.stop - rows.start, LANES))
            else:
                m_ref[h, rows, :], l_ref[h, rows, :], acc_ref[rows, vc] = m_new, l_new, acc

        def below_diagonal():
            for h in range(hp):
                update(h, slice(0, t), t, False)

        def on_diagonal():
            for h in range(hp):
                update(h, slice(0, t // 2), t // 2, True)
                update(h, slice(t // 2, t), t, True)

        pl.when(j < i)(below_diagonal)
        pl.when(j == i)(on_diagonal)

    kidx = lambda off: (lambda h, i, j: (jnp.minimum(i, j), off(h)))
    return _call(
        body, (q, kv, kr, kv), comm, name=name, grid=(heads // hp, nb, nb),
        in_specs=[pl.BlockSpec((t, hp * HEAD_W), lambda h, i, j: (i, h)),
                  pl.BlockSpec((t, hp * NOPE), kidx(lambda h: h)),
                  pl.BlockSpec((t, 2 * ROPE), kidx(lambda h: 0)),
                  pl.BlockSpec((t, hp * VHEAD), kidx(lambda h: heads // hp + h))],
        out_specs=[pl.BlockSpec((t, hp * VHEAD), lambda h, i, j: (i, h)),
                   pl.BlockSpec((hp, t, LANES), lambda h, i, j: (h, i, 0))],
        out_shape=[jax.ShapeDtypeStruct((s, heads * VHEAD), BF16), jax.ShapeDtypeStruct((heads, s, LANES), F32)],
        scratch_shapes=[pltpu.VMEM((hp, t, 1), F32), pltpu.VMEM((hp, t, 1), F32), pltpu.VMEM((t, hp * VHEAD), F32)])


def _attn_bwd(q, kv, kr, o, do, lse, heads, name, comm=None):
    s = q.shape[0]
    t = _div_tile(s, 512)
    nb = s // t
    hp = 2 if heads % 2 == 0 else 1

    def body(q_ref, k_ref, kr_ref, v_ref, o_ref, do_ref, lse_ref, dq_ref, dk_ref, dv_ref, dk_acc, dv_acc):
        j, i = pl.program_id(1), pl.program_id(2)

        @pl.when(jnp.logical_and(j == 0, i == 0))
        def _():
            dq_ref[...] = jnp.zeros(dq_ref.shape, F32)

        def update(h, rows, n_keys, on_diagonal, assign):
            qc, kc, vc = (slice(h * w, (h + 1) * w) for w in (HEAD_W, NOPE, VHEAD))
            n_rows = rows.stop - rows.start
            qv, do_v = q_ref[rows, qc], do_ref[rows, vc]
            k_full = jnp.concatenate([k_ref[:n_keys, kc], kr_ref[:n_keys, :]], axis=1)
            sc = _dot(qv, k_full, NT)
            if on_diagonal:
                row_pos = rows.start + lax.broadcasted_iota(jnp.int32, sc.shape, 0)
                sc = jnp.where(lax.broadcasted_iota(jnp.int32, sc.shape, 1) <= row_pos, sc, -1e30)
            p = jnp.exp(sc - lse_ref[h, rows, :1])
            dp = _dot(do_v, v_ref[:n_keys, vc], NT)
            delta = jnp.sum(do_v.astype(F32) * o_ref[rows, vc].astype(F32), axis=-1, keepdims=True)
            ds = (p * (dp - delta)).astype(BF16)
            dq_ref[pl.ds(pl.multiple_of(i * t + rows.start, n_rows), n_rows), qc] += _dot(ds, k_full, NN)
            dv_part, dk_part = _dot(p.astype(BF16), do_v, TN), _dot(ds, qv, TN)
            if assign:
                dv_acc[:n_keys, vc], dk_acc[:n_keys, qc] = dv_part, dk_part
            else:
                dv_acc[:n_keys, vc] += dv_part
                dk_acc[:n_keys, qc] += dk_part

        def on_diagonal():
            for h in range(hp):
                update(h, slice(t // 2, t), t, True, True)
                update(h, slice(0, t // 2), t // 2, True, False)

        def below_diagonal():
            for h in range(hp):
                update(h, slice(0, t), t, False, False)

        pl.when(i == j)(on_diagonal)
        pl.when(i > j)(below_diagonal)

        @pl.when(i == nb - 1)
        def _():
            dk_ref[...] = dk_acc[...].astype(BF16)
            dv_ref[...] = dv_acc[...].astype(BF16)

    qidx = lambda h, j, i: (jnp.maximum(i, j), h)
    return _call(
        body, (q, kv, kr, kv, o, do, lse), comm, name=name, grid=(heads // hp, nb, nb),
        in_specs=[pl.BlockSpec((t, hp * HEAD_W), qidx),
                  pl.BlockSpec((t, hp * NOPE), lambda h, j, i: (j, h)),
                  pl.BlockSpec((t, 2 * ROPE), lambda h, j, i: (j, 0)),
                  pl.BlockSpec((t, hp * VHEAD), lambda h, j, i: (j, heads // hp + h)),
                  pl.BlockSpec((t, hp * VHEAD), qidx), pl.BlockSpec((t, hp * VHEAD), qidx),
                  pl.BlockSpec((hp, t, LANES), lambda h, j, i: (h, jnp.maximum(i, j), 0))],
        out_specs=[pl.BlockSpec((s, hp * HEAD_W), lambda h, j, i: (0, h)),
                   pl.BlockSpec((t, hp * HEAD_W), lambda h, j, i: (j, h)),
                   pl.BlockSpec((t, hp * VHEAD), lambda h, j, i: (j, h))],
        out_shape=[jax.ShapeDtypeStruct((s, heads * HEAD_W), F32), jax.ShapeDtypeStruct((s, heads * HEAD_W), BF16),
                   jax.ShapeDtypeStruct((s, heads * VHEAD), BF16)],
        scratch_shapes=[pltpu.VMEM((t, hp * HEAD_W), F32), pltpu.VMEM((t, hp * VHEAD), F32)])


def _mla_bwd_mid(dq, dk, dv, rope_q, rope_k, heads, name):
    s = dq.shape[0]
    tb = _div_tile(s, 256, SUBLANES)

    def body(dq_ref, dk_ref, dv_ref, tq_ref, tk_ref, dqb_ref, dkv_ref, dkk_ref):
        tq = tq_ref[...]
        dkr = jnp.zeros((tb, 2 * ROPE), F32)
        for h in range(heads):
            cols = slice(h * HEAD_W, (h + 1) * HEAD_W)
            dqb_ref[:, cols] = (dq_ref[:, cols] * tq).astype(BF16)
            dkv_ref[:, h * NOPE:(h + 1) * NOPE] = dk_ref[:, h * HEAD_W:h * HEAD_W + NOPE]
            dkr = dkr + dk_ref[:, h * HEAD_W + NOPE:(h + 1) * HEAD_W].astype(F32)
        dkv_ref[:, heads * NOPE:] = dv_ref[...]
        dkk_ref[...] = (dkr + pltpu.roll(dkr, ROPE, axis=1)) * tk_ref[...]

    wq, wv = heads * HEAD_W, heads * VHEAD
    return pl.pallas_call(
        body, name=name, grid=(s // tb,),
        in_specs=[pl.BlockSpec((tb, wq), lambda i: (i, 0)), pl.BlockSpec((tb, wq), lambda i: (i, 0)),
                  pl.BlockSpec((tb, wv), lambda i: (i, 0)), pl.BlockSpec((tb, HEAD_W), lambda i: (i, 0)),
                  pl.BlockSpec((tb, 2 * ROPE), lambda i: (i, 0))],
        out_specs=[pl.BlockSpec((tb, wq), lambda i: (i, 0)), pl.BlockSpec((tb, heads * NOPE + wv), lambda i: (i, 0)),
                   pl.BlockSpec((tb, 2 * ROPE), lambda i: (i, 0))],
        out_shape=[jax.ShapeDtypeStruct((s, wq), BF16), jax.ShapeDtypeStruct((s, heads * NOPE + wv), BF16),
                   jax.ShapeDtypeStruct((s, 2 * ROPE), F32)],
        compiler_params=_params(),
    )(dq, dk, dv, rope_q, rope_k)


def _mla_bwd_post(z_lat, dqn, dkvn, dkk, q_g, kv_g, name):
    s, latw = z_lat.shape
    ql, kvl = q_g.shape[1], kv_g.shape[1]
    tb = _div_tile(s, 256, SUBLANES)

    def norm_bwd(xv, dn, g, dg_ref):
        r = lax.rsqrt(_rowmean(xv * xv) + EPS)
        xh = xv * r
        _accumulate(dg_ref, _colsum(dn * xh))
        dxh = dn * g
        return r * (dxh - xh * _rowmean(dxh * xh))

    def body(z_ref, dqn_ref, dkvn_ref, dkk_ref, qg_ref, kvg_ref, dz_ref, gq_ref, gkv_ref):
        dz_ref[:, :ql] = norm_bwd(z_ref[:, :ql], dqn_ref[...], qg_ref[...], gq_ref).astype(BF16)
        dz_ref[:, ql:ql + kvl] = norm_bwd(z_ref[:, ql:ql + kvl], dkvn_ref[...], kvg_ref[...], gkv_ref).astype(BF16)
        dz_ref[:, ql + kvl:] = dkk_ref[...].astype(BF16)

    return pl.pallas_call(
        body, name=name, grid=(s // tb,),
        in_specs=[pl.BlockSpec((tb, latw), lambda i: (i, 0)), pl.BlockSpec((tb, ql), lambda i: (i, 0)),
                  pl.BlockSpec((tb, kvl), lambda i: (i, 0)), pl.BlockSpec((tb, 2 * ROPE), lambda i: (i, 0)),
                  _row_spec(ql), _row_spec(kvl)],
        out_specs=[pl.BlockSpec((tb, latw), lambda i: (i, 0)), _row_spec(ql), _row_spec(kvl)],
        out_shape=[jax.ShapeDtypeStruct((s, latw), BF16), jax.ShapeDtypeStruct((1, ql), F32),
                   jax.ShapeDtypeStruct((1, kvl), F32)],
        compiler_params=_params(),
    )(z_lat, dqn, dkvn, dkk, q_g, kv_g)


CONV_ROWS = 128
CONV_HALO = 16


def _row_steps(n_rows, step):
    step(0, True)
    if n_rows > CONV_ROWS:
        def later(i, carry):
            step(pl.multiple_of(i * CONV_ROWS, CONV_ROWS), False)
            return carry
        lax.fori_loop(1, n_rows // CONV_ROWS, later, 0)


def _conv_taps(pre_ref, r0, first):
    if first:
        win = jnp.concatenate([jnp.zeros((CONV_HALO, pre_ref.shape[1]), F32), pre_ref[0:CONV_ROWS, :].astype(F32)])
    else:
        win = pre_ref[pl.ds(pl.multiple_of(r0 - CONV_HALO, CONV_HALO), CONV_ROWS + CONV_HALO), :].astype(F32)
    return win[CONV_HALO:], pltpu.roll(win, 1, axis=0)[CONV_HALO:], pltpu.roll(win, 2, axis=0)[CONV_HALO:]


def _conv(taps, w_ref, b_ref):
    return w_ref[2:3, :] * taps[0] + w_ref[1:2, :] * taps[1] + w_ref[0:1, :] * taps[2] + b_ref[...]


def _conv_fwd(up_pre, conv_w, conv_b, name):
    s, ff2 = up_pre.shape
    ff = ff2 // 2
    tc = _div_tile(ff, 256)
    nb = ff // tc
    assert s % CONV_ROWS == 0

    def body(pg_ref, pv_ref, wg_ref, wv_ref, bg_ref, bv_ref, act_ref):
        def step(r0, first):
            gate = _conv(_conv_taps(pg_ref, r0, first), wg_ref, bg_ref)
            val = _conv(_conv_taps(pv_ref, r0, first), wv_ref, bv_ref)
            act_ref[pl.ds(r0, CONV_ROWS), :] = (gate * _sigmoid(gate) * val).astype(BF16)

        _row_steps(s, step)

    def col(rows, off):
        return pl.BlockSpec((rows, tc), lambda j: (0, j + off))

    return pl.pallas_call(
        body, name=name, grid=(nb,),
        in_specs=[col(s, 0), col(s, nb), col(CONV_TAPS, 0), col(CONV_TAPS, nb), col(1, 0), col(1, nb)],
        out_specs=col(s, 0), out_shape=jax.ShapeDtypeStruct((s, ff), BF16), compiler_params=_params(),
    )(up_pre, up_pre, conv_w, conv_w, conv_b, conv_b)


def _conv_bwd(up_pre, dact, conv_w, conv_b, name, comm=None):
    s, ff2 = up_pre.shape
    ff = ff2 // 2
    tc = _div_tile(ff, 256)
    nb = ff // tc
    assert s % CONV_ROWS == 0

    def body(pg_ref, pv_ref, da_ref, wg_ref, wv_ref, bg_ref, bv_ref, dup_ref, gwg_ref, gwv_ref, gbg_ref, gbv_ref,
             dxg_ref, dxv_ref):
        for ref in (gwg_ref, gwv_ref, gbg_ref, gbv_ref):
            ref[...] = jnp.zeros(ref.shape, F32)
        for ref in (dxg_ref, dxv_ref):
            ref[s:s + SUBLANES, :] = jnp.zeros((SUBLANES, tc), F32)

        def sums(taps, dx, gw_ref, gb_ref):
            gb_ref[...] += _colsum(dx)
            for k in range(CONV_TAPS):
                gw_ref[k:k + 1, :] += _colsum(dx * taps[CONV_TAPS - 1 - k])

        def forward(r0, first):
            rows = pl.ds(r0, CONV_ROWS)
            taps_g, taps_v = _conv_taps(pg_ref, r0, first), _conv_taps(pv_ref, r0, first)
            gate, val = _conv(taps_g, wg_ref, bg_ref), _conv(taps_v, wv_ref, bv_ref)
            da = da_ref[rows, :].astype(F32)
            sg = _sigmoid(gate)
            dxv, dxg = da * gate * sg, da * val * sg * (1.0 + gate * (1.0 - sg))
            dxv_ref[rows, :], dxg_ref[rows, :] = dxv, dxg
            sums(taps_v, dxv, gwv_ref, gbv_ref)
            sums(taps_g, dxg, gwg_ref, gbg_ref)

        def backward(r0, first):
            del first
            n = CONV_ROWS + SUBLANES
            for dx_ref, w_ref, out_ref in ((dxg_ref, wg_ref, dup_ref.at[0]), (dxv_ref, wv_ref, dup_ref.at[1])):
                win = dx_ref[pl.ds(r0, n), :]
                ahead1 = pltpu.roll(win, n - 1, axis=0)[:CONV_ROWS]
                ahead2 = pltpu.roll(win, n - 2, axis=0)[:CONV_ROWS]
                out_ref[pl.ds(r0, CONV_ROWS), :] = (w_ref[2:3, :] * win[:CONV_ROWS] + w_ref[1:2, :] * ahead1
                                                    + w_ref[0:1, :] * ahead2).astype(BF16)

        _row_steps(s, forward)
        _row_steps(s, backward)

    def col(rows, off):
        return pl.BlockSpec((rows, tc), lambda j: (0, j + off))

    return _call(
        body, (up_pre, up_pre, dact, conv_w, conv_w, conv_b, conv_b), comm, name=name, grid=(nb,),
        in_specs=[col(s, 0), col(s, nb), col(s, 0), col(CONV_TAPS, 0), col(CONV_TAPS, nb), col(1, 0), col(1, nb)],
        out_specs=[pl.BlockSpec((2, s, tc), lambda j: (0, 0, j)), col(CONV_TAPS, 0), col(CONV_TAPS, 0),
                   col(1, 0), col(1, 0)],
        out_shape=[jax.ShapeDtypeStruct((2, s, ff), BF16)] + [jax.ShapeDtypeStruct((CONV_TAPS, ff), F32)] * 2
        + [jax.ShapeDtypeStruct((1, ff), F32)] * 2,
        scratch_shapes=[pltpu.VMEM((s + SUBLANES, tc), F32)] * 2)


def _ada_fwd(c_all, w, b, name):
    nseq, d = c_all.shape
    na = w.shape[1]
    tn = _div_tile(na, 512)

    def body(c_ref, w_ref, b_ref, o_ref):
        cv = c_ref[...]
        sc = cv * _sigmoid(cv)
        o_ref[...] = jnp.dot(sc, w_ref[...], preferred_element_type=F32, precision=lax.Precision.HIGHEST) + b_ref[...]

    return pl.pallas_call(
        body, name=name, grid=(na // tn,),
        in_specs=[pl.BlockSpec((nseq, d), lambda j: (0, 0)), pl.BlockSpec((d, tn), lambda j: (0, j)),
                  pl.BlockSpec((1, tn), lambda j: (0, j))],
        out_specs=pl.BlockSpec((nseq, tn), lambda j: (0, j)),
        out_shape=jax.ShapeDtypeStruct((nseq, na), F32), compiler_params=_params(),
    )(c_all, w, b)


def _ada_bwd(c_all_t, dmod, name):
    d, nseq = c_all_t.shape
    na = dmod.shape[1]
    tm, tn = _div_tile(d, 512, SUBLANES), _div_tile(na, 1024)

    def body(c_ref, dm_ref, o_ref):
        cv = c_ref[...]
        o_ref[...] = jnp.dot(cv * _sigmoid(cv), dm_ref[...], preferred_element_type=F32,
                             precision=lax.Precision.HIGHEST)

    return pl.pallas_call(
        body, name=name, grid=(d // tm, na // tn),
        in_specs=[pl.BlockSpec((tm, nseq), lambda i, j: (i, 0)), pl.BlockSpec((nseq, tn), lambda i, j: (0, j))],
        out_specs=pl.BlockSpec((tm, tn), lambda i, j: (i, j)),
        out_shape=jax.ShapeDtypeStruct((d, na), F32), compiler_params=_params(),
    )(c_all_t, dmod)


def _adamw(w, g, m, v, name, comm=None, after=None):
    rows, cols = w.shape
    tb = _div_tile(rows, max(SUBLANES, (256 * 1024) // cols // SUBLANES * SUBLANES), SUBLANES)
    c1 = 1.0 / (1.0 - ADAM_B1 ** ADAM_STEP)
    c2 = 1.0 / (1.0 - ADAM_B2 ** ADAM_STEP)

    def body(*refs):
        w_ref, g_ref, m_ref, v_ref = refs[:4]
        d_ref, nm_ref, nv_ref = refs[-3:]
        gv = g_ref[...]
        nm = ADAM_B1 * m_ref[...] + (1.0 - ADAM_B1) * gv
        nv = ADAM_B2 * v_ref[...] + (1.0 - ADAM_B2) * (gv * gv)
        nm_ref[...] = nm
        nv_ref[...] = nv
        d_ref[...] = -ADAM_LR * ((nm * c1) / (jnp.sqrt(nv * c2) + ADAM_EPS) + ADAM_WD * w_ref[...])

    blk = pl.BlockSpec((tb, cols), lambda i: (i, 0))
    operands, in_specs = (w, g, m, v), [blk] * 4
    if after is not None:
        operands, in_specs = operands + (after,), in_specs + [pl.BlockSpec(after.shape, lambda i: (0, 0))]
    return _call(body, operands, comm, name=name, grid=(rows // tb,), in_specs=in_specs, out_specs=[blk] * 3,
                 out_shape=[jax.ShapeDtypeStruct((rows, cols), F32)] * 3)


def _sum_leading(parts, name, after=()):
    n, rows, cols = parts.shape
    tb = _div_tile(rows, 512, SUBLANES)

    def body(p_ref, *rest):
        o_ref = rest[-1]
        acc = p_ref[0]
        for k in range(1, n):
            acc = acc + p_ref[k]
        o_ref[...] = acc

    return pl.pallas_call(
        body, name=name, grid=(rows // tb,),
        in_specs=[pl.BlockSpec((n, tb, cols), lambda i: (0, i, 0))] + [pl.BlockSpec(memory_space=pl.ANY)] * len(after),
        out_specs=pl.BlockSpec((tb, cols), lambda i: (i, 0)),
        out_shape=jax.ShapeDtypeStruct((rows, cols), F32), compiler_params=_params(),
    )(parts, *after)


def _place():
    x, y, c = lax.axis_index("x"), lax.axis_index("y"), lax.axis_index("c")
    return x, y, c, [(1 - x, y), (x, 1 - y), (1 - x, 1 - y)]


def _all_gather(block, name):
    m_per, n = block.shape

    def body(x_ref, out_ref, send_sems, recv_sems, local_sem):
        x, y, c, chips = _place()
        me, sibling = (x, y, c), (x, y, 1 - c)

        def rows(px, py, pc):
            return out_ref.at[pl.ds((4 * px + 2 * py + pc) * m_per, m_per), :]

        def copy(k, blk, to, src=None):
            return pltpu.make_async_remote_copy(
                src_ref=rows(*blk) if src is None else src, dst_ref=rows(*blk), send_sem=send_sems.at[k],
                recv_sem=recv_sems.at[k], device_id=to, device_id_type=MESH)

        mine = pltpu.make_async_copy(x_ref, rows(*me), local_sem)
        mine.start()
        first = [copy(0, me, sibling, src=x_ref)]
        first += [copy(1 + j, me, (*chip, c), src=x_ref) for j, chip in enumerate(chips)]
        for cp in first:
            cp.start()
        passed = [copy(4 + j, (*chip, c), sibling) for j, chip in enumerate(chips)]
        for j, chip in enumerate(chips):
            copy(1 + j, (*chip, c), me).wait_recv()
            passed[j].start()
        copy(0, sibling, me).wait_recv()
        for j, chip in enumerate(chips):
            copy(4 + j, (*chip, 1 - c), me).wait_recv()
        for cp in first + passed:
            cp.wait_send()
        mine.wait()

    return pl.pallas_call(
        body, name=name, out_shape=jax.ShapeDtypeStruct((N_DEV * m_per, n), block.dtype),
        in_specs=[pl.BlockSpec(memory_space=pltpu.VMEM)], out_specs=pl.BlockSpec(memory_space=pltpu.VMEM),
        scratch_shapes=[pltpu.SemaphoreType.DMA((7,)), pltpu.SemaphoreType.DMA((7,)), pltpu.SemaphoreType.DMA],
        compiler_params=_params(),
    )(block)


def _hbm_specs(n):
    return [pl.BlockSpec(memory_space=HBM)] * n


def _part(ref, by_cols, half, quarter=None, lead=None):
    extent = ref.shape[-1] if by_cols else ref.shape[-2]
    size = extent // 2 if quarter is None else extent // 4
    first = half * (extent // 2) + (0 if quarter is None else quarter * size)
    tile = LANES if by_cols else 2 * SUBLANES
    span = pl.ds(pl.multiple_of(first, tile) if size % tile == 0 else first, size)
    index = (slice(None), span) if by_cols else (span, slice(None))
    return ref.at[index] if lead is None else ref.at[(lead,) + index]


def _half_rows(ref, half, lead=None):
    return _part(ref, False, half, lead=lead)


class _Comm:
    def __init__(self, operands, out_shape, sem_dims, build, aliases=None):
        self.operands, self.out_shape, self.sem_dims = list(operands), list(out_shape), list(sem_dims)
        self.scratch = [pltpu.SemaphoreType.DMA(d) for d in sem_dims]
        self.build, self.aliases = build, dict(aliases or {})


class _SemGrid:
    def __init__(self, sems, dims):
        self.sems, self.dims, self.at = list(sems), tuple(dims), self

    def __getitem__(self, index):
        index = index if isinstance(index, tuple) else (index,)
        flat = 0
        for i, d in zip(index, self.dims):
            flat = flat * d + i
        return self.sems[flat]


def _call(body, operands, comm=None, *, name, grid, in_specs, out_specs, out_shape, scratch_shapes=(),
          input_output_aliases=None):
    aliases = dict(input_output_aliases or {})
    if comm is None:
        return pl.pallas_call(
            body, name=name, grid=grid, in_specs=in_specs, out_specs=out_specs, out_shape=out_shape,
            scratch_shapes=list(scratch_shapes), input_output_aliases=aliases, compiler_params=_params())(*operands)
    single = not isinstance(out_shape, (list, tuple))
    outs = [out_shape] if single else list(out_shape)
    ospecs = [out_specs] if single else list(out_specs)
    n_in, n_out, n_scr = len(operands), len(outs), len(scratch_shapes)
    c_in, c_out = len(comm.operands), len(comm.out_shape)
    for i, o in comm.aliases.items():
        aliases[n_in + i] = n_out + o

    def hosted(*refs):
        ins, c_ins = refs[:n_in], refs[n_in:n_in + c_in]
        o0 = n_in + c_in
        o_refs, c_outs = refs[o0:o0 + n_out], refs[o0 + n_out:o0 + n_out + c_out]
        s0 = o0 + n_out + c_out
        scr, sems = refs[s0:s0 + n_scr], refs[s0 + n_scr:]
        stages = comm.build(c_ins, c_outs, sems)
        step, n_steps = 0, 1
        for dim, size in enumerate(grid):
            step, n_steps = step * size + pl.program_id(dim), n_steps * size
        pl.when(step == 0)(stages[0])
        body(*ins, *o_refs, *scr)
        for stage in stages[1:-1]:
            pl.when(step == (n_steps * MIDDLE_STAGE_AT) // 100)(stage)
        pl.when(step == n_steps - 1)(stages[-1])

    res = pl.pallas_call(
        hosted, name=name, grid=grid, in_specs=list(in_specs) + _hbm_specs(c_in),
        out_specs=ospecs + _hbm_specs(c_out), out_shape=outs + comm.out_shape,
        scratch_shapes=list(scratch_shapes) + comm.scratch, input_output_aliases=aliases,
        compiler_params=_params())(*operands, *comm.operands)
    return (res[0] if single else res[:n_out]), res[n_out:]


def _run_comm(comm, name):
    c_in, c_out = len(comm.operands), len(comm.out_shape)

    def body(*refs):
        for stage in comm.build(refs[:c_in], refs[c_in:c_in + c_out], refs[c_in + c_out:]):
            stage()

    return pl.pallas_call(
        body, name=name, in_specs=_hbm_specs(c_in), out_specs=_hbm_specs(c_out), out_shape=comm.out_shape,
        scratch_shapes=comm.scratch, input_output_aliases=comm.aliases, compiler_params=_params())(*comm.operands)


def _join_comms(comms):
    def build(in_refs, out_refs, sems):
        staged, i, o, k = [], 0, 0, 0
        for cm in comms:
            ni, no, ns = len(cm.operands), len(cm.out_shape), len(cm.sem_dims)
            staged.append(cm.build(in_refs[i:i + ni], out_refs[o:o + no], sems[k:k + ns]))
            i, o, k = i + ni, o + no, k + ns
        def run(fns):
            def stage():
                for fn in fns:
                    fn()
            return stage

        return (run([st[0] for st in staged]), run([fn for st in staged for fn in st[1:-1]]),
                run([st[-1] for st in staged]))

    aliases, i, o = {}, 0, 0
    for cm in comms:
        aliases.update({i + a: o + b for a, b in cm.aliases.items()})
        i, o = i + len(cm.operands), o + len(cm.out_shape)
    return _Comm(sum((cm.operands for cm in comms), []), sum((cm.out_shape for cm in comms), []),
                 sum((cm.sem_dims for cm in comms), []), build, aliases)


def _gather8_comm(block):
    def build(in_refs, out_refs, sems):
        (src,), (out,), (send_sems, recv_sems) = in_refs, out_refs, sems
        x, y, c, chips = _place()
        me, sibling = (x, y, c), (x, y, 1 - c)

        def copy(k, blk, to, own=False):
            dst = out.at[4 * blk[0] + 2 * blk[1] + blk[2]]
            return pltpu.make_async_remote_copy(
                src_ref=src if own else dst, dst_ref=dst, send_sem=send_sems.at[k], recv_sem=recv_sems.at[k],
                device_id=to, device_id_type=MESH)

        first = [copy(0, me, sibling, own=True)] + [copy(1 + j, me, (*chip, c), own=True)
                                                     for j, chip in enumerate(chips)]
        passed = [copy(4 + j, (*chip, c), sibling) for j, chip in enumerate(chips)]

        def start():
            for cp in first:
                cp.start()

        def middle():
            for j, chip in enumerate(chips):
                copy(1 + j, (*chip, c), me).wait_recv()
                passed[j].start()

        def finish():
            copy(0, sibling, me).wait_recv()
            for j, chip in enumerate(chips):
                copy(4 + j, (*chip, 1 - c), me).wait_recv()
            for cp in first + passed:
                cp.wait_send()

        return start, middle, finish

    return _Comm([block], [jax.ShapeDtypeStruct((N_DEV,) + block.shape, block.dtype)], [(7,), (7,)], build)


def _gather_comm(shards, by_cols=()):
    nw = len(shards)

    def build(in_refs, out_refs, sems):
        send_sems, recv_sems = sems
        x, y, c, chips = _place()
        me, sibling = (x, y, c), (x, y, 1 - c)
        across_x, across_y, diagonal = chips

        def copy(w, k, block, part, to, src=None):
            dst = _part(out_refs[w], w in by_cols, part[1], part[2] if part[0] else None, 2 * block[0] + block[1])
            return pltpu.make_async_remote_copy(
                src_ref=dst if src is None else src, dst_ref=dst, send_sem=send_sems.at[w, k],
                recv_sem=recv_sems.at[w, k], device_id=to, device_id_type=MESH)

        first = [copy(w, j, (x, y), (0, c), (*chip, c), src=_part(in_refs[w], w in by_cols, c))
                 for w in range(nw) for j, chip in enumerate((across_x, across_y))]
        passed = [[copy(w, 2, across_x, (1, c, 0), (*across_y, c)), copy(w, 3, across_y, (1, c, 1), (*across_x, c)),
                   copy(w, 4, across_x, (0, c), sibling), copy(w, 5, across_y, (0, c), sibling)] for w in range(nw)]
        last = [[copy(w, 6, diagonal, (1, c, 0), sibling), copy(w, 7, diagonal, (1, c, 1), sibling)]
                for w in range(nw)]

        def start():
            for cp in first:
                cp.start()

        def middle():
            for w in range(nw):
                copy(w, 0, across_x, (0, c), me).wait_recv()
                copy(w, 1, across_y, (0, c), me).wait_recv()
                for cp in passed[w]:
                    cp.start()

        def finish():
            for w in range(nw):
                copy(w, 2, diagonal, (1, c, 0), me).wait_recv()
                copy(w, 3, diagonal, (1, c, 1), me).wait_recv()
                for cp in last[w]:
                    cp.start()
            for w in range(nw):
                for k, block, part in ((4, across_x, (0, 1 - c)), (5, across_y, (0, 1 - c)),
                                       (6, diagonal, (1, 1 - c, 0)), (7, diagonal, (1, 1 - c, 1))):
                    copy(w, k, block, part, me).wait_recv()
            for cp in first + sum(passed, []) + sum(last, []):
                cp.wait_send()

        return start, middle, finish

    return _Comm(shards, [jax.ShapeDtypeStruct((N_CHIPS,) + w.shape, w.dtype) for w in shards],
                 [(nw, 8), (nw, 8)], build)


def _halved(shape, by_cols):
    return shape[:-1] + (shape[-1] // 2,) if by_cols else shape[:-2] + (shape[-2] // 2, shape[-1])


def _swap_comm(gs, by_cols=()):
    nw = len(gs)

    def build(in_refs, out_refs, sems):
        send_sems, recv_sems = sems
        x, y, c, _ = _place()
        cps = []
        for w in range(nw):
            cps.append(pltpu.make_async_remote_copy(
                src_ref=_part(in_refs[w], w in by_cols, 1 - c, lead=slice(None)), dst_ref=out_refs[w],
                send_sem=send_sems.at[w], recv_sem=recv_sems.at[w], device_id=(x, y, 1 - c), device_id_type=MESH))

        def start():
            for cp in cps:
                cp.start()

        def finish():
            for cp in cps:
                cp.wait()

        return start, finish

    return _Comm(gs, [jax.ShapeDtypeStruct(_halved(g.shape, w in by_cols), g.dtype) for w, g in enumerate(gs)],
                 [(nw,), (nw,)], build)


def _exchange_comm(s1s):
    nw = len(s1s)

    def build(in_refs, out_refs, sems):
        send_sems, recv_sems = sems
        x, y, c, chips = _place()
        cps = [pltpu.make_async_remote_copy(
            src_ref=in_refs[w].at[2 * chip[0] + chip[1]], dst_ref=out_refs[w].at[j], send_sem=send_sems.at[w, j],
            recv_sem=recv_sems.at[w, j], device_id=(*chip, c), device_id_type=MESH)
            for w in range(nw) for j, chip in enumerate(chips)]

        def start():
            for cp in cps:
                cp.start()

        def finish():
            for cp in cps:
                cp.wait()

        return start, finish

    return _Comm(s1s, [jax.ShapeDtypeStruct((N_CHIPS - 1,) + s.shape[1:], s.dtype) for s in s1s],
                 [(nw, 3), (nw, 3)], build)


def _size(dims):
    n = 1
    for d in dims:
        n *= d
    return n


def _sem_grids(comm, sem_refs):
    grids, pos = [], 0
    for dims in comm.sem_dims:
        grids.append(_SemGrid(sem_refs[pos:pos + _size(dims)], dims))
        pos += _size(dims)
    return grids


def _comm_split_start(comm, name, after=()):
    c_in, c_out = len(comm.operands), len(comm.out_shape)
    counts = [_size(d) for d in comm.sem_dims]
    n_sem = sum(counts)
    assert not comm.aliases

    def body(*refs):
        srcs, lands = refs[:c_in], refs[c_in:c_in + c_out]
        first_sem = c_in + c_out + len(after)
        start, _ = comm.build(srcs, lands, _sem_grids(comm, refs[first_sem:first_sem + n_sem]))
        start()
        refs[-1][...] = jnp.zeros(refs[-1].shape, refs[-1].dtype)

    lands = [pltpu.with_memory_space_constraint(lax.empty(o.shape, o.dtype), HBM) for o in comm.out_shape]
    srcs = [pltpu.with_memory_space_constraint(a, HBM) for a in comm.operands]
    res = pl.pallas_call(
        body, name=name, in_specs=_hbm_specs(c_in + c_out) + [pl.BlockSpec(memory_space=pl.ANY)] * len(after),
        out_specs=[pl.BlockSpec(memory_space=pltpu.SEMAPHORE)] * n_sem + _hbm_specs(c_in + c_out)
        + [pl.BlockSpec(memory_space=pltpu.VMEM)],
        out_shape=[pltpu.SemaphoreType.DMA(())] * n_sem + [pltpu.HBM(a.shape, a.dtype) for a in comm.operands]
        + [pltpu.HBM(o.shape, o.dtype) for o in comm.out_shape] + [jax.ShapeDtypeStruct((SUBLANES, LANES), F32)],
        input_output_aliases={i: n_sem + i for i in range(c_in + c_out)},
        compiler_params=_params(has_side_effects=pltpu.SideEffectType.DATAFLOW_SIDE_EFFECTING))(*srcs, *lands, *after)
    return res[:-1], res[-1]


def _comm_split_wait(comm, state, after, name):
    c_in, c_out, n_sem = len(comm.operands), len(comm.out_shape), sum(_size(d) for d in comm.sem_dims)
    sems, srcs, lands = state[:n_sem], state[n_sem:n_sem + c_in], state[n_sem + c_in:]

    def body(*refs):
        src_refs, land_refs = refs[:c_in], refs[c_in:c_in + c_out]
        _, finish = comm.build(src_refs, land_refs, _sem_grids(comm, refs[c_in + c_out:c_in + c_out + n_sem]))
        finish()

    sem_spec = pl.BlockSpec(memory_space=pltpu.SEMAPHORE)
    res = pl.pallas_call(
        body, name=name, in_specs=_hbm_specs(c_in + c_out) + [sem_spec] * n_sem + [pl.BlockSpec(memory_space=pl.ANY)],
        out_specs=_hbm_specs(c_in + c_out),
        out_shape=[pltpu.HBM(a.shape, a.dtype) for a in srcs] + [pltpu.HBM(o.shape, o.dtype) for o in lands],
        input_output_aliases={i: i for i in range(c_in + c_out)},
        compiler_params=_params(has_side_effects=pltpu.SideEffectType.DATAFLOW_SIDE_EFFECTING),
    )(*srcs, *lands, *sems, after)
    return res[:c_in], res[c_in:]


def _share_comm(fs, by_cols=()):
    nw = len(fs)

    def build(in_refs, out_refs, sems):
        del in_refs
        send_sems, recv_sems = sems
        x, y, c, _ = _place()

        def copy(w, half):
            part = _part(out_refs[w], w in by_cols, half)
            return pltpu.make_async_remote_copy(
                src_ref=part, dst_ref=part, send_sem=send_sems.at[w], recv_sem=recv_sems.at[w],
                device_id=(x, y, 1 - c), device_id_type=MESH)

        sends = [copy(w, c) for w in range(nw)]

        def start():
            for cp in sends:
                cp.start()

        def finish():
            for w in range(nw):
                copy(w, 1 - c).wait_recv()
            for cp in sends:
                cp.wait_send()

        return start, finish

    return _Comm(fs, [jax.ShapeDtypeStruct(f.shape, f.dtype) for f in fs],
                 [(nw,), (nw,)], build,
                 aliases={w: w for w in range(nw)})


def _add_sibling(g, r1, place, name, by_cols=False):
    nch, h, cols = r1.shape
    tr = _div_tile(h, 1024 if by_cols else 512, 2 * SUBLANES)
    nb = h // tr
    mine = (lambda k, i, p: (k, i, p[0])) if by_cols else (lambda k, i, p: (k, p[0] * nb + i, 0))

    def body(place_ref, g_ref, r_ref, o_ref):
        del place_ref
        o_ref[...] = (g_ref[...].astype(F32) + r_ref[...].astype(F32)).astype(BF16)

    spec = pltpu.PrefetchScalarGridSpec(
        num_scalar_prefetch=1, grid=(nch, nb),
        in_specs=[pl.BlockSpec((None, tr, cols), mine), pl.BlockSpec((None, tr, cols), lambda k, i, p: (k, i, 0))],
        out_specs=pl.BlockSpec((None, tr, cols), lambda k, i, p: (k, i, 0)))
    return pl.pallas_call(body, name=name, grid_spec=spec, out_shape=jax.ShapeDtypeStruct((nch, h, cols), BF16),
                          compiler_params=_params())(place, g, r1)


def _add_chips(s1, r2, place, name, by_cols=False):
    _, h, cols = s1.shape
    tr = _div_tile(h, 1024 if by_cols else 512, 2 * SUBLANES)
    nb = h // tr
    mine = (lambda i, p: (i, p[0])) if by_cols else (lambda i, p: (p[0] * nb + i, 0))
    whole = (h, 2 * cols) if by_cols else (2 * h, cols)

    def body(place_ref, s_ref, r_ref, o_ref):
        del place_ref
        acc = s_ref[...].astype(F32)
        for j in range(N_CHIPS - 1):
            acc = acc + r_ref[j].astype(F32)
        o_ref[...] = acc

    spec = pltpu.PrefetchScalarGridSpec(
        num_scalar_prefetch=1, grid=(nb,),
        in_specs=[pl.BlockSpec((None, tr, cols), lambda i, p: (p[1], i, 0)),
                  pl.BlockSpec((N_CHIPS - 1, tr, cols), lambda i, p: (0, i, 0))],
        out_specs=pl.BlockSpec((tr, cols), mine))
    return pl.pallas_call(body, name=name, grid_spec=spec, out_shape=jax.ShapeDtypeStruct(whole, F32),
                          compiler_params=_params())(place, s1, r2)


def _quarter_turn(m):
    h = m.shape[-1] // 2
    return jnp.concatenate([-m[..., h:], m[..., :h]], axis=-1)


def _quarter_turn_back(m):
    h = m.shape[-1] // 2
    return jnp.concatenate([m[..., h:], -m[..., :h]], axis=-1)


def _stack_rows(parts):
    out = lax.empty((sum(p.shape[0] for p in parts),) + parts[0].shape[1:], parts[0].dtype)
    row = 0
    for p in parts:
        out = lax.dynamic_update_slice(out, p, (row, 0))
        row += p.shape[0]
    return out


def _join_cols(sh):
    return jnp.concatenate([sh[k] for k in range(N_CHIPS)], axis=1)


def _split_cols(full):
    c = full.shape[1] // N_CHIPS
    return jnp.stack([full[:, k * c:(k + 1) * c] for k in range(N_CHIPS)])


def kernel(x, c, positions, w_ada, b_ada, pre_norm1_g, w_in, gm_ln_g, gm_ln_b, gm_w_s, gm_b_s, w_branch_a, q_norm_g, w_uq, kv_norm_g, w_ukv, w_branch_b, w_out, post_norm1_g, pre_norm2_g, w_up, conv_w, conv_b, w_down, post_norm2_g, loss_target, m_w_ada, m_b_ada, m_pre_norm1_g, m_w_in, m_gm_ln_g, m_gm_ln_b, m_gm_w_s, m_gm_b_s, m_w_branch_a, m_q_norm_g, m_w_uq, m_kv_norm_g, m_w_ukv, m_w_branch_b, m_w_out, m_post_norm1_g, m_pre_norm2_g, m_w_up, m_conv_w, m_conv_b, m_w_down, m_post_norm2_g, v_w_ada, v_b_ada, v_pre_norm1_g, v_w_in, v_gm_ln_g, v_gm_ln_b, v_gm_w_s, v_gm_b_s, v_w_branch_a, v_q_norm_g, v_w_uq, v_kv_norm_g, v_w_ukv, v_w_branch_b, v_w_out, v_post_norm1_g, v_pre_norm2_g, v_w_up, v_conv_w, v_conv_b, v_w_down, v_post_norm2_g):
    given = dict(locals())
    s, d = x.shape[1], x.shape[2]
    gw = gm_ln_g.shape[0]
    ql, kvl = q_norm_g.shape[0], kv_norm_g.shape[0]
    heads = N_CHIPS * w_uq.shape[1] // (NOPE + ROPE)
    ff = N_CHIPS * w_down.shape[0]
    assert gw == d and N_CHIPS * w_ukv.shape[1] == heads * (NOPE + VHEAD)
    ix, iy, ic = lax.axis_index("x"), lax.axis_index("y"), lax.axis_index("c")
    chip = 2 * ix + iy
    dev = 2 * chip + ic
    row = lambda v: v.reshape(1, -1)

    first = _all_gather(jnp.concatenate([jnp.pad(c, ((0, SUBLANES - 1), (0, 0))),
                                         jnp.pad(conv_w, ((0, SUBLANES - CONV_TAPS), (0, 0)))], axis=1), "gather_c")
    first = first.reshape(N_DEV, SUBLANES, d + conv_w.shape[1])
    c_all = first[:, 0, :d]
    conv_wf = first[::N_CORES, :CONV_TAPS, d:].transpose(1, 0, 2).reshape(CONV_TAPS, N_CHIPS * conv_w.shape[1])
    na = w_ada.shape[1]
    b_ada_mine = lax.dynamic_slice(b_ada, (chip * na,), (na,))
    mod_cols = _ada_fwd(c_all, w_ada, row(b_ada_mine), "ada_fwd")
    mod_all = _all_gather(mod_cols, "gather_mod").reshape(N_CHIPS, N_CORES, N_DEV, na)[:, 0]
    mod = lax.dynamic_index_in_dim(mod_all, dev, axis=1, keepdims=False).reshape(N_MOD, d)
    shift1, scale1, gate1, shift2, scale2, gate2 = (mod[i:i + 1] for i in range(N_MOD))

    mine = {n: (given[n].T if n == "w_in" else given[n]).astype(BF16) for n in BIG}
    gather = lambda names: _gather_comm([mine[n] for n in names], [i for i, n in enumerate(names) if n == "w_in"])
    whole = lambda n, g: lax.dynamic_update_slice(g, mine[n][None], (chip, 0, 0))
    rows4 = lambda sh4: sh4.reshape(-1, sh4.shape[2])
    wi_t = rows4(whole("w_in", _run_comm(gather(["w_in"]), "gather_w_in")[0]))
    o_q, o_kv, o_pe, o_ga = 2 * gw, 2 * gw + ql, 2 * gw + ql + kvl, 2 * gw + ql + kvl + ROPE
    w_in_big_t = _stack_rows([wi_t[:o_q], wi_t[o_ga:]])
    w_in_lat_t = _stack_rows([wi_t[o_q:o_ga], _quarter_turn(wi_t[o_pe:o_ga].T).T])

    inv = ROPE_THETA ** (-jnp.arange(0, ROPE, 2, dtype=F32) / ROPE)
    ang = positions[0].astype(F32)[:, None] * inv
    cos, sin = jnp.cos(ang), jnp.sin(ang)
    rope_k = jnp.concatenate([cos, cos, sin, sin], axis=1)
    softmax_scale = float(NOPE + ROPE) ** -0.5
    rope_q = jnp.concatenate([jnp.ones((s, NOPE), F32), rope_k], axis=1) * softmax_scale

    x2d, tgt = x[0], loss_target[0]
    g_pre1, g_post1, g_pre2, g_post2 = row(pre_norm1_g), row(post_norm1_g), row(pre_norm2_g), row(post_norm2_g)
    ln_g, ln_b, q_g, kv_g = row(gm_ln_g), row(gm_ln_b), row(q_norm_g), row(kv_norm_g)
    b_s_t = gm_b_s.T
    conv_bf = row(conv_b)

    h1 = _prenorm(x2d, g_pre1, scale1, shift1, "prenorm1")
    z_big, (g_uq, g_ukv, g_a) = _matmul(h1, w_in_big_t, mode="nt", out_dtype=BF16, name="mm_z_big", tm=s,
                                        comm=gather(["w_uq", "w_ukv", "w_branch_a"]))
    wq = _join_cols(whole("w_uq", g_uq)).reshape(ql, heads, NOPE + ROPE)
    w_q = jnp.concatenate([wq, _quarter_turn(wq[:, :, NOPE:])], axis=2).reshape(ql, heads * HEAD_W)
    w_kv = _join_cols(whole("w_ukv", g_ukv)).reshape(kvl, heads, 2, NOPE).transpose(0, 2, 1, 3)
    w_kv = w_kv.reshape(kvl, 2 * heads * NOPE)
    w_a = rows4(whole("w_branch_a", g_a))
    z_lat = _matmul(h1, w_in_lat_t, mode="nt", out_dtype=F32, name="mm_z_lat", tm=s, tn=1024)
    a_act = _gmlp_fwd(z_big, ln_g, ln_b, gm_w_s, b_s_t, "gmlp_fwd")
    qn, kvn, kr = _mla_prep(z_lat, q_g, kv_g, rope_k, "mla_prep")
    q_rot = _matmul(qn, w_q, mode="nn", out_dtype=BF16, name="mm_q", tm=s, tn=HEAD_W, mul=rope_q)
    kv_all = _matmul(kvn, w_kv, mode="nn", out_dtype=BF16, name="mm_kv", tm=s, tn=1024)
    (o_att, lse), (g_b, g_o, g_up) = _attn_fwd(q_rot, kv_all, kr, heads, "attn_fwd",
                                               comm=gather(["w_branch_b", "w_out", "w_up"]))
    w_b, w_o, w_upf = rows4(whole("w_branch_b", g_b)), rows4(whole("w_out", g_o)), whole("w_up", g_up)
    y_a = _matmul(a_act, w_a, mode="nn", out_dtype=BF16, name="mm_y_a", tm=s)
    y_b = _matmul(o_att, w_b, mode="nn", out_dtype=BF16, name="mm_y_b", tm=s)
    merged = _merge(z_big, y_a, y_b, "merge")
    y1 = _matmul(merged, w_o, mode="nn", out_dtype=F32, name="mm_y1", tm=s)
    x1, h2 = _post_pre(x2d, y1, gate1, g_post1, g_pre2, scale2, shift2, "post1_pre2")

    up_pre, (g_dn,) = _matmul(h2, w_upf, mode="nn", out_dtype=BF16, name="mm_up", tm=s, tn=1408,
                              comm=gather(["w_down"]))
    w_dn = rows4(whole("w_down", g_dn))
    act = _conv_fwd(up_pre, conv_wf, conv_bf, "conv_fwd")
    ffn = _matmul(act, w_dn, mode="nn", out_dtype=F32, name="mm_ffn", tm=s, tn=1024, tk=1408)

    dffn, dgate2, g_post2_grad, dx2, loss_part = _post_bwd(ffn, gate2, g_post2, "post2_bwd", xin=x1, target=tgt)
    loss = lax.psum(loss_part[0, 0], ("x", "y", "c"))
    place = jnp.stack([ic, chip]).astype(jnp.int32)
    rows_of = lambda g: g.reshape(N_CHIPS, g.shape[0] // N_CHIPS, g.shape[1])
    add_sibling = lambda names, gs, r1s: [_add_sibling(g, r1, place, "rs_add_sibling_" + n, by_cols=n == "w_in")
                                          for n, g, r1 in zip(names, gs, r1s)]
    add_chips = lambda names, s1s, r2s: [_add_chips(s1, r2, place, "rs_add_chips_" + n, by_cols=n == "w_in")
                                         for n, s1, r2 in zip(names, s1s, r2s)]
    gp_down = [rows_of(_matmul(act, dffn, mode="tn", out_dtype=BF16, name="mm_gw_down", tn=2048, tk=s))]
    dact, r1_down = _matmul(dffn, w_dn, mode="nt", out_dtype=BF16, name="mm_dact", tm=s, comm=_swap_comm(gp_down))
    s1_down = add_sibling(["w_down"], gp_down, r1_down)
    (dup, gcw_g, gcw_v, gcb_g, gcb_v), r2_down = _conv_bwd(up_pre, dact, conv_wf, conv_bf, "conv_bwd",
                                                            comm=_exchange_comm(s1_down))
    half_down = add_chips(["w_down"], s1_down, r2_down)
    dh2 = _matmul(dup, w_upf, mode="nt", out_dtype=F32, name="mm_dh2", tm=s, tn=1024, tk=1408)
    dx1, dshift2, dscale2, g_pre2_grad = _prenorm_bwd(x1, dh2, dx2, g_pre2, scale2, "prenorm2_bwd")

    dy1, dgate1, g_post1_grad = _post_bwd(y1, gate1, g_post1, "post1_bwd", dxo=dx1)
    dmerged = _matmul(dy1, w_o, mode="nt", out_dtype=BF16, name="mm_dmerged", tm=s)
    gw_out = _matmul(merged, dy1, mode="tn", out_dtype=BF16, name="mm_gw_out", tn=1024, tk=s)
    dy_a, dy_b, dz_big = _merge_bwd(dmerged, z_big, y_a, y_b, "merge_bwd")
    gw_a = _matmul(a_act, dy_a, mode="tn", out_dtype=BF16, name="mm_gw_a", tn=1024, tk=s)
    gw_b = _matmul(o_att, dy_b, mode="tn", out_dtype=BF16, name="mm_gw_b", tn=1024, tk=s)
    mid = ["w_up", "w_out", "w_branch_a", "w_branch_b"]
    gp_oab = [rows_of(gw_out), rows_of(gw_a), rows_of(gw_b)]
    da, r1_oab = _matmul(dy_a, w_a, mode="nt", out_dtype=BF16, name="mm_da", tm=s, comm=_swap_comm(gp_oab))
    s1_oab = add_sibling(mid[1:], gp_oab, r1_oab)
    gw_up, r2_oa = _matmul(h2, dup, mode="tn", out_dtype=BF16, name="mm_gw_up", tm=1024, tn=1408, tk=s,
                           out_groups=N_CHIPS, comm=_exchange_comm(s1_oab[:2]))
    do = _matmul(dy_b, w_b, mode="nt", out_dtype=BF16, name="mm_do", tm=s)
    (dz_big, g_ws, g_bs_t, g_ln_g, g_ln_b), r1_up = _gmlp_bwd(z_big, da, dz_big, ln_g, ln_b, gm_w_s, b_s_t,
                                                               "gmlp_bwd", comm=_swap_comm([gw_up]))
    s1_mid = add_sibling(mid[:1], [gw_up], r1_up) + s1_oab
    (dq, dk, dv), r2_up = _attn_bwd(q_rot, kv_all, kr, o_att, do, lse, heads, "attn_bwd",
                                    comm=_exchange_comm(s1_mid[:1]))
    dq_big, dkv, dkk = _mla_bwd_mid(dq, dk, dv, rope_q, rope_k, heads, "mla_bwd_mid")
    gw_q = _matmul(qn, dq_big, mode="tn", out_dtype=F32, name="mm_gw_q", tn=1024, tk=s)
    dqn = _matmul(dq_big, w_q, mode="nt", out_dtype=F32, name="mm_dqn", tm=s, tk=1024)
    gw_kv = _matmul(kvn, dkv, mode="tn", out_dtype=BF16, name="mm_gw_kv", tn=1024, tk=s)
    dkvn = _matmul(dkv, w_kv, mode="nt", out_dtype=F32, name="mm_dkvn", tm=s, tk=1024)
    dz_lat, g_q, g_kv = _mla_bwd_post(z_lat, dqn, dkvn, dkk, q_g, kv_g, "mla_bwd_post")

    partial = {
        "gm_ln_g": g_ln_g, "gm_ln_b": g_ln_b, "gm_w_s": g_ws, "gm_b_s": g_bs_t[:, :gm_b_s.shape[0]].T,
        "q_norm_g": g_q, "kv_norm_g": g_kv, "post_norm1_g": g_post1_grad, "pre_norm2_g": g_pre2_grad,
        "conv_w": jnp.concatenate([gcw_g, gcw_v], axis=1), "conv_b": jnp.concatenate([gcb_g, gcb_v], axis=1),
        "post_norm2_g": g_post2_grad,
    }
    flat = jnp.concatenate([partial[n].reshape(-1) for n in SMALL_PARTIAL])
    n_small = flat.shape[0]
    rows_small = -(-n_small // (LANES * SMALL_ROW_TILE)) * SMALL_ROW_TILE
    flat = jnp.pad(flat, (0, rows_small * LANES - n_small)).reshape(rows_small, LANES)

    def small_pack(prefix, source):
        v = jnp.concatenate([source[prefix + n].reshape(-1) for n in SMALL])
        rows = -(-v.shape[0] // (LANES * SUBLANES)) * SUBLANES
        return jnp.pad(v, (0, rows * LANES - v.shape[0])).reshape(rows, LANES)

    small_state = [small_pack(prefix, given) for prefix in ("", "m_", "v_")]

    dh1, r2_a_b = _matmul(dz_big, w_in_big_t, mode="nn", out_dtype=F32, name="mm_dh1_big", tm=s, tn=1024, tk=1024,
                          comm=_exchange_comm(s1_mid[3:]))
    half_mid = add_chips(mid, s1_mid, list(r2_up) + list(r2_oa) + list(r2_a_b))
    gw_big_t, hosted = _matmul(dz_big, h1, mode="tn", out_dtype=BF16, name="mm_gw_in_big", tn=2048, tk=s,
                               comm=_join_comms([_share_comm(half_down + half_mid), _gather8_comm(flat)]))
    shared, small_all = hosted[:-1], lax.dynamic_update_slice(hosted[-1], flat[None], (dev, 0, 0))
    small_sum = _sum_leading(small_all, "sum_small", after=small_state + [loss.reshape(1, 1)]).reshape(-1)
    small_grads, off = {}, 0
    for n in SMALL_PARTIAL:
        shape = (CONV_TAPS, 2 * ff) if n == "conv_w" else given[n].shape
        small_grads[n] = small_sum[off:off + partial[n].size].reshape(shape)
        off += partial[n].size
    small_grads["conv_w"] = lax.dynamic_slice(small_grads["conv_w"], (0, chip * conv_w.shape[1]), conv_w.shape)
    grads = dict(zip(["w_down"] + mid, shared), **small_grads)
    gw_lat_t = _matmul(dz_lat, h1, mode="tn", out_dtype=F32, name="mm_gw_in_lat", tm=1024, tn=1024, tk=s)

    gq = gw_q.reshape(ql, heads, HEAD_W)
    gq_pe = gq[:, :, NOPE:NOPE + ROPE] + _quarter_turn_back(gq[:, :, NOPE + ROPE:])
    g_pe_t = gw_lat_t[ql + kvl:ql + kvl + ROPE] + _quarter_turn_back(gw_lat_t[ql + kvl + ROPE:].T).T
    last = ["w_in", "w_uq", "w_ukv"]
    gw_in_t = _stack_rows([gw_big_t[:o_q], gw_lat_t[:ql + kvl].astype(BF16), g_pe_t.astype(BF16), gw_big_t[o_q:]])
    gp_last = [
        gw_in_t.reshape(N_CHIPS, gw_in_t.shape[0] // N_CHIPS, d),
        _split_cols(jnp.concatenate([gq[:, :, :NOPE], gq_pe], axis=2).reshape(ql, heads * (NOPE + ROPE)).astype(BF16)),
        _split_cols(gw_kv.reshape(kvl, 2, heads, NOPE).transpose(0, 2, 1, 3).reshape(kvl, heads * 2 * NOPE)),
    ]
    dh1, r1_last = _matmul(dz_lat, w_in_lat_t, mode="nn", out_dtype=F32, name="mm_dh1_lat", tm=s, tk=1024, add=dh1,
                           comm=_swap_comm(gp_last, by_cols=[0]))
    grad_x, dshift1, dscale1, g_pre1_grad = _prenorm_bwd(x2d, dh1, dx1, g_pre1, scale1, "prenorm1_bwd")
    s1_last = add_sibling(last, gp_last, r1_last)

    dmod = jnp.concatenate([dshift1, dscale1, dgate1, dshift2, dscale2, dgate2, g_pre1_grad], axis=1)
    dmod_all = _all_gather(jnp.pad(dmod, ((0, SUBLANES - 1), (0, 0))), "gather_dmod")
    dmod_all = dmod_all.reshape(N_DEV, SUBLANES, (N_MOD + 1) * d)[:, 0]
    dmod_sum = _sum_leading(dmod_all.reshape(N_DEV, 1, (N_MOD + 1) * d), "sum_dmod")[0]
    grads["b_ada"], grads["pre_norm1_g"] = dmod_sum[:N_MOD * d], dmod_sum[N_MOD * d:]
    dmod_mine = lax.dynamic_slice(dmod_all, (0, chip * na), (N_DEV, na))
    grads["w_ada"] = _ada_bwd(c_all.T, dmod_mine, "ada_bwd")

    delta, new_m, new_v = {}, {}, {}

    def adamw(n, after=None):
        turn = (lambda a: a.T) if n == "w_in" else (lambda a: a)
        outs = _adamw(turn(given[n]), grads[n], turn(given["m_" + n]), turn(given["v_" + n]), "adamw_" + n,
                      after=after)
        grads[n] = turn(grads[n])
        delta[n], new_m[n], new_v[n] = (turn(o) for o in outs)

    exchange_last = _exchange_comm(s1_last)
    in_flight, token = _comm_split_start(exchange_last, "rs_exchange_last_start", after=[dmod_sum, small_sum])
    for n in ["w_ada", "w_down"] + mid:
        adamw(n, after=token)
    s1_last, r2_last = _comm_split_wait(exchange_last, in_flight, delta[mid[-1]], "rs_exchange_last_wait")
    half_last = add_chips(last, s1_last, r2_last)
    grads.update(zip(last, _run_comm(_share_comm(half_last, by_cols=[0]), "rs_share_last")))
    for n in last:
        adamw(n)

    outs = _adamw(small_state[0], small_pack("", grads), small_state[1], small_state[2], "adamw_small")
    off = 0
    for n in SMALL:
        size = given[n].size
        for store, packed_out in zip((delta, new_m, new_v), outs):
            store[n] = packed_out.reshape(-1)[off:off + size].reshape(given[n].shape)
        off += size

    return (loss, grad_x[None], *[grads[n] for n in WEIGHTS], *[delta[n] for n in WEIGHTS],
            *[new_m[n] for n in WEIGHTS], *[new_v[n] for n in WEIGHTS])
```

```python
import functools

import jax
import jax.numpy as jnp
from jax import lax
from jax.experimental import pallas as pl
from jax.experimental.pallas import tpu as pltpu

F32 = jnp.float32
BF16 = jnp.bfloat16
MESH = pl.DeviceIdType.MESH
HBM = pltpu.HBM

EPS = 1e-6
NOPE, ROPE, VHEAD = 128, 64, 128
HEAD_W = NOPE + 2 * ROPE
ROPE_THETA = 10000.0
CONV_TAPS = 3
N_MOD = 6
N_CHIPS, N_CORES, N_DEV = 4, 2, 8
ADAM_LR, ADAM_B1, ADAM_B2, ADAM_EPS, ADAM_WD, ADAM_STEP = 0.001, 0.9, 0.999, 1e-08, 0.01, 10

LANES = 128
SUBLANES = 8
VMEM_LIMIT = 56 * 2**20
SMALL_ROW_TILE = 256

BIG = ("w_in", "w_branch_a", "w_uq", "w_ukv", "w_branch_b", "w_out", "w_up", "w_down")
WEIGHTS = ("w_ada", "b_ada", "pre_norm1_g", "w_in", "gm_ln_g", "gm_ln_b", "gm_w_s", "gm_b_s", "w_branch_a",
           "q_norm_g", "w_uq", "kv_norm_g", "w_ukv", "w_branch_b", "w_out", "post_norm1_g", "pre_norm2_g",
           "w_up", "conv_w", "conv_b", "w_down", "post_norm2_g")
SMALL_PARTIAL = ("gm_ln_g", "gm_ln_b", "gm_w_s", "gm_b_s", "q_norm_g", "kv_norm_g", "post_norm1_g",
                 "pre_norm2_g", "conv_w", "conv_b", "post_norm2_g")
SMALL = ("b_ada", "pre_norm1_g") + SMALL_PARTIAL


def _div_tile(n, cap, mult=LANES):
    t = (min(cap, n) // mult) * mult
    while t >= mult:
        if n % t == 0:
            return t
        t -= mult
    return n


def _params(**kw):
    return pltpu.CompilerParams(vmem_limit_bytes=VMEM_LIMIT, **kw)


def _row_spec(width):
    return pl.BlockSpec((1, width), lambda *_: (0, 0))


def _gelu(x):
    k = 0.7978845608028654
    return 0.5 * x * (1.0 + jnp.tanh(k * (x + 0.044715 * x * x * x)))


def _gelu_grad(x):
    k = 0.7978845608028654
    t = jnp.tanh(k * (x + 0.044715 * x * x * x))
    return 0.5 * (1.0 + t) + 0.5 * x * (1.0 - t * t) * k * (1.0 + 3.0 * 0.044715 * x * x)


def _sigmoid(x):
    return 0.5 * jnp.tanh(0.5 * x) + 0.5


def _dot(a, b, dims):
    return lax.dot_general(a, b, (dims, ((), ())), preferred_element_type=F32)


NN = ((1,), (0,))
NT = ((1,), (1,))
TN = ((0,), (0,))


def _logical(arr):
    if arr.ndim == 2:
        return arr.shape[0], arr.shape[1], arr.shape[1]
    return arr.shape[1], arr.shape[0] * arr.shape[2], arr.shape[2]


def _tile_spec(ndim, group_w, blk_rows, blk_cols, row_of, col_of):
    if ndim == 2:
        return pl.BlockSpec((blk_rows, blk_cols), lambda i, j, k: (row_of(i, j, k), col_of(i, j, k)))
    per = group_w // blk_cols
    return pl.BlockSpec((None, blk_rows, blk_cols),
                        lambda i, j, k: (col_of(i, j, k) // per, row_of(i, j, k), col_of(i, j, k) % per))


def _matmul(a, b, *, mode, out_dtype, name, tm=512, tn=512, tk=2048, mul=None, add=None, out_groups=None, comm=None):
    ar, ac, agw = _logical(a)
    br, bc, bgw = _logical(b)
    if mode == "nn":
        m, kd, n = ar, ac, bc
        m_w, k_w, n_w = (), (agw,), (bgw,)
    elif mode == "nt":
        m, kd, n = ar, ac, br
        m_w, k_w, n_w = (), (agw, bgw), ()
    else:
        m, kd, n = ac, ar, bc
        m_w, k_w, n_w = (agw,), (), (bgw,)
    if out_groups is not None:
        n_w = n_w + (n // out_groups,)
    tm = _div_tile(min((m,) + m_w), tm, LANES if mode == "tn" else SUBLANES)
    tn = _div_tile(min((n,) + n_w), tn)
    tk = _div_tile(min((kd,) + k_w), tk)
    assert all(w % tn == 0 for w in n_w) and all(w % tk == 0 for w in k_w) and all(w % tm == 0 for w in m_w)
    nk = kd // tk
    dims = {"nn": NN, "nt": NT, "tn": TN}[mode]
    gi, gj, gk = (lambda i, j, k: i), (lambda i, j, k: j), (lambda i, j, k: k)
    if mode == "nn":
        a_spec = _tile_spec(a.ndim, agw, tm, tk, gi, gk)
        b_spec = _tile_spec(b.ndim, bgw, tk, tn, gk, gj)
    elif mode == "nt":
        a_spec = _tile_spec(a.ndim, agw, tm, tk, gi, gk)
        b_spec = _tile_spec(b.ndim, bgw, tn, tk, gj, gk)
    else:
        a_spec = _tile_spec(a.ndim, agw, tk, tm, gk, gi)
        b_spec = _tile_spec(b.ndim, bgw, tk, tn, gk, gj)
    in_specs, operands = [a_spec, b_spec], [a, b]
    if mul is not None:
        assert mul.shape == (m, tn)
        in_specs.append(pl.BlockSpec((tm, tn), lambda i, j, k: (i, 0)))
        operands.append(mul)
    if add is not None:
        in_specs.append(pl.BlockSpec((tm, tn), lambda i, j, k: (i, j)))
        operands.append(add)

    def body(*refs):
        a_ref, b_ref = refs[0], refs[1]
        pos = 2
        mul_ref = add_ref = None
        if mul is not None:
            mul_ref, pos = refs[pos], pos + 1
        if add is not None:
            add_ref, pos = refs[pos], pos + 1
        o_ref = refs[pos]

        def finish(r):
            if mul_ref is not None:
                r = r * mul_ref[...]
            if add_ref is not None:
                r = r + add_ref[...]
            o_ref[...] = r.astype(out_dtype)

        part = _dot(a_ref[...], b_ref[...], dims)
        if nk == 1:
            finish(part)
        else:
            acc_ref = refs[pos + 1]
            k = pl.program_id(2)

            @pl.when(k == 0)
            def _():
                acc_ref[...] = part

            @pl.when(k > 0)
            def _():
                acc_ref[...] += part

            @pl.when(k == nk - 1)
            def _():
                finish(acc_ref[...])

    if out_groups is None:
        out_spec, out_dims = _tile_spec(2, n, tm, tn, gi, gj), (m, n)
    else:
        out_spec, out_dims = _tile_spec(3, n // out_groups, tm, tn, gi, gj), (out_groups, m, n // out_groups)
    return _call(body, operands, comm, name=name, grid=(m // tm, n // tn, nk), in_specs=in_specs, out_specs=out_spec,
                 out_shape=jax.ShapeDtypeStruct(out_dims, out_dtype),
                 scratch_shapes=[] if nk == 1 else [pltpu.VMEM((tm, tn), F32)])


def _accumulate(ref, value):
    @pl.when(pl.program_id(0) == 0)
    def _():
        ref[...] = value

    @pl.when(pl.program_id(0) > 0)
    def _():
        ref[...] += value


def _colsum(v):
    return jnp.sum(v, axis=0, keepdims=True)


def _rowmean(v):
    return jnp.mean(v, axis=-1, keepdims=True)


def _prenorm(x, g, scale, shift, name):
    s, d = x.shape
    tb = _div_tile(s, 256, SUBLANES)

    def body(x_ref, g_ref, sc_ref, sh_ref, h_ref):
        xv = x_ref[...]
        r = lax.rsqrt(_rowmean(xv * xv) + EPS)
        h_ref[...] = ((xv * r) * g_ref[...] * (1.0 + sc_ref[...]) + sh_ref[...]).astype(BF16)

    blk = pl.BlockSpec((tb, d), lambda i: (i, 0))
    return pl.pallas_call(
        body, name=name, grid=(s // tb,), in_specs=[blk, _row_spec(d), _row_spec(d), _row_spec(d)],
        out_specs=blk, out_shape=jax.ShapeDtypeStruct((s, d), BF16), compiler_params=_params(),
    )(x, g, scale, shift)


def _post_pre(x, y, gate, pg, g2, scale2, shift2, name):
    s, d = x.shape
    tb = _div_tile(s, 256, SUBLANES)

    def body(x_ref, y_ref, gate_ref, pg_ref, g2_ref, sc_ref, sh_ref, x1_ref, h2_ref):
        yv = y_ref[...]
        rp = lax.rsqrt(_rowmean(yv * yv) + EPS)
        x1 = x_ref[...] + gate_ref[...] * ((yv * rp) * pg_ref[...])
        x1_ref[...] = x1
        r2 = lax.rsqrt(_rowmean(x1 * x1) + EPS)
        h2_ref[...] = ((x1 * r2) * g2_ref[...] * (1.0 + sc_ref[...]) + sh_ref[...]).astype(BF16)

    blk = pl.BlockSpec((tb, d), lambda i: (i, 0))
    return pl.pallas_call(
        body, name=name, grid=(s // tb,), in_specs=[blk, blk] + [_row_spec(d)] * 5,
        out_specs=[blk, blk],
        out_shape=[jax.ShapeDtypeStruct((s, d), F32), jax.ShapeDtypeStruct((s, d), BF16)],
        compiler_params=_params(),
    )(x, y, gate, pg, g2, scale2, shift2)


def _post_bwd(y, gate, pg, name, *, dxo=None, xin=None, target=None):
    s, d = y.shape
    tb = _div_tile(s, 256, SUBLANES)
    from_loss = target is not None

    def body(*refs):
        if from_loss:
            y_ref, gate_ref, pg_ref, xin_ref, t_ref, dy_ref, dgate_ref, dpg_ref, dxo_ref, loss_ref = refs
        else:
            y_ref, gate_ref, pg_ref, dxo_in_ref, dy_ref, dgate_ref, dpg_ref = refs
        yv = y_ref[...]
        rp = lax.rsqrt(_rowmean(yv * yv) + EPS)
        yh = yv * rp
        fn = yh * pg_ref[...]
        gate = gate_ref[...]
        if from_loss:
            err = xin_ref[...] + gate * fn - t_ref[...]
            dxo = err * (1.0 / d)
            dxo_ref[...] = dxo
            part = 0.5 * jnp.sum(_rowmean(err * err), axis=0, keepdims=True)
            _accumulate(loss_ref, jnp.broadcast_to(part, loss_ref.shape))
        else:
            dxo = dxo_in_ref[...]
        _accumulate(dgate_ref, _colsum(dxo * fn))
        dfn = dxo * gate
        _accumulate(dpg_ref, _colsum(dfn * yh))
        dyh = dfn * pg_ref[...]
        dy_ref[...] = (rp * (dyh - yh * _rowmean(dyh * yh))).astype(BF16)

    blk = pl.BlockSpec((tb, d), lambda i: (i, 0))
    in_specs = [blk, _row_spec(d), _row_spec(d)]
    out_specs = [blk, _row_spec(d), _row_spec(d)]
    out_shape = [jax.ShapeDtypeStruct((s, d), BF16), jax.ShapeDtypeStruct((1, d), F32),
                 jax.ShapeDtypeStruct((1, d), F32)]
    if from_loss:
        operands = (y, gate, pg, xin, target)
        in_specs += [blk, blk]
        out_specs += [blk, _row_spec(LANES)]
        out_shape += [jax.ShapeDtypeStruct((s, d), F32), jax.ShapeDtypeStruct((1, LANES), F32)]
    else:
        operands = (y, gate, pg, dxo)
        in_specs += [blk]
    return pl.pallas_call(
        body, name=name, grid=(s // tb,), in_specs=in_specs, out_specs=out_specs, out_shape=out_shape,
        compiler_params=_params(),
    )(*operands)


def _prenorm_bwd(xin, dh, dres, g, scale, name, comm=None):
    s, d = xin.shape
    tb = _div_tile(s, 256, SUBLANES)

    def body(x_ref, dh_ref, dres_ref, g_ref, sc_ref, dx_ref, dshift_ref, dscale_ref, dg_ref):
        xv = x_ref[...]
        r = lax.rsqrt(_rowmean(xv * xv) + EPS)
        xn = xv * r
        dh = dh_ref[...]
        g1 = g_ref[...]
        s1 = 1.0 + sc_ref[...]
        _accumulate(dshift_ref, _colsum(dh))
        _accumulate(dscale_ref, _colsum(dh * xn * g1))
        _accumulate(dg_ref, _colsum(dh * xn * s1))
        dxn = dh * g1 * s1
        dx_ref[...] = dres_ref[...] + r * (dxn - xn * _rowmean(dxn * xn))

    blk = pl.BlockSpec((tb, d), lambda i: (i, 0))
    return _call(
        body, (xin, dh, dres, g, scale), comm, name=name, grid=(s // tb,),
        in_specs=[blk, blk, blk, _row_spec(d), _row_spec(d)],
        out_specs=[blk, _row_spec(d), _row_spec(d), _row_spec(d)],
        out_shape=[jax.ShapeDtypeStruct((s, d), F32)] + [jax.ShapeDtypeStruct((1, d), F32)] * 3)


def _merge(z_big, y_a, y_b, name):
    s, d = y_a.shape
    tb = _div_tile(s, 256, SUBLANES)

    def body(zg_ref, ya_ref, yb_ref, o_ref):
        ga, gb = zg_ref[:, :d].astype(F32), zg_ref[:, d:].astype(F32)
        o_ref[...] = (_sigmoid(ga) * ya_ref[...].astype(F32) + _sigmoid(gb) * yb_ref[...].astype(F32)).astype(BF16)

    blk = pl.BlockSpec((tb, d), lambda i: (i, 0))
    return pl.pallas_call(
        body, name=name, grid=(s // tb,), in_specs=[pl.BlockSpec((tb, 2 * d), lambda i: (i, 1)), blk, blk],
        out_specs=blk, out_shape=jax.ShapeDtypeStruct((s, d), BF16), compiler_params=_params(),
    )(z_big, y_a, y_b)


def _merge_bwd(dmerged, z_big, y_a, y_b, name):
    s, d = y_a.shape
    tb = _div_tile(s, 256, SUBLANES)

    def body(dm_ref, zg_ref, ya_ref, yb_ref, dya_ref, dyb_ref, dz_ref):
        dm = dm_ref[...].astype(F32)
        sa, sb = _sigmoid(zg_ref[:, :d].astype(F32)), _sigmoid(zg_ref[:, d:].astype(F32))
        dya_ref[...] = (dm * sa).astype(BF16)
        dyb_ref[...] = (dm * sb).astype(BF16)
        dz_ref[:, :d] = (dm * ya_ref[...].astype(F32) * sa * (1.0 - sa)).astype(BF16)
        dz_ref[:, d:] = (dm * yb_ref[...].astype(F32) * sb * (1.0 - sb)).astype(BF16)

    blk = pl.BlockSpec((tb, d), lambda i: (i, 0))
    wide = pl.BlockSpec((tb, 2 * d), lambda i: (i, 1))
    return pl.pallas_call(
        body, name=name, grid=(s // tb,), in_specs=[blk, wide, blk, blk], out_specs=[blk, blk, wide],
        out_shape=[jax.ShapeDtypeStruct((s, d), BF16), jax.ShapeDtypeStruct((s, d), BF16),
                   jax.ShapeDtypeStruct((s, 4 * d), BF16)],
        compiler_params=_params(),
    )(dmerged, z_big, y_a, y_b)


def _causal_mask(ch):
    q = lax.broadcasted_iota(jnp.int32, (ch, ch), 0)
    p = lax.broadcasted_iota(jnp.int32, (ch, ch), 1)
    return (p <= q).astype(F32)


def _gmlp_norm(zc, lng, lnb, gw):
    u_pre, v_pre = zc[:, :gw], zc[:, gw:]
    vg = _gelu(v_pre)
    mu = _rowmean(vg)
    cen = vg - mu
    rstd = lax.rsqrt(_rowmean(cen * cen) + EPS)
    vhat = cen * rstd
    return u_pre, v_pre, _gelu(u_pre), vhat, rstd, vhat * lng + lnb


def _gmlp_fwd(z_big, ln_g, ln_b, w_s, b_s_t, name):
    s = z_big.shape[0]
    groups, ch, _ = w_s.shape
    gw = ln_g.shape[1]
    gd = gw // groups

    def body(z_ref, lng_ref, lnb_ref, ws_ref, bt_ref, a_ref):
        _, _, u, _, _, vn = _gmlp_norm(z_ref[...].astype(F32), lng_ref[...], lnb_ref[...], gw)
        mask = _causal_mask(ch)
        for g in range(groups):
            cols = slice(g * gd, (g + 1) * gd)
            wm = (ws_ref[g] * mask).astype(BF16)
            mixed = _dot(wm, vn[:, cols].astype(BF16), NN) + bt_ref[:, g:g + 1]
            a_ref[:, cols] = (u[:, cols] * mixed).astype(BF16)

    return pl.pallas_call(
        body, name=name, grid=(s // ch,),
        in_specs=[pl.BlockSpec((ch, 2 * gw), lambda n: (n, 0)), _row_spec(gw), _row_spec(gw),
                  pl.BlockSpec((groups, ch, ch), lambda n: (0, 0, 0)), pl.BlockSpec((ch, groups), lambda n: (0, 0))],
        out_specs=pl.BlockSpec((ch, gw), lambda n: (n, 0)),
        out_shape=jax.ShapeDtypeStruct((s, gw), BF16), compiler_params=_params(),
    )(z_big, ln_g, ln_b, w_s, b_s_t)


def _gmlp_bwd(z_big, da, dz_big, ln_g, ln_b, w_s, b_s_t, name, comm=None):
    s = z_big.shape[0]
    groups, ch, _ = w_s.shape
    gw = ln_g.shape[1]
    gd = gw // groups

    def body(z_ref, da_ref, dzin_ref, lng_ref, lnb_ref, ws_ref, bt_ref, dz_ref, gws_ref, gbt_ref, glng_ref, glnb_ref,
             vg_ref, dvh_ref):
        del dzin_ref
        mask = _causal_mask(ch)
        lane = lax.broadcasted_iota(jnp.int32, (ch, LANES), 1)
        group_cols = [slice(g * gd, (g + 1) * gd) for g in range(groups)]
        rowsum = lambda v: jnp.sum(v, axis=1, keepdims=True)

        @pl.when(pl.program_id(0) == 0)
        def _():
            for ref in (gws_ref, gbt_ref, glng_ref, glnb_ref):
                ref[...] = jnp.zeros(ref.shape, F32)

        total = jnp.zeros((ch, 1), F32)
        for cols in group_cols:
            vg = _gelu(z_ref[:, gw + cols.start:gw + cols.stop].astype(F32))
            vg_ref[:, cols] = vg
            total = total + rowsum(vg)
        mu = total * (1.0 / gw)
        total = jnp.zeros((ch, 1), F32)
        for cols in group_cols:
            cen = vg_ref[:, cols] - mu
            total = total + rowsum(cen * cen)
        rstd = lax.rsqrt(total * (1.0 / gw) + EPS)
        m1, m2, gb = jnp.zeros((ch, 1), F32), jnp.zeros((ch, 1), F32), jnp.zeros((ch, LANES), F32)
        for g, cols in enumerate(group_cols):
            vhat = (vg_ref[:, cols] - mu) * rstd
            vn_g = (vhat * lng_ref[:, cols] + lnb_ref[:, cols]).astype(BF16)
            wm = (ws_ref[g] * mask).astype(BF16)
            mixed = _dot(wm, vn_g, NN) + bt_ref[:, g:g + 1]
            u_pre, da_g = z_ref[:, cols].astype(F32), da_ref[:, cols].astype(F32)
            dz_ref[:, cols] = (da_g * mixed * _gelu_grad(u_pre)).astype(BF16)
            dmixed = da_g * _gelu(u_pre)
            dm16 = dmixed.astype(BF16)
            dvn = _dot(wm, dm16, TN)
            gws_ref[g] += _dot(dm16, vn_g, NT) * mask
            gb = gb + jnp.where(lane == g, rowsum(dmixed), 0.0)
            glnb_ref[:, cols] += _colsum(dvn)
            glng_ref[:, cols] += _colsum(dvn * vhat)
            dvh = dvn * lng_ref[:, cols]
            dvh_ref[:, cols] = dvh
            m1, m2 = m1 + rowsum(dvh), m2 + rowsum(dvh * vhat)
        gbt_ref[...] += gb
        m1, m2 = m1 * (1.0 / gw), m2 * (1.0 / gw)
        for cols in group_cols:
            vhat = (vg_ref[:, cols] - mu) * rstd
            dvg = rstd * (dvh_ref[:, cols] - m1 - vhat * m2)
            v_pre = z_ref[:, gw + cols.start:gw + cols.stop].astype(F32)
            dz_ref[:, gw + cols.start:gw + cols.stop] = (dvg * _gelu_grad(v_pre)).astype(BF16)

    zspec = pl.BlockSpec((ch, 2 * gw), lambda n: (n, 0))
    return _call(
        body, (z_big, da, dz_big, ln_g, ln_b, w_s, b_s_t), comm, name=name, grid=(s // ch,),
        in_specs=[zspec, pl.BlockSpec((ch, gw), lambda n: (n, 0)), pl.BlockSpec(memory_space=HBM),
                  _row_spec(gw), _row_spec(gw), pl.BlockSpec((groups, ch, ch), lambda n: (0, 0, 0)),
                  pl.BlockSpec((ch, groups), lambda n: (0, 0))],
        out_specs=[zspec, pl.BlockSpec((groups, ch, ch), lambda n: (0, 0, 0)),
                   pl.BlockSpec((ch, LANES), lambda n: (0, 0)), _row_spec(gw), _row_spec(gw)],
        out_shape=[jax.ShapeDtypeStruct(dz_big.shape, BF16), jax.ShapeDtypeStruct((groups, ch, ch), F32),
                   jax.ShapeDtypeStruct((ch, LANES), F32), jax.ShapeDtypeStruct((1, gw), F32),
                   jax.ShapeDtypeStruct((1, gw), F32)],
        scratch_shapes=[pltpu.VMEM((ch, gw), F32)] * 2, input_output_aliases={2: 0})


def _mla_prep(z_lat, q_g, kv_g, rope_k, name):
    s, latw = z_lat.shape
    ql, kvl = q_g.shape[1], kv_g.shape[1]
    tb = _div_tile(s, 256, SUBLANES)

    def body(z_ref, qg_ref, kvg_ref, t_ref, qn_ref, kvn_ref, kr_ref):
        q = z_ref[:, :ql]
        qn_ref[...] = ((q * lax.rsqrt(_rowmean(q * q) + EPS)) * qg_ref[...]).astype(BF16)
        kv = z_ref[:, ql:ql + kvl]
        kvn_ref[...] = ((kv * lax.rsqrt(_rowmean(kv * kv) + EPS)) * kvg_ref[...]).astype(BF16)
        kk = z_ref[:, ql + kvl:] * t_ref[...]
        kr_ref[...] = (kk + pltpu.roll(kk, ROPE, axis=1)).astype(BF16)

    return pl.pallas_call(
        body, name=name, grid=(s // tb,),
        in_specs=[pl.BlockSpec((tb, latw), lambda i: (i, 0)), _row_spec(ql), _row_spec(kvl),
                  pl.BlockSpec((tb, 2 * ROPE), lambda i: (i, 0))],
        out_specs=[pl.BlockSpec((tb, ql), lambda i: (i, 0)), pl.BlockSpec((tb, kvl), lambda i: (i, 0)),
                   pl.BlockSpec((tb, 2 * ROPE), lambda i: (i, 0))],
        out_shape=[jax.ShapeDtypeStruct((s, ql), BF16), jax.ShapeDtypeStruct((s, kvl), BF16),
                   jax.ShapeDtypeStruct((s, 2 * ROPE), BF16)],
        compiler_params=_params(),
    )(z_lat, q_g, kv_g, rope_k)


def _attn_fwd(q, kv, kr, heads, name, comm=None):
    s = q.shape[0]
    t = _div_tile(s, 512)
    nb = s // t
    hp = 2 if heads % 2 == 0 else 1

    def body(q_ref, k_ref, kr_ref, v_ref, o_ref, lse_ref, m_ref, l_ref, acc_ref):
        i, j = pl.program_id(1), pl.program_id(2)

        @pl.when(j == 0)
        def _():
            m_ref[...] = jnp.full(m_ref.shape, -1e30, F32)
            l_ref[...] = jnp.zeros(l_ref.shape, F32)
            acc_ref[...] = jnp.zeros(acc_ref.shape, F32)

        def update(h, rows, n_keys, on_diagonal):
            vc = slice(h * VHEAD, (h + 1) * VHEAD)
            k_full = jnp.concatenate([k_ref[:n_keys, h * NOPE:(h + 1) * NOPE], kr_ref[:n_keys, :]], axis=1)
            sc = _dot(q_ref[rows, h * HEAD_W:(h + 1) * HEAD_W], k_full, NT)
            if on_diagonal:
                row_pos = rows.start + lax.broadcasted_iota(jnp.int32, sc.shape, 0)
                sc = jnp.where(lax.broadcasted_iota(jnp.int32, sc.shape, 1) <= row_pos, sc, -1e30)
            m_old = m_ref[h, rows, :]
            m_new = jnp.maximum(m_old, jnp.max(sc, axis=-1, keepdims=True))
            p = jnp.exp(sc - m_new)
            alpha = jnp.exp(m_old - m_new)
            l_new = alpha * l_ref[h, rows, :] + jnp.sum(p, axis=-1, keepdims=True)
            acc = alpha * acc_ref[rows, vc] + _dot(p.astype(BF16), v_ref[:n_keys, vc], NN)
            if on_diagonal:
                o_ref[rows, vc] = (acc / l_new).astype(BF16)
                lse_ref[h, rows, :] = jnp.broadcast_to(m_new + jnp.log(l_new), (rows.stop - rows.start, LANES))
            else:
                m_ref[h, rows, :], l_ref[h, rows, :], acc_ref[rows, vc] = m_new, l_new, acc

        def below_diagonal():
            for h in range(hp):
                update(h, slice(0, t), t, False)

        def on_diagonal():
            for h in range(hp):
                update(h, slice(0, t // 2), t // 2, True)
                update(h, slice(t // 2, t), t, True)

        pl.when(j < i)(below_diagonal)
        pl.when(j == i)(on_diagonal)

    kidx = lambda off: (lambda h, i, j: (jnp.minimum(i, j), off(h)))
    return _call(
        body, (q, kv, kr, kv), comm, name=name, grid=(heads // hp, nb, nb),
        in_specs=[pl.BlockSpec((t, hp * HEAD_W), lambda h, i, j: (i, h)),
                  pl.BlockSpec((t, hp * NOPE), kidx(lambda h: h)),
                  pl.BlockSpec((t, 2 * ROPE), kidx(lambda h: 0)),
                  pl.BlockSpec((t, hp * VHEAD), kidx(lambda h: heads // hp + h))],
        out_specs=[pl.BlockSpec((t, hp * VHEAD), lambda h, i, j: (i, h)),
                   pl.BlockSpec((hp, t, LANES), lambda h, i, j: (h, i, 0))],
        out_shape=[jax.ShapeDtypeStruct((s, heads * VHEAD), BF16), jax.ShapeDtypeStruct((heads, s, LANES), F32)],
        scratch_shapes=[pltpu.VMEM((hp, t, 1), F32), pltpu.VMEM((hp, t, 1), F32), pltpu.VMEM((t, hp * VHEAD), F32)])


def _attn_bwd(q, kv, kr, o, do, lse, heads, name, comm=None):
    s = q.shape[0]
    t = _div_tile(s, 512)
    nb = s // t
    hp = 2 if heads % 2 == 0 else 1

    def body(q_ref, k_ref, kr_ref, v_ref, o_ref, do_ref, lse_ref, dq_ref, dk_ref, dv_ref, dk_acc, dv_acc):
        j, i = pl.program_id(1), pl.program_id(2)

        @pl.when(jnp.logical_and(j == 0, i == 0))
        def _():
            dq_ref[...] = jnp.zeros(dq_ref.shape, F32)

        def update(h, rows, n_keys, on_diagonal, assign):
            qc, kc, vc = (slice(h * w, (h + 1) * w) for w in (HEAD_W, NOPE, VHEAD))
            n_rows = rows.stop - rows.start
            qv, do_v = q_ref[rows, qc], do_ref[rows, vc]
            k_full = jnp.concatenate([k_ref[:n_keys, kc], kr_ref[:n_keys, :]], axis=1)
            sc = _dot(qv, k_full, NT)
            if on_diagonal:
                row_pos = rows.start + lax.broadcasted_iota(jnp.int32, sc.shape, 0)
                sc = jnp.where(lax.broadcasted_iota(jnp.int32, sc.shape, 1) <= row_pos, sc, -1e30)
            p = jnp.exp(sc - lse_ref[h, rows, :1])
            dp = _dot(do_v, v_ref[:n_keys, vc], NT)
            delta = jnp.sum(do_v.astype(F32) * o_ref[rows, vc].astype(F32), axis=-1, keepdims=True)
            ds = (p * (dp - delta)).astype(BF16)
            dq_ref[pl.ds(pl.multiple_of(i * t + rows.start, n_rows), n_rows), qc] += _dot(ds, k_full, NN)
            dv_part, dk_part = _dot(p.astype(BF16), do_v, TN), _dot(ds, qv, TN)
            if assign:
                dv_acc[:n_keys, vc], dk_acc[:n_keys, qc] = dv_part, dk_part
            else:
                dv_acc[:n_keys, vc] += dv_part
                dk_acc[:n_keys, qc] += dk_part

        def on_diagonal():
            for h in range(hp):
                update(h, slice(t // 2, t), t, True, True)
                update(h, slice(0, t // 2), t // 2, True, False)

        def below_diagonal():
            for h in range(hp):
                update(h, slice(0, t), t, False, False)

        pl.when(i == j)(on_diagonal)
        pl.when(i > j)(below_diagonal)

        @pl.when(i == nb - 1)
        def _():
            dk_ref[...] = dk_acc[...].astype(BF16)
            dv_ref[...] = dv_acc[...].astype(BF16)

    qidx = lambda h, j, i: (jnp.maximum(i, j), h)
    return _call(
        body, (q, kv, kr, kv, o, do, lse), comm, name=name, grid=(heads // hp, nb, nb),
        in_specs=[pl.BlockSpec((t, hp * HEAD_W), qidx),
                  pl.BlockSpec((t, hp * NOPE), lambda h, j, i: (j, h)),
                  pl.BlockSpec((t, 2 * ROPE), lambda h, j, i: (j, 0)),
                  pl.BlockSpec((t, hp * VHEAD), lambda h, j, i: (j, heads // hp + h)),
                  pl.BlockSpec((t, hp * VHEAD), qidx), pl.BlockSpec((t, hp * VHEAD), qidx),
                  pl.BlockSpec((hp, t, LANES), lambda h, j, i: (h, jnp.maximum(i, j), 0))],
        out_specs=[pl.BlockSpec((s, hp * HEAD_W), lambda h, j, i: (0, h)),
                   pl.BlockSpec((t, hp * HEAD_W), lambda h, j, i: (j, h)),
                   pl.BlockSpec((t, hp * VHEAD), lambda h, j, i: (j, h))],
        out_shape=[jax.ShapeDtypeStruct((s, heads * HEAD_W), F32), jax.ShapeDtypeStruct((s, heads * HEAD_W), BF16),
                   jax.ShapeDtypeStruct((s, heads * VHEAD), BF16)],
        scratch_shapes=[pltpu.VMEM((t, hp * HEAD_W), F32), pltpu.VMEM((t, hp * VHEAD), F32)])


def _mla_bwd_mid(dq, dk, dv, rope_q, rope_k, heads, name):
    s = dq.shape[0]
    tb = _div_tile(s, 256, SUBLANES)

    def body(dq_ref, dk_ref, dv_ref, tq_ref, tk_ref, dqb_ref, dkv_ref, dkk_ref):
        tq = tq_ref[...]
        dkr = jnp.zeros((tb, 2 * ROPE), F32)
        for h in range(heads):
            cols = slice(h * HEAD_W, (h + 1) * HEAD_W)
            dqb_ref[:, cols] = (dq_ref[:, cols] * tq).astype(BF16)
            dkv_ref[:, h * NOPE:(h + 1) * NOPE] = dk_ref[:, h * HEAD_W:h * HEAD_W + NOPE]
            dkr = dkr + dk_ref[:, h * HEAD_W + NOPE:(h + 1) * HEAD_W].astype(F32)
        dkv_ref[:, heads * NOPE:] = dv_ref[...]
        dkk_ref[...] = (dkr + pltpu.roll(dkr, ROPE, axis=1)) * tk_ref[...]

    wq, wv = heads * HEAD_W, heads * VHEAD
    return pl.pallas_call(
        body, name=name, grid=(s // tb,),
        in_specs=[pl.BlockSpec((tb, wq), lambda i: (i, 0)), pl.BlockSpec((tb, wq), lambda i: (i, 0)),
                  pl.BlockSpec((tb, wv), lambda i: (i, 0)), pl.BlockSpec((tb, HEAD_W), lambda i: (i, 0)),
                  pl.BlockSpec((tb, 2 * ROPE), lambda i: (i, 0))],
        out_specs=[pl.BlockSpec((tb, wq), lambda i: (i, 0)), pl.BlockSpec((tb, heads * NOPE + wv), lambda i: (i, 0)),
                   pl.BlockSpec((tb, 2 * ROPE), lambda i: (i, 0))],
        out_shape=[jax.ShapeDtypeStruct((s, wq), BF16), jax.ShapeDtypeStruct((s, heads * NOPE + wv), BF16),
                   jax.ShapeDtypeStruct((s, 2 * ROPE), F32)],
        compiler_params=_params(),
    )(dq, dk, dv, rope_q, rope_k)


def _mla_bwd_post(z_lat, dqn, dkvn, dkk, q_g, kv_g, name):
    s, latw = z_lat.shape
    ql, kvl = q_g.shape[1], kv_g.shape[1]
    tb = _div_tile(s, 256, SUBLANES)

    def norm_bwd(xv, dn, g, dg_ref):
        r = lax.rsqrt(_rowmean(xv * xv) + EPS)
        xh = xv * r
        _accumulate(dg_ref, _colsum(dn * xh))
        dxh = dn * g
        return r * (dxh - xh * _rowmean(dxh * xh))

    def body(z_ref, dqn_ref, dkvn_ref, dkk_ref, qg_ref, kvg_ref, dz_ref, gq_ref, gkv_ref):
        dz_ref[:, :ql] = norm_bwd(z_ref[:, :ql], dqn_ref[...], qg_ref[...], gq_ref).astype(BF16)
        dz_ref[:, ql:ql + kvl] = norm_bwd(z_ref[:, ql:ql + kvl], dkvn_ref[...], kvg_ref[...], gkv_ref).astype(BF16)
        dz_ref[:, ql + kvl:] = dkk_ref[...].astype(BF16)

    return pl.pallas_call(
        body, name=name, grid=(s // tb,),
        in_specs=[pl.BlockSpec((tb, latw), lambda i: (i, 0)), pl.BlockSpec((tb, ql), lambda i: (i, 0)),
                  pl.BlockSpec((tb, kvl), lambda i: (i, 0)), pl.BlockSpec((tb, 2 * ROPE), lambda i: (i, 0)),
                  _row_spec(ql), _row_spec(kvl)],
        out_specs=[pl.BlockSpec((tb, latw), lambda i: (i, 0)), _row_spec(ql), _row_spec(kvl)],
        out_shape=[jax.ShapeDtypeStruct((s, latw), BF16), jax.ShapeDtypeStruct((1, ql), F32),
                   jax.ShapeDtypeStruct((1, kvl), F32)],
        compiler_params=_params(),
    )(z_lat, dqn, dkvn, dkk, q_g, kv_g)


CONV_ROWS = 128
CONV_HALO = 16


def _row_steps(n_rows, step):
    step(0, True)
    if n_rows > CONV_ROWS:
        def later(i, carry):
            step(pl.multiple_of(i * CONV_ROWS, CONV_ROWS), False)
            return carry
        lax.fori_loop(1, n_rows // CONV_ROWS, later, 0)


def _conv_taps(pre_ref, r0, first):
    if first:
        win = jnp.concatenate([jnp.zeros((CONV_HALO, pre_ref.shape[1]), F32), pre_ref[0:CONV_ROWS, :].astype(F32)])
    else:
        win = pre_ref[pl.ds(pl.multiple_of(r0 - CONV_HALO, CONV_HALO), CONV_ROWS + CONV_HALO), :].astype(F32)
    return win[CONV_HALO:], pltpu.roll(win, 1, axis=0)[CONV_HALO:], pltpu.roll(win, 2, axis=0)[CONV_HALO:]


def _conv(taps, w_ref, b_ref):
    return w_ref[2:3, :] * taps[0] + w_ref[1:2, :] * taps[1] + w_ref[0:1, :] * taps[2] + b_ref[...]


def _conv_fwd(up_pre, conv_w, conv_b, name):
    s, ff2 = up_pre.shape
    ff = ff2 // 2
    tc = _div_tile(ff, 256)
    nb = ff // tc
    assert s % CONV_ROWS == 0

    def body(pg_ref, pv_ref, wg_ref, wv_ref, bg_ref, bv_ref, act_ref):
        def step(r0, first):
            gate = _conv(_conv_taps(pg_ref, r0, first), wg_ref, bg_ref)
            val = _conv(_conv_taps(pv_ref, r0, first), wv_ref, bv_ref)
            act_ref[pl.ds(r0, CONV_ROWS), :] = (gate * _sigmoid(gate) * val).astype(BF16)

        _row_steps(s, step)

    def col(rows, off):
        return pl.BlockSpec((rows, tc), lambda j: (0, j + off))

    return pl.pallas_call(
        body, name=name, grid=(nb,),
        in_specs=[col(s, 0), col(s, nb), col(CONV_TAPS, 0), col(CONV_TAPS, nb), col(1, 0), col(1, nb)],
        out_specs=col(s, 0), out_shape=jax.ShapeDtypeStruct((s, ff), BF16), compiler_params=_params(),
    )(up_pre, up_pre, conv_w, conv_w, conv_b, conv_b)


def _conv_bwd(up_pre, dact, conv_w, conv_b, name, comm=None):
    s, ff2 = up_pre.shape
    ff = ff2 // 2
    tc = _div_tile(ff, 256)
    nb = ff // tc
    assert s % CONV_ROWS == 0

    def body(pg_ref, pv_ref, da_ref, wg_ref, wv_ref, bg_ref, bv_ref, dup_ref, gwg_ref, gwv_ref, gbg_ref, gbv_ref,
             dxg_ref, dxv_ref):
        for ref in (gwg_ref, gwv_ref, gbg_ref, gbv_ref):
            ref[...] = jnp.zeros(ref.shape, F32)
        for ref in (dxg_ref, dxv_ref):
            ref[s:s + SUBLANES, :] = jnp.zeros((SUBLANES, tc), F32)

        def sums(taps, dx, gw_ref, gb_ref):
            gb_ref[...] += _colsum(dx)
            for k in range(CONV_TAPS):
                gw_ref[k:k + 1, :] += _colsum(dx * taps[CONV_TAPS - 1 - k])

        def forward(r0, first):
            rows = pl.ds(r0, CONV_ROWS)
            taps_g, taps_v = _conv_taps(pg_ref, r0, first), _conv_taps(pv_ref, r0, first)
            gate, val = _conv(taps_g, wg_ref, bg_ref), _conv(taps_v, wv_ref, bv_ref)
            da = da_ref[rows, :].astype(F32)
            sg = _sigmoid(gate)
            dxv, dxg = da * gate * sg, da * val * sg * (1.0 + gate * (1.0 - sg))
            dxv_ref[rows, :], dxg_ref[rows, :] = dxv, dxg
            sums(taps_v, dxv, gwv_ref, gbv_ref)
            sums(taps_g, dxg, gwg_ref, gbg_ref)

        def backward(r0, first):
            del first
            n = CONV_ROWS + SUBLANES
            for dx_ref, w_ref, out_ref in ((dxg_ref, wg_ref, dup_ref.at[0]), (dxv_ref, wv_ref, dup_ref.at[1])):
                win = dx_ref[pl.ds(r0, n), :]
                ahead1 = pltpu.roll(win, n - 1, axis=0)[:CONV_ROWS]
                ahead2 = pltpu.roll(win, n - 2, axis=0)[:CONV_ROWS]
                out_ref[pl.ds(r0, CONV_ROWS), :] = (w_ref[2:3, :] * win[:CONV_ROWS] + w_ref[1:2, :] * ahead1
                                                    + w_ref[0:1, :] * ahead2).astype(BF16)

        _row_steps(s, forward)
        _row_steps(s, backward)

    def col(rows, off):
        return pl.BlockSpec((rows, tc), lambda j: (0, j + off))

    return _call(
        body, (up_pre, up_pre, dact, conv_w, conv_w, conv_b, conv_b), comm, name=name, grid=(nb,),
        in_specs=[col(s, 0), col(s, nb), col(s, 0), col(CONV_TAPS, 0), col(CONV_TAPS, nb), col(1, 0), col(1, nb)],
        out_specs=[pl.BlockSpec((2, s, tc), lambda j: (0, 0, j)), col(CONV_TAPS, 0), col(CONV_TAPS, 0),
                   col(1, 0), col(1, 0)],
        out_shape=[jax.ShapeDtypeStruct((2, s, ff), BF16)] + [jax.ShapeDtypeStruct((CONV_TAPS, ff), F32)] * 2
        + [jax.ShapeDtypeStruct((1, ff), F32)] * 2,
        scratch_shapes=[pltpu.VMEM((s + SUBLANES, tc), F32)] * 2)


def _ada_fwd(c_all, w, b, name):
    nseq, d = c_all.shape
    na = w.shape[1]
    tn = _div_tile(na, 512)

    def body(c_ref, w_ref, b_ref, o_ref):
        cv = c_ref[...]
        sc = cv * _sigmoid(cv)
        o_ref[...] = jnp.dot(sc, w_ref[...], preferred_element_type=F32, precision=lax.Precision.HIGHEST) + b_ref[...]

    return pl.pallas_call(
        body, name=name, grid=(na // tn,),
        in_specs=[pl.BlockSpec((nseq, d), lambda j: (0, 0)), pl.BlockSpec((d, tn), lambda j: (0, j)),
                  pl.BlockSpec((1, tn), lambda j: (0, j))],
        out_specs=pl.BlockSpec((nseq, tn), lambda j: (0, j)),
        out_shape=jax.ShapeDtypeStruct((nseq, na), F32), compiler_params=_params(),
    )(c_all, w, b)


def _ada_bwd(c_all_t, dmod, name):
    d, nseq = c_all_t.shape
    na = dmod.shape[1]
    tm, tn = _div_tile(d, 512, SUBLANES), _div_tile(na, 1024)

    def body(c_ref, dm_ref, o_ref):
        cv = c_ref[...]
        o_ref[...] = jnp.dot(cv * _sigmoid(cv), dm_ref[...], preferred_element_type=F32,
                             precision=lax.Precision.HIGHEST)

    return pl.pallas_call(
        body, name=name, grid=(d // tm, na // tn),
        in_specs=[pl.BlockSpec((tm, nseq), lambda i, j: (i, 0)), pl.BlockSpec((nseq, tn), lambda i, j: (0, j))],
        out_specs=pl.BlockSpec((tm, tn), lambda i, j: (i, j)),
        out_shape=jax.ShapeDtypeStruct((d, na), F32), compiler_params=_params(),
    )(c_all_t, dmod)


def _adamw(w, g, m, v, name, comm=None, after=None):
    rows, cols = w.shape
    tb = _div_tile(rows, max(SUBLANES, (256 * 1024) // cols // SUBLANES * SUBLANES), SUBLANES)
    c1 = 1.0 / (1.0 - ADAM_B1 ** ADAM_STEP)
    c2 = 1.0 / (1.0 - ADAM_B2 ** ADAM_STEP)

    def body(*refs):
        w_ref, g_ref, m_ref, v_ref = refs[:4]
        d_ref, nm_ref, nv_ref = refs[-3:]
        gv = g_ref[...]
        nm = ADAM_B1 * m_ref[...] + (1.0 - ADAM_B1) * gv
        nv = ADAM_B2 * v_ref[...] + (1.0 - ADAM_B2) * (gv * gv)
        nm_ref[...] = nm
        nv_ref[...] = nv
        d_ref[...] = -ADAM_LR * ((nm * c1) / (jnp.sqrt(nv * c2) + ADAM_EPS) + ADAM_WD * w_ref[...])

    blk = pl.BlockSpec((tb, cols), lambda i: (i, 0))
    operands, in_specs = (w, g, m, v), [blk] * 4
    if after is not None:
        operands, in_specs = operands + (after,), in_specs + [pl.BlockSpec(after.shape, lambda i: (0, 0))]
    return _call(body, operands, comm, name=name, grid=(rows // tb,), in_specs=in_specs, out_specs=[blk] * 3,
                 out_shape=[jax.ShapeDtypeStruct((rows, cols), F32)] * 3)


def _sum_leading(parts, name, after=()):
    n, rows, cols = parts.shape
    tb = _div_tile(rows, 512, SUBLANES)

    def body(p_ref, *rest):
        o_ref = rest[-1]
        acc = p_ref[0]
        for k in range(1, n):
            acc = acc + p_ref[k]
        o_ref[...] = acc

    return pl.pallas_call(
        body, name=name, grid=(rows // tb,),
        in_specs=[pl.BlockSpec((n, tb, cols), lambda i: (0, i, 0))] + [pl.BlockSpec(memory_space=pl.ANY)] * len(after),
        out_specs=pl.BlockSpec((tb, cols), lambda i: (i, 0)),
        out_shape=jax.ShapeDtypeStruct((rows, cols), F32), compiler_params=_params(),
    )(parts, *after)


def _place():
    x, y, c = lax.axis_index("x"), lax.axis_index("y"), lax.axis_index("c")
    return x, y, c, [(1 - x, y), (x, 1 - y), (1 - x, 1 - y)]


def _all_gather(block, name):
    m_per, n = block.shape

    def body(x_ref, out_ref, send_sems, recv_sems, local_sem):
        x, y, c, chips = _place()
        me, sibling = (x, y, c), (x, y, 1 - c)

        def rows(px, py, pc):
            return out_ref.at[pl.ds((4 * px + 2 * py + pc) * m_per, m_per), :]

        def copy(k, blk, to, src=None):
            return pltpu.make_async_remote_copy(
                src_ref=rows(*blk) if src is None else src, dst_ref=rows(*blk), send_sem=send_sems.at[k],
                recv_sem=recv_sems.at[k], device_id=to, device_id_type=MESH)

        mine = pltpu.make_async_copy(x_ref, rows(*me), local_sem)
        mine.start()
        first = [copy(0, me, sibling, src=x_ref)]
        first += [copy(1 + j, me, (*chip, c), src=x_ref) for j, chip in enumerate(chips)]
        for cp in first:
            cp.start()
        passed = [copy(4 + j, (*chip, c), sibling) for j, chip in enumerate(chips)]
        for j, chip in enumerate(chips):
            copy(1 + j, (*chip, c), me).wait_recv()
            passed[j].start()
        copy(0, sibling, me).wait_recv()
        for j, chip in enumerate(chips):
            copy(4 + j, (*chip, 1 - c), me).wait_recv()
        for cp in first + passed:
            cp.wait_send()
        mine.wait()

    return pl.pallas_call(
        body, name=name, out_shape=jax.ShapeDtypeStruct((N_DEV * m_per, n), block.dtype),
        in_specs=[pl.BlockSpec(memory_space=pltpu.VMEM)], out_specs=pl.BlockSpec(memory_space=pltpu.VMEM),
        scratch_shapes=[pltpu.SemaphoreType.DMA((7,)), pltpu.SemaphoreType.DMA((7,)), pltpu.SemaphoreType.DMA],
        compiler_params=_params(),
    )(block)


def _hbm_specs(n):
    return [pl.BlockSpec(memory_space=HBM)] * n


def _part(ref, by_cols, half, quarter=None, lead=None):
    extent = ref.shape[-1] if by_cols else ref.shape[-2]
    size = extent // 2 if quarter is None else extent // 4
    first = half * (extent // 2) + (0 if quarter is None else quarter * size)
    tile = LANES if by_cols else 2 * SUBLANES
    span = pl.ds(pl.multiple_of(first, tile) if size % tile == 0 else first, size)
    index = (slice(None), span) if by_cols else (span, slice(None))
    return ref.at[index] if lead is None else ref.at[(lead,) + index]


def _half_rows(ref, half, lead=None):
    return _part(ref, False, half, lead=lead)


class _Comm:
    middle_at = 70

    def __init__(self, operands, out_shape, sem_dims, build, aliases=None):
        self.operands, self.out_shape, self.sem_dims = list(operands), list(out_shape), list(sem_dims)
        self.scratch = [pltpu.SemaphoreType.DMA(d) for d in sem_dims]
        self.build, self.aliases = build, dict(aliases or {})


class _SemGrid:
    def __init__(self, sems, dims):
        self.sems, self.dims, self.at = list(sems), tuple(dims), self

    def __getitem__(self, index):
        index = index if isinstance(index, tuple) else (index,)
        flat = 0
        for i, d in zip(index, self.dims):
            flat = flat * d + i
        return self.sems[flat]


def _call(body, operands, comm=None, *, name, grid, in_specs, out_specs, out_shape, scratch_shapes=(),
          input_output_aliases=None):
    aliases = dict(input_output_aliases or {})
    if comm is None:
        return pl.pallas_call(
            body, name=name, grid=grid, in_specs=in_specs, out_specs=out_specs, out_shape=out_shape,
            scratch_shapes=list(scratch_shapes), input_output_aliases=aliases, compiler_params=_params())(*operands)
    single = not isinstance(out_shape, (list, tuple))
    outs = [out_shape] if single else list(out_shape)
    ospecs = [out_specs] if single else list(out_specs)
    n_in, n_out, n_scr = len(operands), len(outs), len(scratch_shapes)
    c_in, c_out = len(comm.operands), len(comm.out_shape)
    for i, o in comm.aliases.items():
        aliases[n_in + i] = n_out + o

    def hosted(*refs):
        ins, c_ins = refs[:n_in], refs[n_in:n_in + c_in]
        o0 = n_in + c_in
        o_refs, c_outs = refs[o0:o0 + n_out], refs[o0 + n_out:o0 + n_out + c_out]
        s0 = o0 + n_out + c_out
        scr, sems = refs[s0:s0 + n_scr], refs[s0 + n_scr:]
        stages = comm.build(c_ins, c_outs, sems)
        step, n_steps = 0, 1
        for dim, size in enumerate(grid):
            step, n_steps = step * size + pl.program_id(dim), n_steps * size
        pl.when(step == 0)(stages[0])
        body(*ins, *o_refs, *scr)
        for stage in stages[1:-1]:
            pl.when(step == (n_steps * comm.middle_at) // 100)(stage)
        pl.when(step == n_steps - 1)(stages[-1])

    res = pl.pallas_call(
        hosted, name=name, grid=grid, in_specs=list(in_specs) + _hbm_specs(c_in),
        out_specs=ospecs + _hbm_specs(c_out), out_shape=outs + comm.out_shape,
        scratch_shapes=list(scratch_shapes) + comm.scratch, input_output_aliases=aliases,
        compiler_params=_params())(*operands, *comm.operands)
    return (res[0] if single else res[:n_out]), res[n_out:]


def _run_comm(comm, name):
    c_in, c_out = len(comm.operands), len(comm.out_shape)

    def body(*refs):
        for stage in comm.build(refs[:c_in], refs[c_in:c_in + c_out], refs[c_in + c_out:]):
            stage()

    return pl.pallas_call(
        body, name=name, in_specs=_hbm_specs(c_in), out_specs=_hbm_specs(c_out), out_shape=comm.out_shape,
        scratch_shapes=comm.scratch, input_output_aliases=comm.aliases, compiler_params=_params())(*comm.operands)


def _join_comms(comms):
    def build(in_refs, out_refs, sems):
        staged, i, o, k = [], 0, 0, 0
        for cm in comms:
            ni, no, ns = len(cm.operands), len(cm.out_shape), len(cm.sem_dims)
            staged.append(cm.build(in_refs[i:i + ni], out_refs[o:o + no], sems[k:k + ns]))
            i, o, k = i + ni, o + no, k + ns
        def run(fns):
            def stage():
                for fn in fns:
                    fn()
            return stage

        return (run([st[0] for st in staged]), run([fn for st in staged for fn in st[1:-1]]),
                run([st[-1] for st in staged]))

    aliases, i, o = {}, 0, 0
    for cm in comms:
        aliases.update({i + a: o + b for a, b in cm.aliases.items()})
        i, o = i + len(cm.operands), o + len(cm.out_shape)
    return _Comm(sum((cm.operands for cm in comms), []), sum((cm.out_shape for cm in comms), []),
                 sum((cm.sem_dims for cm in comms), []), build, aliases)


def _gather8_comm(block):
    def build(in_refs, out_refs, sems):
        (src,), (out,), (send_sems, recv_sems) = in_refs, out_refs, sems
        x, y, c, chips = _place()
        me, sibling = (x, y, c), (x, y, 1 - c)

        def copy(k, blk, to, own=False):
            dst = out.at[4 * blk[0] + 2 * blk[1] + blk[2]]
            return pltpu.make_async_remote_copy(
                src_ref=src if own else dst, dst_ref=dst, send_sem=send_sems.at[k], recv_sem=recv_sems.at[k],
                device_id=to, device_id_type=MESH)

        first = [copy(0, me, sibling, own=True)] + [copy(1 + j, me, (*chip, c), own=True)
                                                     for j, chip in enumerate(chips)]
        passed = [copy(4 + j, (*chip, c), sibling) for j, chip in enumerate(chips)]

        def start():
            for cp in first:
                cp.start()

        def middle():
            for j, chip in enumerate(chips):
                copy(1 + j, (*chip, c), me).wait_recv()
                passed[j].start()

        def finish():
            copy(0, sibling, me).wait_recv()
            for j, chip in enumerate(chips):
                copy(4 + j, (*chip, 1 - c), me).wait_recv()
            for cp in first + passed:
                cp.wait_send()

        return start, middle, finish

    return _Comm([block], [jax.ShapeDtypeStruct((N_DEV,) + block.shape, block.dtype)], [(7,), (7,)], build)


def _gather_comm(shards, by_cols=()):
    nw = len(shards)

    def build(in_refs, out_refs, sems):
        send_sems, recv_sems = sems
        x, y, c, chips = _place()
        me, sibling = (x, y, c), (x, y, 1 - c)
        across_x, across_y, diagonal = chips

        def copy(w, k, block, part, to, src=None):
            dst = _part(out_refs[w], w in by_cols, part[1], part[2] if part[0] else None, 2 * block[0] + block[1])
            return pltpu.make_async_remote_copy(
                src_ref=dst if src is None else src, dst_ref=dst, send_sem=send_sems.at[w, k],
                recv_sem=recv_sems.at[w, k], device_id=to, device_id_type=MESH)

        first = [copy(w, j, (x, y), (0, c), (*chip, c), src=_part(in_refs[w], w in by_cols, c))
                 for w in range(nw) for j, chip in enumerate((across_x, across_y))]
        passed = [[copy(w, 2, across_x, (1, c, 0), (*across_y, c)), copy(w, 3, across_y, (1, c, 1), (*across_x, c)),
                   copy(w, 4, across_x, (0, c), sibling), copy(w, 5, across_y, (0, c), sibling)] for w in range(nw)]
        last = [[copy(w, 6, diagonal, (1, c, 0), sibling), copy(w, 7, diagonal, (1, c, 1), sibling)]
                for w in range(nw)]

        def start():
            for cp in first:
                cp.start()

        def middle():
            for w in range(nw):
                copy(w, 0, across_x, (0, c), me).wait_recv()
                copy(w, 1, across_y, (0, c), me).wait_recv()
                for cp in passed[w]:
                    cp.start()

        def finish():
            for w in range(nw):
                copy(w, 2, diagonal, (1, c, 0), me).wait_recv()
                copy(w, 3, diagonal, (1, c, 1), me).wait_recv()
                for cp in last[w]:
                    cp.start()
            for w in range(nw):
                for k, block, part in ((4, across_x, (0, 1 - c)), (5, across_y, (0, 1 - c)),
                                       (6, diagonal, (1, 1 - c, 0)), (7, diagonal, (1, 1 - c, 1))):
                    copy(w, k, block, part, me).wait_recv()
            for cp in first + sum(passed, []) + sum(last, []):
                cp.wait_send()

        return start, middle, finish

    return _Comm(shards, [jax.ShapeDtypeStruct((N_CHIPS,) + w.shape, w.dtype) for w in shards],
                 [(nw, 8), (nw, 8)], build)


def _halved(shape, by_cols):
    return shape[:-1] + (shape[-1] // 2,) if by_cols else shape[:-2] + (shape[-2] // 2, shape[-1])


def _swap_comm(gs, by_cols=()):
    nw = len(gs)

    def build(in_refs, out_refs, sems):
        send_sems, recv_sems = sems
        x, y, c, _ = _place()
        cps = []
        for w in range(nw):
            cps.append(pltpu.make_async_remote_copy(
                src_ref=_part(in_refs[w], w in by_cols, 1 - c, lead=slice(None)), dst_ref=out_refs[w],
                send_sem=send_sems.at[w], recv_sem=recv_sems.at[w], device_id=(x, y, 1 - c), device_id_type=MESH))

        def start():
            for cp in cps:
                cp.start()

        def finish():
            for cp in cps:
                cp.wait()

        return start, finish

    return _Comm(gs, [jax.ShapeDtypeStruct(_halved(g.shape, w in by_cols), g.dtype) for w, g in enumerate(gs)],
                 [(nw,), (nw,)], build)


def _exchange_comm(s1s):
    nw = len(s1s)

    def build(in_refs, out_refs, sems):
        send_sems, recv_sems = sems
        x, y, c, chips = _place()
        cps = [pltpu.make_async_remote_copy(
            src_ref=in_refs[w].at[2 * chip[0] + chip[1]], dst_ref=out_refs[w].at[j], send_sem=send_sems.at[w, j],
            recv_sem=recv_sems.at[w, j], device_id=(*chip, c), device_id_type=MESH)
            for w in range(nw) for j, chip in enumerate(chips)]

        def start():
            for cp in cps:
                cp.start()

        def finish():
            for cp in cps:
                cp.wait()

        return start, finish

    return _Comm(s1s, [jax.ShapeDtypeStruct((N_CHIPS - 1,) + s.shape[1:], s.dtype) for s in s1s],
                 [(nw, 3), (nw, 3)], build)


def _size(dims):
    n = 1
    for d in dims:
        n *= d
    return n


def _sem_grids(comm, sem_refs):
    grids, pos = [], 0
    for dims in comm.sem_dims:
        grids.append(_SemGrid(sem_refs[pos:pos + _size(dims)], dims))
        pos += _size(dims)
    return grids


def _comm_split_start(comm, name, after=()):
    c_in, c_out = len(comm.operands), len(comm.out_shape)
    counts = [_size(d) for d in comm.sem_dims]
    n_sem = sum(counts)
    assert not comm.aliases

    def body(*refs):
        srcs, lands = refs[:c_in], refs[c_in:c_in + c_out]
        first_sem = c_in + c_out + len(after)
        start, _ = comm.build(srcs, lands, _sem_grids(comm, refs[first_sem:first_sem + n_sem]))
        start()
        refs[-1][...] = jnp.zeros(refs[-1].shape, refs[-1].dtype)

    lands = [pltpu.with_memory_space_constraint(lax.empty(o.shape, o.dtype), HBM) for o in comm.out_shape]
    srcs = [pltpu.with_memory_space_constraint(a, HBM) for a in comm.operands]
    res = pl.pallas_call(
        body, name=name, in_specs=_hbm_specs(c_in + c_out) + [pl.BlockSpec(memory_space=pl.ANY)] * len(after),
        out_specs=[pl.BlockSpec(memory_space=pltpu.SEMAPHORE)] * n_sem + _hbm_specs(c_in + c_out)
        + [pl.BlockSpec(memory_space=pltpu.VMEM)],
        out_shape=[pltpu.SemaphoreType.DMA(())] * n_sem + [pltpu.HBM(a.shape, a.dtype) for a in comm.operands]
        + [pltpu.HBM(o.shape, o.dtype) for o in comm.out_shape] + [jax.ShapeDtypeStruct((SUBLANES, LANES), F32)],
        input_output_aliases={i: n_sem + i for i in range(c_in + c_out)},
        compiler_params=_params(has_side_effects=pltpu.SideEffectType.DATAFLOW_SIDE_EFFECTING))(*srcs, *lands, *after)
    return res[:-1], res[-1]


def _comm_split_wait(comm, state, after, name):
    c_in, c_out, n_sem = len(comm.operands), len(comm.out_shape), sum(_size(d) for d in comm.sem_dims)
    sems, srcs, lands = state[:n_sem], state[n_sem:n_sem + c_in], state[n_sem + c_in:]

    def body(*refs):
        src_refs, land_refs = refs[:c_in], refs[c_in:c_in + c_out]
        _, finish = comm.build(src_refs, land_refs, _sem_grids(comm, refs[c_in + c_out:c_in + c_out + n_sem]))
        finish()

    sem_spec = pl.BlockSpec(memory_space=pltpu.SEMAPHORE)
    res = pl.pallas_call(
        body, name=name, in_specs=_hbm_specs(c_in + c_out) + [sem_spec] * n_sem + [pl.BlockSpec(memory_space=pl.ANY)],
        out_specs=_hbm_specs(c_in + c_out),
        out_shape=[pltpu.HBM(a.shape, a.dtype) for a in srcs] + [pltpu.HBM(o.shape, o.dtype) for o in lands],
        input_output_aliases={i: i for i in range(c_in + c_out)},
        compiler_params=_params(has_side_effects=pltpu.SideEffectType.DATAFLOW_SIDE_EFFECTING),
    )(*srcs, *lands, *sems, after)
    return res[:c_in], res[c_in:]


def _share_comm(fs, by_cols=()):
    nw = len(fs)

    def build(in_refs, out_refs, sems):
        del in_refs
        send_sems, recv_sems = sems
        x, y, c, _ = _place()

        def copy(w, half):
            part = _part(out_refs[w], w in by_cols, half)
            return pltpu.make_async_remote_copy(
                src_ref=part, dst_ref=part, send_sem=send_sems.at[w], recv_sem=recv_sems.at[w],
                device_id=(x, y, 1 - c), device_id_type=MESH)

        sends = [copy(w, c) for w in range(nw)]

        def start():
            for cp in sends:
                cp.start()

        def finish():
            for w in range(nw):
                copy(w, 1 - c).wait_recv()
            for cp in sends:
                cp.wait_send()

        return start, finish

    return _Comm(fs, [jax.ShapeDtypeStruct(f.shape, f.dtype) for f in fs],
                 [(nw,), (nw,)], build,
                 aliases={w: w for w in range(nw)})


def _add_sibling(g, r1, place, name, by_cols=False):
    nch, h, cols = r1.shape
    tr = _div_tile(h, 1024 if by_cols else 512, 2 * SUBLANES)
    nb = h // tr
    mine = (lambda k, i, p: (k, i, p[0])) if by_cols else (lambda k, i, p: (k, p[0] * nb + i, 0))

    def body(place_ref, g_ref, r_ref, o_ref):
        del place_ref
        o_ref[...] = (g_ref[...].astype(F32) + r_ref[...].astype(F32)).astype(BF16)

    spec = pltpu.PrefetchScalarGridSpec(
        num_scalar_prefetch=1, grid=(nch, nb),
        in_specs=[pl.BlockSpec((None, tr, cols), mine), pl.BlockSpec((None, tr, cols), lambda k, i, p: (k, i, 0))],
        out_specs=pl.BlockSpec((None, tr, cols), lambda k, i, p: (k, i, 0)))
    return pl.pallas_call(body, name=name, grid_spec=spec, out_shape=jax.ShapeDtypeStruct((nch, h, cols), BF16),
                          compiler_params=_params())(place, g, r1)


def _add_chips(s1, r2, place, name, by_cols=False):
    _, h, cols = s1.shape
    tr = _div_tile(h, 1024 if by_cols else 512, 2 * SUBLANES)
    nb = h // tr
    mine = (lambda i, p: (i, p[0])) if by_cols else (lambda i, p: (p[0] * nb + i, 0))
    whole = (h, 2 * cols) if by_cols else (2 * h, cols)

    def body(place_ref, s_ref, r_ref, o_ref):
        del place_ref
        acc = s_ref[...].astype(F32)
        for j in range(N_CHIPS - 1):
            acc = acc + r_ref[j].astype(F32)
        o_ref[...] = acc

    spec = pltpu.PrefetchScalarGridSpec(
        num_scalar_prefetch=1, grid=(nb,),
        in_specs=[pl.BlockSpec((None, tr, cols), lambda i, p: (p[1], i, 0)),
                  pl.BlockSpec((N_CHIPS - 1, tr, cols), lambda i, p: (0, i, 0))],
        out_specs=pl.BlockSpec((tr, cols), mine))
    return pl.pallas_call(body, name=name, grid_spec=spec, out_shape=jax.ShapeDtypeStruct(whole, F32),
                          compiler_params=_params())(place, s1, r2)


def _quarter_turn(m):
    h = m.shape[-1] // 2
    return jnp.concatenate([-m[..., h:], m[..., :h]], axis=-1)


def _quarter_turn_back(m):
    h = m.shape[-1] // 2
    return jnp.concatenate([m[..., h:], -m[..., :h]], axis=-1)


def _stack_rows(parts):
    out = lax.empty((sum(p.shape[0] for p in parts),) + parts[0].shape[1:], parts[0].dtype)
    row = 0
    for p in parts:
        out = lax.dynamic_update_slice(out, p, (row, 0))
        row += p.shape[0]
    return out


def _join_cols(sh):
    return jnp.concatenate([sh[k] for k in range(N_CHIPS)], axis=1)


def _split_cols(full):
    c = full.shape[1] // N_CHIPS
    return jnp.stack([full[:, k * c:(k + 1) * c] for k in range(N_CHIPS)])


def kernel(x, c, positions, w_ada, b_ada, pre_norm1_g, w_in, gm_ln_g, gm_ln_b, gm_w_s, gm_b_s, w_branch_a, q_norm_g, w_uq, kv_norm_g, w_ukv, w_branch_b, w_out, post_norm1_g, pre_norm2_g, w_up, conv_w, conv_b, w_down, post_norm2_g, loss_target, m_w_ada, m_b_ada, m_pre_norm1_g, m_w_in, m_gm_ln_g, m_gm_ln_b, m_gm_w_s, m_gm_b_s, m_w_branch_a, m_q_norm_g, m_w_uq, m_kv_norm_g, m_w_ukv, m_w_branch_b, m_w_out, m_post_norm1_g, m_pre_norm2_g, m_w_up, m_conv_w, m_conv_b, m_w_down, m_post_norm2_g, v_w_ada, v_b_ada, v_pre_norm1_g, v_w_in, v_gm_ln_g, v_gm_ln_b, v_gm_w_s, v_gm_b_s, v_w_branch_a, v_q_norm_g, v_w_uq, v_kv_norm_g, v_w_ukv, v_w_branch_b, v_w_out, v_post_norm1_g, v_pre_norm2_g, v_w_up, v_conv_w, v_conv_b, v_w_down, v_post_norm2_g):
    given = dict(locals())
    s, d = x.shape[1], x.shape[2]
    gw = gm_ln_g.shape[0]
    ql, kvl = q_norm_g.shape[0], kv_norm_g.shape[0]
    heads = N_CHIPS * w_uq.shape[1] // (NOPE + ROPE)
    ff = N_CHIPS * w_down.shape[0]
    assert gw == d and N_CHIPS * w_ukv.shape[1] == heads * (NOPE + VHEAD)
    ix, iy, ic = lax.axis_index("x"), lax.axis_index("y"), lax.axis_index("c")
    chip = 2 * ix + iy
    dev = 2 * chip + ic
    row = lambda v: v.reshape(1, -1)

    first = _all_gather(jnp.concatenate([jnp.pad(c, ((0, SUBLANES - 1), (0, 0))),
                                         jnp.pad(conv_w, ((0, SUBLANES - CONV_TAPS), (0, 0)))], axis=1), "gather_c")
    first = first.reshape(N_DEV, SUBLANES, d + conv_w.shape[1])
    c_all = first[:, 0, :d]
    conv_wf = first[::N_CORES, :CONV_TAPS, d:].transpose(1, 0, 2).reshape(CONV_TAPS, N_CHIPS * conv_w.shape[1])
    na = w_ada.shape[1]
    b_ada_mine = lax.dynamic_slice(b_ada, (chip * na,), (na,))
    mod_cols = _ada_fwd(c_all, w_ada, row(b_ada_mine), "ada_fwd")
    mod_all = _all_gather(mod_cols, "gather_mod").reshape(N_CHIPS, N_CORES, N_DEV, na)[:, 0]
    mod = lax.dynamic_index_in_dim(mod_all, dev, axis=1, keepdims=False).reshape(N_MOD, d)
    shift1, scale1, gate1, shift2, scale2, gate2 = (mod[i:i + 1] for i in range(N_MOD))

    mine = {n: (given[n].T if n == "w_in" else given[n]).astype(BF16) for n in BIG}
    def gather(names, middle_at=None):
        comm = _gather_comm([mine[n] for n in names], [i for i, n in enumerate(names) if n == "w_in"])
        if middle_at is not None:
            comm.middle_at = middle_at
        return comm

    whole = lambda n, g: lax.dynamic_update_slice(g, mine[n][None], (chip, 0, 0))
    rows4 = lambda sh4: sh4.reshape(-1, sh4.shape[2])
    wi_t = rows4(whole("w_in", _run_comm(gather(["w_in"]), "gather_w_in")[0]))
    o_q, o_kv, o_pe, o_ga = 2 * gw, 2 * gw + ql, 2 * gw + ql + kvl, 2 * gw + ql + kvl + ROPE
    w_in_big_t = _stack_rows([wi_t[:o_q], wi_t[o_ga:]])
    w_in_lat_t = _stack_rows([wi_t[o_q:o_ga], _quarter_turn(wi_t[o_pe:o_ga].T).T])

    inv = ROPE_THETA ** (-jnp.arange(0, ROPE, 2, dtype=F32) / ROPE)
    ang = positions[0].astype(F32)[:, None] * inv
    cos, sin = jnp.cos(ang), jnp.sin(ang)
    rope_k = jnp.concatenate([cos, cos, sin, sin], axis=1)
    softmax_scale = float(NOPE + ROPE) ** -0.5
    rope_q = jnp.concatenate([jnp.ones((s, NOPE), F32), rope_k], axis=1) * softmax_scale

    x2d, tgt = x[0], loss_target[0]
    g_pre1, g_post1, g_pre2, g_post2 = row(pre_norm1_g), row(post_norm1_g), row(pre_norm2_g), row(post_norm2_g)
    ln_g, ln_b, q_g, kv_g = row(gm_ln_g), row(gm_ln_b), row(q_norm_g), row(kv_norm_g)
    b_s_t = gm_b_s.T
    conv_bf = row(conv_b)

    h1 = _prenorm(x2d, g_pre1, scale1, shift1, "prenorm1")
    z_big, (g_uq, g_ukv, g_a) = _matmul(h1, w_in_big_t, mode="nt", out_dtype=BF16, name="mm_z_big", tm=s,
                                        comm=gather(["w_uq", "w_ukv", "w_branch_a"], middle_at=55))
    wq = _join_cols(whole("w_uq", g_uq)).reshape(ql, heads, NOPE + ROPE)
    w_q = jnp.concatenate([wq, _quarter_turn(wq[:, :, NOPE:])], axis=2).reshape(ql, heads * HEAD_W)
    w_kv = _join_cols(whole("w_ukv", g_ukv)).reshape(kvl, heads, 2, NOPE).transpose(0, 2, 1, 3)
    w_kv = w_kv.reshape(kvl, 2 * heads * NOPE)
    w_a = rows4(whole("w_branch_a", g_a))
    z_lat = _matmul(h1, w_in_lat_t, mode="nt", out_dtype=F32, name="mm_z_lat", tm=s, tn=1024)
    a_act = _gmlp_fwd(z_big, ln_g, ln_b, gm_w_s, b_s_t, "gmlp_fwd")
    qn, kvn, kr = _mla_prep(z_lat, q_g, kv_g, rope_k, "mla_prep")
    q_rot = _matmul(qn, w_q, mode="nn", out_dtype=BF16, name="mm_q", tm=s, tn=HEAD_W, mul=rope_q)
    kv_all = _matmul(kvn, w_kv, mode="nn", out_dtype=BF16, name="mm_kv", tm=s, tn=1024)
    (o_att, lse), (g_b, g_o, g_up) = _attn_fwd(q_rot, kv_all, kr, heads, "attn_fwd",
                                               comm=gather(["w_branch_b", "w_out", "w_up"]))
    w_b, w_o, w_upf = rows4(whole("w_branch_b", g_b)), rows4(whole("w_out", g_o)), whole("w_up", g_up)
    y_a = _matmul(a_act, w_a, mode="nn", out_dtype=BF16, name="mm_y_a", tm=s)
    y_b = _matmul(o_att, w_b, mode="nn", out_dtype=BF16, name="mm_y_b", tm=s)
    merged = _merge(z_big, y_a, y_b, "merge")
    y1 = _matmul(merged, w_o, mode="nn", out_dtype=F32, name="mm_y1", tm=s)
    x1, h2 = _post_pre(x2d, y1, gate1, g_post1, g_pre2, scale2, shift2, "post1_pre2")

    up_pre, (g_dn,) = _matmul(h2, w_upf, mode="nn", out_dtype=BF16, name="mm_up", tm=s, tn=1408,
                              comm=gather(["w_down"], middle_at=60))
    w_dn = rows4(whole("w_down", g_dn))
    act = _conv_fwd(up_pre, conv_wf, conv_bf, "conv_fwd")
    ffn = _matmul(act, w_dn, mode="nn", out_dtype=F32, name="mm_ffn", tm=s, tn=1024, tk=1408)

    dffn, dgate2, g_post2_grad, dx2, loss_part = _post_bwd(ffn, gate2, g_post2, "post2_bwd", xin=x1, target=tgt)
    loss = lax.psum(loss_part[0, 0], ("x", "y", "c"))
    place = jnp.stack([ic, chip]).astype(jnp.int32)
    rows_of = lambda g: g.reshape(N_CHIPS, g.shape[0] // N_CHIPS, g.shape[1])
    add_sibling = lambda names, gs, r1s: [_add_sibling(g, r1, place, "rs_add_sibling_" + n, by_cols=n == "w_in")
                                          for n, g, r1 in zip(names, gs, r1s)]
    add_chips = lambda names, s1s, r2s: [_add_chips(s1, r2, place, "rs_add_chips_" + n, by_cols=n == "w_in")
                                         for n, s1, r2 in zip(names, s1s, r2s)]
    gp_down = [rows_of(_matmul(act, dffn, mode="tn", out_dtype=BF16, name="mm_gw_down", tn=2048, tk=s))]
    dact, r1_down = _matmul(dffn, w_dn, mode="nt", out_dtype=BF16, name="mm_dact", tm=s, comm=_swap_comm(gp_down))
    s1_down = add_sibling(["w_down"], gp_down, r1_down)
    (dup, gcw_g, gcw_v, gcb_g, gcb_v), r2_down = _conv_bwd(up_pre, dact, conv_wf, conv_bf, "conv_bwd",
                                                            comm=_exchange_comm(s1_down))
    half_down = add_chips(["w_down"], s1_down, r2_down)
    dh2 = _matmul(dup, w_upf, mode="nt", out_dtype=F32, name="mm_dh2", tm=s, tn=1024, tk=1408)
    dx1, dshift2, dscale2, g_pre2_grad = _prenorm_bwd(x1, dh2, dx2, g_pre2, scale2, "prenorm2_bwd")

    dy1, dgate1, g_post1_grad = _post_bwd(y1, gate1, g_post1, "post1_bwd", dxo=dx1)
    dmerged = _matmul(dy1, w_o, mode="nt", out_dtype=BF16, name="mm_dmerged", tm=s)
    gw_out = _matmul(merged, dy1, mode="tn", out_dtype=BF16, name="mm_gw_out", tn=1024, tk=s)
    dy_a, dy_b, dz_big = _merge_bwd(dmerged, z_big, y_a, y_b, "merge_bwd")
    gw_a = _matmul(a_act, dy_a, mode="tn", out_dtype=BF16, name="mm_gw_a", tn=1024, tk=s)
    gw_b = _matmul(o_att, dy_b, mode="tn", out_dtype=BF16, name="mm_gw_b", tn=1024, tk=s)
    mid = ["w_up", "w_out", "w_branch_a", "w_branch_b"]
    gp_oab = [rows_of(gw_out), rows_of(gw_a), rows_of(gw_b)]
    da, r1_oab = _matmul(dy_a, w_a, mode="nt", out_dtype=BF16, name="mm_da", tm=s, comm=_swap_comm(gp_oab))
    s1_oab = add_sibling(mid[1:], gp_oab, r1_oab)
    gw_up, r2_oa = _matmul(h2, dup, mode="tn", out_dtype=BF16, name="mm_gw_up", tm=1024, tn=1408, tk=s,
                           out_groups=N_CHIPS, comm=_exchange_comm(s1_oab[:2]))
    do = _matmul(dy_b, w_b, mode="nt", out_dtype=BF16, name="mm_do", tm=s)
    (dz_big, g_ws, g_bs_t, g_ln_g, g_ln_b), r1_up = _gmlp_bwd(z_big, da, dz_big, ln_g, ln_b, gm_w_s, b_s_t,
                                                               "gmlp_bwd", comm=_swap_comm([gw_up]))
    s1_mid = add_sibling(mid[:1], [gw_up], r1_up) + s1_oab
    (dq, dk, dv), r2_up = _attn_bwd(q_rot, kv_all, kr, o_att, do, lse, heads, "attn_bwd",
                                    comm=_exchange_comm(s1_mid[:1]))
    dq_big, dkv, dkk = _mla_bwd_mid(dq, dk, dv, rope_q, rope_k, heads, "mla_bwd_mid")
    gw_q = _matmul(qn, dq_big, mode="tn", out_dtype=F32, name="mm_gw_q", tn=1024, tk=s)
    dqn = _matmul(dq_big, w_q, mode="nt", out_dtype=F32, name="mm_dqn", tm=s, tk=1024)
    gw_kv = _matmul(kvn, dkv, mode="tn", out_dtype=BF16, name="mm_gw_kv", tn=1024, tk=s)
    dkvn = _matmul(dkv, w_kv, mode="nt", out_dtype=F32, name="mm_dkvn", tm=s, tk=1024)
    dz_lat, g_q, g_kv = _mla_bwd_post(z_lat, dqn, dkvn, dkk, q_g, kv_g, "mla_bwd_post")

    partial = {
        "gm_ln_g": g_ln_g, "gm_ln_b": g_ln_b, "gm_w_s": g_ws, "gm_b_s": g_bs_t[:, :gm_b_s.shape[0]].T,
        "q_norm_g": g_q, "kv_norm_g": g_kv, "post_norm1_g": g_post1_grad, "pre_norm2_g": g_pre2_grad,
        "conv_w": jnp.concatenate([gcw_g, gcw_v], axis=1), "conv_b": jnp.concatenate([gcb_g, gcb_v], axis=1),
        "post_norm2_g": g_post2_grad,
    }
    flat = jnp.concatenate([partial[n].reshape(-1) for n in SMALL_PARTIAL])
    n_small = flat.shape[0]
    rows_small = -(-n_small // (LANES * SMALL_ROW_TILE)) * SMALL_ROW_TILE
    flat = jnp.pad(flat, (0, rows_small * LANES - n_small)).reshape(rows_small, LANES)

    def small_pack(prefix, source):
        v = jnp.concatenate([source[prefix + n].reshape(-1) for n in SMALL])
        rows = -(-v.shape[0] // (LANES * SUBLANES)) * SUBLANES
        return jnp.pad(v, (0, rows * LANES - v.shape[0])).reshape(rows, LANES)

    small_state = [small_pack(prefix, given) for prefix in ("", "m_", "v_")]

    dh1, r2_a_b = _matmul(dz_big, w_in_big_t, mode="nn", out_dtype=F32, name="mm_dh1_big", tm=s, tn=1024, tk=1024,
                          comm=_exchange_comm(s1_mid[3:]))
    half_mid = add_chips(mid, s1_mid, list(r2_up) + list(r2_oa) + list(r2_a_b))
    gw_big_t, hosted = _matmul(dz_big, h1, mode="tn", out_dtype=BF16, name="mm_gw_in_big", tn=2048, tk=s,
                               comm=_join_comms([_share_comm(half_down + half_mid), _gather8_comm(flat)]))
    shared, small_all = hosted[:-1], lax.dynamic_update_slice(hosted[-1], flat[None], (dev, 0, 0))
    small_sum = _sum_leading(small_all, "sum_small", after=small_state + [loss.reshape(1, 1)]).reshape(-1)
    small_grads, off = {}, 0
    for n in SMALL_PARTIAL:
        shape = (CONV_TAPS, 2 * ff) if n == "conv_w" else given[n].shape
        small_grads[n] = small_sum[off:off + partial[n].size].reshape(shape)
        off += partial[n].size
    small_grads["conv_w"] = lax.dynamic_slice(small_grads["conv_w"], (0, chip * conv_w.shape[1]), conv_w.shape)
    grads = dict(zip(["w_down"] + mid, shared), **small_grads)
    gw_lat_t = _matmul(dz_lat, h1, mode="tn", out_dtype=F32, name="mm_gw_in_lat", tm=1024, tn=1024, tk=s)

    gq = gw_q.reshape(ql, heads, HEAD_W)
    gq_pe = gq[:, :, NOPE:NOPE + ROPE] + _quarter_turn_back(gq[:, :, NOPE + ROPE:])
    g_pe_t = gw_lat_t[ql + kvl:ql + kvl + ROPE] + _quarter_turn_back(gw_lat_t[ql + kvl + ROPE:].T).T
    last = ["w_in", "w_uq", "w_ukv"]
    gw_in_t = _stack_rows([gw_big_t[:o_q], gw_lat_t[:ql + kvl].astype(BF16), g_pe_t.astype(BF16), gw_big_t[o_q:]])
    gp_last = [
        gw_in_t.reshape(N_CHIPS, gw_in_t.shape[0] // N_CHIPS, d),
        _split_cols(jnp.concatenate([gq[:, :, :NOPE], gq_pe], axis=2).reshape(ql, heads * (NOPE + ROPE)).astype(BF16)),
        _split_cols(gw_kv.reshape(kvl, 2, heads, NOPE).transpose(0, 2, 1, 3).reshape(kvl, heads * 2 * NOPE)),
    ]
    dh1, r1_last = _matmul(dz_lat, w_in_lat_t, mode="nn", out_dtype=F32, name="mm_dh1_lat", tm=s, tk=1024, add=dh1,
                           comm=_swap_comm(gp_last, by_cols=[0]))
    grad_x, dshift1, dscale1, g_pre1_grad = _prenorm_bwd(x2d, dh1, dx1, g_pre1, scale1, "prenorm1_bwd")
    s1_last = add_sibling(last, gp_last, r1_last)

    dmod = jnp.concatenate([dshift1, dscale1, dgate1, dshift2, dscale2, dgate2, g_pre1_grad], axis=1)
    dmod_all = _all_gather(jnp.pad(dmod, ((0, SUBLANES - 1), (0, 0))), "gather_dmod")
    dmod_all = dmod_all.reshape(N_DEV, SUBLANES, (N_MOD + 1) * d)[:, 0]
    dmod_sum = _sum_leading(dmod_all.reshape(N_DEV, 1, (N_MOD + 1) * d), "sum_dmod")[0]
    grads["b_ada"], grads["pre_norm1_g"] = dmod_sum[:N_MOD * d], dmod_sum[N_MOD * d:]
    dmod_mine = lax.dynamic_slice(dmod_all, (0, chip * na), (N_DEV, na))
    grads["w_ada"] = _ada_bwd(c_all.T, dmod_mine, "ada_bwd")

    delta, new_m, new_v = {}, {}, {}

    def adamw(n, after=None):
        turn = (lambda a: a.T) if n == "w_in" else (lambda a: a)
        outs = _adamw(turn(given[n]), grads[n], turn(given["m_" + n]), turn(given["v_" + n]), "adamw_" + n,
                      after=after)
        grads[n] = turn(grads[n])
        delta[n], new_m[n], new_v[n] = (turn(o) for o in outs)

    exchange_last = _exchange_comm(s1_last)
    in_flight, token = _comm_split_start(exchange_last, "rs_exchange_last_start", after=[dmod_sum, small_sum])
    for n in ["w_ada", "w_down"] + mid:
        adamw(n, after=token)
    s1_last, r2_last = _comm_split_wait(exchange_last, in_flight, delta[mid[-1]], "rs_exchange_last_wait")
    half_last = add_chips(last, s1_last, r2_last)
    grads.update(zip(last, _run_comm(_share_comm(half_last, by_cols=[0]), "rs_share_last")))
    for n in last:
        adamw(n)

    outs = _adamw(small_state[0], small_pack("", grads), small_state[1], small_state[2], "adamw_small")
    off = 0
    for n in SMALL:
        size = given[n].size
        for store, packed_out in zip((delta, new_m, new_v), outs):
            store[n] = packed_out.reshape(-1)[off:off + size].reshape(given[n].shape)
        off += size

    return (loss, grad_x[None], *[grads[n] for n in WEIGHTS], *[delta[n] for n in WEIGHTS],
            *[new_m[n] for n in WEIGHTS], *[new_v[n] for n in WEIGHTS])
```

```python
import functools

import jax
import jax.numpy as jnp
from jax import lax
from jax.experimental import pallas as pl
from jax.experimental.pallas import tpu as pltpu

F32 = jnp.float32
BF16 = jnp.bfloat16
MESH = pl.DeviceIdType.MESH
HBM = pltpu.HBM

EPS = 1e-6
NOPE, ROPE, VHEAD = 128, 64, 128
HEAD_W = NOPE + 2 * ROPE
ROPE_THETA = 10000.0
CONV_TAPS = 3
N_MOD = 6
N_CHIPS, N_CORES, N_DEV = 4, 2, 8
ADAM_LR, ADAM_B1, ADAM_B2, ADAM_EPS, ADAM_WD, ADAM_STEP = 0.001, 0.9, 0.999, 1e-08, 0.01, 10

LANES = 128
SUBLANES = 8
VMEM_LIMIT = 56 * 2**20
MIDDLE_STAGE_AT = 70
SMALL_ROW_TILE = 256
ADAMW_BLOCK_ELEMS = 768 * 1024

BIG = ("w_in", "w_branch_a", "w_uq", "w_ukv", "w_branch_b", "w_out", "w_up", "w_down")
WEIGHTS = ("w_ada", "b_ada", "pre_norm1_g", "w_in", "gm_ln_g", "gm_ln_b", "gm_w_s", "gm_b_s", "w_branch_a",
           "q_norm_g", "w_uq", "kv_norm_g", "w_ukv", "w_branch_b", "w_out", "post_norm1_g", "pre_norm2_g",
           "w_up", "conv_w", "conv_b", "w_down", "post_norm2_g")
SMALL_PARTIAL = ("gm_ln_g", "gm_ln_b", "gm_w_s", "gm_b_s", "q_norm_g", "kv_norm_g", "post_norm1_g",
                 "pre_norm2_g", "conv_w", "conv_b", "post_norm2_g")
SMALL = ("b_ada", "pre_norm1_g") + SMALL_PARTIAL


def _div_tile(n, cap, mult=LANES):
    t = (min(cap, n) // mult) * mult
    while t >= mult:
        if n % t == 0:
            return t
        t -= mult
    return n


def _params(**kw):
    return pltpu.CompilerParams(vmem_limit_bytes=VMEM_LIMIT, **kw)


def _row_spec(width):
    return pl.BlockSpec((1, width), lambda *_: (0, 0))


def _gelu(x):
    k = 0.7978845608028654
    return 0.5 * x * (1.0 + jnp.tanh(k * (x + 0.044715 * x * x * x)))


def _gelu_grad(x):
    k = 0.7978845608028654
    t = jnp.tanh(k * (x + 0.044715 * x * x * x))
    return 0.5 * (1.0 + t) + 0.5 * x * (1.0 - t * t) * k * (1.0 + 3.0 * 0.044715 * x * x)


def _sigmoid(x):
    return 0.5 * jnp.tanh(0.5 * x) + 0.5


def _dot(a, b, dims):
    return lax.dot_general(a, b, (dims, ((), ())), preferred_element_type=F32)


NN = ((1,), (0,))
NT = ((1,), (1,))
TN = ((0,), (0,))


def _logical(arr):
    if arr.ndim == 2:
        return arr.shape[0], arr.shape[1], arr.shape[1]
    return arr.shape[1], arr.shape[0] * arr.shape[2], arr.shape[2]


def _tile_spec(ndim, group_w, blk_rows, blk_cols, row_of, col_of):
    if ndim == 2:
        return pl.BlockSpec((blk_rows, blk_cols), lambda i, j, k: (row_of(i, j, k), col_of(i, j, k)))
    per = group_w // blk_cols
    return pl.BlockSpec((None, blk_rows, blk_cols),
                        lambda i, j, k: (col_of(i, j, k) // per, row_of(i, j, k), col_of(i, j, k) % per))


def _matmul(a, b, *, mode, out_dtype, name, tm=512, tn=512, tk=2048, mul=None, add=None, out_groups=None, comm=None):
    ar, ac, agw = _logical(a)
    br, bc, bgw = _logical(b)
    if mode == "nn":
        m, kd, n = ar, ac, bc
        m_w, k_w, n_w = (), (agw,), (bgw,)
    elif mode == "nt":
        m, kd, n = ar, ac, br
        m_w, k_w, n_w = (), (agw, bgw), ()
    else:
        m, kd, n = ac, ar, bc
        m_w, k_w, n_w = (agw,), (), (bgw,)
    if out_groups is not None:
        n_w = n_w + (n // out_groups,)
    tm = _div_tile(min((m,) + m_w), tm, LANES if mode == "tn" else SUBLANES)
    tn = _div_tile(min((n,) + n_w), tn)
    tk = _div_tile(min((kd,) + k_w), tk)
    assert all(w % tn == 0 for w in n_w) and all(w % tk == 0 for w in k_w) and all(w % tm == 0 for w in m_w)
    nk = kd // tk
    dims = {"nn": NN, "nt": NT, "tn": TN}[mode]
    gi, gj, gk = (lambda i, j, k: i), (lambda i, j, k: j), (lambda i, j, k: k)
    if mode == "nn":
        a_spec = _tile_spec(a.ndim, agw, tm, tk, gi, gk)
        b_spec = _tile_spec(b.ndim, bgw, tk, tn, gk, gj)
    elif mode == "nt":
        a_spec = _tile_spec(a.ndim, agw, tm, tk, gi, gk)
        b_spec = _tile_spec(b.ndim, bgw, tn, tk, gj, gk)
    else:
        a_spec = _tile_spec(a.ndim, agw, tk, tm, gk, gi)
        b_spec = _tile_spec(b.ndim, bgw, tk, tn, gk, gj)
    in_specs, operands = [a_spec, b_spec], [a, b]
    if mul is not None:
        assert mul.shape == (m, tn)
        in_specs.append(pl.BlockSpec((tm, tn), lambda i, j, k: (i, 0)))
        operands.append(mul)
    if add is not None:
        in_specs.append(pl.BlockSpec((tm, tn), lambda i, j, k: (i, j)))
        operands.append(add)

    def body(*refs):
        a_ref, b_ref = refs[0], refs[1]
        pos = 2
        mul_ref = add_ref = None
        if mul is not None:
            mul_ref, pos = refs[pos], pos + 1
        if add is not None:
            add_ref, pos = refs[pos], pos + 1
        o_ref = refs[pos]

        def finish(r):
            if mul_ref is not None:
                r = r * mul_ref[...]
            if add_ref is not None:
                r = r + add_ref[...]
            o_ref[...] = r.astype(out_dtype)

        part = _dot(a_ref[...], b_ref[...], dims)
        if nk == 1:
            finish(part)
        else:
            acc_ref = refs[pos + 1]
            k = pl.program_id(2)

            @pl.when(k == 0)
            def _():
                acc_ref[...] = part

            @pl.when(k > 0)
            def _():
                acc_ref[...] += part

            @pl.when(k == nk - 1)
            def _():
                finish(acc_ref[...])

    if out_groups is None:
        out_spec, out_dims = _tile_spec(2, n, tm, tn, gi, gj), (m, n)
    else:
        out_spec, out_dims = _tile_spec(3, n // out_groups, tm, tn, gi, gj), (out_groups, m, n // out_groups)
    return _call(body, operands, comm, name=name, grid=(m // tm, n // tn, nk), in_specs=in_specs, out_specs=out_spec,
                 out_shape=jax.ShapeDtypeStruct(out_dims, out_dtype),
                 scratch_shapes=[] if nk == 1 else [pltpu.VMEM((tm, tn), F32)])


def _accumulate(ref, value):
    @pl.when(pl.program_id(0) == 0)
    def _():
        ref[...] = value

    @pl.when(pl.program_id(0) > 0)
    def _():
        ref[...] += value


def _colsum(v):
    return jnp.sum(v, axis=0, keepdims=True)


def _rowmean(v):
    return jnp.mean(v, axis=-1, keepdims=True)


def _prenorm(x, g, scale, shift, name):
    s, d = x.shape
    tb = _div_tile(s, 256, SUBLANES)

    def body(x_ref, g_ref, sc_ref, sh_ref, h_ref):
        xv = x_ref[...]
        r = lax.rsqrt(_rowmean(xv * xv) + EPS)
        h_ref[...] = ((xv * r) * g_ref[...] * (1.0 + sc_ref[...]) + sh_ref[...]).astype(BF16)

    blk = pl.BlockSpec((tb, d), lambda i: (i, 0))
    return pl.pallas_call(
        body, name=name, grid=(s // tb,), in_specs=[blk, _row_spec(d), _row_spec(d), _row_spec(d)],
        out_specs=blk, out_shape=jax.ShapeDtypeStruct((s, d), BF16), compiler_params=_params(),
    )(x, g, scale, shift)


def _post_pre(x, y, gate, pg, g2, scale2, shift2, name):
    s, d = x.shape
    tb = _div_tile(s, 256, SUBLANES)

    def body(x_ref, y_ref, gate_ref, pg_ref, g2_ref, sc_ref, sh_ref, x1_ref, h2_ref):
        yv = y_ref[...]
        rp = lax.rsqrt(_rowmean(yv * yv) + EPS)
        x1 = x_ref[...] + gate_ref[...] * ((yv * rp) * pg_ref[...])
        x1_ref[...] = x1
        r2 = lax.rsqrt(_rowmean(x1 * x1) + EPS)
        h2_ref[...] = ((x1 * r2) * g2_ref[...] * (1.0 + sc_ref[...]) + sh_ref[...]).astype(BF16)

    blk = pl.BlockSpec((tb, d), lambda i: (i, 0))
    return pl.pallas_call(
        body, name=name, grid=(s // tb,), in_specs=[blk, blk] + [_row_spec(d)] * 5,
        out_specs=[blk, blk],
        out_shape=[jax.ShapeDtypeStruct((s, d), F32), jax.ShapeDtypeStruct((s, d), BF16)],
        compiler_params=_params(),
    )(x, y, gate, pg, g2, scale2, shift2)


def _post_bwd(y, gate, pg, name, *, dxo=None, xin=None, target=None):
    s, d = y.shape
    tb = _div_tile(s, 256, SUBLANES)
    from_loss = target is not None

    def body(*refs):
        if from_loss:
            y_ref, gate_ref, pg_ref, xin_ref, t_ref, dy_ref, dgate_ref, dpg_ref, dxo_ref, loss_ref = refs
        else:
            y_ref, gate_ref, pg_ref, dxo_in_ref, dy_ref, dgate_ref, dpg_ref = refs
        yv = y_ref[...]
        rp = lax.rsqrt(_rowmean(yv * yv) + EPS)
        yh = yv * rp
        fn = yh * pg_ref[...]
        gate = gate_ref[...]
        if from_loss:
            err = xin_ref[...] + gate * fn - t_ref[...]
            dxo = err * (1.0 / d)
            dxo_ref[...] = dxo
            part = 0.5 * jnp.sum(_rowmean(err * err), axis=0, keepdims=True)
            _accumulate(loss_ref, jnp.broadcast_to(part, loss_ref.shape))
        else:
            dxo = dxo_in_ref[...]
        _accumulate(dgate_ref, _colsum(dxo * fn))
        dfn = dxo * gate
        _accumulate(dpg_ref, _colsum(dfn * yh))
        dyh = dfn * pg_ref[...]
        dy_ref[...] = (rp * (dyh - yh * _rowmean(dyh * yh))).astype(BF16)

    blk = pl.BlockSpec((tb, d), lambda i: (i, 0))
    in_specs = [blk, _row_spec(d), _row_spec(d)]
    out_specs = [blk, _row_spec(d), _row_spec(d)]
    out_shape = [jax.ShapeDtypeStruct((s, d), BF16), jax.ShapeDtypeStruct((1, d), F32),
                 jax.ShapeDtypeStruct((1, d), F32)]
    if from_loss:
        operands = (y, gate, pg, xin, target)
        in_specs += [blk, blk]
        out_specs += [blk, _row_spec(LANES)]
        out_shape += [jax.ShapeDtypeStruct((s, d), F32), jax.ShapeDtypeStruct((1, LANES), F32)]
    else:
        operands = (y, gate, pg, dxo)
        in_specs += [blk]
    return pl.pallas_call(
        body, name=name, grid=(s // tb,), in_specs=in_specs, out_specs=out_specs, out_shape=out_shape,
        compiler_params=_params(),
    )(*operands)


def _prenorm_bwd(xin, dh, dres, g, scale, name, comm=None):
    s, d = xin.shape
    tb = _div_tile(s, 256, SUBLANES)

    def body(x_ref, dh_ref, dres_ref, g_ref, sc_ref, dx_ref, dshift_ref, dscale_ref, dg_ref):
        xv = x_ref[...]
        r = lax.rsqrt(_rowmean(xv * xv) + EPS)
        xn = xv * r
        dh = dh_ref[...]
        g1 = g_ref[...]
        s1 = 1.0 + sc_ref[...]
        _accumulate(dshift_ref, _colsum(dh))
        _accumulate(dscale_ref, _colsum(dh * xn * g1))
        _accumulate(dg_ref, _colsum(dh * xn * s1))
        dxn = dh * g1 * s1
        dx_ref[...] = dres_ref[...] + r * (dxn - xn * _rowmean(dxn * xn))

    blk = pl.BlockSpec((tb, d), lambda i: (i, 0))
    return _call(
        body, (xin, dh, dres, g, scale), comm, name=name, grid=(s // tb,),
        in_specs=[blk, blk, blk, _row_spec(d), _row_spec(d)],
        out_specs=[blk, _row_spec(d), _row_spec(d), _row_spec(d)],
        out_shape=[jax.ShapeDtypeStruct((s, d), F32)] + [jax.ShapeDtypeStruct((1, d), F32)] * 3)


def _merge(z_big, y_a, y_b, name):
    s, d = y_a.shape
    tb = _div_tile(s, 256, SUBLANES)

    def body(zg_ref, ya_ref, yb_ref, o_ref):
        ga, gb = zg_ref[:, :d].astype(F32), zg_ref[:, d:].astype(F32)
        o_ref[...] = (_sigmoid(ga) * ya_ref[...].astype(F32) + _sigmoid(gb) * yb_ref[...].astype(F32)).astype(BF16)

    blk = pl.BlockSpec((tb, d), lambda i: (i, 0))
    return pl.pallas_call(
        body, name=name, grid=(s // tb,), in_specs=[pl.BlockSpec((tb, 2 * d), lambda i: (i, 1)), blk, blk],
        out_specs=blk, out_shape=jax.ShapeDtypeStruct((s, d), BF16), compiler_params=_params(),
    )(z_big, y_a, y_b)


def _merge_bwd(dmerged, z_big, y_a, y_b, name):
    s, d = y_a.shape
    tb = _div_tile(s, 256, SUBLANES)

    def body(dm_ref, zg_ref, ya_ref, yb_ref, dya_ref, dyb_ref, dz_ref):
        dm = dm_ref[...].astype(F32)
        sa, sb = _sigmoid(zg_ref[:, :d].astype(F32)), _sigmoid(zg_ref[:, d:].astype(F32))
        dya_ref[...] = (dm * sa).astype(BF16)
        dyb_ref[...] = (dm * sb).astype(BF16)
        dz_ref[:, :d] = (dm * ya_ref[...].astype(F32) * sa * (1.0 - sa)).astype(BF16)
        dz_ref[:, d:] = (dm * yb_ref[...].astype(F32) * sb * (1.0 - sb)).astype(BF16)

    blk = pl.BlockSpec((tb, d), lambda i: (i, 0))
    wide = pl.BlockSpec((tb, 2 * d), lambda i: (i, 1))
    return pl.pallas_call(
        body, name=name, grid=(s // tb,), in_specs=[blk, wide, blk, blk], out_specs=[blk, blk, wide],
        out_shape=[jax.ShapeDtypeStruct((s, d), BF16), jax.ShapeDtypeStruct((s, d), BF16),
                   jax.ShapeDtypeStruct((s, 4 * d), BF16)],
        compiler_params=_params(),
    )(dmerged, z_big, y_a, y_b)


def _causal_mask(ch):
    q = lax.broadcasted_iota(jnp.int32, (ch, ch), 0)
    p = lax.broadcasted_iota(jnp.int32, (ch, ch), 1)
    return (p <= q).astype(F32)


def _gmlp_norm(zc, lng, lnb, gw):
    u_pre, v_pre = zc[:, :gw], zc[:, gw:]
    vg = _gelu(v_pre)
    mu = _rowmean(vg)
    cen = vg - mu
    rstd = lax.rsqrt(_rowmean(cen * cen) + EPS)
    vhat = cen * rstd
    return u_pre, v_pre, _gelu(u_pre), vhat, rstd, vhat * lng + lnb


def _gmlp_fwd(z_big, ln_g, ln_b, w_s, b_s_t, name):
    s = z_big.shape[0]
    groups, ch, _ = w_s.shape
    gw = ln_g.shape[1]
    gd = gw // groups

    def body(z_ref, lng_ref, lnb_ref, ws_ref, bt_ref, a_ref):
        _, _, u, _, _, vn = _gmlp_norm(z_ref[...].astype(F32), lng_ref[...], lnb_ref[...], gw)
        mask = _causal_mask(ch)
        for g in range(groups):
            cols = slice(g * gd, (g + 1) * gd)
            wm = (ws_ref[g] * mask).astype(BF16)
            mixed = _dot(wm, vn[:, cols].astype(BF16), NN) + bt_ref[:, g:g + 1]
            a_ref[:, cols] = (u[:, cols] * mixed).astype(BF16)

    return pl.pallas_call(
        body, name=name, grid=(s // ch,),
        in_specs=[pl.BlockSpec((ch, 2 * gw), lambda n: (n, 0)), _row_spec(gw), _row_spec(gw),
                  pl.BlockSpec((groups, ch, ch), lambda n: (0, 0, 0)), pl.BlockSpec((ch, groups), lambda n: (0, 0))],
        out_specs=pl.BlockSpec((ch, gw), lambda n: (n, 0)),
        out_shape=jax.ShapeDtypeStruct((s, gw), BF16), compiler_params=_params(),
    )(z_big, ln_g, ln_b, w_s, b_s_t)


def _gmlp_bwd(z_big, da, dz_big, ln_g, ln_b, w_s, b_s_t, name, comm=None):
    s = z_big.shape[0]
    groups, ch, _ = w_s.shape
    gw = ln_g.shape[1]
    gd = gw // groups

    def body(z_ref, da_ref, dzin_ref, lng_ref, lnb_ref, ws_ref, bt_ref, dz_ref, gws_ref, gbt_ref, glng_ref, glnb_ref,
             vg_ref, dvh_ref):
        del dzin_ref
        mask = _causal_mask(ch)
        lane = lax.broadcasted_iota(jnp.int32, (ch, LANES), 1)
        group_cols = [slice(g * gd, (g + 1) * gd) for g in range(groups)]
        rowsum = lambda v: jnp.sum(v, axis=1, keepdims=True)

        @pl.when(pl.program_id(0) == 0)
        def _():
            for ref in (gws_ref, gbt_ref, glng_ref, glnb_ref):
                ref[...] = jnp.zeros(ref.shape, F32)

        total = jnp.zeros((ch, 1), F32)
        for cols in group_cols:
            vg = _gelu(z_ref[:, gw + cols.start:gw + cols.stop].astype(F32))
            vg_ref[:, cols] = vg
            total = total + rowsum(vg)
        mu = total * (1.0 / gw)
        total = jnp.zeros((ch, 1), F32)
        for cols in group_cols:
            cen = vg_ref[:, cols] - mu
            total = total + rowsum(cen * cen)
        rstd = lax.rsqrt(total * (1.0 / gw) + EPS)
        m1, m2, gb = jnp.zeros((ch, 1), F32), jnp.zeros((ch, 1), F32), jnp.zeros((ch, LANES), F32)
        for g, cols in enumerate(group_cols):
            vhat = (vg_ref[:, cols] - mu) * rstd
            vn_g = (vhat * lng_ref[:, cols] + lnb_ref[:, cols]).astype(BF16)
            wm = (ws_ref[g] * mask).astype(BF16)
            mixed = _dot(wm, vn_g, NN) + bt_ref[:, g:g + 1]
            u_pre, da_g = z_ref[:, cols].astype(F32), da_ref[:, cols].astype(F32)
            dz_ref[:, cols] = (da_g * mixed * _gelu_grad(u_pre)).astype(BF16)
            dmixed = da_g * _gelu(u_pre)
            dm16 = dmixed.astype(BF16)
            dvn = _dot(wm, dm16, TN)
            gws_ref[g] += _dot(dm16, vn_g, NT) * mask
            gb = gb + jnp.where(lane == g, rowsum(dmixed), 0.0)
            glnb_ref[:, cols] += _colsum(dvn)
            glng_ref[:, cols] += _colsum(dvn * vhat)
            dvh = dvn * lng_ref[:, cols]
            dvh_ref[:, cols] = dvh
            m1, m2 = m1 + rowsum(dvh), m2 + rowsum(dvh * vhat)
        gbt_ref[...] += gb
        m1, m2 = m1 * (1.0 / gw), m2 * (1.0 / gw)
        for cols in group_cols:
            vhat = (vg_ref[:, cols] - mu) * rstd
            dvg = rstd * (dvh_ref[:, cols] - m1 - vhat * m2)
            v_pre = z_ref[:, gw + cols.start:gw + cols.stop].astype(F32)
            dz_ref[:, gw + cols.start:gw + cols.stop] = (dvg * _gelu_grad(v_pre)).astype(BF16)

    zspec = pl.BlockSpec((ch, 2 * gw), lambda n: (n, 0))
    return _call(
        body, (z_big, da, dz_big, ln_g, ln_b, w_s, b_s_t), comm, name=name, grid=(s // ch,),
        in_specs=[zspec, pl.BlockSpec((ch, gw), lambda n: (n, 0)), pl.BlockSpec(memory_space=HBM),
                  _row_spec(gw), _row_spec(gw), pl.BlockSpec((groups, ch, ch), lambda n: (0, 0, 0)),
                  pl.BlockSpec((ch, groups), lambda n: (0, 0))],
        out_specs=[zspec, pl.BlockSpec((groups, ch, ch), lambda n: (0, 0, 0)),
                   pl.BlockSpec((ch, LANES), lambda n: (0, 0)), _row_spec(gw), _row_spec(gw)],
        out_shape=[jax.ShapeDtypeStruct(dz_big.shape, BF16), jax.ShapeDtypeStruct((groups, ch, ch), F32),
                   jax.ShapeDtypeStruct((ch, LANES), F32), jax.ShapeDtypeStruct((1, gw), F32),
                   jax.ShapeDtypeStruct((1, gw), F32)],
        scratch_shapes=[pltpu.VMEM((ch, gw), F32)] * 2, input_output_aliases={2: 0})


def _mla_prep(z_lat, q_g, kv_g, rope_k, name):
    s, latw = z_lat.shape
    ql, kvl = q_g.shape[1], kv_g.shape[1]
    tb = _div_tile(s, 256, SUBLANES)

    def body(z_ref, qg_ref, kvg_ref, t_ref, qn_ref, kvn_ref, kr_ref):
        q = z_ref[:, :ql]
        qn_ref[...] = ((q * lax.rsqrt(_rowmean(q * q) + EPS)) * qg_ref[...]).astype(BF16)
        kv = z_ref[:, ql:ql + kvl]
        kvn_ref[...] = ((kv * lax.rsqrt(_rowmean(kv * kv) + EPS)) * kvg_ref[...]).astype(BF16)
        kk = z_ref[:, ql + kvl:] * t_ref[...]
        kr_ref[...] = (kk + pltpu.roll(kk, ROPE, axis=1)).astype(BF16)

    return pl.pallas_call(
        body, name=name, grid=(s // tb,),
        in_specs=[pl.BlockSpec((tb, latw), lambda i: (i, 0)), _row_spec(ql), _row_spec(kvl),
                  pl.BlockSpec((tb, 2 * ROPE), lambda i: (i, 0))],
        out_specs=[pl.BlockSpec((tb, ql), lambda i: (i, 0)), pl.BlockSpec((tb, kvl), lambda i: (i, 0)),
                   pl.BlockSpec((tb, 2 * ROPE), lambda i: (i, 0))],
        out_shape=[jax.ShapeDtypeStruct((s, ql), BF16), jax.ShapeDtypeStruct((s, kvl), BF16),
                   jax.ShapeDtypeStruct((s, 2 * ROPE), BF16)],
        compiler_params=_params(),
    )(z_lat, q_g, kv_g, rope_k)


def _attn_fwd(q, kv, kr, heads, name, comm=None):
    s = q.shape[0]
    t = _div_tile(s, 512)
    nb = s // t
    hp = 2 if heads % 2 == 0 else 1

    def body(q_ref, k_ref, kr_ref, v_ref, o_ref, lse_ref, m_ref, l_ref, acc_ref):
        i, j = pl.program_id(1), pl.program_id(2)

        @pl.when(j == 0)
        def _():
            m_ref[...] = jnp.full(m_ref.shape, -1e30, F32)
            l_ref[...] = jnp.zeros(l_ref.shape, F32)
            acc_ref[...] = jnp.zeros(acc_ref.shape, F32)

        def update(h, rows, n_keys, on_diagonal):
            vc = slice(h * VHEAD, (h + 1) * VHEAD)
            k_full = jnp.concatenate([k_ref[:n_keys, h * NOPE:(h + 1) * NOPE], kr_ref[:n_keys, :]], axis=1)
            sc = _dot(q_ref[rows, h * HEAD_W:(h + 1) * HEAD_W], k_full, NT)
            if on_diagonal:
                row_pos = rows.start + lax.broadcasted_iota(jnp.int32, sc.shape, 0)
                sc = jnp.where(lax.broadcasted_iota(jnp.int32, sc.shape, 1) <= row_pos, sc, -1e30)
            m_old = m_ref[h, rows, :]
            m_new = jnp.maximum(m_old, jnp.max(sc, axis=-1, keepdims=True))
            p = jnp.exp(sc - m_new)
            alpha = jnp.exp(m_old - m_new)
            l_new = alpha * l_ref[h, rows, :] + jnp.sum(p, axis=-1, keepdims=True)
            acc = alpha * acc_ref[rows, vc] + _dot(p.astype(BF16), v_ref[:n_keys, vc], NN)
            if on_diagonal:
                o_ref[rows, vc] = (acc / l_new).astype(BF16)
                lse_ref[h, rows, :] = jnp.broadcast_to(m_new + jnp.log(l_new), (rows.stop - rows.start, LANES))
            else:
                m_ref[h, rows, :], l_ref[h, rows, :], acc_ref[rows, vc] = m_new, l_new, acc

        def below_diagonal():
            for h in range(hp):
                update(h, slice(0, t), t, False)

        def on_diagonal():
            for h in range(hp):
                update(h, slice(0, t // 2), t // 2, True)
                update(h, slice(t // 2, t), t, True)

        pl.when(j < i)(below_diagonal)
        pl.when(j == i)(on_diagonal)

    kidx = lambda off: (lambda h, i, j: (jnp.minimum(i, j), off(h)))
    return _call(
        body, (q, kv, kr, kv), comm, name=name, grid=(heads // hp, nb, nb),
        in_specs=[pl.BlockSpec((t, hp * HEAD_W), lambda h, i, j: (i, h)),
                  pl.BlockSpec((t, hp * NOPE), kidx(lambda h: h)),
                  pl.BlockSpec((t, 2 * ROPE), kidx(lambda h: 0)),
                  pl.BlockSpec((t, hp * VHEAD), kidx(lambda h: heads // hp + h))],
        out_specs=[pl.BlockSpec((t, hp * VHEAD), lambda h, i, j: (i, h)),
                   pl.BlockSpec((hp, t, LANES), lambda h, i, j: (h, i, 0))],
        out_shape=[jax.ShapeDtypeStruct((s, heads * VHEAD), BF16), jax.ShapeDtypeStruct((heads, s, LANES), F32)],
        scratch_shapes=[pltpu.VMEM((hp, t, 1), F32), pltpu.VMEM((hp, t, 1), F32), pltpu.VMEM((t, hp * VHEAD), F32)])


def _attn_bwd(q, kv, kr, o, do, lse, heads, name, comm=None):
    s = q.shape[0]
    t = _div_tile(s, 512)
    nb = s // t
    hp = 2 if heads % 2 == 0 else 1

    def body(q_ref, k_ref, kr_ref, v_ref, o_ref, do_ref, lse_ref, dq_ref, dk_ref, dv_ref, dk_acc, dv_acc):
        j, i = pl.program_id(1), pl.program_id(2)

        @pl.when(jnp.logical_and(j == 0, i == 0))
        def _():
            dq_ref[...] = jnp.zeros(dq_ref.shape, F32)

        def update(h, rows, n_keys, on_diagonal, assign):
            qc, kc, vc = (slice(h * w, (h + 1) * w) for w in (HEAD_W, NOPE, VHEAD))
            n_rows = rows.stop - rows.start
            qv, do_v = q_ref[rows, qc], do_ref[rows, vc]
            k_full = jnp.concatenate([k_ref[:n_keys, kc], kr_ref[:n_keys, :]], axis=1)
            sc = _dot(qv, k_full, NT)
            if on_diagonal:
                row_pos = rows.start + lax.broadcasted_iota(jnp.int32, sc.shape, 0)
                sc = jnp.where(lax.broadcasted_iota(jnp.int32, sc.shape, 1) <= row_pos, sc, -1e30)
            p = jnp.exp(sc - lse_ref[h, rows, :1])
            dp = _dot(do_v, v_ref[:n_keys, vc], NT)
            delta = jnp.sum(do_v.astype(F32) * o_ref[rows, vc].astype(F32), axis=-1, keepdims=True)
            ds = (p * (dp - delta)).astype(BF16)
            dq_ref[pl.ds(pl.multiple_of(i * t + rows.start, n_rows), n_rows), qc] += _dot(ds, k_full, NN)
            dv_part, dk_part = _dot(p.astype(BF16), do_v, TN), _dot(ds, qv, TN)
            if assign:
                dv_acc[:n_keys, vc], dk_acc[:n_keys, qc] = dv_part, dk_part
            else:
                dv_acc[:n_keys, vc] += dv_part
                dk_acc[:n_keys, qc] += dk_part

        def on_diagonal():
            for h in range(hp):
                update(h, slice(t // 2, t), t, True, True)
                update(h, slice(0, t // 2), t // 2, True, False)

        def below_diagonal():
            for h in range(hp):
                update(h, slice(0, t), t, False, False)

        pl.when(i == j)(on_diagonal)
        pl.when(i > j)(below_diagonal)

        @pl.when(i == nb - 1)
        def _():
            dk_ref[...] = dk_acc[...].astype(BF16)
            dv_ref[...] = dv_acc[...].astype(BF16)

    qidx = lambda h, j, i: (jnp.maximum(i, j), h)
    return _call(
        body, (q, kv, kr, kv, o, do, lse), comm, name=name, grid=(heads // hp, nb, nb),
        in_specs=[pl.BlockSpec((t, hp * HEAD_W), qidx),
                  pl.BlockSpec((t, hp * NOPE), lambda h, j, i: (j, h)),
                  pl.BlockSpec((t, 2 * ROPE), lambda h, j, i: (j, 0)),
                  pl.BlockSpec((t, hp * VHEAD), lambda h, j, i: (j, heads // hp + h)),
                  pl.BlockSpec((t, hp * VHEAD), qidx), pl.BlockSpec((t, hp * VHEAD), qidx),
                  pl.BlockSpec((hp, t, LANES), lambda h, j, i: (h, jnp.maximum(i, j), 0))],
        out_specs=[pl.BlockSpec((s, hp * HEAD_W), lambda h, j, i: (0, h)),
                   pl.BlockSpec((t, hp * HEAD_W), lambda h, j, i: (j, h)),
                   pl.BlockSpec((t, hp * VHEAD), lambda h, j, i: (j, h))],
        out_shape=[jax.ShapeDtypeStruct((s, heads * HEAD_W), F32), jax.ShapeDtypeStruct((s, heads * HEAD_W), BF16),
                   jax.ShapeDtypeStruct((s, heads * VHEAD), BF16)],
        scratch_shapes=[pltpu.VMEM((t, hp * HEAD_W), F32), pltpu.VMEM((t, hp * VHEAD), F32)])


def _mla_bwd_mid(dq, dk, dv, rope_q, rope_k, heads, name):
    s = dq.shape[0]
    tb = _div_tile(s, 256, SUBLANES)

    def body(dq_ref, dk_ref, dv_ref, tq_ref, tk_ref, dqb_ref, dkv_ref, dkk_ref):
        tq = tq_ref[...]
        dkr = jnp.zeros((tb, 2 * ROPE), F32)
        for h in range(heads):
            cols = slice(h * HEAD_W, (h + 1) * HEAD_W)
            dqb_ref[:, cols] = (dq_ref[:, cols] * tq).astype(BF16)
            dkv_ref[:, h * NOPE:(h + 1) * NOPE] = dk_ref[:, h * HEAD_W:h * HEAD_W + NOPE]
            dkr = dkr + dk_ref[:, h * HEAD_W + NOPE:(h + 1) * HEAD_W].astype(F32)
        dkv_ref[:, heads * NOPE:] = dv_ref[...]
        dkk_ref[...] = (dkr + pltpu.roll(dkr, ROPE, axis=1)) * tk_ref[...]

    wq, wv = heads * HEAD_W, heads * VHEAD
    return pl.pallas_call(
        body, name=name, grid=(s // tb,),
        in_specs=[pl.BlockSpec((tb, wq), lambda i: (i, 0)), pl.BlockSpec((tb, wq), lambda i: (i, 0)),
                  pl.BlockSpec((tb, wv), lambda i: (i, 0)), pl.BlockSpec((tb, HEAD_W), lambda i: (i, 0)),
                  pl.BlockSpec((tb, 2 * ROPE), lambda i: (i, 0))],
        out_specs=[pl.BlockSpec((tb, wq), lambda i: (i, 0)), pl.BlockSpec((tb, heads * NOPE + wv), lambda i: (i, 0)),
                   pl.BlockSpec((tb, 2 * ROPE), lambda i: (i, 0))],
        out_shape=[jax.ShapeDtypeStruct((s, wq), BF16), jax.ShapeDtypeStruct((s, heads * NOPE + wv), BF16),
                   jax.ShapeDtypeStruct((s, 2 * ROPE), F32)],
        compiler_params=_params(),
    )(dq, dk, dv, rope_q, rope_k)


def _mla_bwd_post(z_lat, dqn, dkvn, dkk, q_g, kv_g, name):
    s, latw = z_lat.shape
    ql, kvl = q_g.shape[1], kv_g.shape[1]
    tb = _div_tile(s, 256, SUBLANES)

    def norm_bwd(xv, dn, g, dg_ref):
        r = lax.rsqrt(_rowmean(xv * xv) + EPS)
        xh = xv * r
        _accumulate(dg_ref, _colsum(dn * xh))
        dxh = dn * g
        return r * (dxh - xh * _rowmean(dxh * xh))

    def body(z_ref, dqn_ref, dkvn_ref, dkk_ref, qg_ref, kvg_ref, dz_ref, gq_ref, gkv_ref):
        dz_ref[:, :ql] = norm_bwd(z_ref[:, :ql], dqn_ref[...], qg_ref[...], gq_ref).astype(BF16)
        dz_ref[:, ql:ql + kvl] = norm_bwd(z_ref[:, ql:ql + kvl], dkvn_ref[...], kvg_ref[...], gkv_ref).astype(BF16)
        dz_ref[:, ql + kvl:] = dkk_ref[...].astype(BF16)

    return pl.pallas_call(
        body, name=name, grid=(s // tb,),
        in_specs=[pl.BlockSpec((tb, latw), lambda i: (i, 0)), pl.BlockSpec((tb, ql), lambda i: (i, 0)),
                  pl.BlockSpec((tb, kvl), lambda i: (i, 0)), pl.BlockSpec((tb, 2 * ROPE), lambda i: (i, 0)),
                  _row_spec(ql), _row_spec(kvl)],
        out_specs=[pl.BlockSpec((tb, latw), lambda i: (i, 0)), _row_spec(ql), _row_spec(kvl)],
        out_shape=[jax.ShapeDtypeStruct((s, latw), BF16), jax.ShapeDtypeStruct((1, ql), F32),
                   jax.ShapeDtypeStruct((1, kvl), F32)],
        compiler_params=_params(),
    )(z_lat, dqn, dkvn, dkk, q_g, kv_g)


CONV_ROWS = 128
CONV_HALO = 16


def _row_steps(n_rows, step):
    step(0, True)
    if n_rows > CONV_ROWS:
        def later(i, carry):
            step(pl.multiple_of(i * CONV_ROWS, CONV_ROWS), False)
            return carry
        lax.fori_loop(1, n_rows // CONV_ROWS, later, 0)


def _conv_taps(pre_ref, r0, first):
    if first:
        win = jnp.concatenate([jnp.zeros((CONV_HALO, pre_ref.shape[1]), F32), pre_ref[0:CONV_ROWS, :].astype(F32)])
    else:
        win = pre_ref[pl.ds(pl.multiple_of(r0 - CONV_HALO, CONV_HALO), CONV_ROWS + CONV_HALO), :].astype(F32)
    return win[CONV_HALO:], pltpu.roll(win, 1, axis=0)[CONV_HALO:], pltpu.roll(win, 2, axis=0)[CONV_HALO:]


def _conv(taps, w_ref, b_ref):
    return w_ref[2:3, :] * taps[0] + w_ref[1:2, :] * taps[1] + w_ref[0:1, :] * taps[2] + b_ref[...]


def _conv_fwd(up_pre, conv_w, conv_b, name):
    s, ff2 = up_pre.shape
    ff = ff2 // 2
    tc = _div_tile(ff, 256)
    nb = ff // tc
    assert s % CONV_ROWS == 0

    def body(pg_ref, pv_ref, wg_ref, wv_ref, bg_ref, bv_ref, act_ref):
        def step(r0, first):
            gate = _conv(_conv_taps(pg_ref, r0, first), wg_ref, bg_ref)
            val = _conv(_conv_taps(pv_ref, r0, first), wv_ref, bv_ref)
            act_ref[pl.ds(r0, CONV_ROWS), :] = (gate * _sigmoid(gate) * val).astype(BF16)

        _row_steps(s, step)

    def col(rows, off):
        return pl.BlockSpec((rows, tc), lambda j: (0, j + off))

    return pl.pallas_call(
        body, name=name, grid=(nb,),
        in_specs=[col(s, 0), col(s, nb), col(CONV_TAPS, 0), col(CONV_TAPS, nb), col(1, 0), col(1, nb)],
        out_specs=col(s, 0), out_shape=jax.ShapeDtypeStruct((s, ff), BF16), compiler_params=_params(),
    )(up_pre, up_pre, conv_w, conv_w, conv_b, conv_b)


def _conv_bwd(up_pre, dact, conv_w, conv_b, name, comm=None):
    s, ff2 = up_pre.shape
    ff = ff2 // 2
    tc = _div_tile(ff, 256)
    nb = ff // tc
    assert s % CONV_ROWS == 0

    def body(pg_ref, pv_ref, da_ref, wg_ref, wv_ref, bg_ref, bv_ref, dup_ref, gwg_ref, gwv_ref, gbg_ref, gbv_ref,
             dxg_ref, dxv_ref):
        for ref in (gwg_ref, gwv_ref, gbg_ref, gbv_ref):
            ref[...] = jnp.zeros(ref.shape, F32)
        for ref in (dxg_ref, dxv_ref):
            ref[s:s + SUBLANES, :] = jnp.zeros((SUBLANES, tc), F32)

        def sums(taps, dx, gw_ref, gb_ref):
            gb_ref[...] += _colsum(dx)
            for k in range(CONV_TAPS):
                gw_ref[k:k + 1, :] += _colsum(dx * taps[CONV_TAPS - 1 - k])

        def forward(r0, first):
            rows = pl.ds(r0, CONV_ROWS)
            taps_g, taps_v = _conv_taps(pg_ref, r0, first), _conv_taps(pv_ref, r0, first)
            gate, val = _conv(taps_g, wg_ref, bg_ref), _conv(taps_v, wv_ref, bv_ref)
            da = da_ref[rows, :].astype(F32)
            sg = _sigmoid(gate)
            dxv, dxg = da * gate * sg, da * val * sg * (1.0 + gate * (1.0 - sg))
            dxv_ref[rows, :], dxg_ref[rows, :] = dxv, dxg
            sums(taps_v, dxv, gwv_ref, gbv_ref)
            sums(taps_g, dxg, gwg_ref, gbg_ref)

        def backward(r0, first):
            del first
            n = CONV_ROWS + SUBLANES
            for dx_ref, w_ref, out_ref in ((dxg_ref, wg_ref, dup_ref.at[0]), (dxv_ref, wv_ref, dup_ref.at[1])):
                win = dx_ref[pl.ds(r0, n), :]
                ahead1 = pltpu.roll(win, n - 1, axis=0)[:CONV_ROWS]
                ahead2 = pltpu.roll(win, n - 2, axis=0)[:CONV_ROWS]
                out_ref[pl.ds(r0, CONV_ROWS), :] = (w_ref[2:3, :] * win[:CONV_ROWS] + w_ref[1:2, :] * ahead1
                                                    + w_ref[0:1, :] * ahead2).astype(BF16)

        _row_steps(s, forward)
        _row_steps(s, backward)

    def col(rows, off):
        return pl.BlockSpec((rows, tc), lambda j: (0, j + off))

    return _call(
        body, (up_pre, up_pre, dact, conv_w, conv_w, conv_b, conv_b), comm, name=name, grid=(nb,),
        in_specs=[col(s, 0), col(s, nb), col(s, 0), col(CONV_TAPS, 0), col(CONV_TAPS, nb), col(1, 0), col(1, nb)],
        out_specs=[pl.BlockSpec((2, s, tc), lambda j: (0, 0, j)), col(CONV_TAPS, 0), col(CONV_TAPS, 0),
                   col(1, 0), col(1, 0)],
        out_shape=[jax.ShapeDtypeStruct((2, s, ff), BF16)] + [jax.ShapeDtypeStruct((CONV_TAPS, ff), F32)] * 2
        + [jax.ShapeDtypeStruct((1, ff), F32)] * 2,
        scratch_shapes=[pltpu.VMEM((s + SUBLANES, tc), F32)] * 2)


def _ada_fwd(c_all, w, b, name):
    nseq, d = c_all.shape
    na = w.shape[1]
    tn = _div_tile(na, 512)

    def body(c_ref, w_ref, b_ref, o_ref):
        cv = c_ref[...]
        sc = cv * _sigmoid(cv)
        o_ref[...] = jnp.dot(sc, w_ref[...], preferred_element_type=F32, precision=lax.Precision.HIGHEST) + b_ref[...]

    return pl.pallas_call(
        body, name=name, grid=(na // tn,),
        in_specs=[pl.BlockSpec((nseq, d), lambda j: (0, 0)), pl.BlockSpec((d, tn), lambda j: (0, j)),
                  pl.BlockSpec((1, tn), lambda j: (0, j))],
        out_specs=pl.BlockSpec((nseq, tn), lambda j: (0, j)),
        out_shape=jax.ShapeDtypeStruct((nseq, na), F32), compiler_params=_params(),
    )(c_all, w, b)


def _ada_bwd(c_all_t, dmod, name):
    d, nseq = c_all_t.shape
    na = dmod.shape[1]
    tm, tn = _div_tile(d, 512, SUBLANES), _div_tile(na, 1024)

    def body(c_ref, dm_ref, o_ref):
        cv = c_ref[...]
        o_ref[...] = jnp.dot(cv * _sigmoid(cv), dm_ref[...], preferred_element_type=F32,
                             precision=lax.Precision.HIGHEST)

    return pl.pallas_call(
        body, name=name, grid=(d // tm, na // tn),
        in_specs=[pl.BlockSpec((tm, nseq), lambda i, j: (i, 0)), pl.BlockSpec((nseq, tn), lambda i, j: (0, j))],
        out_specs=pl.BlockSpec((tm, tn), lambda i, j: (i, j)),
        out_shape=jax.ShapeDtypeStruct((d, na), F32), compiler_params=_params(),
    )(c_all_t, dmod)


def _adamw(w, g, m, v, name, comm=None, after=None):
    rows, cols = w.shape
    tb = _div_tile(rows, max(SUBLANES, ADAMW_BLOCK_ELEMS // cols // SUBLANES * SUBLANES), SUBLANES)
    c1 = 1.0 / (1.0 - ADAM_B1 ** ADAM_STEP)
    c2 = 1.0 / (1.0 - ADAM_B2 ** ADAM_STEP)

    def body(*refs):
        w_ref, g_ref, m_ref, v_ref = refs[:4]
        d_ref, nm_ref, nv_ref = refs[-3:]
        gv = g_ref[...]
        nm = ADAM_B1 * m_ref[...] + (1.0 - ADAM_B1) * gv
        nv = ADAM_B2 * v_ref[...] + (1.0 - ADAM_B2) * (gv * gv)
        nm_ref[...] = nm
        nv_ref[...] = nv
        d_ref[...] = -ADAM_LR * ((nm * c1) / (jnp.sqrt(nv * c2) + ADAM_EPS) + ADAM_WD * w_ref[...])

    blk = pl.BlockSpec((tb, cols), lambda i: (i, 0))
    operands, in_specs = (w, g, m, v), [blk] * 4
    if after is not None:
        operands, in_specs = operands + (after,), in_specs + [pl.BlockSpec(after.shape, lambda i: (0, 0))]
    return _call(body, operands, comm, name=name, grid=(rows // tb,), in_specs=in_specs, out_specs=[blk] * 3,
                 out_shape=[jax.ShapeDtypeStruct((rows, cols), F32)] * 3)


def _sum_leading(parts, name, after=()):
    n, rows, cols = parts.shape
    tb = _div_tile(rows, 512, SUBLANES)

    def body(p_ref, *rest):
        o_ref = rest[-1]
        acc = p_ref[0]
        for k in range(1, n):
            acc = acc + p_ref[k]
        o_ref[...] = acc

    return pl.pallas_call(
        body, name=name, grid=(rows // tb,),
        in_specs=[pl.BlockSpec((n, tb, cols), lambda i: (0, i, 0))] + [pl.BlockSpec(memory_space=pl.ANY)] * len(after),
        out_specs=pl.BlockSpec((tb, cols), lambda i: (i, 0)),
        out_shape=jax.ShapeDtypeStruct((rows, cols), F32), compiler_params=_params(),
    )(parts, *after)


def _place():
    x, y, c = lax.axis_index("x"), lax.axis_index("y"), lax.axis_index("c")
    return x, y, c, [(1 - x, y), (x, 1 - y), (1 - x, 1 - y)]


def _all_gather(block, name):
    m_per, n = block.shape

    def body(x_ref, out_ref, send_sems, recv_sems, local_sem):
        x, y, c, chips = _place()
        me, sibling = (x, y, c), (x, y, 1 - c)

        def rows(px, py, pc):
            return out_ref.at[pl.ds((4 * px + 2 * py + pc) * m_per, m_per), :]

        def copy(k, blk, to, src=None):
            return pltpu.make_async_remote_copy(
                src_ref=rows(*blk) if src is None else src, dst_ref=rows(*blk), send_sem=send_sems.at[k],
                recv_sem=recv_sems.at[k], device_id=to, device_id_type=MESH)

        mine = pltpu.make_async_copy(x_ref, rows(*me), local_sem)
        mine.start()
        first = [copy(0, me, sibling, src=x_ref)]
        first += [copy(1 + j, me, (*chip, c), src=x_ref) for j, chip in enumerate(chips)]
        for cp in first:
            cp.start()
        passed = [copy(4 + j, (*chip, c), sibling) for j, chip in enumerate(chips)]
        for j, chip in enumerate(chips):
            copy(1 + j, (*chip, c), me).wait_recv()
            passed[j].start()
        copy(0, sibling, me).wait_recv()
        for j, chip in enumerate(chips):
            copy(4 + j, (*chip, 1 - c), me).wait_recv()
        for cp in first + passed:
            cp.wait_send()
        mine.wait()

    return pl.pallas_call(
        body, name=name, out_shape=jax.ShapeDtypeStruct((N_DEV * m_per, n), block.dtype),
        in_specs=[pl.BlockSpec(memory_space=pltpu.VMEM)], out_specs=pl.BlockSpec(memory_space=pltpu.VMEM),
        scratch_shapes=[pltpu.SemaphoreType.DMA((7,)), pltpu.SemaphoreType.DMA((7,)), pltpu.SemaphoreType.DMA],
        compiler_params=_params(),
    )(block)


def _hbm_specs(n):
    return [pl.BlockSpec(memory_space=HBM)] * n


def _part(ref, by_cols, half, quarter=None, lead=None):
    extent = ref.shape[-1] if by_cols else ref.shape[-2]
    size = extent // 2 if quarter is None else extent // 4
    first = half * (extent // 2) + (0 if quarter is None else quarter * size)
    tile = LANES if by_cols else 2 * SUBLANES
    span = pl.ds(pl.multiple_of(first, tile) if size % tile == 0 else first, size)
    index = (slice(None), span) if by_cols else (span, slice(None))
    return ref.at[index] if lead is None else ref.at[(lead,) + index]


def _half_rows(ref, half, lead=None):
    return _part(ref, False, half, lead=lead)


class _Comm:
    def __init__(self, operands, out_shape, sem_dims, build, aliases=None):
        self.operands, self.out_shape, self.sem_dims = list(operands), list(out_shape), list(sem_dims)
        self.scratch = [pltpu.SemaphoreType.DMA(d) for d in sem_dims]
        self.build, self.aliases = build, dict(aliases or {})


class _SemGrid:
    def __init__(self, sems, dims):
        self.sems, self.dims, self.at = list(sems), tuple(dims), self

    def __getitem__(self, index):
        index = index if isinstance(index, tuple) else (index,)
        flat = 0
        for i, d in zip(index, self.dims):
            flat = flat * d + i
        return self.sems[flat]


def _call(body, operands, comm=None, *, name, grid, in_specs, out_specs, out_shape, scratch_shapes=(),
          input_output_aliases=None):
    aliases = dict(input_output_aliases or {})
    if comm is None:
        return pl.pallas_call(
            body, name=name, grid=grid, in_specs=in_specs, out_specs=out_specs, out_shape=out_shape,
            scratch_shapes=list(scratch_shapes), input_output_aliases=aliases, compiler_params=_params())(*operands)
    single = not isinstance(out_shape, (list, tuple))
    outs = [out_shape] if single else list(out_shape)
    ospecs = [out_specs] if single else list(out_specs)
    n_in, n_out, n_scr = len(operands), len(outs), len(scratch_shapes)
    c_in, c_out = len(comm.operands), len(comm.out_shape)
    for i, o in comm.aliases.items():
        aliases[n_in + i] = n_out + o

    def hosted(*refs):
        ins, c_ins = refs[:n_in], refs[n_in:n_in + c_in]
        o0 = n_in + c_in
        o_refs, c_outs = refs[o0:o0 + n_out], refs[o0 + n_out:o0 + n_out + c_out]
        s0 = o0 + n_out + c_out
        scr, sems = refs[s0:s0 + n_scr], refs[s0 + n_scr:]
        stages = comm.build(c_ins, c_outs, sems)
        step, n_steps = 0, 1
        for dim, size in enumerate(grid):
            step, n_steps = step * size + pl.program_id(dim), n_steps * size
        pl.when(step == 0)(stages[0])
        body(*ins, *o_refs, *scr)
        for stage in stages[1:-1]:
            pl.when(step == (n_steps * MIDDLE_STAGE_AT) // 100)(stage)
        pl.when(step == n_steps - 1)(stages[-1])

    res = pl.pallas_call(
        hosted, name=name, grid=grid, in_specs=list(in_specs) + _hbm_specs(c_in),
        out_specs=ospecs + _hbm_specs(c_out), out_shape=outs + comm.out_shape,
        scratch_shapes=list(scratch_shapes) + comm.scratch, input_output_aliases=aliases,
        compiler_params=_params())(*operands, *comm.operands)
    return (res[0] if single else res[:n_out]), res[n_out:]


def _run_comm(comm, name):
    c_in, c_out = len(comm.operands), len(comm.out_shape)

    def body(*refs):
        for stage in comm.build(refs[:c_in], refs[c_in:c_in + c_out], refs[c_in + c_out:]):
            stage()

    return pl.pallas_call(
        body, name=name, in_specs=_hbm_specs(c_in), out_specs=_hbm_specs(c_out), out_shape=comm.out_shape,
        scratch_shapes=comm.scratch, input_output_aliases=comm.aliases, compiler_params=_params())(*comm.operands)


def _join_comms(comms):
    def build(in_refs, out_refs, sems):
        staged, i, o, k = [], 0, 0, 0
        for cm in comms:
            ni, no, ns = len(cm.operands), len(cm.out_shape), len(cm.sem_dims)
            staged.append(cm.build(in_refs[i:i + ni], out_refs[o:o + no], sems[k:k + ns]))
            i, o, k = i + ni, o + no, k + ns
        def run(fns):
            def stage():
                for fn in fns:
                    fn()
            return stage

        return (run([st[0] for st in staged]), run([fn for st in staged for fn in st[1:-1]]),
                run([st[-1] for st in staged]))

    aliases, i, o = {}, 0, 0
    for cm in comms:
        aliases.update({i + a: o + b for a, b in cm.aliases.items()})
        i, o = i + len(cm.operands), o + len(cm.out_shape)
    return _Comm(sum((cm.operands for cm in comms), []), sum((cm.out_shape for cm in comms), []),
                 sum((cm.sem_dims for cm in comms), []), build, aliases)


def _gather8_comm(block):
    def build(in_refs, out_refs, sems):
        (src,), (out,), (send_sems, recv_sems) = in_refs, out_refs, sems
        x, y, c, chips = _place()
        me, sibling = (x, y, c), (x, y, 1 - c)

        def copy(k, blk, to, own=False):
            dst = out.at[4 * blk[0] + 2 * blk[1] + blk[2]]
            return pltpu.make_async_remote_copy(
                src_ref=src if own else dst, dst_ref=dst, send_sem=send_sems.at[k], recv_sem=recv_sems.at[k],
                device_id=to, device_id_type=MESH)

        first = [copy(0, me, sibling, own=True)] + [copy(1 + j, me, (*chip, c), own=True)
                                                     for j, chip in enumerate(chips)]
        passed = [copy(4 + j, (*chip, c), sibling) for j, chip in enumerate(chips)]

        def start():
            for cp in first:
                cp.start()

        def middle():
            for j, chip in enumerate(chips):
                copy(1 + j, (*chip, c), me).wait_recv()
                passed[j].start()

        def finish():
            copy(0, sibling, me).wait_recv()
            for j, chip in enumerate(chips):
                copy(4 + j, (*chip, 1 - c), me).wait_recv()
            for cp in first + passed:
                cp.wait_send()

        return start, middle, finish

    return _Comm([block], [jax.ShapeDtypeStruct((N_DEV,) + block.shape, block.dtype)], [(7,), (7,)], build)


def _gather_comm(shards, by_cols=()):
    nw = len(shards)

    def build(in_refs, out_refs, sems):
        send_sems, recv_sems = sems
        x, y, c, chips = _place()
        me, sibling = (x, y, c), (x, y, 1 - c)
        across_x, across_y, diagonal = chips

        def copy(w, k, block, part, to, src=None):
            dst = _part(out_refs[w], w in by_cols, part[1], part[2] if part[0] else None, 2 * block[0] + block[1])
            return pltpu.make_async_remote_copy(
                src_ref=dst if src is None else src, dst_ref=dst, send_sem=send_sems.at[w, k],
                recv_sem=recv_sems.at[w, k], device_id=to, device_id_type=MESH)

        first = [copy(w, j, (x, y), (0, c), (*chip, c), src=_part(in_refs[w], w in by_cols, c))
                 for w in range(nw) for j, chip in enumerate((across_x, across_y))]
        passed = [[copy(w, 2, across_x, (1, c, 0), (*across_y, c)), copy(w, 3, across_y, (1, c, 1), (*across_x, c)),
                   copy(w, 4, across_x, (0, c), sibling), copy(w, 5, across_y, (0, c), sibling)] for w in range(nw)]
        last = [[copy(w, 6, diagonal, (1, c, 0), sibling), copy(w, 7, diagonal, (1, c, 1), sibling)]
                for w in range(nw)]

        def start():
            for cp in first:
                cp.start()

        def middle():
            for w in range(nw):
                copy(w, 0, across_x, (0, c), me).wait_recv()
                copy(w, 1, across_y, (0, c), me).wait_recv()
                for cp in passed[w]:
                    cp.start()

        def finish():
            for w in range(nw):
                copy(w, 2, diagonal, (1, c, 0), me).wait_recv()
                copy(w, 3, diagonal, (1, c, 1), me).wait_recv()
                for cp in last[w]:
                    cp.start()
            for w in range(nw):
                for k, block, part in ((4, across_x, (0, 1 - c)), (5, across_y, (0, 1 - c)),
                                       (6, diagonal, (1, 1 - c, 0)), (7, diagonal, (1, 1 - c, 1))):
                    copy(w, k, block, part, me).wait_recv()
            for cp in first + sum(passed, []) + sum(last, []):
                cp.wait_send()

        return start, middle, finish

    return _Comm(shards, [jax.ShapeDtypeStruct((N_CHIPS,) + w.shape, w.dtype) for w in shards],
                 [(nw, 8), (nw, 8)], build)


def _halved(shape, by_cols):
    return shape[:-1] + (shape[-1] // 2,) if by_cols else shape[:-2] + (shape[-2] // 2, shape[-1])


def _swap_comm(gs, by_cols=()):
    nw = len(gs)

    def build(in_refs, out_refs, sems):
        send_sems, recv_sems = sems
        x, y, c, _ = _place()
        cps = []
        for w in range(nw):
            cps.append(pltpu.make_async_remote_copy(
                src_ref=_part(in_refs[w], w in by_cols, 1 - c, lead=slice(None)), dst_ref=out_refs[w],
                send_sem=send_sems.at[w], recv_sem=recv_sems.at[w], device_id=(x, y, 1 - c), device_id_type=MESH))

        def start():
            for cp in cps:
                cp.start()

        def finish():
            for cp in cps:
                cp.wait()

        return start, finish

    return _Comm(gs, [jax.ShapeDtypeStruct(_halved(g.shape, w in by_cols), g.dtype) for w, g in enumerate(gs)],
                 [(nw,), (nw,)], build)


def _exchange_comm(s1s):
    nw = len(s1s)

    def build(in_refs, out_refs, sems):
        send_sems, recv_sems = sems
        x, y, c, chips = _place()
        cps = [pltpu.make_async_remote_copy(
            src_ref=in_refs[w].at[2 * chip[0] + chip[1]], dst_ref=out_refs[w].at[j], send_sem=send_sems.at[w, j],
            recv_sem=recv_sems.at[w, j], device_id=(*chip, c), device_id_type=MESH)
            for w in range(nw) for j, chip in enumerate(chips)]

        def start():
            for cp in cps:
                cp.start()

        def finish():
            for cp in cps:
                cp.wait()

        return start, finish

    return _Comm(s1s, [jax.ShapeDtypeStruct((N_CHIPS - 1,) + s.shape[1:], s.dtype) for s in s1s],
                 [(nw, 3), (nw, 3)], build)


def _size(dims):
    n = 1
    for d in dims:
        n *= d
    return n


def _sem_grids(comm, sem_refs):
    grids, pos = [], 0
    for dims in comm.sem_dims:
        grids.append(_SemGrid(sem_refs[pos:pos + _size(dims)], dims))
        pos += _size(dims)
    return grids


def _comm_split_start(comm, name, after=()):
    c_in, c_out = len(comm.operands), len(comm.out_shape)
    counts = [_size(d) for d in comm.sem_dims]
    n_sem = sum(counts)
    assert not comm.aliases

    def body(*refs):
        srcs, lands = refs[:c_in], refs[c_in:c_in + c_out]
        first_sem = c_in + c_out + len(after)
        start, _ = comm.build(srcs, lands, _sem_grids(comm, refs[first_sem:first_sem + n_sem]))
        start()
        refs[-1][...] = jnp.zeros(refs[-1].shape, refs[-1].dtype)

    lands = [pltpu.with_memory_space_constraint(lax.empty(o.shape, o.dtype), HBM) for o in comm.out_shape]
    srcs = [pltpu.with_memory_space_constraint(a, HBM) for a in comm.operands]
    res = pl.pallas_call(
        body, name=name, in_specs=_hbm_specs(c_in + c_out) + [pl.BlockSpec(memory_space=pl.ANY)] * len(after),
        out_specs=[pl.BlockSpec(memory_space=pltpu.SEMAPHORE)] * n_sem + _hbm_specs(c_in + c_out)
        + [pl.BlockSpec(memory_space=pltpu.VMEM)],
        out_shape=[pltpu.SemaphoreType.DMA(())] * n_sem + [pltpu.HBM(a.shape, a.dtype) for a in comm.operands]
        + [pltpu.HBM(o.shape, o.dtype) for o in comm.out_shape] + [jax.ShapeDtypeStruct((SUBLANES, LANES), F32)],
        input_output_aliases={i: n_sem + i for i in range(c_in + c_out)},
        compiler_params=_params(has_side_effects=pltpu.SideEffectType.DATAFLOW_SIDE_EFFECTING))(*srcs, *lands, *after)
    return res[:-1], res[-1]


def _comm_split_wait(comm, state, after, name):
    c_in, c_out, n_sem = len(comm.operands), len(comm.out_shape), sum(_size(d) for d in comm.sem_dims)
    sems, srcs, lands = state[:n_sem], state[n_sem:n_sem + c_in], state[n_sem + c_in:]

    def body(*refs):
        src_refs, land_refs = refs[:c_in], refs[c_in:c_in + c_out]
        _, finish = comm.build(src_refs, land_refs, _sem_grids(comm, refs[c_in + c_out:c_in + c_out + n_sem]))
        finish()

    sem_spec = pl.BlockSpec(memory_space=pltpu.SEMAPHORE)
    res = pl.pallas_call(
        body, name=name, in_specs=_hbm_specs(c_in + c_out) + [sem_spec] * n_sem + [pl.BlockSpec(memory_space=pl.ANY)],
        out_specs=_hbm_specs(c_in + c_out),
        out_shape=[pltpu.HBM(a.shape, a.dtype) for a in srcs] + [pltpu.HBM(o.shape, o.dtype) for o in lands],
        input_output_aliases={i: i for i in range(c_in + c_out)},
        compiler_params=_params(has_side_effects=pltpu.SideEffectType.DATAFLOW_SIDE_EFFECTING),
    )(*srcs, *lands, *sems, after)
    return res[:c_in], res[c_in:]


def _share_comm(fs, by_cols=()):
    nw = len(fs)

    def build(in_refs, out_refs, sems):
        del in_refs
        send_sems, recv_sems = sems
        x, y, c, _ = _place()

        def copy(w, half):
            part = _part(out_refs[w], w in by_cols, half)
            return pltpu.make_async_remote_copy(
                src_ref=part, dst_ref=part, send_sem=send_sems.at[w], recv_sem=recv_sems.at[w],
                device_id=(x, y, 1 - c), device_id_type=MESH)

        sends = [copy(w, c) for w in range(nw)]

        def start():
            for cp in sends:
                cp.start()

        def finish():
            for w in range(nw):
                copy(w, 1 - c).wait_recv()
            for cp in sends:
                cp.wait_send()

        return start, finish

    return _Comm(fs, [jax.ShapeDtypeStruct(f.shape, f.dtype) for f in fs],
                 [(nw,), (nw,)], build,
                 aliases={w: w for w in range(nw)})


def _add_sibling(g, r1, place, name, by_cols=False):
    nch, h, cols = r1.shape
    tr = _div_tile(h, 1024 if by_cols else 512, 2 * SUBLANES)
    nb = h // tr
    mine = (lambda k, i, p: (k, i, p[0])) if by_cols else (lambda k, i, p: (k, p[0] * nb + i, 0))

    def body(place_ref, g_ref, r_ref, o_ref):
        del place_ref
        o_ref[...] = (g_ref[...].astype(F32) + r_ref[...].astype(F32)).astype(BF16)

    spec = pltpu.PrefetchScalarGridSpec(
        num_scalar_prefetch=1, grid=(nch, nb),
        in_specs=[pl.BlockSpec((None, tr, cols), mine), pl.BlockSpec((None, tr, cols), lambda k, i, p: (k, i, 0))],
        out_specs=pl.BlockSpec((None, tr, cols), lambda k, i, p: (k, i, 0)))
    return pl.pallas_call(body, name=name, grid_spec=spec, out_shape=jax.ShapeDtypeStruct((nch, h, cols), BF16),
                          compiler_params=_params())(place, g, r1)


def _add_chips(s1, r2, place, name, by_cols=False):
    _, h, cols = s1.shape
    tr = _div_tile(h, 1024 if by_cols else 512, 2 * SUBLANES)
    nb = h // tr
    mine = (lambda i, p: (i, p[0])) if by_cols else (lambda i, p: (p[0] * nb + i, 0))
    whole = (h, 2 * cols) if by_cols else (2 * h, cols)

    def body(place_ref, s_ref, r_ref, o_ref):
        del place_ref
        acc = s_ref[...].astype(F32)
        for j in range(N_CHIPS - 1):
            acc = acc + r_ref[j].astype(F32)
        o_ref[...] = acc

    spec = pltpu.PrefetchScalarGridSpec(
        num_scalar_prefetch=1, grid=(nb,),
        in_specs=[pl.BlockSpec((None, tr, cols), lambda i, p: (p[1], i, 0)),
                  pl.BlockSpec((N_CHIPS - 1, tr, cols), lambda i, p: (0, i, 0))],
        out_specs=pl.BlockSpec((tr, cols), mine))
    return pl.pallas_call(body, name=name, grid_spec=spec, out_shape=jax.ShapeDtypeStruct(whole, F32),
                          compiler_params=_params())(place, s1, r2)


def _quarter_turn(m):
    h = m.shape[-1] // 2
    return jnp.concatenate([-m[..., h:], m[..., :h]], axis=-1)


def _quarter_turn_back(m):
    h = m.shape[-1] // 2
    return jnp.concatenate([m[..., h:], -m[..., :h]], axis=-1)


def _stack_rows(parts):
    out = lax.empty((sum(p.shape[0] for p in parts),) + parts[0].shape[1:], parts[0].dtype)
    row = 0
    for p in parts:
        out = lax.dynamic_update_slice(out, p, (row, 0))
        row += p.shape[0]
    return out


def _join_cols(sh):
    return jnp.concatenate([sh[k] for k in range(N_CHIPS)], axis=1)


def _split_cols(full):
    c = full.shape[1] // N_CHIPS
    return jnp.stack([full[:, k * c:(k + 1) * c] for k in range(N_CHIPS)])


def kernel(x, c, positions, w_ada, b_ada, pre_norm1_g, w_in, gm_ln_g, gm_ln_b, gm_w_s, gm_b_s, w_branch_a, q_norm_g, w_uq, kv_norm_g, w_ukv, w_branch_b, w_out, post_norm1_g, pre_norm2_g, w_up, conv_w, conv_b, w_down, post_norm2_g, loss_target, m_w_ada, m_b_ada, m_pre_norm1_g, m_w_in, m_gm_ln_g, m_gm_ln_b, m_gm_w_s, m_gm_b_s, m_w_branch_a, m_q_norm_g, m_w_uq, m_kv_norm_g, m_w_ukv, m_w_branch_b, m_w_out, m_post_norm1_g, m_pre_norm2_g, m_w_up, m_conv_w, m_conv_b, m_w_down, m_post_norm2_g, v_w_ada, v_b_ada, v_pre_norm1_g, v_w_in, v_gm_ln_g, v_gm_ln_b, v_gm_w_s, v_gm_b_s, v_w_branch_a, v_q_norm_g, v_w_uq, v_kv_norm_g, v_w_ukv, v_w_branch_b, v_w_out, v_post_norm1_g, v_pre_norm2_g, v_w_up, v_conv_w, v_conv_b, v_w_down, v_post_norm2_g):
    given = dict(locals())
    s, d = x.shape[1], x.shape[2]
    gw = gm_ln_g.shape[0]
    ql, kvl = q_norm_g.shape[0], kv_norm_g.shape[0]
    heads = N_CHIPS * w_uq.shape[1] // (NOPE + ROPE)
    ff = N_CHIPS * w_down.shape[0]
    assert gw == d and N_CHIPS * w_ukv.shape[1] == heads * (NOPE + VHEAD)
    ix, iy, ic = lax.axis_index("x"), lax.axis_index("y"), lax.axis_index("c")
    chip = 2 * ix + iy
    dev = 2 * chip + ic
    row = lambda v: v.reshape(1, -1)

    first = _all_gather(jnp.concatenate([jnp.pad(c, ((0, SUBLANES - 1), (0, 0))),
                                         jnp.pad(conv_w, ((0, SUBLANES - CONV_TAPS), (0, 0)))], axis=1), "gather_c")
    first = first.reshape(N_DEV, SUBLANES, d + conv_w.shape[1])
    c_all = first[:, 0, :d]
    conv_wf = first[::N_CORES, :CONV_TAPS, d:].transpose(1, 0, 2).reshape(CONV_TAPS, N_CHIPS * conv_w.shape[1])
    na = w_ada.shape[1]
    b_ada_mine = lax.dynamic_slice(b_ada, (chip * na,), (na,))
    mod_cols = _ada_fwd(c_all, w_ada, row(b_ada_mine), "ada_fwd")
    mod_all = _all_gather(mod_cols, "gather_mod").reshape(N_CHIPS, N_CORES, N_DEV, na)[:, 0]
    mod = lax.dynamic_index_in_dim(mod_all, dev, axis=1, keepdims=False).reshape(N_MOD, d)
    shift1, scale1, gate1, shift2, scale2, gate2 = (mod[i:i + 1] for i in range(N_MOD))

    mine = {n: (given[n].T if n == "w_in" else given[n]).astype(BF16) for n in BIG}
    gather = lambda names: _gather_comm([mine[n] for n in names], [i for i, n in enumerate(names) if n == "w_in"])
    whole = lambda n, g: lax.dynamic_update_slice(g, mine[n][None], (chip, 0, 0))
    rows4 = lambda sh4: sh4.reshape(-1, sh4.shape[2])
    wi_t = rows4(whole("w_in", _run_comm(gather(["w_in"]), "gather_w_in")[0]))
    o_q, o_kv, o_pe, o_ga = 2 * gw, 2 * gw + ql, 2 * gw + ql + kvl, 2 * gw + ql + kvl + ROPE
    w_in_big_t = _stack_rows([wi_t[:o_q], wi_t[o_ga:]])
    w_in_lat_t = _stack_rows([wi_t[o_q:o_ga], _quarter_turn(wi_t[o_pe:o_ga].T).T])

    inv = ROPE_THETA ** (-jnp.arange(0, ROPE, 2, dtype=F32) / ROPE)
    ang = positions[0].astype(F32)[:, None] * inv
    cos, sin = jnp.cos(ang), jnp.sin(ang)
    rope_k = jnp.concatenate([cos, cos, sin, sin], axis=1)
    softmax_scale = float(NOPE + ROPE) ** -0.5
    rope_q = jnp.concatenate([jnp.ones((s, NOPE), F32), rope_k], axis=1) * softmax_scale

    x2d, tgt = x[0], loss_target[0]
    g_pre1, g_post1, g_pre2, g_post2 = row(pre_norm1_g), row(post_norm1_g), row(pre_norm2_g), row(post_norm2_g)
    ln_g, ln_b, q_g, kv_g = row(gm_ln_g), row(gm_ln_b), row(q_norm_g), row(kv_norm_g)
    b_s_t = gm_b_s.T
    conv_bf = row(conv_b)

    h1 = _prenorm(x2d, g_pre1, scale1, shift1, "prenorm1")
    z_big, (g_uq, g_ukv, g_a) = _matmul(h1, w_in_big_t, mode="nt", out_dtype=BF16, name="mm_z_big", tm=s,
                                        comm=gather(["w_uq", "w_ukv", "w_branch_a"]))
    wq = _join_cols(whole("w_uq", g_uq)).reshape(ql, heads, NOPE + ROPE)
    w_q = jnp.concatenate([wq, _quarter_turn(wq[:, :, NOPE:])], axis=2).reshape(ql, heads * HEAD_W)
    w_kv = _join_cols(whole("w_ukv", g_ukv)).reshape(kvl, heads, 2, NOPE).transpose(0, 2, 1, 3)
    w_kv = w_kv.reshape(kvl, 2 * heads * NOPE)
    w_a = rows4(whole("w_branch_a", g_a))
    z_lat = _matmul(h1, w_in_lat_t, mode="nt", out_dtype=F32, name="mm_z_lat", tm=s, tn=1024)
    a_act = _gmlp_fwd(z_big, ln_g, ln_b, gm_w_s, b_s_t, "gmlp_fwd")
    qn, kvn, kr = _mla_prep(z_lat, q_g, kv_g, rope_k, "mla_prep")
    q_rot = _matmul(qn, w_q, mode="nn", out_dtype=BF16, name="mm_q", tm=s, tn=HEAD_W, mul=rope_q)
    kv_all = _matmul(kvn, w_kv, mode="nn", out_dtype=BF16, name="mm_kv", tm=s, tn=1024)
    (o_att, lse), (g_b, g_o, g_up) = _attn_fwd(q_rot, kv_all, kr, heads, "attn_fwd",
                                               comm=gather(["w_branch_b", "w_out", "w_up"]))
    w_b, w_o, w_upf = rows4(whole("w_branch_b", g_b)), rows4(whole("w_out", g_o)), whole("w_up", g_up)
    y_a = _matmul(a_act, w_a, mode="nn", out_dtype=BF16, name="mm_y_a", tm=s)
    y_b = _matmul(o_att, w_b, mode="nn", out_dtype=BF16, name="mm_y_b", tm=s)
    merged = _merge(z_big, y_a, y_b, "merge")
    y1 = _matmul(merged, w_o, mode="nn", out_dtype=F32, name="mm_y1", tm=s)
    x1, h2 = _post_pre(x2d, y1, gate1, g_post1, g_pre2, scale2, shift2, "post1_pre2")

    up_pre, (g_dn,) = _matmul(h2, w_upf, mode="nn", out_dtype=BF16, name="mm_up", tm=s, tn=1408,
                              comm=gather(["w_down"]))
    w_dn = rows4(whole("w_down", g_dn))
    act = _conv_fwd(up_pre, conv_wf, conv_bf, "conv_fwd")
    ffn = _matmul(act, w_dn, mode="nn", out_dtype=F32, name="mm_ffn", tm=s, tn=1024, tk=1408)

    dffn, dgate2, g_post2_grad, dx2, loss_part = _post_bwd(ffn, gate2, g_post2, "post2_bwd", xin=x1, target=tgt)
    loss = lax.psum(loss_part[0, 0], ("x", "y", "c"))
    place = jnp.stack([ic, chip]).astype(jnp.int32)
    rows_of = lambda g: g.reshape(N_CHIPS, g.shape[0] // N_CHIPS, g.shape[1])
    add_sibling = lambda names, gs, r1s: [_add_sibling(g, r1, place, "rs_add_sibling_" + n, by_cols=n == "w_in")
                                          for n, g, r1 in zip(names, gs, r1s)]
    add_chips = lambda names, s1s, r2s: [_add_chips(s1, r2, place, "rs_add_chips_" + n, by_cols=n == "w_in")
                                         for n, s1, r2 in zip(names, s1s, r2s)]
    gp_down = [rows_of(_matmul(act, dffn, mode="tn", out_dtype=BF16, name="mm_gw_down", tn=2048, tk=s))]
    dact, r1_down = _matmul(dffn, w_dn, mode="nt", out_dtype=BF16, name="mm_dact", tm=s, comm=_swap_comm(gp_down))
    s1_down = add_sibling(["w_down"], gp_down, r1_down)
    (dup, gcw_g, gcw_v, gcb_g, gcb_v), r2_down = _conv_bwd(up_pre, dact, conv_wf, conv_bf, "conv_bwd",
                                                            comm=_exchange_comm(s1_down))
    half_down = add_chips(["w_down"], s1_down, r2_down)
    dh2 = _matmul(dup, w_upf, mode="nt", out_dtype=F32, name="mm_dh2", tm=s, tn=1024, tk=1408)
    dx1, dshift2, dscale2, g_pre2_grad = _prenorm_bwd(x1, dh2, dx2, g_pre2, scale2, "prenorm2_bwd")

    dy1, dgate1, g_post1_grad = _post_bwd(y1, gate1, g_post1, "post1_bwd", dxo=dx1)
    dmerged = _matmul(dy1, w_o, mode="nt", out_dtype=BF16, name="mm_dmerged", tm=s)
    gw_out = _matmul(merged, dy1, mode="tn", out_dtype=BF16, name="mm_gw_out", tn=1024, tk=s)
    dy_a, dy_b, dz_big = _merge_bwd(dmerged, z_big, y_a, y_b, "merge_bwd")
    gw_a = _matmul(a_act, dy_a, mode="tn", out_dtype=BF16, name="mm_gw_a", tn=1024, tk=s)
    gw_b = _matmul(o_att, dy_b, mode="tn", out_dtype=BF16, name="mm_gw_b", tn=1024, tk=s)
    mid = ["w_up", "w_out", "w_branch_a", "w_branch_b"]
    gp_oab = [rows_of(gw_out), rows_of(gw_a), rows_of(gw_b)]
    da, r1_oab = _matmul(dy_a, w_a, mode="nt", out_dtype=BF16, name="mm_da", tm=s, comm=_swap_comm(gp_oab))
    s1_oab = add_sibling(mid[1:], gp_oab, r1_oab)
    gw_up, r2_oa = _matmul(h2, dup, mode="tn", out_dtype=BF16, name="mm_gw_up", tm=1024, tn=1408, tk=s,
                           out_groups=N_CHIPS, comm=_exchange_comm(s1_oab[:2]))
    do = _matmul(dy_b, w_b, mode="nt", out_dtype=BF16, name="mm_do", tm=s)
    (dz_big, g_ws, g_bs_t, g_ln_g, g_ln_b), r1_up = _gmlp_bwd(z_big, da, dz_big, ln_g, ln_b, gm_w_s, b_s_t,
                                                               "gmlp_bwd", comm=_swap_comm([gw_up]))
    s1_mid = add_sibling(mid[:1], [gw_up], r1_up) + s1_oab
    (dq, dk, dv), r2_up = _attn_bwd(q_rot, kv_all, kr, o_att, do, lse, heads, "attn_bwd",
                                    comm=_exchange_comm(s1_mid[:1]))
    dq_big, dkv, dkk = _mla_bwd_mid(dq, dk, dv, rope_q, rope_k, heads, "mla_bwd_mid")
    gw_q = _matmul(qn, dq_big, mode="tn", out_dtype=F32, name="mm_gw_q", tn=1024, tk=s)
    dqn = _matmul(dq_big, w_q, mode="nt", out_dtype=F32, name="mm_dqn", tm=s, tk=1024)
    gw_kv = _matmul(kvn, dkv, mode="tn", out_dtype=BF16, name="mm_gw_kv", tn=1024, tk=s)
    dkvn = _matmul(dkv, w_kv, mode="nt", out_dtype=F32, name="mm_dkvn", tm=s, tk=1024)
    dz_lat, g_q, g_kv = _mla_bwd_post(z_lat, dqn, dkvn, dkk, q_g, kv_g, "mla_bwd_post")

    partial = {
        "gm_ln_g": g_ln_g, "gm_ln_b": g_ln_b, "gm_w_s": g_ws, "gm_b_s": g_bs_t[:, :gm_b_s.shape[0]].T,
        "q_norm_g": g_q, "kv_norm_g": g_kv, "post_norm1_g": g_post1_grad, "pre_norm2_g": g_pre2_grad,
        "conv_w": jnp.concatenate([gcw_g, gcw_v], axis=1), "conv_b": jnp.concatenate([gcb_g, gcb_v], axis=1),
        "post_norm2_g": g_post2_grad,
    }
    flat = jnp.concatenate([partial[n].reshape(-1) for n in SMALL_PARTIAL])
    n_small = flat.shape[0]
    rows_small = -(-n_small // (LANES * SMALL_ROW_TILE)) * SMALL_ROW_TILE
    flat = jnp.pad(flat, (0, rows_small * LANES - n_small)).reshape(rows_small, LANES)

    def small_pack(prefix, source):
        v = jnp.concatenate([source[prefix + n].reshape(-1) for n in SMALL])
        rows = -(-v.shape[0] // (LANES * SUBLANES)) * SUBLANES
        return jnp.pad(v, (0, rows * LANES - v.shape[0])).reshape(rows, LANES)

    small_state = [small_pack(prefix, given) for prefix in ("", "m_", "v_")]

    dh1, r2_a_b = _matmul(dz_big, w_in_big_t, mode="nn", out_dtype=F32, name="mm_dh1_big", tm=s, tn=1024, tk=1024,
                          comm=_exchange_comm(s1_mid[3:]))
    half_mid = add_chips(mid, s1_mid, list(r2_up) + list(r2_oa) + list(r2_a_b))
    gw_big_t, hosted = _matmul(dz_big, h1, mode="tn", out_dtype=BF16, name="mm_gw_in_big", tn=2048, tk=s,
                               comm=_join_comms([_share_comm(half_down + half_mid), _gather8_comm(flat)]))
    shared, small_all = hosted[:-1], lax.dynamic_update_slice(hosted[-1], flat[None], (dev, 0, 0))
    small_sum = _sum_leading(small_all, "sum_small", after=small_state + [loss.reshape(1, 1)]).reshape(-1)
    small_grads, off = {}, 0
    for n in SMALL_PARTIAL:
        shape = (CONV_TAPS, 2 * ff) if n == "conv_w" else given[n].shape
        small_grads[n] = small_sum[off:off + partial[n].size].reshape(shape)
        off += partial[n].size
    small_grads["conv_w"] = lax.dynamic_slice(small_grads["conv_w"], (0, chip * conv_w.shape[1]), conv_w.shape)
    grads = dict(zip(["w_down"] + mid, shared), **small_grads)
    gw_lat_t = _matmul(dz_lat, h1, mode="tn", out_dtype=F32, name="mm_gw_in_lat", tm=1024, tn=1024, tk=s)

    gq = gw_q.reshape(ql, heads, HEAD_W)
    gq_pe = gq[:, :, NOPE:NOPE + ROPE] + _quarter_turn_back(gq[:, :, NOPE + ROPE:])
    g_pe_t = gw_lat_t[ql + kvl:ql + kvl + ROPE] + _quarter_turn_back(gw_lat_t[ql + kvl + ROPE:].T).T
    last = ["w_in", "w_uq", "w_ukv"]
    gw_in_t = _stack_rows([gw_big_t[:o_q], gw_lat_t[:ql + kvl].astype(BF16), g_pe_t.astype(BF16), gw_big_t[o_q:]])
    gp_last = [
        gw_in_t.reshape(N_CHIPS, gw_in_t.shape[0] // N_CHIPS, d),
        _split_cols(jnp.concatenate([gq[:, :, :NOPE], gq_pe], axis=2).reshape(ql, heads * (NOPE + ROPE)).astype(BF16)),
        _split_cols(gw_kv.reshape(kvl, 2, heads, NOPE).transpose(0, 2, 1, 3).reshape(kvl, heads * 2 * NOPE)),
    ]
    dh1, r1_last = _matmul(dz_lat, w_in_lat_t, mode="nn", out_dtype=F32, name="mm_dh1_lat", tm=s, tk=1024, add=dh1,
                           comm=_swap_comm(gp_last, by_cols=[0]))
    grad_x, dshift1, dscale1, g_pre1_grad = _prenorm_bwd(x2d, dh1, dx1, g_pre1, scale1, "prenorm1_bwd")
    s1_last = add_sibling(last, gp_last, r1_last)

    dmod = jnp.concatenate([dshift1, dscale1, dgate1, dshift2, dscale2, dgate2, g_pre1_grad], axis=1)
    dmod_all = _all_gather(jnp.pad(dmod, ((0, SUBLANES - 1), (0, 0))), "gather_dmod")
    dmod_all = dmod_all.reshape(N_DEV, SUBLANES, (N_MOD + 1) * d)[:, 0]
    dmod_sum = _sum_leading(dmod_all.reshape(N_DEV, 1, (N_MOD + 1) * d), "sum_dmod")[0]
    grads["b_ada"], grads["pre_norm1_g"] = dmod_sum[:N_MOD * d], dmod_sum[N_MOD * d:]
    dmod_mine = lax.dynamic_slice(dmod_all, (0, chip * na), (N_DEV, na))
    grads["w_ada"] = _ada_bwd(c_all.T, dmod_mine, "ada_bwd")

    delta, new_m, new_v = {}, {}, {}

    def adamw(n, after=None):
        turn = (lambda a: a.T) if n == "w_in" else (lambda a: a)
        outs = _adamw(turn(given[n]), grads[n], turn(given["m_" + n]), turn(given["v_" + n]), "adamw_" + n,
                      after=after)
        grads[n] = turn(grads[n])
        delta[n], new_m[n], new_v[n] = (turn(o) for o in outs)

    exchange_last = _exchange_comm(s1_last)
    in_flight, token = _comm_split_start(exchange_last, "rs_exchange_last_start", after=[dmod_sum, small_sum])
    for n in ["w_ada", "w_down"] + mid:
        adamw(n, after=token)
    s1_last, r2_last = _comm_split_wait(exchange_last, in_flight, delta[mid[-1]], "rs_exchange_last_wait")
    half_last = add_chips(last, s1_last, r2_last)
    grads.update(zip(last, _run_comm(_share_comm(half_last, by_cols=[0]), "rs_share_last")))
    for n in last:
        adamw(n)

    outs = _adamw(small_state[0], small_pack("", grads), small_state[1], small_state[2], "adamw_small")
    off = 0
    for n in SMALL:
        size = given[n].size
        for store, packed_out in zip((delta, new_m, new_v), outs):
            store[n] = packed_out.reshape(-1)[off:off + size].reshape(given[n].shape)
        off += size

    return (loss, grad_x[None], *[grads[n] for n in WEIGHTS], *[delta[n] for n in WEIGHTS],
            *[new_m[n] for n in WEIGHTS], *[new_v[n] for n in WEIGHTS])
```

```python
import functools

import jax
import jax.numpy as jnp
from jax import lax
from jax.experimental import pallas as pl
from jax.experimental.pallas import tpu as pltpu

F32 = jnp.float32
BF16 = jnp.bfloat16
MESH = pl.DeviceIdType.MESH
HBM = pltpu.HBM

EPS = 1e-6
NOPE, ROPE, VHEAD = 128, 64, 128
HEAD_W = NOPE + 2 * ROPE
ROPE_THETA = 10000.0
CONV_TAPS = 3
N_MOD = 6
N_CHIPS, N_CORES, N_DEV = 4, 2, 8
ADAM_LR, ADAM_B1, ADAM_B2, ADAM_EPS, ADAM_WD, ADAM_STEP = 0.001, 0.9, 0.999, 1e-08, 0.01, 10

LANES = 128
SUBLANES = 8
VMEM_LIMIT = 56 * 2**20
MIDDLE_STAGE_AT = 70
SMALL_ROW_TILE = 256
ADAMW_BLOCK_ELEMS = 768 * 1024

BIG = ("w_in", "w_branch_a", "w_uq", "w_ukv", "w_branch_b", "w_out", "w_up", "w_down")
WEIGHTS = ("w_ada", "b_ada", "pre_norm1_g", "w_in", "gm_ln_g", "gm_ln_b", "gm_w_s", "gm_b_s", "w_branch_a",
           "q_norm_g", "w_uq", "kv_norm_g", "w_ukv", "w_branch_b", "w_out", "post_norm1_g", "pre_norm2_g",
           "w_up", "conv_w", "conv_b", "w_down", "post_norm2_g")
SMALL_PARTIAL = ("gm_ln_g", "gm_ln_b", "gm_w_s", "gm_b_s", "q_norm_g", "kv_norm_g", "post_norm1_g",
                 "pre_norm2_g", "conv_w", "conv_b", "post_norm2_g")
SMALL = ("b_ada", "pre_norm1_g") + SMALL_PARTIAL


def _div_tile(n, cap, mult=LANES):
    t = (min(cap, n) // mult) * mult
    while t >= mult:
        if n % t == 0:
            return t
        t -= mult
    return n


def _params(**kw):
    return pltpu.CompilerParams(vmem_limit_bytes=VMEM_LIMIT, **kw)


def _row_spec(width):
    return pl.BlockSpec((1, width), lambda *_: (0, 0))


def _gelu(x):
    k = 0.7978845608028654
    return 0.5 * x * (1.0 + jnp.tanh(k * (x + 0.044715 * x * x * x)))


def _gelu_grad(x):
    k = 0.7978845608028654
    t = jnp.tanh(k * (x + 0.044715 * x * x * x))
    return 0.5 * (1.0 + t) + 0.5 * x * (1.0 - t * t) * k * (1.0 + 3.0 * 0.044715 * x * x)


def _sigmoid(x):
    return 0.5 * jnp.tanh(0.5 * x) + 0.5


def _dot(a, b, dims):
    return lax.dot_general(a, b, (dims, ((), ())), preferred_element_type=F32)


NN = ((1,), (0,))
NT = ((1,), (1,))
TN = ((0,), (0,))


def _logical(arr):
    if arr.ndim == 2:
        return arr.shape[0], arr.shape[1], arr.shape[1]
    return arr.shape[1], arr.shape[0] * arr.shape[2], arr.shape[2]


def _tile_spec(ndim, group_w, blk_rows, blk_cols, row_of, col_of):
    if ndim == 2:
        return pl.BlockSpec((blk_rows, blk_cols), lambda i, j, k: (row_of(i, j, k), col_of(i, j, k)))
    per = group_w // blk_cols
    return pl.BlockSpec((None, blk_rows, blk_cols),
                        lambda i, j, k: (col_of(i, j, k) // per, row_of(i, j, k), col_of(i, j, k) % per))


def _matmul(a, b, *, mode, out_dtype, name, tm=512, tn=512, tk=2048, mul=None, add=None, out_groups=None, comm=None):
    ar, ac, agw = _logical(a)
    br, bc, bgw = _logical(b)
    if mode == "nn":
        m, kd, n = ar, ac, bc
        m_w, k_w, n_w = (), (agw,), (bgw,)
    elif mode == "nt":
        m, kd, n = ar, ac, br
        m_w, k_w, n_w = (), (agw, bgw), ()
    else:
        m, kd, n = ac, ar, bc
        m_w, k_w, n_w = (agw,), (), (bgw,)
    if out_groups is not None:
        n_w = n_w + (n // out_groups,)
    tm = _div_tile(min((m,) + m_w), tm, LANES if mode == "tn" else SUBLANES)
    tn = _div_tile(min((n,) + n_w), tn)
    tk = _div_tile(min((kd,) + k_w), tk)
    assert all(w % tn == 0 for w in n_w) and all(w % tk == 0 for w in k_w) and all(w % tm == 0 for w in m_w)
    nk = kd // tk
    dims = {"nn": NN, "nt": NT, "tn": TN}[mode]
    gi, gj, gk = (lambda i, j, k: i), (lambda i, j, k: j), (lambda i, j, k: k)
    if mode == "nn":
        a_spec = _tile_spec(a.ndim, agw, tm, tk, gi, gk)
        b_spec = _tile_spec(b.ndim, bgw, tk, tn, gk, gj)
    elif mode == "nt":
        a_spec = _tile_spec(a.ndim, agw, tm, tk, gi, gk)
        b_spec = _tile_spec(b.ndim, bgw, tn, tk, gj, gk)
    else:
        a_spec = _tile_spec(a.ndim, agw, tk, tm, gk, gi)
        b_spec = _tile_spec(b.ndim, bgw, tk, tn, gk, gj)
    in_specs, operands = [a_spec, b_spec], [a, b]
    if mul is not None:
        assert mul.shape == (m, tn)
        in_specs.append(pl.BlockSpec((tm, tn), lambda i, j, k: (i, 0)))
        operands.append(mul)
    if add is not None:
        in_specs.append(pl.BlockSpec((tm, tn), lambda i, j, k: (i, j)))
        operands.append(add)

    def body(*refs):
        a_ref, b_ref = refs[0], refs[1]
        pos = 2
        mul_ref = add_ref = None
        if mul is not None:
            mul_ref, pos = refs[pos], pos + 1
        if add is not None:
            add_ref, pos = refs[pos], pos + 1
        o_ref = refs[pos]

        def finish(r):
            if mul_ref is not None:
                r = r * mul_ref[...]
            if add_ref is not None:
                r = r + add_ref[...]
            o_ref[...] = r.astype(out_dtype)

        part = _dot(a_ref[...], b_ref[...], dims)
        if nk == 1:
            finish(part)
        else:
            acc_ref = refs[pos + 1]
            k = pl.program_id(2)

            @pl.when(k == 0)
            def _():
                acc_ref[...] = part

            @pl.when(k > 0)
            def _():
                acc_ref[...] += part

            @pl.when(k == nk - 1)
            def _():
                finish(acc_ref[...])

    if out_groups is None:
        out_spec, out_dims = _tile_spec(2, n, tm, tn, gi, gj), (m, n)
    else:
        out_spec, out_dims = _tile_spec(3, n // out_groups, tm, tn, gi, gj), (out_groups, m, n // out_groups)
    return _call(body, operands, comm, name=name, grid=(m // tm, n // tn, nk), in_specs=in_specs, out_specs=out_spec,
                 out_shape=jax.ShapeDtypeStruct(out_dims, out_dtype),
                 scratch_shapes=[] if nk == 1 else [pltpu.VMEM((tm, tn), F32)])


def _accumulate(ref, value):
    @pl.when(pl.program_id(0) == 0)
    def _():
        ref[...] = value

    @pl.when(pl.program_id(0) > 0)
    def _():
        ref[...] += value


def _colsum(v):
    return jnp.sum(v, axis=0, keepdims=True)


def _rowmean(v):
    return jnp.mean(v, axis=-1, keepdims=True)


def _prenorm(x, g, scale, shift, name):
    s, d = x.shape
    tb = _div_tile(s, 256, SUBLANES)

    def body(x_ref, g_ref, sc_ref, sh_ref, h_ref):
        xv = x_ref[...]
        r = lax.rsqrt(_rowmean(xv * xv) + EPS)
        h_ref[...] = ((xv * r) * g_ref[...] * (1.0 + sc_ref[...]) + sh_ref[...]).astype(BF16)

    blk = pl.BlockSpec((tb, d), lambda i: (i, 0))
    return pl.pallas_call(
        body, name=name, grid=(s // tb,), in_specs=[blk, _row_spec(d), _row_spec(d), _row_spec(d)],
        out_specs=blk, out_shape=jax.ShapeDtypeStruct((s, d), BF16), compiler_params=_params(),
    )(x, g, scale, shift)


def _post_pre(x, y, gate, pg, g2, scale2, shift2, name):
    s, d = x.shape
    tb = _div_tile(s, 256, SUBLANES)

    def body(x_ref, y_ref, gate_ref, pg_ref, g2_ref, sc_ref, sh_ref, x1_ref, h2_ref):
        yv = y_ref[...]
        rp = lax.rsqrt(_rowmean(yv * yv) + EPS)
        x1 = x_ref[...] + gate_ref[...] * ((yv * rp) * pg_ref[...])
        x1_ref[...] = x1
        r2 = lax.rsqrt(_rowmean(x1 * x1) + EPS)
        h2_ref[...] = ((x1 * r2) * g2_ref[...] * (1.0 + sc_ref[...]) + sh_ref[...]).astype(BF16)

    blk = pl.BlockSpec((tb, d), lambda i: (i, 0))
    return pl.pallas_call(
        body, name=name, grid=(s // tb,), in_specs=[blk, blk] + [_row_spec(d)] * 5,
        out_specs=[blk, blk],
        out_shape=[jax.ShapeDtypeStruct((s, d), F32), jax.ShapeDtypeStruct((s, d), BF16)],
        compiler_params=_params(),
    )(x, y, gate, pg, g2, scale2, shift2)


def _post_bwd(y, gate, pg, name, *, dxo=None, xin=None, target=None):
    s, d = y.shape
    tb = _div_tile(s, 256, SUBLANES)
    from_loss = target is not None

    def body(*refs):
        if from_loss:
            y_ref, gate_ref, pg_ref, xin_ref, t_ref, dy_ref, dgate_ref, dpg_ref, dxo_ref, loss_ref = refs
        else:
            y_ref, gate_ref, pg_ref, dxo_in_ref, dy_ref, dgate_ref, dpg_ref = refs
        yv = y_ref[...]
        rp = lax.rsqrt(_rowmean(yv * yv) + EPS)
        yh = yv * rp
        fn = yh * pg_ref[...]
        gate = gate_ref[...]
        if from_loss:
            err = xin_ref[...] + gate * fn - t_ref[...]
            dxo = err * (1.0 / d)
            dxo_ref[...] = dxo
            part = 0.5 * jnp.sum(_rowmean(err * err), axis=0, keepdims=True)
            _accumulate(loss_ref, jnp.broadcast_to(part, loss_ref.shape))
        else:
            dxo = dxo_in_ref[...]
        _accumulate(dgate_ref, _colsum(dxo * fn))
        dfn = dxo * gate
        _accumulate(dpg_ref, _colsum(dfn * yh))
        dyh = dfn * pg_ref[...]
        dy_ref[...] = (rp * (dyh - yh * _rowmean(dyh * yh))).astype(BF16)

    blk = pl.BlockSpec((tb, d), lambda i: (i, 0))
    in_specs = [blk, _row_spec(d), _row_spec(d)]
    out_specs = [blk, _row_spec(d), _row_spec(d)]
    out_shape = [jax.ShapeDtypeStruct((s, d), BF16), jax.ShapeDtypeStruct((1, d), F32),
                 jax.ShapeDtypeStruct((1, d), F32)]
    if from_loss:
        operands = (y, gate, pg, xin, target)
        in_specs += [blk, blk]
        out_specs += [blk, _row_spec(LANES)]
        out_shape += [jax.ShapeDtypeStruct((s, d), F32), jax.ShapeDtypeStruct((1, LANES), F32)]
    else:
        operands = (y, gate, pg, dxo)
        in_specs += [blk]
    return pl.pallas_call(
        body, name=name, grid=(s // tb,), in_specs=in_specs, out_specs=out_specs, out_shape=out_shape,
        compiler_params=_params(),
    )(*operands)


def _prenorm_bwd(xin, dh, dres, g, scale, name, comm=None):
    s, d = xin.shape
    tb = _div_tile(s, 256, SUBLANES)

    def body(x_ref, dh_ref, dres_ref, g_ref, sc_ref, dx_ref, dshift_ref, dscale_ref, dg_ref):
        xv = x_ref[...]
        r = lax.rsqrt(_rowmean(xv * xv) + EPS)
        xn = xv * r
        dh = dh_ref[...]
        g1 = g_ref[...]
        s1 = 1.0 + sc_ref[...]
        _accumulate(dshift_ref, _colsum(dh))
        _accumulate(dscale_ref, _colsum(dh * xn * g1))
        _accumulate(dg_ref, _colsum(dh * xn * s1))
        dxn = dh * g1 * s1
        dx_ref[...] = dres_ref[...] + r * (dxn - xn * _rowmean(dxn * xn))

    blk = pl.BlockSpec((tb, d), lambda i: (i, 0))
    return _call(
        body, (xin, dh, dres, g, scale), comm, name=name, grid=(s // tb,),
        in_specs=[blk, blk, blk, _row_spec(d), _row_spec(d)],
        out_specs=[blk, _row_spec(d), _row_spec(d), _row_spec(d)],
        out_shape=[jax.ShapeDtypeStruct((s, d), F32)] + [jax.ShapeDtypeStruct((1, d), F32)] * 3)


def _merge(z_big, y_a, y_b, name):
    s, d = y_a.shape
    tb = _div_tile(s, 256, SUBLANES)

    def body(zg_ref, ya_ref, yb_ref, o_ref):
        ga, gb = zg_ref[:, :d].astype(F32), zg_ref[:, d:].astype(F32)
        o_ref[...] = (_sigmoid(ga) * ya_ref[...].astype(F32) + _sigmoid(gb) * yb_ref[...].astype(F32)).astype(BF16)

    blk = pl.BlockSpec((tb, d), lambda i: (i, 0))
    return pl.pallas_call(
        body, name=name, grid=(s // tb,), in_specs=[pl.BlockSpec((tb, 2 * d), lambda i: (i, 1)), blk, blk],
        out_specs=blk, out_shape=jax.ShapeDtypeStruct((s, d), BF16), compiler_params=_params(),
    )(z_big, y_a, y_b)


def _merge_bwd(dmerged, z_big, y_a, y_b, name):
    s, d = y_a.shape
    tb = _div_tile(s, 256, SUBLANES)

    def body(dm_ref, zg_ref, ya_ref, yb_ref, dya_ref, dyb_ref, dz_ref):
        dm = dm_ref[...].astype(F32)
        sa, sb = _sigmoid(zg_ref[:, :d].astype(F32)), _sigmoid(zg_ref[:, d:].astype(F32))
        dya_ref[...] = (dm * sa).astype(BF16)
        dyb_ref[...] = (dm * sb).astype(BF16)
        dz_ref[:, :d] = (dm * ya_ref[...].astype(F32) * sa * (1.0 - sa)).astype(BF16)
        dz_ref[:, d:] = (dm * yb_ref[...].astype(F32) * sb * (1.0 - sb)).astype(BF16)

    blk = pl.BlockSpec((tb, d), lambda i: (i, 0))
    wide = pl.BlockSpec((tb, 2 * d), lambda i: (i, 1))
    return pl.pallas_call(
        body, name=name, grid=(s // tb,), in_specs=[blk, wide, blk, blk], out_specs=[blk, blk, wide],
        out_shape=[jax.ShapeDtypeStruct((s, d), BF16), jax.ShapeDtypeStruct((s, d), BF16),
                   jax.ShapeDtypeStruct((s, 4 * d), BF16)],
        compiler_params=_params(),
    )(dmerged, z_big, y_a, y_b)


def _causal_mask(ch):
    q = lax.broadcasted_iota(jnp.int32, (ch, ch), 0)
    p = lax.broadcasted_iota(jnp.int32, (ch, ch), 1)
    return (p <= q).astype(F32)


def _gmlp_norm(zc, lng, lnb, gw):
    u_pre, v_pre = zc[:, :gw], zc[:, gw:]
    vg = _gelu(v_pre)
    mu = _rowmean(vg)
    cen = vg - mu
    rstd = lax.rsqrt(_rowmean(cen * cen) + EPS)
    vhat = cen * rstd
    return u_pre, v_pre, _gelu(u_pre), vhat, rstd, vhat * lng + lnb


def _gmlp_fwd(z_big, ln_g, ln_b, w_s, b_s_t, name):
    s = z_big.shape[0]
    groups, ch, _ = w_s.shape
    gw = ln_g.shape[1]
    gd = gw // groups

    def body(z_ref, lng_ref, lnb_ref, ws_ref, bt_ref, a_ref):
        _, _, u, _, _, vn = _gmlp_norm(z_ref[...].astype(F32), lng_ref[...], lnb_ref[...], gw)
        mask = _causal_mask(ch)
        for g in range(groups):
            cols = slice(g * gd, (g + 1) * gd)
            wm = (ws_ref[g] * mask).astype(BF16)
            mixed = _dot(wm, vn[:, cols].astype(BF16), NN) + bt_ref[:, g:g + 1]
            a_ref[:, cols] = (u[:, cols] * mixed).astype(BF16)

    return pl.pallas_call(
        body, name=name, grid=(s // ch,),
        in_specs=[pl.BlockSpec((ch, 2 * gw), lambda n: (n, 0)), _row_spec(gw), _row_spec(gw),
                  pl.BlockSpec((groups, ch, ch), lambda n: (0, 0, 0)), pl.BlockSpec((ch, groups), lambda n: (0, 0))],
        out_specs=pl.BlockSpec((ch, gw), lambda n: (n, 0)),
        out_shape=jax.ShapeDtypeStruct((s, gw), BF16), compiler_params=_params(),
    )(z_big, ln_g, ln_b, w_s, b_s_t)


def _gmlp_bwd(z_big, da, dz_big, ln_g, ln_b, w_s, b_s_t, name, comm=None):
    s = z_big.shape[0]
    groups, ch, _ = w_s.shape
    gw = ln_g.shape[1]
    gd = gw // groups

    def body(z_ref, da_ref, dzin_ref, lng_ref, lnb_ref, ws_ref, bt_ref, dz_ref, gws_ref, gbt_ref, glng_ref, glnb_ref,
             vg_ref, dvh_ref):
        del dzin_ref
        mask = _causal_mask(ch)
        lane = lax.broadcasted_iota(jnp.int32, (ch, LANES), 1)
        group_cols = [slice(g * gd, (g + 1) * gd) for g in range(groups)]
        rowsum = lambda v: jnp.sum(v, axis=1, keepdims=True)

        @pl.when(pl.program_id(0) == 0)
        def _():
            for ref in (gws_ref, gbt_ref, glng_ref, glnb_ref):
                ref[...] = jnp.zeros(ref.shape, F32)

        total = jnp.zeros((ch, 1), F32)
        for cols in group_cols:
            vg = _gelu(z_ref[:, gw + cols.start:gw + cols.stop].astype(F32))
            vg_ref[:, cols] = vg
            total = total + rowsum(vg)
        mu = total * (1.0 / gw)
        total = jnp.zeros((ch, 1), F32)
        for cols in group_cols:
            cen = vg_ref[:, cols] - mu
            total = total + rowsum(cen * cen)
        rstd = lax.rsqrt(total * (1.0 / gw) + EPS)
        m1, m2, gb = jnp.zeros((ch, 1), F32), jnp.zeros((ch, 1), F32), jnp.zeros((ch, LANES), F32)
        for g, cols in enumerate(group_cols):
            vhat = (vg_ref[:, cols] - mu) * rstd
            vn_g = (vhat * lng_ref[:, cols] + lnb_ref[:, cols]).astype(BF16)
            wm = (ws_ref[g] * mask).astype(BF16)
            mixed = _dot(wm, vn_g, NN) + bt_ref[:, g:g + 1]
            u_pre, da_g = z_ref[:, cols].astype(F32), da_ref[:, cols].astype(F32)
            dz_ref[:, cols] = (da_g * mixed * _gelu_grad(u_pre)).astype(BF16)
            dmixed = da_g * _gelu(u_pre)
            dm16 = dmixed.astype(BF16)
            dvn = _dot(wm, dm16, TN)
            gws_ref[g] += _dot(dm16, vn_g, NT) * mask
            gb = gb + jnp.where(lane == g, rowsum(dmixed), 0.0)
            glnb_ref[:, cols] += _colsum(dvn)
            glng_ref[:, cols] += _colsum(dvn * vhat)
            dvh = dvn * lng_ref[:, cols]
            dvh_ref[:, cols] = dvh
            m1, m2 = m1 + rowsum(dvh), m2 + rowsum(dvh * vhat)
        gbt_ref[...] += gb
        m1, m2 = m1 * (1.0 / gw), m2 * (1.0 / gw)
        for cols in group_cols:
            vhat = (vg_ref[:, cols] - mu) * rstd
            dvg = rstd * (dvh_ref[:, cols] - m1 - vhat * m2)
            v_pre = z_ref[:, gw + cols.start:gw + cols.stop].astype(F32)
            dz_ref[:, gw + cols.start:gw + cols.stop] = (dvg * _gelu_grad(v_pre)).astype(BF16)

    zspec = pl.BlockSpec((ch, 2 * gw), lambda n: (n, 0))
    return _call(
        body, (z_big, da, dz_big, ln_g, ln_b, w_s, b_s_t), comm, name=name, grid=(s // ch,),
        in_specs=[zspec, pl.BlockSpec((ch, gw), lambda n: (n, 0)), pl.BlockSpec(memory_space=HBM),
                  _row_spec(gw), _row_spec(gw), pl.BlockSpec((groups, ch, ch), lambda n: (0, 0, 0)),
                  pl.BlockSpec((ch, groups), lambda n: (0, 0))],
        out_specs=[zspec, pl.BlockSpec((groups, ch, ch), lambda n: (0, 0, 0)),
                   pl.BlockSpec((ch, LANES), lambda n: (0, 0)), _row_spec(gw), _row_spec(gw)],
        out_shape=[jax.ShapeDtypeStruct(dz_big.shape, BF16), jax.ShapeDtypeStruct((groups, ch, ch), F32),
                   jax.ShapeDtypeStruct((ch, LANES), F32), jax.ShapeDtypeStruct((1, gw), F32),
                   jax.ShapeDtypeStruct((1, gw), F32)],
        scratch_shapes=[pltpu.VMEM((ch, gw), F32)] * 2, input_output_aliases={2: 0})


def _mla_prep(z_lat, q_g, kv_g, rope_k, name):
    s, latw = z_lat.shape
    ql, kvl = q_g.shape[1], kv_g.shape[1]
    tb = _div_tile(s, 256, SUBLANES)

    def body(z_ref, qg_ref, kvg_ref, t_ref, qn_ref, kvn_ref, kr_ref):
        q = z_ref[:, :ql]
        qn_ref[...] = ((q * lax.rsqrt(_rowmean(q * q) + EPS)) * qg_ref[...]).astype(BF16)
        kv = z_ref[:, ql:ql + kvl]
        kvn_ref[...] = ((kv * lax.rsqrt(_rowmean(kv * kv) + EPS)) * kvg_ref[...]).astype(BF16)
        kk = z_ref[:, ql + kvl:] * t_ref[...]
        kr_ref[...] = (kk + pltpu.roll(kk, ROPE, axis=1)).astype(BF16)

    return pl.pallas_call(
        body, name=name, grid=(s // tb,),
        in_specs=[pl.BlockSpec((tb, latw), lambda i: (i, 0)), _row_spec(ql), _row_spec(kvl),
                  pl.BlockSpec((tb, 2 * ROPE), lambda i: (i, 0))],
        out_specs=[pl.BlockSpec((tb, ql), lambda i: (i, 0)), pl.BlockSpec((tb, kvl), lambda i: (i, 0)),
                   pl.BlockSpec((tb, 2 * ROPE), lambda i: (i, 0))],
        out_shape=[jax.ShapeDtypeStruct((s, ql), BF16), jax.ShapeDtypeStruct((s, kvl), BF16),
                   jax.ShapeDtypeStruct((s, 2 * ROPE), BF16)],
        compiler_params=_params(),
    )(z_lat, q_g, kv_g, rope_k)


def _attn_fwd(q, kv, kr, heads, name, comm=None):
    s = q.shape[0]
    t = _div_tile(s, 512)
    nb = s // t
    hp = 2 if heads % 2 == 0 else 1

    def body(q_ref, k_ref, kr_ref, v_ref, o_ref, lse_ref, m_ref, l_ref, acc_ref):
        i, j = pl.program_id(1), pl.program_id(2)

        @pl.when(j == 0)
        def _():
            m_ref[...] = jnp.full(m_ref.shape, -1e30, F32)
            l_ref[...] = jnp.zeros(l_ref.shape, F32)
            acc_ref[...] = jnp.zeros(acc_ref.shape, F32)

        def update(h, rows, n_keys, on_diagonal):
            vc = slice(h * VHEAD, (h + 1) * VHEAD)
            k_full = jnp.concatenate([k_ref[:n_keys, h * NOPE:(h + 1) * NOPE], kr_ref[:n_keys, :]], axis=1)
            sc = _dot(q_ref[rows, h * HEAD_W:(h + 1) * HEAD_W], k_full, NT)
            if on_diagonal:
                row_pos = rows.start + lax.broadcasted_iota(jnp.int32, sc.shape, 0)
                sc = jnp.where(lax.broadcasted_iota(jnp.int32, sc.shape, 1) <= row_pos, sc, -1e30)
            m_old = m_ref[h, rows, :]
            m_new = jnp.maximum(m_old, jnp.max(sc, axis=-1, keepdims=True))
            p = jnp.exp(sc - m_new)
            alpha = jnp.exp(m_old - m_new)
            l_new = alpha * l_ref[h, rows, :] + jnp.sum(p, axis=-1, keepdims=True)
            acc = alpha * acc_ref[rows, vc] + _dot(p.astype(BF16), v_ref[:n_keys, vc], NN)
            if on_diagonal:
                o_ref[rows, vc] = (acc / l_new).astype(BF16)
                lse_ref[h, rows, :] = jnp.broadcast_to(m_new + jnp.log(l_new), (rows.stop - rows.start, LANES))
            else:
                m_ref[h, rows, :], l_ref[h, rows, :], acc_ref[rows, vc] = m_new, l_new, acc

        def below_diagonal():
            for h in range(hp):
                update(h, slice(0, t), t, False)

        def on_diagonal():
            for h in range(hp):
                update(h, slice(0, t // 2), t // 2, True)
                update(h, slice(t // 2, t), t, True)

        pl.when(j < i)(below_diagonal)
        pl.when(j == i)(on_diagonal)

    kidx = lambda off: (lambda h, i, j: (jnp.minimum(i, j), off(h)))
    return _call(
        body, (q, kv, kr, kv), comm, name=name, grid=(heads // hp, nb, nb),
        in_specs=[pl.BlockSpec((t, hp * HEAD_W), lambda h, i, j: (i, h)),
                  pl.BlockSpec((t, hp * NOPE), kidx(lambda h: h)),
                  pl.BlockSpec((t, 2 * ROPE), kidx(lambda h: 0)),
                  pl.BlockSpec((t, hp * VHEAD), kidx(lambda h: heads // hp + h))],
        out_specs=[pl.BlockSpec((t, hp * VHEAD), lambda h, i, j: (i, h)),
                   pl.BlockSpec((hp, t, LANES), lambda h, i, j: (h, i, 0))],
        out_shape=[jax.ShapeDtypeStruct((s, heads * VHEAD), BF16), jax.ShapeDtypeStruct((heads, s, LANES), F32)],
        scratch_shapes=[pltpu.VMEM((hp, t, 1), F32), pltpu.VMEM((hp, t, 1), F32), pltpu.VMEM((t, hp * VHEAD), F32)])


def _attn_bwd(q, kv, kr, o, do, lse, heads, name, comm=None):
    s = q.shape[0]
    t = _div_tile(s, 512)
    nb = s // t
    hp = 2 if heads % 2 == 0 else 1

    def body(q_ref, k_ref, kr_ref, v_ref, o_ref, do_ref, lse_ref, dq_ref, dk_ref, dv_ref, dk_acc, dv_acc):
        j, i = pl.program_id(1), pl.program_id(2)

        @pl.when(jnp.logical_and(j == 0, i == 0))
        def _():
            dq_ref[...] = jnp.zeros(dq_ref.shape, F32)

        def update(h, rows, n_keys, on_diagonal, assign):
            qc, kc, vc = (slice(h * w, (h + 1) * w) for w in (HEAD_W, NOPE, VHEAD))
            n_rows = rows.stop - rows.start
            qv, do_v = q_ref[rows, qc], do_ref[rows, vc]
            k_full = jnp.concatenate([k_ref[:n_keys, kc], kr_ref[:n_keys, :]], axis=1)
            sc = _dot(qv, k_full, NT)
            if on_diagonal:
                row_pos = rows.start + lax.broadcasted_iota(jnp.int32, sc.shape, 0)
                sc = jnp.where(lax.broadcasted_iota(jnp.int32, sc.shape, 1) <= row_pos, sc, -1e30)
            p = jnp.exp(sc - lse_ref[h, rows, :1])
            dp = _dot(do_v, v_ref[:n_keys, vc], NT)
            delta = jnp.sum(do_v.astype(F32) * o_ref[rows, vc].astype(F32), axis=-1, keepdims=True)
            ds = (p * (dp - delta)).astype(BF16)
            dq_ref[pl.ds(pl.multiple_of(i * t + rows.start, n_rows), n_rows), qc] += _dot(ds, k_full, NN)
            dv_part, dk_part = _dot(p.astype(BF16), do_v, TN), _dot(ds, qv, TN)
            if assign:
                dv_acc[:n_keys, vc], dk_acc[:n_keys, qc] = dv_part, dk_part
            else:
                dv_acc[:n_keys, vc] += dv_part
                dk_acc[:n_keys, qc] += dk_part

        def on_diagonal():
            for h in range(hp):
                update(h, slice(t // 2, t), t, True, True)
                update(h, slice(0, t // 2), t // 2, True, False)

        def below_diagonal():
            for h in range(hp):
                update(h, slice(0, t), t, False, False)

        pl.when(i == j)(on_diagonal)
        pl.when(i > j)(below_diagonal)

        @pl.when(i == nb - 1)
        def _():
            dk_ref[...] = dk_acc[...].astype(BF16)
            dv_ref[...] = dv_acc[...].astype(BF16)

    qidx = lambda h, j, i: (jnp.maximum(i, j), h)
    return _call(
        body, (q, kv, kr, kv, o, do, lse), comm, name=name, grid=(heads // hp, nb, nb),
        in_specs=[pl.BlockSpec((t, hp * HEAD_W), qidx),
                  pl.BlockSpec((t, hp * NOPE), lambda h, j, i: (j, h)),
                  pl.BlockSpec((t, 2 * ROPE), lambda h, j, i: (j, 0)),
                  pl.BlockSpec((t, hp * VHEAD), lambda h, j, i: (j, heads // hp + h)),
                  pl.BlockSpec((t, hp * VHEAD), qidx), pl.BlockSpec((t, hp * VHEAD), qidx),
                  pl.BlockSpec((hp, t, LANES), lambda h, j, i: (h, jnp.maximum(i, j), 0))],
        out_specs=[pl.BlockSpec((s, hp * HEAD_W), lambda h, j, i: (0, h)),
                   pl.BlockSpec((t, hp * HEAD_W), lambda h, j, i: (j, h)),
                   pl.BlockSpec((t, hp * VHEAD), lambda h, j, i: (j, h))],
        out_shape=[jax.ShapeDtypeStruct((s, heads * HEAD_W), F32), jax.ShapeDtypeStruct((s, heads * HEAD_W), BF16),
                   jax.ShapeDtypeStruct((s, heads * VHEAD), BF16)],
        scratch_shapes=[pltpu.VMEM((t, hp * HEAD_W), F32), pltpu.VMEM((t, hp * VHEAD), F32)])


def _mla_bwd_mid(dq, dk, dv, rope_q, rope_k, heads, name):
    s = dq.shape[0]
    tb = _div_tile(s, 256, SUBLANES)

    def body(dq_ref, dk_ref, dv_ref, tq_ref, tk_ref, dqb_ref, dkv_ref, dkk_ref):
        tq = tq_ref[...]
        dkr = jnp.zeros((tb, 2 * ROPE), F32)
        for h in range(heads):
            cols = slice(h * HEAD_W, (h + 1) * HEAD_W)
            dqb_ref[:, cols] = (dq_ref[:, cols] * tq).astype(BF16)
            dkv_ref[:, h * NOPE:(h + 1) * NOPE] = dk_ref[:, h * HEAD_W:h * HEAD_W + NOPE]
            dkr = dkr + dk_ref[:, h * HEAD_W + NOPE:(h + 1) * HEAD_W].astype(F32)
        dkv_ref[:, heads * NOPE:] = dv_ref[...]
        dkk_ref[...] = (dkr + pltpu.roll(dkr, ROPE, axis=1)) * tk_ref[...]

    wq, wv = heads * HEAD_W, heads * VHEAD
    return pl.pallas_call(
        body, name=name, grid=(s // tb,),
        in_specs=[pl.BlockSpec((tb, wq), lambda i: (i, 0)), pl.BlockSpec((tb, wq), lambda i: (i, 0)),
                  pl.BlockSpec((tb, wv), lambda i: (i, 0)), pl.BlockSpec((tb, HEAD_W), lambda i: (i, 0)),
                  pl.BlockSpec((tb, 2 * ROPE), lambda i: (i, 0))],
        out_specs=[pl.BlockSpec((tb, wq), lambda i: (i, 0)), pl.BlockSpec((tb, heads * NOPE + wv), lambda i: (i, 0)),
                   pl.BlockSpec((tb, 2 * ROPE), lambda i: (i, 0))],
        out_shape=[jax.ShapeDtypeStruct((s, wq), BF16), jax.ShapeDtypeStruct((s, heads * NOPE + wv), BF16),
                   jax.ShapeDtypeStruct((s, 2 * ROPE), F32)],
        compiler_params=_params(),
    )(dq, dk, dv, rope_q, rope_k)


def _mla_bwd_post(z_lat, dqn, dkvn, dkk, q_g, kv_g, name):
    s, latw = z_lat.shape
    ql, kvl = q_g.shape[1], kv_g.shape[1]
    tb = _div_tile(s, 256, SUBLANES)

    def norm_bwd(xv, dn, g, dg_ref):
        r = lax.rsqrt(_rowmean(xv * xv) + EPS)
        xh = xv * r
        _accumulate(dg_ref, _colsum(dn * xh))
        dxh = dn * g
        return r * (dxh - xh * _rowmean(dxh * xh))

    def body(z_ref, dqn_ref, dkvn_ref, dkk_ref, qg_ref, kvg_ref, dz_ref, gq_ref, gkv_ref):
        dz_ref[:, :ql] = norm_bwd(z_ref[:, :ql], dqn_ref[...], qg_ref[...], gq_ref).astype(BF16)
        dz_ref[:, ql:ql + kvl] = norm_bwd(z_ref[:, ql:ql + kvl], dkvn_ref[...], kvg_ref[...], gkv_ref).astype(BF16)
        dz_ref[:, ql + kvl:] = dkk_ref[...].astype(BF16)

    return pl.pallas_call(
        body, name=name, grid=(s // tb,),
        in_specs=[pl.BlockSpec((tb, latw), lambda i: (i, 0)), pl.BlockSpec((tb, ql), lambda i: (i, 0)),
                  pl.BlockSpec((tb, kvl), lambda i: (i, 0)), pl.BlockSpec((tb, 2 * ROPE), lambda i: (i, 0)),
                  _row_spec(ql), _row_spec(kvl)],
        out_specs=[pl.BlockSpec((tb, latw), lambda i: (i, 0)), _row_spec(ql), _row_spec(kvl)],
        out_shape=[jax.ShapeDtypeStruct((s, latw), BF16), jax.ShapeDtypeStruct((1, ql), F32),
                   jax.ShapeDtypeStruct((1, kvl), F32)],
        compiler_params=_params(),
    )(z_lat, dqn, dkvn, dkk, q_g, kv_g)


CONV_ROWS = 128
CONV_HALO = 16


def _row_steps(n_rows, step):
    step(0, True)
    if n_rows > CONV_ROWS:
        def later(i, carry):
            step(pl.multiple_of(i * CONV_ROWS, CONV_ROWS), False)
            return carry
        lax.fori_loop(1, n_rows // CONV_ROWS, later, 0)


def _conv_taps(pre_ref, r0, first):
    if first:
        win = jnp.concatenate([jnp.zeros((CONV_HALO, pre_ref.shape[1]), F32), pre_ref[0:CONV_ROWS, :].astype(F32)])
    else:
        win = pre_ref[pl.ds(pl.multiple_of(r0 - CONV_HALO, CONV_HALO), CONV_ROWS + CONV_HALO), :].astype(F32)
    return win[CONV_HALO:], pltpu.roll(win, 1, axis=0)[CONV_HALO:], pltpu.roll(win, 2, axis=0)[CONV_HALO:]


def _conv(taps, w_ref, b_ref):
    return w_ref[2:3, :] * taps[0] + w_ref[1:2, :] * taps[1] + w_ref[0:1, :] * taps[2] + b_ref[...]


def _conv_fwd(up_pre, conv_w, conv_b, name):
    s, ff2 = up_pre.shape
    ff = ff2 // 2
    tc = _div_tile(ff, 256)
    nb = ff // tc
    assert s % CONV_ROWS == 0

    def body(pg_ref, pv_ref, wg_ref, wv_ref, bg_ref, bv_ref, act_ref):
        def step(r0, first):
            gate = _conv(_conv_taps(pg_ref, r0, first), wg_ref, bg_ref)
            val = _conv(_conv_taps(pv_ref, r0, first), wv_ref, bv_ref)
            act_ref[pl.ds(r0, CONV_ROWS), :] = (gate * _sigmoid(gate) * val).astype(BF16)

        _row_steps(s, step)

    def col(rows, off):
        return pl.BlockSpec((rows, tc), lambda j: (0, j + off))

    return pl.pallas_call(
        body, name=name, grid=(nb,),
        in_specs=[col(s, 0), col(s, nb), col(CONV_TAPS, 0), col(CONV_TAPS, nb), col(1, 0), col(1, nb)],
        out_specs=col(s, 0), out_shape=jax.ShapeDtypeStruct((s, ff), BF16), compiler_params=_params(),
    )(up_pre, up_pre, conv_w, conv_w, conv_b, conv_b)


def _conv_bwd(up_pre, dact, conv_w, conv_b, name, comm=None):
    s, ff2 = up_pre.shape
    ff = ff2 // 2
    tc = _div_tile(ff, 256)
    nb = ff // tc
    assert s % CONV_ROWS == 0

    def body(pg_ref, pv_ref, da_ref, wg_ref, wv_ref, bg_ref, bv_ref, dup_ref, gwg_ref, gwv_ref, gbg_ref, gbv_ref,
             dxg_ref, dxv_ref):
        for ref in (gwg_ref, gwv_ref, gbg_ref, gbv_ref):
            ref[...] = jnp.zeros(ref.shape, F32)
        for ref in (dxg_ref, dxv_ref):
            ref[s:s + SUBLANES, :] = jnp.zeros((SUBLANES, tc), F32)

        def sums(taps, dx, gw_ref, gb_ref):
            gb_ref[...] += _colsum(dx)
            for k in range(CONV_TAPS):
                gw_ref[k:k + 1, :] += _colsum(dx * taps[CONV_TAPS - 1 - k])

        def forward(r0, first):
            rows = pl.ds(r0, CONV_ROWS)
            taps_g, taps_v = _conv_taps(pg_ref, r0, first), _conv_taps(pv_ref, r0, first)
            gate, val = _conv(taps_g, wg_ref, bg_ref), _conv(taps_v, wv_ref, bv_ref)
            da = da_ref[rows, :].astype(F32)
            sg = _sigmoid(gate)
            dxv, dxg = da * gate * sg, da * val * sg * (1.0 + gate * (1.0 - sg))
            dxv_ref[rows, :], dxg_ref[rows, :] = dxv, dxg
            sums(taps_v, dxv, gwv_ref, gbv_ref)
            sums(taps_g, dxg, gwg_ref, gbg_ref)

        def backward(r0, first):
            del first
            n = CONV_ROWS + SUBLANES
            for dx_ref, w_ref, out_ref in ((dxg_ref, wg_ref, dup_ref.at[0]), (dxv_ref, wv_ref, dup_ref.at[1])):
                win = dx_ref[pl.ds(r0, n), :]
                ahead1 = pltpu.roll(win, n - 1, axis=0)[:CONV_ROWS]
                ahead2 = pltpu.roll(win, n - 2, axis=0)[:CONV_ROWS]
                out_ref[pl.ds(r0, CONV_ROWS), :] = (w_ref[2:3, :] * win[:CONV_ROWS] + w_ref[1:2, :] * ahead1
                                                    + w_ref[0:1, :] * ahead2).astype(BF16)

        _row_steps(s, forward)
        _row_steps(s, backward)

    def col(rows, off):
        return pl.BlockSpec((rows, tc), lambda j: (0, j + off))

    return _call(
        body, (up_pre, up_pre, dact, conv_w, conv_w, conv_b, conv_b), comm, name=name, grid=(nb,),
        in_specs=[col(s, 0), col(s, nb), col(s, 0), col(CONV_TAPS, 0), col(CONV_TAPS, nb), col(1, 0), col(1, nb)],
        out_specs=[pl.BlockSpec((2, s, tc), lambda j: (0, 0, j)), col(CONV_TAPS, 0), col(CONV_TAPS, 0),
                   col(1, 0), col(1, 0)],
        out_shape=[jax.ShapeDtypeStruct((2, s, ff), BF16)] + [jax.ShapeDtypeStruct((CONV_TAPS, ff), F32)] * 2
        + [jax.ShapeDtypeStruct((1, ff), F32)] * 2,
        scratch_shapes=[pltpu.VMEM((s + SUBLANES, tc), F32)] * 2)


def _ada_fwd(c_all, w, b, name):
    nseq, d = c_all.shape
    na = w.shape[1]
    tn = _div_tile(na, 512)

    def body(c_ref, w_ref, b_ref, o_ref):
        cv = c_ref[...]
        sc = cv * _sigmoid(cv)
        o_ref[...] = jnp.dot(sc, w_ref[...], preferred_element_type=F32, precision=lax.Precision.HIGHEST) + b_ref[...]

    return pl.pallas_call(
        body, name=name, grid=(na // tn,),
        in_specs=[pl.BlockSpec((nseq, d), lambda j: (0, 0)), pl.BlockSpec((d, tn), lambda j: (0, j)),
                  pl.BlockSpec((1, tn), lambda j: (0, j))],
        out_specs=pl.BlockSpec((nseq, tn), lambda j: (0, j)),
        out_shape=jax.ShapeDtypeStruct((nseq, na), F32), compiler_params=_params(),
    )(c_all, w, b)


def _ada_bwd(c_all_t, dmod, name):
    d, nseq = c_all_t.shape
    na = dmod.shape[1]
    tm, tn = _div_tile(d, 512, SUBLANES), _div_tile(na, 1024)

    def body(c_ref, dm_ref, o_ref):
        cv = c_ref[...]
        o_ref[...] = jnp.dot(cv * _sigmoid(cv), dm_ref[...], preferred_element_type=F32,
                             precision=lax.Precision.HIGHEST)

    return pl.pallas_call(
        body, name=name, grid=(d // tm, na // tn),
        in_specs=[pl.BlockSpec((tm, nseq), lambda i, j: (i, 0)), pl.BlockSpec((nseq, tn), lambda i, j: (0, j))],
        out_specs=pl.BlockSpec((tm, tn), lambda i, j: (i, j)),
        out_shape=jax.ShapeDtypeStruct((d, na), F32), compiler_params=_params(),
    )(c_all_t, dmod)


def _adamw(w, g, m, v, name, comm=None, after=None):
    rows, cols = w.shape
    tb = _div_tile(rows, max(SUBLANES, ADAMW_BLOCK_ELEMS // cols // SUBLANES * SUBLANES), SUBLANES)
    c1 = 1.0 / (1.0 - ADAM_B1 ** ADAM_STEP)
    c2 = 1.0 / (1.0 - ADAM_B2 ** ADAM_STEP)

    def body(*refs):
        w_ref, g_ref, m_ref, v_ref = refs[:4]
        d_ref, nm_ref, nv_ref = refs[-3:]
        gv = g_ref[...]
        nm = ADAM_B1 * m_ref[...] + (1.0 - ADAM_B1) * gv
        nv = ADAM_B2 * v_ref[...] + (1.0 - ADAM_B2) * (gv * gv)
        nm_ref[...] = nm
        nv_ref[...] = nv
        d_ref[...] = -ADAM_LR * ((nm * c1) / (jnp.sqrt(nv * c2) + ADAM_EPS) + ADAM_WD * w_ref[...])

    blk = pl.BlockSpec((tb, cols), lambda i: (i, 0))
    operands, in_specs = (w, g, m, v), [blk] * 4
    if after is not None:
        operands, in_specs = operands + (after,), in_specs + [pl.BlockSpec(after.shape, lambda i: (0, 0))]
    return _call(body, operands, comm, name=name, grid=(rows // tb,), in_specs=in_specs, out_specs=[blk] * 3,
                 out_shape=[jax.ShapeDtypeStruct((rows, cols), F32)] * 3)


def _sum_leading(parts, name, after=()):
    n, rows, cols = parts.shape
    tb = _div_tile(rows, 512, SUBLANES)

    def body(p_ref, *rest):
        o_ref = rest[-1]
        acc = p_ref[0]
        for k in range(1, n):
            acc = acc + p_ref[k]
        o_ref[...] = acc

    return pl.pallas_call(
        body, name=name, grid=(rows // tb,),
        in_specs=[pl.BlockSpec((n, tb, cols), lambda i: (0, i, 0))] + [pl.BlockSpec(memory_space=pl.ANY)] * len(after),
        out_specs=pl.BlockSpec((tb, cols), lambda i: (i, 0)),
        out_shape=jax.ShapeDtypeStruct((rows, cols), F32), compiler_params=_params(),
    )(parts, *after)


def _place():
    x, y, c = lax.axis_index("x"), lax.axis_index("y"), lax.axis_index("c")
    return x, y, c, [(1 - x, y), (x, 1 - y), (1 - x, 1 - y)]


def _all_gather(block, name):
    m_per, n = block.shape

    def body(x_ref, out_ref, send_sems, recv_sems, local_sem):
        x, y, c, chips = _place()
        me, sibling = (x, y, c), (x, y, 1 - c)

        def rows(px, py, pc):
            return out_ref.at[pl.ds((4 * px + 2 * py + pc) * m_per, m_per), :]

        def copy(k, blk, to, src=None):
            return pltpu.make_async_remote_copy(
                src_ref=rows(*blk) if src is None else src, dst_ref=rows(*blk), send_sem=send_sems.at[k],
                recv_sem=recv_sems.at[k], device_id=to, device_id_type=MESH)

        mine = pltpu.make_async_copy(x_ref, rows(*me), local_sem)
        mine.start()
        first = [copy(0, me, sibling, src=x_ref)]
        first += [copy(1 + j, me, (*chip, c), src=x_ref) for j, chip in enumerate(chips)]
        for cp in first:
            cp.start()
        passed = [copy(4 + j, (*chip, c), sibling) for j, chip in enumerate(chips)]
        for j, chip in enumerate(chips):
            copy(1 + j, (*chip, c), me).wait_recv()
            passed[j].start()
        copy(0, sibling, me).wait_recv()
        for j, chip in enumerate(chips):
            copy(4 + j, (*chip, 1 - c), me).wait_recv()
        for cp in first + passed:
            cp.wait_send()
        mine.wait()

    return pl.pallas_call(
        body, name=name, out_shape=jax.ShapeDtypeStruct((N_DEV * m_per, n), block.dtype),
        in_specs=[pl.BlockSpec(memory_space=pltpu.VMEM)], out_specs=pl.BlockSpec(memory_space=pltpu.VMEM),
        scratch_shapes=[pltpu.SemaphoreType.DMA((7,)), pltpu.SemaphoreType.DMA((7,)), pltpu.SemaphoreType.DMA],
        compiler_params=_params(),
    )(block)


def _hbm_specs(n):
    return [pl.BlockSpec(memory_space=HBM)] * n


def _part(ref, by_cols, half, quarter=None, lead=None):
    extent = ref.shape[-1] if by_cols else ref.shape[-2]
    size = extent // 2 if quarter is None else extent // 4
    first = half * (extent // 2) + (0 if quarter is None else quarter * size)
    tile = LANES if by_cols else 2 * SUBLANES
    span = pl.ds(pl.multiple_of(first, tile) if size % tile == 0 else first, size)
    index = (slice(None), span) if by_cols else (span, slice(None))
    return ref.at[index] if lead is None else ref.at[(lead,) + index]


def _half_rows(ref, half, lead=None):
    return _part(ref, False, half, lead=lead)


class _Comm:
    def __init__(self, operands, out_shape, sem_dims, build, aliases=None):
        self.operands, self.out_shape, self.sem_dims = list(operands), list(out_shape), list(sem_dims)
        self.scratch = [pltpu.SemaphoreType.DMA(d) for d in sem_dims]
        self.build, self.aliases = build, dict(aliases or {})


class _SemGrid:
    def __init__(self, sems, dims):
        self.sems, self.dims, self.at = list(sems), tuple(dims), self

    def __getitem__(self, index):
        index = index if isinstance(index, tuple) else (index,)
        flat = 0
        for i, d in zip(index, self.dims):
            flat = flat * d + i
        return self.sems[flat]


def _call(body, operands, comm=None, *, name, grid, in_specs, out_specs, out_shape, scratch_shapes=(),
          input_output_aliases=None):
    aliases = dict(input_output_aliases or {})
    if comm is None:
        return pl.pallas_call(
            body, name=name, grid=grid, in_specs=in_specs, out_specs=out_specs, out_shape=out_shape,
            scratch_shapes=list(scratch_shapes), input_output_aliases=aliases, compiler_params=_params())(*operands)
    single = not isinstance(out_shape, (list, tuple))
    outs = [out_shape] if single else list(out_shape)
    ospecs = [out_specs] if single else list(out_specs)
    n_in, n_out, n_scr = len(operands), len(outs), len(scratch_shapes)
    c_in, c_out = len(comm.operands), len(comm.out_shape)
    for i, o in comm.aliases.items():
        aliases[n_in + i] = n_out + o

    def hosted(*refs):
        ins, c_ins = refs[:n_in], refs[n_in:n_in + c_in]
        o0 = n_in + c_in
        o_refs, c_outs = refs[o0:o0 + n_out], refs[o0 + n_out:o0 + n_out + c_out]
        s0 = o0 + n_out + c_out
        scr, sems = refs[s0:s0 + n_scr], refs[s0 + n_scr:]
        stages = comm.build(c_ins, c_outs, sems)
        step, n_steps = 0, 1
        for dim, size in enumerate(grid):
            step, n_steps = step * size + pl.program_id(dim), n_steps * size
        pl.when(step == 0)(stages[0])
        body(*ins, *o_refs, *scr)
        for stage in stages[1:-1]:
            pl.when(step == (n_steps * MIDDLE_STAGE_AT) // 100)(stage)
        pl.when(step == n_steps - 1)(stages[-1])

    res = pl.pallas_call(
        hosted, name=name, grid=grid, in_specs=list(in_specs) + _hbm_specs(c_in),
        out_specs=ospecs + _hbm_specs(c_out), out_shape=outs + comm.out_shape,
        scratch_shapes=list(scratch_shapes) + comm.scratch, input_output_aliases=aliases,
        compiler_params=_params())(*operands, *comm.operands)
    return (res[0] if single else res[:n_out]), res[n_out:]


def _run_comm(comm, name):
    c_in, c_out = len(comm.operands), len(comm.out_shape)

    def body(*refs):
        for stage in comm.build(refs[:c_in], refs[c_in:c_in + c_out], refs[c_in + c_out:]):
            stage()

    return pl.pallas_call(
        body, name=name, in_specs=_hbm_specs(c_in), out_specs=_hbm_specs(c_out), out_shape=comm.out_shape,
        scratch_shapes=comm.scratch, input_output_aliases=comm.aliases, compiler_params=_params())(*comm.operands)


def _join_comms(comms):
    def build(in_refs, out_refs, sems):
        staged, i, o, k = [], 0, 0, 0
        for cm in comms:
            ni, no, ns = len(cm.operands), len(cm.out_shape), len(cm.sem_dims)
            staged.append(cm.build(in_refs[i:i + ni], out_refs[o:o + no], sems[k:k + ns]))
            i, o, k = i + ni, o + no, k + ns
        def run(fns):
            def stage():
                for fn in fns:
                    fn()
            return stage

        return (run([st[0] for st in staged]), run([fn for st in staged for fn in st[1:-1]]),
                run([st[-1] for st in staged]))

    aliases, i, o = {}, 0, 0
    for cm in comms:
        aliases.update({i + a: o + b for a, b in cm.aliases.items()})
        i, o = i + len(cm.operands), o + len(cm.out_shape)
    return _Comm(sum((cm.operands for cm in comms), []), sum((cm.out_shape for cm in comms), []),
                 sum((cm.sem_dims for cm in comms), []), build, aliases)


def _gather8_comm(block):
    def build(in_refs, out_refs, sems):
        (src,), (out,), (send_sems, recv_sems) = in_refs, out_refs, sems
        x, y, c, chips = _place()
        me, sibling = (x, y, c), (x, y, 1 - c)

        def copy(k, blk, to, own=False):
            dst = out.at[4 * blk[0] + 2 * blk[1] + blk[2]]
            return pltpu.make_async_remote_copy(
                src_ref=src if own else dst, dst_ref=dst, send_sem=send_sems.at[k], recv_sem=recv_sems.at[k],
                device_id=to, device_id_type=MESH)

        first = [copy(0, me, sibling, own=True)] + [copy(1 + j, me, (*chip, c), own=True)
                                                     for j, chip in enumerate(chips)]
        passed = [copy(4 + j, (*chip, c), sibling) for j, chip in enumerate(chips)]

        def start():
            for cp in first:
                cp.start()

        def middle():
            for j, chip in enumerate(chips):
                copy(1 + j, (*chip, c), me).wait_recv()
                passed[j].start()

        def finish():
            copy(0, sibling, me).wait_recv()
            for j, chip in enumerate(chips):
                copy(4 + j, (*chip, 1 - c), me).wait_recv()
            for cp in first + passed:
                cp.wait_send()

        return start, middle, finish

    return _Comm([block], [jax.ShapeDtypeStruct((N_DEV,) + block.shape, block.dtype)], [(7,), (7,)], build)


def _gather_comm(shards, by_cols=()):
    nw = len(shards)

    def build(in_refs, out_refs, sems):
        send_sems, recv_sems = sems
        x, y, c, chips = _place()
        me, sibling = (x, y, c), (x, y, 1 - c)
        across_x, across_y, diagonal = chips

        def copy(w, k, block, part, to, src=None):
            dst = _part(out_refs[w], w in by_cols, part[1], part[2] if part[0] else None, 2 * block[0] + block[1])
            return pltpu.make_async_remote_copy(
                src_ref=dst if src is None else src, dst_ref=dst, send_sem=send_sems.at[w, k],
                recv_sem=recv_sems.at[w, k], device_id=to, device_id_type=MESH)

        first = [copy(w, j, (x, y), (0, c), (*chip, c), src=_part(in_refs[w], w in by_cols, c))
                 for w in range(nw) for j, chip in enumerate((across_x, across_y))]
        first += [pltpu.make_async_remote_copy(
            src_ref=in_refs[w], dst_ref=out_refs[w].at[2 * x + y], send_sem=send_sems.at[w, 8],
            recv_sem=recv_sems.at[w, 8], device_id=sibling, device_id_type=MESH) for w in range(nw)]
        passed = [[copy(w, 2, across_x, (1, c, 0), (*across_y, c)), copy(w, 3, across_y, (1, c, 1), (*across_x, c)),
                   copy(w, 4, across_x, (0, c), sibling), copy(w, 5, across_y, (0, c), sibling)] for w in range(nw)]
        last = [[copy(w, 6, diagonal, (1, c, 0), sibling), copy(w, 7, diagonal, (1, c, 1), sibling)]
                for w in range(nw)]

        def start():
            for cp in first:
                cp.start()

        def middle():
            for w in range(nw):
                copy(w, 0, across_x, (0, c), me).wait_recv()
                copy(w, 1, across_y, (0, c), me).wait_recv()
                for cp in passed[w]:
                    cp.start()

        def finish():
            for w in range(nw):
                copy(w, 2, diagonal, (1, c, 0), me).wait_recv()
                copy(w, 3, diagonal, (1, c, 1), me).wait_recv()
                for cp in last[w]:
                    cp.start()
            for w in range(nw):
                for k, block, part in ((4, across_x, (0, 1 - c)), (5, across_y, (0, 1 - c)),
                                       (6, diagonal, (1, 1 - c, 0)), (7, diagonal, (1, 1 - c, 1))):
                    copy(w, k, block, part, me).wait_recv()
                pltpu.make_async_remote_copy(
                    src_ref=in_refs[w], dst_ref=out_refs[w].at[2 * x + y], send_sem=send_sems.at[w, 8],
                    recv_sem=recv_sems.at[w, 8], device_id=sibling, device_id_type=MESH).wait_recv()
            for cp in first + sum(passed, []) + sum(last, []):
                cp.wait_send()

        return start, middle, finish

    return _Comm(shards, [jax.ShapeDtypeStruct((N_CHIPS,) + w.shape, w.dtype) for w in shards],
                 [(nw, 9), (nw, 9)], build)


def _halved(shape, by_cols):
    return shape[:-1] + (shape[-1] // 2,) if by_cols else shape[:-2] + (shape[-2] // 2, shape[-1])


def _swap_comm(gs, by_cols=()):
    nw = len(gs)

    def build(in_refs, out_refs, sems):
        send_sems, recv_sems = sems
        x, y, c, _ = _place()
        cps = []
        for w in range(nw):
            cps.append(pltpu.make_async_remote_copy(
                src_ref=_part(in_refs[w], w in by_cols, 1 - c, lead=slice(None)), dst_ref=out_refs[w],
                send_sem=send_sems.at[w], recv_sem=recv_sems.at[w], device_id=(x, y, 1 - c), device_id_type=MESH))

        def start():
            for cp in cps:
                cp.start()

        def finish():
            for cp in cps:
                cp.wait()

        return start, finish

    return _Comm(gs, [jax.ShapeDtypeStruct(_halved(g.shape, w in by_cols), g.dtype) for w, g in enumerate(gs)],
                 [(nw,), (nw,)], build)


def _exchange_comm(s1s):
    nw = len(s1s)

    def build(in_refs, out_refs, sems):
        send_sems, recv_sems = sems
        x, y, c, chips = _place()
        cps = [pltpu.make_async_remote_copy(
            src_ref=in_refs[w].at[2 * chip[0] + chip[1]], dst_ref=out_refs[w].at[j], send_sem=send_sems.at[w, j],
            recv_sem=recv_sems.at[w, j], device_id=(*chip, c), device_id_type=MESH)
            for w in range(nw) for j, chip in enumerate(chips)]

        def start():
            for cp in cps:
                cp.start()

        def finish():
            for cp in cps:
                cp.wait()

        return start, finish

    return _Comm(s1s, [jax.ShapeDtypeStruct((N_CHIPS - 1,) + s.shape[1:], s.dtype) for s in s1s],
                 [(nw, 3), (nw, 3)], build)


def _size(dims):
    n = 1
    for d in dims:
        n *= d
    return n


def _sem_grids(comm, sem_refs):
    grids, pos = [], 0
    for dims in comm.sem_dims:
        grids.append(_SemGrid(sem_refs[pos:pos + _size(dims)], dims))
        pos += _size(dims)
    return grids


def _comm_split_start(comm, name, after=()):
    c_in, c_out = len(comm.operands), len(comm.out_shape)
    counts = [_size(d) for d in comm.sem_dims]
    n_sem = sum(counts)
    assert not comm.aliases

    def body(*refs):
        srcs, lands = refs[:c_in], refs[c_in:c_in + c_out]
        first_sem = c_in + c_out + len(after)
        start, _ = comm.build(srcs, lands, _sem_grids(comm, refs[first_sem:first_sem + n_sem]))
        start()
        refs[-1][...] = jnp.zeros(refs[-1].shape, refs[-1].dtype)

    lands = [pltpu.with_memory_space_constraint(lax.empty(o.shape, o.dtype), HBM) for o in comm.out_shape]
    srcs = [pltpu.with_memory_space_constraint(a, HBM) for a in comm.operands]
    res = pl.pallas_call(
        body, name=name, in_specs=_hbm_specs(c_in + c_out) + [pl.BlockSpec(memory_space=pl.ANY)] * len(after),
        out_specs=[pl.BlockSpec(memory_space=pltpu.SEMAPHORE)] * n_sem + _hbm_specs(c_in + c_out)
        + [pl.BlockSpec(memory_space=pltpu.VMEM)],
        out_shape=[pltpu.SemaphoreType.DMA(())] * n_sem + [pltpu.HBM(a.shape, a.dtype) for a in comm.operands]
        + [pltpu.HBM(o.shape, o.dtype) for o in comm.out_shape] + [jax.ShapeDtypeStruct((SUBLANES, LANES), F32)],
        input_output_aliases={i: n_sem + i for i in range(c_in + c_out)},
        compiler_params=_params(has_side_effects=pltpu.SideEffectType.DATAFLOW_SIDE_EFFECTING))(*srcs, *lands, *after)
    return res[:-1], res[-1]


def _comm_split_wait(comm, state, after, name):
    c_in, c_out, n_sem = len(comm.operands), len(comm.out_shape), sum(_size(d) for d in comm.sem_dims)
    sems, srcs, lands = state[:n_sem], state[n_sem:n_sem + c_in], state[n_sem + c_in:]

    def body(*refs):
        src_refs, land_refs = refs[:c_in], refs[c_in:c_in + c_out]
        _, finish = comm.build(src_refs, land_refs, _sem_grids(comm, refs[c_in + c_out:c_in + c_out + n_sem]))
        finish()

    sem_spec = pl.BlockSpec(memory_space=pltpu.SEMAPHORE)
    res = pl.pallas_call(
        body, name=name, in_specs=_hbm_specs(c_in + c_out) + [sem_spec] * n_sem + [pl.BlockSpec(memory_space=pl.ANY)],
        out_specs=_hbm_specs(c_in + c_out),
        out_shape=[pltpu.HBM(a.shape, a.dtype) for a in srcs] + [pltpu.HBM(o.shape, o.dtype) for o in lands],
        input_output_aliases={i: i for i in range(c_in + c_out)},
        compiler_params=_params(has_side_effects=pltpu.SideEffectType.DATAFLOW_SIDE_EFFECTING),
    )(*srcs, *lands, *sems, after)
    return res[:c_in], res[c_in:]


def _share_comm(fs, by_cols=()):
    nw = len(fs)

    def build(in_refs, out_refs, sems):
        del in_refs
        send_sems, recv_sems = sems
        x, y, c, _ = _place()

        def copy(w, half):
            part = _part(out_refs[w], w in by_cols, half)
            return pltpu.make_async_remote_copy(
                src_ref=part, dst_ref=part, send_sem=send_sems.at[w], recv_sem=recv_sems.at[w],
                device_id=(x, y, 1 - c), device_id_type=MESH)

        sends = [copy(w, c) for w in range(nw)]

        def start():
            for cp in sends:
                cp.start()

        def finish():
            for w in range(nw):
                copy(w, 1 - c).wait_recv()
            for cp in sends:
                cp.wait_send()

        return start, finish

    return _Comm(fs, [jax.ShapeDtypeStruct(f.shape, f.dtype) for f in fs],
                 [(nw,), (nw,)], build,
                 aliases={w: w for w in range(nw)})


def _add_sibling(g, r1, place, name, by_cols=False):
    nch, h, cols = r1.shape
    tr = _div_tile(h, 1024 if by_cols else 512, 2 * SUBLANES)
    nb = h // tr
    mine = (lambda k, i, p: (k, i, p[0])) if by_cols else (lambda k, i, p: (k, p[0] * nb + i, 0))

    def body(place_ref, g_ref, r_ref, o_ref):
        del place_ref
        o_ref[...] = (g_ref[...].astype(F32) + r_ref[...].astype(F32)).astype(BF16)

    spec = pltpu.PrefetchScalarGridSpec(
        num_scalar_prefetch=1, grid=(nch, nb),
        in_specs=[pl.BlockSpec((None, tr, cols), mine), pl.BlockSpec((None, tr, cols), lambda k, i, p: (k, i, 0))],
        out_specs=pl.BlockSpec((None, tr, cols), lambda k, i, p: (k, i, 0)))
    return pl.pallas_call(body, name=name, grid_spec=spec, out_shape=jax.ShapeDtypeStruct((nch, h, cols), BF16),
                          compiler_params=_params())(place, g, r1)


def _add_chips(s1, r2, place, name, by_cols=False):
    _, h, cols = s1.shape
    tr = _div_tile(h, 1024 if by_cols else 512, 2 * SUBLANES)
    nb = h // tr
    mine = (lambda i, p: (i, p[0])) if by_cols else (lambda i, p: (p[0] * nb + i, 0))
    whole = (h, 2 * cols) if by_cols else (2 * h, cols)

    def body(place_ref, s_ref, r_ref, o_ref):
        del place_ref
        acc = s_ref[...].astype(F32)
        for j in range(N_CHIPS - 1):
            acc = acc + r_ref[j].astype(F32)
        o_ref[...] = acc

    spec = pltpu.PrefetchScalarGridSpec(
        num_scalar_prefetch=1, grid=(nb,),
        in_specs=[pl.BlockSpec((None, tr, cols), lambda i, p: (p[1], i, 0)),
                  pl.BlockSpec((N_CHIPS - 1, tr, cols), lambda i, p: (0, i, 0))],
        out_specs=pl.BlockSpec((tr, cols), mine))
    return pl.pallas_call(body, name=name, grid_spec=spec, out_shape=jax.ShapeDtypeStruct(whole, F32),
                          compiler_params=_params())(place, s1, r2)


def _quarter_turn(m):
    h = m.shape[-1] // 2
    return jnp.concatenate([-m[..., h:], m[..., :h]], axis=-1)


def _quarter_turn_back(m):
    h = m.shape[-1] // 2
    return jnp.concatenate([m[..., h:], -m[..., :h]], axis=-1)


def _stack_rows(parts):
    out = lax.empty((sum(p.shape[0] for p in parts),) + parts[0].shape[1:], parts[0].dtype)
    row = 0
    for p in parts:
        out = lax.dynamic_update_slice(out, p, (row, 0))
        row += p.shape[0]
    return out


def _join_cols(sh):
    return jnp.concatenate([sh[k] for k in range(N_CHIPS)], axis=1)


def _split_cols(full):
    c = full.shape[1] // N_CHIPS
    return jnp.stack([full[:, k * c:(k + 1) * c] for k in range(N_CHIPS)])


def kernel(x, c, positions, w_ada, b_ada, pre_norm1_g, w_in, gm_ln_g, gm_ln_b, gm_w_s, gm_b_s, w_branch_a, q_norm_g, w_uq, kv_norm_g, w_ukv, w_branch_b, w_out, post_norm1_g, pre_norm2_g, w_up, conv_w, conv_b, w_down, post_norm2_g, loss_target, m_w_ada, m_b_ada, m_pre_norm1_g, m_w_in, m_gm_ln_g, m_gm_ln_b, m_gm_w_s, m_gm_b_s, m_w_branch_a, m_q_norm_g, m_w_uq, m_kv_norm_g, m_w_ukv, m_w_branch_b, m_w_out, m_post_norm1_g, m_pre_norm2_g, m_w_up, m_conv_w, m_conv_b, m_w_down, m_post_norm2_g, v_w_ada, v_b_ada, v_pre_norm1_g, v_w_in, v_gm_ln_g, v_gm_ln_b, v_gm_w_s, v_gm_b_s, v_w_branch_a, v_q_norm_g, v_w_uq, v_kv_norm_g, v_w_ukv, v_w_branch_b, v_w_out, v_post_norm1_g, v_pre_norm2_g, v_w_up, v_conv_w, v_conv_b, v_w_down, v_post_norm2_g):
    given = dict(locals())
    s, d = x.shape[1], x.shape[2]
    gw = gm_ln_g.shape[0]
    ql, kvl = q_norm_g.shape[0], kv_norm_g.shape[0]
    heads = N_CHIPS * w_uq.shape[1] // (NOPE + ROPE)
    ff = N_CHIPS * w_down.shape[0]
    assert gw == d and N_CHIPS * w_ukv.shape[1] == heads * (NOPE + VHEAD)
    ix, iy, ic = lax.axis_index("x"), lax.axis_index("y"), lax.axis_index("c")
    chip = 2 * ix + iy
    dev = 2 * chip + ic
    row = lambda v: v.reshape(1, -1)

    first = _all_gather(jnp.concatenate([jnp.pad(c, ((0, SUBLANES - 1), (0, 0))),
                                         jnp.pad(conv_w, ((0, SUBLANES - CONV_TAPS), (0, 0)))], axis=1), "gather_c")
    first = first.reshape(N_DEV, SUBLANES, d + conv_w.shape[1])
    c_all = first[:, 0, :d]
    conv_wf = first[::N_CORES, :CONV_TAPS, d:].transpose(1, 0, 2).reshape(CONV_TAPS, N_CHIPS * conv_w.shape[1])
    na = w_ada.shape[1]
    b_ada_mine = lax.dynamic_slice(b_ada, (chip * na,), (na,))
    mod_cols = _ada_fwd(c_all, w_ada, row(b_ada_mine), "ada_fwd")
    mod_all = _all_gather(mod_cols, "gather_mod").reshape(N_CHIPS, N_CORES, N_DEV, na)[:, 0]
    mod = lax.dynamic_index_in_dim(mod_all, dev, axis=1, keepdims=False).reshape(N_MOD, d)
    shift1, scale1, gate1, shift2, scale2, gate2 = (mod[i:i + 1] for i in range(N_MOD))

    mine = {n: (given[n].T if n == "w_in" else given[n]).astype(BF16) for n in BIG}
    gather = lambda names: _gather_comm([mine[n] for n in names], [i for i, n in enumerate(names) if n == "w_in"])
    whole = lambda n, g: g
    rows4 = lambda sh4: sh4.reshape(-1, sh4.shape[2])
    wi_t = rows4(whole("w_in", _run_comm(gather(["w_in"]), "gather_w_in")[0]))
    o_q, o_kv, o_pe, o_ga = 2 * gw, 2 * gw + ql, 2 * gw + ql + kvl, 2 * gw + ql + kvl + ROPE
    w_in_big_t = _stack_rows([wi_t[:o_q], wi_t[o_ga:]])
    w_in_lat_t = _stack_rows([wi_t[o_q:o_ga], _quarter_turn(wi_t[o_pe:o_ga].T).T])

    inv = ROPE_THETA ** (-jnp.arange(0, ROPE, 2, dtype=F32) / ROPE)
    ang = positions[0].astype(F32)[:, None] * inv
    cos, sin = jnp.cos(ang), jnp.sin(ang)
    rope_k = jnp.concatenate([cos, cos, sin, sin], axis=1)
    softmax_scale = float(NOPE + ROPE) ** -0.5
    rope_q = jnp.concatenate([jnp.ones((s, NOPE), F32), rope_k], axis=1) * softmax_scale

    x2d, tgt = x[0], loss_target[0]
    g_pre1, g_post1, g_pre2, g_post2 = row(pre_norm1_g), row(post_norm1_g), row(pre_norm2_g), row(post_norm2_g)
    ln_g, ln_b, q_g, kv_g = row(gm_ln_g), row(gm_ln_b), row(q_norm_g), row(kv_norm_g)
    b_s_t = gm_b_s.T
    conv_bf = row(conv_b)

    h1 = _prenorm(x2d, g_pre1, scale1, shift1, "prenorm1")
    z_big, (g_uq, g_ukv, g_a) = _matmul(h1, w_in_big_t, mode="nt", out_dtype=BF16, name="mm_z_big", tm=s,
                                        comm=gather(["w_uq", "w_ukv", "w_branch_a"]))
    wq = _join_cols(whole("w_uq", g_uq)).reshape(ql, heads, NOPE + ROPE)
    w_q = jnp.concatenate([wq, _quarter_turn(wq[:, :, NOPE:])], axis=2).reshape(ql, heads * HEAD_W)
    w_kv = _join_cols(whole("w_ukv", g_ukv)).reshape(kvl, heads, 2, NOPE).transpose(0, 2, 1, 3)
    w_kv = w_kv.reshape(kvl, 2 * heads * NOPE)
    w_a = rows4(whole("w_branch_a", g_a))
    z_lat = _matmul(h1, w_in_lat_t, mode="nt", out_dtype=F32, name="mm_z_lat", tm=s, tn=1024)
    a_act = _gmlp_fwd(z_big, ln_g, ln_b, gm_w_s, b_s_t, "gmlp_fwd")
    qn, kvn, kr = _mla_prep(z_lat, q_g, kv_g, rope_k, "mla_prep")
    q_rot = _matmul(qn, w_q, mode="nn", out_dtype=BF16, name="mm_q", tm=s, tn=HEAD_W, mul=rope_q)
    kv_all = _matmul(kvn, w_kv, mode="nn", out_dtype=BF16, name="mm_kv", tm=s, tn=1024)
    (o_att, lse), (g_b, g_o, g_up) = _attn_fwd(q_rot, kv_all, kr, heads, "attn_fwd",
                                               comm=gather(["w_branch_b", "w_out", "w_up"]))
    w_b, w_o, w_upf = rows4(whole("w_branch_b", g_b)), rows4(whole("w_out", g_o)), whole("w_up", g_up)
    y_a = _matmul(a_act, w_a, mode="nn", out_dtype=BF16, name="mm_y_a", tm=s)
    y_b = _matmul(o_att, w_b, mode="nn", out_dtype=BF16, name="mm_y_b", tm=s)
    merged = _merge(z_big, y_a, y_b, "merge")
    y1 = _matmul(merged, w_o, mode="nn", out_dtype=F32, name="mm_y1", tm=s)
    x1, h2 = _post_pre(x2d, y1, gate1, g_post1, g_pre2, scale2, shift2, "post1_pre2")

    up_pre, (g_dn,) = _matmul(h2, w_upf, mode="nn", out_dtype=BF16, name="mm_up", tm=s, tn=1408,
                              comm=gather(["w_down"]))
    w_dn = rows4(whole("w_down", g_dn))
    act = _conv_fwd(up_pre, conv_wf, conv_bf, "conv_fwd")
    ffn = _matmul(act, w_dn, mode="nn", out_dtype=F32, name="mm_ffn", tm=s, tn=1024, tk=1408)

    dffn, dgate2, g_post2_grad, dx2, loss_part = _post_bwd(ffn, gate2, g_post2, "post2_bwd", xin=x1, target=tgt)
    loss = lax.psum(loss_part[0, 0], ("x", "y", "c"))
    place = jnp.stack([ic, chip]).astype(jnp.int32)
    rows_of = lambda g: g.reshape(N_CHIPS, g.shape[0] // N_CHIPS, g.shape[1])
    add_sibling = lambda names, gs, r1s: [_add_sibling(g, r1, place, "rs_add_sibling_" + n, by_cols=n == "w_in")
                                          for n, g, r1 in zip(names, gs, r1s)]
    add_chips = lambda names, s1s, r2s: [_add_chips(s1, r2, place, "rs_add_chips_" + n, by_cols=n == "w_in")
                                         for n, s1, r2 in zip(names, s1s, r2s)]
    gp_down = [rows_of(_matmul(act, dffn, mode="tn", out_dtype=BF16, name="mm_gw_down", tn=2048, tk=s))]
    dact, r1_down = _matmul(dffn, w_dn, mode="nt", out_dtype=BF16, name="mm_dact", tm=s, comm=_swap_comm(gp_down))
    s1_down = add_sibling(["w_down"], gp_down, r1_down)
    (dup, gcw_g, gcw_v, gcb_g, gcb_v), r2_down = _conv_bwd(up_pre, dact, conv_wf, conv_bf, "conv_bwd",
                                                            comm=_exchange_comm(s1_down))
    half_down = add_chips(["w_down"], s1_down, r2_down)
    dh2 = _matmul(dup, w_upf, mode="nt", out_dtype=F32, name="mm_dh2", tm=s, tn=1024, tk=1408)
    dx1, dshift2, dscale2, g_pre2_grad = _prenorm_bwd(x1, dh2, dx2, g_pre2, scale2, "prenorm2_bwd")

    dy1, dgate1, g_post1_grad = _post_bwd(y1, gate1, g_post1, "post1_bwd", dxo=dx1)
    dmerged = _matmul(dy1, w_o, mode="nt", out_dtype=BF16, name="mm_dmerged", tm=s)
    gw_out = _matmul(merged, dy1, mode="tn", out_dtype=BF16, name="mm_gw_out", tn=1024, tk=s)
    dy_a, dy_b, dz_big = _merge_bwd(dmerged, z_big, y_a, y_b, "merge_bwd")
    gw_a = _matmul(a_act, dy_a, mode="tn", out_dtype=BF16, name="mm_gw_a", tn=1024, tk=s)
    gw_b = _matmul(o_att, dy_b, mode="tn", out_dtype=BF16, name="mm_gw_b", tn=1024, tk=s)
    mid = ["w_up", "w_out", "w_branch_a", "w_branch_b"]
    gp_oab = [rows_of(gw_out), rows_of(gw_a), rows_of(gw_b)]
    da, r1_oab = _matmul(dy_a, w_a, mode="nt", out_dtype=BF16, name="mm_da", tm=s, comm=_swap_comm(gp_oab))
    s1_oab = add_sibling(mid[1:], gp_oab, r1_oab)
    gw_up, r2_oa = _matmul(h2, dup, mode="tn", out_dtype=BF16, name="mm_gw_up", tm=1024, tn=1408, tk=s,
                           out_groups=N_CHIPS, comm=_exchange_comm(s1_oab[:2]))
    do = _matmul(dy_b, w_b, mode="nt", out_dtype=BF16, name="mm_do", tm=s)
    (dz_big, g_ws, g_bs_t, g_ln_g, g_ln_b), r1_up = _gmlp_bwd(z_big, da, dz_big, ln_g, ln_b, gm_w_s, b_s_t,
                                                               "gmlp_bwd", comm=_swap_comm([gw_up]))
    s1_mid = add_sibling(mid[:1], [gw_up], r1_up) + s1_oab
    (dq, dk, dv), r2_up = _attn_bwd(q_rot, kv_all, kr, o_att, do, lse, heads, "attn_bwd",
                                    comm=_exchange_comm(s1_mid[:1]))
    dq_big, dkv, dkk = _mla_bwd_mid(dq, dk, dv, rope_q, rope_k, heads, "mla_bwd_mid")
    gw_q = _matmul(qn, dq_big, mode="tn", out_dtype=F32, name="mm_gw_q", tn=1024, tk=s)
    dqn = _matmul(dq_big, w_q, mode="nt", out_dtype=F32, name="mm_dqn", tm=s, tk=1024)
    gw_kv = _matmul(kvn, dkv, mode="tn", out_dtype=BF16, name="mm_gw_kv", tn=1024, tk=s)
    dkvn = _matmul(dkv, w_kv, mode="nt", out_dtype=F32, name="mm_dkvn", tm=s, tk=1024)
    dz_lat, g_q, g_kv = _mla_bwd_post(z_lat, dqn, dkvn, dkk, q_g, kv_g, "mla_bwd_post")

    partial = {
        "gm_ln_g": g_ln_g, "gm_ln_b": g_ln_b, "gm_w_s": g_ws, "gm_b_s": g_bs_t[:, :gm_b_s.shape[0]].T,
        "q_norm_g": g_q, "kv_norm_g": g_kv, "post_norm1_g": g_post1_grad, "pre_norm2_g": g_pre2_grad,
        "conv_w": jnp.concatenate([gcw_g, gcw_v], axis=1), "conv_b": jnp.concatenate([gcb_g, gcb_v], axis=1),
        "post_norm2_g": g_post2_grad,
    }
    flat = jnp.concatenate([partial[n].reshape(-1) for n in SMALL_PARTIAL])
    n_small = flat.shape[0]
    rows_small = -(-n_small // (LANES * SMALL_ROW_TILE)) * SMALL_ROW_TILE
    flat = jnp.pad(flat, (0, rows_small * LANES - n_small)).reshape(rows_small, LANES)

    def small_pack(prefix, source):
        v = jnp.concatenate([source[prefix + n].reshape(-1) for n in SMALL])
        rows = -(-v.shape[0] // (LANES * SUBLANES)) * SUBLANES
        return jnp.pad(v, (0, rows * LANES - v.shape[0])).reshape(rows, LANES)

    small_state = [small_pack(prefix, given) for prefix in ("", "m_", "v_")]

    dh1, r2_a_b = _matmul(dz_big, w_in_big_t, mode="nn", out_dtype=F32, name="mm_dh1_big", tm=s, tn=1024, tk=1024,
                          comm=_exchange_comm(s1_mid[3:]))
    half_mid = add_chips(mid, s1_mid, list(r2_up) + list(r2_oa) + list(r2_a_b))
    gw_big_t, hosted = _matmul(dz_big, h1, mode="tn", out_dtype=BF16, name="mm_gw_in_big", tn=2048, tk=s,
                               comm=_join_comms([_share_comm(half_down + half_mid), _gather8_comm(flat)]))
    shared, small_all = hosted[:-1], lax.dynamic_update_slice(hosted[-1], flat[None], (dev, 0, 0))
    small_sum = _sum_leading(small_all, "sum_small", after=small_state + [loss.reshape(1, 1)]).reshape(-1)
    small_grads, off = {}, 0
    for n in SMALL_PARTIAL:
        shape = (CONV_TAPS, 2 * ff) if n == "conv_w" else given[n].shape
        small_grads[n] = small_sum[off:off + partial[n].size].reshape(shape)
        off += partial[n].size
    small_grads["conv_w"] = lax.dynamic_slice(small_grads["conv_w"], (0, chip * conv_w.shape[1]), conv_w.shape)
    grads = dict(zip(["w_down"] + mid, shared), **small_grads)
    gw_lat_t = _matmul(dz_lat, h1, mode="tn", out_dtype=F32, name="mm_gw_in_lat", tm=1024, tn=1024, tk=s)

    gq = gw_q.reshape(ql, heads, HEAD_W)
    gq_pe = gq[:, :, NOPE:NOPE + ROPE] + _quarter_turn_back(gq[:, :, NOPE + ROPE:])
    g_pe_t = gw_lat_t[ql + kvl:ql + kvl + ROPE] + _quarter_turn_back(gw_lat_t[ql + kvl + ROPE:].T).T
    last = ["w_in", "w_uq", "w_ukv"]
    gw_in_t = _stack_rows([gw_big_t[:o_q], gw_lat_t[:ql + kvl].astype(BF16), g_pe_t.astype(BF16), gw_big_t[o_q:]])
    gp_last = [
        gw_in_t.reshape(N_CHIPS, gw_in_t.shape[0] // N_CHIPS, d),
        _split_cols(jnp.concatenate([gq[:, :, :NOPE], gq_pe], axis=2).reshape(ql, heads * (NOPE + ROPE)).astype(BF16)),
        _split_cols(gw_kv.reshape(kvl, 2, heads, NOPE).transpose(0, 2, 1, 3).reshape(kvl, heads * 2 * NOPE)),
    ]
    dh1, r1_last = _matmul(dz_lat, w_in_lat_t, mode="nn", out_dtype=F32, name="mm_dh1_lat", tm=s, tk=1024, add=dh1,
                           comm=_swap_comm(gp_last, by_cols=[0]))
    grad_x, dshift1, dscale1, g_pre1_grad = _prenorm_bwd(x2d, dh1, dx1, g_pre1, scale1, "prenorm1_bwd")
    s1_last = add_sibling(last, gp_last, r1_last)

    dmod = jnp.concatenate([dshift1, dscale1, dgate1, dshift2, dscale2, dgate2, g_pre1_grad], axis=1)
    dmod_all = _all_gather(jnp.pad(dmod, ((0, SUBLANES - 1), (0, 0))), "gather_dmod")
    dmod_all = dmod_all.reshape(N_DEV, SUBLANES, (N_MOD + 1) * d)[:, 0]
    dmod_sum = _sum_leading(dmod_all.reshape(N_DEV, 1, (N_MOD + 1) * d), "sum_dmod")[0]
    grads["b_ada"], grads["pre_norm1_g"] = dmod_sum[:N_MOD * d], dmod_sum[N_MOD * d:]
    dmod_mine = lax.dynamic_slice(dmod_all, (0, chip * na), (N_DEV, na))
    grads["w_ada"] = _ada_bwd(c_all.T, dmod_mine, "ada_bwd")

    delta, new_m, new_v = {}, {}, {}

    def adamw(n, after=None):
        turn = (lambda a: a.T) if n == "w_in" else (lambda a: a)
        outs = _adamw(turn(given[n]), grads[n], turn(given["m_" + n]), turn(given["v_" + n]), "adamw_" + n,
                      after=after)
        grads[n] = turn(grads[n])
        delta[n], new_m[n], new_v[n] = (turn(o) for o in outs)

    exchange_last = _exchange_comm(s1_last)
    in_flight, token = _comm_split_start(exchange_last, "rs_exchange_last_start", after=[dmod_sum, small_sum])
    for n in ["w_ada", "w_down"] + mid:
        adamw(n, after=token)
    s1_last, r2_last = _comm_split_wait(exchange_last, in_flight, delta[mid[-1]], "rs_exchange_last_wait")
    half_last = add_chips(last, s1_last, r2_last)
    grads.update(zip(last, _run_comm(_share_comm(half_last, by_cols=[0]), "rs_share_last")))
    for n in last:
        adamw(n)

    outs = _adamw(small_state[0], small_pack("", grads), small_state[1], small_state[2], "adamw_small")
    off = 0
    for n in SMALL:
        size = given[n].size
        for store, packed_out in zip((delta, new_m, new_v), outs):
            store[n] = packed_out.reshape(-1)[off:off + size].reshape(given[n].shape)
        off += size

    return (loss, grad_x[None], *[grads[n] for n in WEIGHTS], *[delta[n] for n in WEIGHTS],
            *[new_m[n] for n in WEIGHTS], *[new_v[n] for n in WEIGHTS])
```

```python
import functools

import jax
import jax.numpy as jnp
from jax import lax
from jax.experimental import pallas as pl
from jax.experimental.pallas import tpu as pltpu

F32 = jnp.float32
BF16 = jnp.bfloat16
MESH = pl.DeviceIdType.MESH
HBM = pltpu.HBM

EPS = 1e-6
NOPE, ROPE, VHEAD = 128, 64, 128
HEAD_W = NOPE + 2 * ROPE
ROPE_THETA = 10000.0
CONV_TAPS = 3
N_MOD = 6
N_CHIPS, N_CORES, N_DEV = 4, 2, 8
ADAM_LR, ADAM_B1, ADAM_B2, ADAM_EPS, ADAM_WD, ADAM_STEP = 0.001, 0.9, 0.999, 1e-08, 0.01, 10

LANES = 128
SUBLANES = 8
VMEM_LIMIT = 56 * 2**20
MIDDLE_STAGE_AT = 70
SMALL_ROW_TILE = 256
ADAMW_BLOCK_ELEMS = 768 * 1024

BIG = ("w_in", "w_branch_a", "w_uq", "w_ukv", "w_branch_b", "w_out", "w_up", "w_down")
WEIGHTS = ("w_ada", "b_ada", "pre_norm1_g", "w_in", "gm_ln_g", "gm_ln_b", "gm_w_s", "gm_b_s", "w_branch_a",
           "q_norm_g", "w_uq", "kv_norm_g", "w_ukv", "w_branch_b", "w_out", "post_norm1_g", "pre_norm2_g",
           "w_up", "conv_w", "conv_b", "w_down", "post_norm2_g")
SMALL_PARTIAL = ("gm_ln_g", "gm_ln_b", "gm_w_s", "gm_b_s", "q_norm_g", "kv_norm_g", "post_norm1_g",
                 "pre_norm2_g", "conv_w", "conv_b", "post_norm2_g")
SMALL = ("b_ada", "pre_norm1_g") + SMALL_PARTIAL


def _div_tile(n, cap, mult=LANES):
    t = (min(cap, n) // mult) * mult
    while t >= mult:
        if n % t == 0:
            return t
        t -= mult
    return n


def _params(**kw):
    return pltpu.CompilerParams(vmem_limit_bytes=VMEM_LIMIT, **kw)


def _row_spec(width):
    return pl.BlockSpec((1, width), lambda *_: (0, 0))


def _gelu(x):
    k = 0.7978845608028654
    return 0.5 * x * (1.0 + jnp.tanh(k * (x + 0.044715 * x * x * x)))


def _gelu_grad(x):
    k = 0.7978845608028654
    t = jnp.tanh(k * (x + 0.044715 * x * x * x))
    return 0.5 * (1.0 + t) + 0.5 * x * (1.0 - t * t) * k * (1.0 + 3.0 * 0.044715 * x * x)


def _sigmoid(x):
    return 0.5 * jnp.tanh(0.5 * x) + 0.5


def _dot(a, b, dims):
    return lax.dot_general(a, b, (dims, ((), ())), preferred_element_type=F32)


NN = ((1,), (0,))
NT = ((1,), (1,))
TN = ((0,), (0,))


def _logical(arr):
    if arr.ndim == 2:
        return arr.shape[0], arr.shape[1], arr.shape[1]
    return arr.shape[1], arr.shape[0] * arr.shape[2], arr.shape[2]


def _tile_spec(ndim, group_w, blk_rows, blk_cols, row_of, col_of):
    if ndim == 2:
        return pl.BlockSpec((blk_rows, blk_cols), lambda i, j, k: (row_of(i, j, k), col_of(i, j, k)))
    per = group_w // blk_cols
    return pl.BlockSpec((None, blk_rows, blk_cols),
                        lambda i, j, k: (col_of(i, j, k) // per, row_of(i, j, k), col_of(i, j, k) % per))


def _matmul(a, b, *, mode, out_dtype, name, tm=512, tn=512, tk=2048, mul=None, add=None, out_groups=None, comm=None):
    ar, ac, agw = _logical(a)
    br, bc, bgw = _logical(b)
    if mode == "nn":
        m, kd, n = ar, ac, bc
        m_w, k_w, n_w = (), (agw,), (bgw,)
    elif mode == "nt":
        m, kd, n = ar, ac, br
        m_w, k_w, n_w = (), (agw, bgw), ()
    else:
        m, kd, n = ac, ar, bc
        m_w, k_w, n_w = (agw,), (), (bgw,)
    if out_groups is not None:
        n_w = n_w + (n // out_groups,)
    tm = _div_tile(min((m,) + m_w), tm, LANES if mode == "tn" else SUBLANES)
    tn = _div_tile(min((n,) + n_w), tn)
    tk = _div_tile(min((kd,) + k_w), tk)
    assert all(w % tn == 0 for w in n_w) and all(w % tk == 0 for w in k_w) and all(w % tm == 0 for w in m_w)
    nk = kd // tk
    dims = {"nn": NN, "nt": NT, "tn": TN}[mode]
    gi, gj, gk = (lambda i, j, k: i), (lambda i, j, k: j), (lambda i, j, k: k)
    if mode == "nn":
        a_spec = _tile_spec(a.ndim, agw, tm, tk, gi, gk)
        b_spec = _tile_spec(b.ndim, bgw, tk, tn, gk, gj)
    elif mode == "nt":
        a_spec = _tile_spec(a.ndim, agw, tm, tk, gi, gk)
        b_spec = _tile_spec(b.ndim, bgw, tn, tk, gj, gk)
    else:
        a_spec = _tile_spec(a.ndim, agw, tk, tm, gk, gi)
        b_spec = _tile_spec(b.ndim, bgw, tk, tn, gk, gj)
    in_specs, operands = [a_spec, b_spec], [a, b]
    if mul is not None:
        assert mul.shape == (m, tn)
        in_specs.append(pl.BlockSpec((tm, tn), lambda i, j, k: (i, 0)))
        operands.append(mul)
    if add is not None:
        in_specs.append(pl.BlockSpec((tm, tn), lambda i, j, k: (i, j)))
        operands.append(add)

    def body(*refs):
        a_ref, b_ref = refs[0], refs[1]
        pos = 2
        mul_ref = add_ref = None
        if mul is not None:
            mul_ref, pos = refs[pos], pos + 1
        if add is not None:
            add_ref, pos = refs[pos], pos + 1
        o_ref = refs[pos]

        def finish(r):
            if mul_ref is not None:
                r = r * mul_ref[...]
            if add_ref is not None:
                r = r + add_ref[...]
            o_ref[...] = r.astype(out_dtype)

        part = _dot(a_ref[...], b_ref[...], dims)
        if nk == 1:
            finish(part)
        else:
            acc_ref = refs[pos + 1]
            k = pl.program_id(2)

            @pl.when(k == 0)
            def _():
                acc_ref[...] = part

            @pl.when(k > 0)
            def _():
                acc_ref[...] += part

            @pl.when(k == nk - 1)
            def _():
                finish(acc_ref[...])

    if out_groups is None:
        out_spec, out_dims = _tile_spec(2, n, tm, tn, gi, gj), (m, n)
    else:
        out_spec, out_dims = _tile_spec(3, n // out_groups, tm, tn, gi, gj), (out_groups, m, n // out_groups)
    return _call(body, operands, comm, name=name, grid=(m // tm, n // tn, nk), in_specs=in_specs, out_specs=out_spec,
                 out_shape=jax.ShapeDtypeStruct(out_dims, out_dtype),
                 scratch_shapes=[] if nk == 1 else [pltpu.VMEM((tm, tn), F32)])


def _accumulate(ref, value):
    @pl.when(pl.program_id(0) == 0)
    def _():
        ref[...] = value

    @pl.when(pl.program_id(0) > 0)
    def _():
        ref[...] += value


def _colsum(v):
    return jnp.sum(v, axis=0, keepdims=True)


def _rowmean(v):
    return jnp.mean(v, axis=-1, keepdims=True)


def _prenorm(x, g, scale, shift, name):
    s, d = x.shape
    tb = _div_tile(s, 256, SUBLANES)

    def body(x_ref, g_ref, sc_ref, sh_ref, h_ref):
        xv = x_ref[...]
        r = lax.rsqrt(_rowmean(xv * xv) + EPS)
        h_ref[...] = ((xv * r) * g_ref[...] * (1.0 + sc_ref[...]) + sh_ref[...]).astype(BF16)

    blk = pl.BlockSpec((tb, d), lambda i: (i, 0))
    return pl.pallas_call(
        body, name=name, grid=(s // tb,), in_specs=[blk, _row_spec(d), _row_spec(d), _row_spec(d)],
        out_specs=blk, out_shape=jax.ShapeDtypeStruct((s, d), BF16), compiler_params=_params(),
    )(x, g, scale, shift)


def _post_pre(x, y, gate, pg, g2, scale2, shift2, name):
    s, d = x.shape
    tb = _div_tile(s, 256, SUBLANES)

    def body(x_ref, y_ref, gate_ref, pg_ref, g2_ref, sc_ref, sh_ref, x1_ref, h2_ref):
        yv = y_ref[...]
        rp = lax.rsqrt(_rowmean(yv * yv) + EPS)
        x1 = x_ref[...] + gate_ref[...] * ((yv * rp) * pg_ref[...])
        x1_ref[...] = x1
        r2 = lax.rsqrt(_rowmean(x1 * x1) + EPS)
        h2_ref[...] = ((x1 * r2) * g2_ref[...] * (1.0 + sc_ref[...]) + sh_ref[...]).astype(BF16)

    blk = pl.BlockSpec((tb, d), lambda i: (i, 0))
    return pl.pallas_call(
        body, name=name, grid=(s // tb,), in_specs=[blk, blk] + [_row_spec(d)] * 5,
        out_specs=[blk, blk],
        out_shape=[jax.ShapeDtypeStruct((s, d), F32), jax.ShapeDtypeStruct((s, d), BF16)],
        compiler_params=_params(),
    )(x, y, gate, pg, g2, scale2, shift2)


def _post_bwd(y, gate, pg, name, *, dxo=None, xin=None, target=None):
    s, d = y.shape
    tb = _div_tile(s, 256, SUBLANES)
    from_loss = target is not None

    def body(*refs):
        if from_loss:
            y_ref, gate_ref, pg_ref, xin_ref, t_ref, dy_ref, dgate_ref, dpg_ref, dxo_ref, loss_ref = refs
        else:
            y_ref, gate_ref, pg_ref, dxo_in_ref, dy_ref, dgate_ref, dpg_ref = refs
        yv = y_ref[...]
        rp = lax.rsqrt(_rowmean(yv * yv) + EPS)
        yh = yv * rp
        fn = yh * pg_ref[...]
        gate = gate_ref[...]
        if from_loss:
            err = xin_ref[...] + gate * fn - t_ref[...]
            dxo = err * (1.0 / d)
            dxo_ref[...] = dxo
            part = 0.5 * jnp.sum(_rowmean(err * err), axis=0, keepdims=True)
            _accumulate(loss_ref, jnp.broadcast_to(part, loss_ref.shape))
        else:
            dxo = dxo_in_ref[...]
        _accumulate(dgate_ref, _colsum(dxo * fn))
        dfn = dxo * gate
        _accumulate(dpg_ref, _colsum(dfn * yh))
        dyh = dfn * pg_ref[...]
        dy_ref[...] = (rp * (dyh - yh * _rowmean(dyh * yh))).astype(BF16)

    blk = pl.BlockSpec((tb, d), lambda i: (i, 0))
    in_specs = [blk, _row_spec(d), _row_spec(d)]
    out_specs = [blk, _row_spec(d), _row_spec(d)]
    out_shape = [jax.ShapeDtypeStruct((s, d), BF16), jax.ShapeDtypeStruct((1, d), F32),
                 jax.ShapeDtypeStruct((1, d), F32)]
    if from_loss:
        operands = (y, gate, pg, xin, target)
        in_specs += [blk, blk]
        out_specs += [blk, _row_spec(LANES)]
        out_shape += [jax.ShapeDtypeStruct((s, d), F32), jax.ShapeDtypeStruct((1, LANES), F32)]
    else:
        operands = (y, gate, pg, dxo)
        in_specs += [blk]
    return pl.pallas_call(
        body, name=name, grid=(s // tb,), in_specs=in_specs, out_specs=out_specs, out_shape=out_shape,
        compiler_params=_params(),
    )(*operands)


def _prenorm_bwd(xin, dh, dres, g, scale, name, comm=None):
    s, d = xin.shape
    tb = _div_tile(s, 256, SUBLANES)

    def body(x_ref, dh_ref, dres_ref, g_ref, sc_ref, dx_ref, dshift_ref, dscale_ref, dg_ref):
        xv = x_ref[...]
        r = lax.rsqrt(_rowmean(xv * xv) + EPS)
        xn = xv * r
        dh = dh_ref[...]
        g1 = g_ref[...]
        s1 = 1.0 + sc_ref[...]
        _accumulate(dshift_ref, _colsum(dh))
        _accumulate(dscale_ref, _colsum(dh * xn * g1))
        _accumulate(dg_ref, _colsum(dh * xn * s1))
        dxn = dh * g1 * s1
        dx_ref[...] = dres_ref[...] + r * (dxn - xn * _rowmean(dxn * xn))

    blk = pl.BlockSpec((tb, d), lambda i: (i, 0))
    return _call(
        body, (xin, dh, dres, g, scale), comm, name=name, grid=(s // tb,),
        in_specs=[blk, blk, blk, _row_spec(d), _row_spec(d)],
        out_specs=[blk, _row_spec(d), _row_spec(d), _row_spec(d)],
        out_shape=[jax.ShapeDtypeStruct((s, d), F32)] + [jax.ShapeDtypeStruct((1, d), F32)] * 3)


def _merge(z_big, y_a, y_b, name):
    s, d = y_a.shape
    tb = _div_tile(s, 256, SUBLANES)

    def body(zg_ref, ya_ref, yb_ref, o_ref):
        ga, gb = zg_ref[:, :d].astype(F32), zg_ref[:, d:].astype(F32)
        o_ref[...] = (_sigmoid(ga) * ya_ref[...].astype(F32) + _sigmoid(gb) * yb_ref[...].astype(F32)).astype(BF16)

    blk = pl.BlockSpec((tb, d), lambda i: (i, 0))
    return pl.pallas_call(
        body, name=name, grid=(s // tb,), in_specs=[pl.BlockSpec((tb, 2 * d), lambda i: (i, 1)), blk, blk],
        out_specs=blk, out_shape=jax.ShapeDtypeStruct((s, d), BF16), compiler_params=_params(),
    )(z_big, y_a, y_b)


def _merge_bwd(dmerged, z_big, y_a, y_b, name):
    s, d = y_a.shape
    tb = _div_tile(s, 256, SUBLANES)

    def body(dm_ref, zg_ref, ya_ref, yb_ref, dya_ref, dyb_ref, dz_ref):
        dm = dm_ref[...].astype(F32)
        sa, sb = _sigmoid(zg_ref[:, :d].astype(F32)), _sigmoid(zg_ref[:, d:].astype(F32))
        dya_ref[...] = (dm * sa).astype(BF16)
        dyb_ref[...] = (dm * sb).astype(BF16)
        dz_ref[:, :d] = (dm * ya_ref[...].astype(F32) * sa * (1.0 - sa)).astype(BF16)
        dz_ref[:, d:] = (dm * yb_ref[...].astype(F32) * sb * (1.0 - sb)).astype(BF16)

    blk = pl.BlockSpec((tb, d), lambda i: (i, 0))
    wide = pl.BlockSpec((tb, 2 * d), lambda i: (i, 1))
    return pl.pallas_call(
        body, name=name, grid=(s // tb,), in_specs=[blk, wide, blk, blk], out_specs=[blk, blk, wide],
        out_shape=[jax.ShapeDtypeStruct((s, d), BF16), jax.ShapeDtypeStruct((s, d), BF16),
                   jax.ShapeDtypeStruct((s, 4 * d), BF16)],
        compiler_params=_params(),
    )(dmerged, z_big, y_a, y_b)


def _causal_mask(ch):
    q = lax.broadcasted_iota(jnp.int32, (ch, ch), 0)
    p = lax.broadcasted_iota(jnp.int32, (ch, ch), 1)
    return (p <= q).astype(F32)


def _gmlp_norm(zc, lng, lnb, gw):
    u_pre, v_pre = zc[:, :gw], zc[:, gw:]
    vg = _gelu(v_pre)
    mu = _rowmean(vg)
    cen = vg - mu
    rstd = lax.rsqrt(_rowmean(cen * cen) + EPS)
    vhat = cen * rstd
    return u_pre, v_pre, _gelu(u_pre), vhat, rstd, vhat * lng + lnb


def _gmlp_fwd(z_big, ln_g, ln_b, w_s, b_s_t, name):
    s = z_big.shape[0]
    groups, ch, _ = w_s.shape
    gw = ln_g.shape[1]
    gd = gw // groups

    def body(z_ref, lng_ref, lnb_ref, ws_ref, bt_ref, a_ref):
        _, _, u, _, _, vn = _gmlp_norm(z_ref[...].astype(F32), lng_ref[...], lnb_ref[...], gw)
        mask = _causal_mask(ch)
        for g in range(groups):
            cols = slice(g * gd, (g + 1) * gd)
            wm = (ws_ref[g] * mask).astype(BF16)
            mixed = _dot(wm, vn[:, cols].astype(BF16), NN) + bt_ref[:, g:g + 1]
            a_ref[:, cols] = (u[:, cols] * mixed).astype(BF16)

    return pl.pallas_call(
        body, name=name, grid=(s // ch,),
        in_specs=[pl.BlockSpec((ch, 2 * gw), lambda n: (n, 0)), _row_spec(gw), _row_spec(gw),
                  pl.BlockSpec((groups, ch, ch), lambda n: (0, 0, 0)), pl.BlockSpec((ch, groups), lambda n: (0, 0))],
        out_specs=pl.BlockSpec((ch, gw), lambda n: (n, 0)),
        out_shape=jax.ShapeDtypeStruct((s, gw), BF16), compiler_params=_params(),
    )(z_big, ln_g, ln_b, w_s, b_s_t)


def _gmlp_bwd(z_big, da, dz_big, ln_g, ln_b, w_s, b_s_t, name, comm=None):
    s = z_big.shape[0]
    groups, ch, _ = w_s.shape
    gw = ln_g.shape[1]
    gd = gw // groups

    def body(z_ref, da_ref, dzin_ref, lng_ref, lnb_ref, ws_ref, bt_ref, dz_ref, gws_ref, gbt_ref, glng_ref, glnb_ref,
             vg_ref, dvh_ref):
        del dzin_ref
        mask = _causal_mask(ch)
        lane = lax.broadcasted_iota(jnp.int32, (ch, LANES), 1)
        group_cols = [slice(g * gd, (g + 1) * gd) for g in range(groups)]
        rowsum = lambda v: jnp.sum(v, axis=1, keepdims=True)

        @pl.when(pl.program_id(0) == 0)
        def _():
            for ref in (gws_ref, gbt_ref, glng_ref, glnb_ref):
                ref[...] = jnp.zeros(ref.shape, F32)

        total = jnp.zeros((ch, 1), F32)
        for cols in group_cols:
            vg = _gelu(z_ref[:, gw + cols.start:gw + cols.stop].astype(F32))
            vg_ref[:, cols] = vg
            total = total + rowsum(vg)
        mu = total * (1.0 / gw)
        total = jnp.zeros((ch, 1), F32)
        for cols in group_cols:
            cen = vg_ref[:, cols] - mu
            total = total + rowsum(cen * cen)
        rstd = lax.rsqrt(total * (1.0 / gw) + EPS)
        m1, m2, gb = jnp.zeros((ch, 1), F32), jnp.zeros((ch, 1), F32), jnp.zeros((ch, LANES), F32)
        for g, cols in enumerate(group_cols):
            vhat = (vg_ref[:, cols] - mu) * rstd
            vn_g = (vhat * lng_ref[:, cols] + lnb_ref[:, cols]).astype(BF16)
            wm = (ws_ref[g] * mask).astype(BF16)
            mixed = _dot(wm, vn_g, NN) + bt_ref[:, g:g + 1]
            u_pre, da_g = z_ref[:, cols].astype(F32), da_ref[:, cols].astype(F32)
            dz_ref[:, cols] = (da_g * mixed * _gelu_grad(u_pre)).astype(BF16)
            dmixed = da_g * _gelu(u_pre)
            dm16 = dmixed.astype(BF16)
            dvn = _dot(wm, dm16, TN)
            gws_ref[g] += _dot(dm16, vn_g, NT) * mask
            gb = gb + jnp.where(lane == g, rowsum(dmixed), 0.0)
            glnb_ref[:, cols] += _colsum(dvn)
            glng_ref[:, cols] += _colsum(dvn * vhat)
            dvh = dvn * lng_ref[:, cols]
            dvh_ref[:, cols] = dvh
            m1, m2 = m1 + rowsum(dvh), m2 + rowsum(dvh * vhat)
        gbt_ref[...] += gb
        m1, m2 = m1 * (1.0 / gw), m2 * (1.0 / gw)
        for cols in group_cols:
            vhat = (vg_ref[:, cols] - mu) * rstd
            dvg = rstd * (dvh_ref[:, cols] - m1 - vhat * m2)
            v_pre = z_ref[:, gw + cols.start:gw + cols.stop].astype(F32)
            dz_ref[:, gw + cols.start:gw + cols.stop] = (dvg * _gelu_grad(v_pre)).astype(BF16)

    zspec = pl.BlockSpec((ch, 2 * gw), lambda n: (n, 0))
    return _call(
        body, (z_big, da, dz_big, ln_g, ln_b, w_s, b_s_t), comm, name=name, grid=(s // ch,),
        in_specs=[zspec, pl.BlockSpec((ch, gw), lambda n: (n, 0)), pl.BlockSpec(memory_space=HBM),
                  _row_spec(gw), _row_spec(gw), pl.BlockSpec((groups, ch, ch), lambda n: (0, 0, 0)),
                  pl.BlockSpec((ch, groups), lambda n: (0, 0))],
        out_specs=[zspec, pl.BlockSpec((groups, ch, ch), lambda n: (0, 0, 0)),
                   pl.BlockSpec((ch, LANES), lambda n: (0, 0)), _row_spec(gw), _row_spec(gw)],
        out_shape=[jax.ShapeDtypeStruct(dz_big.shape, BF16), jax.ShapeDtypeStruct((groups, ch, ch), F32),
                   jax.ShapeDtypeStruct((ch, LANES), F32), jax.ShapeDtypeStruct((1, gw), F32),
                   jax.ShapeDtypeStruct((1, gw), F32)],
        scratch_shapes=[pltpu.VMEM((ch, gw), F32)] * 2, input_output_aliases={2: 0})


def _mla_prep(z_lat, q_g, kv_g, rope_k, name):
    s, latw = z_lat.shape
    ql, kvl = q_g.shape[1], kv_g.shape[1]
    tb = _div_tile(s, 256, SUBLANES)

    def body(z_ref, qg_ref, kvg_ref, t_ref, qn_ref, kvn_ref, kr_ref):
        q = z_ref[:, :ql]
        qn_ref[...] = ((q * lax.rsqrt(_rowmean(q * q) + EPS)) * qg_ref[...]).astype(BF16)
        kv = z_ref[:, ql:ql + kvl]
        kvn_ref[...] = ((kv * lax.rsqrt(_rowmean(kv * kv) + EPS)) * kvg_ref[...]).astype(BF16)
        kk = z_ref[:, ql + kvl:] * t_ref[...]
        kr_ref[...] = (kk + pltpu.roll(kk, ROPE, axis=1)).astype(BF16)

    return pl.pallas_call(
        body, name=name, grid=(s // tb,),
        in_specs=[pl.BlockSpec((tb, latw), lambda i: (i, 0)), _row_spec(ql), _row_spec(kvl),
                  pl.BlockSpec((tb, 2 * ROPE), lambda i: (i, 0))],
        out_specs=[pl.BlockSpec((tb, ql), lambda i: (i, 0)), pl.BlockSpec((tb, kvl), lambda i: (i, 0)),
                   pl.BlockSpec((tb, 2 * ROPE), lambda i: (i, 0))],
        out_shape=[jax.ShapeDtypeStruct((s, ql), BF16), jax.ShapeDtypeStruct((s, kvl), BF16),
                   jax.ShapeDtypeStruct((s, 2 * ROPE), BF16)],
        compiler_params=_params(),
    )(z_lat, q_g, kv_g, rope_k)


def _attn_fwd(q, kv, kr, heads, name, comm=None):
    s = q.shape[0]
    t = _div_tile(s, 512)
    nb = s // t
    hp = 2 if heads % 2 == 0 else 1

    def body(q_ref, k_ref, kr_ref, v_ref, o_ref, lse_ref, m_ref, l_ref, acc_ref):
        i, j = pl.program_id(1), pl.program_id(2)

        @pl.when(j == 0)
        def _():
            m_ref[...] = jnp.full(m_ref.shape, -1e30, F32)
            l_ref[...] = jnp.zeros(l_ref.shape, F32)
            acc_ref[...] = jnp.zeros(acc_ref.shape, F32)

        def update(h, rows, n_keys, on_diagonal):
            vc = slice(h * VHEAD, (h + 1) * VHEAD)
            k_full = jnp.concatenate([k_ref[:n_keys, h * NOPE:(h + 1) * NOPE], kr_ref[:n_keys, :]], axis=1)
            sc = _dot(q_ref[rows, h * HEAD_W:(h + 1) * HEAD_W], k_full, NT)
            if on_diagonal:
                row_pos = rows.start + lax.broadcasted_iota(jnp.int32, sc.shape, 0)
                sc = jnp.where(lax.broadcasted_iota(jnp.int32, sc.shape, 1) <= row_pos, sc, -1e30)
            m_old = m_ref[h, rows, :]
            m_new = jnp.maximum(m_old, jnp.max(sc, axis=-1, keepdims=True))
            p = jnp.exp(sc - m_new)
            alpha = jnp.exp(m_old - m_new)
            l_new = alpha * l_ref[h, rows, :] + jnp.sum(p, axis=-1, keepdims=True)
            acc = alpha * acc_ref[rows, vc] + _dot(p.astype(BF16), v_ref[:n_keys, vc], NN)
            if on_diagonal:
                o_ref[rows, vc] = (acc / l_new).astype(BF16)
                lse_ref[h, rows, :] = jnp.broadcast_to(m_new + jnp.log(l_new), (rows.stop - rows.start, LANES))
            else:
                m_ref[h, rows, :], l_ref[h, rows, :], acc_ref[rows, vc] = m_new, l_new, acc

        def below_diagonal():
            for h in range(hp):
                update(h, slice(0, t), t, False)

        def on_diagonal():
            for h in range(hp):
                update(h, slice(0, t // 2), t // 2, True)
                update(h, slice(t // 2, t), t, True)

        pl.when(j < i)(below_diagonal)
        pl.when(j == i)(on_diagonal)

    kidx = lambda off: (lambda h, i, j: (jnp.minimum(i, j), off(h)))
    return _call(
        body, (q, kv, kr, kv), comm, name=name, grid=(heads // hp, nb, nb),
        in_specs=[pl.BlockSpec((t, hp * HEAD_W), lambda h, i, j: (i, h)),
                  pl.BlockSpec((t, hp * NOPE), kidx(lambda h: h)),
                  pl.BlockSpec((t, 2 * ROPE), kidx(lambda h: 0)),
                  pl.BlockSpec((t, hp * VHEAD), kidx(lambda h: heads // hp + h))],
        out_specs=[pl.BlockSpec((t, hp * VHEAD), lambda h, i, j: (i, h)),
                   pl.BlockSpec((hp, t, LANES), lambda h, i, j: (h, i, 0))],
        out_shape=[jax.ShapeDtypeStruct((s, heads * VHEAD), BF16), jax.ShapeDtypeStruct((heads, s, LANES), F32)],
        scratch_shapes=[pltpu.VMEM((hp, t, 1), F32), pltpu.VMEM((hp, t, 1), F32), pltpu.VMEM((t, hp * VHEAD), F32)])


def _attn_bwd(q, kv, kr, o, do, lse, heads, name, comm=None):
    s = q.shape[0]
    t = _div_tile(s, 512)
    nb = s // t
    hp = 2 if heads % 2 == 0 else 1

    def body(q_ref, k_ref, kr_ref, v_ref, o_ref, do_ref, lse_ref, dq_ref, dk_ref, dv_ref, dk_acc, dv_acc):
        j, i = pl.program_id(1), pl.program_id(2)

        @pl.when(jnp.logical_and(j == 0, i == 0))
        def _():
            dq_ref[...] = jnp.zeros(dq_ref.shape, F32)

        def update(h, rows, n_keys, on_diagonal, assign):
            qc, kc, vc = (slice(h * w, (h + 1) * w) for w in (HEAD_W, NOPE, VHEAD))
            n_rows = rows.stop - rows.start
            qv, do_v = q_ref[rows, qc], do_ref[rows, vc]
            k_full = jnp.concatenate([k_ref[:n_keys, kc], kr_ref[:n_keys, :]], axis=1)
            sc = _dot(qv, k_full, NT)
            if on_diagonal:
                row_pos = rows.start + lax.broadcasted_iota(jnp.int32, sc.shape, 0)
                sc = jnp.where(lax.broadcasted_iota(jnp.int32, sc.shape, 1) <= row_pos, sc, -1e30)
            p = jnp.exp(sc - lse_ref[h, rows, :1])
            dp = _dot(do_v, v_ref[:n_keys, vc], NT)
            delta = jnp.sum(do_v.astype(F32) * o_ref[rows, vc].astype(F32), axis=-1, keepdims=True)
            ds = (p * (dp - delta)).astype(BF16)
            dq_ref[pl.ds(pl.multiple_of(i * t + rows.start, n_rows), n_rows), qc] += _dot(ds, k_full, NN)
            dv_part, dk_part = _dot(p.astype(BF16), do_v, TN), _dot(ds, qv, TN)
            if assign:
                dv_acc[:n_keys, vc], dk_acc[:n_keys, qc] = dv_part, dk_part
            else:
                dv_acc[:n_keys, vc] += dv_part
                dk_acc[:n_keys, qc] += dk_part

        def on_diagonal():
            for h in range(hp):
                update(h, slice(t // 2, t), t, True, True)
                update(h, slice(0, t // 2), t // 2, True, False)

        def below_diagonal():
            for h in range(hp):
                update(h, slice(0, t), t, False, False)

        pl.when(i == j)(on_diagonal)
        pl.when(i > j)(below_diagonal)

        @pl.when(i == nb - 1)
        def _():
            dk_ref[...] = dk_acc[...].astype(BF16)
            dv_ref[...] = dv_acc[...].astype(BF16)

    qidx = lambda h, j, i: (jnp.maximum(i, j), h)
    return _call(
        body, (q, kv, kr, kv, o, do, lse), comm, name=name, grid=(heads // hp, nb, nb),
        in_specs=[pl.BlockSpec((t, hp * HEAD_W), qidx),
                  pl.BlockSpec((t, hp * NOPE), lambda h, j, i: (j, h)),
                  pl.BlockSpec((t, 2 * ROPE), lambda h, j, i: (j, 0)),
                  pl.BlockSpec((t, hp * VHEAD), lambda h, j, i: (j, heads // hp + h)),
                  pl.BlockSpec((t, hp * VHEAD), qidx), pl.BlockSpec((t, hp * VHEAD), qidx),
                  pl.BlockSpec((hp, t, LANES), lambda h, j, i: (h, jnp.maximum(i, j), 0))],
        out_specs=[pl.BlockSpec((s, hp * HEAD_W), lambda h, j, i: (0, h)),
                   pl.BlockSpec((t, hp * HEAD_W), lambda h, j, i: (j, h)),
                   pl.BlockSpec((t, hp * VHEAD), lambda h, j, i: (j, h))],
        out_shape=[jax.ShapeDtypeStruct((s, heads * HEAD_W), F32), jax.ShapeDtypeStruct((s, heads * HEAD_W), BF16),
                   jax.ShapeDtypeStruct((s, heads * VHEAD), BF16)],
        scratch_shapes=[pltpu.VMEM((t, hp * HEAD_W), F32), pltpu.VMEM((t, hp * VHEAD), F32)])


def _mla_bwd_mid(dq, dk, dv, rope_q, rope_k, heads, name):
    s = dq.shape[0]
    tb = _div_tile(s, 256, SUBLANES)

    def body(dq_ref, dk_ref, dv_ref, tq_ref, tk_ref, dqb_ref, dkv_ref, dkk_ref):
        tq = tq_ref[...]
        dkr = jnp.zeros((tb, 2 * ROPE), F32)
        for h in range(heads):
            cols = slice(h * HEAD_W, (h + 1) * HEAD_W)
            dqb_ref[:, cols] = (dq_ref[:, cols] * tq).astype(BF16)
            dkv_ref[:, h * NOPE:(h + 1) * NOPE] = dk_ref[:, h * HEAD_W:h * HEAD_W + NOPE]
            dkr = dkr + dk_ref[:, h * HEAD_W + NOPE:(h + 1) * HEAD_W].astype(F32)
        dkv_ref[:, heads * NOPE:] = dv_ref[...]
        dkk_ref[...] = (dkr + pltpu.roll(dkr, ROPE, axis=1)) * tk_ref[...]

    wq, wv = heads * HEAD_W, heads * VHEAD
    return pl.pallas_call(
        body, name=name, grid=(s // tb,),
        in_specs=[pl.BlockSpec((tb, wq), lambda i: (i, 0)), pl.BlockSpec((tb, wq), lambda i: (i, 0)),
                  pl.BlockSpec((tb, wv), lambda i: (i, 0)), pl.BlockSpec((tb, HEAD_W), lambda i: (i, 0)),
                  pl.BlockSpec((tb, 2 * ROPE), lambda i: (i, 0))],
        out_specs=[pl.BlockSpec((tb, wq), lambda i: (i, 0)), pl.BlockSpec((tb, heads * NOPE + wv), lambda i: (i, 0)),
                   pl.BlockSpec((tb, 2 * ROPE), lambda i: (i, 0))],
        out_shape=[jax.ShapeDtypeStruct((s, wq), BF16), jax.ShapeDtypeStruct((s, heads * NOPE + wv), BF16),
                   jax.ShapeDtypeStruct((s, 2 * ROPE), F32)],
        compiler_params=_params(),
    )(dq, dk, dv, rope_q, rope_k)


def _mla_bwd_post(z_lat, dqn, dkvn, dkk, q_g, kv_g, name):
    s, latw = z_lat.shape
    ql, kvl = q_g.shape[1], kv_g.shape[1]
    tb = _div_tile(s, 256, SUBLANES)

    def norm_bwd(xv, dn, g, dg_ref):
        r = lax.rsqrt(_rowmean(xv * xv) + EPS)
        xh = xv * r
        _accumulate(dg_ref, _colsum(dn * xh))
        dxh = dn * g
        return r * (dxh - xh * _rowmean(dxh * xh))

    def body(z_ref, dqn_ref, dkvn_ref, dkk_ref, qg_ref, kvg_ref, dz_ref, gq_ref, gkv_ref):
        dz_ref[:, :ql] = norm_bwd(z_ref[:, :ql], dqn_ref[...], qg_ref[...], gq_ref).astype(BF16)
        dz_ref[:, ql:ql + kvl] = norm_bwd(z_ref[:, ql:ql + kvl], dkvn_ref[...], kvg_ref[...], gkv_ref).astype(BF16)
        dz_ref[:, ql + kvl:] = dkk_ref[...].astype(BF16)

    return pl.pallas_call(
        body, name=name, grid=(s // tb,),
        in_specs=[pl.BlockSpec((tb, latw), lambda i: (i, 0)), pl.BlockSpec((tb, ql), lambda i: (i, 0)),
                  pl.BlockSpec((tb, kvl), lambda i: (i, 0)), pl.BlockSpec((tb, 2 * ROPE), lambda i: (i, 0)),
                  _row_spec(ql), _row_spec(kvl)],
        out_specs=[pl.BlockSpec((tb, latw), lambda i: (i, 0)), _row_spec(ql), _row_spec(kvl)],
        out_shape=[jax.ShapeDtypeStruct((s, latw), BF16), jax.ShapeDtypeStruct((1, ql), F32),
                   jax.ShapeDtypeStruct((1, kvl), F32)],
        compiler_params=_params(),
    )(z_lat, dqn, dkvn, dkk, q_g, kv_g)


CONV_ROWS = 128
CONV_HALO = 16


def _row_steps(n_rows, step):
    step(0, True)
    if n_rows > CONV_ROWS:
        def later(i, carry):
            step(pl.multiple_of(i * CONV_ROWS, CONV_ROWS), False)
            return carry
        lax.fori_loop(1, n_rows // CONV_ROWS, later, 0)


def _conv_taps(pre_ref, r0, first):
    if first:
        win = jnp.concatenate([jnp.zeros((CONV_HALO, pre_ref.shape[1]), F32), pre_ref[0:CONV_ROWS, :].astype(F32)])
    else:
        win = pre_ref[pl.ds(pl.multiple_of(r0 - CONV_HALO, CONV_HALO), CONV_ROWS + CONV_HALO), :].astype(F32)
    return win[CONV_HALO:], pltpu.roll(win, 1, axis=0)[CONV_HALO:], pltpu.roll(win, 2, axis=0)[CONV_HALO:]


def _conv(taps, w_ref, b_ref):
    return w_ref[2:3, :] * taps[0] + w_ref[1:2, :] * taps[1] + w_ref[0:1, :] * taps[2] + b_ref[...]


def _conv_fwd(up_pre, conv_w, conv_b, name):
    s, ff2 = up_pre.shape
    ff = ff2 // 2
    tc = _div_tile(ff, 256)
    nb = ff // tc
    assert s % CONV_ROWS == 0

    def body(pg_ref, pv_ref, wg_ref, wv_ref, bg_ref, bv_ref, act_ref):
        def step(r0, first):
            gate = _conv(_conv_taps(pg_ref, r0, first), wg_ref, bg_ref)
            val = _conv(_conv_taps(pv_ref, r0, first), wv_ref, bv_ref)
            act_ref[pl.ds(r0, CONV_ROWS), :] = (gate * _sigmoid(gate) * val).astype(BF16)

        _row_steps(s, step)

    def col(rows, off):
        return pl.BlockSpec((rows, tc), lambda j: (0, j + off))

    return pl.pallas_call(
        body, name=name, grid=(nb,),
        in_specs=[col(s, 0), col(s, nb), col(CONV_TAPS, 0), col(CONV_TAPS, nb), col(1, 0), col(1, nb)],
        out_specs=col(s, 0), out_shape=jax.ShapeDtypeStruct((s, ff), BF16), compiler_params=_params(),
    )(up_pre, up_pre, conv_w, conv_w, conv_b, conv_b)


def _conv_bwd(up_pre, dact, conv_w, conv_b, name, comm=None):
    s, ff2 = up_pre.shape
    ff = ff2 // 2
    tc = _div_tile(ff, 256)
    nb = ff // tc
    assert s % CONV_ROWS == 0

    def body(pg_ref, pv_ref, da_ref, wg_ref, wv_ref, bg_ref, bv_ref, dup_ref, gwg_ref, gwv_ref, gbg_ref, gbv_ref,
             dxg_ref, dxv_ref):
        for ref in (gwg_ref, gwv_ref, gbg_ref, gbv_ref):
            ref[...] = jnp.zeros(ref.shape, F32)
        for ref in (dxg_ref, dxv_ref):
            ref[s:s + SUBLANES, :] = jnp.zeros((SUBLANES, tc), F32)

        def sums(taps, dx, gw_ref, gb_ref):
            gb_ref[...] += _colsum(dx)
            for k in range(CONV_TAPS):
                gw_ref[k:k + 1, :] += _colsum(dx * taps[CONV_TAPS - 1 - k])

        def forward(r0, first):
            rows = pl.ds(r0, CONV_ROWS)
            taps_g, taps_v = _conv_taps(pg_ref, r0, first), _conv_taps(pv_ref, r0, first)
            gate, val = _conv(taps_g, wg_ref, bg_ref), _conv(taps_v, wv_ref, bv_ref)
            da = da_ref[rows, :].astype(F32)
            sg = _sigmoid(gate)
            dxv, dxg = da * gate * sg, da * val * sg * (1.0 + gate * (1.0 - sg))
            dxv_ref[rows, :], dxg_ref[rows, :] = dxv, dxg
            sums(taps_v, dxv, gwv_ref, gbv_ref)
            sums(taps_g, dxg, gwg_ref, gbg_ref)

        def backward(r0, first):
            del first
            n = CONV_ROWS + SUBLANES
            for dx_ref, w_ref, out_ref in ((dxg_ref, wg_ref, dup_ref.at[0]), (dxv_ref, wv_ref, dup_ref.at[1])):
                win = dx_ref[pl.ds(r0, n), :]
                ahead1 = pltpu.roll(win, n - 1, axis=0)[:CONV_ROWS]
                ahead2 = pltpu.roll(win, n - 2, axis=0)[:CONV_ROWS]
                out_ref[pl.ds(r0, CONV_ROWS), :] = (w_ref[2:3, :] * win[:CONV_ROWS] + w_ref[1:2, :] * ahead1
                                                    + w_ref[0:1, :] * ahead2).astype(BF16)

        _row_steps(s, forward)
        _row_steps(s, backward)

    def col(rows, off):
        return pl.BlockSpec((rows, tc), lambda j: (0, j + off))

    return _call(
        body, (up_pre, up_pre, dact, conv_w, conv_w, conv_b, conv_b), comm, name=name, grid=(nb,),
        in_specs=[col(s, 0), col(s, nb), col(s, 0), col(CONV_TAPS, 0), col(CONV_TAPS, nb), col(1, 0), col(1, nb)],
        out_specs=[pl.BlockSpec((2, s, tc), lambda j: (0, 0, j)), col(CONV_TAPS, 0), col(CONV_TAPS, 0),
                   col(1, 0), col(1, 0)],
        out_shape=[jax.ShapeDtypeStruct((2, s, ff), BF16)] + [jax.ShapeDtypeStruct((CONV_TAPS, ff), F32)] * 2
        + [jax.ShapeDtypeStruct((1, ff), F32)] * 2,
        scratch_shapes=[pltpu.VMEM((s + SUBLANES, tc), F32)] * 2)


def _ada_fwd(c_all, w, b, name):
    nseq, d = c_all.shape
    na = w.shape[1]
    tn = _div_tile(na, 512)

    def body(c_ref, w_ref, b_ref, o_ref):
        cv = c_ref[...]
        sc = cv * _sigmoid(cv)
        o_ref[...] = jnp.dot(sc, w_ref[...], preferred_element_type=F32, precision=lax.Precision.HIGHEST) + b_ref[...]

    return pl.pallas_call(
        body, name=name, grid=(na // tn,),
        in_specs=[pl.BlockSpec((nseq, d), lambda j: (0, 0)), pl.BlockSpec((d, tn), lambda j: (0, j)),
                  pl.BlockSpec((1, tn), lambda j: (0, j))],
        out_specs=pl.BlockSpec((nseq, tn), lambda j: (0, j)),
        out_shape=jax.ShapeDtypeStruct((nseq, na), F32), compiler_params=_params(),
    )(c_all, w, b)


def _ada_bwd(c_all_t, dmod, name):
    d, nseq = c_all_t.shape
    na = dmod.shape[1]
    tm, tn = _div_tile(d, 512, SUBLANES), _div_tile(na, 1024)

    def body(c_ref, dm_ref, o_ref):
        cv = c_ref[...]
        o_ref[...] = jnp.dot(cv * _sigmoid(cv), dm_ref[...], preferred_element_type=F32,
                             precision=lax.Precision.HIGHEST)

    return pl.pallas_call(
        body, name=name, grid=(d // tm, na // tn),
        in_specs=[pl.BlockSpec((tm, nseq), lambda i, j: (i, 0)), pl.BlockSpec((nseq, tn), lambda i, j: (0, j))],
        out_specs=pl.BlockSpec((tm, tn), lambda i, j: (i, j)),
        out_shape=jax.ShapeDtypeStruct((d, na), F32), compiler_params=_params(),
    )(c_all_t, dmod)


def _adamw(w, g, m, v, name, comm=None, after=None, emit_grad=False):
    rows, cols = w.shape
    n_out = 4 if emit_grad else 3
    tb = _div_tile(rows, max(SUBLANES, ADAMW_BLOCK_ELEMS // cols // SUBLANES * SUBLANES), SUBLANES)
    c1 = 1.0 / (1.0 - ADAM_B1 ** ADAM_STEP)
    c2 = 1.0 / (1.0 - ADAM_B2 ** ADAM_STEP)

    def body(*refs):
        w_ref, g_ref, m_ref, v_ref = refs[:4]
        d_ref, nm_ref, nv_ref = refs[-3:]
        gv = g_ref[...]
        if emit_grad:
            refs[-4][...] = gv
        nm =ADAM_B1 * m_ref[...] + (1.0 - ADAM_B1) * gv
        nv = ADAM_B2 * v_ref[...] + (1.0 - ADAM_B2) * (gv * gv)
        nm_ref[...] = nm
        nv_ref[...] = nv
        d_ref[...] = -ADAM_LR * ((nm * c1) / (jnp.sqrt(nv * c2) + ADAM_EPS) + ADAM_WD * w_ref[...])

    blk = pl.BlockSpec((tb, cols), lambda i: (i, 0))
    operands, in_specs = (w, g, m, v), [blk] * 4
    if after is not None:
        operands, in_specs = operands + (after,), in_specs + [pl.BlockSpec(after.shape, lambda i: (0, 0))]
    return _call(body, operands, comm, name=name, grid=(rows // tb,), in_specs=in_specs, out_specs=[blk] * n_out,
                 out_shape=[jax.ShapeDtypeStruct((rows, cols), F32)] * n_out)


def _sum_leading(parts, name, after=()):
    n, rows, cols = parts.shape
    tb = _div_tile(rows, 512, SUBLANES)

    def body(p_ref, *rest):
        o_ref = rest[-1]
        acc = p_ref[0]
        for k in range(1, n):
            acc = acc + p_ref[k]
        o_ref[...] = acc

    return pl.pallas_call(
        body, name=name, grid=(rows // tb,),
        in_specs=[pl.BlockSpec((n, tb, cols), lambda i: (0, i, 0))] + [pl.BlockSpec(memory_space=pl.ANY)] * len(after),
        out_specs=pl.BlockSpec((tb, cols), lambda i: (i, 0)),
        out_shape=jax.ShapeDtypeStruct((rows, cols), F32), compiler_params=_params(),
    )(parts, *after)


def _place():
    x, y, c = lax.axis_index("x"), lax.axis_index("y"), lax.axis_index("c")
    return x, y, c, [(1 - x, y), (x, 1 - y), (1 - x, 1 - y)]


def _all_gather(block, name):
    m_per, n = block.shape

    def body(x_ref, out_ref, send_sems, recv_sems, local_sem):
        x, y, c, chips = _place()
        me, sibling = (x, y, c), (x, y, 1 - c)

        def rows(px, py, pc):
            return out_ref.at[pl.ds((4 * px + 2 * py + pc) * m_per, m_per), :]

        def copy(k, blk, to, src=None):
            return pltpu.make_async_remote_copy(
                src_ref=rows(*blk) if src is None else src, dst_ref=rows(*blk), send_sem=send_sems.at[k],
                recv_sem=recv_sems.at[k], device_id=to, device_id_type=MESH)

        mine = pltpu.make_async_copy(x_ref, rows(*me), local_sem)
        mine.start()
        first = [copy(0, me, sibling, src=x_ref)]
        first += [copy(1 + j, me, (*chip, c), src=x_ref) for j, chip in enumerate(chips)]
        for cp in first:
            cp.start()
        passed = [copy(4 + j, (*chip, c), sibling) for j, chip in enumerate(chips)]
        for j, chip in enumerate(chips):
            copy(1 + j, (*chip, c), me).wait_recv()
            passed[j].start()
        copy(0, sibling, me).wait_recv()
        for j, chip in enumerate(chips):
            copy(4 + j, (*chip, 1 - c), me).wait_recv()
        for cp in first + passed:
            cp.wait_send()
        mine.wait()

    return pl.pallas_call(
        body, name=name, out_shape=jax.ShapeDtypeStruct((N_DEV * m_per, n), block.dtype),
        in_specs=[pl.BlockSpec(memory_space=pltpu.VMEM)], out_specs=pl.BlockSpec(memory_space=pltpu.VMEM),
        scratch_shapes=[pltpu.SemaphoreType.DMA((7,)), pltpu.SemaphoreType.DMA((7,)), pltpu.SemaphoreType.DMA],
        compiler_params=_params(),
    )(block)


def _hbm_specs(n):
    return [pl.BlockSpec(memory_space=HBM)] * n


def _part(ref, by_cols, half, quarter=None, lead=None):
    extent = ref.shape[-1] if by_cols else ref.shape[-2]
    size = extent // 2 if quarter is None else extent // 4
    first = half * (extent // 2) + (0 if quarter is None else quarter * size)
    tile = LANES if by_cols else 2 * SUBLANES
    span = pl.ds(pl.multiple_of(first, tile) if size % tile == 0 else first, size)
    index = (slice(None), span) if by_cols else (span, slice(None))
    return ref.at[index] if lead is None else ref.at[(lead,) + index]


def _half_rows(ref, half, lead=None):
    return _part(ref, False, half, lead=lead)


class _Comm:
    def __init__(self, operands, out_shape, sem_dims, build, aliases=None):
        self.operands, self.out_shape, self.sem_dims = list(operands), list(out_shape), list(sem_dims)
        self.scratch = [pltpu.SemaphoreType.DMA(d) for d in sem_dims]
        self.build, self.aliases = build, dict(aliases or {})


class _SemGrid:
    def __init__(self, sems, dims):
        self.sems, self.dims, self.at = list(sems), tuple(dims), self

    def __getitem__(self, index):
        index = index if isinstance(index, tuple) else (index,)
        flat = 0
        for i, d in zip(index, self.dims):
            flat = flat * d + i
        return self.sems[flat]


def _call(body, operands, comm=None, *, name, grid, in_specs, out_specs, out_shape, scratch_shapes=(),
          input_output_aliases=None):
    aliases = dict(input_output_aliases or {})
    if comm is None:
        return pl.pallas_call(
            body, name=name, grid=grid, in_specs=in_specs, out_specs=out_specs, out_shape=out_shape,
            scratch_shapes=list(scratch_shapes), input_output_aliases=aliases, compiler_params=_params())(*operands)
    single = not isinstance(out_shape, (list, tuple))
    outs = [out_shape] if single else list(out_shape)
    ospecs = [out_specs] if single else list(out_specs)
    n_in, n_out, n_scr = len(operands), len(outs), len(scratch_shapes)
    c_in, c_out = len(comm.operands), len(comm.out_shape)
    for i, o in comm.aliases.items():
        aliases[n_in + i] = n_out + o

    def hosted(*refs):
        ins, c_ins = refs[:n_in], refs[n_in:n_in + c_in]
        o0 = n_in + c_in
        o_refs, c_outs = refs[o0:o0 + n_out], refs[o0 + n_out:o0 + n_out + c_out]
        s0 = o0 + n_out + c_out
        scr, sems = refs[s0:s0 + n_scr], refs[s0 + n_scr:]
        stages = comm.build(c_ins, c_outs, sems)
        step, n_steps = 0, 1
        for dim, size in enumerate(grid):
            step, n_steps = step * size + pl.program_id(dim), n_steps * size
        pl.when(step == 0)(stages[0])
        body(*ins, *o_refs, *scr)
        for stage in stages[1:-1]:
            pl.when(step == (n_steps * MIDDLE_STAGE_AT) // 100)(stage)
        pl.when(step == n_steps - 1)(stages[-1])

    res = pl.pallas_call(
        hosted, name=name, grid=grid, in_specs=list(in_specs) + _hbm_specs(c_in),
        out_specs=ospecs + _hbm_specs(c_out), out_shape=outs + comm.out_shape,
        scratch_shapes=list(scratch_shapes) + comm.scratch, input_output_aliases=aliases,
        compiler_params=_params())(*operands, *comm.operands)
    return (res[0] if single else res[:n_out]), res[n_out:]


def _run_comm(comm, name):
    c_in, c_out = len(comm.operands), len(comm.out_shape)

    def body(*refs):
        for stage in comm.build(refs[:c_in], refs[c_in:c_in + c_out], refs[c_in + c_out:]):
            stage()

    return pl.pallas_call(
        body, name=name, in_specs=_hbm_specs(c_in), out_specs=_hbm_specs(c_out), out_shape=comm.out_shape,
        scratch_shapes=comm.scratch, input_output_aliases=comm.aliases, compiler_params=_params())(*comm.operands)


def _join_comms(comms):
    def build(in_refs, out_refs, sems):
        staged, i, o, k = [], 0, 0, 0
        for cm in comms:
            ni, no, ns = len(cm.operands), len(cm.out_shape), len(cm.sem_dims)
            staged.append(cm.build(in_refs[i:i + ni], out_refs[o:o + no], sems[k:k + ns]))
            i, o, k = i + ni, o + no, k + ns
        def run(fns):
            def stage():
                for fn in fns:
                    fn()
            return stage

        return (run([st[0] for st in staged]), run([fn for st in staged for fn in st[1:-1]]),
                run([st[-1] for st in staged]))

    aliases, i, o = {}, 0, 0
    for cm in comms:
        aliases.update({i + a: o + b for a, b in cm.aliases.items()})
        i, o = i + len(cm.operands), o + len(cm.out_shape)
    return _Comm(sum((cm.operands for cm in comms), []), sum((cm.out_shape for cm in comms), []),
                 sum((cm.sem_dims for cm in comms), []), build, aliases)


def _gather8_comm(block):
    def build(in_refs, out_refs, sems):
        (src,), (out,), (send_sems, recv_sems) = in_refs, out_refs, sems
        x, y, c, chips = _place()
        me, sibling = (x, y, c), (x, y, 1 - c)

        def copy(k, blk, to, own=False):
            dst = out.at[4 * blk[0] + 2 * blk[1] + blk[2]]
            return pltpu.make_async_remote_copy(
                src_ref=src if own else dst, dst_ref=dst, send_sem=send_sems.at[k], recv_sem=recv_sems.at[k],
                device_id=to, device_id_type=MESH)

        first = [copy(0, me, sibling, own=True)] + [copy(1 + j, me, (*chip, c), own=True)
                                                     for j, chip in enumerate(chips)]
        passed = [copy(4 + j, (*chip, c), sibling) for j, chip in enumerate(chips)]

        def start():
            for cp in first:
                cp.start()

        def middle():
            for j, chip in enumerate(chips):
                copy(1 + j, (*chip, c), me).wait_recv()
                passed[j].start()

        def finish():
            copy(0, sibling, me).wait_recv()
            for j, chip in enumerate(chips):
                copy(4 + j, (*chip, 1 - c), me).wait_recv()
            for cp in first + passed:
                cp.wait_send()

        return start, middle, finish

    return _Comm([block], [jax.ShapeDtypeStruct((N_DEV,) + block.shape, block.dtype)], [(7,), (7,)], build)


def _gather_comm(shards, by_cols=()):
    nw = len(shards)

    def build(in_refs, out_refs, sems):
        send_sems, recv_sems = sems
        x, y, c, chips = _place()
        me, sibling = (x, y, c), (x, y, 1 - c)
        across_x, across_y, diagonal = chips

        def copy(w, k, block, part, to, src=None):
            dst = _part(out_refs[w], w in by_cols, part[1], part[2] if part[0] else None, 2 * block[0] + block[1])
            return pltpu.make_async_remote_copy(
                src_ref=dst if src is None else src, dst_ref=dst, send_sem=send_sems.at[w, k],
                recv_sem=recv_sems.at[w, k], device_id=to, device_id_type=MESH)

        first = [copy(w, j, (x, y), (0, c), (*chip, c), src=_part(in_refs[w], w in by_cols, c))
                 for w in range(nw) for j, chip in enumerate((across_x, across_y))]
        first += [pltpu.make_async_remote_copy(
            src_ref=in_refs[w], dst_ref=out_refs[w].at[2 * x + y], send_sem=send_sems.at[w, 8],
            recv_sem=recv_sems.at[w, 8], device_id=sibling, device_id_type=MESH) for w in range(nw)]
        passed = [[copy(w, 2, across_x, (1, c, 0), (*across_y, c)), copy(w, 3, across_y, (1, c, 1), (*across_x, c)),
                   copy(w, 4, across_x, (0, c), sibling), copy(w, 5, across_y, (0, c), sibling)] for w in range(nw)]
        last = [[copy(w, 6, diagonal, (1, c, 0), sibling), copy(w, 7, diagonal, (1, c, 1), sibling)]
                for w in range(nw)]

        def start():
            for cp in first:
                cp.start()

        def middle():
            for w in range(nw):
                copy(w, 0, across_x, (0, c), me).wait_recv()
                copy(w, 1, across_y, (0, c), me).wait_recv()
                for cp in passed[w]:
                    cp.start()

        def finish():
            for w in range(nw):
                copy(w, 2, diagonal, (1, c, 0), me).wait_recv()
                copy(w, 3, diagonal, (1, c, 1), me).wait_recv()
                for cp in last[w]:
                    cp.start()
            for w in range(nw):
                for k, block, part in ((4, across_x, (0, 1 - c)), (5, across_y, (0, 1 - c)),
                                       (6, diagonal, (1, 1 - c, 0)), (7, diagonal, (1, 1 - c, 1))):
                    copy(w, k, block, part, me).wait_recv()
                pltpu.make_async_remote_copy(
                    src_ref=in_refs[w], dst_ref=out_refs[w].at[2 * x + y], send_sem=send_sems.at[w, 8],
                    recv_sem=recv_sems.at[w, 8], device_id=sibling, device_id_type=MESH).wait_recv()
            for cp in first + sum(passed, []) + sum(last, []):
                cp.wait_send()

        return start, middle, finish

    return _Comm(shards, [jax.ShapeDtypeStruct((N_CHIPS,) + w.shape, w.dtype) for w in shards],
                 [(nw, 9), (nw, 9)], build)


def _halved(shape, by_cols):
    return shape[:-1] + (shape[-1] // 2,) if by_cols else shape[:-2] + (shape[-2] // 2, shape[-1])


def _swap_comm(gs, by_cols=()):
    nw = len(gs)

    def build(in_refs, out_refs, sems):
        send_sems, recv_sems = sems
        x, y, c, _ = _place()
        cps = []
        for w in range(nw):
            cps.append(pltpu.make_async_remote_copy(
                src_ref=_part(in_refs[w], w in by_cols, 1 - c, lead=slice(None)), dst_ref=out_refs[w],
                send_sem=send_sems.at[w], recv_sem=recv_sems.at[w], device_id=(x, y, 1 - c), device_id_type=MESH))

        def start():
            for cp in cps:
                cp.start()

        def finish():
            for cp in cps:
                cp.wait()

        return start, finish

    return _Comm(gs, [jax.ShapeDtypeStruct(_halved(g.shape, w in by_cols), g.dtype) for w, g in enumerate(gs)],
                 [(nw,), (nw,)], build)


def _exchange_comm(s1s):
    nw = len(s1s)

    def build(in_refs, out_refs, sems):
        send_sems, recv_sems = sems
        x, y, c, chips = _place()
        cps = [pltpu.make_async_remote_copy(
            src_ref=in_refs[w].at[2 * chip[0] + chip[1]], dst_ref=out_refs[w].at[j], send_sem=send_sems.at[w, j],
            recv_sem=recv_sems.at[w, j], device_id=(*chip, c), device_id_type=MESH)
            for w in range(nw) for j, chip in enumerate(chips)]

        def start():
            for cp in cps:
                cp.start()

        def finish():
            for cp in cps:
                cp.wait()

        return start, finish

    return _Comm(s1s, [jax.ShapeDtypeStruct((N_CHIPS - 1,) + s.shape[1:], s.dtype) for s in s1s],
                 [(nw, 3), (nw, 3)], build)


def _size(dims):
    n = 1
    for d in dims:
        n *= d
    return n


def _sem_grids(comm, sem_refs):
    grids, pos = [], 0
    for dims in comm.sem_dims:
        grids.append(_SemGrid(sem_refs[pos:pos + _size(dims)], dims))
        pos += _size(dims)
    return grids


def _comm_split_start(comm, name, after=()):
    c_in, c_out = len(comm.operands), len(comm.out_shape)
    counts = [_size(d) for d in comm.sem_dims]
    n_sem = sum(counts)
    assert not comm.aliases

    def body(*refs):
        srcs, lands = refs[:c_in], refs[c_in:c_in + c_out]
        first_sem = c_in + c_out + len(after)
        start, _ = comm.build(srcs, lands, _sem_grids(comm, refs[first_sem:first_sem + n_sem]))
        start()
        refs[-1][...] = jnp.zeros(refs[-1].shape, refs[-1].dtype)

    lands = [pltpu.with_memory_space_constraint(lax.empty(o.shape, o.dtype), HBM) for o in comm.out_shape]
    srcs = [pltpu.with_memory_space_constraint(a, HBM) for a in comm.operands]
    res = pl.pallas_call(
        body, name=name, in_specs=_hbm_specs(c_in + c_out) + [pl.BlockSpec(memory_space=pl.ANY)] * len(after),
        out_specs=[pl.BlockSpec(memory_space=pltpu.SEMAPHORE)] * n_sem + _hbm_specs(c_in + c_out)
        + [pl.BlockSpec(memory_space=pltpu.VMEM)],
        out_shape=[pltpu.SemaphoreType.DMA(())] * n_sem + [pltpu.HBM(a.shape, a.dtype) for a in comm.operands]
        + [pltpu.HBM(o.shape, o.dtype) for o in comm.out_shape] + [jax.ShapeDtypeStruct((SUBLANES, LANES), F32)],
        input_output_aliases={i: n_sem + i for i in range(c_in + c_out)},
        compiler_params=_params(has_side_effects=pltpu.SideEffectType.DATAFLOW_SIDE_EFFECTING))(*srcs, *lands, *after)
    return res[:-1], res[-1]


def _comm_split_wait(comm, state, after, name):
    c_in, c_out, n_sem = len(comm.operands), len(comm.out_shape), sum(_size(d) for d in comm.sem_dims)
    sems, srcs, lands = state[:n_sem], state[n_sem:n_sem + c_in], state[n_sem + c_in:]

    def body(*refs):
        src_refs, land_refs = refs[:c_in], refs[c_in:c_in + c_out]
        _, finish = comm.build(src_refs, land_refs, _sem_grids(comm, refs[c_in + c_out:c_in + c_out + n_sem]))
        finish()

    sem_spec = pl.BlockSpec(memory_space=pltpu.SEMAPHORE)
    res = pl.pallas_call(
        body, name=name, in_specs=_hbm_specs(c_in + c_out) + [sem_spec] * n_sem + [pl.BlockSpec(memory_space=pl.ANY)],
        out_specs=_hbm_specs(c_in + c_out),
        out_shape=[pltpu.HBM(a.shape, a.dtype) for a in srcs] + [pltpu.HBM(o.shape, o.dtype) for o in lands],
        input_output_aliases={i: i for i in range(c_in + c_out)},
        compiler_params=_params(has_side_effects=pltpu.SideEffectType.DATAFLOW_SIDE_EFFECTING),
    )(*srcs, *lands, *sems, after)
    return res[:c_in], res[c_in:]


def _share_comm(fs, by_cols=()):
    nw = len(fs)

    def build(in_refs, out_refs, sems):
        del in_refs
        send_sems, recv_sems = sems
        x, y, c, _ = _place()

        def copy(w, half):
            part = _part(out_refs[w], w in by_cols, half)
            return pltpu.make_async_remote_copy(
                src_ref=part, dst_ref=part, send_sem=send_sems.at[w], recv_sem=recv_sems.at[w],
                device_id=(x, y, 1 - c), device_id_type=MESH)

        sends = [copy(w, c) for w in range(nw)]

        def start():
            for cp in sends:
                cp.start()

        def finish():
            for w in range(nw):
                copy(w, 1 - c).wait_recv()
            for cp in sends:
                cp.wait_send()

        return start, finish

    return _Comm(fs, [jax.ShapeDtypeStruct(f.shape, f.dtype) for f in fs],
                 [(nw,), (nw,)], build,
                 aliases={w: w for w in range(nw)})


def _add_sibling(g, r1, place, name, by_cols=False):
    nch, h, cols = r1.shape
    tr = _div_tile(h, 1024 if by_cols else 512, 2 * SUBLANES)
    nb = h // tr
    mine = (lambda k, i, p: (k, i, p[0])) if by_cols else (lambda k, i, p: (k, p[0] * nb + i, 0))

    def body(place_ref, g_ref, r_ref, o_ref):
        del place_ref
        o_ref[...] = (g_ref[...].astype(F32) + r_ref[...].astype(F32)).astype(BF16)

    spec = pltpu.PrefetchScalarGridSpec(
        num_scalar_prefetch=1, grid=(nch, nb),
        in_specs=[pl.BlockSpec((None, tr, cols), mine), pl.BlockSpec((None, tr, cols), lambda k, i, p: (k, i, 0))],
        out_specs=pl.BlockSpec((None, tr, cols), lambda k, i, p: (k, i, 0)))
    return pl.pallas_call(body, name=name, grid_spec=spec, out_shape=jax.ShapeDtypeStruct((nch, h, cols), BF16),
                          compiler_params=_params())(place, g, r1)


def _add_chips(s1, r2, place, name, by_cols=False):
    _, h, cols = s1.shape
    tr = _div_tile(h, 1024 if by_cols else 512, 2 * SUBLANES)
    nb = h // tr
    mine = (lambda i, p: (i, p[0])) if by_cols else (lambda i, p: (p[0] * nb + i, 0))
    whole = (h, 2 * cols) if by_cols else (2 * h, cols)

    def body(place_ref, s_ref, r_ref, o_ref):
        del place_ref
        acc = s_ref[...].astype(F32)
        for j in range(N_CHIPS - 1):
            acc = acc + r_ref[j].astype(F32)
        o_ref[...] = acc

    spec = pltpu.PrefetchScalarGridSpec(
        num_scalar_prefetch=1, grid=(nb,),
        in_specs=[pl.BlockSpec((None, tr, cols), lambda i, p: (p[1], i, 0)),
                  pl.BlockSpec((N_CHIPS - 1, tr, cols), lambda i, p: (0, i, 0))],
        out_specs=pl.BlockSpec((tr, cols), mine))
    return pl.pallas_call(body, name=name, grid_spec=spec, out_shape=jax.ShapeDtypeStruct(whole, F32),
                          compiler_params=_params())(place, s1, r2)


def _quarter_turn(m):
    h = m.shape[-1] // 2
    return jnp.concatenate([-m[..., h:], m[..., :h]], axis=-1)


def _quarter_turn_back(m):
    h = m.shape[-1] // 2
    return jnp.concatenate([m[..., h:], -m[..., :h]], axis=-1)


def _stack_rows(parts):
    out = lax.empty((sum(p.shape[0] for p in parts),) + parts[0].shape[1:], parts[0].dtype)
    row = 0
    for p in parts:
        out = lax.dynamic_update_slice(out, p, (row, 0))
        row += p.shape[0]
    return out


def _join_cols(sh):
    return jnp.concatenate([sh[k] for k in range(N_CHIPS)], axis=1)


def _split_cols(full):
    c = full.shape[1] // N_CHIPS
    return jnp.stack([full[:, k * c:(k + 1) * c] for k in range(N_CHIPS)])


def kernel(x, c, positions, w_ada, b_ada, pre_norm1_g, w_in, gm_ln_g, gm_ln_b, gm_w_s, gm_b_s, w_branch_a, q_norm_g, w_uq, kv_norm_g, w_ukv, w_branch_b, w_out, post_norm1_g, pre_norm2_g, w_up, conv_w, conv_b, w_down, post_norm2_g, loss_target, m_w_ada, m_b_ada, m_pre_norm1_g, m_w_in, m_gm_ln_g, m_gm_ln_b, m_gm_w_s, m_gm_b_s, m_w_branch_a, m_q_norm_g, m_w_uq, m_kv_norm_g, m_w_ukv, m_w_branch_b, m_w_out, m_post_norm1_g, m_pre_norm2_g, m_w_up, m_conv_w, m_conv_b, m_w_down, m_post_norm2_g, v_w_ada, v_b_ada, v_pre_norm1_g, v_w_in, v_gm_ln_g, v_gm_ln_b, v_gm_w_s, v_gm_b_s, v_w_branch_a, v_q_norm_g, v_w_uq, v_kv_norm_g, v_w_ukv, v_w_branch_b, v_w_out, v_post_norm1_g, v_pre_norm2_g, v_w_up, v_conv_w, v_conv_b, v_w_down, v_post_norm2_g):
    given = dict(locals())
    s, d = x.shape[1], x.shape[2]
    gw = gm_ln_g.shape[0]
    ql, kvl = q_norm_g.shape[0], kv_norm_g.shape[0]
    heads = N_CHIPS * w_uq.shape[1] // (NOPE + ROPE)
    ff = N_CHIPS * w_down.shape[0]
    assert gw == d and N_CHIPS * w_ukv.shape[1] == heads * (NOPE + VHEAD)
    ix, iy, ic = lax.axis_index("x"), lax.axis_index("y"), lax.axis_index("c")
    chip = 2 * ix + iy
    dev = 2 * chip + ic
    row = lambda v: v.reshape(1, -1)

    first = _all_gather(jnp.concatenate([jnp.pad(c, ((0, SUBLANES - 1), (0, 0))),
                                         jnp.pad(conv_w, ((0, SUBLANES - CONV_TAPS), (0, 0)))], axis=1), "gather_c")
    first = first.reshape(N_DEV, SUBLANES, d + conv_w.shape[1])
    c_all = first[:, 0, :d]
    conv_wf = first[::N_CORES, :CONV_TAPS, d:].transpose(1, 0, 2).reshape(CONV_TAPS, N_CHIPS * conv_w.shape[1])
    na = w_ada.shape[1]
    b_ada_mine = lax.dynamic_slice(b_ada, (chip * na,), (na,))
    mod_cols = _ada_fwd(c_all, w_ada, row(b_ada_mine), "ada_fwd")
    mod_all = _all_gather(mod_cols, "gather_mod").reshape(N_CHIPS, N_CORES, N_DEV, na)[:, 0]
    mod = lax.dynamic_index_in_dim(mod_all, dev, axis=1, keepdims=False).reshape(N_MOD, d)
    shift1, scale1, gate1, shift2, scale2, gate2 = (mod[i:i + 1] for i in range(N_MOD))

    mine = {n: (given[n].T if n == "w_in" else given[n]).astype(BF16) for n in BIG}
    gather = lambda names: _gather_comm([mine[n] for n in names], [i for i, n in enumerate(names) if n == "w_in"])
    whole = lambda n, g: g
    rows4 = lambda sh4: sh4.reshape(-1, sh4.shape[2])
    wi_t = rows4(whole("w_in", _run_comm(gather(["w_in"]), "gather_w_in")[0]))
    o_q, o_kv, o_pe, o_ga = 2 * gw, 2 * gw + ql, 2 * gw + ql + kvl, 2 * gw + ql + kvl + ROPE
    w_in_big_t = _stack_rows([wi_t[:o_q], wi_t[o_ga:]])
    w_in_lat_t = _stack_rows([wi_t[o_q:o_ga], _quarter_turn(wi_t[o_pe:o_ga].T).T])

    inv = ROPE_THETA ** (-jnp.arange(0, ROPE, 2, dtype=F32) / ROPE)
    ang = positions[0].astype(F32)[:, None] * inv
    cos, sin = jnp.cos(ang), jnp.sin(ang)
    rope_k = jnp.concatenate([cos, cos, sin, sin], axis=1)
    softmax_scale = float(NOPE + ROPE) ** -0.5
    rope_q = jnp.concatenate([jnp.ones((s, NOPE), F32), rope_k], axis=1) * softmax_scale

    x2d, tgt = x[0], loss_target[0]
    g_pre1, g_post1, g_pre2, g_post2 = row(pre_norm1_g), row(post_norm1_g), row(pre_norm2_g), row(post_norm2_g)
    ln_g, ln_b, q_g, kv_g = row(gm_ln_g), row(gm_ln_b), row(q_norm_g), row(kv_norm_g)
    b_s_t = gm_b_s.T
    conv_bf = row(conv_b)

    h1 = _prenorm(x2d, g_pre1, scale1, shift1, "prenorm1")
    z_big, (g_uq, g_ukv, g_a) = _matmul(h1, w_in_big_t, mode="nt", out_dtype=BF16, name="mm_z_big", tm=s,
                                        comm=gather(["w_uq", "w_ukv", "w_branch_a"]))
    wq = _join_cols(whole("w_uq", g_uq)).reshape(ql, heads, NOPE + ROPE)
    w_q = jnp.concatenate([wq, _quarter_turn(wq[:, :, NOPE:])], axis=2).reshape(ql, heads * HEAD_W)
    w_kv = _join_cols(whole("w_ukv", g_ukv)).reshape(kvl, heads, 2, NOPE).transpose(0, 2, 1, 3)
    w_kv = w_kv.reshape(kvl, 2 * heads * NOPE)
    w_a = rows4(whole("w_branch_a", g_a))
    z_lat = _matmul(h1, w_in_lat_t, mode="nt", out_dtype=F32, name="mm_z_lat", tm=s, tn=1024)
    a_act = _gmlp_fwd(z_big, ln_g, ln_b, gm_w_s, b_s_t, "gmlp_fwd")
    qn, kvn, kr = _mla_prep(z_lat, q_g, kv_g, rope_k, "mla_prep")
    q_rot = _matmul(qn, w_q, mode="nn", out_dtype=BF16, name="mm_q", tm=s, tn=HEAD_W, mul=rope_q)
    kv_all = _matmul(kvn, w_kv, mode="nn", out_dtype=BF16, name="mm_kv", tm=s, tn=1024)
    (o_att, lse), (g_b, g_o, g_up) = _attn_fwd(q_rot, kv_all, kr, heads, "attn_fwd",
                                               comm=gather(["w_branch_b", "w_out", "w_up"]))
    w_b, w_o, w_upf = rows4(whole("w_branch_b", g_b)), rows4(whole("w_out", g_o)), whole("w_up", g_up)
    y_a = _matmul(a_act, w_a, mode="nn", out_dtype=BF16, name="mm_y_a", tm=s)
    y_b = _matmul(o_att, w_b, mode="nn", out_dtype=BF16, name="mm_y_b", tm=s)
    merged = _merge(z_big, y_a, y_b, "merge")
    y1 = _matmul(merged, w_o, mode="nn", out_dtype=F32, name="mm_y1", tm=s)
    x1, h2 = _post_pre(x2d, y1, gate1, g_post1, g_pre2, scale2, shift2, "post1_pre2")

    up_pre, (g_dn,) = _matmul(h2, w_upf, mode="nn", out_dtype=BF16, name="mm_up", tm=s, tn=1408,
                              comm=gather(["w_down"]))
    w_dn = rows4(whole("w_down", g_dn))
    act = _conv_fwd(up_pre, conv_wf, conv_bf, "conv_fwd")
    ffn = _matmul(act, w_dn, mode="nn", out_dtype=F32, name="mm_ffn", tm=s, tn=1024, tk=1408)

    dffn, dgate2, g_post2_grad, dx2, loss_part = _post_bwd(ffn, gate2, g_post2, "post2_bwd", xin=x1, target=tgt)
    loss = lax.psum(loss_part[0, 0], ("x", "y", "c"))
    place = jnp.stack([ic, chip]).astype(jnp.int32)
    rows_of = lambda g: g.reshape(N_CHIPS, g.shape[0] // N_CHIPS, g.shape[1])
    add_sibling = lambda names, gs, r1s: [_add_sibling(g, r1, place, "rs_add_sibling_" + n, by_cols=n == "w_in")
                                          for n, g, r1 in zip(names, gs, r1s)]
    add_chips = lambda names, s1s, r2s: [_add_chips(s1, r2, place, "rs_add_chips_" + n, by_cols=n == "w_in")
                                         for n, s1, r2 in zip(names, s1s, r2s)]
    gp_down = [rows_of(_matmul(act, dffn, mode="tn", out_dtype=BF16, name="mm_gw_down", tn=2048, tk=s))]
    dact, r1_down = _matmul(dffn, w_dn, mode="nt", out_dtype=BF16, name="mm_dact", tm=s, comm=_swap_comm(gp_down))
    s1_down = add_sibling(["w_down"], gp_down, r1_down)
    (dup, gcw_g, gcw_v, gcb_g, gcb_v), r2_down = _conv_bwd(up_pre, dact, conv_wf, conv_bf, "conv_bwd",
                                                            comm=_exchange_comm(s1_down))
    half_down = add_chips(["w_down"], s1_down, r2_down)
    dh2 = _matmul(dup, w_upf, mode="nt", out_dtype=F32, name="mm_dh2", tm=s, tn=1024, tk=1408)
    dx1, dshift2, dscale2, g_pre2_grad = _prenorm_bwd(x1, dh2, dx2, g_pre2, scale2, "prenorm2_bwd")

    dy1, dgate1, g_post1_grad = _post_bwd(y1, gate1, g_post1, "post1_bwd", dxo=dx1)
    dmerged = _matmul(dy1, w_o, mode="nt", out_dtype=BF16, name="mm_dmerged", tm=s)
    gw_out = _matmul(merged, dy1, mode="tn", out_dtype=BF16, name="mm_gw_out", tn=1024, tk=s)
    dy_a, dy_b, dz_big = _merge_bwd(dmerged, z_big, y_a, y_b, "merge_bwd")
    gw_a = _matmul(a_act, dy_a, mode="tn", out_dtype=BF16, name="mm_gw_a", tn=1024, tk=s)
    gw_b = _matmul(o_att, dy_b, mode="tn", out_dtype=BF16, name="mm_gw_b", tn=1024, tk=s)
    mid = ["w_up", "w_out", "w_branch_a", "w_branch_b"]
    gp_oab = [rows_of(gw_out), rows_of(gw_a), rows_of(gw_b)]
    da, r1_oab = _matmul(dy_a, w_a, mode="nt", out_dtype=BF16, name="mm_da", tm=s, comm=_swap_comm(gp_oab))
    s1_oab = add_sibling(mid[1:], gp_oab, r1_oab)
    gw_up, r2_oa = _matmul(h2, dup, mode="tn", out_dtype=BF16, name="mm_gw_up", tm=1024, tn=1408, tk=s,
                           out_groups=N_CHIPS, comm=_exchange_comm(s1_oab[:2]))
    do = _matmul(dy_b, w_b, mode="nt", out_dtype=BF16, name="mm_do", tm=s)
    (dz_big, g_ws, g_bs_t, g_ln_g, g_ln_b), r1_up = _gmlp_bwd(z_big, da, dz_big, ln_g, ln_b, gm_w_s, b_s_t,
                                                               "gmlp_bwd", comm=_swap_comm([gw_up]))
    s1_mid = add_sibling(mid[:1], [gw_up], r1_up) + s1_oab
    (dq, dk, dv), r2_up = _attn_bwd(q_rot, kv_all, kr, o_att, do, lse, heads, "attn_bwd",
                                    comm=_exchange_comm(s1_mid[:1]))
    dq_big, dkv, dkk = _mla_bwd_mid(dq, dk, dv, rope_q, rope_k, heads, "mla_bwd_mid")
    gw_q = _matmul(qn, dq_big, mode="tn", out_dtype=F32, name="mm_gw_q", tn=1024, tk=s)
    dqn = _matmul(dq_big, w_q, mode="nt", out_dtype=F32, name="mm_dqn", tm=s, tk=1024)
    gw_kv = _matmul(kvn, dkv, mode="tn", out_dtype=BF16, name="mm_gw_kv", tn=1024, tk=s)
    dkvn = _matmul(dkv, w_kv, mode="nt", out_dtype=F32, name="mm_dkvn", tm=s, tk=1024)
    dz_lat, g_q, g_kv = _mla_bwd_post(z_lat, dqn, dkvn, dkk, q_g, kv_g, "mla_bwd_post")

    partial = {
        "gm_ln_g": g_ln_g, "gm_ln_b": g_ln_b, "gm_w_s": g_ws, "gm_b_s": g_bs_t[:, :gm_b_s.shape[0]].T,
        "q_norm_g": g_q, "kv_norm_g": g_kv, "post_norm1_g": g_post1_grad, "pre_norm2_g": g_pre2_grad,
        "conv_w": jnp.concatenate([gcw_g, gcw_v], axis=1), "conv_b": jnp.concatenate([gcb_g, gcb_v], axis=1),
        "post_norm2_g": g_post2_grad,
    }
    flat = jnp.concatenate([partial[n].reshape(-1) for n in SMALL_PARTIAL])
    n_small = flat.shape[0]
    rows_small = -(-n_small // (LANES * SMALL_ROW_TILE)) * SMALL_ROW_TILE
    flat = jnp.pad(flat, (0, rows_small * LANES - n_small)).reshape(rows_small, LANES)

    def small_pack(prefix, source):
        v = jnp.concatenate([source[prefix + n].reshape(-1) for n in SMALL])
        rows = -(-v.shape[0] // (LANES * SUBLANES)) * SUBLANES
        return jnp.pad(v, (0, rows * LANES - v.shape[0])).reshape(rows, LANES)

    small_state = [small_pack(prefix, given) for prefix in ("", "m_", "v_")]

    dh1, r2_a_b = _matmul(dz_big, w_in_big_t, mode="nn", out_dtype=F32, name="mm_dh1_big", tm=s, tn=1024, tk=1024,
                          comm=_exchange_comm(s1_mid[3:]))
    half_mid = add_chips(mid, s1_mid, list(r2_up) + list(r2_oa) + list(r2_a_b))
    gw_big_t, hosted = _matmul(dz_big, h1, mode="tn", out_dtype=BF16, name="mm_gw_in_big", tn=2048, tk=s,
                               comm=_join_comms([_share_comm(half_down + half_mid), _gather8_comm(flat)]))
    shared, small_all = hosted[:-1], lax.dynamic_update_slice(hosted[-1], flat[None], (dev, 0, 0))
    small_sum = _sum_leading(small_all, "sum_small", after=small_state + [loss.reshape(1, 1)]).reshape(-1)
    small_grads, off = {}, 0
    for n in SMALL_PARTIAL:
        shape = (CONV_TAPS, 2 * ff) if n == "conv_w" else given[n].shape
        small_grads[n] = small_sum[off:off + partial[n].size].reshape(shape)
        off += partial[n].size
    small_grads["conv_w"] = lax.dynamic_slice(small_grads["conv_w"], (0, chip * conv_w.shape[1]), conv_w.shape)
    grads = dict(zip(["w_down"] + mid, shared), **small_grads)
    gw_lat_t = _matmul(dz_lat, h1, mode="tn", out_dtype=F32, name="mm_gw_in_lat", tm=1024, tn=1024, tk=s)

    gq = gw_q.reshape(ql, heads, HEAD_W)
    gq_pe = gq[:, :, NOPE:NOPE + ROPE] + _quarter_turn_back(gq[:, :, NOPE + ROPE:])
    g_pe_t = gw_lat_t[ql + kvl:ql + kvl + ROPE] + _quarter_turn_back(gw_lat_t[ql + kvl + ROPE:].T).T
    last = ["w_in", "w_uq", "w_ukv"]
    gw_in_t = _stack_rows([gw_big_t[:o_q], gw_lat_t[:ql + kvl].astype(BF16), g_pe_t.astype(BF16), gw_big_t[o_q:]])
    gp_last = [
        gw_in_t.reshape(N_CHIPS, gw_in_t.shape[0] // N_CHIPS, d),
        _split_cols(jnp.concatenate([gq[:, :, :NOPE], gq_pe], axis=2).reshape(ql, heads * (NOPE + ROPE)).astype(BF16)),
        _split_cols(gw_kv.reshape(kvl, 2, heads, NOPE).transpose(0, 2, 1, 3).reshape(kvl, heads * 2 * NOPE)),
    ]
    dh1, r1_last = _matmul(dz_lat, w_in_lat_t, mode="nn", out_dtype=F32, name="mm_dh1_lat", tm=s, tk=1024, add=dh1,
                           comm=_swap_comm(gp_last, by_cols=[0]))
    grad_x, dshift1, dscale1, g_pre1_grad = _prenorm_bwd(x2d, dh1, dx1, g_pre1, scale1, "prenorm1_bwd")
    s1_last = add_sibling(last, gp_last, r1_last)

    dmod = jnp.concatenate([dshift1, dscale1, dgate1, dshift2, dscale2, dgate2, g_pre1_grad], axis=1)
    dmod_all = _all_gather(jnp.pad(dmod, ((0, SUBLANES - 1), (0, 0))), "gather_dmod")
    dmod_all = dmod_all.reshape(N_DEV, SUBLANES, (N_MOD + 1) * d)[:, 0]
    dmod_sum = _sum_leading(dmod_all.reshape(N_DEV, 1, (N_MOD + 1) * d), "sum_dmod")[0]
    grads["b_ada"], grads["pre_norm1_g"] = dmod_sum[:N_MOD * d], dmod_sum[N_MOD * d:]
    dmod_mine = lax.dynamic_slice(dmod_all, (0, chip * na), (N_DEV, na))
    grads["w_ada"] = _ada_bwd(c_all.T, dmod_mine, "ada_bwd")

    delta, new_m, new_v = {}, {}, {}

    def adamw(n, after=None):
        turn = (lambda a: a.T) if n == "w_in" else (lambda a: a)
        outs = _adamw(turn(given[n]), grads[n], turn(given["m_" + n]), turn(given["v_" + n]), "adamw_" + n,
                      after=after, emit_grad=n in BIG)
        g_out = outs[0] if n in BIG else grads[n]
        grads[n], delta[n], new_m[n], new_v[n] = (turn(o) for o in (g_out, *outs[-3:]))

    exchange_last = _exchange_comm(s1_last)
    in_flight, token = _comm_split_start(exchange_last, "rs_exchange_last_start", after=[dmod_sum, small_sum])
    for n in ["w_ada", "w_down"] + mid:
        adamw(n, after=token)
    s1_last, r2_last = _comm_split_wait(exchange_last, in_flight, delta[mid[-1]], "rs_exchange_last_wait")
    half_last = add_chips(last, s1_last, r2_last)
    grads.update(zip(last, _run_comm(_share_comm(half_last, by_cols=[0]), "rs_share_last")))
    for n in last:
        adamw(n)

    outs = _adamw(small_state[0], small_pack("", grads), small_state[1], small_state[2], "adamw_small")
    off = 0
    for n in SMALL:
        size = given[n].size
        for store, packed_out in zip((delta, new_m, new_v), outs):
            store[n] = packed_out.reshape(-1)[off:off + size].reshape(given[n].shape)
        off += size

    return (loss, grad_x[None], *[grads[n] for n in WEIGHTS], *[delta[n] for n in WEIGHTS],
            *[new_m[n] for n in WEIGHTS], *[new_v[n] for n in WEIGHTS])
```

```python
import functools

import jax
import jax.numpy as jnp
from jax import lax
from jax.experimental import pallas as pl
from jax.experimental.pallas import tpu as pltpu

F32 = jnp.float32
BF16 = jnp.bfloat16
MESH = pl.DeviceIdType.MESH
HBM = pltpu.HBM

EPS = 1e-6
NOPE, ROPE, VHEAD = 128, 64, 128
HEAD_W = NOPE + 2 * ROPE
ROPE_THETA = 10000.0
CONV_TAPS = 3
N_MOD = 6
N_CHIPS, N_CORES, N_DEV = 4, 2, 8
ADAM_LR, ADAM_B1, ADAM_B2, ADAM_EPS, ADAM_WD, ADAM_STEP = 0.001, 0.9, 0.999, 1e-08, 0.01, 10

LANES = 128
SUBLANES = 8
VMEM_LIMIT = 56 * 2**20
MIDDLE_STAGE_AT = 70
SMALL_ROW_TILE = 256
ADAMW_BLOCK_ELEMS = 768 * 1024

BIG = ("w_in", "w_branch_a", "w_uq", "w_ukv", "w_branch_b", "w_out", "w_up", "w_down")
WEIGHTS = ("w_ada", "b_ada", "pre_norm1_g", "w_in", "gm_ln_g", "gm_ln_b", "gm_w_s", "gm_b_s", "w_branch_a",
           "q_norm_g", "w_uq", "kv_norm_g", "w_ukv", "w_branch_b", "w_out", "post_norm1_g", "pre_norm2_g",
           "w_up", "conv_w", "conv_b", "w_down", "post_norm2_g")
SMALL_PARTIAL = ("gm_ln_g", "gm_ln_b", "gm_w_s", "gm_b_s", "q_norm_g", "kv_norm_g", "post_norm1_g",
                 "pre_norm2_g", "conv_w", "conv_b", "post_norm2_g")
SMALL = ("b_ada", "pre_norm1_g") + SMALL_PARTIAL


def _div_tile(n, cap, mult=LANES):
    t = (min(cap, n) // mult) * mult
    while t >= mult:
        if n % t == 0:
            return t
        t -= mult
    return n


def _params(**kw):
    return pltpu.CompilerParams(vmem_limit_bytes=VMEM_LIMIT, **kw)


def _row_spec(width):
    return pl.BlockSpec((1, width), lambda *_: (0, 0))


def _gelu(x):
    k = 0.7978845608028654
    return 0.5 * x * (1.0 + jnp.tanh(k * (x + 0.044715 * x * x * x)))


def _gelu_grad(x):
    k = 0.7978845608028654
    t = jnp.tanh(k * (x + 0.044715 * x * x * x))
    return 0.5 * (1.0 + t) + 0.5 * x * (1.0 - t * t) * k * (1.0 + 3.0 * 0.044715 * x * x)


def _sigmoid(x):
    return 0.5 * jnp.tanh(0.5 * x) + 0.5


def _dot(a, b, dims):
    return lax.dot_general(a, b, (dims, ((), ())), preferred_element_type=F32)


NN = ((1,), (0,))
NT = ((1,), (1,))
TN = ((0,), (0,))


def _logical(arr):
    if arr.ndim == 2:
        return arr.shape[0], arr.shape[1], arr.shape[1]
    return arr.shape[1], arr.shape[0] * arr.shape[2], arr.shape[2]


def _tile_spec(ndim, group_w, blk_rows, blk_cols, row_of, col_of):
    if ndim == 2:
        return pl.BlockSpec((blk_rows, blk_cols), lambda i, j, k: (row_of(i, j, k), col_of(i, j, k)))
    per = group_w // blk_cols
    return pl.BlockSpec((None, blk_rows, blk_cols),
                        lambda i, j, k: (col_of(i, j, k) // per, row_of(i, j, k), col_of(i, j, k) % per))


def _matmul(a, b, *, mode, out_dtype, name, tm=512, tn=512, tk=2048, mul=None, add=None, out_groups=None, comm=None):
    ar, ac, agw = _logical(a)
    br, bc, bgw = _logical(b)
    if mode == "nn":
        m, kd, n = ar, ac, bc
        m_w, k_w, n_w = (), (agw,), (bgw,)
    elif mode == "nt":
        m, kd, n = ar, ac, br
        m_w, k_w, n_w = (), (agw, bgw), ()
    else:
        m, kd, n = ac, ar, bc
        m_w, k_w, n_w = (agw,), (), (bgw,)
    if out_groups is not None:
        n_w = n_w + (n // out_groups,)
    tm = _div_tile(min((m,) + m_w), tm, LANES if mode == "tn" else SUBLANES)
    tn = _div_tile(min((n,) + n_w), tn)
    tk = _div_tile(min((kd,) + k_w), tk)
    assert all(w % tn == 0 for w in n_w) and all(w % tk == 0 for w in k_w) and all(w % tm == 0 for w in m_w)
    nk = kd // tk
    dims = {"nn": NN, "nt": NT, "tn": TN}[mode]
    gi, gj, gk = (lambda i, j, k: i), (lambda i, j, k: j), (lambda i, j, k: k)
    if mode == "nn":
        a_spec = _tile_spec(a.ndim, agw, tm, tk, gi, gk)
        b_spec = _tile_spec(b.ndim, bgw, tk, tn, gk, gj)
    elif mode == "nt":
        a_spec = _tile_spec(a.ndim, agw, tm, tk, gi, gk)
        b_spec = _tile_spec(b.ndim, bgw, tn, tk, gj, gk)
    else:
        a_spec = _tile_spec(a.ndim, agw, tk, tm, gk, gi)
        b_spec = _tile_spec(b.ndim, bgw, tk, tn, gk, gj)
    in_specs, operands = [a_spec, b_spec], [a, b]
    if mul is not None:
        assert mul.shape == (m, tn)
        in_specs.append(pl.BlockSpec((tm, tn), lambda i, j, k: (i, 0)))
        operands.append(mul)
    if add is not None:
        in_specs.append(pl.BlockSpec((tm, tn), lambda i, j, k: (i, j)))
        operands.append(add)

    def body(*refs):
        a_ref, b_ref = refs[0], refs[1]
        pos = 2
        mul_ref = add_ref = None
        if mul is not None:
            mul_ref, pos = refs[pos], pos + 1
        if add is not None:
            add_ref, pos = refs[pos], pos + 1
        o_ref = refs[pos]

        def finish(r):
            if mul_ref is not None:
                r = r * mul_ref[...]
            if add_ref is not None:
                r = r + add_ref[...]
            o_ref[...] = r.astype(out_dtype)

        part = _dot(a_ref[...], b_ref[...], dims)
        if nk == 1:
            finish(part)
        else:
            acc_ref = refs[pos + 1]
            k = pl.program_id(2)

            @pl.when(k == 0)
            def _():
                acc_ref[...] = part

            @pl.when(k > 0)
            def _():
                acc_ref[...] += part

            @pl.when(k == nk - 1)
            def _():
                finish(acc_ref[...])

    if out_groups is None:
        out_spec, out_dims = _tile_spec(2, n, tm, tn, gi, gj), (m, n)
    else:
        out_spec, out_dims = _tile_spec(3, n // out_groups, tm, tn, gi, gj), (out_groups, m, n // out_groups)
    return _call(body, operands, comm, name=name, grid=(m // tm, n // tn, nk), in_specs=in_specs, out_specs=out_spec,
                 out_shape=jax.ShapeDtypeStruct(out_dims, out_dtype),
                 scratch_shapes=[] if nk == 1 else [pltpu.VMEM((tm, tn), F32)])


def _accumulate(ref, value):
    @pl.when(pl.program_id(0) == 0)
    def _():
        ref[...] = value

    @pl.when(pl.program_id(0) > 0)
    def _():
        ref[...] += value


def _colsum(v):
    return jnp.sum(v, axis=0, keepdims=True)


def _rowmean(v):
    return jnp.mean(v, axis=-1, keepdims=True)


def _prenorm(x, g, scale, shift, name):
    s, d = x.shape
    tb = _div_tile(s, 256, SUBLANES)

    def body(x_ref, g_ref, sc_ref, sh_ref, h_ref):
        xv = x_ref[...]
        r = lax.rsqrt(_rowmean(xv * xv) + EPS)
        h_ref[...] = ((xv * r) * g_ref[...] * (1.0 + sc_ref[...]) + sh_ref[...]).astype(BF16)

    blk = pl.BlockSpec((tb, d), lambda i: (i, 0))
    return pl.pallas_call(
        body, name=name, grid=(s // tb,), in_specs=[blk, _row_spec(d), _row_spec(d), _row_spec(d)],
        out_specs=blk, out_shape=jax.ShapeDtypeStruct((s, d), BF16), compiler_params=_params(),
    )(x, g, scale, shift)


def _post_pre(x, y, gate, pg, g2, scale2, shift2, name):
    s, d = x.shape
    tb = _div_tile(s, 256, SUBLANES)

    def body(x_ref, y_ref, gate_ref, pg_ref, g2_ref, sc_ref, sh_ref, x1_ref, h2_ref):
        yv = y_ref[...]
        rp = lax.rsqrt(_rowmean(yv * yv) + EPS)
        x1 = x_ref[...] + gate_ref[...] * ((yv * rp) * pg_ref[...])
        x1_ref[...] = x1
        r2 = lax.rsqrt(_rowmean(x1 * x1) + EPS)
        h2_ref[...] = ((x1 * r2) * g2_ref[...] * (1.0 + sc_ref[...]) + sh_ref[...]).astype(BF16)

    blk = pl.BlockSpec((tb, d), lambda i: (i, 0))
    return pl.pallas_call(
        body, name=name, grid=(s // tb,), in_specs=[blk, blk] + [_row_spec(d)] * 5,
        out_specs=[blk, blk],
        out_shape=[jax.ShapeDtypeStruct((s, d), F32), jax.ShapeDtypeStruct((s, d), BF16)],
        compiler_params=_params(),
    )(x, y, gate, pg, g2, scale2, shift2)


def _post_bwd(y, gate, pg, name, *, dxo=None, xin=None, target=None):
    s, d = y.shape
    tb = _div_tile(s, 256, SUBLANES)
    from_loss = target is not None

    def body(*refs):
        if from_loss:
            y_ref, gate_ref, pg_ref, xin_ref, t_ref, dy_ref, dgate_ref, dpg_ref, dxo_ref, loss_ref = refs
        else:
            y_ref, gate_ref, pg_ref, dxo_in_ref, dy_ref, dgate_ref, dpg_ref = refs
        yv = y_ref[...]
        rp = lax.rsqrt(_rowmean(yv * yv) + EPS)
        yh = yv * rp
        fn = yh * pg_ref[...]
        gate = gate_ref[...]
        if from_loss:
            err = xin_ref[...] + gate * fn - t_ref[...]
            dxo = err * (1.0 / d)
            dxo_ref[...] = dxo
            part = 0.5 * jnp.sum(_rowmean(err * err), axis=0, keepdims=True)
            _accumulate(loss_ref, jnp.broadcast_to(part, loss_ref.shape))
        else:
            dxo = dxo_in_ref[...]
        _accumulate(dgate_ref, _colsum(dxo * fn))
        dfn = dxo * gate
        _accumulate(dpg_ref, _colsum(dfn * yh))
        dyh = dfn * pg_ref[...]
        dy_ref[...] = (rp * (dyh - yh * _rowmean(dyh * yh))).astype(BF16)

    blk = pl.BlockSpec((tb, d), lambda i: (i, 0))
    in_specs = [blk, _row_spec(d), _row_spec(d)]
    out_specs = [blk, _row_spec(d), _row_spec(d)]
    out_shape = [jax.ShapeDtypeStruct((s, d), BF16), jax.ShapeDtypeStruct((1, d), F32),
                 jax.ShapeDtypeStruct((1, d), F32)]
    if from_loss:
        operands = (y, gate, pg, xin, target)
        in_specs += [blk, blk]
        out_specs += [blk, _row_spec(LANES)]
        out_shape += [jax.ShapeDtypeStruct((s, d), F32), jax.ShapeDtypeStruct((1, LANES), F32)]
    else:
        operands = (y, gate, pg, dxo)
        in_specs += [blk]
    return pl.pallas_call(
        body, name=name, grid=(s // tb,), in_specs=in_specs, out_specs=out_specs, out_shape=out_shape,
        compiler_params=_params(),
    )(*operands)


def _prenorm_bwd(xin, dh, dres, g, scale, name, comm=None):
    s, d = xin.shape
    tb = _div_tile(s, 256, SUBLANES)

    def body(x_ref, dh_ref, dres_ref, g_ref, sc_ref, dx_ref, dshift_ref, dscale_ref, dg_ref):
        xv = x_ref[...]
        r = lax.rsqrt(_rowmean(xv * xv) + EPS)
        xn = xv * r
        dh = dh_ref[...]
        g1 = g_ref[...]
        s1 = 1.0 + sc_ref[...]
        _accumulate(dshift_ref, _colsum(dh))
        _accumulate(dscale_ref, _colsum(dh * xn * g1))
        _accumulate(dg_ref, _colsum(dh * xn * s1))
        dxn = dh * g1 * s1
        dx_ref[...] = dres_ref[...] + r * (dxn - xn * _rowmean(dxn * xn))

    blk = pl.BlockSpec((tb, d), lambda i: (i, 0))
    return _call(
        body, (xin, dh, dres, g, scale), comm, name=name, grid=(s // tb,),
        in_specs=[blk, blk, blk, _row_spec(d), _row_spec(d)],
        out_specs=[blk, _row_spec(d), _row_spec(d), _row_spec(d)],
        out_shape=[jax.ShapeDtypeStruct((s, d), F32)] + [jax.ShapeDtypeStruct((1, d), F32)] * 3)


def _merge(z_big, y_a, y_b, name):
    s, d = y_a.shape
    tb = _div_tile(s, 256, SUBLANES)

    def body(zg_ref, ya_ref, yb_ref, o_ref):
        ga, gb = zg_ref[:, :d].astype(F32), zg_ref[:, d:].astype(F32)
        o_ref[...] = (_sigmoid(ga) * ya_ref[...].astype(F32) + _sigmoid(gb) * yb_ref[...].astype(F32)).astype(BF16)

    blk = pl.BlockSpec((tb, d), lambda i: (i, 0))
    return pl.pallas_call(
        body, name=name, grid=(s // tb,), in_specs=[pl.BlockSpec((tb, 2 * d), lambda i: (i, 1)), blk, blk],
        out_specs=blk, out_shape=jax.ShapeDtypeStruct((s, d), BF16), compiler_params=_params(),
    )(z_big, y_a, y_b)


def _merge_bwd(dmerged, z_big, y_a, y_b, name):
    s, d = y_a.shape
    tb = _div_tile(s, 256, SUBLANES)

    def body(dm_ref, zg_ref, ya_ref, yb_ref, dya_ref, dyb_ref, dz_ref):
        dm = dm_ref[...].astype(F32)
        sa, sb = _sigmoid(zg_ref[:, :d].astype(F32)), _sigmoid(zg_ref[:, d:].astype(F32))
        dya_ref[...] = (dm * sa).astype(BF16)
        dyb_ref[...] = (dm * sb).astype(BF16)
        dz_ref[:, :d] = (dm * ya_ref[...].astype(F32) * sa * (1.0 - sa)).astype(BF16)
        dz_ref[:, d:] = (dm * yb_ref[...].astype(F32) * sb * (1.0 - sb)).astype(BF16)

    blk = pl.BlockSpec((tb, d), lambda i: (i, 0))
    wide = pl.BlockSpec((tb, 2 * d), lambda i: (i, 1))
    return pl.pallas_call(
        body, name=name, grid=(s // tb,), in_specs=[blk, wide, blk, blk], out_specs=[blk, blk, wide],
        out_shape=[jax.ShapeDtypeStruct((s, d), BF16), jax.ShapeDtypeStruct((s, d), BF16),
                   jax.ShapeDtypeStruct((s, 4 * d), BF16)],
        compiler_params=_params(),
    )(dmerged, z_big, y_a, y_b)


def _causal_mask(ch):
    q = lax.broadcasted_iota(jnp.int32, (ch, ch), 0)
    p = lax.broadcasted_iota(jnp.int32, (ch, ch), 1)
    return (p <= q).astype(F32)


def _gmlp_norm(zc, lng, lnb, gw):
    u_pre, v_pre = zc[:, :gw], zc[:, gw:]
    vg = _gelu(v_pre)
    mu = _rowmean(vg)
    cen = vg - mu
    rstd = lax.rsqrt(_rowmean(cen * cen) + EPS)
    vhat = cen * rstd
    return u_pre, v_pre, _gelu(u_pre), vhat, rstd, vhat * lng + lnb


def _gmlp_fwd(z_big, ln_g, ln_b, w_s, b_s_t, name):
    s = z_big.shape[0]
    groups, ch, _ = w_s.shape
    gw = ln_g.shape[1]
    gd = gw // groups

    def body(z_ref, lng_ref, lnb_ref, ws_ref, bt_ref, a_ref):
        _, _, u, _, _, vn = _gmlp_norm(z_ref[...].astype(F32), lng_ref[...], lnb_ref[...], gw)
        mask = _causal_mask(ch)
        for g in range(groups):
            cols = slice(g * gd, (g + 1) * gd)
            wm = (ws_ref[g] * mask).astype(BF16)
            mixed = _dot(wm, vn[:, cols].astype(BF16), NN) + bt_ref[:, g:g + 1]
            a_ref[:, cols] = (u[:, cols] * mixed).astype(BF16)

    return pl.pallas_call(
        body, name=name, grid=(s // ch,),
        in_specs=[pl.BlockSpec((ch, 2 * gw), lambda n: (n, 0)), _row_spec(gw), _row_spec(gw),
                  pl.BlockSpec((groups, ch, ch), lambda n: (0, 0, 0)), pl.BlockSpec((ch, groups), lambda n: (0, 0))],
        out_specs=pl.BlockSpec((ch, gw), lambda n: (n, 0)),
        out_shape=jax.ShapeDtypeStruct((s, gw), BF16), compiler_params=_params(),
    )(z_big, ln_g, ln_b, w_s, b_s_t)


def _gmlp_bwd(z_big, da, dz_big, ln_g, ln_b, w_s, b_s_t, name, comm=None):
    s = z_big.shape[0]
    groups, ch, _ = w_s.shape
    gw = ln_g.shape[1]
    gd = gw // groups

    def body(z_ref, da_ref, dzin_ref, lng_ref, lnb_ref, ws_ref, bt_ref, dz_ref, gws_ref, gbt_ref, glng_ref, glnb_ref,
             vg_ref, dvh_ref):
        del dzin_ref
        mask = _causal_mask(ch)
        lane = lax.broadcasted_iota(jnp.int32, (ch, LANES), 1)
        group_cols = [slice(g * gd, (g + 1) * gd) for g in range(groups)]
        rowsum = lambda v: jnp.sum(v, axis=1, keepdims=True)

        @pl.when(pl.program_id(0) == 0)
        def _():
            for ref in (gws_ref, gbt_ref, glng_ref, glnb_ref):
                ref[...] = jnp.zeros(ref.shape, F32)

        total = jnp.zeros((ch, 1), F32)
        for cols in group_cols:
            vg = _gelu(z_ref[:, gw + cols.start:gw + cols.stop].astype(F32))
            vg_ref[:, cols] = vg
            total = total + rowsum(vg)
        mu = total * (1.0 / gw)
        total = jnp.zeros((ch, 1), F32)
        for cols in group_cols:
            cen = vg_ref[:, cols] - mu
            total = total + rowsum(cen * cen)
        rstd = lax.rsqrt(total * (1.0 / gw) + EPS)
        m1, m2, gb = jnp.zeros((ch, 1), F32), jnp.zeros((ch, 1), F32), jnp.zeros((ch, LANES), F32)
        for g, cols in enumerate(group_cols):
            vhat = (vg_ref[:, cols] - mu) * rstd
            vn_g = (vhat * lng_ref[:, cols] + lnb_ref[:, cols]).astype(BF16)
            wm = (ws_ref[g] * mask).astype(BF16)
            mixed = _dot(wm, vn_g, NN) + bt_ref[:, g:g + 1]
            u_pre, da_g = z_ref[:, cols].astype(F32), da_ref[:, cols].astype(F32)
            dz_ref[:, cols] = (da_g * mixed * _gelu_grad(u_pre)).astype(BF16)
            dmixed = da_g * _gelu(u_pre)
            dm16 = dmixed.astype(BF16)
            dvn = _dot(wm, dm16, TN)
            gws_ref[g] += _dot(dm16, vn_g, NT) * mask
            gb = gb + jnp.where(lane == g, rowsum(dmixed), 0.0)
            glnb_ref[:, cols] += _colsum(dvn)
            glng_ref[:, cols] += _colsum(dvn * vhat)
            dvh = dvn * lng_ref[:, cols]
            dvh_ref[:, cols] = dvh
            m1, m2 = m1 + rowsum(dvh), m2 + rowsum(dvh * vhat)
        gbt_ref[...] += gb
        m1, m2 = m1 * (1.0 / gw), m2 * (1.0 / gw)
        for cols in group_cols:
            vhat = (vg_ref[:, cols] - mu) * rstd
            dvg = rstd * (dvh_ref[:, cols] - m1 - vhat * m2)
            v_pre = z_ref[:, gw + cols.start:gw + cols.stop].astype(F32)
            dz_ref[:, gw + cols.start:gw + cols.stop] = (dvg * _gelu_grad(v_pre)).astype(BF16)

    zspec = pl.BlockSpec((ch, 2 * gw), lambda n: (n, 0))
    return _call(
        body, (z_big, da, dz_big, ln_g, ln_b, w_s, b_s_t), comm, name=name, grid=(s // ch,),
        in_specs=[zspec, pl.BlockSpec((ch, gw), lambda n: (n, 0)), pl.BlockSpec(memory_space=HBM),
                  _row_spec(gw), _row_spec(gw), pl.BlockSpec((groups, ch, ch), lambda n: (0, 0, 0)),
                  pl.BlockSpec((ch, groups), lambda n: (0, 0))],
        out_specs=[zspec, pl.BlockSpec((groups, ch, ch), lambda n: (0, 0, 0)),
                   pl.BlockSpec((ch, LANES), lambda n: (0, 0)), _row_spec(gw), _row_spec(gw)],
        out_shape=[jax.ShapeDtypeStruct(dz_big.shape, BF16), jax.ShapeDtypeStruct((groups, ch, ch), F32),
                   jax.ShapeDtypeStruct((ch, LANES), F32), jax.ShapeDtypeStruct((1, gw), F32),
                   jax.ShapeDtypeStruct((1, gw), F32)],
        scratch_shapes=[pltpu.VMEM((ch, gw), F32)] * 2, input_output_aliases={2: 0})


def _mla_prep(z_lat, q_g, kv_g, rope_k, name):
    s, latw = z_lat.shape
    ql, kvl = q_g.shape[1], kv_g.shape[1]
    tb = _div_tile(s, 256, SUBLANES)

    def body(z_ref, qg_ref, kvg_ref, t_ref, qn_ref, kvn_ref, kr_ref):
        q = z_ref[:, :ql]
        qn_ref[...] = ((q * lax.rsqrt(_rowmean(q * q) + EPS)) * qg_ref[...]).astype(BF16)
        kv = z_ref[:, ql:ql + kvl]
        kvn_ref[...] = ((kv * lax.rsqrt(_rowmean(kv * kv) + EPS)) * kvg_ref[...]).astype(BF16)
        kk = z_ref[:, ql + kvl:] * t_ref[...]
        kr_ref[...] = (kk + pltpu.roll(kk, ROPE, axis=1)).astype(BF16)

    return pl.pallas_call(
        body, name=name, grid=(s // tb,),
        in_specs=[pl.BlockSpec((tb, latw), lambda i: (i, 0)), _row_spec(ql), _row_spec(kvl),
                  pl.BlockSpec((tb, 2 * ROPE), lambda i: (i, 0))],
        out_specs=[pl.BlockSpec((tb, ql), lambda i: (i, 0)), pl.BlockSpec((tb, kvl), lambda i: (i, 0)),
                   pl.BlockSpec((tb, 2 * ROPE), lambda i: (i, 0))],
        out_shape=[jax.ShapeDtypeStruct((s, ql), BF16), jax.ShapeDtypeStruct((s, kvl), BF16),
                   jax.ShapeDtypeStruct((s, 2 * ROPE), BF16)],
        compiler_params=_params(),
    )(z_lat, q_g, kv_g, rope_k)


def _attn_fwd(q, kv, kr, heads, name, comm=None):
    s = q.shape[0]
    t = _div_tile(s, 512)
    nb = s // t
    hp = 2 if heads % 2 == 0 else 1

    def body(q_ref, k_ref, kr_ref, v_ref, o_ref, lse_ref, m_ref, l_ref, acc_ref):
        i, j = pl.program_id(1), pl.program_id(2)

        @pl.when(j == 0)
        def _():
            m_ref[...] = jnp.full(m_ref.shape, -1e30, F32)
            l_ref[...] = jnp.zeros(l_ref.shape, F32)
            acc_ref[...] = jnp.zeros(acc_ref.shape, F32)

        def update(h, rows, n_keys, on_diagonal):
            vc = slice(h * VHEAD, (h + 1) * VHEAD)
            k_full = jnp.concatenate([k_ref[:n_keys, h * NOPE:(h + 1) * NOPE], kr_ref[:n_keys, :]], axis=1)
            sc = _dot(q_ref[rows, h * HEAD_W:(h + 1) * HEAD_W], k_full, NT)
            if on_diagonal:
                row_pos = rows.start + lax.broadcasted_iota(jnp.int32, sc.shape, 0)
                sc = jnp.where(lax.broadcasted_iota(jnp.int32, sc.shape, 1) <= row_pos, sc, -1e30)
            m_old = m_ref[h, rows, :]
            m_new = jnp.maximum(m_old, jnp.max(sc, axis=-1, keepdims=True))
            p = jnp.exp(sc - m_new)
            alpha = jnp.exp(m_old - m_new)
            l_new = alpha * l_ref[h, rows, :] + jnp.sum(p, axis=-1, keepdims=True)
            acc = alpha * acc_ref[rows, vc] + _dot(p.astype(BF16), v_ref[:n_keys, vc], NN)
            if on_diagonal:
                o_ref[rows, vc] = (acc / l_new).astype(BF16)
                lse_ref[h, rows, :] = jnp.broadcast_to(m_new + jnp.log(l_new), (rows.stop - rows.start, LANES))
            else:
                m_ref[h, rows, :], l_ref[h, rows, :], acc_ref[rows, vc] = m_new, l_new, acc

        def below_diagonal():
            for h in range(hp):
                update(h, slice(0, t), t, False)

        def on_diagonal():
            for h in range(hp):
                update(h, slice(0, t // 2), t // 2, True)
                update(h, slice(t // 2, t), t, True)

        pl.when(j < i)(below_diagonal)
        pl.when(j == i)(on_diagonal)

    kidx = lambda off: (lambda h, i, j: (jnp.minimum(i, j), off(h)))
    return _call(
        body, (q, kv, kr, kv), comm, name=name, grid=(heads // hp, nb, nb),
        in_specs=[pl.BlockSpec((t, hp * HEAD_W), lambda h, i, j: (i, h)),
                  pl.BlockSpec((t, hp * NOPE), kidx(lambda h: h)),
                  pl.BlockSpec((t, 2 * ROPE), kidx(lambda h: 0)),
                  pl.BlockSpec((t, hp * VHEAD), kidx(lambda h: heads // hp + h))],
        out_specs=[pl.BlockSpec((t, hp * VHEAD), lambda h, i, j: (i, h)),
                   pl.BlockSpec((hp, t, LANES), lambda h, i, j: (h, i, 0))],
        out_shape=[jax.ShapeDtypeStruct((s, heads * VHEAD), BF16), jax.ShapeDtypeStruct((heads, s, LANES), F32)],
        scratch_shapes=[pltpu.VMEM((hp, t, 1), F32), pltpu.VMEM((hp, t, 1), F32), pltpu.VMEM((t, hp * VHEAD), F32)])


def _attn_bwd(q, kv, kr, o, do, lse, rope_q, heads, name, comm=None):
    s = q.shape[0]
    t = _div_tile(s, 512)
    nb = s // t
    hp = 2 if heads % 2 == 0 else 1

    def body(q_ref, k_ref, kr_ref, v_ref, o_ref, do_ref, lse_ref, tq_ref, dqb_ref, dk_ref, dv_ref, dk_acc, dv_acc,
             dq_ref):
        j, i = pl.program_id(1), pl.program_id(2)

        @pl.when(jnp.logical_and(j == 0, i == 0))
        def _():
            dq_ref[...] = jnp.zeros(dq_ref.shape, F32)

        def update(h, rows, n_keys, on_diagonal, assign):
            qc, kc, vc = (slice(h * w, (h + 1) * w) for w in (HEAD_W, NOPE, VHEAD))
            n_rows = rows.stop - rows.start
            qv, do_v = q_ref[rows, qc], do_ref[rows, vc]
            k_full = jnp.concatenate([k_ref[:n_keys, kc], kr_ref[:n_keys, :]], axis=1)
            sc = _dot(qv, k_full, NT)
            if on_diagonal:
                row_pos = rows.start + lax.broadcasted_iota(jnp.int32, sc.shape, 0)
                sc = jnp.where(lax.broadcasted_iota(jnp.int32, sc.shape, 1) <= row_pos, sc, -1e30)
            p = jnp.exp(sc - lse_ref[h, rows, :1])
            dp = _dot(do_v, v_ref[:n_keys, vc], NT)
            delta = jnp.sum(do_v.astype(F32) * o_ref[rows, vc].astype(F32), axis=-1, keepdims=True)
            ds = (p * (dp - delta)).astype(BF16)
            dq_ref[pl.ds(pl.multiple_of(i * t + rows.start, n_rows), n_rows), qc] += _dot(ds, k_full, NN)
            dv_part, dk_part = _dot(p.astype(BF16), do_v, TN), _dot(ds, qv, TN)
            if assign:
                dv_acc[:n_keys, vc], dk_acc[:n_keys, qc] = dv_part, dk_part
            else:
                dv_acc[:n_keys, vc] += dv_part
                dk_acc[:n_keys, qc] += dk_part

        def on_diagonal():
            for h in range(hp):
                update(h, slice(t // 2, t), t, True, True)
                update(h, slice(0, t // 2), t // 2, True, False)

        def below_diagonal():
            for h in range(hp):
                update(h, slice(0, t), t, False, False)

        pl.when(i == j)(on_diagonal)
        pl.when(i > j)(below_diagonal)

        @pl.when(i == nb - 1)
        def _():
            dk_ref[...] = dk_acc[...].astype(BF16)
            dv_ref[...] = dv_acc[...].astype(BF16)

        @pl.when(jnp.logical_and(j == nb - 1, i == nb - 1))
        def _():
            for h in range(hp):
                qc = slice(h * HEAD_W, (h + 1) * HEAD_W)
                dqb_ref[:, qc] = (dq_ref[:, qc] * tq_ref[...]).astype(BF16)

    qidx = lambda h, j, i: (jnp.maximum(i, j), h)
    return _call(
        body, (q, kv, kr, kv, o, do, lse, rope_q), comm, name=name, grid=(heads // hp, nb, nb),
        in_specs=[pl.BlockSpec((t, hp * HEAD_W), qidx),
                  pl.BlockSpec((t, hp * NOPE), lambda h, j, i: (j, h)),
                  pl.BlockSpec((t, 2 * ROPE), lambda h, j, i: (j, 0)),
                  pl.BlockSpec((t, hp * VHEAD), lambda h, j, i: (j, heads // hp + h)),
                  pl.BlockSpec((t, hp * VHEAD), qidx), pl.BlockSpec((t, hp * VHEAD), qidx),
                  pl.BlockSpec((hp, t, LANES), lambda h, j, i: (h, jnp.maximum(i, j), 0)),
                  pl.BlockSpec((s, HEAD_W), lambda h, j, i: (0, 0))],
        out_specs=[pl.BlockSpec((s, hp * HEAD_W), lambda h, j, i: (0, h)),
                   pl.BlockSpec((t, hp * HEAD_W), lambda h, j, i: (j, h)),
                   pl.BlockSpec((t, hp * VHEAD), lambda h, j, i: (j, h))],
        out_shape=[jax.ShapeDtypeStruct((s, heads * HEAD_W), BF16), jax.ShapeDtypeStruct((s, heads * HEAD_W), BF16),
                   jax.ShapeDtypeStruct((s, heads * VHEAD), BF16)],
        scratch_shapes=[pltpu.VMEM((t, hp * HEAD_W), F32), pltpu.VMEM((t, hp * VHEAD), F32),
                        pltpu.VMEM((s, hp * HEAD_W), F32)])


def _mla_bwd_mid(dk, dv, rope_k, heads, name):
    s = dk.shape[0]
    tb = _div_tile(s, 256, SUBLANES)

    def body(dk_ref, dv_ref, tk_ref, dkv_ref, dkk_ref):
        dkr = jnp.zeros((tb, 2 * ROPE), F32)
        for h in range(heads):
            dkv_ref[:, h * NOPE:(h + 1) * NOPE] = dk_ref[:, h * HEAD_W:h * HEAD_W + NOPE]
            dkr = dkr + dk_ref[:, h * HEAD_W + NOPE:(h + 1) * HEAD_W].astype(F32)
        dkv_ref[:, heads * NOPE:] = dv_ref[...]
        dkk_ref[...] = (dkr + pltpu.roll(dkr, ROPE, axis=1)) * tk_ref[...]

    wq, wv = heads * HEAD_W, heads * VHEAD
    return pl.pallas_call(
        body, name=name, grid=(s // tb,),
        in_specs=[pl.BlockSpec((tb, wq), lambda i: (i, 0)), pl.BlockSpec((tb, wv), lambda i: (i, 0)),
                  pl.BlockSpec((tb, 2 * ROPE), lambda i: (i, 0))],
        out_specs=[pl.BlockSpec((tb, heads * NOPE + wv), lambda i: (i, 0)),
                   pl.BlockSpec((tb, 2 * ROPE), lambda i: (i, 0))],
        out_shape=[jax.ShapeDtypeStruct((s, heads * NOPE + wv), BF16), jax.ShapeDtypeStruct((s, 2 * ROPE), F32)],
        compiler_params=_params(),
    )(dk, dv, rope_k)


def _mla_bwd_post(z_lat, dqn, dkvn, dkk, q_g, kv_g, name):
    s, latw = z_lat.shape
    ql, kvl = q_g.shape[1], kv_g.shape[1]
    tb = _div_tile(s, 256, SUBLANES)

    def norm_bwd(xv, dn, g, dg_ref):
        r = lax.rsqrt(_rowmean(xv * xv) + EPS)
        xh = xv * r
        _accumulate(dg_ref, _colsum(dn * xh))
        dxh = dn * g
        return r * (dxh - xh * _rowmean(dxh * xh))

    def body(z_ref, dqn_ref, dkvn_ref, dkk_ref, qg_ref, kvg_ref, dz_ref, gq_ref, gkv_ref):
        dz_ref[:, :ql] = norm_bwd(z_ref[:, :ql], dqn_ref[...], qg_ref[...], gq_ref).astype(BF16)
        dz_ref[:, ql:ql + kvl] = norm_bwd(z_ref[:, ql:ql + kvl], dkvn_ref[...], kvg_ref[...], gkv_ref).astype(BF16)
        dz_ref[:, ql + kvl:] = dkk_ref[...].astype(BF16)

    return pl.pallas_call(
        body, name=name, grid=(s // tb,),
        in_specs=[pl.BlockSpec((tb, latw), lambda i: (i, 0)), pl.BlockSpec((tb, ql), lambda i: (i, 0)),
                  pl.BlockSpec((tb, kvl), lambda i: (i, 0)), pl.BlockSpec((tb, 2 * ROPE), lambda i: (i, 0)),
                  _row_spec(ql), _row_spec(kvl)],
        out_specs=[pl.BlockSpec((tb, latw), lambda i: (i, 0)), _row_spec(ql), _row_spec(kvl)],
        out_shape=[jax.ShapeDtypeStruct((s, latw), BF16), jax.ShapeDtypeStruct((1, ql), F32),
                   jax.ShapeDtypeStruct((1, kvl), F32)],
        compiler_params=_params(),
    )(z_lat, dqn, dkvn, dkk, q_g, kv_g)


CONV_ROWS = 128
CONV_HALO = 16


def _row_steps(n_rows, step):
    step(0, True)
    if n_rows > CONV_ROWS:
        def later(i, carry):
            step(pl.multiple_of(i * CONV_ROWS, CONV_ROWS), False)
            return carry
        lax.fori_loop(1, n_rows // CONV_ROWS, later, 0)


def _conv_taps(pre_ref, r0, first):
    if first:
        win = jnp.concatenate([jnp.zeros((CONV_HALO, pre_ref.shape[1]), F32), pre_ref[0:CONV_ROWS, :].astype(F32)])
    else:
        win = pre_ref[pl.ds(pl.multiple_of(r0 - CONV_HALO, CONV_HALO), CONV_ROWS + CONV_HALO), :].astype(F32)
    return win[CONV_HALO:], pltpu.roll(win, 1, axis=0)[CONV_HALO:], pltpu.roll(win, 2, axis=0)[CONV_HALO:]


def _conv(taps, w_ref, b_ref):
    return w_ref[2:3, :] * taps[0] + w_ref[1:2, :] * taps[1] + w_ref[0:1, :] * taps[2] + b_ref[...]


def _conv_fwd(up_pre, conv_w, conv_b, name):
    s, ff2 = up_pre.shape
    ff = ff2 // 2
    tc = _div_tile(ff, 256)
    nb = ff // tc
    assert s % CONV_ROWS == 0

    def body(pg_ref, pv_ref, wg_ref, wv_ref, bg_ref, bv_ref, act_ref):
        def step(r0, first):
            gate = _conv(_conv_taps(pg_ref, r0, first), wg_ref, bg_ref)
            val = _conv(_conv_taps(pv_ref, r0, first), wv_ref, bv_ref)
            act_ref[pl.ds(r0, CONV_ROWS), :] = (gate * _sigmoid(gate) * val).astype(BF16)

        _row_steps(s, step)

    def col(rows, off):
        return pl.BlockSpec((rows, tc), lambda j: (0, j + off))

    return pl.pallas_call(
        body, name=name, grid=(nb,),
        in_specs=[col(s, 0), col(s, nb), col(CONV_TAPS, 0), col(CONV_TAPS, nb), col(1, 0), col(1, nb)],
        out_specs=col(s, 0), out_shape=jax.ShapeDtypeStruct((s, ff), BF16), compiler_params=_params(),
    )(up_pre, up_pre, conv_w, conv_w, conv_b, conv_b)


def _conv_bwd(up_pre, dact, conv_w, conv_b, name, comm=None):
    s, ff2 = up_pre.shape
    ff = ff2 // 2
    tc = _div_tile(ff, 256)
    nb = ff // tc
    assert s % CONV_ROWS == 0

    def body(pg_ref, pv_ref, da_ref, wg_ref, wv_ref, bg_ref, bv_ref, dup_ref, gwg_ref, gwv_ref, gbg_ref, gbv_ref,
             dxg_ref, dxv_ref):
        for ref in (gwg_ref, gwv_ref, gbg_ref, gbv_ref):
            ref[...] = jnp.zeros(ref.shape, F32)
        for ref in (dxg_ref, dxv_ref):
            ref[s:s + SUBLANES, :] = jnp.zeros((SUBLANES, tc), F32)

        def sums(taps, dx, gw_ref, gb_ref):
            gb_ref[...] += _colsum(dx)
            for k in range(CONV_TAPS):
                gw_ref[k:k + 1, :] += _colsum(dx * taps[CONV_TAPS - 1 - k])

        def forward(r0, first):
            rows = pl.ds(r0, CONV_ROWS)
            taps_g, taps_v = _conv_taps(pg_ref, r0, first), _conv_taps(pv_ref, r0, first)
            gate, val = _conv(taps_g, wg_ref, bg_ref), _conv(taps_v, wv_ref, bv_ref)
            da = da_ref[rows, :].astype(F32)
            sg = _sigmoid(gate)
            dxv, dxg = da * gate * sg, da * val * sg * (1.0 + gate * (1.0 - sg))
            dxv_ref[rows, :], dxg_ref[rows, :] = dxv, dxg
            sums(taps_v, dxv, gwv_ref, gbv_ref)
            sums(taps_g, dxg, gwg_ref, gbg_ref)

        def backward(r0, first):
            del first
            n = CONV_ROWS + SUBLANES
            for dx_ref, w_ref, out_ref in ((dxg_ref, wg_ref, dup_ref.at[0]), (dxv_ref, wv_ref, dup_ref.at[1])):
                win = dx_ref[pl.ds(r0, n), :]
                ahead1 = pltpu.roll(win, n - 1, axis=0)[:CONV_ROWS]
                ahead2 = pltpu.roll(win, n - 2, axis=0)[:CONV_ROWS]
                out_ref[pl.ds(r0, CONV_ROWS), :] = (w_ref[2:3, :] * win[:CONV_ROWS] + w_ref[1:2, :] * ahead1
                                                    + w_ref[0:1, :] * ahead2).astype(BF16)

        _row_steps(s, forward)
        _row_steps(s, backward)

    def col(rows, off):
        return pl.BlockSpec((rows, tc), lambda j: (0, j + off))

    return _call(
        body, (up_pre, up_pre, dact, conv_w, conv_w, conv_b, conv_b), comm, name=name, grid=(nb,),
        in_specs=[col(s, 0), col(s, nb), col(s, 0), col(CONV_TAPS, 0), col(CONV_TAPS, nb), col(1, 0), col(1, nb)],
        out_specs=[pl.BlockSpec((2, s, tc), lambda j: (0, 0, j)), col(CONV_TAPS, 0), col(CONV_TAPS, 0),
                   col(1, 0), col(1, 0)],
        out_shape=[jax.ShapeDtypeStruct((2, s, ff), BF16)] + [jax.ShapeDtypeStruct((CONV_TAPS, ff), F32)] * 2
        + [jax.ShapeDtypeStruct((1, ff), F32)] * 2,
        scratch_shapes=[pltpu.VMEM((s + SUBLANES, tc), F32)] * 2)


def _ada_fwd(c_all, w, b, name):
    nseq, d = c_all.shape
    na = w.shape[1]
    tn = _div_tile(na, 512)

    def body(c_ref, w_ref, b_ref, o_ref):
        cv = c_ref[...]
        sc = cv * _sigmoid(cv)
        o_ref[...] = jnp.dot(sc, w_ref[...], preferred_element_type=F32, precision=lax.Precision.HIGHEST) + b_ref[...]

    return pl.pallas_call(
        body, name=name, grid=(na // tn,),
        in_specs=[pl.BlockSpec((nseq, d), lambda j: (0, 0)), pl.BlockSpec((d, tn), lambda j: (0, j)),
                  pl.BlockSpec((1, tn), lambda j: (0, j))],
        out_specs=pl.BlockSpec((nseq, tn), lambda j: (0, j)),
        out_shape=jax.ShapeDtypeStruct((nseq, na), F32), compiler_params=_params(),
    )(c_all, w, b)


def _ada_bwd(c_all_t, dmod, name):
    d, nseq = c_all_t.shape
    na = dmod.shape[1]
    tm, tn = _div_tile(d, 512, SUBLANES), _div_tile(na, 1024)

    def body(c_ref, dm_ref, o_ref):
        cv = c_ref[...]
        o_ref[...] = jnp.dot(cv * _sigmoid(cv), dm_ref[...], preferred_element_type=F32,
                             precision=lax.Precision.HIGHEST)

    return pl.pallas_call(
        body, name=name, grid=(d // tm, na // tn),
        in_specs=[pl.BlockSpec((tm, nseq), lambda i, j: (i, 0)), pl.BlockSpec((nseq, tn), lambda i, j: (0, j))],
        out_specs=pl.BlockSpec((tm, tn), lambda i, j: (i, j)),
        out_shape=jax.ShapeDtypeStruct((d, na), F32), compiler_params=_params(),
    )(c_all_t, dmod)


def _adamw(w, g, m, v, name, comm=None, after=None, emit_grad=False):
    rows, cols = w.shape
    n_out = 4 if emit_grad else 3
    tb = _div_tile(rows, max(SUBLANES, ADAMW_BLOCK_ELEMS // cols // SUBLANES * SUBLANES), SUBLANES)
    c1 = 1.0 / (1.0 - ADAM_B1 ** ADAM_STEP)
    c2 = 1.0 / (1.0 - ADAM_B2 ** ADAM_STEP)

    def body(*refs):
        w_ref, g_ref, m_ref, v_ref = refs[:4]
        d_ref, nm_ref, nv_ref = refs[-3:]
        gv = g_ref[...]
        if emit_grad:
            refs[-4][...] = gv
        nm =ADAM_B1 * m_ref[...] + (1.0 - ADAM_B1) * gv
        nv = ADAM_B2 * v_ref[...] + (1.0 - ADAM_B2) * (gv * gv)
        nm_ref[...] = nm
        nv_ref[...] = nv
        d_ref[...] = -ADAM_LR * ((nm * c1) / (jnp.sqrt(nv * c2) + ADAM_EPS) + ADAM_WD * w_ref[...])

    blk = pl.BlockSpec((tb, cols), lambda i: (i, 0))
    operands, in_specs = (w, g, m, v), [blk] * 4
    if after is not None:
        operands, in_specs = operands + (after,), in_specs + [pl.BlockSpec(after.shape, lambda i: (0, 0))]
    return _call(body, operands, comm, name=name, grid=(rows // tb,), in_specs=in_specs, out_specs=[blk] * n_out,
                 out_shape=[jax.ShapeDtypeStruct((rows, cols), F32)] * n_out)


def _sum_leading(parts, name, after=()):
    n, rows, cols = parts.shape
    tb = _div_tile(rows, 512, SUBLANES)

    def body(p_ref, *rest):
        o_ref = rest[-1]
        acc = p_ref[0]
        for k in range(1, n):
            acc = acc + p_ref[k]
        o_ref[...] = acc

    return pl.pallas_call(
        body, name=name, grid=(rows // tb,),
        in_specs=[pl.BlockSpec((n, tb, cols), lambda i: (0, i, 0))] + [pl.BlockSpec(memory_space=pl.ANY)] * len(after),
        out_specs=pl.BlockSpec((tb, cols), lambda i: (i, 0)),
        out_shape=jax.ShapeDtypeStruct((rows, cols), F32), compiler_params=_params(),
    )(parts, *after)


def _place():
    x, y, c = lax.axis_index("x"), lax.axis_index("y"), lax.axis_index("c")
    return x, y, c, [(1 - x, y), (x, 1 - y), (1 - x, 1 - y)]


def _all_gather(block, name):
    m_per, n = block.shape

    def body(x_ref, out_ref, send_sems, recv_sems, local_sem):
        x, y, c, chips = _place()
        me, sibling = (x, y, c), (x, y, 1 - c)

        def rows(px, py, pc):
            return out_ref.at[pl.ds((4 * px + 2 * py + pc) * m_per, m_per), :]

        def copy(k, blk, to, src=None):
            return pltpu.make_async_remote_copy(
                src_ref=rows(*blk) if src is None else src, dst_ref=rows(*blk), send_sem=send_sems.at[k],
                recv_sem=recv_sems.at[k], device_id=to, device_id_type=MESH)

        mine = pltpu.make_async_copy(x_ref, rows(*me), local_sem)
        mine.start()
        first = [copy(0, me, sibling, src=x_ref)]
        first += [copy(1 + j, me, (*chip, c), src=x_ref) for j, chip in enumerate(chips)]
        for cp in first:
            cp.start()
        passed = [copy(4 + j, (*chip, c), sibling) for j, chip in enumerate(chips)]
        for j, chip in enumerate(chips):
            copy(1 + j, (*chip, c), me).wait_recv()
            passed[j].start()
        copy(0, sibling, me).wait_recv()
        for j, chip in enumerate(chips):
            copy(4 + j, (*chip, 1 - c), me).wait_recv()
        for cp in first + passed:
            cp.wait_send()
        mine.wait()

    return pl.pallas_call(
        body, name=name, out_shape=jax.ShapeDtypeStruct((N_DEV * m_per, n), block.dtype),
        in_specs=[pl.BlockSpec(memory_space=pltpu.VMEM)], out_specs=pl.BlockSpec(memory_space=pltpu.VMEM),
        scratch_shapes=[pltpu.SemaphoreType.DMA((7,)), pltpu.SemaphoreType.DMA((7,)), pltpu.SemaphoreType.DMA],
        compiler_params=_params(),
    )(block)


def _hbm_specs(n):
    return [pl.BlockSpec(memory_space=HBM)] * n


def _part(ref, by_cols, half, quarter=None, lead=None):
    extent = ref.shape[-1] if by_cols else ref.shape[-2]
    size = extent // 2 if quarter is None else extent // 4
    first = half * (extent // 2) + (0 if quarter is None else quarter * size)
    tile = LANES if by_cols else 2 * SUBLANES
    span = pl.ds(pl.multiple_of(first, tile) if size % tile == 0 else first, size)
    index = (slice(None), span) if by_cols else (span, slice(None))
    return ref.at[index] if lead is None else ref.at[(lead,) + index]


def _half_rows(ref, half, lead=None):
    return _part(ref, False, half, lead=lead)


class _Comm:
    def __init__(self, operands, out_shape, sem_dims, build, aliases=None):
        self.operands, self.out_shape, self.sem_dims = list(operands), list(out_shape), list(sem_dims)
        self.scratch = [pltpu.SemaphoreType.DMA(d) for d in sem_dims]
        self.build, self.aliases = build, dict(aliases or {})


class _SemGrid:
    def __init__(self, sems, dims):
        self.sems, self.dims, self.at = list(sems), tuple(dims), self

    def __getitem__(self, index):
        index = index if isinstance(index, tuple) else (index,)
        flat = 0
        for i, d in zip(index, self.dims):
            flat = flat * d + i
        return self.sems[flat]


def _call(body, operands, comm=None, *, name, grid, in_specs, out_specs, out_shape, scratch_shapes=(),
          input_output_aliases=None):
    aliases = dict(input_output_aliases or {})
    if comm is None:
        return pl.pallas_call(
            body, name=name, grid=grid, in_specs=in_specs, out_specs=out_specs, out_shape=out_shape,
            scratch_shapes=list(scratch_shapes), input_output_aliases=aliases, compiler_params=_params())(*operands)
    single = not isinstance(out_shape, (list, tuple))
    outs = [out_shape] if single else list(out_shape)
    ospecs = [out_specs] if single else list(out_specs)
    n_in, n_out, n_scr = len(operands), len(outs), len(scratch_shapes)
    c_in, c_out = len(comm.operands), len(comm.out_shape)
    for i, o in comm.aliases.items():
        aliases[n_in + i] = n_out + o

    def hosted(*refs):
        ins, c_ins = refs[:n_in], refs[n_in:n_in + c_in]
        o0 = n_in + c_in
        o_refs, c_outs = refs[o0:o0 + n_out], refs[o0 + n_out:o0 + n_out + c_out]
        s0 = o0 + n_out + c_out
        scr, sems = refs[s0:s0 + n_scr], refs[s0 + n_scr:]
        stages = comm.build(c_ins, c_outs, sems)
        step, n_steps = 0, 1
        for dim, size in enumerate(grid):
            step, n_steps = step * size + pl.program_id(dim), n_steps * size
        pl.when(step == 0)(stages[0])
        body(*ins, *o_refs, *scr)
        for stage in stages[1:-1]:
            pl.when(step == (n_steps * MIDDLE_STAGE_AT) // 100)(stage)
        pl.when(step == n_steps - 1)(stages[-1])

    res = pl.pallas_call(
        hosted, name=name, grid=grid, in_specs=list(in_specs) + _hbm_specs(c_in),
        out_specs=ospecs + _hbm_specs(c_out), out_shape=outs + comm.out_shape,
        scratch_shapes=list(scratch_shapes) + comm.scratch, input_output_aliases=aliases,
        compiler_params=_params())(*operands, *comm.operands)
    return (res[0] if single else res[:n_out]), res[n_out:]


def _run_comm(comm, name):
    c_in, c_out = len(comm.operands), len(comm.out_shape)

    def body(*refs):
        for stage in comm.build(refs[:c_in], refs[c_in:c_in + c_out], refs[c_in + c_out:]):
            stage()

    return pl.pallas_call(
        body, name=name, in_specs=_hbm_specs(c_in), out_specs=_hbm_specs(c_out), out_shape=comm.out_shape,
        scratch_shapes=comm.scratch, input_output_aliases=comm.aliases, compiler_params=_params())(*comm.operands)


def _join_comms(comms):
    def build(in_refs, out_refs, sems):
        staged, i, o, k = [], 0, 0, 0
        for cm in comms:
            ni, no, ns = len(cm.operands), len(cm.out_shape), len(cm.sem_dims)
            staged.append(cm.build(in_refs[i:i + ni], out_refs[o:o + no], sems[k:k + ns]))
            i, o, k = i + ni, o + no, k + ns
        def run(fns):
            def stage():
                for fn in fns:
                    fn()
            return stage

        return (run([st[0] for st in staged]), run([fn for st in staged for fn in st[1:-1]]),
                run([st[-1] for st in staged]))

    aliases, i, o = {}, 0, 0
    for cm in comms:
        aliases.update({i + a: o + b for a, b in cm.aliases.items()})
        i, o = i + len(cm.operands), o + len(cm.out_shape)
    return _Comm(sum((cm.operands for cm in comms), []), sum((cm.out_shape for cm in comms), []),
                 sum((cm.sem_dims for cm in comms), []), build, aliases)


def _gather8_comm(block):
    def build(in_refs, out_refs, sems):
        (src,), (out,), (send_sems, recv_sems) = in_refs, out_refs, sems
        x, y, c, chips = _place()
        me, sibling = (x, y, c), (x, y, 1 - c)

        def copy(k, blk, to, own=False):
            dst = out.at[4 * blk[0] + 2 * blk[1] + blk[2]]
            return pltpu.make_async_remote_copy(
                src_ref=src if own else dst, dst_ref=dst, send_sem=send_sems.at[k], recv_sem=recv_sems.at[k],
                device_id=to, device_id_type=MESH)

        first = [copy(0, me, sibling, own=True)] + [copy(1 + j, me, (*chip, c), own=True)
                                                     for j, chip in enumerate(chips)]
        passed = [copy(4 + j, (*chip, c), sibling) for j, chip in enumerate(chips)]

        def start():
            for cp in first:
                cp.start()

        def middle():
            for j, chip in enumerate(chips):
                copy(1 + j, (*chip, c), me).wait_recv()
                passed[j].start()

        def finish():
            copy(0, sibling, me).wait_recv()
            for j, chip in enumerate(chips):
                copy(4 + j, (*chip, 1 - c), me).wait_recv()
            for cp in first + passed:
                cp.wait_send()

        return start, middle, finish

    return _Comm([block], [jax.ShapeDtypeStruct((N_DEV,) + block.shape, block.dtype)], [(7,), (7,)], build)


def _gather_comm(shards, by_cols=()):
    nw = len(shards)

    def build(in_refs, out_refs, sems):
        send_sems, recv_sems = sems
        x, y, c, chips = _place()
        me, sibling = (x, y, c), (x, y, 1 - c)
        across_x, across_y, diagonal = chips

        def copy(w, k, block, part, to, src=None):
            dst = _part(out_refs[w], w in by_cols, part[1], part[2] if part[0] else None, 2 * block[0] + block[1])
            return pltpu.make_async_remote_copy(
                src_ref=dst if src is None else src, dst_ref=dst, send_sem=send_sems.at[w, k],
                recv_sem=recv_sems.at[w, k], device_id=to, device_id_type=MESH)

        first = [copy(w, j, (x, y), (0, c), (*chip, c), src=_part(in_refs[w], w in by_cols, c))
                 for w in range(nw) for j, chip in enumerate((across_x, across_y))]
        first += [pltpu.make_async_remote_copy(
            src_ref=in_refs[w], dst_ref=out_refs[w].at[2 * x + y], send_sem=send_sems.at[w, 8],
            recv_sem=recv_sems.at[w, 8], device_id=sibling, device_id_type=MESH) for w in range(nw)]
        passed = [[copy(w, 2, across_x, (1, c, 0), (*across_y, c)), copy(w, 3, across_y, (1, c, 1), (*across_x, c)),
                   copy(w, 4, across_x, (0, c), sibling), copy(w, 5, across_y, (0, c), sibling)] for w in range(nw)]
        last = [[copy(w, 6, diagonal, (1, c, 0), sibling), copy(w, 7, diagonal, (1, c, 1), sibling)]
                for w in range(nw)]

        def start():
            for cp in first:
                cp.start()

        def middle():
            for w in range(nw):
                copy(w, 0, across_x, (0, c), me).wait_recv()
                copy(w, 1, across_y, (0, c), me).wait_recv()
                for cp in passed[w]:
                    cp.start()

        def finish():
            for w in range(nw):
                copy(w, 2, diagonal, (1, c, 0), me).wait_recv()
                copy(w, 3, diagonal, (1, c, 1), me).wait_recv()
                for cp in last[w]:
                    cp.start()
            for w in range(nw):
                for k, block, part in ((4, across_x, (0, 1 - c)), (5, across_y, (0, 1 - c)),
                                       (6, diagonal, (1, 1 - c, 0)), (7, diagonal, (1, 1 - c, 1))):
                    copy(w, k, block, part, me).wait_recv()
                pltpu.make_async_remote_copy(
                    src_ref=in_refs[w], dst_ref=out_refs[w].at[2 * x + y], send_sem=send_sems.at[w, 8],
                    recv_sem=recv_sems.at[w, 8], device_id=sibling, device_id_type=MESH).wait_recv()
            for cp in first + sum(passed, []) + sum(last, []):
                cp.wait_send()

        return start, middle, finish

    return _Comm(shards, [jax.ShapeDtypeStruct((N_CHIPS,) + w.shape, w.dtype) for w in shards],
                 [(nw, 9), (nw, 9)], build)


def _halved(shape, by_cols):
    return shape[:-1] + (shape[-1] // 2,) if by_cols else shape[:-2] + (shape[-2] // 2, shape[-1])


def _swap_comm(gs, by_cols=()):
    nw = len(gs)

    def build(in_refs, out_refs, sems):
        send_sems, recv_sems = sems
        x, y, c, _ = _place()
        cps = []
        for w in range(nw):
            cps.append(pltpu.make_async_remote_copy(
                src_ref=_part(in_refs[w], w in by_cols, 1 - c, lead=slice(None)), dst_ref=out_refs[w],
                send_sem=send_sems.at[w], recv_sem=recv_sems.at[w], device_id=(x, y, 1 - c), device_id_type=MESH))

        def start():
            for cp in cps:
                cp.start()

        def finish():
            for cp in cps:
                cp.wait()

        return start, finish

    return _Comm(gs, [jax.ShapeDtypeStruct(_halved(g.shape, w in by_cols), g.dtype) for w, g in enumerate(gs)],
                 [(nw,), (nw,)], build)


def _exchange_comm(s1s):
    nw = len(s1s)

    def build(in_refs, out_refs, sems):
        send_sems, recv_sems = sems
        x, y, c, chips = _place()
        cps = [pltpu.make_async_remote_copy(
            src_ref=in_refs[w].at[2 * chip[0] + chip[1]], dst_ref=out_refs[w].at[j], send_sem=send_sems.at[w, j],
            recv_sem=recv_sems.at[w, j], device_id=(*chip, c), device_id_type=MESH)
            for w in range(nw) for j, chip in enumerate(chips)]

        def start():
            for cp in cps:
                cp.start()

        def finish():
            for cp in cps:
                cp.wait()

        return start, finish

    return _Comm(s1s, [jax.ShapeDtypeStruct((N_CHIPS - 1,) + s.shape[1:], s.dtype) for s in s1s],
                 [(nw, 3), (nw, 3)], build)


def _size(dims):
    n = 1
    for d in dims:
        n *= d
    return n


def _sem_grids(comm, sem_refs):
    grids, pos = [], 0
    for dims in comm.sem_dims:
        grids.append(_SemGrid(sem_refs[pos:pos + _size(dims)], dims))
        pos += _size(dims)
    return grids


def _comm_split_start(comm, name, after=()):
    c_in, c_out = len(comm.operands), len(comm.out_shape)
    counts = [_size(d) for d in comm.sem_dims]
    n_sem = sum(counts)
    assert not comm.aliases

    def body(*refs):
        srcs, lands = refs[:c_in], refs[c_in:c_in + c_out]
        first_sem = c_in + c_out + len(after)
        start, _ = comm.build(srcs, lands, _sem_grids(comm, refs[first_sem:first_sem + n_sem]))
        start()
        refs[-1][...] = jnp.zeros(refs[-1].shape, refs[-1].dtype)

    lands = [pltpu.with_memory_space_constraint(lax.empty(o.shape, o.dtype), HBM) for o in comm.out_shape]
    srcs = [pltpu.with_memory_space_constraint(a, HBM) for a in comm.operands]
    res = pl.pallas_call(
        body, name=name, in_specs=_hbm_specs(c_in + c_out) + [pl.BlockSpec(memory_space=pl.ANY)] * len(after),
        out_specs=[pl.BlockSpec(memory_space=pltpu.SEMAPHORE)] * n_sem + _hbm_specs(c_in + c_out)
        + [pl.BlockSpec(memory_space=pltpu.VMEM)],
        out_shape=[pltpu.SemaphoreType.DMA(())] * n_sem + [pltpu.HBM(a.shape, a.dtype) for a in comm.operands]
        + [pltpu.HBM(o.shape, o.dtype) for o in comm.out_shape] + [jax.ShapeDtypeStruct((SUBLANES, LANES), F32)],
        input_output_aliases={i: n_sem + i for i in range(c_in + c_out)},
        compiler_params=_params(has_side_effects=pltpu.SideEffectType.DATAFLOW_SIDE_EFFECTING))(*srcs, *lands, *after)
    return res[:-1], res[-1]


def _comm_split_wait(comm, state, after, name):
    c_in, c_out, n_sem = len(comm.operands), len(comm.out_shape), sum(_size(d) for d in comm.sem_dims)
    sems, srcs, lands = state[:n_sem], state[n_sem:n_sem + c_in], state[n_sem + c_in:]

    def body(*refs):
        src_refs, land_refs = refs[:c_in], refs[c_in:c_in + c_out]
        _, finish = comm.build(src_refs, land_refs, _sem_grids(comm, refs[c_in + c_out:c_in + c_out + n_sem]))
        finish()

    sem_spec = pl.BlockSpec(memory_space=pltpu.SEMAPHORE)
    res = pl.pallas_call(
        body, name=name, in_specs=_hbm_specs(c_in + c_out) + [sem_spec] * n_sem + [pl.BlockSpec(memory_space=pl.ANY)],
        out_specs=_hbm_specs(c_in + c_out),
        out_shape=[pltpu.HBM(a.shape, a.dtype) for a in srcs] + [pltpu.HBM(o.shape, o.dtype) for o in lands],
        input_output_aliases={i: i for i in range(c_in + c_out)},
        compiler_params=_params(has_side_effects=pltpu.SideEffectType.DATAFLOW_SIDE_EFFECTING),
    )(*srcs, *lands, *sems, after)
    return res[:c_in], res[c_in:]


def _share_comm(fs, by_cols=()):
    nw = len(fs)

    def build(in_refs, out_refs, sems):
        del in_refs
        send_sems, recv_sems = sems
        x, y, c, _ = _place()

        def copy(w, half):
            part = _part(out_refs[w], w in by_cols, half)
            return pltpu.make_async_remote_copy(
                src_ref=part, dst_ref=part, send_sem=send_sems.at[w], recv_sem=recv_sems.at[w],
                device_id=(x, y, 1 - c), device_id_type=MESH)

        sends = [copy(w, c) for w in range(nw)]

        def start():
            for cp in sends:
                cp.start()

        def finish():
            for w in range(nw):
                copy(w, 1 - c).wait_recv()
            for cp in sends:
                cp.wait_send()

        return start, finish

    return _Comm(fs, [jax.ShapeDtypeStruct(f.shape, f.dtype) for f in fs],
                 [(nw,), (nw,)], build,
                 aliases={w: w for w in range(nw)})


def _add_sibling(g, r1, place, name, by_cols=False):
    nch, h, cols = r1.shape
    tr = _div_tile(h, 1024 if by_cols else 512, 2 * SUBLANES)
    nb = h // tr
    mine = (lambda k, i, p: (k, i, p[0])) if by_cols else (lambda k, i, p: (k, p[0] * nb + i, 0))

    def body(place_ref, g_ref, r_ref, o_ref):
        del place_ref
        o_ref[...] = (g_ref[...].astype(F32) + r_ref[...].astype(F32)).astype(BF16)

    spec = pltpu.PrefetchScalarGridSpec(
        num_scalar_prefetch=1, grid=(nch, nb),
        in_specs=[pl.BlockSpec((None, tr, cols), mine), pl.BlockSpec((None, tr, cols), lambda k, i, p: (k, i, 0))],
        out_specs=pl.BlockSpec((None, tr, cols), lambda k, i, p: (k, i, 0)))
    return pl.pallas_call(body, name=name, grid_spec=spec, out_shape=jax.ShapeDtypeStruct((nch, h, cols), BF16),
                          compiler_params=_params())(place, g, r1)


def _add_chips(s1, r2, place, name, by_cols=False):
    _, h, cols = s1.shape
    tr = _div_tile(h, 1024 if by_cols else 512, 2 * SUBLANES)
    nb = h // tr
    mine = (lambda i, p: (i, p[0])) if by_cols else (lambda i, p: (p[0] * nb + i, 0))
    whole = (h, 2 * cols) if by_cols else (2 * h, cols)

    def body(place_ref, s_ref, r_ref, o_ref):
        del place_ref
        acc = s_ref[...].astype(F32)
        for j in range(N_CHIPS - 1):
            acc = acc + r_ref[j].astype(F32)
        o_ref[...] = acc

    spec = pltpu.PrefetchScalarGridSpec(
        num_scalar_prefetch=1, grid=(nb,),
        in_specs=[pl.BlockSpec((None, tr, cols), lambda i, p: (p[1], i, 0)),
                  pl.BlockSpec((N_CHIPS - 1, tr, cols), lambda i, p: (0, i, 0))],
        out_specs=pl.BlockSpec((tr, cols), mine))
    return pl.pallas_call(body, name=name, grid_spec=spec, out_shape=jax.ShapeDtypeStruct(whole, F32),
                          compiler_params=_params())(place, s1, r2)


def _quarter_turn(m):
    h = m.shape[-1] // 2
    return jnp.concatenate([-m[..., h:], m[..., :h]], axis=-1)


def _quarter_turn_back(m):
    h = m.shape[-1] // 2
    return jnp.concatenate([m[..., h:], -m[..., :h]], axis=-1)


def _stack_rows(parts):
    out = lax.empty((sum(p.shape[0] for p in parts),) + parts[0].shape[1:], parts[0].dtype)
    row = 0
    for p in parts:
        out = lax.dynamic_update_slice(out, p, (row, 0))
        row += p.shape[0]
    return out


def _join_cols(sh):
    return jnp.concatenate([sh[k] for k in range(N_CHIPS)], axis=1)


def _split_cols(full):
    c = full.shape[1] // N_CHIPS
    return jnp.stack([full[:, k * c:(k + 1) * c] for k in range(N_CHIPS)])


def kernel(x, c, positions, w_ada, b_ada, pre_norm1_g, w_in, gm_ln_g, gm_ln_b, gm_w_s, gm_b_s, w_branch_a, q_norm_g, w_uq, kv_norm_g, w_ukv, w_branch_b, w_out, post_norm1_g, pre_norm2_g, w_up, conv_w, conv_b, w_down, post_norm2_g, loss_target, m_w_ada, m_b_ada, m_pre_norm1_g, m_w_in, m_gm_ln_g, m_gm_ln_b, m_gm_w_s, m_gm_b_s, m_w_branch_a, m_q_norm_g, m_w_uq, m_kv_norm_g, m_w_ukv, m_w_branch_b, m_w_out, m_post_norm1_g, m_pre_norm2_g, m_w_up, m_conv_w, m_conv_b, m_w_down, m_post_norm2_g, v_w_ada, v_b_ada, v_pre_norm1_g, v_w_in, v_gm_ln_g, v_gm_ln_b, v_gm_w_s, v_gm_b_s, v_w_branch_a, v_q_norm_g, v_w_uq, v_kv_norm_g, v_w_ukv, v_w_branch_b, v_w_out, v_post_norm1_g, v_pre_norm2_g, v_w_up, v_conv_w, v_conv_b, v_w_down, v_post_norm2_g):
    given = dict(locals())
    s, d = x.shape[1], x.shape[2]
    gw = gm_ln_g.shape[0]
    ql, kvl = q_norm_g.shape[0], kv_norm_g.shape[0]
    heads = N_CHIPS * w_uq.shape[1] // (NOPE + ROPE)
    ff = N_CHIPS * w_down.shape[0]
    assert gw == d and N_CHIPS * w_ukv.shape[1] == heads * (NOPE + VHEAD)
    ix, iy, ic = lax.axis_index("x"), lax.axis_index("y"), lax.axis_index("c")
    chip = 2 * ix + iy
    dev = 2 * chip + ic
    row = lambda v: v.reshape(1, -1)

    first = _all_gather(jnp.concatenate([jnp.pad(c, ((0, SUBLANES - 1), (0, 0))),
                                         jnp.pad(conv_w, ((0, SUBLANES - CONV_TAPS), (0, 0)))], axis=1), "gather_c")
    first = first.reshape(N_DEV, SUBLANES, d + conv_w.shape[1])
    c_all = first[:, 0, :d]
    conv_wf = first[::N_CORES, :CONV_TAPS, d:].transpose(1, 0, 2).reshape(CONV_TAPS, N_CHIPS * conv_w.shape[1])
    na = w_ada.shape[1]
    b_ada_mine = lax.dynamic_slice(b_ada, (chip * na,), (na,))
    mod_cols = _ada_fwd(c_all, w_ada, row(b_ada_mine), "ada_fwd")
    mod_all = _all_gather(mod_cols, "gather_mod").reshape(N_CHIPS, N_CORES, N_DEV, na)[:, 0]
    mod = lax.dynamic_index_in_dim(mod_all, dev, axis=1, keepdims=False).reshape(N_MOD, d)
    shift1, scale1, gate1, shift2, scale2, gate2 = (mod[i:i + 1] for i in range(N_MOD))

    mine = {n: (given[n].T if n == "w_in" else given[n]).astype(BF16) for n in BIG}
    gather = lambda names: _gather_comm([mine[n] for n in names], [i for i, n in enumerate(names) if n == "w_in"])
    whole = lambda n, g: g
    rows4 = lambda sh4: sh4.reshape(-1, sh4.shape[2])
    wi_t = rows4(whole("w_in", _run_comm(gather(["w_in"]), "gather_w_in")[0]))
    o_q, o_kv, o_pe, o_ga = 2 * gw, 2 * gw + ql, 2 * gw + ql + kvl, 2 * gw + ql + kvl + ROPE
    w_in_big_t = _stack_rows([wi_t[:o_q], wi_t[o_ga:]])
    w_in_lat_t = _stack_rows([wi_t[o_q:o_ga], _quarter_turn(wi_t[o_pe:o_ga].T).T])

    inv = ROPE_THETA ** (-jnp.arange(0, ROPE, 2, dtype=F32) / ROPE)
    ang = positions[0].astype(F32)[:, None] * inv
    cos, sin = jnp.cos(ang), jnp.sin(ang)
    rope_k = jnp.concatenate([cos, cos, sin, sin], axis=1)
    softmax_scale = float(NOPE + ROPE) ** -0.5
    rope_q = jnp.concatenate([jnp.ones((s, NOPE), F32), rope_k], axis=1) * softmax_scale

    x2d, tgt = x[0], loss_target[0]
    g_pre1, g_post1, g_pre2, g_post2 = row(pre_norm1_g), row(post_norm1_g), row(pre_norm2_g), row(post_norm2_g)
    ln_g, ln_b, q_g, kv_g = row(gm_ln_g), row(gm_ln_b), row(q_norm_g), row(kv_norm_g)
    b_s_t = gm_b_s.T
    conv_bf = row(conv_b)

    h1 = _prenorm(x2d, g_pre1, scale1, shift1, "prenorm1")
    z_big, (g_uq, g_ukv, g_a) = _matmul(h1, w_in_big_t, mode="nt", out_dtype=BF16, name="mm_z_big", tm=s,
                                        comm=gather(["w_uq", "w_ukv", "w_branch_a"]))
    wq = _join_cols(whole("w_uq", g_uq)).reshape(ql, heads, NOPE + ROPE)
    w_q = jnp.concatenate([wq, _quarter_turn(wq[:, :, NOPE:])], axis=2).reshape(ql, heads * HEAD_W)
    w_kv = _join_cols(whole("w_ukv", g_ukv)).reshape(kvl, heads, 2, NOPE).transpose(0, 2, 1, 3)
    w_kv = w_kv.reshape(kvl, 2 * heads * NOPE)
    w_a = rows4(whole("w_branch_a", g_a))
    z_lat = _matmul(h1, w_in_lat_t, mode="nt", out_dtype=F32, name="mm_z_lat", tm=s, tn=1024)
    a_act = _gmlp_fwd(z_big, ln_g, ln_b, gm_w_s, b_s_t, "gmlp_fwd")
    qn, kvn, kr = _mla_prep(z_lat, q_g, kv_g, rope_k, "mla_prep")
    q_rot = _matmul(qn, w_q, mode="nn", out_dtype=BF16, name="mm_q", tm=s, tn=HEAD_W, mul=rope_q)
    kv_all = _matmul(kvn, w_kv, mode="nn", out_dtype=BF16, name="mm_kv", tm=s, tn=1024)
    (o_att, lse), (g_b, g_o, g_up) = _attn_fwd(q_rot, kv_all, kr, heads, "attn_fwd",
                                               comm=gather(["w_branch_b", "w_out", "w_up"]))
    w_b, w_o, w_upf = rows4(whole("w_branch_b", g_b)), rows4(whole("w_out", g_o)), whole("w_up", g_up)
    y_a = _matmul(a_act, w_a, mode="nn", out_dtype=BF16, name="mm_y_a", tm=s)
    y_b = _matmul(o_att, w_b, mode="nn", out_dtype=BF16, name="mm_y_b", tm=s)
    merged = _merge(z_big, y_a, y_b, "merge")
    y1 = _matmul(merged, w_o, mode="nn", out_dtype=F32, name="mm_y1", tm=s)
    x1, h2 = _post_pre(x2d, y1, gate1, g_post1, g_pre2, scale2, shift2, "post1_pre2")

    up_pre, (g_dn,) = _matmul(h2, w_upf, mode="nn", out_dtype=BF16, name="mm_up", tm=s, tn=1408,
                              comm=gather(["w_down"]))
    w_dn = rows4(whole("w_down", g_dn))
    act = _conv_fwd(up_pre, conv_wf, conv_bf, "conv_fwd")
    ffn = _matmul(act, w_dn, mode="nn", out_dtype=F32, name="mm_ffn", tm=s, tn=1024, tk=1408)

    dffn, dgate2, g_post2_grad, dx2, loss_part = _post_bwd(ffn, gate2, g_post2, "post2_bwd", xin=x1, target=tgt)
    loss = lax.psum(loss_part[0, 0], ("x", "y", "c"))
    place = jnp.stack([ic, chip]).astype(jnp.int32)
    rows_of = lambda g: g.reshape(N_CHIPS, g.shape[0] // N_CHIPS, g.shape[1])
    add_sibling = lambda names, gs, r1s: [_add_sibling(g, r1, place, "rs_add_sibling_" + n, by_cols=n == "w_in")
                                          for n, g, r1 in zip(names, gs, r1s)]
    add_chips = lambda names, s1s, r2s: [_add_chips(s1, r2, place, "rs_add_chips_" + n, by_cols=n == "w_in")
                                         for n, s1, r2 in zip(names, s1s, r2s)]
    gp_down = [rows_of(_matmul(act, dffn, mode="tn", out_dtype=BF16, name="mm_gw_down", tn=2048, tk=s))]
    dact, r1_down = _matmul(dffn, w_dn, mode="nt", out_dtype=BF16, name="mm_dact", tm=s, comm=_swap_comm(gp_down))
    s1_down = add_sibling(["w_down"], gp_down, r1_down)
    (dup, gcw_g, gcw_v, gcb_g, gcb_v), r2_down = _conv_bwd(up_pre, dact, conv_wf, conv_bf, "conv_bwd",
                                                            comm=_exchange_comm(s1_down))
    half_down = add_chips(["w_down"], s1_down, r2_down)
    dh2 = _matmul(dup, w_upf, mode="nt", out_dtype=F32, name="mm_dh2", tm=s, tn=1024, tk=1408)
    dx1, dshift2, dscale2, g_pre2_grad = _prenorm_bwd(x1, dh2, dx2, g_pre2, scale2, "prenorm2_bwd")

    dy1, dgate1, g_post1_grad = _post_bwd(y1, gate1, g_post1, "post1_bwd", dxo=dx1)
    dmerged = _matmul(dy1, w_o, mode="nt", out_dtype=BF16, name="mm_dmerged", tm=s)
    gw_out = _matmul(merged, dy1, mode="tn", out_dtype=BF16, name="mm_gw_out", tn=1024, tk=s)
    dy_a, dy_b, dz_big = _merge_bwd(dmerged, z_big, y_a, y_b, "merge_bwd")
    gw_a = _matmul(a_act, dy_a, mode="tn", out_dtype=BF16, name="mm_gw_a", tn=1024, tk=s)
    gw_b = _matmul(o_att, dy_b, mode="tn", out_dtype=BF16, name="mm_gw_b", tn=1024, tk=s)
    mid = ["w_up", "w_out", "w_branch_a", "w_branch_b"]
    gp_oab = [rows_of(gw_out), rows_of(gw_a), rows_of(gw_b)]
    da, r1_oab = _matmul(dy_a, w_a, mode="nt", out_dtype=BF16, name="mm_da", tm=s, comm=_swap_comm(gp_oab))
    s1_oab = add_sibling(mid[1:], gp_oab, r1_oab)
    gw_up, r2_oa = _matmul(h2, dup, mode="tn", out_dtype=BF16, name="mm_gw_up", tm=1024, tn=1408, tk=s,
                           out_groups=N_CHIPS, comm=_exchange_comm(s1_oab[:2]))
    do = _matmul(dy_b, w_b, mode="nt", out_dtype=BF16, name="mm_do", tm=s)
    (dz_big, g_ws, g_bs_t, g_ln_g, g_ln_b), r1_up = _gmlp_bwd(z_big, da, dz_big, ln_g, ln_b, gm_w_s, b_s_t,
                                                               "gmlp_bwd", comm=_swap_comm([gw_up]))
    s1_mid = add_sibling(mid[:1], [gw_up], r1_up) + s1_oab
    (dq_big, dk, dv), r2_up = _attn_bwd(q_rot, kv_all, kr, o_att, do, lse, rope_q, heads, "attn_bwd",
                                        comm=_exchange_comm(s1_mid[:1]))
    dkv, dkk = _mla_bwd_mid(dk, dv, rope_k, heads, "mla_bwd_mid")
    gw_q = _matmul(qn, dq_big, mode="tn", out_dtype=F32, name="mm_gw_q", tn=1024, tk=s)
    dqn = _matmul(dq_big, w_q, mode="nt", out_dtype=F32, name="mm_dqn", tm=s, tk=1024)
    gw_kv = _matmul(kvn, dkv, mode="tn", out_dtype=BF16, name="mm_gw_kv", tn=1024, tk=s)
    dkvn = _matmul(dkv, w_kv, mode="nt", out_dtype=F32, name="mm_dkvn", tm=s, tk=1024)
    dz_lat, g_q, g_kv = _mla_bwd_post(z_lat, dqn, dkvn, dkk, q_g, kv_g, "mla_bwd_post")

    partial = {
        "gm_ln_g": g_ln_g, "gm_ln_b": g_ln_b, "gm_w_s": g_ws, "gm_b_s": g_bs_t[:, :gm_b_s.shape[0]].T,
        "q_norm_g": g_q, "kv_norm_g": g_kv, "post_norm1_g": g_post1_grad, "pre_norm2_g": g_pre2_grad,
        "conv_w": jnp.concatenate([gcw_g, gcw_v], axis=1), "conv_b": jnp.concatenate([gcb_g, gcb_v], axis=1),
        "post_norm2_g": g_post2_grad,
    }
    flat = jnp.concatenate([partial[n].reshape(-1) for n in SMALL_PARTIAL])
    n_small = flat.shape[0]
    rows_small = -(-n_small // (LANES * SMALL_ROW_TILE)) * SMALL_ROW_TILE
    flat = jnp.pad(flat, (0, rows_small * LANES - n_small)).reshape(rows_small, LANES)

    def small_pack(prefix, source):
        v = jnp.concatenate([source[prefix + n].reshape(-1) for n in SMALL])
        rows = -(-v.shape[0] // (LANES * SUBLANES)) * SUBLANES
        return jnp.pad(v, (0, rows * LANES - v.shape[0])).reshape(rows, LANES)

    small_state = [small_pack(prefix, given) for prefix in ("", "m_", "v_")]

    dh1, r2_a_b = _matmul(dz_big, w_in_big_t, mode="nn", out_dtype=F32, name="mm_dh1_big", tm=s, tn=1024, tk=1024,
                          comm=_exchange_comm(s1_mid[3:]))
    half_mid = add_chips(mid, s1_mid, list(r2_up) + list(r2_oa) + list(r2_a_b))
    gw_big_t, hosted = _matmul(dz_big, h1, mode="tn", out_dtype=BF16, name="mm_gw_in_big", tn=2048, tk=s,
                               comm=_join_comms([_share_comm(half_down + half_mid), _gather8_comm(flat)]))
    shared, small_all = hosted[:-1], lax.dynamic_update_slice(hosted[-1], flat[None], (dev, 0, 0))
    small_sum = _sum_leading(small_all, "sum_small", after=small_state + [loss.reshape(1, 1)]).reshape(-1)
    small_grads, off = {}, 0
    for n in SMALL_PARTIAL:
        shape = (CONV_TAPS, 2 * ff) if n == "conv_w" else given[n].shape
        small_grads[n] = small_sum[off:off + partial[n].size].reshape(shape)
        off += partial[n].size
    small_grads["conv_w"] = lax.dynamic_slice(small_grads["conv_w"], (0, chip * conv_w.shape[1]), conv_w.shape)
    grads = dict(zip(["w_down"] + mid, shared), **small_grads)
    gw_lat_t = _matmul(dz_lat, h1, mode="tn", out_dtype=F32, name="mm_gw_in_lat", tm=1024, tn=1024, tk=s)

    gq = gw_q.reshape(ql, heads, HEAD_W)
    gq_pe = gq[:, :, NOPE:NOPE + ROPE] + _quarter_turn_back(gq[:, :, NOPE + ROPE:])
    g_pe_t = gw_lat_t[ql + kvl:ql + kvl + ROPE] + _quarter_turn_back(gw_lat_t[ql + kvl + ROPE:].T).T
    last = ["w_in", "w_uq", "w_ukv"]
    gw_in_t = _stack_rows([gw_big_t[:o_q], gw_lat_t[:ql + kvl].astype(BF16), g_pe_t.astype(BF16), gw_big_t[o_q:]])
    gp_last = [
        gw_in_t.reshape(N_CHIPS, gw_in_t.shape[0] // N_CHIPS, d),
        _split_cols(jnp.concatenate([gq[:, :, :NOPE], gq_pe], axis=2).reshape(ql, heads * (NOPE + ROPE)).astype(BF16)),
        _split_cols(gw_kv.reshape(kvl, 2, heads, NOPE).transpose(0, 2, 1, 3).reshape(kvl, heads * 2 * NOPE)),
    ]
    dh1, r1_last = _matmul(dz_lat, w_in_lat_t, mode="nn", out_dtype=F32, name="mm_dh1_lat", tm=s, tk=1024, add=dh1,
                           comm=_swap_comm(gp_last, by_cols=[0]))
    grad_x, dshift1, dscale1, g_pre1_grad = _prenorm_bwd(x2d, dh1, dx1, g_pre1, scale1, "prenorm1_bwd")
    s1_last = add_sibling(last, gp_last, r1_last)

    dmod = jnp.concatenate([dshift1, dscale1, dgate1, dshift2, dscale2, dgate2, g_pre1_grad], axis=1)
    dmod_all = _all_gather(jnp.pad(dmod, ((0, SUBLANES - 1), (0, 0))), "gather_dmod")
    dmod_all = dmod_all.reshape(N_DEV, SUBLANES, (N_MOD + 1) * d)[:, 0]
    dmod_sum = _sum_leading(dmod_all.reshape(N_DEV, 1, (N_MOD + 1) * d), "sum_dmod")[0]
    grads["b_ada"], grads["pre_norm1_g"] = dmod_sum[:N_MOD * d], dmod_sum[N_MOD * d:]
    dmod_mine = lax.dynamic_slice(dmod_all, (0, chip * na), (N_DEV, na))
    grads["w_ada"] = _ada_bwd(c_all.T, dmod_mine, "ada_bwd")

    delta, new_m, new_v = {}, {}, {}

    def adamw(n, after=None):
        turn = (lambda a: a.T) if n == "w_in" else (lambda a: a)
        outs = _adamw(turn(given[n]), grads[n], turn(given["m_" + n]), turn(given["v_" + n]), "adamw_" + n,
                      after=after, emit_grad=n in BIG)
        g_out = outs[0] if n in BIG else grads[n]
        grads[n], delta[n], new_m[n], new_v[n] = (turn(o) for o in (g_out, *outs[-3:]))

    exchange_last = _exchange_comm(s1_last)
    in_flight, token = _comm_split_start(exchange_last, "rs_exchange_last_start", after=[dmod_sum, small_sum])
    for n in ["w_ada", "w_down"] + mid:
        adamw(n, after=token)
    s1_last, r2_last = _comm_split_wait(exchange_last, in_flight, delta[mid[-1]], "rs_exchange_last_wait")
    half_last = add_chips(last, s1_last, r2_last)
    grads.update(zip(last, _run_comm(_share_comm(half_last, by_cols=[0]), "rs_share_last")))
    for n in last:
        adamw(n)

    outs = _adamw(small_state[0], small_pack("", grads), small_state[1], small_state[2], "adamw_small")
    off = 0
    for n in SMALL:
        size = given[n].size
        for store, packed_out in zip((delta, new_m, new_v), outs):
            store[n] = packed_out.reshape(-1)[off:off + size].reshape(given[n].shape)
        off += size

    return (loss, grad_x[None], *[grads[n] for n in WEIGHTS], *[delta[n] for n in WEIGHTS],
            *[new_m[n] for n in WEIGHTS], *[new_v[n] for n in WEIGHTS])
```

```python
import functools

import jax
import jax.numpy as jnp
from jax import lax
from jax.experimental import pallas as pl
from jax.experimental.pallas import tpu as pltpu

F32 = jnp.float32
BF16 = jnp.bfloat16
MESH = pl.DeviceIdType.MESH
HBM = pltpu.HBM

EPS = 1e-6
NOPE, ROPE, VHEAD = 128, 64, 128
HEAD_W = NOPE + 2 * ROPE
ROPE_THETA = 10000.0
CONV_TAPS = 3
N_MOD = 6
N_CHIPS, N_CORES, N_DEV = 4, 2, 8
ADAM_LR, ADAM_B1, ADAM_B2, ADAM_EPS, ADAM_WD, ADAM_STEP = 0.001, 0.9, 0.999, 1e-08, 0.01, 10

LANES = 128
SUBLANES = 8
VMEM_LIMIT = 56 * 2**20
MIDDLE_STAGE_AT = 70
TOKEN_BLOCK = 512
SMALL_ROW_TILE = 256
ADAMW_BLOCK_ELEMS = 768 * 1024

BIG = ("w_in", "w_branch_a", "w_uq", "w_ukv", "w_branch_b", "w_out", "w_up", "w_down")
WEIGHTS = ("w_ada", "b_ada", "pre_norm1_g", "w_in", "gm_ln_g", "gm_ln_b", "gm_w_s", "gm_b_s", "w_branch_a",
           "q_norm_g", "w_uq", "kv_norm_g", "w_ukv", "w_branch_b", "w_out", "post_norm1_g", "pre_norm2_g",
           "w_up", "conv_w", "conv_b", "w_down", "post_norm2_g")
SMALL_PARTIAL = ("gm_ln_g", "gm_ln_b", "gm_w_s", "gm_b_s", "q_norm_g", "kv_norm_g", "post_norm1_g",
                 "pre_norm2_g", "conv_w", "conv_b", "post_norm2_g")
SMALL = ("b_ada", "pre_norm1_g") + SMALL_PARTIAL


def _div_tile(n, cap, mult=LANES):
    t = (min(cap, n) // mult) * mult
    while t >= mult:
        if n % t == 0:
            return t
        t -= mult
    return n


def _params(**kw):
    return pltpu.CompilerParams(vmem_limit_bytes=VMEM_LIMIT, **kw)


def _row_spec(width):
    return pl.BlockSpec((1, width), lambda *_: (0, 0))


def _gelu(x):
    k = 0.7978845608028654
    return 0.5 * x * (1.0 + jnp.tanh(k * (x + 0.044715 * x * x * x)))


def _gelu_grad(x):
    k = 0.7978845608028654
    t = jnp.tanh(k * (x + 0.044715 * x * x * x))
    return 0.5 * (1.0 + t) + 0.5 * x * (1.0 - t * t) * k * (1.0 + 3.0 * 0.044715 * x * x)


def _sigmoid(x):
    return 0.5 * jnp.tanh(0.5 * x) + 0.5


def _dot(a, b, dims):
    return lax.dot_general(a, b, (dims, ((), ())), preferred_element_type=F32)


NN = ((1,), (0,))
NT = ((1,), (1,))
TN = ((0,), (0,))


def _logical(arr):
    if arr.ndim == 2:
        return arr.shape[0], arr.shape[1], arr.shape[1]
    return arr.shape[1], arr.shape[0] * arr.shape[2], arr.shape[2]


def _tile_spec(ndim, group_w, blk_rows, blk_cols, row_of, col_of):
    if ndim == 2:
        return pl.BlockSpec((blk_rows, blk_cols), lambda i, j, k: (row_of(i, j, k), col_of(i, j, k)))
    per = group_w // blk_cols
    return pl.BlockSpec((None, blk_rows, blk_cols),
                        lambda i, j, k: (col_of(i, j, k) // per, row_of(i, j, k), col_of(i, j, k) % per))


def _matmul(a, b, *, mode, out_dtype, name, tm=512, tn=512, tk=2048, mul=None, add=None, out_groups=None, comm=None):
    ar, ac, agw = _logical(a)
    br, bc, bgw = _logical(b)
    if mode == "nn":
        m, kd, n = ar, ac, bc
        m_w, k_w, n_w = (), (agw,), (bgw,)
    elif mode == "nt":
        m, kd, n = ar, ac, br
        m_w, k_w, n_w = (), (agw, bgw), ()
    else:
        m, kd, n = ac, ar, bc
        m_w, k_w, n_w = (agw,), (), (bgw,)
    if out_groups is not None:
        n_w = n_w + (n // out_groups,)
    tm = _div_tile(min((m,) + m_w), tm, LANES if mode == "tn" else SUBLANES)
    tn = _div_tile(min((n,) + n_w), tn)
    tk = _div_tile(min((kd,) + k_w), tk)
    assert all(w % tn == 0 for w in n_w) and all(w % tk == 0 for w in k_w) and all(w % tm == 0 for w in m_w)
    nk = kd // tk
    dims = {"nn": NN, "nt": NT, "tn": TN}[mode]
    gi, gj, gk = (lambda i, j, k: i), (lambda i, j, k: j), (lambda i, j, k: k)
    if mode == "nn":
        a_spec = _tile_spec(a.ndim, agw, tm, tk, gi, gk)
        b_spec = _tile_spec(b.ndim, bgw, tk, tn, gk, gj)
    elif mode == "nt":
        a_spec = _tile_spec(a.ndim, agw, tm, tk, gi, gk)
        b_spec = _tile_spec(b.ndim, bgw, tn, tk, gj, gk)
    else:
        a_spec = _tile_spec(a.ndim, agw, tk, tm, gk, gi)
        b_spec = _tile_spec(b.ndim, bgw, tk, tn, gk, gj)
    in_specs, operands = [a_spec, b_spec], [a, b]
    if mul is not None:
        assert mul.shape == (m, tn)
        in_specs.append(pl.BlockSpec((tm, tn), lambda i, j, k: (i, 0)))
        operands.append(mul)
    if add is not None:
        in_specs.append(pl.BlockSpec((tm, tn), lambda i, j, k: (i, j)))
        operands.append(add)

    def body(*refs):
        a_ref, b_ref = refs[0], refs[1]
        pos = 2
        mul_ref = add_ref = None
        if mul is not None:
            mul_ref, pos = refs[pos], pos + 1
        if add is not None:
            add_ref, pos = refs[pos], pos + 1
        o_ref = refs[pos]

        def finish(r):
            if mul_ref is not None:
                r = r * mul_ref[...]
            if add_ref is not None:
                r = r + add_ref[...]
            o_ref[...] = r.astype(out_dtype)

        part = _dot(a_ref[...], b_ref[...], dims)
        if nk == 1:
            finish(part)
        else:
            acc_ref = refs[pos + 1]
            k = pl.program_id(2)

            @pl.when(k == 0)
            def _():
                acc_ref[...] = part

            @pl.when(k > 0)
            def _():
                acc_ref[...] += part

            @pl.when(k == nk - 1)
            def _():
                finish(acc_ref[...])

    if out_groups is None:
        out_spec, out_dims = _tile_spec(2, n, tm, tn, gi, gj), (m, n)
    else:
        out_spec, out_dims = _tile_spec(3, n // out_groups, tm, tn, gi, gj), (out_groups, m, n // out_groups)
    return _call(body, operands, comm, name=name, grid=(m // tm, n // tn, nk), in_specs=in_specs, out_specs=out_spec,
                 out_shape=jax.ShapeDtypeStruct(out_dims, out_dtype),
                 scratch_shapes=[] if nk == 1 else [pltpu.VMEM((tm, tn), F32)])


def _accumulate(ref, value):
    @pl.when(pl.program_id(0) == 0)
    def _():
        ref[...] = value

    @pl.when(pl.program_id(0) > 0)
    def _():
        ref[...] += value


def _colsum(v):
    return jnp.sum(v, axis=0, keepdims=True)


def _rowmean(v):
    return jnp.mean(v, axis=-1, keepdims=True)


def _prenorm(x, g, scale, shift, name):
    s, d = x.shape
    tb = _div_tile(s, TOKEN_BLOCK, SUBLANES)

    def body(x_ref, g_ref, sc_ref, sh_ref, h_ref):
        xv = x_ref[...]
        r = lax.rsqrt(_rowmean(xv * xv) + EPS)
        h_ref[...] = ((xv * r) * g_ref[...] * (1.0 + sc_ref[...]) + sh_ref[...]).astype(BF16)

    blk = pl.BlockSpec((tb, d), lambda i: (i, 0))
    return pl.pallas_call(
        body, name=name, grid=(s // tb,), in_specs=[blk, _row_spec(d), _row_spec(d), _row_spec(d)],
        out_specs=blk, out_shape=jax.ShapeDtypeStruct((s, d), BF16), compiler_params=_params(),
    )(x, g, scale, shift)


def _post_pre(x, y, gate, pg, g2, scale2, shift2, name):
    s, d = x.shape
    tb = _div_tile(s, TOKEN_BLOCK, SUBLANES)

    def body(x_ref, y_ref, gate_ref, pg_ref, g2_ref, sc_ref, sh_ref, x1_ref, h2_ref):
        yv = y_ref[...]
        rp = lax.rsqrt(_rowmean(yv * yv) + EPS)
        x1 = x_ref[...] + gate_ref[...] * ((yv * rp) * pg_ref[...])
        x1_ref[...] = x1
        r2 = lax.rsqrt(_rowmean(x1 * x1) + EPS)
        h2_ref[...] = ((x1 * r2) * g2_ref[...] * (1.0 + sc_ref[...]) + sh_ref[...]).astype(BF16)

    blk = pl.BlockSpec((tb, d), lambda i: (i, 0))
    return pl.pallas_call(
        body, name=name, grid=(s // tb,), in_specs=[blk, blk] + [_row_spec(d)] * 5,
        out_specs=[blk, blk],
        out_shape=[jax.ShapeDtypeStruct((s, d), F32), jax.ShapeDtypeStruct((s, d), BF16)],
        compiler_params=_params(),
    )(x, y, gate, pg, g2, scale2, shift2)


def _post_bwd(y, gate, pg, name, *, dxo=None, xin=None, target=None):
    s, d = y.shape
    tb = _div_tile(s, TOKEN_BLOCK, SUBLANES)
    from_loss = target is not None

    def body(*refs):
        if from_loss:
            y_ref, gate_ref, pg_ref, xin_ref, t_ref, dy_ref, dgate_ref, dpg_ref, dxo_ref, loss_ref = refs
        else:
            y_ref, gate_ref, pg_ref, dxo_in_ref, dy_ref, dgate_ref, dpg_ref = refs
        yv = y_ref[...]
        rp = lax.rsqrt(_rowmean(yv * yv) + EPS)
        yh = yv * rp
        fn = yh * pg_ref[...]
        gate = gate_ref[...]
        if from_loss:
            err = xin_ref[...] + gate * fn - t_ref[...]
            dxo = err * (1.0 / d)
            dxo_ref[...] = dxo
            part = 0.5 * jnp.sum(_rowmean(err * err), axis=0, keepdims=True)
            _accumulate(loss_ref, jnp.broadcast_to(part, loss_ref.shape))
        else:
            dxo = dxo_in_ref[...]
        _accumulate(dgate_ref, _colsum(dxo * fn))
        dfn = dxo * gate
        _accumulate(dpg_ref, _colsum(dfn * yh))
        dyh = dfn * pg_ref[...]
        dy_ref[...] = (rp * (dyh - yh * _rowmean(dyh * yh))).astype(BF16)

    blk = pl.BlockSpec((tb, d), lambda i: (i, 0))
    in_specs = [blk, _row_spec(d), _row_spec(d)]
    out_specs = [blk, _row_spec(d), _row_spec(d)]
    out_shape = [jax.ShapeDtypeStruct((s, d), BF16), jax.ShapeDtypeStruct((1, d), F32),
                 jax.ShapeDtypeStruct((1, d), F32)]
    if from_loss:
        operands = (y, gate, pg, xin, target)
        in_specs += [blk, blk]
        out_specs += [blk, _row_spec(LANES)]
        out_shape += [jax.ShapeDtypeStruct((s, d), F32), jax.ShapeDtypeStruct((1, LANES), F32)]
    else:
        operands = (y, gate, pg, dxo)
        in_specs += [blk]
    return pl.pallas_call(
        body, name=name, grid=(s // tb,), in_specs=in_specs, out_specs=out_specs, out_shape=out_shape,
        compiler_params=_params(),
    )(*operands)


def _prenorm_bwd(xin, dh, dres, g, scale, name, comm=None):
    s, d = xin.shape
    tb = _div_tile(s, TOKEN_BLOCK, SUBLANES)

    def body(x_ref, dh_ref, dres_ref, g_ref, sc_ref, dx_ref, dshift_ref, dscale_ref, dg_ref):
        xv = x_ref[...]
        r = lax.rsqrt(_rowmean(xv * xv) + EPS)
        xn = xv * r
        dh = dh_ref[...]
        g1 = g_ref[...]
        s1 = 1.0 + sc_ref[...]
        _accumulate(dshift_ref, _colsum(dh))
        _accumulate(dscale_ref, _colsum(dh * xn * g1))
        _accumulate(dg_ref, _colsum(dh * xn * s1))
        dxn = dh * g1 * s1
        dx_ref[...] = dres_ref[...] + r * (dxn - xn * _rowmean(dxn * xn))

    blk = pl.BlockSpec((tb, d), lambda i: (i, 0))
    return _call(
        body, (xin, dh, dres, g, scale), comm, name=name, grid=(s // tb,),
        in_specs=[blk, blk, blk, _row_spec(d), _row_spec(d)],
        out_specs=[blk, _row_spec(d), _row_spec(d), _row_spec(d)],
        out_shape=[jax.ShapeDtypeStruct((s, d), F32)] + [jax.ShapeDtypeStruct((1, d), F32)] * 3)


def _merge(z_big, y_a, y_b, name):
    s, d = y_a.shape
    tb = _div_tile(s, TOKEN_BLOCK, SUBLANES)

    def body(zg_ref, ya_ref, yb_ref, o_ref):
        ga, gb = zg_ref[:, :d].astype(F32), zg_ref[:, d:].astype(F32)
        o_ref[...] = (_sigmoid(ga) * ya_ref[...].astype(F32) + _sigmoid(gb) * yb_ref[...].astype(F32)).astype(BF16)

    blk = pl.BlockSpec((tb, d), lambda i: (i, 0))
    return pl.pallas_call(
        body, name=name, grid=(s // tb,), in_specs=[pl.BlockSpec((tb, 2 * d), lambda i: (i, 1)), blk, blk],
        out_specs=blk, out_shape=jax.ShapeDtypeStruct((s, d), BF16), compiler_params=_params(),
    )(z_big, y_a, y_b)


def _merge_bwd(dmerged, z_big, y_a, y_b, name):
    s, d = y_a.shape
    tb = _div_tile(s, TOKEN_BLOCK, SUBLANES)

    def body(dm_ref, zg_ref, ya_ref, yb_ref, dya_ref, dyb_ref, dz_ref):
        dm = dm_ref[...].astype(F32)
        sa, sb = _sigmoid(zg_ref[:, :d].astype(F32)), _sigmoid(zg_ref[:, d:].astype(F32))
        dya_ref[...] = (dm * sa).astype(BF16)
        dyb_ref[...] = (dm * sb).astype(BF16)
        dz_ref[:, :d] = (dm * ya_ref[...].astype(F32) * sa * (1.0 - sa)).astype(BF16)
        dz_ref[:, d:] = (dm * yb_ref[...].astype(F32) * sb * (1.0 - sb)).astype(BF16)

    blk = pl.BlockSpec((tb, d), lambda i: (i, 0))
    wide = pl.BlockSpec((tb, 2 * d), lambda i: (i, 1))
    return pl.pallas_call(
        body, name=name, grid=(s // tb,), in_specs=[blk, wide, blk, blk], out_specs=[blk, blk, wide],
        out_shape=[jax.ShapeDtypeStruct((s, d), BF16), jax.ShapeDtypeStruct((s, d), BF16),
                   jax.ShapeDtypeStruct((s, 4 * d), BF16)],
        compiler_params=_params(),
    )(dmerged, z_big, y_a, y_b)


def _causal_mask(ch):
    q = lax.broadcasted_iota(jnp.int32, (ch, ch), 0)
    p = lax.broadcasted_iota(jnp.int32, (ch, ch), 1)
    return (p <= q).astype(F32)


def _gmlp_norm(zc, lng, lnb, gw):
    u_pre, v_pre = zc[:, :gw], zc[:, gw:]
    vg = _gelu(v_pre)
    mu = _rowmean(vg)
    cen = vg - mu
    rstd = lax.rsqrt(_rowmean(cen * cen) + EPS)
    vhat = cen * rstd
    return u_pre, v_pre, _gelu(u_pre), vhat, rstd, vhat * lng + lnb


def _gmlp_fwd(z_big, ln_g, ln_b, w_s, b_s_t, name):
    s = z_big.shape[0]
    groups, ch, _ = w_s.shape
    gw = ln_g.shape[1]
    gd = gw // groups

    def body(z_ref, lng_ref, lnb_ref, ws_ref, bt_ref, a_ref):
        _, _, u, _, _, vn = _gmlp_norm(z_ref[...].astype(F32), lng_ref[...], lnb_ref[...], gw)
        mask = _causal_mask(ch)
        for g in range(groups):
            cols = slice(g * gd, (g + 1) * gd)
            wm = (ws_ref[g] * mask).astype(BF16)
            mixed = _dot(wm, vn[:, cols].astype(BF16), NN) + bt_ref[:, g:g + 1]
            a_ref[:, cols] = (u[:, cols] * mixed).astype(BF16)

    return pl.pallas_call(
        body, name=name, grid=(s // ch,),
        in_specs=[pl.BlockSpec((ch, 2 * gw), lambda n: (n, 0)), _row_spec(gw), _row_spec(gw),
                  pl.BlockSpec((groups, ch, ch), lambda n: (0, 0, 0)), pl.BlockSpec((ch, groups), lambda n: (0, 0))],
        out_specs=pl.BlockSpec((ch, gw), lambda n: (n, 0)),
        out_shape=jax.ShapeDtypeStruct((s, gw), BF16), compiler_params=_params(),
    )(z_big, ln_g, ln_b, w_s, b_s_t)


def _gmlp_bwd(z_big, da, dz_big, ln_g, ln_b, w_s, b_s_t, name, comm=None):
    s = z_big.shape[0]
    groups, ch, _ = w_s.shape
    gw = ln_g.shape[1]
    gd = gw // groups

    def body(z_ref, da_ref, dzin_ref, lng_ref, lnb_ref, ws_ref, bt_ref, dz_ref, gws_ref, gbt_ref, glng_ref, glnb_ref,
             vg_ref, dvh_ref):
        del dzin_ref
        mask = _causal_mask(ch)
        lane = lax.broadcasted_iota(jnp.int32, (ch, LANES), 1)
        group_cols = [slice(g * gd, (g + 1) * gd) for g in range(groups)]
        rowsum = lambda v: jnp.sum(v, axis=1, keepdims=True)

        @pl.when(pl.program_id(0) == 0)
        def _():
            for ref in (gws_ref, gbt_ref, glng_ref, glnb_ref):
                ref[...] = jnp.zeros(ref.shape, F32)

        total = jnp.zeros((ch, 1), F32)
        for cols in group_cols:
            vg = _gelu(z_ref[:, gw + cols.start:gw + cols.stop].astype(F32))
            vg_ref[:, cols] = vg
            total = total + rowsum(vg)
        mu = total * (1.0 / gw)
        total = jnp.zeros((ch, 1), F32)
        for cols in group_cols:
            cen = vg_ref[:, cols] - mu
            total = total + rowsum(cen * cen)
        rstd = lax.rsqrt(total * (1.0 / gw) + EPS)
        m1, m2, gb = jnp.zeros((ch, 1), F32), jnp.zeros((ch, 1), F32), jnp.zeros((ch, LANES), F32)
        for g, cols in enumerate(group_cols):
            vhat = (vg_ref[:, cols] - mu) * rstd
            vn_g = (vhat * lng_ref[:, cols] + lnb_ref[:, cols]).astype(BF16)
            wm = (ws_ref[g] * mask).astype(BF16)
            mixed = _dot(wm, vn_g, NN) + bt_ref[:, g:g + 1]
            u_pre, da_g = z_ref[:, cols].astype(F32), da_ref[:, cols].astype(F32)
            dz_ref[:, cols] = (da_g * mixed * _gelu_grad(u_pre)).astype(BF16)
            dmixed = da_g * _gelu(u_pre)
            dm16 = dmixed.astype(BF16)
            dvn = _dot(wm, dm16, TN)
            gws_ref[g] += _dot(dm16, vn_g, NT) * mask
            gb = gb + jnp.where(lane == g, rowsum(dmixed), 0.0)
            glnb_ref[:, cols] += _colsum(dvn)
            glng_ref[:, cols] += _colsum(dvn * vhat)
            dvh = dvn * lng_ref[:, cols]
            dvh_ref[:, cols] = dvh
            m1, m2 = m1 + rowsum(dvh), m2 + rowsum(dvh * vhat)
        gbt_ref[...] += gb
        m1, m2 = m1 * (1.0 / gw), m2 * (1.0 / gw)
        for cols in group_cols:
            vhat = (vg_ref[:, cols] - mu) * rstd
            dvg = rstd * (dvh_ref[:, cols] - m1 - vhat * m2)
            v_pre = z_ref[:, gw + cols.start:gw + cols.stop].astype(F32)
            dz_ref[:, gw + cols.start:gw + cols.stop] = (dvg * _gelu_grad(v_pre)).astype(BF16)

    zspec = pl.BlockSpec((ch, 2 * gw), lambda n: (n, 0))
    return _call(
        body, (z_big, da, dz_big, ln_g, ln_b, w_s, b_s_t), comm, name=name, grid=(s // ch,),
        in_specs=[zspec, pl.BlockSpec((ch, gw), lambda n: (n, 0)), pl.BlockSpec(memory_space=HBM),
                  _row_spec(gw), _row_spec(gw), pl.BlockSpec((groups, ch, ch), lambda n: (0, 0, 0)),
                  pl.BlockSpec((ch, groups), lambda n: (0, 0))],
        out_specs=[zspec, pl.BlockSpec((groups, ch, ch), lambda n: (0, 0, 0)),
                   pl.BlockSpec((ch, LANES), lambda n: (0, 0)), _row_spec(gw), _row_spec(gw)],
        out_shape=[jax.ShapeDtypeStruct(dz_big.shape, BF16), jax.ShapeDtypeStruct((groups, ch, ch), F32),
                   jax.ShapeDtypeStruct((ch, LANES), F32), jax.ShapeDtypeStruct((1, gw), F32),
                   jax.ShapeDtypeStruct((1, gw), F32)],
        scratch_shapes=[pltpu.VMEM((ch, gw), F32)] * 2, input_output_aliases={2: 0})


def _mla_prep(z_lat, q_g, kv_g, rope_k, name):
    s, latw = z_lat.shape
    ql, kvl = q_g.shape[1], kv_g.shape[1]
    tb = _div_tile(s, TOKEN_BLOCK, SUBLANES)

    def body(z_ref, qg_ref, kvg_ref, t_ref, qn_ref, kvn_ref, kr_ref):
        q = z_ref[:, :ql]
        qn_ref[...] = ((q * lax.rsqrt(_rowmean(q * q) + EPS)) * qg_ref[...]).astype(BF16)
        kv = z_ref[:, ql:ql + kvl]
        kvn_ref[...] = ((kv * lax.rsqrt(_rowmean(kv * kv) + EPS)) * kvg_ref[...]).astype(BF16)
        kk = z_ref[:, ql + kvl:] * t_ref[...]
        kr_ref[...] = (kk + pltpu.roll(kk, ROPE, axis=1)).astype(BF16)

    return pl.pallas_call(
        body, name=name, grid=(s // tb,),
        in_specs=[pl.BlockSpec((tb, latw), lambda i: (i, 0)), _row_spec(ql), _row_spec(kvl),
                  pl.BlockSpec((tb, 2 * ROPE), lambda i: (i, 0))],
        out_specs=[pl.BlockSpec((tb, ql), lambda i: (i, 0)), pl.BlockSpec((tb, kvl), lambda i: (i, 0)),
                   pl.BlockSpec((tb, 2 * ROPE), lambda i: (i, 0))],
        out_shape=[jax.ShapeDtypeStruct((s, ql), BF16), jax.ShapeDtypeStruct((s, kvl), BF16),
                   jax.ShapeDtypeStruct((s, 2 * ROPE), BF16)],
        compiler_params=_params(),
    )(z_lat, q_g, kv_g, rope_k)


def _attn_fwd(q, kv, kr, heads, name, comm=None):
    s = q.shape[0]
    t = _div_tile(s, 512)
    nb = s // t
    hp = 2 if heads % 2 == 0 else 1

    def body(q_ref, k_ref, kr_ref, v_ref, o_ref, lse_ref, m_ref, l_ref, acc_ref):
        i, j = pl.program_id(1), pl.program_id(2)

        @pl.when(j == 0)
        def _():
            m_ref[...] = jnp.full(m_ref.shape, -1e30, F32)
            l_ref[...] = jnp.zeros(l_ref.shape, F32)
            acc_ref[...] = jnp.zeros(acc_ref.shape, F32)

        def update(h, rows, n_keys, on_diagonal):
            vc = slice(h * VHEAD, (h + 1) * VHEAD)
            k_full = jnp.concatenate([k_ref[:n_keys, h * NOPE:(h + 1) * NOPE], kr_ref[:n_keys, :]], axis=1)
            sc = _dot(q_ref[rows, h * HEAD_W:(h + 1) * HEAD_W], k_full, NT)
            if on_diagonal:
                row_pos = rows.start + lax.broadcasted_iota(jnp.int32, sc.shape, 0)
                sc = jnp.where(lax.broadcasted_iota(jnp.int32, sc.shape, 1) <= row_pos, sc, -1e30)
            m_old = m_ref[h, rows, :]
            m_new = jnp.maximum(m_old, jnp.max(sc, axis=-1, keepdims=True))
            p = jnp.exp(sc - m_new)
            alpha = jnp.exp(m_old - m_new)
            l_new = alpha * l_ref[h, rows, :] + jnp.sum(p, axis=-1, keepdims=True)
            acc = alpha * acc_ref[rows, vc] + _dot(p.astype(BF16), v_ref[:n_keys, vc], NN)
            if on_diagonal:
                o_ref[rows, vc] = (acc / l_new).astype(BF16)
                lse_ref[h, rows, :] = jnp.broadcast_to(m_new + jnp.log(l_new), (rows.stop - rows.start, LANES))
            else:
                m_ref[h, rows, :], l_ref[h, rows, :], acc_ref[rows, vc] = m_new, l_new, acc

        def below_diagonal():
            for h in range(hp):
                update(h, slice(0, t), t, False)

        def on_diagonal():
            for h in range(hp):
                update(h, slice(0, t // 2), t // 2, True)
                update(h, slice(t // 2, t), t, True)

        pl.when(j < i)(below_diagonal)
        pl.when(j == i)(on_diagonal)

    kidx = lambda off: (lambda h, i, j: (jnp.minimum(i, j), off(h)))
    return _call(
        body, (q, kv, kr, kv), comm, name=name, grid=(heads // hp, nb, nb),
        in_specs=[pl.BlockSpec((t, hp * HEAD_W), lambda h, i, j: (i, h)),
                  pl.BlockSpec((t, hp * NOPE), kidx(lambda h: h)),
                  pl.BlockSpec((t, 2 * ROPE), kidx(lambda h: 0)),
                  pl.BlockSpec((t, hp * VHEAD), kidx(lambda h: heads // hp + h))],
        out_specs=[pl.BlockSpec((t, hp * VHEAD), lambda h, i, j: (i, h)),
                   pl.BlockSpec((hp, t, LANES), lambda h, i, j: (h, i, 0))],
        out_shape=[jax.ShapeDtypeStruct((s, heads * VHEAD), BF16), jax.ShapeDtypeStruct((heads, s, LANES), F32)],
        scratch_shapes=[pltpu.VMEM((hp, t, 1), F32), pltpu.VMEM((hp, t, 1), F32), pltpu.VMEM((t, hp * VHEAD), F32)])


def _attn_bwd(q, kv, kr, o, do, lse, rope_q, heads, name, comm=None):
    s = q.shape[0]
    t = _div_tile(s, 512)
    nb = s // t
    hp = 2 if heads % 2 == 0 else 1

    def body(q_ref, k_ref, kr_ref, v_ref, o_ref, do_ref, lse_ref, tq_ref, dqb_ref, dk_ref, dv_ref, dk_acc, dv_acc,
             dq_ref):
        j, i = pl.program_id(1), pl.program_id(2)

        @pl.when(jnp.logical_and(j == 0, i == 0))
        def _():
            dq_ref[...] = jnp.zeros(dq_ref.shape, F32)

        def update(h, rows, n_keys, on_diagonal, assign):
            qc, kc, vc = (slice(h * w, (h + 1) * w) for w in (HEAD_W, NOPE, VHEAD))
            n_rows = rows.stop - rows.start
            qv, do_v = q_ref[rows, qc], do_ref[rows, vc]
            k_full = jnp.concatenate([k_ref[:n_keys, kc], kr_ref[:n_keys, :]], axis=1)
            sc = _dot(qv, k_full, NT)
            if on_diagonal:
                row_pos = rows.start + lax.broadcasted_iota(jnp.int32, sc.shape, 0)
                sc = jnp.where(lax.broadcasted_iota(jnp.int32, sc.shape, 1) <= row_pos, sc, -1e30)
            p = jnp.exp(sc - lse_ref[h, rows, :1])
            dp = _dot(do_v, v_ref[:n_keys, vc], NT)
            delta = jnp.sum(do_v.astype(F32) * o_ref[rows, vc].astype(F32), axis=-1, keepdims=True)
            ds = (p * (dp - delta)).astype(BF16)
            dq_ref[pl.ds(pl.multiple_of(i * t + rows.start, n_rows), n_rows), qc] += _dot(ds, k_full, NN)
            dv_part, dk_part = _dot(p.astype(BF16), do_v, TN), _dot(ds, qv, TN)
            if assign:
                dv_acc[:n_keys, vc], dk_acc[:n_keys, qc] = dv_part, dk_part
            else:
                dv_acc[:n_keys, vc] += dv_part
                dk_acc[:n_keys, qc] += dk_part

        def on_diagonal():
            for h in range(hp):
                update(h, slice(t // 2, t), t, True, True)
                update(h, slice(0, t // 2), t // 2, True, False)

        def below_diagonal():
            for h in range(hp):
                update(h, slice(0, t), t, False, False)

        pl.when(i == j)(on_diagonal)
        pl.when(i > j)(below_diagonal)

        @pl.when(i == nb - 1)
        def _():
            dk_ref[...] = dk_acc[...].astype(BF16)
            dv_ref[...] = dv_acc[...].astype(BF16)

        @pl.when(jnp.logical_and(j == nb - 1, i == nb - 1))
        def _():
            for h in range(hp):
                qc = slice(h * HEAD_W, (h + 1) * HEAD_W)
                dqb_ref[:, qc] = (dq_ref[:, qc] * tq_ref[...]).astype(BF16)

    qidx = lambda h, j, i: (jnp.maximum(i, j), h)
    return _call(
        body, (q, kv, kr, kv, o, do, lse, rope_q), comm, name=name, grid=(heads // hp, nb, nb),
        in_specs=[pl.BlockSpec((t, hp * HEAD_W), qidx),
                  pl.BlockSpec((t, hp * NOPE), lambda h, j, i: (j, h)),
                  pl.BlockSpec((t, 2 * ROPE), lambda h, j, i: (j, 0)),
                  pl.BlockSpec((t, hp * VHEAD), lambda h, j, i: (j, heads // hp + h)),
                  pl.BlockSpec((t, hp * VHEAD), qidx), pl.BlockSpec((t, hp * VHEAD), qidx),
                  pl.BlockSpec((hp, t, LANES), lambda h, j, i: (h, jnp.maximum(i, j), 0)),
                  pl.BlockSpec((s, HEAD_W), lambda h, j, i: (0, 0))],
        out_specs=[pl.BlockSpec((s, hp * HEAD_W), lambda h, j, i: (0, h)),
                   pl.BlockSpec((t, hp * HEAD_W), lambda h, j, i: (j, h)),
                   pl.BlockSpec((t, hp * VHEAD), lambda h, j, i: (j, h))],
        out_shape=[jax.ShapeDtypeStruct((s, heads * HEAD_W), BF16), jax.ShapeDtypeStruct((s, heads * HEAD_W), BF16),
                   jax.ShapeDtypeStruct((s, heads * VHEAD), BF16)],
        scratch_shapes=[pltpu.VMEM((t, hp * HEAD_W), F32), pltpu.VMEM((t, hp * VHEAD), F32),
                        pltpu.VMEM((s, hp * HEAD_W), F32)])


def _mla_bwd_mid(dk, dv, rope_k, heads, name):
    s = dk.shape[0]
    tb = _div_tile(s, TOKEN_BLOCK, SUBLANES)

    def body(dk_ref, dv_ref, tk_ref, dkv_ref, dkk_ref):
        dkr = jnp.zeros((tb, 2 * ROPE), F32)
        for h in range(heads):
            dkv_ref[:, h * NOPE:(h + 1) * NOPE] = dk_ref[:, h * HEAD_W:h * HEAD_W + NOPE]
            dkr = dkr + dk_ref[:, h * HEAD_W + NOPE:(h + 1) * HEAD_W].astype(F32)
        dkv_ref[:, heads * NOPE:] = dv_ref[...]
        dkk_ref[...] = (dkr + pltpu.roll(dkr, ROPE, axis=1)) * tk_ref[...]

    wq, wv = heads * HEAD_W, heads * VHEAD
    return pl.pallas_call(
        body, name=name, grid=(s // tb,),
        in_specs=[pl.BlockSpec((tb, wq), lambda i: (i, 0)), pl.BlockSpec((tb, wv), lambda i: (i, 0)),
                  pl.BlockSpec((tb, 2 * ROPE), lambda i: (i, 0))],
        out_specs=[pl.BlockSpec((tb, heads * NOPE + wv), lambda i: (i, 0)),
                   pl.BlockSpec((tb, 2 * ROPE), lambda i: (i, 0))],
        out_shape=[jax.ShapeDtypeStruct((s, heads * NOPE + wv), BF16), jax.ShapeDtypeStruct((s, 2 * ROPE), F32)],
        compiler_params=_params(),
    )(dk, dv, rope_k)


def _mla_bwd_post(z_lat, dqn, dkvn, dkk, q_g, kv_g, name):
    s, latw = z_lat.shape
    ql, kvl = q_g.shape[1], kv_g.shape[1]
    tb = _div_tile(s, TOKEN_BLOCK, SUBLANES)

    def norm_bwd(xv, dn, g, dg_ref):
        r = lax.rsqrt(_rowmean(xv * xv) + EPS)
        xh = xv * r
        _accumulate(dg_ref, _colsum(dn * xh))
        dxh = dn * g
        return r * (dxh - xh * _rowmean(dxh * xh))

    def body(z_ref, dqn_ref, dkvn_ref, dkk_ref, qg_ref, kvg_ref, dz_ref, gq_ref, gkv_ref):
        dz_ref[:, :ql] = norm_bwd(z_ref[:, :ql], dqn_ref[...], qg_ref[...], gq_ref).astype(BF16)
        dz_ref[:, ql:ql + kvl] = norm_bwd(z_ref[:, ql:ql + kvl], dkvn_ref[...], kvg_ref[...], gkv_ref).astype(BF16)
        dz_ref[:, ql + kvl:] = dkk_ref[...].astype(BF16)

    return pl.pallas_call(
        body, name=name, grid=(s // tb,),
        in_specs=[pl.BlockSpec((tb, latw), lambda i: (i, 0)), pl.BlockSpec((tb, ql), lambda i: (i, 0)),
                  pl.BlockSpec((tb, kvl), lambda i: (i, 0)), pl.BlockSpec((tb, 2 * ROPE), lambda i: (i, 0)),
                  _row_spec(ql), _row_spec(kvl)],
        out_specs=[pl.BlockSpec((tb, latw), lambda i: (i, 0)), _row_spec(ql), _row_spec(kvl)],
        out_shape=[jax.ShapeDtypeStruct((s, latw), BF16), jax.ShapeDtypeStruct((1, ql), F32),
                   jax.ShapeDtypeStruct((1, kvl), F32)],
        compiler_params=_params(),
    )(z_lat, dqn, dkvn, dkk, q_g, kv_g)


CONV_ROWS = 128
CONV_HALO = 16


def _row_steps(n_rows, step):
    step(0, True)
    if n_rows > CONV_ROWS:
        def later(i, carry):
            step(pl.multiple_of(i * CONV_ROWS, CONV_ROWS), False)
            return carry
        lax.fori_loop(1, n_rows // CONV_ROWS, later, 0)


def _conv_taps(pre_ref, r0, first):
    if first:
        win = jnp.concatenate([jnp.zeros((CONV_HALO, pre_ref.shape[1]), F32), pre_ref[0:CONV_ROWS, :].astype(F32)])
    else:
        win = pre_ref[pl.ds(pl.multiple_of(r0 - CONV_HALO, CONV_HALO), CONV_ROWS + CONV_HALO), :].astype(F32)
    return win[CONV_HALO:], pltpu.roll(win, 1, axis=0)[CONV_HALO:], pltpu.roll(win, 2, axis=0)[CONV_HALO:]


def _conv(taps, w_ref, b_ref):
    return w_ref[2:3, :] * taps[0] + w_ref[1:2, :] * taps[1] + w_ref[0:1, :] * taps[2] + b_ref[...]


def _conv_fwd(up_pre, conv_w, conv_b, name):
    s, ff2 = up_pre.shape
    ff = ff2 // 2
    tc = _div_tile(ff, 256)
    nb = ff // tc
    assert s % CONV_ROWS == 0

    def body(pg_ref, pv_ref, wg_ref, wv_ref, bg_ref, bv_ref, act_ref):
        def step(r0, first):
            gate = _conv(_conv_taps(pg_ref, r0, first), wg_ref, bg_ref)
            val = _conv(_conv_taps(pv_ref, r0, first), wv_ref, bv_ref)
            act_ref[pl.ds(r0, CONV_ROWS), :] = (gate * _sigmoid(gate) * val).astype(BF16)

        _row_steps(s, step)

    def col(rows, off):
        return pl.BlockSpec((rows, tc), lambda j: (0, j + off))

    return pl.pallas_call(
        body, name=name, grid=(nb,),
        in_specs=[col(s, 0), col(s, nb), col(CONV_TAPS, 0), col(CONV_TAPS, nb), col(1, 0), col(1, nb)],
        out_specs=col(s, 0), out_shape=jax.ShapeDtypeStruct((s, ff), BF16), compiler_params=_params(),
    )(up_pre, up_pre, conv_w, conv_w, conv_b, conv_b)


def _conv_bwd(up_pre, dact, conv_w, conv_b, name, comm=None):
    s, ff2 = up_pre.shape
    ff = ff2 // 2
    tc = _div_tile(ff, 256)
    nb = ff // tc
    assert s % CONV_ROWS == 0

    def body(pg_ref, pv_ref, da_ref, wg_ref, wv_ref, bg_ref, bv_ref, dup_ref, gwg_ref, gwv_ref, gbg_ref, gbv_ref,
             dxg_ref, dxv_ref):
        for ref in (gwg_ref, gwv_ref, gbg_ref, gbv_ref):
            ref[...] = jnp.zeros(ref.shape, F32)
        for ref in (dxg_ref, dxv_ref):
            ref[s:s + SUBLANES, :] = jnp.zeros((SUBLANES, tc), F32)

        def sums(taps, dx, gw_ref, gb_ref):
            gb_ref[...] += _colsum(dx)
            for k in range(CONV_TAPS):
                gw_ref[k:k + 1, :] += _colsum(dx * taps[CONV_TAPS - 1 - k])

        def forward(r0, first):
            rows = pl.ds(r0, CONV_ROWS)
            taps_g, taps_v = _conv_taps(pg_ref, r0, first), _conv_taps(pv_ref, r0, first)
            gate, val = _conv(taps_g, wg_ref, bg_ref), _conv(taps_v, wv_ref, bv_ref)
            da = da_ref[rows, :].astype(F32)
            sg = _sigmoid(gate)
            dxv, dxg = da * gate * sg, da * val * sg * (1.0 + gate * (1.0 - sg))
            dxv_ref[rows, :], dxg_ref[rows, :] = dxv, dxg
            sums(taps_v, dxv, gwv_ref, gbv_ref)
            sums(taps_g, dxg, gwg_ref, gbg_ref)

        def backward(r0, first):
            del first
            n = CONV_ROWS + SUBLANES
            for dx_ref, w_ref, out_ref in ((dxg_ref, wg_ref, dup_ref.at[0]), (dxv_ref, wv_ref, dup_ref.at[1])):
                win = dx_ref[pl.ds(r0, n), :]
                ahead1 = pltpu.roll(win, n - 1, axis=0)[:CONV_ROWS]
                ahead2 = pltpu.roll(win, n - 2, axis=0)[:CONV_ROWS]
                out_ref[pl.ds(r0, CONV_ROWS), :] = (w_ref[2:3, :] * win[:CONV_ROWS] + w_ref[1:2, :] * ahead1
                                                    + w_ref[0:1, :] * ahead2).astype(BF16)

        _row_steps(s, forward)
        _row_steps(s, backward)

    def col(rows, off):
        return pl.BlockSpec((rows, tc), lambda j: (0, j + off))

    return _call(
        body, (up_pre, up_pre, dact, conv_w, conv_w, conv_b, conv_b), comm, name=name, grid=(nb,),
        in_specs=[col(s, 0), col(s, nb), col(s, 0), col(CONV_TAPS, 0), col(CONV_TAPS, nb), col(1, 0), col(1, nb)],
        out_specs=[pl.BlockSpec((2, s, tc), lambda j: (0, 0, j)), col(CONV_TAPS, 0), col(CONV_TAPS, 0),
                   col(1, 0), col(1, 0)],
        out_shape=[jax.ShapeDtypeStruct((2, s, ff), BF16)] + [jax.ShapeDtypeStruct((CONV_TAPS, ff), F32)] * 2
        + [jax.ShapeDtypeStruct((1, ff), F32)] * 2,
        scratch_shapes=[pltpu.VMEM((s + SUBLANES, tc), F32)] * 2)


def _ada_fwd(c_all, w, b, name):
    nseq, d = c_all.shape
    na = w.shape[1]
    tn = _div_tile(na, 512)

    def body(c_ref, w_ref, b_ref, o_ref):
        cv = c_ref[...]
        sc = cv * _sigmoid(cv)
        o_ref[...] = jnp.dot(sc, w_ref[...], preferred_element_type=F32, precision=lax.Precision.HIGHEST) + b_ref[...]

    return pl.pallas_call(
        body, name=name, grid=(na // tn,),
        in_specs=[pl.BlockSpec((nseq, d), lambda j: (0, 0)), pl.BlockSpec((d, tn), lambda j: (0, j)),
                  pl.BlockSpec((1, tn), lambda j: (0, j))],
        out_specs=pl.BlockSpec((nseq, tn), lambda j: (0, j)),
        out_shape=jax.ShapeDtypeStruct((nseq, na), F32), compiler_params=_params(),
    )(c_all, w, b)


def _ada_bwd(c_all_t, dmod, name):
    d, nseq = c_all_t.shape
    na = dmod.shape[1]
    tm, tn = _div_tile(d, 512, SUBLANES), _div_tile(na, 1024)

    def body(c_ref, dm_ref, o_ref):
        cv = c_ref[...]
        o_ref[...] = jnp.dot(cv * _sigmoid(cv), dm_ref[...], preferred_element_type=F32,
                             precision=lax.Precision.HIGHEST)

    return pl.pallas_call(
        body, name=name, grid=(d // tm, na // tn),
        in_specs=[pl.BlockSpec((tm, nseq), lambda i, j: (i, 0)), pl.BlockSpec((nseq, tn), lambda i, j: (0, j))],
        out_specs=pl.BlockSpec((tm, tn), lambda i, j: (i, j)),
        out_shape=jax.ShapeDtypeStruct((d, na), F32), compiler_params=_params(),
    )(c_all_t, dmod)


def _adamw(w, g, m, v, name, comm=None, after=None, emit_grad=False):
    rows, cols = w.shape
    n_out = 4 if emit_grad else 3
    tb = _div_tile(rows, max(SUBLANES, ADAMW_BLOCK_ELEMS // cols // SUBLANES * SUBLANES), SUBLANES)
    c1 = 1.0 / (1.0 - ADAM_B1 ** ADAM_STEP)
    c2 = 1.0 / (1.0 - ADAM_B2 ** ADAM_STEP)

    def body(*refs):
        w_ref, g_ref, m_ref, v_ref = refs[:4]
        d_ref, nm_ref, nv_ref = refs[-3:]
        gv = g_ref[...]
        if emit_grad:
            refs[-4][...] = gv
        nm =ADAM_B1 * m_ref[...] + (1.0 - ADAM_B1) * gv
        nv = ADAM_B2 * v_ref[...] + (1.0 - ADAM_B2) * (gv * gv)
        nm_ref[...] = nm
        nv_ref[...] = nv
        d_ref[...] = -ADAM_LR * ((nm * c1) / (jnp.sqrt(nv * c2) + ADAM_EPS) + ADAM_WD * w_ref[...])

    blk = pl.BlockSpec((tb, cols), lambda i: (i, 0))
    operands, in_specs = (w, g, m, v), [blk] * 4
    if after is not None:
        operands, in_specs = operands + (after,), in_specs + [pl.BlockSpec(after.shape, lambda i: (0, 0))]
    return _call(body, operands, comm, name=name, grid=(rows // tb,), in_specs=in_specs, out_specs=[blk] * n_out,
                 out_shape=[jax.ShapeDtypeStruct((rows, cols), F32)] * n_out)


def _sum_leading(parts, name, after=()):
    n, rows, cols = parts.shape
    tb = _div_tile(rows, 512, SUBLANES)

    def body(p_ref, *rest):
        o_ref = rest[-1]
        acc = p_ref[0]
        for k in range(1, n):
            acc = acc + p_ref[k]
        o_ref[...] = acc

    return pl.pallas_call(
        body, name=name, grid=(rows // tb,),
        in_specs=[pl.BlockSpec((n, tb, cols), lambda i: (0, i, 0))] + [pl.BlockSpec(memory_space=pl.ANY)] * len(after),
        out_specs=pl.BlockSpec((tb, cols), lambda i: (i, 0)),
        out_shape=jax.ShapeDtypeStruct((rows, cols), F32), compiler_params=_params(),
    )(parts, *after)


def _place():
    x, y, c = lax.axis_index("x"), lax.axis_index("y"), lax.axis_index("c")
    return x, y, c, [(1 - x, y), (x, 1 - y), (1 - x, 1 - y)]


def _all_gather(block, name):
    m_per, n = block.shape

    def body(x_ref, out_ref, send_sems, recv_sems, local_sem):
        x, y, c, chips = _place()
        me, sibling = (x, y, c), (x, y, 1 - c)

        def rows(px, py, pc):
            return out_ref.at[pl.ds((4 * px + 2 * py + pc) * m_per, m_per), :]

        def copy(k, blk, to, src=None):
            return pltpu.make_async_remote_copy(
                src_ref=rows(*blk) if src is None else src, dst_ref=rows(*blk), send_sem=send_sems.at[k],
                recv_sem=recv_sems.at[k], device_id=to, device_id_type=MESH)

        mine = pltpu.make_async_copy(x_ref, rows(*me), local_sem)
        mine.start()
        first = [copy(0, me, sibling, src=x_ref)]
        first += [copy(1 + j, me, (*chip, c), src=x_ref) for j, chip in enumerate(chips)]
        for cp in first:
            cp.start()
        passed = [copy(4 + j, (*chip, c), sibling) for j, chip in enumerate(chips)]
        for j, chip in enumerate(chips):
            copy(1 + j, (*chip, c), me).wait_recv()
            passed[j].start()
        copy(0, sibling, me).wait_recv()
        for j, chip in enumerate(chips):
            copy(4 + j, (*chip, 1 - c), me).wait_recv()
        for cp in first + passed:
            cp.wait_send()
        mine.wait()

    return pl.pallas_call(
        body, name=name, out_shape=jax.ShapeDtypeStruct((N_DEV * m_per, n), block.dtype),
        in_specs=[pl.BlockSpec(memory_space=pltpu.VMEM)], out_specs=pl.BlockSpec(memory_space=pltpu.VMEM),
        scratch_shapes=[pltpu.SemaphoreType.DMA((7,)), pltpu.SemaphoreType.DMA((7,)), pltpu.SemaphoreType.DMA],
        compiler_params=_params(),
    )(block)


def _hbm_specs(n):
    return [pl.BlockSpec(memory_space=HBM)] * n


def _part(ref, by_cols, half, quarter=None, lead=None):
    extent = ref.shape[-1] if by_cols else ref.shape[-2]
    size = extent // 2 if quarter is None else extent // 4
    first = half * (extent // 2) + (0 if quarter is None else quarter * size)
    tile = LANES if by_cols else 2 * SUBLANES
    span = pl.ds(pl.multiple_of(first, tile) if size % tile == 0 else first, size)
    index = (slice(None), span) if by_cols else (span, slice(None))
    return ref.at[index] if lead is None else ref.at[(lead,) + index]


def _half_rows(ref, half, lead=None):
    return _part(ref, False, half, lead=lead)


class _Comm:
    def __init__(self, operands, out_shape, sem_dims, build, aliases=None):
        self.operands, self.out_shape, self.sem_dims = list(operands), list(out_shape), list(sem_dims)
        self.scratch = [pltpu.SemaphoreType.DMA(d) for d in sem_dims]
        self.build, self.aliases = build, dict(aliases or {})


class _SemGrid:
    def __init__(self, sems, dims):
        self.sems, self.dims, self.at = list(sems), tuple(dims), self

    def __getitem__(self, index):
        index = index if isinstance(index, tuple) else (index,)
        flat = 0
        for i, d in zip(index, self.dims):
            flat = flat * d + i
        return self.sems[flat]


def _call(body, operands, comm=None, *, name, grid, in_specs, out_specs, out_shape, scratch_shapes=(),
          input_output_aliases=None):
    aliases = dict(input_output_aliases or {})
    if comm is None:
        return pl.pallas_call(
            body, name=name, grid=grid, in_specs=in_specs, out_specs=out_specs, out_shape=out_shape,
            scratch_shapes=list(scratch_shapes), input_output_aliases=aliases, compiler_params=_params())(*operands)
    single = not isinstance(out_shape, (list, tuple))
    outs = [out_shape] if single else list(out_shape)
    ospecs = [out_specs] if single else list(out_specs)
    n_in, n_out, n_scr = len(operands), len(outs), len(scratch_shapes)
    c_in, c_out = len(comm.operands), len(comm.out_shape)
    for i, o in comm.aliases.items():
        aliases[n_in + i] = n_out + o

    def hosted(*refs):
        ins, c_ins = refs[:n_in], refs[n_in:n_in + c_in]
        o0 = n_in + c_in
        o_refs, c_outs = refs[o0:o0 + n_out], refs[o0 + n_out:o0 + n_out + c_out]
        s0 = o0 + n_out + c_out
        scr, sems = refs[s0:s0 + n_scr], refs[s0 + n_scr:]
        stages = comm.build(c_ins, c_outs, sems)
        step, n_steps = 0, 1
        for dim, size in enumerate(grid):
            step, n_steps = step * size + pl.program_id(dim), n_steps * size
        pl.when(step == 0)(stages[0])
        body(*ins, *o_refs, *scr)
        for stage in stages[1:-1]:
            pl.when(step == (n_steps * MIDDLE_STAGE_AT) // 100)(stage)
        pl.when(step == n_steps - 1)(stages[-1])

    res = pl.pallas_call(
        hosted, name=name, grid=grid, in_specs=list(in_specs) + _hbm_specs(c_in),
        out_specs=ospecs + _hbm_specs(c_out), out_shape=outs + comm.out_shape,
        scratch_shapes=list(scratch_shapes) + comm.scratch, input_output_aliases=aliases,
        compiler_params=_params())(*operands, *comm.operands)
    return (res[0] if single else res[:n_out]), res[n_out:]


def _run_comm(comm, name):
    c_in, c_out = len(comm.operands), len(comm.out_shape)

    def body(*refs):
        for stage in comm.build(refs[:c_in], refs[c_in:c_in + c_out], refs[c_in + c_out:]):
            stage()

    return pl.pallas_call(
        body, name=name, in_specs=_hbm_specs(c_in), out_specs=_hbm_specs(c_out), out_shape=comm.out_shape,
        scratch_shapes=comm.scratch, input_output_aliases=comm.aliases, compiler_params=_params())(*comm.operands)


def _join_comms(comms):
    def build(in_refs, out_refs, sems):
        staged, i, o, k = [], 0, 0, 0
        for cm in comms:
            ni, no, ns = len(cm.operands), len(cm.out_shape), len(cm.sem_dims)
            staged.append(cm.build(in_refs[i:i + ni], out_refs[o:o + no], sems[k:k + ns]))
            i, o, k = i + ni, o + no, k + ns
        def run(fns):
            def stage():
                for fn in fns:
                    fn()
            return stage

        return (run([st[0] for st in staged]), run([fn for st in staged for fn in st[1:-1]]),
                run([st[-1] for st in staged]))

    aliases, i, o = {}, 0, 0
    for cm in comms:
        aliases.update({i + a: o + b for a, b in cm.aliases.items()})
        i, o = i + len(cm.operands), o + len(cm.out_shape)
    return _Comm(sum((cm.operands for cm in comms), []), sum((cm.out_shape for cm in comms), []),
                 sum((cm.sem_dims for cm in comms), []), build, aliases)


def _gather8_comm(block):
    def build(in_refs, out_refs, sems):
        (src,), (out,), (send_sems, recv_sems) = in_refs, out_refs, sems
        x, y, c, chips = _place()
        me, sibling = (x, y, c), (x, y, 1 - c)

        def copy(k, blk, to, own=False):
            dst = out.at[4 * blk[0] + 2 * blk[1] + blk[2]]
            return pltpu.make_async_remote_copy(
                src_ref=src if own else dst, dst_ref=dst, send_sem=send_sems.at[k], recv_sem=recv_sems.at[k],
                device_id=to, device_id_type=MESH)

        first = [copy(0, me, sibling, own=True)] + [copy(1 + j, me, (*chip, c), own=True)
                                                     for j, chip in enumerate(chips)]
        passed = [copy(4 + j, (*chip, c), sibling) for j, chip in enumerate(chips)]

        def start():
            for cp in first:
                cp.start()

        def middle():
            for j, chip in enumerate(chips):
                copy(1 + j, (*chip, c), me).wait_recv()
                passed[j].start()

        def finish():
            copy(0, sibling, me).wait_recv()
            for j, chip in enumerate(chips):
                copy(4 + j, (*chip, 1 - c), me).wait_recv()
            for cp in first + passed:
                cp.wait_send()

        return start, middle, finish

    return _Comm([block], [jax.ShapeDtypeStruct((N_DEV,) + block.shape, block.dtype)], [(7,), (7,)], build)


def _gather_comm(shards, by_cols=()):
    nw = len(shards)

    def build(in_refs, out_refs, sems):
        send_sems, recv_sems = sems
        x, y, c, chips = _place()
        me, sibling = (x, y, c), (x, y, 1 - c)
        across_x, across_y, diagonal = chips

        def copy(w, k, block, part, to, src=None):
            dst = _part(out_refs[w], w in by_cols, part[1], part[2] if part[0] else None, 2 * block[0] + block[1])
            return pltpu.make_async_remote_copy(
                src_ref=dst if src is None else src, dst_ref=dst, send_sem=send_sems.at[w, k],
                recv_sem=recv_sems.at[w, k], device_id=to, device_id_type=MESH)

        first = [copy(w, j, (x, y), (0, c), (*chip, c), src=_part(in_refs[w], w in by_cols, c))
                 for w in range(nw) for j, chip in enumerate((across_x, across_y))]
        first += [pltpu.make_async_remote_copy(
            src_ref=in_refs[w], dst_ref=out_refs[w].at[2 * x + y], send_sem=send_sems.at[w, 8],
            recv_sem=recv_sems.at[w, 8], device_id=sibling, device_id_type=MESH) for w in range(nw)]
        passed = [[copy(w, 2, across_x, (1, c, 0), (*across_y, c)), copy(w, 3, across_y, (1, c, 1), (*across_x, c)),
                   copy(w, 4, across_x, (0, c), sibling), copy(w, 5, across_y, (0, c), sibling)] for w in range(nw)]
        last = [[copy(w, 6, diagonal, (1, c, 0), sibling), copy(w, 7, diagonal, (1, c, 1), sibling)]
                for w in range(nw)]

        def start():
            for cp in first:
                cp.start()

        def middle():
            for w in range(nw):
                copy(w, 0, across_x, (0, c), me).wait_recv()
                copy(w, 1, across_y, (0, c), me).wait_recv()
                for cp in passed[w]:
                    cp.start()

        def finish():
            for w in range(nw):
                copy(w, 2, diagonal, (1, c, 0), me).wait_recv()
                copy(w, 3, diagonal, (1, c, 1), me).wait_recv()
                for cp in last[w]:
                    cp.start()
            for w in range(nw):
                for k, block, part in ((4, across_x, (0, 1 - c)), (5, across_y, (0, 1 - c)),
                                       (6, diagonal, (1, 1 - c, 0)), (7, diagonal, (1, 1 - c, 1))):
                    copy(w, k, block, part, me).wait_recv()
                pltpu.make_async_remote_copy(
                    src_ref=in_refs[w], dst_ref=out_refs[w].at[2 * x + y], send_sem=send_sems.at[w, 8],
                    recv_sem=recv_sems.at[w, 8], device_id=sibling, device_id_type=MESH).wait_recv()
            for cp in first + sum(passed, []) + sum(last, []):
                cp.wait_send()

        return start, middle, finish

    return _Comm(shards, [jax.ShapeDtypeStruct((N_CHIPS,) + w.shape, w.dtype) for w in shards],
                 [(nw, 9), (nw, 9)], build)


def _halved(shape, by_cols):
    return shape[:-1] + (shape[-1] // 2,) if by_cols else shape[:-2] + (shape[-2] // 2, shape[-1])


def _swap_comm(gs, by_cols=()):
    nw = len(gs)

    def build(in_refs, out_refs, sems):
        send_sems, recv_sems = sems
        x, y, c, _ = _place()
        cps = []
        for w in range(nw):
            cps.append(pltpu.make_async_remote_copy(
                src_ref=_part(in_refs[w], w in by_cols, 1 - c, lead=slice(None)), dst_ref=out_refs[w],
                send_sem=send_sems.at[w], recv_sem=recv_sems.at[w], device_id=(x, y, 1 - c), device_id_type=MESH))

        def start():
            for cp in cps:
                cp.start()

        def finish():
            for cp in cps:
                cp.wait()

        return start, finish

    return _Comm(gs, [jax.ShapeDtypeStruct(_halved(g.shape, w in by_cols), g.dtype) for w, g in enumerate(gs)],
                 [(nw,), (nw,)], build)


def _exchange_comm(s1s):
    nw = len(s1s)

    def build(in_refs, out_refs, sems):
        send_sems, recv_sems = sems
        x, y, c, chips = _place()
        cps = [pltpu.make_async_remote_copy(
            src_ref=in_refs[w].at[2 * chip[0] + chip[1]], dst_ref=out_refs[w].at[j], send_sem=send_sems.at[w, j],
            recv_sem=recv_sems.at[w, j], device_id=(*chip, c), device_id_type=MESH)
            for w in range(nw) for j, chip in enumerate(chips)]

        def start():
            for cp in cps:
                cp.start()

        def finish():
            for cp in cps:
                cp.wait()

        return start, finish

    return _Comm(s1s, [jax.ShapeDtypeStruct((N_CHIPS - 1,) + s.shape[1:], s.dtype) for s in s1s],
                 [(nw, 3), (nw, 3)], build)


def _size(dims):
    n = 1
    for d in dims:
        n *= d
    return n


def _sem_grids(comm, sem_refs):
    grids, pos = [], 0
    for dims in comm.sem_dims:
        grids.append(_SemGrid(sem_refs[pos:pos + _size(dims)], dims))
        pos += _size(dims)
    return grids


def _comm_split_start(comm, name, after=()):
    c_in, c_out = len(comm.operands), len(comm.out_shape)
    counts = [_size(d) for d in comm.sem_dims]
    n_sem = sum(counts)
    assert not comm.aliases

    def body(*refs):
        srcs, lands = refs[:c_in], refs[c_in:c_in + c_out]
        first_sem = c_in + c_out + len(after)
        start, _ = comm.build(srcs, lands, _sem_grids(comm, refs[first_sem:first_sem + n_sem]))
        start()
        refs[-1][...] = jnp.zeros(refs[-1].shape, refs[-1].dtype)

    lands = [pltpu.with_memory_space_constraint(lax.empty(o.shape, o.dtype), HBM) for o in comm.out_shape]
    srcs = [pltpu.with_memory_space_constraint(a, HBM) for a in comm.operands]
    res = pl.pallas_call(
        body, name=name, in_specs=_hbm_specs(c_in + c_out) + [pl.BlockSpec(memory_space=pl.ANY)] * len(after),
        out_specs=[pl.BlockSpec(memory_space=pltpu.SEMAPHORE)] * n_sem + _hbm_specs(c_in + c_out)
        + [pl.BlockSpec(memory_space=pltpu.VMEM)],
        out_shape=[pltpu.SemaphoreType.DMA(())] * n_sem + [pltpu.HBM(a.shape, a.dtype) for a in comm.operands]
        + [pltpu.HBM(o.shape, o.dtype) for o in comm.out_shape] + [jax.ShapeDtypeStruct((SUBLANES, LANES), F32)],
        input_output_aliases={i: n_sem + i for i in range(c_in + c_out)},
        compiler_params=_params(has_side_effects=pltpu.SideEffectType.DATAFLOW_SIDE_EFFECTING))(*srcs, *lands, *after)
    return res[:-1], res[-1]


def _comm_split_wait(comm, state, after, name):
    c_in, c_out, n_sem = len(comm.operands), len(comm.out_shape), sum(_size(d) for d in comm.sem_dims)
    sems, srcs, lands = state[:n_sem], state[n_sem:n_sem + c_in], state[n_sem + c_in:]

    def body(*refs):
        src_refs, land_refs = refs[:c_in], refs[c_in:c_in + c_out]
        _, finish = comm.build(src_refs, land_refs, _sem_grids(comm, refs[c_in + c_out:c_in + c_out + n_sem]))
        finish()

    sem_spec = pl.BlockSpec(memory_space=pltpu.SEMAPHORE)
    res = pl.pallas_call(
        body, name=name, in_specs=_hbm_specs(c_in + c_out) + [sem_spec] * n_sem + [pl.BlockSpec(memory_space=pl.ANY)],
        out_specs=_hbm_specs(c_in + c_out),
        out_shape=[pltpu.HBM(a.shape, a.dtype) for a in srcs] + [pltpu.HBM(o.shape, o.dtype) for o in lands],
        input_output_aliases={i: i for i in range(c_in + c_out)},
        compiler_params=_params(has_side_effects=pltpu.SideEffectType.DATAFLOW_SIDE_EFFECTING),
    )(*srcs, *lands, *sems, after)
    return res[:c_in], res[c_in:]


def _share_comm(fs, by_cols=()):
    nw = len(fs)

    def build(in_refs, out_refs, sems):
        del in_refs
        send_sems, recv_sems = sems
        x, y, c, _ = _place()

        def copy(w, half):
            part = _part(out_refs[w], w in by_cols, half)
            return pltpu.make_async_remote_copy(
                src_ref=part, dst_ref=part, send_sem=send_sems.at[w], recv_sem=recv_sems.at[w],
                device_id=(x, y, 1 - c), device_id_type=MESH)

        sends = [copy(w, c) for w in range(nw)]

        def start():
            for cp in sends:
                cp.start()

        def finish():
            for w in range(nw):
                copy(w, 1 - c).wait_recv()
            for cp in sends:
                cp.wait_send()

        return start, finish

    return _Comm(fs, [jax.ShapeDtypeStruct(f.shape, f.dtype) for f in fs],
                 [(nw,), (nw,)], build,
                 aliases={w: w for w in range(nw)})


def _add_sibling(g, r1, place, name, by_cols=False):
    nch, h, cols = r1.shape
    tr = _div_tile(h, 1024 if by_cols else 512, 2 * SUBLANES)
    nb = h // tr
    mine = (lambda k, i, p: (k, i, p[0])) if by_cols else (lambda k, i, p: (k, p[0] * nb + i, 0))

    def body(place_ref, g_ref, r_ref, o_ref):
        del place_ref
        o_ref[...] = (g_ref[...].astype(F32) + r_ref[...].astype(F32)).astype(BF16)

    spec = pltpu.PrefetchScalarGridSpec(
        num_scalar_prefetch=1, grid=(nch, nb),
        in_specs=[pl.BlockSpec((None, tr, cols), mine), pl.BlockSpec((None, tr, cols), lambda k, i, p: (k, i, 0))],
        out_specs=pl.BlockSpec((None, tr, cols), lambda k, i, p: (k, i, 0)))
    return pl.pallas_call(body, name=name, grid_spec=spec, out_shape=jax.ShapeDtypeStruct((nch, h, cols), BF16),
                          compiler_params=_params())(place, g, r1)


def _add_chips(s1, r2, place, name, by_cols=False):
    _, h, cols = s1.shape
    tr = _div_tile(h, 1024 if by_cols else 512, 2 * SUBLANES)
    nb = h // tr
    mine = (lambda i, p: (i, p[0])) if by_cols else (lambda i, p: (p[0] * nb + i, 0))
    whole = (h, 2 * cols) if by_cols else (2 * h, cols)

    def body(place_ref, s_ref, r_ref, o_ref):
        del place_ref
        acc = s_ref[...].astype(F32)
        for j in range(N_CHIPS - 1):
            acc = acc + r_ref[j].astype(F32)
        o_ref[...] = acc

    spec = pltpu.PrefetchScalarGridSpec(
        num_scalar_prefetch=1, grid=(nb,),
        in_specs=[pl.BlockSpec((None, tr, cols), lambda i, p: (p[1], i, 0)),
                  pl.BlockSpec((N_CHIPS - 1, tr, cols), lambda i, p: (0, i, 0))],
        out_specs=pl.BlockSpec((tr, cols), mine))
    return pl.pallas_call(body, name=name, grid_spec=spec, out_shape=jax.ShapeDtypeStruct(whole, F32),
                          compiler_params=_params())(place, s1, r2)


def _quarter_turn(m):
    h = m.shape[-1] // 2
    return jnp.concatenate([-m[..., h:], m[..., :h]], axis=-1)


def _quarter_turn_back(m):
    h = m.shape[-1] // 2
    return jnp.concatenate([m[..., h:], -m[..., :h]], axis=-1)


def _stack_rows(parts):
    out = lax.empty((sum(p.shape[0] for p in parts),) + parts[0].shape[1:], parts[0].dtype)
    row = 0
    for p in parts:
        out = lax.dynamic_update_slice(out, p, (row, 0))
        row += p.shape[0]
    return out


def _join_cols(sh):
    return jnp.concatenate([sh[k] for k in range(N_CHIPS)], axis=1)


def _split_cols(full):
    c = full.shape[1] // N_CHIPS
    return jnp.stack([full[:, k * c:(k + 1) * c] for k in range(N_CHIPS)])


def kernel(x, c, positions, w_ada, b_ada, pre_norm1_g, w_in, gm_ln_g, gm_ln_b, gm_w_s, gm_b_s, w_branch_a, q_norm_g, w_uq, kv_norm_g, w_ukv, w_branch_b, w_out, post_norm1_g, pre_norm2_g, w_up, conv_w, conv_b, w_down, post_norm2_g, loss_target, m_w_ada, m_b_ada, m_pre_norm1_g, m_w_in, m_gm_ln_g, m_gm_ln_b, m_gm_w_s, m_gm_b_s, m_w_branch_a, m_q_norm_g, m_w_uq, m_kv_norm_g, m_w_ukv, m_w_branch_b, m_w_out, m_post_norm1_g, m_pre_norm2_g, m_w_up, m_conv_w, m_conv_b, m_w_down, m_post_norm2_g, v_w_ada, v_b_ada, v_pre_norm1_g, v_w_in, v_gm_ln_g, v_gm_ln_b, v_gm_w_s, v_gm_b_s, v_w_branch_a, v_q_norm_g, v_w_uq, v_kv_norm_g, v_w_ukv, v_w_branch_b, v_w_out, v_post_norm1_g, v_pre_norm2_g, v_w_up, v_conv_w, v_conv_b, v_w_down, v_post_norm2_g):
    given = dict(locals())
    s, d = x.shape[1], x.shape[2]
    gw = gm_ln_g.shape[0]
    ql, kvl = q_norm_g.shape[0], kv_norm_g.shape[0]
    heads = N_CHIPS * w_uq.shape[1] // (NOPE + ROPE)
    ff = N_CHIPS * w_down.shape[0]
    assert gw == d and N_CHIPS * w_ukv.shape[1] == heads * (NOPE + VHEAD)
    ix, iy, ic = lax.axis_index("x"), lax.axis_index("y"), lax.axis_index("c")
    chip = 2 * ix + iy
    dev = 2 * chip + ic
    row = lambda v: v.reshape(1, -1)

    first = _all_gather(jnp.concatenate([jnp.pad(c, ((0, SUBLANES - 1), (0, 0))),
                                         jnp.pad(conv_w, ((0, SUBLANES - CONV_TAPS), (0, 0)))], axis=1), "gather_c")
    first = first.reshape(N_DEV, SUBLANES, d + conv_w.shape[1])
    c_all = first[:, 0, :d]
    conv_wf = first[::N_CORES, :CONV_TAPS, d:].transpose(1, 0, 2).reshape(CONV_TAPS, N_CHIPS * conv_w.shape[1])
    na = w_ada.shape[1]
    b_ada_mine = lax.dynamic_slice(b_ada, (chip * na,), (na,))
    mod_cols = _ada_fwd(c_all, w_ada, row(b_ada_mine), "ada_fwd")
    mod_all = _all_gather(mod_cols, "gather_mod").reshape(N_CHIPS, N_CORES, N_DEV, na)[:, 0]
    mod = lax.dynamic_index_in_dim(mod_all, dev, axis=1, keepdims=False).reshape(N_MOD, d)
    shift1, scale1, gate1, shift2, scale2, gate2 = (mod[i:i + 1] for i in range(N_MOD))

    mine = {n: (given[n].T if n == "w_in" else given[n]).astype(BF16) for n in BIG}
    gather = lambda names: _gather_comm([mine[n] for n in names], [i for i, n in enumerate(names) if n == "w_in"])
    whole = lambda n, g: g
    rows4 = lambda sh4: sh4.reshape(-1, sh4.shape[2])
    wi_t = rows4(whole("w_in", _run_comm(gather(["w_in"]), "gather_w_in")[0]))
    o_q, o_kv, o_pe, o_ga = 2 * gw, 2 * gw + ql, 2 * gw + ql + kvl, 2 * gw + ql + kvl + ROPE
    w_in_big_t = _stack_rows([wi_t[:o_q], wi_t[o_ga:]])
    w_in_lat_t = _stack_rows([wi_t[o_q:o_ga], _quarter_turn(wi_t[o_pe:o_ga].T).T])

    inv = ROPE_THETA ** (-jnp.arange(0, ROPE, 2, dtype=F32) / ROPE)
    ang = positions[0].astype(F32)[:, None] * inv
    cos, sin = jnp.cos(ang), jnp.sin(ang)
    rope_k = jnp.concatenate([cos, cos, sin, sin], axis=1)
    softmax_scale = float(NOPE + ROPE) ** -0.5
    rope_q = jnp.concatenate([jnp.ones((s, NOPE), F32), rope_k], axis=1) * softmax_scale

    x2d, tgt = x[0], loss_target[0]
    g_pre1, g_post1, g_pre2, g_post2 = row(pre_norm1_g), row(post_norm1_g), row(pre_norm2_g), row(post_norm2_g)
    ln_g, ln_b, q_g, kv_g = row(gm_ln_g), row(gm_ln_b), row(q_norm_g), row(kv_norm_g)
    b_s_t = gm_b_s.T
    conv_bf = row(conv_b)

    h1 = _prenorm(x2d, g_pre1, scale1, shift1, "prenorm1")
    z_big, (g_uq, g_ukv, g_a) = _matmul(h1, w_in_big_t, mode="nt", out_dtype=BF16, name="mm_z_big", tm=s,
                                        comm=gather(["w_uq", "w_ukv", "w_branch_a"]))
    wq = _join_cols(whole("w_uq", g_uq)).reshape(ql, heads, NOPE + ROPE)
    w_q = jnp.concatenate([wq, _quarter_turn(wq[:, :, NOPE:])], axis=2).reshape(ql, heads * HEAD_W)
    w_kv = _join_cols(whole("w_ukv", g_ukv)).reshape(kvl, heads, 2, NOPE).transpose(0, 2, 1, 3)
    w_kv = w_kv.reshape(kvl, 2 * heads * NOPE)
    w_a = rows4(whole("w_branch_a", g_a))
    z_lat = _matmul(h1, w_in_lat_t, mode="nt", out_dtype=F32, name="mm_z_lat", tm=s, tn=1024)
    a_act = _gmlp_fwd(z_big, ln_g, ln_b, gm_w_s, b_s_t, "gmlp_fwd")
    qn, kvn, kr = _mla_prep(z_lat, q_g, kv_g, rope_k, "mla_prep")
    q_rot = _matmul(qn, w_q, mode="nn", out_dtype=BF16, name="mm_q", tm=s, tn=HEAD_W, mul=rope_q)
    kv_all = _matmul(kvn, w_kv, mode="nn", out_dtype=BF16, name="mm_kv", tm=s, tn=1024)
    (o_att, lse), (g_b, g_o, g_up) = _attn_fwd(q_rot, kv_all, kr, heads, "attn_fwd",
                                               comm=gather(["w_branch_b", "w_out", "w_up"]))
    w_b, w_o, w_upf = rows4(whole("w_branch_b", g_b)), rows4(whole("w_out", g_o)), whole("w_up", g_up)
    y_a = _matmul(a_act, w_a, mode="nn", out_dtype=BF16, name="mm_y_a", tm=s)
    y_b = _matmul(o_att, w_b, mode="nn", out_dtype=BF16, name="mm_y_b", tm=s)
    merged = _merge(z_big, y_a, y_b, "merge")
    y1 = _matmul(merged, w_o, mode="nn", out_dtype=F32, name="mm_y1", tm=s)
    x1, h2 = _post_pre(x2d, y1, gate1, g_post1, g_pre2, scale2, shift2, "post1_pre2")

    up_pre, (g_dn,) = _matmul(h2, w_upf, mode="nn", out_dtype=BF16, name="mm_up", tm=s, tn=1408,
                              comm=gather(["w_down"]))
    w_dn = rows4(whole("w_down", g_dn))
    act = _conv_fwd(up_pre, conv_wf, conv_bf, "conv_fwd")
    ffn = _matmul(act, w_dn, mode="nn", out_dtype=F32, name="mm_ffn", tm=s, tn=1024, tk=1408)

    dffn, dgate2, g_post2_grad, dx2, loss_part = _post_bwd(ffn, gate2, g_post2, "post2_bwd", xin=x1, target=tgt)
    loss = lax.psum(loss_part[0, 0], ("x", "y", "c"))
    place = jnp.stack([ic, chip]).astype(jnp.int32)
    rows_of = lambda g: g.reshape(N_CHIPS, g.shape[0] // N_CHIPS, g.shape[1])
    add_sibling = lambda names, gs, r1s: [_add_sibling(g, r1, place, "rs_add_sibling_" + n, by_cols=n == "w_in")
                                          for n, g, r1 in zip(names, gs, r1s)]
    add_chips = lambda names, s1s, r2s: [_add_chips(s1, r2, place, "rs_add_chips_" + n, by_cols=n == "w_in")
                                         for n, s1, r2 in zip(names, s1s, r2s)]
    gp_down = [rows_of(_matmul(act, dffn, mode="tn", out_dtype=BF16, name="mm_gw_down", tn=2048, tk=s))]
    dact, r1_down = _matmul(dffn, w_dn, mode="nt", out_dtype=BF16, name="mm_dact", tm=s, comm=_swap_comm(gp_down))
    s1_down = add_sibling(["w_down"], gp_down, r1_down)
    (dup, gcw_g, gcw_v, gcb_g, gcb_v), r2_down = _conv_bwd(up_pre, dact, conv_wf, conv_bf, "conv_bwd",
                                                            comm=_exchange_comm(s1_down))
    half_down = add_chips(["w_down"], s1_down, r2_down)
    dh2 = _matmul(dup, w_upf, mode="nt", out_dtype=F32, name="mm_dh2", tm=s, tn=1024, tk=1408)
    dx1, dshift2, dscale2, g_pre2_grad = _prenorm_bwd(x1, dh2, dx2, g_pre2, scale2, "prenorm2_bwd")

    dy1, dgate1, g_post1_grad = _post_bwd(y1, gate1, g_post1, "post1_bwd", dxo=dx1)
    dmerged = _matmul(dy1, w_o, mode="nt", out_dtype=BF16, name="mm_dmerged", tm=s)
    gw_out = _matmul(merged, dy1, mode="tn", out_dtype=BF16, name="mm_gw_out", tn=1024, tk=s)
    dy_a, dy_b, dz_big = _merge_bwd(dmerged, z_big, y_a, y_b, "merge_bwd")
    gw_a = _matmul(a_act, dy_a, mode="tn", out_dtype=BF16, name="mm_gw_a", tn=1024, tk=s)
    gw_b = _matmul(o_att, dy_b, mode="tn", out_dtype=BF16, name="mm_gw_b", tn=1024, tk=s)
    mid = ["w_up", "w_out", "w_branch_a", "w_branch_b"]
    gp_oab = [rows_of(gw_out), rows_of(gw_a), rows_of(gw_b)]
    da, r1_oab = _matmul(dy_a, w_a, mode="nt", out_dtype=BF16, name="mm_da", tm=s, comm=_swap_comm(gp_oab))
    s1_oab = add_sibling(mid[1:], gp_oab, r1_oab)
    gw_up, r2_oa = _matmul(h2, dup, mode="tn", out_dtype=BF16, name="mm_gw_up", tm=1024, tn=1408, tk=s,
                           out_groups=N_CHIPS, comm=_exchange_comm(s1_oab[:2]))
    do = _matmul(dy_b, w_b, mode="nt", out_dtype=BF16, name="mm_do", tm=s)
    (dz_big, g_ws, g_bs_t, g_ln_g, g_ln_b), r1_up = _gmlp_bwd(z_big, da, dz_big, ln_g, ln_b, gm_w_s, b_s_t,
                                                               "gmlp_bwd", comm=_swap_comm([gw_up]))
    s1_mid = add_sibling(mid[:1], [gw_up], r1_up) + s1_oab
    (dq_big, dk, dv), r2_up = _attn_bwd(q_rot, kv_all, kr, o_att, do, lse, rope_q, heads, "attn_bwd",
                                        comm=_exchange_comm(s1_mid[:1]))
    dkv, dkk = _mla_bwd_mid(dk, dv, rope_k, heads, "mla_bwd_mid")
    gw_q = _matmul(qn, dq_big, mode="tn", out_dtype=F32, name="mm_gw_q", tn=1024, tk=s)
    dqn = _matmul(dq_big, w_q, mode="nt", out_dtype=F32, name="mm_dqn", tm=s, tk=1024)
    gw_kv = _matmul(kvn, dkv, mode="tn", out_dtype=BF16, name="mm_gw_kv", tn=1024, tk=s)
    dkvn = _matmul(dkv, w_kv, mode="nt", out_dtype=F32, name="mm_dkvn", tm=s, tk=1024)
    dz_lat, g_q, g_kv = _mla_bwd_post(z_lat, dqn, dkvn, dkk, q_g, kv_g, "mla_bwd_post")

    partial = {
        "gm_ln_g": g_ln_g, "gm_ln_b": g_ln_b, "gm_w_s": g_ws, "gm_b_s": g_bs_t[:, :gm_b_s.shape[0]].T,
        "q_norm_g": g_q, "kv_norm_g": g_kv, "post_norm1_g": g_post1_grad, "pre_norm2_g": g_pre2_grad,
        "conv_w": jnp.concatenate([gcw_g, gcw_v], axis=1), "conv_b": jnp.concatenate([gcb_g, gcb_v], axis=1),
        "post_norm2_g": g_post2_grad,
    }
    flat = jnp.concatenate([partial[n].reshape(-1) for n in SMALL_PARTIAL])
    n_small = flat.shape[0]
    rows_small = -(-n_small // (LANES * SMALL_ROW_TILE)) * SMALL_ROW_TILE
    flat = jnp.pad(flat, (0, rows_small * LANES - n_small)).reshape(rows_small, LANES)

    def small_pack(prefix, source):
        v = jnp.concatenate([source[prefix + n].reshape(-1) for n in SMALL])
        rows = -(-v.shape[0] // (LANES * SUBLANES)) * SUBLANES
        return jnp.pad(v, (0, rows * LANES - v.shape[0])).reshape(rows, LANES)

    small_state = [small_pack(prefix, given) for prefix in ("", "m_", "v_")]

    dh1, r2_a_b = _matmul(dz_big, w_in_big_t, mode="nn", out_dtype=F32, name="mm_dh1_big", tm=s, tn=1024, tk=1024,
                          comm=_exchange_comm(s1_mid[3:]))
    half_mid = add_chips(mid, s1_mid, list(r2_up) + list(r2_oa) + list(r2_a_b))
    gw_big_t, hosted = _matmul(dz_big, h1, mode="tn", out_dtype=BF16, name="mm_gw_in_big", tn=2048, tk=s,
                               comm=_join_comms([_share_comm(half_down + half_mid), _gather8_comm(flat)]))
    shared, small_all = hosted[:-1], lax.dynamic_update_slice(hosted[-1], flat[None], (dev, 0, 0))
    small_sum = _sum_leading(small_all, "sum_small", after=small_state + [loss.reshape(1, 1)]).reshape(-1)
    small_grads, off = {}, 0
    for n in SMALL_PARTIAL:
        shape = (CONV_TAPS, 2 * ff) if n == "conv_w" else given[n].shape
        small_grads[n] = small_sum[off:off + partial[n].size].reshape(shape)
        off += partial[n].size
    small_grads["conv_w"] = lax.dynamic_slice(small_grads["conv_w"], (0, chip * conv_w.shape[1]), conv_w.shape)
    grads = dict(zip(["w_down"] + mid, shared), **small_grads)
    gw_lat_t = _matmul(dz_lat, h1, mode="tn", out_dtype=F32, name="mm_gw_in_lat", tm=1024, tn=1024, tk=s)

    gq = gw_q.reshape(ql, heads, HEAD_W)
    gq_pe = gq[:, :, NOPE:NOPE + ROPE] + _quarter_turn_back(gq[:, :, NOPE + ROPE:])
    g_pe_t = gw_lat_t[ql + kvl:ql + kvl + ROPE] + _quarter_turn_back(gw_lat_t[ql + kvl + ROPE:].T).T
    last = ["w_in", "w_uq", "w_ukv"]
    gw_in_t = _stack_rows([gw_big_t[:o_q], gw_lat_t[:ql + kvl].astype(BF16), g_pe_t.astype(BF16), gw_big_t[o_q:]])
    gp_last = [
        gw_in_t.reshape(N_CHIPS, gw_in_t.shape[0] // N_CHIPS, d),
        _split_cols(jnp.concatenate([gq[:, :, :NOPE], gq_pe], axis=2).reshape(ql, heads * (NOPE + ROPE)).astype(BF16)),
        _split_cols(gw_kv.reshape(kvl, 2, heads, NOPE).transpose(0, 2, 1, 3).reshape(kvl, heads * 2 * NOPE)),
    ]
    dh1, r1_last = _matmul(dz_lat, w_in_lat_t, mode="nn", out_dtype=F32, name="mm_dh1_lat", tm=s, tk=1024, add=dh1,
                           comm=_swap_comm(gp_last, by_cols=[0]))
    grad_x, dshift1, dscale1, g_pre1_grad = _prenorm_bwd(x2d, dh1, dx1, g_pre1, scale1, "prenorm1_bwd")
    s1_last = add_sibling(last, gp_last, r1_last)

    dmod = jnp.concatenate([dshift1, dscale1, dgate1, dshift2, dscale2, dgate2, g_pre1_grad], axis=1)
    dmod_all = _all_gather(jnp.pad(dmod, ((0, SUBLANES - 1), (0, 0))), "gather_dmod")
    dmod_all = dmod_all.reshape(N_DEV, SUBLANES, (N_MOD + 1) * d)[:, 0]
    dmod_sum = _sum_leading(dmod_all.reshape(N_DEV, 1, (N_MOD + 1) * d), "sum_dmod")[0]
    grads["b_ada"], grads["pre_norm1_g"] = dmod_sum[:N_MOD * d], dmod_sum[N_MOD * d:]
    dmod_mine = lax.dynamic_slice(dmod_all, (0, chip * na), (N_DEV, na))
    grads["w_ada"] = _ada_bwd(c_all.T, dmod_mine, "ada_bwd")

    delta, new_m, new_v = {}, {}, {}

    def adamw(n, after=None):
        turn = (lambda a: a.T) if n == "w_in" else (lambda a: a)
        outs = _adamw(turn(given[n]), grads[n], turn(given["m_" + n]), turn(given["v_" + n]), "adamw_" + n,
                      after=after, emit_grad=n in BIG)
        g_out = outs[0] if n in BIG else grads[n]
        grads[n], delta[n], new_m[n], new_v[n] = (turn(o) for o in (g_out, *outs[-3:]))

    exchange_last = _exchange_comm(s1_last)
    in_flight, token = _comm_split_start(exchange_last, "rs_exchange_last_start", after=[dmod_sum, small_sum])
    for n in ["w_ada", "w_down"] + mid:
        adamw(n, after=token)
    s1_last, r2_last = _comm_split_wait(exchange_last, in_flight, delta[mid[-1]], "rs_exchange_last_wait")
    half_last = add_chips(last, s1_last, r2_last)
    grads.update(zip(last, _run_comm(_share_comm(half_last, by_cols=[0]), "rs_share_last")))
    for n in last:
        adamw(n)

    outs = _adamw(small_state[0], small_pack("", grads), small_state[1], small_state[2], "adamw_small")
    off = 0
    for n in SMALL:
        size = given[n].size
        for store, packed_out in zip((delta, new_m, new_v), outs):
            store[n] = packed_out.reshape(-1)[off:off + size].reshape(given[n].shape)
        off += size

    return (loss, grad_x[None], *[grads[n] for n in WEIGHTS], *[delta[n] for n in WEIGHTS],
            *[new_m[n] for n in WEIGHTS], *[new_v[n] for n in WEIGHTS])
```

```python
import functools

import jax
import jax.numpy as jnp
from jax import lax
from jax.experimental import pallas as pl
from jax.experimental.pallas import tpu as pltpu

F32 = jnp.float32
BF16 = jnp.bfloat16
MESH = pl.DeviceIdType.MESH
HBM = pltpu.HBM

EPS = 1e-6
NOPE, ROPE, VHEAD = 128, 64, 128
HEAD_W = NOPE + 2 * ROPE
ROPE_THETA = 10000.0
CONV_TAPS = 3
N_MOD = 6
N_CHIPS, N_CORES, N_DEV = 4, 2, 8
ADAM_LR, ADAM_B1, ADAM_B2, ADAM_EPS, ADAM_WD, ADAM_STEP = 0.001, 0.9, 0.999, 1e-08, 0.01, 10

LANES = 128
SUBLANES = 8
VMEM_LIMIT = 56 * 2**20
MIDDLE_STAGE_AT = 70
TOKEN_BLOCK = 512
SMALL_ROW_TILE = 256
ADAMW_BLOCK_ELEMS = 768 * 1024

BIG = ("w_in", "w_branch_a", "w_uq", "w_ukv", "w_branch_b", "w_out", "w_up", "w_down")
WEIGHTS = ("w_ada", "b_ada", "pre_norm1_g", "w_in", "gm_ln_g", "gm_ln_b", "gm_w_s", "gm_b_s", "w_branch_a",
           "q_norm_g", "w_uq", "kv_norm_g", "w_ukv", "w_branch_b", "w_out", "post_norm1_g", "pre_norm2_g",
           "w_up", "conv_w", "conv_b", "w_down", "post_norm2_g")
SMALL_PARTIAL = ("gm_ln_g", "gm_ln_b", "gm_w_s", "gm_b_s", "q_norm_g", "kv_norm_g", "post_norm1_g",
                 "pre_norm2_g", "conv_w", "conv_b", "post_norm2_g")
SMALL = ("b_ada", "pre_norm1_g") + SMALL_PARTIAL


def _div_tile(n, cap, mult=LANES):
    t = (min(cap, n) // mult) * mult
    while t >= mult:
        if n % t == 0:
            return t
        t -= mult
    return n


def _params(**kw):
    return pltpu.CompilerParams(vmem_limit_bytes=VMEM_LIMIT, **kw)


def _row_spec(width):
    return pl.BlockSpec((1, width), lambda *_: (0, 0))


def _gelu(x):
    k = 0.7978845608028654
    return 0.5 * x * (1.0 + jnp.tanh(k * (x + 0.044715 * x * x * x)))


def _gelu_grad(x):
    k = 0.7978845608028654
    t = jnp.tanh(k * (x + 0.044715 * x * x * x))
    return 0.5 * (1.0 + t) + 0.5 * x * (1.0 - t * t) * k * (1.0 + 3.0 * 0.044715 * x * x)


def _sigmoid(x):
    return 0.5 * jnp.tanh(0.5 * x) + 0.5


def _dot(a, b, dims):
    return lax.dot_general(a, b, (dims, ((), ())), preferred_element_type=F32)


NN = ((1,), (0,))
NT = ((1,), (1,))
TN = ((0,), (0,))


def _logical(arr):
    if arr.ndim == 2:
        return arr.shape[0], arr.shape[1], arr.shape[1]
    return arr.shape[1], arr.shape[0] * arr.shape[2], arr.shape[2]


def _tile_spec(ndim, group_w, blk_rows, blk_cols, row_of, col_of):
    if ndim == 2:
        return pl.BlockSpec((blk_rows, blk_cols), lambda i, j, k: (row_of(i, j, k), col_of(i, j, k)))
    per = group_w // blk_cols
    return pl.BlockSpec((None, blk_rows, blk_cols),
                        lambda i, j, k: (col_of(i, j, k) // per, row_of(i, j, k), col_of(i, j, k) % per))


def _matmul(a, b, *, mode, out_dtype, name, tm=512, tn=512, tk=2048, mul=None, add=None, out_groups=None, comm=None):
    ar, ac, agw = _logical(a)
    br, bc, bgw = _logical(b)
    if mode == "nn":
        m, kd, n = ar, ac, bc
        m_w, k_w, n_w = (), (agw,), (bgw,)
    elif mode == "nt":
        m, kd, n = ar, ac, br
        m_w, k_w, n_w = (), (agw, bgw), ()
    else:
        m, kd, n = ac, ar, bc
        m_w, k_w, n_w = (agw,), (), (bgw,)
    if out_groups is not None:
        n_w = n_w + (n // out_groups,)
    tm = _div_tile(min((m,) + m_w), tm, LANES if mode == "tn" else SUBLANES)
    tn = _div_tile(min((n,) + n_w), tn)
    tk = _div_tile(min((kd,) + k_w), tk)
    assert all(w % tn == 0 for w in n_w) and all(w % tk == 0 for w in k_w) and all(w % tm == 0 for w in m_w)
    nk = kd // tk
    dims = {"nn": NN, "nt": NT, "tn": TN}[mode]
    gi, gj, gk = (lambda i, j, k: i), (lambda i, j, k: j), (lambda i, j, k: k)
    if mode == "nn":
        a_spec = _tile_spec(a.ndim, agw, tm, tk, gi, gk)
        b_spec = _tile_spec(b.ndim, bgw, tk, tn, gk, gj)
    elif mode == "nt":
        a_spec = _tile_spec(a.ndim, agw, tm, tk, gi, gk)
        b_spec = _tile_spec(b.ndim, bgw, tn, tk, gj, gk)
    else:
        a_spec = _tile_spec(a.ndim, agw, tk, tm, gk, gi)
        b_spec = _tile_spec(b.ndim, bgw, tk, tn, gk, gj)
    in_specs, operands = [a_spec, b_spec], [a, b]
    if mul is not None:
        assert mul.shape == (m, tn)
        in_specs.append(pl.BlockSpec((tm, tn), lambda i, j, k: (i, 0)))
        operands.append(mul)
    if add is not None:
        in_specs.append(pl.BlockSpec((tm, tn), lambda i, j, k: (i, j)))
        operands.append(add)

    def body(*refs):
        a_ref, b_ref = refs[0], refs[1]
        pos = 2
        mul_ref = add_ref = None
        if mul is not None:
            mul_ref, pos = refs[pos], pos + 1
        if add is not None:
            add_ref, pos = refs[pos], pos + 1
        o_ref = refs[pos]

        def finish(r):
            if mul_ref is not None:
                r = r * mul_ref[...]
            if add_ref is not None:
                r = r + add_ref[...]
            o_ref[...] = r.astype(out_dtype)

        part = _dot(a_ref[...], b_ref[...], dims)
        if nk == 1:
            finish(part)
        else:
            acc_ref = refs[pos + 1]
            k = pl.program_id(2)

            @pl.when(k == 0)
            def _():
                acc_ref[...] = part

            @pl.when(k > 0)
            def _():
                acc_ref[...] += part

            @pl.when(k == nk - 1)
            def _():
                finish(acc_ref[...])

    if out_groups is None:
        out_spec, out_dims = _tile_spec(2, n, tm, tn, gi, gj), (m, n)
    else:
        out_spec, out_dims = _tile_spec(3, n // out_groups, tm, tn, gi, gj), (out_groups, m, n // out_groups)
    return _call(body, operands, comm, name=name, grid=(m // tm, n // tn, nk), in_specs=in_specs, out_specs=out_spec,
                 out_shape=jax.ShapeDtypeStruct(out_dims, out_dtype),
                 scratch_shapes=[] if nk == 1 else [pltpu.VMEM((tm, tn), F32)])


def _accumulate(ref, value):
    @pl.when(pl.program_id(0) == 0)
    def _():
        ref[...] = value

    @pl.when(pl.program_id(0) > 0)
    def _():
        ref[...] += value


def _colsum(v):
    return jnp.sum(v, axis=0, keepdims=True)


def _rowmean(v):
    return jnp.mean(v, axis=-1, keepdims=True)


def _prenorm(x, g, scale, shift, name):
    s, d = x.shape
    tb = _div_tile(s, TOKEN_BLOCK, SUBLANES)

    def body(x_ref, g_ref, sc_ref, sh_ref, h_ref):
        xv = x_ref[...]
        r = lax.rsqrt(_rowmean(xv * xv) + EPS)
        h_ref[...] = ((xv * r) * g_ref[...] * (1.0 + sc_ref[...]) + sh_ref[...]).astype(BF16)

    blk = pl.BlockSpec((tb, d), lambda i: (i, 0))
    return pl.pallas_call(
        body, name=name, grid=(s // tb,), in_specs=[blk, _row_spec(d), _row_spec(d), _row_spec(d)],
        out_specs=blk, out_shape=jax.ShapeDtypeStruct((s, d), BF16), compiler_params=_params(),
    )(x, g, scale, shift)


def _post_pre(x, y, gate, pg, g2, scale2, shift2, name):
    s, d = x.shape
    tb = _div_tile(s, TOKEN_BLOCK, SUBLANES)

    def body(x_ref, y_ref, gate_ref, pg_ref, g2_ref, sc_ref, sh_ref, x1_ref, h2_ref):
        yv = y_ref[...]
        rp = lax.rsqrt(_rowmean(yv * yv) + EPS)
        x1 = x_ref[...] + gate_ref[...] * ((yv * rp) * pg_ref[...])
        x1_ref[...] = x1
        r2 = lax.rsqrt(_rowmean(x1 * x1) + EPS)
        h2_ref[...] = ((x1 * r2) * g2_ref[...] * (1.0 + sc_ref[...]) + sh_ref[...]).astype(BF16)

    blk = pl.BlockSpec((tb, d), lambda i: (i, 0))
    return pl.pallas_call(
        body, name=name, grid=(s // tb,), in_specs=[blk, blk] + [_row_spec(d)] * 5,
        out_specs=[blk, blk],
        out_shape=[jax.ShapeDtypeStruct((s, d), F32), jax.ShapeDtypeStruct((s, d), BF16)],
        compiler_params=_params(),
    )(x, y, gate, pg, g2, scale2, shift2)


def _post_bwd(y, gate, pg, name, *, dxo=None, xin=None, target=None):
    s, d = y.shape
    tb = _div_tile(s, TOKEN_BLOCK, SUBLANES)
    from_loss = target is not None

    def body(*refs):
        if from_loss:
            y_ref, gate_ref, pg_ref, xin_ref, t_ref, dy_ref, dgate_ref, dpg_ref, dxo_ref, loss_ref = refs
        else:
            y_ref, gate_ref, pg_ref, dxo_in_ref, dy_ref, dgate_ref, dpg_ref = refs
        yv = y_ref[...]
        rp = lax.rsqrt(_rowmean(yv * yv) + EPS)
        yh = yv * rp
        fn = yh * pg_ref[...]
        gate = gate_ref[...]
        if from_loss:
            err = xin_ref[...] + gate * fn - t_ref[...]
            dxo = err * (1.0 / d)
            dxo_ref[...] = dxo
            part = 0.5 * jnp.sum(_rowmean(err * err), axis=0, keepdims=True)
            _accumulate(loss_ref, jnp.broadcast_to(part, loss_ref.shape))
        else:
            dxo = dxo_in_ref[...]
        _accumulate(dgate_ref, _colsum(dxo * fn))
        dfn = dxo * gate
        _accumulate(dpg_ref, _colsum(dfn * yh))
        dyh = dfn * pg_ref[...]
        dy_ref[...] = (rp * (dyh - yh * _rowmean(dyh * yh))).astype(BF16)

    blk = pl.BlockSpec((tb, d), lambda i: (i, 0))
    in_specs = [blk, _row_spec(d), _row_spec(d)]
    out_specs = [blk, _row_spec(d), _row_spec(d)]
    out_shape = [jax.ShapeDtypeStruct((s, d), BF16), jax.ShapeDtypeStruct((1, d), F32),
                 jax.ShapeDtypeStruct((1, d), F32)]
    if from_loss:
        operands = (y, gate, pg, xin, target)
        in_specs += [blk, blk]
        out_specs += [blk, _row_spec(LANES)]
        out_shape += [jax.ShapeDtypeStruct((s, d), F32), jax.ShapeDtypeStruct((1, LANES), F32)]
    else:
        operands = (y, gate, pg, dxo)
        in_specs += [blk]
    return pl.pallas_call(
        body, name=name, grid=(s // tb,), in_specs=in_specs, out_specs=out_specs, out_shape=out_shape,
        compiler_params=_params(),
    )(*operands)


def _prenorm_bwd(xin, dh, dres, g, scale, name, comm=None):
    s, d = xin.shape
    tb = _div_tile(s, TOKEN_BLOCK, SUBLANES)

    def body(x_ref, dh_ref, dres_ref, g_ref, sc_ref, dx_ref, dshift_ref, dscale_ref, dg_ref):
        xv = x_ref[...]
        r = lax.rsqrt(_rowmean(xv * xv) + EPS)
        xn = xv * r
        dh = dh_ref[...]
        g1 = g_ref[...]
        s1 = 1.0 + sc_ref[...]
        _accumulate(dshift_ref, _colsum(dh))
        _accumulate(dscale_ref, _colsum(dh * xn * g1))
        _accumulate(dg_ref, _colsum(dh * xn * s1))
        dxn = dh * g1 * s1
        dx_ref[...] = dres_ref[...] + r * (dxn - xn * _rowmean(dxn * xn))

    blk = pl.BlockSpec((tb, d), lambda i: (i, 0))
    return _call(
        body, (xin, dh, dres, g, scale), comm, name=name, grid=(s // tb,),
        in_specs=[blk, blk, blk, _row_spec(d), _row_spec(d)],
        out_specs=[blk, _row_spec(d), _row_spec(d), _row_spec(d)],
        out_shape=[jax.ShapeDtypeStruct((s, d), F32)] + [jax.ShapeDtypeStruct((1, d), F32)] * 3)


def _merge(z_big, y_a, y_b, name):
    s, d = y_a.shape
    tb = _div_tile(s, TOKEN_BLOCK, SUBLANES)

    def body(zg_ref, ya_ref, yb_ref, o_ref):
        ga, gb = zg_ref[:, :d].astype(F32), zg_ref[:, d:].astype(F32)
        o_ref[...] = (_sigmoid(ga) * ya_ref[...].astype(F32) + _sigmoid(gb) * yb_ref[...].astype(F32)).astype(BF16)

    blk = pl.BlockSpec((tb, d), lambda i: (i, 0))
    return pl.pallas_call(
        body, name=name, grid=(s // tb,), in_specs=[pl.BlockSpec((tb, 2 * d), lambda i: (i, 1)), blk, blk],
        out_specs=blk, out_shape=jax.ShapeDtypeStruct((s, d), BF16), compiler_params=_params(),
    )(z_big, y_a, y_b)


def _merge_bwd(dmerged, z_big, y_a, y_b, name):
    s, d = y_a.shape
    tb = _div_tile(s, TOKEN_BLOCK, SUBLANES)

    def body(dm_ref, zg_ref, ya_ref, yb_ref, dya_ref, dyb_ref, dz_ref):
        dm = dm_ref[...].astype(F32)
        sa, sb = _sigmoid(zg_ref[:, :d].astype(F32)), _sigmoid(zg_ref[:, d:].astype(F32))
        dya_ref[...] = (dm * sa).astype(BF16)
        dyb_ref[...] = (dm * sb).astype(BF16)
        dz_ref[:, :d] = (dm * ya_ref[...].astype(F32) * sa * (1.0 - sa)).astype(BF16)
        dz_ref[:, d:] = (dm * yb_ref[...].astype(F32) * sb * (1.0 - sb)).astype(BF16)

    blk = pl.BlockSpec((tb, d), lambda i: (i, 0))
    wide = pl.BlockSpec((tb, 2 * d), lambda i: (i, 1))
    return pl.pallas_call(
        body, name=name, grid=(s // tb,), in_specs=[blk, wide, blk, blk], out_specs=[blk, blk, wide],
        out_shape=[jax.ShapeDtypeStruct((s, d), BF16), jax.ShapeDtypeStruct((s, d), BF16),
                   jax.ShapeDtypeStruct((s, 4 * d), BF16)],
        compiler_params=_params(),
    )(dmerged, z_big, y_a, y_b)


def _causal_mask(ch):
    q = lax.broadcasted_iota(jnp.int32, (ch, ch), 0)
    p = lax.broadcasted_iota(jnp.int32, (ch, ch), 1)
    return (p <= q).astype(F32)


def _gmlp_norm(zc, lng, lnb, gw):
    u_pre, v_pre = zc[:, :gw], zc[:, gw:]
    vg = _gelu(v_pre)
    mu = _rowmean(vg)
    cen = vg - mu
    rstd = lax.rsqrt(_rowmean(cen * cen) + EPS)
    vhat = cen * rstd
    return u_pre, v_pre, _gelu(u_pre), vhat, rstd, vhat * lng + lnb


def _gmlp_fwd(z_big, ln_g, ln_b, w_s, b_s_t, name):
    s = z_big.shape[0]
    groups, ch, _ = w_s.shape
    gw = ln_g.shape[1]
    gd = gw // groups

    def body(z_ref, lng_ref, lnb_ref, ws_ref, bt_ref, a_ref):
        _, _, u, _, _, vn = _gmlp_norm(z_ref[...].astype(F32), lng_ref[...], lnb_ref[...], gw)
        mask = _causal_mask(ch)
        for g in range(groups):
            cols = slice(g * gd, (g + 1) * gd)
            wm = (ws_ref[g] * mask).astype(BF16)
            mixed = _dot(wm, vn[:, cols].astype(BF16), NN) + bt_ref[:, g:g + 1]
            a_ref[:, cols] = (u[:, cols] * mixed).astype(BF16)

    return pl.pallas_call(
        body, name=name, grid=(s // ch,),
        in_specs=[pl.BlockSpec((ch, 2 * gw), lambda n: (n, 0)), _row_spec(gw), _row_spec(gw),
                  pl.BlockSpec((groups, ch, ch), lambda n: (0, 0, 0)), pl.BlockSpec((ch, groups), lambda n: (0, 0))],
        out_specs=pl.BlockSpec((ch, gw), lambda n: (n, 0)),
        out_shape=jax.ShapeDtypeStruct((s, gw), BF16), compiler_params=_params(),
    )(z_big, ln_g, ln_b, w_s, b_s_t)


def _gmlp_bwd(z_big, da, dz_big, ln_g, ln_b, w_s, b_s_t, name, comm=None):
    s = z_big.shape[0]
    groups, ch, _ = w_s.shape
    gw = ln_g.shape[1]
    gd = gw // groups

    def body(z_ref, da_ref, dzin_ref, lng_ref, lnb_ref, ws_ref, bt_ref, dz_ref, gws_ref, gbt_ref, glng_ref, glnb_ref,
             vg_ref, dvh_ref):
        del dzin_ref
        mask = _causal_mask(ch)
        lane = lax.broadcasted_iota(jnp.int32, (ch, LANES), 1)
        group_cols = [slice(g * gd, (g + 1) * gd) for g in range(groups)]
        rowsum = lambda v: jnp.sum(v, axis=1, keepdims=True)

        @pl.when(pl.program_id(0) == 0)
        def _():
            for ref in (gws_ref, gbt_ref, glng_ref, glnb_ref):
                ref[...] = jnp.zeros(ref.shape, F32)

        total = jnp.zeros((ch, 1), F32)
        for cols in group_cols:
            vg = _gelu(z_ref[:, gw + cols.start:gw + cols.stop].astype(F32))
            vg_ref[:, cols] = vg
            total = total + rowsum(vg)
        mu = total * (1.0 / gw)
        total = jnp.zeros((ch, 1), F32)
        for cols in group_cols:
            cen = vg_ref[:, cols] - mu
            total = total + rowsum(cen * cen)
        rstd = lax.rsqrt(total * (1.0 / gw) + EPS)
        m1, m2, gb = jnp.zeros((ch, 1), F32), jnp.zeros((ch, 1), F32), jnp.zeros((ch, LANES), F32)
        for g, cols in enumerate(group_cols):
            vhat = (vg_ref[:, cols] - mu) * rstd
            vn_g = (vhat * lng_ref[:, cols] + lnb_ref[:, cols]).astype(BF16)
            wm = (ws_ref[g] * mask).astype(BF16)
            mixed = _dot(wm, vn_g, NN) + bt_ref[:, g:g + 1]
            u_pre, da_g = z_ref[:, cols].astype(F32), da_ref[:, cols].astype(F32)
            dz_ref[:, cols] = (da_g * mixed * _gelu_grad(u_pre)).astype(BF16)
            dmixed = da_g * _gelu(u_pre)
            dm16 = dmixed.astype(BF16)
            dvn = _dot(wm, dm16, TN)
            gws_ref[g] += _dot(dm16, vn_g, NT) * mask
            gb = gb + jnp.where(lane == g, rowsum(dmixed), 0.0)
            glnb_ref[:, cols] += _colsum(dvn)
            glng_ref[:, cols] += _colsum(dvn * vhat)
            dvh = dvn * lng_ref[:, cols]
            dvh_ref[:, cols] = dvh
            m1, m2 = m1 + rowsum(dvh), m2 + rowsum(dvh * vhat)
        gbt_ref[...] += gb
        m1, m2 = m1 * (1.0 / gw), m2 * (1.0 / gw)
        for cols in group_cols:
            vhat = (vg_ref[:, cols] - mu) * rstd
            dvg = rstd * (dvh_ref[:, cols] - m1 - vhat * m2)
            v_pre = z_ref[:, gw + cols.start:gw + cols.stop].astype(F32)
            dz_ref[:, gw + cols.start:gw + cols.stop] = (dvg * _gelu_grad(v_pre)).astype(BF16)

    zspec = pl.BlockSpec((ch, 2 * gw), lambda n: (n, 0))
    return _call(
        body, (z_big, da, dz_big, ln_g, ln_b, w_s, b_s_t), comm, name=name, grid=(s // ch,),
        in_specs=[zspec, pl.BlockSpec((ch, gw), lambda n: (n, 0)), pl.BlockSpec(memory_space=HBM),
                  _row_spec(gw), _row_spec(gw), pl.BlockSpec((groups, ch, ch), lambda n: (0, 0, 0)),
                  pl.BlockSpec((ch, groups), lambda n: (0, 0))],
        out_specs=[zspec, pl.BlockSpec((groups, ch, ch), lambda n: (0, 0, 0)),
                   pl.BlockSpec((ch, LANES), lambda n: (0, 0)), _row_spec(gw), _row_spec(gw)],
        out_shape=[jax.ShapeDtypeStruct(dz_big.shape, BF16), jax.ShapeDtypeStruct((groups, ch, ch), F32),
                   jax.ShapeDtypeStruct((ch, LANES), F32), jax.ShapeDtypeStruct((1, gw), F32),
                   jax.ShapeDtypeStruct((1, gw), F32)],
        scratch_shapes=[pltpu.VMEM((ch, gw), F32)] * 2, input_output_aliases={2: 0})


def _mla_prep(z_lat, q_g, kv_g, rope_k, name):
    s, latw = z_lat.shape
    ql, kvl = q_g.shape[1], kv_g.shape[1]
    tb = _div_tile(s, TOKEN_BLOCK, SUBLANES)

    def body(z_ref, qg_ref, kvg_ref, t_ref, qn_ref, kvn_ref, kr_ref):
        q = z_ref[:, :ql]
        qn_ref[...] = ((q * lax.rsqrt(_rowmean(q * q) + EPS)) * qg_ref[...]).astype(BF16)
        kv = z_ref[:, ql:ql + kvl]
        kvn_ref[...] = ((kv * lax.rsqrt(_rowmean(kv * kv) + EPS)) * kvg_ref[...]).astype(BF16)
        kk = z_ref[:, ql + kvl:] * t_ref[...]
        kr_ref[...] = (kk + pltpu.roll(kk, ROPE, axis=1)).astype(BF16)

    return pl.pallas_call(
        body, name=name, grid=(s // tb,),
        in_specs=[pl.BlockSpec((tb, latw), lambda i: (i, 0)), _row_spec(ql), _row_spec(kvl),
                  pl.BlockSpec((tb, 2 * ROPE), lambda i: (i, 0))],
        out_specs=[pl.BlockSpec((tb, ql), lambda i: (i, 0)), pl.BlockSpec((tb, kvl), lambda i: (i, 0)),
                   pl.BlockSpec((tb, 2 * ROPE), lambda i: (i, 0))],
        out_shape=[jax.ShapeDtypeStruct((s, ql), BF16), jax.ShapeDtypeStruct((s, kvl), BF16),
                   jax.ShapeDtypeStruct((s, 2 * ROPE), BF16)],
        compiler_params=_params(),
    )(z_lat, q_g, kv_g, rope_k)


def _attn_fwd(q, kv, kr, heads, name, comm=None):
    s = q.shape[0]
    t = _div_tile(s, 512)
    nb = s // t
    hp = 2 if heads % 2 == 0 else 1

    def body(q_ref, k_ref, kr_ref, v_ref, o_ref, lse_ref, m_ref, l_ref, acc_ref):
        i, j = pl.program_id(1), pl.program_id(2)

        @pl.when(j == 0)
        def _():
            m_ref[...] = jnp.full(m_ref.shape, -1e30, F32)
            l_ref[...] = jnp.zeros(l_ref.shape, F32)
            acc_ref[...] = jnp.zeros(acc_ref.shape, F32)

        def update(h, rows, n_keys, on_diagonal):
            vc = slice(h * VHEAD, (h + 1) * VHEAD)
            k_full = jnp.concatenate([k_ref[:n_keys, h * NOPE:(h + 1) * NOPE], kr_ref[:n_keys, :]], axis=1)
            sc = _dot(q_ref[rows, h * HEAD_W:(h + 1) * HEAD_W], k_full, NT)
            if on_diagonal:
                row_pos = rows.start + lax.broadcasted_iota(jnp.int32, sc.shape, 0)
                sc = jnp.where(lax.broadcasted_iota(jnp.int32, sc.shape, 1) <= row_pos, sc, -1e30)
            m_old = m_ref[h, rows, :]
            m_new = jnp.maximum(m_old, jnp.max(sc, axis=-1, keepdims=True))
            p = jnp.exp(sc - m_new)
            alpha = jnp.exp(m_old - m_new)
            l_new = alpha * l_ref[h, rows, :] + jnp.sum(p, axis=-1, keepdims=True)
            acc = alpha * acc_ref[rows, vc] + _dot(p.astype(BF16), v_ref[:n_keys, vc], NN)
            if on_diagonal:
                o_ref[rows, vc] = (acc / l_new).astype(BF16)
                lse_ref[h, rows, :] = jnp.broadcast_to(m_new + jnp.log(l_new), (rows.stop - rows.start, LANES))
            else:
                m_ref[h, rows, :], l_ref[h, rows, :], acc_ref[rows, vc] = m_new, l_new, acc

        def below_diagonal():
            for h in range(hp):
                update(h, slice(0, t // 2), t, False)
                update(h, slice(t // 2, t), t, False)

        def on_diagonal():
            for h in range(hp):
                update(h, slice(0, t // 2), t // 2, True)
                update(h, slice(t // 2, t), t, True)

        pl.when(j < i)(below_diagonal)
        pl.when(j == i)(on_diagonal)

    kidx = lambda off: (lambda h, i, j: (jnp.minimum(i, j), off(h)))
    return _call(
        body, (q, kv, kr, kv), comm, name=name, grid=(heads // hp, nb, nb),
        in_specs=[pl.BlockSpec((t, hp * HEAD_W), lambda h, i, j: (i, h)),
                  pl.BlockSpec((t, hp * NOPE), kidx(lambda h: h)),
                  pl.BlockSpec((t, 2 * ROPE), kidx(lambda h: 0)),
                  pl.BlockSpec((t, hp * VHEAD), kidx(lambda h: heads // hp + h))],
        out_specs=[pl.BlockSpec((t, hp * VHEAD), lambda h, i, j: (i, h)),
                   pl.BlockSpec((hp, t, LANES), lambda h, i, j: (h, i, 0))],
        out_shape=[jax.ShapeDtypeStruct((s, heads * VHEAD), BF16), jax.ShapeDtypeStruct((heads, s, LANES), F32)],
        scratch_shapes=[pltpu.VMEM((hp, t, 1), F32), pltpu.VMEM((hp, t, 1), F32), pltpu.VMEM((t, hp * VHEAD), F32)])


def _attn_bwd(q, kv, kr, o, do, lse, rope_q, heads, name, comm=None):
    s = q.shape[0]
    t = _div_tile(s, 512)
    nb = s // t
    hp = 2 if heads % 2 == 0 else 1

    def body(q_ref, k_ref, kr_ref, v_ref, o_ref, do_ref, lse_ref, tq_ref, dqb_ref, dk_ref, dv_ref, dk_acc, dv_acc,
             dq_ref):
        j, i = pl.program_id(1), pl.program_id(2)

        @pl.when(jnp.logical_and(j == 0, i == 0))
        def _():
            dq_ref[...] = jnp.zeros(dq_ref.shape, F32)

        def update(h, rows, n_keys, on_diagonal, assign):
            qc, kc, vc = (slice(h * w, (h + 1) * w) for w in (HEAD_W, NOPE, VHEAD))
            n_rows = rows.stop - rows.start
            qv, do_v = q_ref[rows, qc], do_ref[rows, vc]
            k_full = jnp.concatenate([k_ref[:n_keys, kc], kr_ref[:n_keys, :]], axis=1)
            sc = _dot(qv, k_full, NT)
            if on_diagonal:
                row_pos = rows.start + lax.broadcasted_iota(jnp.int32, sc.shape, 0)
                sc = jnp.where(lax.broadcasted_iota(jnp.int32, sc.shape, 1) <= row_pos, sc, -1e30)
            p = jnp.exp(sc - lse_ref[h, rows, :1])
            dp = _dot(do_v, v_ref[:n_keys, vc], NT)
            delta = jnp.sum(do_v.astype(F32) * o_ref[rows, vc].astype(F32), axis=-1, keepdims=True)
            ds = (p * (dp - delta)).astype(BF16)
            dq_ref[pl.ds(pl.multiple_of(i * t + rows.start, n_rows), n_rows), qc] += _dot(ds, k_full, NN)
            dv_part, dk_part = _dot(p.astype(BF16), do_v, TN), _dot(ds, qv, TN)
            if assign:
                dv_acc[:n_keys, vc], dk_acc[:n_keys, qc] = dv_part, dk_part
            else:
                dv_acc[:n_keys, vc] += dv_part
                dk_acc[:n_keys, qc] += dk_part

        def on_diagonal():
            for h in range(hp):
                update(h, slice(t // 2, t), t, True, True)
                update(h, slice(0, t // 2), t // 2, True, False)

        def below_diagonal():
            for h in range(hp):
                update(h, slice(0, t // 2), t, False, False)
                update(h, slice(t // 2, t), t, False, False)

        pl.when(i == j)(on_diagonal)
        pl.when(i > j)(below_diagonal)

        @pl.when(i == nb - 1)
        def _():
            dk_ref[...] = dk_acc[...].astype(BF16)
            dv_ref[...] = dv_acc[...].astype(BF16)

        @pl.when(jnp.logical_and(j == nb - 1, i == nb - 1))
        def _():
            for h in range(hp):
                qc = slice(h * HEAD_W, (h + 1) * HEAD_W)
                dqb_ref[:, qc] = (dq_ref[:, qc] * tq_ref[...]).astype(BF16)

    qidx = lambda h, j, i: (jnp.maximum(i, j), h)
    return _call(
        body, (q, kv, kr, kv, o, do, lse, rope_q), comm, name=name, grid=(heads // hp, nb, nb),
        in_specs=[pl.BlockSpec((t, hp * HEAD_W), qidx),
                  pl.BlockSpec((t, hp * NOPE), lambda h, j, i: (j, h)),
                  pl.BlockSpec((t, 2 * ROPE), lambda h, j, i: (j, 0)),
                  pl.BlockSpec((t, hp * VHEAD), lambda h, j, i: (j, heads // hp + h)),
                  pl.BlockSpec((t, hp * VHEAD), qidx), pl.BlockSpec((t, hp * VHEAD), qidx),
                  pl.BlockSpec((hp, t, LANES), lambda h, j, i: (h, jnp.maximum(i, j), 0)),
                  pl.BlockSpec((s, HEAD_W), lambda h, j, i: (0, 0))],
        out_specs=[pl.BlockSpec((s, hp * HEAD_W), lambda h, j, i: (0, h)),
                   pl.BlockSpec((t, hp * HEAD_W), lambda h, j, i: (j, h)),
                   pl.BlockSpec((t, hp * VHEAD), lambda h, j, i: (j, h))],
        out_shape=[jax.ShapeDtypeStruct((s, heads * HEAD_W), BF16), jax.ShapeDtypeStruct((s, heads * HEAD_W), BF16),
                   jax.ShapeDtypeStruct((s, heads * VHEAD), BF16)],
        scratch_shapes=[pltpu.VMEM((t, hp * HEAD_W), F32), pltpu.VMEM((t, hp * VHEAD), F32),
                        pltpu.VMEM((s, hp * HEAD_W), F32)])


def _mla_bwd_mid(dk, dv, rope_k, heads, name):
    s = dk.shape[0]
    tb = _div_tile(s, TOKEN_BLOCK, SUBLANES)

    def body(dk_ref, dv_ref, tk_ref, dkv_ref, dkk_ref):
        dkr = jnp.zeros((tb, 2 * ROPE), F32)
        for h in range(heads):
            dkv_ref[:, h * NOPE:(h + 1) * NOPE] = dk_ref[:, h * HEAD_W:h * HEAD_W + NOPE]
            dkr = dkr + dk_ref[:, h * HEAD_W + NOPE:(h + 1) * HEAD_W].astype(F32)
        dkv_ref[:, heads * NOPE:] = dv_ref[...]
        dkk_ref[...] = (dkr + pltpu.roll(dkr, ROPE, axis=1)) * tk_ref[...]

    wq, wv = heads * HEAD_W, heads * VHEAD
    return pl.pallas_call(
        body, name=name, grid=(s // tb,),
        in_specs=[pl.BlockSpec((tb, wq), lambda i: (i, 0)), pl.BlockSpec((tb, wv), lambda i: (i, 0)),
                  pl.BlockSpec((tb, 2 * ROPE), lambda i: (i, 0))],
        out_specs=[pl.BlockSpec((tb, heads * NOPE + wv), lambda i: (i, 0)),
                   pl.BlockSpec((tb, 2 * ROPE), lambda i: (i, 0))],
        out_shape=[jax.ShapeDtypeStruct((s, heads * NOPE + wv), BF16), jax.ShapeDtypeStruct((s, 2 * ROPE), F32)],
        compiler_params=_params(),
    )(dk, dv, rope_k)


def _mla_bwd_post(z_lat, dqn, dkvn, dkk, q_g, kv_g, name):
    s, latw = z_lat.shape
    ql, kvl = q_g.shape[1], kv_g.shape[1]
    tb = _div_tile(s, TOKEN_BLOCK, SUBLANES)

    def norm_bwd(xv, dn, g, dg_ref):
        r = lax.rsqrt(_rowmean(xv * xv) + EPS)
        xh = xv * r
        _accumulate(dg_ref, _colsum(dn * xh))
        dxh = dn * g
        return r * (dxh - xh * _rowmean(dxh * xh))

    def body(z_ref, dqn_ref, dkvn_ref, dkk_ref, qg_ref, kvg_ref, dz_ref, gq_ref, gkv_ref):
        dz_ref[:, :ql] = norm_bwd(z_ref[:, :ql], dqn_ref[...], qg_ref[...], gq_ref).astype(BF16)
        dz_ref[:, ql:ql + kvl] = norm_bwd(z_ref[:, ql:ql + kvl], dkvn_ref[...], kvg_ref[...], gkv_ref).astype(BF16)
        dz_ref[:, ql + kvl:] = dkk_ref[...].astype(BF16)

    return pl.pallas_call(
        body, name=name, grid=(s // tb,),
        in_specs=[pl.BlockSpec((tb, latw), lambda i: (i, 0)), pl.BlockSpec((tb, ql), lambda i: (i, 0)),
                  pl.BlockSpec((tb, kvl), lambda i: (i, 0)), pl.BlockSpec((tb, 2 * ROPE), lambda i: (i, 0)),
                  _row_spec(ql), _row_spec(kvl)],
        out_specs=[pl.BlockSpec((tb, latw), lambda i: (i, 0)), _row_spec(ql), _row_spec(kvl)],
        out_shape=[jax.ShapeDtypeStruct((s, latw), BF16), jax.ShapeDtypeStruct((1, ql), F32),
                   jax.ShapeDtypeStruct((1, kvl), F32)],
        compiler_params=_params(),
    )(z_lat, dqn, dkvn, dkk, q_g, kv_g)


CONV_ROWS = 128
CONV_HALO = 16


def _row_steps(n_rows, step):
    step(0, True)
    if n_rows > CONV_ROWS:
        def later(i, carry):
            step(pl.multiple_of(i * CONV_ROWS, CONV_ROWS), False)
            return carry
        lax.fori_loop(1, n_rows // CONV_ROWS, later, 0)


def _conv_taps(pre_ref, r0, first):
    if first:
        win = jnp.concatenate([jnp.zeros((CONV_HALO, pre_ref.shape[1]), F32), pre_ref[0:CONV_ROWS, :].astype(F32)])
    else:
        win = pre_ref[pl.ds(pl.multiple_of(r0 - CONV_HALO, CONV_HALO), CONV_ROWS + CONV_HALO), :].astype(F32)
    return win[CONV_HALO:], pltpu.roll(win, 1, axis=0)[CONV_HALO:], pltpu.roll(win, 2, axis=0)[CONV_HALO:]


def _conv(taps, w_ref, b_ref):
    return w_ref[2:3, :] * taps[0] + w_ref[1:2, :] * taps[1] + w_ref[0:1, :] * taps[2] + b_ref[...]


def _conv_fwd(up_pre, conv_w, conv_b, name):
    s, ff2 = up_pre.shape
    ff = ff2 // 2
    tc = _div_tile(ff, 256)
    nb = ff // tc
    assert s % CONV_ROWS == 0

    def body(pg_ref, pv_ref, wg_ref, wv_ref, bg_ref, bv_ref, act_ref):
        def step(r0, first):
            gate = _conv(_conv_taps(pg_ref, r0, first), wg_ref, bg_ref)
            val = _conv(_conv_taps(pv_ref, r0, first), wv_ref, bv_ref)
            act_ref[pl.ds(r0, CONV_ROWS), :] = (gate * _sigmoid(gate) * val).astype(BF16)

        _row_steps(s, step)

    def col(rows, off):
        return pl.BlockSpec((rows, tc), lambda j: (0, j + off))

    return pl.pallas_call(
        body, name=name, grid=(nb,),
        in_specs=[col(s, 0), col(s, nb), col(CONV_TAPS, 0), col(CONV_TAPS, nb), col(1, 0), col(1, nb)],
        out_specs=col(s, 0), out_shape=jax.ShapeDtypeStruct((s, ff), BF16), compiler_params=_params(),
    )(up_pre, up_pre, conv_w, conv_w, conv_b, conv_b)


def _conv_bwd(up_pre, dact, conv_w, conv_b, name, comm=None):
    s, ff2 = up_pre.shape
    ff = ff2 // 2
    tc = _div_tile(ff, 256)
    nb = ff // tc
    assert s % CONV_ROWS == 0

    def body(pg_ref, pv_ref, da_ref, wg_ref, wv_ref, bg_ref, bv_ref, dup_ref, gwg_ref, gwv_ref, gbg_ref, gbv_ref,
             dxg_ref, dxv_ref):
        for ref in (gwg_ref, gwv_ref, gbg_ref, gbv_ref):
            ref[...] = jnp.zeros(ref.shape, F32)
        for ref in (dxg_ref, dxv_ref):
            ref[s:s + SUBLANES, :] = jnp.zeros((SUBLANES, tc), F32)

        def sums(taps, dx, gw_ref, gb_ref):
            gb_ref[...] += _colsum(dx)
            for k in range(CONV_TAPS):
                gw_ref[k:k + 1, :] += _colsum(dx * taps[CONV_TAPS - 1 - k])

        def forward(r0, first):
            rows = pl.ds(r0, CONV_ROWS)
            taps_g, taps_v = _conv_taps(pg_ref, r0, first), _conv_taps(pv_ref, r0, first)
            gate, val = _conv(taps_g, wg_ref, bg_ref), _conv(taps_v, wv_ref, bv_ref)
            da = da_ref[rows, :].astype(F32)
            sg = _sigmoid(gate)
            dxv, dxg = da * gate * sg, da * val * sg * (1.0 + gate * (1.0 - sg))
            dxv_ref[rows, :], dxg_ref[rows, :] = dxv, dxg
            sums(taps_v, dxv, gwv_ref, gbv_ref)
            sums(taps_g, dxg, gwg_ref, gbg_ref)

        def backward(r0, first):
            del first
            n = CONV_ROWS + SUBLANES
            for dx_ref, w_ref, out_ref in ((dxg_ref, wg_ref, dup_ref.at[0]), (dxv_ref, wv_ref, dup_ref.at[1])):
                win = dx_ref[pl.ds(r0, n), :]
                ahead1 = pltpu.roll(win, n - 1, axis=0)[:CONV_ROWS]
                ahead2 = pltpu.roll(win, n - 2, axis=0)[:CONV_ROWS]
                out_ref[pl.ds(r0, CONV_ROWS), :] = (w_ref[2:3, :] * win[:CONV_ROWS] + w_ref[1:2, :] * ahead1
                                                    + w_ref[0:1, :] * ahead2).astype(BF16)

        _row_steps(s, forward)
        _row_steps(s, backward)

    def col(rows, off):
        return pl.BlockSpec((rows, tc), lambda j: (0, j + off))

    return _call(
        body, (up_pre, up_pre, dact, conv_w, conv_w, conv_b, conv_b), comm, name=name, grid=(nb,),
        in_specs=[col(s, 0), col(s, nb), col(s, 0), col(CONV_TAPS, 0), col(CONV_TAPS, nb), col(1, 0), col(1, nb)],
        out_specs=[pl.BlockSpec((2, s, tc), lambda j: (0, 0, j)), col(CONV_TAPS, 0), col(CONV_TAPS, 0),
                   col(1, 0), col(1, 0)],
        out_shape=[jax.ShapeDtypeStruct((2, s, ff), BF16)] + [jax.ShapeDtypeStruct((CONV_TAPS, ff), F32)] * 2
        + [jax.ShapeDtypeStruct((1, ff), F32)] * 2,
        scratch_shapes=[pltpu.VMEM((s + SUBLANES, tc), F32)] * 2)


def _ada_fwd(c_all, w, b, name):
    nseq, d = c_all.shape
    na = w.shape[1]
    tn = _div_tile(na, 512)

    def body(c_ref, w_ref, b_ref, o_ref):
        cv = c_ref[...]
        sc = cv * _sigmoid(cv)
        o_ref[...] = jnp.dot(sc, w_ref[...], preferred_element_type=F32, precision=lax.Precision.HIGHEST) + b_ref[...]

    return pl.pallas_call(
        body, name=name, grid=(na // tn,),
        in_specs=[pl.BlockSpec((nseq, d), lambda j: (0, 0)), pl.BlockSpec((d, tn), lambda j: (0, j)),
                  pl.BlockSpec((1, tn), lambda j: (0, j))],
        out_specs=pl.BlockSpec((nseq, tn), lambda j: (0, j)),
        out_shape=jax.ShapeDtypeStruct((nseq, na), F32), compiler_params=_params(),
    )(c_all, w, b)


def _ada_bwd(c_all_t, dmod, name):
    d, nseq = c_all_t.shape
    na = dmod.shape[1]
    tm, tn = _div_tile(d, 512, SUBLANES), _div_tile(na, 1024)

    def body(c_ref, dm_ref, o_ref):
        cv = c_ref[...]
        o_ref[...] = jnp.dot(cv * _sigmoid(cv), dm_ref[...], preferred_element_type=F32,
                             precision=lax.Precision.HIGHEST)

    return pl.pallas_call(
        body, name=name, grid=(d // tm, na // tn),
        in_specs=[pl.BlockSpec((tm, nseq), lambda i, j: (i, 0)), pl.BlockSpec((nseq, tn), lambda i, j: (0, j))],
        out_specs=pl.BlockSpec((tm, tn), lambda i, j: (i, j)),
        out_shape=jax.ShapeDtypeStruct((d, na), F32), compiler_params=_params(),
    )(c_all_t, dmod)


def _adamw(w, g, m, v, name, comm=None, after=None, emit_grad=False):
    rows, cols = w.shape
    n_out = 4 if emit_grad else 3
    tb = _div_tile(rows, max(SUBLANES, ADAMW_BLOCK_ELEMS // cols // SUBLANES * SUBLANES), SUBLANES)
    c1 = 1.0 / (1.0 - ADAM_B1 ** ADAM_STEP)
    c2 = 1.0 / (1.0 - ADAM_B2 ** ADAM_STEP)

    def body(*refs):
        w_ref, g_ref, m_ref, v_ref = refs[:4]
        d_ref, nm_ref, nv_ref = refs[-3:]
        gv = g_ref[...]
        if emit_grad:
            refs[-4][...] = gv
        nm =ADAM_B1 * m_ref[...] + (1.0 - ADAM_B1) * gv
        nv = ADAM_B2 * v_ref[...] + (1.0 - ADAM_B2) * (gv * gv)
        nm_ref[...] = nm
        nv_ref[...] = nv
        d_ref[...] = -ADAM_LR * ((nm * c1) / (jnp.sqrt(nv * c2) + ADAM_EPS) + ADAM_WD * w_ref[...])

    blk = pl.BlockSpec((tb, cols), lambda i: (i, 0))
    operands, in_specs = (w, g, m, v), [blk] * 4
    if after is not None:
        operands, in_specs = operands + (after,), in_specs + [pl.BlockSpec(after.shape, lambda i: (0, 0))]
    return _call(body, operands, comm, name=name, grid=(rows // tb,), in_specs=in_specs, out_specs=[blk] * n_out,
                 out_shape=[jax.ShapeDtypeStruct((rows, cols), F32)] * n_out)


def _sum_leading(parts, name, after=()):
    n, rows, cols = parts.shape
    tb = _div_tile(rows, 512, SUBLANES)

    def body(p_ref, *rest):
        o_ref = rest[-1]
        acc = p_ref[0]
        for k in range(1, n):
            acc = acc + p_ref[k]
        o_ref[...] = acc

    return pl.pallas_call(
        body, name=name, grid=(rows // tb,),
        in_specs=[pl.BlockSpec((n, tb, cols), lambda i: (0, i, 0))] + [pl.BlockSpec(memory_space=pl.ANY)] * len(after),
        out_specs=pl.BlockSpec((tb, cols), lambda i: (i, 0)),
        out_shape=jax.ShapeDtypeStruct((rows, cols), F32), compiler_params=_params(),
    )(parts, *after)


def _place():
    x, y, c = lax.axis_index("x"), lax.axis_index("y"), lax.axis_index("c")
    return x, y, c, [(1 - x, y), (x, 1 - y), (1 - x, 1 - y)]


def _all_gather(block, name):
    m_per, n = block.shape

    def body(x_ref, out_ref, send_sems, recv_sems, local_sem):
        x, y, c, chips = _place()
        me, sibling = (x, y, c), (x, y, 1 - c)

        def rows(px, py, pc):
            return out_ref.at[pl.ds((4 * px + 2 * py + pc) * m_per, m_per), :]

        def copy(k, blk, to, src=None):
            return pltpu.make_async_remote_copy(
                src_ref=rows(*blk) if src is None else src, dst_ref=rows(*blk), send_sem=send_sems.at[k],
                recv_sem=recv_sems.at[k], device_id=to, device_id_type=MESH)

        mine = pltpu.make_async_copy(x_ref, rows(*me), local_sem)
        mine.start()
        first = [copy(0, me, sibling, src=x_ref)]
        first += [copy(1 + j, me, (*chip, c), src=x_ref) for j, chip in enumerate(chips)]
        for cp in first:
            cp.start()
        passed = [copy(4 + j, (*chip, c), sibling) for j, chip in enumerate(chips)]
        for j, chip in enumerate(chips):
            copy(1 + j, (*chip, c), me).wait_recv()
            passed[j].start()
        copy(0, sibling, me).wait_recv()
        for j, chip in enumerate(chips):
            copy(4 + j, (*chip, 1 - c), me).wait_recv()
        for cp in first + passed:
            cp.wait_send()
        mine.wait()

    return pl.pallas_call(
        body, name=name, out_shape=jax.ShapeDtypeStruct((N_DEV * m_per, n), block.dtype),
        in_specs=[pl.BlockSpec(memory_space=pltpu.VMEM)], out_specs=pl.BlockSpec(memory_space=pltpu.VMEM),
        scratch_shapes=[pltpu.SemaphoreType.DMA((7,)), pltpu.SemaphoreType.DMA((7,)), pltpu.SemaphoreType.DMA],
        compiler_params=_params(),
    )(block)


def _hbm_specs(n):
    return [pl.BlockSpec(memory_space=HBM)] * n


def _part(ref, by_cols, half, quarter=None, lead=None):
    extent = ref.shape[-1] if by_cols else ref.shape[-2]
    size = extent // 2 if quarter is None else extent // 4
    first = half * (extent // 2) + (0 if quarter is None else quarter * size)
    tile = LANES if by_cols else 2 * SUBLANES
    span = pl.ds(pl.multiple_of(first, tile) if size % tile == 0 else first, size)
    index = (slice(None), span) if by_cols else (span, slice(None))
    return ref.at[index] if lead is None else ref.at[(lead,) + index]


def _half_rows(ref, half, lead=None):
    return _part(ref, False, half, lead=lead)


class _Comm:
    def __init__(self, operands, out_shape, sem_dims, build, aliases=None):
        self.operands, self.out_shape, self.sem_dims = list(operands), list(out_shape), list(sem_dims)
        self.scratch = [pltpu.SemaphoreType.DMA(d) for d in sem_dims]
        self.build, self.aliases = build, dict(aliases or {})


class _SemGrid:
    def __init__(self, sems, dims):
        self.sems, self.dims, self.at = list(sems), tuple(dims), self

    def __getitem__(self, index):
        index = index if isinstance(index, tuple) else (index,)
        flat = 0
        for i, d in zip(index, self.dims):
            flat = flat * d + i
        return self.sems[flat]


def _call(body, operands, comm=None, *, name, grid, in_specs, out_specs, out_shape, scratch_shapes=(),
          input_output_aliases=None):
    aliases = dict(input_output_aliases or {})
    if comm is None:
        return pl.pallas_call(
            body, name=name, grid=grid, in_specs=in_specs, out_specs=out_specs, out_shape=out_shape,
            scratch_shapes=list(scratch_shapes), input_output_aliases=aliases, compiler_params=_params())(*operands)
    single = not isinstance(out_shape, (list, tuple))
    outs = [out_shape] if single else list(out_shape)
    ospecs = [out_specs] if single else list(out_specs)
    n_in, n_out, n_scr = len(operands), len(outs), len(scratch_shapes)
    c_in, c_out = len(comm.operands), len(comm.out_shape)
    for i, o in comm.aliases.items():
        aliases[n_in + i] = n_out + o

    def hosted(*refs):
        ins, c_ins = refs[:n_in], refs[n_in:n_in + c_in]
        o0 = n_in + c_in
        o_refs, c_outs = refs[o0:o0 + n_out], refs[o0 + n_out:o0 + n_out + c_out]
        s0 = o0 + n_out + c_out
        scr, sems = refs[s0:s0 + n_scr], refs[s0 + n_scr:]
        stages = comm.build(c_ins, c_outs, sems)
        step, n_steps = 0, 1
        for dim, size in enumerate(grid):
            step, n_steps = step * size + pl.program_id(dim), n_steps * size
        pl.when(step == 0)(stages[0])
        body(*ins, *o_refs, *scr)
        for stage in stages[1:-1]:
            pl.when(step == (n_steps * MIDDLE_STAGE_AT) // 100)(stage)
        pl.when(step == n_steps - 1)(stages[-1])

    res = pl.pallas_call(
        hosted, name=name, grid=grid, in_specs=list(in_specs) + _hbm_specs(c_in),
        out_specs=ospecs + _hbm_specs(c_out), out_shape=outs + comm.out_shape,
        scratch_shapes=list(scratch_shapes) + comm.scratch, input_output_aliases=aliases,
        compiler_params=_params())(*operands, *comm.operands)
    return (res[0] if single else res[:n_out]), res[n_out:]


def _run_comm(comm, name):
    c_in, c_out = len(comm.operands), len(comm.out_shape)

    def body(*refs):
        for stage in comm.build(refs[:c_in], refs[c_in:c_in + c_out], refs[c_in + c_out:]):
            stage()

    return pl.pallas_call(
        body, name=name, in_specs=_hbm_specs(c_in), out_specs=_hbm_specs(c_out), out_shape=comm.out_shape,
        scratch_shapes=comm.scratch, input_output_aliases=comm.aliases, compiler_params=_params())(*comm.operands)


def _join_comms(comms):
    def build(in_refs, out_refs, sems):
        staged, i, o, k = [], 0, 0, 0
        for cm in comms:
            ni, no, ns = len(cm.operands), len(cm.out_shape), len(cm.sem_dims)
            staged.append(cm.build(in_refs[i:i + ni], out_refs[o:o + no], sems[k:k + ns]))
            i, o, k = i + ni, o + no, k + ns
        def run(fns):
            def stage():
                for fn in fns:
                    fn()
            return stage

        return (run([st[0] for st in staged]), run([fn for st in staged for fn in st[1:-1]]),
                run([st[-1] for st in staged]))

    aliases, i, o = {}, 0, 0
    for cm in comms:
        aliases.update({i + a: o + b for a, b in cm.aliases.items()})
        i, o = i + len(cm.operands), o + len(cm.out_shape)
    return _Comm(sum((cm.operands for cm in comms), []), sum((cm.out_shape for cm in comms), []),
                 sum((cm.sem_dims for cm in comms), []), build, aliases)


def _gather8_comm(block):
    def build(in_refs, out_refs, sems):
        (src,), (out,), (send_sems, recv_sems) = in_refs, out_refs, sems
        x, y, c, chips = _place()
        me, sibling = (x, y, c), (x, y, 1 - c)

        def copy(k, blk, to, own=False):
            dst = out.at[4 * blk[0] + 2 * blk[1] + blk[2]]
            return pltpu.make_async_remote_copy(
                src_ref=src if own else dst, dst_ref=dst, send_sem=send_sems.at[k], recv_sem=recv_sems.at[k],
                device_id=to, device_id_type=MESH)

        first = [copy(0, me, sibling, own=True)] + [copy(1 + j, me, (*chip, c), own=True)
                                                     for j, chip in enumerate(chips)]
        passed = [copy(4 + j, (*chip, c), sibling) for j, chip in enumerate(chips)]

        def start():
            for cp in first:
                cp.start()

        def middle():
            for j, chip in enumerate(chips):
                copy(1 + j, (*chip, c), me).wait_recv()
                passed[j].start()

        def finish():
            copy(0, sibling, me).wait_recv()
            for j, chip in enumerate(chips):
                copy(4 + j, (*chip, 1 - c), me).wait_recv()
            for cp in first + passed:
                cp.wait_send()

        return start, middle, finish

    return _Comm([block], [jax.ShapeDtypeStruct((N_DEV,) + block.shape, block.dtype)], [(7,), (7,)], build)


def _gather_comm(shards, by_cols=()):
    nw = len(shards)

    def build(in_refs, out_refs, sems):
        send_sems, recv_sems = sems
        x, y, c, chips = _place()
        me, sibling = (x, y, c), (x, y, 1 - c)
        across_x, across_y, diagonal = chips

        def copy(w, k, block, part, to, src=None):
            dst = _part(out_refs[w], w in by_cols, part[1], part[2] if part[0] else None, 2 * block[0] + block[1])
            return pltpu.make_async_remote_copy(
                src_ref=dst if src is None else src, dst_ref=dst, send_sem=send_sems.at[w, k],
                recv_sem=recv_sems.at[w, k], device_id=to, device_id_type=MESH)

        first = [copy(w, j, (x, y), (0, c), (*chip, c), src=_part(in_refs[w], w in by_cols, c))
                 for w in range(nw) for j, chip in enumerate((across_x, across_y))]
        first += [pltpu.make_async_remote_copy(
            src_ref=in_refs[w], dst_ref=out_refs[w].at[2 * x + y], send_sem=send_sems.at[w, 8],
            recv_sem=recv_sems.at[w, 8], device_id=sibling, device_id_type=MESH) for w in range(nw)]
        passed = [[copy(w, 2, across_x, (1, c, 0), (*across_y, c)), copy(w, 3, across_y, (1, c, 1), (*across_x, c)),
                   copy(w, 4, across_x, (0, c), sibling), copy(w, 5, across_y, (0, c), sibling)] for w in range(nw)]
        last = [[copy(w, 6, diagonal, (1, c, 0), sibling), copy(w, 7, diagonal, (1, c, 1), sibling)]
                for w in range(nw)]

        def start():
            for cp in first:
                cp.start()

        def middle():
            for w in range(nw):
                copy(w, 0, across_x, (0, c), me).wait_recv()
                copy(w, 1, across_y, (0, c), me).wait_recv()
                for cp in passed[w]:
                    cp.start()

        def finish():
            for w in range(nw):
                copy(w, 2, diagonal, (1, c, 0), me).wait_recv()
                copy(w, 3, diagonal, (1, c, 1), me).wait_recv()
                for cp in last[w]:
                    cp.start()
            for w in range(nw):
                for k, block, part in ((4, across_x, (0, 1 - c)), (5, across_y, (0, 1 - c)),
                                       (6, diagonal, (1, 1 - c, 0)), (7, diagonal, (1, 1 - c, 1))):
                    copy(w, k, block, part, me).wait_recv()
                pltpu.make_async_remote_copy(
                    src_ref=in_refs[w], dst_ref=out_refs[w].at[2 * x + y], send_sem=send_sems.at[w, 8],
                    recv_sem=recv_sems.at[w, 8], device_id=sibling, device_id_type=MESH).wait_recv()
            for cp in first + sum(passed, []) + sum(last, []):
                cp.wait_send()

        return start, middle, finish

    return _Comm(shards, [jax.ShapeDtypeStruct((N_CHIPS,) + w.shape, w.dtype) for w in shards],
                 [(nw, 9), (nw, 9)], build)


def _halved(shape, by_cols):
    return shape[:-1] + (shape[-1] // 2,) if by_cols else shape[:-2] + (shape[-2] // 2, shape[-1])


def _swap_comm(gs, by_cols=()):
    nw = len(gs)

    def build(in_refs, out_refs, sems):
        send_sems, recv_sems = sems
        x, y, c, _ = _place()
        cps = []
        for w in range(nw):
            cps.append(pltpu.make_async_remote_copy(
                src_ref=_part(in_refs[w], w in by_cols, 1 - c, lead=slice(None)), dst_ref=out_refs[w],
                send_sem=send_sems.at[w], recv_sem=recv_sems.at[w], device_id=(x, y, 1 - c), device_id_type=MESH))

        def start():
            for cp in cps:
                cp.start()

        def finish():
            for cp in cps:
                cp.wait()

        return start, finish

    return _Comm(gs, [jax.ShapeDtypeStruct(_halved(g.shape, w in by_cols), g.dtype) for w, g in enumerate(gs)],
                 [(nw,), (nw,)], build)


def _exchange_comm(s1s):
    nw = len(s1s)

    def build(in_refs, out_refs, sems):
        send_sems, recv_sems = sems
        x, y, c, chips = _place()
        cps = [pltpu.make_async_remote_copy(
            src_ref=in_refs[w].at[2 * chip[0] + chip[1]], dst_ref=out_refs[w].at[j], send_sem=send_sems.at[w, j],
            recv_sem=recv_sems.at[w, j], device_id=(*chip, c), device_id_type=MESH)
            for w in range(nw) for j, chip in enumerate(chips)]

        def start():
            for cp in cps:
                cp.start()

        def finish():
            for cp in cps:
                cp.wait()

        return start, finish

    return _Comm(s1s, [jax.ShapeDtypeStruct((N_CHIPS - 1,) + s.shape[1:], s.dtype) for s in s1s],
                 [(nw, 3), (nw, 3)], build)


def _size(dims):
    n = 1
    for d in dims:
        n *= d
    return n


def _sem_grids(comm, sem_refs):
    grids, pos = [], 0
    for dims in comm.sem_dims:
        grids.append(_SemGrid(sem_refs[pos:pos + _size(dims)], dims))
        pos += _size(dims)
    return grids


def _comm_split_start(comm, name, after=()):
    c_in, c_out = len(comm.operands), len(comm.out_shape)
    counts = [_size(d) for d in comm.sem_dims]
    n_sem = sum(counts)
    assert not comm.aliases

    def body(*refs):
        srcs, lands = refs[:c_in], refs[c_in:c_in + c_out]
        first_sem = c_in + c_out + len(after)
        start, _ = comm.build(srcs, lands, _sem_grids(comm, refs[first_sem:first_sem + n_sem]))
        start()
        refs[-1][...] = jnp.zeros(refs[-1].shape, refs[-1].dtype)

    lands = [pltpu.with_memory_space_constraint(lax.empty(o.shape, o.dtype), HBM) for o in comm.out_shape]
    srcs = [pltpu.with_memory_space_constraint(a, HBM) for a in comm.operands]
    res = pl.pallas_call(
        body, name=name, in_specs=_hbm_specs(c_in + c_out) + [pl.BlockSpec(memory_space=pl.ANY)] * len(after),
        out_specs=[pl.BlockSpec(memory_space=pltpu.SEMAPHORE)] * n_sem + _hbm_specs(c_in + c_out)
        + [pl.BlockSpec(memory_space=pltpu.VMEM)],
        out_shape=[pltpu.SemaphoreType.DMA(())] * n_sem + [pltpu.HBM(a.shape, a.dtype) for a in comm.operands]
        + [pltpu.HBM(o.shape, o.dtype) for o in comm.out_shape] + [jax.ShapeDtypeStruct((SUBLANES, LANES), F32)],
        input_output_aliases={i: n_sem + i for i in range(c_in + c_out)},
        compiler_params=_params(has_side_effects=pltpu.SideEffectType.DATAFLOW_SIDE_EFFECTING))(*srcs, *lands, *after)
    return res[:-1], res[-1]


def _comm_split_wait(comm, state, after, name):
    c_in, c_out, n_sem = len(comm.operands), len(comm.out_shape), sum(_size(d) for d in comm.sem_dims)
    sems, srcs, lands = state[:n_sem], state[n_sem:n_sem + c_in], state[n_sem + c_in:]

    def body(*refs):
        src_refs, land_refs = refs[:c_in], refs[c_in:c_in + c_out]
        _, finish = comm.build(src_refs, land_refs, _sem_grids(comm, refs[c_in + c_out:c_in + c_out + n_sem]))
        finish()

    sem_spec = pl.BlockSpec(memory_space=pltpu.SEMAPHORE)
    res = pl.pallas_call(
        body, name=name, in_specs=_hbm_specs(c_in + c_out) + [sem_spec] * n_sem + [pl.BlockSpec(memory_space=pl.ANY)],
        out_specs=_hbm_specs(c_in + c_out),
        out_shape=[pltpu.HBM(a.shape, a.dtype) for a in srcs] + [pltpu.HBM(o.shape, o.dtype) for o in lands],
        input_output_aliases={i: i for i in range(c_in + c_out)},
        compiler_params=_params(has_side_effects=pltpu.SideEffectType.DATAFLOW_SIDE_EFFECTING),
    )(*srcs, *lands, *sems, after)
    return res[:c_in], res[c_in:]


def _share_comm(fs, by_cols=()):
    nw = len(fs)

    def build(in_refs, out_refs, sems):
        del in_refs
        send_sems, recv_sems = sems
        x, y, c, _ = _place()

        def copy(w, half):
            part = _part(out_refs[w], w in by_cols, half)
            return pltpu.make_async_remote_copy(
                src_ref=part, dst_ref=part, send_sem=send_sems.at[w], recv_sem=recv_sems.at[w],
                device_id=(x, y, 1 - c), device_id_type=MESH)

        sends = [copy(w, c) for w in range(nw)]

        def start():
            for cp in sends:
                cp.start()

        def finish():
            for w in range(nw):
                copy(w, 1 - c).wait_recv()
            for cp in sends:
                cp.wait_send()

        return start, finish

    return _Comm(fs, [jax.ShapeDtypeStruct(f.shape, f.dtype) for f in fs],
                 [(nw,), (nw,)], build,
                 aliases={w: w for w in range(nw)})


def _add_sibling(g, r1, place, name, by_cols=False):
    nch, h, cols = r1.shape
    tr = _div_tile(h, 1024 if by_cols else 512, 2 * SUBLANES)
    nb = h // tr
    mine = (lambda k, i, p: (k, i, p[0])) if by_cols else (lambda k, i, p: (k, p[0] * nb + i, 0))

    def body(place_ref, g_ref, r_ref, o_ref):
        del place_ref
        o_ref[...] = (g_ref[...].astype(F32) + r_ref[...].astype(F32)).astype(BF16)

    spec = pltpu.PrefetchScalarGridSpec(
        num_scalar_prefetch=1, grid=(nch, nb),
        in_specs=[pl.BlockSpec((None, tr, cols), mine), pl.BlockSpec((None, tr, cols), lambda k, i, p: (k, i, 0))],
        out_specs=pl.BlockSpec((None, tr, cols), lambda k, i, p: (k, i, 0)))
    return pl.pallas_call(body, name=name, grid_spec=spec, out_shape=jax.ShapeDtypeStruct((nch, h, cols), BF16),
                          compiler_params=_params())(place, g, r1)


def _add_chips(s1, r2, place, name, by_cols=False):
    _, h, cols = s1.shape
    tr = _div_tile(h, 1024 if by_cols else 512, 2 * SUBLANES)
    nb = h // tr
    mine = (lambda i, p: (i, p[0])) if by_cols else (lambda i, p: (p[0] * nb + i, 0))
    whole = (h, 2 * cols) if by_cols else (2 * h, cols)

    def body(place_ref, s_ref, r_ref, o_ref):
        del place_ref
        acc = s_ref[...].astype(F32)
        for j in range(N_CHIPS - 1):
            acc = acc + r_ref[j].astype(F32)
        o_ref[...] = acc

    spec = pltpu.PrefetchScalarGridSpec(
        num_scalar_prefetch=1, grid=(nb,),
        in_specs=[pl.BlockSpec((None, tr, cols), lambda i, p: (p[1], i, 0)),
                  pl.BlockSpec((N_CHIPS - 1, tr, cols), lambda i, p: (0, i, 0))],
        out_specs=pl.BlockSpec((tr, cols), mine))
    return pl.pallas_call(body, name=name, grid_spec=spec, out_shape=jax.ShapeDtypeStruct(whole, F32),
                          compiler_params=_params())(place, s1, r2)


def _quarter_turn(m):
    h = m.shape[-1] // 2
    return jnp.concatenate([-m[..., h:], m[..., :h]], axis=-1)


def _quarter_turn_back(m):
    h = m.shape[-1] // 2
    return jnp.concatenate([m[..., h:], -m[..., :h]], axis=-1)


def _stack_rows(parts):
    out = lax.empty((sum(p.shape[0] for p in parts),) + parts[0].shape[1:], parts[0].dtype)
    row = 0
    for p in parts:
        out = lax.dynamic_update_slice(out, p, (row, 0))
        row += p.shape[0]
    return out


def _join_cols(sh):
    return jnp.concatenate([sh[k] for k in range(N_CHIPS)], axis=1)


def _split_cols(full):
    c = full.shape[1] // N_CHIPS
    return jnp.stack([full[:, k * c:(k + 1) * c] for k in range(N_CHIPS)])


def kernel(x, c, positions, w_ada, b_ada, pre_norm1_g, w_in, gm_ln_g, gm_ln_b, gm_w_s, gm_b_s, w_branch_a, q_norm_g, w_uq, kv_norm_g, w_ukv, w_branch_b, w_out, post_norm1_g, pre_norm2_g, w_up, conv_w, conv_b, w_down, post_norm2_g, loss_target, m_w_ada, m_b_ada, m_pre_norm1_g, m_w_in, m_gm_ln_g, m_gm_ln_b, m_gm_w_s, m_gm_b_s, m_w_branch_a, m_q_norm_g, m_w_uq, m_kv_norm_g, m_w_ukv, m_w_branch_b, m_w_out, m_post_norm1_g, m_pre_norm2_g, m_w_up, m_conv_w, m_conv_b, m_w_down, m_post_norm2_g, v_w_ada, v_b_ada, v_pre_norm1_g, v_w_in, v_gm_ln_g, v_gm_ln_b, v_gm_w_s, v_gm_b_s, v_w_branch_a, v_q_norm_g, v_w_uq, v_kv_norm_g, v_w_ukv, v_w_branch_b, v_w_out, v_post_norm1_g, v_pre_norm2_g, v_w_up, v_conv_w, v_conv_b, v_w_down, v_post_norm2_g):
    given = dict(locals())
    s, d = x.shape[1], x.shape[2]
    gw = gm_ln_g.shape[0]
    ql, kvl = q_norm_g.shape[0], kv_norm_g.shape[0]
    heads = N_CHIPS * w_uq.shape[1] // (NOPE + ROPE)
    ff = N_CHIPS * w_down.shape[0]
    assert gw == d and N_CHIPS * w_ukv.shape[1] == heads * (NOPE + VHEAD)
    ix, iy, ic = lax.axis_index("x"), lax.axis_index("y"), lax.axis_index("c")
    chip = 2 * ix + iy
    dev = 2 * chip + ic
    row = lambda v: v.reshape(1, -1)

    first = _all_gather(jnp.concatenate([jnp.pad(c, ((0, SUBLANES - 1), (0, 0))),
                                         jnp.pad(conv_w, ((0, SUBLANES - CONV_TAPS), (0, 0)))], axis=1), "gather_c")
    first = first.reshape(N_DEV, SUBLANES, d + conv_w.shape[1])
    c_all = first[:, 0, :d]
    conv_wf = first[::N_CORES, :CONV_TAPS, d:].transpose(1, 0, 2).reshape(CONV_TAPS, N_CHIPS * conv_w.shape[1])
    na = w_ada.shape[1]
    b_ada_mine = lax.dynamic_slice(b_ada, (chip * na,), (na,))
    mod_cols = _ada_fwd(c_all, w_ada, row(b_ada_mine), "ada_fwd")
    mod_all = _all_gather(mod_cols, "gather_mod").reshape(N_CHIPS, N_CORES, N_DEV, na)[:, 0]
    mod = lax.dynamic_index_in_dim(mod_all, dev, axis=1, keepdims=False).reshape(N_MOD, d)
    shift1, scale1, gate1, shift2, scale2, gate2 = (mod[i:i + 1] for i in range(N_MOD))

    mine = {n: (given[n].T if n == "w_in" else given[n]).astype(BF16) for n in BIG}
    gather = lambda names: _gather_comm([mine[n] for n in names], [i for i, n in enumerate(names) if n == "w_in"])
    whole = lambda n, g: g
    rows4 = lambda sh4: sh4.reshape(-1, sh4.shape[2])
    wi_t = rows4(whole("w_in", _run_comm(gather(["w_in"]), "gather_w_in")[0]))
    o_q, o_kv, o_pe, o_ga = 2 * gw, 2 * gw + ql, 2 * gw + ql + kvl, 2 * gw + ql + kvl + ROPE
    w_in_big_t = _stack_rows([wi_t[:o_q], wi_t[o_ga:]])
    w_in_lat_t = _stack_rows([wi_t[o_q:o_ga], _quarter_turn(wi_t[o_pe:o_ga].T).T])

    inv = ROPE_THETA ** (-jnp.arange(0, ROPE, 2, dtype=F32) / ROPE)
    ang = positions[0].astype(F32)[:, None] * inv
    cos, sin = jnp.cos(ang), jnp.sin(ang)
    rope_k = jnp.concatenate([cos, cos, sin, sin], axis=1)
    softmax_scale = float(NOPE + ROPE) ** -0.5
    rope_q = jnp.concatenate([jnp.ones((s, NOPE), F32), rope_k], axis=1) * softmax_scale

    x2d, tgt = x[0], loss_target[0]
    g_pre1, g_post1, g_pre2, g_post2 = row(pre_norm1_g), row(post_norm1_g), row(pre_norm2_g), row(post_norm2_g)
    ln_g, ln_b, q_g, kv_g = row(gm_ln_g), row(gm_ln_b), row(q_norm_g), row(kv_norm_g)
    b_s_t = gm_b_s.T
    conv_bf = row(conv_b)

    h1 = _prenorm(x2d, g_pre1, scale1, shift1, "prenorm1")
    z_big, (g_uq, g_ukv, g_a) = _matmul(h1, w_in_big_t, mode="nt", out_dtype=BF16, name="mm_z_big", tm=s,
                                        comm=gather(["w_uq", "w_ukv", "w_branch_a"]))
    wq = _join_cols(whole("w_uq", g_uq)).reshape(ql, heads, NOPE + ROPE)
    w_q = jnp.concatenate([wq, _quarter_turn(wq[:, :, NOPE:])], axis=2).reshape(ql, heads * HEAD_W)
    w_kv = _join_cols(whole("w_ukv", g_ukv)).reshape(kvl, heads, 2, NOPE).transpose(0, 2, 1, 3)
    w_kv = w_kv.reshape(kvl, 2 * heads * NOPE)
    w_a = rows4(whole("w_branch_a", g_a))
    z_lat = _matmul(h1, w_in_lat_t, mode="nt", out_dtype=F32, name="mm_z_lat", tm=s, tn=1024)
    a_act = _gmlp_fwd(z_big, ln_g, ln_b, gm_w_s, b_s_t, "gmlp_fwd")
    qn, kvn, kr = _mla_prep(z_lat, q_g, kv_g, rope_k, "mla_prep")
    q_rot = _matmul(qn, w_q, mode="nn", out_dtype=BF16, name="mm_q", tm=s, tn=HEAD_W, mul=rope_q)
    kv_all = _matmul(kvn, w_kv, mode="nn", out_dtype=BF16, name="mm_kv", tm=s, tn=1024)
    (o_att, lse), (g_b, g_o, g_up) = _attn_fwd(q_rot, kv_all, kr, heads, "attn_fwd",
                                               comm=gather(["w_branch_b", "w_out", "w_up"]))
    w_b, w_o, w_upf = rows4(whole("w_branch_b", g_b)), rows4(whole("w_out", g_o)), whole("w_up", g_up)
    y_a = _matmul(a_act, w_a, mode="nn", out_dtype=BF16, name="mm_y_a", tm=s)
    y_b = _matmul(o_att, w_b, mode="nn", out_dtype=BF16, name="mm_y_b", tm=s)
    merged = _merge(z_big, y_a, y_b, "merge")
    y1 = _matmul(merged, w_o, mode="nn", out_dtype=F32, name="mm_y1", tm=s)
    x1, h2 = _post_pre(x2d, y1, gate1, g_post1, g_pre2, scale2, shift2, "post1_pre2")

    up_pre, (g_dn,) = _matmul(h2, w_upf, mode="nn", out_dtype=BF16, name="mm_up", tm=s, tn=1408,
                              comm=gather(["w_down"]))
    w_dn = rows4(whole("w_down", g_dn))
    act = _conv_fwd(up_pre, conv_wf, conv_bf, "conv_fwd")
    ffn = _matmul(act, w_dn, mode="nn", out_dtype=F32, name="mm_ffn", tm=s, tn=1024, tk=1408)

    dffn, dgate2, g_post2_grad, dx2, loss_part = _post_bwd(ffn, gate2, g_post2, "post2_bwd", xin=x1, target=tgt)
    loss = lax.psum(loss_part[0, 0], ("x", "y", "c"))
    place = jnp.stack([ic, chip]).astype(jnp.int32)
    rows_of = lambda g: g.reshape(N_CHIPS, g.shape[0] // N_CHIPS, g.shape[1])
    add_sibling = lambda names, gs, r1s: [_add_sibling(g, r1, place, "rs_add_sibling_" + n, by_cols=n == "w_in")
                                          for n, g, r1 in zip(names, gs, r1s)]
    add_chips = lambda names, s1s, r2s: [_add_chips(s1, r2, place, "rs_add_chips_" + n, by_cols=n == "w_in")
                                         for n, s1, r2 in zip(names, s1s, r2s)]
    gp_down = [rows_of(_matmul(act, dffn, mode="tn", out_dtype=BF16, name="mm_gw_down", tn=2048, tk=s))]
    dact, r1_down = _matmul(dffn, w_dn, mode="nt", out_dtype=BF16, name="mm_dact", tm=s, comm=_swap_comm(gp_down))
    s1_down = add_sibling(["w_down"], gp_down, r1_down)
    (dup, gcw_g, gcw_v, gcb_g, gcb_v), r2_down = _conv_bwd(up_pre, dact, conv_wf, conv_bf, "conv_bwd",
                                                            comm=_exchange_comm(s1_down))
    half_down = add_chips(["w_down"], s1_down, r2_down)
    dh2 = _matmul(dup, w_upf, mode="nt", out_dtype=F32, name="mm_dh2", tm=s, tn=1024, tk=1408)
    dx1, dshift2, dscale2, g_pre2_grad = _prenorm_bwd(x1, dh2, dx2, g_pre2, scale2, "prenorm2_bwd")

    dy1, dgate1, g_post1_grad = _post_bwd(y1, gate1, g_post1, "post1_bwd", dxo=dx1)
    dmerged = _matmul(dy1, w_o, mode="nt", out_dtype=BF16, name="mm_dmerged", tm=s)
    gw_out = _matmul(merged, dy1, mode="tn", out_dtype=BF16, name="mm_gw_out", tn=1024, tk=s)
    dy_a, dy_b, dz_big = _merge_bwd(dmerged, z_big, y_a, y_b, "merge_bwd")
    gw_a = _matmul(a_act, dy_a, mode="tn", out_dtype=BF16, name="mm_gw_a", tn=1024, tk=s)
    gw_b = _matmul(o_att, dy_b, mode="tn", out_dtype=BF16, name="mm_gw_b", tn=1024, tk=s)
    mid = ["w_up", "w_out", "w_branch_a", "w_branch_b"]
    gp_oab = [rows_of(gw_out), rows_of(gw_a), rows_of(gw_b)]
    da, r1_oab = _matmul(dy_a, w_a, mode="nt", out_dtype=BF16, name="mm_da", tm=s, comm=_swap_comm(gp_oab))
    s1_oab = add_sibling(mid[1:], gp_oab, r1_oab)
    gw_up, r2_oa = _matmul(h2, dup, mode="tn", out_dtype=BF16, name="mm_gw_up", tm=1024, tn=1408, tk=s,
                           out_groups=N_CHIPS, comm=_exchange_comm(s1_oab[:2]))
    do = _matmul(dy_b, w_b, mode="nt", out_dtype=BF16, name="mm_do", tm=s)
    (dz_big, g_ws, g_bs_t, g_ln_g, g_ln_b), r1_up = _gmlp_bwd(z_big, da, dz_big, ln_g, ln_b, gm_w_s, b_s_t,
                                                               "gmlp_bwd", comm=_swap_comm([gw_up]))
    s1_mid = add_sibling(mid[:1], [gw_up], r1_up) + s1_oab
    (dq_big, dk, dv), r2_up = _attn_bwd(q_rot, kv_all, kr, o_att, do, lse, rope_q, heads, "attn_bwd",
                                        comm=_exchange_comm(s1_mid[:1]))
    dkv, dkk = _mla_bwd_mid(dk, dv, rope_k, heads, "mla_bwd_mid")
    gw_q = _matmul(qn, dq_big, mode="tn", out_dtype=F32, name="mm_gw_q", tn=1024, tk=s)
    dqn = _matmul(dq_big, w_q, mode="nt", out_dtype=F32, name="mm_dqn", tm=s, tk=1024)
    gw_kv = _matmul(kvn, dkv, mode="tn", out_dtype=BF16, name="mm_gw_kv", tn=1024, tk=s)
    dkvn = _matmul(dkv, w_kv, mode="nt", out_dtype=F32, name="mm_dkvn", tm=s, tk=1024)
    dz_lat, g_q, g_kv = _mla_bwd_post(z_lat, dqn, dkvn, dkk, q_g, kv_g, "mla_bwd_post")

    partial = {
        "gm_ln_g": g_ln_g, "gm_ln_b": g_ln_b, "gm_w_s": g_ws, "gm_b_s": g_bs_t[:, :gm_b_s.shape[0]].T,
        "q_norm_g": g_q, "kv_norm_g": g_kv, "post_norm1_g": g_post1_grad, "pre_norm2_g": g_pre2_grad,
        "conv_w": jnp.concatenate([gcw_g, gcw_v], axis=1), "conv_b": jnp.concatenate([gcb_g, gcb_v], axis=1),
        "post_norm2_g": g_post2_grad,
    }
    flat = jnp.concatenate([partial[n].reshape(-1) for n in SMALL_PARTIAL])
    n_small = flat.shape[0]
    rows_small = -(-n_small // (LANES * SMALL_ROW_TILE)) * SMALL_ROW_TILE
    flat = jnp.pad(flat, (0, rows_small * LANES - n_small)).reshape(rows_small, LANES)

    def small_pack(prefix, source):
        v = jnp.concatenate([source[prefix + n].reshape(-1) for n in SMALL])
        rows = -(-v.shape[0] // (LANES * SUBLANES)) * SUBLANES
        return jnp.pad(v, (0, rows * LANES - v.shape[0])).reshape(rows, LANES)

    small_state = [small_pack(prefix, given) for prefix in ("", "m_", "v_")]

    dh1, r2_a_b = _matmul(dz_big, w_in_big_t, mode="nn", out_dtype=F32, name="mm_dh1_big", tm=s, tn=1024, tk=1024,
                          comm=_exchange_comm(s1_mid[3:]))
    half_mid = add_chips(mid, s1_mid, list(r2_up) + list(r2_oa) + list(r2_a_b))
    gw_big_t, hosted = _matmul(dz_big, h1, mode="tn", out_dtype=BF16, name="mm_gw_in_big", tn=2048, tk=s,
                               comm=_join_comms([_share_comm(half_down + half_mid), _gather8_comm(flat)]))
    shared, small_all = hosted[:-1], lax.dynamic_update_slice(hosted[-1], flat[None], (dev, 0, 0))
    small_sum = _sum_leading(small_all, "sum_small", after=small_state + [loss.reshape(1, 1)]).reshape(-1)
    small_grads, off = {}, 0
    for n in SMALL_PARTIAL:
        shape = (CONV_TAPS, 2 * ff) if n == "conv_w" else given[n].shape
        small_grads[n] = small_sum[off:off + partial[n].size].reshape(shape)
        off += partial[n].size
    small_grads["conv_w"] = lax.dynamic_slice(small_grads["conv_w"], (0, chip * conv_w.shape[1]), conv_w.shape)
    grads = dict(zip(["w_down"] + mid, shared), **small_grads)
    gw_lat_t = _matmul(dz_lat, h1, mode="tn", out_dtype=F32, name="mm_gw_in_lat", tm=1024, tn=1024, tk=s)

    gq = gw_q.reshape(ql, heads, HEAD_W)
    gq_pe = gq[:, :, NOPE:NOPE + ROPE] + _quarter_turn_back(gq[:, :, NOPE + ROPE:])
    g_pe_t = gw_lat_t[ql + kvl:ql + kvl + ROPE] + _quarter_turn_back(gw_lat_t[ql + kvl + ROPE:].T).T
    last = ["w_in", "w_uq", "w_ukv"]
    gw_in_t = _stack_rows([gw_big_t[:o_q], gw_lat_t[:ql + kvl].astype(BF16), g_pe_t.astype(BF16), gw_big_t[o_q:]])
    gp_last = [
        gw_in_t.reshape(N_CHIPS, gw_in_t.shape[0] // N_CHIPS, d),
        _split_cols(jnp.concatenate([gq[:, :, :NOPE], gq_pe], axis=2).reshape(ql, heads * (NOPE + ROPE)).astype(BF16)),
        _split_cols(gw_kv.reshape(kvl, 2, heads, NOPE).transpose(0, 2, 1, 3).reshape(kvl, heads * 2 * NOPE)),
    ]
    dh1, r1_last = _matmul(dz_lat, w_in_lat_t, mode="nn", out_dtype=F32, name="mm_dh1_lat", tm=s, tk=1024, add=dh1,
                           comm=_swap_comm(gp_last, by_cols=[0]))
    grad_x, dshift1, dscale1, g_pre1_grad = _prenorm_bwd(x2d, dh1, dx1, g_pre1, scale1, "prenorm1_bwd")
    s1_last = add_sibling(last, gp_last, r1_last)

    dmod = jnp.concatenate([dshift1, dscale1, dgate1, dshift2, dscale2, dgate2, g_pre1_grad], axis=1)
    dmod_all = _all_gather(jnp.pad(dmod, ((0, SUBLANES - 1), (0, 0))), "gather_dmod")
    dmod_all = dmod_all.reshape(N_DEV, SUBLANES, (N_MOD + 1) * d)[:, 0]
    dmod_sum = _sum_leading(dmod_all.reshape(N_DEV, 1, (N_MOD + 1) * d), "sum_dmod")[0]
    grads["b_ada"], grads["pre_norm1_g"] = dmod_sum[:N_MOD * d], dmod_sum[N_MOD * d:]
    dmod_mine = lax.dynamic_slice(dmod_all, (0, chip * na), (N_DEV, na))
    grads["w_ada"] = _ada_bwd(c_all.T, dmod_mine, "ada_bwd")

    delta, new_m, new_v = {}, {}, {}

    def adamw(n, after=None):
        turn = (lambda a: a.T) if n == "w_in" else (lambda a: a)
        outs = _adamw(turn(given[n]), grads[n], turn(given["m_" + n]), turn(given["v_" + n]), "adamw_" + n,
                      after=after, emit_grad=n in BIG)
        g_out = outs[0] if n in BIG else grads[n]
        grads[n], delta[n], new_m[n], new_v[n] = (turn(o) for o in (g_out, *outs[-3:]))

    exchange_last = _exchange_comm(s1_last)
    in_flight, token = _comm_split_start(exchange_last, "rs_exchange_last_start", after=[dmod_sum, small_sum])
    for n in ["w_ada", "w_down"] + mid:
        adamw(n, after=token)
    s1_last, r2_last = _comm_split_wait(exchange_last, in_flight, delta[mid[-1]], "rs_exchange_last_wait")
    half_last = add_chips(last, s1_last, r2_last)
    grads.update(zip(last, _run_comm(_share_comm(half_last, by_cols=[0]), "rs_share_last")))
    for n in last:
        adamw(n)

    outs = _adamw(small_state[0], small_pack("", grads), small_state[1], small_state[2], "adamw_small")
    off = 0
    for n in SMALL:
        size = given[n].size
        for store, packed_out in zip((delta, new_m, new_v), outs):
            store[n] = packed_out.reshape(-1)[off:off + size].reshape(given[n].shape)
        off += size

    return (loss, grad_x[None], *[grads[n] for n in WEIGHTS], *[delta[n] for n in WEIGHTS],
            *[new_m[n] for n in WEIGHTS], *[new_v[n] for n in WEIGHTS])
```
